```python
import jax, jax.numpy as jnp
from jax import lax
import numpy as np

D_MODEL = 1024
BATCH = 8
SEQ = 4096
DEPTH = 1

N_META = 16
N_Q_HEADS = 16
N_KV_HEADS = 2
HEAD_DIM = 64
GROUP = N_Q_HEADS // N_KV_HEADS
ROT_DIM = HEAD_DIM // 4
ROPE_THETA = 500000.0
WINDOW = 128
BLOCK = 128
ATTN_WIDTH = N_Q_HEADS * HEAD_DIM
KV_WIDTH = N_KV_HEADS * HEAD_DIM
CONV_CH = D_MODEL
CONV_K = 31
FFN_DIM = 2816
FFN_CONV_K = 3
IN_WIDTH = ATTN_WIDTH + 2 * KV_WIDTH + 2 * CONV_CH + 2 * D_MODEL
RMS_EPS = 1e-6
LN_EPS = 1e-5
NEG_INF = -1e30

kernel_name = "hybrid_swa_sink_conformer_convffn_block"


def rms_norm(x, g):
    xf = x.astype(jnp.float32)
    y = xf * lax.rsqrt(jnp.mean(xf * xf, axis=-1, keepdims=True) + RMS_EPS)
    return (y * g.astype(jnp.float32)).astype(x.dtype)


def layer_norm(x, g, b):
    xf = x.astype(jnp.float32)
    mu = jnp.mean(xf, axis=-1, keepdims=True)
    var = jnp.mean(jnp.square(xf - mu), axis=-1, keepdims=True)
    y = (xf - mu) * lax.rsqrt(var + LN_EPS)
    return (y * g.astype(jnp.float32) + b.astype(jnp.float32)).astype(x.dtype)


def causal_dwconv(x, w, b):
    k = w.shape[0]
    y = lax.conv_general_dilated(
        x, w[:, None, :].astype(x.dtype), window_strides=(1,), padding=[(k - 1, 0)],
        dimension_numbers=("NWC", "WIO", "NWC"), feature_group_count=x.shape[-1])
    return y + b.astype(x.dtype)


def partial_rope(x, pos):
    half = ROT_DIM // 2
    inv_freq = ROPE_THETA ** (-jnp.arange(half, dtype=jnp.float32) * 2.0 / ROT_DIM)
    ang = pos.astype(jnp.float32)[:, None] * inv_freq[None, :]
    cos = jnp.cos(ang)[None, :, None, :]
    sin = jnp.sin(ang)[None, :, None, :]
    xr = x[..., :ROT_DIM].astype(jnp.float32)
    x1, x2 = xr[..., :half], xr[..., half:]
    rot = jnp.concatenate([x1 * cos - x2 * sin, x2 * cos + x1 * sin], axis=-1).astype(x.dtype)
    return jnp.concatenate([rot, x[..., ROT_DIM:]], axis=-1)


def sliding_window_sink_attention(q, k, v, sinks):
    bsz, seq_len = q.shape[0], q.shape[1]
    pad = BLOCK - N_META
    padded = seq_len + pad
    nb = padded // BLOCK
    scale = HEAD_DIM ** -0.5

    def pad_front(a):
        return jnp.pad(a, ((0, 0), (pad, 0), (0, 0), (0, 0)))

    def shift_block(a):
        return jnp.concatenate([jnp.zeros_like(a[:, :1]), a[:, :-1]], axis=1)

    qb = (pad_front(q) * scale).reshape(bsz, nb, BLOCK, N_KV_HEADS, GROUP, HEAD_DIM)
    kb = pad_front(k).reshape(bsz, nb, BLOCK, N_KV_HEADS, HEAD_DIM)
    vb = pad_front(v).reshape(bsz, nb, BLOCK, N_KV_HEADS, HEAD_DIM)
    k_meta = jnp.broadcast_to(k[:, None, :N_META], (bsz, nb, N_META, N_KV_HEADS, HEAD_DIM))
    v_meta = jnp.broadcast_to(v[:, None, :N_META], (bsz, nb, N_META, N_KV_HEADS, HEAD_DIM))
    keys = jnp.concatenate([k_meta, shift_block(kb), kb], axis=2)
    vals = jnp.concatenate([v_meta, shift_block(vb), vb], axis=2)

    tpos = (jnp.arange(padded) - pad).reshape(nb, BLOCK)
    tq = tpos[:, :, None]
    t_meta = jnp.arange(N_META)[None, None, :]
    t_loc = jnp.concatenate([tpos - BLOCK, tpos], axis=1)[:, None, :]
    meta_ok = jnp.broadcast_to(t_meta <= tq, (nb, BLOCK, N_META))
    loc_ok = (t_loc >= N_META) & (t_loc <= tq) & (tq - t_loc < WINDOW)
    mask = jnp.concatenate([meta_ok, loc_ok], axis=-1)

    s = jnp.einsum("bnqhgd,bnkhd->bnhgqk", qb, keys).astype(jnp.float32)
    s = jnp.where(mask[None, :, None, None], s, NEG_INF)
    sink = sinks.astype(jnp.float32).reshape(N_KV_HEADS, GROUP)[None, None, :, :, None, None]
    sink = jnp.broadcast_to(sink, s.shape[:-1] + (1,))
    probs = jax.nn.softmax(jnp.concatenate([s, sink], axis=-1), axis=-1)[..., :-1]
    o = jnp.einsum("bnhgqk,bnkhd->bnqhgd", probs.astype(v.dtype), vals)
    return o.reshape(bsz, padded, ATTN_WIDTH)[:, pad:]


def hybrid_mixer(h, pos, w_in, b_in, attn_sinks, w_attn_proj, conv_dw_w, conv_dw_b,
                 conv_ln_g, conv_ln_b, w_conv_proj, b_conv_proj, w_out):
    bsz, seq_len = h.shape[0], h.shape[1]
    proj = h @ w_in + b_in
    cuts = np.cumsum([ATTN_WIDTH, KV_WIDTH, KV_WIDTH, 2 * CONV_CH, D_MODEL]).tolist()
    q, k, v, glu_in, gate_a, gate_c = jnp.split(proj, cuts, axis=-1)

    q = partial_rope(q.reshape(bsz, seq_len, N_Q_HEADS, HEAD_DIM), pos)
    k = partial_rope(k.reshape(bsz, seq_len, N_KV_HEADS, HEAD_DIM), pos)
    v = v.reshape(bsz, seq_len, N_KV_HEADS, HEAD_DIM)
    attn = sliding_window_sink_attention(q, k, v, attn_sinks) @ w_attn_proj

    a, g = jnp.split(glu_in, 2, axis=-1)
    c = causal_dwconv(a * jax.nn.sigmoid(g), conv_dw_w, conv_dw_b)
    c = jax.nn.silu(layer_norm(c, conv_ln_g, conv_ln_b))
    conv = c @ w_conv_proj + b_conv_proj

    merged = jax.nn.sigmoid(gate_a) * attn + jax.nn.sigmoid(gate_c) * conv
    return merged @ w_out


def conv_ffn(h, w_up, ffn_dw_w, ffn_dw_b, w_down):
    u = causal_dwconv(h @ w_up, ffn_dw_w, ffn_dw_b)
    gate, val = jnp.split(u, 2, axis=-1)
    return (jax.nn.silu(gate) * val) @ w_down


def _fwd_setup_inputs(seed: int = 0) -> dict:
    key = jax.random.key(seed)
    ks = jax.random.split(key, 24)
    f32 = jnp.float32

    def nrm(k, shape, scale):
        return jax.random.normal(k, shape, f32) * scale

    def gain(k, shape):
        return 1.0 + 0.1 * jax.random.normal(k, shape, f32)

    L = DEPTH
    return {
        "x": nrm(ks[0], (BATCH, SEQ, D_MODEL), 1.0),
        "meta_tokens": nrm(ks[1], (N_META, D_MODEL), 1.0),
        "norm_pre_mix": gain(ks[2], (L, D_MODEL)),
        "norm_post_mix": gain(ks[3], (L, D_MODEL)),
        "w_in": nrm(ks[4], (L, D_MODEL, IN_WIDTH), D_MODEL ** -0.5),
        "b_in": nrm(ks[5], (L, IN_WIDTH), 0.02),
        "attn_sinks": nrm(ks[6], (L, N_Q_HEADS), 0.5),
        "w_attn_proj": nrm(ks[7], (L, ATTN_WIDTH, D_MODEL), ATTN_WIDTH ** -0.5),
        "conv_dw_w": nrm(ks[8], (L, CONV_K, CONV_CH), CONV_K ** -0.5),
        "conv_dw_b": nrm(ks[9], (L, CONV_CH), 0.02),
        "conv_ln_g": gain(ks[10], (L, CONV_CH)),
        "conv_ln_b": nrm(ks[11], (L, CONV_CH), 0.02),
        "w_conv_proj": nrm(ks[12], (L, CONV_CH, D_MODEL), CONV_CH ** -0.5),
        "b_conv_proj": nrm(ks[13], (L, D_MODEL), 0.02),
        "w_out": nrm(ks[14], (L, D_MODEL, D_MODEL), D_MODEL ** -0.5),
        "norm_pre_ffn": gain(ks[15], (L, D_MODEL)),
        "norm_post_ffn": gain(ks[16], (L, D_MODEL)),
        "w_up": nrm(ks[17], (L, D_MODEL, 2 * FFN_DIM), D_MODEL ** -0.5),
        "ffn_dw_w": nrm(ks[18], (L, FFN_CONV_K, 2 * FFN_DIM), FFN_CONV_K ** -0.5),
        "ffn_dw_b": nrm(ks[19], (L, 2 * FFN_DIM), 0.02),
        "w_down": nrm(ks[20], (L, FFN_DIM, D_MODEL), FFN_DIM ** -0.5),
    }


def _fwd_reference(x, meta_tokens, norm_pre_mix, norm_post_mix, w_in, b_in, attn_sinks, w_attn_proj,
              conv_dw_w, conv_dw_b, conv_ln_g, conv_ln_b, w_conv_proj, b_conv_proj, w_out,
              norm_pre_ffn, norm_post_ffn, w_up, ffn_dw_w, ffn_dw_b, w_down):
    bsz = x.shape[0]
    meta = jnp.broadcast_to(meta_tokens.astype(x.dtype)[None], (bsz, N_META, D_MODEL))
    h = jnp.concatenate([meta, x], axis=1)
    pos = jnp.arange(h.shape[1])
    for l in range(DEPTH):
        mix = hybrid_mixer(rms_norm(h, norm_pre_mix[l]), pos, w_in[l], b_in[l], attn_sinks[l],
                           w_attn_proj[l], conv_dw_w[l], conv_dw_b[l], conv_ln_g[l], conv_ln_b[l],
                           w_conv_proj[l], b_conv_proj[l], w_out[l])
        h = h + rms_norm(mix, norm_post_mix[l])
        ffn = conv_ffn(rms_norm(h, norm_pre_ffn[l]), w_up[l], ffn_dw_w[l], ffn_dw_b[l], w_down[l])
        h = h + rms_norm(ffn, norm_post_ffn[l])
    return h[:, N_META:]


import jax as _jax
import jax.numpy as _jnp

TWIN_FORMAT = 'train_step'
FWD_PARAMS = ['x', 'meta_tokens', 'norm_pre_mix', 'norm_post_mix', 'w_in', 'b_in', 'attn_sinks', 'w_attn_proj', 'conv_dw_w', 'conv_dw_b', 'conv_ln_g', 'conv_ln_b', 'w_conv_proj', 'b_conv_proj', 'w_out', 'norm_pre_ffn', 'norm_post_ffn', 'w_up', 'ffn_dw_w', 'ffn_dw_b', 'w_down']
TWIN_WEIGHTS = ['meta_tokens', 'norm_pre_mix', 'norm_post_mix', 'w_in', 'b_in', 'attn_sinks', 'w_attn_proj', 'conv_dw_w', 'conv_dw_b', 'conv_ln_g', 'conv_ln_b', 'w_conv_proj', 'b_conv_proj', 'w_out', 'norm_pre_ffn', 'norm_post_ffn', 'w_up', 'ffn_dw_w', 'ffn_dw_b', 'w_down']
TWIN_DIFF_INPUT = 'x'
TWIN_INPUTS = ['x', 'meta_tokens', 'norm_pre_mix', 'norm_post_mix', 'w_in', 'b_in', 'attn_sinks', 'w_attn_proj', 'conv_dw_w', 'conv_dw_b', 'conv_ln_g', 'conv_ln_b', 'w_conv_proj', 'b_conv_proj', 'w_out', 'norm_pre_ffn', 'norm_post_ffn', 'w_up', 'ffn_dw_w', 'ffn_dw_b', 'w_down', 'loss_target', 'm_meta_tokens', 'm_norm_pre_mix', 'm_norm_post_mix', 'm_w_in', 'm_b_in', 'm_attn_sinks', 'm_w_attn_proj', 'm_conv_dw_w', 'm_conv_dw_b', 'm_conv_ln_g', 'm_conv_ln_b', 'm_w_conv_proj', 'm_b_conv_proj', 'm_w_out', 'm_norm_pre_ffn', 'm_norm_post_ffn', 'm_w_up', 'm_ffn_dw_w', 'm_ffn_dw_b', 'm_w_down', 'v_meta_tokens', 'v_norm_pre_mix', 'v_norm_post_mix', 'v_w_in', 'v_b_in', 'v_attn_sinks', 'v_w_attn_proj', 'v_conv_dw_w', 'v_conv_dw_b', 'v_conv_ln_g', 'v_conv_ln_b', 'v_w_conv_proj', 'v_b_conv_proj', 'v_w_out', 'v_norm_pre_ffn', 'v_norm_post_ffn', 'v_w_up', 'v_ffn_dw_w', 'v_ffn_dw_b', 'v_w_down']
TWIN_OUTPUTS = ['loss', 'grad_x', 'grad_meta_tokens', 'grad_norm_pre_mix', 'grad_norm_post_mix', 'grad_w_in', 'grad_b_in', 'grad_attn_sinks', 'grad_w_attn_proj', 'grad_conv_dw_w', 'grad_conv_dw_b', 'grad_conv_ln_g', 'grad_conv_ln_b', 'grad_w_conv_proj', 'grad_b_conv_proj', 'grad_w_out', 'grad_norm_pre_ffn', 'grad_norm_post_ffn', 'grad_w_up', 'grad_ffn_dw_w', 'grad_ffn_dw_b', 'grad_w_down', 'delta_meta_tokens', 'delta_norm_pre_mix', 'delta_norm_post_mix', 'delta_w_in', 'delta_b_in', 'delta_attn_sinks', 'delta_w_attn_proj', 'delta_conv_dw_w', 'delta_conv_dw_b', 'delta_conv_ln_g', 'delta_conv_ln_b', 'delta_w_conv_proj', 'delta_b_conv_proj', 'delta_w_out', 'delta_norm_pre_ffn', 'delta_norm_post_ffn', 'delta_w_up', 'delta_ffn_dw_w', 'delta_ffn_dw_b', 'delta_w_down', 'new_m_meta_tokens', 'new_m_norm_pre_mix', 'new_m_norm_post_mix', 'new_m_w_in', 'new_m_b_in', 'new_m_attn_sinks', 'new_m_w_attn_proj', 'new_m_conv_dw_w', 'new_m_conv_dw_b', 'new_m_conv_ln_g', 'new_m_conv_ln_b', 'new_m_w_conv_proj', 'new_m_b_conv_proj', 'new_m_w_out', 'new_m_norm_pre_ffn', 'new_m_norm_post_ffn', 'new_m_w_up', 'new_m_ffn_dw_w', 'new_m_ffn_dw_b', 'new_m_w_down', 'new_v_meta_tokens', 'new_v_norm_pre_mix', 'new_v_norm_post_mix', 'new_v_w_in', 'new_v_b_in', 'new_v_attn_sinks', 'new_v_w_attn_proj', 'new_v_conv_dw_w', 'new_v_conv_dw_b', 'new_v_conv_ln_g', 'new_v_conv_ln_b', 'new_v_w_conv_proj', 'new_v_b_conv_proj', 'new_v_w_out', 'new_v_norm_pre_ffn', 'new_v_norm_post_ffn', 'new_v_w_up', 'new_v_ffn_dw_w', 'new_v_ffn_dw_b', 'new_v_w_down']
TWIN_LEAF_KINDS = {'loss': 'loss', 'grad_x': 'grad_x', 'grad_meta_tokens': 'grad_w', 'grad_norm_pre_mix': 'grad_w', 'grad_norm_post_mix': 'grad_w', 'grad_w_in': 'grad_w', 'grad_b_in': 'grad_w', 'grad_attn_sinks': 'grad_w', 'grad_w_attn_proj': 'grad_w', 'grad_conv_dw_w': 'grad_w', 'grad_conv_dw_b': 'grad_w', 'grad_conv_ln_g': 'grad_w', 'grad_conv_ln_b': 'grad_w', 'grad_w_conv_proj': 'grad_w', 'grad_b_conv_proj': 'grad_w', 'grad_w_out': 'grad_w', 'grad_norm_pre_ffn': 'grad_w', 'grad_norm_post_ffn': 'grad_w', 'grad_w_up': 'grad_w', 'grad_ffn_dw_w': 'grad_w', 'grad_ffn_dw_b': 'grad_w', 'grad_w_down': 'grad_w', 'delta_meta_tokens': 'delta_w', 'delta_norm_pre_mix': 'delta_w', 'delta_norm_post_mix': 'delta_w', 'delta_w_in': 'delta_w', 'delta_b_in': 'delta_w', 'delta_attn_sinks': 'delta_w', 'delta_w_attn_proj': 'delta_w', 'delta_conv_dw_w': 'delta_w', 'delta_conv_dw_b': 'delta_w', 'delta_conv_ln_g': 'delta_w', 'delta_conv_ln_b': 'delta_w', 'delta_w_conv_proj': 'delta_w', 'delta_b_conv_proj': 'delta_w', 'delta_w_out': 'delta_w', 'delta_norm_pre_ffn': 'delta_w', 'delta_norm_post_ffn': 'delta_w', 'delta_w_up': 'delta_w', 'delta_ffn_dw_w': 'delta_w', 'delta_ffn_dw_b': 'delta_w', 'delta_w_down': 'delta_w', 'new_m_meta_tokens': 'new_m', 'new_m_norm_pre_mix': 'new_m', 'new_m_norm_post_mix': 'new_m', 'new_m_w_in': 'new_m', 'new_m_b_in': 'new_m', 'new_m_attn_sinks': 'new_m', 'new_m_w_attn_proj': 'new_m', 'new_m_conv_dw_w': 'new_m', 'new_m_conv_dw_b': 'new_m', 'new_m_conv_ln_g': 'new_m', 'new_m_conv_ln_b': 'new_m', 'new_m_w_conv_proj': 'new_m', 'new_m_b_conv_proj': 'new_m', 'new_m_w_out': 'new_m', 'new_m_norm_pre_ffn': 'new_m', 'new_m_norm_post_ffn': 'new_m', 'new_m_w_up': 'new_m', 'new_m_ffn_dw_w': 'new_m', 'new_m_ffn_dw_b': 'new_m', 'new_m_w_down': 'new_m', 'new_v_meta_tokens': 'new_v', 'new_v_norm_pre_mix': 'new_v', 'new_v_norm_post_mix': 'new_v', 'new_v_w_in': 'new_v', 'new_v_b_in': 'new_v', 'new_v_attn_sinks': 'new_v', 'new_v_w_attn_proj': 'new_v', 'new_v_conv_dw_w': 'new_v', 'new_v_conv_dw_b': 'new_v', 'new_v_conv_ln_g': 'new_v', 'new_v_conv_ln_b': 'new_v', 'new_v_w_conv_proj': 'new_v', 'new_v_b_conv_proj': 'new_v', 'new_v_w_out': 'new_v', 'new_v_norm_pre_ffn': 'new_v', 'new_v_norm_post_ffn': 'new_v', 'new_v_w_up': 'new_v', 'new_v_ffn_dw_w': 'new_v', 'new_v_ffn_dw_b': 'new_v', 'new_v_w_down': 'new_v'}


def _forward(args):
    return _fwd_reference(*[args[k] for k in FWD_PARAMS])


def _output_shape():
    def fwd():
        inp = _fwd_setup_inputs(0)
        return _fwd_reference(*[inp[k] for k in FWD_PARAMS])
    out = _jax.eval_shape(fwd)
    return out.shape, out.dtype

N_MICROBATCH = 1
ADAM_LR = 0.001
ADAM_B1 = 0.9
ADAM_B2 = 0.999
ADAM_EPS = 1e-08
ADAM_WD = 0.01
ADAM_STEP = 10
PER_EXAMPLE_BATCH_AXIS = {'x': 0, 'loss_target': 0}
SHARED_INPUTS = []
_WEIGHT_DTYPES = {'meta_tokens': _jnp.float32, 'norm_pre_mix': _jnp.float32, 'norm_post_mix': _jnp.float32, 'w_in': _jnp.float32, 'b_in': _jnp.float32, 'attn_sinks': _jnp.float32, 'w_attn_proj': _jnp.float32, 'conv_dw_w': _jnp.float32, 'conv_dw_b': _jnp.float32, 'conv_ln_g': _jnp.float32, 'conv_ln_b': _jnp.float32, 'w_conv_proj': _jnp.float32, 'b_conv_proj': _jnp.float32, 'w_out': _jnp.float32, 'norm_pre_ffn': _jnp.float32, 'norm_post_ffn': _jnp.float32, 'w_up': _jnp.float32, 'ffn_dw_w': _jnp.float32, 'ffn_dw_b': _jnp.float32, 'w_down': _jnp.float32}
MOMENT_SCALE = {'meta_tokens': 4.571643e-02, 'norm_pre_mix': 5.825983e-01, 'norm_post_mix': 3.245680e+01, 'w_in': 2.557813e-01, 'b_in': 3.734958e+00, 'attn_sinks': 2.453826e-02, 'w_attn_proj': 2.086365e-01, 'conv_dw_w': 5.926196e-01, 'conv_dw_b': 7.439600e+00, 'conv_ln_g': 2.698075e+00, 'conv_ln_b': 4.015915e+00, 'w_conv_proj': 1.572487e+00, 'b_conv_proj': 7.933440e+00, 'w_out': 1.730813e+00, 'norm_pre_ffn': 1.338775e+00, 'norm_post_ffn': 3.230846e+01, 'w_up': 5.478584e-01, 'ffn_dw_w': 6.555079e-01, 'ffn_dw_b': 2.233401e+00, 'w_down': 1.047694e+00}


def _to_microbatches(a, axis):
    t = _jnp.moveaxis(a, axis, 0)
    t = t.reshape((N_MICROBATCH, t.shape[0] // N_MICROBATCH) + t.shape[1:])
    return _jnp.moveaxis(t, 1, axis + 1)


def setup_inputs(seed: int = 0) -> dict:
    inp = _fwd_setup_inputs(seed)
    key = _jax.random.fold_in(_jax.random.key(seed), 7919)
    shape, _ = _output_shape()
    out = dict(inp)
    out["loss_target"] = _jax.random.normal(_jax.random.fold_in(key, 0), shape, _jnp.float32)
    for i, name in enumerate(TWIN_WEIGHTS):
        w = inp[name].astype(_jnp.float32)
        if MOMENT_SCALE is None:
            s = _jnp.sqrt(_jnp.mean(_jnp.square(w)) + 1e-30)
        else:
            s = MOMENT_SCALE[name]
        km, kv = _jax.random.split(_jax.random.fold_in(key, i + 1))
        out[name] = w
        out["m_" + name] = s * _jax.random.normal(km, w.shape, _jnp.float32)
        out["v_" + name] = (s * s) * _jax.random.uniform(kv, w.shape, _jnp.float32, 0.5, 1.5)
    if N_MICROBATCH > 1:
        for name, axis in PER_EXAMPLE_BATCH_AXIS.items():
            out[name] = _to_microbatches(out[name], axis)
    return {'x': out['x'], 'meta_tokens': out['meta_tokens'], 'norm_pre_mix': out['norm_pre_mix'], 'norm_post_mix': out['norm_post_mix'], 'w_in': out['w_in'], 'b_in': out['b_in'], 'attn_sinks': out['attn_sinks'], 'w_attn_proj': out['w_attn_proj'], 'conv_dw_w': out['conv_dw_w'], 'conv_dw_b': out['conv_dw_b'], 'conv_ln_g': out['conv_ln_g'], 'conv_ln_b': out['conv_ln_b'], 'w_conv_proj': out['w_conv_proj'], 'b_conv_proj': out['b_conv_proj'], 'w_out': out['w_out'], 'norm_pre_ffn': out['norm_pre_ffn'], 'norm_post_ffn': out['norm_post_ffn'], 'w_up': out['w_up'], 'ffn_dw_w': out['ffn_dw_w'], 'ffn_dw_b': out['ffn_dw_b'], 'w_down': out['w_down'], 'loss_target': out['loss_target'], 'm_meta_tokens': out['m_meta_tokens'], 'm_norm_pre_mix': out['m_norm_pre_mix'], 'm_norm_post_mix': out['m_norm_post_mix'], 'm_w_in': out['m_w_in'], 'm_b_in': out['m_b_in'], 'm_attn_sinks': out['m_attn_sinks'], 'm_w_attn_proj': out['m_w_attn_proj'], 'm_conv_dw_w': out['m_conv_dw_w'], 'm_conv_dw_b': out['m_conv_dw_b'], 'm_conv_ln_g': out['m_conv_ln_g'], 'm_conv_ln_b': out['m_conv_ln_b'], 'm_w_conv_proj': out['m_w_conv_proj'], 'm_b_conv_proj': out['m_b_conv_proj'], 'm_w_out': out['m_w_out'], 'm_norm_pre_ffn': out['m_norm_pre_ffn'], 'm_norm_post_ffn': out['m_norm_post_ffn'], 'm_w_up': out['m_w_up'], 'm_ffn_dw_w': out['m_ffn_dw_w'], 'm_ffn_dw_b': out['m_ffn_dw_b'], 'm_w_down': out['m_w_down'], 'v_meta_tokens': out['v_meta_tokens'], 'v_norm_pre_mix': out['v_norm_pre_mix'], 'v_norm_post_mix': out['v_norm_post_mix'], 'v_w_in': out['v_w_in'], 'v_b_in': out['v_b_in'], 'v_attn_sinks': out['v_attn_sinks'], 'v_w_attn_proj': out['v_w_attn_proj'], 'v_conv_dw_w': out['v_conv_dw_w'], 'v_conv_dw_b': out['v_conv_dw_b'], 'v_conv_ln_g': out['v_conv_ln_g'], 'v_conv_ln_b': out['v_conv_ln_b'], 'v_w_conv_proj': out['v_w_conv_proj'], 'v_b_conv_proj': out['v_b_conv_proj'], 'v_w_out': out['v_w_out'], 'v_norm_pre_ffn': out['v_norm_pre_ffn'], 'v_norm_post_ffn': out['v_norm_post_ffn'], 'v_w_up': out['v_w_up'], 'v_ffn_dw_w': out['v_ffn_dw_w'], 'v_ffn_dw_b': out['v_ffn_dw_b'], 'v_w_down': out['v_w_down']}


def _loss(weights, diff, rest, loss_target):
    with _jax.named_scope("forward"):
        args = {**rest, TWIN_DIFF_INPUT: diff, **{k: w.astype(_WEIGHT_DTYPES[k]) for k, w in weights.items()}}
        y = _forward(args)
    with _jax.named_scope("loss_head"):
        err = _jnp.square(y.astype(_jnp.float32) - loss_target)
        return 0.5 * _jnp.sum(_jnp.mean(err, axis=-1)) if err.ndim else 0.5 * err


def _adamw(w, g, m, v):
    m = ADAM_B1 * m + (1.0 - ADAM_B1) * g
    v = ADAM_B2 * v + (1.0 - ADAM_B2) * _jnp.square(g)
    m_hat = m / (1.0 - ADAM_B1 ** ADAM_STEP)
    v_hat = v / (1.0 - ADAM_B2 ** ADAM_STEP)
    delta = -ADAM_LR * (m_hat / (_jnp.sqrt(v_hat) + ADAM_EPS) + ADAM_WD * w)
    return delta, m, v


def reference(x, meta_tokens, norm_pre_mix, norm_post_mix, w_in, b_in, attn_sinks, w_attn_proj, conv_dw_w, conv_dw_b, conv_ln_g, conv_ln_b, w_conv_proj, b_conv_proj, w_out, norm_pre_ffn, norm_post_ffn, w_up, ffn_dw_w, ffn_dw_b, w_down, loss_target, m_meta_tokens, m_norm_pre_mix, m_norm_post_mix, m_w_in, m_b_in, m_attn_sinks, m_w_attn_proj, m_conv_dw_w, m_conv_dw_b, m_conv_ln_g, m_conv_ln_b, m_w_conv_proj, m_b_conv_proj, m_w_out, m_norm_pre_ffn, m_norm_post_ffn, m_w_up, m_ffn_dw_w, m_ffn_dw_b, m_w_down, v_meta_tokens, v_norm_pre_mix, v_norm_post_mix, v_w_in, v_b_in, v_attn_sinks, v_w_attn_proj, v_conv_dw_w, v_conv_dw_b, v_conv_ln_g, v_conv_ln_b, v_w_conv_proj, v_b_conv_proj, v_w_out, v_norm_pre_ffn, v_norm_post_ffn, v_w_up, v_ffn_dw_w, v_ffn_dw_b, v_w_down):
    given = dict(x=x, meta_tokens=meta_tokens, norm_pre_mix=norm_pre_mix, norm_post_mix=norm_post_mix, w_in=w_in, b_in=b_in, attn_sinks=attn_sinks, w_attn_proj=w_attn_proj, conv_dw_w=conv_dw_w, conv_dw_b=conv_dw_b, conv_ln_g=conv_ln_g, conv_ln_b=conv_ln_b, w_conv_proj=w_conv_proj, b_conv_proj=b_conv_proj, w_out=w_out, norm_pre_ffn=norm_pre_ffn, norm_post_ffn=norm_post_ffn, w_up=w_up, ffn_dw_w=ffn_dw_w, ffn_dw_b=ffn_dw_b, w_down=w_down, loss_target=loss_target, m_meta_tokens=m_meta_tokens, m_norm_pre_mix=m_norm_pre_mix, m_norm_post_mix=m_norm_post_mix, m_w_in=m_w_in, m_b_in=m_b_in, m_attn_sinks=m_attn_sinks, m_w_attn_proj=m_w_attn_proj, m_conv_dw_w=m_conv_dw_w, m_conv_dw_b=m_conv_dw_b, m_conv_ln_g=m_conv_ln_g, m_conv_ln_b=m_conv_ln_b, m_w_conv_proj=m_w_conv_proj, m_b_conv_proj=m_b_conv_proj, m_w_out=m_w_out, m_norm_pre_ffn=m_norm_pre_ffn, m_norm_post_ffn=m_norm_post_ffn, m_w_up=m_w_up, m_ffn_dw_w=m_ffn_dw_w, m_ffn_dw_b=m_ffn_dw_b, m_w_down=m_w_down, v_meta_tokens=v_meta_tokens, v_norm_pre_mix=v_norm_pre_mix, v_norm_post_mix=v_norm_post_mix, v_w_in=v_w_in, v_b_in=v_b_in, v_attn_sinks=v_attn_sinks, v_w_attn_proj=v_w_attn_proj, v_conv_dw_w=v_conv_dw_w, v_conv_dw_b=v_conv_dw_b, v_conv_ln_g=v_conv_ln_g, v_conv_ln_b=v_conv_ln_b, v_w_conv_proj=v_w_conv_proj, v_b_conv_proj=v_b_conv_proj, v_w_out=v_w_out, v_norm_pre_ffn=v_norm_pre_ffn, v_norm_post_ffn=v_norm_post_ffn, v_w_up=v_w_up, v_ffn_dw_w=v_ffn_dw_w, v_ffn_dw_b=v_ffn_dw_b, v_w_down=v_w_down)
    weights = {n: given[n] for n in TWIN_WEIGHTS}
    shared = {n: given[n] for n in SHARED_INPUTS}
    per_example = {n: given[n] for n in ['x']}
    grad_fn = _jax.value_and_grad(_loss, argnums=(0, 1))

    def one_microbatch(ex, loss_target):
        ex = dict(ex)
        diff = ex.pop(TWIN_DIFF_INPUT)
        return grad_fn(weights, diff, {**shared, **ex}, loss_target)

    if N_MICROBATCH == 1:
        loss, (grad_w, grad_x) = one_microbatch(per_example, given["loss_target"])
    else:
        def body(carry, xs):
            loss_sum, grad_sum = carry
            l_k, (gw_k, gx_k) = one_microbatch(xs[0], xs[1])
            with _jax.named_scope("update"):
                return (loss_sum + l_k, _jax.tree.map(_jnp.add, grad_sum, gw_k)), gx_k

        init = (_jnp.zeros((), _jnp.float32), _jax.tree.map(_jnp.zeros_like, weights))
        (loss, grad_w), grad_x = _jax.lax.scan(body, init, (per_example, given["loss_target"]))
    with _jax.named_scope("update"):
        delta_w, new_m, new_v = {}, {}, {}
        for n in TWIN_WEIGHTS:
            delta_w[n], new_m[n], new_v[n] = _adamw(weights[n], grad_w[n], given["m_" + n], given["v_" + n])
    return (loss, grad_x, *[grad_w[n] for n in TWIN_WEIGHTS], *[delta_w[n] for n in TWIN_WEIGHTS],
            *[new_m[n] for n in TWIN_WEIGHTS], *[new_v[n] for n in TWIN_WEIGHTS])
```

```python
import functools

import jax
import jax.numpy as jnp
from jax import lax
from jax.experimental import pallas as pl
from jax.experimental.pallas import tpu as pltpu

F32 = jnp.float32
BF16 = jnp.bfloat16
MESH = pl.DeviceIdType.MESH

D = 1024
HEAD_DIM = 64
N_META = 16
BLK = 128
PAD = BLK - N_META
CONV_K = 31
FFN = 2816
FFN_K = 3
QKV_W = 1280
IN_W = 5376
ROT_DIM = 16
ROPE_THETA = 500000.0
RMS_EPS = 1e-6
LN_EPS = 1e-5
NEG_INF = -1e30
SCALE = HEAD_DIM ** -0.5
N_DEV = 8

ADAM_LR = 0.001
ADAM_B1 = 0.9
ADAM_B2 = 0.999
ADAM_EPS = 1e-08
ADAM_WD = 0.01
ADAM_STEP = 10

VMEM_BYTES_V7X = 64 * 1024 * 1024
VMEM_LIMIT = VMEM_BYTES_V7X - 8 * 1024 * 1024

NT = (((1,), (1,)), ((), ()))
TN = (((0,), (0,)), ((), ()))
VM = pl.BlockSpec(memory_space=pltpu.VMEM)
ANY = pl.BlockSpec(memory_space=pl.ANY)


def _cparams(*sem):
    return pltpu.CompilerParams(dimension_semantics=sem or None, vmem_limit_bytes=VMEM_LIMIT)


def _row_tile(p):
    return 384 if p % 384 == 0 else 128


def _dot(a, b):
    return jnp.dot(a, b, preferred_element_type=F32)


def _dot_nt(a, b):
    return lax.dot_general(a, b, NT, preferred_element_type=F32)


def _dot_tn(a, b):
    return lax.dot_general(a, b, TN, preferred_element_type=F32)


def _rms(x, g):
    return x * lax.rsqrt(jnp.mean(x * x, axis=-1, keepdims=True) + RMS_EPS) * g


def _lnsilu(x, g, b):
    mu = jnp.mean(x, axis=-1, keepdims=True)
    var = jnp.mean(jnp.square(x - mu), axis=-1, keepdims=True)
    z = (x - mu) * lax.rsqrt(var + LN_EPS) * g + b
    return z * jax.nn.sigmoid(z)


def _rope(v, c, s1, s2):
    return v * c + pltpu.roll(v, BLK - 8, 1) * s1 + pltpu.roll(v, 8, 1) * s2


def _rows(i, tm):
    return i * tm + lax.broadcasted_iota(jnp.int32, (tm, 1), 0)


def _place():
    return lax.axis_index("x"), lax.axis_index("y"), lax.axis_index("c")


def _blk(ref, idx, r, dtype):
    return ref.at[pl.ds(pl.multiple_of(idx * r, 16 if dtype == BF16 else 8), r), :]


def _all_gather(arrs, name):
    n = len(arrs)

    def body(*refs):
        ins, outs = refs[:n], refs[n:2 * n]
        send_sems, recv_sems, local_sems = refs[2 * n:]
        x, y, c = _place()
        me, sibling = (x, y, c), (x, y, 1 - c)
        chips = [(1 - x, y), (x, 1 - y), (1 - x, 1 - y)]

        def rows(a, p):
            return _blk(outs[a], 4 * p[0] + 2 * p[1] + p[2], arrs[a].shape[0], arrs[a].dtype)

        def copy(a, k, block, to, src=None):
            return pltpu.make_async_remote_copy(
                src_ref=rows(a, block) if src is None else src, dst_ref=rows(a, block),
                send_sem=send_sems.at[a, k], recv_sem=recv_sems.at[a, k], device_id=to, device_id_type=MESH)

        mine = [pltpu.make_async_copy(ins[a], rows(a, me), local_sems.at[a]) for a in range(n)]
        for cp in mine:
            cp.start()
        first = []
        for a in range(n):
            first.append(copy(a, 0, me, sibling, src=ins[a]))
            first += [copy(a, 1 + j, me, (*chip, c), src=ins[a]) for j, chip in enumerate(chips)]
        for cp in first:
            cp.start()
        passed = []
        for j, chip in enumerate(chips):
            for a in range(n):
                copy(a, 1 + j, (*chip, c), me).wait_recv()
                fwd = copy(a, 4 + j, (*chip, c), sibling)
                fwd.start()
                passed.append(fwd)
        for a in range(n):
            copy(a, 0, sibling, me).wait_recv()
            for j, chip in enumerate(chips):
                copy(a, 4 + j, (*chip, 1 - c), me).wait_recv()
        for cp in first + passed:
            cp.wait_send()
        for cp in mine:
            cp.wait()

    return pl.pallas_call(
        body, name=name,
        out_shape=[jax.ShapeDtypeStruct((N_DEV * a.shape[0], a.shape[1]), a.dtype) for a in arrs],
        in_specs=[ANY] * n, out_specs=[ANY] * n,
        scratch_shapes=[pltpu.SemaphoreType.DMA((n, 7)), pltpu.SemaphoreType.DMA((n, 7)), pltpu.SemaphoreType.DMA((n,))],
    )(*arrs)


FLIPS = [(0, 0, 1), (1, 0, 0), (0, 1, 0), (1, 1, 0), (1, 0, 1), (0, 1, 1), (1, 1, 1)]


def _reduce_scatter(arrs, name):
    n = len(arrs)

    def body(*refs):
        ins, outs = refs[:n], refs[n:2 * n]
        send_sems, recv_sems, local_sems = refs[2 * n:]
        x, y, c = _place()
        me = 4 * x + 2 * y + c

        def flip(v, f):
            return 1 - v if f else v

        def blk(ref, a, idx):
            return _blk(ref, idx, arrs[a].shape[0] // N_DEV, arrs[a].dtype)

        mine = [pltpu.make_async_copy(blk(ins[a], a, me), blk(outs[a], a, me), local_sems.at[a]) for a in range(n)]
        for cp in mine:
            cp.start()
        copies = []
        for k, f in enumerate(FLIPS):
            peer = (flip(x, f[0]), flip(y, f[1]), flip(c, f[2]))
            pidx = 4 * peer[0] + 2 * peer[1] + peer[2]
            for a in range(n):
                copies.append((pltpu.make_async_remote_copy(
                    src_ref=blk(ins[a], a, pidx), dst_ref=blk(outs[a], a, me),
                    send_sem=send_sems.at[a, k], recv_sem=recv_sems.at[a, k], device_id=peer, device_id_type=MESH),
                    pltpu.make_async_remote_copy(
                    src_ref=blk(ins[a], a, pidx), dst_ref=blk(outs[a], a, pidx),
                    send_sem=send_sems.at[a, k], recv_sem=recv_sems.at[a, k], device_id=peer, device_id_type=MESH)))
        for snd, _ in copies:
            snd.start()
        for _, rcv in copies:
            rcv.wait_recv()
        for snd, _ in copies:
            snd.wait_send()
        for cp in mine:
            cp.wait()

    return pl.pallas_call(
        body, name=name,
        out_shape=[jax.ShapeDtypeStruct(a.shape, a.dtype) for a in arrs],
        in_specs=[ANY] * n, out_specs=[ANY] * n,
        scratch_shapes=[pltpu.SemaphoreType.DMA((n, 7)), pltpu.SemaphoreType.DMA((n, 7)), pltpu.SemaphoreType.DMA((n,))],
    )(*arrs)


def _in_proj(h0p, gain, w_int, b_in, tabs):
    p = h0p.shape[0]
    tm = _row_tile(p)

    def body(h_ref, g_ref, w_ref, b_ref, t_ref, n1_ref, q_ref, kv_ref, ag_ref, gt_ref):
        n = _rms(h_ref[...], g_ref[...]).astype(BF16)
        n1_ref[...] = n
        c, s1, s2 = t_ref[:, 0:128], t_ref[:, 128:256], t_ref[:, 256:384]

        def mm(c0, w):
            return _dot_nt(n, w_ref[c0:c0 + w, :]) + b_ref[:, c0:c0 + w]

        for j in range(4):
            acc = mm(256 * j, 256)
            for t in range(2):
                lo = 256 * j + 128 * t
                q_ref[:, lo:lo + 128] = (_rope(acc[:, 128 * t:128 * (t + 1)], c, s1, s2) * SCALE).astype(BF16)
        acc = mm(1024, 256)
        kv_ref[:, 0:128] = _rope(acc[:, 0:128], c, s1, s2).astype(BF16)
        kv_ref[:, 128:256] = acc[:, 128:256].astype(BF16)
        for j in range(8):
            ag_ref[:, 256 * j:256 * (j + 1)] = mm(QKV_W + 256 * j, 256).astype(BF16)
        for j in range(8):
            gt_ref[:, 256 * j:256 * (j + 1)] = mm(QKV_W + 2048 + 256 * j, 256).astype(BF16)

    def row(w):
        return pl.BlockSpec((tm, w), lambda i: (i, 0))

    return pl.pallas_call(
        body, name="in_proj", grid=(p // tm,),
        in_specs=[row(D), VM, VM, VM, row(384)],
        out_specs=[row(D), row(D), row(256), row(2048), row(2048)],
        out_shape=[jax.ShapeDtypeStruct((p, w), BF16) for w in (D, D, 256, 2048, 2048)],
        compiler_params=_cparams("parallel"),
    )(h0p, gain, w_int, b_in, tabs)


def _attn_probs(n, h, q_ref, km_ref, kp_ref, kc_ref, sink_ref):
    lane = lax.broadcasted_iota(jnp.int32, (BLK, BLK), 1)
    lo = lane < HEAD_DIM
    lo3 = lax.broadcasted_iota(jnp.int32, (3 * BLK, BLK), 1) < HEAD_DIM

    def dup(lanes):
        cat = jnp.concatenate([km_ref[:, lanes], kp_ref[:, lanes], kc_ref[:, lanes]], axis=0).astype(F32)
        rolled = pltpu.roll(cat, HEAD_DIM, 1)
        return (jnp.where(lo3, cat, rolled) if h == 0 else jnp.where(lo3, rolled, cat)).astype(BF16)

    k2 = dup(slice(0, 128))
    v2 = dup(slice(128, 256))
    qs = _stack_heads(q_ref, h, lo)
    s = _dot_nt(qs, k2).reshape(8, BLK, 3 * BLK)

    r = lax.broadcasted_iota(jnp.int32, (BLK, BLK), 0)
    tq = BLK * n + r - PAD
    t_m = lane - PAD
    t_p = BLK * (n - 1) + lane - PAD
    t_c = BLK * n + lane - PAD
    ok_m = jnp.logical_and(t_m >= 0, t_m <= tq)
    ok_p = jnp.logical_and(t_p >= N_META, tq - t_p < BLK)
    ok_c = jnp.logical_and(t_c >= N_META, t_c <= tq)
    bias = jnp.concatenate([jnp.where(ok, 0.0, NEG_INF).astype(F32) for ok in (ok_m, ok_p, ok_c)], axis=1)
    s = s + bias[None]

    gidx = lax.broadcasted_iota(jnp.int32, (8, 1, 1), 0)
    sink = jnp.zeros((8, 1, 1), F32)
    for g in range(8):
        sink = jnp.where(gidx == g, sink_ref[0, 8 * h + g], sink)
    m = jnp.maximum(jnp.max(s, axis=-1, keepdims=True), sink)
    e = jnp.exp(s - m)
    es = jnp.exp(sink - m)
    inv = 1.0 / (jnp.sum(e, axis=-1, keepdims=True) + es)
    return qs, k2, v2, e * inv, es * inv


def _stack_heads(ref, h, lo):
    pieces = []
    for jp in range(4):
        v = ref[:, BLK * (4 * h + jp):BLK * (4 * h + jp + 1)]
        zero = jnp.zeros_like(v)
        pieces += [jnp.where(lo, v, zero), jnp.where(lo, zero, v)]
    return jnp.concatenate(pieces, axis=0)


def _unstack_heads(v, jp, lo):
    return jnp.where(lo, v[256 * jp:256 * jp + 128], v[256 * jp + 128:256 * jp + 256])


def _attn_fwd(q, kv, sinks):
    p = q.shape[0]
    nb = p // BLK

    def body(q_ref, km_ref, kp_ref, kc_ref, sink_ref, o_ref):
        n = pl.program_id(0)
        lo = lax.broadcasted_iota(jnp.int32, (BLK, BLK), 1) < HEAD_DIM
        for h in range(2):
            _, _, v2, pn, _ = _attn_probs(n, h, q_ref, km_ref, kp_ref, kc_ref, sink_ref)
            o = _dot(pn.reshape(8 * BLK, 3 * BLK).astype(BF16), v2)
            for jp in range(4):
                o_ref[:, BLK * (4 * h + jp):BLK * (4 * h + jp + 1)] = _unstack_heads(o, jp, lo).astype(BF16)

    return pl.pallas_call(
        body, name="attn_fwd", grid=(nb,),
        in_specs=[pl.BlockSpec((BLK, D), lambda i: (i, 0)),
                  pl.BlockSpec((BLK, 256), lambda i: (0, 0)),
                  pl.BlockSpec((BLK, 256), lambda i: (jnp.maximum(i - 1, 0), 0)),
                  pl.BlockSpec((BLK, 256), lambda i: (i, 0)),
                  pl.BlockSpec(memory_space=pltpu.SMEM)],
        out_specs=pl.BlockSpec((BLK, D), lambda i: (i, 0)),
        out_shape=jax.ShapeDtypeStruct((p, D), BF16),
        compiler_params=_cparams("parallel"),
    )(q, kv, kv, kv, sinks)


def _conv31_fwd(ag, w32, b):
    p = ag.shape[0]
    nch = p // BLK

    def body(a_ref, g_ref, w_ref, b_ref, o_ref, gp):
        gp[0:32, :] = jnp.zeros((32, BLK), F32)
        for ci in range(nch):
            r0 = BLK * ci
            glu = a_ref[r0:r0 + BLK, :].astype(F32) * jax.nn.sigmoid(g_ref[r0:r0 + BLK, :].astype(F32))
            if ci == 0:
                glu = jnp.where(_rows(0, BLK) >= PAD, glu, 0.0)
            gp[32 + r0:32 + r0 + BLK, :] = glu
        for ci in range(nch):
            r0 = BLK * ci
            acc = jnp.broadcast_to(b_ref[...], (BLK, BLK))
            for j in range(CONV_K):
                acc = acc + w_ref[j:j + 1, :] * gp[r0 + j + 2:r0 + j + 2 + BLK, :]
            o_ref[r0:r0 + BLK, :] = acc

    return pl.pallas_call(
        body, name="conv31_fwd", grid=(D // BLK,),
        in_specs=[pl.BlockSpec((p, BLK), lambda j: (0, j)), pl.BlockSpec((p, BLK), lambda j: (0, 8 + j)),
                  pl.BlockSpec((32, BLK), lambda j: (0, j)), pl.BlockSpec((1, BLK), lambda j: (0, j))],
        out_specs=pl.BlockSpec((p, BLK), lambda j: (0, j)),
        out_shape=jax.ShapeDtypeStruct((p, D), F32),
        scratch_shapes=[pltpu.VMEM((p + 32, BLK), F32)],
        compiler_params=_cparams("parallel"),
    )(ag, ag, w32, b)


def _mixer_fwd(ao, c0, gates, h0p, wa, wc, wo, vecs):
    p = ao.shape[0]
    tm = _row_tile(p)

    def body(ao_ref, c0_ref, gt_ref, h_ref, wa_ref, wc_ref, wo_ref, v_ref,
             c1_ref, at_ref, cv_ref, mg_ref, mix_ref, h1_ref, n2_ref):
        i = pl.program_id(0)
        c1 = _lnsilu(c0_ref[...], v_ref[0:1, :], v_ref[1:2, :]).astype(BF16)
        c1_ref[...] = c1
        attn = _dot(ao_ref[...], wa_ref[...])
        conv = _dot(c1, wc_ref[...]) + v_ref[2:3, :]
        at_ref[...] = attn.astype(BF16)
        cv_ref[...] = conv.astype(BF16)
        merged = (jax.nn.sigmoid(gt_ref[:, 0:D].astype(F32)) * attn
                  + jax.nn.sigmoid(gt_ref[:, D:2 * D].astype(F32)) * conv).astype(BF16)
        mg_ref[...] = merged
        mix = _dot(merged, wo_ref[...])
        mix_ref[...] = mix
        h1 = jnp.where(_rows(i, tm) >= PAD, h_ref[...] + _rms(mix, v_ref[3:4, :]), 0.0)
        h1_ref[...] = h1
        n2_ref[...] = _rms(h1, v_ref[4:5, :]).astype(BF16)

    def row(w):
        return pl.BlockSpec((tm, w), lambda i: (i, 0))

    return pl.pallas_call(
        body, name="mixer_fwd", grid=(p // tm,),
        in_specs=[row(D), row(D), row(2 * D), row(D), VM, VM, VM, VM],
        out_specs=[row(D)] * 7,
        out_shape=[jax.ShapeDtypeStruct((p, D), t) for t in (BF16, BF16, BF16, BF16, F32, F32, BF16)],
        compiler_params=_cparams("parallel"),
    )(ao, c0, gates, h0p, wa, wc, wo, vecs)


def _mm_nt(a, w_t, name):
    p, k = a.shape
    n = w_t.shape[0]
    tm = _row_tile(p)
    ch = 512

    def body(a_ref, w_ref, o_ref):
        a_v = a_ref[...]
        for c0 in range(0, n, ch):
            o_ref[:, c0:c0 + ch] = _dot_nt(a_v, w_ref[c0:c0 + ch, :]).astype(BF16)

    return pl.pallas_call(
        body, name=name, grid=(p // tm,),
        in_specs=[pl.BlockSpec((tm, k), lambda i: (i, 0)), VM],
        out_specs=pl.BlockSpec((tm, n), lambda i: (i, 0)),
        out_shape=jax.ShapeDtypeStruct((p, n), BF16),
        compiler_params=_cparams("parallel"),
    )(a, w_t)


def _conv3(xp_ref, w_ref, r0):
    return (w_ref[0:1, :] * xp_ref[r0 + 6:r0 + 6 + BLK, :] + w_ref[1:2, :] * xp_ref[r0 + 7:r0 + 7 + BLK, :]
            + w_ref[2:3, :] * xp_ref[r0 + 8:r0 + 8 + BLK, :])


def _ffn_slab_specs(p):
    ncol = FFN // BLK
    return [pl.BlockSpec((p, BLK), lambda j: (0, j)), pl.BlockSpec((p, BLK), lambda j: (0, ncol + j)),
            pl.BlockSpec((FFN_K, BLK), lambda j: (0, j)), pl.BlockSpec((FFN_K, BLK), lambda j: (0, ncol + j)),
            pl.BlockSpec((1, BLK), lambda j: (0, j)), pl.BlockSpec((1, BLK), lambda j: (0, ncol + j))]


def _fill_shifted(dst, src_ref, nch):
    dst[0:8, :] = jnp.zeros((8, BLK), F32)
    for ci in range(nch):
        dst[8 + BLK * ci:8 + BLK * (ci + 1), :] = src_ref[BLK * ci:BLK * (ci + 1), :].astype(F32)


def _ffn_act(u0, fw, fb):
    p = u0.shape[0]
    nch = p // BLK

    def body(g_ref, v_ref, wg_ref, wv_ref, bg_ref, bv_ref, o_ref, xg, xv):
        _fill_shifted(xg, g_ref, nch)
        _fill_shifted(xv, v_ref, nch)
        for ci in range(nch):
            r0 = BLK * ci
            ug = _conv3(xg, wg_ref, r0) + bg_ref[...]
            uv = _conv3(xv, wv_ref, r0) + bv_ref[...]
            o_ref[r0:r0 + BLK, :] = (ug * jax.nn.sigmoid(ug) * uv).astype(BF16)

    return pl.pallas_call(
        body, name="ffn_act", grid=(FFN // BLK,),
        in_specs=_ffn_slab_specs(p),
        out_specs=pl.BlockSpec((p, BLK), lambda j: (0, j)),
        out_shape=jax.ShapeDtypeStruct((p, FFN), BF16),
        scratch_shapes=[pltpu.VMEM((p + 8, BLK), F32)] * 2,
        compiler_params=_cparams("parallel"),
    )(u0, u0, fw, fw, fb, fb)


def _ffn_down_loss(act, wd, h1, tgt, gain):
    p = act.shape[0]
    tm = _row_tile(p)

    def body(a_ref, w_ref, h_ref, t_ref, g_ref, df_ref, da_ref, dy_ref, acc_ref):
        i = pl.program_id(0)

        @pl.when(i == 0)
        def _():
            acc_ref[...] = jnp.zeros_like(acc_ref)

        ffn = _dot(a_ref[...], w_ref[...])
        r, vjp = jax.vjp(_rms, ffn, g_ref[...])
        diff = jnp.where(_rows(i, tm) >= BLK, h_ref[...] + r - t_ref[...], 0.0)
        dy = diff * (1.0 / D)
        dffn, dg = vjp(dy)
        acc_ref[0:1, :] += dg
        acc_ref[1:2, :] += jnp.sum(diff * diff, axis=0, keepdims=True) * (0.5 / D)
        dy_ref[...] = dy
        dfb = dffn.astype(BF16)
        df_ref[...] = dfb
        for c0 in range(0, FFN, 256):
            da_ref[:, c0:c0 + 256] = _dot_nt(dfb, w_ref[c0:c0 + 256, :]).astype(BF16)

    def row(w):
        return pl.BlockSpec((tm, w), lambda i: (i, 0))

    return pl.pallas_call(
        body, name="ffn_down_loss", grid=(p // tm,),
        in_specs=[row(FFN), VM, row(D), row(D), VM],
        out_specs=[row(D), row(FFN), row(D), pl.BlockSpec((8, D), lambda i: (0, 0))],
        out_shape=[jax.ShapeDtypeStruct((p, D), BF16), jax.ShapeDtypeStruct((p, FFN), BF16),
                   jax.ShapeDtypeStruct((p, D), F32), jax.ShapeDtypeStruct((8, D), F32)],
        compiler_params=_cparams("arbitrary"),
    )(act, wd, h1, tgt, gain)


def _mm_tn(pieces, b, name, col_sums=False):
    p, n = b.shape
    tk = 256
    nblk = [a.shape[1] // tk for a in pieces]
    offs = [sum(nblk[:q]) for q in range(len(pieces))]
    total = sum(nblk)
    npc = len(pieces)

    def body(*refs):
        a_refs, b_ref, o_ref = refs[:npc], refs[npc], refs[npc + 1]
        i = pl.program_id(0)
        for q, a_ref in enumerate(a_refs):
            @pl.when(jnp.logical_and(i >= offs[q], i < offs[q] + nblk[q]))
            def _(a_ref=a_ref):
                a_v = a_ref[...]
                o_ref[...] = _dot_tn(a_v, b_ref[...]).astype(BF16)
                if col_sums:
                    refs[npc + 2][...] = jnp.sum(a_v.astype(F32), axis=0, keepdims=True)

    def a_spec(q):
        return pl.BlockSpec((p, tk), lambda i: (0, jnp.clip(i - offs[q], 0, nblk[q] - 1)))

    out_specs = [pl.BlockSpec((tk, n), lambda i: (i, 0))]
    out_shape = [jax.ShapeDtypeStruct((total * tk, n), BF16)]
    if col_sums:
        out_specs.append(pl.BlockSpec((1, tk), lambda i: (0, i)))
        out_shape.append(jax.ShapeDtypeStruct((1, total * tk), F32))
    res = pl.pallas_call(
        body, name=name, grid=(total,),
        in_specs=[a_spec(q) for q in range(npc)] + [VM],
        out_specs=out_specs, out_shape=out_shape,
        compiler_params=_cparams("parallel"),
    )(*pieces, b)
    return res if col_sums else res[0]


def _ffn_act_bwd(u0, dact, fw, fb):
    p = u0.shape[0]
    nch = p // BLK
    ncol = FFN // BLK

    def body(g_ref, v_ref, wg_ref, wv_ref, bg_ref, bv_ref, da_ref,
             dg_ref, dv_ref, gwg_ref, gwv_ref, gbg_ref, gbv_ref, xg, xv, eg, ev):
        _fill_shifted(xg, g_ref, nch)
        _fill_shifted(xv, v_ref, nch)
        eg[p:p + 8, :] = jnp.zeros((8, BLK), F32)
        ev[p:p + 8, :] = jnp.zeros((8, BLK), F32)
        for ci in range(nch):
            r0 = BLK * ci
            ug = _conv3(xg, wg_ref, r0) + bg_ref[...]
            uv = _conv3(xv, wv_ref, r0) + bv_ref[...]
            sg = jax.nn.sigmoid(ug)
            d = da_ref[r0:r0 + BLK, :].astype(F32)
            eg[r0:r0 + BLK, :] = d * uv * (sg * (1.0 + ug * (1.0 - sg)))
            ev[r0:r0 + BLK, :] = d * ug * sg
        for e_s, x_s, w_ref, d_ref, gw_ref, gb_ref in ((eg, xg, wg_ref, dg_ref, gwg_ref, gbg_ref),
                                                      (ev, xv, wv_ref, dv_ref, gwv_ref, gbv_ref)):
            sums = [jnp.zeros((BLK, BLK), F32) for _ in range(FFN_K + 1)]
            for ci in range(nch):
                r0 = BLK * ci
                e0 = e_s[r0:r0 + BLK, :]
                du = (w_ref[2:3, :] * e0 + w_ref[1:2, :] * e_s[r0 + 1:r0 + 1 + BLK, :]
                      + w_ref[0:1, :] * e_s[r0 + 2:r0 + 2 + BLK, :])
                if ci == 0:
                    du = jnp.where(_rows(0, BLK) >= PAD, du, 0.0)
                d_ref[r0:r0 + BLK, :] = du.astype(BF16)
                for j in range(FFN_K):
                    sums[j] = sums[j] + e0 * x_s[r0 + 6 + j:r0 + 6 + j + BLK, :]
                sums[FFN_K] = sums[FFN_K] + e0
            for j in range(FFN_K):
                gw_ref[j:j + 1, :] = jnp.sum(sums[j], axis=0, keepdims=True)
            gb_ref[...] = jnp.sum(sums[FFN_K], axis=0, keepdims=True)

    slab = pl.BlockSpec((p, BLK), lambda j: (0, j))
    wspec = pl.BlockSpec((FFN_K, BLK), lambda j: (0, j))
    bspec = pl.BlockSpec((1, BLK), lambda j: (0, j))
    return pl.pallas_call(
        body, name="ffn_act_bwd", grid=(ncol,),
        in_specs=_ffn_slab_specs(p) + [slab],
        out_specs=[slab, slab, wspec, wspec, bspec, bspec],
        out_shape=[jax.ShapeDtypeStruct((p, FFN), BF16)] * 2 + [jax.ShapeDtypeStruct((FFN_K, FFN), F32)] * 2
        + [jax.ShapeDtypeStruct((1, FFN), F32)] * 2,
        scratch_shapes=[pltpu.VMEM((p + 8, BLK), F32)] * 4,
        compiler_params=_cparams("parallel"),
    )(u0, u0, fw, fw, fb, fb, dact)


def _ffn_in_bwd(dug, duv, w_upt, h1, dy, gain):
    p = h1.shape[0]
    tm = _row_tile(p)

    def body(dg_ref, dv_ref, w_ref, h_ref, dy_ref, g_ref, o_ref, acc_ref):
        i = pl.program_id(0)

        @pl.when(i == 0)
        def _():
            acc_ref[...] = jnp.zeros_like(acc_ref)

        dn = _dot(dg_ref[...], w_ref[0:FFN, :]) + _dot(dv_ref[...], w_ref[FFN:2 * FFN, :])
        _, vjp = jax.vjp(_rms, h_ref[...], g_ref[...])
        dh, dg = vjp(dn)
        o_ref[...] = dy_ref[...] + dh
        acc_ref[0:1, :] += dg

    def row(w):
        return pl.BlockSpec((tm, w), lambda i: (i, 0))

    return pl.pallas_call(
        body, name="ffn_in_bwd", grid=(p // tm,),
        in_specs=[row(FFN), row(FFN), VM, row(D), row(D), VM],
        out_specs=[row(D), pl.BlockSpec((8, D), lambda i: (0, 0))],
        out_shape=[jax.ShapeDtypeStruct((p, D), F32), jax.ShapeDtypeStruct((8, D), F32)],
        compiler_params=_cparams("arbitrary"),
    )(dug, duv, w_upt, h1, dy, gain)


def _mixer_bwd(dh1, mix, attn, conv, gates, c0, wa, wc, wo, vecs):
    p = dh1.shape[0]
    tm = _row_tile(p)

    def body(dh_ref, mix_ref, at_ref, cv_ref, gt_ref, c0_ref, wa_ref, wc_ref, wo_ref, v_ref,
             dmix_ref, dat_ref, dcv_ref, dgt_ref, dao_ref, dc0_ref, acc_ref):
        i = pl.program_id(0)

        @pl.when(i == 0)
        def _():
            acc_ref[...] = jnp.zeros_like(acc_ref)

        _, vjp = jax.vjp(_rms, mix_ref[...], v_ref[3:4, :])
        dmix, dgp = vjp(dh_ref[...])
        dmix = dmix.astype(BF16)
        dmix_ref[...] = dmix
        dmg = _dot_nt(dmix, wo_ref[...])
        sa = jax.nn.sigmoid(gt_ref[:, 0:D].astype(F32))
        sc = jax.nn.sigmoid(gt_ref[:, D:2 * D].astype(F32))
        dat = dmg * sa
        dcv = dmg * sc
        dgt_ref[:, 0:D] = (dmg * at_ref[...].astype(F32) * sa * (1.0 - sa)).astype(BF16)
        dgt_ref[:, D:2 * D] = (dmg * cv_ref[...].astype(F32) * sc * (1.0 - sc)).astype(BF16)
        datb = dat.astype(BF16)
        dcvb = dcv.astype(BF16)
        dat_ref[...] = datb
        dcv_ref[...] = dcvb
        dao_ref[...] = _dot_nt(datb, wa_ref[...]).astype(BF16)
        dc1 = _dot_nt(dcvb, wc_ref[...])
        _, vjp2 = jax.vjp(_lnsilu, c0_ref[...], v_ref[0:1, :], v_ref[1:2, :])
        dc0, dlg, dlb = vjp2(dc1)
        dc0_ref[...] = dc0
        acc_ref[0:1, :] += dgp
        acc_ref[1:2, :] += jnp.sum(dcv, axis=0, keepdims=True)
        acc_ref[2:3, :] += dlg
        acc_ref[3:4, :] += dlb

    def row(w):
        return pl.BlockSpec((tm, w), lambda i: (i, 0))

    return pl.pallas_call(
        body, name="mixer_bwd", grid=(p // tm,),
        in_specs=[row(D), row(D), row(D), row(D), row(2 * D), row(D), VM, VM, VM, VM],
        out_specs=[row(D), row(D), row(D), row(2 * D), row(D), row(D), pl.BlockSpec((8, D), lambda i: (0, 0))],
        out_shape=[jax.ShapeDtypeStruct((p, D), BF16)] * 3 + [jax.ShapeDtypeStruct((p, 2 * D), BF16),
                                                             jax.ShapeDtypeStruct((p, D), BF16),
                                                             jax.ShapeDtypeStruct((p, D), F32),
                                                             jax.ShapeDtypeStruct((8, D), F32)],
        compiler_params=_cparams("arbitrary"),
    )(dh1, mix, attn, conv, gates, c0, wa, wc, wo, vecs)


def _conv31_bwd(ag, dc0, w32):
    p = ag.shape[0]
    nch = p // BLK

    def body(a_ref, g_ref, dc_ref, w_ref, da_ref, dg_ref, gw_ref, gb_ref, gp, dp):
        gp[0:32, :] = jnp.zeros((32, BLK), F32)
        dp[p:p + 32, :] = jnp.zeros((32, BLK), F32)
        bsum = jnp.zeros((BLK, BLK), F32)
        for ci in range(nch):
            r0 = BLK * ci
            glu = a_ref[r0:r0 + BLK, :].astype(F32) * jax.nn.sigmoid(g_ref[r0:r0 + BLK, :].astype(F32))
            if ci == 0:
                glu = jnp.where(_rows(0, BLK) >= PAD, glu, 0.0)
            gp[32 + r0:32 + r0 + BLK, :] = glu
            d = dc_ref[r0:r0 + BLK, :]
            dp[r0:r0 + BLK, :] = d
            bsum = bsum + d
        gb_ref[...] = jnp.sum(bsum, axis=0, keepdims=True)
        for ci in range(nch):
            r0 = BLK * ci
            acc = jnp.zeros((BLK, BLK), F32)
            for j in range(CONV_K):
                acc = acc + w_ref[j:j + 1, :] * dp[r0 + 30 - j:r0 + 30 - j + BLK, :]
            if ci == 0:
                acc = jnp.where(_rows(0, BLK) >= PAD, acc, 0.0)
            a = a_ref[r0:r0 + BLK, :].astype(F32)
            sg = jax.nn.sigmoid(g_ref[r0:r0 + BLK, :].astype(F32))
            da_ref[r0:r0 + BLK, :] = (acc * sg).astype(BF16)
            dg_ref[r0:r0 + BLK, :] = (acc * a * sg * (1.0 - sg)).astype(BF16)
        for j in range(CONV_K):
            acc = jnp.zeros((BLK, BLK), F32)
            for ci in range(nch):
                r0 = BLK * ci
                acc = acc + dp[r0:r0 + BLK, :] * gp[r0 + j + 2:r0 + j + 2 + BLK, :]
            gw_ref[j:j + 1, :] = jnp.sum(acc, axis=0, keepdims=True)
        gw_ref[CONV_K:32, :] = jnp.zeros((32 - CONV_K, BLK), F32)

    slab = pl.BlockSpec((p, BLK), lambda j: (0, j))
    return pl.pallas_call(
        body, name="conv31_bwd", grid=(D // BLK,),
        in_specs=[slab, pl.BlockSpec((p, BLK), lambda j: (0, 8 + j)), slab, pl.BlockSpec((32, BLK), lambda j: (0, j))],
        out_specs=[slab, slab, pl.BlockSpec((32, BLK), lambda j: (0, j)), pl.BlockSpec((1, BLK), lambda j: (0, j))],
        out_shape=[jax.ShapeDtypeStruct((p, D), BF16)] * 2 + [jax.ShapeDtypeStruct((32, D), F32),
                                                             jax.ShapeDtypeStruct((1, D), F32)],
        scratch_shapes=[pltpu.VMEM((p + 32, BLK), F32)] * 2,
        compiler_params=_cparams("parallel"),
    )(ag, ag, dc0, w32)


def _attn_bwd(q, kv, dao, sinks, tabs):
    p = q.shape[0]
    nb = p // BLK

    def body(q_ref, km_ref, kp_ref, kc_ref, do_ref, sink_ref, t_ref, dqkv_ref, dsink_ref, carry, macc):
        i = pl.program_id(0)
        n = nb - 1 - i

        @pl.when(i == 0)
        def _():
            carry[...] = jnp.zeros_like(carry)
            macc[...] = jnp.zeros_like(macc)
            dsink_ref[...] = jnp.zeros_like(dsink_ref)

        lane = lax.broadcasted_iota(jnp.int32, (BLK, BLK), 1)
        lo = lane < HEAD_DIM
        lo3 = lax.broadcasted_iota(jnp.int32, (3 * BLK, BLK), 1) < HEAD_DIM
        lane8 = lax.broadcasted_iota(jnp.int32, (8, BLK), 1)
        c, s1, s2 = t_ref[:, 0:128], -t_ref[:, 128:256], -t_ref[:, 256:384]
        dk = jnp.zeros((3 * BLK, BLK), F32)
        dv = jnp.zeros((3 * BLK, BLK), F32)
        for h in range(2):
            qs, k2, v2, pn, ps = _attn_probs(n, h, q_ref, km_ref, kp_ref, kc_ref, sink_ref)
            dos = _stack_heads(do_ref, h, lo)
            dp = _dot_nt(dos, v2).reshape(8, BLK, 3 * BLK)
            delta = jnp.sum(pn * dp, axis=-1, keepdims=True)
            ds = (pn * (dp - delta)).reshape(8 * BLK, 3 * BLK).astype(BF16)
            dsk = -jnp.sum(ps * delta, axis=1, keepdims=True)
            for g in range(8):
                dsink_ref[...] += jnp.where(lane8 == 8 * h + g, dsk[g], 0.0)
            dq = _dot(ds, k2)
            for jp in range(4):
                lo_c = BLK * (4 * h + jp)
                dqkv_ref[:, lo_c:lo_c + BLK] = (_rope(_unstack_heads(dq, jp, lo), c, s1, s2) * SCALE).astype(BF16)
            dk2 = _dot_tn(ds, qs)
            dv2 = _dot_tn(pn.reshape(8 * BLK, 3 * BLK).astype(BF16), dos)
            dk2 = dk2 + pltpu.roll(dk2, HEAD_DIM, 1)
            dv2 = dv2 + pltpu.roll(dv2, HEAD_DIM, 1)
            own = lo3 if h == 0 else jnp.logical_not(lo3)
            dk = jnp.where(own, dk2, dk)
            dv = jnp.where(own, dv2, dv)
        macc[:, 0:BLK] += dk[0:BLK]
        macc[:, BLK:2 * BLK] += dv[0:BLK]
        last = (n == 0).astype(F32)
        dk_c = dk[2 * BLK:3 * BLK] + carry[:, 0:BLK] + last * macc[:, 0:BLK]
        dv_c = dv[2 * BLK:3 * BLK] + carry[:, BLK:2 * BLK] + last * macc[:, BLK:2 * BLK]
        carry[:, 0:BLK] = dk[BLK:2 * BLK]
        carry[:, BLK:2 * BLK] = dv[BLK:2 * BLK]
        dqkv_ref[:, D:D + BLK] = _rope(dk_c, c, s1, s2).astype(BF16)
        dqkv_ref[:, D + BLK:D + 2 * BLK] = dv_c.astype(BF16)

    def rev(w):
        return pl.BlockSpec((BLK, w), lambda i: (nb - 1 - i, 0))

    return pl.pallas_call(
        body, name="attn_bwd", grid=(nb,),
        in_specs=[rev(D),
                  pl.BlockSpec((BLK, 256), lambda i: (0, 0)),
                  pl.BlockSpec((BLK, 256), lambda i: (jnp.maximum(nb - 2 - i, 0), 0)),
                  rev(256), rev(D),
                  pl.BlockSpec(memory_space=pltpu.SMEM), rev(384)],
        out_specs=[rev(QKV_W), pl.BlockSpec((8, BLK), lambda i: (0, 0))],
        out_shape=[jax.ShapeDtypeStruct((p, QKV_W), BF16), jax.ShapeDtypeStruct((8, BLK), F32)],
        scratch_shapes=[pltpu.VMEM((BLK, 256), F32)] * 2,
        compiler_params=_cparams("arbitrary"),
    )(q, kv, kv, kv, dao, sinks, tabs)


def _in_bwd(dqkv, da, dg, dgt, w_int, h0p, dh1, gain):
    p = h0p.shape[0]
    tm = _row_tile(p)

    def body(dq_ref, da_ref, dg_ref, dt_ref, w_ref, h_ref, dh_ref, g_ref, o_ref, acc_ref):
        i = pl.program_id(0)

        @pl.when(i == 0)
        def _():
            acc_ref[...] = jnp.zeros_like(acc_ref)

        dn = (_dot(dq_ref[...], w_ref[0:QKV_W, :]) + _dot(da_ref[...], w_ref[QKV_W:QKV_W + D, :])
              + _dot(dg_ref[...], w_ref[QKV_W + D:QKV_W + 2 * D, :]) + _dot(dt_ref[...], w_ref[QKV_W + 2 * D:IN_W, :]))
        _, vjp = jax.vjp(_rms, h_ref[...], g_ref[...])
        dh, dgain = vjp(dn)
        o_ref[...] = dh_ref[...] + dh
        acc_ref[0:1, :] += dgain

    def row(w):
        return pl.BlockSpec((tm, w), lambda i: (i, 0))

    return pl.pallas_call(
        body, name="in_bwd", grid=(p // tm,),
        in_specs=[row(QKV_W), row(D), row(D), row(2 * D), VM, row(D), row(D), VM],
        out_specs=[row(D), pl.BlockSpec((8, D), lambda i: (0, 0))],
        out_shape=[jax.ShapeDtypeStruct((p, D), F32), jax.ShapeDtypeStruct((8, D), F32)],
        compiler_params=_cparams("arbitrary"),
    )(dqkv, da, dg, dgt, w_int, h0p, dh1, gain)


def _sum_slots(slots, name):
    r = slots.shape[0] // N_DEV
    cols = slots.shape[1]
    tr = r if r <= 352 else (r // 2 if (r // 2) % 16 == 0 else r // 3)
    steps = r // tr

    def body(*refs):
        acc = refs[0][...].astype(F32)
        for s in range(1, N_DEV):
            acc = acc + refs[s][...].astype(F32)
        refs[N_DEV][...] = acc

    return pl.pallas_call(
        body, name=name, grid=(steps,),
        in_specs=[pl.BlockSpec((tr, cols), functools.partial(lambda i, s: (s * steps + i, 0), s=s)) for s in range(N_DEV)],
        out_specs=pl.BlockSpec((tr, cols), lambda i: (i, 0)),
        out_shape=jax.ShapeDtypeStruct((r, cols), F32),
        compiler_params=_cparams("parallel"),
    )(*([slots] * N_DEV))


def _adamw(w, g, m, v, name):
    r, cols = w.shape
    tr = 256 if r % 256 == 0 else r

    def body(w_ref, g_ref, m_ref, v_ref, d_ref, nm_ref, nv_ref):
        g_v = g_ref[...]
        m_n = ADAM_B1 * m_ref[...] + (1.0 - ADAM_B1) * g_v
        v_n = ADAM_B2 * v_ref[...] + (1.0 - ADAM_B2) * jnp.square(g_v)
        m_hat = m_n / (1.0 - ADAM_B1 ** ADAM_STEP)
        v_hat = v_n / (1.0 - ADAM_B2 ** ADAM_STEP)
        d_ref[...] = -ADAM_LR * (m_hat / (jnp.sqrt(v_hat) + ADAM_EPS) + ADAM_WD * w_ref[...])
        nm_ref[...] = m_n
        nv_ref[...] = v_n

    spec = pl.BlockSpec((tr, cols), lambda i: (i, 0))
    return pl.pallas_call(
        body, name=name, grid=(r // tr,),
        in_specs=[spec] * 4, out_specs=[spec] * 3,
        out_shape=[jax.ShapeDtypeStruct((r, cols), F32)] * 3,
        compiler_params=_cparams("parallel"),
    )(w, g, m, v)


def _rope_tables(p):
    half = ROT_DIM // 2
    inv_freq = ROPE_THETA ** (-jnp.arange(half, dtype=F32) * 2.0 / ROT_DIM)
    pos = (jnp.arange(p) - PAD).astype(F32)
    ang = pos[:, None] * inv_freq[None, :]
    lane = jnp.arange(BLK)
    seg = (lane % HEAD_DIM) // half
    cos = jnp.cos(ang)[:, lane % half]
    sin = jnp.sin(ang)[:, lane % half]
    c = jnp.where(seg[None, :] < 2, cos, 1.0)
    s1 = jnp.where(seg[None, :] == 0, -sin, 0.0)
    s2 = jnp.where(seg[None, :] == 1, sin, 0.0)
    return jnp.concatenate([c, s1, s2], axis=1).astype(F32)


def _flat_pack(parts, rows):
    flat = jnp.concatenate([a.reshape(-1).astype(F32) for a in parts])
    return jnp.pad(flat, (0, rows * D - flat.shape[0])).reshape(rows, D)


def _flat_unpack(pack, shapes):
    flat = pack.reshape(-1)
    out, off = [], 0
    for s in shapes:
        size = 1
        for e in s:
            size *= e
        out.append(flat[off:off + size].reshape(s))
        off += size
    return out


def kernel(x, meta_tokens, norm_pre_mix, norm_post_mix, w_in, b_in, attn_sinks, w_attn_proj, conv_dw_w, conv_dw_b, conv_ln_g, conv_ln_b, w_conv_proj, b_conv_proj, w_out, norm_pre_ffn, norm_post_ffn, w_up, ffn_dw_w, ffn_dw_b, w_down, loss_target, m_meta_tokens, m_norm_pre_mix, m_norm_post_mix, m_w_in, m_b_in, m_attn_sinks, m_w_attn_proj, m_conv_dw_w, m_conv_dw_b, m_conv_ln_g, m_conv_ln_b, m_w_conv_proj, m_b_conv_proj, m_w_out, m_norm_pre_ffn, m_norm_post_ffn, m_w_up, m_ffn_dw_w, m_ffn_dw_b, m_w_down, v_meta_tokens, v_norm_pre_mix, v_norm_post_mix, v_w_in, v_b_in, v_attn_sinks, v_w_attn_proj, v_conv_dw_w, v_conv_dw_b, v_conv_ln_g, v_conv_ln_b, v_w_conv_proj, v_b_conv_proj, v_w_out, v_norm_pre_ffn, v_norm_post_ffn, v_w_up, v_ffn_dw_w, v_ffn_dw_b, v_w_down):
    seq = x.shape[1]
    p = seq + BLK
    me = 4 * lax.axis_index("x") + 2 * lax.axis_index("y") + lax.axis_index("c")
    in_cols = w_in.shape[2]
    up_cols = w_up.shape[2]

    small = jnp.zeros((56, up_cols), F32)
    small = small.at[0:N_META, 0:BLK].set(meta_tokens)
    small = small.at[16:16 + CONV_K, 0:BLK].set(conv_dw_w[0])
    small = small.at[48:48 + FFN_K, :].set(ffn_dw_w[0])
    w_int, w_upt, wa, wc, wo, wd, small_all = _all_gather(
        [w_in[0].T.astype(BF16), w_up[0].T.astype(BF16), w_attn_proj[0].astype(BF16), w_conv_proj[0].astype(BF16),
         w_out[0].astype(BF16), w_down[0].astype(BF16), small], "gather_weights")
    small_all = small_all.reshape(N_DEV, 56, up_cols)
    meta_full = small_all[:, 0:N_META, 0:BLK].transpose(1, 0, 2).reshape(N_META, D)
    cdw = small_all[:, 16:16 + CONV_K, 0:BLK].transpose(1, 0, 2).reshape(CONV_K, D)
    cdw32 = jnp.pad(cdw, ((0, 32 - CONV_K), (0, 0)))
    fdw = small_all[:, 48:48 + FFN_K, :].transpose(1, 0, 2).reshape(FFN_K, 2 * FFN)

    h0p = jnp.concatenate([jnp.zeros((PAD, D), F32), meta_full, x[0]], axis=0)
    tgt = jnp.concatenate([jnp.zeros((BLK, D), F32), loss_target[0]], axis=0)
    tabs = _rope_tables(p)
    vecs = jnp.concatenate([conv_ln_g, conv_ln_b, b_conv_proj, norm_post_mix, norm_pre_ffn, jnp.zeros((3, D), F32)], axis=0)

    n1, q, kv, ag, gates = _in_proj(h0p, norm_pre_mix, w_int, b_in, tabs)
    ao = _attn_fwd(q, kv, attn_sinks)
    c0 = _conv31_fwd(ag, cdw32, conv_dw_b)
    c1, attn, conv, merged, mix, h1, n2 = _mixer_fwd(ao, c0, gates, h0p, wa, wc, wo, vecs)
    u0 = _mm_nt(n2, w_upt, "ffn_up")
    act = _ffn_act(u0, fdw, ffn_dw_b)
    dffn, dact, dy, acc_f = _ffn_down_loss(act, wd, h1, tgt, norm_post_ffn)

    g_wd = _mm_tn([act], dffn, "grad_w_down")
    dug, duv, gfw_g, gfw_v, gfb_g, gfb_v = _ffn_act_bwd(u0, dact, fdw, ffn_dw_b)
    g_wupt = _mm_tn([dug, duv], n2, "grad_w_up")
    dh1, acc_u = _ffn_in_bwd(dug, duv, w_upt, h1, dy, norm_pre_ffn)
    dmix, dat, dcv, dgt, dao, dc0, acc_m = _mixer_bwd(dh1, mix, attn, conv, gates, c0, wa, wc, wo, vecs)
    g_wo = _mm_tn([merged], dmix, "grad_w_out")
    g_wa = _mm_tn([ao], dat, "grad_w_attn_proj")
    g_wc = _mm_tn([c1], dcv, "grad_w_conv_proj")
    da, dg, g_cdw, g_cdb = _conv31_bwd(ag, dc0, cdw32)
    dqkv, dsink = _attn_bwd(q, kv, dao, attn_sinks, tabs)
    g_wint, g_bin = _mm_tn([dqkv, da, dg, dgt], n1, "grad_w_in", col_sums=True)
    dh0, acc_i = _in_bwd(dqkv, da, dg, dgt, w_int, h0p, dh1, norm_pre_mix)

    slots = _reduce_scatter([g_wint, g_wupt, g_wa, g_wc, g_wo, g_wd], "scatter_grads")
    g_big = [_sum_slots(s, "sum_" + nm) for s, nm in zip(slots, ("w_in", "w_up", "w_attn_proj", "w_conv_proj", "w_out", "w_down"))]
    g_big[0] = g_big[0].T
    g_big[1] = g_big[1].T
    big = []
    for nm, w, g, m, v in (("w_in", w_in, g_big[0], m_w_in, v_w_in), ("w_up", w_up, g_big[1], m_w_up, v_w_up),
                           ("w_attn_proj", w_attn_proj, g_big[2], m_w_attn_proj, v_w_attn_proj),
                           ("w_conv_proj", w_conv_proj, g_big[3], m_w_conv_proj, v_w_conv_proj),
                           ("w_out", w_out, g_big[4], m_w_out, v_w_out), ("w_down", w_down, g_big[5], m_w_down, v_w_down)):
        big.append((g[None],) + tuple(o[None] for o in _adamw(w[0], g, m[0], v[0], "adamw_" + nm)))

    loss_row = jnp.sum(acc_f[1:2, :], axis=1, keepdims=True)
    parts = [loss_row, dh0[PAD:BLK], acc_i[0:1], acc_m[0:1], g_bin, dsink[0:1, 0:16], g_cdw[0:CONV_K], g_cdb,
             acc_m[2:3], acc_m[3:4], acc_m[1:2], acc_u[0:1], acc_f[0:1],
             jnp.concatenate([gfw_g, gfw_v], axis=1), jnp.concatenate([gfb_g, gfb_v], axis=1)]
    shapes = [a.shape for a in parts]
    pack_rows = 88
    (gathered,) = _all_gather([_flat_pack(parts, pack_rows)], "gather_small_grads")
    tot = _flat_unpack(_sum_slots(gathered, "sum_small_grads"), shapes)
    (loss, g_meta, g_npm, g_nqm, g_bi, g_sk, g_cw, g_cb, g_lg, g_lb, g_bc, g_npf, g_nqf, g_fw, g_fb) = tot
    loss = loss.reshape(())
    g_meta = lax.dynamic_slice_in_dim(g_meta, me * BLK, BLK, axis=1)
    g_cw = lax.dynamic_slice_in_dim(g_cw, me * BLK, BLK, axis=1)[None]
    g_fw = lax.dynamic_slice_in_dim(g_fw, me * up_cols, up_cols, axis=1)[None]

    sm_w = [meta_tokens, norm_pre_mix, norm_post_mix, b_in, attn_sinks, conv_dw_w, conv_dw_b, conv_ln_g, conv_ln_b,
            b_conv_proj, norm_pre_ffn, norm_post_ffn, ffn_dw_w, ffn_dw_b]
    sm_g = [g_meta, g_npm, g_nqm, g_bi, g_sk, g_cw, g_cb, g_lg, g_lb, g_bc, g_npf, g_nqf, g_fw, g_fb]
    sm_m = [m_meta_tokens, m_norm_pre_mix, m_norm_post_mix, m_b_in, m_attn_sinks, m_conv_dw_w, m_conv_dw_b, m_conv_ln_g,
            m_conv_ln_b, m_b_conv_proj, m_norm_pre_ffn, m_norm_post_ffn, m_ffn_dw_w, m_ffn_dw_b]
    sm_v = [v_meta_tokens, v_norm_pre_mix, v_norm_post_mix, v_b_in, v_attn_sinks, v_conv_dw_w, v_conv_dw_b, v_conv_ln_g,
            v_conv_ln_b, v_b_conv_proj, v_norm_pre_ffn, v_norm_post_ffn, v_ffn_dw_w, v_ffn_dw_b]
    sm_shapes = [a.shape for a in sm_w]
    upd_rows = 32
    v_pack = _flat_pack(sm_v, upd_rows)
    sm_out = _adamw(_flat_pack(sm_w, upd_rows), _flat_pack(sm_g, upd_rows), _flat_pack(sm_m, upd_rows), v_pack, "adamw_small")
    sm_d, sm_nm, sm_nv = (_flat_unpack(o, sm_shapes) for o in sm_out)

    order = ["meta_tokens", "norm_pre_mix", "norm_post_mix", "w_in", "b_in", "attn_sinks", "w_attn_proj", "conv_dw_w",
             "conv_dw_b", "conv_ln_g", "conv_ln_b", "w_conv_proj", "b_conv_proj", "w_out", "norm_pre_ffn", "norm_post_ffn",
             "w_up", "ffn_dw_w", "ffn_dw_b", "w_down"]
    small_names = ["meta_tokens", "norm_pre_mix", "norm_post_mix", "b_in", "attn_sinks", "conv_dw_w", "conv_dw_b", "conv_ln_g",
                   "conv_ln_b", "b_conv_proj", "norm_pre_ffn", "norm_post_ffn", "ffn_dw_w", "ffn_dw_b"]
    big_names = ["w_in", "w_up", "w_attn_proj", "w_conv_proj", "w_out", "w_down"]
    table = {}
    for k, nm in enumerate(small_names):
        table[nm] = (sm_g[k], sm_d[k], sm_nm[k], sm_nv[k])
    for k, nm in enumerate(big_names):
        table[nm] = big[k]
    grad_x = dh0[BLK:][None]
    outs = [loss, grad_x]
    for field in range(4):
        outs += [table[nm][field] for nm in order]
    return tuple(outs)
```

```python
import functools

import jax
import jax.numpy as jnp
from jax import lax
from jax.experimental import pallas as pl
from jax.experimental.pallas import tpu as pltpu

F32 = jnp.float32
BF16 = jnp.bfloat16
MESH = pl.DeviceIdType.MESH

D = 1024
HEAD_DIM = 64
N_META = 16
BLK = 128
PAD = BLK - N_META
CONV_K = 31
FFN = 2816
FFN_K = 3
QKV_W = 1280
IN_W = 5376
ROT_DIM = 16
ROPE_THETA = 500000.0
RMS_EPS = 1e-6
LN_EPS = 1e-5
NEG_INF = -1e30
SCALE = HEAD_DIM ** -0.5
N_DEV = 8

ADAM_LR = 0.001
ADAM_B1 = 0.9
ADAM_B2 = 0.999
ADAM_EPS = 1e-08
ADAM_WD = 0.01
ADAM_STEP = 10

VMEM_BYTES_V7X = 64 * 1024 * 1024
VMEM_LIMIT = VMEM_BYTES_V7X - 8 * 1024 * 1024

NT = (((1,), (1,)), ((), ()))
TN = (((0,), (0,)), ((), ()))
VM = pl.BlockSpec(memory_space=pltpu.VMEM)
ANY = pl.BlockSpec(memory_space=pl.ANY)


def _cparams(*sem):
    return pltpu.CompilerParams(dimension_semantics=sem or None, vmem_limit_bytes=VMEM_LIMIT)


def _row_tile(p):
    return 384 if p % 384 == 0 else 128


def _dot(a, b):
    return jnp.dot(a, b, preferred_element_type=F32)


def _dot_nt(a, b):
    return lax.dot_general(a, b, NT, preferred_element_type=F32)


def _dot_tn(a, b):
    return lax.dot_general(a, b, TN, preferred_element_type=F32)


def _rms(x, g):
    return x * lax.rsqrt(jnp.mean(x * x, axis=-1, keepdims=True) + RMS_EPS) * g


def _lnsilu(x, g, b):
    mu = jnp.mean(x, axis=-1, keepdims=True)
    var = jnp.mean(jnp.square(x - mu), axis=-1, keepdims=True)
    z = (x - mu) * lax.rsqrt(var + LN_EPS) * g + b
    return z * jax.nn.sigmoid(z)


def _rope(v, c, s1, s2):
    return v * c + pltpu.roll(v, BLK - 8, 1) * s1 + pltpu.roll(v, 8, 1) * s2


def _rows(i, tm):
    return i * tm + lax.broadcasted_iota(jnp.int32, (tm, 1), 0)


def _place():
    return lax.axis_index("x"), lax.axis_index("y"), lax.axis_index("c")


def _blk(ref, idx, r, dtype):
    return ref.at[pl.ds(pl.multiple_of(idx * r, 16 if dtype == BF16 else 8), r), :]


class _Gather:
    def __init__(self, arrs):
        self.ins = list(arrs)
        n = len(arrs)
        self.out_shape = [jax.ShapeDtypeStruct((N_DEV * a.shape[0], a.shape[1]), a.dtype) for a in arrs]
        self.scratch = [pltpu.SemaphoreType.DMA((n, 7)), pltpu.SemaphoreType.DMA((n, 7)), pltpu.SemaphoreType.DMA((n,))]

    def _parts(self, ins, outs, sems):
        send_sems, recv_sems, local_sems = sems
        n = len(ins)
        x, y, c = _place()
        me, sibling = (x, y, c), (x, y, 1 - c)
        chips = [(1 - x, y), (x, 1 - y), (1 - x, 1 - y)]

        def rows(a, p):
            return _blk(outs[a], 4 * p[0] + 2 * p[1] + p[2], self.ins[a].shape[0], self.ins[a].dtype)

        def copy(a, k, block, to, src=None):
            return pltpu.make_async_remote_copy(
                src_ref=rows(a, block) if src is None else src, dst_ref=rows(a, block),
                send_sem=send_sems.at[a, k], recv_sem=recv_sems.at[a, k], device_id=to, device_id_type=MESH)

        mine = [pltpu.make_async_copy(ins[a], rows(a, me), local_sems.at[a]) for a in range(n)]
        first = []
        for a in range(n):
            first.append(copy(a, 0, me, sibling, src=ins[a]))
            first += [copy(a, 1 + j, me, (*chip, c), src=ins[a]) for j, chip in enumerate(chips)]
        return n, c, me, sibling, chips, copy, mine, first

    def start(self, ins, outs, sems):
        *_, mine, first = self._parts(ins, outs, sems)
        for cp in mine + first:
            cp.start()

    def finish(self, ins, outs, sems):
        n, c, me, sibling, chips, copy, mine, first = self._parts(ins, outs, sems)
        passed = []
        for j, chip in enumerate(chips):
            for a in range(n):
                copy(a, 1 + j, (*chip, c), me).wait_recv()
                fwd = copy(a, 4 + j, (*chip, c), sibling)
                fwd.start()
                passed.append(fwd)
        for a in range(n):
            copy(a, 0, sibling, me).wait_recv()
            for j, chip in enumerate(chips):
                copy(a, 4 + j, (*chip, 1 - c), me).wait_recv()
        for cp in first + passed:
            cp.wait_send()
        for cp in mine:
            cp.wait()


FLIPS = [(0, 0, 1), (1, 0, 0), (0, 1, 0), (1, 1, 0), (1, 0, 1), (0, 1, 1), (1, 1, 1)]


class _Scatter:
    def __init__(self, arrs):
        self.ins = list(arrs)
        n = len(arrs)
        self.out_shape = [jax.ShapeDtypeStruct(a.shape, a.dtype) for a in arrs]
        self.scratch = [pltpu.SemaphoreType.DMA((n, 7)), pltpu.SemaphoreType.DMA((n, 7)), pltpu.SemaphoreType.DMA((n,))]

    def _parts(self, ins, outs, sems):
        send_sems, recv_sems, local_sems = sems
        n = len(ins)
        x, y, c = _place()
        me = 4 * x + 2 * y + c

        def flip(v, f):
            return 1 - v if f else v

        def blk(ref, a, idx):
            return _blk(ref, idx, self.ins[a].shape[0] // N_DEV, self.ins[a].dtype)

        mine = [pltpu.make_async_copy(blk(ins[a], a, me), blk(outs[a], a, me), local_sems.at[a]) for a in range(n)]
        sends, recvs = [], []
        for k, f in enumerate(FLIPS):
            peer = (flip(x, f[0]), flip(y, f[1]), flip(c, f[2]))
            pidx = 4 * peer[0] + 2 * peer[1] + peer[2]
            for a in range(n):
                sends.append(pltpu.make_async_remote_copy(
                    src_ref=blk(ins[a], a, pidx), dst_ref=blk(outs[a], a, me),
                    send_sem=send_sems.at[a, k], recv_sem=recv_sems.at[a, k], device_id=peer, device_id_type=MESH))
                recvs.append(pltpu.make_async_remote_copy(
                    src_ref=blk(ins[a], a, pidx), dst_ref=blk(outs[a], a, pidx),
                    send_sem=send_sems.at[a, k], recv_sem=recv_sems.at[a, k], device_id=peer, device_id_type=MESH))
        return mine, sends, recvs

    def start(self, ins, outs, sems):
        mine, sends, _ = self._parts(ins, outs, sems)
        for cp in mine + sends:
            cp.start()

    def finish(self, ins, outs, sems):
        mine, sends, recvs = self._parts(ins, outs, sems)
        for cp in recvs:
            cp.wait_recv()
        for cp in sends:
            cp.wait_send()
        for cp in mine:
            cp.wait()


def _exchange(comm, name):
    n, m = len(comm.ins), len(comm.out_shape)

    def body(*refs):
        ins, outs, sems = refs[:n], refs[n:n + m], refs[n + m:]
        comm.start(ins, outs, sems)
        comm.finish(ins, outs, sems)

    return pl.pallas_call(
        body, name=name, out_shape=comm.out_shape, in_specs=[ANY] * n, out_specs=[ANY] * m, scratch_shapes=comm.scratch,
    )(*comm.ins)


def _call(body, *, name, grid, in_specs, out_specs, out_shape, args, scratch=(), sem="parallel", comm=None):
    if comm is None:
        outs = pl.pallas_call(
            body, name=name, grid=grid, in_specs=list(in_specs), out_specs=list(out_specs), out_shape=list(out_shape),
            scratch_shapes=list(scratch), compiler_params=_cparams(sem))(*args)
        return outs, []
    n_in, n_out, n_sc = len(in_specs), len(out_specs), len(scratch)
    n_ci, n_co = len(comm.ins), len(comm.out_shape)
    last = grid[0] - 1

    def fused(*refs):
        ins, refs = refs[:n_in], refs[n_in:]
        c_ins, refs = refs[:n_ci], refs[n_ci:]
        outs, refs = refs[:n_out], refs[n_out:]
        c_outs, refs = refs[:n_co], refs[n_co:]
        sc, c_sems = refs[:n_sc], refs[n_sc:]
        step = pl.program_id(0)

        @pl.when(step == 0)
        def _():
            comm.start(c_ins, c_outs, c_sems)

        body(*ins, *outs, *sc)

        @pl.when(step == last)
        def _():
            comm.finish(c_ins, c_outs, c_sems)

    outs = pl.pallas_call(
        fused, name=name, grid=grid, in_specs=list(in_specs) + [ANY] * n_ci, out_specs=list(out_specs) + [ANY] * n_co,
        out_shape=list(out_shape) + comm.out_shape, scratch_shapes=list(scratch) + comm.scratch,
        compiler_params=_cparams("arbitrary"))(*args, *comm.ins)
    return outs[:n_out], outs[n_out:]


def _in_proj(h0p, gain, w_int, b_in, tabs, comm=None):
    p = h0p.shape[0]
    tm = _row_tile(p)

    def body(h_ref, g_ref, w_ref, b_ref, t_ref, n1_ref, q_ref, kv_ref, ag_ref, gt_ref):
        n = _rms(h_ref[...], g_ref[...]).astype(BF16)
        n1_ref[...] = n
        c, s1, s2 = t_ref[:, 0:128], t_ref[:, 128:256], t_ref[:, 256:384]

        def mm(c0, w):
            return _dot_nt(n, w_ref[c0:c0 + w, :]) + b_ref[:, c0:c0 + w]

        for j in range(4):
            acc = mm(256 * j, 256)
            for t in range(2):
                lo = 256 * j + 128 * t
                q_ref[:, lo:lo + 128] = (_rope(acc[:, 128 * t:128 * (t + 1)], c, s1, s2) * SCALE).astype(BF16)
        acc = mm(1024, 256)
        kv_ref[:, 0:128] = _rope(acc[:, 0:128], c, s1, s2).astype(BF16)
        kv_ref[:, 128:256] = acc[:, 128:256].astype(BF16)
        for j in range(8):
            ag_ref[:, 256 * j:256 * (j + 1)] = mm(QKV_W + 256 * j, 256).astype(BF16)
        for j in range(8):
            gt_ref[:, 256 * j:256 * (j + 1)] = mm(QKV_W + 2048 + 256 * j, 256).astype(BF16)

    def row(w):
        return pl.BlockSpec((tm, w), lambda i: (i, 0))

    return _call(
        body, name="in_proj", grid=(p // tm,),
        in_specs=[row(D), VM, VM, VM, row(384)],
        out_specs=[row(D), row(D), row(256), row(2048), row(2048)],
        out_shape=[jax.ShapeDtypeStruct((p, w), BF16) for w in (D, D, 256, 2048, 2048)],
        args=(h0p, gain, w_int, b_in, tabs), comm=comm)


def _attn_probs(n, h, q_ref, km_ref, kp_ref, kc_ref, sink_ref):
    lane = lax.broadcasted_iota(jnp.int32, (BLK, BLK), 1)
    lo = lane < HEAD_DIM
    lo3 = lax.broadcasted_iota(jnp.int32, (3 * BLK, BLK), 1) < HEAD_DIM

    def dup(lanes):
        cat = jnp.concatenate([km_ref[:, lanes], kp_ref[:, lanes], kc_ref[:, lanes]], axis=0).astype(F32)
        rolled = pltpu.roll(cat, HEAD_DIM, 1)
        return (jnp.where(lo3, cat, rolled) if h == 0 else jnp.where(lo3, rolled, cat)).astype(BF16)

    k2 = dup(slice(0, 128))
    v2 = dup(slice(128, 256))
    qs = _stack_heads(q_ref, h, lo)
    s = _dot_nt(qs, k2).reshape(8, BLK, 3 * BLK)

    r = lax.broadcasted_iota(jnp.int32, (BLK, BLK), 0)
    tq = BLK * n + r - PAD
    t_m = lane - PAD
    t_p = BLK * (n - 1) + lane - PAD
    t_c = BLK * n + lane - PAD
    ok_m = jnp.logical_and(t_m >= 0, t_m <= tq)
    ok_p = jnp.logical_and(t_p >= N_META, tq - t_p < BLK)
    ok_c = jnp.logical_and(t_c >= N_META, t_c <= tq)
    bias = jnp.concatenate([jnp.where(ok, 0.0, NEG_INF).astype(F32) for ok in (ok_m, ok_p, ok_c)], axis=1)
    s = s + bias[None]

    gidx = lax.broadcasted_iota(jnp.int32, (8, 1, 1), 0)
    sink = jnp.zeros((8, 1, 1), F32)
    for g in range(8):
        sink = jnp.where(gidx == g, sink_ref[0, 8 * h + g], sink)
    m = jnp.maximum(jnp.max(s, axis=-1, keepdims=True), sink)
    e = jnp.exp(s - m)
    es = jnp.exp(sink - m)
    inv = 1.0 / (jnp.sum(e, axis=-1, keepdims=True) + es)
    return qs, k2, v2, e * inv, es * inv


def _stack_heads(ref, h, lo):
    pieces = []
    for jp in range(4):
        v = ref[:, BLK * (4 * h + jp):BLK * (4 * h + jp + 1)]
        zero = jnp.zeros_like(v)
        pieces += [jnp.where(lo, v, zero), jnp.where(lo, zero, v)]
    return jnp.concatenate(pieces, axis=0)


def _unstack_heads(v, jp, lo):
    return jnp.where(lo, v[256 * jp:256 * jp + 128], v[256 * jp + 128:256 * jp + 256])


def _attn_fwd(q, kv, sinks, comm=None):
    p = q.shape[0]
    nb = p // BLK

    def body(q_ref, km_ref, kp_ref, kc_ref, sink_ref, o_ref):
        n = pl.program_id(0)
        lo = lax.broadcasted_iota(jnp.int32, (BLK, BLK), 1) < HEAD_DIM
        for h in range(2):
            _, _, v2, pn, _ = _attn_probs(n, h, q_ref, km_ref, kp_ref, kc_ref, sink_ref)
            o = _dot(pn.reshape(8 * BLK, 3 * BLK).astype(BF16), v2)
            for jp in range(4):
                o_ref[:, BLK * (4 * h + jp):BLK * (4 * h + jp + 1)] = _unstack_heads(o, jp, lo).astype(BF16)

    return _call(
        body, name="attn_fwd", grid=(nb,),
        in_specs=[pl.BlockSpec((BLK, D), lambda i: (i, 0)),
                  pl.BlockSpec((BLK, 256), lambda i: (0, 0)),
                  pl.BlockSpec((BLK, 256), lambda i: (jnp.maximum(i - 1, 0), 0)),
                  pl.BlockSpec((BLK, 256), lambda i: (i, 0)),
                  pl.BlockSpec(memory_space=pltpu.SMEM)],
        out_specs=[pl.BlockSpec((BLK, D), lambda i: (i, 0))],
        out_shape=[jax.ShapeDtypeStruct((p, D), BF16)],
        args=(q, kv, kv, kv, sinks), comm=comm)


def _conv31_fwd(ag, w32, b, comm=None):
    p = ag.shape[0]
    nch = p // BLK

    def body(a_ref, g_ref, w_ref, b_ref, o_ref, gp):
        gp[0:32, :] = jnp.zeros((32, BLK), F32)
        for ci in range(nch):
            r0 = BLK * ci
            glu = a_ref[r0:r0 + BLK, :].astype(F32) * jax.nn.sigmoid(g_ref[r0:r0 + BLK, :].astype(F32))
            if ci == 0:
                glu = jnp.where(_rows(0, BLK) >= PAD, glu, 0.0)
            gp[32 + r0:32 + r0 + BLK, :] = glu
        for ci in range(nch):
            r0 = BLK * ci
            acc = jnp.broadcast_to(b_ref[...], (BLK, BLK))
            for j in range(CONV_K):
                acc = acc + w_ref[j:j + 1, :] * gp[r0 + j + 2:r0 + j + 2 + BLK, :]
            o_ref[r0:r0 + BLK, :] = acc

    return _call(
        body, name="conv31_fwd", grid=(D // BLK,),
        in_specs=[pl.BlockSpec((p, BLK), lambda j: (0, j)), pl.BlockSpec((p, BLK), lambda j: (0, 8 + j)),
                  pl.BlockSpec((32, BLK), lambda j: (0, j)), pl.BlockSpec((1, BLK), lambda j: (0, j))],
        out_specs=[pl.BlockSpec((p, BLK), lambda j: (0, j))],
        out_shape=[jax.ShapeDtypeStruct((p, D), F32)],
        scratch=[pltpu.VMEM((p + 32, BLK), F32)],
        args=(ag, ag, w32, b), comm=comm)


def _mixer_fwd(ao, c0, gates, h0p, wa, wc, wo, vecs):
    p = ao.shape[0]
    tm = _row_tile(p)

    def body(ao_ref, c0_ref, gt_ref, h_ref, wa_ref, wc_ref, wo_ref, v_ref,
             c1_ref, at_ref, cv_ref, mg_ref, mix_ref, h1_ref, n2_ref):
        i = pl.program_id(0)
        c1 = _lnsilu(c0_ref[...], v_ref[0:1, :], v_ref[1:2, :]).astype(BF16)
        c1_ref[...] = c1
        attn = _dot(ao_ref[...], wa_ref[...])
        conv = _dot(c1, wc_ref[...]) + v_ref[2:3, :]
        at_ref[...] = attn.astype(BF16)
        cv_ref[...] = conv.astype(BF16)
        merged = (jax.nn.sigmoid(gt_ref[:, 0:D].astype(F32)) * attn
                  + jax.nn.sigmoid(gt_ref[:, D:2 * D].astype(F32)) * conv).astype(BF16)
        mg_ref[...] = merged
        mix = _dot(merged, wo_ref[...])
        mix_ref[...] = mix
        h1 = jnp.where(_rows(i, tm) >= PAD, h_ref[...] + _rms(mix, v_ref[3:4, :]), 0.0)
        h1_ref[...] = h1
        n2_ref[...] = _rms(h1, v_ref[4:5, :]).astype(BF16)

    def row(w):
        return pl.BlockSpec((tm, w), lambda i: (i, 0))

    return pl.pallas_call(
        body, name="mixer_fwd", grid=(p // tm,),
        in_specs=[row(D), row(D), row(2 * D), row(D), VM, VM, VM, VM],
        out_specs=[row(D)] * 7,
        out_shape=[jax.ShapeDtypeStruct((p, D), t) for t in (BF16, BF16, BF16, BF16, F32, F32, BF16)],
        compiler_params=_cparams("parallel"),
    )(ao, c0, gates, h0p, wa, wc, wo, vecs)


def _mm_nt(a, w_t, name):
    p, k = a.shape
    n = w_t.shape[0]
    tm = _row_tile(p)
    ch = 512

    def body(a_ref, w_ref, o_ref):
        a_v = a_ref[...]
        for c0 in range(0, n, ch):
            o_ref[:, c0:c0 + ch] = _dot_nt(a_v, w_ref[c0:c0 + ch, :]).astype(BF16)

    return pl.pallas_call(
        body, name=name, grid=(p // tm,),
        in_specs=[pl.BlockSpec((tm, k), lambda i: (i, 0)), VM],
        out_specs=pl.BlockSpec((tm, n), lambda i: (i, 0)),
        out_shape=jax.ShapeDtypeStruct((p, n), BF16),
        compiler_params=_cparams("parallel"),
    )(a, w_t)


def _conv3(xp_ref, w_ref, r0):
    return (w_ref[0:1, :] * xp_ref[r0 + 6:r0 + 6 + BLK, :] + w_ref[1:2, :] * xp_ref[r0 + 7:r0 + 7 + BLK, :]
            + w_ref[2:3, :] * xp_ref[r0 + 8:r0 + 8 + BLK, :])


def _ffn_slab_specs(p):
    ncol = FFN // BLK
    return [pl.BlockSpec((p, BLK), lambda j: (0, j)), pl.BlockSpec((p, BLK), lambda j: (0, ncol + j)),
            pl.BlockSpec((FFN_K, BLK), lambda j: (0, j)), pl.BlockSpec((FFN_K, BLK), lambda j: (0, ncol + j)),
            pl.BlockSpec((1, BLK), lambda j: (0, j)), pl.BlockSpec((1, BLK), lambda j: (0, ncol + j))]


def _fill_shifted(dst, src_ref, nch):
    dst[0:8, :] = jnp.zeros((8, BLK), F32)
    for ci in range(nch):
        dst[8 + BLK * ci:8 + BLK * (ci + 1), :] = src_ref[BLK * ci:BLK * (ci + 1), :].astype(F32)


def _ffn_act(u0, fw, fb):
    p = u0.shape[0]
    nch = p // BLK

    def body(g_ref, v_ref, wg_ref, wv_ref, bg_ref, bv_ref, o_ref, xg, xv):
        _fill_shifted(xg, g_ref, nch)
        _fill_shifted(xv, v_ref, nch)
        for ci in range(nch):
            r0 = BLK * ci
            ug = _conv3(xg, wg_ref, r0) + bg_ref[...]
            uv = _conv3(xv, wv_ref, r0) + bv_ref[...]
            o_ref[r0:r0 + BLK, :] = (ug * jax.nn.sigmoid(ug) * uv).astype(BF16)

    return pl.pallas_call(
        body, name="ffn_act", grid=(FFN // BLK,),
        in_specs=_ffn_slab_specs(p),
        out_specs=pl.BlockSpec((p, BLK), lambda j: (0, j)),
        out_shape=jax.ShapeDtypeStruct((p, FFN), BF16),
        scratch_shapes=[pltpu.VMEM((p + 8, BLK), F32)] * 2,
        compiler_params=_cparams("parallel"),
    )(u0, u0, fw, fw, fb, fb)


def _ffn_down_loss(act, wd, h1, tgt, gain):
    p = act.shape[0]
    tm = _row_tile(p)

    def body(a_ref, w_ref, h_ref, t_ref, g_ref, df_ref, da_ref, dy_ref, acc_ref):
        i = pl.program_id(0)

        @pl.when(i == 0)
        def _():
            acc_ref[...] = jnp.zeros_like(acc_ref)

        ffn = _dot(a_ref[...], w_ref[...])
        r, vjp = jax.vjp(_rms, ffn, g_ref[...])
        diff = jnp.where(_rows(i, tm) >= BLK, h_ref[...] + r - t_ref[...], 0.0)
        dy = diff * (1.0 / D)
        dffn, dg = vjp(dy)
        acc_ref[0:1, :] += dg
        acc_ref[1:2, :] += jnp.sum(diff * diff, axis=0, keepdims=True) * (0.5 / D)
        dy_ref[...] = dy
        dfb = dffn.astype(BF16)
        df_ref[...] = dfb
        for c0 in range(0, FFN, 256):
            da_ref[:, c0:c0 + 256] = _dot_nt(dfb, w_ref[c0:c0 + 256, :]).astype(BF16)

    def row(w):
        return pl.BlockSpec((tm, w), lambda i: (i, 0))

    return pl.pallas_call(
        body, name="ffn_down_loss", grid=(p // tm,),
        in_specs=[row(FFN), VM, row(D), row(D), VM],
        out_specs=[row(D), row(FFN), row(D), pl.BlockSpec((8, D), lambda i: (0, 0))],
        out_shape=[jax.ShapeDtypeStruct((p, D), BF16), jax.ShapeDtypeStruct((p, FFN), BF16),
                   jax.ShapeDtypeStruct((p, D), F32), jax.ShapeDtypeStruct((8, D), F32)],
        compiler_params=_cparams("arbitrary"),
    )(act, wd, h1, tgt, gain)


def _mm_tn(pieces, b, name, col_sums=False):
    p, n = b.shape
    tk = 256
    nblk = [a.shape[1] // tk for a in pieces]
    offs = [sum(nblk[:q]) for q in range(len(pieces))]
    total = sum(nblk)
    npc = len(pieces)

    def body(*refs):
        a_refs, b_ref, o_ref = refs[:npc], refs[npc], refs[npc + 1]
        i = pl.program_id(0)
        for q, a_ref in enumerate(a_refs):
            @pl.when(jnp.logical_and(i >= offs[q], i < offs[q] + nblk[q]))
            def _(a_ref=a_ref):
                a_v = a_ref[...]
                o_ref[...] = _dot_tn(a_v, b_ref[...]).astype(BF16)
                if col_sums:
                    refs[npc + 2][...] = jnp.sum(a_v.astype(F32), axis=0, keepdims=True)

    def a_spec(q):
        return pl.BlockSpec((p, tk), lambda i: (0, jnp.clip(i - offs[q], 0, nblk[q] - 1)))

    out_specs = [pl.BlockSpec((tk, n), lambda i: (i, 0))]
    out_shape = [jax.ShapeDtypeStruct((total * tk, n), BF16)]
    if col_sums:
        out_specs.append(pl.BlockSpec((1, tk), lambda i: (0, i)))
        out_shape.append(jax.ShapeDtypeStruct((1, total * tk), F32))
    res = pl.pallas_call(
        body, name=name, grid=(total,),
        in_specs=[a_spec(q) for q in range(npc)] + [VM],
        out_specs=out_specs, out_shape=out_shape,
        compiler_params=_cparams("parallel"),
    )(*pieces, b)
    return res if col_sums else res[0]


def _ffn_act_bwd(u0, dact, fw, fb, comm=None):
    p = u0.shape[0]
    nch = p // BLK
    ncol = FFN // BLK

    def body(g_ref, v_ref, wg_ref, wv_ref, bg_ref, bv_ref, da_ref,
             dg_ref, dv_ref, gwg_ref, gwv_ref, gbg_ref, gbv_ref, xg, xv, eg, ev):
        _fill_shifted(xg, g_ref, nch)
        _fill_shifted(xv, v_ref, nch)
        eg[p:p + 8, :] = jnp.zeros((8, BLK), F32)
        ev[p:p + 8, :] = jnp.zeros((8, BLK), F32)
        for ci in range(nch):
            r0 = BLK * ci
            ug = _conv3(xg, wg_ref, r0) + bg_ref[...]
            uv = _conv3(xv, wv_ref, r0) + bv_ref[...]
            sg = jax.nn.sigmoid(ug)
            d = da_ref[r0:r0 + BLK, :].astype(F32)
            eg[r0:r0 + BLK, :] = d * uv * (sg * (1.0 + ug * (1.0 - sg)))
            ev[r0:r0 + BLK, :] = d * ug * sg
        for e_s, x_s, w_ref, d_ref, gw_ref, gb_ref in ((eg, xg, wg_ref, dg_ref, gwg_ref, gbg_ref),
                                                      (ev, xv, wv_ref, dv_ref, gwv_ref, gbv_ref)):
            sums = [jnp.zeros((BLK, BLK), F32) for _ in range(FFN_K + 1)]
            for ci in range(nch):
                r0 = BLK * ci
                e0 = e_s[r0:r0 + BLK, :]
                du = (w_ref[2:3, :] * e0 + w_ref[1:2, :] * e_s[r0 + 1:r0 + 1 + BLK, :]
                      + w_ref[0:1, :] * e_s[r0 + 2:r0 + 2 + BLK, :])
                if ci == 0:
                    du = jnp.where(_rows(0, BLK) >= PAD, du, 0.0)
                d_ref[r0:r0 + BLK, :] = du.astype(BF16)
                for j in range(FFN_K):
                    sums[j] = sums[j] + e0 * x_s[r0 + 6 + j:r0 + 6 + j + BLK, :]
                sums[FFN_K] = sums[FFN_K] + e0
            for j in range(FFN_K):
                gw_ref[j:j + 1, :] = jnp.sum(sums[j], axis=0, keepdims=True)
            gb_ref[...] = jnp.sum(sums[FFN_K], axis=0, keepdims=True)

    slab = pl.BlockSpec((p, BLK), lambda j: (0, j))
    wspec = pl.BlockSpec((FFN_K, BLK), lambda j: (0, j))
    bspec = pl.BlockSpec((1, BLK), lambda j: (0, j))
    return _call(
        body, name="ffn_act_bwd", grid=(ncol,),
        in_specs=_ffn_slab_specs(p) + [slab],
        out_specs=[slab, slab, wspec, wspec, bspec, bspec],
        out_shape=[jax.ShapeDtypeStruct((p, FFN), BF16)] * 2 + [jax.ShapeDtypeStruct((FFN_K, FFN), F32)] * 2
        + [jax.ShapeDtypeStruct((1, FFN), F32)] * 2,
        scratch=[pltpu.VMEM((p + 8, BLK), F32)] * 4,
        args=(u0, u0, fw, fw, fb, fb, dact), comm=comm)


def _ffn_in_bwd(dug, duv, w_upt, h1, dy, gain):
    p = h1.shape[0]
    tm = _row_tile(p)

    def body(dg_ref, dv_ref, w_ref, h_ref, dy_ref, g_ref, o_ref, acc_ref):
        i = pl.program_id(0)

        @pl.when(i == 0)
        def _():
            acc_ref[...] = jnp.zeros_like(acc_ref)

        dn = _dot(dg_ref[...], w_ref[0:FFN, :]) + _dot(dv_ref[...], w_ref[FFN:2 * FFN, :])
        _, vjp = jax.vjp(_rms, h_ref[...], g_ref[...])
        dh, dg = vjp(dn)
        o_ref[...] = dy_ref[...] + dh
        acc_ref[0:1, :] += dg

    def row(w):
        return pl.BlockSpec((tm, w), lambda i: (i, 0))

    return pl.pallas_call(
        body, name="ffn_in_bwd", grid=(p // tm,),
        in_specs=[row(FFN), row(FFN), VM, row(D), row(D), VM],
        out_specs=[row(D), pl.BlockSpec((8, D), lambda i: (0, 0))],
        out_shape=[jax.ShapeDtypeStruct((p, D), F32), jax.ShapeDtypeStruct((8, D), F32)],
        compiler_params=_cparams("arbitrary"),
    )(dug, duv, w_upt, h1, dy, gain)


def _mixer_bwd(dh1, mix, attn, conv, gates, c0, wa, wc, wo, vecs):
    p = dh1.shape[0]
    tm = _row_tile(p)

    def body(dh_ref, mix_ref, at_ref, cv_ref, gt_ref, c0_ref, wa_ref, wc_ref, wo_ref, v_ref,
             dmix_ref, dat_ref, dcv_ref, dgt_ref, dao_ref, dc0_ref, acc_ref):
        i = pl.program_id(0)

        @pl.when(i == 0)
        def _():
            acc_ref[...] = jnp.zeros_like(acc_ref)

        _, vjp = jax.vjp(_rms, mix_ref[...], v_ref[3:4, :])
        dmix, dgp = vjp(dh_ref[...])
        dmix = dmix.astype(BF16)
        dmix_ref[...] = dmix
        dmg = _dot_nt(dmix, wo_ref[...])
        sa = jax.nn.sigmoid(gt_ref[:, 0:D].astype(F32))
        sc = jax.nn.sigmoid(gt_ref[:, D:2 * D].astype(F32))
        dat = dmg * sa
        dcv = dmg * sc
        dgt_ref[:, 0:D] = (dmg * at_ref[...].astype(F32) * sa * (1.0 - sa)).astype(BF16)
        dgt_ref[:, D:2 * D] = (dmg * cv_ref[...].astype(F32) * sc * (1.0 - sc)).astype(BF16)
        datb = dat.astype(BF16)
        dcvb = dcv.astype(BF16)
        dat_ref[...] = datb
        dcv_ref[...] = dcvb
        dao_ref[...] = _dot_nt(datb, wa_ref[...]).astype(BF16)
        dc1 = _dot_nt(dcvb, wc_ref[...])
        _, vjp2 = jax.vjp(_lnsilu, c0_ref[...], v_ref[0:1, :], v_ref[1:2, :])
        dc0, dlg, dlb = vjp2(dc1)
        dc0_ref[...] = dc0
        acc_ref[0:1, :] += dgp
        acc_ref[1:2, :] += jnp.sum(dcv, axis=0, keepdims=True)
        acc_ref[2:3, :] += dlg
        acc_ref[3:4, :] += dlb

    def row(w):
        return pl.BlockSpec((tm, w), lambda i: (i, 0))

    return pl.pallas_call(
        body, name="mixer_bwd", grid=(p // tm,),
        in_specs=[row(D), row(D), row(D), row(D), row(2 * D), row(D), VM, VM, VM, VM],
        out_specs=[row(D), row(D), row(D), row(2 * D), row(D), row(D), pl.BlockSpec((8, D), lambda i: (0, 0))],
        out_shape=[jax.ShapeDtypeStruct((p, D), BF16)] * 3 + [jax.ShapeDtypeStruct((p, 2 * D), BF16),
                                                             jax.ShapeDtypeStruct((p, D), BF16),
                                                             jax.ShapeDtypeStruct((p, D), F32),
                                                             jax.ShapeDtypeStruct((8, D), F32)],
        compiler_params=_cparams("arbitrary"),
    )(dh1, mix, attn, conv, gates, c0, wa, wc, wo, vecs)


def _conv31_bwd(ag, dc0, w32, comm=None):
    p = ag.shape[0]
    nch = p // BLK

    def body(a_ref, g_ref, dc_ref, w_ref, da_ref, dg_ref, gw_ref, gb_ref, gp, dp):
        gp[0:32, :] = jnp.zeros((32, BLK), F32)
        dp[p:p + 32, :] = jnp.zeros((32, BLK), F32)
        bsum = jnp.zeros((BLK, BLK), F32)
        for ci in range(nch):
            r0 = BLK * ci
            glu = a_ref[r0:r0 + BLK, :].astype(F32) * jax.nn.sigmoid(g_ref[r0:r0 + BLK, :].astype(F32))
            if ci == 0:
                glu = jnp.where(_rows(0, BLK) >= PAD, glu, 0.0)
            gp[32 + r0:32 + r0 + BLK, :] = glu
            d = dc_ref[r0:r0 + BLK, :]
            dp[r0:r0 + BLK, :] = d
            bsum = bsum + d
        gb_ref[...] = jnp.sum(bsum, axis=0, keepdims=True)
        for ci in range(nch):
            r0 = BLK * ci
            acc = jnp.zeros((BLK, BLK), F32)
            for j in range(CONV_K):
                acc = acc + w_ref[j:j + 1, :] * dp[r0 + 30 - j:r0 + 30 - j + BLK, :]
            if ci == 0:
                acc = jnp.where(_rows(0, BLK) >= PAD, acc, 0.0)
            a = a_ref[r0:r0 + BLK, :].astype(F32)
            sg = jax.nn.sigmoid(g_ref[r0:r0 + BLK, :].astype(F32))
            da_ref[r0:r0 + BLK, :] = (acc * sg).astype(BF16)
            dg_ref[r0:r0 + BLK, :] = (acc * a * sg * (1.0 - sg)).astype(BF16)
        for j in range(CONV_K):
            acc = jnp.zeros((BLK, BLK), F32)
            for ci in range(nch):
                r0 = BLK * ci
                acc = acc + dp[r0:r0 + BLK, :] * gp[r0 + j + 2:r0 + j + 2 + BLK, :]
            gw_ref[j:j + 1, :] = jnp.sum(acc, axis=0, keepdims=True)
        gw_ref[CONV_K:32, :] = jnp.zeros((32 - CONV_K, BLK), F32)

    slab = pl.BlockSpec((p, BLK), lambda j: (0, j))
    return _call(
        body, name="conv31_bwd", grid=(D // BLK,),
        in_specs=[slab, pl.BlockSpec((p, BLK), lambda j: (0, 8 + j)), slab, pl.BlockSpec((32, BLK), lambda j: (0, j))],
        out_specs=[slab, slab, pl.BlockSpec((32, BLK), lambda j: (0, j)), pl.BlockSpec((1, BLK), lambda j: (0, j))],
        out_shape=[jax.ShapeDtypeStruct((p, D), BF16)] * 2 + [jax.ShapeDtypeStruct((32, D), F32),
                                                             jax.ShapeDtypeStruct((1, D), F32)],
        scratch=[pltpu.VMEM((p + 32, BLK), F32)] * 2,
        args=(ag, ag, dc0, w32), comm=comm)


def _attn_bwd(q, kv, dao, sinks, tabs, comm=None):
    p = q.shape[0]
    nb = p // BLK

    def body(q_ref, km_ref, kp_ref, kc_ref, do_ref, sink_ref, t_ref, dqkv_ref, dsink_ref, carry, macc):
        i = pl.program_id(0)
        n = nb - 1 - i

        @pl.when(i == 0)
        def _():
            carry[...] = jnp.zeros_like(carry)
            macc[...] = jnp.zeros_like(macc)
            dsink_ref[...] = jnp.zeros_like(dsink_ref)

        lane = lax.broadcasted_iota(jnp.int32, (BLK, BLK), 1)
        lo = lane < HEAD_DIM
        lo3 = lax.broadcasted_iota(jnp.int32, (3 * BLK, BLK), 1) < HEAD_DIM
        lane8 = lax.broadcasted_iota(jnp.int32, (8, BLK), 1)
        c, s1, s2 = t_ref[:, 0:128], -t_ref[:, 128:256], -t_ref[:, 256:384]
        dk = jnp.zeros((3 * BLK, BLK), F32)
        dv = jnp.zeros((3 * BLK, BLK), F32)
        for h in range(2):
            qs, k2, v2, pn, ps = _attn_probs(n, h, q_ref, km_ref, kp_ref, kc_ref, sink_ref)
            dos = _stack_heads(do_ref, h, lo)
            dp = _dot_nt(dos, v2).reshape(8, BLK, 3 * BLK)
            delta = jnp.sum(pn * dp, axis=-1, keepdims=True)
            ds = (pn * (dp - delta)).reshape(8 * BLK, 3 * BLK).astype(BF16)
            dsk = -jnp.sum(ps * delta, axis=1, keepdims=True)
            for g in range(8):
                dsink_ref[...] += jnp.where(lane8 == 8 * h + g, dsk[g], 0.0)
            dq = _dot(ds, k2)
            for jp in range(4):
                lo_c = BLK * (4 * h + jp)
                dqkv_ref[:, lo_c:lo_c + BLK] = (_rope(_unstack_heads(dq, jp, lo), c, s1, s2) * SCALE).astype(BF16)
            dk2 = _dot_tn(ds, qs)
            dv2 = _dot_tn(pn.reshape(8 * BLK, 3 * BLK).astype(BF16), dos)
            dk2 = dk2 + pltpu.roll(dk2, HEAD_DIM, 1)
            dv2 = dv2 + pltpu.roll(dv2, HEAD_DIM, 1)
            own = lo3 if h == 0 else jnp.logical_not(lo3)
            dk = jnp.where(own, dk2, dk)
            dv = jnp.where(own, dv2, dv)
        macc[:, 0:BLK] += dk[0:BLK]
        macc[:, BLK:2 * BLK] += dv[0:BLK]
        last = (n == 0).astype(F32)
        dk_c = dk[2 * BLK:3 * BLK] + carry[:, 0:BLK] + last * macc[:, 0:BLK]
        dv_c = dv[2 * BLK:3 * BLK] + carry[:, BLK:2 * BLK] + last * macc[:, BLK:2 * BLK]
        carry[:, 0:BLK] = dk[BLK:2 * BLK]
        carry[:, BLK:2 * BLK] = dv[BLK:2 * BLK]
        dqkv_ref[:, D:D + BLK] = _rope(dk_c, c, s1, s2).astype(BF16)
        dqkv_ref[:, D + BLK:D + 2 * BLK] = dv_c.astype(BF16)

    def rev(w):
        return pl.BlockSpec((BLK, w), lambda i: (nb - 1 - i, 0))

    return _call(
        body, name="attn_bwd", grid=(nb,),
        in_specs=[rev(D),
                  pl.BlockSpec((BLK, 256), lambda i: (0, 0)),
                  pl.BlockSpec((BLK, 256), lambda i: (jnp.maximum(nb - 2 - i, 0), 0)),
                  rev(256), rev(D),
                  pl.BlockSpec(memory_space=pltpu.SMEM), rev(384)],
        out_specs=[rev(QKV_W), pl.BlockSpec((8, BLK), lambda i: (0, 0))],
        out_shape=[jax.ShapeDtypeStruct((p, QKV_W), BF16), jax.ShapeDtypeStruct((8, BLK), F32)],
        scratch=[pltpu.VMEM((BLK, 256), F32)] * 2, sem="arbitrary",
        args=(q, kv, kv, kv, dao, sinks, tabs), comm=comm)


def _in_bwd(dqkv, da, dg, dgt, w_int, h0p, dh1, gain, comm=None):
    p = h0p.shape[0]
    tm = _row_tile(p)

    def body(dq_ref, da_ref, dg_ref, dt_ref, w_ref, h_ref, dh_ref, g_ref, o_ref, acc_ref):
        i = pl.program_id(0)

        @pl.when(i == 0)
        def _():
            acc_ref[...] = jnp.zeros_like(acc_ref)

        dn = (_dot(dq_ref[...], w_ref[0:QKV_W, :]) + _dot(da_ref[...], w_ref[QKV_W:QKV_W + D, :])
              + _dot(dg_ref[...], w_ref[QKV_W + D:QKV_W + 2 * D, :]) + _dot(dt_ref[...], w_ref[QKV_W + 2 * D:IN_W, :]))
        _, vjp = jax.vjp(_rms, h_ref[...], g_ref[...])
        dh, dgain = vjp(dn)
        o_ref[...] = dh_ref[...] + dh
        acc_ref[0:1, :] += dgain

    def row(w):
        return pl.BlockSpec((tm, w), lambda i: (i, 0))

    return _call(
        body, name="in_bwd", grid=(p // tm,),
        in_specs=[row(QKV_W), row(D), row(D), row(2 * D), VM, row(D), row(D), VM],
        out_specs=[row(D), pl.BlockSpec((8, D), lambda i: (0, 0))],
        out_shape=[jax.ShapeDtypeStruct((p, D), F32), jax.ShapeDtypeStruct((8, D), F32)],
        sem="arbitrary", args=(dqkv, da, dg, dgt, w_int, h0p, dh1, gain), comm=comm)


def _sum_slots(slots, name):
    r = slots.shape[0] // N_DEV
    cols = slots.shape[1]
    tr = r if r <= 352 else (r // 2 if (r // 2) % 16 == 0 else r // 3)
    steps = r // tr

    def body(*refs):
        acc = refs[0][...].astype(F32)
        for s in range(1, N_DEV):
            acc = acc + refs[s][...].astype(F32)
        refs[N_DEV][...] = acc

    return pl.pallas_call(
        body, name=name, grid=(steps,),
        in_specs=[pl.BlockSpec((tr, cols), functools.partial(lambda i, s: (s * steps + i, 0), s=s)) for s in range(N_DEV)],
        out_specs=pl.BlockSpec((tr, cols), lambda i: (i, 0)),
        out_shape=jax.ShapeDtypeStruct((r, cols), F32),
        compiler_params=_cparams("parallel"),
    )(*([slots] * N_DEV))


def _adamw(w, g, m, v, name):
    r, cols = w.shape
    tr = 256 if r % 256 == 0 else r

    def body(w_ref, g_ref, m_ref, v_ref, d_ref, nm_ref, nv_ref):
        g_v = g_ref[...]
        m_n = ADAM_B1 * m_ref[...] + (1.0 - ADAM_B1) * g_v
        v_n = ADAM_B2 * v_ref[...] + (1.0 - ADAM_B2) * jnp.square(g_v)
        m_hat = m_n / (1.0 - ADAM_B1 ** ADAM_STEP)
        v_hat = v_n / (1.0 - ADAM_B2 ** ADAM_STEP)
        d_ref[...] = -ADAM_LR * (m_hat / (jnp.sqrt(v_hat) + ADAM_EPS) + ADAM_WD * w_ref[...])
        nm_ref[...] = m_n
        nv_ref[...] = v_n

    spec = pl.BlockSpec((tr, cols), lambda i: (i, 0))
    return pl.pallas_call(
        body, name=name, grid=(r // tr,),
        in_specs=[spec] * 4, out_specs=[spec] * 3,
        out_shape=[jax.ShapeDtypeStruct((r, cols), F32)] * 3,
        compiler_params=_cparams("parallel"),
    )(w, g, m, v)


def _rope_tables(p):
    half = ROT_DIM // 2
    inv_freq = ROPE_THETA ** (-jnp.arange(half, dtype=F32) * 2.0 / ROT_DIM)
    pos = (jnp.arange(p) - PAD).astype(F32)
    ang = pos[:, None] * inv_freq[None, :]
    lane = jnp.arange(BLK)
    seg = (lane % HEAD_DIM) // half
    cos = jnp.cos(ang)[:, lane % half]
    sin = jnp.sin(ang)[:, lane % half]
    c = jnp.where(seg[None, :] < 2, cos, 1.0)
    s1 = jnp.where(seg[None, :] == 0, -sin, 0.0)
    s2 = jnp.where(seg[None, :] == 1, sin, 0.0)
    return jnp.concatenate([c, s1, s2], axis=1).astype(F32)


def _flat_pack(parts, rows):
    flat = jnp.concatenate([a.reshape(-1).astype(F32) for a in parts])
    return jnp.pad(flat, (0, rows * D - flat.shape[0])).reshape(rows, D)


def _flat_unpack(pack, shapes):
    flat = pack.reshape(-1)
    out, off = [], 0
    for s in shapes:
        size = 1
        for e in s:
            size *= e
        out.append(flat[off:off + size].reshape(s))
        off += size
    return out


def kernel(x, meta_tokens, norm_pre_mix, norm_post_mix, w_in, b_in, attn_sinks, w_attn_proj, conv_dw_w, conv_dw_b, conv_ln_g, conv_ln_b, w_conv_proj, b_conv_proj, w_out, norm_pre_ffn, norm_post_ffn, w_up, ffn_dw_w, ffn_dw_b, w_down, loss_target, m_meta_tokens, m_norm_pre_mix, m_norm_post_mix, m_w_in, m_b_in, m_attn_sinks, m_w_attn_proj, m_conv_dw_w, m_conv_dw_b, m_conv_ln_g, m_conv_ln_b, m_w_conv_proj, m_b_conv_proj, m_w_out, m_norm_pre_ffn, m_norm_post_ffn, m_w_up, m_ffn_dw_w, m_ffn_dw_b, m_w_down, v_meta_tokens, v_norm_pre_mix, v_norm_post_mix, v_w_in, v_b_in, v_attn_sinks, v_w_attn_proj, v_conv_dw_w, v_conv_dw_b, v_conv_ln_g, v_conv_ln_b, v_w_conv_proj, v_b_conv_proj, v_w_out, v_norm_pre_ffn, v_norm_post_ffn, v_w_up, v_ffn_dw_w, v_ffn_dw_b, v_w_down):
    seq = x.shape[1]
    p = seq + BLK
    me = 4 * lax.axis_index("x") + 2 * lax.axis_index("y") + lax.axis_index("c")
    in_cols = w_in.shape[2]
    up_cols = w_up.shape[2]

    small = jnp.zeros((56, up_cols), F32)
    small = small.at[0:N_META, 0:BLK].set(meta_tokens)
    small = small.at[16:16 + CONV_K, 0:BLK].set(conv_dw_w[0])
    small = small.at[48:48 + FFN_K, :].set(ffn_dw_w[0])
    w_int, small_all = _exchange(_Gather([w_in[0].T.astype(BF16), small]), "gather_w_in")
    small_all = small_all.reshape(N_DEV, 56, up_cols)
    meta_full = small_all[:, 0:N_META, 0:BLK].transpose(1, 0, 2).reshape(N_META, D)
    cdw = small_all[:, 16:16 + CONV_K, 0:BLK].transpose(1, 0, 2).reshape(CONV_K, D)
    cdw32 = jnp.pad(cdw, ((0, 32 - CONV_K), (0, 0)))
    fdw = small_all[:, 48:48 + FFN_K, :].transpose(1, 0, 2).reshape(FFN_K, 2 * FFN)

    h0p = jnp.concatenate([jnp.zeros((PAD, D), F32), meta_full, x[0]], axis=0)
    tgt = jnp.concatenate([jnp.zeros((BLK, D), F32), loss_target[0]], axis=0)
    tabs = _rope_tables(p)
    vecs = jnp.concatenate([conv_ln_g, conv_ln_b, b_conv_proj, norm_post_mix, norm_pre_ffn, jnp.zeros((3, D), F32)], axis=0)

    (n1, q, kv, ag, gates), (wa, wc, wo) = _in_proj(
        h0p, norm_pre_mix, w_int, b_in, tabs,
        comm=_Gather([w_attn_proj[0].astype(BF16), w_conv_proj[0].astype(BF16), w_out[0].astype(BF16)]))
    (ao,), (w_upt,) = _attn_fwd(q, kv, attn_sinks, comm=_Gather([w_up[0].T.astype(BF16)]))
    (c0,), (wd,) = _conv31_fwd(ag, cdw32, conv_dw_b, comm=_Gather([w_down[0].astype(BF16)]))
    c1, attn, conv, merged, mix, h1, n2 = _mixer_fwd(ao, c0, gates, h0p, wa, wc, wo, vecs)
    u0 = _mm_nt(n2, w_upt, "ffn_up")
    act = _ffn_act(u0, fdw, ffn_dw_b)
    dffn, dact, dy, acc_f = _ffn_down_loss(act, wd, h1, tgt, norm_post_ffn)

    g_wd = _mm_tn([act], dffn, "grad_w_down")
    (dug, duv, gfw_g, gfw_v, gfb_g, gfb_v), (s_wd,) = _ffn_act_bwd(u0, dact, fdw, ffn_dw_b, comm=_Scatter([g_wd]))
    g_wupt = _mm_tn([dug, duv], n2, "grad_w_up")
    dh1, acc_u = _ffn_in_bwd(dug, duv, w_upt, h1, dy, norm_pre_ffn)
    dmix, dat, dcv, dgt, dao, dc0, acc_m = _mixer_bwd(dh1, mix, attn, conv, gates, c0, wa, wc, wo, vecs)
    g_wo = _mm_tn([merged], dmix, "grad_w_out")
    g_wa = _mm_tn([ao], dat, "grad_w_attn_proj")
    g_wc = _mm_tn([c1], dcv, "grad_w_conv_proj")
    (da, dg, g_cdw, g_cdb), (s_wa, s_wc, s_wo) = _conv31_bwd(ag, dc0, cdw32, comm=_Scatter([g_wa, g_wc, g_wo]))
    (dqkv, dsink), (s_wup,) = _attn_bwd(q, kv, dao, attn_sinks, tabs, comm=_Scatter([g_wupt]))
    g_wint, g_bin = _mm_tn([dqkv, da, dg, dgt], n1, "grad_w_in", col_sums=True)
    (dh0, acc_i), (s_win,) = _in_bwd(dqkv, da, dg, dgt, w_int, h0p, dh1, norm_pre_mix, comm=_Scatter([g_wint]))

    slots = [s_win, s_wup, s_wa, s_wc, s_wo, s_wd]
    g_big = [_sum_slots(s, "sum_" + nm) for s, nm in zip(slots, ("w_in", "w_up", "w_attn_proj", "w_conv_proj", "w_out", "w_down"))]
    g_big[0] = g_big[0].T
    g_big[1] = g_big[1].T
    big = []
    for nm, w, g, m, v in (("w_in", w_in, g_big[0], m_w_in, v_w_in), ("w_up", w_up, g_big[1], m_w_up, v_w_up),
                           ("w_attn_proj", w_attn_proj, g_big[2], m_w_attn_proj, v_w_attn_proj),
                           ("w_conv_proj", w_conv_proj, g_big[3], m_w_conv_proj, v_w_conv_proj),
                           ("w_out", w_out, g_big[4], m_w_out, v_w_out), ("w_down", w_down, g_big[5], m_w_down, v_w_down)):
        big.append((g[None],) + tuple(o[None] for o in _adamw(w[0], g, m[0], v[0], "adamw_" + nm)))

    loss_row = jnp.sum(acc_f[1:2, :], axis=1, keepdims=True)
    parts = [loss_row, dh0[PAD:BLK], acc_i[0:1], acc_m[0:1], g_bin, dsink[0:1, 0:16], g_cdw[0:CONV_K], g_cdb,
             acc_m[2:3], acc_m[3:4], acc_m[1:2], acc_u[0:1], acc_f[0:1],
             jnp.concatenate([gfw_g, gfw_v], axis=1), jnp.concatenate([gfb_g, gfb_v], axis=1)]
    shapes = [a.shape for a in parts]
    pack_rows = 88
    (gathered,) = _exchange(_Gather([_flat_pack(parts, pack_rows)]), "gather_small_grads")
    tot = _flat_unpack(_sum_slots(gathered, "sum_small_grads"), shapes)
    (loss, g_meta, g_npm, g_nqm, g_bi, g_sk, g_cw, g_cb, g_lg, g_lb, g_bc, g_npf, g_nqf, g_fw, g_fb) = tot
    loss = loss.reshape(())
    g_meta = lax.dynamic_slice_in_dim(g_meta, me * BLK, BLK, axis=1)
    g_cw = lax.dynamic_slice_in_dim(g_cw, me * BLK, BLK, axis=1)[None]
    g_fw = lax.dynamic_slice_in_dim(g_fw, me * up_cols, up_cols, axis=1)[None]

    sm_w = [meta_tokens, norm_pre_mix, norm_post_mix, b_in, attn_sinks, conv_dw_w, conv_dw_b, conv_ln_g, conv_ln_b,
            b_conv_proj, norm_pre_ffn, norm_post_ffn, ffn_dw_w, ffn_dw_b]
    sm_g = [g_meta, g_npm, g_nqm, g_bi, g_sk, g_cw, g_cb, g_lg, g_lb, g_bc, g_npf, g_nqf, g_fw, g_fb]
    sm_m = [m_meta_tokens, m_norm_pre_mix, m_norm_post_mix, m_b_in, m_attn_sinks, m_conv_dw_w, m_conv_dw_b, m_conv_ln_g,
            m_conv_ln_b, m_b_conv_proj, m_norm_pre_ffn, m_norm_post_ffn, m_ffn_dw_w, m_ffn_dw_b]
    sm_v = [v_meta_tokens, v_norm_pre_mix, v_norm_post_mix, v_b_in, v_attn_sinks, v_conv_dw_w, v_conv_dw_b, v_conv_ln_g,
            v_conv_ln_b, v_b_conv_proj, v_norm_pre_ffn, v_norm_post_ffn, v_ffn_dw_w, v_ffn_dw_b]
    sm_shapes = [a.shape for a in sm_w]
    upd_rows = 32
    v_pack = _flat_pack(sm_v, upd_rows)
    sm_out = _adamw(_flat_pack(sm_w, upd_rows), _flat_pack(sm_g, upd_rows), _flat_pack(sm_m, upd_rows), v_pack, "adamw_small")
    sm_d, sm_nm, sm_nv = (_flat_unpack(o, sm_shapes) for o in sm_out)

    order = ["meta_tokens", "norm_pre_mix", "norm_post_mix", "w_in", "b_in", "attn_sinks", "w_attn_proj", "conv_dw_w",
             "conv_dw_b", "conv_ln_g", "conv_ln_b", "w_conv_proj", "b_conv_proj", "w_out", "norm_pre_ffn", "norm_post_ffn",
             "w_up", "ffn_dw_w", "ffn_dw_b", "w_down"]
    small_names = ["meta_tokens", "norm_pre_mix", "norm_post_mix", "b_in", "attn_sinks", "conv_dw_w", "conv_dw_b", "conv_ln_g",
                   "conv_ln_b", "b_conv_proj", "norm_pre_ffn", "norm_post_ffn", "ffn_dw_w", "ffn_dw_b"]
    big_names = ["w_in", "w_up", "w_attn_proj", "w_conv_proj", "w_out", "w_down"]
    table = {}
    for k, nm in enumerate(small_names):
        table[nm] = (sm_g[k], sm_d[k], sm_nm[k], sm_nv[k])
    for k, nm in enumerate(big_names):
        table[nm] = big[k]
    grad_x = dh0[BLK:][None]
    outs = [loss, grad_x]
    for field in range(4):
        outs += [table[nm][field] for nm in order]
    return tuple(outs)
```

```python
import functools

import jax
import jax.numpy as jnp
from jax import lax
from jax.experimental import pallas as pl
from jax.experimental.pallas import tpu as pltpu

F32 = jnp.float32
BF16 = jnp.bfloat16
MESH = pl.DeviceIdType.MESH

D = 1024
HEAD_DIM = 64
N_META = 16
BLK = 128
PAD = BLK - N_META
CONV_K = 31
FFN = 2816
FFN_K = 3
QKV_W = 1280
IN_W = 5376
ROT_DIM = 16
ROPE_THETA = 500000.0
RMS_EPS = 1e-6
LN_EPS = 1e-5
NEG_INF = -1e30
SCALE = HEAD_DIM ** -0.5
N_DEV = 8

ADAM_LR = 0.001
ADAM_B1 = 0.9
ADAM_B2 = 0.999
ADAM_EPS = 1e-08
ADAM_WD = 0.01
ADAM_STEP = 10

VMEM_BYTES_V7X = 64 * 1024 * 1024
VMEM_LIMIT = VMEM_BYTES_V7X - 8 * 1024 * 1024

NT = (((1,), (1,)), ((), ()))
TN = (((0,), (0,)), ((), ()))
VM = pl.BlockSpec(memory_space=pltpu.VMEM)
ANY = pl.BlockSpec(memory_space=pl.ANY)


def _cparams(*sem):
    return pltpu.CompilerParams(dimension_semantics=sem or None, vmem_limit_bytes=VMEM_LIMIT)


def _row_tile(p):
    return 384 if p % 384 == 0 else 128


def _dot(a, b):
    return jnp.dot(a, b, preferred_element_type=F32)


def _dot_nt(a, b):
    return lax.dot_general(a, b, NT, preferred_element_type=F32)


def _dot_tn(a, b):
    return lax.dot_general(a, b, TN, preferred_element_type=F32)


def _rms(x, g):
    return x * lax.rsqrt(jnp.mean(x * x, axis=-1, keepdims=True) + RMS_EPS) * g


def _lnsilu(x, g, b):
    mu = jnp.mean(x, axis=-1, keepdims=True)
    var = jnp.mean(jnp.square(x - mu), axis=-1, keepdims=True)
    z = (x - mu) * lax.rsqrt(var + LN_EPS) * g + b
    return z * jax.nn.sigmoid(z)


def _rope(v, c, s1, s2):
    return v * c + pltpu.roll(v, BLK - 8, 1) * s1 + pltpu.roll(v, 8, 1) * s2


def _rows(i, tm):
    return i * tm + lax.broadcasted_iota(jnp.int32, (tm, 1), 0)


def _place():
    return lax.axis_index("x"), lax.axis_index("y"), lax.axis_index("c")


def _blk(ref, idx, r, dtype):
    return ref.at[pl.ds(pl.multiple_of(idx * r, 16 if dtype == BF16 else 8), r), :]


class _Gather:
    def __init__(self, arrs):
        self.ins = list(arrs)
        n = len(arrs)
        self.out_shape = [jax.ShapeDtypeStruct((N_DEV * a.shape[0], a.shape[1]), a.dtype) for a in arrs]
        self.scratch = [pltpu.SemaphoreType.DMA((n, 7)), pltpu.SemaphoreType.DMA((n, 7)), pltpu.SemaphoreType.DMA((n,))]

    def _parts(self, ins, outs, sems):
        send_sems, recv_sems, local_sems = sems
        n = len(ins)
        x, y, c = _place()
        me, sibling = (x, y, c), (x, y, 1 - c)
        chips = [(1 - x, y), (x, 1 - y), (1 - x, 1 - y)]

        def rows(a, p):
            return _blk(outs[a], 4 * p[0] + 2 * p[1] + p[2], self.ins[a].shape[0], self.ins[a].dtype)

        def copy(a, k, block, to, src=None):
            return pltpu.make_async_remote_copy(
                src_ref=rows(a, block) if src is None else src, dst_ref=rows(a, block),
                send_sem=send_sems.at[a, k], recv_sem=recv_sems.at[a, k], device_id=to, device_id_type=MESH)

        mine = [pltpu.make_async_copy(ins[a], rows(a, me), local_sems.at[a]) for a in range(n)]
        first = []
        for a in range(n):
            first.append(copy(a, 0, me, sibling, src=ins[a]))
            first += [copy(a, 1 + j, me, (*chip, c), src=ins[a]) for j, chip in enumerate(chips)]
        return n, c, me, sibling, chips, copy, mine, first

    def start(self, ins, outs, sems):
        *_, mine, first = self._parts(ins, outs, sems)
        for cp in mine + first:
            cp.start()

    def finish(self, ins, outs, sems):
        n, c, me, sibling, chips, copy, mine, first = self._parts(ins, outs, sems)
        passed = []
        for j, chip in enumerate(chips):
            for a in range(n):
                copy(a, 1 + j, (*chip, c), me).wait_recv()
                fwd = copy(a, 4 + j, (*chip, c), sibling)
                fwd.start()
                passed.append(fwd)
        for a in range(n):
            copy(a, 0, sibling, me).wait_recv()
            for j, chip in enumerate(chips):
                copy(a, 4 + j, (*chip, 1 - c), me).wait_recv()
        for cp in first + passed:
            cp.wait_send()
        for cp in mine:
            cp.wait()


FLIPS = [(0, 0, 1), (1, 0, 0), (0, 1, 0), (1, 1, 0), (1, 0, 1), (0, 1, 1), (1, 1, 1)]


class _Scatter:
    def __init__(self, arrs):
        self.ins = list(arrs)
        n = len(arrs)
        self.out_shape = [jax.ShapeDtypeStruct(a.shape, a.dtype) for a in arrs]
        self.scratch = [pltpu.SemaphoreType.DMA((n, 7)), pltpu.SemaphoreType.DMA((n, 7)), pltpu.SemaphoreType.DMA((n,))]

    def _parts(self, ins, outs, sems):
        send_sems, recv_sems, local_sems = sems
        n = len(ins)
        x, y, c = _place()
        me = 4 * x + 2 * y + c

        def flip(v, f):
            return 1 - v if f else v

        def blk(ref, a, idx):
            return _blk(ref, idx, self.ins[a].shape[0] // N_DEV, self.ins[a].dtype)

        mine = [pltpu.make_async_copy(blk(ins[a], a, me), blk(outs[a], a, me), local_sems.at[a]) for a in range(n)]
        sends, recvs = [], []
        for k, f in enumerate(FLIPS):
            peer = (flip(x, f[0]), flip(y, f[1]), flip(c, f[2]))
            pidx = 4 * peer[0] + 2 * peer[1] + peer[2]
            for a in range(n):
                sends.append(pltpu.make_async_remote_copy(
                    src_ref=blk(ins[a], a, pidx), dst_ref=blk(outs[a], a, me),
                    send_sem=send_sems.at[a, k], recv_sem=recv_sems.at[a, k], device_id=peer, device_id_type=MESH))
                recvs.append(functools.partial(
                    pltpu.make_async_remote_copy,
                    src_ref=blk(ins[a], a, pidx), dst_ref=blk(outs[a], a, pidx),
                    send_sem=send_sems.at[a, k], recv_sem=recv_sems.at[a, k], device_id=peer, device_id_type=MESH))
        return mine, sends, recvs

    def start(self, ins, outs, sems):
        mine, sends, _ = self._parts(ins, outs, sems)
        for cp in mine + sends:
            cp.start()

    def finish(self, ins, outs, sems):
        mine, sends, recvs = self._parts(ins, outs, sems)
        for make in recvs:
            make().wait_recv()
        for cp in sends:
            cp.wait_send()
        for cp in mine:
            cp.wait()


def _exchange(comm, name):
    n, m = len(comm.ins), len(comm.out_shape)

    def body(*refs):
        ins, outs, sems = refs[:n], refs[n:n + m], refs[n + m:]
        comm.start(ins, outs, sems)
        comm.finish(ins, outs, sems)

    return pl.pallas_call(
        body, name=name, out_shape=comm.out_shape, in_specs=[ANY] * n, out_specs=[ANY] * m, scratch_shapes=comm.scratch,
    )(*comm.ins)


def _call(body, *, name, grid, in_specs, out_specs, out_shape, args, scratch=(), sem="parallel", comm=None):
    if comm is None:
        outs = pl.pallas_call(
            body, name=name, grid=grid, in_specs=list(in_specs), out_specs=list(out_specs), out_shape=list(out_shape),
            scratch_shapes=list(scratch), compiler_params=_cparams(sem))(*args)
        return outs, []
    n_in, n_out, n_sc = len(in_specs), len(out_specs), len(scratch)
    n_ci, n_co = len(comm.ins), len(comm.out_shape)
    last = grid[0] - 1

    def fused(*refs):
        ins, refs = refs[:n_in], refs[n_in:]
        c_ins, refs = refs[:n_ci], refs[n_ci:]
        outs, refs = refs[:n_out], refs[n_out:]
        c_outs, refs = refs[:n_co], refs[n_co:]
        sc, c_sems = refs[:n_sc], refs[n_sc:]
        step = pl.program_id(0)

        @pl.when(step == 0)
        def _():
            comm.start(c_ins, c_outs, c_sems)

        body(*ins, *outs, *sc)

        @pl.when(step == last)
        def _():
            comm.finish(c_ins, c_outs, c_sems)

    outs = pl.pallas_call(
        fused, name=name, grid=grid, in_specs=list(in_specs) + [ANY] * n_ci, out_specs=list(out_specs) + [ANY] * n_co,
        out_shape=list(out_shape) + comm.out_shape, scratch_shapes=list(scratch) + comm.scratch,
        compiler_params=_cparams("arbitrary"))(*args, *comm.ins)
    return outs[:n_out], outs[n_out:]


def _token_specs(tm):
    k = tm // BLK
    return [pl.BlockSpec((BLK, D), functools.partial(lambda i, t: (jnp.maximum(k * i + t - 1, 0), 0), t=t)) for t in range(k)]


def _in_proj(x2d, meta, gain, w_int, b_in, tabs, comm=None):
    p = x2d.shape[0] + BLK
    tm = _row_tile(p)
    k = tm // BLK

    def body(*refs):
        x_refs = refs[:k]
        m_ref, g_ref, w_ref, b_ref, t_ref, h_ref, n1_ref, q_ref, kv_ref, ag_ref, gt_ref = refs[k:]
        i = pl.program_id(0)
        head = jnp.concatenate([jnp.zeros((PAD, D), F32), m_ref[...]], axis=0)
        first = jnp.where(i == 0, head, x_refs[0][...])
        h = jnp.concatenate([first] + [r[...] for r in x_refs[1:]], axis=0) if k > 1 else first
        h_ref[...] = h
        n = _rms(h, g_ref[...]).astype(BF16)
        n1_ref[...] = n
        c, s1, s2 = t_ref[:, 0:128], t_ref[:, 128:256], t_ref[:, 256:384]

        def mm(c0, w):
            return _dot_nt(n, w_ref[c0:c0 + w, :]) + b_ref[:, c0:c0 + w]

        for j in range(4):
            acc = mm(256 * j, 256)
            for t in range(2):
                lo = 256 * j + 128 * t
                q_ref[:, lo:lo + 128] = (_rope(acc[:, 128 * t:128 * (t + 1)], c, s1, s2) * SCALE).astype(BF16)
        acc = mm(1024, 256)
        kv_ref[:, 0:128] = _rope(acc[:, 0:128], c, s1, s2).astype(BF16)
        kv_ref[:, 128:256] = acc[:, 128:256].astype(BF16)
        for j in range(8):
            ag_ref[:, 256 * j:256 * (j + 1)] = mm(QKV_W + 256 * j, 256).astype(BF16)
        for j in range(8):
            gt_ref[:, 256 * j:256 * (j + 1)] = mm(QKV_W + 2048 + 256 * j, 256).astype(BF16)

    def row(w):
        return pl.BlockSpec((tm, w), lambda i: (i, 0))

    return _call(
        body, name="in_proj", grid=(p // tm,),
        in_specs=_token_specs(tm) + [VM, VM, VM, VM, row(384)],
        out_specs=[row(D), row(D), row(D), row(256), row(2048), row(2048)],
        out_shape=[jax.ShapeDtypeStruct((p, D), F32)] + [jax.ShapeDtypeStruct((p, w), BF16) for w in (D, D, 256, 2048, 2048)],
        args=(x2d,) * k + (meta, gain, w_int, b_in, tabs), comm=comm)


def _attn_probs(n, h, q_ref, km_ref, kp_ref, kc_ref, sink_ref):
    lane = lax.broadcasted_iota(jnp.int32, (BLK, BLK), 1)
    lo = lane < HEAD_DIM
    lo3 = lax.broadcasted_iota(jnp.int32, (3 * BLK, BLK), 1) < HEAD_DIM

    def dup(lanes):
        cat = jnp.concatenate([km_ref[:, lanes], kp_ref[:, lanes], kc_ref[:, lanes]], axis=0).astype(F32)
        rolled = pltpu.roll(cat, HEAD_DIM, 1)
        return (jnp.where(lo3, cat, rolled) if h == 0 else jnp.where(lo3, rolled, cat)).astype(BF16)

    k2 = dup(slice(0, 128))
    v2 = dup(slice(128, 256))
    qs = _stack_heads(q_ref, h, lo)
    s = _dot_nt(qs, k2).reshape(8, BLK, 3 * BLK)

    r = lax.broadcasted_iota(jnp.int32, (BLK, BLK), 0)
    tq = BLK * n + r - PAD
    t_m = lane - PAD
    t_p = BLK * (n - 1) + lane - PAD
    t_c = BLK * n + lane - PAD
    ok_m = jnp.logical_and(t_m >= 0, t_m <= tq)
    ok_p = jnp.logical_and(t_p >= N_META, tq - t_p < BLK)
    ok_c = jnp.logical_and(t_c >= N_META, t_c <= tq)
    bias = jnp.concatenate([jnp.where(ok, 0.0, NEG_INF).astype(F32) for ok in (ok_m, ok_p, ok_c)], axis=1)
    s = s + bias[None]

    gidx = lax.broadcasted_iota(jnp.int32, (8, 1, 1), 0)
    sink = jnp.zeros((8, 1, 1), F32)
    for g in range(8):
        sink = jnp.where(gidx == g, sink_ref[0, 8 * h + g], sink)
    m = jnp.maximum(jnp.max(s, axis=-1, keepdims=True), sink)
    e = jnp.exp(s - m)
    es = jnp.exp(sink - m)
    inv = 1.0 / (jnp.sum(e, axis=-1, keepdims=True) + es)
    return qs, k2, v2, e * inv, es * inv


def _stack_heads(ref, h, lo):
    pieces = []
    for jp in range(4):
        v = ref[:, BLK * (4 * h + jp):BLK * (4 * h + jp + 1)]
        zero = jnp.zeros_like(v)
        pieces += [jnp.where(lo, v, zero), jnp.where(lo, zero, v)]
    return jnp.concatenate(pieces, axis=0)


def _unstack_heads(v, jp, lo):
    return jnp.where(lo, v[256 * jp:256 * jp + 128], v[256 * jp + 128:256 * jp + 256])


def _attn_fwd(q, kv, sinks, comm=None):
    p = q.shape[0]
    nb = p // BLK

    def body(q_ref, km_ref, kp_ref, kc_ref, sink_ref, o_ref):
        n = pl.program_id(0)
        lo = lax.broadcasted_iota(jnp.int32, (BLK, BLK), 1) < HEAD_DIM
        for h in range(2):
            _, _, v2, pn, _ = _attn_probs(n, h, q_ref, km_ref, kp_ref, kc_ref, sink_ref)
            o = _dot(pn.reshape(8 * BLK, 3 * BLK).astype(BF16), v2)
            for jp in range(4):
                o_ref[:, BLK * (4 * h + jp):BLK * (4 * h + jp + 1)] = _unstack_heads(o, jp, lo).astype(BF16)

    return _call(
        body, name="attn_fwd", grid=(nb,),
        in_specs=[pl.BlockSpec((BLK, D), lambda i: (i, 0)),
                  pl.BlockSpec((BLK, 256), lambda i: (0, 0)),
                  pl.BlockSpec((BLK, 256), lambda i: (jnp.maximum(i - 1, 0), 0)),
                  pl.BlockSpec((BLK, 256), lambda i: (i, 0)),
                  pl.BlockSpec(memory_space=pltpu.SMEM)],
        out_specs=[pl.BlockSpec((BLK, D), lambda i: (i, 0))],
        out_shape=[jax.ShapeDtypeStruct((p, D), BF16)],
        args=(q, kv, kv, kv, sinks), comm=comm)


def _conv31_fwd(ag, w32, b, comm=None):
    p = ag.shape[0]
    nch = p // BLK

    def body(a_ref, g_ref, w_ref, b_ref, o_ref, gp):
        gp[0:32, :] = jnp.zeros((32, BLK), F32)
        for ci in range(nch):
            r0 = BLK * ci
            glu = a_ref[r0:r0 + BLK, :].astype(F32) * jax.nn.sigmoid(g_ref[r0:r0 + BLK, :].astype(F32))
            if ci == 0:
                glu = jnp.where(_rows(0, BLK) >= PAD, glu, 0.0)
            gp[32 + r0:32 + r0 + BLK, :] = glu
        for ci in range(nch):
            r0 = BLK * ci
            acc = jnp.broadcast_to(b_ref[...], (BLK, BLK))
            for j in range(CONV_K):
                acc = acc + w_ref[j:j + 1, :] * gp[r0 + j + 2:r0 + j + 2 + BLK, :]
            o_ref[r0:r0 + BLK, :] = acc

    return _call(
        body, name="conv31_fwd", grid=(D // BLK,),
        in_specs=[pl.BlockSpec((p, BLK), lambda j: (0, j)), pl.BlockSpec((p, BLK), lambda j: (0, 8 + j)),
                  pl.BlockSpec((32, BLK), lambda j: (0, j)), pl.BlockSpec((1, BLK), lambda j: (0, j))],
        out_specs=[pl.BlockSpec((p, BLK), lambda j: (0, j))],
        out_shape=[jax.ShapeDtypeStruct((p, D), F32)],
        scratch=[pltpu.VMEM((p + 32, BLK), F32)],
        args=(ag, ag, w32, b), comm=comm)


def _mixer_fwd(ao, c0, gates, h0p, wa, wc, wo, vecs):
    p = ao.shape[0]
    tm = _row_tile(p)

    def body(ao_ref, c0_ref, gt_ref, h_ref, wa_ref, wc_ref, wo_ref, v_ref,
             c1_ref, at_ref, cv_ref, mg_ref, mix_ref, h1_ref, n2_ref):
        i = pl.program_id(0)
        c1 = _lnsilu(c0_ref[...], v_ref[0:1, :], v_ref[1:2, :]).astype(BF16)
        c1_ref[...] = c1
        attn = _dot(ao_ref[...], wa_ref[...])
        conv = _dot(c1, wc_ref[...]) + v_ref[2:3, :]
        at_ref[...] = attn.astype(BF16)
        cv_ref[...] = conv.astype(BF16)
        merged = (jax.nn.sigmoid(gt_ref[:, 0:D].astype(F32)) * attn
                  + jax.nn.sigmoid(gt_ref[:, D:2 * D].astype(F32)) * conv).astype(BF16)
        mg_ref[...] = merged
        mix = _dot(merged, wo_ref[...])
        mix_ref[...] = mix
        h1 = jnp.where(_rows(i, tm) >= PAD, h_ref[...] + _rms(mix, v_ref[3:4, :]), 0.0)
        h1_ref[...] = h1
        n2_ref[...] = _rms(h1, v_ref[4:5, :]).astype(BF16)

    def row(w):
        return pl.BlockSpec((tm, w), lambda i: (i, 0))

    return pl.pallas_call(
        body, name="mixer_fwd", grid=(p // tm,),
        in_specs=[row(D), row(D), row(2 * D), row(D), VM, VM, VM, VM],
        out_specs=[row(D)] * 7,
        out_shape=[jax.ShapeDtypeStruct((p, D), t) for t in (BF16, BF16, BF16, BF16, F32, F32, BF16)],
        compiler_params=_cparams("parallel"),
    )(ao, c0, gates, h0p, wa, wc, wo, vecs)


def _mm_nt(a, w_t, name):
    p, k = a.shape
    n = w_t.shape[0]
    tm = _row_tile(p)
    ch = 512

    def body(a_ref, w_ref, o_ref):
        a_v = a_ref[...]
        for c0 in range(0, n, ch):
            o_ref[:, c0:c0 + ch] = _dot_nt(a_v, w_ref[c0:c0 + ch, :]).astype(BF16)

    return pl.pallas_call(
        body, name=name, grid=(p // tm,),
        in_specs=[pl.BlockSpec((tm, k), lambda i: (i, 0)), VM],
        out_specs=pl.BlockSpec((tm, n), lambda i: (i, 0)),
        out_shape=jax.ShapeDtypeStruct((p, n), BF16),
        compiler_params=_cparams("parallel"),
    )(a, w_t)


def _conv3(xp_ref, w_ref, r0):
    return (w_ref[0:1, :] * xp_ref[r0 + 6:r0 + 6 + BLK, :] + w_ref[1:2, :] * xp_ref[r0 + 7:r0 + 7 + BLK, :]
            + w_ref[2:3, :] * xp_ref[r0 + 8:r0 + 8 + BLK, :])


def _ffn_slab_specs(p):
    ncol = FFN // BLK
    return [pl.BlockSpec((p, BLK), lambda j: (0, j)), pl.BlockSpec((p, BLK), lambda j: (0, ncol + j)),
            pl.BlockSpec((FFN_K, BLK), lambda j: (0, j)), pl.BlockSpec((FFN_K, BLK), lambda j: (0, ncol + j)),
            pl.BlockSpec((1, BLK), lambda j: (0, j)), pl.BlockSpec((1, BLK), lambda j: (0, ncol + j))]


def _fill_shifted(dst, src_ref, nch):
    dst[0:8, :] = jnp.zeros((8, BLK), F32)
    for ci in range(nch):
        dst[8 + BLK * ci:8 + BLK * (ci + 1), :] = src_ref[BLK * ci:BLK * (ci + 1), :].astype(F32)


def _ffn_act(u0, fw, fb):
    p = u0.shape[0]
    nch = p // BLK

    def body(g_ref, v_ref, wg_ref, wv_ref, bg_ref, bv_ref, o_ref, xg, xv):
        _fill_shifted(xg, g_ref, nch)
        _fill_shifted(xv, v_ref, nch)
        for ci in range(nch):
            r0 = BLK * ci
            ug = _conv3(xg, wg_ref, r0) + bg_ref[...]
            uv = _conv3(xv, wv_ref, r0) + bv_ref[...]
            o_ref[r0:r0 + BLK, :] = (ug * jax.nn.sigmoid(ug) * uv).astype(BF16)

    return pl.pallas_call(
        body, name="ffn_act", grid=(FFN // BLK,),
        in_specs=_ffn_slab_specs(p),
        out_specs=pl.BlockSpec((p, BLK), lambda j: (0, j)),
        out_shape=jax.ShapeDtypeStruct((p, FFN), BF16),
        scratch_shapes=[pltpu.VMEM((p + 8, BLK), F32)] * 2,
        compiler_params=_cparams("parallel"),
    )(u0, u0, fw, fw, fb, fb)


def _ffn_down_loss(act, wd, h1, tgt, gain):
    p = act.shape[0]
    tm = _row_tile(p)
    k = tm // BLK

    def body(*refs):
        a_ref, w_ref, h_ref = refs[:3]
        t_refs = refs[3:3 + k]
        g_ref, df_ref, da_ref, dy_ref, acc_ref = refs[3 + k:]
        i = pl.program_id(0)

        @pl.when(i == 0)
        def _():
            acc_ref[...] = jnp.zeros_like(acc_ref)

        ffn = _dot(a_ref[...], w_ref[...])
        r, vjp = jax.vjp(_rms, ffn, g_ref[...])
        t = jnp.concatenate([t_ref[...] for t_ref in t_refs], axis=0) if k > 1 else t_refs[0][...]
        diff = jnp.where(_rows(i, tm) >= BLK, h_ref[...] + r - t, 0.0)
        dy = diff * (1.0 / D)
        dffn, dg = vjp(dy)
        acc_ref[0:1, :] += dg
        acc_ref[1:2, :] += jnp.sum(diff * diff, axis=0, keepdims=True) * (0.5 / D)
        dy_ref[...] = dy
        dfb = dffn.astype(BF16)
        df_ref[...] = dfb
        for c0 in range(0, FFN, 256):
            da_ref[:, c0:c0 + 256] = _dot_nt(dfb, w_ref[c0:c0 + 256, :]).astype(BF16)

    def row(w):
        return pl.BlockSpec((tm, w), lambda i: (i, 0))

    return pl.pallas_call(
        body, name="ffn_down_loss", grid=(p // tm,),
        in_specs=[row(FFN), VM, row(D)] + _token_specs(tm) + [VM],
        out_specs=[row(D), row(FFN), row(D), pl.BlockSpec((8, D), lambda i: (0, 0))],
        out_shape=[jax.ShapeDtypeStruct((p, D), BF16), jax.ShapeDtypeStruct((p, FFN), BF16),
                   jax.ShapeDtypeStruct((p, D), F32), jax.ShapeDtypeStruct((8, D), F32)],
        compiler_params=_cparams("arbitrary"),
    )(act, wd, h1, *([tgt] * k), gain)


def _mm_tn(pieces, b, name, col_sums=False):
    p, n = b.shape
    tk = 256
    nblk = [a.shape[1] // tk for a in pieces]
    offs = [sum(nblk[:q]) for q in range(len(pieces))]
    total = sum(nblk)
    npc = len(pieces)

    def body(*refs):
        a_refs, b_ref, o_ref = refs[:npc], refs[npc], refs[npc + 1]
        i = pl.program_id(0)
        for q, a_ref in enumerate(a_refs):
            @pl.when(jnp.logical_and(i >= offs[q], i < offs[q] + nblk[q]))
            def _(a_ref=a_ref):
                a_v = a_ref[...]
                o_ref[...] = _dot_tn(a_v, b_ref[...]).astype(BF16)
                if col_sums:
                    refs[npc + 2][...] = jnp.sum(a_v.astype(F32), axis=0, keepdims=True)

    def a_spec(q):
        return pl.BlockSpec((p, tk), lambda i: (0, jnp.clip(i - offs[q], 0, nblk[q] - 1)))

    out_specs = [pl.BlockSpec((tk, n), lambda i: (i, 0))]
    out_shape = [jax.ShapeDtypeStruct((total * tk, n), BF16)]
    if col_sums:
        out_specs.append(pl.BlockSpec((1, tk), lambda i: (0, i)))
        out_shape.append(jax.ShapeDtypeStruct((1, total * tk), F32))
    res = pl.pallas_call(
        body, name=name, grid=(total,),
        in_specs=[a_spec(q) for q in range(npc)] + [VM],
        out_specs=out_specs, out_shape=out_shape,
        compiler_params=_cparams("parallel"),
    )(*pieces, b)
    return res if col_sums else res[0]


def _ffn_act_bwd(u0, dact, fw, fb, comm=None):
    p = u0.shape[0]
    nch = p // BLK
    ncol = FFN // BLK

    def body(g_ref, v_ref, wg_ref, wv_ref, bg_ref, bv_ref, da_ref,
             dg_ref, dv_ref, gwg_ref, gwv_ref, gbg_ref, gbv_ref, xg, xv, eg, ev):
        _fill_shifted(xg, g_ref, nch)
        _fill_shifted(xv, v_ref, nch)
        eg[p:p + 8, :] = jnp.zeros((8, BLK), F32)
        ev[p:p + 8, :] = jnp.zeros((8, BLK), F32)
        for ci in range(nch):
            r0 = BLK * ci
            ug = _conv3(xg, wg_ref, r0) + bg_ref[...]
            uv = _conv3(xv, wv_ref, r0) + bv_ref[...]
            sg = jax.nn.sigmoid(ug)
            d = da_ref[r0:r0 + BLK, :].astype(F32)
            eg[r0:r0 + BLK, :] = d * uv * (sg * (1.0 + ug * (1.0 - sg)))
            ev[r0:r0 + BLK, :] = d * ug * sg
        for e_s, x_s, w_ref, d_ref, gw_ref, gb_ref in ((eg, xg, wg_ref, dg_ref, gwg_ref, gbg_ref),
                                                      (ev, xv, wv_ref, dv_ref, gwv_ref, gbv_ref)):
            sums = [jnp.zeros((BLK, BLK), F32) for _ in range(FFN_K + 1)]
            for ci in range(nch):
                r0 = BLK * ci
                e0 = e_s[r0:r0 + BLK, :]
                du = (w_ref[2:3, :] * e0 + w_ref[1:2, :] * e_s[r0 + 1:r0 + 1 + BLK, :]
                      + w_ref[0:1, :] * e_s[r0 + 2:r0 + 2 + BLK, :])
                if ci == 0:
                    du = jnp.where(_rows(0, BLK) >= PAD, du, 0.0)
                d_ref[r0:r0 + BLK, :] = du.astype(BF16)
                for j in range(FFN_K):
                    sums[j] = sums[j] + e0 * x_s[r0 + 6 + j:r0 + 6 + j + BLK, :]
                sums[FFN_K] = sums[FFN_K] + e0
            for j in range(FFN_K):
                gw_ref[j:j + 1, :] = jnp.sum(sums[j], axis=0, keepdims=True)
            gb_ref[...] = jnp.sum(sums[FFN_K], axis=0, keepdims=True)

    slab = pl.BlockSpec((p, BLK), lambda j: (0, j))
    wspec = pl.BlockSpec((FFN_K, BLK), lambda j: (0, j))
    bspec = pl.BlockSpec((1, BLK), lambda j: (0, j))
    return _call(
        body, name="ffn_act_bwd", grid=(ncol,),
        in_specs=_ffn_slab_specs(p) + [slab],
        out_specs=[slab, slab, wspec, wspec, bspec, bspec],
        out_shape=[jax.ShapeDtypeStruct((p, FFN), BF16)] * 2 + [jax.ShapeDtypeStruct((FFN_K, FFN), F32)] * 2
        + [jax.ShapeDtypeStruct((1, FFN), F32)] * 2,
        scratch=[pltpu.VMEM((p + 8, BLK), F32)] * 4,
        args=(u0, u0, fw, fw, fb, fb, dact), comm=comm)


def _ffn_in_bwd(dug, duv, w_upt, h1, dy, gain):
    p = h1.shape[0]
    tm = _row_tile(p)

    def body(dg_ref, dv_ref, w_ref, h_ref, dy_ref, g_ref, o_ref, acc_ref):
        i = pl.program_id(0)

        @pl.when(i == 0)
        def _():
            acc_ref[...] = jnp.zeros_like(acc_ref)

        dn = _dot(dg_ref[...], w_ref[0:FFN, :]) + _dot(dv_ref[...], w_ref[FFN:2 * FFN, :])
        _, vjp = jax.vjp(_rms, h_ref[...], g_ref[...])
        dh, dg = vjp(dn)
        o_ref[...] = dy_ref[...] + dh
        acc_ref[0:1, :] += dg

    def row(w):
        return pl.BlockSpec((tm, w), lambda i: (i, 0))

    return pl.pallas_call(
        body, name="ffn_in_bwd", grid=(p // tm,),
        in_specs=[row(FFN), row(FFN), VM, row(D), row(D), VM],
        out_specs=[row(D), pl.BlockSpec((8, D), lambda i: (0, 0))],
        out_shape=[jax.ShapeDtypeStruct((p, D), F32), jax.ShapeDtypeStruct((8, D), F32)],
        compiler_params=_cparams("arbitrary"),
    )(dug, duv, w_upt, h1, dy, gain)


def _mixer_bwd(dh1, mix, attn, conv, gates, c0, wa, wc, wo, vecs):
    p = dh1.shape[0]
    tm = _row_tile(p)

    def body(dh_ref, mix_ref, at_ref, cv_ref, gt_ref, c0_ref, wa_ref, wc_ref, wo_ref, v_ref,
             dmix_ref, dat_ref, dcv_ref, dgt_ref, dao_ref, dc0_ref, acc_ref):
        i = pl.program_id(0)

        @pl.when(i == 0)
        def _():
            acc_ref[...] = jnp.zeros_like(acc_ref)

        _, vjp = jax.vjp(_rms, mix_ref[...], v_ref[3:4, :])
        dmix, dgp = vjp(dh_ref[...])
        dmix = dmix.astype(BF16)
        dmix_ref[...] = dmix
        dmg = _dot_nt(dmix, wo_ref[...])
        sa = jax.nn.sigmoid(gt_ref[:, 0:D].astype(F32))
        sc = jax.nn.sigmoid(gt_ref[:, D:2 * D].astype(F32))
        dat = dmg * sa
        dcv = dmg * sc
        dgt_ref[:, 0:D] = (dmg * at_ref[...].astype(F32) * sa * (1.0 - sa)).astype(BF16)
        dgt_ref[:, D:2 * D] = (dmg * cv_ref[...].astype(F32) * sc * (1.0 - sc)).astype(BF16)
        datb = dat.astype(BF16)
        dcvb = dcv.astype(BF16)
        dat_ref[...] = datb
        dcv_ref[...] = dcvb
        dao_ref[...] = _dot_nt(datb, wa_ref[...]).astype(BF16)
        dc1 = _dot_nt(dcvb, wc_ref[...])
        _, vjp2 = jax.vjp(_lnsilu, c0_ref[...], v_ref[0:1, :], v_ref[1:2, :])
        dc0, dlg, dlb = vjp2(dc1)
        dc0_ref[...] = dc0
        acc_ref[0:1, :] += dgp
        acc_ref[1:2, :] += jnp.sum(dcv, axis=0, keepdims=True)
        acc_ref[2:3, :] += dlg
        acc_ref[3:4, :] += dlb

    def row(w):
        return pl.BlockSpec((tm, w), lambda i: (i, 0))

    return pl.pallas_call(
        body, name="mixer_bwd", grid=(p // tm,),
        in_specs=[row(D), row(D), row(D), row(D), row(2 * D), row(D), VM, VM, VM, VM],
        out_specs=[row(D), row(D), row(D), row(2 * D), row(D), row(D), pl.BlockSpec((8, D), lambda i: (0, 0))],
        out_shape=[jax.ShapeDtypeStruct((p, D), BF16)] * 3 + [jax.ShapeDtypeStruct((p, 2 * D), BF16),
                                                             jax.ShapeDtypeStruct((p, D), BF16),
                                                             jax.ShapeDtypeStruct((p, D), F32),
                                                             jax.ShapeDtypeStruct((8, D), F32)],
        compiler_params=_cparams("arbitrary"),
    )(dh1, mix, attn, conv, gates, c0, wa, wc, wo, vecs)


def _conv31_bwd(ag, dc0, w32, comm=None):
    p = ag.shape[0]
    nch = p // BLK

    def body(a_ref, g_ref, dc_ref, w_ref, da_ref, dg_ref, gw_ref, gb_ref, gp, dp):
        gp[0:32, :] = jnp.zeros((32, BLK), F32)
        dp[p:p + 32, :] = jnp.zeros((32, BLK), F32)
        bsum = jnp.zeros((BLK, BLK), F32)
        for ci in range(nch):
            r0 = BLK * ci
            glu = a_ref[r0:r0 + BLK, :].astype(F32) * jax.nn.sigmoid(g_ref[r0:r0 + BLK, :].astype(F32))
            if ci == 0:
                glu = jnp.where(_rows(0, BLK) >= PAD, glu, 0.0)
            gp[32 + r0:32 + r0 + BLK, :] = glu
            d = dc_ref[r0:r0 + BLK, :]
            dp[r0:r0 + BLK, :] = d
            bsum = bsum + d
        gb_ref[...] = jnp.sum(bsum, axis=0, keepdims=True)
        for ci in range(nch):
            r0 = BLK * ci
            acc = jnp.zeros((BLK, BLK), F32)
            for j in range(CONV_K):
                acc = acc + w_ref[j:j + 1, :] * dp[r0 + 30 - j:r0 + 30 - j + BLK, :]
            if ci == 0:
                acc = jnp.where(_rows(0, BLK) >= PAD, acc, 0.0)
            a = a_ref[r0:r0 + BLK, :].astype(F32)
            sg = jax.nn.sigmoid(g_ref[r0:r0 + BLK, :].astype(F32))
            da_ref[r0:r0 + BLK, :] = (acc * sg).astype(BF16)
            dg_ref[r0:r0 + BLK, :] = (acc * a * sg * (1.0 - sg)).astype(BF16)
        for j in range(CONV_K):
            acc = jnp.zeros((BLK, BLK), F32)
            for ci in range(nch):
                r0 = BLK * ci
                acc = acc + dp[r0:r0 + BLK, :] * gp[r0 + j + 2:r0 + j + 2 + BLK, :]
            gw_ref[j:j + 1, :] = jnp.sum(acc, axis=0, keepdims=True)
        gw_ref[CONV_K:32, :] = jnp.zeros((32 - CONV_K, BLK), F32)

    slab = pl.BlockSpec((p, BLK), lambda j: (0, j))
    return _call(
        body, name="conv31_bwd", grid=(D // BLK,),
        in_specs=[slab, pl.BlockSpec((p, BLK), lambda j: (0, 8 + j)), slab, pl.BlockSpec((32, BLK), lambda j: (0, j))],
        out_specs=[slab, slab, pl.BlockSpec((32, BLK), lambda j: (0, j)), pl.BlockSpec((1, BLK), lambda j: (0, j))],
        out_shape=[jax.ShapeDtypeStruct((p, D), BF16)] * 2 + [jax.ShapeDtypeStruct((32, D), F32),
                                                             jax.ShapeDtypeStruct((1, D), F32)],
        scratch=[pltpu.VMEM((p + 32, BLK), F32)] * 2,
        args=(ag, ag, dc0, w32), comm=comm)


def _attn_bwd(q, kv, dao, sinks, tabs, comm=None):
    p = q.shape[0]
    nb = p // BLK

    def body(q_ref, km_ref, kp_ref, kc_ref, do_ref, sink_ref, t_ref, dqkv_ref, dsink_ref, carry, macc):
        i = pl.program_id(0)
        n = nb - 1 - i

        @pl.when(i == 0)
        def _():
            carry[...] = jnp.zeros_like(carry)
            macc[...] = jnp.zeros_like(macc)
            dsink_ref[...] = jnp.zeros_like(dsink_ref)

        lane = lax.broadcasted_iota(jnp.int32, (BLK, BLK), 1)
        lo = lane < HEAD_DIM
        lo3 = lax.broadcasted_iota(jnp.int32, (3 * BLK, BLK), 1) < HEAD_DIM
        lane8 = lax.broadcasted_iota(jnp.int32, (8, BLK), 1)
        c, s1, s2 = t_ref[:, 0:128], -t_ref[:, 128:256], -t_ref[:, 256:384]
        dk = jnp.zeros((3 * BLK, BLK), F32)
        dv = jnp.zeros((3 * BLK, BLK), F32)
        for h in range(2):
            qs, k2, v2, pn, ps = _attn_probs(n, h, q_ref, km_ref, kp_ref, kc_ref, sink_ref)
            dos = _stack_heads(do_ref, h, lo)
            dp = _dot_nt(dos, v2).reshape(8, BLK, 3 * BLK)
            delta = jnp.sum(pn * dp, axis=-1, keepdims=True)
            ds = (pn * (dp - delta)).reshape(8 * BLK, 3 * BLK).astype(BF16)
            dsk = -jnp.sum(ps * delta, axis=1, keepdims=True)
            for g in range(8):
                dsink_ref[...] += jnp.where(lane8 == 8 * h + g, dsk[g], 0.0)
            dq = _dot(ds, k2)
            for jp in range(4):
                lo_c = BLK * (4 * h + jp)
                dqkv_ref[:, lo_c:lo_c + BLK] = (_rope(_unstack_heads(dq, jp, lo), c, s1, s2) * SCALE).astype(BF16)
            dk2 = _dot_tn(ds, qs)
            dv2 = _dot_tn(pn.reshape(8 * BLK, 3 * BLK).astype(BF16), dos)
            dk2 = dk2 + pltpu.roll(dk2, HEAD_DIM, 1)
            dv2 = dv2 + pltpu.roll(dv2, HEAD_DIM, 1)
            own = lo3 if h == 0 else jnp.logical_not(lo3)
            dk = jnp.where(own, dk2, dk)
            dv = jnp.where(own, dv2, dv)
        macc[:, 0:BLK] += dk[0:BLK]
        macc[:, BLK:2 * BLK] += dv[0:BLK]
        last = (n == 0).astype(F32)
        dk_c = dk[2 * BLK:3 * BLK] + carry[:, 0:BLK] + last * macc[:, 0:BLK]
        dv_c = dv[2 * BLK:3 * BLK] + carry[:, BLK:2 * BLK] + last * macc[:, BLK:2 * BLK]
        carry[:, 0:BLK] = dk[BLK:2 * BLK]
        carry[:, BLK:2 * BLK] = dv[BLK:2 * BLK]
        dqkv_ref[:, D:D + BLK] = _rope(dk_c, c, s1, s2).astype(BF16)
        dqkv_ref[:, D + BLK:D + 2 * BLK] = dv_c.astype(BF16)

    def rev(w):
        return pl.BlockSpec((BLK, w), lambda i: (nb - 1 - i, 0))

    return _call(
        body, name="attn_bwd", grid=(nb,),
        in_specs=[rev(D),
                  pl.BlockSpec((BLK, 256), lambda i: (0, 0)),
                  pl.BlockSpec((BLK, 256), lambda i: (jnp.maximum(nb - 2 - i, 0), 0)),
                  rev(256), rev(D),
                  pl.BlockSpec(memory_space=pltpu.SMEM), rev(384)],
        out_specs=[rev(QKV_W), pl.BlockSpec((8, BLK), lambda i: (0, 0))],
        out_shape=[jax.ShapeDtypeStruct((p, QKV_W), BF16), jax.ShapeDtypeStruct((8, BLK), F32)],
        scratch=[pltpu.VMEM((BLK, 256), F32)] * 2, sem="arbitrary",
        args=(q, kv, kv, kv, dao, sinks, tabs), comm=comm)


def _in_bwd(dqkv, da, dg, dgt, w_int, h0p, dh1, gain, comm=None):
    p = h0p.shape[0]
    tm = _row_tile(p)

    def body(dq_ref, da_ref, dg_ref, dt_ref, w_ref, h_ref, dh_ref, g_ref, o_ref, acc_ref):
        i = pl.program_id(0)

        @pl.when(i == 0)
        def _():
            acc_ref[...] = jnp.zeros_like(acc_ref)

        dn = (_dot(dq_ref[...], w_ref[0:QKV_W, :]) + _dot(da_ref[...], w_ref[QKV_W:QKV_W + D, :])
              + _dot(dg_ref[...], w_ref[QKV_W + D:QKV_W + 2 * D, :]) + _dot(dt_ref[...], w_ref[QKV_W + 2 * D:IN_W, :]))
        _, vjp = jax.vjp(_rms, h_ref[...], g_ref[...])
        dh, dgain = vjp(dn)
        o_ref[...] = dh_ref[...] + dh
        acc_ref[0:1, :] += dgain

    def row(w):
        return pl.BlockSpec((tm, w), lambda i: (i, 0))

    return _call(
        body, name="in_bwd", grid=(p // tm,),
        in_specs=[row(QKV_W), row(D), row(D), row(2 * D), VM, row(D), row(D), VM],
        out_specs=[row(D), pl.BlockSpec((8, D), lambda i: (0, 0))],
        out_shape=[jax.ShapeDtypeStruct((p, D), F32), jax.ShapeDtypeStruct((8, D), F32)],
        sem="arbitrary", args=(dqkv, da, dg, dgt, w_int, h0p, dh1, gain), comm=comm)


def _sum_slots(slots, name):
    r = slots.shape[0] // N_DEV
    cols = slots.shape[1]
    tr = r if r <= 352 else (r // 2 if (r // 2) % 16 == 0 else r // 3)
    steps = r // tr

    def body(*refs):
        acc = refs[0][...].astype(F32)
        for s in range(1, N_DEV):
            acc = acc + refs[s][...].astype(F32)
        refs[N_DEV][...] = acc

    return pl.pallas_call(
        body, name=name, grid=(steps,),
        in_specs=[pl.BlockSpec((tr, cols), functools.partial(lambda i, s: (s * steps + i, 0), s=s)) for s in range(N_DEV)],
        out_specs=pl.BlockSpec((tr, cols), lambda i: (i, 0)),
        out_shape=jax.ShapeDtypeStruct((r, cols), F32),
        compiler_params=_cparams("parallel"),
    )(*([slots] * N_DEV))


def _adamw_math(w, g, m, v):
    m_n = ADAM_B1 * m + (1.0 - ADAM_B1) * g
    v_n = ADAM_B2 * v + (1.0 - ADAM_B2) * jnp.square(g)
    m_hat = m_n / (1.0 - ADAM_B1 ** ADAM_STEP)
    v_hat = v_n / (1.0 - ADAM_B2 ** ADAM_STEP)
    return -ADAM_LR * (m_hat / (jnp.sqrt(v_hat) + ADAM_EPS) + ADAM_WD * w), m_n, v_n


def _sum_adamw(slots, w, m, v, name):
    r, cols = w.shape
    tr = r if r <= 352 else (r // 2 if (r // 2) % 16 == 0 else r // 3)
    steps = r // tr

    def body(*refs):
        g = refs[0][...].astype(F32)
        for s in range(1, N_DEV):
            g = g + refs[s][...].astype(F32)
        w_ref, m_ref, v_ref, g_ref, d_ref, nm_ref, nv_ref = refs[N_DEV:]
        g_ref[...] = g
        d_ref[...], nm_ref[...], nv_ref[...] = _adamw_math(w_ref[...], g, m_ref[...], v_ref[...])

    spec = pl.BlockSpec((tr, cols), lambda i: (i, 0))
    return pl.pallas_call(
        body, name=name, grid=(steps,),
        in_specs=[pl.BlockSpec((tr, cols), functools.partial(lambda i, s: (s * steps + i, 0), s=s)) for s in range(N_DEV)]
        + [spec] * 3,
        out_specs=[spec] * 4, out_shape=[jax.ShapeDtypeStruct((r, cols), F32)] * 4,
        compiler_params=_cparams("parallel"),
    )(*([slots] * N_DEV), w, m, v)


def _adamw(w, g, m, v, name):
    r, cols = w.shape
    tr = 256 if r % 256 == 0 else r

    def body(w_ref, g_ref, m_ref, v_ref, d_ref, nm_ref, nv_ref):
        d_ref[...], nm_ref[...], nv_ref[...] = _adamw_math(w_ref[...], g_ref[...], m_ref[...], v_ref[...])

    spec = pl.BlockSpec((tr, cols), lambda i: (i, 0))
    return pl.pallas_call(
        body, name=name, grid=(r // tr,),
        in_specs=[spec] * 4, out_specs=[spec] * 3,
        out_shape=[jax.ShapeDtypeStruct((r, cols), F32)] * 3,
        compiler_params=_cparams("parallel"),
    )(w, g, m, v)


def _rope_tables(p):
    half = ROT_DIM // 2
    inv_freq = ROPE_THETA ** (-jnp.arange(half, dtype=F32) * 2.0 / ROT_DIM)
    pos = (jnp.arange(p) - PAD).astype(F32)
    ang = pos[:, None] * inv_freq[None, :]
    lane = jnp.arange(BLK)
    seg = (lane % HEAD_DIM) // half
    cos = jnp.cos(ang)[:, lane % half]
    sin = jnp.sin(ang)[:, lane % half]
    c = jnp.where(seg[None, :] < 2, cos, 1.0)
    s1 = jnp.where(seg[None, :] == 0, -sin, 0.0)
    s2 = jnp.where(seg[None, :] == 1, sin, 0.0)
    return jnp.concatenate([c, s1, s2], axis=1).astype(F32)


def _flat_pack(parts, rows):
    flat = jnp.concatenate([a.reshape(-1).astype(F32) for a in parts])
    return jnp.pad(flat, (0, rows * D - flat.shape[0])).reshape(rows, D)


def _flat_unpack(pack, shapes):
    flat = pack.reshape(-1)
    out, off = [], 0
    for s in shapes:
        size = 1
        for e in s:
            size *= e
        out.append(flat[off:off + size].reshape(s))
        off += size
    return out


def kernel(x, meta_tokens, norm_pre_mix, norm_post_mix, w_in, b_in, attn_sinks, w_attn_proj, conv_dw_w, conv_dw_b, conv_ln_g, conv_ln_b, w_conv_proj, b_conv_proj, w_out, norm_pre_ffn, norm_post_ffn, w_up, ffn_dw_w, ffn_dw_b, w_down, loss_target, m_meta_tokens, m_norm_pre_mix, m_norm_post_mix, m_w_in, m_b_in, m_attn_sinks, m_w_attn_proj, m_conv_dw_w, m_conv_dw_b, m_conv_ln_g, m_conv_ln_b, m_w_conv_proj, m_b_conv_proj, m_w_out, m_norm_pre_ffn, m_norm_post_ffn, m_w_up, m_ffn_dw_w, m_ffn_dw_b, m_w_down, v_meta_tokens, v_norm_pre_mix, v_norm_post_mix, v_w_in, v_b_in, v_attn_sinks, v_w_attn_proj, v_conv_dw_w, v_conv_dw_b, v_conv_ln_g, v_conv_ln_b, v_w_conv_proj, v_b_conv_proj, v_w_out, v_norm_pre_ffn, v_norm_post_ffn, v_w_up, v_ffn_dw_w, v_ffn_dw_b, v_w_down):
    seq = x.shape[1]
    p = seq + BLK
    me = 4 * lax.axis_index("x") + 2 * lax.axis_index("y") + lax.axis_index("c")
    in_cols = w_in.shape[2]
    up_cols = w_up.shape[2]

    small = jnp.zeros((56, up_cols), F32)
    small = small.at[0:N_META, 0:BLK].set(meta_tokens)
    small = small.at[16:16 + CONV_K, 0:BLK].set(conv_dw_w[0])
    small = small.at[48:48 + FFN_K, :].set(ffn_dw_w[0])
    w_int, small_all = _exchange(_Gather([w_in[0].T.astype(BF16), small]), "gather_w_in")
    small_all = small_all.reshape(N_DEV, 56, up_cols)
    meta_full = small_all[:, 0:N_META, 0:BLK].transpose(1, 0, 2).reshape(N_META, D)
    cdw = small_all[:, 16:16 + CONV_K, 0:BLK].transpose(1, 0, 2).reshape(CONV_K, D)
    cdw32 = jnp.pad(cdw, ((0, 32 - CONV_K), (0, 0)))
    fdw = small_all[:, 48:48 + FFN_K, :].transpose(1, 0, 2).reshape(FFN_K, 2 * FFN)

    tabs = _rope_tables(p)
    vecs = jnp.concatenate([conv_ln_g, conv_ln_b, b_conv_proj, norm_post_mix, norm_pre_ffn, jnp.zeros((3, D), F32)], axis=0)

    (h0p, n1, q, kv, ag, gates), (wa, wc, wo) = _in_proj(
        x[0], meta_full, norm_pre_mix, w_int, b_in, tabs,
        comm=_Gather([w_attn_proj[0].astype(BF16), w_conv_proj[0].astype(BF16), w_out[0].astype(BF16)]))
    (ao,), (w_upt,) = _attn_fwd(q, kv, attn_sinks, comm=_Gather([w_up[0].T.astype(BF16)]))
    (c0,), (wd,) = _conv31_fwd(ag, cdw32, conv_dw_b, comm=_Gather([w_down[0].astype(BF16)]))
    c1, attn, conv, merged, mix, h1, n2 = _mixer_fwd(ao, c0, gates, h0p, wa, wc, wo, vecs)
    u0 = _mm_nt(n2, w_upt, "ffn_up")
    act = _ffn_act(u0, fdw, ffn_dw_b)
    dffn, dact, dy, acc_f = _ffn_down_loss(act, wd, h1, loss_target[0], norm_post_ffn)

    g_wd = _mm_tn([act], dffn, "grad_w_down")
    (dug, duv, gfw_g, gfw_v, gfb_g, gfb_v), (s_wd,) = _ffn_act_bwd(u0, dact, fdw, ffn_dw_b, comm=_Scatter([g_wd]))
    g_wupt = _mm_tn([dug, duv], n2, "grad_w_up")
    dh1, acc_u = _ffn_in_bwd(dug, duv, w_upt, h1, dy, norm_pre_ffn)
    dmix, dat, dcv, dgt, dao, dc0, acc_m = _mixer_bwd(dh1, mix, attn, conv, gates, c0, wa, wc, wo, vecs)
    g_wo = _mm_tn([merged], dmix, "grad_w_out")
    g_wa = _mm_tn([ao], dat, "grad_w_attn_proj")
    g_wc = _mm_tn([c1], dcv, "grad_w_conv_proj")
    (da, dg, g_cdw, g_cdb), (s_wa, s_wc, s_wo) = _conv31_bwd(ag, dc0, cdw32, comm=_Scatter([g_wa, g_wc, g_wo]))
    (dqkv, dsink), (s_wup,) = _attn_bwd(q, kv, dao, attn_sinks, tabs, comm=_Scatter([g_wupt]))
    g_wint, g_bin = _mm_tn([dqkv, da, dg, dgt], n1, "grad_w_in", col_sums=True)
    (dh0, acc_i), (s_win,) = _in_bwd(dqkv, da, dg, dgt, w_int, h0p, dh1, norm_pre_mix, comm=_Scatter([g_wint]))

    big = []
    for nm, slots, w, m, v, tr in (("w_in", s_win, w_in, m_w_in, v_w_in, True), ("w_up", s_wup, w_up, m_w_up, v_w_up, True),
                                   ("w_attn_proj", s_wa, w_attn_proj, m_w_attn_proj, v_w_attn_proj, False),
                                   ("w_conv_proj", s_wc, w_conv_proj, m_w_conv_proj, v_w_conv_proj, False),
                                   ("w_out", s_wo, w_out, m_w_out, v_w_out, False),
                                   ("w_down", s_wd, w_down, m_w_down, v_w_down, False)):
        ins = [a[0].T if tr else a[0] for a in (w, m, v)]
        big.append(tuple((o.T if tr else o)[None] for o in _sum_adamw(slots, *ins, "update_" + nm)))

    loss_row = jnp.sum(acc_f[1:2, :], axis=1, keepdims=True)
    parts = [loss_row, dh0[PAD:BLK], acc_i[0:1], acc_m[0:1], g_bin, dsink[0:1, 0:16], g_cdw[0:CONV_K], g_cdb,
             acc_m[2:3], acc_m[3:4], acc_m[1:2], acc_u[0:1], acc_f[0:1],
             jnp.concatenate([gfw_g, gfw_v], axis=1), jnp.concatenate([gfb_g, gfb_v], axis=1)]
    shapes = [a.shape for a in parts]
    pack_rows = 88
    (gathered,) = _exchange(_Gather([_flat_pack(parts, pack_rows)]), "gather_small_grads")
    tot = _flat_unpack(_sum_slots(gathered, "sum_small_grads"), shapes)
    (loss, g_meta, g_npm, g_nqm, g_bi, g_sk, g_cw, g_cb, g_lg, g_lb, g_bc, g_npf, g_nqf, g_fw, g_fb) = tot
    loss = loss.reshape(())
    g_meta = lax.dynamic_slice_in_dim(g_meta, me * BLK, BLK, axis=1)
    g_cw = lax.dynamic_slice_in_dim(g_cw, me * BLK, BLK, axis=1)[None]
    g_fw = lax.dynamic_slice_in_dim(g_fw, me * up_cols, up_cols, axis=1)[None]

    sm_w = [meta_tokens, norm_pre_mix, norm_post_mix, b_in, attn_sinks, conv_dw_w, conv_dw_b, conv_ln_g, conv_ln_b,
            b_conv_proj, norm_pre_ffn, norm_post_ffn, ffn_dw_w, ffn_dw_b]
    sm_g = [g_meta, g_npm, g_nqm, g_bi, g_sk, g_cw, g_cb, g_lg, g_lb, g_bc, g_npf, g_nqf, g_fw, g_fb]
    sm_m = [m_meta_tokens, m_norm_pre_mix, m_norm_post_mix, m_b_in, m_attn_sinks, m_conv_dw_w, m_conv_dw_b, m_conv_ln_g,
            m_conv_ln_b, m_b_conv_proj, m_norm_pre_ffn, m_norm_post_ffn, m_ffn_dw_w, m_ffn_dw_b]
    sm_v = [v_meta_tokens, v_norm_pre_mix, v_norm_post_mix, v_b_in, v_attn_sinks, v_conv_dw_w, v_conv_dw_b, v_conv_ln_g,
            v_conv_ln_b, v_b_conv_proj, v_norm_pre_ffn, v_norm_post_ffn, v_ffn_dw_w, v_ffn_dw_b]
    sm_shapes = [a.shape for a in sm_w]
    upd_rows = 32
    v_pack = _flat_pack(sm_v, upd_rows)
    sm_out = _adamw(_flat_pack(sm_w, upd_rows), _flat_pack(sm_g, upd_rows), _flat_pack(sm_m, upd_rows), v_pack, "adamw_small")
    sm_d, sm_nm, sm_nv = (_flat_unpack(o, sm_shapes) for o in sm_out)

    order = ["meta_tokens", "norm_pre_mix", "norm_post_mix", "w_in", "b_in", "attn_sinks", "w_attn_proj", "conv_dw_w",
             "conv_dw_b", "conv_ln_g", "conv_ln_b", "w_conv_proj", "b_conv_proj", "w_out", "norm_pre_ffn", "norm_post_ffn",
             "w_up", "ffn_dw_w", "ffn_dw_b", "w_down"]
    small_names = ["meta_tokens", "norm_pre_mix", "norm_post_mix", "b_in", "attn_sinks", "conv_dw_w", "conv_dw_b", "conv_ln_g",
                   "conv_ln_b", "b_conv_proj", "norm_pre_ffn", "norm_post_ffn", "ffn_dw_w", "ffn_dw_b"]
    big_names = ["w_in", "w_up", "w_attn_proj", "w_conv_proj", "w_out", "w_down"]
    table = {}
    for k, nm in enumerate(small_names):
        table[nm] = (sm_g[k], sm_d[k], sm_nm[k], sm_nv[k])
    for k, nm in enumerate(big_names):
        table[nm] = big[k]
    grad_x = dh0[BLK:][None]
    outs = [loss, grad_x]
    for field in range(4):
        outs += [table[nm][field] for nm in order]
    return tuple(outs)
```

```python
import functools

import jax
import jax.numpy as jnp
from jax import lax
from jax.experimental import pallas as pl
from jax.experimental.pallas import tpu as pltpu

F32 = jnp.float32
BF16 = jnp.bfloat16
MESH = pl.DeviceIdType.MESH

D = 1024
HEAD_DIM = 64
N_META = 16
BLK = 128
PAD = BLK - N_META
CONV_K = 31
FFN = 2816
FFN_K = 3
QKV_W = 1280
IN_W = 5376
ROT_DIM = 16
ROPE_THETA = 500000.0
RMS_EPS = 1e-6
LN_EPS = 1e-5
NEG_INF = -1e30
SCALE = HEAD_DIM ** -0.5
N_DEV = 8

ADAM_LR = 0.001
ADAM_B1 = 0.9
ADAM_B2 = 0.999
ADAM_EPS = 1e-08
ADAM_WD = 0.01
ADAM_STEP = 10

VMEM_BYTES_V7X = 64 * 1024 * 1024
VMEM_LIMIT = VMEM_BYTES_V7X - 8 * 1024 * 1024

NT = (((1,), (1,)), ((), ()))
TN = (((0,), (0,)), ((), ()))
VM = pl.BlockSpec(memory_space=pltpu.VMEM)
ANY = pl.BlockSpec(memory_space=pl.ANY)


def _cparams(*sem):
    return pltpu.CompilerParams(dimension_semantics=sem or None, vmem_limit_bytes=VMEM_LIMIT)


def _row_tile(p):
    return 384 if p % 384 == 0 else 128


def _dot(a, b):
    return jnp.dot(a, b, preferred_element_type=F32)


def _dot_nt(a, b):
    return lax.dot_general(a, b, NT, preferred_element_type=F32)


def _dot_tn(a, b):
    return lax.dot_general(a, b, TN, preferred_element_type=F32)


def _rms(x, g):
    return x * lax.rsqrt(jnp.mean(x * x, axis=-1, keepdims=True) + RMS_EPS) * g


def _lnsilu(x, g, b):
    mu = jnp.mean(x, axis=-1, keepdims=True)
    var = jnp.mean(jnp.square(x - mu), axis=-1, keepdims=True)
    z = (x - mu) * lax.rsqrt(var + LN_EPS) * g + b
    return z * jax.nn.sigmoid(z)


def _rope(v, c, s1, s2):
    return v * c + pltpu.roll(v, BLK - 8, 1) * s1 + pltpu.roll(v, 8, 1) * s2


def _rows(i, tm):
    return i * tm + lax.broadcasted_iota(jnp.int32, (tm, 1), 0)


def _place():
    return lax.axis_index("x"), lax.axis_index("y"), lax.axis_index("c")


def _blk(ref, idx, r, dtype):
    return ref.at[pl.ds(pl.multiple_of(idx * r, 16 if dtype == BF16 else 8), r), :]


class _Gather:
    def __init__(self, arrs):
        self.ins = list(arrs)
        n = len(arrs)
        self.out_shape = [jax.ShapeDtypeStruct((N_DEV * a.shape[0], a.shape[1]), a.dtype) for a in arrs]
        self.scratch = [pltpu.SemaphoreType.DMA((n, 7)), pltpu.SemaphoreType.DMA((n, 7)), pltpu.SemaphoreType.DMA((n,))]

    def _parts(self, ins, outs, sems):
        send_sems, recv_sems, local_sems = sems
        n = len(ins)
        x, y, c = _place()
        me, sibling = (x, y, c), (x, y, 1 - c)
        chips = [(1 - x, y), (x, 1 - y), (1 - x, 1 - y)]

        def rows(a, p):
            return _blk(outs[a], 4 * p[0] + 2 * p[1] + p[2], self.ins[a].shape[0], self.ins[a].dtype)

        def copy(a, k, block, to, src=None):
            return pltpu.make_async_remote_copy(
                src_ref=rows(a, block) if src is None else src, dst_ref=rows(a, block),
                send_sem=send_sems.at[a, k], recv_sem=recv_sems.at[a, k], device_id=to, device_id_type=MESH)

        mine = [pltpu.make_async_copy(ins[a], rows(a, me), local_sems.at[a]) for a in range(n)]
        first = []
        for a in range(n):
            first.append(copy(a, 0, me, sibling, src=ins[a]))
            first += [copy(a, 1 + j, me, (*chip, c), src=ins[a]) for j, chip in enumerate(chips)]
        return n, c, me, sibling, chips, copy, mine, first

    def start(self, ins, outs, sems):
        *_, mine, first = self._parts(ins, outs, sems)
        for cp in mine + first:
            cp.start()

    def finish(self, ins, outs, sems):
        n, c, me, sibling, chips, copy, mine, first = self._parts(ins, outs, sems)
        passed = []
        for j, chip in enumerate(chips):
            for a in range(n):
                copy(a, 1 + j, (*chip, c), me).wait_recv()
                fwd = copy(a, 4 + j, (*chip, c), sibling)
                fwd.start()
                passed.append(fwd)
        for a in range(n):
            copy(a, 0, sibling, me).wait_recv()
            for j, chip in enumerate(chips):
                copy(a, 4 + j, (*chip, 1 - c), me).wait_recv()
        for cp in first + passed:
            cp.wait_send()
        for cp in mine:
            cp.wait()


FLIPS = [(0, 0, 1), (1, 0, 0), (0, 1, 0), (1, 1, 0), (1, 0, 1), (0, 1, 1), (1, 1, 1)]


class _Scatter:
    def __init__(self, arrs):
        self.ins = list(arrs)
        n = len(arrs)
        self.out_shape = [jax.ShapeDtypeStruct(a.shape, a.dtype) for a in arrs]
        self.scratch = [pltpu.SemaphoreType.DMA((n, 7)), pltpu.SemaphoreType.DMA((n, 7)), pltpu.SemaphoreType.DMA((n,))]

    def _parts(self, ins, outs, sems):
        send_sems, recv_sems, local_sems = sems
        n = len(ins)
        x, y, c = _place()
        me = 4 * x + 2 * y + c

        def flip(v, f):
            return 1 - v if f else v

        def blk(ref, a, idx):
            return _blk(ref, idx, self.ins[a].shape[0] // N_DEV, self.ins[a].dtype)

        mine = [pltpu.make_async_copy(blk(ins[a], a, me), blk(outs[a], a, me), local_sems.at[a]) for a in range(n)]
        sends, recvs = [], []
        for k, f in enumerate(FLIPS):
            peer = (flip(x, f[0]), flip(y, f[1]), flip(c, f[2]))
            pidx = 4 * peer[0] + 2 * peer[1] + peer[2]
            for a in range(n):
                sends.append(pltpu.make_async_remote_copy(
                    src_ref=blk(ins[a], a, pidx), dst_ref=blk(outs[a], a, me),
                    send_sem=send_sems.at[a, k], recv_sem=recv_sems.at[a, k], device_id=peer, device_id_type=MESH))
                recvs.append(functools.partial(
                    pltpu.make_async_remote_copy,
                    src_ref=blk(ins[a], a, pidx), dst_ref=blk(outs[a], a, pidx),
                    send_sem=send_sems.at[a, k], recv_sem=recv_sems.at[a, k], device_id=peer, device_id_type=MESH))
        return mine, sends, recvs

    def start(self, ins, outs, sems):
        mine, sends, _ = self._parts(ins, outs, sems)
        for cp in mine + sends:
            cp.start()

    def finish(self, ins, outs, sems):
        mine, sends, recvs = self._parts(ins, outs, sems)
        for make in recvs:
            make().wait_recv()
        for cp in sends:
            cp.wait_send()
        for cp in mine:
            cp.wait()


def _exchange(comm, name):
    n, m = len(comm.ins), len(comm.out_shape)

    def body(*refs):
        ins, outs, sems = refs[:n], refs[n:n + m], refs[n + m:]
        comm.start(ins, outs, sems)
        comm.finish(ins, outs, sems)

    return pl.pallas_call(
        body, name=name, out_shape=comm.out_shape, in_specs=[ANY] * n, out_specs=[ANY] * m, scratch_shapes=comm.scratch,
    )(*comm.ins)


def _call(body, *, name, grid, in_specs, out_specs, out_shape, args, scratch=(), sem="parallel", comm=None):
    if comm is None:
        outs = pl.pallas_call(
            body, name=name, grid=grid, in_specs=list(in_specs), out_specs=list(out_specs), out_shape=list(out_shape),
            scratch_shapes=list(scratch), compiler_params=_cparams(sem))(*args)
        return outs, []
    n_in, n_out, n_sc = len(in_specs), len(out_specs), len(scratch)
    n_ci, n_co = len(comm.ins), len(comm.out_shape)
    last = grid[0] - 1

    def fused(*refs):
        ins, refs = refs[:n_in], refs[n_in:]
        c_ins, refs = refs[:n_ci], refs[n_ci:]
        outs, refs = refs[:n_out], refs[n_out:]
        c_outs, refs = refs[:n_co], refs[n_co:]
        sc, c_sems = refs[:n_sc], refs[n_sc:]
        step = pl.program_id(0)

        @pl.when(step == 0)
        def _():
            comm.start(c_ins, c_outs, c_sems)

        body(*ins, *outs, *sc)

        @pl.when(step == last)
        def _():
            comm.finish(c_ins, c_outs, c_sems)

    outs = pl.pallas_call(
        fused, name=name, grid=grid, in_specs=list(in_specs) + [ANY] * n_ci, out_specs=list(out_specs) + [ANY] * n_co,
        out_shape=list(out_shape) + comm.out_shape, scratch_shapes=list(scratch) + comm.scratch,
        compiler_params=_cparams("arbitrary"))(*args, *comm.ins)
    return outs[:n_out], outs[n_out:]


def _token_specs(tm):
    k = tm // BLK
    return [pl.BlockSpec((BLK, D), functools.partial(lambda i, t: (jnp.maximum(k * i + t - 1, 0), 0), t=t)) for t in range(k)]


def _in_proj(x2d, meta, gain, w_int, b_in, tabs, comm=None):
    p = x2d.shape[0] + BLK
    tm = _row_tile(p)
    k = tm // BLK

    def body(*refs):
        x_refs = refs[:k]
        m_ref, g_ref, w_ref, b_ref, t_ref, h_ref, n1_ref, q_ref, kv_ref, ag_ref, gt_ref = refs[k:]
        i = pl.program_id(0)
        head = jnp.concatenate([jnp.zeros((PAD, D), F32), m_ref[...]], axis=0)
        first = jnp.where(i == 0, head, x_refs[0][...])
        h = jnp.concatenate([first] + [r[...] for r in x_refs[1:]], axis=0) if k > 1 else first
        h_ref[...] = h
        n = _rms(h, g_ref[...]).astype(BF16)
        n1_ref[...] = n
        c, s1, s2 = t_ref[:, 0:128], t_ref[:, 128:256], t_ref[:, 256:384]

        def mm(c0, w):
            return _dot_nt(n, w_ref[c0:c0 + w, :]) + b_ref[:, c0:c0 + w]

        for j in range(4):
            acc = mm(256 * j, 256)
            for t in range(2):
                lo = 256 * j + 128 * t
                q_ref[:, lo:lo + 128] = (_rope(acc[:, 128 * t:128 * (t + 1)], c, s1, s2) * SCALE).astype(BF16)
        acc = mm(1024, 256)
        kv_ref[:, 0:128] = _rope(acc[:, 0:128], c, s1, s2).astype(BF16)
        kv_ref[:, 128:256] = acc[:, 128:256].astype(BF16)
        for j in range(8):
            ag_ref[:, 256 * j:256 * (j + 1)] = mm(QKV_W + 256 * j, 256).astype(BF16)
        for j in range(8):
            gt_ref[:, 256 * j:256 * (j + 1)] = mm(QKV_W + 2048 + 256 * j, 256).astype(BF16)

    def row(w):
        return pl.BlockSpec((tm, w), lambda i: (i, 0))

    return _call(
        body, name="in_proj", grid=(p // tm,),
        in_specs=_token_specs(tm) + [VM, VM, VM, VM, row(384)],
        out_specs=[row(D), row(D), row(D), row(256), row(2048), row(2048)],
        out_shape=[jax.ShapeDtypeStruct((p, D), F32)] + [jax.ShapeDtypeStruct((p, w), BF16) for w in (D, D, 256, 2048, 2048)],
        args=(x2d,) * k + (meta, gain, w_int, b_in, tabs), comm=comm)


N_KEY = 2 * BLK + N_META


def _attn_setup(n, h, q_ref, km_ref, kp_ref, kc_ref):
    lo = lax.broadcasted_iota(jnp.int32, (BLK, BLK), 1) < HEAD_DIM
    lok = lax.broadcasted_iota(jnp.int32, (N_KEY, BLK), 1) < HEAD_DIM

    def dup(lanes):
        cat = jnp.concatenate([kp_ref[:, lanes], kc_ref[:, lanes], km_ref[PAD:BLK, lanes]], axis=0).astype(F32)
        rolled = pltpu.roll(cat, HEAD_DIM, 1)
        return (jnp.where(lok, cat, rolled) if h == 0 else jnp.where(lok, rolled, cat)).astype(BF16)

    k2 = dup(slice(0, 128))
    v2 = dup(slice(128, 256))
    qs = _stack_heads(q_ref, h, lo)

    kr = lax.broadcasted_iota(jnp.int32, (BLK, BLK), 0)
    tq = BLK * n + lax.broadcasted_iota(jnp.int32, (BLK, BLK), 1) - PAD
    t_p = BLK * (n - 1) + kr - PAD
    t_c = BLK * n + kr - PAD
    ok_p = jnp.logical_and(t_p >= N_META, tq - t_p < BLK)
    ok_c = jnp.logical_and(t_c >= N_META, t_c <= tq)
    ok_m = lax.broadcasted_iota(jnp.int32, (N_META, BLK), 0) <= BLK * n + lax.broadcasted_iota(jnp.int32, (N_META, BLK), 1) - PAD
    bias = jnp.concatenate([jnp.where(ok, 0.0, NEG_INF).astype(F32) for ok in (ok_p, ok_c, ok_m)], axis=0)
    return qs, k2, v2, bias, lok


def _attn_head(s, bias, sink):
    s = s + bias
    m = jnp.maximum(jnp.max(s, axis=0, keepdims=True), sink)
    e = jnp.exp(s - m)
    es = jnp.exp(sink - m)
    inv = 1.0 / (jnp.sum(e, axis=0, keepdims=True) + es)
    return e * inv, es * inv


def _stack_heads(ref, h, lo):
    pieces = []
    for jp in range(4):
        v = ref[:, BLK * (4 * h + jp):BLK * (4 * h + jp + 1)]
        zero = jnp.zeros_like(v)
        pieces += [jnp.where(lo, v, zero), jnp.where(lo, zero, v)]
    return jnp.concatenate(pieces, axis=0)


def _unstack_heads(v, jp, lo):
    return jnp.where(lo, v[256 * jp:256 * jp + 128], v[256 * jp + 128:256 * jp + 256])


def _attn_fwd(q, kv, sinks, comm=None):
    p = q.shape[0]
    nb = p // BLK

    def body(q_ref, km_ref, kp_ref, kc_ref, sink_ref, o_ref):
        n = pl.program_id(0)
        lo = lax.broadcasted_iota(jnp.int32, (BLK, BLK), 1) < HEAD_DIM
        for h in range(2):
            qs, k2, v2, bias, _ = _attn_setup(n, h, q_ref, km_ref, kp_ref, kc_ref)
            st = _dot_nt(k2, qs)
            pt = jnp.concatenate(
                [_attn_head(st[:, BLK * g:BLK * (g + 1)], bias, sink_ref[0, 8 * h + g])[0].astype(BF16) for g in range(8)],
                axis=1)
            o = _dot_tn(pt, v2)
            for jp in range(4):
                o_ref[:, BLK * (4 * h + jp):BLK * (4 * h + jp + 1)] = _unstack_heads(o, jp, lo).astype(BF16)

    return _call(
        body, name="attn_fwd", grid=(nb,),
        in_specs=[pl.BlockSpec((BLK, D), lambda i: (i, 0)),
                  pl.BlockSpec((BLK, 256), lambda i: (0, 0)),
                  pl.BlockSpec((BLK, 256), lambda i: (jnp.maximum(i - 1, 0), 0)),
                  pl.BlockSpec((BLK, 256), lambda i: (i, 0)),
                  pl.BlockSpec(memory_space=pltpu.SMEM)],
        out_specs=[pl.BlockSpec((BLK, D), lambda i: (i, 0))],
        out_shape=[jax.ShapeDtypeStruct((p, D), BF16)],
        args=(q, kv, kv, kv, sinks), comm=comm)


def _conv31_fwd(ag, w32, b, comm=None):
    p = ag.shape[0]
    nch = p // BLK

    def body(a_ref, g_ref, w_ref, b_ref, o_ref, gp):
        gp[0:32, :] = jnp.zeros((32, BLK), F32)
        for ci in range(nch):
            r0 = BLK * ci
            glu = a_ref[r0:r0 + BLK, :].astype(F32) * jax.nn.sigmoid(g_ref[r0:r0 + BLK, :].astype(F32))
            if ci == 0:
                glu = jnp.where(_rows(0, BLK) >= PAD, glu, 0.0)
            gp[32 + r0:32 + r0 + BLK, :] = glu
        for ci in range(nch):
            r0 = BLK * ci
            acc = jnp.broadcast_to(b_ref[...], (BLK, BLK))
            for j in range(CONV_K):
                acc = acc + w_ref[j:j + 1, :] * gp[r0 + j + 2:r0 + j + 2 + BLK, :]
            o_ref[r0:r0 + BLK, :] = acc

    return _call(
        body, name="conv31_fwd", grid=(D // BLK,),
        in_specs=[pl.BlockSpec((p, BLK), lambda j: (0, j)), pl.BlockSpec((p, BLK), lambda j: (0, 8 + j)),
                  pl.BlockSpec((32, BLK), lambda j: (0, j)), pl.BlockSpec((1, BLK), lambda j: (0, j))],
        out_specs=[pl.BlockSpec((p, BLK), lambda j: (0, j))],
        out_shape=[jax.ShapeDtypeStruct((p, D), F32)],
        scratch=[pltpu.VMEM((p + 32, BLK), F32)],
        args=(ag, ag, w32, b), comm=comm)


def _mixer_fwd(ao, c0, gates, h0p, wa, wc, wo, vecs):
    p = ao.shape[0]
    tm = _row_tile(p)

    def body(ao_ref, c0_ref, gt_ref, h_ref, wa_ref, wc_ref, wo_ref, v_ref,
             c1_ref, at_ref, cv_ref, mg_ref, mix_ref, h1_ref, n2_ref):
        i = pl.program_id(0)
        c1 = _lnsilu(c0_ref[...], v_ref[0:1, :], v_ref[1:2, :]).astype(BF16)
        c1_ref[...] = c1
        attn = _dot(ao_ref[...], wa_ref[...])
        conv = _dot(c1, wc_ref[...]) + v_ref[2:3, :]
        at_ref[...] = attn.astype(BF16)
        cv_ref[...] = conv.astype(BF16)
        merged = (jax.nn.sigmoid(gt_ref[:, 0:D].astype(F32)) * attn
                  + jax.nn.sigmoid(gt_ref[:, D:2 * D].astype(F32)) * conv).astype(BF16)
        mg_ref[...] = merged
        mix = _dot(merged, wo_ref[...])
        mix_ref[...] = mix
        h1 = jnp.where(_rows(i, tm) >= PAD, h_ref[...] + _rms(mix, v_ref[3:4, :]), 0.0)
        h1_ref[...] = h1
        n2_ref[...] = _rms(h1, v_ref[4:5, :]).astype(BF16)

    def row(w):
        return pl.BlockSpec((tm, w), lambda i: (i, 0))

    return pl.pallas_call(
        body, name="mixer_fwd", grid=(p // tm,),
        in_specs=[row(D), row(D), row(2 * D), row(D), VM, VM, VM, VM],
        out_specs=[row(D)] * 7,
        out_shape=[jax.ShapeDtypeStruct((p, D), t) for t in (BF16, BF16, BF16, BF16, F32, F32, BF16)],
        compiler_params=_cparams("parallel"),
    )(ao, c0, gates, h0p, wa, wc, wo, vecs)


def _mm_nt(a, w_t, name):
    p, k = a.shape
    n = w_t.shape[0]
    tm = _row_tile(p)
    ch = 512

    def body(a_ref, w_ref, o_ref):
        a_v = a_ref[...]
        for c0 in range(0, n, ch):
            o_ref[:, c0:c0 + ch] = _dot_nt(a_v, w_ref[c0:c0 + ch, :]).astype(BF16)

    return pl.pallas_call(
        body, name=name, grid=(p // tm,),
        in_specs=[pl.BlockSpec((tm, k), lambda i: (i, 0)), VM],
        out_specs=pl.BlockSpec((tm, n), lambda i: (i, 0)),
        out_shape=jax.ShapeDtypeStruct((p, n), BF16),
        compiler_params=_cparams("parallel"),
    )(a, w_t)


def _conv3(xp_ref, w_ref, r0):
    return (w_ref[0:1, :] * xp_ref[r0 + 6:r0 + 6 + BLK, :] + w_ref[1:2, :] * xp_ref[r0 + 7:r0 + 7 + BLK, :]
            + w_ref[2:3, :] * xp_ref[r0 + 8:r0 + 8 + BLK, :])


def _ffn_slab_specs(p):
    ncol = FFN // BLK
    return [pl.BlockSpec((p, BLK), lambda j: (0, j)), pl.BlockSpec((p, BLK), lambda j: (0, ncol + j)),
            pl.BlockSpec((FFN_K, BLK), lambda j: (0, j)), pl.BlockSpec((FFN_K, BLK), lambda j: (0, ncol + j)),
            pl.BlockSpec((1, BLK), lambda j: (0, j)), pl.BlockSpec((1, BLK), lambda j: (0, ncol + j))]


def _fill_shifted(dst, src_ref, nch):
    dst[0:8, :] = jnp.zeros((8, BLK), F32)
    for ci in range(nch):
        dst[8 + BLK * ci:8 + BLK * (ci + 1), :] = src_ref[BLK * ci:BLK * (ci + 1), :].astype(F32)


def _ffn_act(u0, fw, fb):
    p = u0.shape[0]
    nch = p // BLK

    def body(g_ref, v_ref, wg_ref, wv_ref, bg_ref, bv_ref, o_ref, xg, xv):
        _fill_shifted(xg, g_ref, nch)
        _fill_shifted(xv, v_ref, nch)
        for ci in range(nch):
            r0 = BLK * ci
            ug = _conv3(xg, wg_ref, r0) + bg_ref[...]
            uv = _conv3(xv, wv_ref, r0) + bv_ref[...]
            o_ref[r0:r0 + BLK, :] = (ug * jax.nn.sigmoid(ug) * uv).astype(BF16)

    return pl.pallas_call(
        body, name="ffn_act", grid=(FFN // BLK,),
        in_specs=_ffn_slab_specs(p),
        out_specs=pl.BlockSpec((p, BLK), lambda j: (0, j)),
        out_shape=jax.ShapeDtypeStruct((p, FFN), BF16),
        scratch_shapes=[pltpu.VMEM((p + 8, BLK), F32)] * 2,
        compiler_params=_cparams("parallel"),
    )(u0, u0, fw, fw, fb, fb)


def _ffn_down_loss(act, wd, h1, tgt, gain):
    p = act.shape[0]
    tm = _row_tile(p)
    k = tm // BLK

    def body(*refs):
        a_ref, w_ref, h_ref = refs[:3]
        t_refs = refs[3:3 + k]
        g_ref, df_ref, da_ref, dy_ref, acc_ref = refs[3 + k:]
        i = pl.program_id(0)

        @pl.when(i == 0)
        def _():
            acc_ref[...] = jnp.zeros_like(acc_ref)

        ffn = _dot(a_ref[...], w_ref[...])
        r, vjp = jax.vjp(_rms, ffn, g_ref[...])
        t = jnp.concatenate([t_ref[...] for t_ref in t_refs], axis=0) if k > 1 else t_refs[0][...]
        diff = jnp.where(_rows(i, tm) >= BLK, h_ref[...] + r - t, 0.0)
        dy = diff * (1.0 / D)
        dffn, dg = vjp(dy)
        acc_ref[0:1, :] += dg
        acc_ref[1:2, :] += jnp.sum(diff * diff, axis=0, keepdims=True) * (0.5 / D)
        dy_ref[...] = dy
        dfb = dffn.astype(BF16)
        df_ref[...] = dfb
        for c0 in range(0, FFN, 256):
            da_ref[:, c0:c0 + 256] = _dot_nt(dfb, w_ref[c0:c0 + 256, :]).astype(BF16)

    def row(w):
        return pl.BlockSpec((tm, w), lambda i: (i, 0))

    return pl.pallas_call(
        body, name="ffn_down_loss", grid=(p // tm,),
        in_specs=[row(FFN), VM, row(D)] + _token_specs(tm) + [VM],
        out_specs=[row(D), row(FFN), row(D), pl.BlockSpec((8, D), lambda i: (0, 0))],
        out_shape=[jax.ShapeDtypeStruct((p, D), BF16), jax.ShapeDtypeStruct((p, FFN), BF16),
                   jax.ShapeDtypeStruct((p, D), F32), jax.ShapeDtypeStruct((8, D), F32)],
        compiler_params=_cparams("arbitrary"),
    )(act, wd, h1, *([tgt] * k), gain)


def _mm_tn(pieces, b, name, col_sums=False):
    p, n = b.shape
    tk = 256
    nblk = [a.shape[1] // tk for a in pieces]
    offs = [sum(nblk[:q]) for q in range(len(pieces))]
    total = sum(nblk)
    npc = len(pieces)

    def body(*refs):
        a_refs, b_ref, o_ref = refs[:npc], refs[npc], refs[npc + 1]
        i = pl.program_id(0)
        for q, a_ref in enumerate(a_refs):
            @pl.when(jnp.logical_and(i >= offs[q], i < offs[q] + nblk[q]))
            def _(a_ref=a_ref):
                a_v = a_ref[...]
                o_ref[...] = _dot_tn(a_v, b_ref[...]).astype(BF16)
                if col_sums:
                    refs[npc + 2][...] = jnp.sum(a_v.astype(F32), axis=0, keepdims=True)

    def a_spec(q):
        return pl.BlockSpec((p, tk), lambda i: (0, jnp.clip(i - offs[q], 0, nblk[q] - 1)))

    out_specs = [pl.BlockSpec((tk, n), lambda i: (i, 0))]
    out_shape = [jax.ShapeDtypeStruct((total * tk, n), BF16)]
    if col_sums:
        out_specs.append(pl.BlockSpec((1, tk), lambda i: (0, i)))
        out_shape.append(jax.ShapeDtypeStruct((1, total * tk), F32))
    res = pl.pallas_call(
        body, name=name, grid=(total,),
        in_specs=[a_spec(q) for q in range(npc)] + [VM],
        out_specs=out_specs, out_shape=out_shape,
        compiler_params=_cparams("parallel"),
    )(*pieces, b)
    return res if col_sums else res[0]


def _ffn_act_bwd(u0, dact, fw, fb, comm=None):
    p = u0.shape[0]
    nch = p // BLK
    ncol = FFN // BLK

    def body(g_ref, v_ref, wg_ref, wv_ref, bg_ref, bv_ref, da_ref,
             dg_ref, dv_ref, gwg_ref, gwv_ref, gbg_ref, gbv_ref, xg, xv, eg, ev):
        _fill_shifted(xg, g_ref, nch)
        _fill_shifted(xv, v_ref, nch)
        eg[p:p + 8, :] = jnp.zeros((8, BLK), F32)
        ev[p:p + 8, :] = jnp.zeros((8, BLK), F32)
        for ci in range(nch):
            r0 = BLK * ci
            ug = _conv3(xg, wg_ref, r0) + bg_ref[...]
            uv = _conv3(xv, wv_ref, r0) + bv_ref[...]
            sg = jax.nn.sigmoid(ug)
            d = da_ref[r0:r0 + BLK, :].astype(F32)
            eg[r0:r0 + BLK, :] = d * uv * (sg * (1.0 + ug * (1.0 - sg)))
            ev[r0:r0 + BLK, :] = d * ug * sg
        for e_s, x_s, w_ref, d_ref, gw_ref, gb_ref in ((eg, xg, wg_ref, dg_ref, gwg_ref, gbg_ref),
                                                      (ev, xv, wv_ref, dv_ref, gwv_ref, gbv_ref)):
            sums = [jnp.zeros((BLK, BLK), F32) for _ in range(FFN_K + 1)]
            for ci in range(nch):
                r0 = BLK * ci
                e0 = e_s[r0:r0 + BLK, :]
                du = (w_ref[2:3, :] * e0 + w_ref[1:2, :] * e_s[r0 + 1:r0 + 1 + BLK, :]
                      + w_ref[0:1, :] * e_s[r0 + 2:r0 + 2 + BLK, :])
                if ci == 0:
                    du = jnp.where(_rows(0, BLK) >= PAD, du, 0.0)
                d_ref[r0:r0 + BLK, :] = du.astype(BF16)
                for j in range(FFN_K):
                    sums[j] = sums[j] + e0 * x_s[r0 + 6 + j:r0 + 6 + j + BLK, :]
                sums[FFN_K] = sums[FFN_K] + e0
            for j in range(FFN_K):
                gw_ref[j:j + 1, :] = jnp.sum(sums[j], axis=0, keepdims=True)
            gb_ref[...] = jnp.sum(sums[FFN_K], axis=0, keepdims=True)

    slab = pl.BlockSpec((p, BLK), lambda j: (0, j))
    wspec = pl.BlockSpec((FFN_K, BLK), lambda j: (0, j))
    bspec = pl.BlockSpec((1, BLK), lambda j: (0, j))
    return _call(
        body, name="ffn_act_bwd", grid=(ncol,),
        in_specs=_ffn_slab_specs(p) + [slab],
        out_specs=[slab, slab, wspec, wspec, bspec, bspec],
        out_shape=[jax.ShapeDtypeStruct((p, FFN), BF16)] * 2 + [jax.ShapeDtypeStruct((FFN_K, FFN), F32)] * 2
        + [jax.ShapeDtypeStruct((1, FFN), F32)] * 2,
        scratch=[pltpu.VMEM((p + 8, BLK), F32)] * 4,
        args=(u0, u0, fw, fw, fb, fb, dact), comm=comm)


def _ffn_in_bwd(dug, duv, w_upt, h1, dy, gain):
    p = h1.shape[0]
    tm = _row_tile(p)

    def body(dg_ref, dv_ref, w_ref, h_ref, dy_ref, g_ref, o_ref, acc_ref):
        i = pl.program_id(0)

        @pl.when(i == 0)
        def _():
            acc_ref[...] = jnp.zeros_like(acc_ref)

        dn = _dot(dg_ref[...], w_ref[0:FFN, :]) + _dot(dv_ref[...], w_ref[FFN:2 * FFN, :])
        _, vjp = jax.vjp(_rms, h_ref[...], g_ref[...])
        dh, dg = vjp(dn)
        o_ref[...] = dy_ref[...] + dh
        acc_ref[0:1, :] += dg

    def row(w):
        return pl.BlockSpec((tm, w), lambda i: (i, 0))

    return pl.pallas_call(
        body, name="ffn_in_bwd", grid=(p // tm,),
        in_specs=[row(FFN), row(FFN), VM, row(D), row(D), VM],
        out_specs=[row(D), pl.BlockSpec((8, D), lambda i: (0, 0))],
        out_shape=[jax.ShapeDtypeStruct((p, D), F32), jax.ShapeDtypeStruct((8, D), F32)],
        compiler_params=_cparams("arbitrary"),
    )(dug, duv, w_upt, h1, dy, gain)


def _mixer_bwd(dh1, mix, attn, conv, gates, c0, wa, wc, wo, vecs):
    p = dh1.shape[0]
    tm = _row_tile(p)

    def body(dh_ref, mix_ref, at_ref, cv_ref, gt_ref, c0_ref, wa_ref, wc_ref, wo_ref, v_ref,
             dmix_ref, dat_ref, dcv_ref, dgt_ref, dao_ref, dc0_ref, acc_ref):
        i = pl.program_id(0)

        @pl.when(i == 0)
        def _():
            acc_ref[...] = jnp.zeros_like(acc_ref)

        _, vjp = jax.vjp(_rms, mix_ref[...], v_ref[3:4, :])
        dmix, dgp = vjp(dh_ref[...])
        dmix = dmix.astype(BF16)
        dmix_ref[...] = dmix
        dmg = _dot_nt(dmix, wo_ref[...])
        sa = jax.nn.sigmoid(gt_ref[:, 0:D].astype(F32))
        sc = jax.nn.sigmoid(gt_ref[:, D:2 * D].astype(F32))
        dat = dmg * sa
        dcv = dmg * sc
        dgt_ref[:, 0:D] = (dmg * at_ref[...].astype(F32) * sa * (1.0 - sa)).astype(BF16)
        dgt_ref[:, D:2 * D] = (dmg * cv_ref[...].astype(F32) * sc * (1.0 - sc)).astype(BF16)
        datb = dat.astype(BF16)
        dcvb = dcv.astype(BF16)
        dat_ref[...] = datb
        dcv_ref[...] = dcvb
        dao_ref[...] = _dot_nt(datb, wa_ref[...]).astype(BF16)
        dc1 = _dot_nt(dcvb, wc_ref[...])
        _, vjp2 = jax.vjp(_lnsilu, c0_ref[...], v_ref[0:1, :], v_ref[1:2, :])
        dc0, dlg, dlb = vjp2(dc1)
        dc0_ref[...] = dc0
        acc_ref[0:1, :] += dgp
        acc_ref[1:2, :] += jnp.sum(dcv, axis=0, keepdims=True)
        acc_ref[2:3, :] += dlg
        acc_ref[3:4, :] += dlb

    def row(w):
        return pl.BlockSpec((tm, w), lambda i: (i, 0))

    return pl.pallas_call(
        body, name="mixer_bwd", grid=(p // tm,),
        in_specs=[row(D), row(D), row(D), row(D), row(2 * D), row(D), VM, VM, VM, VM],
        out_specs=[row(D), row(D), row(D), row(2 * D), row(D), row(D), pl.BlockSpec((8, D), lambda i: (0, 0))],
        out_shape=[jax.ShapeDtypeStruct((p, D), BF16)] * 3 + [jax.ShapeDtypeStruct((p, 2 * D), BF16),
                                                             jax.ShapeDtypeStruct((p, D), BF16),
                                                             jax.ShapeDtypeStruct((p, D), F32),
                                                             jax.ShapeDtypeStruct((8, D), F32)],
        compiler_params=_cparams("arbitrary"),
    )(dh1, mix, attn, conv, gates, c0, wa, wc, wo, vecs)


def _conv31_bwd(ag, dc0, w32, comm=None):
    p = ag.shape[0]
    nch = p // BLK

    def body(a_ref, g_ref, dc_ref, w_ref, da_ref, dg_ref, gw_ref, gb_ref, gp, dp):
        gp[0:32, :] = jnp.zeros((32, BLK), F32)
        dp[p:p + 32, :] = jnp.zeros((32, BLK), F32)
        bsum = jnp.zeros((BLK, BLK), F32)
        for ci in range(nch):
            r0 = BLK * ci
            glu = a_ref[r0:r0 + BLK, :].astype(F32) * jax.nn.sigmoid(g_ref[r0:r0 + BLK, :].astype(F32))
            if ci == 0:
                glu = jnp.where(_rows(0, BLK) >= PAD, glu, 0.0)
            gp[32 + r0:32 + r0 + BLK, :] = glu
            d = dc_ref[r0:r0 + BLK, :]
            dp[r0:r0 + BLK, :] = d
            bsum = bsum + d
        gb_ref[...] = jnp.sum(bsum, axis=0, keepdims=True)
        for ci in range(nch):
            r0 = BLK * ci
            acc = jnp.zeros((BLK, BLK), F32)
            for j in range(CONV_K):
                acc = acc + w_ref[j:j + 1, :] * dp[r0 + 30 - j:r0 + 30 - j + BLK, :]
            if ci == 0:
                acc = jnp.where(_rows(0, BLK) >= PAD, acc, 0.0)
            a = a_ref[r0:r0 + BLK, :].astype(F32)
            sg = jax.nn.sigmoid(g_ref[r0:r0 + BLK, :].astype(F32))
            da_ref[r0:r0 + BLK, :] = (acc * sg).astype(BF16)
            dg_ref[r0:r0 + BLK, :] = (acc * a * sg * (1.0 - sg)).astype(BF16)
        for j in range(CONV_K):
            acc = jnp.zeros((BLK, BLK), F32)
            for ci in range(nch):
                r0 = BLK * ci
                acc = acc + dp[r0:r0 + BLK, :] * gp[r0 + j + 2:r0 + j + 2 + BLK, :]
            gw_ref[j:j + 1, :] = jnp.sum(acc, axis=0, keepdims=True)
        gw_ref[CONV_K:32, :] = jnp.zeros((32 - CONV_K, BLK), F32)

    slab = pl.BlockSpec((p, BLK), lambda j: (0, j))
    return _call(
        body, name="conv31_bwd", grid=(D // BLK,),
        in_specs=[slab, pl.BlockSpec((p, BLK), lambda j: (0, 8 + j)), slab, pl.BlockSpec((32, BLK), lambda j: (0, j))],
        out_specs=[slab, slab, pl.BlockSpec((32, BLK), lambda j: (0, j)), pl.BlockSpec((1, BLK), lambda j: (0, j))],
        out_shape=[jax.ShapeDtypeStruct((p, D), BF16)] * 2 + [jax.ShapeDtypeStruct((32, D), F32),
                                                             jax.ShapeDtypeStruct((1, D), F32)],
        scratch=[pltpu.VMEM((p + 32, BLK), F32)] * 2,
        args=(ag, ag, dc0, w32), comm=comm)


def _attn_bwd(q, kv, dao, sinks, tabs, comm=None):
    p = q.shape[0]
    nb = p // BLK

    def body(q_ref, km_ref, kp_ref, kc_ref, do_ref, sink_ref, t_ref, dqkv_ref, dsink_ref, carry, macc):
        i = pl.program_id(0)
        n = nb - 1 - i

        @pl.when(i == 0)
        def _():
            carry[...] = jnp.zeros_like(carry)
            macc[...] = jnp.zeros_like(macc)
            dsink_ref[...] = jnp.zeros_like(dsink_ref)

        lo = lax.broadcasted_iota(jnp.int32, (BLK, BLK), 1) < HEAD_DIM
        lane8 = lax.broadcasted_iota(jnp.int32, (8, BLK), 1)
        c, s1, s2 = t_ref[:, 0:128], -t_ref[:, 128:256], -t_ref[:, 256:384]
        dk = jnp.zeros((N_KEY, BLK), F32)
        dv = jnp.zeros((N_KEY, BLK), F32)
        for h in range(2):
            qs, k2, v2, bias, lok = _attn_setup(n, h, q_ref, km_ref, kp_ref, kc_ref)
            dos = _stack_heads(do_ref, h, lo)
            st = _dot_nt(k2, qs)
            dpt = _dot_nt(v2, dos)
            p_parts, ds_parts = [], []
            for g in range(8):
                cols = slice(BLK * g, BLK * (g + 1))
                pn, ps = _attn_head(st[:, cols], bias, sink_ref[0, 8 * h + g])
                dp = dpt[:, cols]
                delta = jnp.sum(pn * dp, axis=0, keepdims=True)
                ds_parts.append((pn * (dp - delta)).astype(BF16))
                p_parts.append(pn.astype(BF16))
                dsk = -jnp.sum(ps * delta, axis=1, keepdims=True)
                dsink_ref[...] += jnp.where(lane8 == 8 * h + g, dsk, 0.0)
            dst = jnp.concatenate(ds_parts, axis=1)
            pt = jnp.concatenate(p_parts, axis=1)
            dq = _dot_tn(dst, k2)
            for jp in range(4):
                lo_c = BLK * (4 * h + jp)
                dqkv_ref[:, lo_c:lo_c + BLK] = (_rope(_unstack_heads(dq, jp, lo), c, s1, s2) * SCALE).astype(BF16)
            dk2 = _dot(dst, qs)
            dv2 = _dot(pt, dos)
            dk2 = dk2 + pltpu.roll(dk2, HEAD_DIM, 1)
            dv2 = dv2 + pltpu.roll(dv2, HEAD_DIM, 1)
            own = lok if h == 0 else jnp.logical_not(lok)
            dk = jnp.where(own, dk2, dk)
            dv = jnp.where(own, dv2, dv)
        macc[:, 0:BLK] += dk[2 * BLK:N_KEY]
        macc[:, BLK:2 * BLK] += dv[2 * BLK:N_KEY]
        last = (n == 0).astype(F32)
        zpad = jnp.zeros((PAD, BLK), F32)
        dk_c = dk[BLK:2 * BLK] + carry[:, 0:BLK] + last * jnp.concatenate([zpad, macc[:, 0:BLK]], axis=0)
        dv_c = dv[BLK:2 * BLK] + carry[:, BLK:2 * BLK] + last * jnp.concatenate([zpad, macc[:, BLK:2 * BLK]], axis=0)
        carry[:, 0:BLK] = dk[0:BLK]
        carry[:, BLK:2 * BLK] = dv[0:BLK]
        dqkv_ref[:, D:D + BLK] = _rope(dk_c, c, s1, s2).astype(BF16)
        dqkv_ref[:, D + BLK:D + 2 * BLK] = dv_c.astype(BF16)

    def rev(w):
        return pl.BlockSpec((BLK, w), lambda i: (nb - 1 - i, 0))

    return _call(
        body, name="attn_bwd", grid=(nb,),
        in_specs=[rev(D),
                  pl.BlockSpec((BLK, 256), lambda i: (0, 0)),
                  pl.BlockSpec((BLK, 256), lambda i: (jnp.maximum(nb - 2 - i, 0), 0)),
                  rev(256), rev(D),
                  pl.BlockSpec(memory_space=pltpu.SMEM), rev(384)],
        out_specs=[rev(QKV_W), pl.BlockSpec((8, BLK), lambda i: (0, 0))],
        out_shape=[jax.ShapeDtypeStruct((p, QKV_W), BF16), jax.ShapeDtypeStruct((8, BLK), F32)],
        scratch=[pltpu.VMEM((BLK, 256), F32), pltpu.VMEM((N_META, 256), F32)], sem="arbitrary",
        args=(q, kv, kv, kv, dao, sinks, tabs), comm=comm)


def _in_bwd(dqkv, da, dg, dgt, w_int, h0p, dh1, gain, comm=None):
    p = h0p.shape[0]
    tm = _row_tile(p)

    def body(dq_ref, da_ref, dg_ref, dt_ref, w_ref, h_ref, dh_ref, g_ref, o_ref, acc_ref):
        i = pl.program_id(0)

        @pl.when(i == 0)
        def _():
            acc_ref[...] = jnp.zeros_like(acc_ref)

        dn = (_dot(dq_ref[...], w_ref[0:QKV_W, :]) + _dot(da_ref[...], w_ref[QKV_W:QKV_W + D, :])
              + _dot(dg_ref[...], w_ref[QKV_W + D:QKV_W + 2 * D, :]) + _dot(dt_ref[...], w_ref[QKV_W + 2 * D:IN_W, :]))
        _, vjp = jax.vjp(_rms, h_ref[...], g_ref[...])
        dh, dgain = vjp(dn)
        o_ref[...] = dh_ref[...] + dh
        acc_ref[0:1, :] += dgain

    def row(w):
        return pl.BlockSpec((tm, w), lambda i: (i, 0))

    return _call(
        body, name="in_bwd", grid=(p // tm,),
        in_specs=[row(QKV_W), row(D), row(D), row(2 * D), VM, row(D), row(D), VM],
        out_specs=[row(D), pl.BlockSpec((8, D), lambda i: (0, 0))],
        out_shape=[jax.ShapeDtypeStruct((p, D), F32), jax.ShapeDtypeStruct((8, D), F32)],
        sem="arbitrary", args=(dqkv, da, dg, dgt, w_int, h0p, dh1, gain), comm=comm)


def _sum_slots(slots, name):
    r = slots.shape[0] // N_DEV
    cols = slots.shape[1]
    tr = r if r <= 352 else (r // 2 if (r // 2) % 16 == 0 else r // 3)
    steps = r // tr

    def body(*refs):
        acc = refs[0][...].astype(F32)
        for s in range(1, N_DEV):
            acc = acc + refs[s][...].astype(F32)
        refs[N_DEV][...] = acc

    return pl.pallas_call(
        body, name=name, grid=(steps,),
        in_specs=[pl.BlockSpec((tr, cols), functools.partial(lambda i, s: (s * steps + i, 0), s=s)) for s in range(N_DEV)],
        out_specs=pl.BlockSpec((tr, cols), lambda i: (i, 0)),
        out_shape=jax.ShapeDtypeStruct((r, cols), F32),
        compiler_params=_cparams("parallel"),
    )(*([slots] * N_DEV))


def _adamw_math(w, g, m, v):
    m_n = ADAM_B1 * m + (1.0 - ADAM_B1) * g
    v_n = ADAM_B2 * v + (1.0 - ADAM_B2) * jnp.square(g)
    m_hat = m_n / (1.0 - ADAM_B1 ** ADAM_STEP)
    v_hat = v_n / (1.0 - ADAM_B2 ** ADAM_STEP)
    return -ADAM_LR * (m_hat / (jnp.sqrt(v_hat) + ADAM_EPS) + ADAM_WD * w), m_n, v_n


def _sum_adamw(slots, w, m, v, name):
    r, cols = w.shape
    tr = r if r <= 352 else (r // 2 if (r // 2) % 16 == 0 else r // 3)
    steps = r // tr

    def body(*refs):
        g = refs[0][...].astype(F32)
        for s in range(1, N_DEV):
            g = g + refs[s][...].astype(F32)
        w_ref, m_ref, v_ref, g_ref, d_ref, nm_ref, nv_ref = refs[N_DEV:]
        g_ref[...] = g
        d_ref[...], nm_ref[...], nv_ref[...] = _adamw_math(w_ref[...], g, m_ref[...], v_ref[...])

    spec = pl.BlockSpec((tr, cols), lambda i: (i, 0))
    return pl.pallas_call(
        body, name=name, grid=(steps,),
        in_specs=[pl.BlockSpec((tr, cols), functools.partial(lambda i, s: (s * steps + i, 0), s=s)) for s in range(N_DEV)]
        + [spec] * 3,
        out_specs=[spec] * 4, out_shape=[jax.ShapeDtypeStruct((r, cols), F32)] * 4,
        compiler_params=_cparams("parallel"),
    )(*([slots] * N_DEV), w, m, v)


def _adamw(w, g, m, v, name):
    r, cols = w.shape
    tr = 256 if r % 256 == 0 else r

    def body(w_ref, g_ref, m_ref, v_ref, d_ref, nm_ref, nv_ref):
        d_ref[...], nm_ref[...], nv_ref[...] = _adamw_math(w_ref[...], g_ref[...], m_ref[...], v_ref[...])

    spec = pl.BlockSpec((tr, cols), lambda i: (i, 0))
    return pl.pallas_call(
        body, name=name, grid=(r // tr,),
        in_specs=[spec] * 4, out_specs=[spec] * 3,
        out_shape=[jax.ShapeDtypeStruct((r, cols), F32)] * 3,
        compiler_params=_cparams("parallel"),
    )(w, g, m, v)


def _rope_tables(p):
    half = ROT_DIM // 2
    inv_freq = ROPE_THETA ** (-jnp.arange(half, dtype=F32) * 2.0 / ROT_DIM)
    pos = (jnp.arange(p) - PAD).astype(F32)
    ang = pos[:, None] * inv_freq[None, :]
    lane = jnp.arange(BLK)
    seg = (lane % HEAD_DIM) // half
    cos = jnp.cos(ang)[:, lane % half]
    sin = jnp.sin(ang)[:, lane % half]
    c = jnp.where(seg[None, :] < 2, cos, 1.0)
    s1 = jnp.where(seg[None, :] == 0, -sin, 0.0)
    s2 = jnp.where(seg[None, :] == 1, sin, 0.0)
    return jnp.concatenate([c, s1, s2], axis=1).astype(F32)


def _flat_pack(parts, rows):
    flat = jnp.concatenate([a.reshape(-1).astype(F32) for a in parts])
    return jnp.pad(flat, (0, rows * D - flat.shape[0])).reshape(rows, D)


def _flat_unpack(pack, shapes):
    flat = pack.reshape(-1)
    out, off = [], 0
    for s in shapes:
        size = 1
        for e in s:
            size *= e
        out.append(flat[off:off + size].reshape(s))
        off += size
    return out


def kernel(x, meta_tokens, norm_pre_mix, norm_post_mix, w_in, b_in, attn_sinks, w_attn_proj, conv_dw_w, conv_dw_b, conv_ln_g, conv_ln_b, w_conv_proj, b_conv_proj, w_out, norm_pre_ffn, norm_post_ffn, w_up, ffn_dw_w, ffn_dw_b, w_down, loss_target, m_meta_tokens, m_norm_pre_mix, m_norm_post_mix, m_w_in, m_b_in, m_attn_sinks, m_w_attn_proj, m_conv_dw_w, m_conv_dw_b, m_conv_ln_g, m_conv_ln_b, m_w_conv_proj, m_b_conv_proj, m_w_out, m_norm_pre_ffn, m_norm_post_ffn, m_w_up, m_ffn_dw_w, m_ffn_dw_b, m_w_down, v_meta_tokens, v_norm_pre_mix, v_norm_post_mix, v_w_in, v_b_in, v_attn_sinks, v_w_attn_proj, v_conv_dw_w, v_conv_dw_b, v_conv_ln_g, v_conv_ln_b, v_w_conv_proj, v_b_conv_proj, v_w_out, v_norm_pre_ffn, v_norm_post_ffn, v_w_up, v_ffn_dw_w, v_ffn_dw_b, v_w_down):
    seq = x.shape[1]
    p = seq + BLK
    me = 4 * lax.axis_index("x") + 2 * lax.axis_index("y") + lax.axis_index("c")
    in_cols = w_in.shape[2]
    up_cols = w_up.shape[2]

    small = jnp.zeros((56, up_cols), F32)
    small = small.at[0:N_META, 0:BLK].set(meta_tokens)
    small = small.at[16:16 + CONV_K, 0:BLK].set(conv_dw_w[0])
    small = small.at[48:48 + FFN_K, :].set(ffn_dw_w[0])
    w_int, small_all = _exchange(_Gather([w_in[0].T.astype(BF16), small]), "gather_w_in")
    small_all = small_all.reshape(N_DEV, 56, up_cols)
    meta_full = small_all[:, 0:N_META, 0:BLK].transpose(1, 0, 2).reshape(N_META, D)
    cdw = small_all[:, 16:16 + CONV_K, 0:BLK].transpose(1, 0, 2).reshape(CONV_K, D)
    cdw32 = jnp.pad(cdw, ((0, 32 - CONV_K), (0, 0)))
    fdw = small_all[:, 48:48 + FFN_K, :].transpose(1, 0, 2).reshape(FFN_K, 2 * FFN)

    tabs = _rope_tables(p)
    vecs = jnp.concatenate([conv_ln_g, conv_ln_b, b_conv_proj, norm_post_mix, norm_pre_ffn, jnp.zeros((3, D), F32)], axis=0)

    (h0p, n1, q, kv, ag, gates), (wa, wc, wo) = _in_proj(
        x[0], meta_full, norm_pre_mix, w_int, b_in, tabs,
        comm=_Gather([w_attn_proj[0].astype(BF16), w_conv_proj[0].astype(BF16), w_out[0].astype(BF16)]))
    (ao,), (w_upt,) = _attn_fwd(q, kv, attn_sinks, comm=_Gather([w_up[0].T.astype(BF16)]))
    (c0,), (wd,) = _conv31_fwd(ag, cdw32, conv_dw_b, comm=_Gather([w_down[0].astype(BF16)]))
    c1, attn, conv, merged, mix, h1, n2 = _mixer_fwd(ao, c0, gates, h0p, wa, wc, wo, vecs)
    u0 = _mm_nt(n2, w_upt, "ffn_up")
    act = _ffn_act(u0, fdw, ffn_dw_b)
    dffn, dact, dy, acc_f = _ffn_down_loss(act, wd, h1, loss_target[0], norm_post_ffn)

    g_wd = _mm_tn([act], dffn, "grad_w_down")
    (dug, duv, gfw_g, gfw_v, gfb_g, gfb_v), (s_wd,) = _ffn_act_bwd(u0, dact, fdw, ffn_dw_b, comm=_Scatter([g_wd]))
    g_wupt = _mm_tn([dug, duv], n2, "grad_w_up")
    dh1, acc_u = _ffn_in_bwd(dug, duv, w_upt, h1, dy, norm_pre_ffn)
    dmix, dat, dcv, dgt, dao, dc0, acc_m = _mixer_bwd(dh1, mix, attn, conv, gates, c0, wa, wc, wo, vecs)
    g_wo = _mm_tn([merged], dmix, "grad_w_out")
    g_wa = _mm_tn([ao], dat, "grad_w_attn_proj")
    g_wc = _mm_tn([c1], dcv, "grad_w_conv_proj")
    (da, dg, g_cdw, g_cdb), (s_wa, s_wc, s_wo) = _conv31_bwd(ag, dc0, cdw32, comm=_Scatter([g_wa, g_wc, g_wo]))
    (dqkv, dsink), (s_wup,) = _attn_bwd(q, kv, dao, attn_sinks, tabs, comm=_Scatter([g_wupt]))
    g_wint, g_bin = _mm_tn([dqkv, da, dg, dgt], n1, "grad_w_in", col_sums=True)
    (dh0, acc_i), (s_win,) = _in_bwd(dqkv, da, dg, dgt, w_int, h0p, dh1, norm_pre_mix, comm=_Scatter([g_wint]))

    big = []
    for nm, slots, w, m, v, tr in (("w_in", s_win, w_in, m_w_in, v_w_in, True), ("w_up", s_wup, w_up, m_w_up, v_w_up, True),
                                   ("w_attn_proj", s_wa, w_attn_proj, m_w_attn_proj, v_w_attn_proj, False),
                                   ("w_conv_proj", s_wc, w_conv_proj, m_w_conv_proj, v_w_conv_proj, False),
                                   ("w_out", s_wo, w_out, m_w_out, v_w_out, False),
                                   ("w_down", s_wd, w_down, m_w_down, v_w_down, False)):
        ins = [a[0].T if tr else a[0] for a in (w, m, v)]
        big.append(tuple((o.T if tr else o)[None] for o in _sum_adamw(slots, *ins, "update_" + nm)))

    loss_row = jnp.sum(acc_f[1:2, :], axis=1, keepdims=True)
    parts = [loss_row, dh0[PAD:BLK], acc_i[0:1], acc_m[0:1], g_bin, dsink[0:1, 0:16], g_cdw[0:CONV_K], g_cdb,
             acc_m[2:3], acc_m[3:4], acc_m[1:2], acc_u[0:1], acc_f[0:1],
             jnp.concatenate([gfw_g, gfw_v], axis=1), jnp.concatenate([gfb_g, gfb_v], axis=1)]
    shapes = [a.shape for a in parts]
    pack_rows = 88
    (gathered,) = _exchange(_Gather([_flat_pack(parts, pack_rows)]), "gather_small_grads")
    tot = _flat_unpack(_sum_slots(gathered, "sum_small_grads"), shapes)
    (loss, g_meta, g_npm, g_nqm, g_bi, g_sk, g_cw, g_cb, g_lg, g_lb, g_bc, g_npf, g_nqf, g_fw, g_fb) = tot
    loss = loss.reshape(())
    g_meta = lax.dynamic_slice_in_dim(g_meta, me * BLK, BLK, axis=1)
    g_cw = lax.dynamic_slice_in_dim(g_cw, me * BLK, BLK, axis=1)[None]
    g_fw = lax.dynamic_slice_in_dim(g_fw, me * up_cols, up_cols, axis=1)[None]

    sm_w = [meta_tokens, norm_pre_mix, norm_post_mix, b_in, attn_sinks, conv_dw_w, conv_dw_b, conv_ln_g, conv_ln_b,
            b_conv_proj, norm_pre_ffn, norm_post_ffn, ffn_dw_w, ffn_dw_b]
    sm_g = [g_meta, g_npm, g_nqm, g_bi, g_sk, g_cw, g_cb, g_lg, g_lb, g_bc, g_npf, g_nqf, g_fw, g_fb]
    sm_m = [m_meta_tokens, m_norm_pre_mix, m_norm_post_mix, m_b_in, m_attn_sinks, m_conv_dw_w, m_conv_dw_b, m_conv_ln_g,
            m_conv_ln_b, m_b_conv_proj, m_norm_pre_ffn, m_norm_post_ffn, m_ffn_dw_w, m_ffn_dw_b]
    sm_v = [v_meta_tokens, v_norm_pre_mix, v_norm_post_mix, v_b_in, v_attn_sinks, v_conv_dw_w, v_conv_dw_b, v_conv_ln_g,
            v_conv_ln_b, v_b_conv_proj, v_norm_pre_ffn, v_norm_post_ffn, v_ffn_dw_w, v_ffn_dw_b]
    sm_shapes = [a.shape for a in sm_w]
    upd_rows = 32
    v_pack = _flat_pack(sm_v, upd_rows)
    sm_out = _adamw(_flat_pack(sm_w, upd_rows), _flat_pack(sm_g, upd_rows), _flat_pack(sm_m, upd_rows), v_pack, "adamw_small")
    sm_d, sm_nm, sm_nv = (_flat_unpack(o, sm_shapes) for o in sm_out)

    order = ["meta_tokens", "norm_pre_mix", "norm_post_mix", "w_in", "b_in", "attn_sinks", "w_attn_proj", "conv_dw_w",
             "conv_dw_b", "conv_ln_g", "conv_ln_b", "w_conv_proj", "b_conv_proj", "w_out", "norm_pre_ffn", "norm_post_ffn",
             "w_up", "ffn_dw_w", "ffn_dw_b", "w_down"]
    small_names = ["meta_tokens", "norm_pre_mix", "norm_post_mix", "b_in", "attn_sinks", "conv_dw_w", "conv_dw_b", "conv_ln_g",
                   "conv_ln_b", "b_conv_proj", "norm_pre_ffn", "norm_post_ffn", "ffn_dw_w", "ffn_dw_b"]
    big_names = ["w_in", "w_up", "w_attn_proj", "w_conv_proj", "w_out", "w_down"]
    table = {}
    for k, nm in enumerate(small_names):
        table[nm] = (sm_g[k], sm_d[k], sm_nm[k], sm_nv[k])
    for k, nm in enumerate(big_names):
        table[nm] = big[k]
    grad_x = dh0[BLK:][None]
    outs = [loss, grad_x]
    for field in range(4):
        outs += [table[nm][field] for nm in order]
    return tuple(outs)
```

```python
import functools

import jax
import jax.numpy as jnp
from jax import lax
from jax.experimental import pallas as pl
from jax.experimental.pallas import tpu as pltpu

F32 = jnp.float32
BF16 = jnp.bfloat16
MESH = pl.DeviceIdType.MESH

D = 1024
HEAD_DIM = 64
N_META = 16
BLK = 128
PAD = BLK - N_META
CONV_K = 31
FFN = 2816
FFN_K = 3
QKV_W = 1280
IN_W = 5376
ROT_DIM = 16
ROPE_THETA = 500000.0
RMS_EPS = 1e-6
LN_EPS = 1e-5
NEG_INF = -1e30
SCALE = HEAD_DIM ** -0.5
N_DEV = 8

ADAM_LR = 0.001
ADAM_B1 = 0.9
ADAM_B2 = 0.999
ADAM_EPS = 1e-08
ADAM_WD = 0.01
ADAM_STEP = 10

VMEM_BYTES_V7X = 64 * 1024 * 1024
VMEM_LIMIT = VMEM_BYTES_V7X - 8 * 1024 * 1024

NT = (((1,), (1,)), ((), ()))
TN = (((0,), (0,)), ((), ()))
VM = pl.BlockSpec(memory_space=pltpu.VMEM)
ANY = pl.BlockSpec(memory_space=pl.ANY)


def _cparams(*sem):
    return pltpu.CompilerParams(dimension_semantics=sem or None, vmem_limit_bytes=VMEM_LIMIT)


def _row_tile(p):
    return 384 if p % 384 == 0 else 128


def _dot(a, b):
    return jnp.dot(a, b, preferred_element_type=F32)


def _dot_nt(a, b):
    return lax.dot_general(a, b, NT, preferred_element_type=F32)


def _dot_tn(a, b):
    return lax.dot_general(a, b, TN, preferred_element_type=F32)


def _rms(x, g):
    return x * lax.rsqrt(jnp.mean(x * x, axis=-1, keepdims=True) + RMS_EPS) * g


def _lnsilu(x, g, b):
    mu = jnp.mean(x, axis=-1, keepdims=True)
    var = jnp.mean(jnp.square(x - mu), axis=-1, keepdims=True)
    z = (x - mu) * lax.rsqrt(var + LN_EPS) * g + b
    return z * jax.nn.sigmoid(z)


def _rope(v, c, s1, s2):
    return v * c + pltpu.roll(v, BLK - 8, 1) * s1 + pltpu.roll(v, 8, 1) * s2


def _rows(i, tm):
    return i * tm + lax.broadcasted_iota(jnp.int32, (tm, 1), 0)


def _place():
    return lax.axis_index("x"), lax.axis_index("y"), lax.axis_index("c")


def _blk(ref, idx, r, dtype):
    return ref.at[pl.ds(pl.multiple_of(idx * r, 16 if dtype == BF16 else 8), r), :]


class _Gather:
    def __init__(self, arrs):
        self.ins = list(arrs)
        n = len(arrs)
        self.out_shape = [jax.ShapeDtypeStruct((N_DEV * a.shape[0], a.shape[1]), a.dtype) for a in arrs]
        self.scratch = [pltpu.SemaphoreType.DMA((n, 7)), pltpu.SemaphoreType.DMA((n, 7)), pltpu.SemaphoreType.DMA((n,))]

    def _parts(self, ins, outs, sems):
        send_sems, recv_sems, local_sems = sems
        n = len(ins)
        x, y, c = _place()
        me, sibling = (x, y, c), (x, y, 1 - c)
        chips = [(1 - x, y), (x, 1 - y), (1 - x, 1 - y)]

        def rows(a, p):
            return _blk(outs[a], 4 * p[0] + 2 * p[1] + p[2], self.ins[a].shape[0], self.ins[a].dtype)

        def copy(a, k, block, to, src=None):
            return pltpu.make_async_remote_copy(
                src_ref=rows(a, block) if src is None else src, dst_ref=rows(a, block),
                send_sem=send_sems.at[a, k], recv_sem=recv_sems.at[a, k], device_id=to, device_id_type=MESH)

        mine = [pltpu.make_async_copy(ins[a], rows(a, me), local_sems.at[a]) for a in range(n)]
        first = []
        for a in range(n):
            first.append(copy(a, 0, me, sibling, src=ins[a]))
            first += [copy(a, 1 + j, me, (*chip, c), src=ins[a]) for j, chip in enumerate(chips)]
        return n, c, me, sibling, chips, copy, mine, first

    def start(self, ins, outs, sems):
        *_, mine, first = self._parts(ins, outs, sems)
        for cp in mine + first:
            cp.start()

    def finish(self, ins, outs, sems):
        n, c, me, sibling, chips, copy, mine, first = self._parts(ins, outs, sems)
        passed = []
        for j, chip in enumerate(chips):
            for a in range(n):
                copy(a, 1 + j, (*chip, c), me).wait_recv()
                fwd = copy(a, 4 + j, (*chip, c), sibling)
                fwd.start()
                passed.append(fwd)
        for a in range(n):
            copy(a, 0, sibling, me).wait_recv()
            for j, chip in enumerate(chips):
                copy(a, 4 + j, (*chip, 1 - c), me).wait_recv()
        for cp in first + passed:
            cp.wait_send()
        for cp in mine:
            cp.wait()


FLIPS = [(0, 0, 1), (1, 0, 0), (0, 1, 0), (1, 1, 0), (1, 0, 1), (0, 1, 1), (1, 1, 1)]


class _Scatter:
    def __init__(self, arrs, part=0, nparts=1):
        self.ins = list(arrs)
        self.part, self.nparts = part, nparts
        n = len(arrs)
        self.out_shape = [jax.ShapeDtypeStruct((a.shape[0] // nparts, a.shape[1]), a.dtype) for a in arrs]
        self.scratch = [pltpu.SemaphoreType.DMA((n, 7)), pltpu.SemaphoreType.DMA((n, 7)), pltpu.SemaphoreType.DMA((n,))]

    def _parts(self, ins, outs, sems):
        send_sems, recv_sems, local_sems = sems
        n = len(ins)
        x, y, c = _place()
        me = 4 * x + 2 * y + c

        def flip(v, f):
            return 1 - v if f else v

        def src(a, idx):
            r = self.ins[a].shape[0] // N_DEV
            rs = r // self.nparts
            return ins[a].at[pl.ds(pl.multiple_of(idx * r + self.part * rs, 16), rs), :]

        def dst(a, idx):
            rs = self.ins[a].shape[0] // N_DEV // self.nparts
            return outs[a].at[pl.ds(pl.multiple_of(idx * rs, 16), rs), :]

        mine = [pltpu.make_async_copy(src(a, me), dst(a, me), local_sems.at[a]) for a in range(n)]
        sends, recvs = [], []
        for k, f in enumerate(FLIPS):
            peer = (flip(x, f[0]), flip(y, f[1]), flip(c, f[2]))
            pidx = 4 * peer[0] + 2 * peer[1] + peer[2]
            for a in range(n):
                sends.append(pltpu.make_async_remote_copy(
                    src_ref=src(a, pidx), dst_ref=dst(a, me),
                    send_sem=send_sems.at[a, k], recv_sem=recv_sems.at[a, k], device_id=peer, device_id_type=MESH))
                recvs.append(functools.partial(
                    pltpu.make_async_remote_copy,
                    src_ref=src(a, pidx), dst_ref=dst(a, pidx),
                    send_sem=send_sems.at[a, k], recv_sem=recv_sems.at[a, k], device_id=peer, device_id_type=MESH))
        return mine, sends, recvs

    def start(self, ins, outs, sems):
        mine, sends, _ = self._parts(ins, outs, sems)
        for cp in mine + sends:
            cp.start()

    def finish(self, ins, outs, sems):
        mine, sends, recvs = self._parts(ins, outs, sems)
        for make in recvs:
            make().wait_recv()
        for cp in sends:
            cp.wait_send()
        for cp in mine:
            cp.wait()


N_CHIP = 4


class _SiblingSwap:
    def __init__(self, arr):
        self.ins = [arr]
        self.r = arr.shape[0] // N_DEV
        self.out_shape = [jax.ShapeDtypeStruct((N_CHIP * self.r, arr.shape[1]), arr.dtype)]
        self.scratch = [pltpu.SemaphoreType.DMA((N_CHIP,)), pltpu.SemaphoreType.DMA((N_CHIP,))]

    def _copies(self, ins, outs, sems):
        send_sems, recv_sems = sems
        x, y, c = _place()
        r = self.r
        return [pltpu.make_async_remote_copy(
            src_ref=ins[0].at[pl.ds(pl.multiple_of((2 * j + 1 - c) * r, 16), r), :],
            dst_ref=outs[0].at[pl.ds(j * r, r), :],
            send_sem=send_sems.at[j], recv_sem=recv_sems.at[j], device_id=(x, y, 1 - c), device_id_type=MESH)
            for j in range(N_CHIP)]

    def start(self, ins, outs, sems):
        for cp in self._copies(ins, outs, sems):
            cp.start()

    def finish(self, ins, outs, sems):
        for cp in self._copies(ins, outs, sems):
            cp.wait()


class _ChipScatter:
    def __init__(self, arr):
        self.ins = [arr]
        self.r = arr.shape[0] // N_CHIP
        self.out_shape = [jax.ShapeDtypeStruct(arr.shape, arr.dtype)]
        self.scratch = [pltpu.SemaphoreType.DMA((3,)), pltpu.SemaphoreType.DMA((3,)), pltpu.SemaphoreType.DMA]

    def _parts(self, ins, outs, sems):
        send_sems, recv_sems, local_sem = sems
        x, y, c = _place()
        r = self.r
        my_chip = 2 * x + y

        def rows(ref, j):
            return ref.at[pl.ds(pl.multiple_of(j * r, 16), r), :]

        mine = pltpu.make_async_copy(rows(ins[0], my_chip), rows(outs[0], my_chip), local_sem)
        sends, recvs = [], []
        for k, (fx, fy) in enumerate(((1, 0), (0, 1), (1, 1))):
            px, py = (1 - x if fx else x), (1 - y if fy else y)
            peer_chip = 2 * px + py
            sends.append(pltpu.make_async_remote_copy(
                src_ref=rows(ins[0], peer_chip), dst_ref=rows(outs[0], my_chip),
                send_sem=send_sems.at[k], recv_sem=recv_sems.at[k], device_id=(px, py, c), device_id_type=MESH))
            recvs.append(functools.partial(
                pltpu.make_async_remote_copy,
                src_ref=rows(ins[0], peer_chip), dst_ref=rows(outs[0], peer_chip),
                send_sem=send_sems.at[k], recv_sem=recv_sems.at[k], device_id=(px, py, c), device_id_type=MESH))
        return mine, sends, recvs

    def start(self, ins, outs, sems):
        mine, sends, _ = self._parts(ins, outs, sems)
        for cp in [mine] + sends:
            cp.start()

    def finish(self, ins, outs, sems):
        mine, sends, recvs = self._parts(ins, outs, sems)
        for make in recvs:
            make().wait_recv()
        for cp in sends:
            cp.wait_send()
        mine.wait()


def _pair_add(partial, recv):
    r = recv.shape[0] // N_CHIP
    cols = recv.shape[1]
    tr = r // 2 if (r // 2) % 16 == 0 else r
    steps = r // tr
    core = lax.axis_index("c").astype(jnp.int32).reshape(1)

    def body(c_ref, p_ref, s_ref, o_ref):
        o_ref[...] = (p_ref[...].astype(F32) + s_ref[...].astype(F32)).astype(BF16)

    spec = pl.BlockSpec((tr, cols), lambda j, i, c_ref: (j * steps + i, 0))
    return pl.pallas_call(
        body, name="pair_add",
        grid_spec=pltpu.PrefetchScalarGridSpec(
            num_scalar_prefetch=1, grid=(N_CHIP, steps),
            in_specs=[pl.BlockSpec((tr, cols), lambda j, i, c_ref: ((2 * j + c_ref[0]) * steps + i, 0)), spec],
            out_specs=spec),
        out_shape=jax.ShapeDtypeStruct(recv.shape, BF16),
        compiler_params=_cparams("parallel", "parallel"),
    )(core, partial, recv)


def _exchange(comm, name):
    n, m = len(comm.ins), len(comm.out_shape)

    def body(*refs):
        ins, outs, sems = refs[:n], refs[n:n + m], refs[n + m:]
        comm.start(ins, outs, sems)
        comm.finish(ins, outs, sems)

    return pl.pallas_call(
        body, name=name, out_shape=comm.out_shape, in_specs=[ANY] * n, out_specs=[ANY] * m, scratch_shapes=comm.scratch,
    )(*comm.ins)


def _call(body, *, name, grid, in_specs, out_specs, out_shape, args, scratch=(), sem="parallel", comm=None):
    if comm is None:
        outs = pl.pallas_call(
            body, name=name, grid=grid, in_specs=list(in_specs), out_specs=list(out_specs), out_shape=list(out_shape),
            scratch_shapes=list(scratch), compiler_params=_cparams(sem))(*args)
        return outs, []
    n_in, n_out, n_sc = len(in_specs), len(out_specs), len(scratch)
    n_ci, n_co = len(comm.ins), len(comm.out_shape)
    last = grid[0] - 1

    def fused(*refs):
        ins, refs = refs[:n_in], refs[n_in:]
        c_ins, refs = refs[:n_ci], refs[n_ci:]
        outs, refs = refs[:n_out], refs[n_out:]
        c_outs, refs = refs[:n_co], refs[n_co:]
        sc, c_sems = refs[:n_sc], refs[n_sc:]
        step = pl.program_id(0)

        @pl.when(step == 0)
        def _():
            comm.start(c_ins, c_outs, c_sems)

        body(*ins, *outs, *sc)

        @pl.when(step == last)
        def _():
            comm.finish(c_ins, c_outs, c_sems)

    outs = pl.pallas_call(
        fused, name=name, grid=grid, in_specs=list(in_specs) + [ANY] * n_ci, out_specs=list(out_specs) + [ANY] * n_co,
        out_shape=list(out_shape) + comm.out_shape, scratch_shapes=list(scratch) + comm.scratch,
        compiler_params=_cparams("arbitrary"))(*args, *comm.ins)
    return outs[:n_out], outs[n_out:]


def _token_specs(tm):
    k = tm // BLK
    return [pl.BlockSpec((BLK, D), functools.partial(lambda i, t: (jnp.maximum(k * i + t - 1, 0), 0), t=t)) for t in range(k)]


def _in_proj(x2d, meta, gain, w_int, b_in, tabs, comm=None):
    p = x2d.shape[0] + BLK
    tm = _row_tile(p)
    k = tm // BLK

    def body(*refs):
        x_refs = refs[:k]
        m_ref, g_ref, w_ref, b_ref, t_ref, h_ref, n1_ref, q_ref, kv_ref, ag_ref, gt_ref = refs[k:]
        i = pl.program_id(0)
        head = jnp.concatenate([jnp.zeros((PAD, D), F32), m_ref[...]], axis=0)
        first = jnp.where(i == 0, head, x_refs[0][...])
        h = jnp.concatenate([first] + [r[...] for r in x_refs[1:]], axis=0) if k > 1 else first
        h_ref[...] = h
        n = _rms(h, g_ref[...]).astype(BF16)
        n1_ref[...] = n
        c, s1, s2 = t_ref[:, 0:128], t_ref[:, 128:256], t_ref[:, 256:384]

        def mm(c0, w):
            return _dot_nt(n, w_ref[c0:c0 + w, :]) + b_ref[:, c0:c0 + w]

        for j in range(4):
            acc = mm(256 * j, 256)
            for t in range(2):
                lo = 256 * j + 128 * t
                q_ref[:, lo:lo + 128] = (_rope(acc[:, 128 * t:128 * (t + 1)], c, s1, s2) * SCALE).astype(BF16)
        acc = mm(1024, 256)
        kv_ref[:, 0:128] = _rope(acc[:, 0:128], c, s1, s2).astype(BF16)
        kv_ref[:, 128:256] = acc[:, 128:256].astype(BF16)
        for j in range(8):
            ag_ref[:, 256 * j:256 * (j + 1)] = mm(QKV_W + 256 * j, 256).astype(BF16)
        for j in range(8):
            gt_ref[:, 256 * j:256 * (j + 1)] = mm(QKV_W + 2048 + 256 * j, 256).astype(BF16)

    def row(w):
        return pl.BlockSpec((tm, w), lambda i: (i, 0))

    return _call(
        body, name="in_proj", grid=(p // tm,),
        in_specs=_token_specs(tm) + [VM, VM, VM, VM, row(384)],
        out_specs=[row(D), row(D), row(D), row(256), row(2048), row(2048)],
        out_shape=[jax.ShapeDtypeStruct((p, D), F32)] + [jax.ShapeDtypeStruct((p, w), BF16) for w in (D, D, 256, 2048, 2048)],
        args=(x2d,) * k + (meta, gain, w_int, b_in, tabs), comm=comm)


N_KEY = 2 * BLK + N_META


def _attn_setup(n, h, q_ref, km_ref, kp_ref, kc_ref):
    lo = lax.broadcasted_iota(jnp.int32, (BLK, BLK), 1) < HEAD_DIM
    lok = lax.broadcasted_iota(jnp.int32, (N_KEY, BLK), 1) < HEAD_DIM

    def dup(lanes):
        cat = jnp.concatenate([kp_ref[:, lanes], kc_ref[:, lanes], km_ref[PAD:BLK, lanes]], axis=0).astype(F32)
        rolled = pltpu.roll(cat, HEAD_DIM, 1)
        return (jnp.where(lok, cat, rolled) if h == 0 else jnp.where(lok, rolled, cat)).astype(BF16)

    k2 = dup(slice(0, 128))
    v2 = dup(slice(128, 256))
    qs = _stack_heads(q_ref, h, lo)

    kr = lax.broadcasted_iota(jnp.int32, (BLK, BLK), 0)
    tq = BLK * n + lax.broadcasted_iota(jnp.int32, (BLK, BLK), 1) - PAD
    t_p = BLK * (n - 1) + kr - PAD
    t_c = BLK * n + kr - PAD
    ok_p = jnp.logical_and(t_p >= N_META, tq - t_p < BLK)
    ok_c = jnp.logical_and(t_c >= N_META, t_c <= tq)
    ok_m = lax.broadcasted_iota(jnp.int32, (N_META, BLK), 0) <= BLK * n + lax.broadcasted_iota(jnp.int32, (N_META, BLK), 1) - PAD
    bias = jnp.concatenate([jnp.where(ok, 0.0, NEG_INF).astype(F32) for ok in (ok_p, ok_c, ok_m)], axis=0)
    return qs, k2, v2, bias, lok


def _attn_head(s, bias, sink):
    s = s + bias
    m = jnp.maximum(jnp.max(s, axis=0, keepdims=True), sink)
    e = jnp.exp(s - m)
    es = jnp.exp(sink - m)
    inv = 1.0 / (jnp.sum(e, axis=0, keepdims=True) + es)
    return e * inv, es * inv


def _stack_heads(ref, h, lo):
    pieces = []
    for jp in range(4):
        v = ref[:, BLK * (4 * h + jp):BLK * (4 * h + jp + 1)]
        zero = jnp.zeros_like(v)
        pieces += [jnp.where(lo, v, zero), jnp.where(lo, zero, v)]
    return jnp.concatenate(pieces, axis=0)


def _unstack_heads(v, jp, lo):
    return jnp.where(lo, v[256 * jp:256 * jp + 128], v[256 * jp + 128:256 * jp + 256])


def _attn_fwd(q, kv, sinks, comm=None):
    p = q.shape[0]
    nb = p // BLK

    def body(q_ref, km_ref, kp_ref, kc_ref, sink_ref, o_ref):
        n = pl.program_id(0)
        lo = lax.broadcasted_iota(jnp.int32, (BLK, BLK), 1) < HEAD_DIM
        for h in range(2):
            qs, k2, v2, bias, _ = _attn_setup(n, h, q_ref, km_ref, kp_ref, kc_ref)
            st = _dot_nt(k2, qs)
            pt = jnp.concatenate(
                [_attn_head(st[:, BLK * g:BLK * (g + 1)], bias, sink_ref[0, 8 * h + g])[0].astype(BF16) for g in range(8)],
                axis=1)
            o = _dot_tn(pt, v2)
            for jp in range(4):
                o_ref[:, BLK * (4 * h + jp):BLK * (4 * h + jp + 1)] = _unstack_heads(o, jp, lo).astype(BF16)

    return _call(
        body, name="attn_fwd", grid=(nb,),
        in_specs=[pl.BlockSpec((BLK, D), lambda i: (i, 0)),
                  pl.BlockSpec((BLK, 256), lambda i: (0, 0)),
                  pl.BlockSpec((BLK, 256), lambda i: (jnp.maximum(i - 1, 0), 0)),
                  pl.BlockSpec((BLK, 256), lambda i: (i, 0)),
                  pl.BlockSpec(memory_space=pltpu.SMEM)],
        out_specs=[pl.BlockSpec((BLK, D), lambda i: (i, 0))],
        out_shape=[jax.ShapeDtypeStruct((p, D), BF16)],
        args=(q, kv, kv, kv, sinks), comm=comm)


def _conv31_fwd(ag, w32, b, comm=None):
    p = ag.shape[0]
    nch = p // BLK

    def body(a_ref, g_ref, w_ref, b_ref, o_ref, gp):
        gp[0:32, :] = jnp.zeros((32, BLK), F32)
        for ci in range(nch):
            r0 = BLK * ci
            glu = a_ref[r0:r0 + BLK, :].astype(F32) * jax.nn.sigmoid(g_ref[r0:r0 + BLK, :].astype(F32))
            if ci == 0:
                glu = jnp.where(_rows(0, BLK) >= PAD, glu, 0.0)
            gp[32 + r0:32 + r0 + BLK, :] = glu
        for ci in range(nch):
            r0 = BLK * ci
            acc = jnp.broadcast_to(b_ref[...], (BLK, BLK))
            for j in range(CONV_K):
                acc = acc + w_ref[j:j + 1, :] * gp[r0 + j + 2:r0 + j + 2 + BLK, :]
            o_ref[r0:r0 + BLK, :] = acc

    return _call(
        body, name="conv31_fwd", grid=(D // BLK,),
        in_specs=[pl.BlockSpec((p, BLK), lambda j: (0, j)), pl.BlockSpec((p, BLK), lambda j: (0, 8 + j)),
                  pl.BlockSpec((32, BLK), lambda j: (0, j)), pl.BlockSpec((1, BLK), lambda j: (0, j))],
        out_specs=[pl.BlockSpec((p, BLK), lambda j: (0, j))],
        out_shape=[jax.ShapeDtypeStruct((p, D), F32)],
        scratch=[pltpu.VMEM((p + 32, BLK), F32)],
        args=(ag, ag, w32, b), comm=comm)


def _mixer_fwd(ao, c0, gates, h0p, wa, wc, wo, vecs):
    p = ao.shape[0]
    tm = _row_tile(p)

    def body(ao_ref, c0_ref, gt_ref, h_ref, wa_ref, wc_ref, wo_ref, v_ref,
             c1_ref, at_ref, cv_ref, mg_ref, mix_ref, h1_ref, n2_ref):
        i = pl.program_id(0)
        c1 = _lnsilu(c0_ref[...], v_ref[0:1, :], v_ref[1:2, :]).astype(BF16)
        c1_ref[...] = c1
        attn = _dot(ao_ref[...], wa_ref[...])
        conv = _dot(c1, wc_ref[...]) + v_ref[2:3, :]
        at_ref[...] = attn.astype(BF16)
        cv_ref[...] = conv.astype(BF16)
        merged = (jax.nn.sigmoid(gt_ref[:, 0:D].astype(F32)) * attn
                  + jax.nn.sigmoid(gt_ref[:, D:2 * D].astype(F32)) * conv).astype(BF16)
        mg_ref[...] = merged
        mix = _dot(merged, wo_ref[...])
        mix_ref[...] = mix
        h1 = jnp.where(_rows(i, tm) >= PAD, h_ref[...] + _rms(mix, v_ref[3:4, :]), 0.0)
        h1_ref[...] = h1
        n2_ref[...] = _rms(h1, v_ref[4:5, :]).astype(BF16)

    def row(w):
        return pl.BlockSpec((tm, w), lambda i: (i, 0))

    return pl.pallas_call(
        body, name="mixer_fwd", grid=(p // tm,),
        in_specs=[row(D), row(D), row(2 * D), row(D), VM, VM, VM, VM],
        out_specs=[row(D)] * 7,
        out_shape=[jax.ShapeDtypeStruct((p, D), t) for t in (BF16, BF16, BF16, BF16, F32, F32, BF16)],
        compiler_params=_cparams("parallel"),
    )(ao, c0, gates, h0p, wa, wc, wo, vecs)


def _mm_nt(a, w_t, name):
    p, k = a.shape
    n = w_t.shape[0]
    tm = _row_tile(p)
    ch = 512

    def body(a_ref, w_ref, o_ref):
        a_v = a_ref[...]
        for c0 in range(0, n, ch):
            o_ref[:, c0:c0 + ch] = _dot_nt(a_v, w_ref[c0:c0 + ch, :]).astype(BF16)

    return pl.pallas_call(
        body, name=name, grid=(p // tm,),
        in_specs=[pl.BlockSpec((tm, k), lambda i: (i, 0)), VM],
        out_specs=pl.BlockSpec((tm, n), lambda i: (i, 0)),
        out_shape=jax.ShapeDtypeStruct((p, n), BF16),
        compiler_params=_cparams("parallel"),
    )(a, w_t)


def _conv3(xp_ref, w_ref, r0):
    return (w_ref[0:1, :] * xp_ref[r0 + 6:r0 + 6 + BLK, :] + w_ref[1:2, :] * xp_ref[r0 + 7:r0 + 7 + BLK, :]
            + w_ref[2:3, :] * xp_ref[r0 + 8:r0 + 8 + BLK, :])


def _ffn_slab_specs(p):
    ncol = FFN // BLK
    return [pl.BlockSpec((p, BLK), lambda j: (0, j)), pl.BlockSpec((p, BLK), lambda j: (0, ncol + j)),
            pl.BlockSpec((FFN_K, BLK), lambda j: (0, j)), pl.BlockSpec((FFN_K, BLK), lambda j: (0, ncol + j)),
            pl.BlockSpec((1, BLK), lambda j: (0, j)), pl.BlockSpec((1, BLK), lambda j: (0, ncol + j))]


def _fill_shifted(dst, src_ref, nch):
    dst[0:8, :] = jnp.zeros((8, BLK), F32)
    for ci in range(nch):
        dst[8 + BLK * ci:8 + BLK * (ci + 1), :] = src_ref[BLK * ci:BLK * (ci + 1), :].astype(F32)


def _ffn_act(u0, fw, fb):
    p = u0.shape[0]
    nch = p // BLK

    def body(g_ref, v_ref, wg_ref, wv_ref, bg_ref, bv_ref, o_ref, xg, xv):
        _fill_shifted(xg, g_ref, nch)
        _fill_shifted(xv, v_ref, nch)
        for ci in range(nch):
            r0 = BLK * ci
            ug = _conv3(xg, wg_ref, r0) + bg_ref[...]
            uv = _conv3(xv, wv_ref, r0) + bv_ref[...]
            o_ref[r0:r0 + BLK, :] = (ug * jax.nn.sigmoid(ug) * uv).astype(BF16)

    return pl.pallas_call(
        body, name="ffn_act", grid=(FFN // BLK,),
        in_specs=_ffn_slab_specs(p),
        out_specs=pl.BlockSpec((p, BLK), lambda j: (0, j)),
        out_shape=jax.ShapeDtypeStruct((p, FFN), BF16),
        scratch_shapes=[pltpu.VMEM((p + 8, BLK), F32)] * 2,
        compiler_params=_cparams("parallel"),
    )(u0, u0, fw, fw, fb, fb)


def _ffn_down_loss(act, wd, h1, tgt, gain):
    p = act.shape[0]
    tm = _row_tile(p)
    k = tm // BLK

    def body(*refs):
        a_ref, w_ref, h_ref = refs[:3]
        t_refs = refs[3:3 + k]
        g_ref, df_ref, da_ref, dy_ref, acc_ref = refs[3 + k:]
        i = pl.program_id(0)

        @pl.when(i == 0)
        def _():
            acc_ref[...] = jnp.zeros_like(acc_ref)

        ffn = _dot(a_ref[...], w_ref[...])
        r, vjp = jax.vjp(_rms, ffn, g_ref[...])
        t = jnp.concatenate([t_ref[...] for t_ref in t_refs], axis=0) if k > 1 else t_refs[0][...]
        diff = jnp.where(_rows(i, tm) >= BLK, h_ref[...] + r - t, 0.0)
        dy = diff * (1.0 / D)
        dffn, dg = vjp(dy)
        acc_ref[0:1, :] += dg
        acc_ref[1:2, :] += jnp.sum(diff * diff, axis=0, keepdims=True) * (0.5 / D)
        dy_ref[...] = dy
        dfb = dffn.astype(BF16)
        df_ref[...] = dfb
        for c0 in range(0, FFN, 256):
            da_ref[:, c0:c0 + 256] = _dot_nt(dfb, w_ref[c0:c0 + 256, :]).astype(BF16)

    def row(w):
        return pl.BlockSpec((tm, w), lambda i: (i, 0))

    return pl.pallas_call(
        body, name="ffn_down_loss", grid=(p // tm,),
        in_specs=[row(FFN), VM, row(D)] + _token_specs(tm) + [VM],
        out_specs=[row(D), row(FFN), row(D), pl.BlockSpec((8, D), lambda i: (0, 0))],
        out_shape=[jax.ShapeDtypeStruct((p, D), BF16), jax.ShapeDtypeStruct((p, FFN), BF16),
                   jax.ShapeDtypeStruct((p, D), F32), jax.ShapeDtypeStruct((8, D), F32)],
        compiler_params=_cparams("arbitrary"),
    )(act, wd, h1, *([tgt] * k), gain)


def _mm_tn(pieces, b, name, col_sums=False):
    p, n = b.shape
    tk = 256
    nblk = [a.shape[1] // tk for a in pieces]
    offs = [sum(nblk[:q]) for q in range(len(pieces))]
    total = sum(nblk)
    npc = len(pieces)

    def body(*refs):
        a_refs, b_ref, o_ref = refs[:npc], refs[npc], refs[npc + 1]
        i = pl.program_id(0)
        for q, a_ref in enumerate(a_refs):
            @pl.when(jnp.logical_and(i >= offs[q], i < offs[q] + nblk[q]))
            def _(a_ref=a_ref):
                a_v = a_ref[...]
                o_ref[...] = _dot_tn(a_v, b_ref[...]).astype(BF16)
                if col_sums:
                    refs[npc + 2][...] = jnp.sum(a_v.astype(F32), axis=0, keepdims=True)

    def a_spec(q):
        return pl.BlockSpec((p, tk), lambda i: (0, jnp.clip(i - offs[q], 0, nblk[q] - 1)))

    out_specs = [pl.BlockSpec((tk, n), lambda i: (i, 0))]
    out_shape = [jax.ShapeDtypeStruct((total * tk, n), BF16)]
    if col_sums:
        out_specs.append(pl.BlockSpec((1, tk), lambda i: (0, i)))
        out_shape.append(jax.ShapeDtypeStruct((1, total * tk), F32))
    res = pl.pallas_call(
        body, name=name, grid=(total,),
        in_specs=[a_spec(q) for q in range(npc)] + [VM],
        out_specs=out_specs, out_shape=out_shape,
        compiler_params=_cparams("parallel"),
    )(*pieces, b)
    return res if col_sums else res[0]


def _ffn_act_bwd(u0, dact, fw, fb, comm=None):
    p = u0.shape[0]
    nch = p // BLK
    ncol = FFN // BLK

    def body(g_ref, v_ref, wg_ref, wv_ref, bg_ref, bv_ref, da_ref,
             dg_ref, dv_ref, gwg_ref, gwv_ref, gbg_ref, gbv_ref, xg, xv, eg, ev):
        _fill_shifted(xg, g_ref, nch)
        _fill_shifted(xv, v_ref, nch)
        eg[p:p + 8, :] = jnp.zeros((8, BLK), F32)
        ev[p:p + 8, :] = jnp.zeros((8, BLK), F32)
        for ci in range(nch):
            r0 = BLK * ci
            ug = _conv3(xg, wg_ref, r0) + bg_ref[...]
            uv = _conv3(xv, wv_ref, r0) + bv_ref[...]
            sg = jax.nn.sigmoid(ug)
            d = da_ref[r0:r0 + BLK, :].astype(F32)
            eg[r0:r0 + BLK, :] = d * uv * (sg * (1.0 + ug * (1.0 - sg)))
            ev[r0:r0 + BLK, :] = d * ug * sg
        for e_s, x_s, w_ref, d_ref, gw_ref, gb_ref in ((eg, xg, wg_ref, dg_ref, gwg_ref, gbg_ref),
                                                      (ev, xv, wv_ref, dv_ref, gwv_ref, gbv_ref)):
            sums = [jnp.zeros((BLK, BLK), F32) for _ in range(FFN_K + 1)]
            for ci in range(nch):
                r0 = BLK * ci
                e0 = e_s[r0:r0 + BLK, :]
                du = (w_ref[2:3, :] * e0 + w_ref[1:2, :] * e_s[r0 + 1:r0 + 1 + BLK, :]
                      + w_ref[0:1, :] * e_s[r0 + 2:r0 + 2 + BLK, :])
                if ci == 0:
                    du = jnp.where(_rows(0, BLK) >= PAD, du, 0.0)
                d_ref[r0:r0 + BLK, :] = du.astype(BF16)
                for j in range(FFN_K):
                    sums[j] = sums[j] + e0 * x_s[r0 + 6 + j:r0 + 6 + j + BLK, :]
                sums[FFN_K] = sums[FFN_K] + e0
            for j in range(FFN_K):
                gw_ref[j:j + 1, :] = jnp.sum(sums[j], axis=0, keepdims=True)
            gb_ref[...] = jnp.sum(sums[FFN_K], axis=0, keepdims=True)

    slab = pl.BlockSpec((p, BLK), lambda j: (0, j))
    wspec = pl.BlockSpec((FFN_K, BLK), lambda j: (0, j))
    bspec = pl.BlockSpec((1, BLK), lambda j: (0, j))
    return _call(
        body, name="ffn_act_bwd", grid=(ncol,),
        in_specs=_ffn_slab_specs(p) + [slab],
        out_specs=[slab, slab, wspec, wspec, bspec, bspec],
        out_shape=[jax.ShapeDtypeStruct((p, FFN), BF16)] * 2 + [jax.ShapeDtypeStruct((FFN_K, FFN), F32)] * 2
        + [jax.ShapeDtypeStruct((1, FFN), F32)] * 2,
        scratch=[pltpu.VMEM((p + 8, BLK), F32)] * 4,
        args=(u0, u0, fw, fw, fb, fb, dact), comm=comm)


def _ffn_in_bwd(dug, duv, w_upt, h1, dy, gain, comm=None):
    p = h1.shape[0]
    tm = _row_tile(p)

    def body(dg_ref, dv_ref, w_ref, h_ref, dy_ref, g_ref, o_ref, acc_ref):
        i = pl.program_id(0)

        @pl.when(i == 0)
        def _():
            acc_ref[...] = jnp.zeros_like(acc_ref)

        dn = _dot(dg_ref[...], w_ref[0:FFN, :]) + _dot(dv_ref[...], w_ref[FFN:2 * FFN, :])
        _, vjp = jax.vjp(_rms, h_ref[...], g_ref[...])
        dh, dg = vjp(dn)
        o_ref[...] = dy_ref[...] + dh
        acc_ref[0:1, :] += dg

    def row(w):
        return pl.BlockSpec((tm, w), lambda i: (i, 0))

    return _call(
        body, name="ffn_in_bwd", grid=(p // tm,),
        in_specs=[row(FFN), row(FFN), VM, row(D), row(D), VM],
        out_specs=[row(D), pl.BlockSpec((8, D), lambda i: (0, 0))],
        out_shape=[jax.ShapeDtypeStruct((p, D), F32), jax.ShapeDtypeStruct((8, D), F32)],
        sem="arbitrary", args=(dug, duv, w_upt, h1, dy, gain), comm=comm)


def _mixer_bwd(dh1, mix, attn, conv, gates, c0, wa, wc, wo, vecs, comm=None):
    p = dh1.shape[0]
    tm = _row_tile(p)

    def body(dh_ref, mix_ref, at_ref, cv_ref, gt_ref, c0_ref, wa_ref, wc_ref, wo_ref, v_ref,
             dmix_ref, dat_ref, dcv_ref, dgt_ref, dao_ref, dc0_ref, acc_ref):
        i = pl.program_id(0)

        @pl.when(i == 0)
        def _():
            acc_ref[...] = jnp.zeros_like(acc_ref)

        _, vjp = jax.vjp(_rms, mix_ref[...], v_ref[3:4, :])
        dmix, dgp = vjp(dh_ref[...])
        dmix = dmix.astype(BF16)
        dmix_ref[...] = dmix
        dmg = _dot_nt(dmix, wo_ref[...])
        sa = jax.nn.sigmoid(gt_ref[:, 0:D].astype(F32))
        sc = jax.nn.sigmoid(gt_ref[:, D:2 * D].astype(F32))
        dat = dmg * sa
        dcv = dmg * sc
        dgt_ref[:, 0:D] = (dmg * at_ref[...].astype(F32) * sa * (1.0 - sa)).astype(BF16)
        dgt_ref[:, D:2 * D] = (dmg * cv_ref[...].astype(F32) * sc * (1.0 - sc)).astype(BF16)
        datb = dat.astype(BF16)
        dcvb = dcv.astype(BF16)
        dat_ref[...] = datb
        dcv_ref[...] = dcvb
        dao_ref[...] = _dot_nt(datb, wa_ref[...]).astype(BF16)
        dc1 = _dot_nt(dcvb, wc_ref[...])
        _, vjp2 = jax.vjp(_lnsilu, c0_ref[...], v_ref[0:1, :], v_ref[1:2, :])
        dc0, dlg, dlb = vjp2(dc1)
        dc0_ref[...] = dc0
        acc_ref[0:1, :] += dgp
        acc_ref[1:2, :] += jnp.sum(dcv, axis=0, keepdims=True)
        acc_ref[2:3, :] += dlg
        acc_ref[3:4, :] += dlb

    def row(w):
        return pl.BlockSpec((tm, w), lambda i: (i, 0))

    return _call(
        body, name="mixer_bwd", grid=(p // tm,),
        in_specs=[row(D), row(D), row(D), row(D), row(2 * D), row(D), VM, VM, VM, VM],
        out_specs=[row(D), row(D), row(D), row(2 * D), row(D), row(D), pl.BlockSpec((8, D), lambda i: (0, 0))],
        out_shape=[jax.ShapeDtypeStruct((p, D), BF16)] * 3 + [jax.ShapeDtypeStruct((p, 2 * D), BF16),
                                                             jax.ShapeDtypeStruct((p, D), BF16),
                                                             jax.ShapeDtypeStruct((p, D), F32),
                                                             jax.ShapeDtypeStruct((8, D), F32)],
        sem="arbitrary", args=(dh1, mix, attn, conv, gates, c0, wa, wc, wo, vecs), comm=comm)


def _conv31_bwd(ag, dc0, w32, comm=None):
    p = ag.shape[0]
    nch = p // BLK

    def body(a_ref, g_ref, dc_ref, w_ref, da_ref, dg_ref, gw_ref, gb_ref, gp, dp):
        gp[0:32, :] = jnp.zeros((32, BLK), F32)
        dp[p:p + 32, :] = jnp.zeros((32, BLK), F32)
        bsum = jnp.zeros((BLK, BLK), F32)
        for ci in range(nch):
            r0 = BLK * ci
            glu = a_ref[r0:r0 + BLK, :].astype(F32) * jax.nn.sigmoid(g_ref[r0:r0 + BLK, :].astype(F32))
            if ci == 0:
                glu = jnp.where(_rows(0, BLK) >= PAD, glu, 0.0)
            gp[32 + r0:32 + r0 + BLK, :] = glu
            d = dc_ref[r0:r0 + BLK, :]
            dp[r0:r0 + BLK, :] = d
            bsum = bsum + d
        gb_ref[...] = jnp.sum(bsum, axis=0, keepdims=True)
        for ci in range(nch):
            r0 = BLK * ci
            acc = jnp.zeros((BLK, BLK), F32)
            for j in range(CONV_K):
                acc = acc + w_ref[j:j + 1, :] * dp[r0 + 30 - j:r0 + 30 - j + BLK, :]
            if ci == 0:
                acc = jnp.where(_rows(0, BLK) >= PAD, acc, 0.0)
            a = a_ref[r0:r0 + BLK, :].astype(F32)
            sg = jax.nn.sigmoid(g_ref[r0:r0 + BLK, :].astype(F32))
            da_ref[r0:r0 + BLK, :] = (acc * sg).astype(BF16)
            dg_ref[r0:r0 + BLK, :] = (acc * a * sg * (1.0 - sg)).astype(BF16)
        for j in range(CONV_K):
            acc = jnp.zeros((BLK, BLK), F32)
            for ci in range(nch):
                r0 = BLK * ci
                acc = acc + dp[r0:r0 + BLK, :] * gp[r0 + j + 2:r0 + j + 2 + BLK, :]
            gw_ref[j:j + 1, :] = jnp.sum(acc, axis=0, keepdims=True)
        gw_ref[CONV_K:32, :] = jnp.zeros((32 - CONV_K, BLK), F32)

    slab = pl.BlockSpec((p, BLK), lambda j: (0, j))
    return _call(
        body, name="conv31_bwd", grid=(D // BLK,),
        in_specs=[slab, pl.BlockSpec((p, BLK), lambda j: (0, 8 + j)), slab, pl.BlockSpec((32, BLK), lambda j: (0, j))],
        out_specs=[slab, slab, pl.BlockSpec((32, BLK), lambda j: (0, j)), pl.BlockSpec((1, BLK), lambda j: (0, j))],
        out_shape=[jax.ShapeDtypeStruct((p, D), BF16)] * 2 + [jax.ShapeDtypeStruct((32, D), F32),
                                                             jax.ShapeDtypeStruct((1, D), F32)],
        scratch=[pltpu.VMEM((p + 32, BLK), F32)] * 2,
        args=(ag, ag, dc0, w32), comm=comm)


def _attn_bwd(q, kv, dao, sinks, tabs, comm=None):
    p = q.shape[0]
    nb = p // BLK

    def body(q_ref, km_ref, kp_ref, kc_ref, do_ref, sink_ref, t_ref, dqkv_ref, dsink_ref, carry, macc):
        i = pl.program_id(0)
        n = nb - 1 - i

        @pl.when(i == 0)
        def _():
            carry[...] = jnp.zeros_like(carry)
            macc[...] = jnp.zeros_like(macc)
            dsink_ref[...] = jnp.zeros_like(dsink_ref)

        lo = lax.broadcasted_iota(jnp.int32, (BLK, BLK), 1) < HEAD_DIM
        lane8 = lax.broadcasted_iota(jnp.int32, (8, BLK), 1)
        c, s1, s2 = t_ref[:, 0:128], -t_ref[:, 128:256], -t_ref[:, 256:384]
        dk = jnp.zeros((N_KEY, BLK), F32)
        dv = jnp.zeros((N_KEY, BLK), F32)
        for h in range(2):
            qs, k2, v2, bias, lok = _attn_setup(n, h, q_ref, km_ref, kp_ref, kc_ref)
            dos = _stack_heads(do_ref, h, lo)
            st = _dot_nt(k2, qs)
            dpt = _dot_nt(v2, dos)
            p_parts, ds_parts = [], []
            for g in range(8):
                cols = slice(BLK * g, BLK * (g + 1))
                pn, ps = _attn_head(st[:, cols], bias, sink_ref[0, 8 * h + g])
                dp = dpt[:, cols]
                delta = jnp.sum(pn * dp, axis=0, keepdims=True)
                ds_parts.append((pn * (dp - delta)).astype(BF16))
                p_parts.append(pn.astype(BF16))
                dsk = -jnp.sum(ps * delta, axis=1, keepdims=True)
                dsink_ref[...] += jnp.where(lane8 == 8 * h + g, dsk, 0.0)
            dst = jnp.concatenate(ds_parts, axis=1)
            pt = jnp.concatenate(p_parts, axis=1)
            dq = _dot_tn(dst, k2)
            for jp in range(4):
                lo_c = BLK * (4 * h + jp)
                dqkv_ref[:, lo_c:lo_c + BLK] = (_rope(_unstack_heads(dq, jp, lo), c, s1, s2) * SCALE).astype(BF16)
            dk2 = _dot(dst, qs)
            dv2 = _dot(pt, dos)
            dk2 = dk2 + pltpu.roll(dk2, HEAD_DIM, 1)
            dv2 = dv2 + pltpu.roll(dv2, HEAD_DIM, 1)
            own = lok if h == 0 else jnp.logical_not(lok)
            dk = jnp.where(own, dk2, dk)
            dv = jnp.where(own, dv2, dv)
        macc[:, 0:BLK] += dk[2 * BLK:N_KEY]
        macc[:, BLK:2 * BLK] += dv[2 * BLK:N_KEY]
        last = (n == 0).astype(F32)
        zpad = jnp.zeros((PAD, BLK), F32)
        dk_c = dk[BLK:2 * BLK] + carry[:, 0:BLK] + last * jnp.concatenate([zpad, macc[:, 0:BLK]], axis=0)
        dv_c = dv[BLK:2 * BLK] + carry[:, BLK:2 * BLK] + last * jnp.concatenate([zpad, macc[:, BLK:2 * BLK]], axis=0)
        carry[:, 0:BLK] = dk[0:BLK]
        carry[:, BLK:2 * BLK] = dv[0:BLK]
        dqkv_ref[:, D:D + BLK] = _rope(dk_c, c, s1, s2).astype(BF16)
        dqkv_ref[:, D + BLK:D + 2 * BLK] = dv_c.astype(BF16)

    def rev(w):
        return pl.BlockSpec((BLK, w), lambda i: (nb - 1 - i, 0))

    return _call(
        body, name="attn_bwd", grid=(nb,),
        in_specs=[rev(D),
                  pl.BlockSpec((BLK, 256), lambda i: (0, 0)),
                  pl.BlockSpec((BLK, 256), lambda i: (jnp.maximum(nb - 2 - i, 0), 0)),
                  rev(256), rev(D),
                  pl.BlockSpec(memory_space=pltpu.SMEM), rev(384)],
        out_specs=[rev(QKV_W), pl.BlockSpec((8, BLK), lambda i: (0, 0))],
        out_shape=[jax.ShapeDtypeStruct((p, QKV_W), BF16), jax.ShapeDtypeStruct((8, BLK), F32)],
        scratch=[pltpu.VMEM((BLK, 256), F32), pltpu.VMEM((N_META, 256), F32)], sem="arbitrary",
        args=(q, kv, kv, kv, dao, sinks, tabs), comm=comm)


def _in_bwd(dqkv, da, dg, dgt, w_int, h0p, dh1, gain, comm=None):
    p = h0p.shape[0]
    tm = _row_tile(p)

    def body(dq_ref, da_ref, dg_ref, dt_ref, w_ref, h_ref, dh_ref, g_ref, o_ref, acc_ref):
        i = pl.program_id(0)

        @pl.when(i == 0)
        def _():
            acc_ref[...] = jnp.zeros_like(acc_ref)

        dn = (_dot(dq_ref[...], w_ref[0:QKV_W, :]) + _dot(da_ref[...], w_ref[QKV_W:QKV_W + D, :])
              + _dot(dg_ref[...], w_ref[QKV_W + D:QKV_W + 2 * D, :]) + _dot(dt_ref[...], w_ref[QKV_W + 2 * D:IN_W, :]))
        _, vjp = jax.vjp(_rms, h_ref[...], g_ref[...])
        dh, dgain = vjp(dn)
        o_ref[...] = dh_ref[...] + dh
        acc_ref[0:1, :] += dgain

    def row(w):
        return pl.BlockSpec((tm, w), lambda i: (i, 0))

    return _call(
        body, name="in_bwd", grid=(p // tm,),
        in_specs=[row(QKV_W), row(D), row(D), row(2 * D), VM, row(D), row(D), VM],
        out_specs=[row(D), pl.BlockSpec((8, D), lambda i: (0, 0))],
        out_shape=[jax.ShapeDtypeStruct((p, D), F32), jax.ShapeDtypeStruct((8, D), F32)],
        sem="arbitrary", args=(dqkv, da, dg, dgt, w_int, h0p, dh1, gain), comm=comm)


def _sum_slots(slots, name):
    r = slots.shape[0] // N_DEV
    cols = slots.shape[1]
    tr = r if r <= 352 else (r // 2 if (r // 2) % 16 == 0 else r // 3)
    steps = r // tr

    def body(*refs):
        acc = refs[0][...].astype(F32)
        for s in range(1, N_DEV):
            acc = acc + refs[s][...].astype(F32)
        refs[N_DEV][...] = acc

    return pl.pallas_call(
        body, name=name, grid=(steps,),
        in_specs=[pl.BlockSpec((tr, cols), functools.partial(lambda i, s: (s * steps + i, 0), s=s)) for s in range(N_DEV)],
        out_specs=pl.BlockSpec((tr, cols), lambda i: (i, 0)),
        out_shape=jax.ShapeDtypeStruct((r, cols), F32),
        compiler_params=_cparams("parallel"),
    )(*([slots] * N_DEV))


def _adamw_math(w, g, m, v):
    m_n = ADAM_B1 * m + (1.0 - ADAM_B1) * g
    v_n = ADAM_B2 * v + (1.0 - ADAM_B2) * jnp.square(g)
    m_hat = m_n / (1.0 - ADAM_B1 ** ADAM_STEP)
    v_hat = v_n / (1.0 - ADAM_B2 ** ADAM_STEP)
    return -ADAM_LR * (m_hat / (jnp.sqrt(v_hat) + ADAM_EPS) + ADAM_WD * w), m_n, v_n


def _sum_adamw(parts, w, m, v, name, nslots=N_DEV):
    r, cols = w.shape
    rs = r // len(parts)
    tr = rs if rs <= 352 else (rs // 2 if (rs // 2) % 16 == 0 else rs // 3)
    steps = rs // tr

    def body(*refs):
        w_ref, m_ref, v_ref, g_ref, d_ref, nm_ref, nv_ref = refs[nslots * len(parts):]
        i = pl.program_id(0)
        for q in range(len(parts)):
            @pl.when(i // steps == q)
            def _(q=q):
                g = refs[nslots * q][...].astype(F32)
                for s in range(1, nslots):
                    g = g + refs[nslots * q + s][...].astype(F32)
                g_ref[...] = g
                d_ref[...], nm_ref[...], nv_ref[...] = _adamw_math(w_ref[...], g, m_ref[...], v_ref[...])

    def slot_spec(q, s):
        return pl.BlockSpec((tr, cols), lambda i: (s * steps + jnp.clip(i - q * steps, 0, steps - 1), 0))

    spec = pl.BlockSpec((tr, cols), lambda i: (i, 0))
    return pl.pallas_call(
        body, name=name, grid=(steps * len(parts),),
        in_specs=[slot_spec(q, s) for q in range(len(parts)) for s in range(nslots)] + [spec] * 3,
        out_specs=[spec] * 4, out_shape=[jax.ShapeDtypeStruct((r, cols), F32)] * 4,
        compiler_params=_cparams("parallel"),
    )(*[a for a in parts for _ in range(nslots)], w, m, v)


def _adamw(w, g, m, v, name):
    r, cols = w.shape
    tr = 256 if r % 256 == 0 else r

    def body(w_ref, g_ref, m_ref, v_ref, d_ref, nm_ref, nv_ref):
        d_ref[...], nm_ref[...], nv_ref[...] = _adamw_math(w_ref[...], g_ref[...], m_ref[...], v_ref[...])

    spec = pl.BlockSpec((tr, cols), lambda i: (i, 0))
    return pl.pallas_call(
        body, name=name, grid=(r // tr,),
        in_specs=[spec] * 4, out_specs=[spec] * 3,
        out_shape=[jax.ShapeDtypeStruct((r, cols), F32)] * 3,
        compiler_params=_cparams("parallel"),
    )(w, g, m, v)


def _rope_tables(p):
    half = ROT_DIM // 2
    inv_freq = ROPE_THETA ** (-jnp.arange(half, dtype=F32) * 2.0 / ROT_DIM)
    pos = (jnp.arange(p) - PAD).astype(F32)
    ang = pos[:, None] * inv_freq[None, :]
    lane = jnp.arange(BLK)
    seg = (lane % HEAD_DIM) // half
    cos = jnp.cos(ang)[:, lane % half]
    sin = jnp.sin(ang)[:, lane % half]
    c = jnp.where(seg[None, :] < 2, cos, 1.0)
    s1 = jnp.where(seg[None, :] == 0, -sin, 0.0)
    s2 = jnp.where(seg[None, :] == 1, sin, 0.0)
    return jnp.concatenate([c, s1, s2], axis=1).astype(F32)


def _flat_pack(parts, rows):
    flat = jnp.concatenate([a.reshape(-1).astype(F32) for a in parts])
    return jnp.pad(flat, (0, rows * D - flat.shape[0])).reshape(rows, D)


def _flat_unpack(pack, shapes):
    flat = pack.reshape(-1)
    out, off = [], 0
    for s in shapes:
        size = 1
        for e in s:
            size *= e
        out.append(flat[off:off + size].reshape(s))
        off += size
    return out


def kernel(x, meta_tokens, norm_pre_mix, norm_post_mix, w_in, b_in, attn_sinks, w_attn_proj, conv_dw_w, conv_dw_b, conv_ln_g, conv_ln_b, w_conv_proj, b_conv_proj, w_out, norm_pre_ffn, norm_post_ffn, w_up, ffn_dw_w, ffn_dw_b, w_down, loss_target, m_meta_tokens, m_norm_pre_mix, m_norm_post_mix, m_w_in, m_b_in, m_attn_sinks, m_w_attn_proj, m_conv_dw_w, m_conv_dw_b, m_conv_ln_g, m_conv_ln_b, m_w_conv_proj, m_b_conv_proj, m_w_out, m_norm_pre_ffn, m_norm_post_ffn, m_w_up, m_ffn_dw_w, m_ffn_dw_b, m_w_down, v_meta_tokens, v_norm_pre_mix, v_norm_post_mix, v_w_in, v_b_in, v_attn_sinks, v_w_attn_proj, v_conv_dw_w, v_conv_dw_b, v_conv_ln_g, v_conv_ln_b, v_w_conv_proj, v_b_conv_proj, v_w_out, v_norm_pre_ffn, v_norm_post_ffn, v_w_up, v_ffn_dw_w, v_ffn_dw_b, v_w_down):
    seq = x.shape[1]
    p = seq + BLK
    me = 4 * lax.axis_index("x") + 2 * lax.axis_index("y") + lax.axis_index("c")
    in_cols = w_in.shape[2]
    up_cols = w_up.shape[2]

    small = jnp.zeros((56, up_cols), F32)
    small = small.at[0:N_META, 0:BLK].set(meta_tokens)
    small = small.at[16:16 + CONV_K, 0:BLK].set(conv_dw_w[0])
    small = small.at[48:48 + FFN_K, :].set(ffn_dw_w[0])
    w_int, small_all = _exchange(_Gather([w_in[0].T.astype(BF16), small]), "gather_w_in")
    small_all = small_all.reshape(N_DEV, 56, up_cols)
    meta_full = small_all[:, 0:N_META, 0:BLK].transpose(1, 0, 2).reshape(N_META, D)
    cdw = small_all[:, 16:16 + CONV_K, 0:BLK].transpose(1, 0, 2).reshape(CONV_K, D)
    cdw32 = jnp.pad(cdw, ((0, 32 - CONV_K), (0, 0)))
    fdw = small_all[:, 48:48 + FFN_K, :].transpose(1, 0, 2).reshape(FFN_K, 2 * FFN)

    tabs = _rope_tables(p)
    vecs = jnp.concatenate([conv_ln_g, conv_ln_b, b_conv_proj, norm_post_mix, norm_pre_ffn, jnp.zeros((3, D), F32)], axis=0)

    (h0p, n1, q, kv, ag, gates), (wa, wc, wo) = _in_proj(
        x[0], meta_full, norm_pre_mix, w_int, b_in, tabs,
        comm=_Gather([w_attn_proj[0].astype(BF16), w_conv_proj[0].astype(BF16), w_out[0].astype(BF16)]))
    (ao,), (w_upt,) = _attn_fwd(q, kv, attn_sinks, comm=_Gather([w_up[0].T.astype(BF16)]))
    (c0,), (wd,) = _conv31_fwd(ag, cdw32, conv_dw_b, comm=_Gather([w_down[0].astype(BF16)]))
    c1, attn, conv, merged, mix, h1, n2 = _mixer_fwd(ao, c0, gates, h0p, wa, wc, wo, vecs)
    u0 = _mm_nt(n2, w_upt, "ffn_up")
    act = _ffn_act(u0, fdw, ffn_dw_b)
    dffn, dact, dy, acc_f = _ffn_down_loss(act, wd, h1, loss_target[0], norm_post_ffn)

    g_wd = _mm_tn([act], dffn, "grad_w_down")
    (dug, duv, gfw_g, gfw_v, gfb_g, gfb_v), (s_wd,) = _ffn_act_bwd(u0, dact, fdw, ffn_dw_b, comm=_Scatter([g_wd]))
    g_wupt = _mm_tn([dug, duv], n2, "grad_w_up")
    (dh1, acc_u), (s_wup0,) = _ffn_in_bwd(dug, duv, w_upt, h1, dy, norm_pre_ffn, comm=_Scatter([g_wupt], 0, 2))
    (dmix, dat, dcv, dgt, dao, dc0, acc_m), (s_wup1,) = _mixer_bwd(
        dh1, mix, attn, conv, gates, c0, wa, wc, wo, vecs, comm=_Scatter([g_wupt], 1, 2))
    g_wo = _mm_tn([merged], dmix, "grad_w_out")
    g_wa = _mm_tn([ao], dat, "grad_w_attn_proj")
    g_wc = _mm_tn([c1], dcv, "grad_w_conv_proj")
    (da, dg, g_cdw, g_cdb), (s_wa, s_wc, s_wo) = _conv31_bwd(ag, dc0, cdw32, comm=_Scatter([g_wa, g_wc, g_wo]))
    (dqkv, dsink), _ = _attn_bwd(q, kv, dao, attn_sinks, tabs)
    g_wint, g_bin = _mm_tn([dqkv, da, dg, dgt], n1, "grad_w_in", col_sums=True)
    (from_sibling,) = _exchange(_SiblingSwap(g_wint), "swap_w_in")
    (dh0, acc_i), (s_win,) = _in_bwd(dqkv, da, dg, dgt, w_int, h0p, dh1, norm_pre_mix,
                                     comm=_ChipScatter(_pair_add(g_wint, from_sibling)))

    big = []
    for nm, parts, nslots, w, m, v, tr in (
            ("w_in", [s_win], N_CHIP, w_in, m_w_in, v_w_in, True), ("w_up", [s_wup0, s_wup1], N_DEV, w_up, m_w_up, v_w_up, True),
            ("w_attn_proj", [s_wa], N_DEV, w_attn_proj, m_w_attn_proj, v_w_attn_proj, False),
            ("w_conv_proj", [s_wc], N_DEV, w_conv_proj, m_w_conv_proj, v_w_conv_proj, False),
            ("w_out", [s_wo], N_DEV, w_out, m_w_out, v_w_out, False),
            ("w_down", [s_wd], N_DEV, w_down, m_w_down, v_w_down, False)):
        ins = [a[0].T if tr else a[0] for a in (w, m, v)]
        big.append(tuple((o.T if tr else o)[None] for o in _sum_adamw(parts, *ins, "update_" + nm, nslots)))

    loss_row = jnp.sum(acc_f[1:2, :], axis=1, keepdims=True)
    parts = [loss_row, dh0[PAD:BLK], acc_i[0:1], acc_m[0:1], g_bin, dsink[0:1, 0:16], g_cdw[0:CONV_K], g_cdb,
             acc_m[2:3], acc_m[3:4], acc_m[1:2], acc_u[0:1], acc_f[0:1],
             jnp.concatenate([gfw_g, gfw_v], axis=1), jnp.concatenate([gfb_g, gfb_v], axis=1)]
    shapes = [a.shape for a in parts]
    pack_rows = 88
    (gathered,) = _exchange(_Gather([_flat_pack(parts, pack_rows)]), "gather_small_grads")
    tot = _flat_unpack(_sum_slots(gathered, "sum_small_grads"), shapes)
    (loss, g_meta, g_npm, g_nqm, g_bi, g_sk, g_cw, g_cb, g_lg, g_lb, g_bc, g_npf, g_nqf, g_fw, g_fb) = tot
    loss = loss.reshape(())
    g_meta = lax.dynamic_slice_in_dim(g_meta, me * BLK, BLK, axis=1)
    g_cw = lax.dynamic_slice_in_dim(g_cw, me * BLK, BLK, axis=1)[None]
    g_fw = lax.dynamic_slice_in_dim(g_fw, me * up_cols, up_cols, axis=1)[None]

    sm_w = [meta_tokens, norm_pre_mix, norm_post_mix, b_in, attn_sinks, conv_dw_w, conv_dw_b, conv_ln_g, conv_ln_b,
            b_conv_proj, norm_pre_ffn, norm_post_ffn, ffn_dw_w, ffn_dw_b]
    sm_g = [g_meta, g_npm, g_nqm, g_bi, g_sk, g_cw, g_cb, g_lg, g_lb, g_bc, g_npf, g_nqf, g_fw, g_fb]
    sm_m = [m_meta_tokens, m_norm_pre_mix, m_norm_post_mix, m_b_in, m_attn_sinks, m_conv_dw_w, m_conv_dw_b, m_conv_ln_g,
            m_conv_ln_b, m_b_conv_proj, m_norm_pre_ffn, m_norm_post_ffn, m_ffn_dw_w, m_ffn_dw_b]
    sm_v = [v_meta_tokens, v_norm_pre_mix, v_norm_post_mix, v_b_in, v_attn_sinks, v_conv_dw_w, v_conv_dw_b, v_conv_ln_g,
            v_conv_ln_b, v_b_conv_proj, v_norm_pre_ffn, v_norm_post_ffn, v_ffn_dw_w, v_ffn_dw_b]
    sm_shapes = [a.shape for a in sm_w]
    upd_rows = 32
    v_pack = _flat_pack(sm_v, upd_rows)
    sm_out = _adamw(_flat_pack(sm_w, upd_rows), _flat_pack(sm_g, upd_rows), _flat_pack(sm_m, upd_rows), v_pack, "adamw_small")
    sm_d, sm_nm, sm_nv = (_flat_unpack(o, sm_shapes) for o in sm_out)

    order = ["meta_tokens", "norm_pre_mix", "norm_post_mix", "w_in", "b_in", "attn_sinks", "w_attn_proj", "conv_dw_w",
             "conv_dw_b", "conv_ln_g", "conv_ln_b", "w_conv_proj", "b_conv_proj", "w_out", "norm_pre_ffn", "norm_post_ffn",
             "w_up", "ffn_dw_w", "ffn_dw_b", "w_down"]
    small_names = ["meta_tokens", "norm_pre_mix", "norm_post_mix", "b_in", "attn_sinks", "conv_dw_w", "conv_dw_b", "conv_ln_g",
                   "conv_ln_b", "b_conv_proj", "norm_pre_ffn", "norm_post_ffn", "ffn_dw_w", "ffn_dw_b"]
    big_names = ["w_in", "w_up", "w_attn_proj", "w_conv_proj", "w_out", "w_down"]
    table = {}
    for k, nm in enumerate(small_names):
        table[nm] = (sm_g[k], sm_d[k], sm_nm[k], sm_nv[k])
    for k, nm in enumerate(big_names):
        table[nm] = big[k]
    grad_x = dh0[BLK:][None]
    outs = [loss, grad_x]
    for field in range(4):
        outs += [table[nm][field] for nm in order]
    return tuple(outs)
```

```python
import functools

import jax
import jax.numpy as jnp
from jax import lax
from jax.experimental import pallas as pl
from jax.experimental.pallas import tpu as pltpu

F32 = jnp.float32
BF16 = jnp.bfloat16
MESH = pl.DeviceIdType.MESH

D = 1024
HEAD_DIM = 64
N_META = 16
BLK = 128
PAD = BLK - N_META
CONV_K = 31
FFN = 2816
FFN_K = 3
QKV_W = 1280
IN_W = 5376
ROT_DIM = 16
ROPE_THETA = 500000.0
RMS_EPS = 1e-6
LN_EPS = 1e-5
NEG_INF = -1e30
SCALE = HEAD_DIM ** -0.5
N_DEV = 8

ADAM_LR = 0.001
ADAM_B1 = 0.9
ADAM_B2 = 0.999
ADAM_EPS = 1e-08
ADAM_WD = 0.01
ADAM_STEP = 10

VMEM_BYTES_V7X = 64 * 1024 * 1024
VMEM_LIMIT = VMEM_BYTES_V7X - 8 * 1024 * 1024

NT = (((1,), (1,)), ((), ()))
TN = (((0,), (0,)), ((), ()))
VM = pl.BlockSpec(memory_space=pltpu.VMEM)
ANY = pl.BlockSpec(memory_space=pl.ANY)


def _cparams(*sem):
    return pltpu.CompilerParams(dimension_semantics=sem or None, vmem_limit_bytes=VMEM_LIMIT)


def _row_tile(p):
    return 384 if p % 384 == 0 else 128


def _dot(a, b):
    return jnp.dot(a, b, preferred_element_type=F32)


def _dot_nt(a, b):
    return lax.dot_general(a, b, NT, preferred_element_type=F32)


def _dot_tn(a, b):
    return lax.dot_general(a, b, TN, preferred_element_type=F32)


def _rms(x, g):
    return x * lax.rsqrt(jnp.mean(x * x, axis=-1, keepdims=True) + RMS_EPS) * g


def _lnsilu(x, g, b):
    mu = jnp.mean(x, axis=-1, keepdims=True)
    var = jnp.mean(jnp.square(x - mu), axis=-1, keepdims=True)
    z = (x - mu) * lax.rsqrt(var + LN_EPS) * g + b
    return z * jax.nn.sigmoid(z)


def _rope(v, c, s1, s2):
    return v * c + pltpu.roll(v, BLK - 8, 1) * s1 + pltpu.roll(v, 8, 1) * s2


def _rows(i, tm):
    return i * tm + lax.broadcasted_iota(jnp.int32, (tm, 1), 0)


def _place():
    return lax.axis_index("x"), lax.axis_index("y"), lax.axis_index("c")


def _blk(ref, idx, r, dtype):
    return ref.at[pl.ds(pl.multiple_of(idx * r, 16 if dtype == BF16 else 8), r), :]


class _Gather:
    def __init__(self, arrs):
        self.ins = list(arrs)
        n = len(arrs)
        self.out_shape = [jax.ShapeDtypeStruct((N_DEV * a.shape[0], a.shape[1]), a.dtype) for a in arrs]
        self.scratch = [pltpu.SemaphoreType.DMA((n, 7)), pltpu.SemaphoreType.DMA((n, 7)), pltpu.SemaphoreType.DMA((n,))]

    def _parts(self, ins, outs, sems):
        send_sems, recv_sems, local_sems = sems
        n = len(ins)
        x, y, c = _place()
        me, sibling = (x, y, c), (x, y, 1 - c)
        chips = [(1 - x, y), (x, 1 - y), (1 - x, 1 - y)]

        def rows(a, p):
            return _blk(outs[a], 4 * p[0] + 2 * p[1] + p[2], self.ins[a].shape[0], self.ins[a].dtype)

        def copy(a, k, block, to, src=None):
            return pltpu.make_async_remote_copy(
                src_ref=rows(a, block) if src is None else src, dst_ref=rows(a, block),
                send_sem=send_sems.at[a, k], recv_sem=recv_sems.at[a, k], device_id=to, device_id_type=MESH)

        mine = [pltpu.make_async_copy(ins[a], rows(a, me), local_sems.at[a]) for a in range(n)]
        first = []
        for a in range(n):
            first.append(copy(a, 0, me, sibling, src=ins[a]))
            first += [copy(a, 1 + j, me, (*chip, c), src=ins[a]) for j, chip in enumerate(chips)]
        return n, c, me, sibling, chips, copy, mine, first

    def start(self, ins, outs, sems):
        *_, mine, first = self._parts(ins, outs, sems)
        for cp in mine + first:
            cp.start()

    def finish(self, ins, outs, sems):
        n, c, me, sibling, chips, copy, mine, first = self._parts(ins, outs, sems)
        passed = []
        for j, chip in enumerate(chips):
            for a in range(n):
                copy(a, 1 + j, (*chip, c), me).wait_recv()
                fwd = copy(a, 4 + j, (*chip, c), sibling)
                fwd.start()
                passed.append(fwd)
        for a in range(n):
            copy(a, 0, sibling, me).wait_recv()
            for j, chip in enumerate(chips):
                copy(a, 4 + j, (*chip, 1 - c), me).wait_recv()
        for cp in first + passed:
            cp.wait_send()
        for cp in mine:
            cp.wait()


FLIPS = [(0, 0, 1), (1, 0, 0), (0, 1, 0), (1, 1, 0), (1, 0, 1), (0, 1, 1), (1, 1, 1)]


class _Scatter:
    def __init__(self, arrs, part=0, nparts=1):
        self.ins = list(arrs)
        self.part, self.nparts = part, nparts
        n = len(arrs)
        self.out_shape = [jax.ShapeDtypeStruct((a.shape[0] // nparts, a.shape[1]), a.dtype) for a in arrs]
        self.scratch = [pltpu.SemaphoreType.DMA((n, 7)), pltpu.SemaphoreType.DMA((n, 7)), pltpu.SemaphoreType.DMA((n,))]

    def _parts(self, ins, outs, sems):
        send_sems, recv_sems, local_sems = sems
        n = len(ins)
        x, y, c = _place()
        me = 4 * x + 2 * y + c

        def flip(v, f):
            return 1 - v if f else v

        def src(a, idx):
            r = self.ins[a].shape[0] // N_DEV
            rs = r // self.nparts
            return ins[a].at[pl.ds(pl.multiple_of(idx * r + self.part * rs, 16), rs), :]

        def dst(a, idx):
            rs = self.ins[a].shape[0] // N_DEV // self.nparts
            return outs[a].at[pl.ds(pl.multiple_of(idx * rs, 16), rs), :]

        mine = [pltpu.make_async_copy(src(a, me), dst(a, me), local_sems.at[a]) for a in range(n)]
        sends, recvs = [], []
        for k, f in enumerate(FLIPS):
            peer = (flip(x, f[0]), flip(y, f[1]), flip(c, f[2]))
            pidx = 4 * peer[0] + 2 * peer[1] + peer[2]
            for a in range(n):
                sends.append(pltpu.make_async_remote_copy(
                    src_ref=src(a, pidx), dst_ref=dst(a, me),
                    send_sem=send_sems.at[a, k], recv_sem=recv_sems.at[a, k], device_id=peer, device_id_type=MESH))
                recvs.append(functools.partial(
                    pltpu.make_async_remote_copy,
                    src_ref=src(a, pidx), dst_ref=dst(a, pidx),
                    send_sem=send_sems.at[a, k], recv_sem=recv_sems.at[a, k], device_id=peer, device_id_type=MESH))
        return mine, sends, recvs

    def start(self, ins, outs, sems):
        mine, sends, _ = self._parts(ins, outs, sems)
        for cp in mine + sends:
            cp.start()

    def finish(self, ins, outs, sems):
        mine, sends, recvs = self._parts(ins, outs, sems)
        for make in recvs:
            make().wait_recv()
        for cp in sends:
            cp.wait_send()
        for cp in mine:
            cp.wait()


N_CHIP = 4


class _SiblingSwap:
    def __init__(self, arr):
        self.ins = [arr]
        self.r = arr.shape[0] // N_DEV
        self.out_shape = [jax.ShapeDtypeStruct((N_CHIP * self.r, arr.shape[1]), arr.dtype)]
        self.scratch = [pltpu.SemaphoreType.DMA((N_CHIP,)), pltpu.SemaphoreType.DMA((N_CHIP,))]

    def _copies(self, ins, outs, sems):
        send_sems, recv_sems = sems
        x, y, c = _place()
        r = self.r
        return [pltpu.make_async_remote_copy(
            src_ref=ins[0].at[pl.ds(pl.multiple_of((2 * j + 1 - c) * r, 16), r), :],
            dst_ref=outs[0].at[pl.ds(j * r, r), :],
            send_sem=send_sems.at[j], recv_sem=recv_sems.at[j], device_id=(x, y, 1 - c), device_id_type=MESH)
            for j in range(N_CHIP)]

    def start(self, ins, outs, sems):
        for cp in self._copies(ins, outs, sems):
            cp.start()

    def finish(self, ins, outs, sems):
        for cp in self._copies(ins, outs, sems):
            cp.wait()


class _ChipScatter:
    def __init__(self, arr):
        self.ins = [arr]
        self.r = arr.shape[0] // N_CHIP
        self.out_shape = [jax.ShapeDtypeStruct(arr.shape, arr.dtype)]
        self.scratch = [pltpu.SemaphoreType.DMA((3,)), pltpu.SemaphoreType.DMA((3,)), pltpu.SemaphoreType.DMA]

    def _parts(self, ins, outs, sems):
        send_sems, recv_sems, local_sem = sems
        x, y, c = _place()
        r = self.r
        my_chip = 2 * x + y

        def rows(ref, j):
            return ref.at[pl.ds(pl.multiple_of(j * r, 16), r), :]

        mine = pltpu.make_async_copy(rows(ins[0], my_chip), rows(outs[0], my_chip), local_sem)
        sends, recvs = [], []
        for k, (fx, fy) in enumerate(((1, 0), (0, 1), (1, 1))):
            px, py = (1 - x if fx else x), (1 - y if fy else y)
            peer_chip = 2 * px + py
            sends.append(pltpu.make_async_remote_copy(
                src_ref=rows(ins[0], peer_chip), dst_ref=rows(outs[0], my_chip),
                send_sem=send_sems.at[k], recv_sem=recv_sems.at[k], device_id=(px, py, c), device_id_type=MESH))
            recvs.append(functools.partial(
                pltpu.make_async_remote_copy,
                src_ref=rows(ins[0], peer_chip), dst_ref=rows(outs[0], peer_chip),
                send_sem=send_sems.at[k], recv_sem=recv_sems.at[k], device_id=(px, py, c), device_id_type=MESH))
        return mine, sends, recvs

    def start(self, ins, outs, sems):
        mine, sends, _ = self._parts(ins, outs, sems)
        for cp in [mine] + sends:
            cp.start()

    def finish(self, ins, outs, sems):
        mine, sends, recvs = self._parts(ins, outs, sems)
        for make in recvs:
            make().wait_recv()
        for cp in sends:
            cp.wait_send()
        mine.wait()


def _pair_add(partial, recv):
    r = recv.shape[0] // N_CHIP
    cols = recv.shape[1]
    tr = r // 2 if (r // 2) % 16 == 0 else r
    steps = r // tr
    core = lax.axis_index("c").astype(jnp.int32).reshape(1)

    def body(c_ref, p_ref, s_ref, o_ref):
        o_ref[...] = (p_ref[...].astype(F32) + s_ref[...].astype(F32)).astype(BF16)

    spec = pl.BlockSpec((tr, cols), lambda j, i, c_ref: (j * steps + i, 0))
    return pl.pallas_call(
        body, name="pair_add",
        grid_spec=pltpu.PrefetchScalarGridSpec(
            num_scalar_prefetch=1, grid=(N_CHIP, steps),
            in_specs=[pl.BlockSpec((tr, cols), lambda j, i, c_ref: ((2 * j + c_ref[0]) * steps + i, 0)), spec],
            out_specs=spec),
        out_shape=jax.ShapeDtypeStruct(recv.shape, BF16),
        compiler_params=_cparams("parallel", "parallel"),
    )(core, partial, recv)


def _exchange(comm, name):
    n, m = len(comm.ins), len(comm.out_shape)

    def body(*refs):
        ins, outs, sems = refs[:n], refs[n:n + m], refs[n + m:]
        comm.start(ins, outs, sems)
        comm.finish(ins, outs, sems)

    return pl.pallas_call(
        body, name=name, out_shape=comm.out_shape, in_specs=[ANY] * n, out_specs=[ANY] * m, scratch_shapes=comm.scratch,
    )(*comm.ins)


def _call(body, *, name, grid, in_specs, out_specs, out_shape, args, scratch=(), sem="parallel", comm=None):
    if comm is None:
        outs = pl.pallas_call(
            body, name=name, grid=grid, in_specs=list(in_specs), out_specs=list(out_specs), out_shape=list(out_shape),
            scratch_shapes=list(scratch), compiler_params=_cparams(sem))(*args)
        return outs, []
    n_in, n_out, n_sc = len(in_specs), len(out_specs), len(scratch)
    n_ci, n_co = len(comm.ins), len(comm.out_shape)
    last = grid[0] - 1

    def fused(*refs):
        ins, refs = refs[:n_in], refs[n_in:]
        c_ins, refs = refs[:n_ci], refs[n_ci:]
        outs, refs = refs[:n_out], refs[n_out:]
        c_outs, refs = refs[:n_co], refs[n_co:]
        sc, c_sems = refs[:n_sc], refs[n_sc:]
        step = pl.program_id(0)

        @pl.when(step == 0)
        def _():
            comm.start(c_ins, c_outs, c_sems)

        body(*ins, *outs, *sc)

        @pl.when(step == last)
        def _():
            comm.finish(c_ins, c_outs, c_sems)

    outs = pl.pallas_call(
        fused, name=name, grid=grid, in_specs=list(in_specs) + [ANY] * n_ci, out_specs=list(out_specs) + [ANY] * n_co,
        out_shape=list(out_shape) + comm.out_shape, scratch_shapes=list(scratch) + comm.scratch,
        compiler_params=_cparams("arbitrary"))(*args, *comm.ins)
    return outs[:n_out], outs[n_out:]


def _token_specs(tm):
    k = tm // BLK
    return [pl.BlockSpec((BLK, D), functools.partial(lambda i, t: (jnp.maximum(k * i + t - 1, 0), 0), t=t)) for t in range(k)]


def _in_proj(x2d, meta, gain, w_int, b_in, tabs, comm=None):
    p = x2d.shape[0] + BLK
    tm = _row_tile(p)
    k = tm // BLK

    def body(*refs):
        x_refs = refs[:k]
        m_ref, g_ref, w_ref, b_ref, t_ref, h_ref, n1_ref, q_ref, kv_ref, ag_ref, gt_ref = refs[k:]
        i = pl.program_id(0)
        head = jnp.concatenate([jnp.zeros((PAD, D), F32), m_ref[...]], axis=0)
        first = jnp.where(i == 0, head, x_refs[0][...])
        h = jnp.concatenate([first] + [r[...] for r in x_refs[1:]], axis=0) if k > 1 else first
        h_ref[...] = h
        n = _rms(h, g_ref[...]).astype(BF16)
        n1_ref[...] = n
        c, s1, s2 = t_ref[:, 0:128], t_ref[:, 128:256], t_ref[:, 256:384]

        def mm(c0, w):
            return _dot_nt(n, w_ref[c0:c0 + w, :]) + b_ref[:, c0:c0 + w]

        for j in range(4):
            acc = mm(256 * j, 256)
            for t in range(2):
                lo = 256 * j + 128 * t
                q_ref[:, lo:lo + 128] = (_rope(acc[:, 128 * t:128 * (t + 1)], c, s1, s2) * SCALE).astype(BF16)
        acc = mm(1024, 256)
        kv_ref[:, 0:128] = _rope(acc[:, 0:128], c, s1, s2).astype(BF16)
        kv_ref[:, 128:256] = acc[:, 128:256].astype(BF16)
        for j in range(8):
            ag_ref[:, 256 * j:256 * (j + 1)] = mm(QKV_W + 256 * j, 256).astype(BF16)
        for j in range(8):
            gt_ref[:, 256 * j:256 * (j + 1)] = mm(QKV_W + 2048 + 256 * j, 256).astype(BF16)

    def row(w):
        return pl.BlockSpec((tm, w), lambda i: (i, 0))

    return _call(
        body, name="in_proj", grid=(p // tm,),
        in_specs=_token_specs(tm) + [VM, VM, VM, VM, row(384)],
        out_specs=[row(D), row(D), row(D), row(256), row(2048), row(2048)],
        out_shape=[jax.ShapeDtypeStruct((p, D), F32)] + [jax.ShapeDtypeStruct((p, w), BF16) for w in (D, D, 256, 2048, 2048)],
        args=(x2d,) * k + (meta, gain, w_int, b_in, tabs), comm=comm)


N_KEY = 2 * BLK + N_META


def _attn_setup(n, h, q_ref, km_ref, kp_ref, kc_ref):
    lo = lax.broadcasted_iota(jnp.int32, (BLK, BLK), 1) < HEAD_DIM
    lok = lax.broadcasted_iota(jnp.int32, (N_KEY, BLK), 1) < HEAD_DIM

    def dup(lanes):
        cat = jnp.concatenate([kp_ref[:, lanes], kc_ref[:, lanes], km_ref[PAD:BLK, lanes]], axis=0).astype(F32)
        rolled = pltpu.roll(cat, HEAD_DIM, 1)
        return (jnp.where(lok, cat, rolled) if h == 0 else jnp.where(lok, rolled, cat)).astype(BF16)

    k2 = dup(slice(0, 128))
    v2 = dup(slice(128, 256))
    qs = _stack_heads(q_ref, h, lo)

    kr = lax.broadcasted_iota(jnp.int32, (BLK, BLK), 0)
    tq = BLK * n + lax.broadcasted_iota(jnp.int32, (BLK, BLK), 1) - PAD
    t_p = BLK * (n - 1) + kr - PAD
    t_c = BLK * n + kr - PAD
    ok_p = jnp.logical_and(t_p >= N_META, tq - t_p < BLK)
    ok_c = jnp.logical_and(t_c >= N_META, t_c <= tq)
    ok_m = lax.broadcasted_iota(jnp.int32, (N_META, BLK), 0) <= BLK * n + lax.broadcasted_iota(jnp.int32, (N_META, BLK), 1) - PAD
    bias = jnp.concatenate([jnp.where(ok, 0.0, NEG_INF).astype(F32) for ok in (ok_p, ok_c, ok_m)], axis=0)
    return qs, k2, v2, bias, lok


def _attn_head(s, bias, sink):
    s = s + bias
    m = jnp.maximum(jnp.max(s, axis=0, keepdims=True), sink)
    e = jnp.exp(s - m)
    es = jnp.exp(sink - m)
    inv = 1.0 / (jnp.sum(e, axis=0, keepdims=True) + es)
    return e * inv, es * inv


def _stack_heads(ref, h, lo):
    pieces = []
    for jp in range(4):
        v = ref[:, BLK * (4 * h + jp):BLK * (4 * h + jp + 1)]
        zero = jnp.zeros_like(v)
        pieces += [jnp.where(lo, v, zero), jnp.where(lo, zero, v)]
    return jnp.concatenate(pieces, axis=0)


def _unstack_heads(v, jp, lo):
    return jnp.where(lo, v[256 * jp:256 * jp + 128], v[256 * jp + 128:256 * jp + 256])


def _attn_fwd(q, kv, sinks, comm=None):
    p = q.shape[0]
    nb = p // BLK

    def body(q_ref, km_ref, kp_ref, kc_ref, sink_ref, o_ref):
        n = pl.program_id(0)
        lo = lax.broadcasted_iota(jnp.int32, (BLK, BLK), 1) < HEAD_DIM
        for h in range(2):
            qs, k2, v2, bias, _ = _attn_setup(n, h, q_ref, km_ref, kp_ref, kc_ref)
            st = _dot_nt(k2, qs)
            pt = jnp.concatenate(
                [_attn_head(st[:, BLK * g:BLK * (g + 1)], bias, sink_ref[0, 8 * h + g])[0].astype(BF16) for g in range(8)],
                axis=1)
            o = _dot_tn(pt, v2)
            for jp in range(4):
                o_ref[:, BLK * (4 * h + jp):BLK * (4 * h + jp + 1)] = _unstack_heads(o, jp, lo).astype(BF16)

    return _call(
        body, name="attn_fwd", grid=(nb,),
        in_specs=[pl.BlockSpec((BLK, D), lambda i: (i, 0)),
                  pl.BlockSpec((BLK, 256), lambda i: (0, 0)),
                  pl.BlockSpec((BLK, 256), lambda i: (jnp.maximum(i - 1, 0), 0)),
                  pl.BlockSpec((BLK, 256), lambda i: (i, 0)),
                  pl.BlockSpec(memory_space=pltpu.SMEM)],
        out_specs=[pl.BlockSpec((BLK, D), lambda i: (i, 0))],
        out_shape=[jax.ShapeDtypeStruct((p, D), BF16)],
        args=(q, kv, kv, kv, sinks), comm=comm)


def _conv31_fwd(ag, w32, b, comm=None):
    p = ag.shape[0]
    nch = p // BLK

    def body(a_ref, g_ref, w_ref, b_ref, o_ref, gp):
        gp[0:32, :] = jnp.zeros((32, BLK), F32)
        for ci in range(nch):
            r0 = BLK * ci
            glu = a_ref[r0:r0 + BLK, :].astype(F32) * jax.nn.sigmoid(g_ref[r0:r0 + BLK, :].astype(F32))
            if ci == 0:
                glu = jnp.where(_rows(0, BLK) >= PAD, glu, 0.0)
            gp[32 + r0:32 + r0 + BLK, :] = glu
        for ci in range(nch):
            r0 = BLK * ci
            acc = jnp.broadcast_to(b_ref[...], (BLK, BLK))
            for j in range(CONV_K):
                acc = acc + w_ref[j:j + 1, :] * gp[r0 + j + 2:r0 + j + 2 + BLK, :]
            o_ref[r0:r0 + BLK, :] = acc

    return _call(
        body, name="conv31_fwd", grid=(D // BLK,),
        in_specs=[pl.BlockSpec((p, BLK), lambda j: (0, j)), pl.BlockSpec((p, BLK), lambda j: (0, 8 + j)),
                  pl.BlockSpec((32, BLK), lambda j: (0, j)), pl.BlockSpec((1, BLK), lambda j: (0, j))],
        out_specs=[pl.BlockSpec((p, BLK), lambda j: (0, j))],
        out_shape=[jax.ShapeDtypeStruct((p, D), F32)],
        scratch=[pltpu.VMEM((p + 32, BLK), F32)],
        args=(ag, ag, w32, b), comm=comm)


def _mixer_fwd(ao, c0, gates, h0p, wa, wc, wo, vecs):
    p = ao.shape[0]
    tm = _row_tile(p)

    def body(ao_ref, c0_ref, gt_ref, h_ref, wa_ref, wc_ref, wo_ref, v_ref,
             c1_ref, at_ref, cv_ref, mg_ref, mix_ref, h1_ref, n2_ref):
        i = pl.program_id(0)
        c1 = _lnsilu(c0_ref[...], v_ref[0:1, :], v_ref[1:2, :]).astype(BF16)
        c1_ref[...] = c1
        attn = _dot(ao_ref[...], wa_ref[...])
        conv = _dot(c1, wc_ref[...]) + v_ref[2:3, :]
        at_ref[...] = attn.astype(BF16)
        cv_ref[...] = conv.astype(BF16)
        merged = (jax.nn.sigmoid(gt_ref[:, 0:D].astype(F32)) * attn
                  + jax.nn.sigmoid(gt_ref[:, D:2 * D].astype(F32)) * conv).astype(BF16)
        mg_ref[...] = merged
        mix = _dot(merged, wo_ref[...])
        mix_ref[...] = mix
        h1 = jnp.where(_rows(i, tm) >= PAD, h_ref[...] + _rms(mix, v_ref[3:4, :]), 0.0)
        h1_ref[...] = h1
        n2_ref[...] = _rms(h1, v_ref[4:5, :]).astype(BF16)

    def row(w):
        return pl.BlockSpec((tm, w), lambda i: (i, 0))

    return pl.pallas_call(
        body, name="mixer_fwd", grid=(p // tm,),
        in_specs=[row(D), row(D), row(2 * D), row(D), VM, VM, VM, VM],
        out_specs=[row(D)] * 7,
        out_shape=[jax.ShapeDtypeStruct((p, D), t) for t in (BF16, BF16, BF16, BF16, F32, F32, BF16)],
        compiler_params=_cparams("parallel"),
    )(ao, c0, gates, h0p, wa, wc, wo, vecs)


def _mm_nt(a, w_t, name):
    p, k = a.shape
    n = w_t.shape[0]
    tm = _row_tile(p)
    ch = 512

    def body(a_ref, w_ref, o_ref):
        a_v = a_ref[...]
        for c0 in range(0, n, ch):
            o_ref[:, c0:c0 + ch] = _dot_nt(a_v, w_ref[c0:c0 + ch, :]).astype(BF16)

    return pl.pallas_call(
        body, name=name, grid=(p // tm,),
        in_specs=[pl.BlockSpec((tm, k), lambda i: (i, 0)), VM],
        out_specs=pl.BlockSpec((tm, n), lambda i: (i, 0)),
        out_shape=jax.ShapeDtypeStruct((p, n), BF16),
        compiler_params=_cparams("parallel"),
    )(a, w_t)


def _conv3(xp_ref, w_ref, r0):
    return (w_ref[0:1, :] * xp_ref[r0 + 6:r0 + 6 + BLK, :] + w_ref[1:2, :] * xp_ref[r0 + 7:r0 + 7 + BLK, :]
            + w_ref[2:3, :] * xp_ref[r0 + 8:r0 + 8 + BLK, :])


def _ffn_slab_specs(p):
    ncol = FFN // BLK
    return [pl.BlockSpec((p, BLK), lambda j: (0, j)), pl.BlockSpec((p, BLK), lambda j: (0, ncol + j)),
            pl.BlockSpec((FFN_K, BLK), lambda j: (0, j)), pl.BlockSpec((FFN_K, BLK), lambda j: (0, ncol + j)),
            pl.BlockSpec((1, BLK), lambda j: (0, j)), pl.BlockSpec((1, BLK), lambda j: (0, ncol + j))]


def _fill_shifted(dst, src_ref, nch):
    dst[0:8, :] = jnp.zeros((8, BLK), F32)
    for ci in range(nch):
        dst[8 + BLK * ci:8 + BLK * (ci + 1), :] = src_ref[BLK * ci:BLK * (ci + 1), :].astype(F32)


def _ffn_act(u0, fw, fb):
    p = u0.shape[0]
    nch = p // BLK

    def body(g_ref, v_ref, wg_ref, wv_ref, bg_ref, bv_ref, o_ref, xg, xv):
        _fill_shifted(xg, g_ref, nch)
        _fill_shifted(xv, v_ref, nch)
        for ci in range(nch):
            r0 = BLK * ci
            ug = _conv3(xg, wg_ref, r0) + bg_ref[...]
            uv = _conv3(xv, wv_ref, r0) + bv_ref[...]
            o_ref[r0:r0 + BLK, :] = (ug * jax.nn.sigmoid(ug) * uv).astype(BF16)

    return pl.pallas_call(
        body, name="ffn_act", grid=(FFN // BLK,),
        in_specs=_ffn_slab_specs(p),
        out_specs=pl.BlockSpec((p, BLK), lambda j: (0, j)),
        out_shape=jax.ShapeDtypeStruct((p, FFN), BF16),
        scratch_shapes=[pltpu.VMEM((p + 8, BLK), F32)] * 2,
        compiler_params=_cparams("parallel"),
    )(u0, u0, fw, fw, fb, fb)


def _ffn_down_loss(act, wd, h1, tgt, gain):
    p = act.shape[0]
    tm = _row_tile(p)
    k = tm // BLK

    def body(*refs):
        a_ref, w_ref, h_ref = refs[:3]
        t_refs = refs[3:3 + k]
        g_ref, df_ref, da_ref, dy_ref, acc_ref = refs[3 + k:]
        i = pl.program_id(0)

        @pl.when(i == 0)
        def _():
            acc_ref[...] = jnp.zeros_like(acc_ref)

        ffn = _dot(a_ref[...], w_ref[...])
        r, vjp = jax.vjp(_rms, ffn, g_ref[...])
        t = jnp.concatenate([t_ref[...] for t_ref in t_refs], axis=0) if k > 1 else t_refs[0][...]
        diff = jnp.where(_rows(i, tm) >= BLK, h_ref[...] + r - t, 0.0)
        dy = diff * (1.0 / D)
        dffn, dg = vjp(dy)
        acc_ref[0:1, :] += dg
        acc_ref[1:2, :] += jnp.sum(diff * diff, axis=0, keepdims=True) * (0.5 / D)
        dy_ref[...] = dy
        dfb = dffn.astype(BF16)
        df_ref[...] = dfb
        for c0 in range(0, FFN, 256):
            da_ref[:, c0:c0 + 256] = _dot_nt(dfb, w_ref[c0:c0 + 256, :]).astype(BF16)

    def row(w):
        return pl.BlockSpec((tm, w), lambda i: (i, 0))

    return pl.pallas_call(
        body, name="ffn_down_loss", grid=(p // tm,),
        in_specs=[row(FFN), VM, row(D)] + _token_specs(tm) + [VM],
        out_specs=[row(D), row(FFN), row(D), pl.BlockSpec((8, D), lambda i: (0, 0))],
        out_shape=[jax.ShapeDtypeStruct((p, D), BF16), jax.ShapeDtypeStruct((p, FFN), BF16),
                   jax.ShapeDtypeStruct((p, D), F32), jax.ShapeDtypeStruct((8, D), F32)],
        compiler_params=_cparams("arbitrary"),
    )(act, wd, h1, *([tgt] * k), gain)


def _mm_tn(pieces, b, name, col_sums=False, comm=None):
    p, n = b.shape
    tk = 256
    nblk = [a.shape[1] // tk for a in pieces]
    offs = [sum(nblk[:q]) for q in range(len(pieces))]
    total = sum(nblk)
    npc = len(pieces)

    def body(*refs):
        a_refs, b_ref, o_ref = refs[:npc], refs[npc], refs[npc + 1]
        i = pl.program_id(0)
        for q, a_ref in enumerate(a_refs):
            @pl.when(jnp.logical_and(i >= offs[q], i < offs[q] + nblk[q]))
            def _(a_ref=a_ref):
                a_v = a_ref[...]
                o_ref[...] = _dot_tn(a_v, b_ref[...]).astype(BF16)
                if col_sums:
                    refs[npc + 2][...] = jnp.sum(a_v.astype(F32), axis=0, keepdims=True)

    def a_spec(q):
        return pl.BlockSpec((p, tk), lambda i: (0, jnp.clip(i - offs[q], 0, nblk[q] - 1)))

    out_specs = [pl.BlockSpec((tk, n), lambda i: (i, 0))]
    out_shape = [jax.ShapeDtypeStruct((total * tk, n), BF16)]
    if col_sums:
        out_specs.append(pl.BlockSpec((1, tk), lambda i: (0, i)))
        out_shape.append(jax.ShapeDtypeStruct((1, total * tk), F32))
    res, sent = _call(
        body, name=name, grid=(total,),
        in_specs=[a_spec(q) for q in range(npc)] + [VM],
        out_specs=out_specs, out_shape=out_shape, args=(*pieces, b), comm=comm)
    res = res if col_sums else res[0]
    return res if comm is None else (res, sent)


def _ffn_act_bwd(u0, dact, fw, fb, act, dffn, comm=None):
    p = u0.shape[0]
    nch = p // BLK
    ncol = FFN // BLK

    def body(g_ref, v_ref, wg_ref, wv_ref, bg_ref, bv_ref, da_ref, act_ref, df_ref,
             dg_ref, dv_ref, gwg_ref, gwv_ref, gbg_ref, gbv_ref, gwd_ref, xg, xv, eg, ev):
        gwd_ref[...] = _dot_tn(act_ref[...], df_ref[...]).astype(BF16)
        _fill_shifted(xg, g_ref, nch)
        _fill_shifted(xv, v_ref, nch)
        eg[p:p + 8, :] = jnp.zeros((8, BLK), F32)
        ev[p:p + 8, :] = jnp.zeros((8, BLK), F32)
        for ci in range(nch):
            r0 = BLK * ci
            ug = _conv3(xg, wg_ref, r0) + bg_ref[...]
            uv = _conv3(xv, wv_ref, r0) + bv_ref[...]
            sg = jax.nn.sigmoid(ug)
            d = da_ref[r0:r0 + BLK, :].astype(F32)
            eg[r0:r0 + BLK, :] = d * uv * (sg * (1.0 + ug * (1.0 - sg)))
            ev[r0:r0 + BLK, :] = d * ug * sg
        for e_s, x_s, w_ref, d_ref, gw_ref, gb_ref in ((eg, xg, wg_ref, dg_ref, gwg_ref, gbg_ref),
                                                      (ev, xv, wv_ref, dv_ref, gwv_ref, gbv_ref)):
            sums = [jnp.zeros((BLK, BLK), F32) for _ in range(FFN_K + 1)]
            for ci in range(nch):
                r0 = BLK * ci
                e0 = e_s[r0:r0 + BLK, :]
                du = (w_ref[2:3, :] * e0 + w_ref[1:2, :] * e_s[r0 + 1:r0 + 1 + BLK, :]
                      + w_ref[0:1, :] * e_s[r0 + 2:r0 + 2 + BLK, :])
                if ci == 0:
                    du = jnp.where(_rows(0, BLK) >= PAD, du, 0.0)
                d_ref[r0:r0 + BLK, :] = du.astype(BF16)
                for j in range(FFN_K):
                    sums[j] = sums[j] + e0 * x_s[r0 + 6 + j:r0 + 6 + j + BLK, :]
                sums[FFN_K] = sums[FFN_K] + e0
            for j in range(FFN_K):
                gw_ref[j:j + 1, :] = jnp.sum(sums[j], axis=0, keepdims=True)
            gb_ref[...] = jnp.sum(sums[FFN_K], axis=0, keepdims=True)

    slab = pl.BlockSpec((p, BLK), lambda j: (0, j))
    wspec = pl.BlockSpec((FFN_K, BLK), lambda j: (0, j))
    bspec = pl.BlockSpec((1, BLK), lambda j: (0, j))
    return _call(
        body, name="ffn_act_bwd", grid=(ncol,),
        in_specs=_ffn_slab_specs(p) + [slab, slab, VM],
        out_specs=[slab, slab, wspec, wspec, bspec, bspec, pl.BlockSpec((BLK, D), lambda j: (j, 0))],
        out_shape=[jax.ShapeDtypeStruct((p, FFN), BF16)] * 2 + [jax.ShapeDtypeStruct((FFN_K, FFN), F32)] * 2
        + [jax.ShapeDtypeStruct((1, FFN), F32)] * 2 + [jax.ShapeDtypeStruct((FFN, D), BF16)],
        scratch=[pltpu.VMEM((p + 8, BLK), F32)] * 4,
        args=(u0, u0, fw, fw, fb, fb, dact, act, dffn), comm=comm)


def _ffn_in_bwd(dug, duv, w_upt, h1, dy, gain, comm=None):
    p = h1.shape[0]
    tm = _row_tile(p)

    def body(dg_ref, dv_ref, w_ref, h_ref, dy_ref, g_ref, o_ref, acc_ref):
        i = pl.program_id(0)

        @pl.when(i == 0)
        def _():
            acc_ref[...] = jnp.zeros_like(acc_ref)

        dn = _dot(dg_ref[...], w_ref[0:FFN, :]) + _dot(dv_ref[...], w_ref[FFN:2 * FFN, :])
        _, vjp = jax.vjp(_rms, h_ref[...], g_ref[...])
        dh, dg = vjp(dn)
        o_ref[...] = dy_ref[...] + dh
        acc_ref[0:1, :] += dg

    def row(w):
        return pl.BlockSpec((tm, w), lambda i: (i, 0))

    return _call(
        body, name="ffn_in_bwd", grid=(p // tm,),
        in_specs=[row(FFN), row(FFN), VM, row(D), row(D), VM],
        out_specs=[row(D), pl.BlockSpec((8, D), lambda i: (0, 0))],
        out_shape=[jax.ShapeDtypeStruct((p, D), F32), jax.ShapeDtypeStruct((8, D), F32)],
        sem="arbitrary", args=(dug, duv, w_upt, h1, dy, gain), comm=comm)


def _mixer_bwd(dh1, mix, attn, conv, gates, c0, wa, wc, wo, vecs, comm=None):
    p = dh1.shape[0]
    tm = _row_tile(p)

    def body(dh_ref, mix_ref, at_ref, cv_ref, gt_ref, c0_ref, wa_ref, wc_ref, wo_ref, v_ref,
             dmix_ref, dat_ref, dcv_ref, dgt_ref, dao_ref, dc0_ref, acc_ref):
        i = pl.program_id(0)

        @pl.when(i == 0)
        def _():
            acc_ref[...] = jnp.zeros_like(acc_ref)

        _, vjp = jax.vjp(_rms, mix_ref[...], v_ref[3:4, :])
        dmix, dgp = vjp(dh_ref[...])
        dmix = dmix.astype(BF16)
        dmix_ref[...] = dmix
        dmg = _dot_nt(dmix, wo_ref[...])
        sa = jax.nn.sigmoid(gt_ref[:, 0:D].astype(F32))
        sc = jax.nn.sigmoid(gt_ref[:, D:2 * D].astype(F32))
        dat = dmg * sa
        dcv = dmg * sc
        dgt_ref[:, 0:D] = (dmg * at_ref[...].astype(F32) * sa * (1.0 - sa)).astype(BF16)
        dgt_ref[:, D:2 * D] = (dmg * cv_ref[...].astype(F32) * sc * (1.0 - sc)).astype(BF16)
        datb = dat.astype(BF16)
        dcvb = dcv.astype(BF16)
        dat_ref[...] = datb
        dcv_ref[...] = dcvb
        dao_ref[...] = _dot_nt(datb, wa_ref[...]).astype(BF16)
        dc1 = _dot_nt(dcvb, wc_ref[...])
        _, vjp2 = jax.vjp(_lnsilu, c0_ref[...], v_ref[0:1, :], v_ref[1:2, :])
        dc0, dlg, dlb = vjp2(dc1)
        dc0_ref[...] = dc0
        acc_ref[0:1, :] += dgp
        acc_ref[1:2, :] += jnp.sum(dcv, axis=0, keepdims=True)
        acc_ref[2:3, :] += dlg
        acc_ref[3:4, :] += dlb

    def row(w):
        return pl.BlockSpec((tm, w), lambda i: (i, 0))

    return _call(
        body, name="mixer_bwd", grid=(p // tm,),
        in_specs=[row(D), row(D), row(D), row(D), row(2 * D), row(D), VM, VM, VM, VM],
        out_specs=[row(D), row(D), row(D), row(2 * D), row(D), row(D), pl.BlockSpec((8, D), lambda i: (0, 0))],
        out_shape=[jax.ShapeDtypeStruct((p, D), BF16)] * 3 + [jax.ShapeDtypeStruct((p, 2 * D), BF16),
                                                             jax.ShapeDtypeStruct((p, D), BF16),
                                                             jax.ShapeDtypeStruct((p, D), F32),
                                                             jax.ShapeDtypeStruct((8, D), F32)],
        sem="arbitrary", args=(dh1, mix, attn, conv, gates, c0, wa, wc, wo, vecs), comm=comm)


def _conv31_bwd(ag, dc0, w32, tn_pairs, comm=None):
    p = ag.shape[0]
    nch = p // BLK
    npair = len(tn_pairs)

    def body(*refs):
        a_ref, g_ref, dc_ref, w_ref = refs[:4]
        tn_a, tn_b = refs[4:4 + npair], refs[4 + npair:4 + 2 * npair]
        da_ref, dg_ref, gw_ref, gb_ref = refs[4 + 2 * npair:8 + 2 * npair]
        tn_o = refs[8 + 2 * npair:8 + 3 * npair]
        gp, dp = refs[8 + 3 * npair:]
        for ta, tb, to in zip(tn_a, tn_b, tn_o):
            to[...] = _dot_tn(ta[...], tb[...]).astype(BF16)
        gp[0:32, :] = jnp.zeros((32, BLK), F32)
        dp[p:p + 32, :] = jnp.zeros((32, BLK), F32)
        bsum = jnp.zeros((BLK, BLK), F32)
        for ci in range(nch):
            r0 = BLK * ci
            glu = a_ref[r0:r0 + BLK, :].astype(F32) * jax.nn.sigmoid(g_ref[r0:r0 + BLK, :].astype(F32))
            if ci == 0:
                glu = jnp.where(_rows(0, BLK) >= PAD, glu, 0.0)
            gp[32 + r0:32 + r0 + BLK, :] = glu
            d = dc_ref[r0:r0 + BLK, :]
            dp[r0:r0 + BLK, :] = d
            bsum = bsum + d
        gb_ref[...] = jnp.sum(bsum, axis=0, keepdims=True)
        for ci in range(nch):
            r0 = BLK * ci
            acc = jnp.zeros((BLK, BLK), F32)
            for j in range(CONV_K):
                acc = acc + w_ref[j:j + 1, :] * dp[r0 + 30 - j:r0 + 30 - j + BLK, :]
            if ci == 0:
                acc = jnp.where(_rows(0, BLK) >= PAD, acc, 0.0)
            a = a_ref[r0:r0 + BLK, :].astype(F32)
            sg = jax.nn.sigmoid(g_ref[r0:r0 + BLK, :].astype(F32))
            da_ref[r0:r0 + BLK, :] = (acc * sg).astype(BF16)
            dg_ref[r0:r0 + BLK, :] = (acc * a * sg * (1.0 - sg)).astype(BF16)
        for j in range(CONV_K):
            acc = jnp.zeros((BLK, BLK), F32)
            for ci in range(nch):
                r0 = BLK * ci
                acc = acc + dp[r0:r0 + BLK, :] * gp[r0 + j + 2:r0 + j + 2 + BLK, :]
            gw_ref[j:j + 1, :] = jnp.sum(acc, axis=0, keepdims=True)
        gw_ref[CONV_K:32, :] = jnp.zeros((32 - CONV_K, BLK), F32)

    slab = pl.BlockSpec((p, BLK), lambda j: (0, j))
    return _call(
        body, name="conv31_bwd", grid=(D // BLK,),
        in_specs=[slab, pl.BlockSpec((p, BLK), lambda j: (0, 8 + j)), slab, pl.BlockSpec((32, BLK), lambda j: (0, j))]
        + [slab] * npair + [VM] * npair,
        out_specs=[slab, slab, pl.BlockSpec((32, BLK), lambda j: (0, j)), pl.BlockSpec((1, BLK), lambda j: (0, j))]
        + [pl.BlockSpec((BLK, D), lambda j: (j, 0))] * npair,
        out_shape=[jax.ShapeDtypeStruct((p, D), BF16)] * 2 + [jax.ShapeDtypeStruct((32, D), F32),
                                                             jax.ShapeDtypeStruct((1, D), F32)]
        + [jax.ShapeDtypeStruct((D, D), BF16)] * npair,
        scratch=[pltpu.VMEM((p + 32, BLK), F32)] * 2,
        args=(ag, ag, dc0, w32, *[a for a, _ in tn_pairs], *[b for _, b in tn_pairs]), comm=comm)


def _attn_bwd(q, kv, dao, sinks, tabs, comm=None):
    p = q.shape[0]
    nb = p // BLK

    def body(q_ref, km_ref, kp_ref, kc_ref, do_ref, sink_ref, t_ref, dqkv_ref, dsink_ref, carry, macc):
        i = pl.program_id(0)
        n = nb - 1 - i

        @pl.when(i == 0)
        def _():
            carry[...] = jnp.zeros_like(carry)
            macc[...] = jnp.zeros_like(macc)
            dsink_ref[...] = jnp.zeros_like(dsink_ref)

        lo = lax.broadcasted_iota(jnp.int32, (BLK, BLK), 1) < HEAD_DIM
        lane8 = lax.broadcasted_iota(jnp.int32, (8, BLK), 1)
        c, s1, s2 = t_ref[:, 0:128], -t_ref[:, 128:256], -t_ref[:, 256:384]
        dk = jnp.zeros((N_KEY, BLK), F32)
        dv = jnp.zeros((N_KEY, BLK), F32)
        for h in range(2):
            qs, k2, v2, bias, lok = _attn_setup(n, h, q_ref, km_ref, kp_ref, kc_ref)
            dos = _stack_heads(do_ref, h, lo)
            st = _dot_nt(k2, qs)
            dpt = _dot_nt(v2, dos)
            p_parts, ds_parts = [], []
            for g in range(8):
                cols = slice(BLK * g, BLK * (g + 1))
                pn, ps = _attn_head(st[:, cols], bias, sink_ref[0, 8 * h + g])
                dp = dpt[:, cols]
                delta = jnp.sum(pn * dp, axis=0, keepdims=True)
                ds_parts.append((pn * (dp - delta)).astype(BF16))
                p_parts.append(pn.astype(BF16))
                dsk = -jnp.sum(ps * delta, axis=1, keepdims=True)
                dsink_ref[...] += jnp.where(lane8 == 8 * h + g, dsk, 0.0)
            dst = jnp.concatenate(ds_parts, axis=1)
            pt = jnp.concatenate(p_parts, axis=1)
            dq = _dot_tn(dst, k2)
            for jp in range(4):
                lo_c = BLK * (4 * h + jp)
                dqkv_ref[:, lo_c:lo_c + BLK] = (_rope(_unstack_heads(dq, jp, lo), c, s1, s2) * SCALE).astype(BF16)
            dk2 = _dot(dst, qs)
            dv2 = _dot(pt, dos)
            dk2 = dk2 + pltpu.roll(dk2, HEAD_DIM, 1)
            dv2 = dv2 + pltpu.roll(dv2, HEAD_DIM, 1)
            own = lok if h == 0 else jnp.logical_not(lok)
            dk = jnp.where(own, dk2, dk)
            dv = jnp.where(own, dv2, dv)
        macc[:, 0:BLK] += dk[2 * BLK:N_KEY]
        macc[:, BLK:2 * BLK] += dv[2 * BLK:N_KEY]
        last = (n == 0).astype(F32)
        zpad = jnp.zeros((PAD, BLK), F32)
        dk_c = dk[BLK:2 * BLK] + carry[:, 0:BLK] + last * jnp.concatenate([zpad, macc[:, 0:BLK]], axis=0)
        dv_c = dv[BLK:2 * BLK] + carry[:, BLK:2 * BLK] + last * jnp.concatenate([zpad, macc[:, BLK:2 * BLK]], axis=0)
        carry[:, 0:BLK] = dk[0:BLK]
        carry[:, BLK:2 * BLK] = dv[0:BLK]
        dqkv_ref[:, D:D + BLK] = _rope(dk_c, c, s1, s2).astype(BF16)
        dqkv_ref[:, D + BLK:D + 2 * BLK] = dv_c.astype(BF16)

    def rev(w):
        return pl.BlockSpec((BLK, w), lambda i: (nb - 1 - i, 0))

    return _call(
        body, name="attn_bwd", grid=(nb,),
        in_specs=[rev(D),
                  pl.BlockSpec((BLK, 256), lambda i: (0, 0)),
                  pl.BlockSpec((BLK, 256), lambda i: (jnp.maximum(nb - 2 - i, 0), 0)),
                  rev(256), rev(D),
                  pl.BlockSpec(memory_space=pltpu.SMEM), rev(384)],
        out_specs=[rev(QKV_W), pl.BlockSpec((8, BLK), lambda i: (0, 0))],
        out_shape=[jax.ShapeDtypeStruct((p, QKV_W), BF16), jax.ShapeDtypeStruct((8, BLK), F32)],
        scratch=[pltpu.VMEM((BLK, 256), F32), pltpu.VMEM((N_META, 256), F32)], sem="arbitrary",
        args=(q, kv, kv, kv, dao, sinks, tabs), comm=comm)


def _in_bwd(dqkv, da, dg, dgt, w_int, h0p, dh1, gain, comm=None):
    p = h0p.shape[0]
    tm = _row_tile(p)

    def body(dq_ref, da_ref, dg_ref, dt_ref, w_ref, h_ref, dh_ref, g_ref, o_ref, acc_ref):
        i = pl.program_id(0)

        @pl.when(i == 0)
        def _():
            acc_ref[...] = jnp.zeros_like(acc_ref)

        dn = (_dot(dq_ref[...], w_ref[0:QKV_W, :]) + _dot(da_ref[...], w_ref[QKV_W:QKV_W + D, :])
              + _dot(dg_ref[...], w_ref[QKV_W + D:QKV_W + 2 * D, :]) + _dot(dt_ref[...], w_ref[QKV_W + 2 * D:IN_W, :]))
        _, vjp = jax.vjp(_rms, h_ref[...], g_ref[...])
        dh, dgain = vjp(dn)
        o_ref[...] = dh_ref[...] + dh
        acc_ref[0:1, :] += dgain

    def row(w):
        return pl.BlockSpec((tm, w), lambda i: (i, 0))

    return _call(
        body, name="in_bwd", grid=(p // tm,),
        in_specs=[row(QKV_W), row(D), row(D), row(2 * D), VM, row(D), row(D), VM],
        out_specs=[row(D), pl.BlockSpec((8, D), lambda i: (0, 0))],
        out_shape=[jax.ShapeDtypeStruct((p, D), F32), jax.ShapeDtypeStruct((8, D), F32)],
        sem="arbitrary", args=(dqkv, da, dg, dgt, w_int, h0p, dh1, gain), comm=comm)


def _sum_slots(slots, name):
    r = slots.shape[0] // N_DEV
    cols = slots.shape[1]
    tr = r if r <= 352 else (r // 2 if (r // 2) % 16 == 0 else r // 3)
    steps = r // tr

    def body(*refs):
        acc = refs[0][...].astype(F32)
        for s in range(1, N_DEV):
            acc = acc + refs[s][...].astype(F32)
        refs[N_DEV][...] = acc

    return pl.pallas_call(
        body, name=name, grid=(steps,),
        in_specs=[pl.BlockSpec((tr, cols), functools.partial(lambda i, s: (s * steps + i, 0), s=s)) for s in range(N_DEV)],
        out_specs=pl.BlockSpec((tr, cols), lambda i: (i, 0)),
        out_shape=jax.ShapeDtypeStruct((r, cols), F32),
        compiler_params=_cparams("parallel"),
    )(*([slots] * N_DEV))


def _adamw_math(w, g, m, v):
    m_n = ADAM_B1 * m + (1.0 - ADAM_B1) * g
    v_n = ADAM_B2 * v + (1.0 - ADAM_B2) * jnp.square(g)
    m_hat = m_n / (1.0 - ADAM_B1 ** ADAM_STEP)
    v_hat = v_n / (1.0 - ADAM_B2 ** ADAM_STEP)
    return -ADAM_LR * (m_hat / (jnp.sqrt(v_hat) + ADAM_EPS) + ADAM_WD * w), m_n, v_n


def _sum_adamw(parts, w, m, v, name, nslots=N_DEV):
    r, cols = w.shape
    rs = r // len(parts)
    tr = rs if rs <= 352 else (rs // 2 if (rs // 2) % 16 == 0 else rs // 3)
    steps = rs // tr

    def body(*refs):
        w_ref, m_ref, v_ref, g_ref, d_ref, nm_ref, nv_ref = refs[nslots * len(parts):]
        i = pl.program_id(0)
        for q in range(len(parts)):
            @pl.when(i // steps == q)
            def _(q=q):
                g = refs[nslots * q][...].astype(F32)
                for s in range(1, nslots):
                    g = g + refs[nslots * q + s][...].astype(F32)
                g_ref[...] = g
                d_ref[...], nm_ref[...], nv_ref[...] = _adamw_math(w_ref[...], g, m_ref[...], v_ref[...])

    def slot_spec(q, s):
        return pl.BlockSpec((tr, cols), lambda i: (s * steps + jnp.clip(i - q * steps, 0, steps - 1), 0))

    spec = pl.BlockSpec((tr, cols), lambda i: (i, 0))
    return pl.pallas_call(
        body, name=name, grid=(steps * len(parts),),
        in_specs=[slot_spec(q, s) for q in range(len(parts)) for s in range(nslots)] + [spec] * 3,
        out_specs=[spec] * 4, out_shape=[jax.ShapeDtypeStruct((r, cols), F32)] * 4,
        compiler_params=_cparams("parallel"),
    )(*[a for a in parts for _ in range(nslots)], w, m, v)


def _adamw(w, g, m, v, name):
    r, cols = w.shape
    tr = 256 if r % 256 == 0 else r

    def body(w_ref, g_ref, m_ref, v_ref, d_ref, nm_ref, nv_ref):
        d_ref[...], nm_ref[...], nv_ref[...] = _adamw_math(w_ref[...], g_ref[...], m_ref[...], v_ref[...])

    spec = pl.BlockSpec((tr, cols), lambda i: (i, 0))
    return pl.pallas_call(
        body, name=name, grid=(r // tr,),
        in_specs=[spec] * 4, out_specs=[spec] * 3,
        out_shape=[jax.ShapeDtypeStruct((r, cols), F32)] * 3,
        compiler_params=_cparams("parallel"),
    )(w, g, m, v)


def _rope_tables(p):
    half = ROT_DIM // 2
    inv_freq = ROPE_THETA ** (-jnp.arange(half, dtype=F32) * 2.0 / ROT_DIM)
    pos = (jnp.arange(p) - PAD).astype(F32)
    ang = pos[:, None] * inv_freq[None, :]
    lane = jnp.arange(BLK)
    seg = (lane % HEAD_DIM) // half
    cos = jnp.cos(ang)[:, lane % half]
    sin = jnp.sin(ang)[:, lane % half]
    c = jnp.where(seg[None, :] < 2, cos, 1.0)
    s1 = jnp.where(seg[None, :] == 0, -sin, 0.0)
    s2 = jnp.where(seg[None, :] == 1, sin, 0.0)
    return jnp.concatenate([c, s1, s2], axis=1).astype(F32)


def _flat_pack(parts, rows):
    flat = jnp.concatenate([a.reshape(-1).astype(F32) for a in parts])
    return jnp.pad(flat, (0, rows * D - flat.shape[0])).reshape(rows, D)


def _flat_unpack(pack, shapes):
    flat = pack.reshape(-1)
    out, off = [], 0
    for s in shapes:
        size = 1
        for e in s:
            size *= e
        out.append(flat[off:off + size].reshape(s))
        off += size
    return out


def kernel(x, meta_tokens, norm_pre_mix, norm_post_mix, w_in, b_in, attn_sinks, w_attn_proj, conv_dw_w, conv_dw_b, conv_ln_g, conv_ln_b, w_conv_proj, b_conv_proj, w_out, norm_pre_ffn, norm_post_ffn, w_up, ffn_dw_w, ffn_dw_b, w_down, loss_target, m_meta_tokens, m_norm_pre_mix, m_norm_post_mix, m_w_in, m_b_in, m_attn_sinks, m_w_attn_proj, m_conv_dw_w, m_conv_dw_b, m_conv_ln_g, m_conv_ln_b, m_w_conv_proj, m_b_conv_proj, m_w_out, m_norm_pre_ffn, m_norm_post_ffn, m_w_up, m_ffn_dw_w, m_ffn_dw_b, m_w_down, v_meta_tokens, v_norm_pre_mix, v_norm_post_mix, v_w_in, v_b_in, v_attn_sinks, v_w_attn_proj, v_conv_dw_w, v_conv_dw_b, v_conv_ln_g, v_conv_ln_b, v_w_conv_proj, v_b_conv_proj, v_w_out, v_norm_pre_ffn, v_norm_post_ffn, v_w_up, v_ffn_dw_w, v_ffn_dw_b, v_w_down):
    seq = x.shape[1]
    p = seq + BLK
    me = 4 * lax.axis_index("x") + 2 * lax.axis_index("y") + lax.axis_index("c")
    in_cols = w_in.shape[2]
    up_cols = w_up.shape[2]

    small = jnp.zeros((56, up_cols), F32)
    small = small.at[0:N_META, 0:BLK].set(meta_tokens)
    small = small.at[16:16 + CONV_K, 0:BLK].set(conv_dw_w[0])
    small = small.at[48:48 + FFN_K, :].set(ffn_dw_w[0])
    w_int, small_all = _exchange(_Gather([w_in[0].T.astype(BF16), small]), "gather_w_in")
    small_all = small_all.reshape(N_DEV, 56, up_cols)
    meta_full = small_all[:, 0:N_META, 0:BLK].transpose(1, 0, 2).reshape(N_META, D)
    cdw = small_all[:, 16:16 + CONV_K, 0:BLK].transpose(1, 0, 2).reshape(CONV_K, D)
    cdw32 = jnp.pad(cdw, ((0, 32 - CONV_K), (0, 0)))
    fdw = small_all[:, 48:48 + FFN_K, :].transpose(1, 0, 2).reshape(FFN_K, 2 * FFN)

    tabs = _rope_tables(p)
    vecs = jnp.concatenate([conv_ln_g, conv_ln_b, b_conv_proj, norm_post_mix, norm_pre_ffn, jnp.zeros((3, D), F32)], axis=0)

    (h0p, n1, q, kv, ag, gates), (wa, wc, wo) = _in_proj(
        x[0], meta_full, norm_pre_mix, w_int, b_in, tabs,
        comm=_Gather([w_attn_proj[0].astype(BF16), w_conv_proj[0].astype(BF16), w_out[0].astype(BF16)]))
    (ao,), (w_upt,) = _attn_fwd(q, kv, attn_sinks, comm=_Gather([w_up[0].T.astype(BF16)]))
    (c0,), (wd,) = _conv31_fwd(ag, cdw32, conv_dw_b, comm=_Gather([w_down[0].astype(BF16)]))
    c1, attn, conv, merged, mix, h1, n2 = _mixer_fwd(ao, c0, gates, h0p, wa, wc, wo, vecs)
    u0 = _mm_nt(n2, w_upt, "ffn_up")
    act = _ffn_act(u0, fdw, ffn_dw_b)
    dffn, dact, dy, acc_f = _ffn_down_loss(act, wd, h1, loss_target[0], norm_post_ffn)

    (dug, duv, gfw_g, gfw_v, gfb_g, gfb_v, g_wd), _ = _ffn_act_bwd(u0, dact, fdw, ffn_dw_b, act, dffn)
    g_wupt, (s_wd,) = _mm_tn([dug, duv], n2, "grad_w_up", comm=_Scatter([g_wd]))
    (dh1, acc_u), (s_wup0,) = _ffn_in_bwd(dug, duv, w_upt, h1, dy, norm_pre_ffn, comm=_Scatter([g_wupt], 0, 2))
    (dmix, dat, dcv, dgt, dao, dc0, acc_m), (s_wup1,) = _mixer_bwd(
        dh1, mix, attn, conv, gates, c0, wa, wc, wo, vecs, comm=_Scatter([g_wupt], 1, 2))
    (da, dg, g_cdw, g_cdb, g_wo, g_wa, g_wc), _ = _conv31_bwd(ag, dc0, cdw32, [(merged, dmix), (ao, dat), (c1, dcv)])
    (dqkv, dsink), (s_wa, s_wc, s_wo) = _attn_bwd(q, kv, dao, attn_sinks, tabs, comm=_Scatter([g_wa, g_wc, g_wo]))
    g_wint, g_bin = _mm_tn([dqkv, da, dg, dgt], n1, "grad_w_in", col_sums=True)
    (from_sibling,) = _exchange(_SiblingSwap(g_wint), "swap_w_in")
    (dh0, acc_i), (s_win,) = _in_bwd(dqkv, da, dg, dgt, w_int, h0p, dh1, norm_pre_mix,
                                     comm=_ChipScatter(_pair_add(g_wint, from_sibling)))

    big = []
    for nm, parts, nslots, w, m, v, tr in (
            ("w_in", [s_win], N_CHIP, w_in, m_w_in, v_w_in, True), ("w_up", [s_wup0, s_wup1], N_DEV, w_up, m_w_up, v_w_up, True),
            ("w_attn_proj", [s_wa], N_DEV, w_attn_proj, m_w_attn_proj, v_w_attn_proj, False),
            ("w_conv_proj", [s_wc], N_DEV, w_conv_proj, m_w_conv_proj, v_w_conv_proj, False),
            ("w_out", [s_wo], N_DEV, w_out, m_w_out, v_w_out, False),
            ("w_down", [s_wd], N_DEV, w_down, m_w_down, v_w_down, False)):
        ins = [a[0].T if tr else a[0] for a in (w, m, v)]
        big.append(tuple((o.T if tr else o)[None] for o in _sum_adamw(parts, *ins, "update_" + nm, nslots)))

    loss_row = jnp.sum(acc_f[1:2, :], axis=1, keepdims=True)
    parts = [loss_row, dh0[PAD:BLK], acc_i[0:1], acc_m[0:1], g_bin, dsink[0:1, 0:16], g_cdw[0:CONV_K], g_cdb,
             acc_m[2:3], acc_m[3:4], acc_m[1:2], acc_u[0:1], acc_f[0:1],
             jnp.concatenate([gfw_g, gfw_v], axis=1), jnp.concatenate([gfb_g, gfb_v], axis=1)]
    shapes = [a.shape for a in parts]
    pack_rows = 88
    (gathered,) = _exchange(_Gather([_flat_pack(parts, pack_rows)]), "gather_small_grads")
    tot = _flat_unpack(_sum_slots(gathered, "sum_small_grads"), shapes)
    (loss, g_meta, g_npm, g_nqm, g_bi, g_sk, g_cw, g_cb, g_lg, g_lb, g_bc, g_npf, g_nqf, g_fw, g_fb) = tot
    loss = loss.reshape(())
    g_meta = lax.dynamic_slice_in_dim(g_meta, me * BLK, BLK, axis=1)
    g_cw = lax.dynamic_slice_in_dim(g_cw, me * BLK, BLK, axis=1)[None]
    g_fw = lax.dynamic_slice_in_dim(g_fw, me * up_cols, up_cols, axis=1)[None]

    sm_w = [meta_tokens, norm_pre_mix, norm_post_mix, b_in, attn_sinks, conv_dw_w, conv_dw_b, conv_ln_g, conv_ln_b,
            b_conv_proj, norm_pre_ffn, norm_post_ffn, ffn_dw_w, ffn_dw_b]
    sm_g = [g_meta, g_npm, g_nqm, g_bi, g_sk, g_cw, g_cb, g_lg, g_lb, g_bc, g_npf, g_nqf, g_fw, g_fb]
    sm_m = [m_meta_tokens, m_norm_pre_mix, m_norm_post_mix, m_b_in, m_attn_sinks, m_conv_dw_w, m_conv_dw_b, m_conv_ln_g,
            m_conv_ln_b, m_b_conv_proj, m_norm_pre_ffn, m_norm_post_ffn, m_ffn_dw_w, m_ffn_dw_b]
    sm_v = [v_meta_tokens, v_norm_pre_mix, v_norm_post_mix, v_b_in, v_attn_sinks, v_conv_dw_w, v_conv_dw_b, v_conv_ln_g,
            v_conv_ln_b, v_b_conv_proj, v_norm_pre_ffn, v_norm_post_ffn, v_ffn_dw_w, v_ffn_dw_b]
    sm_shapes = [a.shape for a in sm_w]
    upd_rows = 32
    v_pack = _flat_pack(sm_v, upd_rows)
    sm_out = _adamw(_flat_pack(sm_w, upd_rows), _flat_pack(sm_g, upd_rows), _flat_pack(sm_m, upd_rows), v_pack, "adamw_small")
    sm_d, sm_nm, sm_nv = (_flat_unpack(o, sm_shapes) for o in sm_out)

    order = ["meta_tokens", "norm_pre_mix", "norm_post_mix", "w_in", "b_in", "attn_sinks", "w_attn_proj", "conv_dw_w",
             "conv_dw_b", "conv_ln_g", "conv_ln_b", "w_conv_proj", "b_conv_proj", "w_out", "norm_pre_ffn", "norm_post_ffn",
             "w_up", "ffn_dw_w", "ffn_dw_b", "w_down"]
    small_names = ["meta_tokens", "norm_pre_mix", "norm_post_mix", "b_in", "attn_sinks", "conv_dw_w", "conv_dw_b", "conv_ln_g",
                   "conv_ln_b", "b_conv_proj", "norm_pre_ffn", "norm_post_ffn", "ffn_dw_w", "ffn_dw_b"]
    big_names = ["w_in", "w_up", "w_attn_proj", "w_conv_proj", "w_out", "w_down"]
    table = {}
    for k, nm in enumerate(small_names):
        table[nm] = (sm_g[k], sm_d[k], sm_nm[k], sm_nv[k])
    for k, nm in enumerate(big_names):
        table[nm] = big[k]
    grad_x = dh0[BLK:][None]
    outs = [loss, grad_x]
    for field in range(4):
        outs += [table[nm][field] for nm in order]
    return tuple(outs)
```

```python
import functools

import jax
import jax.numpy as jnp
from jax import lax
from jax.experimental import pallas as pl
from jax.experimental.pallas import tpu as pltpu

F32 = jnp.float32
BF16 = jnp.bfloat16
MESH = pl.DeviceIdType.MESH

D = 1024
HEAD_DIM = 64
N_META = 16
BLK = 128
PAD = BLK - N_META
CONV_K = 31
FFN = 2816
FFN_K = 3
QKV_W = 1280
IN_W = 5376
ROT_DIM = 16
ROPE_THETA = 500000.0
RMS_EPS = 1e-6
LN_EPS = 1e-5
NEG_INF = -1e30
SCALE = HEAD_DIM ** -0.5
N_DEV = 8

ADAM_LR = 0.001
ADAM_B1 = 0.9
ADAM_B2 = 0.999
ADAM_EPS = 1e-08
ADAM_WD = 0.01
ADAM_STEP = 10

VMEM_BYTES_V7X = 64 * 1024 * 1024
VMEM_LIMIT = VMEM_BYTES_V7X - 8 * 1024 * 1024

NT = (((1,), (1,)), ((), ()))
TN = (((0,), (0,)), ((), ()))
VM = pl.BlockSpec(memory_space=pltpu.VMEM)
ANY = pl.BlockSpec(memory_space=pl.ANY)


def _cparams(*sem):
    return pltpu.CompilerParams(dimension_semantics=sem or None, vmem_limit_bytes=VMEM_LIMIT)


def _row_tile(p):
    return 384 if p % 384 == 0 else 128


def _dot(a, b):
    return jnp.dot(a, b, preferred_element_type=F32)


def _dot_nt(a, b):
    return lax.dot_general(a, b, NT, preferred_element_type=F32)


def _dot_tn(a, b):
    return lax.dot_general(a, b, TN, preferred_element_type=F32)


def _rms(x, g):
    return x * lax.rsqrt(jnp.mean(x * x, axis=-1, keepdims=True) + RMS_EPS) * g


def _lnsilu(x, g, b):
    mu = jnp.mean(x, axis=-1, keepdims=True)
    var = jnp.mean(jnp.square(x - mu), axis=-1, keepdims=True)
    z = (x - mu) * lax.rsqrt(var + LN_EPS) * g + b
    return z * jax.nn.sigmoid(z)


def _rope(v, c, s1, s2):
    return v * c + pltpu.roll(v, BLK - 8, 1) * s1 + pltpu.roll(v, 8, 1) * s2


def _rows(i, tm):
    return i * tm + lax.broadcasted_iota(jnp.int32, (tm, 1), 0)


def _place():
    return lax.axis_index("x"), lax.axis_index("y"), lax.axis_index("c")


def _blk(ref, idx, r, dtype):
    return ref.at[pl.ds(pl.multiple_of(idx * r, 16 if dtype == BF16 else 8), r), :]


class _Gather:
    def __init__(self, arrs):
        self.ins = list(arrs)
        n = len(arrs)
        self.out_shape = [jax.ShapeDtypeStruct((N_DEV * a.shape[0], a.shape[1]), a.dtype) for a in arrs]
        self.scratch = [pltpu.SemaphoreType.DMA((n, 7)), pltpu.SemaphoreType.DMA((n, 7)), pltpu.SemaphoreType.DMA((n,))]

    def _parts(self, ins, outs, sems):
        send_sems, recv_sems, local_sems = sems
        n = len(ins)
        x, y, c = _place()
        me, sibling = (x, y, c), (x, y, 1 - c)
        chips = [(1 - x, y), (x, 1 - y), (1 - x, 1 - y)]

        def rows(a, p):
            return _blk(outs[a], 4 * p[0] + 2 * p[1] + p[2], self.ins[a].shape[0], self.ins[a].dtype)

        def copy(a, k, block, to, src=None):
            return pltpu.make_async_remote_copy(
                src_ref=rows(a, block) if src is None else src, dst_ref=rows(a, block),
                send_sem=send_sems.at[a, k], recv_sem=recv_sems.at[a, k], device_id=to, device_id_type=MESH)

        mine = [pltpu.make_async_copy(ins[a], rows(a, me), local_sems.at[a]) for a in range(n)]
        first = []
        for a in range(n):
            first.append(copy(a, 0, me, sibling, src=ins[a]))
            first += [copy(a, 1 + j, me, (*chip, c), src=ins[a]) for j, chip in enumerate(chips)]
        return n, c, me, sibling, chips, copy, mine, first

    def start(self, ins, outs, sems):
        *_, mine, first = self._parts(ins, outs, sems)
        for cp in mine + first:
            cp.start()

    def finish(self, ins, outs, sems):
        n, c, me, sibling, chips, copy, mine, first = self._parts(ins, outs, sems)
        passed = []
        for j, chip in enumerate(chips):
            for a in range(n):
                copy(a, 1 + j, (*chip, c), me).wait_recv()
                fwd = copy(a, 4 + j, (*chip, c), sibling)
                fwd.start()
                passed.append(fwd)
        for a in range(n):
            copy(a, 0, sibling, me).wait_recv()
            for j, chip in enumerate(chips):
                copy(a, 4 + j, (*chip, 1 - c), me).wait_recv()
        for cp in first + passed:
            cp.wait_send()
        for cp in mine:
            cp.wait()


FLIPS = [(0, 0, 1), (1, 0, 0), (0, 1, 0), (1, 1, 0), (1, 0, 1), (0, 1, 1), (1, 1, 1)]


class _Scatter:
    def __init__(self, arrs, part=0, nparts=1):
        self.ins = list(arrs)
        self.part, self.nparts = part, nparts
        n = len(arrs)
        self.out_shape = [jax.ShapeDtypeStruct((a.shape[0] // nparts, a.shape[1]), a.dtype) for a in arrs]
        self.scratch = [pltpu.SemaphoreType.DMA((n, 7)), pltpu.SemaphoreType.DMA((n, 7)), pltpu.SemaphoreType.DMA((n,))]

    def _parts(self, ins, outs, sems):
        send_sems, recv_sems, local_sems = sems
        n = len(ins)
        x, y, c = _place()
        me = 4 * x + 2 * y + c

        def flip(v, f):
            return 1 - v if f else v

        def src(a, idx):
            r = self.ins[a].shape[0] // N_DEV
            rs = r // self.nparts
            return ins[a].at[pl.ds(pl.multiple_of(idx * r + self.part * rs, 16), rs), :]

        def dst(a, idx):
            rs = self.ins[a].shape[0] // N_DEV // self.nparts
            return outs[a].at[pl.ds(pl.multiple_of(idx * rs, 16), rs), :]

        mine = [pltpu.make_async_copy(src(a, me), dst(a, me), local_sems.at[a]) for a in range(n)]
        sends, recvs = [], []
        for k, f in enumerate(FLIPS):
            peer = (flip(x, f[0]), flip(y, f[1]), flip(c, f[2]))
            pidx = 4 * peer[0] + 2 * peer[1] + peer[2]
            for a in range(n):
                sends.append(pltpu.make_async_remote_copy(
                    src_ref=src(a, pidx), dst_ref=dst(a, me),
                    send_sem=send_sems.at[a, k], recv_sem=recv_sems.at[a, k], device_id=peer, device_id_type=MESH))
                recvs.append(functools.partial(
                    pltpu.make_async_remote_copy,
                    src_ref=src(a, pidx), dst_ref=dst(a, pidx),
                    send_sem=send_sems.at[a, k], recv_sem=recv_sems.at[a, k], device_id=peer, device_id_type=MESH))
        return mine, sends, recvs

    def start(self, ins, outs, sems):
        mine, sends, _ = self._parts(ins, outs, sems)
        for cp in mine + sends:
            cp.start()

    def finish(self, ins, outs, sems):
        mine, sends, recvs = self._parts(ins, outs, sems)
        for make in recvs:
            make().wait_recv()
        for cp in sends:
            cp.wait_send()
        for cp in mine:
            cp.wait()


N_CHIP = 4


class _SiblingSwap:
    def __init__(self, arr):
        self.ins = [arr]
        self.r = arr.shape[0] // N_DEV
        self.out_shape = [jax.ShapeDtypeStruct((N_CHIP * self.r, arr.shape[1]), arr.dtype)]
        self.scratch = [pltpu.SemaphoreType.DMA((N_CHIP,)), pltpu.SemaphoreType.DMA((N_CHIP,))]

    def _copies(self, ins, outs, sems):
        send_sems, recv_sems = sems
        x, y, c = _place()
        r = self.r
        return [pltpu.make_async_remote_copy(
            src_ref=ins[0].at[pl.ds(pl.multiple_of((2 * j + 1 - c) * r, 16), r), :],
            dst_ref=outs[0].at[pl.ds(j * r, r), :],
            send_sem=send_sems.at[j], recv_sem=recv_sems.at[j], device_id=(x, y, 1 - c), device_id_type=MESH)
            for j in range(N_CHIP)]

    def start(self, ins, outs, sems):
        for cp in self._copies(ins, outs, sems):
            cp.start()

    def finish(self, ins, outs, sems):
        for cp in self._copies(ins, outs, sems):
            cp.wait()


class _ChipScatter:
    def __init__(self, arr):
        self.ins = [arr]
        self.r = arr.shape[0] // N_CHIP
        self.out_shape = [jax.ShapeDtypeStruct(arr.shape, arr.dtype)]
        self.scratch = [pltpu.SemaphoreType.DMA((3,)), pltpu.SemaphoreType.DMA((3,)), pltpu.SemaphoreType.DMA]

    def _parts(self, ins, outs, sems):
        send_sems, recv_sems, local_sem = sems
        x, y, c = _place()
        r = self.r
        my_chip = 2 * x + y

        def rows(ref, j):
            return ref.at[pl.ds(pl.multiple_of(j * r, 16), r), :]

        mine = pltpu.make_async_copy(rows(ins[0], my_chip), rows(outs[0], my_chip), local_sem)
        sends, recvs = [], []
        for k, (fx, fy) in enumerate(((1, 0), (0, 1), (1, 1))):
            px, py = (1 - x if fx else x), (1 - y if fy else y)
            peer_chip = 2 * px + py
            sends.append(pltpu.make_async_remote_copy(
                src_ref=rows(ins[0], peer_chip), dst_ref=rows(outs[0], my_chip),
                send_sem=send_sems.at[k], recv_sem=recv_sems.at[k], device_id=(px, py, c), device_id_type=MESH))
            recvs.append(functools.partial(
                pltpu.make_async_remote_copy,
                src_ref=rows(ins[0], peer_chip), dst_ref=rows(outs[0], peer_chip),
                send_sem=send_sems.at[k], recv_sem=recv_sems.at[k], device_id=(px, py, c), device_id_type=MESH))
        return mine, sends, recvs

    def start(self, ins, outs, sems):
        mine, sends, _ = self._parts(ins, outs, sems)
        for cp in [mine] + sends:
            cp.start()

    def finish(self, ins, outs, sems):
        mine, sends, recvs = self._parts(ins, outs, sems)
        for make in recvs:
            make().wait_recv()
        for cp in sends:
            cp.wait_send()
        mine.wait()


def _pair_add(partial, recv):
    r = recv.shape[0] // N_CHIP
    cols = recv.shape[1]
    tr = r // 2 if (r // 2) % 16 == 0 else r
    steps = r // tr
    core = lax.axis_index("c").astype(jnp.int32).reshape(1)

    def body(c_ref, p_ref, s_ref, o_ref):
        o_ref[...] = (p_ref[...].astype(F32) + s_ref[...].astype(F32)).astype(BF16)

    spec = pl.BlockSpec((tr, cols), lambda j, i, c_ref: (j * steps + i, 0))
    return pl.pallas_call(
        body, name="pair_add",
        grid_spec=pltpu.PrefetchScalarGridSpec(
            num_scalar_prefetch=1, grid=(N_CHIP, steps),
            in_specs=[pl.BlockSpec((tr, cols), lambda j, i, c_ref: ((2 * j + c_ref[0]) * steps + i, 0)), spec],
            out_specs=spec),
        out_shape=jax.ShapeDtypeStruct(recv.shape, BF16),
        compiler_params=_cparams("parallel", "parallel"),
    )(core, partial, recv)


def _exchange(comm, name):
    n, m = len(comm.ins), len(comm.out_shape)

    def body(*refs):
        ins, outs, sems = refs[:n], refs[n:n + m], refs[n + m:]
        comm.start(ins, outs, sems)
        comm.finish(ins, outs, sems)

    return pl.pallas_call(
        body, name=name, out_shape=comm.out_shape, in_specs=[ANY] * n, out_specs=[ANY] * m, scratch_shapes=comm.scratch,
    )(*comm.ins)


def _call(body, *, name, grid, in_specs, out_specs, out_shape, args, scratch=(), sem="parallel", comm=None):
    if comm is None:
        outs = pl.pallas_call(
            body, name=name, grid=grid, in_specs=list(in_specs), out_specs=list(out_specs), out_shape=list(out_shape),
            scratch_shapes=list(scratch), compiler_params=_cparams(sem))(*args)
        return outs, []
    n_in, n_out, n_sc = len(in_specs), len(out_specs), len(scratch)
    n_ci, n_co = len(comm.ins), len(comm.out_shape)
    last = grid[0] - 1

    def fused(*refs):
        ins, refs = refs[:n_in], refs[n_in:]
        c_ins, refs = refs[:n_ci], refs[n_ci:]
        outs, refs = refs[:n_out], refs[n_out:]
        c_outs, refs = refs[:n_co], refs[n_co:]
        sc, c_sems = refs[:n_sc], refs[n_sc:]
        step = pl.program_id(0)

        @pl.when(step == 0)
        def _():
            comm.start(c_ins, c_outs, c_sems)

        body(*ins, *outs, *sc)

        @pl.when(step == last)
        def _():
            comm.finish(c_ins, c_outs, c_sems)

    outs = pl.pallas_call(
        fused, name=name, grid=grid, in_specs=list(in_specs) + [ANY] * n_ci, out_specs=list(out_specs) + [ANY] * n_co,
        out_shape=list(out_shape) + comm.out_shape, scratch_shapes=list(scratch) + comm.scratch,
        compiler_params=_cparams("arbitrary"))(*args, *comm.ins)
    return outs[:n_out], outs[n_out:]


def _token_specs(tm):
    k = tm // BLK
    return [pl.BlockSpec((BLK, D), functools.partial(lambda i, t: (jnp.maximum(k * i + t - 1, 0), 0), t=t)) for t in range(k)]


def _in_proj(x2d, meta, gain, w_int, b_in, tabs, comm=None):
    p = x2d.shape[0] + BLK
    tm = _row_tile(p)
    k = tm // BLK

    def body(*refs):
        x_refs = refs[:k]
        m_ref, g_ref, w_ref, b_ref, t_ref, h_ref, n1_ref, q_ref, kv_ref, ag_ref, gt_ref = refs[k:]
        i = pl.program_id(0)
        head = jnp.concatenate([jnp.zeros((PAD, D), F32), m_ref[...]], axis=0)
        first = jnp.where(i == 0, head, x_refs[0][...])
        h = jnp.concatenate([first] + [r[...] for r in x_refs[1:]], axis=0) if k > 1 else first
        h_ref[...] = h
        n = _rms(h, g_ref[...]).astype(BF16)
        n1_ref[...] = n
        c, s1, s2 = t_ref[:, 0:128], t_ref[:, 128:256], t_ref[:, 256:384]

        def mm(c0, w):
            return _dot_nt(n, w_ref[c0:c0 + w, :]) + b_ref[:, c0:c0 + w]

        for j in range(4):
            acc = mm(256 * j, 256)
            for t in range(2):
                lo = 256 * j + 128 * t
                q_ref[:, lo:lo + 128] = (_rope(acc[:, 128 * t:128 * (t + 1)], c, s1, s2) * SCALE).astype(BF16)
        acc = mm(1024, 256)
        kv_ref[:, 0:128] = _rope(acc[:, 0:128], c, s1, s2).astype(BF16)
        kv_ref[:, 128:256] = acc[:, 128:256].astype(BF16)
        for j in range(8):
            ag_ref[:, 256 * j:256 * (j + 1)] = mm(QKV_W + 256 * j, 256).astype(BF16)
        for j in range(8):
            gt_ref[:, 256 * j:256 * (j + 1)] = mm(QKV_W + 2048 + 256 * j, 256).astype(BF16)

    def row(w):
        return pl.BlockSpec((tm, w), lambda i: (i, 0))

    return _call(
        body, name="in_proj", grid=(p // tm,),
        in_specs=_token_specs(tm) + [VM, VM, VM, VM, row(384)],
        out_specs=[row(D), row(D), row(D), row(256), row(2048), row(2048)],
        out_shape=[jax.ShapeDtypeStruct((p, D), F32)] + [jax.ShapeDtypeStruct((p, w), BF16) for w in (D, D, 256, 2048, 2048)],
        args=(x2d,) * k + (meta, gain, w_int, b_in, tabs), comm=comm)


N_KEY = 2 * BLK + N_META


def _attn_setup(n, h, q_ref, km_ref, kp_ref, kc_ref):
    lo = lax.broadcasted_iota(jnp.int32, (BLK, BLK), 1) < HEAD_DIM
    lok = lax.broadcasted_iota(jnp.int32, (N_KEY, BLK), 1) < HEAD_DIM

    def dup(lanes):
        cat = jnp.concatenate([kp_ref[:, lanes], kc_ref[:, lanes], km_ref[PAD:BLK, lanes]], axis=0).astype(F32)
        rolled = pltpu.roll(cat, HEAD_DIM, 1)
        return (jnp.where(lok, cat, rolled) if h == 0 else jnp.where(lok, rolled, cat)).astype(BF16)

    k2 = dup(slice(0, 128))
    v2 = dup(slice(128, 256))
    qs = _stack_heads(q_ref, h, lo)

    kr = lax.broadcasted_iota(jnp.int32, (BLK, BLK), 0)
    tq = BLK * n + lax.broadcasted_iota(jnp.int32, (BLK, BLK), 1) - PAD
    t_p = BLK * (n - 1) + kr - PAD
    t_c = BLK * n + kr - PAD
    ok_p = jnp.logical_and(t_p >= N_META, tq - t_p < BLK)
    ok_c = jnp.logical_and(t_c >= N_META, t_c <= tq)
    ok_m = lax.broadcasted_iota(jnp.int32, (N_META, BLK), 0) <= BLK * n + lax.broadcasted_iota(jnp.int32, (N_META, BLK), 1) - PAD
    bias = jnp.concatenate([jnp.where(ok, 0.0, NEG_INF).astype(F32) for ok in (ok_p, ok_c, ok_m)], axis=0)
    return qs, k2, v2, bias, lok


def _attn_head(s, bias, sink):
    s = s + bias
    m = jnp.maximum(jnp.max(s, axis=0, keepdims=True), sink)
    e = jnp.exp(s - m)
    es = jnp.exp(sink - m)
    inv = 1.0 / (jnp.sum(e, axis=0, keepdims=True) + es)
    return e * inv, es * inv


def _stack_heads(ref, h, lo):
    pieces = []
    for jp in range(4):
        v = ref[:, BLK * (4 * h + jp):BLK * (4 * h + jp + 1)]
        zero = jnp.zeros_like(v)
        pieces += [jnp.where(lo, v, zero), jnp.where(lo, zero, v)]
    return jnp.concatenate(pieces, axis=0)


def _unstack_heads(v, jp, lo):
    return jnp.where(lo, v[256 * jp:256 * jp + 128], v[256 * jp + 128:256 * jp + 256])


def _attn_fwd(q, kv, sinks, comm=None):
    p = q.shape[0]
    nb = p // BLK

    def body(q_ref, km_ref, kp_ref, kc_ref, sink_ref, o_ref):
        n = pl.program_id(0)
        lo = lax.broadcasted_iota(jnp.int32, (BLK, BLK), 1) < HEAD_DIM
        for h in range(2):
            qs, k2, v2, bias, _ = _attn_setup(n, h, q_ref, km_ref, kp_ref, kc_ref)
            st = _dot_nt(k2, qs)
            pt = jnp.concatenate(
                [_attn_head(st[:, BLK * g:BLK * (g + 1)], bias, sink_ref[0, 8 * h + g])[0].astype(BF16) for g in range(8)],
                axis=1)
            o = _dot_tn(pt, v2)
            for jp in range(4):
                o_ref[:, BLK * (4 * h + jp):BLK * (4 * h + jp + 1)] = _unstack_heads(o, jp, lo).astype(BF16)

    return _call(
        body, name="attn_fwd", grid=(nb,),
        in_specs=[pl.BlockSpec((BLK, D), lambda i: (i, 0)),
                  pl.BlockSpec((BLK, 256), lambda i: (0, 0)),
                  pl.BlockSpec((BLK, 256), lambda i: (jnp.maximum(i - 1, 0), 0)),
                  pl.BlockSpec((BLK, 256), lambda i: (i, 0)),
                  pl.BlockSpec(memory_space=pltpu.SMEM)],
        out_specs=[pl.BlockSpec((BLK, D), lambda i: (i, 0))],
        out_shape=[jax.ShapeDtypeStruct((p, D), BF16)],
        args=(q, kv, kv, kv, sinks), comm=comm)


def _conv31_fwd(ag, w32, b, comm=None):
    p = ag.shape[0]
    nch = p // BLK

    def body(a_ref, g_ref, w_ref, b_ref, o_ref, gp):
        gp[0:32, :] = jnp.zeros((32, BLK), F32)
        for ci in range(nch):
            r0 = BLK * ci
            glu = a_ref[r0:r0 + BLK, :].astype(F32) * jax.nn.sigmoid(g_ref[r0:r0 + BLK, :].astype(F32))
            if ci == 0:
                glu = jnp.where(_rows(0, BLK) >= PAD, glu, 0.0)
            gp[32 + r0:32 + r0 + BLK, :] = glu
        for ci in range(nch):
            r0 = BLK * ci
            acc = jnp.broadcast_to(b_ref[...], (BLK, BLK))
            for j in range(CONV_K):
                acc = acc + w_ref[j:j + 1, :] * gp[r0 + j + 2:r0 + j + 2 + BLK, :]
            o_ref[r0:r0 + BLK, :] = acc

    return _call(
        body, name="conv31_fwd", grid=(D // BLK,),
        in_specs=[pl.BlockSpec((p, BLK), lambda j: (0, j)), pl.BlockSpec((p, BLK), lambda j: (0, 8 + j)),
                  pl.BlockSpec((32, BLK), lambda j: (0, j)), pl.BlockSpec((1, BLK), lambda j: (0, j))],
        out_specs=[pl.BlockSpec((p, BLK), lambda j: (0, j))],
        out_shape=[jax.ShapeDtypeStruct((p, D), F32)],
        scratch=[pltpu.VMEM((p + 32, BLK), F32)],
        args=(ag, ag, w32, b), comm=comm)


def _mixer_fwd(ao, c0, gates, h0p, wa, wc, wo, vecs):
    p = ao.shape[0]
    tm = _row_tile(p)

    def body(ao_ref, c0_ref, gt_ref, h_ref, wa_ref, wc_ref, wo_ref, v_ref,
             c1_ref, at_ref, cv_ref, mg_ref, mix_ref, h1_ref, n2_ref):
        i = pl.program_id(0)
        c1 = _lnsilu(c0_ref[...], v_ref[0:1, :], v_ref[1:2, :]).astype(BF16)
        c1_ref[...] = c1
        attn = _dot(ao_ref[...], wa_ref[...])
        conv = _dot(c1, wc_ref[...]) + v_ref[2:3, :]
        at_ref[...] = attn.astype(BF16)
        cv_ref[...] = conv.astype(BF16)
        merged = (jax.nn.sigmoid(gt_ref[:, 0:D].astype(F32)) * attn
                  + jax.nn.sigmoid(gt_ref[:, D:2 * D].astype(F32)) * conv).astype(BF16)
        mg_ref[...] = merged
        mix = _dot(merged, wo_ref[...])
        mix_ref[...] = mix
        h1 = jnp.where(_rows(i, tm) >= PAD, h_ref[...] + _rms(mix, v_ref[3:4, :]), 0.0)
        h1_ref[...] = h1
        n2_ref[...] = _rms(h1, v_ref[4:5, :]).astype(BF16)

    def row(w):
        return pl.BlockSpec((tm, w), lambda i: (i, 0))

    return pl.pallas_call(
        body, name="mixer_fwd", grid=(p // tm,),
        in_specs=[row(D), row(D), row(2 * D), row(D), VM, VM, VM, VM],
        out_specs=[row(D)] * 7,
        out_shape=[jax.ShapeDtypeStruct((p, D), t) for t in (BF16, BF16, BF16, BF16, F32, F32, BF16)],
        compiler_params=_cparams("parallel"),
    )(ao, c0, gates, h0p, wa, wc, wo, vecs)


HALO = 16


def _ffn_up_act(n2, w_upt, fw, fb):
    p = n2.shape[0]
    tm = _row_tile(p)
    ch = 256

    def body(a_ref, h_ref, w_ref, fw_ref, fb_ref, u_ref, act_ref, shifted):
        i = pl.program_id(0)
        halo = h_ref[...]
        a_ext = jnp.concatenate([jnp.where(i == 0, jnp.zeros_like(halo), halo), a_ref[...]], axis=0)
        for k in range(FFN // ch):
            us = []
            for s_ref, c0 in ((shifted.at[2 * k], ch * k), (shifted.at[2 * k + 1], FFN + ch * k)):
                u = _dot_nt(a_ext, w_ref[c0:c0 + ch, :]).astype(BF16)
                u_ref[:, c0:c0 + ch] = u[HALO:]
                s_ref[...] = u.astype(F32)
                us.append(fb_ref[:, c0:c0 + ch]
                          + fw_ref[0:1, c0:c0 + ch] * s_ref[HALO - 2:HALO - 2 + tm, :]
                          + fw_ref[1:2, c0:c0 + ch] * s_ref[HALO - 1:HALO - 1 + tm, :]
                          + fw_ref[2:3, c0:c0 + ch] * s_ref[HALO:HALO + tm, :])
            ug, uv = us
            act_ref[:, ch * k:ch * (k + 1)] = (ug * jax.nn.sigmoid(ug) * uv).astype(BF16)

    return pl.pallas_call(
        body, name="ffn_up_act", grid=(p // tm,),
        in_specs=[pl.BlockSpec((tm, D), lambda i: (i, 0)),
                  pl.BlockSpec((HALO, D), lambda i: (jnp.maximum(i * (tm // HALO) - 1, 0), 0)), VM, VM, VM],
        out_specs=[pl.BlockSpec((tm, 2 * FFN), lambda i: (i, 0)), pl.BlockSpec((tm, FFN), lambda i: (i, 0))],
        out_shape=[jax.ShapeDtypeStruct((p, 2 * FFN), BF16), jax.ShapeDtypeStruct((p, FFN), BF16)],
        scratch_shapes=[pltpu.VMEM((2 * FFN // ch, tm + HALO, ch), F32)],
        compiler_params=_cparams("parallel"),
    )(n2, n2, w_upt, fw, fb)


def _conv3(xp_ref, w_ref, r0):
    return (w_ref[0:1, :] * xp_ref[r0 + 6:r0 + 6 + BLK, :] + w_ref[1:2, :] * xp_ref[r0 + 7:r0 + 7 + BLK, :]
            + w_ref[2:3, :] * xp_ref[r0 + 8:r0 + 8 + BLK, :])


def _ffn_slab_specs(p):
    ncol = FFN // BLK
    return [pl.BlockSpec((p, BLK), lambda j: (0, j)), pl.BlockSpec((p, BLK), lambda j: (0, ncol + j)),
            pl.BlockSpec((FFN_K, BLK), lambda j: (0, j)), pl.BlockSpec((FFN_K, BLK), lambda j: (0, ncol + j)),
            pl.BlockSpec((1, BLK), lambda j: (0, j)), pl.BlockSpec((1, BLK), lambda j: (0, ncol + j))]


def _fill_shifted(dst, src_ref, nch):
    dst[0:8, :] = jnp.zeros((8, BLK), F32)
    for ci in range(nch):
        dst[8 + BLK * ci:8 + BLK * (ci + 1), :] = src_ref[BLK * ci:BLK * (ci + 1), :].astype(F32)


def _ffn_down_loss(act, wd, h1, tgt, gain):
    p = act.shape[0]
    tm = _row_tile(p)
    k = tm // BLK

    def body(*refs):
        a_ref, w_ref, h_ref = refs[:3]
        t_refs = refs[3:3 + k]
        g_ref, df_ref, da_ref, dy_ref, acc_ref = refs[3 + k:]
        i = pl.program_id(0)

        @pl.when(i == 0)
        def _():
            acc_ref[...] = jnp.zeros_like(acc_ref)

        ffn = _dot(a_ref[...], w_ref[...])
        r, vjp = jax.vjp(_rms, ffn, g_ref[...])
        t = jnp.concatenate([t_ref[...] for t_ref in t_refs], axis=0) if k > 1 else t_refs[0][...]
        diff = jnp.where(_rows(i, tm) >= BLK, h_ref[...] + r - t, 0.0)
        dy = diff * (1.0 / D)
        dffn, dg = vjp(dy)
        acc_ref[0:1, :] += dg
        acc_ref[1:2, :] += jnp.sum(diff * diff, axis=0, keepdims=True) * (0.5 / D)
        dy_ref[...] = dy
        dfb = dffn.astype(BF16)
        df_ref[...] = dfb
        for c0 in range(0, FFN, 256):
            da_ref[:, c0:c0 + 256] = _dot_nt(dfb, w_ref[c0:c0 + 256, :]).astype(BF16)

    def row(w):
        return pl.BlockSpec((tm, w), lambda i: (i, 0))

    return pl.pallas_call(
        body, name="ffn_down_loss", grid=(p // tm,),
        in_specs=[row(FFN), VM, row(D)] + _token_specs(tm) + [VM],
        out_specs=[row(D), row(FFN), row(D), pl.BlockSpec((8, D), lambda i: (0, 0))],
        out_shape=[jax.ShapeDtypeStruct((p, D), BF16), jax.ShapeDtypeStruct((p, FFN), BF16),
                   jax.ShapeDtypeStruct((p, D), F32), jax.ShapeDtypeStruct((8, D), F32)],
        compiler_params=_cparams("arbitrary"),
    )(act, wd, h1, *([tgt] * k), gain)


def _mm_tn(pieces, b, name, col_sums=False, comm=None):
    p, n = b.shape
    tk = 256
    nblk = [a.shape[1] // tk for a in pieces]
    offs = [sum(nblk[:q]) for q in range(len(pieces))]
    total = sum(nblk)
    npc = len(pieces)

    def body(*refs):
        a_refs, b_ref, o_ref = refs[:npc], refs[npc], refs[npc + 1]
        i = pl.program_id(0)
        for q, a_ref in enumerate(a_refs):
            @pl.when(jnp.logical_and(i >= offs[q], i < offs[q] + nblk[q]))
            def _(a_ref=a_ref):
                a_v = a_ref[...]
                o_ref[...] = _dot_tn(a_v, b_ref[...]).astype(BF16)
                if col_sums:
                    refs[npc + 2][...] = jnp.sum(a_v.astype(F32), axis=0, keepdims=True)

    def a_spec(q):
        return pl.BlockSpec((p, tk), lambda i: (0, jnp.clip(i - offs[q], 0, nblk[q] - 1)))

    out_specs = [pl.BlockSpec((tk, n), lambda i: (i, 0))]
    out_shape = [jax.ShapeDtypeStruct((total * tk, n), BF16)]
    if col_sums:
        out_specs.append(pl.BlockSpec((1, tk), lambda i: (0, i)))
        out_shape.append(jax.ShapeDtypeStruct((1, total * tk), F32))
    res, sent = _call(
        body, name=name, grid=(total,),
        in_specs=[a_spec(q) for q in range(npc)] + [VM],
        out_specs=out_specs, out_shape=out_shape, args=(*pieces, b), comm=comm)
    res = res if col_sums else res[0]
    return res if comm is None else (res, sent)


def _ffn_act_bwd(u0, dact, fw, fb, act, dffn, comm=None):
    p = u0.shape[0]
    nch = p // BLK
    ncol = FFN // BLK

    def body(g_ref, v_ref, wg_ref, wv_ref, bg_ref, bv_ref, da_ref, act_ref, df_ref,
             dg_ref, dv_ref, gwg_ref, gwv_ref, gbg_ref, gbv_ref, gwd_ref, xg, xv, eg, ev):
        gwd_ref[...] = _dot_tn(act_ref[...], df_ref[...]).astype(BF16)
        _fill_shifted(xg, g_ref, nch)
        _fill_shifted(xv, v_ref, nch)
        eg[p:p + 8, :] = jnp.zeros((8, BLK), F32)
        ev[p:p + 8, :] = jnp.zeros((8, BLK), F32)
        for ci in range(nch):
            r0 = BLK * ci
            ug = _conv3(xg, wg_ref, r0) + bg_ref[...]
            uv = _conv3(xv, wv_ref, r0) + bv_ref[...]
            sg = jax.nn.sigmoid(ug)
            d = da_ref[r0:r0 + BLK, :].astype(F32)
            eg[r0:r0 + BLK, :] = d * uv * (sg * (1.0 + ug * (1.0 - sg)))
            ev[r0:r0 + BLK, :] = d * ug * sg
        for e_s, x_s, w_ref, d_ref, gw_ref, gb_ref in ((eg, xg, wg_ref, dg_ref, gwg_ref, gbg_ref),
                                                      (ev, xv, wv_ref, dv_ref, gwv_ref, gbv_ref)):
            sums = [jnp.zeros((BLK, BLK), F32) for _ in range(FFN_K + 1)]
            for ci in range(nch):
                r0 = BLK * ci
                e0 = e_s[r0:r0 + BLK, :]
                du = (w_ref[2:3, :] * e0 + w_ref[1:2, :] * e_s[r0 + 1:r0 + 1 + BLK, :]
                      + w_ref[0:1, :] * e_s[r0 + 2:r0 + 2 + BLK, :])
                if ci == 0:
                    du = jnp.where(_rows(0, BLK) >= PAD, du, 0.0)
                d_ref[r0:r0 + BLK, :] = du.astype(BF16)
                for j in range(FFN_K):
                    sums[j] = sums[j] + e0 * x_s[r0 + 6 + j:r0 + 6 + j + BLK, :]
                sums[FFN_K] = sums[FFN_K] + e0
            for j in range(FFN_K):
                gw_ref[j:j + 1, :] = jnp.sum(sums[j], axis=0, keepdims=True)
            gb_ref[...] = jnp.sum(sums[FFN_K], axis=0, keepdims=True)

    slab = pl.BlockSpec((p, BLK), lambda j: (0, j))
    wspec = pl.BlockSpec((FFN_K, BLK), lambda j: (0, j))
    bspec = pl.BlockSpec((1, BLK), lambda j: (0, j))
    return _call(
        body, name="ffn_act_bwd", grid=(ncol,),
        in_specs=_ffn_slab_specs(p) + [slab, slab, VM],
        out_specs=[slab, slab, wspec, wspec, bspec, bspec, pl.BlockSpec((BLK, D), lambda j: (j, 0))],
        out_shape=[jax.ShapeDtypeStruct((p, FFN), BF16)] * 2 + [jax.ShapeDtypeStruct((FFN_K, FFN), F32)] * 2
        + [jax.ShapeDtypeStruct((1, FFN), F32)] * 2 + [jax.ShapeDtypeStruct((FFN, D), BF16)],
        scratch=[pltpu.VMEM((p + 8, BLK), F32)] * 4,
        args=(u0, u0, fw, fw, fb, fb, dact, act, dffn), comm=comm)


def _ffn_in_bwd(dug, duv, w_upt, h1, dy, gain, comm=None):
    p = h1.shape[0]
    tm = _row_tile(p)

    def body(dg_ref, dv_ref, w_ref, h_ref, dy_ref, g_ref, o_ref, acc_ref):
        i = pl.program_id(0)

        @pl.when(i == 0)
        def _():
            acc_ref[...] = jnp.zeros_like(acc_ref)

        dn = _dot(dg_ref[...], w_ref[0:FFN, :]) + _dot(dv_ref[...], w_ref[FFN:2 * FFN, :])
        _, vjp = jax.vjp(_rms, h_ref[...], g_ref[...])
        dh, dg = vjp(dn)
        o_ref[...] = dy_ref[...] + dh
        acc_ref[0:1, :] += dg

    def row(w):
        return pl.BlockSpec((tm, w), lambda i: (i, 0))

    return _call(
        body, name="ffn_in_bwd", grid=(p // tm,),
        in_specs=[row(FFN), row(FFN), VM, row(D), row(D), VM],
        out_specs=[row(D), pl.BlockSpec((8, D), lambda i: (0, 0))],
        out_shape=[jax.ShapeDtypeStruct((p, D), F32), jax.ShapeDtypeStruct((8, D), F32)],
        sem="arbitrary", args=(dug, duv, w_upt, h1, dy, gain), comm=comm)


def _mixer_bwd(dh1, mix, attn, conv, gates, c0, wa, wc, wo, vecs, comm=None):
    p = dh1.shape[0]
    tm = _row_tile(p)

    def body(dh_ref, mix_ref, at_ref, cv_ref, gt_ref, c0_ref, wa_ref, wc_ref, wo_ref, v_ref,
             dmix_ref, dat_ref, dcv_ref, dgt_ref, dao_ref, dc0_ref, acc_ref):
        i = pl.program_id(0)

        @pl.when(i == 0)
        def _():
            acc_ref[...] = jnp.zeros_like(acc_ref)

        _, vjp = jax.vjp(_rms, mix_ref[...], v_ref[3:4, :])
        dmix, dgp = vjp(dh_ref[...])
        dmix = dmix.astype(BF16)
        dmix_ref[...] = dmix
        dmg = _dot_nt(dmix, wo_ref[...])
        sa = jax.nn.sigmoid(gt_ref[:, 0:D].astype(F32))
        sc = jax.nn.sigmoid(gt_ref[:, D:2 * D].astype(F32))
        dat = dmg * sa
        dcv = dmg * sc
        dgt_ref[:, 0:D] = (dmg * at_ref[...].astype(F32) * sa * (1.0 - sa)).astype(BF16)
        dgt_ref[:, D:2 * D] = (dmg * cv_ref[...].astype(F32) * sc * (1.0 - sc)).astype(BF16)
        datb = dat.astype(BF16)
        dcvb = dcv.astype(BF16)
        dat_ref[...] = datb
        dcv_ref[...] = dcvb
        dao_ref[...] = _dot_nt(datb, wa_ref[...]).astype(BF16)
        dc1 = _dot_nt(dcvb, wc_ref[...])
        _, vjp2 = jax.vjp(_lnsilu, c0_ref[...], v_ref[0:1, :], v_ref[1:2, :])
        dc0, dlg, dlb = vjp2(dc1)
        dc0_ref[...] = dc0
        acc_ref[0:1, :] += dgp
        acc_ref[1:2, :] += jnp.sum(dcv, axis=0, keepdims=True)
        acc_ref[2:3, :] += dlg
        acc_ref[3:4, :] += dlb

    def row(w):
        return pl.BlockSpec((tm, w), lambda i: (i, 0))

    return _call(
        body, name="mixer_bwd", grid=(p // tm,),
        in_specs=[row(D), row(D), row(D), row(D), row(2 * D), row(D), VM, VM, VM, VM],
        out_specs=[row(D), row(D), row(D), row(2 * D), row(D), row(D), pl.BlockSpec((8, D), lambda i: (0, 0))],
        out_shape=[jax.ShapeDtypeStruct((p, D), BF16)] * 3 + [jax.ShapeDtypeStruct((p, 2 * D), BF16),
                                                             jax.ShapeDtypeStruct((p, D), BF16),
                                                             jax.ShapeDtypeStruct((p, D), F32),
                                                             jax.ShapeDtypeStruct((8, D), F32)],
        sem="arbitrary", args=(dh1, mix, attn, conv, gates, c0, wa, wc, wo, vecs), comm=comm)


def _conv31_bwd(ag, dc0, w32, tn_pairs, comm=None):
    p = ag.shape[0]
    nch = p // BLK
    npair = len(tn_pairs)

    def body(*refs):
        a_ref, g_ref, dc_ref, w_ref = refs[:4]
        tn_a, tn_b = refs[4:4 + npair], refs[4 + npair:4 + 2 * npair]
        da_ref, dg_ref, gw_ref, gb_ref = refs[4 + 2 * npair:8 + 2 * npair]
        tn_o = refs[8 + 2 * npair:8 + 3 * npair]
        gp, dp = refs[8 + 3 * npair:]
        for ta, tb, to in zip(tn_a, tn_b, tn_o):
            to[...] = _dot_tn(ta[...], tb[...]).astype(BF16)
        gp[0:32, :] = jnp.zeros((32, BLK), F32)
        dp[p:p + 32, :] = jnp.zeros((32, BLK), F32)
        bsum = jnp.zeros((BLK, BLK), F32)
        for ci in range(nch):
            r0 = BLK * ci
            glu = a_ref[r0:r0 + BLK, :].astype(F32) * jax.nn.sigmoid(g_ref[r0:r0 + BLK, :].astype(F32))
            if ci == 0:
                glu = jnp.where(_rows(0, BLK) >= PAD, glu, 0.0)
            gp[32 + r0:32 + r0 + BLK, :] = glu
            d = dc_ref[r0:r0 + BLK, :]
            dp[r0:r0 + BLK, :] = d
            bsum = bsum + d
        gb_ref[...] = jnp.sum(bsum, axis=0, keepdims=True)
        for ci in range(nch):
            r0 = BLK * ci
            acc = jnp.zeros((BLK, BLK), F32)
            for j in range(CONV_K):
                acc = acc + w_ref[j:j + 1, :] * dp[r0 + 30 - j:r0 + 30 - j + BLK, :]
            if ci == 0:
                acc = jnp.where(_rows(0, BLK) >= PAD, acc, 0.0)
            a = a_ref[r0:r0 + BLK, :].astype(F32)
            sg = jax.nn.sigmoid(g_ref[r0:r0 + BLK, :].astype(F32))
            da_ref[r0:r0 + BLK, :] = (acc * sg).astype(BF16)
            dg_ref[r0:r0 + BLK, :] = (acc * a * sg * (1.0 - sg)).astype(BF16)
        for j in range(CONV_K):
            acc = jnp.zeros((BLK, BLK), F32)
            for ci in range(nch):
                r0 = BLK * ci
                acc = acc + dp[r0:r0 + BLK, :] * gp[r0 + j + 2:r0 + j + 2 + BLK, :]
            gw_ref[j:j + 1, :] = jnp.sum(acc, axis=0, keepdims=True)
        gw_ref[CONV_K:32, :] = jnp.zeros((32 - CONV_K, BLK), F32)

    slab = pl.BlockSpec((p, BLK), lambda j: (0, j))
    return _call(
        body, name="conv31_bwd", grid=(D // BLK,),
        in_specs=[slab, pl.BlockSpec((p, BLK), lambda j: (0, 8 + j)), slab, pl.BlockSpec((32, BLK), lambda j: (0, j))]
        + [slab] * npair + [VM] * npair,
        out_specs=[slab, slab, pl.BlockSpec((32, BLK), lambda j: (0, j)), pl.BlockSpec((1, BLK), lambda j: (0, j))]
        + [pl.BlockSpec((BLK, D), lambda j: (j, 0))] * npair,
        out_shape=[jax.ShapeDtypeStruct((p, D), BF16)] * 2 + [jax.ShapeDtypeStruct((32, D), F32),
                                                             jax.ShapeDtypeStruct((1, D), F32)]
        + [jax.ShapeDtypeStruct((D, D), BF16)] * npair,
        scratch=[pltpu.VMEM((p + 32, BLK), F32)] * 2,
        args=(ag, ag, dc0, w32, *[a for a, _ in tn_pairs], *[b for _, b in tn_pairs]), comm=comm)


def _attn_bwd(q, kv, dao, sinks, tabs, comm=None):
    p = q.shape[0]
    nb = p // BLK

    def body(q_ref, km_ref, kp_ref, kc_ref, do_ref, sink_ref, t_ref, dqkv_ref, dsink_ref, carry, macc):
        i = pl.program_id(0)
        n = nb - 1 - i

        @pl.when(i == 0)
        def _():
            carry[...] = jnp.zeros_like(carry)
            macc[...] = jnp.zeros_like(macc)
            dsink_ref[...] = jnp.zeros_like(dsink_ref)

        lo = lax.broadcasted_iota(jnp.int32, (BLK, BLK), 1) < HEAD_DIM
        lane8 = lax.broadcasted_iota(jnp.int32, (8, BLK), 1)
        c, s1, s2 = t_ref[:, 0:128], -t_ref[:, 128:256], -t_ref[:, 256:384]
        dk = jnp.zeros((N_KEY, BLK), F32)
        dv = jnp.zeros((N_KEY, BLK), F32)
        for h in range(2):
            qs, k2, v2, bias, lok = _attn_setup(n, h, q_ref, km_ref, kp_ref, kc_ref)
            dos = _stack_heads(do_ref, h, lo)
            st = _dot_nt(k2, qs)
            dpt = _dot_nt(v2, dos)
            p_parts, ds_parts = [], []
            for g in range(8):
                cols = slice(BLK * g, BLK * (g + 1))
                pn, ps = _attn_head(st[:, cols], bias, sink_ref[0, 8 * h + g])
                dp = dpt[:, cols]
                delta = jnp.sum(pn * dp, axis=0, keepdims=True)
                ds_parts.append((pn * (dp - delta)).astype(BF16))
                p_parts.append(pn.astype(BF16))
                dsk = -jnp.sum(ps * delta, axis=1, keepdims=True)
                dsink_ref[...] += jnp.where(lane8 == 8 * h + g, dsk, 0.0)
            dst = jnp.concatenate(ds_parts, axis=1)
            pt = jnp.concatenate(p_parts, axis=1)
            dq = _dot_tn(dst, k2)
            for jp in range(4):
                lo_c = BLK * (4 * h + jp)
                dqkv_ref[:, lo_c:lo_c + BLK] = (_rope(_unstack_heads(dq, jp, lo), c, s1, s2) * SCALE).astype(BF16)
            dk2 = _dot(dst, qs)
            dv2 = _dot(pt, dos)
            dk2 = dk2 + pltpu.roll(dk2, HEAD_DIM, 1)
            dv2 = dv2 + pltpu.roll(dv2, HEAD_DIM, 1)
            own = lok if h == 0 else jnp.logical_not(lok)
            dk = jnp.where(own, dk2, dk)
            dv = jnp.where(own, dv2, dv)
        macc[:, 0:BLK] += dk[2 * BLK:N_KEY]
        macc[:, BLK:2 * BLK] += dv[2 * BLK:N_KEY]
        last = (n == 0).astype(F32)
        zpad = jnp.zeros((PAD, BLK), F32)
        dk_c = dk[BLK:2 * BLK] + carry[:, 0:BLK] + last * jnp.concatenate([zpad, macc[:, 0:BLK]], axis=0)
        dv_c = dv[BLK:2 * BLK] + carry[:, BLK:2 * BLK] + last * jnp.concatenate([zpad, macc[:, BLK:2 * BLK]], axis=0)
        carry[:, 0:BLK] = dk[0:BLK]
        carry[:, BLK:2 * BLK] = dv[0:BLK]
        dqkv_ref[:, D:D + BLK] = _rope(dk_c, c, s1, s2).astype(BF16)
        dqkv_ref[:, D + BLK:D + 2 * BLK] = dv_c.astype(BF16)

    def rev(w):
        return pl.BlockSpec((BLK, w), lambda i: (nb - 1 - i, 0))

    return _call(
        body, name="attn_bwd", grid=(nb,),
        in_specs=[rev(D),
                  pl.BlockSpec((BLK, 256), lambda i: (0, 0)),
                  pl.BlockSpec((BLK, 256), lambda i: (jnp.maximum(nb - 2 - i, 0), 0)),
                  rev(256), rev(D),
                  pl.BlockSpec(memory_space=pltpu.SMEM), rev(384)],
        out_specs=[rev(QKV_W), pl.BlockSpec((8, BLK), lambda i: (0, 0))],
        out_shape=[jax.ShapeDtypeStruct((p, QKV_W), BF16), jax.ShapeDtypeStruct((8, BLK), F32)],
        scratch=[pltpu.VMEM((BLK, 256), F32), pltpu.VMEM((N_META, 256), F32)], sem="arbitrary",
        args=(q, kv, kv, kv, dao, sinks, tabs), comm=comm)


def _in_bwd(dqkv, da, dg, dgt, w_int, h0p, dh1, gain, comm=None):
    p = h0p.shape[0]
    tm = _row_tile(p)

    def body(dq_ref, da_ref, dg_ref, dt_ref, w_ref, h_ref, dh_ref, g_ref, o_ref, acc_ref):
        i = pl.program_id(0)

        @pl.when(i == 0)
        def _():
            acc_ref[...] = jnp.zeros_like(acc_ref)

        dn = (_dot(dq_ref[...], w_ref[0:QKV_W, :]) + _dot(da_ref[...], w_ref[QKV_W:QKV_W + D, :])
              + _dot(dg_ref[...], w_ref[QKV_W + D:QKV_W + 2 * D, :]) + _dot(dt_ref[...], w_ref[QKV_W + 2 * D:IN_W, :]))
        _, vjp = jax.vjp(_rms, h_ref[...], g_ref[...])
        dh, dgain = vjp(dn)
        o_ref[...] = dh_ref[...] + dh
        acc_ref[0:1, :] += dgain

    def row(w):
        return pl.BlockSpec((tm, w), lambda i: (i, 0))

    return _call(
        body, name="in_bwd", grid=(p // tm,),
        in_specs=[row(QKV_W), row(D), row(D), row(2 * D), VM, row(D), row(D), VM],
        out_specs=[row(D), pl.BlockSpec((8, D), lambda i: (0, 0))],
        out_shape=[jax.ShapeDtypeStruct((p, D), F32), jax.ShapeDtypeStruct((8, D), F32)],
        sem="arbitrary", args=(dqkv, da, dg, dgt, w_int, h0p, dh1, gain), comm=comm)


def _sum_slots(slots, name):
    r = slots.shape[0] // N_DEV
    cols = slots.shape[1]
    tr = r if r <= 352 else (r // 2 if (r // 2) % 16 == 0 else r // 3)
    steps = r // tr

    def body(*refs):
        acc = refs[0][...].astype(F32)
        for s in range(1, N_DEV):
            acc = acc + refs[s][...].astype(F32)
        refs[N_DEV][...] = acc

    return pl.pallas_call(
        body, name=name, grid=(steps,),
        in_specs=[pl.BlockSpec((tr, cols), functools.partial(lambda i, s: (s * steps + i, 0), s=s)) for s in range(N_DEV)],
        out_specs=pl.BlockSpec((tr, cols), lambda i: (i, 0)),
        out_shape=jax.ShapeDtypeStruct((r, cols), F32),
        compiler_params=_cparams("parallel"),
    )(*([slots] * N_DEV))


def _adamw_math(w, g, m, v):
    m_n = ADAM_B1 * m + (1.0 - ADAM_B1) * g
    v_n = ADAM_B2 * v + (1.0 - ADAM_B2) * jnp.square(g)
    m_hat = m_n / (1.0 - ADAM_B1 ** ADAM_STEP)
    v_hat = v_n / (1.0 - ADAM_B2 ** ADAM_STEP)
    return -ADAM_LR * (m_hat / (jnp.sqrt(v_hat) + ADAM_EPS) + ADAM_WD * w), m_n, v_n


def _sum_adamw(parts, w, m, v, name, nslots=N_DEV):
    r, cols = w.shape
    rs = r // len(parts)
    tr = rs if rs <= 352 else (rs // 2 if (rs // 2) % 16 == 0 else rs // 3)
    steps = rs // tr

    def body(*refs):
        w_ref, m_ref, v_ref, g_ref, d_ref, nm_ref, nv_ref = refs[nslots * len(parts):]
        i = pl.program_id(0)
        for q in range(len(parts)):
            @pl.when(i // steps == q)
            def _(q=q):
                g = refs[nslots * q][...].astype(F32)
                for s in range(1, nslots):
                    g = g + refs[nslots * q + s][...].astype(F32)
                g_ref[...] = g
                d_ref[...], nm_ref[...], nv_ref[...] = _adamw_math(w_ref[...], g, m_ref[...], v_ref[...])

    def slot_spec(q, s):
        return pl.BlockSpec((tr, cols), lambda i: (s * steps + jnp.clip(i - q * steps, 0, steps - 1), 0))

    spec = pl.BlockSpec((tr, cols), lambda i: (i, 0))
    return pl.pallas_call(
        body, name=name, grid=(steps * len(parts),),
        in_specs=[slot_spec(q, s) for q in range(len(parts)) for s in range(nslots)] + [spec] * 3,
        out_specs=[spec] * 4, out_shape=[jax.ShapeDtypeStruct((r, cols), F32)] * 4,
        compiler_params=_cparams("parallel"),
    )(*[a for a in parts for _ in range(nslots)], w, m, v)


def _adamw(w, g, m, v, name):
    r, cols = w.shape
    tr = 256 if r % 256 == 0 else r

    def body(w_ref, g_ref, m_ref, v_ref, d_ref, nm_ref, nv_ref):
        d_ref[...], nm_ref[...], nv_ref[...] = _adamw_math(w_ref[...], g_ref[...], m_ref[...], v_ref[...])

    spec = pl.BlockSpec((tr, cols), lambda i: (i, 0))
    return pl.pallas_call(
        body, name=name, grid=(r // tr,),
        in_specs=[spec] * 4, out_specs=[spec] * 3,
        out_shape=[jax.ShapeDtypeStruct((r, cols), F32)] * 3,
        compiler_params=_cparams("parallel"),
    )(w, g, m, v)


def _rope_tables(p):
    half = ROT_DIM // 2
    inv_freq = ROPE_THETA ** (-jnp.arange(half, dtype=F32) * 2.0 / ROT_DIM)
    pos = (jnp.arange(p) - PAD).astype(F32)
    ang = pos[:, None] * inv_freq[None, :]
    lane = jnp.arange(BLK)
    seg = (lane % HEAD_DIM) // half
    cos = jnp.cos(ang)[:, lane % half]
    sin = jnp.sin(ang)[:, lane % half]
    c = jnp.where(seg[None, :] < 2, cos, 1.0)
    s1 = jnp.where(seg[None, :] == 0, -sin, 0.0)
    s2 = jnp.where(seg[None, :] == 1, sin, 0.0)
    return jnp.concatenate([c, s1, s2], axis=1).astype(F32)


def _flat_pack(parts, rows):
    flat = jnp.concatenate([a.reshape(-1).astype(F32) for a in parts])
    return jnp.pad(flat, (0, rows * D - flat.shape[0])).reshape(rows, D)


def _flat_unpack(pack, shapes):
    flat = pack.reshape(-1)
    out, off = [], 0
    for s in shapes:
        size = 1
        for e in s:
            size *= e
        out.append(flat[off:off + size].reshape(s))
        off += size
    return out


def kernel(x, meta_tokens, norm_pre_mix, norm_post_mix, w_in, b_in, attn_sinks, w_attn_proj, conv_dw_w, conv_dw_b, conv_ln_g, conv_ln_b, w_conv_proj, b_conv_proj, w_out, norm_pre_ffn, norm_post_ffn, w_up, ffn_dw_w, ffn_dw_b, w_down, loss_target, m_meta_tokens, m_norm_pre_mix, m_norm_post_mix, m_w_in, m_b_in, m_attn_sinks, m_w_attn_proj, m_conv_dw_w, m_conv_dw_b, m_conv_ln_g, m_conv_ln_b, m_w_conv_proj, m_b_conv_proj, m_w_out, m_norm_pre_ffn, m_norm_post_ffn, m_w_up, m_ffn_dw_w, m_ffn_dw_b, m_w_down, v_meta_tokens, v_norm_pre_mix, v_norm_post_mix, v_w_in, v_b_in, v_attn_sinks, v_w_attn_proj, v_conv_dw_w, v_conv_dw_b, v_conv_ln_g, v_conv_ln_b, v_w_conv_proj, v_b_conv_proj, v_w_out, v_norm_pre_ffn, v_norm_post_ffn, v_w_up, v_ffn_dw_w, v_ffn_dw_b, v_w_down):
    seq = x.shape[1]
    p = seq + BLK
    me = 4 * lax.axis_index("x") + 2 * lax.axis_index("y") + lax.axis_index("c")
    in_cols = w_in.shape[2]
    up_cols = w_up.shape[2]

    small = jnp.zeros((56, up_cols), F32)
    small = small.at[0:N_META, 0:BLK].set(meta_tokens)
    small = small.at[16:16 + CONV_K, 0:BLK].set(conv_dw_w[0])
    small = small.at[48:48 + FFN_K, :].set(ffn_dw_w[0])
    w_int, small_all = _exchange(_Gather([w_in[0].T.astype(BF16), small]), "gather_w_in")
    small_all = small_all.reshape(N_DEV, 56, up_cols)
    meta_full = small_all[:, 0:N_META, 0:BLK].transpose(1, 0, 2).reshape(N_META, D)
    cdw = small_all[:, 16:16 + CONV_K, 0:BLK].transpose(1, 0, 2).reshape(CONV_K, D)
    cdw32 = jnp.pad(cdw, ((0, 32 - CONV_K), (0, 0)))
    fdw = small_all[:, 48:48 + FFN_K, :].transpose(1, 0, 2).reshape(FFN_K, 2 * FFN)

    tabs = _rope_tables(p)
    vecs = jnp.concatenate([conv_ln_g, conv_ln_b, b_conv_proj, norm_post_mix, norm_pre_ffn, jnp.zeros((3, D), F32)], axis=0)

    (h0p, n1, q, kv, ag, gates), (wa, wc, wo) = _in_proj(
        x[0], meta_full, norm_pre_mix, w_int, b_in, tabs,
        comm=_Gather([w_attn_proj[0].astype(BF16), w_conv_proj[0].astype(BF16), w_out[0].astype(BF16)]))
    (ao,), (w_upt,) = _attn_fwd(q, kv, attn_sinks, comm=_Gather([w_up[0].T.astype(BF16)]))
    (c0,), (wd,) = _conv31_fwd(ag, cdw32, conv_dw_b, comm=_Gather([w_down[0].astype(BF16)]))
    c1, attn, conv, merged, mix, h1, n2 = _mixer_fwd(ao, c0, gates, h0p, wa, wc, wo, vecs)
    u0, act = _ffn_up_act(n2, w_upt, fdw, ffn_dw_b)
    dffn, dact, dy, acc_f = _ffn_down_loss(act, wd, h1, loss_target[0], norm_post_ffn)

    (dug, duv, gfw_g, gfw_v, gfb_g, gfb_v, g_wd), _ = _ffn_act_bwd(u0, dact, fdw, ffn_dw_b, act, dffn)
    g_wupt, (s_wd,) = _mm_tn([dug, duv], n2, "grad_w_up", comm=_Scatter([g_wd]))
    (dh1, acc_u), (s_wup0,) = _ffn_in_bwd(dug, duv, w_upt, h1, dy, norm_pre_ffn, comm=_Scatter([g_wupt], 0, 2))
    (dmix, dat, dcv, dgt, dao, dc0, acc_m), (s_wup1,) = _mixer_bwd(
        dh1, mix, attn, conv, gates, c0, wa, wc, wo, vecs, comm=_Scatter([g_wupt], 1, 2))
    (da, dg, g_cdw, g_cdb, g_wo, g_wa, g_wc), _ = _conv31_bwd(ag, dc0, cdw32, [(merged, dmix), (ao, dat), (c1, dcv)])
    (dqkv, dsink), (s_wa, s_wc, s_wo) = _attn_bwd(q, kv, dao, attn_sinks, tabs, comm=_Scatter([g_wa, g_wc, g_wo]))
    g_wint, g_bin = _mm_tn([dqkv, da, dg, dgt], n1, "grad_w_in", col_sums=True)
    (from_sibling,) = _exchange(_SiblingSwap(g_wint), "swap_w_in")
    (dh0, acc_i), (s_win,) = _in_bwd(dqkv, da, dg, dgt, w_int, h0p, dh1, norm_pre_mix,
                                     comm=_ChipScatter(_pair_add(g_wint, from_sibling)))

    big = []
    for nm, parts, nslots, w, m, v, tr in (
            ("w_in", [s_win], N_CHIP, w_in, m_w_in, v_w_in, True), ("w_up", [s_wup0, s_wup1], N_DEV, w_up, m_w_up, v_w_up, True),
            ("w_attn_proj", [s_wa], N_DEV, w_attn_proj, m_w_attn_proj, v_w_attn_proj, False),
            ("w_conv_proj", [s_wc], N_DEV, w_conv_proj, m_w_conv_proj, v_w_conv_proj, False),
            ("w_out", [s_wo], N_DEV, w_out, m_w_out, v_w_out, False),
            ("w_down", [s_wd], N_DEV, w_down, m_w_down, v_w_down, False)):
        ins = [a[0].T if tr else a[0] for a in (w, m, v)]
        big.append(tuple((o.T if tr else o)[None] for o in _sum_adamw(parts, *ins, "update_" + nm, nslots)))

    loss_row = jnp.sum(acc_f[1:2, :], axis=1, keepdims=True)
    parts = [loss_row, dh0[PAD:BLK], acc_i[0:1], acc_m[0:1], g_bin, dsink[0:1, 0:16], g_cdw[0:CONV_K], g_cdb,
             acc_m[2:3], acc_m[3:4], acc_m[1:2], acc_u[0:1], acc_f[0:1],
             jnp.concatenate([gfw_g, gfw_v], axis=1), jnp.concatenate([gfb_g, gfb_v], axis=1)]
    shapes = [a.shape for a in parts]
    pack_rows = 88
    (gathered,) = _exchange(_Gather([_flat_pack(parts, pack_rows)]), "gather_small_grads")
    tot = _flat_unpack(_sum_slots(gathered, "sum_small_grads"), shapes)
    (loss, g_meta, g_npm, g_nqm, g_bi, g_sk, g_cw, g_cb, g_lg, g_lb, g_bc, g_npf, g_nqf, g_fw, g_fb) = tot
    loss = loss.reshape(())
    g_meta = lax.dynamic_slice_in_dim(g_meta, me * BLK, BLK, axis=1)
    g_cw = lax.dynamic_slice_in_dim(g_cw, me * BLK, BLK, axis=1)[None]
    g_fw = lax.dynamic_slice_in_dim(g_fw, me * up_cols, up_cols, axis=1)[None]

    sm_w = [meta_tokens, norm_pre_mix, norm_post_mix, b_in, attn_sinks, conv_dw_w, conv_dw_b, conv_ln_g, conv_ln_b,
            b_conv_proj, norm_pre_ffn, norm_post_ffn, ffn_dw_w, ffn_dw_b]
    sm_g = [g_meta, g_npm, g_nqm, g_bi, g_sk, g_cw, g_cb, g_lg, g_lb, g_bc, g_npf, g_nqf, g_fw, g_fb]
    sm_m = [m_meta_tokens, m_norm_pre_mix, m_norm_post_mix, m_b_in, m_attn_sinks, m_conv_dw_w, m_conv_dw_b, m_conv_ln_g,
            m_conv_ln_b, m_b_conv_proj, m_norm_pre_ffn, m_norm_post_ffn, m_ffn_dw_w, m_ffn_dw_b]
    sm_v = [v_meta_tokens, v_norm_pre_mix, v_norm_post_mix, v_b_in, v_attn_sinks, v_conv_dw_w, v_conv_dw_b, v_conv_ln_g,
            v_conv_ln_b, v_b_conv_proj, v_norm_pre_ffn, v_norm_post_ffn, v_ffn_dw_w, v_ffn_dw_b]
    sm_shapes = [a.shape for a in sm_w]
    upd_rows = 32
    v_pack = _flat_pack(sm_v, upd_rows)
    sm_out = _adamw(_flat_pack(sm_w, upd_rows), _flat_pack(sm_g, upd_rows), _flat_pack(sm_m, upd_rows), v_pack, "adamw_small")
    sm_d, sm_nm, sm_nv = (_flat_unpack(o, sm_shapes) for o in sm_out)

    order = ["meta_tokens", "norm_pre_mix", "norm_post_mix", "w_in", "b_in", "attn_sinks", "w_attn_proj", "conv_dw_w",
             "conv_dw_b", "conv_ln_g", "conv_ln_b", "w_conv_proj", "b_conv_proj", "w_out", "norm_pre_ffn", "norm_post_ffn",
             "w_up", "ffn_dw_w", "ffn_dw_b", "w_down"]
    small_names = ["meta_tokens", "norm_pre_mix", "norm_post_mix", "b_in", "attn_sinks", "conv_dw_w", "conv_dw_b", "conv_ln_g",
                   "conv_ln_b", "b_conv_proj", "norm_pre_ffn", "norm_post_ffn", "ffn_dw_w", "ffn_dw_b"]
    big_names = ["w_in", "w_up", "w_attn_proj", "w_conv_proj", "w_out", "w_down"]
    table = {}
    for k, nm in enumerate(small_names):
        table[nm] = (sm_g[k], sm_d[k], sm_nm[k], sm_nv[k])
    for k, nm in enumerate(big_names):
        table[nm] = big[k]
    grad_x = dh0[BLK:][None]
    outs = [loss, grad_x]
    for field in range(4):
        outs += [table[nm][field] for nm in order]
    return tuple(outs)
```

```python
import functools

import jax
import jax.numpy as jnp
from jax import lax
from jax.experimental import pallas as pl
from jax.experimental.pallas import tpu as pltpu

F32 = jnp.float32
BF16 = jnp.bfloat16
MESH = pl.DeviceIdType.MESH

D = 1024
HEAD_DIM = 64
N_META = 16
BLK = 128
PAD = BLK - N_META
CONV_K = 31
FFN = 2816
FFN_K = 3
QKV_W = 1280
IN_W = 5376
ROT_DIM = 16
ROPE_THETA = 500000.0
RMS_EPS = 1e-6
LN_EPS = 1e-5
NEG_INF = -1e30
SCALE = HEAD_DIM ** -0.5
N_DEV = 8

ADAM_LR = 0.001
ADAM_B1 = 0.9
ADAM_B2 = 0.999
ADAM_EPS = 1e-08
ADAM_WD = 0.01
ADAM_STEP = 10

VMEM_BYTES_V7X = 64 * 1024 * 1024
VMEM_LIMIT = VMEM_BYTES_V7X - 8 * 1024 * 1024

NT = (((1,), (1,)), ((), ()))
TN = (((0,), (0,)), ((), ()))
VM = pl.BlockSpec(memory_space=pltpu.VMEM)
ANY = pl.BlockSpec(memory_space=pl.ANY)


def _cparams(*sem):
    return pltpu.CompilerParams(dimension_semantics=sem or None, vmem_limit_bytes=VMEM_LIMIT)


def _row_tile(p):
    return 384 if p % 384 == 0 else 128


def _dot(a, b):
    return jnp.dot(a, b, preferred_element_type=F32)


def _dot_nt(a, b):
    return lax.dot_general(a, b, NT, preferred_element_type=F32)


def _dot_tn(a, b):
    return lax.dot_general(a, b, TN, preferred_element_type=F32)


def _rms(x, g):
    return x * lax.rsqrt(jnp.mean(x * x, axis=-1, keepdims=True) + RMS_EPS) * g


def _lnsilu(x, g, b):
    mu = jnp.mean(x, axis=-1, keepdims=True)
    var = jnp.mean(jnp.square(x - mu), axis=-1, keepdims=True)
    z = (x - mu) * lax.rsqrt(var + LN_EPS) * g + b
    return z * jax.nn.sigmoid(z)


def _rope(v, c, s1, s2):
    return v * c + pltpu.roll(v, BLK - 8, 1) * s1 + pltpu.roll(v, 8, 1) * s2


def _rows(i, tm):
    return i * tm + lax.broadcasted_iota(jnp.int32, (tm, 1), 0)


def _place():
    return lax.axis_index("x"), lax.axis_index("y"), lax.axis_index("c")


def _blk(ref, idx, r, dtype):
    return ref.at[pl.ds(pl.multiple_of(idx * r, 16 if dtype == BF16 else 8), r), :]


class _Gather:
    def __init__(self, arrs):
        self.ins = list(arrs)
        n = len(arrs)
        self.out_shape = [jax.ShapeDtypeStruct((N_DEV * a.shape[0], a.shape[1]), a.dtype) for a in arrs]
        self.scratch = [pltpu.SemaphoreType.DMA((n, 7)), pltpu.SemaphoreType.DMA((n, 7)), pltpu.SemaphoreType.DMA((n,))]

    def _parts(self, ins, outs, sems):
        send_sems, recv_sems, local_sems = sems
        n = len(ins)
        x, y, c = _place()
        me, sibling = (x, y, c), (x, y, 1 - c)
        chips = [(1 - x, y), (x, 1 - y), (1 - x, 1 - y)]

        def rows(a, p):
            return _blk(outs[a], 4 * p[0] + 2 * p[1] + p[2], self.ins[a].shape[0], self.ins[a].dtype)

        def copy(a, k, block, to, src=None):
            return pltpu.make_async_remote_copy(
                src_ref=rows(a, block) if src is None else src, dst_ref=rows(a, block),
                send_sem=send_sems.at[a, k], recv_sem=recv_sems.at[a, k], device_id=to, device_id_type=MESH)

        mine = [pltpu.make_async_copy(ins[a], rows(a, me), local_sems.at[a]) for a in range(n)]
        first = []
        for a in range(n):
            first.append(copy(a, 0, me, sibling, src=ins[a]))
            first += [copy(a, 1 + j, me, (*chip, c), src=ins[a]) for j, chip in enumerate(chips)]
        return n, c, me, sibling, chips, copy, mine, first

    def start(self, ins, outs, sems):
        *_, mine, first = self._parts(ins, outs, sems)
        for cp in mine + first:
            cp.start()

    def finish(self, ins, outs, sems):
        n, c, me, sibling, chips, copy, mine, first = self._parts(ins, outs, sems)
        passed = []
        for j, chip in enumerate(chips):
            for a in range(n):
                copy(a, 1 + j, (*chip, c), me).wait_recv()
                fwd = copy(a, 4 + j, (*chip, c), sibling)
                fwd.start()
                passed.append(fwd)
        for a in range(n):
            copy(a, 0, sibling, me).wait_recv()
            for j, chip in enumerate(chips):
                copy(a, 4 + j, (*chip, 1 - c), me).wait_recv()
        for cp in first + passed:
            cp.wait_send()
        for cp in mine:
            cp.wait()


FLIPS = [(0, 0, 1), (1, 0, 0), (0, 1, 0), (1, 1, 0), (1, 0, 1), (0, 1, 1), (1, 1, 1)]


class _Scatter:
    def __init__(self, arrs, part=0, nparts=1):
        self.ins = list(arrs)
        self.part, self.nparts = part, nparts
        n = len(arrs)
        self.out_shape = [jax.ShapeDtypeStruct((a.shape[0] // nparts, a.shape[1]), a.dtype) for a in arrs]
        self.scratch = [pltpu.SemaphoreType.DMA((n, 7)), pltpu.SemaphoreType.DMA((n, 7)), pltpu.SemaphoreType.DMA((n,))]

    def _parts(self, ins, outs, sems):
        send_sems, recv_sems, local_sems = sems
        n = len(ins)
        x, y, c = _place()
        me = 4 * x + 2 * y + c

        def flip(v, f):
            return 1 - v if f else v

        def src(a, idx):
            r = self.ins[a].shape[0] // N_DEV
            rs = r // self.nparts
            return ins[a].at[pl.ds(pl.multiple_of(idx * r + self.part * rs, 16), rs), :]

        def dst(a, idx):
            rs = self.ins[a].shape[0] // N_DEV // self.nparts
            return outs[a].at[pl.ds(pl.multiple_of(idx * rs, 16), rs), :]

        mine = [pltpu.make_async_copy(src(a, me), dst(a, me), local_sems.at[a]) for a in range(n)]
        sends, recvs = [], []
        for k, f in enumerate(FLIPS):
            peer = (flip(x, f[0]), flip(y, f[1]), flip(c, f[2]))
            pidx = 4 * peer[0] + 2 * peer[1] + peer[2]
            for a in range(n):
                sends.append(pltpu.make_async_remote_copy(
                    src_ref=src(a, pidx), dst_ref=dst(a, me),
                    send_sem=send_sems.at[a, k], recv_sem=recv_sems.at[a, k], device_id=peer, device_id_type=MESH))
                recvs.append(functools.partial(
                    pltpu.make_async_remote_copy,
                    src_ref=src(a, pidx), dst_ref=dst(a, pidx),
                    send_sem=send_sems.at[a, k], recv_sem=recv_sems.at[a, k], device_id=peer, device_id_type=MESH))
        return mine, sends, recvs

    def start(self, ins, outs, sems):
        mine, sends, _ = self._parts(ins, outs, sems)
        for cp in mine + sends:
            cp.start()

    def finish(self, ins, outs, sems):
        mine, sends, recvs = self._parts(ins, outs, sems)
        for make in recvs:
            make().wait_recv()
        for cp in sends:
            cp.wait_send()
        for cp in mine:
            cp.wait()


N_CHIP = 4


class _SiblingSwap:
    def __init__(self, arr):
        self.ins = [arr]
        self.r = arr.shape[0] // N_DEV
        self.out_shape = [jax.ShapeDtypeStruct((N_CHIP * self.r, arr.shape[1]), arr.dtype)]
        self.scratch = [pltpu.SemaphoreType.DMA((N_CHIP,)), pltpu.SemaphoreType.DMA((N_CHIP,))]

    def _copies(self, ins, outs, sems):
        send_sems, recv_sems = sems
        x, y, c = _place()
        r = self.r
        return [pltpu.make_async_remote_copy(
            src_ref=ins[0].at[pl.ds(pl.multiple_of((2 * j + 1 - c) * r, 16), r), :],
            dst_ref=outs[0].at[pl.ds(j * r, r), :],
            send_sem=send_sems.at[j], recv_sem=recv_sems.at[j], device_id=(x, y, 1 - c), device_id_type=MESH)
            for j in range(N_CHIP)]

    def start(self, ins, outs, sems):
        for cp in self._copies(ins, outs, sems):
            cp.start()

    def finish(self, ins, outs, sems):
        for cp in self._copies(ins, outs, sems):
            cp.wait()


class _ChipScatter:
    def __init__(self, arr):
        self.ins = [arr]
        self.r = arr.shape[0] // N_CHIP
        self.out_shape = [jax.ShapeDtypeStruct(arr.shape, arr.dtype)]
        self.scratch = [pltpu.SemaphoreType.DMA((3,)), pltpu.SemaphoreType.DMA((3,)), pltpu.SemaphoreType.DMA]

    def _parts(self, ins, outs, sems):
        send_sems, recv_sems, local_sem = sems
        x, y, c = _place()
        r = self.r
        my_chip = 2 * x + y

        def rows(ref, j):
            return ref.at[pl.ds(pl.multiple_of(j * r, 16), r), :]

        mine = pltpu.make_async_copy(rows(ins[0], my_chip), rows(outs[0], my_chip), local_sem)
        sends, recvs = [], []
        for k, (fx, fy) in enumerate(((1, 0), (0, 1), (1, 1))):
            px, py = (1 - x if fx else x), (1 - y if fy else y)
            peer_chip = 2 * px + py
            sends.append(pltpu.make_async_remote_copy(
                src_ref=rows(ins[0], peer_chip), dst_ref=rows(outs[0], my_chip),
                send_sem=send_sems.at[k], recv_sem=recv_sems.at[k], device_id=(px, py, c), device_id_type=MESH))
            recvs.append(functools.partial(
                pltpu.make_async_remote_copy,
                src_ref=rows(ins[0], peer_chip), dst_ref=rows(outs[0], peer_chip),
                send_sem=send_sems.at[k], recv_sem=recv_sems.at[k], device_id=(px, py, c), device_id_type=MESH))
        return mine, sends, recvs

    def start(self, ins, outs, sems):
        mine, sends, _ = self._parts(ins, outs, sems)
        for cp in [mine] + sends:
            cp.start()

    def finish(self, ins, outs, sems):
        mine, sends, recvs = self._parts(ins, outs, sems)
        for make in recvs:
            make().wait_recv()
        for cp in sends:
            cp.wait_send()
        mine.wait()


def _pair_add(partial, recv):
    r = recv.shape[0] // N_CHIP
    cols = recv.shape[1]
    tr = r // 2 if (r // 2) % 16 == 0 else r
    steps = r // tr
    core = lax.axis_index("c").astype(jnp.int32).reshape(1)

    def body(c_ref, p_ref, s_ref, o_ref):
        o_ref[...] = (p_ref[...].astype(F32) + s_ref[...].astype(F32)).astype(BF16)

    spec = pl.BlockSpec((tr, cols), lambda j, i, c_ref: (j * steps + i, 0))
    return pl.pallas_call(
        body, name="pair_add",
        grid_spec=pltpu.PrefetchScalarGridSpec(
            num_scalar_prefetch=1, grid=(N_CHIP, steps),
            in_specs=[pl.BlockSpec((tr, cols), lambda j, i, c_ref: ((2 * j + c_ref[0]) * steps + i, 0)), spec],
            out_specs=spec),
        out_shape=jax.ShapeDtypeStruct(recv.shape, BF16),
        compiler_params=_cparams("parallel", "parallel"),
    )(core, partial, recv)


class _Both:
    def __init__(self, a, b):
        self.a, self.b = a, b
        self.ins = a.ins + b.ins
        self.out_shape = a.out_shape + b.out_shape
        self.scratch = a.scratch + b.scratch

    def _split(self, ins, outs, sems):
        ni, no, ns = len(self.a.ins), len(self.a.out_shape), len(self.a.scratch)
        return (ins[:ni], outs[:no], sems[:ns]), (ins[ni:], outs[no:], sems[ns:])

    def start(self, ins, outs, sems):
        ra, rb = self._split(ins, outs, sems)
        self.a.start(*ra)
        self.b.start(*rb)

    def finish(self, ins, outs, sems):
        ra, rb = self._split(ins, outs, sems)
        self.a.finish(*ra)
        self.b.finish(*rb)


def _exchange(comm, name):
    n, m = len(comm.ins), len(comm.out_shape)

    def body(*refs):
        ins, outs, sems = refs[:n], refs[n:n + m], refs[n + m:]
        comm.start(ins, outs, sems)
        comm.finish(ins, outs, sems)

    return pl.pallas_call(
        body, name=name, out_shape=comm.out_shape, in_specs=[ANY] * n, out_specs=[ANY] * m, scratch_shapes=comm.scratch,
    )(*comm.ins)


def _call(body, *, name, grid, in_specs, out_specs, out_shape, args, scratch=(), sem="parallel", comm=None):
    if comm is None:
        outs = pl.pallas_call(
            body, name=name, grid=grid, in_specs=list(in_specs), out_specs=list(out_specs), out_shape=list(out_shape),
            scratch_shapes=list(scratch), compiler_params=_cparams(sem))(*args)
        return outs, []
    n_in, n_out, n_sc = len(in_specs), len(out_specs), len(scratch)
    n_ci, n_co = len(comm.ins), len(comm.out_shape)
    last = grid[0] - 1

    def fused(*refs):
        ins, refs = refs[:n_in], refs[n_in:]
        c_ins, refs = refs[:n_ci], refs[n_ci:]
        outs, refs = refs[:n_out], refs[n_out:]
        c_outs, refs = refs[:n_co], refs[n_co:]
        sc, c_sems = refs[:n_sc], refs[n_sc:]
        step = pl.program_id(0)

        @pl.when(step == 0)
        def _():
            comm.start(c_ins, c_outs, c_sems)

        body(*ins, *outs, *sc)

        @pl.when(step == last)
        def _():
            comm.finish(c_ins, c_outs, c_sems)

    outs = pl.pallas_call(
        fused, name=name, grid=grid, in_specs=list(in_specs) + [ANY] * n_ci, out_specs=list(out_specs) + [ANY] * n_co,
        out_shape=list(out_shape) + comm.out_shape, scratch_shapes=list(scratch) + comm.scratch,
        compiler_params=_cparams("arbitrary"))(*args, *comm.ins)
    return outs[:n_out], outs[n_out:]


def _token_specs(tm):
    k = tm // BLK
    return [pl.BlockSpec((BLK, D), functools.partial(lambda i, t: (jnp.maximum(k * i + t - 1, 0), 0), t=t)) for t in range(k)]


def _in_proj(x2d, meta, gain, w_int, b_in, tabs, comm=None):
    p = x2d.shape[0] + BLK
    tm = _row_tile(p)
    k = tm // BLK

    def body(*refs):
        x_refs = refs[:k]
        m_ref, g_ref, w_ref, b_ref, t_ref, h_ref, n1_ref, q_ref, kv_ref, ag_ref, gt_ref = refs[k:]
        i = pl.program_id(0)
        head = jnp.concatenate([jnp.zeros((PAD, D), F32), m_ref[...]], axis=0)
        first = jnp.where(i == 0, head, x_refs[0][...])
        h = jnp.concatenate([first] + [r[...] for r in x_refs[1:]], axis=0) if k > 1 else first
        h_ref[...] = h
        n = _rms(h, g_ref[...]).astype(BF16)
        n1_ref[...] = n
        c, s1, s2 = t_ref[:, 0:128], t_ref[:, 128:256], t_ref[:, 256:384]

        def mm(c0, w):
            return _dot_nt(n, w_ref[c0:c0 + w, :]) + b_ref[:, c0:c0 + w]

        for j in range(4):
            acc = mm(256 * j, 256)
            for t in range(2):
                lo = 256 * j + 128 * t
                q_ref[:, lo:lo + 128] = (_rope(acc[:, 128 * t:128 * (t + 1)], c, s1, s2) * SCALE).astype(BF16)
        acc = mm(1024, 256)
        kv_ref[:, 0:128] = _rope(acc[:, 0:128], c, s1, s2).astype(BF16)
        kv_ref[:, 128:256] = acc[:, 128:256].astype(BF16)
        for j in range(8):
            ag_ref[:, 256 * j:256 * (j + 1)] = mm(QKV_W + 256 * j, 256).astype(BF16)
        for j in range(8):
            gt_ref[:, 256 * j:256 * (j + 1)] = mm(QKV_W + 2048 + 256 * j, 256).astype(BF16)

    def row(w):
        return pl.BlockSpec((tm, w), lambda i: (i, 0))

    return _call(
        body, name="in_proj", grid=(p // tm,),
        in_specs=_token_specs(tm) + [VM, VM, VM, VM, row(384)],
        out_specs=[row(D), row(D), row(D), row(256), row(2048), row(2048)],
        out_shape=[jax.ShapeDtypeStruct((p, D), F32)] + [jax.ShapeDtypeStruct((p, w), BF16) for w in (D, D, 256, 2048, 2048)],
        args=(x2d,) * k + (meta, gain, w_int, b_in, tabs), comm=comm)


N_KEY = 2 * BLK + N_META


def _attn_setup(n, h, q_ref, km_ref, kp_ref, kc_ref):
    lo = lax.broadcasted_iota(jnp.int32, (BLK, BLK), 1) < HEAD_DIM
    lok = lax.broadcasted_iota(jnp.int32, (N_KEY, BLK), 1) < HEAD_DIM

    def dup(lanes):
        cat = jnp.concatenate([kp_ref[:, lanes], kc_ref[:, lanes], km_ref[PAD:BLK, lanes]], axis=0).astype(F32)
        rolled = pltpu.roll(cat, HEAD_DIM, 1)
        return (jnp.where(lok, cat, rolled) if h == 0 else jnp.where(lok, rolled, cat)).astype(BF16)

    k2 = dup(slice(0, 128))
    v2 = dup(slice(128, 256))
    qs = _stack_heads(q_ref, h, lo)

    kr = lax.broadcasted_iota(jnp.int32, (BLK, BLK), 0)
    tq = BLK * n + lax.broadcasted_iota(jnp.int32, (BLK, BLK), 1) - PAD
    t_p = BLK * (n - 1) + kr - PAD
    t_c = BLK * n + kr - PAD
    ok_p = jnp.logical_and(t_p >= N_META, tq - t_p < BLK)
    ok_c = jnp.logical_and(t_c >= N_META, t_c <= tq)
    ok_m = lax.broadcasted_iota(jnp.int32, (N_META, BLK), 0) <= BLK * n + lax.broadcasted_iota(jnp.int32, (N_META, BLK), 1) - PAD
    bias = jnp.concatenate([jnp.where(ok, 0.0, NEG_INF).astype(F32) for ok in (ok_p, ok_c, ok_m)], axis=0)
    return qs, k2, v2, bias, lok


def _attn_head(s, bias, sink):
    s = s + bias
    m = jnp.maximum(jnp.max(s, axis=0, keepdims=True), sink)
    e = jnp.exp(s - m)
    es = jnp.exp(sink - m)
    inv = 1.0 / (jnp.sum(e, axis=0, keepdims=True) + es)
    return e * inv, es * inv


def _stack_heads(ref, h, lo):
    pieces = []
    for jp in range(4):
        v = ref[:, BLK * (4 * h + jp):BLK * (4 * h + jp + 1)]
        zero = jnp.zeros_like(v)
        pieces += [jnp.where(lo, v, zero), jnp.where(lo, zero, v)]
    return jnp.concatenate(pieces, axis=0)


def _unstack_heads(v, jp, lo):
    return jnp.where(lo, v[256 * jp:256 * jp + 128], v[256 * jp + 128:256 * jp + 256])


def _attn_fwd(q, kv, sinks, comm=None):
    p = q.shape[0]
    nb = p // BLK

    def body(q_ref, km_ref, kp_ref, kc_ref, sink_ref, o_ref):
        n = pl.program_id(0)
        lo = lax.broadcasted_iota(jnp.int32, (BLK, BLK), 1) < HEAD_DIM
        for h in range(2):
            qs, k2, v2, bias, _ = _attn_setup(n, h, q_ref, km_ref, kp_ref, kc_ref)
            st = _dot_nt(k2, qs)
            pt = jnp.concatenate(
                [_attn_head(st[:, BLK * g:BLK * (g + 1)], bias, sink_ref[0, 8 * h + g])[0].astype(BF16) for g in range(8)],
                axis=1)
            o = _dot_tn(pt, v2)
            for jp in range(4):
                o_ref[:, BLK * (4 * h + jp):BLK * (4 * h + jp + 1)] = _unstack_heads(o, jp, lo).astype(BF16)

    return _call(
        body, name="attn_fwd", grid=(nb,),
        in_specs=[pl.BlockSpec((BLK, D), lambda i: (i, 0)),
                  pl.BlockSpec((BLK, 256), lambda i: (0, 0)),
                  pl.BlockSpec((BLK, 256), lambda i: (jnp.maximum(i - 1, 0), 0)),
                  pl.BlockSpec((BLK, 256), lambda i: (i, 0)),
                  pl.BlockSpec(memory_space=pltpu.SMEM)],
        out_specs=[pl.BlockSpec((BLK, D), lambda i: (i, 0))],
        out_shape=[jax.ShapeDtypeStruct((p, D), BF16)],
        args=(q, kv, kv, kv, sinks), comm=comm)


def _conv31_fwd(ag, w32, b, comm=None):
    p = ag.shape[0]
    nch = p // BLK

    def body(a_ref, g_ref, w_ref, b_ref, o_ref, gp):
        gp[0:32, :] = jnp.zeros((32, BLK), F32)
        for ci in range(nch):
            r0 = BLK * ci
            glu = a_ref[r0:r0 + BLK, :].astype(F32) * jax.nn.sigmoid(g_ref[r0:r0 + BLK, :].astype(F32))
            if ci == 0:
                glu = jnp.where(_rows(0, BLK) >= PAD, glu, 0.0)
            gp[32 + r0:32 + r0 + BLK, :] = glu
        for ci in range(nch):
            r0 = BLK * ci
            acc = jnp.broadcast_to(b_ref[...], (BLK, BLK))
            for j in range(CONV_K):
                acc = acc + w_ref[j:j + 1, :] * gp[r0 + j + 2:r0 + j + 2 + BLK, :]
            o_ref[r0:r0 + BLK, :] = acc

    return _call(
        body, name="conv31_fwd", grid=(D // BLK,),
        in_specs=[pl.BlockSpec((p, BLK), lambda j: (0, j)), pl.BlockSpec((p, BLK), lambda j: (0, 8 + j)),
                  pl.BlockSpec((32, BLK), lambda j: (0, j)), pl.BlockSpec((1, BLK), lambda j: (0, j))],
        out_specs=[pl.BlockSpec((p, BLK), lambda j: (0, j))],
        out_shape=[jax.ShapeDtypeStruct((p, D), F32)],
        scratch=[pltpu.VMEM((p + 32, BLK), F32)],
        args=(ag, ag, w32, b), comm=comm)


def _mixer_fwd(ao, c0, gates, h0p, wa, wc, wo, vecs):
    p = ao.shape[0]
    tm = _row_tile(p)

    def body(ao_ref, c0_ref, gt_ref, h_ref, wa_ref, wc_ref, wo_ref, v_ref,
             c1_ref, at_ref, cv_ref, mg_ref, mix_ref, h1_ref, n2_ref):
        i = pl.program_id(0)
        c1 = _lnsilu(c0_ref[...], v_ref[0:1, :], v_ref[1:2, :]).astype(BF16)
        c1_ref[...] = c1
        attn = _dot(ao_ref[...], wa_ref[...])
        conv = _dot(c1, wc_ref[...]) + v_ref[2:3, :]
        at_ref[...] = attn.astype(BF16)
        cv_ref[...] = conv.astype(BF16)
        merged = (jax.nn.sigmoid(gt_ref[:, 0:D].astype(F32)) * attn
                  + jax.nn.sigmoid(gt_ref[:, D:2 * D].astype(F32)) * conv).astype(BF16)
        mg_ref[...] = merged
        mix = _dot(merged, wo_ref[...])
        mix_ref[...] = mix
        h1 = jnp.where(_rows(i, tm) >= PAD, h_ref[...] + _rms(mix, v_ref[3:4, :]), 0.0)
        h1_ref[...] = h1
        n2_ref[...] = _rms(h1, v_ref[4:5, :]).astype(BF16)

    def row(w):
        return pl.BlockSpec((tm, w), lambda i: (i, 0))

    return pl.pallas_call(
        body, name="mixer_fwd", grid=(p // tm,),
        in_specs=[row(D), row(D), row(2 * D), row(D), VM, VM, VM, VM],
        out_specs=[row(D)] * 7,
        out_shape=[jax.ShapeDtypeStruct((p, D), t) for t in (BF16, BF16, BF16, BF16, F32, F32, BF16)],
        compiler_params=_cparams("parallel"),
    )(ao, c0, gates, h0p, wa, wc, wo, vecs)


def _mm_nt(a, w_t, name):
    p, k = a.shape
    n = w_t.shape[0]
    tm = _row_tile(p)
    ch = 512

    def body(a_ref, w_ref, o_ref):
        a_v = a_ref[...]
        for c0 in range(0, n, ch):
            o_ref[:, c0:c0 + ch] = _dot_nt(a_v, w_ref[c0:c0 + ch, :]).astype(BF16)

    return pl.pallas_call(
        body, name=name, grid=(p // tm,),
        in_specs=[pl.BlockSpec((tm, k), lambda i: (i, 0)), VM],
        out_specs=pl.BlockSpec((tm, n), lambda i: (i, 0)),
        out_shape=jax.ShapeDtypeStruct((p, n), BF16),
        compiler_params=_cparams("parallel"),
    )(a, w_t)


def _conv3(xp_ref, w_ref, r0):
    return (w_ref[0:1, :] * xp_ref[r0 + 6:r0 + 6 + BLK, :] + w_ref[1:2, :] * xp_ref[r0 + 7:r0 + 7 + BLK, :]
            + w_ref[2:3, :] * xp_ref[r0 + 8:r0 + 8 + BLK, :])


def _ffn_slab_specs(p):
    ncol = FFN // BLK
    return [pl.BlockSpec((p, BLK), lambda j: (0, j)), pl.BlockSpec((p, BLK), lambda j: (0, ncol + j)),
            pl.BlockSpec((FFN_K, BLK), lambda j: (0, j)), pl.BlockSpec((FFN_K, BLK), lambda j: (0, ncol + j)),
            pl.BlockSpec((1, BLK), lambda j: (0, j)), pl.BlockSpec((1, BLK), lambda j: (0, ncol + j))]


def _fill_shifted(dst, src_ref, nch):
    dst[0:8, :] = jnp.zeros((8, BLK), F32)
    for ci in range(nch):
        dst[8 + BLK * ci:8 + BLK * (ci + 1), :] = src_ref[BLK * ci:BLK * (ci + 1), :].astype(F32)


def _ffn_act(u0, fw, fb):
    p = u0.shape[0]
    nch = p // BLK

    def body(g_ref, v_ref, wg_ref, wv_ref, bg_ref, bv_ref, o_ref, dv_ref, dg_ref, xg, xv):
        _fill_shifted(xg, g_ref, nch)
        _fill_shifted(xv, v_ref, nch)
        for ci in range(nch):
            r0 = BLK * ci
            ug = _conv3(xg, wg_ref, r0) + bg_ref[...]
            uv = _conv3(xv, wv_ref, r0) + bv_ref[...]
            sg = jax.nn.sigmoid(ug)
            silu = ug * sg
            o_ref[r0:r0 + BLK, :] = (silu * uv).astype(BF16)
            dv_ref[r0:r0 + BLK, :] = silu.astype(BF16)
            dg_ref[r0:r0 + BLK, :] = (uv * (sg * (1.0 + ug * (1.0 - sg)))).astype(BF16)

    slab = pl.BlockSpec((p, BLK), lambda j: (0, j))
    return pl.pallas_call(
        body, name="ffn_act", grid=(FFN // BLK,),
        in_specs=_ffn_slab_specs(p),
        out_specs=[slab] * 3,
        out_shape=[jax.ShapeDtypeStruct((p, FFN), BF16)] * 3,
        scratch_shapes=[pltpu.VMEM((p + 8, BLK), F32)] * 2,
        compiler_params=_cparams("parallel"),
    )(u0, u0, fw, fw, fb, fb)


def _ffn_down_loss(act, wd, h1, tgt, gain):
    p = act.shape[0]
    tm = _row_tile(p)
    k = tm // BLK

    def body(*refs):
        a_ref, w_ref, h_ref = refs[:3]
        t_refs = refs[3:3 + k]
        g_ref, df_ref, da_ref, dy_ref, acc_ref = refs[3 + k:]
        i = pl.program_id(0)

        @pl.when(i == 0)
        def _():
            acc_ref[...] = jnp.zeros_like(acc_ref)

        ffn = _dot(a_ref[...], w_ref[...])
        r, vjp = jax.vjp(_rms, ffn, g_ref[...])
        t = jnp.concatenate([t_ref[...] for t_ref in t_refs], axis=0) if k > 1 else t_refs[0][...]
        diff = jnp.where(_rows(i, tm) >= BLK, h_ref[...] + r - t, 0.0)
        dy = diff * (1.0 / D)
        dffn, dg = vjp(dy)
        acc_ref[0:1, :] += dg
        acc_ref[1:2, :] += jnp.sum(diff * diff, axis=0, keepdims=True) * (0.5 / D)
        dy_ref[...] = dy
        dfb = dffn.astype(BF16)
        df_ref[...] = dfb
        for c0 in range(0, FFN, 256):
            da_ref[:, c0:c0 + 256] = _dot_nt(dfb, w_ref[c0:c0 + 256, :]).astype(BF16)

    def row(w):
        return pl.BlockSpec((tm, w), lambda i: (i, 0))

    return pl.pallas_call(
        body, name="ffn_down_loss", grid=(p // tm,),
        in_specs=[row(FFN), VM, row(D)] + _token_specs(tm) + [VM],
        out_specs=[row(D), row(FFN), row(D), pl.BlockSpec((8, D), lambda i: (0, 0))],
        out_shape=[jax.ShapeDtypeStruct((p, D), BF16), jax.ShapeDtypeStruct((p, FFN), BF16),
                   jax.ShapeDtypeStruct((p, D), F32), jax.ShapeDtypeStruct((8, D), F32)],
        compiler_params=_cparams("arbitrary"),
    )(act, wd, h1, *([tgt] * k), gain)


def _mm_tn(pieces, b, name, col_sums=False, comm=None):
    p, n = b.shape
    tk = 256
    nblk = [a.shape[1] // tk for a in pieces]
    offs = [sum(nblk[:q]) for q in range(len(pieces))]
    total = sum(nblk)
    npc = len(pieces)

    def body(*refs):
        a_refs, b_ref, o_ref = refs[:npc], refs[npc], refs[npc + 1]
        i = pl.program_id(0)
        for q, a_ref in enumerate(a_refs):
            @pl.when(jnp.logical_and(i >= offs[q], i < offs[q] + nblk[q]))
            def _(a_ref=a_ref):
                a_v = a_ref[...]
                o_ref[...] = _dot_tn(a_v, b_ref[...]).astype(BF16)
                if col_sums:
                    refs[npc + 2][...] = jnp.sum(a_v.astype(F32), axis=0, keepdims=True)

    def a_spec(q):
        return pl.BlockSpec((p, tk), lambda i: (0, jnp.clip(i - offs[q], 0, nblk[q] - 1)))

    out_specs = [pl.BlockSpec((tk, n), lambda i: (i, 0))]
    out_shape = [jax.ShapeDtypeStruct((total * tk, n), BF16)]
    if col_sums:
        out_specs.append(pl.BlockSpec((1, tk), lambda i: (0, i)))
        out_shape.append(jax.ShapeDtypeStruct((1, total * tk), F32))
    res, sent = _call(
        body, name=name, grid=(total,),
        in_specs=[a_spec(q) for q in range(npc)] + [VM],
        out_specs=out_specs, out_shape=out_shape, args=(*pieces, b), comm=comm)
    res = res if col_sums else res[0]
    return res if comm is None else (res, sent)


def _ffn_act_bwd(u0, dact, dact_dg, dact_dv, fw, act, dffn, comm=None):
    p = u0.shape[0]
    nch = p // BLK
    ncol = FFN // BLK

    def body(g_ref, v_ref, wg_ref, wv_ref, da_ref, lg_ref, lv_ref, act_ref, df_ref,
             dg_ref, dv_ref, gwg_ref, gwv_ref, gbg_ref, gbv_ref, gwd_ref, xg, xv, eg, ev):
        gwd_ref[...] = _dot_tn(act_ref[...], df_ref[...]).astype(BF16)
        _fill_shifted(xg, g_ref, nch)
        _fill_shifted(xv, v_ref, nch)
        eg[p:p + 8, :] = jnp.zeros((8, BLK), F32)
        ev[p:p + 8, :] = jnp.zeros((8, BLK), F32)
        for ci in range(nch):
            r0 = BLK * ci
            d = da_ref[r0:r0 + BLK, :].astype(F32)
            eg[r0:r0 + BLK, :] = d * lg_ref[r0:r0 + BLK, :].astype(F32)
            ev[r0:r0 + BLK, :] = d * lv_ref[r0:r0 + BLK, :].astype(F32)
        for e_s, x_s, w_ref, d_ref, gw_ref, gb_ref in ((eg, xg, wg_ref, dg_ref, gwg_ref, gbg_ref),
                                                      (ev, xv, wv_ref, dv_ref, gwv_ref, gbv_ref)):
            sums = [jnp.zeros((BLK, BLK), F32) for _ in range(FFN_K + 1)]
            for ci in range(nch):
                r0 = BLK * ci
                e0 = e_s[r0:r0 + BLK, :]
                du = (w_ref[2:3, :] * e0 + w_ref[1:2, :] * e_s[r0 + 1:r0 + 1 + BLK, :]
                      + w_ref[0:1, :] * e_s[r0 + 2:r0 + 2 + BLK, :])
                if ci == 0:
                    du = jnp.where(_rows(0, BLK) >= PAD, du, 0.0)
                d_ref[r0:r0 + BLK, :] = du.astype(BF16)
                for j in range(FFN_K):
                    sums[j] = sums[j] + e0 * x_s[r0 + 6 + j:r0 + 6 + j + BLK, :]
                sums[FFN_K] = sums[FFN_K] + e0
            for j in range(FFN_K):
                gw_ref[j:j + 1, :] = jnp.sum(sums[j], axis=0, keepdims=True)
            gb_ref[...] = jnp.sum(sums[FFN_K], axis=0, keepdims=True)

    slab = pl.BlockSpec((p, BLK), lambda j: (0, j))
    wspec = pl.BlockSpec((FFN_K, BLK), lambda j: (0, j))
    bspec = pl.BlockSpec((1, BLK), lambda j: (0, j))
    return _call(
        body, name="ffn_act_bwd", grid=(ncol,),
        in_specs=_ffn_slab_specs(p)[:4] + [slab] * 4 + [VM],
        out_specs=[slab, slab, wspec, wspec, bspec, bspec, pl.BlockSpec((BLK, D), lambda j: (j, 0))],
        out_shape=[jax.ShapeDtypeStruct((p, FFN), BF16)] * 2 + [jax.ShapeDtypeStruct((FFN_K, FFN), F32)] * 2
        + [jax.ShapeDtypeStruct((1, FFN), F32)] * 2 + [jax.ShapeDtypeStruct((FFN, D), BF16)],
        scratch=[pltpu.VMEM((p + 8, BLK), F32)] * 4,
        args=(u0, u0, fw, fw, dact, dact_dg, dact_dv, act, dffn), comm=comm)


def _ffn_in_bwd(dug, duv, w_upt, h1, dy, gain, comm=None):
    p = h1.shape[0]
    tm = _row_tile(p)

    def body(dg_ref, dv_ref, w_ref, h_ref, dy_ref, g_ref, o_ref, acc_ref):
        i = pl.program_id(0)

        @pl.when(i == 0)
        def _():
            acc_ref[...] = jnp.zeros_like(acc_ref)

        dn = _dot(dg_ref[...], w_ref[0:FFN, :]) + _dot(dv_ref[...], w_ref[FFN:2 * FFN, :])
        _, vjp = jax.vjp(_rms, h_ref[...], g_ref[...])
        dh, dg = vjp(dn)
        o_ref[...] = dy_ref[...] + dh
        acc_ref[0:1, :] += dg

    def row(w):
        return pl.BlockSpec((tm, w), lambda i: (i, 0))

    return _call(
        body, name="ffn_in_bwd", grid=(p // tm,),
        in_specs=[row(FFN), row(FFN), VM, row(D), row(D), VM],
        out_specs=[row(D), pl.BlockSpec((8, D), lambda i: (0, 0))],
        out_shape=[jax.ShapeDtypeStruct((p, D), F32), jax.ShapeDtypeStruct((8, D), F32)],
        sem="arbitrary", args=(dug, duv, w_upt, h1, dy, gain), comm=comm)


def _mixer_bwd(dh1, mix, attn, conv, gates, c0, wa, wc, wo, vecs, comm=None):
    p = dh1.shape[0]
    tm = _row_tile(p)

    def body(dh_ref, mix_ref, at_ref, cv_ref, gt_ref, c0_ref, wa_ref, wc_ref, wo_ref, v_ref,
             dmix_ref, dat_ref, dcv_ref, dgt_ref, dao_ref, dc0_ref, acc_ref):
        i = pl.program_id(0)

        @pl.when(i == 0)
        def _():
            acc_ref[...] = jnp.zeros_like(acc_ref)

        _, vjp = jax.vjp(_rms, mix_ref[...], v_ref[3:4, :])
        dmix, dgp = vjp(dh_ref[...])
        dmix = dmix.astype(BF16)
        dmix_ref[...] = dmix
        dmg = _dot_nt(dmix, wo_ref[...])
        sa = jax.nn.sigmoid(gt_ref[:, 0:D].astype(F32))
        sc = jax.nn.sigmoid(gt_ref[:, D:2 * D].astype(F32))
        dat = dmg * sa
        dcv = dmg * sc
        dgt_ref[:, 0:D] = (dmg * at_ref[...].astype(F32) * sa * (1.0 - sa)).astype(BF16)
        dgt_ref[:, D:2 * D] = (dmg * cv_ref[...].astype(F32) * sc * (1.0 - sc)).astype(BF16)
        datb = dat.astype(BF16)
        dcvb = dcv.astype(BF16)
        dat_ref[...] = datb
        dcv_ref[...] = dcvb
        dao_ref[...] = _dot_nt(datb, wa_ref[...]).astype(BF16)
        dc1 = _dot_nt(dcvb, wc_ref[...])
        _, vjp2 = jax.vjp(_lnsilu, c0_ref[...], v_ref[0:1, :], v_ref[1:2, :])
        dc0, dlg, dlb = vjp2(dc1)
        dc0_ref[...] = dc0
        acc_ref[0:1, :] += dgp
        acc_ref[1:2, :] += jnp.sum(dcv, axis=0, keepdims=True)
        acc_ref[2:3, :] += dlg
        acc_ref[3:4, :] += dlb

    def row(w):
        return pl.BlockSpec((tm, w), lambda i: (i, 0))

    return _call(
        body, name="mixer_bwd", grid=(p // tm,),
        in_specs=[row(D), row(D), row(D), row(D), row(2 * D), row(D), VM, VM, VM, VM],
        out_specs=[row(D), row(D), row(D), row(2 * D), row(D), row(D), pl.BlockSpec((8, D), lambda i: (0, 0))],
        out_shape=[jax.ShapeDtypeStruct((p, D), BF16)] * 3 + [jax.ShapeDtypeStruct((p, 2 * D), BF16),
                                                             jax.ShapeDtypeStruct((p, D), BF16),
                                                             jax.ShapeDtypeStruct((p, D), F32),
                                                             jax.ShapeDtypeStruct((8, D), F32)],
        sem="arbitrary", args=(dh1, mix, attn, conv, gates, c0, wa, wc, wo, vecs), comm=comm)


def _conv31_bwd(ag, dc0, w32, tn_pairs, comm=None):
    p = ag.shape[0]
    nch = p // BLK
    npair = len(tn_pairs)

    def body(*refs):
        a_ref, g_ref, dc_ref, w_ref = refs[:4]
        tn_a, tn_b = refs[4:4 + npair], refs[4 + npair:4 + 2 * npair]
        da_ref, dg_ref, gw_ref, gb_ref = refs[4 + 2 * npair:8 + 2 * npair]
        tn_o = refs[8 + 2 * npair:8 + 3 * npair]
        gp, dp = refs[8 + 3 * npair:]
        for ta, tb, to in zip(tn_a, tn_b, tn_o):
            to[...] = _dot_tn(ta[...], tb[...]).astype(BF16)
        gp[0:32, :] = jnp.zeros((32, BLK), F32)
        dp[p:p + 32, :] = jnp.zeros((32, BLK), F32)
        bsum = jnp.zeros((BLK, BLK), F32)
        for ci in range(nch):
            r0 = BLK * ci
            glu = a_ref[r0:r0 + BLK, :].astype(F32) * jax.nn.sigmoid(g_ref[r0:r0 + BLK, :].astype(F32))
            if ci == 0:
                glu = jnp.where(_rows(0, BLK) >= PAD, glu, 0.0)
            gp[32 + r0:32 + r0 + BLK, :] = glu
            d = dc_ref[r0:r0 + BLK, :]
            dp[r0:r0 + BLK, :] = d
            bsum = bsum + d
        gb_ref[...] = jnp.sum(bsum, axis=0, keepdims=True)
        for ci in range(nch):
            r0 = BLK * ci
            acc = jnp.zeros((BLK, BLK), F32)
            for j in range(CONV_K):
                acc = acc + w_ref[j:j + 1, :] * dp[r0 + 30 - j:r0 + 30 - j + BLK, :]
            if ci == 0:
                acc = jnp.where(_rows(0, BLK) >= PAD, acc, 0.0)
            a = a_ref[r0:r0 + BLK, :].astype(F32)
            sg = jax.nn.sigmoid(g_ref[r0:r0 + BLK, :].astype(F32))
            da_ref[r0:r0 + BLK, :] = (acc * sg).astype(BF16)
            dg_ref[r0:r0 + BLK, :] = (acc * a * sg * (1.0 - sg)).astype(BF16)
        for j in range(CONV_K):
            acc = jnp.zeros((BLK, BLK), F32)
            for ci in range(nch):
                r0 = BLK * ci
                acc = acc + dp[r0:r0 + BLK, :] * gp[r0 + j + 2:r0 + j + 2 + BLK, :]
            gw_ref[j:j + 1, :] = jnp.sum(acc, axis=0, keepdims=True)
        gw_ref[CONV_K:32, :] = jnp.zeros((32 - CONV_K, BLK), F32)

    slab = pl.BlockSpec((p, BLK), lambda j: (0, j))
    return _call(
        body, name="conv31_bwd", grid=(D // BLK,),
        in_specs=[slab, pl.BlockSpec((p, BLK), lambda j: (0, 8 + j)), slab, pl.BlockSpec((32, BLK), lambda j: (0, j))]
        + [slab] * npair + [VM] * npair,
        out_specs=[slab, slab, pl.BlockSpec((32, BLK), lambda j: (0, j)), pl.BlockSpec((1, BLK), lambda j: (0, j))]
        + [pl.BlockSpec((BLK, D), lambda j: (j, 0))] * npair,
        out_shape=[jax.ShapeDtypeStruct((p, D), BF16)] * 2 + [jax.ShapeDtypeStruct((32, D), F32),
                                                             jax.ShapeDtypeStruct((1, D), F32)]
        + [jax.ShapeDtypeStruct((D, D), BF16)] * npair,
        scratch=[pltpu.VMEM((p + 32, BLK), F32)] * 2,
        args=(ag, ag, dc0, w32, *[a for a, _ in tn_pairs], *[b for _, b in tn_pairs]), comm=comm)


def _attn_bwd(q, kv, dao, sinks, tabs, comm=None):
    p = q.shape[0]
    nb = p // BLK

    def body(q_ref, km_ref, kp_ref, kc_ref, do_ref, sink_ref, t_ref, dqkv_ref, dsink_ref, carry, macc):
        i = pl.program_id(0)
        n = nb - 1 - i

        @pl.when(i == 0)
        def _():
            carry[...] = jnp.zeros_like(carry)
            macc[...] = jnp.zeros_like(macc)
            dsink_ref[...] = jnp.zeros_like(dsink_ref)

        lo = lax.broadcasted_iota(jnp.int32, (BLK, BLK), 1) < HEAD_DIM
        lane8 = lax.broadcasted_iota(jnp.int32, (8, BLK), 1)
        c, s1, s2 = t_ref[:, 0:128], -t_ref[:, 128:256], -t_ref[:, 256:384]
        dk = jnp.zeros((N_KEY, BLK), F32)
        dv = jnp.zeros((N_KEY, BLK), F32)
        for h in range(2):
            qs, k2, v2, bias, lok = _attn_setup(n, h, q_ref, km_ref, kp_ref, kc_ref)
            dos = _stack_heads(do_ref, h, lo)
            st = _dot_nt(k2, qs)
            dpt = _dot_nt(v2, dos)
            p_parts, ds_parts = [], []
            for g in range(8):
                cols = slice(BLK * g, BLK * (g + 1))
                pn, ps = _attn_head(st[:, cols], bias, sink_ref[0, 8 * h + g])
                dp = dpt[:, cols]
                delta = jnp.sum(pn * dp, axis=0, keepdims=True)
                ds_parts.append((pn * (dp - delta)).astype(BF16))
                p_parts.append(pn.astype(BF16))
                dsk = -jnp.sum(ps * delta, axis=1, keepdims=True)
                dsink_ref[...] += jnp.where(lane8 == 8 * h + g, dsk, 0.0)
            dst = jnp.concatenate(ds_parts, axis=1)
            pt = jnp.concatenate(p_parts, axis=1)
            dq = _dot_tn(dst, k2)
            for jp in range(4):
                lo_c = BLK * (4 * h + jp)
                dqkv_ref[:, lo_c:lo_c + BLK] = (_rope(_unstack_heads(dq, jp, lo), c, s1, s2) * SCALE).astype(BF16)
            dk2 = _dot(dst, qs)
            dv2 = _dot(pt, dos)
            dk2 = dk2 + pltpu.roll(dk2, HEAD_DIM, 1)
            dv2 = dv2 + pltpu.roll(dv2, HEAD_DIM, 1)
            own = lok if h == 0 else jnp.logical_not(lok)
            dk = jnp.where(own, dk2, dk)
            dv = jnp.where(own, dv2, dv)
        macc[:, 0:BLK] += dk[2 * BLK:N_KEY]
        macc[:, BLK:2 * BLK] += dv[2 * BLK:N_KEY]
        last = (n == 0).astype(F32)
        zpad = jnp.zeros((PAD, BLK), F32)
        dk_c = dk[BLK:2 * BLK] + carry[:, 0:BLK] + last * jnp.concatenate([zpad, macc[:, 0:BLK]], axis=0)
        dv_c = dv[BLK:2 * BLK] + carry[:, BLK:2 * BLK] + last * jnp.concatenate([zpad, macc[:, BLK:2 * BLK]], axis=0)
        carry[:, 0:BLK] = dk[0:BLK]
        carry[:, BLK:2 * BLK] = dv[0:BLK]
        dqkv_ref[:, D:D + BLK] = _rope(dk_c, c, s1, s2).astype(BF16)
        dqkv_ref[:, D + BLK:D + 2 * BLK] = dv_c.astype(BF16)

    def rev(w):
        return pl.BlockSpec((BLK, w), lambda i: (nb - 1 - i, 0))

    return _call(
        body, name="attn_bwd", grid=(nb,),
        in_specs=[rev(D),
                  pl.BlockSpec((BLK, 256), lambda i: (0, 0)),
                  pl.BlockSpec((BLK, 256), lambda i: (jnp.maximum(nb - 2 - i, 0), 0)),
                  rev(256), rev(D),
                  pl.BlockSpec(memory_space=pltpu.SMEM), rev(384)],
        out_specs=[rev(QKV_W), pl.BlockSpec((8, BLK), lambda i: (0, 0))],
        out_shape=[jax.ShapeDtypeStruct((p, QKV_W), BF16), jax.ShapeDtypeStruct((8, BLK), F32)],
        scratch=[pltpu.VMEM((BLK, 256), F32), pltpu.VMEM((N_META, 256), F32)], sem="arbitrary",
        args=(q, kv, kv, kv, dao, sinks, tabs), comm=comm)


def _in_bwd(dqkv, da, dg, dgt, w_int, h0p, dh1, gain, comm=None):
    p = h0p.shape[0]
    tm = _row_tile(p)

    def body(dq_ref, da_ref, dg_ref, dt_ref, w_ref, h_ref, dh_ref, g_ref, o_ref, acc_ref):
        i = pl.program_id(0)

        @pl.when(i == 0)
        def _():
            acc_ref[...] = jnp.zeros_like(acc_ref)

        dn = (_dot(dq_ref[...], w_ref[0:QKV_W, :]) + _dot(da_ref[...], w_ref[QKV_W:QKV_W + D, :])
              + _dot(dg_ref[...], w_ref[QKV_W + D:QKV_W + 2 * D, :]) + _dot(dt_ref[...], w_ref[QKV_W + 2 * D:IN_W, :]))
        _, vjp = jax.vjp(_rms, h_ref[...], g_ref[...])
        dh, dgain = vjp(dn)
        o_ref[...] = dh_ref[...] + dh
        acc_ref[0:1, :] += dgain

    def row(w):
        return pl.BlockSpec((tm, w), lambda i: (i, 0))

    return _call(
        body, name="in_bwd", grid=(p // tm,),
        in_specs=[row(QKV_W), row(D), row(D), row(2 * D), VM, row(D), row(D), VM],
        out_specs=[row(D), pl.BlockSpec((8, D), lambda i: (0, 0))],
        out_shape=[jax.ShapeDtypeStruct((p, D), F32), jax.ShapeDtypeStruct((8, D), F32)],
        sem="arbitrary", args=(dqkv, da, dg, dgt, w_int, h0p, dh1, gain), comm=comm)


def _sum_slots(slots, name):
    r = slots.shape[0] // N_DEV
    cols = slots.shape[1]
    tr = r if r <= 352 else (r // 2 if (r // 2) % 16 == 0 else r // 3)
    steps = r // tr

    def body(*refs):
        acc = refs[0][...].astype(F32)
        for s in range(1, N_DEV):
            acc = acc + refs[s][...].astype(F32)
        refs[N_DEV][...] = acc

    return pl.pallas_call(
        body, name=name, grid=(steps,),
        in_specs=[pl.BlockSpec((tr, cols), functools.partial(lambda i, s: (s * steps + i, 0), s=s)) for s in range(N_DEV)],
        out_specs=pl.BlockSpec((tr, cols), lambda i: (i, 0)),
        out_shape=jax.ShapeDtypeStruct((r, cols), F32),
        compiler_params=_cparams("parallel"),
    )(*([slots] * N_DEV))


def _adamw_math(w, g, m, v):
    m_n = ADAM_B1 * m + (1.0 - ADAM_B1) * g
    v_n = ADAM_B2 * v + (1.0 - ADAM_B2) * jnp.square(g)
    m_hat = m_n / (1.0 - ADAM_B1 ** ADAM_STEP)
    v_hat = v_n / (1.0 - ADAM_B2 ** ADAM_STEP)
    return -ADAM_LR * (m_hat / (jnp.sqrt(v_hat) + ADAM_EPS) + ADAM_WD * w), m_n, v_n


def _sum_adamw(parts, w, m, v, name, nslots=N_DEV):
    r, cols = w.shape
    rs = r // len(parts)
    tr = rs if rs <= 352 else (rs // 2 if (rs // 2) % 16 == 0 else rs // 3)
    steps = rs // tr

    def body(*refs):
        w_ref, m_ref, v_ref, g_ref, d_ref, nm_ref, nv_ref = refs[nslots * len(parts):]
        i = pl.program_id(0)
        for q in range(len(parts)):
            @pl.when(i // steps == q)
            def _(q=q):
                g = refs[nslots * q][...].astype(F32)
                for s in range(1, nslots):
                    g = g + refs[nslots * q + s][...].astype(F32)
                g_ref[...] = g
                d_ref[...], nm_ref[...], nv_ref[...] = _adamw_math(w_ref[...], g, m_ref[...], v_ref[...])

    def slot_spec(q, s):
        return pl.BlockSpec((tr, cols), lambda i: (s * steps + jnp.clip(i - q * steps, 0, steps - 1), 0))

    spec = pl.BlockSpec((tr, cols), lambda i: (i, 0))
    return pl.pallas_call(
        body, name=name, grid=(steps * len(parts),),
        in_specs=[slot_spec(q, s) for q in range(len(parts)) for s in range(nslots)] + [spec] * 3,
        out_specs=[spec] * 4, out_shape=[jax.ShapeDtypeStruct((r, cols), F32)] * 4,
        compiler_params=_cparams("parallel"),
    )(*[a for a in parts for _ in range(nslots)], w, m, v)


def _adamw(w, g, m, v, name):
    r, cols = w.shape
    tr = 256 if r % 256 == 0 else r

    def body(w_ref, g_ref, m_ref, v_ref, d_ref, nm_ref, nv_ref):
        d_ref[...], nm_ref[...], nv_ref[...] = _adamw_math(w_ref[...], g_ref[...], m_ref[...], v_ref[...])

    spec = pl.BlockSpec((tr, cols), lambda i: (i, 0))
    return pl.pallas_call(
        body, name=name, grid=(r // tr,),
        in_specs=[spec] * 4, out_specs=[spec] * 3,
        out_shape=[jax.ShapeDtypeStruct((r, cols), F32)] * 3,
        compiler_params=_cparams("parallel"),
    )(w, g, m, v)


def _rope_tables(p):
    half = ROT_DIM // 2
    inv_freq = ROPE_THETA ** (-jnp.arange(half, dtype=F32) * 2.0 / ROT_DIM)
    pos = (jnp.arange(p) - PAD).astype(F32)
    ang = pos[:, None] * inv_freq[None, :]
    lane = jnp.arange(BLK)
    seg = (lane % HEAD_DIM) // half
    cos = jnp.cos(ang)[:, lane % half]
    sin = jnp.sin(ang)[:, lane % half]
    c = jnp.where(seg[None, :] < 2, cos, 1.0)
    s1 = jnp.where(seg[None, :] == 0, -sin, 0.0)
    s2 = jnp.where(seg[None, :] == 1, sin, 0.0)
    return jnp.concatenate([c, s1, s2], axis=1).astype(F32)


def _flat_pack(parts, rows):
    flat = jnp.concatenate([a.reshape(-1).astype(F32) for a in parts])
    return jnp.pad(flat, (0, rows * D - flat.shape[0])).reshape(rows, D)


def _flat_unpack(pack, shapes):
    flat = pack.reshape(-1)
    out, off = [], 0
    for s in shapes:
        size = 1
        for e in s:
            size *= e
        out.append(flat[off:off + size].reshape(s))
        off += size
    return out


def kernel(x, meta_tokens, norm_pre_mix, norm_post_mix, w_in, b_in, attn_sinks, w_attn_proj, conv_dw_w, conv_dw_b, conv_ln_g, conv_ln_b, w_conv_proj, b_conv_proj, w_out, norm_pre_ffn, norm_post_ffn, w_up, ffn_dw_w, ffn_dw_b, w_down, loss_target, m_meta_tokens, m_norm_pre_mix, m_norm_post_mix, m_w_in, m_b_in, m_attn_sinks, m_w_attn_proj, m_conv_dw_w, m_conv_dw_b, m_conv_ln_g, m_conv_ln_b, m_w_conv_proj, m_b_conv_proj, m_w_out, m_norm_pre_ffn, m_norm_post_ffn, m_w_up, m_ffn_dw_w, m_ffn_dw_b, m_w_down, v_meta_tokens, v_norm_pre_mix, v_norm_post_mix, v_w_in, v_b_in, v_attn_sinks, v_w_attn_proj, v_conv_dw_w, v_conv_dw_b, v_conv_ln_g, v_conv_ln_b, v_w_conv_proj, v_b_conv_proj, v_w_out, v_norm_pre_ffn, v_norm_post_ffn, v_w_up, v_ffn_dw_w, v_ffn_dw_b, v_w_down):
    seq = x.shape[1]
    p = seq + BLK
    me = 4 * lax.axis_index("x") + 2 * lax.axis_index("y") + lax.axis_index("c")
    in_cols = w_in.shape[2]
    up_cols = w_up.shape[2]

    small = jnp.zeros((56, up_cols), F32)
    small = small.at[0:N_META, 0:BLK].set(meta_tokens)
    small = small.at[16:16 + CONV_K, 0:BLK].set(conv_dw_w[0])
    small = small.at[48:48 + FFN_K, :].set(ffn_dw_w[0])
    w_int, small_all = _exchange(_Gather([w_in[0].T.astype(BF16), small]), "gather_w_in")
    small_all = small_all.reshape(N_DEV, 56, up_cols)
    meta_full = small_all[:, 0:N_META, 0:BLK].transpose(1, 0, 2).reshape(N_META, D)
    cdw = small_all[:, 16:16 + CONV_K, 0:BLK].transpose(1, 0, 2).reshape(CONV_K, D)
    cdw32 = jnp.pad(cdw, ((0, 32 - CONV_K), (0, 0)))
    fdw = small_all[:, 48:48 + FFN_K, :].transpose(1, 0, 2).reshape(FFN_K, 2 * FFN)

    tabs = _rope_tables(p)
    vecs = jnp.concatenate([conv_ln_g, conv_ln_b, b_conv_proj, norm_post_mix, norm_pre_ffn, jnp.zeros((3, D), F32)], axis=0)

    (h0p, n1, q, kv, ag, gates), (wa, wc, wo) = _in_proj(
        x[0], meta_full, norm_pre_mix, w_int, b_in, tabs,
        comm=_Gather([w_attn_proj[0].astype(BF16), w_conv_proj[0].astype(BF16), w_out[0].astype(BF16)]))
    (ao,), (w_upt,) = _attn_fwd(q, kv, attn_sinks, comm=_Gather([w_up[0].T.astype(BF16)]))
    (c0,), (wd,) = _conv31_fwd(ag, cdw32, conv_dw_b, comm=_Gather([w_down[0].astype(BF16)]))
    c1, attn, conv, merged, mix, h1, n2 = _mixer_fwd(ao, c0, gates, h0p, wa, wc, wo, vecs)
    u0 = _mm_nt(n2, w_upt, "ffn_up")
    act, dact_dv, dact_dg = _ffn_act(u0, fdw, ffn_dw_b)
    dffn, dact, dy, acc_f = _ffn_down_loss(act, wd, h1, loss_target[0], norm_post_ffn)

    (dug, duv, gfw_g, gfw_v, gfb_g, gfb_v, g_wd), _ = _ffn_act_bwd(u0, dact, dact_dg, dact_dv, fdw, act, dffn)
    g_wupt, (s_wd,) = _mm_tn([dug, duv], n2, "grad_w_up", comm=_Scatter([g_wd]))
    (dh1, acc_u), (s_wup0,) = _ffn_in_bwd(dug, duv, w_upt, h1, dy, norm_pre_ffn, comm=_Scatter([g_wupt], 0, 2))
    (dmix, dat, dcv, dgt, dao, dc0, acc_m), (s_wup1,) = _mixer_bwd(
        dh1, mix, attn, conv, gates, c0, wa, wc, wo, vecs, comm=_Scatter([g_wupt], 1, 2))
    (da, dg, g_cdw, g_cdb, g_wo, g_wa, g_wc), _ = _conv31_bwd(ag, dc0, cdw32, [(merged, dmix), (ao, dat), (c1, dcv)])
    (dqkv, dsink), (s_wa, s_wc, s_wo) = _attn_bwd(q, kv, dao, attn_sinks, tabs, comm=_Scatter([g_wa, g_wc, g_wo]))
    g_wint, g_bin = _mm_tn([dqkv, da, dg, dgt], n1, "grad_w_in", col_sums=True)
    (from_sibling,) = _exchange(_SiblingSwap(g_wint), "swap_w_in")
    loss_row = jnp.sum(acc_f[1:2, :], axis=1, keepdims=True)
    early = [loss_row, acc_m[0:1], g_bin, dsink[0:1, 0:16], g_cdw[0:CONV_K], g_cdb,
             acc_m[2:3], acc_m[3:4], acc_m[1:2], acc_u[0:1], acc_f[0:1],
             jnp.concatenate([gfw_g, gfw_v], axis=1), jnp.concatenate([gfb_g, gfb_v], axis=1)]
    (dh0, acc_i), (s_win, gathered_early) = _in_bwd(
        dqkv, da, dg, dgt, w_int, h0p, dh1, norm_pre_mix,
        comm=_Both(_ChipScatter(_pair_add(g_wint, from_sibling)), _Gather([_flat_pack(early, 72)])))

    big = []
    for nm, parts, nslots, w, m, v, tr in (
            ("w_in", [s_win], N_CHIP, w_in, m_w_in, v_w_in, True), ("w_up", [s_wup0, s_wup1], N_DEV, w_up, m_w_up, v_w_up, True),
            ("w_attn_proj", [s_wa], N_DEV, w_attn_proj, m_w_attn_proj, v_w_attn_proj, False),
            ("w_conv_proj", [s_wc], N_DEV, w_conv_proj, m_w_conv_proj, v_w_conv_proj, False),
            ("w_out", [s_wo], N_DEV, w_out, m_w_out, v_w_out, False),
            ("w_down", [s_wd], N_DEV, w_down, m_w_down, v_w_down, False)):
        ins = [a[0].T if tr else a[0] for a in (w, m, v)]
        big.append(tuple((o.T if tr else o)[None] for o in _sum_adamw(parts, *ins, "update_" + nm, nslots)))

    late = [dh0[PAD:BLK], acc_i[0:1]]
    (gathered_late,) = _exchange(_Gather([_flat_pack(late, 24)]), "gather_small_grads")
    g_meta, g_npm = _flat_unpack(_sum_slots(gathered_late, "sum_late_grads"), [a.shape for a in late])
    tot = _flat_unpack(_sum_slots(gathered_early, "sum_small_grads"), [a.shape for a in early])
    (loss, g_nqm, g_bi, g_sk, g_cw, g_cb, g_lg, g_lb, g_bc, g_npf, g_nqf, g_fw, g_fb) = tot
    loss = loss.reshape(())
    g_meta = lax.dynamic_slice_in_dim(g_meta, me * BLK, BLK, axis=1)
    g_cw = lax.dynamic_slice_in_dim(g_cw, me * BLK, BLK, axis=1)[None]
    g_fw = lax.dynamic_slice_in_dim(g_fw, me * up_cols, up_cols, axis=1)[None]

    sm_w = [meta_tokens, norm_pre_mix, norm_post_mix, b_in, attn_sinks, conv_dw_w, conv_dw_b, conv_ln_g, conv_ln_b,
            b_conv_proj, norm_pre_ffn, norm_post_ffn, ffn_dw_w, ffn_dw_b]
    sm_g = [g_meta, g_npm, g_nqm, g_bi, g_sk, g_cw, g_cb, g_lg, g_lb, g_bc, g_npf, g_nqf, g_fw, g_fb]
    sm_m = [m_meta_tokens, m_norm_pre_mix, m_norm_post_mix, m_b_in, m_attn_sinks, m_conv_dw_w, m_conv_dw_b, m_conv_ln_g,
            m_conv_ln_b, m_b_conv_proj, m_norm_pre_ffn, m_norm_post_ffn, m_ffn_dw_w, m_ffn_dw_b]
    sm_v = [v_meta_tokens, v_norm_pre_mix, v_norm_post_mix, v_b_in, v_attn_sinks, v_conv_dw_w, v_conv_dw_b, v_conv_ln_g,
            v_conv_ln_b, v_b_conv_proj, v_norm_pre_ffn, v_norm_post_ffn, v_ffn_dw_w, v_ffn_dw_b]
    sm_shapes = [a.shape for a in sm_w]
    upd_rows = 32
    v_pack = _flat_pack(sm_v, upd_rows)
    sm_out = _adamw(_flat_pack(sm_w, upd_rows), _flat_pack(sm_g, upd_rows), _flat_pack(sm_m, upd_rows), v_pack, "adamw_small")
    sm_d, sm_nm, sm_nv = (_flat_unpack(o, sm_shapes) for o in sm_out)

    order = ["meta_tokens", "norm_pre_mix", "norm_post_mix", "w_in", "b_in", "attn_sinks", "w_attn_proj", "conv_dw_w",
             "conv_dw_b", "conv_ln_g", "conv_ln_b", "w_conv_proj", "b_conv_proj", "w_out", "norm_pre_ffn", "norm_post_ffn",
             "w_up", "ffn_dw_w", "ffn_dw_b", "w_down"]
    small_names = ["meta_tokens", "norm_pre_mix", "norm_post_mix", "b_in", "attn_sinks", "conv_dw_w", "conv_dw_b", "conv_ln_g",
                   "conv_ln_b", "b_conv_proj", "norm_pre_ffn", "norm_post_ffn", "ffn_dw_w", "ffn_dw_b"]
    big_names = ["w_in", "w_up", "w_attn_proj", "w_conv_proj", "w_out", "w_down"]
    table = {}
    for k, nm in enumerate(small_names):
        table[nm] = (sm_g[k], sm_d[k], sm_nm[k], sm_nv[k])
    for k, nm in enumerate(big_names):
        table[nm] = big[k]
    grad_x = dh0[BLK:][None]
    outs = [loss, grad_x]
    for field in range(4):
        outs += [table[nm][field] for nm in order]
    return tuple(outs)
```

```python
import functools

import jax
import jax.numpy as jnp
from jax import lax
from jax.experimental import pallas as pl
from jax.experimental.pallas import tpu as pltpu

F32 = jnp.float32
BF16 = jnp.bfloat16
MESH = pl.DeviceIdType.MESH

D = 1024
HEAD_DIM = 64
N_META = 16
BLK = 128
PAD = BLK - N_META
CONV_K = 31
FFN = 2816
FFN_K = 3
QKV_W = 1280
IN_W = 5376
ROT_DIM = 16
ROPE_THETA = 500000.0
RMS_EPS = 1e-6
LN_EPS = 1e-5
NEG_INF = -1e30
SCALE = HEAD_DIM ** -0.5
N_DEV = 8

ADAM_LR = 0.001
ADAM_B1 = 0.9
ADAM_B2 = 0.999
ADAM_EPS = 1e-08
ADAM_WD = 0.01
ADAM_STEP = 10

VMEM_BYTES_V7X = 64 * 1024 * 1024
VMEM_LIMIT = VMEM_BYTES_V7X - 8 * 1024 * 1024

NT = (((1,), (1,)), ((), ()))
TN = (((0,), (0,)), ((), ()))
VM = pl.BlockSpec(memory_space=pltpu.VMEM)
ANY = pl.BlockSpec(memory_space=pl.ANY)


def _cparams(*sem):
    return pltpu.CompilerParams(dimension_semantics=sem or None, vmem_limit_bytes=VMEM_LIMIT)


def _row_tile(p):
    return 384 if p % 384 == 0 else 128


def _dot(a, b):
    return jnp.dot(a, b, preferred_element_type=F32)


def _dot_nt(a, b):
    return lax.dot_general(a, b, NT, preferred_element_type=F32)


def _dot_tn(a, b):
    return lax.dot_general(a, b, TN, preferred_element_type=F32)


def _rms(x, g):
    return x * lax.rsqrt(jnp.mean(x * x, axis=-1, keepdims=True) + RMS_EPS) * g


def _lnsilu(x, g, b):
    mu = jnp.mean(x, axis=-1, keepdims=True)
    var = jnp.mean(jnp.square(x - mu), axis=-1, keepdims=True)
    z = (x - mu) * lax.rsqrt(var + LN_EPS) * g + b
    return z * jax.nn.sigmoid(z)


def _rope(v, c, s1, s2):
    return v * c + pltpu.roll(v, BLK - 8, 1) * s1 + pltpu.roll(v, 8, 1) * s2


def _rows(i, tm):
    return i * tm + lax.broadcasted_iota(jnp.int32, (tm, 1), 0)


def _place():
    return lax.axis_index("x"), lax.axis_index("y"), lax.axis_index("c")


def _blk(ref, idx, r, dtype):
    return ref.at[pl.ds(pl.multiple_of(idx * r, 16 if dtype == BF16 else 8), r), :]


class _Gather:
    def __init__(self, arrs):
        self.ins = list(arrs)
        n = len(arrs)
        self.out_shape = [jax.ShapeDtypeStruct((N_DEV * a.shape[0], a.shape[1]), a.dtype) for a in arrs]
        self.scratch = [pltpu.SemaphoreType.DMA((n, 7)), pltpu.SemaphoreType.DMA((n, 7)), pltpu.SemaphoreType.DMA((n,))]

    def _parts(self, ins, outs, sems):
        send_sems, recv_sems, local_sems = sems
        n = len(ins)
        x, y, c = _place()
        me, sibling = (x, y, c), (x, y, 1 - c)
        chips = [(1 - x, y), (x, 1 - y), (1 - x, 1 - y)]

        def rows(a, p):
            return _blk(outs[a], 4 * p[0] + 2 * p[1] + p[2], self.ins[a].shape[0], self.ins[a].dtype)

        def copy(a, k, block, to, src=None):
            return pltpu.make_async_remote_copy(
                src_ref=rows(a, block) if src is None else src, dst_ref=rows(a, block),
                send_sem=send_sems.at[a, k], recv_sem=recv_sems.at[a, k], device_id=to, device_id_type=MESH)

        mine = [pltpu.make_async_copy(ins[a], rows(a, me), local_sems.at[a]) for a in range(n)]
        first = []
        for a in range(n):
            first.append(copy(a, 0, me, sibling, src=ins[a]))
            first += [copy(a, 1 + j, me, (*chip, c), src=ins[a]) for j, chip in enumerate(chips)]
        return n, c, me, sibling, chips, copy, mine, first

    def start(self, ins, outs, sems):
        *_, mine, first = self._parts(ins, outs, sems)
        for cp in mine + first:
            cp.start()

    def finish(self, ins, outs, sems):
        n, c, me, sibling, chips, copy, mine, first = self._parts(ins, outs, sems)
        passed = []
        for j, chip in enumerate(chips):
            for a in range(n):
                copy(a, 1 + j, (*chip, c), me).wait_recv()
                fwd = copy(a, 4 + j, (*chip, c), sibling)
                fwd.start()
                passed.append(fwd)
        for a in range(n):
            copy(a, 0, sibling, me).wait_recv()
            for j, chip in enumerate(chips):
                copy(a, 4 + j, (*chip, 1 - c), me).wait_recv()
        for cp in first + passed:
            cp.wait_send()
        for cp in mine:
            cp.wait()


FLIPS = [(0, 0, 1), (1, 0, 0), (0, 1, 0), (1, 1, 0), (1, 0, 1), (0, 1, 1), (1, 1, 1)]


class _Scatter:
    def __init__(self, arrs, part=0, nparts=1):
        self.ins = list(arrs)
        self.part, self.nparts = part, nparts
        n = len(arrs)
        self.out_shape = [jax.ShapeDtypeStruct((a.shape[0] // nparts, a.shape[1]), a.dtype) for a in arrs]
        self.scratch = [pltpu.SemaphoreType.DMA((n, 7)), pltpu.SemaphoreType.DMA((n, 7)), pltpu.SemaphoreType.DMA((n,))]

    def _parts(self, ins, outs, sems):
        send_sems, recv_sems, local_sems = sems
        n = len(ins)
        x, y, c = _place()
        me = 4 * x + 2 * y + c

        def flip(v, f):
            return 1 - v if f else v

        def src(a, idx):
            r = self.ins[a].shape[0] // N_DEV
            rs = r // self.nparts
            return ins[a].at[pl.ds(pl.multiple_of(idx * r + self.part * rs, 16), rs), :]

        def dst(a, idx):
            rs = self.ins[a].shape[0] // N_DEV // self.nparts
            return outs[a].at[pl.ds(pl.multiple_of(idx * rs, 16), rs), :]

        mine = [pltpu.make_async_copy(src(a, me), dst(a, me), local_sems.at[a]) for a in range(n)]
        sends, recvs = [], []
        for k, f in enumerate(FLIPS):
            peer = (flip(x, f[0]), flip(y, f[1]), flip(c, f[2]))
            pidx = 4 * peer[0] + 2 * peer[1] + peer[2]
            for a in range(n):
                sends.append(pltpu.make_async_remote_copy(
                    src_ref=src(a, pidx), dst_ref=dst(a, me),
                    send_sem=send_sems.at[a, k], recv_sem=recv_sems.at[a, k], device_id=peer, device_id_type=MESH))
                recvs.append(functools.partial(
                    pltpu.make_async_remote_copy,
                    src_ref=src(a, pidx), dst_ref=dst(a, pidx),
                    send_sem=send_sems.at[a, k], recv_sem=recv_sems.at[a, k], device_id=peer, device_id_type=MESH))
        return mine, sends, recvs

    def start(self, ins, outs, sems):
        mine, sends, _ = self._parts(ins, outs, sems)
        for cp in mine + sends:
            cp.start()

    def finish(self, ins, outs, sems):
        mine, sends, recvs = self._parts(ins, outs, sems)
        for make in recvs:
            make().wait_recv()
        for cp in sends:
            cp.wait_send()
        for cp in mine:
            cp.wait()


N_CHIP = 4


class _SiblingSwap:
    def __init__(self, arr):
        self.ins = [arr]
        self.r = arr.shape[0] // N_DEV
        self.out_shape = [jax.ShapeDtypeStruct((N_CHIP * self.r, arr.shape[1]), arr.dtype)]
        self.scratch = [pltpu.SemaphoreType.DMA((N_CHIP,)), pltpu.SemaphoreType.DMA((N_CHIP,))]

    def _copies(self, ins, outs, sems):
        send_sems, recv_sems = sems
        x, y, c = _place()
        r = self.r
        return [pltpu.make_async_remote_copy(
            src_ref=ins[0].at[pl.ds(pl.multiple_of((2 * j + 1 - c) * r, 16), r), :],
            dst_ref=outs[0].at[pl.ds(j * r, r), :],
            send_sem=send_sems.at[j], recv_sem=recv_sems.at[j], device_id=(x, y, 1 - c), device_id_type=MESH)
            for j in range(N_CHIP)]

    def start(self, ins, outs, sems):
        for cp in self._copies(ins, outs, sems):
            cp.start()

    def finish(self, ins, outs, sems):
        for cp in self._copies(ins, outs, sems):
            cp.wait()


class _ChipScatter:
    def __init__(self, arr):
        self.ins = [arr]
        self.r = arr.shape[0] // N_CHIP
        self.out_shape = [jax.ShapeDtypeStruct(arr.shape, arr.dtype)]
        self.scratch = [pltpu.SemaphoreType.DMA((3,)), pltpu.SemaphoreType.DMA((3,)), pltpu.SemaphoreType.DMA]

    def _parts(self, ins, outs, sems):
        send_sems, recv_sems, local_sem = sems
        x, y, c = _place()
        r = self.r
        my_chip = 2 * x + y

        def rows(ref, j):
            return ref.at[pl.ds(pl.multiple_of(j * r, 16), r), :]

        mine = pltpu.make_async_copy(rows(ins[0], my_chip), rows(outs[0], my_chip), local_sem)
        sends, recvs = [], []
        for k, (fx, fy) in enumerate(((1, 0), (0, 1), (1, 1))):
            px, py = (1 - x if fx else x), (1 - y if fy else y)
            peer_chip = 2 * px + py
            sends.append(pltpu.make_async_remote_copy(
                src_ref=rows(ins[0], peer_chip), dst_ref=rows(outs[0], my_chip),
                send_sem=send_sems.at[k], recv_sem=recv_sems.at[k], device_id=(px, py, c), device_id_type=MESH))
            recvs.append(functools.partial(
                pltpu.make_async_remote_copy,
                src_ref=rows(ins[0], peer_chip), dst_ref=rows(outs[0], peer_chip),
                send_sem=send_sems.at[k], recv_sem=recv_sems.at[k], device_id=(px, py, c), device_id_type=MESH))
        return mine, sends, recvs

    def start(self, ins, outs, sems):
        mine, sends, _ = self._parts(ins, outs, sems)
        for cp in [mine] + sends:
            cp.start()

    def finish(self, ins, outs, sems):
        mine, sends, recvs = self._parts(ins, outs, sems)
        for make in recvs:
            make().wait_recv()
        for cp in sends:
            cp.wait_send()
        mine.wait()


def _pair_add(partial, recv):
    r = recv.shape[0] // N_CHIP
    cols = recv.shape[1]
    tr = r // 2 if (r // 2) % 16 == 0 else r
    steps = r // tr
    core = lax.axis_index("c").astype(jnp.int32).reshape(1)

    def body(c_ref, p_ref, s_ref, o_ref):
        o_ref[...] = (p_ref[...].astype(F32) + s_ref[...].astype(F32)).astype(BF16)

    spec = pl.BlockSpec((tr, cols), lambda j, i, c_ref: (j * steps + i, 0))
    return pl.pallas_call(
        body, name="pair_add",
        grid_spec=pltpu.PrefetchScalarGridSpec(
            num_scalar_prefetch=1, grid=(N_CHIP, steps),
            in_specs=[pl.BlockSpec((tr, cols), lambda j, i, c_ref: ((2 * j + c_ref[0]) * steps + i, 0)), spec],
            out_specs=spec),
        out_shape=jax.ShapeDtypeStruct(recv.shape, BF16),
        compiler_params=_cparams("parallel", "parallel"),
    )(core, partial, recv)


class _Both:
    def __init__(self, a, b):
        self.a, self.b = a, b
        self.ins = a.ins + b.ins
        self.out_shape = a.out_shape + b.out_shape
        self.scratch = a.scratch + b.scratch

    def _split(self, ins, outs, sems):
        ni, no, ns = len(self.a.ins), len(self.a.out_shape), len(self.a.scratch)
        return (ins[:ni], outs[:no], sems[:ns]), (ins[ni:], outs[no:], sems[ns:])

    def start(self, ins, outs, sems):
        ra, rb = self._split(ins, outs, sems)
        self.a.start(*ra)
        self.b.start(*rb)

    def finish(self, ins, outs, sems):
        ra, rb = self._split(ins, outs, sems)
        self.a.finish(*ra)
        self.b.finish(*rb)


def _exchange(comm, name):
    n, m = len(comm.ins), len(comm.out_shape)

    def body(*refs):
        ins, outs, sems = refs[:n], refs[n:n + m], refs[n + m:]
        comm.start(ins, outs, sems)
        comm.finish(ins, outs, sems)

    return pl.pallas_call(
        body, name=name, out_shape=comm.out_shape, in_specs=[ANY] * n, out_specs=[ANY] * m, scratch_shapes=comm.scratch,
    )(*comm.ins)


def _call(body, *, name, grid, in_specs, out_specs, out_shape, args, scratch=(), sem="parallel", comm=None):
    if comm is None:
        outs = pl.pallas_call(
            body, name=name, grid=grid, in_specs=list(in_specs), out_specs=list(out_specs), out_shape=list(out_shape),
            scratch_shapes=list(scratch), compiler_params=_cparams(sem))(*args)
        return outs, []
    n_in, n_out, n_sc = len(in_specs), len(out_specs), len(scratch)
    n_ci, n_co = len(comm.ins), len(comm.out_shape)
    last = grid[0] - 1

    def fused(*refs):
        ins, refs = refs[:n_in], refs[n_in:]
        c_ins, refs = refs[:n_ci], refs[n_ci:]
        outs, refs = refs[:n_out], refs[n_out:]
        c_outs, refs = refs[:n_co], refs[n_co:]
        sc, c_sems = refs[:n_sc], refs[n_sc:]
        step = pl.program_id(0)

        @pl.when(step == 0)
        def _():
            comm.start(c_ins, c_outs, c_sems)

        body(*ins, *outs, *sc)

        @pl.when(step == last)
        def _():
            comm.finish(c_ins, c_outs, c_sems)

    outs = pl.pallas_call(
        fused, name=name, grid=grid, in_specs=list(in_specs) + [ANY] * n_ci, out_specs=list(out_specs) + [ANY] * n_co,
        out_shape=list(out_shape) + comm.out_shape, scratch_shapes=list(scratch) + comm.scratch,
        compiler_params=_cparams("arbitrary"))(*args, *comm.ins)
    return outs[:n_out], outs[n_out:]


def _token_specs(tm):
    k = tm // BLK
    return [pl.BlockSpec((BLK, D), functools.partial(lambda i, t: (jnp.maximum(k * i + t - 1, 0), 0), t=t)) for t in range(k)]


def _in_proj(x2d, meta, gain, w_int, b_in, tabs, comm=None):
    p = x2d.shape[0] + BLK
    tm = _row_tile(p)
    k = tm // BLK

    def body(*refs):
        x_refs = refs[:k]
        m_ref, g_ref, w_ref, b_ref, t_ref, h_ref, n1_ref, q_ref, kv_ref, ag_ref, gt_ref = refs[k:]
        i = pl.program_id(0)
        head = jnp.concatenate([jnp.zeros((PAD, D), F32), m_ref[...]], axis=0)
        first = jnp.where(i == 0, head, x_refs[0][...])
        h = jnp.concatenate([first] + [r[...] for r in x_refs[1:]], axis=0) if k > 1 else first
        h_ref[...] = h
        n = _rms(h, g_ref[...]).astype(BF16)
        n1_ref[...] = n
        c, s1, s2 = t_ref[:, 0:128], t_ref[:, 128:256], t_ref[:, 256:384]

        def mm(c0, w):
            return _dot_nt(n, w_ref[c0:c0 + w, :]) + b_ref[:, c0:c0 + w]

        for j in range(4):
            acc = mm(256 * j, 256)
            for t in range(2):
                lo = 256 * j + 128 * t
                q_ref[:, lo:lo + 128] = (_rope(acc[:, 128 * t:128 * (t + 1)], c, s1, s2) * SCALE).astype(BF16)
        acc = mm(1024, 256)
        kv_ref[:, 0:128] = _rope(acc[:, 0:128], c, s1, s2).astype(BF16)
        kv_ref[:, 128:256] = acc[:, 128:256].astype(BF16)
        for j in range(8):
            ag_ref[:, 256 * j:256 * (j + 1)] = mm(QKV_W + 256 * j, 256).astype(BF16)
        for j in range(8):
            gt_ref[:, 256 * j:256 * (j + 1)] = mm(QKV_W + 2048 + 256 * j, 256).astype(BF16)

    def row(w):
        return pl.BlockSpec((tm, w), lambda i: (i, 0))

    return _call(
        body, name="in_proj", grid=(p // tm,),
        in_specs=_token_specs(tm) + [VM, VM, VM, VM, row(384)],
        out_specs=[row(D), row(D), row(D), row(256), row(2048), row(2048)],
        out_shape=[jax.ShapeDtypeStruct((p, D), F32)] + [jax.ShapeDtypeStruct((p, w), BF16) for w in (D, D, 256, 2048, 2048)],
        args=(x2d,) * k + (meta, gain, w_int, b_in, tabs), comm=comm)


N_KEY = 2 * BLK + N_META


def _attn_setup(n, h, q_ref, km_ref, kp_ref, kc_ref):
    lo = lax.broadcasted_iota(jnp.int32, (BLK, BLK), 1) < HEAD_DIM
    lok = lax.broadcasted_iota(jnp.int32, (N_KEY, BLK), 1) < HEAD_DIM

    def dup(lanes):
        cat = jnp.concatenate([kp_ref[:, lanes], kc_ref[:, lanes], km_ref[PAD:BLK, lanes]], axis=0).astype(F32)
        rolled = pltpu.roll(cat, HEAD_DIM, 1)
        return (jnp.where(lok, cat, rolled) if h == 0 else jnp.where(lok, rolled, cat)).astype(BF16)

    k2 = dup(slice(0, 128))
    v2 = dup(slice(128, 256))
    qs = _stack_heads(q_ref, h, lo)

    kr = lax.broadcasted_iota(jnp.int32, (BLK, BLK), 0)
    tq = BLK * n + lax.broadcasted_iota(jnp.int32, (BLK, BLK), 1) - PAD
    t_p = BLK * (n - 1) + kr - PAD
    t_c = BLK * n + kr - PAD
    ok_p = jnp.logical_and(t_p >= N_META, tq - t_p < BLK)
    ok_c = jnp.logical_and(t_c >= N_META, t_c <= tq)
    ok_m = lax.broadcasted_iota(jnp.int32, (N_META, BLK), 0) <= BLK * n + lax.broadcasted_iota(jnp.int32, (N_META, BLK), 1) - PAD
    bias = jnp.concatenate([jnp.where(ok, 0.0, NEG_INF).astype(F32) for ok in (ok_p, ok_c, ok_m)], axis=0)
    return qs, k2, v2, bias, lok


def _attn_head(s, bias, sink):
    s = s + bias
    m = jnp.maximum(jnp.max(s, axis=0, keepdims=True), sink)
    e = jnp.exp(s - m)
    es = jnp.exp(sink - m)
    inv = 1.0 / (jnp.sum(e, axis=0, keepdims=True) + es)
    return e * inv, es * inv


def _stack_heads(ref, h, lo):
    pieces = []
    for jp in range(4):
        v = ref[:, BLK * (4 * h + jp):BLK * (4 * h + jp + 1)]
        zero = jnp.zeros_like(v)
        pieces += [jnp.where(lo, v, zero), jnp.where(lo, zero, v)]
    return jnp.concatenate(pieces, axis=0)


def _unstack_heads(v, jp, lo):
    return jnp.where(lo, v[256 * jp:256 * jp + 128], v[256 * jp + 128:256 * jp + 256])


def _attn_fwd(q, kv, sinks, comm=None):
    p = q.shape[0]
    nb = p // BLK

    def body(q_ref, km_ref, kp_ref, kc_ref, sink_ref, o_ref):
        n = pl.program_id(0)
        lo = lax.broadcasted_iota(jnp.int32, (BLK, BLK), 1) < HEAD_DIM
        for h in range(2):
            qs, k2, v2, bias, _ = _attn_setup(n, h, q_ref, km_ref, kp_ref, kc_ref)
            st = _dot_nt(k2, qs)
            pt = jnp.concatenate(
                [_attn_head(st[:, BLK * g:BLK * (g + 1)], bias, sink_ref[0, 8 * h + g])[0].astype(BF16) for g in range(8)],
                axis=1)
            o = _dot_tn(pt, v2)
            for jp in range(4):
                o_ref[:, BLK * (4 * h + jp):BLK * (4 * h + jp + 1)] = _unstack_heads(o, jp, lo).astype(BF16)

    return _call(
        body, name="attn_fwd", grid=(nb,),
        in_specs=[pl.BlockSpec((BLK, D), lambda i: (i, 0)),
                  pl.BlockSpec((BLK, 256), lambda i: (0, 0)),
                  pl.BlockSpec((BLK, 256), lambda i: (jnp.maximum(i - 1, 0), 0)),
                  pl.BlockSpec((BLK, 256), lambda i: (i, 0)),
                  pl.BlockSpec(memory_space=pltpu.SMEM)],
        out_specs=[pl.BlockSpec((BLK, D), lambda i: (i, 0))],
        out_shape=[jax.ShapeDtypeStruct((p, D), BF16)],
        args=(q, kv, kv, kv, sinks), comm=comm)


def _conv31_fwd(ag, w32, b, comm=None):
    p = ag.shape[0]
    nch = p // BLK

    def body(a_ref, g_ref, w_ref, b_ref, o_ref, gp):
        gp[0:32, :] = jnp.zeros((32, BLK), F32)
        for ci in range(nch):
            r0 = BLK * ci
            glu = a_ref[r0:r0 + BLK, :].astype(F32) * jax.nn.sigmoid(g_ref[r0:r0 + BLK, :].astype(F32))
            if ci == 0:
                glu = jnp.where(_rows(0, BLK) >= PAD, glu, 0.0)
            gp[32 + r0:32 + r0 + BLK, :] = glu
        for ci in range(nch):
            r0 = BLK * ci
            acc = jnp.broadcast_to(b_ref[...], (BLK, BLK))
            for j in range(CONV_K):
                acc = acc + w_ref[j:j + 1, :] * gp[r0 + j + 2:r0 + j + 2 + BLK, :]
            o_ref[r0:r0 + BLK, :] = acc

    return _call(
        body, name="conv31_fwd", grid=(D // BLK,),
        in_specs=[pl.BlockSpec((p, BLK), lambda j: (0, j)), pl.BlockSpec((p, BLK), lambda j: (0, 8 + j)),
                  pl.BlockSpec((32, BLK), lambda j: (0, j)), pl.BlockSpec((1, BLK), lambda j: (0, j))],
        out_specs=[pl.BlockSpec((p, BLK), lambda j: (0, j))],
        out_shape=[jax.ShapeDtypeStruct((p, D), F32)],
        scratch=[pltpu.VMEM((p + 32, BLK), F32)],
        args=(ag, ag, w32, b), comm=comm)


def _mixer_fwd(ao, c0, gates, h0p, wa, wc, wo, vecs):
    p = ao.shape[0]
    tm = _row_tile(p)

    def body(ao_ref, c0_ref, gt_ref, h_ref, wa_ref, wc_ref, wo_ref, v_ref,
             c1_ref, at_ref, cv_ref, mg_ref, mix_ref, h1_ref, n2_ref):
        i = pl.program_id(0)
        c1 = _lnsilu(c0_ref[...], v_ref[0:1, :], v_ref[1:2, :]).astype(BF16)
        c1_ref[...] = c1
        attn = _dot(ao_ref[...], wa_ref[...])
        conv = _dot(c1, wc_ref[...]) + v_ref[2:3, :]
        at_ref[...] = attn.astype(BF16)
        cv_ref[...] = conv.astype(BF16)
        merged = (jax.nn.sigmoid(gt_ref[:, 0:D].astype(F32)) * attn
                  + jax.nn.sigmoid(gt_ref[:, D:2 * D].astype(F32)) * conv).astype(BF16)
        mg_ref[...] = merged
        mix = _dot(merged, wo_ref[...])
        mix_ref[...] = mix
        h1 = jnp.where(_rows(i, tm) >= PAD, h_ref[...] + _rms(mix, v_ref[3:4, :]), 0.0)
        h1_ref[...] = h1
        n2_ref[...] = _rms(h1, v_ref[4:5, :]).astype(BF16)

    def row(w):
        return pl.BlockSpec((tm, w), lambda i: (i, 0))

    return pl.pallas_call(
        body, name="mixer_fwd", grid=(p // tm,),
        in_specs=[row(D), row(D), row(2 * D), row(D), VM, VM, VM, VM],
        out_specs=[row(D)] * 7,
        out_shape=[jax.ShapeDtypeStruct((p, D), t) for t in (BF16, BF16, BF16, BF16, F32, F32, BF16)],
        compiler_params=_cparams("parallel"),
    )(ao, c0, gates, h0p, wa, wc, wo, vecs)


def _mm_nt(a, w_t, name):
    p, k = a.shape
    n = w_t.shape[0]
    tm = _row_tile(p)
    ch = 512

    def body(a_ref, w_ref, o_ref):
        a_v = a_ref[...]
        for c0 in range(0, n, ch):
            o_ref[:, c0:c0 + ch] = _dot_nt(a_v, w_ref[c0:c0 + ch, :]).astype(BF16)

    return pl.pallas_call(
        body, name=name, grid=(p // tm,),
        in_specs=[pl.BlockSpec((tm, k), lambda i: (i, 0)), VM],
        out_specs=pl.BlockSpec((tm, n), lambda i: (i, 0)),
        out_shape=jax.ShapeDtypeStruct((p, n), BF16),
        compiler_params=_cparams("parallel"),
    )(a, w_t)


def _conv3(xp_ref, w_ref, r0):
    return (w_ref[0:1, :] * xp_ref[r0 + 6:r0 + 6 + BLK, :] + w_ref[1:2, :] * xp_ref[r0 + 7:r0 + 7 + BLK, :]
            + w_ref[2:3, :] * xp_ref[r0 + 8:r0 + 8 + BLK, :])


def _ffn_slab_specs(p):
    ncol = FFN // BLK
    return [pl.BlockSpec((p, BLK), lambda j: (0, j)), pl.BlockSpec((p, BLK), lambda j: (0, ncol + j)),
            pl.BlockSpec((FFN_K, BLK), lambda j: (0, j)), pl.BlockSpec((FFN_K, BLK), lambda j: (0, ncol + j)),
            pl.BlockSpec((1, BLK), lambda j: (0, j)), pl.BlockSpec((1, BLK), lambda j: (0, ncol + j))]


def _fill_shifted(dst, src_ref, nch):
    dst[0:8, :] = jnp.zeros((8, BLK), F32)
    for ci in range(nch):
        dst[8 + BLK * ci:8 + BLK * (ci + 1), :] = src_ref[BLK * ci:BLK * (ci + 1), :].astype(F32)


def _ffn_act(u0, fw, fb):
    p = u0.shape[0]
    nch = p // BLK

    def body(g_ref, v_ref, wg_ref, wv_ref, bg_ref, bv_ref, o_ref, dv_ref, dg_ref, xg, xv):
        _fill_shifted(xg, g_ref, nch)
        _fill_shifted(xv, v_ref, nch)
        for ci in range(nch):
            r0 = BLK * ci
            ug = _conv3(xg, wg_ref, r0) + bg_ref[...]
            uv = _conv3(xv, wv_ref, r0) + bv_ref[...]
            sg = jax.nn.sigmoid(ug)
            silu = ug * sg
            o_ref[r0:r0 + BLK, :] = (silu * uv).astype(BF16)
            dv_ref[r0:r0 + BLK, :] = silu.astype(BF16)
            dg_ref[r0:r0 + BLK, :] = (uv * (sg * (1.0 + ug * (1.0 - sg)))).astype(BF16)

    slab = pl.BlockSpec((p, BLK), lambda j: (0, j))
    return pl.pallas_call(
        body, name="ffn_act", grid=(FFN // BLK,),
        in_specs=_ffn_slab_specs(p),
        out_specs=[slab] * 3,
        out_shape=[jax.ShapeDtypeStruct((p, FFN), BF16)] * 3,
        scratch_shapes=[pltpu.VMEM((p + 8, BLK), F32)] * 2,
        compiler_params=_cparams("parallel"),
    )(u0, u0, fw, fw, fb, fb)


def _ffn_down_loss(act, wd, h1, tgt, gain):
    p = act.shape[0]
    tm = _row_tile(p)
    k = tm // BLK

    def body(*refs):
        a_ref, w_ref, h_ref = refs[:3]
        t_refs = refs[3:3 + k]
        g_ref, df_ref, da_ref, dy_ref, acc_ref = refs[3 + k:]
        i = pl.program_id(0)

        @pl.when(i == 0)
        def _():
            acc_ref[...] = jnp.zeros_like(acc_ref)

        ffn = _dot(a_ref[...], w_ref[...])
        r, vjp = jax.vjp(_rms, ffn, g_ref[...])
        t = jnp.concatenate([t_ref[...] for t_ref in t_refs], axis=0) if k > 1 else t_refs[0][...]
        diff = jnp.where(_rows(i, tm) >= BLK, h_ref[...] + r - t, 0.0)
        dy = diff * (1.0 / D)
        dffn, dg = vjp(dy)
        acc_ref[0:1, :] += dg
        acc_ref[1:2, :] += jnp.sum(diff * diff, axis=0, keepdims=True) * (0.5 / D)
        dy_ref[...] = dy
        dfb = dffn.astype(BF16)
        df_ref[...] = dfb
        for c0 in range(0, FFN, 256):
            da_ref[:, c0:c0 + 256] = _dot_nt(dfb, w_ref[c0:c0 + 256, :]).astype(BF16)

    def row(w):
        return pl.BlockSpec((tm, w), lambda i: (i, 0))

    return pl.pallas_call(
        body, name="ffn_down_loss", grid=(p // tm,),
        in_specs=[row(FFN), VM, row(D)] + _token_specs(tm) + [VM],
        out_specs=[row(D), row(FFN), row(D), pl.BlockSpec((8, D), lambda i: (0, 0))],
        out_shape=[jax.ShapeDtypeStruct((p, D), BF16), jax.ShapeDtypeStruct((p, FFN), BF16),
                   jax.ShapeDtypeStruct((p, D), F32), jax.ShapeDtypeStruct((8, D), F32)],
        compiler_params=_cparams("arbitrary"),
    )(act, wd, h1, *([tgt] * k), gain)


def _mm_tn(pieces, b, name, col_sums=False, comm=None):
    p, n = b.shape
    tk = 256
    nblk = [a.shape[1] // tk for a in pieces]
    offs = [sum(nblk[:q]) for q in range(len(pieces))]
    total = sum(nblk)
    npc = len(pieces)

    def body(*refs):
        a_refs, b_ref, o_ref = refs[:npc], refs[npc], refs[npc + 1]
        i = pl.program_id(0)
        for q, a_ref in enumerate(a_refs):
            @pl.when(jnp.logical_and(i >= offs[q], i < offs[q] + nblk[q]))
            def _(a_ref=a_ref):
                a_v = a_ref[...]
                o_ref[...] = _dot_tn(a_v, b_ref[...]).astype(BF16)
                if col_sums:
                    refs[npc + 2][...] = jnp.sum(a_v.astype(F32), axis=0, keepdims=True)

    def a_spec(q):
        return pl.BlockSpec((p, tk), lambda i: (0, jnp.clip(i - offs[q], 0, nblk[q] - 1)))

    out_specs = [pl.BlockSpec((tk, n), lambda i: (i, 0))]
    out_shape = [jax.ShapeDtypeStruct((total * tk, n), BF16)]
    if col_sums:
        out_specs.append(pl.BlockSpec((1, tk), lambda i: (0, i)))
        out_shape.append(jax.ShapeDtypeStruct((1, total * tk), F32))
    res, sent = _call(
        body, name=name, grid=(total,),
        in_specs=[a_spec(q) for q in range(npc)] + [VM],
        out_specs=out_specs, out_shape=out_shape, args=(*pieces, b), comm=comm)
    res = res if col_sums else res[0]
    return res if comm is None else (res, sent)


def _ffn_act_bwd(u0, dact, dact_dg, dact_dv, fw, act, dffn, comm=None):
    p = u0.shape[0]
    nch = p // BLK
    ncol = FFN // BLK

    def body(g_ref, v_ref, wg_ref, wv_ref, da_ref, lg_ref, lv_ref, act_ref, df_ref,
             dg_ref, dv_ref, gwg_ref, gwv_ref, gbg_ref, gbv_ref, gwd_ref, xg, xv, eg, ev):
        gwd_ref[...] = _dot_tn(act_ref[...], df_ref[...]).astype(BF16)
        _fill_shifted(xg, g_ref, nch)
        _fill_shifted(xv, v_ref, nch)
        eg[p:p + 8, :] = jnp.zeros((8, BLK), F32)
        ev[p:p + 8, :] = jnp.zeros((8, BLK), F32)
        for ci in range(nch):
            r0 = BLK * ci
            d = da_ref[r0:r0 + BLK, :].astype(F32)
            eg[r0:r0 + BLK, :] = d * lg_ref[r0:r0 + BLK, :].astype(F32)
            ev[r0:r0 + BLK, :] = d * lv_ref[r0:r0 + BLK, :].astype(F32)
        for e_s, x_s, w_ref, d_ref, gw_ref, gb_ref in ((eg, xg, wg_ref, dg_ref, gwg_ref, gbg_ref),
                                                      (ev, xv, wv_ref, dv_ref, gwv_ref, gbv_ref)):
            sums = [jnp.zeros((BLK, BLK), F32) for _ in range(FFN_K + 1)]
            for ci in range(nch):
                r0 = BLK * ci
                e0 = e_s[r0:r0 + BLK, :]
                du = (w_ref[2:3, :] * e0 + w_ref[1:2, :] * e_s[r0 + 1:r0 + 1 + BLK, :]
                      + w_ref[0:1, :] * e_s[r0 + 2:r0 + 2 + BLK, :])
                if ci == 0:
                    du = jnp.where(_rows(0, BLK) >= PAD, du, 0.0)
                d_ref[r0:r0 + BLK, :] = du.astype(BF16)
                for j in range(FFN_K):
                    sums[j] = sums[j] + e0 * x_s[r0 + 6 + j:r0 + 6 + j + BLK, :]
                sums[FFN_K] = sums[FFN_K] + e0
            for j in range(FFN_K):
                gw_ref[j:j + 1, :] = jnp.sum(sums[j], axis=0, keepdims=True)
            gb_ref[...] = jnp.sum(sums[FFN_K], axis=0, keepdims=True)

    slab = pl.BlockSpec((p, BLK), lambda j: (0, j))
    wspec = pl.BlockSpec((FFN_K, BLK), lambda j: (0, j))
    bspec = pl.BlockSpec((1, BLK), lambda j: (0, j))
    return _call(
        body, name="ffn_act_bwd", grid=(ncol,),
        in_specs=_ffn_slab_specs(p)[:4] + [slab] * 4 + [VM],
        out_specs=[slab, slab, wspec, wspec, bspec, bspec, pl.BlockSpec((BLK, D), lambda j: (j, 0))],
        out_shape=[jax.ShapeDtypeStruct((p, FFN), BF16)] * 2 + [jax.ShapeDtypeStruct((FFN_K, FFN), F32)] * 2
        + [jax.ShapeDtypeStruct((1, FFN), F32)] * 2 + [jax.ShapeDtypeStruct((FFN, D), BF16)],
        scratch=[pltpu.VMEM((p + 8, BLK), F32)] * 4,
        args=(u0, u0, fw, fw, dact, dact_dg, dact_dv, act, dffn), comm=comm)


def _ffn_in_bwd(dug, duv, w_upt, h1, dy, gain, comm=None):
    p = h1.shape[0]
    tm = _row_tile(p)

    def body(dg_ref, dv_ref, w_ref, h_ref, dy_ref, g_ref, o_ref, acc_ref):
        i = pl.program_id(0)

        @pl.when(i == 0)
        def _():
            acc_ref[...] = jnp.zeros_like(acc_ref)

        dn = _dot(dg_ref[...], w_ref[0:FFN, :]) + _dot(dv_ref[...], w_ref[FFN:2 * FFN, :])
        _, vjp = jax.vjp(_rms, h_ref[...], g_ref[...])
        dh, dg = vjp(dn)
        o_ref[...] = dy_ref[...] + dh
        acc_ref[0:1, :] += dg

    def row(w):
        return pl.BlockSpec((tm, w), lambda i: (i, 0))

    return _call(
        body, name="ffn_in_bwd", grid=(p // tm,),
        in_specs=[row(FFN), row(FFN), VM, row(D), row(D), VM],
        out_specs=[row(D), pl.BlockSpec((8, D), lambda i: (0, 0))],
        out_shape=[jax.ShapeDtypeStruct((p, D), F32), jax.ShapeDtypeStruct((8, D), F32)],
        sem="arbitrary", args=(dug, duv, w_upt, h1, dy, gain), comm=comm)


def _mixer_bwd(dh1, mix, attn, conv, gates, c0, wa, wc, wo, vecs, comm=None):
    p = dh1.shape[0]
    tm = _row_tile(p)

    def body(dh_ref, mix_ref, at_ref, cv_ref, gt_ref, c0_ref, wa_ref, wc_ref, wo_ref, v_ref,
             dmix_ref, dat_ref, dcv_ref, dgt_ref, dao_ref, dc0_ref, acc_ref):
        i = pl.program_id(0)

        @pl.when(i == 0)
        def _():
            acc_ref[...] = jnp.zeros_like(acc_ref)

        _, vjp = jax.vjp(_rms, mix_ref[...], v_ref[3:4, :])
        dmix, dgp = vjp(dh_ref[...])
        dmix = dmix.astype(BF16)
        dmix_ref[...] = dmix
        dmg = _dot_nt(dmix, wo_ref[...])
        sa = jax.nn.sigmoid(gt_ref[:, 0:D].astype(F32))
        sc = jax.nn.sigmoid(gt_ref[:, D:2 * D].astype(F32))
        dat = dmg * sa
        dcv = dmg * sc
        dgt_ref[:, 0:D] = (dmg * at_ref[...].astype(F32) * sa * (1.0 - sa)).astype(BF16)
        dgt_ref[:, D:2 * D] = (dmg * cv_ref[...].astype(F32) * sc * (1.0 - sc)).astype(BF16)
        datb = dat.astype(BF16)
        dcvb = dcv.astype(BF16)
        dat_ref[...] = datb
        dcv_ref[...] = dcvb
        dao_ref[...] = _dot_nt(datb, wa_ref[...]).astype(BF16)
        dc1 = _dot_nt(dcvb, wc_ref[...])
        _, vjp2 = jax.vjp(_lnsilu, c0_ref[...], v_ref[0:1, :], v_ref[1:2, :])
        dc0, dlg, dlb = vjp2(dc1)
        dc0_ref[...] = dc0
        acc_ref[0:1, :] += dgp
        acc_ref[1:2, :] += jnp.sum(dcv, axis=0, keepdims=True)
        acc_ref[2:3, :] += dlg
        acc_ref[3:4, :] += dlb

    def row(w):
        return pl.BlockSpec((tm, w), lambda i: (i, 0))

    return _call(
        body, name="mixer_bwd", grid=(p // tm,),
        in_specs=[row(D), row(D), row(D), row(D), row(2 * D), row(D), VM, VM, VM, VM],
        out_specs=[row(D), row(D), row(D), row(2 * D), row(D), row(D), pl.BlockSpec((8, D), lambda i: (0, 0))],
        out_shape=[jax.ShapeDtypeStruct((p, D), BF16)] * 3 + [jax.ShapeDtypeStruct((p, 2 * D), BF16),
                                                             jax.ShapeDtypeStruct((p, D), BF16),
                                                             jax.ShapeDtypeStruct((p, D), F32),
                                                             jax.ShapeDtypeStruct((8, D), F32)],
        sem="arbitrary", args=(dh1, mix, attn, conv, gates, c0, wa, wc, wo, vecs), comm=comm)


def _conv31_bwd(ag, dc0, w32, tn_pairs, comm=None):
    p = ag.shape[0]
    nch = p // BLK
    npair = len(tn_pairs)

    def body(*refs):
        a_ref, g_ref, dc_ref, w_ref = refs[:4]
        tn_a, tn_b = refs[4:4 + npair], refs[4 + npair:4 + 2 * npair]
        da_ref, dg_ref, gw_ref, gb_ref = refs[4 + 2 * npair:8 + 2 * npair]
        tn_o = refs[8 + 2 * npair:8 + 3 * npair]
        gp, dp = refs[8 + 3 * npair:]
        for ta, tb, to in zip(tn_a, tn_b, tn_o):
            to[...] = _dot_tn(ta[...], tb[...]).astype(BF16)
        gp[0:32, :] = jnp.zeros((32, BLK), F32)
        dp[p:p + 32, :] = jnp.zeros((32, BLK), F32)
        bsum = jnp.zeros((BLK, BLK), F32)
        for ci in range(nch):
            r0 = BLK * ci
            glu = a_ref[r0:r0 + BLK, :].astype(F32) * jax.nn.sigmoid(g_ref[r0:r0 + BLK, :].astype(F32))
            if ci == 0:
                glu = jnp.where(_rows(0, BLK) >= PAD, glu, 0.0)
            gp[32 + r0:32 + r0 + BLK, :] = glu
            d = dc_ref[r0:r0 + BLK, :]
            dp[r0:r0 + BLK, :] = d
            bsum = bsum + d
        gb_ref[...] = jnp.sum(bsum, axis=0, keepdims=True)
        for ci in range(nch):
            r0 = BLK * ci
            acc = jnp.zeros((BLK, BLK), F32)
            for j in range(CONV_K):
                acc = acc + w_ref[j:j + 1, :] * dp[r0 + 30 - j:r0 + 30 - j + BLK, :]
            if ci == 0:
                acc = jnp.where(_rows(0, BLK) >= PAD, acc, 0.0)
            a = a_ref[r0:r0 + BLK, :].astype(F32)
            sg = jax.nn.sigmoid(g_ref[r0:r0 + BLK, :].astype(F32))
            da_ref[r0:r0 + BLK, :] = (acc * sg).astype(BF16)
            dg_ref[r0:r0 + BLK, :] = (acc * a * sg * (1.0 - sg)).astype(BF16)
        for j in range(CONV_K):
            acc = jnp.zeros((BLK, BLK), F32)
            for ci in range(nch):
                r0 = BLK * ci
                acc = acc + dp[r0:r0 + BLK, :] * gp[r0 + j + 2:r0 + j + 2 + BLK, :]
            gw_ref[j:j + 1, :] = jnp.sum(acc, axis=0, keepdims=True)
        gw_ref[CONV_K:32, :] = jnp.zeros((32 - CONV_K, BLK), F32)

    slab = pl.BlockSpec((p, BLK), lambda j: (0, j))
    return _call(
        body, name="conv31_bwd", grid=(D // BLK,),
        in_specs=[slab, pl.BlockSpec((p, BLK), lambda j: (0, 8 + j)), slab, pl.BlockSpec((32, BLK), lambda j: (0, j))]
        + [slab] * npair + [VM] * npair,
        out_specs=[slab, slab, pl.BlockSpec((32, BLK), lambda j: (0, j)), pl.BlockSpec((1, BLK), lambda j: (0, j))]
        + [pl.BlockSpec((BLK, D), lambda j: (j, 0))] * npair,
        out_shape=[jax.ShapeDtypeStruct((p, D), BF16)] * 2 + [jax.ShapeDtypeStruct((32, D), F32),
                                                             jax.ShapeDtypeStruct((1, D), F32)]
        + [jax.ShapeDtypeStruct((D, D), BF16)] * npair,
        scratch=[pltpu.VMEM((p + 32, BLK), F32)] * 2,
        args=(ag, ag, dc0, w32, *[a for a, _ in tn_pairs], *[b for _, b in tn_pairs]), comm=comm)


def _attn_bwd(q, kv, dao, sinks, tabs, comm=None):
    p = q.shape[0]
    nb = p // BLK

    def body(q_ref, km_ref, kp_ref, kc_ref, do_ref, sink_ref, t_ref, dqkv_ref, dsink_ref, carry, macc):
        i = pl.program_id(0)
        n = nb - 1 - i

        @pl.when(i == 0)
        def _():
            carry[...] = jnp.zeros_like(carry)
            macc[...] = jnp.zeros_like(macc)
            dsink_ref[...] = jnp.zeros_like(dsink_ref)

        lo = lax.broadcasted_iota(jnp.int32, (BLK, BLK), 1) < HEAD_DIM
        lane8 = lax.broadcasted_iota(jnp.int32, (8, BLK), 1)
        c, s1, s2 = t_ref[:, 0:128], -t_ref[:, 128:256], -t_ref[:, 256:384]
        dk = jnp.zeros((N_KEY, BLK), F32)
        dv = jnp.zeros((N_KEY, BLK), F32)
        for h in range(2):
            qs, k2, v2, bias, lok = _attn_setup(n, h, q_ref, km_ref, kp_ref, kc_ref)
            dos = _stack_heads(do_ref, h, lo)
            st = _dot_nt(k2, qs)
            dpt = _dot_nt(v2, dos)
            p_parts, ds_parts = [], []
            for g in range(8):
                cols = slice(BLK * g, BLK * (g + 1))
                pn, ps = _attn_head(st[:, cols], bias, sink_ref[0, 8 * h + g])
                dp = dpt[:, cols]
                delta = jnp.sum(pn * dp, axis=0, keepdims=True)
                ds_parts.append((pn * (dp - delta)).astype(BF16))
                p_parts.append(pn.astype(BF16))
                dsk = -jnp.sum(ps * delta, axis=1, keepdims=True)
                dsink_ref[...] += jnp.where(lane8 == 8 * h + g, dsk, 0.0)
            dst = jnp.concatenate(ds_parts, axis=1)
            pt = jnp.concatenate(p_parts, axis=1)
            dq = _dot_tn(dst, k2)
            for jp in range(4):
                lo_c = BLK * (4 * h + jp)
                dqkv_ref[:, lo_c:lo_c + BLK] = (_rope(_unstack_heads(dq, jp, lo), c, s1, s2) * SCALE).astype(BF16)
            dk2 = _dot(dst, qs)
            dv2 = _dot(pt, dos)
            dk2 = dk2 + pltpu.roll(dk2, HEAD_DIM, 1)
            dv2 = dv2 + pltpu.roll(dv2, HEAD_DIM, 1)
            own = lok if h == 0 else jnp.logical_not(lok)
            dk = jnp.where(own, dk2, dk)
            dv = jnp.where(own, dv2, dv)
        macc[:, 0:BLK] += dk[2 * BLK:N_KEY]
        macc[:, BLK:2 * BLK] += dv[2 * BLK:N_KEY]
        last = (n == 0).astype(F32)
        zpad = jnp.zeros((PAD, BLK), F32)
        dk_c = dk[BLK:2 * BLK] + carry[:, 0:BLK] + last * jnp.concatenate([zpad, macc[:, 0:BLK]], axis=0)
        dv_c = dv[BLK:2 * BLK] + carry[:, BLK:2 * BLK] + last * jnp.concatenate([zpad, macc[:, BLK:2 * BLK]], axis=0)
        carry[:, 0:BLK] = dk[0:BLK]
        carry[:, BLK:2 * BLK] = dv[0:BLK]
        dqkv_ref[:, D:D + BLK] = _rope(dk_c, c, s1, s2).astype(BF16)
        dqkv_ref[:, D + BLK:D + 2 * BLK] = dv_c.astype(BF16)

    def rev(w):
        return pl.BlockSpec((BLK, w), lambda i: (nb - 1 - i, 0))

    return _call(
        body, name="attn_bwd", grid=(nb,),
        in_specs=[rev(D),
                  pl.BlockSpec((BLK, 256), lambda i: (0, 0)),
                  pl.BlockSpec((BLK, 256), lambda i: (jnp.maximum(nb - 2 - i, 0), 0)),
                  rev(256), rev(D),
                  pl.BlockSpec(memory_space=pltpu.SMEM), rev(384)],
        out_specs=[rev(QKV_W), pl.BlockSpec((8, BLK), lambda i: (0, 0))],
        out_shape=[jax.ShapeDtypeStruct((p, QKV_W), BF16), jax.ShapeDtypeStruct((8, BLK), F32)],
        scratch=[pltpu.VMEM((BLK, 256), F32), pltpu.VMEM((N_META, 256), F32)], sem="arbitrary",
        args=(q, kv, kv, kv, dao, sinks, tabs), comm=comm)


def _in_bwd(dqkv, da, dg, dgt, w_int, h0p, dh1, gain, comm=None):
    p = h0p.shape[0]
    tm = _row_tile(p)
    nt = p // tm
    first_rows = tm - BLK

    def body(dq_ref, da_ref, dg_ref, dt_ref, w_ref, h_ref, dh_ref, g_ref, gx_ref, dm_ref, acc_ref, buf, sems):
        i = pl.program_id(0)
        slot = i % 2

        @pl.when(i == 0)
        def _():
            acc_ref[...] = jnp.zeros_like(acc_ref)

        dn = (_dot(dq_ref[...], w_ref[0:QKV_W, :]) + _dot(da_ref[...], w_ref[QKV_W:QKV_W + D, :])
              + _dot(dg_ref[...], w_ref[QKV_W + D:QKV_W + 2 * D, :]) + _dot(dt_ref[...], w_ref[QKV_W + 2 * D:IN_W, :]))
        _, vjp = jax.vjp(_rms, h_ref[...], g_ref[...])
        dh, dgain = vjp(dn)
        dh0 = dh_ref[...] + dh
        acc_ref[0:1, :] += dgain
        buf[slot] = dh0

        @pl.when(i == 0)
        def _():
            dm_ref[...] = dh0[PAD:BLK]

        def first_copy():
            return pltpu.make_async_copy(buf.at[0, pl.ds(BLK, first_rows), :], gx_ref.at[pl.ds(0, first_rows), :], sems.at[0])

        def tile_copy(j, s):
            return pltpu.make_async_copy(buf.at[s], gx_ref.at[pl.ds(pl.multiple_of(j * tm - BLK, BLK), tm), :], sems.at[s])

        if first_rows:
            @pl.when(i == 1)
            def _():
                first_copy().wait()

        @pl.when(i >= 2)
        def _():
            tile_copy(i - 1, 1 - slot).wait()

        if first_rows:
            @pl.when(i == 0)
            def _():
                first_copy().start()

        @pl.when(i > 0)
        def _():
            tile_copy(i, slot).start()

        @pl.when(i == nt - 1)
        def _():
            tile_copy(i, slot).wait()

    def row(w):
        return pl.BlockSpec((tm, w), lambda i: (i, 0))

    return _call(
        body, name="in_bwd", grid=(nt,),
        in_specs=[row(QKV_W), row(D), row(D), row(2 * D), VM, row(D), row(D), VM],
        out_specs=[ANY, pl.BlockSpec((N_META, D), lambda i: (0, 0)), pl.BlockSpec((8, D), lambda i: (0, 0))],
        out_shape=[jax.ShapeDtypeStruct((p - BLK, D), F32), jax.ShapeDtypeStruct((N_META, D), F32),
                   jax.ShapeDtypeStruct((8, D), F32)],
        scratch=[pltpu.VMEM((2, tm, D), F32), pltpu.SemaphoreType.DMA((2,))],
        sem="arbitrary", args=(dqkv, da, dg, dgt, w_int, h0p, dh1, gain), comm=comm)


def _sum_slots(slots, name):
    r = slots.shape[0] // N_DEV
    cols = slots.shape[1]
    tr = r if r <= 352 else (r // 2 if (r // 2) % 16 == 0 else r // 3)
    steps = r // tr

    def body(*refs):
        acc = refs[0][...].astype(F32)
        for s in range(1, N_DEV):
            acc = acc + refs[s][...].astype(F32)
        refs[N_DEV][...] = acc

    return pl.pallas_call(
        body, name=name, grid=(steps,),
        in_specs=[pl.BlockSpec((tr, cols), functools.partial(lambda i, s: (s * steps + i, 0), s=s)) for s in range(N_DEV)],
        out_specs=pl.BlockSpec((tr, cols), lambda i: (i, 0)),
        out_shape=jax.ShapeDtypeStruct((r, cols), F32),
        compiler_params=_cparams("parallel"),
    )(*([slots] * N_DEV))


def _adamw_math(w, g, m, v):
    m_n = ADAM_B1 * m + (1.0 - ADAM_B1) * g
    v_n = ADAM_B2 * v + (1.0 - ADAM_B2) * jnp.square(g)
    m_hat = m_n / (1.0 - ADAM_B1 ** ADAM_STEP)
    v_hat = v_n / (1.0 - ADAM_B2 ** ADAM_STEP)
    return -ADAM_LR * (m_hat / (jnp.sqrt(v_hat) + ADAM_EPS) + ADAM_WD * w), m_n, v_n


def _sum_adamw(parts, w, m, v, name, nslots=N_DEV):
    r, cols = w.shape
    rs = r // len(parts)
    tr = rs if rs <= 352 else (rs // 2 if (rs // 2) % 16 == 0 else rs // 3)
    steps = rs // tr

    def body(*refs):
        w_ref, m_ref, v_ref, g_ref, d_ref, nm_ref, nv_ref = refs[nslots * len(parts):]
        i = pl.program_id(0)
        for q in range(len(parts)):
            @pl.when(i // steps == q)
            def _(q=q):
                g = refs[nslots * q][...].astype(F32)
                for s in range(1, nslots):
                    g = g + refs[nslots * q + s][...].astype(F32)
                g_ref[...] = g
                d_ref[...], nm_ref[...], nv_ref[...] = _adamw_math(w_ref[...], g, m_ref[...], v_ref[...])

    def slot_spec(q, s):
        return pl.BlockSpec((tr, cols), lambda i: (s * steps + jnp.clip(i - q * steps, 0, steps - 1), 0))

    spec = pl.BlockSpec((tr, cols), lambda i: (i, 0))
    return pl.pallas_call(
        body, name=name, grid=(steps * len(parts),),
        in_specs=[slot_spec(q, s) for q in range(len(parts)) for s in range(nslots)] + [spec] * 3,
        out_specs=[spec] * 4, out_shape=[jax.ShapeDtypeStruct((r, cols), F32)] * 4,
        compiler_params=_cparams("parallel"),
    )(*[a for a in parts for _ in range(nslots)], w, m, v)


def _adamw(w, g, m, v, name):
    r, cols = w.shape
    tr = 256 if r % 256 == 0 else r

    def body(w_ref, g_ref, m_ref, v_ref, d_ref, nm_ref, nv_ref):
        d_ref[...], nm_ref[...], nv_ref[...] = _adamw_math(w_ref[...], g_ref[...], m_ref[...], v_ref[...])

    spec = pl.BlockSpec((tr, cols), lambda i: (i, 0))
    return pl.pallas_call(
        body, name=name, grid=(r // tr,),
        in_specs=[spec] * 4, out_specs=[spec] * 3,
        out_shape=[jax.ShapeDtypeStruct((r, cols), F32)] * 3,
        compiler_params=_cparams("parallel"),
    )(w, g, m, v)


def _rope_tables(p):
    half = ROT_DIM // 2
    inv_freq = ROPE_THETA ** (-jnp.arange(half, dtype=F32) * 2.0 / ROT_DIM)
    pos = (jnp.arange(p) - PAD).astype(F32)
    ang = pos[:, None] * inv_freq[None, :]
    lane = jnp.arange(BLK)
    seg = (lane % HEAD_DIM) // half
    cos = jnp.cos(ang)[:, lane % half]
    sin = jnp.sin(ang)[:, lane % half]
    c = jnp.where(seg[None, :] < 2, cos, 1.0)
    s1 = jnp.where(seg[None, :] == 0, -sin, 0.0)
    s2 = jnp.where(seg[None, :] == 1, sin, 0.0)
    return jnp.concatenate([c, s1, s2], axis=1).astype(F32)


def _flat_pack(parts, rows):
    flat = jnp.concatenate([a.reshape(-1).astype(F32) for a in parts])
    return jnp.pad(flat, (0, rows * D - flat.shape[0])).reshape(rows, D)


def _flat_unpack(pack, shapes):
    flat = pack.reshape(-1)
    out, off = [], 0
    for s in shapes:
        size = 1
        for e in s:
            size *= e
        out.append(flat[off:off + size].reshape(s))
        off += size
    return out


def kernel(x, meta_tokens, norm_pre_mix, norm_post_mix, w_in, b_in, attn_sinks, w_attn_proj, conv_dw_w, conv_dw_b, conv_ln_g, conv_ln_b, w_conv_proj, b_conv_proj, w_out, norm_pre_ffn, norm_post_ffn, w_up, ffn_dw_w, ffn_dw_b, w_down, loss_target, m_meta_tokens, m_norm_pre_mix, m_norm_post_mix, m_w_in, m_b_in, m_attn_sinks, m_w_attn_proj, m_conv_dw_w, m_conv_dw_b, m_conv_ln_g, m_conv_ln_b, m_w_conv_proj, m_b_conv_proj, m_w_out, m_norm_pre_ffn, m_norm_post_ffn, m_w_up, m_ffn_dw_w, m_ffn_dw_b, m_w_down, v_meta_tokens, v_norm_pre_mix, v_norm_post_mix, v_w_in, v_b_in, v_attn_sinks, v_w_attn_proj, v_conv_dw_w, v_conv_dw_b, v_conv_ln_g, v_conv_ln_b, v_w_conv_proj, v_b_conv_proj, v_w_out, v_norm_pre_ffn, v_norm_post_ffn, v_w_up, v_ffn_dw_w, v_ffn_dw_b, v_w_down):
    seq = x.shape[1]
    p = seq + BLK
    me = 4 * lax.axis_index("x") + 2 * lax.axis_index("y") + lax.axis_index("c")
    in_cols = w_in.shape[2]
    up_cols = w_up.shape[2]

    small = jnp.zeros((56, up_cols), F32)
    small = small.at[0:N_META, 0:BLK].set(meta_tokens)
    small = small.at[16:16 + CONV_K, 0:BLK].set(conv_dw_w[0])
    small = small.at[48:48 + FFN_K, :].set(ffn_dw_w[0])
    w_int, small_all = _exchange(_Gather([w_in[0].T.astype(BF16), small]), "gather_w_in")
    small_all = small_all.reshape(N_DEV, 56, up_cols)
    meta_full = small_all[:, 0:N_META, 0:BLK].transpose(1, 0, 2).reshape(N_META, D)
    cdw = small_all[:, 16:16 + CONV_K, 0:BLK].transpose(1, 0, 2).reshape(CONV_K, D)
    cdw32 = jnp.pad(cdw, ((0, 32 - CONV_K), (0, 0)))
    fdw = small_all[:, 48:48 + FFN_K, :].transpose(1, 0, 2).reshape(FFN_K, 2 * FFN)

    tabs = _rope_tables(p)
    vecs = jnp.concatenate([conv_ln_g, conv_ln_b, b_conv_proj, norm_post_mix, norm_pre_ffn, jnp.zeros((3, D), F32)], axis=0)

    (h0p, n1, q, kv, ag, gates), (wa, wc, wo) = _in_proj(
        x[0], meta_full, norm_pre_mix, w_int, b_in, tabs,
        comm=_Gather([w_attn_proj[0].astype(BF16), w_conv_proj[0].astype(BF16), w_out[0].astype(BF16)]))
    (ao,), (w_upt,) = _attn_fwd(q, kv, attn_sinks, comm=_Gather([w_up[0].T.astype(BF16)]))
    (c0,), (wd,) = _conv31_fwd(ag, cdw32, conv_dw_b, comm=_Gather([w_down[0].astype(BF16)]))
    c1, attn, conv, merged, mix, h1, n2 = _mixer_fwd(ao, c0, gates, h0p, wa, wc, wo, vecs)
    u0 = _mm_nt(n2, w_upt, "ffn_up")
    act, dact_dv, dact_dg = _ffn_act(u0, fdw, ffn_dw_b)
    dffn, dact, dy, acc_f = _ffn_down_loss(act, wd, h1, loss_target[0], norm_post_ffn)

    (dug, duv, gfw_g, gfw_v, gfb_g, gfb_v, g_wd), _ = _ffn_act_bwd(u0, dact, dact_dg, dact_dv, fdw, act, dffn)
    g_wupt, (s_wd,) = _mm_tn([dug, duv], n2, "grad_w_up", comm=_Scatter([g_wd]))
    (dh1, acc_u), (s_wup0,) = _ffn_in_bwd(dug, duv, w_upt, h1, dy, norm_pre_ffn, comm=_Scatter([g_wupt], 0, 2))
    (dmix, dat, dcv, dgt, dao, dc0, acc_m), (s_wup1,) = _mixer_bwd(
        dh1, mix, attn, conv, gates, c0, wa, wc, wo, vecs, comm=_Scatter([g_wupt], 1, 2))
    (da, dg, g_cdw, g_cdb, g_wo, g_wa, g_wc), _ = _conv31_bwd(ag, dc0, cdw32, [(merged, dmix), (ao, dat), (c1, dcv)])
    (dqkv, dsink), (s_wa, s_wc, s_wo) = _attn_bwd(q, kv, dao, attn_sinks, tabs, comm=_Scatter([g_wa, g_wc, g_wo]))
    loss_row = jnp.sum(acc_f[1:2, :], axis=1, keepdims=True)
    early = [loss_row, acc_m[0:1], dsink[0:1, 0:16], g_cdw[0:CONV_K], g_cdb,
             acc_m[2:3], acc_m[3:4], acc_m[1:2], acc_u[0:1], acc_f[0:1],
             jnp.concatenate([gfw_g, gfw_v], axis=1), jnp.concatenate([gfb_g, gfb_v], axis=1)]
    (g_wint, g_bin), (gathered_early,) = _mm_tn([dqkv, da, dg, dgt], n1, "grad_w_in", col_sums=True,
                                                comm=_Gather([_flat_pack(early, 64)]))
    (from_sibling,) = _exchange(_SiblingSwap(g_wint), "swap_w_in")
    (grad_x2d, dmeta, acc_i), (s_win,) = _in_bwd(dqkv, da, dg, dgt, w_int, h0p, dh1, norm_pre_mix,
                                                 comm=_ChipScatter(_pair_add(g_wint, from_sibling)))

    big = []
    for nm, parts, nslots, w, m, v, tr in (
            ("w_in", [s_win], N_CHIP, w_in, m_w_in, v_w_in, True), ("w_up", [s_wup0, s_wup1], N_DEV, w_up, m_w_up, v_w_up, True),
            ("w_attn_proj", [s_wa], N_DEV, w_attn_proj, m_w_attn_proj, v_w_attn_proj, False),
            ("w_conv_proj", [s_wc], N_DEV, w_conv_proj, m_w_conv_proj, v_w_conv_proj, False),
            ("w_out", [s_wo], N_DEV, w_out, m_w_out, v_w_out, False),
            ("w_down", [s_wd], N_DEV, w_down, m_w_down, v_w_down, False)):
        ins = [a[0].T if tr else a[0] for a in (w, m, v)]
        big.append(tuple((o.T if tr else o)[None] for o in _sum_adamw(parts, *ins, "update_" + nm, nslots)))

    late = [dmeta, acc_i[0:1], g_bin]
    (gathered_late,) = _exchange(_Gather([_flat_pack(late, 24)]), "gather_small_grads")
    g_meta, g_npm, g_bi = _flat_unpack(_sum_slots(gathered_late, "sum_late_grads"), [a.shape for a in late])
    tot = _flat_unpack(_sum_slots(gathered_early, "sum_small_grads"), [a.shape for a in early])
    (loss, g_nqm, g_sk, g_cw, g_cb, g_lg, g_lb, g_bc, g_npf, g_nqf, g_fw, g_fb) = tot
    loss = loss.reshape(())
    g_meta = lax.dynamic_slice_in_dim(g_meta, me * BLK, BLK, axis=1)
    g_cw = lax.dynamic_slice_in_dim(g_cw, me * BLK, BLK, axis=1)[None]
    g_fw = lax.dynamic_slice_in_dim(g_fw, me * up_cols, up_cols, axis=1)[None]

    sm_w = [meta_tokens, norm_pre_mix, norm_post_mix, b_in, attn_sinks, conv_dw_w, conv_dw_b, conv_ln_g, conv_ln_b,
            b_conv_proj, norm_pre_ffn, norm_post_ffn, ffn_dw_w, ffn_dw_b]
    sm_g = [g_meta, g_npm, g_nqm, g_bi, g_sk, g_cw, g_cb, g_lg, g_lb, g_bc, g_npf, g_nqf, g_fw, g_fb]
    sm_m = [m_meta_tokens, m_norm_pre_mix, m_norm_post_mix, m_b_in, m_attn_sinks, m_conv_dw_w, m_conv_dw_b, m_conv_ln_g,
            m_conv_ln_b, m_b_conv_proj, m_norm_pre_ffn, m_norm_post_ffn, m_ffn_dw_w, m_ffn_dw_b]
    sm_v = [v_meta_tokens, v_norm_pre_mix, v_norm_post_mix, v_b_in, v_attn_sinks, v_conv_dw_w, v_conv_dw_b, v_conv_ln_g,
            v_conv_ln_b, v_b_conv_proj, v_norm_pre_ffn, v_norm_post_ffn, v_ffn_dw_w, v_ffn_dw_b]
    sm_shapes = [a.shape for a in sm_w]
    upd_rows = 32
    v_pack = _flat_pack(sm_v, upd_rows)
    sm_out = _adamw(_flat_pack(sm_w, upd_rows), _flat_pack(sm_g, upd_rows), _flat_pack(sm_m, upd_rows), v_pack, "adamw_small")
    sm_d, sm_nm, sm_nv = (_flat_unpack(o, sm_shapes) for o in sm_out)

    order = ["meta_tokens", "norm_pre_mix", "norm_post_mix", "w_in", "b_in", "attn_sinks", "w_attn_proj", "conv_dw_w",
             "conv_dw_b", "conv_ln_g", "conv_ln_b", "w_conv_proj", "b_conv_proj", "w_out", "norm_pre_ffn", "norm_post_ffn",
             "w_up", "ffn_dw_w", "ffn_dw_b", "w_down"]
    small_names = ["meta_tokens", "norm_pre_mix", "norm_post_mix", "b_in", "attn_sinks", "conv_dw_w", "conv_dw_b", "conv_ln_g",
                   "conv_ln_b", "b_conv_proj", "norm_pre_ffn", "norm_post_ffn", "ffn_dw_w", "ffn_dw_b"]
    big_names = ["w_in", "w_up", "w_attn_proj", "w_conv_proj", "w_out", "w_down"]
    table = {}
    for k, nm in enumerate(small_names):
        table[nm] = (sm_g[k], sm_d[k], sm_nm[k], sm_nv[k])
    for k, nm in enumerate(big_names):
        table[nm] = big[k]
    grad_x = grad_x2d[None]
    outs = [loss, grad_x]
    for field in range(4):
        outs += [table[nm][field] for nm in order]
    return tuple(outs)
```

```python
import functools

import jax
import jax.numpy as jnp
from jax import lax
from jax.experimental import pallas as pl
from jax.experimental.pallas import tpu as pltpu

F32 = jnp.float32
BF16 = jnp.bfloat16
MESH = pl.DeviceIdType.MESH

D = 1024
HEAD_DIM = 64
N_META = 16
BLK = 128
PAD = BLK - N_META
CONV_K = 31
FFN = 2816
FFN_K = 3
QKV_W = 1280
IN_W = 5376
ROT_DIM = 16
ROPE_THETA = 500000.0
RMS_EPS = 1e-6
LN_EPS = 1e-5
NEG_INF = -1e30
SCALE = HEAD_DIM ** -0.5
N_DEV = 8

ADAM_LR = 0.001
ADAM_B1 = 0.9
ADAM_B2 = 0.999
ADAM_EPS = 1e-08
ADAM_WD = 0.01
ADAM_STEP = 10

VMEM_BYTES_V7X = 64 * 1024 * 1024
VMEM_LIMIT = VMEM_BYTES_V7X - 8 * 1024 * 1024

NT = (((1,), (1,)), ((), ()))
TN = (((0,), (0,)), ((), ()))
VM = pl.BlockSpec(memory_space=pltpu.VMEM)
ANY = pl.BlockSpec(memory_space=pl.ANY)


def _cparams(*sem):
    return pltpu.CompilerParams(dimension_semantics=sem or None, vmem_limit_bytes=VMEM_LIMIT)


def _row_tile(p):
    return 384 if p % 384 == 0 else 128


def _dot(a, b):
    return jnp.dot(a, b, preferred_element_type=F32)


def _dot_nt(a, b):
    return lax.dot_general(a, b, NT, preferred_element_type=F32)


def _dot_tn(a, b):
    return lax.dot_general(a, b, TN, preferred_element_type=F32)


def _rms(x, g):
    return x * lax.rsqrt(jnp.mean(x * x, axis=-1, keepdims=True) + RMS_EPS) * g


def _lnsilu(x, g, b):
    mu = jnp.mean(x, axis=-1, keepdims=True)
    var = jnp.mean(jnp.square(x - mu), axis=-1, keepdims=True)
    z = (x - mu) * lax.rsqrt(var + LN_EPS) * g + b
    return z * jax.nn.sigmoid(z)


def _rms_bwd(x, g, dy):
    r = lax.rsqrt(jnp.mean(x * x, axis=-1, keepdims=True) + RMS_EPS)
    xn = x * r
    u = dy * g
    dg = jnp.sum(dy * xn, axis=0, keepdims=True)
    dx = r * (u - xn * jnp.mean(u * xn, axis=-1, keepdims=True))
    return dx, dg


def _lnsilu_bwd(x, g, b, dout):
    mu = jnp.mean(x, axis=-1, keepdims=True)
    xc = x - mu
    rs = lax.rsqrt(jnp.mean(xc * xc, axis=-1, keepdims=True) + LN_EPS)
    yh = xc * rs
    z = yh * g + b
    sg = jax.nn.sigmoid(z)
    dz = dout * (sg * (1.0 + z * (1.0 - sg)))
    dg = jnp.sum(dz * yh, axis=0, keepdims=True)
    db = jnp.sum(dz, axis=0, keepdims=True)
    dyh = dz * g
    dx = rs * (dyh - jnp.mean(dyh, axis=-1, keepdims=True) - yh * jnp.mean(dyh * yh, axis=-1, keepdims=True))
    return dx, dg, db


def _rope(v, c, s1, s2):
    return v * c + pltpu.roll(v, BLK - 8, 1) * s1 + pltpu.roll(v, 8, 1) * s2


def _rows(i, tm):
    return i * tm + lax.broadcasted_iota(jnp.int32, (tm, 1), 0)


def _place():
    return lax.axis_index("x"), lax.axis_index("y"), lax.axis_index("c")


def _blk(ref, idx, r, dtype):
    return ref.at[pl.ds(pl.multiple_of(idx * r, 16 if dtype == BF16 else 8), r), :]


class _Gather:
    def __init__(self, arrs):
        self.ins = list(arrs)
        n = len(arrs)
        self.out_shape = [jax.ShapeDtypeStruct((N_DEV * a.shape[0], a.shape[1]), a.dtype) for a in arrs]
        self.scratch = [pltpu.SemaphoreType.DMA((n, 7)), pltpu.SemaphoreType.DMA((n, 7)), pltpu.SemaphoreType.DMA((n,))]

    def _parts(self, ins, outs, sems):
        send_sems, recv_sems, local_sems = sems
        n = len(ins)
        x, y, c = _place()
        me, sibling = (x, y, c), (x, y, 1 - c)
        chips = [(1 - x, y), (x, 1 - y), (1 - x, 1 - y)]

        def rows(a, p):
            return _blk(outs[a], 4 * p[0] + 2 * p[1] + p[2], self.ins[a].shape[0], self.ins[a].dtype)

        def copy(a, k, block, to, src=None):
            return pltpu.make_async_remote_copy(
                src_ref=rows(a, block) if src is None else src, dst_ref=rows(a, block),
                send_sem=send_sems.at[a, k], recv_sem=recv_sems.at[a, k], device_id=to, device_id_type=MESH)

        mine = [pltpu.make_async_copy(ins[a], rows(a, me), local_sems.at[a]) for a in range(n)]
        first = []
        for a in range(n):
            first.append(copy(a, 0, me, sibling, src=ins[a]))
            first += [copy(a, 1 + j, me, (*chip, c), src=ins[a]) for j, chip in enumerate(chips)]
        return n, c, me, sibling, chips, copy, mine, first

    def start(self, ins, outs, sems):
        *_, mine, first = self._parts(ins, outs, sems)
        for cp in mine + first:
            cp.start()

    def finish(self, ins, outs, sems):
        n, c, me, sibling, chips, copy, mine, first = self._parts(ins, outs, sems)
        passed = []
        for j, chip in enumerate(chips):
            for a in range(n):
                copy(a, 1 + j, (*chip, c), me).wait_recv()
                fwd = copy(a, 4 + j, (*chip, c), sibling)
                fwd.start()
                passed.append(fwd)
        for a in range(n):
            copy(a, 0, sibling, me).wait_recv()
            for j, chip in enumerate(chips):
                copy(a, 4 + j, (*chip, 1 - c), me).wait_recv()
        for cp in first + passed:
            cp.wait_send()
        for cp in mine:
            cp.wait()


class _GatherRelay:
    def __init__(self, arr):
        self.ins = [arr]
        self.r = arr.shape[0]
        self.out_shape = [jax.ShapeDtypeStruct((N_DEV * self.r, arr.shape[1]), arr.dtype)]
        self.scratch = [pltpu.SemaphoreType.DMA((9,)), pltpu.SemaphoreType.DMA((9,)), pltpu.SemaphoreType.DMA]

    def _parts(self, ins, outs, sems):
        send_sems, recv_sems, local_sem = sems
        x, y, c = _place()
        r, half = self.r, self.r // 2
        out = outs[0]
        me, sib, xn, yn = (x, y, c), (x, y, 1 - c), (1 - x, y, c), (x, 1 - y, c)
        dg = (1 - x, 1 - y, c)

        def rows(p, lo=0, n=r):
            return out.at[pl.ds(pl.multiple_of((4 * p[0] + 2 * p[1] + p[2]) * r + lo, 16), n), :]

        def copy(k, dev_rows, to, src=None):
            return pltpu.make_async_remote_copy(
                src_ref=dev_rows if src is None else src, dst_ref=dev_rows,
                send_sem=send_sems.at[k], recv_sem=recv_sems.at[k], device_id=to, device_id_type=MESH)

        mine = pltpu.make_async_copy(ins[0], rows(me), local_sem)
        first = [copy(0, rows(me), sib, src=ins[0]), copy(1, rows(me), xn, src=ins[0]), copy(2, rows(me), yn, src=ins[0])]
        arrive = {0: rows(sib), 1: rows(xn), 2: rows(yn), 3: rows(dg, 0, half), 4: rows(dg, half, half),
                  5: rows((1 - x, y, 1 - c)), 6: rows((x, 1 - y, 1 - c)),
                  7: rows((1 - x, 1 - y, 1 - c), 0, half), 8: rows((1 - x, 1 - y, 1 - c), half, half)}
        relay = {1: [(3, rows(xn, 0, half), yn), (5, rows(xn), sib)],
                 2: [(4, rows(yn, half, half), xn), (6, rows(yn), sib)],
                 3: [(7, rows(dg, 0, half), sib)], 4: [(8, rows(dg, half, half), sib)]}
        return copy, mine, first, arrive, relay, me

    def start(self, ins, outs, sems):
        _, mine, first, _, _, _ = self._parts(ins, outs, sems)
        for cp in [mine] + first:
            cp.start()

    def finish(self, ins, outs, sems):
        copy, mine, first, arrive, relay, me = self._parts(ins, outs, sems)
        passed = []
        for k in (1, 2, 3, 4):
            copy(k, arrive[k], me).wait_recv()
            for k2, dev_rows, to in relay[k]:
                fwd = copy(k2, dev_rows, to)
                fwd.start()
                passed.append(fwd)
        for k in (0, 5, 6, 7, 8):
            copy(k, arrive[k], me).wait_recv()
        for cp in first + passed:
            cp.wait_send()
        mine.wait()


FLIPS = [(0, 0, 1), (1, 0, 0), (0, 1, 0), (1, 1, 0), (1, 0, 1), (0, 1, 1), (1, 1, 1)]


class _Scatter:
    def __init__(self, arrs, part=0, nparts=1):
        self.ins = list(arrs)
        self.part, self.nparts = part, nparts
        n = len(arrs)
        self.out_shape = [jax.ShapeDtypeStruct((a.shape[0] // nparts, a.shape[1]), a.dtype) for a in arrs]
        self.scratch = [pltpu.SemaphoreType.DMA((n, 7)), pltpu.SemaphoreType.DMA((n, 7)), pltpu.SemaphoreType.DMA((n,))]

    def _parts(self, ins, outs, sems):
        send_sems, recv_sems, local_sems = sems
        n = len(ins)
        x, y, c = _place()
        me = 4 * x + 2 * y + c

        def flip(v, f):
            return 1 - v if f else v

        def src(a, idx):
            r = self.ins[a].shape[0] // N_DEV
            rs = r // self.nparts
            return ins[a].at[pl.ds(pl.multiple_of(idx * r + self.part * rs, 16), rs), :]

        def dst(a, idx):
            rs = self.ins[a].shape[0] // N_DEV // self.nparts
            return outs[a].at[pl.ds(pl.multiple_of(idx * rs, 16), rs), :]

        mine = [pltpu.make_async_copy(src(a, me), dst(a, me), local_sems.at[a]) for a in range(n)]
        sends, recvs = [], []
        for k, f in enumerate(FLIPS):
            peer = (flip(x, f[0]), flip(y, f[1]), flip(c, f[2]))
            pidx = 4 * peer[0] + 2 * peer[1] + peer[2]
            for a in range(n):
                sends.append(pltpu.make_async_remote_copy(
                    src_ref=src(a, pidx), dst_ref=dst(a, me),
                    send_sem=send_sems.at[a, k], recv_sem=recv_sems.at[a, k], device_id=peer, device_id_type=MESH))
                recvs.append(functools.partial(
                    pltpu.make_async_remote_copy,
                    src_ref=src(a, pidx), dst_ref=dst(a, pidx),
                    send_sem=send_sems.at[a, k], recv_sem=recv_sems.at[a, k], device_id=peer, device_id_type=MESH))
        return mine, sends, recvs

    def start(self, ins, outs, sems):
        mine, sends, _ = self._parts(ins, outs, sems)
        for cp in mine + sends:
            cp.start()

    def finish(self, ins, outs, sems):
        mine, sends, recvs = self._parts(ins, outs, sems)
        for make in recvs:
            make().wait_recv()
        for cp in sends:
            cp.wait_send()
        for cp in mine:
            cp.wait()


N_CHIP = 4


class _SiblingSwap:
    def __init__(self, arr):
        self.ins = [arr]
        self.r = arr.shape[0] // N_DEV
        self.out_shape = [jax.ShapeDtypeStruct((N_CHIP * self.r, arr.shape[1]), arr.dtype)]
        self.scratch = [pltpu.SemaphoreType.DMA((N_CHIP,)), pltpu.SemaphoreType.DMA((N_CHIP,))]

    def _copies(self, ins, outs, sems):
        send_sems, recv_sems = sems
        x, y, c = _place()
        r = self.r
        return [pltpu.make_async_remote_copy(
            src_ref=ins[0].at[pl.ds(pl.multiple_of((2 * j + 1 - c) * r, 16), r), :],
            dst_ref=outs[0].at[pl.ds(j * r, r), :],
            send_sem=send_sems.at[j], recv_sem=recv_sems.at[j], device_id=(x, y, 1 - c), device_id_type=MESH)
            for j in range(N_CHIP)]

    def start(self, ins, outs, sems):
        for cp in self._copies(ins, outs, sems):
            cp.start()

    def finish(self, ins, outs, sems):
        for cp in self._copies(ins, outs, sems):
            cp.wait()


class _ChipScatter:
    def __init__(self, arr):
        self.ins = [arr]
        self.r = arr.shape[0] // N_CHIP
        self.out_shape = [jax.ShapeDtypeStruct(arr.shape, arr.dtype)]
        self.scratch = [pltpu.SemaphoreType.DMA((3,)), pltpu.SemaphoreType.DMA((3,)), pltpu.SemaphoreType.DMA]

    def _parts(self, ins, outs, sems):
        send_sems, recv_sems, local_sem = sems
        x, y, c = _place()
        r = self.r
        my_chip = 2 * x + y

        def rows(ref, j):
            return ref.at[pl.ds(pl.multiple_of(j * r, 16), r), :]

        mine = pltpu.make_async_copy(rows(ins[0], my_chip), rows(outs[0], my_chip), local_sem)
        sends, recvs = [], []
        for k, (fx, fy) in enumerate(((1, 0), (0, 1), (1, 1))):
            px, py = (1 - x if fx else x), (1 - y if fy else y)
            peer_chip = 2 * px + py
            sends.append(pltpu.make_async_remote_copy(
                src_ref=rows(ins[0], peer_chip), dst_ref=rows(outs[0], my_chip),
                send_sem=send_sems.at[k], recv_sem=recv_sems.at[k], device_id=(px, py, c), device_id_type=MESH))
            recvs.append(functools.partial(
                pltpu.make_async_remote_copy,
                src_ref=rows(ins[0], peer_chip), dst_ref=rows(outs[0], peer_chip),
                send_sem=send_sems.at[k], recv_sem=recv_sems.at[k], device_id=(px, py, c), device_id_type=MESH))
        return mine, sends, recvs

    def start(self, ins, outs, sems):
        mine, sends, _ = self._parts(ins, outs, sems)
        for cp in [mine] + sends:
            cp.start()

    def finish(self, ins, outs, sems):
        mine, sends, recvs = self._parts(ins, outs, sems)
        for make in recvs:
            make().wait_recv()
        for cp in sends:
            cp.wait_send()
        mine.wait()


def _pair_add(partial, recv):
    r = recv.shape[0] // N_CHIP
    cols = recv.shape[1]
    tr = r // 2 if (r // 2) % 16 == 0 else r
    steps = r // tr
    core = lax.axis_index("c").astype(jnp.int32).reshape(1)

    def body(c_ref, p_ref, s_ref, o_ref):
        o_ref[...] = (p_ref[...].astype(F32) + s_ref[...].astype(F32)).astype(BF16)

    spec = pl.BlockSpec((tr, cols), lambda j, i, c_ref: (j * steps + i, 0))
    return pl.pallas_call(
        body, name="pair_add",
        grid_spec=pltpu.PrefetchScalarGridSpec(
            num_scalar_prefetch=1, grid=(N_CHIP, steps),
            in_specs=[pl.BlockSpec((tr, cols), lambda j, i, c_ref: ((2 * j + c_ref[0]) * steps + i, 0)), spec],
            out_specs=spec),
        out_shape=jax.ShapeDtypeStruct(recv.shape, BF16),
        compiler_params=_cparams("parallel", "parallel"),
    )(core, partial, recv)


class _Both:
    def __init__(self, a, b):
        self.a, self.b = a, b
        self.ins = a.ins + b.ins
        self.out_shape = a.out_shape + b.out_shape
        self.scratch = a.scratch + b.scratch

    def _split(self, ins, outs, sems):
        ni, no, ns = len(self.a.ins), len(self.a.out_shape), len(self.a.scratch)
        return (ins[:ni], outs[:no], sems[:ns]), (ins[ni:], outs[no:], sems[ns:])

    def start(self, ins, outs, sems):
        ra, rb = self._split(ins, outs, sems)
        self.a.start(*ra)
        self.b.start(*rb)

    def finish(self, ins, outs, sems):
        ra, rb = self._split(ins, outs, sems)
        self.a.finish(*ra)
        self.b.finish(*rb)


def _exchange(comm, name):
    n, m = len(comm.ins), len(comm.out_shape)

    def body(*refs):
        ins, outs, sems = refs[:n], refs[n:n + m], refs[n + m:]
        comm.start(ins, outs, sems)
        comm.finish(ins, outs, sems)

    return pl.pallas_call(
        body, name=name, out_shape=comm.out_shape, in_specs=[ANY] * n, out_specs=[ANY] * m, scratch_shapes=comm.scratch,
    )(*comm.ins)


def _call(body, *, name, grid, in_specs, out_specs, out_shape, args, scratch=(), sem="parallel", comm=None):
    if comm is None:
        outs = pl.pallas_call(
            body, name=name, grid=grid, in_specs=list(in_specs), out_specs=list(out_specs), out_shape=list(out_shape),
            scratch_shapes=list(scratch), compiler_params=_cparams(sem))(*args)
        return outs, []
    n_in, n_out, n_sc = len(in_specs), len(out_specs), len(scratch)
    n_ci, n_co = len(comm.ins), len(comm.out_shape)
    last = grid[0] - 1

    def fused(*refs):
        ins, refs = refs[:n_in], refs[n_in:]
        c_ins, refs = refs[:n_ci], refs[n_ci:]
        outs, refs = refs[:n_out], refs[n_out:]
        c_outs, refs = refs[:n_co], refs[n_co:]
        sc, c_sems = refs[:n_sc], refs[n_sc:]
        step = pl.program_id(0)

        @pl.when(step == 0)
        def _():
            comm.start(c_ins, c_outs, c_sems)

        body(*ins, *outs, *sc)

        @pl.when(step == last)
        def _():
            comm.finish(c_ins, c_outs, c_sems)

    outs = pl.pallas_call(
        fused, name=name, grid=grid, in_specs=list(in_specs) + [ANY] * n_ci, out_specs=list(out_specs) + [ANY] * n_co,
        out_shape=list(out_shape) + comm.out_shape, scratch_shapes=list(scratch) + comm.scratch,
        compiler_params=_cparams("arbitrary"))(*args, *comm.ins)
    return outs[:n_out], outs[n_out:]


def _token_specs(tm):
    k = tm // BLK
    return [pl.BlockSpec((BLK, D), functools.partial(lambda i, t: (jnp.maximum(k * i + t - 1, 0), 0), t=t)) for t in range(k)]


def _in_proj(x2d, meta, gain, w_int, b_in, tabs, comm=None):
    p = x2d.shape[0] + BLK
    tm = _row_tile(p)
    k = tm // BLK

    def body(*refs):
        x_refs = refs[:k]
        m_ref, g_ref, w_ref, b_ref, t_ref, h_ref, n1_ref, q_ref, kv_ref, ag_ref, gt_ref = refs[k:]
        i = pl.program_id(0)
        head = jnp.concatenate([jnp.zeros((PAD, D), F32), m_ref[...]], axis=0)
        first = jnp.where(i == 0, head, x_refs[0][...])
        h = jnp.concatenate([first] + [r[...] for r in x_refs[1:]], axis=0) if k > 1 else first
        h_ref[...] = h
        n = _rms(h, g_ref[...]).astype(BF16)
        n1_ref[...] = n
        c, s1, s2 = t_ref[:, 0:128], t_ref[:, 128:256], t_ref[:, 256:384]

        def mm(c0, w):
            return _dot_nt(n, w_ref[c0:c0 + w, :]) + b_ref[:, c0:c0 + w]

        for j in range(4):
            acc = mm(256 * j, 256)
            for t in range(2):
                lo = 256 * j + 128 * t
                q_ref[:, lo:lo + 128] = (_rope(acc[:, 128 * t:128 * (t + 1)], c, s1, s2) * SCALE).astype(BF16)
        acc = mm(1024, 256)
        kv_ref[:, 0:128] = _rope(acc[:, 0:128], c, s1, s2).astype(BF16)
        kv_ref[:, 128:256] = acc[:, 128:256].astype(BF16)
        for j in range(8):
            ag_ref[:, 256 * j:256 * (j + 1)] = mm(QKV_W + 256 * j, 256).astype(BF16)
        for j in range(8):
            gt_ref[:, 256 * j:256 * (j + 1)] = mm(QKV_W + 2048 + 256 * j, 256).astype(BF16)

    def row(w):
        return pl.BlockSpec((tm, w), lambda i: (i, 0))

    return _call(
        body, name="in_proj", grid=(p // tm,),
        in_specs=_token_specs(tm) + [VM, VM, VM, VM, row(384)],
        out_specs=[row(D), row(D), row(D), row(256), row(2048), row(2048)],
        out_shape=[jax.ShapeDtypeStruct((p, D), F32)] + [jax.ShapeDtypeStruct((p, w), BF16) for w in (D, D, 256, 2048, 2048)],
        args=(x2d,) * k + (meta, gain, w_int, b_in, tabs), comm=comm)


N_KEY = 2 * BLK + N_META


def _attn_setup(n, h, q_ref, km_ref, kp_ref, kc_ref):
    lo = lax.broadcasted_iota(jnp.int32, (BLK, BLK), 1) < HEAD_DIM
    lok = lax.broadcasted_iota(jnp.int32, (N_KEY, BLK), 1) < HEAD_DIM

    def dup(lanes):
        cat = jnp.concatenate([kp_ref[:, lanes], kc_ref[:, lanes], km_ref[PAD:BLK, lanes]], axis=0).astype(F32)
        rolled = pltpu.roll(cat, HEAD_DIM, 1)
        return (jnp.where(lok, cat, rolled) if h == 0 else jnp.where(lok, rolled, cat)).astype(BF16)

    k2 = dup(slice(0, 128))
    v2 = dup(slice(128, 256))
    qs = _stack_heads(q_ref, h, lo)

    kr = lax.broadcasted_iota(jnp.int32, (BLK, BLK), 0)
    tq = BLK * n + lax.broadcasted_iota(jnp.int32, (BLK, BLK), 1) - PAD
    t_p = BLK * (n - 1) + kr - PAD
    t_c = BLK * n + kr - PAD
    ok_p = jnp.logical_and(t_p >= N_META, tq - t_p < BLK)
    ok_c = jnp.logical_and(t_c >= N_META, t_c <= tq)
    ok_m = lax.broadcasted_iota(jnp.int32, (N_META, BLK), 0) <= BLK * n + lax.broadcasted_iota(jnp.int32, (N_META, BLK), 1) - PAD
    bias = jnp.concatenate([jnp.where(ok, 0.0, NEG_INF).astype(F32) for ok in (ok_p, ok_c, ok_m)], axis=0)
    return qs, k2, v2, bias, lok


def _attn_head(s, bias, sink):
    s = s + bias
    m = jnp.maximum(jnp.max(s, axis=0, keepdims=True), sink)
    e = jnp.exp(s - m)
    es = jnp.exp(sink - m)
    inv = 1.0 / (jnp.sum(e, axis=0, keepdims=True) + es)
    return e * inv, es * inv


def _stack_heads(ref, h, lo):
    pieces = []
    for jp in range(4):
        v = ref[:, BLK * (4 * h + jp):BLK * (4 * h + jp + 1)]
        zero = jnp.zeros_like(v)
        pieces += [jnp.where(lo, v, zero), jnp.where(lo, zero, v)]
    return jnp.concatenate(pieces, axis=0)


def _unstack_heads(v, jp, lo):
    return jnp.where(lo, v[256 * jp:256 * jp + 128], v[256 * jp + 128:256 * jp + 256])


def _attn_fwd(q, kv, sinks, comm=None):
    p = q.shape[0]
    nb = p // BLK

    def body(q_ref, km_ref, kp_ref, kc_ref, sink_ref, o_ref):
        n = pl.program_id(0)
        lo = lax.broadcasted_iota(jnp.int32, (BLK, BLK), 1) < HEAD_DIM
        for h in range(2):
            qs, k2, v2, bias, _ = _attn_setup(n, h, q_ref, km_ref, kp_ref, kc_ref)
            st = _dot_nt(k2, qs)
            pt = jnp.concatenate(
                [_attn_head(st[:, BLK * g:BLK * (g + 1)], bias, sink_ref[0, 8 * h + g])[0].astype(BF16) for g in range(8)],
                axis=1)
            o = _dot_tn(pt, v2)
            for jp in range(4):
                o_ref[:, BLK * (4 * h + jp):BLK * (4 * h + jp + 1)] = _unstack_heads(o, jp, lo).astype(BF16)

    return _call(
        body, name="attn_fwd", grid=(nb,),
        in_specs=[pl.BlockSpec((BLK, D), lambda i: (i, 0)),
                  pl.BlockSpec((BLK, 256), lambda i: (0, 0)),
                  pl.BlockSpec((BLK, 256), lambda i: (jnp.maximum(i - 1, 0), 0)),
                  pl.BlockSpec((BLK, 256), lambda i: (i, 0)),
                  pl.BlockSpec(memory_space=pltpu.SMEM)],
        out_specs=[pl.BlockSpec((BLK, D), lambda i: (i, 0))],
        out_shape=[jax.ShapeDtypeStruct((p, D), BF16)],
        args=(q, kv, kv, kv, sinks), comm=comm)


def _conv31_fwd(ag, w32, b, comm=None):
    p = ag.shape[0]
    nch = p // BLK

    def body(a_ref, g_ref, w_ref, b_ref, o_ref, gp):
        gp[0:32, :] = jnp.zeros((32, BLK), F32)
        for ci in range(nch):
            r0 = BLK * ci
            glu = a_ref[r0:r0 + BLK, :].astype(F32) * jax.nn.sigmoid(g_ref[r0:r0 + BLK, :].astype(F32))
            if ci == 0:
                glu = jnp.where(_rows(0, BLK) >= PAD, glu, 0.0)
            gp[32 + r0:32 + r0 + BLK, :] = glu
        for ci in range(nch):
            r0 = BLK * ci
            acc = jnp.broadcast_to(b_ref[...], (BLK, BLK))
            for j in range(CONV_K):
                acc = acc + w_ref[j:j + 1, :] * gp[r0 + j + 2:r0 + j + 2 + BLK, :]
            o_ref[r0:r0 + BLK, :] = acc

    return _call(
        body, name="conv31_fwd", grid=(D // BLK,),
        in_specs=[pl.BlockSpec((p, BLK), lambda j: (0, j)), pl.BlockSpec((p, BLK), lambda j: (0, 8 + j)),
                  pl.BlockSpec((32, BLK), lambda j: (0, j)), pl.BlockSpec((1, BLK), lambda j: (0, j))],
        out_specs=[pl.BlockSpec((p, BLK), lambda j: (0, j))],
        out_shape=[jax.ShapeDtypeStruct((p, D), F32)],
        scratch=[pltpu.VMEM((p + 32, BLK), F32)],
        args=(ag, ag, w32, b), comm=comm)


def _mixer_fwd(ao, c0, gates, h0p, wa, wc, wo, vecs):
    p = ao.shape[0]
    tm = _row_tile(p)

    def body(ao_ref, c0_ref, gt_ref, h_ref, wa_ref, wc_ref, wo_ref, v_ref,
             c1_ref, at_ref, cv_ref, mg_ref, mix_ref, h1_ref, n2_ref):
        i = pl.program_id(0)
        c1 = _lnsilu(c0_ref[...], v_ref[0:1, :], v_ref[1:2, :]).astype(BF16)
        c1_ref[...] = c1
        attn = _dot(ao_ref[...], wa_ref[...])
        conv = _dot(c1, wc_ref[...]) + v_ref[2:3, :]
        at_ref[...] = attn.astype(BF16)
        cv_ref[...] = conv.astype(BF16)
        merged = (jax.nn.sigmoid(gt_ref[:, 0:D].astype(F32)) * attn
                  + jax.nn.sigmoid(gt_ref[:, D:2 * D].astype(F32)) * conv).astype(BF16)
        mg_ref[...] = merged
        mix = _dot(merged, wo_ref[...])
        mix_ref[...] = mix
        h1 = jnp.where(_rows(i, tm) >= PAD, h_ref[...] + _rms(mix, v_ref[3:4, :]), 0.0)
        h1_ref[...] = h1
        n2_ref[...] = _rms(h1, v_ref[4:5, :]).astype(BF16)

    def row(w):
        return pl.BlockSpec((tm, w), lambda i: (i, 0))

    return pl.pallas_call(
        body, name="mixer_fwd", grid=(p // tm,),
        in_specs=[row(D), row(D), row(2 * D), row(D), VM, VM, VM, VM],
        out_specs=[row(D)] * 7,
        out_shape=[jax.ShapeDtypeStruct((p, D), t) for t in (BF16, BF16, BF16, BF16, F32, F32, BF16)],
        compiler_params=_cparams("parallel"),
    )(ao, c0, gates, h0p, wa, wc, wo, vecs)


def _mm_nt(a, w_t, name):
    p, k = a.shape
    n = w_t.shape[0]
    tm = _row_tile(p)
    ch = 512

    def body(a_ref, w_ref, o_ref):
        a_v = a_ref[...]
        for c0 in range(0, n, ch):
            o_ref[:, c0:c0 + ch] = _dot_nt(a_v, w_ref[c0:c0 + ch, :]).astype(BF16)

    return pl.pallas_call(
        body, name=name, grid=(p // tm,),
        in_specs=[pl.BlockSpec((tm, k), lambda i: (i, 0)), VM],
        out_specs=pl.BlockSpec((tm, n), lambda i: (i, 0)),
        out_shape=jax.ShapeDtypeStruct((p, n), BF16),
        compiler_params=_cparams("parallel"),
    )(a, w_t)


def _conv3(xp_ref, w_ref, r0):
    return (w_ref[0:1, :] * xp_ref[r0 + 6:r0 + 6 + BLK, :] + w_ref[1:2, :] * xp_ref[r0 + 7:r0 + 7 + BLK, :]
            + w_ref[2:3, :] * xp_ref[r0 + 8:r0 + 8 + BLK, :])


def _ffn_slab_specs(p):
    ncol = FFN // BLK
    return [pl.BlockSpec((p, BLK), lambda j: (0, j)), pl.BlockSpec((p, BLK), lambda j: (0, ncol + j)),
            pl.BlockSpec((FFN_K, BLK), lambda j: (0, j)), pl.BlockSpec((FFN_K, BLK), lambda j: (0, ncol + j)),
            pl.BlockSpec((1, BLK), lambda j: (0, j)), pl.BlockSpec((1, BLK), lambda j: (0, ncol + j))]


def _fill_shifted(dst, src_ref, nch):
    dst[0:8, :] = jnp.zeros((8, BLK), F32)
    for ci in range(nch):
        dst[8 + BLK * ci:8 + BLK * (ci + 1), :] = src_ref[BLK * ci:BLK * (ci + 1), :].astype(F32)


def _ffn_act(u0, fw, fb):
    p = u0.shape[0]
    nch = p // BLK

    def body(g_ref, v_ref, wg_ref, wv_ref, bg_ref, bv_ref, o_ref, dv_ref, dg_ref, xg, xv):
        _fill_shifted(xg, g_ref, nch)
        _fill_shifted(xv, v_ref, nch)
        for ci in range(nch):
            r0 = BLK * ci
            ug = _conv3(xg, wg_ref, r0) + bg_ref[...]
            uv = _conv3(xv, wv_ref, r0) + bv_ref[...]
            sg = jax.nn.sigmoid(ug)
            silu = ug * sg
            o_ref[r0:r0 + BLK, :] = (silu * uv).astype(BF16)
            dv_ref[r0:r0 + BLK, :] = silu.astype(BF16)
            dg_ref[r0:r0 + BLK, :] = (uv * (sg * (1.0 + ug * (1.0 - sg)))).astype(BF16)

    slab = pl.BlockSpec((p, BLK), lambda j: (0, j))
    return pl.pallas_call(
        body, name="ffn_act", grid=(FFN // BLK,),
        in_specs=_ffn_slab_specs(p),
        out_specs=[slab] * 3,
        out_shape=[jax.ShapeDtypeStruct((p, FFN), BF16)] * 3,
        scratch_shapes=[pltpu.VMEM((p + 8, BLK), F32)] * 2,
        compiler_params=_cparams("parallel"),
    )(u0, u0, fw, fw, fb, fb)


def _ffn_down_loss(act, wd, h1, tgt, gain):
    p = act.shape[0]
    tm = _row_tile(p)
    k = tm // BLK

    def body(*refs):
        a_ref, w_ref, h_ref = refs[:3]
        t_refs = refs[3:3 + k]
        g_ref, df_ref, da_ref, dy_ref, acc_ref = refs[3 + k:]
        i = pl.program_id(0)

        @pl.when(i == 0)
        def _():
            acc_ref[...] = jnp.zeros_like(acc_ref)

        ffn = _dot(a_ref[...], w_ref[...])
        t = jnp.concatenate([t_ref[...] for t_ref in t_refs], axis=0) if k > 1 else t_refs[0][...]
        diff = jnp.where(_rows(i, tm) >= BLK, h_ref[...] + _rms(ffn, g_ref[...]) - t, 0.0)
        dy = diff * (1.0 / D)
        dffn, dg = _rms_bwd(ffn, g_ref[...], dy)
        acc_ref[0:1, :] += dg
        acc_ref[1:2, :] += jnp.sum(diff * diff, axis=0, keepdims=True) * (0.5 / D)
        dy_ref[...] = dy
        dfb = dffn.astype(BF16)
        df_ref[...] = dfb
        for c0 in range(0, FFN, 256):
            da_ref[:, c0:c0 + 256] = _dot_nt(dfb, w_ref[c0:c0 + 256, :]).astype(BF16)

    def row(w):
        return pl.BlockSpec((tm, w), lambda i: (i, 0))

    return pl.pallas_call(
        body, name="ffn_down_loss", grid=(p // tm,),
        in_specs=[row(FFN), VM, row(D)] + _token_specs(tm) + [VM],
        out_specs=[row(D), row(FFN), row(D), pl.BlockSpec((8, D), lambda i: (0, 0))],
        out_shape=[jax.ShapeDtypeStruct((p, D), BF16), jax.ShapeDtypeStruct((p, FFN), BF16),
                   jax.ShapeDtypeStruct((p, D), F32), jax.ShapeDtypeStruct((8, D), F32)],
        compiler_params=_cparams("arbitrary"),
    )(act, wd, h1, *([tgt] * k), gain)


def _mm_tn(pieces, b, name, col_sums=False, comm=None):
    p, n = b.shape
    tk = 256
    nblk = [a.shape[1] // tk for a in pieces]
    offs = [sum(nblk[:q]) for q in range(len(pieces))]
    total = sum(nblk)
    npc = len(pieces)

    def body(*refs):
        a_refs, b_ref, o_ref = refs[:npc], refs[npc], refs[npc + 1]
        i = pl.program_id(0)
        for q, a_ref in enumerate(a_refs):
            @pl.when(jnp.logical_and(i >= offs[q], i < offs[q] + nblk[q]))
            def _(a_ref=a_ref):
                a_v = a_ref[...]
                o_ref[...] = _dot_tn(a_v, b_ref[...]).astype(BF16)
                if col_sums:
                    refs[npc + 2][...] = jnp.sum(a_v.astype(F32), axis=0, keepdims=True)

    def a_spec(q):
        return pl.BlockSpec((p, tk), lambda i: (0, jnp.clip(i - offs[q], 0, nblk[q] - 1)))

    out_specs = [pl.BlockSpec((tk, n), lambda i: (i, 0))]
    out_shape = [jax.ShapeDtypeStruct((total * tk, n), BF16)]
    if col_sums:
        out_specs.append(pl.BlockSpec((1, tk), lambda i: (0, i)))
        out_shape.append(jax.ShapeDtypeStruct((1, total * tk), F32))
    res, sent = _call(
        body, name=name, grid=(total,),
        in_specs=[a_spec(q) for q in range(npc)] + [VM],
        out_specs=out_specs, out_shape=out_shape, args=(*pieces, b), comm=comm)
    res = res if col_sums else res[0]
    return res if comm is None else (res, sent)


def _ffn_act_bwd(u0, dact, dact_dg, dact_dv, fw, act, dffn, comm=None):
    p = u0.shape[0]
    nch = p // BLK
    ncol = FFN // BLK

    def body(g_ref, v_ref, wg_ref, wv_ref, da_ref, lg_ref, lv_ref, act_ref, df_ref,
             dg_ref, dv_ref, gwg_ref, gwv_ref, gbg_ref, gbv_ref, gwd_ref, xg, xv, eg, ev):
        gwd_ref[...] = _dot_tn(act_ref[...], df_ref[...]).astype(BF16)
        _fill_shifted(xg, g_ref, nch)
        _fill_shifted(xv, v_ref, nch)
        eg[p:p + 8, :] = jnp.zeros((8, BLK), F32)
        ev[p:p + 8, :] = jnp.zeros((8, BLK), F32)
        for ci in range(nch):
            r0 = BLK * ci
            d = da_ref[r0:r0 + BLK, :].astype(F32)
            eg[r0:r0 + BLK, :] = d * lg_ref[r0:r0 + BLK, :].astype(F32)
            ev[r0:r0 + BLK, :] = d * lv_ref[r0:r0 + BLK, :].astype(F32)
        for e_s, x_s, w_ref, d_ref, gw_ref, gb_ref in ((eg, xg, wg_ref, dg_ref, gwg_ref, gbg_ref),
                                                      (ev, xv, wv_ref, dv_ref, gwv_ref, gbv_ref)):
            sums = [jnp.zeros((BLK, BLK), F32) for _ in range(FFN_K + 1)]
            for ci in range(nch):
                r0 = BLK * ci
                e0 = e_s[r0:r0 + BLK, :]
                du = (w_ref[2:3, :] * e0 + w_ref[1:2, :] * e_s[r0 + 1:r0 + 1 + BLK, :]
                      + w_ref[0:1, :] * e_s[r0 + 2:r0 + 2 + BLK, :])
                if ci == 0:
                    du = jnp.where(_rows(0, BLK) >= PAD, du, 0.0)
                d_ref[r0:r0 + BLK, :] = du.astype(BF16)
                for j in range(FFN_K):
                    sums[j] = sums[j] + e0 * x_s[r0 + 6 + j:r0 + 6 + j + BLK, :]
                sums[FFN_K] = sums[FFN_K] + e0
            for j in range(FFN_K):
                gw_ref[j:j + 1, :] = jnp.sum(sums[j], axis=0, keepdims=True)
            gb_ref[...] = jnp.sum(sums[FFN_K], axis=0, keepdims=True)

    slab = pl.BlockSpec((p, BLK), lambda j: (0, j))
    wspec = pl.BlockSpec((FFN_K, BLK), lambda j: (0, j))
    bspec = pl.BlockSpec((1, BLK), lambda j: (0, j))
    return _call(
        body, name="ffn_act_bwd", grid=(ncol,),
        in_specs=_ffn_slab_specs(p)[:4] + [slab] * 4 + [VM],
        out_specs=[slab, slab, wspec, wspec, bspec, bspec, pl.BlockSpec((BLK, D), lambda j: (j, 0))],
        out_shape=[jax.ShapeDtypeStruct((p, FFN), BF16)] * 2 + [jax.ShapeDtypeStruct((FFN_K, FFN), F32)] * 2
        + [jax.ShapeDtypeStruct((1, FFN), F32)] * 2 + [jax.ShapeDtypeStruct((FFN, D), BF16)],
        scratch=[pltpu.VMEM((p + 8, BLK), F32)] * 4,
        args=(u0, u0, fw, fw, dact, dact_dg, dact_dv, act, dffn), comm=comm)


def _ffn_in_bwd(dug, duv, w_upt, h1, dy, gain, comm=None):
    p = h1.shape[0]
    tm = _row_tile(p)

    def body(dg_ref, dv_ref, w_ref, h_ref, dy_ref, g_ref, o_ref, acc_ref):
        i = pl.program_id(0)

        @pl.when(i == 0)
        def _():
            acc_ref[...] = jnp.zeros_like(acc_ref)

        dn = _dot(dg_ref[...], w_ref[0:FFN, :]) + _dot(dv_ref[...], w_ref[FFN:2 * FFN, :])
        dh, dg = _rms_bwd(h_ref[...], g_ref[...], dn)
        o_ref[...] = dy_ref[...] + dh
        acc_ref[0:1, :] += dg

    def row(w):
        return pl.BlockSpec((tm, w), lambda i: (i, 0))

    return _call(
        body, name="ffn_in_bwd", grid=(p // tm,),
        in_specs=[row(FFN), row(FFN), VM, row(D), row(D), VM],
        out_specs=[row(D), pl.BlockSpec((8, D), lambda i: (0, 0))],
        out_shape=[jax.ShapeDtypeStruct((p, D), F32), jax.ShapeDtypeStruct((8, D), F32)],
        sem="arbitrary", args=(dug, duv, w_upt, h1, dy, gain), comm=comm)


def _mixer_bwd(dh1, mix, attn, conv, gates, c0, wa, wc, wo, vecs, comm=None):
    p = dh1.shape[0]
    tm = _row_tile(p)

    def body(dh_ref, mix_ref, at_ref, cv_ref, gt_ref, c0_ref, wa_ref, wc_ref, wo_ref, v_ref,
             dmix_ref, dat_ref, dcv_ref, dgt_ref, dao_ref, dc0_ref, acc_ref):
        i = pl.program_id(0)

        @pl.when(i == 0)
        def _():
            acc_ref[...] = jnp.zeros_like(acc_ref)

        dmix, dgp = _rms_bwd(mix_ref[...], v_ref[3:4, :], dh_ref[...])
        dmix = dmix.astype(BF16)
        dmix_ref[...] = dmix
        dmg = _dot_nt(dmix, wo_ref[...])
        sa = jax.nn.sigmoid(gt_ref[:, 0:D].astype(F32))
        sc = jax.nn.sigmoid(gt_ref[:, D:2 * D].astype(F32))
        dat = dmg * sa
        dcv = dmg * sc
        dgt_ref[:, 0:D] = (dmg * at_ref[...].astype(F32) * sa * (1.0 - sa)).astype(BF16)
        dgt_ref[:, D:2 * D] = (dmg * cv_ref[...].astype(F32) * sc * (1.0 - sc)).astype(BF16)
        datb = dat.astype(BF16)
        dcvb = dcv.astype(BF16)
        dat_ref[...] = datb
        dcv_ref[...] = dcvb
        dao_ref[...] = _dot_nt(datb, wa_ref[...]).astype(BF16)
        dc1 = _dot_nt(dcvb, wc_ref[...])
        dc0, dlg, dlb = _lnsilu_bwd(c0_ref[...], v_ref[0:1, :], v_ref[1:2, :], dc1)
        dc0_ref[...] = dc0
        acc_ref[0:1, :] += dgp
        acc_ref[1:2, :] += jnp.sum(dcv, axis=0, keepdims=True)
        acc_ref[2:3, :] += dlg
        acc_ref[3:4, :] += dlb

    def row(w):
        return pl.BlockSpec((tm, w), lambda i: (i, 0))

    return _call(
        body, name="mixer_bwd", grid=(p // tm,),
        in_specs=[row(D), row(D), row(D), row(D), row(2 * D), row(D), VM, VM, VM, VM],
        out_specs=[row(D), row(D), row(D), row(2 * D), row(D), row(D), pl.BlockSpec((8, D), lambda i: (0, 0))],
        out_shape=[jax.ShapeDtypeStruct((p, D), BF16)] * 3 + [jax.ShapeDtypeStruct((p, 2 * D), BF16),
                                                             jax.ShapeDtypeStruct((p, D), BF16),
                                                             jax.ShapeDtypeStruct((p, D), F32),
                                                             jax.ShapeDtypeStruct((8, D), F32)],
        sem="arbitrary", args=(dh1, mix, attn, conv, gates, c0, wa, wc, wo, vecs), comm=comm)


def _conv31_bwd(ag, dc0, w32, tn_pairs, comm=None):
    p = ag.shape[0]
    nch = p // BLK
    npair = len(tn_pairs)

    def body(*refs):
        a_ref, g_ref, dc_ref, w_ref = refs[:4]
        tn_a, tn_b = refs[4:4 + npair], refs[4 + npair:4 + 2 * npair]
        da_ref, dg_ref, gw_ref, gb_ref = refs[4 + 2 * npair:8 + 2 * npair]
        tn_o = refs[8 + 2 * npair:8 + 3 * npair]
        gp, dp = refs[8 + 3 * npair:]
        for ta, tb, to in zip(tn_a, tn_b, tn_o):
            to[...] = _dot_tn(ta[...], tb[...]).astype(BF16)
        gp[0:32, :] = jnp.zeros((32, BLK), F32)
        dp[p:p + 32, :] = jnp.zeros((32, BLK), F32)
        bsum = jnp.zeros((BLK, BLK), F32)
        for ci in range(nch):
            r0 = BLK * ci
            glu = a_ref[r0:r0 + BLK, :].astype(F32) * jax.nn.sigmoid(g_ref[r0:r0 + BLK, :].astype(F32))
            if ci == 0:
                glu = jnp.where(_rows(0, BLK) >= PAD, glu, 0.0)
            gp[32 + r0:32 + r0 + BLK, :] = glu
            d = dc_ref[r0:r0 + BLK, :]
            dp[r0:r0 + BLK, :] = d
            bsum = bsum + d
        gb_ref[...] = jnp.sum(bsum, axis=0, keepdims=True)
        for ci in range(nch):
            r0 = BLK * ci
            acc = jnp.zeros((BLK, BLK), F32)
            for j in range(CONV_K):
                acc = acc + w_ref[j:j + 1, :] * dp[r0 + 30 - j:r0 + 30 - j + BLK, :]
            if ci == 0:
                acc = jnp.where(_rows(0, BLK) >= PAD, acc, 0.0)
            a = a_ref[r0:r0 + BLK, :].astype(F32)
            sg = jax.nn.sigmoid(g_ref[r0:r0 + BLK, :].astype(F32))
            da_ref[r0:r0 + BLK, :] = (acc * sg).astype(BF16)
            dg_ref[r0:r0 + BLK, :] = (acc * a * sg * (1.0 - sg)).astype(BF16)
        for j in range(CONV_K):
            acc = jnp.zeros((BLK, BLK), F32)
            for ci in range(nch):
                r0 = BLK * ci
                acc = acc + dp[r0:r0 + BLK, :] * gp[r0 + j + 2:r0 + j + 2 + BLK, :]
            gw_ref[j:j + 1, :] = jnp.sum(acc, axis=0, keepdims=True)
        gw_ref[CONV_K:32, :] = jnp.zeros((32 - CONV_K, BLK), F32)

    slab = pl.BlockSpec((p, BLK), lambda j: (0, j))
    return _call(
        body, name="conv31_bwd", grid=(D // BLK,),
        in_specs=[slab, pl.BlockSpec((p, BLK), lambda j: (0, 8 + j)), slab, pl.BlockSpec((32, BLK), lambda j: (0, j))]
        + [slab] * npair + [VM] * npair,
        out_specs=[slab, slab, pl.BlockSpec((32, BLK), lambda j: (0, j)), pl.BlockSpec((1, BLK), lambda j: (0, j))]
        + [pl.BlockSpec((BLK, D), lambda j: (j, 0))] * npair,
        out_shape=[jax.ShapeDtypeStruct((p, D), BF16)] * 2 + [jax.ShapeDtypeStruct((32, D), F32),
                                                             jax.ShapeDtypeStruct((1, D), F32)]
        + [jax.ShapeDtypeStruct((D, D), BF16)] * npair,
        scratch=[pltpu.VMEM((p + 32, BLK), F32)] * 2,
        args=(ag, ag, dc0, w32, *[a for a, _ in tn_pairs], *[b for _, b in tn_pairs]), comm=comm)


def _attn_bwd(q, kv, dao, sinks, tabs, comm=None):
    p = q.shape[0]
    nb = p // BLK

    def body(q_ref, km_ref, kp_ref, kc_ref, do_ref, sink_ref, t_ref, dqkv_ref, dsink_ref, carry, macc):
        i = pl.program_id(0)
        n = nb - 1 - i

        @pl.when(i == 0)
        def _():
            carry[...] = jnp.zeros_like(carry)
            macc[...] = jnp.zeros_like(macc)
            dsink_ref[...] = jnp.zeros_like(dsink_ref)

        lo = lax.broadcasted_iota(jnp.int32, (BLK, BLK), 1) < HEAD_DIM
        lane8 = lax.broadcasted_iota(jnp.int32, (8, BLK), 1)
        c, s1, s2 = t_ref[:, 0:128], -t_ref[:, 128:256], -t_ref[:, 256:384]
        dk = jnp.zeros((N_KEY, BLK), F32)
        dv = jnp.zeros((N_KEY, BLK), F32)
        for h in range(2):
            qs, k2, v2, bias, lok = _attn_setup(n, h, q_ref, km_ref, kp_ref, kc_ref)
            dos = _stack_heads(do_ref, h, lo)
            st = _dot_nt(k2, qs)
            dpt = _dot_nt(v2, dos)
            p_parts, ds_parts = [], []
            for g in range(8):
                cols = slice(BLK * g, BLK * (g + 1))
                pn, ps = _attn_head(st[:, cols], bias, sink_ref[0, 8 * h + g])
                dp = dpt[:, cols]
                delta = jnp.sum(pn * dp, axis=0, keepdims=True)
                ds_parts.append((pn * (dp - delta)).astype(BF16))
                p_parts.append(pn.astype(BF16))
                dsk = -jnp.sum(ps * delta, axis=1, keepdims=True)
                dsink_ref[...] += jnp.where(lane8 == 8 * h + g, dsk, 0.0)
            dst = jnp.concatenate(ds_parts, axis=1)
            pt = jnp.concatenate(p_parts, axis=1)
            dq = _dot_tn(dst, k2)
            for jp in range(4):
                lo_c = BLK * (4 * h + jp)
                dqkv_ref[:, lo_c:lo_c + BLK] = (_rope(_unstack_heads(dq, jp, lo), c, s1, s2) * SCALE).astype(BF16)
            dk2 = _dot(dst, qs)
            dv2 = _dot(pt, dos)
            dk2 = dk2 + pltpu.roll(dk2, HEAD_DIM, 1)
            dv2 = dv2 + pltpu.roll(dv2, HEAD_DIM, 1)
            own = lok if h == 0 else jnp.logical_not(lok)
            dk = jnp.where(own, dk2, dk)
            dv = jnp.where(own, dv2, dv)
        macc[:, 0:BLK] += dk[2 * BLK:N_KEY]
        macc[:, BLK:2 * BLK] += dv[2 * BLK:N_KEY]
        last = (n == 0).astype(F32)
        zpad = jnp.zeros((PAD, BLK), F32)
        dk_c = dk[BLK:2 * BLK] + carry[:, 0:BLK] + last * jnp.concatenate([zpad, macc[:, 0:BLK]], axis=0)
        dv_c = dv[BLK:2 * BLK] + carry[:, BLK:2 * BLK] + last * jnp.concatenate([zpad, macc[:, BLK:2 * BLK]], axis=0)
        carry[:, 0:BLK] = dk[0:BLK]
        carry[:, BLK:2 * BLK] = dv[0:BLK]
        dqkv_ref[:, D:D + BLK] = _rope(dk_c, c, s1, s2).astype(BF16)
        dqkv_ref[:, D + BLK:D + 2 * BLK] = dv_c.astype(BF16)

    def rev(w):
        return pl.BlockSpec((BLK, w), lambda i: (nb - 1 - i, 0))

    return _call(
        body, name="attn_bwd", grid=(nb,),
        in_specs=[rev(D),
                  pl.BlockSpec((BLK, 256), lambda i: (0, 0)),
                  pl.BlockSpec((BLK, 256), lambda i: (jnp.maximum(nb - 2 - i, 0), 0)),
                  rev(256), rev(D),
                  pl.BlockSpec(memory_space=pltpu.SMEM), rev(384)],
        out_specs=[rev(QKV_W), pl.BlockSpec((8, BLK), lambda i: (0, 0))],
        out_shape=[jax.ShapeDtypeStruct((p, QKV_W), BF16), jax.ShapeDtypeStruct((8, BLK), F32)],
        scratch=[pltpu.VMEM((BLK, 256), F32), pltpu.VMEM((N_META, 256), F32)], sem="arbitrary",
        args=(q, kv, kv, kv, dao, sinks, tabs), comm=comm)


def _in_bwd(dqkv, da, dg, dgt, w_int, h0p, dh1, gain, comm=None):
    p = h0p.shape[0]
    tm = _row_tile(p)
    nt = p // tm
    first_rows = tm - BLK

    def body(dq_ref, da_ref, dg_ref, dt_ref, w_ref, h_ref, dh_ref, g_ref, gx_ref, dm_ref, acc_ref, buf, sems):
        i = pl.program_id(0)
        slot = i % 2

        @pl.when(i == 0)
        def _():
            acc_ref[...] = jnp.zeros_like(acc_ref)

        dn = (_dot(dq_ref[...], w_ref[0:QKV_W, :]) + _dot(da_ref[...], w_ref[QKV_W:QKV_W + D, :])
              + _dot(dg_ref[...], w_ref[QKV_W + D:QKV_W + 2 * D, :]) + _dot(dt_ref[...], w_ref[QKV_W + 2 * D:IN_W, :]))
        dh, dgain = _rms_bwd(h_ref[...], g_ref[...], dn)
        dh0 = dh_ref[...] + dh
        acc_ref[0:1, :] += dgain
        buf[slot] = dh0

        @pl.when(i == 0)
        def _():
            dm_ref[...] = dh0[PAD:BLK]

        def first_copy():
            return pltpu.make_async_copy(buf.at[0, pl.ds(BLK, first_rows), :], gx_ref.at[pl.ds(0, first_rows), :], sems.at[0])

        def tile_copy(j, s):
            return pltpu.make_async_copy(buf.at[s], gx_ref.at[pl.ds(pl.multiple_of(j * tm - BLK, BLK), tm), :], sems.at[s])

        if first_rows:
            @pl.when(i == 1)
            def _():
                first_copy().wait()

        @pl.when(i >= 2)
        def _():
            tile_copy(i - 1, 1 - slot).wait()

        if first_rows:
            @pl.when(i == 0)
            def _():
                first_copy().start()

        @pl.when(i > 0)
        def _():
            tile_copy(i, slot).start()

        @pl.when(i == nt - 1)
        def _():
            tile_copy(i, slot).wait()

    def row(w):
        return pl.BlockSpec((tm, w), lambda i: (i, 0))

    return _call(
        body, name="in_bwd", grid=(nt,),
        in_specs=[row(QKV_W), row(D), row(D), row(2 * D), VM, row(D), row(D), VM],
        out_specs=[ANY, pl.BlockSpec((N_META, D), lambda i: (0, 0)), pl.BlockSpec((8, D), lambda i: (0, 0))],
        out_shape=[jax.ShapeDtypeStruct((p - BLK, D), F32), jax.ShapeDtypeStruct((N_META, D), F32),
                   jax.ShapeDtypeStruct((8, D), F32)],
        scratch=[pltpu.VMEM((2, tm, D), F32), pltpu.SemaphoreType.DMA((2,))],
        sem="arbitrary", args=(dqkv, da, dg, dgt, w_int, h0p, dh1, gain), comm=comm)


def _sum_slots(slots, name):
    r = slots.shape[0] // N_DEV
    cols = slots.shape[1]
    tr = r if r <= 352 else (r // 2 if (r // 2) % 16 == 0 else r // 3)
    steps = r // tr

    def body(*refs):
        acc = refs[0][...].astype(F32)
        for s in range(1, N_DEV):
            acc = acc + refs[s][...].astype(F32)
        refs[N_DEV][...] = acc

    return pl.pallas_call(
        body, name=name, grid=(steps,),
        in_specs=[pl.BlockSpec((tr, cols), functools.partial(lambda i, s: (s * steps + i, 0), s=s)) for s in range(N_DEV)],
        out_specs=pl.BlockSpec((tr, cols), lambda i: (i, 0)),
        out_shape=jax.ShapeDtypeStruct((r, cols), F32),
        compiler_params=_cparams("parallel"),
    )(*([slots] * N_DEV))


def _adamw_math(w, g, m, v):
    m_n = ADAM_B1 * m + (1.0 - ADAM_B1) * g
    v_n = ADAM_B2 * v + (1.0 - ADAM_B2) * jnp.square(g)
    m_hat = m_n / (1.0 - ADAM_B1 ** ADAM_STEP)
    v_hat = v_n / (1.0 - ADAM_B2 ** ADAM_STEP)
    return -ADAM_LR * (m_hat / (jnp.sqrt(v_hat) + ADAM_EPS) + ADAM_WD * w), m_n, v_n


def _sum_adamw(parts, w, m, v, name, nslots=N_DEV):
    r, cols = w.shape
    rs = r // len(parts)
    tr = rs if rs <= 352 else (rs // 2 if (rs // 2) % 16 == 0 else rs // 3)
    steps = rs // tr

    def body(*refs):
        w_ref, m_ref, v_ref, g_ref, d_ref, nm_ref, nv_ref = refs[nslots * len(parts):]
        i = pl.program_id(0)
        for q in range(len(parts)):
            @pl.when(i // steps == q)
            def _(q=q):
                g = refs[nslots * q][...].astype(F32)
                for s in range(1, nslots):
                    g = g + refs[nslots * q + s][...].astype(F32)
                g_ref[...] = g
                d_ref[...], nm_ref[...], nv_ref[...] = _adamw_math(w_ref[...], g, m_ref[...], v_ref[...])

    def slot_spec(q, s):
        return pl.BlockSpec((tr, cols), lambda i: (s * steps + jnp.clip(i - q * steps, 0, steps - 1), 0))

    spec = pl.BlockSpec((tr, cols), lambda i: (i, 0))
    return pl.pallas_call(
        body, name=name, grid=(steps * len(parts),),
        in_specs=[slot_spec(q, s) for q in range(len(parts)) for s in range(nslots)] + [spec] * 3,
        out_specs=[spec] * 4, out_shape=[jax.ShapeDtypeStruct((r, cols), F32)] * 4,
        compiler_params=_cparams("parallel"),
    )(*[a for a in parts for _ in range(nslots)], w, m, v)


def _adamw(w, g, m, v, name):
    r, cols = w.shape
    tr = 256 if r % 256 == 0 else r

    def body(w_ref, g_ref, m_ref, v_ref, d_ref, nm_ref, nv_ref):
        d_ref[...], nm_ref[...], nv_ref[...] = _adamw_math(w_ref[...], g_ref[...], m_ref[...], v_ref[...])

    spec = pl.BlockSpec((tr, cols), lambda i: (i, 0))
    return pl.pallas_call(
        body, name=name, grid=(r // tr,),
        in_specs=[spec] * 4, out_specs=[spec] * 3,
        out_shape=[jax.ShapeDtypeStruct((r, cols), F32)] * 3,
        compiler_params=_cparams("parallel"),
    )(w, g, m, v)


def _rope_tables(p):
    half = ROT_DIM // 2
    inv_freq = ROPE_THETA ** (-jnp.arange(half, dtype=F32) * 2.0 / ROT_DIM)
    pos = (jnp.arange(p) - PAD).astype(F32)
    ang = pos[:, None] * inv_freq[None, :]
    lane = jnp.arange(BLK)
    seg = (lane % HEAD_DIM) // half
    cos = jnp.cos(ang)[:, lane % half]
    sin = jnp.sin(ang)[:, lane % half]
    c = jnp.where(seg[None, :] < 2, cos, 1.0)
    s1 = jnp.where(seg[None, :] == 0, -sin, 0.0)
    s2 = jnp.where(seg[None, :] == 1, sin, 0.0)
    return jnp.concatenate([c, s1, s2], axis=1).astype(F32)


def _flat_pack(parts, rows):
    flat = jnp.concatenate([a.reshape(-1).astype(F32) for a in parts])
    return jnp.pad(flat, (0, rows * D - flat.shape[0])).reshape(rows, D)


def _flat_unpack(pack, shapes):
    flat = pack.reshape(-1)
    out, off = [], 0
    for s in shapes:
        size = 1
        for e in s:
            size *= e
        out.append(flat[off:off + size].reshape(s))
        off += size
    return out


def kernel(x, meta_tokens, norm_pre_mix, norm_post_mix, w_in, b_in, attn_sinks, w_attn_proj, conv_dw_w, conv_dw_b, conv_ln_g, conv_ln_b, w_conv_proj, b_conv_proj, w_out, norm_pre_ffn, norm_post_ffn, w_up, ffn_dw_w, ffn_dw_b, w_down, loss_target, m_meta_tokens, m_norm_pre_mix, m_norm_post_mix, m_w_in, m_b_in, m_attn_sinks, m_w_attn_proj, m_conv_dw_w, m_conv_dw_b, m_conv_ln_g, m_conv_ln_b, m_w_conv_proj, m_b_conv_proj, m_w_out, m_norm_pre_ffn, m_norm_post_ffn, m_w_up, m_ffn_dw_w, m_ffn_dw_b, m_w_down, v_meta_tokens, v_norm_pre_mix, v_norm_post_mix, v_w_in, v_b_in, v_attn_sinks, v_w_attn_proj, v_conv_dw_w, v_conv_dw_b, v_conv_ln_g, v_conv_ln_b, v_w_conv_proj, v_b_conv_proj, v_w_out, v_norm_pre_ffn, v_norm_post_ffn, v_w_up, v_ffn_dw_w, v_ffn_dw_b, v_w_down):
    seq = x.shape[1]
    p = seq + BLK
    me = 4 * lax.axis_index("x") + 2 * lax.axis_index("y") + lax.axis_index("c")
    in_cols = w_in.shape[2]
    up_cols = w_up.shape[2]

    small = jnp.zeros((56, up_cols), F32)
    small = small.at[0:N_META, 0:BLK].set(meta_tokens)
    small = small.at[16:16 + CONV_K, 0:BLK].set(conv_dw_w[0])
    small = small.at[48:48 + FFN_K, :].set(ffn_dw_w[0])
    w_int, small_all = _exchange(_Both(_GatherRelay(w_in[0].T.astype(BF16)), _Gather([small])), "gather_w_in")
    small_all = small_all.reshape(N_DEV, 56, up_cols)
    meta_full = small_all[:, 0:N_META, 0:BLK].transpose(1, 0, 2).reshape(N_META, D)
    cdw = small_all[:, 16:16 + CONV_K, 0:BLK].transpose(1, 0, 2).reshape(CONV_K, D)
    cdw32 = jnp.pad(cdw, ((0, 32 - CONV_K), (0, 0)))
    fdw = small_all[:, 48:48 + FFN_K, :].transpose(1, 0, 2).reshape(FFN_K, 2 * FFN)

    tabs = _rope_tables(p)
    vecs = jnp.concatenate([conv_ln_g, conv_ln_b, b_conv_proj, norm_post_mix, norm_pre_ffn, jnp.zeros((3, D), F32)], axis=0)

    (h0p, n1, q, kv, ag, gates), (wa, wc, wo) = _in_proj(
        x[0], meta_full, norm_pre_mix, w_int, b_in, tabs,
        comm=_Gather([w_attn_proj[0].astype(BF16), w_conv_proj[0].astype(BF16), w_out[0].astype(BF16)]))
    (ao,), (w_upt,) = _attn_fwd(q, kv, attn_sinks, comm=_Gather([w_up[0].T.astype(BF16)]))
    (c0,), (wd,) = _conv31_fwd(ag, cdw32, conv_dw_b, comm=_Gather([w_down[0].astype(BF16)]))
    c1, attn, conv, merged, mix, h1, n2 = _mixer_fwd(ao, c0, gates, h0p, wa, wc, wo, vecs)
    u0 = _mm_nt(n2, w_upt, "ffn_up")
    act, dact_dv, dact_dg = _ffn_act(u0, fdw, ffn_dw_b)
    dffn, dact, dy, acc_f = _ffn_down_loss(act, wd, h1, loss_target[0], norm_post_ffn)

    (dug, duv, gfw_g, gfw_v, gfb_g, gfb_v, g_wd), _ = _ffn_act_bwd(u0, dact, dact_dg, dact_dv, fdw, act, dffn)
    g_wupt, (s_wd,) = _mm_tn([dug, duv], n2, "grad_w_up", comm=_Scatter([g_wd]))
    (dh1, acc_u), (s_wup0,) = _ffn_in_bwd(dug, duv, w_upt, h1, dy, norm_pre_ffn, comm=_Scatter([g_wupt], 0, 2))
    (dmix, dat, dcv, dgt, dao, dc0, acc_m), (s_wup1,) = _mixer_bwd(
        dh1, mix, attn, conv, gates, c0, wa, wc, wo, vecs, comm=_Scatter([g_wupt], 1, 2))
    (da, dg, g_cdw, g_cdb, g_wo, g_wa, g_wc), _ = _conv31_bwd(ag, dc0, cdw32, [(merged, dmix), (ao, dat), (c1, dcv)])
    (dqkv, dsink), (s_wa, s_wc, s_wo) = _attn_bwd(q, kv, dao, attn_sinks, tabs, comm=_Scatter([g_wa, g_wc, g_wo]))
    loss_row = jnp.sum(acc_f[1:2, :], axis=1, keepdims=True)
    early = [loss_row, acc_m[0:1], dsink[0:1, 0:16], g_cdw[0:CONV_K], g_cdb,
             acc_m[2:3], acc_m[3:4], acc_m[1:2], acc_u[0:1], acc_f[0:1],
             jnp.concatenate([gfw_g, gfw_v], axis=1), jnp.concatenate([gfb_g, gfb_v], axis=1)]
    (g_wint, g_bin), (gathered_early,) = _mm_tn([dqkv, da, dg, dgt], n1, "grad_w_in", col_sums=True,
                                                comm=_Gather([_flat_pack(early, 64)]))
    (from_sibling,) = _exchange(_SiblingSwap(g_wint), "swap_w_in")
    (grad_x2d, dmeta, acc_i), (s_win,) = _in_bwd(dqkv, da, dg, dgt, w_int, h0p, dh1, norm_pre_mix,
                                                 comm=_ChipScatter(_pair_add(g_wint, from_sibling)))

    big = []
    for nm, parts, nslots, w, m, v, tr in (
            ("w_in", [s_win], N_CHIP, w_in, m_w_in, v_w_in, True), ("w_up", [s_wup0, s_wup1], N_DEV, w_up, m_w_up, v_w_up, True),
            ("w_attn_proj", [s_wa], N_DEV, w_attn_proj, m_w_attn_proj, v_w_attn_proj, False),
            ("w_conv_proj", [s_wc], N_DEV, w_conv_proj, m_w_conv_proj, v_w_conv_proj, False),
            ("w_out", [s_wo], N_DEV, w_out, m_w_out, v_w_out, False),
            ("w_down", [s_wd], N_DEV, w_down, m_w_down, v_w_down, False)):
        ins = [a[0].T if tr else a[0] for a in (w, m, v)]
        big.append(tuple((o.T if tr else o)[None] for o in _sum_adamw(parts, *ins, "update_" + nm, nslots)))

    late = [dmeta, acc_i[0:1], g_bin]
    (gathered_late,) = _exchange(_Gather([_flat_pack(late, 24)]), "gather_small_grads")
    g_meta, g_npm, g_bi = _flat_unpack(_sum_slots(gathered_late, "sum_late_grads"), [a.shape for a in late])
    tot = _flat_unpack(_sum_slots(gathered_early, "sum_small_grads"), [a.shape for a in early])
    (loss, g_nqm, g_sk, g_cw, g_cb, g_lg, g_lb, g_bc, g_npf, g_nqf, g_fw, g_fb) = tot
    loss = loss.reshape(())
    g_meta = lax.dynamic_slice_in_dim(g_meta, me * BLK, BLK, axis=1)
    g_cw = lax.dynamic_slice_in_dim(g_cw, me * BLK, BLK, axis=1)[None]
    g_fw = lax.dynamic_slice_in_dim(g_fw, me * up_cols, up_cols, axis=1)[None]

    sm_w = [meta_tokens, norm_pre_mix, norm_post_mix, b_in, attn_sinks, conv_dw_w, conv_dw_b, conv_ln_g, conv_ln_b,
            b_conv_proj, norm_pre_ffn, norm_post_ffn, ffn_dw_w, ffn_dw_b]
    sm_g = [g_meta, g_npm, g_nqm, g_bi, g_sk, g_cw, g_cb, g_lg, g_lb, g_bc, g_npf, g_nqf, g_fw, g_fb]
    sm_m = [m_meta_tokens, m_norm_pre_mix, m_norm_post_mix, m_b_in, m_attn_sinks, m_conv_dw_w, m_conv_dw_b, m_conv_ln_g,
            m_conv_ln_b, m_b_conv_proj, m_norm_pre_ffn, m_norm_post_ffn, m_ffn_dw_w, m_ffn_dw_b]
    sm_v = [v_meta_tokens, v_norm_pre_mix, v_norm_post_mix, v_b_in, v_attn_sinks, v_conv_dw_w, v_conv_dw_b, v_conv_ln_g,
            v_conv_ln_b, v_b_conv_proj, v_norm_pre_ffn, v_norm_post_ffn, v_ffn_dw_w, v_ffn_dw_b]
    sm_shapes = [a.shape for a in sm_w]
    upd_rows = 32
    v_pack = _flat_pack(sm_v, upd_rows)
    sm_out = _adamw(_flat_pack(sm_w, upd_rows), _flat_pack(sm_g, upd_rows), _flat_pack(sm_m, upd_rows), v_pack, "adamw_small")
    sm_d, sm_nm, sm_nv = (_flat_unpack(o, sm_shapes) for o in sm_out)

    order = ["meta_tokens", "norm_pre_mix", "norm_post_mix", "w_in", "b_in", "attn_sinks", "w_attn_proj", "conv_dw_w",
             "conv_dw_b", "conv_ln_g", "conv_ln_b", "w_conv_proj", "b_conv_proj", "w_out", "norm_pre_ffn", "norm_post_ffn",
             "w_up", "ffn_dw_w", "ffn_dw_b", "w_down"]
    small_names = ["meta_tokens", "norm_pre_mix", "norm_post_mix", "b_in", "attn_sinks", "conv_dw_w", "conv_dw_b", "conv_ln_g",
                   "conv_ln_b", "b_conv_proj", "norm_pre_ffn", "norm_post_ffn", "ffn_dw_w", "ffn_dw_b"]
    big_names = ["w_in", "w_up", "w_attn_proj", "w_conv_proj", "w_out", "w_down"]
    table = {}
    for k, nm in enumerate(small_names):
        table[nm] = (sm_g[k], sm_d[k], sm_nm[k], sm_nv[k])
    for k, nm in enumerate(big_names):
        table[nm] = big[k]
    grad_x = grad_x2d[None]
    outs = [loss, grad_x]
    for field in range(4):
        outs += [table[nm][field] for nm in order]
    return tuple(outs)
```

```python
import functools

import jax
import jax.numpy as jnp
from jax import lax
from jax.experimental import pallas as pl
from jax.experimental.pallas import tpu as pltpu

F32 = jnp.float32
BF16 = jnp.bfloat16
MESH = pl.DeviceIdType.MESH

D = 1024
HEAD_DIM = 64
N_META = 16
BLK = 128
PAD = BLK - N_META
CONV_K = 31
FFN = 2816
FFN_K = 3
QKV_W = 1280
IN_W = 5376
ROT_DIM = 16
ROPE_THETA = 500000.0
RMS_EPS = 1e-6
LN_EPS = 1e-5
NEG_INF = -1e30
SCALE = HEAD_DIM ** -0.5
N_DEV = 8

ADAM_LR = 0.001
ADAM_B1 = 0.9
ADAM_B2 = 0.999
ADAM_EPS = 1e-08
ADAM_WD = 0.01
ADAM_STEP = 10

VMEM_BYTES_V7X = 64 * 1024 * 1024
VMEM_LIMIT = VMEM_BYTES_V7X - 8 * 1024 * 1024

NT = (((1,), (1,)), ((), ()))
TN = (((0,), (0,)), ((), ()))
VM = pl.BlockSpec(memory_space=pltpu.VMEM)
ANY = pl.BlockSpec(memory_space=pl.ANY)


def _cparams(*sem):
    return pltpu.CompilerParams(dimension_semantics=sem or None, vmem_limit_bytes=VMEM_LIMIT)


def _row_tile(p):
    return 384 if p % 384 == 0 else 128


def _dot(a, b):
    return jnp.dot(a, b, preferred_element_type=F32)


def _dot_nt(a, b):
    return lax.dot_general(a, b, NT, preferred_element_type=F32)


def _dot_tn(a, b):
    return lax.dot_general(a, b, TN, preferred_element_type=F32)


def _rms(x, g):
    return x * lax.rsqrt(jnp.mean(x * x, axis=-1, keepdims=True) + RMS_EPS) * g


def _lnsilu(x, g, b):
    mu = jnp.mean(x, axis=-1, keepdims=True)
    var = jnp.mean(jnp.square(x - mu), axis=-1, keepdims=True)
    z = (x - mu) * lax.rsqrt(var + LN_EPS) * g + b
    return z * jax.nn.sigmoid(z)


def _rms_bwd(x, g, dy):
    r = lax.rsqrt(jnp.mean(x * x, axis=-1, keepdims=True) + RMS_EPS)
    xn = x * r
    u = dy * g
    dg = jnp.sum(dy * xn, axis=0, keepdims=True)
    dx = r * (u - xn * jnp.mean(u * xn, axis=-1, keepdims=True))
    return dx, dg


def _lnsilu_bwd(x, g, b, dout):
    mu = jnp.mean(x, axis=-1, keepdims=True)
    xc = x - mu
    rs = lax.rsqrt(jnp.mean(xc * xc, axis=-1, keepdims=True) + LN_EPS)
    yh = xc * rs
    z = yh * g + b
    sg = jax.nn.sigmoid(z)
    dz = dout * (sg * (1.0 + z * (1.0 - sg)))
    dg = jnp.sum(dz * yh, axis=0, keepdims=True)
    db = jnp.sum(dz, axis=0, keepdims=True)
    dyh = dz * g
    dx = rs * (dyh - jnp.mean(dyh, axis=-1, keepdims=True) - yh * jnp.mean(dyh * yh, axis=-1, keepdims=True))
    return dx, dg, db


def _rope(v, c, s1, s2):
    return v * c + pltpu.roll(v, BLK - 8, 1) * s1 + pltpu.roll(v, 8, 1) * s2


def _rows(i, tm):
    return i * tm + lax.broadcasted_iota(jnp.int32, (tm, 1), 0)


def _place():
    return lax.axis_index("x"), lax.axis_index("y"), lax.axis_index("c")


def _blk(ref, idx, r, dtype):
    return ref.at[pl.ds(pl.multiple_of(idx * r, 16 if dtype == BF16 else 8), r), :]


class _Gather:
    def __init__(self, arrs):
        self.ins = list(arrs)
        n = len(arrs)
        self.out_shape = [jax.ShapeDtypeStruct((N_DEV * a.shape[0], a.shape[1]), a.dtype) for a in arrs]
        self.scratch = [pltpu.SemaphoreType.DMA((n, 7)), pltpu.SemaphoreType.DMA((n, 7)), pltpu.SemaphoreType.DMA((n,))]

    def _parts(self, ins, outs, sems):
        send_sems, recv_sems, local_sems = sems
        n = len(ins)
        x, y, c = _place()
        me, sibling = (x, y, c), (x, y, 1 - c)
        chips = [(1 - x, y), (x, 1 - y), (1 - x, 1 - y)]

        def rows(a, p):
            return _blk(outs[a], 4 * p[0] + 2 * p[1] + p[2], self.ins[a].shape[0], self.ins[a].dtype)

        def copy(a, k, block, to, src=None):
            return pltpu.make_async_remote_copy(
                src_ref=rows(a, block) if src is None else src, dst_ref=rows(a, block),
                send_sem=send_sems.at[a, k], recv_sem=recv_sems.at[a, k], device_id=to, device_id_type=MESH)

        mine = [pltpu.make_async_copy(ins[a], rows(a, me), local_sems.at[a]) for a in range(n)]
        first = []
        for a in range(n):
            first.append(copy(a, 0, me, sibling, src=ins[a]))
            first += [copy(a, 1 + j, me, (*chip, c), src=ins[a]) for j, chip in enumerate(chips)]
        return n, c, me, sibling, chips, copy, mine, first

    def start(self, ins, outs, sems):
        *_, mine, first = self._parts(ins, outs, sems)
        for cp in mine + first:
            cp.start()

    def finish(self, ins, outs, sems):
        n, c, me, sibling, chips, copy, mine, first = self._parts(ins, outs, sems)
        passed = []
        for j, chip in enumerate(chips):
            for a in range(n):
                copy(a, 1 + j, (*chip, c), me).wait_recv()
                fwd = copy(a, 4 + j, (*chip, c), sibling)
                fwd.start()
                passed.append(fwd)
        for a in range(n):
            copy(a, 0, sibling, me).wait_recv()
            for j, chip in enumerate(chips):
                copy(a, 4 + j, (*chip, 1 - c), me).wait_recv()
        for cp in first + passed:
            cp.wait_send()
        for cp in mine:
            cp.wait()


class _GatherRelay:
    def __init__(self, arr):
        self.ins = [arr]
        self.r = arr.shape[0]
        self.out_shape = [jax.ShapeDtypeStruct((N_DEV * self.r, arr.shape[1]), arr.dtype)]
        self.scratch = [pltpu.SemaphoreType.DMA((9,)), pltpu.SemaphoreType.DMA((9,)), pltpu.SemaphoreType.DMA]

    def _parts(self, ins, outs, sems):
        send_sems, recv_sems, local_sem = sems
        x, y, c = _place()
        r, half = self.r, self.r // 2
        out = outs[0]
        me, sib, xn, yn = (x, y, c), (x, y, 1 - c), (1 - x, y, c), (x, 1 - y, c)
        dg = (1 - x, 1 - y, c)

        def rows(p, lo=0, n=r):
            return out.at[pl.ds(pl.multiple_of((4 * p[0] + 2 * p[1] + p[2]) * r + lo, 16), n), :]

        def copy(k, dev_rows, to, src=None):
            return pltpu.make_async_remote_copy(
                src_ref=dev_rows if src is None else src, dst_ref=dev_rows,
                send_sem=send_sems.at[k], recv_sem=recv_sems.at[k], device_id=to, device_id_type=MESH)

        mine = pltpu.make_async_copy(ins[0], rows(me), local_sem)
        first = [copy(0, rows(me), sib, src=ins[0]), copy(1, rows(me), xn, src=ins[0]), copy(2, rows(me), yn, src=ins[0])]
        arrive = {0: rows(sib), 1: rows(xn), 2: rows(yn), 3: rows(dg, 0, half), 4: rows(dg, half, half),
                  5: rows((1 - x, y, 1 - c)), 6: rows((x, 1 - y, 1 - c)),
                  7: rows((1 - x, 1 - y, 1 - c), 0, half), 8: rows((1 - x, 1 - y, 1 - c), half, half)}
        relay = {1: [(3, rows(xn, 0, half), yn), (5, rows(xn), sib)],
                 2: [(4, rows(yn, half, half), xn), (6, rows(yn), sib)],
                 3: [(7, rows(dg, 0, half), sib)], 4: [(8, rows(dg, half, half), sib)]}
        return copy, mine, first, arrive, relay, me

    def start(self, ins, outs, sems):
        _, mine, first, _, _, _ = self._parts(ins, outs, sems)
        for cp in [mine] + first:
            cp.start()

    def finish(self, ins, outs, sems):
        copy, mine, first, arrive, relay, me = self._parts(ins, outs, sems)
        passed = []
        for k in (1, 2, 3, 4):
            copy(k, arrive[k], me).wait_recv()
            for k2, dev_rows, to in relay[k]:
                fwd = copy(k2, dev_rows, to)
                fwd.start()
                passed.append(fwd)
        for k in (0, 5, 6, 7, 8):
            copy(k, arrive[k], me).wait_recv()
        for cp in first + passed:
            cp.wait_send()
        mine.wait()


FLIPS = [(0, 0, 1), (1, 0, 0), (0, 1, 0), (1, 1, 0), (1, 0, 1), (0, 1, 1), (1, 1, 1)]


class _Scatter:
    def __init__(self, arrs, part=0, nparts=1):
        self.ins = list(arrs)
        self.part, self.nparts = part, nparts
        n = len(arrs)
        self.out_shape = [jax.ShapeDtypeStruct((a.shape[0] // nparts, a.shape[1]), a.dtype) for a in arrs]
        self.scratch = [pltpu.SemaphoreType.DMA((n, 7)), pltpu.SemaphoreType.DMA((n, 7)), pltpu.SemaphoreType.DMA((n,))]

    def _parts(self, ins, outs, sems):
        send_sems, recv_sems, local_sems = sems
        n = len(ins)
        x, y, c = _place()
        me = 4 * x + 2 * y + c

        def flip(v, f):
            return 1 - v if f else v

        def src(a, idx):
            r = self.ins[a].shape[0] // N_DEV
            rs = r // self.nparts
            return ins[a].at[pl.ds(pl.multiple_of(idx * r + self.part * rs, 16), rs), :]

        def dst(a, idx):
            rs = self.ins[a].shape[0] // N_DEV // self.nparts
            return outs[a].at[pl.ds(pl.multiple_of(idx * rs, 16), rs), :]

        mine = [pltpu.make_async_copy(src(a, me), dst(a, me), local_sems.at[a]) for a in range(n)]
        sends, recvs = [], []
        for k, f in enumerate(FLIPS):
            peer = (flip(x, f[0]), flip(y, f[1]), flip(c, f[2]))
            pidx = 4 * peer[0] + 2 * peer[1] + peer[2]
            for a in range(n):
                sends.append(pltpu.make_async_remote_copy(
                    src_ref=src(a, pidx), dst_ref=dst(a, me),
                    send_sem=send_sems.at[a, k], recv_sem=recv_sems.at[a, k], device_id=peer, device_id_type=MESH))
                recvs.append(functools.partial(
                    pltpu.make_async_remote_copy,
                    src_ref=src(a, pidx), dst_ref=dst(a, pidx),
                    send_sem=send_sems.at[a, k], recv_sem=recv_sems.at[a, k], device_id=peer, device_id_type=MESH))
        return mine, sends, recvs

    def start(self, ins, outs, sems):
        mine, sends, _ = self._parts(ins, outs, sems)
        for cp in mine + sends:
            cp.start()

    def finish(self, ins, outs, sems):
        mine, sends, recvs = self._parts(ins, outs, sems)
        for make in recvs:
            make().wait_recv()
        for cp in sends:
            cp.wait_send()
        for cp in mine:
            cp.wait()


N_CHIP = 4


class _SiblingSwap:
    def __init__(self, arr):
        self.ins = [arr]
        self.r = arr.shape[0] // N_DEV
        self.out_shape = [jax.ShapeDtypeStruct((N_CHIP * self.r, arr.shape[1]), arr.dtype)]
        self.scratch = [pltpu.SemaphoreType.DMA((N_CHIP,)), pltpu.SemaphoreType.DMA((N_CHIP,))]

    def _copies(self, ins, outs, sems):
        send_sems, recv_sems = sems
        x, y, c = _place()
        r = self.r
        return [pltpu.make_async_remote_copy(
            src_ref=ins[0].at[pl.ds(pl.multiple_of((2 * j + 1 - c) * r, 16), r), :],
            dst_ref=outs[0].at[pl.ds(j * r, r), :],
            send_sem=send_sems.at[j], recv_sem=recv_sems.at[j], device_id=(x, y, 1 - c), device_id_type=MESH)
            for j in range(N_CHIP)]

    def start(self, ins, outs, sems):
        for cp in self._copies(ins, outs, sems):
            cp.start()

    def finish(self, ins, outs, sems):
        for cp in self._copies(ins, outs, sems):
            cp.wait()


class _ChipScatter:
    def __init__(self, arr):
        self.ins = [arr]
        self.r = arr.shape[0] // N_CHIP
        self.out_shape = [jax.ShapeDtypeStruct(arr.shape, arr.dtype)]
        self.scratch = [pltpu.SemaphoreType.DMA((3,)), pltpu.SemaphoreType.DMA((3,)), pltpu.SemaphoreType.DMA]

    def _parts(self, ins, outs, sems):
        send_sems, recv_sems, local_sem = sems
        x, y, c = _place()
        r = self.r
        my_chip = 2 * x + y

        def rows(ref, j):
            return ref.at[pl.ds(pl.multiple_of(j * r, 16), r), :]

        mine = pltpu.make_async_copy(rows(ins[0], my_chip), rows(outs[0], my_chip), local_sem)
        sends, recvs = [], []
        for k, (fx, fy) in enumerate(((1, 0), (0, 1), (1, 1))):
            px, py = (1 - x if fx else x), (1 - y if fy else y)
            peer_chip = 2 * px + py
            sends.append(pltpu.make_async_remote_copy(
                src_ref=rows(ins[0], peer_chip), dst_ref=rows(outs[0], my_chip),
                send_sem=send_sems.at[k], recv_sem=recv_sems.at[k], device_id=(px, py, c), device_id_type=MESH))
            recvs.append(functools.partial(
                pltpu.make_async_remote_copy,
                src_ref=rows(ins[0], peer_chip), dst_ref=rows(outs[0], peer_chip),
                send_sem=send_sems.at[k], recv_sem=recv_sems.at[k], device_id=(px, py, c), device_id_type=MESH))
        return mine, sends, recvs

    def start(self, ins, outs, sems):
        mine, sends, _ = self._parts(ins, outs, sems)
        for cp in [mine] + sends:
            cp.start()

    def finish(self, ins, outs, sems):
        mine, sends, recvs = self._parts(ins, outs, sems)
        for make in recvs:
            make().wait_recv()
        for cp in sends:
            cp.wait_send()
        mine.wait()


def _pair_add(partial, recv):
    r = recv.shape[0] // N_CHIP
    cols = recv.shape[1]
    tr = r // 2 if (r // 2) % 16 == 0 else r
    steps = r // tr
    core = lax.axis_index("c").astype(jnp.int32).reshape(1)

    def body(c_ref, p_ref, s_ref, o_ref):
        o_ref[...] = (p_ref[...].astype(F32) + s_ref[...].astype(F32)).astype(BF16)

    spec = pl.BlockSpec((tr, cols), lambda j, i, c_ref: (j * steps + i, 0))
    return pl.pallas_call(
        body, name="pair_add",
        grid_spec=pltpu.PrefetchScalarGridSpec(
            num_scalar_prefetch=1, grid=(N_CHIP, steps),
            in_specs=[pl.BlockSpec((tr, cols), lambda j, i, c_ref: ((2 * j + c_ref[0]) * steps + i, 0)), spec],
            out_specs=spec),
        out_shape=jax.ShapeDtypeStruct(recv.shape, BF16),
        compiler_params=_cparams("parallel", "parallel"),
    )(core, partial, recv)


class _Both:
    def __init__(self, a, b):
        self.a, self.b = a, b
        self.ins = a.ins + b.ins
        self.out_shape = a.out_shape + b.out_shape
        self.scratch = a.scratch + b.scratch

    def _split(self, ins, outs, sems):
        ni, no, ns = len(self.a.ins), len(self.a.out_shape), len(self.a.scratch)
        return (ins[:ni], outs[:no], sems[:ns]), (ins[ni:], outs[no:], sems[ns:])

    def start(self, ins, outs, sems):
        ra, rb = self._split(ins, outs, sems)
        self.a.start(*ra)
        self.b.start(*rb)

    def finish(self, ins, outs, sems):
        ra, rb = self._split(ins, outs, sems)
        self.a.finish(*ra)
        self.b.finish(*rb)


def _exchange(comm, name):
    n, m = len(comm.ins), len(comm.out_shape)

    def body(*refs):
        ins, outs, sems = refs[:n], refs[n:n + m], refs[n + m:]
        comm.start(ins, outs, sems)
        comm.finish(ins, outs, sems)

    return pl.pallas_call(
        body, name=name, out_shape=comm.out_shape, in_specs=[ANY] * n, out_specs=[ANY] * m, scratch_shapes=comm.scratch,
    )(*comm.ins)


def _call(body, *, name, grid, in_specs, out_specs, out_shape, args, scratch=(), sem="parallel", comm=None):
    if comm is None:
        outs = pl.pallas_call(
            body, name=name, grid=grid, in_specs=list(in_specs), out_specs=list(out_specs), out_shape=list(out_shape),
            scratch_shapes=list(scratch), compiler_params=_cparams(sem))(*args)
        return outs, []
    n_in, n_out, n_sc = len(in_specs), len(out_specs), len(scratch)
    n_ci, n_co = len(comm.ins), len(comm.out_shape)
    last = grid[0] - 1

    def fused(*refs):
        ins, refs = refs[:n_in], refs[n_in:]
        c_ins, refs = refs[:n_ci], refs[n_ci:]
        outs, refs = refs[:n_out], refs[n_out:]
        c_outs, refs = refs[:n_co], refs[n_co:]
        sc, c_sems = refs[:n_sc], refs[n_sc:]
        step = pl.program_id(0)

        @pl.when(step == 0)
        def _():
            comm.start(c_ins, c_outs, c_sems)

        body(*ins, *outs, *sc)

        @pl.when(step == last)
        def _():
            comm.finish(c_ins, c_outs, c_sems)

    outs = pl.pallas_call(
        fused, name=name, grid=grid, in_specs=list(in_specs) + [ANY] * n_ci, out_specs=list(out_specs) + [ANY] * n_co,
        out_shape=list(out_shape) + comm.out_shape, scratch_shapes=list(scratch) + comm.scratch,
        compiler_params=_cparams("arbitrary"))(*args, *comm.ins)
    return outs[:n_out], outs[n_out:]


def _token_specs(tm):
    k = tm // BLK
    return [pl.BlockSpec((BLK, D), functools.partial(lambda i, t: (jnp.maximum(k * i + t - 1, 0), 0), t=t)) for t in range(k)]


def _in_proj(x2d, meta, gain, w_int, b_in, tabs, comm=None):
    p = x2d.shape[0] + BLK
    tm = _row_tile(p)
    k = tm // BLK

    def body(*refs):
        x_refs = refs[:k]
        m_ref, g_ref, w_ref, b_ref, t_ref, h_ref, n1_ref, q_ref, kv_ref, ag_ref, gt_ref = refs[k:]
        i = pl.program_id(0)
        head = jnp.concatenate([jnp.zeros((PAD, D), F32), m_ref[...]], axis=0)
        first = jnp.where(i == 0, head, x_refs[0][...])
        h = jnp.concatenate([first] + [r[...] for r in x_refs[1:]], axis=0) if k > 1 else first
        h_ref[...] = h
        n = _rms(h, g_ref[...]).astype(BF16)
        n1_ref[...] = n
        c, s1, s2 = t_ref[:, 0:128], t_ref[:, 128:256], t_ref[:, 256:384]

        def mm(c0, w):
            return _dot_nt(n, w_ref[c0:c0 + w, :]) + b_ref[:, c0:c0 + w]

        for j in range(4):
            acc = mm(256 * j, 256)
            for t in range(2):
                lo = 256 * j + 128 * t
                q_ref[:, lo:lo + 128] = (_rope(acc[:, 128 * t:128 * (t + 1)], c, s1, s2) * SCALE).astype(BF16)
        acc = mm(1024, 256)
        kv_ref[:, 0:128] = _rope(acc[:, 0:128], c, s1, s2).astype(BF16)
        kv_ref[:, 128:256] = acc[:, 128:256].astype(BF16)
        for j in range(8):
            ag_ref[:, 256 * j:256 * (j + 1)] = mm(QKV_W + 256 * j, 256).astype(BF16)
        for j in range(8):
            gt_ref[:, 256 * j:256 * (j + 1)] = mm(QKV_W + 2048 + 256 * j, 256).astype(BF16)

    def row(w):
        return pl.BlockSpec((tm, w), lambda i: (i, 0))

    return _call(
        body, name="in_proj", grid=(p // tm,),
        in_specs=_token_specs(tm) + [VM, VM, VM, VM, row(384)],
        out_specs=[row(D), row(D), row(D), row(256), row(2048), row(2048)],
        out_shape=[jax.ShapeDtypeStruct((p, D), F32)] + [jax.ShapeDtypeStruct((p, w), BF16) for w in (D, D, 256, 2048, 2048)],
        args=(x2d,) * k + (meta, gain, w_int, b_in, tabs), comm=comm)


N_KEY = 2 * BLK + N_META


def _attn_setup(n, h, q_ref, km_ref, kp_ref, kc_ref):
    lo = lax.broadcasted_iota(jnp.int32, (BLK, BLK), 1) < HEAD_DIM
    lok = lax.broadcasted_iota(jnp.int32, (N_KEY, BLK), 1) < HEAD_DIM

    def dup(lanes):
        cat = jnp.concatenate([kp_ref[:, lanes], kc_ref[:, lanes], km_ref[PAD:BLK, lanes]], axis=0).astype(F32)
        rolled = pltpu.roll(cat, HEAD_DIM, 1)
        return (jnp.where(lok, cat, rolled) if h == 0 else jnp.where(lok, rolled, cat)).astype(BF16)

    k2 = dup(slice(0, 128))
    v2 = dup(slice(128, 256))
    qs = _stack_heads(q_ref, h, lo)

    kr = lax.broadcasted_iota(jnp.int32, (BLK, BLK), 0)
    tq = BLK * n + lax.broadcasted_iota(jnp.int32, (BLK, BLK), 1) - PAD
    t_p = BLK * (n - 1) + kr - PAD
    t_c = BLK * n + kr - PAD
    ok_p = jnp.logical_and(t_p >= N_META, tq - t_p < BLK)
    ok_c = jnp.logical_and(t_c >= N_META, t_c <= tq)
    ok_m = lax.broadcasted_iota(jnp.int32, (N_META, BLK), 0) <= BLK * n + lax.broadcasted_iota(jnp.int32, (N_META, BLK), 1) - PAD
    bias = jnp.concatenate([jnp.where(ok, 0.0, NEG_INF).astype(F32) for ok in (ok_p, ok_c, ok_m)], axis=0)
    return qs, k2, v2, bias, lok


def _attn_head(s, bias, sink):
    s = s + bias
    m = jnp.maximum(jnp.max(s, axis=0, keepdims=True), sink)
    e = jnp.exp(s - m)
    es = jnp.exp(sink - m)
    inv = 1.0 / (jnp.sum(e, axis=0, keepdims=True) + es)
    return e * inv, es * inv


def _stack_heads(ref, h, lo):
    pieces = []
    for jp in range(4):
        v = ref[:, BLK * (4 * h + jp):BLK * (4 * h + jp + 1)]
        zero = jnp.zeros_like(v)
        pieces += [jnp.where(lo, v, zero), jnp.where(lo, zero, v)]
    return jnp.concatenate(pieces, axis=0)


def _unstack_heads(v, jp, lo):
    return jnp.where(lo, v[256 * jp:256 * jp + 128], v[256 * jp + 128:256 * jp + 256])


def _attn_fwd(q, kv, sinks, comm=None):
    p = q.shape[0]
    nb = p // BLK

    def body(q_ref, km_ref, kp_ref, kc_ref, sink_ref, o_ref):
        n = pl.program_id(0)
        lo = lax.broadcasted_iota(jnp.int32, (BLK, BLK), 1) < HEAD_DIM
        for h in range(2):
            qs, k2, v2, bias, _ = _attn_setup(n, h, q_ref, km_ref, kp_ref, kc_ref)
            st = _dot_nt(k2, qs)
            pt = jnp.concatenate(
                [_attn_head(st[:, BLK * g:BLK * (g + 1)], bias, sink_ref[0, 8 * h + g])[0].astype(BF16) for g in range(8)],
                axis=1)
            o = _dot_tn(pt, v2)
            for jp in range(4):
                o_ref[:, BLK * (4 * h + jp):BLK * (4 * h + jp + 1)] = _unstack_heads(o, jp, lo).astype(BF16)

    return _call(
        body, name="attn_fwd", grid=(nb,),
        in_specs=[pl.BlockSpec((BLK, D), lambda i: (i, 0)),
                  pl.BlockSpec((BLK, 256), lambda i: (0, 0)),
                  pl.BlockSpec((BLK, 256), lambda i: (jnp.maximum(i - 1, 0), 0)),
                  pl.BlockSpec((BLK, 256), lambda i: (i, 0)),
                  pl.BlockSpec(memory_space=pltpu.SMEM)],
        out_specs=[pl.BlockSpec((BLK, D), lambda i: (i, 0))],
        out_shape=[jax.ShapeDtypeStruct((p, D), BF16)],
        args=(q, kv, kv, kv, sinks), comm=comm)


def _conv31_fwd(ag, w32, b, comm=None):
    p = ag.shape[0]
    nch = p // BLK

    def body(a_ref, g_ref, w_ref, b_ref, o_ref, gp):
        gp[0:32, :] = jnp.zeros((32, BLK), F32)
        for ci in range(nch):
            r0 = BLK * ci
            glu = a_ref[r0:r0 + BLK, :].astype(F32) * jax.nn.sigmoid(g_ref[r0:r0 + BLK, :].astype(F32))
            if ci == 0:
                glu = jnp.where(_rows(0, BLK) >= PAD, glu, 0.0)
            gp[32 + r0:32 + r0 + BLK, :] = glu
        for ci in range(nch):
            r0 = BLK * ci
            acc = jnp.broadcast_to(b_ref[...], (BLK, BLK))
            for j in range(CONV_K):
                acc = acc + w_ref[j:j + 1, :] * gp[r0 + j + 2:r0 + j + 2 + BLK, :]
            o_ref[r0:r0 + BLK, :] = acc

    return _call(
        body, name="conv31_fwd", grid=(D // BLK,),
        in_specs=[pl.BlockSpec((p, BLK), lambda j: (0, j)), pl.BlockSpec((p, BLK), lambda j: (0, 8 + j)),
                  pl.BlockSpec((32, BLK), lambda j: (0, j)), pl.BlockSpec((1, BLK), lambda j: (0, j))],
        out_specs=[pl.BlockSpec((p, BLK), lambda j: (0, j))],
        out_shape=[jax.ShapeDtypeStruct((p, D), F32)],
        scratch=[pltpu.VMEM((p + 32, BLK), F32)],
        args=(ag, ag, w32, b), comm=comm)


def _mixer_fwd(ao, c0, gates, h0p, wa, wc, wo, vecs):
    p = ao.shape[0]
    tm = _row_tile(p)

    def body(ao_ref, c0_ref, gt_ref, h_ref, wa_ref, wc_ref, wo_ref, v_ref,
             c1_ref, at_ref, cv_ref, mg_ref, mix_ref, h1_ref, n2_ref):
        i = pl.program_id(0)
        c1 = _lnsilu(c0_ref[...], v_ref[0:1, :], v_ref[1:2, :]).astype(BF16)
        c1_ref[...] = c1
        attn = _dot(ao_ref[...], wa_ref[...])
        conv = _dot(c1, wc_ref[...]) + v_ref[2:3, :]
        at_ref[...] = attn.astype(BF16)
        cv_ref[...] = conv.astype(BF16)
        merged = (jax.nn.sigmoid(gt_ref[:, 0:D].astype(F32)) * attn
                  + jax.nn.sigmoid(gt_ref[:, D:2 * D].astype(F32)) * conv).astype(BF16)
        mg_ref[...] = merged
        mix = _dot(merged, wo_ref[...])
        mix_ref[...] = mix
        h1 = jnp.where(_rows(i, tm) >= PAD, h_ref[...] + _rms(mix, v_ref[3:4, :]), 0.0)
        h1_ref[...] = h1
        n2_ref[...] = _rms(h1, v_ref[4:5, :]).astype(BF16)

    def row(w):
        return pl.BlockSpec((tm, w), lambda i: (i, 0))

    return pl.pallas_call(
        body, name="mixer_fwd", grid=(p // tm,),
        in_specs=[row(D), row(D), row(2 * D), row(D), VM, VM, VM, VM],
        out_specs=[row(D)] * 7,
        out_shape=[jax.ShapeDtypeStruct((p, D), t) for t in (BF16, BF16, BF16, BF16, F32, F32, BF16)],
        compiler_params=_cparams("parallel"),
    )(ao, c0, gates, h0p, wa, wc, wo, vecs)


def _mm_nt(a, w_t, name):
    p, k = a.shape
    n = w_t.shape[0]
    tm = _row_tile(p)
    ch = 512

    def body(a_ref, w_ref, o_ref):
        a_v = a_ref[...]
        for c0 in range(0, n, ch):
            o_ref[:, c0:c0 + ch] = _dot_nt(a_v, w_ref[c0:c0 + ch, :]).astype(BF16)

    return pl.pallas_call(
        body, name=name, grid=(p // tm,),
        in_specs=[pl.BlockSpec((tm, k), lambda i: (i, 0)), VM],
        out_specs=pl.BlockSpec((tm, n), lambda i: (i, 0)),
        out_shape=jax.ShapeDtypeStruct((p, n), BF16),
        compiler_params=_cparams("parallel"),
    )(a, w_t)


def _conv3(xp_ref, w_ref, r0):
    return (w_ref[0:1, :] * xp_ref[r0 + 6:r0 + 6 + BLK, :] + w_ref[1:2, :] * xp_ref[r0 + 7:r0 + 7 + BLK, :]
            + w_ref[2:3, :] * xp_ref[r0 + 8:r0 + 8 + BLK, :])


def _ffn_slab_specs(p):
    ncol = FFN // BLK
    return [pl.BlockSpec((p, BLK), lambda j: (0, j)), pl.BlockSpec((p, BLK), lambda j: (0, ncol + j)),
            pl.BlockSpec((FFN_K, BLK), lambda j: (0, j)), pl.BlockSpec((FFN_K, BLK), lambda j: (0, ncol + j)),
            pl.BlockSpec((1, BLK), lambda j: (0, j)), pl.BlockSpec((1, BLK), lambda j: (0, ncol + j))]


def _fill_shifted(dst, src_ref, nch):
    dst[0:8, :] = jnp.zeros((8, BLK), F32)
    for ci in range(nch):
        dst[8 + BLK * ci:8 + BLK * (ci + 1), :] = src_ref[BLK * ci:BLK * (ci + 1), :].astype(F32)


def _ffn_act(u0, fw, fb):
    p = u0.shape[0]
    nch = p // BLK

    def body(g_ref, v_ref, wg_ref, wv_ref, bg_ref, bv_ref, o_ref, dv_ref, dg_ref, xg, xv):
        _fill_shifted(xg, g_ref, nch)
        _fill_shifted(xv, v_ref, nch)
        for ci in range(nch):
            r0 = BLK * ci
            ug = _conv3(xg, wg_ref, r0) + bg_ref[...]
            uv = _conv3(xv, wv_ref, r0) + bv_ref[...]
            sg = jax.nn.sigmoid(ug)
            silu = ug * sg
            o_ref[r0:r0 + BLK, :] = (silu * uv).astype(BF16)
            dv_ref[r0:r0 + BLK, :] = silu.astype(BF16)
            dg_ref[r0:r0 + BLK, :] = (uv * (sg * (1.0 + ug * (1.0 - sg)))).astype(BF16)

    slab = pl.BlockSpec((p, BLK), lambda j: (0, j))
    return pl.pallas_call(
        body, name="ffn_act", grid=(FFN // BLK,),
        in_specs=_ffn_slab_specs(p),
        out_specs=[slab] * 3,
        out_shape=[jax.ShapeDtypeStruct((p, FFN), BF16)] * 3,
        scratch_shapes=[pltpu.VMEM((p + 8, BLK), F32)] * 2,
        compiler_params=_cparams("parallel"),
    )(u0, u0, fw, fw, fb, fb)


def _ffn_down_loss(act, wd, h1, tgt, gain):
    p = act.shape[0]
    tm = _row_tile(p)
    k = tm // BLK

    def body(*refs):
        a_ref, w_ref, h_ref = refs[:3]
        t_refs = refs[3:3 + k]
        g_ref, df_ref, da_ref, dy_ref, acc_ref = refs[3 + k:]
        i = pl.program_id(0)

        @pl.when(i == 0)
        def _():
            acc_ref[...] = jnp.zeros_like(acc_ref)

        ffn = _dot(a_ref[...], w_ref[...])
        t = jnp.concatenate([t_ref[...] for t_ref in t_refs], axis=0) if k > 1 else t_refs[0][...]
        diff = jnp.where(_rows(i, tm) >= BLK, h_ref[...] + _rms(ffn, g_ref[...]) - t, 0.0)
        dy = diff * (1.0 / D)
        dffn, dg = _rms_bwd(ffn, g_ref[...], dy)
        acc_ref[0:1, :] += dg
        acc_ref[1:2, :] += jnp.sum(diff * diff, axis=0, keepdims=True) * (0.5 / D)
        dy_ref[...] = dy
        dfb = dffn.astype(BF16)
        df_ref[...] = dfb
        for c0 in range(0, FFN, 256):
            da_ref[:, c0:c0 + 256] = _dot_nt(dfb, w_ref[c0:c0 + 256, :]).astype(BF16)

    def row(w):
        return pl.BlockSpec((tm, w), lambda i: (i, 0))

    return pl.pallas_call(
        body, name="ffn_down_loss", grid=(p // tm,),
        in_specs=[row(FFN), VM, row(D)] + _token_specs(tm) + [VM],
        out_specs=[row(D), row(FFN), row(D), pl.BlockSpec((8, D), lambda i: (0, 0))],
        out_shape=[jax.ShapeDtypeStruct((p, D), BF16), jax.ShapeDtypeStruct((p, FFN), BF16),
                   jax.ShapeDtypeStruct((p, D), F32), jax.ShapeDtypeStruct((8, D), F32)],
        compiler_params=_cparams("arbitrary"),
    )(act, wd, h1, *([tgt] * k), gain)


def _mm_tn(pieces, b, name, col_sums=False, comm=None):
    p, n = b.shape
    tk = 256
    nblk = [a.shape[1] // tk for a in pieces]
    offs = [sum(nblk[:q]) for q in range(len(pieces))]
    total = sum(nblk)
    npc = len(pieces)

    def body(*refs):
        a_refs, b_ref, o_ref = refs[:npc], refs[npc], refs[npc + 1]
        i = pl.program_id(0)
        for q, a_ref in enumerate(a_refs):
            @pl.when(jnp.logical_and(i >= offs[q], i < offs[q] + nblk[q]))
            def _(a_ref=a_ref):
                a_v = a_ref[...]
                o_ref[...] = _dot_tn(a_v, b_ref[...]).astype(BF16)
                if col_sums:
                    refs[npc + 2][...] = jnp.sum(a_v.astype(F32), axis=0, keepdims=True)

    def a_spec(q):
        return pl.BlockSpec((p, tk), lambda i: (0, jnp.clip(i - offs[q], 0, nblk[q] - 1)))

    out_specs = [pl.BlockSpec((tk, n), lambda i: (i, 0))]
    out_shape = [jax.ShapeDtypeStruct((total * tk, n), BF16)]
    if col_sums:
        out_specs.append(pl.BlockSpec((1, tk), lambda i: (0, i)))
        out_shape.append(jax.ShapeDtypeStruct((1, total * tk), F32))
    res, sent = _call(
        body, name=name, grid=(total,),
        in_specs=[a_spec(q) for q in range(npc)] + [VM],
        out_specs=out_specs, out_shape=out_shape, args=(*pieces, b), comm=comm)
    res = res if col_sums else res[0]
    return res if comm is None else (res, sent)


def _ffn_act_bwd(u0, dact, dact_dg, dact_dv, fw, act, dffn, comm=None):
    p = u0.shape[0]
    nch = p // BLK
    ncol = FFN // BLK

    def body(g_ref, v_ref, wg_ref, wv_ref, da_ref, lg_ref, lv_ref, act_ref, df_ref,
             dg_ref, dv_ref, gwg_ref, gwv_ref, gbg_ref, gbv_ref, gwd_ref, xg, xv, eg, ev):
        gwd_ref[...] = _dot_tn(act_ref[...], df_ref[...]).astype(BF16)
        _fill_shifted(xg, g_ref, nch)
        _fill_shifted(xv, v_ref, nch)
        eg[p:p + 8, :] = jnp.zeros((8, BLK), F32)
        ev[p:p + 8, :] = jnp.zeros((8, BLK), F32)
        for ci in range(nch):
            r0 = BLK * ci
            d = da_ref[r0:r0 + BLK, :].astype(F32)
            eg[r0:r0 + BLK, :] = d * lg_ref[r0:r0 + BLK, :].astype(F32)
            ev[r0:r0 + BLK, :] = d * lv_ref[r0:r0 + BLK, :].astype(F32)
        for e_s, x_s, w_ref, d_ref, gw_ref, gb_ref in ((eg, xg, wg_ref, dg_ref, gwg_ref, gbg_ref),
                                                      (ev, xv, wv_ref, dv_ref, gwv_ref, gbv_ref)):
            sums = [jnp.zeros((BLK, BLK), F32) for _ in range(FFN_K + 1)]
            for ci in range(nch):
                r0 = BLK * ci
                e0 = e_s[r0:r0 + BLK, :]
                du = (w_ref[2:3, :] * e0 + w_ref[1:2, :] * e_s[r0 + 1:r0 + 1 + BLK, :]
                      + w_ref[0:1, :] * e_s[r0 + 2:r0 + 2 + BLK, :])
                if ci == 0:
                    du = jnp.where(_rows(0, BLK) >= PAD, du, 0.0)
                d_ref[r0:r0 + BLK, :] = du.astype(BF16)
                for j in range(FFN_K):
                    sums[j] = sums[j] + e0 * x_s[r0 + 6 + j:r0 + 6 + j + BLK, :]
                sums[FFN_K] = sums[FFN_K] + e0
            for j in range(FFN_K):
                gw_ref[j:j + 1, :] = jnp.sum(sums[j], axis=0, keepdims=True)
            gb_ref[...] = jnp.sum(sums[FFN_K], axis=0, keepdims=True)

    slab = pl.BlockSpec((p, BLK), lambda j: (0, j))
    wspec = pl.BlockSpec((FFN_K, BLK), lambda j: (0, j))
    bspec = pl.BlockSpec((1, BLK), lambda j: (0, j))
    return _call(
        body, name="ffn_act_bwd", grid=(ncol,),
        in_specs=_ffn_slab_specs(p)[:4] + [slab] * 4 + [VM],
        out_specs=[slab, slab, wspec, wspec, bspec, bspec, pl.BlockSpec((BLK, D), lambda j: (j, 0))],
        out_shape=[jax.ShapeDtypeStruct((p, FFN), BF16)] * 2 + [jax.ShapeDtypeStruct((FFN_K, FFN), F32)] * 2
        + [jax.ShapeDtypeStruct((1, FFN), F32)] * 2 + [jax.ShapeDtypeStruct((FFN, D), BF16)],
        scratch=[pltpu.VMEM((p + 8, BLK), F32)] * 4,
        args=(u0, u0, fw, fw, dact, dact_dg, dact_dv, act, dffn), comm=comm)


def _ffn_in_bwd(dug, duv, w_upt, h1, dy, gain, comm=None):
    p = h1.shape[0]
    tm = _row_tile(p)

    def body(dg_ref, dv_ref, w_ref, h_ref, dy_ref, g_ref, o_ref, acc_ref):
        i = pl.program_id(0)

        @pl.when(i == 0)
        def _():
            acc_ref[...] = jnp.zeros_like(acc_ref)

        dn = _dot(dg_ref[...], w_ref[0:FFN, :]) + _dot(dv_ref[...], w_ref[FFN:2 * FFN, :])
        dh, dg = _rms_bwd(h_ref[...], g_ref[...], dn)
        o_ref[...] = dy_ref[...] + dh
        acc_ref[0:1, :] += dg

    def row(w):
        return pl.BlockSpec((tm, w), lambda i: (i, 0))

    return _call(
        body, name="ffn_in_bwd", grid=(p // tm,),
        in_specs=[row(FFN), row(FFN), VM, row(D), row(D), VM],
        out_specs=[row(D), pl.BlockSpec((8, D), lambda i: (0, 0))],
        out_shape=[jax.ShapeDtypeStruct((p, D), F32), jax.ShapeDtypeStruct((8, D), F32)],
        sem="arbitrary", args=(dug, duv, w_upt, h1, dy, gain), comm=comm)


def _mixer_bwd(dh1, mix, attn, conv, gates, c0, wa, wc, wo, vecs, comm=None):
    p = dh1.shape[0]
    tm = _row_tile(p)

    def body(dh_ref, mix_ref, at_ref, cv_ref, gt_ref, c0_ref, wa_ref, wc_ref, wo_ref, v_ref,
             dmix_ref, dat_ref, dcv_ref, dgt_ref, dao_ref, dc0_ref, acc_ref):
        i = pl.program_id(0)

        @pl.when(i == 0)
        def _():
            acc_ref[...] = jnp.zeros_like(acc_ref)

        dmix, dgp = _rms_bwd(mix_ref[...], v_ref[3:4, :], dh_ref[...])
        dmix = dmix.astype(BF16)
        dmix_ref[...] = dmix
        dmg = _dot_nt(dmix, wo_ref[...])
        sa = jax.nn.sigmoid(gt_ref[:, 0:D].astype(F32))
        sc = jax.nn.sigmoid(gt_ref[:, D:2 * D].astype(F32))
        dat = dmg * sa
        dcv = dmg * sc
        dgt_ref[:, 0:D] = (dmg * at_ref[...].astype(F32) * sa * (1.0 - sa)).astype(BF16)
        dgt_ref[:, D:2 * D] = (dmg * cv_ref[...].astype(F32) * sc * (1.0 - sc)).astype(BF16)
        datb = dat.astype(BF16)
        dcvb = dcv.astype(BF16)
        dat_ref[...] = datb
        dcv_ref[...] = dcvb
        dao_ref[...] = _dot_nt(datb, wa_ref[...]).astype(BF16)
        dc1 = _dot_nt(dcvb, wc_ref[...])
        dc0, dlg, dlb = _lnsilu_bwd(c0_ref[...], v_ref[0:1, :], v_ref[1:2, :], dc1)
        dc0_ref[...] = dc0
        acc_ref[0:1, :] += dgp
        acc_ref[1:2, :] += jnp.sum(dcv, axis=0, keepdims=True)
        acc_ref[2:3, :] += dlg
        acc_ref[3:4, :] += dlb

    def row(w):
        return pl.BlockSpec((tm, w), lambda i: (i, 0))

    return _call(
        body, name="mixer_bwd", grid=(p // tm,),
        in_specs=[row(D), row(D), row(D), row(D), row(2 * D), row(D), VM, VM, VM, VM],
        out_specs=[row(D), row(D), row(D), row(2 * D), row(D), row(D), pl.BlockSpec((8, D), lambda i: (0, 0))],
        out_shape=[jax.ShapeDtypeStruct((p, D), BF16)] * 3 + [jax.ShapeDtypeStruct((p, 2 * D), BF16),
                                                             jax.ShapeDtypeStruct((p, D), BF16),
                                                             jax.ShapeDtypeStruct((p, D), F32),
                                                             jax.ShapeDtypeStruct((8, D), F32)],
        sem="arbitrary", args=(dh1, mix, attn, conv, gates, c0, wa, wc, wo, vecs), comm=comm)


def _conv31_bwd(ag, dc0, w32, tn_pairs, comm=None):
    p = ag.shape[0]
    nch = p // BLK
    npair = len(tn_pairs)

    def body(*refs):
        a_ref, g_ref, dc_ref, w_ref = refs[:4]
        tn_a, tn_b = refs[4:4 + npair], refs[4 + npair:4 + 2 * npair]
        da_ref, dg_ref, gw_ref, gb_ref = refs[4 + 2 * npair:8 + 2 * npair]
        tn_o = refs[8 + 2 * npair:8 + 3 * npair]
        gp, dp = refs[8 + 3 * npair:]
        for ta, tb, to in zip(tn_a, tn_b, tn_o):
            to[...] = _dot_tn(ta[...], tb[...]).astype(BF16)
        gp[0:32, :] = jnp.zeros((32, BLK), F32)
        dp[p:p + 32, :] = jnp.zeros((32, BLK), F32)
        bsum = jnp.zeros((BLK, BLK), F32)
        for ci in range(nch):
            r0 = BLK * ci
            glu = a_ref[r0:r0 + BLK, :].astype(F32) * jax.nn.sigmoid(g_ref[r0:r0 + BLK, :].astype(F32))
            if ci == 0:
                glu = jnp.where(_rows(0, BLK) >= PAD, glu, 0.0)
            gp[32 + r0:32 + r0 + BLK, :] = glu
            d = dc_ref[r0:r0 + BLK, :]
            dp[r0:r0 + BLK, :] = d
            bsum = bsum + d
        gb_ref[...] = jnp.sum(bsum, axis=0, keepdims=True)
        for ci in range(nch):
            r0 = BLK * ci
            acc = jnp.zeros((BLK, BLK), F32)
            for j in range(CONV_K):
                acc = acc + w_ref[j:j + 1, :] * dp[r0 + 30 - j:r0 + 30 - j + BLK, :]
            if ci == 0:
                acc = jnp.where(_rows(0, BLK) >= PAD, acc, 0.0)
            a = a_ref[r0:r0 + BLK, :].astype(F32)
            sg = jax.nn.sigmoid(g_ref[r0:r0 + BLK, :].astype(F32))
            da_ref[r0:r0 + BLK, :] = (acc * sg).astype(BF16)
            dg_ref[r0:r0 + BLK, :] = (acc * a * sg * (1.0 - sg)).astype(BF16)
        sub = BLK // 2
        accs = [jnp.zeros((8, BLK), F32) for _ in range(CONV_K)]
        for r0 in range(0, p, sub):
            d = dp[r0:r0 + sub, :]
            for j in range(CONV_K):
                prod = d * gp[r0 + j + 2:r0 + j + 2 + sub, :]
                accs[j] = accs[j] + jnp.sum(prod.reshape(sub // 8, 8, BLK), axis=0)
        for j in range(CONV_K):
            gw_ref[j:j + 1, :] = jnp.sum(accs[j], axis=0, keepdims=True)
        gw_ref[CONV_K:32, :] = jnp.zeros((32 - CONV_K, BLK), F32)

    slab = pl.BlockSpec((p, BLK), lambda j: (0, j))
    return _call(
        body, name="conv31_bwd", grid=(D // BLK,),
        in_specs=[slab, pl.BlockSpec((p, BLK), lambda j: (0, 8 + j)), slab, pl.BlockSpec((32, BLK), lambda j: (0, j))]
        + [slab] * npair + [VM] * npair,
        out_specs=[slab, slab, pl.BlockSpec((32, BLK), lambda j: (0, j)), pl.BlockSpec((1, BLK), lambda j: (0, j))]
        + [pl.BlockSpec((BLK, D), lambda j: (j, 0))] * npair,
        out_shape=[jax.ShapeDtypeStruct((p, D), BF16)] * 2 + [jax.ShapeDtypeStruct((32, D), F32),
                                                             jax.ShapeDtypeStruct((1, D), F32)]
        + [jax.ShapeDtypeStruct((D, D), BF16)] * npair,
        scratch=[pltpu.VMEM((p + 32, BLK), F32)] * 2,
        args=(ag, ag, dc0, w32, *[a for a, _ in tn_pairs], *[b for _, b in tn_pairs]), comm=comm)


def _attn_bwd(q, kv, dao, sinks, tabs, comm=None):
    p = q.shape[0]
    nb = p // BLK

    def body(q_ref, km_ref, kp_ref, kc_ref, do_ref, sink_ref, t_ref, dqkv_ref, dsink_ref, carry, macc):
        i = pl.program_id(0)
        n = nb - 1 - i

        @pl.when(i == 0)
        def _():
            carry[...] = jnp.zeros_like(carry)
            macc[...] = jnp.zeros_like(macc)
            dsink_ref[...] = jnp.zeros_like(dsink_ref)

        lo = lax.broadcasted_iota(jnp.int32, (BLK, BLK), 1) < HEAD_DIM
        lane8 = lax.broadcasted_iota(jnp.int32, (8, BLK), 1)
        c, s1, s2 = t_ref[:, 0:128], -t_ref[:, 128:256], -t_ref[:, 256:384]
        dk = jnp.zeros((N_KEY, BLK), F32)
        dv = jnp.zeros((N_KEY, BLK), F32)
        for h in range(2):
            qs, k2, v2, bias, lok = _attn_setup(n, h, q_ref, km_ref, kp_ref, kc_ref)
            dos = _stack_heads(do_ref, h, lo)
            st = _dot_nt(k2, qs)
            dpt = _dot_nt(v2, dos)
            p_parts, ds_parts = [], []
            for g in range(8):
                cols = slice(BLK * g, BLK * (g + 1))
                pn, ps = _attn_head(st[:, cols], bias, sink_ref[0, 8 * h + g])
                dp = dpt[:, cols]
                delta = jnp.sum(pn * dp, axis=0, keepdims=True)
                ds_parts.append((pn * (dp - delta)).astype(BF16))
                p_parts.append(pn.astype(BF16))
                dsk = -jnp.sum(ps * delta, axis=1, keepdims=True)
                dsink_ref[...] += jnp.where(lane8 == 8 * h + g, dsk, 0.0)
            dst = jnp.concatenate(ds_parts, axis=1)
            pt = jnp.concatenate(p_parts, axis=1)
            dq = _dot_tn(dst, k2)
            for jp in range(4):
                lo_c = BLK * (4 * h + jp)
                dqkv_ref[:, lo_c:lo_c + BLK] = (_rope(_unstack_heads(dq, jp, lo), c, s1, s2) * SCALE).astype(BF16)
            dk2 = _dot(dst, qs)
            dv2 = _dot(pt, dos)
            dk2 = dk2 + pltpu.roll(dk2, HEAD_DIM, 1)
            dv2 = dv2 + pltpu.roll(dv2, HEAD_DIM, 1)
            own = lok if h == 0 else jnp.logical_not(lok)
            dk = jnp.where(own, dk2, dk)
            dv = jnp.where(own, dv2, dv)
        macc[:, 0:BLK] += dk[2 * BLK:N_KEY]
        macc[:, BLK:2 * BLK] += dv[2 * BLK:N_KEY]
        last = (n == 0).astype(F32)
        zpad = jnp.zeros((PAD, BLK), F32)
        dk_c = dk[BLK:2 * BLK] + carry[:, 0:BLK] + last * jnp.concatenate([zpad, macc[:, 0:BLK]], axis=0)
        dv_c = dv[BLK:2 * BLK] + carry[:, BLK:2 * BLK] + last * jnp.concatenate([zpad, macc[:, BLK:2 * BLK]], axis=0)
        carry[:, 0:BLK] = dk[0:BLK]
        carry[:, BLK:2 * BLK] = dv[0:BLK]
        dqkv_ref[:, D:D + BLK] = _rope(dk_c, c, s1, s2).astype(BF16)
        dqkv_ref[:, D + BLK:D + 2 * BLK] = dv_c.astype(BF16)

    def rev(w):
        return pl.BlockSpec((BLK, w), lambda i: (nb - 1 - i, 0))

    return _call(
        body, name="attn_bwd", grid=(nb,),
        in_specs=[rev(D),
                  pl.BlockSpec((BLK, 256), lambda i: (0, 0)),
                  pl.BlockSpec((BLK, 256), lambda i: (jnp.maximum(nb - 2 - i, 0), 0)),
                  rev(256), rev(D),
                  pl.BlockSpec(memory_space=pltpu.SMEM), rev(384)],
        out_specs=[rev(QKV_W), pl.BlockSpec((8, BLK), lambda i: (0, 0))],
        out_shape=[jax.ShapeDtypeStruct((p, QKV_W), BF16), jax.ShapeDtypeStruct((8, BLK), F32)],
        scratch=[pltpu.VMEM((BLK, 256), F32), pltpu.VMEM((N_META, 256), F32)], sem="arbitrary",
        args=(q, kv, kv, kv, dao, sinks, tabs), comm=comm)


def _in_bwd(dqkv, da, dg, dgt, w_int, h0p, dh1, gain, comm=None):
    p = h0p.shape[0]
    tm = _row_tile(p)
    nt = p // tm
    first_rows = tm - BLK

    def body(dq_ref, da_ref, dg_ref, dt_ref, w_ref, h_ref, dh_ref, g_ref, gx_ref, dm_ref, acc_ref, buf, sems):
        i = pl.program_id(0)
        slot = i % 2

        @pl.when(i == 0)
        def _():
            acc_ref[...] = jnp.zeros_like(acc_ref)

        dn = (_dot(dq_ref[...], w_ref[0:QKV_W, :]) + _dot(da_ref[...], w_ref[QKV_W:QKV_W + D, :])
              + _dot(dg_ref[...], w_ref[QKV_W + D:QKV_W + 2 * D, :]) + _dot(dt_ref[...], w_ref[QKV_W + 2 * D:IN_W, :]))
        dh, dgain = _rms_bwd(h_ref[...], g_ref[...], dn)
        dh0 = dh_ref[...] + dh
        acc_ref[0:1, :] += dgain
        buf[slot] = dh0

        @pl.when(i == 0)
        def _():
            dm_ref[...] = dh0[PAD:BLK]

        def first_copy():
            return pltpu.make_async_copy(buf.at[0, pl.ds(BLK, first_rows), :], gx_ref.at[pl.ds(0, first_rows), :], sems.at[0])

        def tile_copy(j, s):
            return pltpu.make_async_copy(buf.at[s], gx_ref.at[pl.ds(pl.multiple_of(j * tm - BLK, BLK), tm), :], sems.at[s])

        if first_rows:
            @pl.when(i == 1)
            def _():
                first_copy().wait()

        @pl.when(i >= 2)
        def _():
            tile_copy(i - 1, 1 - slot).wait()

        if first_rows:
            @pl.when(i == 0)
            def _():
                first_copy().start()

        @pl.when(i > 0)
        def _():
            tile_copy(i, slot).start()

        @pl.when(i == nt - 1)
        def _():
            tile_copy(i, slot).wait()

    def row(w):
        return pl.BlockSpec((tm, w), lambda i: (i, 0))

    return _call(
        body, name="in_bwd", grid=(nt,),
        in_specs=[row(QKV_W), row(D), row(D), row(2 * D), VM, row(D), row(D), VM],
        out_specs=[ANY, pl.BlockSpec((N_META, D), lambda i: (0, 0)), pl.BlockSpec((8, D), lambda i: (0, 0))],
        out_shape=[jax.ShapeDtypeStruct((p - BLK, D), F32), jax.ShapeDtypeStruct((N_META, D), F32),
                   jax.ShapeDtypeStruct((8, D), F32)],
        scratch=[pltpu.VMEM((2, tm, D), F32), pltpu.SemaphoreType.DMA((2,))],
        sem="arbitrary", args=(dqkv, da, dg, dgt, w_int, h0p, dh1, gain), comm=comm)


def _sum_slots(slots, name):
    r = slots.shape[0] // N_DEV
    cols = slots.shape[1]
    tr = r if r <= 352 else (r // 2 if (r // 2) % 16 == 0 else r // 3)
    steps = r // tr

    def body(*refs):
        acc = refs[0][...].astype(F32)
        for s in range(1, N_DEV):
            acc = acc + refs[s][...].astype(F32)
        refs[N_DEV][...] = acc

    return pl.pallas_call(
        body, name=name, grid=(steps,),
        in_specs=[pl.BlockSpec((tr, cols), functools.partial(lambda i, s: (s * steps + i, 0), s=s)) for s in range(N_DEV)],
        out_specs=pl.BlockSpec((tr, cols), lambda i: (i, 0)),
        out_shape=jax.ShapeDtypeStruct((r, cols), F32),
        compiler_params=_cparams("parallel"),
    )(*([slots] * N_DEV))


def _adamw_math(w, g, m, v):
    m_n = ADAM_B1 * m + (1.0 - ADAM_B1) * g
    v_n = ADAM_B2 * v + (1.0 - ADAM_B2) * jnp.square(g)
    m_hat = m_n / (1.0 - ADAM_B1 ** ADAM_STEP)
    v_hat = v_n / (1.0 - ADAM_B2 ** ADAM_STEP)
    return -ADAM_LR * (m_hat / (jnp.sqrt(v_hat) + ADAM_EPS) + ADAM_WD * w), m_n, v_n


def _sum_adamw(parts, w, m, v, name, nslots=N_DEV):
    r, cols = w.shape
    rs = r // len(parts)
    tr = rs if rs <= 352 else (rs // 2 if (rs // 2) % 16 == 0 else rs // 3)
    steps = rs // tr

    def body(*refs):
        w_ref, m_ref, v_ref, g_ref, d_ref, nm_ref, nv_ref = refs[nslots * len(parts):]
        i = pl.program_id(0)
        for q in range(len(parts)):
            @pl.when(i // steps == q)
            def _(q=q):
                g = refs[nslots * q][...].astype(F32)
                for s in range(1, nslots):
                    g = g + refs[nslots * q + s][...].astype(F32)
                g_ref[...] = g
                d_ref[...], nm_ref[...], nv_ref[...] = _adamw_math(w_ref[...], g, m_ref[...], v_ref[...])

    def slot_spec(q, s):
        return pl.BlockSpec((tr, cols), lambda i: (s * steps + jnp.clip(i - q * steps, 0, steps - 1), 0))

    spec = pl.BlockSpec((tr, cols), lambda i: (i, 0))
    return pl.pallas_call(
        body, name=name, grid=(steps * len(parts),),
        in_specs=[slot_spec(q, s) for q in range(len(parts)) for s in range(nslots)] + [spec] * 3,
        out_specs=[spec] * 4, out_shape=[jax.ShapeDtypeStruct((r, cols), F32)] * 4,
        compiler_params=_cparams("parallel"),
    )(*[a for a in parts for _ in range(nslots)], w, m, v)


def _adamw(w, g, m, v, name):
    r, cols = w.shape
    tr = 256 if r % 256 == 0 else r

    def body(w_ref, g_ref, m_ref, v_ref, d_ref, nm_ref, nv_ref):
        d_ref[...], nm_ref[...], nv_ref[...] = _adamw_math(w_ref[...], g_ref[...], m_ref[...], v_ref[...])

    spec = pl.BlockSpec((tr, cols), lambda i: (i, 0))
    return pl.pallas_call(
        body, name=name, grid=(r // tr,),
        in_specs=[spec] * 4, out_specs=[spec] * 3,
        out_shape=[jax.ShapeDtypeStruct((r, cols), F32)] * 3,
        compiler_params=_cparams("parallel"),
    )(w, g, m, v)


def _rope_tables(p):
    half = ROT_DIM // 2
    inv_freq = ROPE_THETA ** (-jnp.arange(half, dtype=F32) * 2.0 / ROT_DIM)
    pos = (jnp.arange(p) - PAD).astype(F32)
    ang = pos[:, None] * inv_freq[None, :]
    lane = jnp.arange(BLK)
    seg = (lane % HEAD_DIM) // half
    cos = jnp.cos(ang)[:, lane % half]
    sin = jnp.sin(ang)[:, lane % half]
    c = jnp.where(seg[None, :] < 2, cos, 1.0)
    s1 = jnp.where(seg[None, :] == 0, -sin, 0.0)
    s2 = jnp.where(seg[None, :] == 1, sin, 0.0)
    return jnp.concatenate([c, s1, s2], axis=1).astype(F32)


def _flat_pack(parts, rows):
    flat = jnp.concatenate([a.reshape(-1).astype(F32) for a in parts])
    return jnp.pad(flat, (0, rows * D - flat.shape[0])).reshape(rows, D)


def _flat_unpack(pack, shapes):
    flat = pack.reshape(-1)
    out, off = [], 0
    for s in shapes:
        size = 1
        for e in s:
            size *= e
        out.append(flat[off:off + size].reshape(s))
        off += size
    return out


def kernel(x, meta_tokens, norm_pre_mix, norm_post_mix, w_in, b_in, attn_sinks, w_attn_proj, conv_dw_w, conv_dw_b, conv_ln_g, conv_ln_b, w_conv_proj, b_conv_proj, w_out, norm_pre_ffn, norm_post_ffn, w_up, ffn_dw_w, ffn_dw_b, w_down, loss_target, m_meta_tokens, m_norm_pre_mix, m_norm_post_mix, m_w_in, m_b_in, m_attn_sinks, m_w_attn_proj, m_conv_dw_w, m_conv_dw_b, m_conv_ln_g, m_conv_ln_b, m_w_conv_proj, m_b_conv_proj, m_w_out, m_norm_pre_ffn, m_norm_post_ffn, m_w_up, m_ffn_dw_w, m_ffn_dw_b, m_w_down, v_meta_tokens, v_norm_pre_mix, v_norm_post_mix, v_w_in, v_b_in, v_attn_sinks, v_w_attn_proj, v_conv_dw_w, v_conv_dw_b, v_conv_ln_g, v_conv_ln_b, v_w_conv_proj, v_b_conv_proj, v_w_out, v_norm_pre_ffn, v_norm_post_ffn, v_w_up, v_ffn_dw_w, v_ffn_dw_b, v_w_down):
    seq = x.shape[1]
    p = seq + BLK
    me = 4 * lax.axis_index("x") + 2 * lax.axis_index("y") + lax.axis_index("c")
    in_cols = w_in.shape[2]
    up_cols = w_up.shape[2]

    small = jnp.zeros((56, up_cols), F32)
    small = small.at[0:N_META, 0:BLK].set(meta_tokens)
    small = small.at[16:16 + CONV_K, 0:BLK].set(conv_dw_w[0])
    small = small.at[48:48 + FFN_K, :].set(ffn_dw_w[0])
    w_int, small_all = _exchange(_Both(_GatherRelay(w_in[0].T.astype(BF16)), _Gather([small])), "gather_w_in")
    small_all = small_all.reshape(N_DEV, 56, up_cols)
    meta_full = small_all[:, 0:N_META, 0:BLK].transpose(1, 0, 2).reshape(N_META, D)
    cdw = small_all[:, 16:16 + CONV_K, 0:BLK].transpose(1, 0, 2).reshape(CONV_K, D)
    cdw32 = jnp.pad(cdw, ((0, 32 - CONV_K), (0, 0)))
    fdw = small_all[:, 48:48 + FFN_K, :].transpose(1, 0, 2).reshape(FFN_K, 2 * FFN)

    tabs = _rope_tables(p)
    vecs = jnp.concatenate([conv_ln_g, conv_ln_b, b_conv_proj, norm_post_mix, norm_pre_ffn, jnp.zeros((3, D), F32)], axis=0)

    (h0p, n1, q, kv, ag, gates), (wa, wc, wo) = _in_proj(
        x[0], meta_full, norm_pre_mix, w_int, b_in, tabs,
        comm=_Gather([w_attn_proj[0].astype(BF16), w_conv_proj[0].astype(BF16), w_out[0].astype(BF16)]))
    (ao,), (w_upt,) = _attn_fwd(q, kv, attn_sinks, comm=_Gather([w_up[0].T.astype(BF16)]))
    (c0,), (wd,) = _conv31_fwd(ag, cdw32, conv_dw_b, comm=_Gather([w_down[0].astype(BF16)]))
    c1, attn, conv, merged, mix, h1, n2 = _mixer_fwd(ao, c0, gates, h0p, wa, wc, wo, vecs)
    u0 = _mm_nt(n2, w_upt, "ffn_up")
    act, dact_dv, dact_dg = _ffn_act(u0, fdw, ffn_dw_b)
    dffn, dact, dy, acc_f = _ffn_down_loss(act, wd, h1, loss_target[0], norm_post_ffn)

    (dug, duv, gfw_g, gfw_v, gfb_g, gfb_v, g_wd), _ = _ffn_act_bwd(u0, dact, dact_dg, dact_dv, fdw, act, dffn)
    g_wupt = _mm_tn([dug, duv], n2, "grad_w_up")
    (dh1, acc_u), (s_wup0,) = _ffn_in_bwd(dug, duv, w_upt, h1, dy, norm_pre_ffn, comm=_Scatter([g_wupt], 0, 4))
    (dmix, dat, dcv, dgt, dao, dc0, acc_m), (s_wup1,) = _mixer_bwd(
        dh1, mix, attn, conv, gates, c0, wa, wc, wo, vecs, comm=_Scatter([g_wupt], 1, 4))
    (da, dg, g_cdw, g_cdb, g_wo, g_wa, g_wc), (s_wup2, s_wup3, s_wd) = _conv31_bwd(
        ag, dc0, cdw32, [(merged, dmix), (ao, dat), (c1, dcv)],
        comm=_Both(_Both(_Scatter([g_wupt], 2, 4), _Scatter([g_wupt], 3, 4)), _Scatter([g_wd])))
    (dqkv, dsink), (s_wa, s_wc, s_wo) = _attn_bwd(q, kv, dao, attn_sinks, tabs, comm=_Scatter([g_wa, g_wc, g_wo]))
    loss_row = jnp.sum(acc_f[1:2, :], axis=1, keepdims=True)
    early = [loss_row, acc_m[0:1], dsink[0:1, 0:16], g_cdw[0:CONV_K], g_cdb,
             acc_m[2:3], acc_m[3:4], acc_m[1:2], acc_u[0:1], acc_f[0:1],
             jnp.concatenate([gfw_g, gfw_v], axis=1), jnp.concatenate([gfb_g, gfb_v], axis=1)]
    (g_wint, g_bin), (gathered_early,) = _mm_tn([dqkv, da, dg, dgt], n1, "grad_w_in", col_sums=True,
                                                comm=_Gather([_flat_pack(early, 64)]))
    (from_sibling,) = _exchange(_SiblingSwap(g_wint), "swap_w_in")
    (grad_x2d, dmeta, acc_i), (s_win,) = _in_bwd(dqkv, da, dg, dgt, w_int, h0p, dh1, norm_pre_mix,
                                                 comm=_ChipScatter(_pair_add(g_wint, from_sibling)))

    big = []
    for nm, parts, nslots, w, m, v, tr in (
            ("w_in", [s_win], N_CHIP, w_in, m_w_in, v_w_in, True), ("w_up", [s_wup0, s_wup1, s_wup2, s_wup3], N_DEV, w_up, m_w_up, v_w_up, True),
            ("w_attn_proj", [s_wa], N_DEV, w_attn_proj, m_w_attn_proj, v_w_attn_proj, False),
            ("w_conv_proj", [s_wc], N_DEV, w_conv_proj, m_w_conv_proj, v_w_conv_proj, False),
            ("w_out", [s_wo], N_DEV, w_out, m_w_out, v_w_out, False),
            ("w_down", [s_wd], N_DEV, w_down, m_w_down, v_w_down, False)):
        ins = [a[0].T if tr else a[0] for a in (w, m, v)]
        big.append(tuple((o.T if tr else o)[None] for o in _sum_adamw(parts, *ins, "update_" + nm, nslots)))

    late = [dmeta, acc_i[0:1], g_bin]
    (gathered_late,) = _exchange(_Gather([_flat_pack(late, 24)]), "gather_small_grads")
    g_meta, g_npm, g_bi = _flat_unpack(_sum_slots(gathered_late, "sum_late_grads"), [a.shape for a in late])
    tot = _flat_unpack(_sum_slots(gathered_early, "sum_small_grads"), [a.shape for a in early])
    (loss, g_nqm, g_sk, g_cw, g_cb, g_lg, g_lb, g_bc, g_npf, g_nqf, g_fw, g_fb) = tot
    loss = loss.reshape(())
    g_meta = lax.dynamic_slice_in_dim(g_meta, me * BLK, BLK, axis=1)
    g_cw = lax.dynamic_slice_in_dim(g_cw, me * BLK, BLK, axis=1)[None]
    g_fw = lax.dynamic_slice_in_dim(g_fw, me * up_cols, up_cols, axis=1)[None]

    sm_w = [meta_tokens, norm_pre_mix, norm_post_mix, b_in, attn_sinks, conv_dw_w, conv_dw_b, conv_ln_g, conv_ln_b,
            b_conv_proj, norm_pre_ffn, norm_post_ffn, ffn_dw_w, ffn_dw_b]
    sm_g = [g_meta, g_npm, g_nqm, g_bi, g_sk, g_cw, g_cb, g_lg, g_lb, g_bc, g_npf, g_nqf, g_fw, g_fb]
    sm_m = [m_meta_tokens, m_norm_pre_mix, m_norm_post_mix, m_b_in, m_attn_sinks, m_conv_dw_w, m_conv_dw_b, m_conv_ln_g,
            m_conv_ln_b, m_b_conv_proj, m_norm_pre_ffn, m_norm_post_ffn, m_ffn_dw_w, m_ffn_dw_b]
    sm_v = [v_meta_tokens, v_norm_pre_mix, v_norm_post_mix, v_b_in, v_attn_sinks, v_conv_dw_w, v_conv_dw_b, v_conv_ln_g,
            v_conv_ln_b, v_b_conv_proj, v_norm_pre_ffn, v_norm_post_ffn, v_ffn_dw_w, v_ffn_dw_b]
    sm_shapes = [a.shape for a in sm_w]
    upd_rows = 32
    v_pack = _flat_pack(sm_v, upd_rows)
    sm_out = _adamw(_flat_pack(sm_w, upd_rows), _flat_pack(sm_g, upd_rows), _flat_pack(sm_m, upd_rows), v_pack, "adamw_small")
    sm_d, sm_nm, sm_nv = (_flat_unpack(o, sm_shapes) for o in sm_out)

    order = ["meta_tokens", "norm_pre_mix", "norm_post_mix", "w_in", "b_in", "attn_sinks", "w_attn_proj", "conv_dw_w",
             "conv_dw_b", "conv_ln_g", "conv_ln_b", "w_conv_proj", "b_conv_proj", "w_out", "norm_pre_ffn", "norm_post_ffn",
             "w_up", "ffn_dw_w", "ffn_dw_b", "w_down"]
    small_names = ["meta_tokens", "norm_pre_mix", "norm_post_mix", "b_in", "attn_sinks", "conv_dw_w", "conv_dw_b", "conv_ln_g",
                   "conv_ln_b", "b_conv_proj", "norm_pre_ffn", "norm_post_ffn", "ffn_dw_w", "ffn_dw_b"]
    big_names = ["w_in", "w_up", "w_attn_proj", "w_conv_proj", "w_out", "w_down"]
    table = {}
    for k, nm in enumerate(small_names):
        table[nm] = (sm_g[k], sm_d[k], sm_nm[k], sm_nv[k])
    for k, nm in enumerate(big_names):
        table[nm] = big[k]
    grad_x = grad_x2d[None]
    outs = [loss, grad_x]
    for field in range(4):
        outs += [table[nm][field] for nm in order]
    return tuple(outs)
```

```python
import functools

import jax
import jax.numpy as jnp
from jax import lax
from jax.experimental import pallas as pl
from jax.experimental.pallas import tpu as pltpu

F32 = jnp.float32
BF16 = jnp.bfloat16
MESH = pl.DeviceIdType.MESH

D = 1024
HEAD_DIM = 64
N_META = 16
BLK = 128
PAD = BLK - N_META
CONV_K = 31
FFN = 2816
FFN_K = 3
QKV_W = 1280
IN_W = 5376
ROT_DIM = 16
ROPE_THETA = 500000.0
RMS_EPS = 1e-6
LN_EPS = 1e-5
NEG_INF = -1e30
SCALE = HEAD_DIM ** -0.5
N_DEV = 8

ADAM_LR = 0.001
ADAM_B1 = 0.9
ADAM_B2 = 0.999
ADAM_EPS = 1e-08
ADAM_WD = 0.01
ADAM_STEP = 10

VMEM_BYTES_V7X = 64 * 1024 * 1024
VMEM_LIMIT = VMEM_BYTES_V7X - 8 * 1024 * 1024

NT = (((1,), (1,)), ((), ()))
TN = (((0,), (0,)), ((), ()))
VM = pl.BlockSpec(memory_space=pltpu.VMEM)
ANY = pl.BlockSpec(memory_space=pl.ANY)


def _cparams(*sem):
    return pltpu.CompilerParams(dimension_semantics=sem or None, vmem_limit_bytes=VMEM_LIMIT)


def _row_tile(p):
    return 384 if p % 384 == 0 else 128


def _dot(a, b):
    return jnp.dot(a, b, preferred_element_type=F32)


def _dot_nt(a, b):
    return lax.dot_general(a, b, NT, preferred_element_type=F32)


def _dot_tn(a, b):
    return lax.dot_general(a, b, TN, preferred_element_type=F32)


def _rms(x, g):
    return x * lax.rsqrt(jnp.mean(x * x, axis=-1, keepdims=True) + RMS_EPS) * g


def _lnsilu(x, g, b):
    mu = jnp.mean(x, axis=-1, keepdims=True)
    var = jnp.mean(jnp.square(x - mu), axis=-1, keepdims=True)
    z = (x - mu) * lax.rsqrt(var + LN_EPS) * g + b
    return z * jax.nn.sigmoid(z)


def _rms_bwd(x, g, dy):
    r = lax.rsqrt(jnp.mean(x * x, axis=-1, keepdims=True) + RMS_EPS)
    xn = x * r
    u = dy * g
    dg = jnp.sum(dy * xn, axis=0, keepdims=True)
    dx = r * (u - xn * jnp.mean(u * xn, axis=-1, keepdims=True))
    return dx, dg


def _lnsilu_bwd(x, g, b, dout):
    mu = jnp.mean(x, axis=-1, keepdims=True)
    xc = x - mu
    rs = lax.rsqrt(jnp.mean(xc * xc, axis=-1, keepdims=True) + LN_EPS)
    yh = xc * rs
    z = yh * g + b
    sg = jax.nn.sigmoid(z)
    dz = dout * (sg * (1.0 + z * (1.0 - sg)))
    dg = jnp.sum(dz * yh, axis=0, keepdims=True)
    db = jnp.sum(dz, axis=0, keepdims=True)
    dyh = dz * g
    dx = rs * (dyh - jnp.mean(dyh, axis=-1, keepdims=True) - yh * jnp.mean(dyh * yh, axis=-1, keepdims=True))
    return dx, dg, db


def _rope(v, c, s1, s2):
    return v * c + pltpu.roll(v, BLK - 8, 1) * s1 + pltpu.roll(v, 8, 1) * s2


def _rows(i, tm):
    return i * tm + lax.broadcasted_iota(jnp.int32, (tm, 1), 0)


def _place():
    return lax.axis_index("x"), lax.axis_index("y"), lax.axis_index("c")


def _blk(ref, idx, r, dtype):
    return ref.at[pl.ds(pl.multiple_of(idx * r, 16 if dtype == BF16 else 8), r), :]


class _Gather:
    def __init__(self, arrs):
        self.ins = list(arrs)
        n = len(arrs)
        self.out_shape = [jax.ShapeDtypeStruct((N_DEV * a.shape[0], a.shape[1]), a.dtype) for a in arrs]
        self.scratch = [pltpu.SemaphoreType.DMA((n, 7)), pltpu.SemaphoreType.DMA((n, 7)), pltpu.SemaphoreType.DMA((n,))]

    def _parts(self, ins, outs, sems):
        send_sems, recv_sems, local_sems = sems
        n = len(ins)
        x, y, c = _place()
        me, sibling = (x, y, c), (x, y, 1 - c)
        chips = [(1 - x, y), (x, 1 - y), (1 - x, 1 - y)]

        def rows(a, p):
            return _blk(outs[a], 4 * p[0] + 2 * p[1] + p[2], self.ins[a].shape[0], self.ins[a].dtype)

        def copy(a, k, block, to, src=None):
            return pltpu.make_async_remote_copy(
                src_ref=rows(a, block) if src is None else src, dst_ref=rows(a, block),
                send_sem=send_sems.at[a, k], recv_sem=recv_sems.at[a, k], device_id=to, device_id_type=MESH)

        mine = [pltpu.make_async_copy(ins[a], rows(a, me), local_sems.at[a]) for a in range(n)]
        first = []
        for a in range(n):
            first.append(copy(a, 0, me, sibling, src=ins[a]))
            first += [copy(a, 1 + j, me, (*chip, c), src=ins[a]) for j, chip in enumerate(chips)]
        return n, c, me, sibling, chips, copy, mine, first

    def start(self, ins, outs, sems):
        *_, mine, first = self._parts(ins, outs, sems)
        for cp in mine + first:
            cp.start()

    def finish(self, ins, outs, sems):
        n, c, me, sibling, chips, copy, mine, first = self._parts(ins, outs, sems)
        passed = []
        for j, chip in enumerate(chips):
            for a in range(n):
                copy(a, 1 + j, (*chip, c), me).wait_recv()
                fwd = copy(a, 4 + j, (*chip, c), sibling)
                fwd.start()
                passed.append(fwd)
        for a in range(n):
            copy(a, 0, sibling, me).wait_recv()
            for j, chip in enumerate(chips):
                copy(a, 4 + j, (*chip, 1 - c), me).wait_recv()
        for cp in first + passed:
            cp.wait_send()
        for cp in mine:
            cp.wait()


class _GatherRelay:
    def __init__(self, arr):
        self.ins = [arr]
        self.r = arr.shape[0]
        self.out_shape = [jax.ShapeDtypeStruct((N_DEV * self.r, arr.shape[1]), arr.dtype)]
        self.scratch = [pltpu.SemaphoreType.DMA((9,)), pltpu.SemaphoreType.DMA((9,)), pltpu.SemaphoreType.DMA]

    def _parts(self, ins, outs, sems):
        send_sems, recv_sems, local_sem = sems
        x, y, c = _place()
        r, half = self.r, self.r // 2
        out = outs[0]
        me, sib, xn, yn = (x, y, c), (x, y, 1 - c), (1 - x, y, c), (x, 1 - y, c)
        dg = (1 - x, 1 - y, c)

        def rows(p, lo=0, n=r):
            return out.at[pl.ds(pl.multiple_of((4 * p[0] + 2 * p[1] + p[2]) * r + lo, 16), n), :]

        def copy(k, dev_rows, to, src=None):
            return pltpu.make_async_remote_copy(
                src_ref=dev_rows if src is None else src, dst_ref=dev_rows,
                send_sem=send_sems.at[k], recv_sem=recv_sems.at[k], device_id=to, device_id_type=MESH)

        mine = pltpu.make_async_copy(ins[0], rows(me), local_sem)
        first = [copy(0, rows(me), sib, src=ins[0]), copy(1, rows(me), xn, src=ins[0]), copy(2, rows(me), yn, src=ins[0])]
        arrive = {0: rows(sib), 1: rows(xn), 2: rows(yn), 3: rows(dg, 0, half), 4: rows(dg, half, half),
                  5: rows((1 - x, y, 1 - c)), 6: rows((x, 1 - y, 1 - c)),
                  7: rows((1 - x, 1 - y, 1 - c), 0, half), 8: rows((1 - x, 1 - y, 1 - c), half, half)}
        relay = {1: [(3, rows(xn, 0, half), yn), (5, rows(xn), sib)],
                 2: [(4, rows(yn, half, half), xn), (6, rows(yn), sib)],
                 3: [(7, rows(dg, 0, half), sib)], 4: [(8, rows(dg, half, half), sib)]}
        return copy, mine, first, arrive, relay, me

    def start(self, ins, outs, sems):
        _, mine, first, _, _, _ = self._parts(ins, outs, sems)
        for cp in [mine] + first:
            cp.start()

    def finish(self, ins, outs, sems):
        copy, mine, first, arrive, relay, me = self._parts(ins, outs, sems)
        passed = []
        for k in (1, 2, 3, 4):
            copy(k, arrive[k], me).wait_recv()
            for k2, dev_rows, to in relay[k]:
                fwd = copy(k2, dev_rows, to)
                fwd.start()
                passed.append(fwd)
        for k in (0, 5, 6, 7, 8):
            copy(k, arrive[k], me).wait_recv()
        for cp in first + passed:
            cp.wait_send()
        mine.wait()


FLIPS = [(0, 0, 1), (1, 0, 0), (0, 1, 0), (1, 1, 0), (1, 0, 1), (0, 1, 1), (1, 1, 1)]


class _Scatter:
    def __init__(self, arrs, part=0, nparts=1):
        self.ins = list(arrs)
        self.part, self.nparts = part, nparts
        n = len(arrs)
        self.out_shape = [jax.ShapeDtypeStruct((a.shape[0] // nparts, a.shape[1]), a.dtype) for a in arrs]
        self.scratch = [pltpu.SemaphoreType.DMA((n, 7)), pltpu.SemaphoreType.DMA((n, 7)), pltpu.SemaphoreType.DMA((n,))]

    def _parts(self, ins, outs, sems):
        send_sems, recv_sems, local_sems = sems
        n = len(ins)
        x, y, c = _place()
        me = 4 * x + 2 * y + c

        def flip(v, f):
            return 1 - v if f else v

        def src(a, idx):
            r = self.ins[a].shape[0] // N_DEV
            rs = r // self.nparts
            return ins[a].at[pl.ds(pl.multiple_of(idx * r + self.part * rs, 16), rs), :]

        def dst(a, idx):
            rs = self.ins[a].shape[0] // N_DEV // self.nparts
            return outs[a].at[pl.ds(pl.multiple_of(idx * rs, 16), rs), :]

        mine = [pltpu.make_async_copy(src(a, me), dst(a, me), local_sems.at[a]) for a in range(n)]
        sends, recvs = [], []
        for k, f in enumerate(FLIPS):
            peer = (flip(x, f[0]), flip(y, f[1]), flip(c, f[2]))
            pidx = 4 * peer[0] + 2 * peer[1] + peer[2]
            for a in range(n):
                sends.append(pltpu.make_async_remote_copy(
                    src_ref=src(a, pidx), dst_ref=dst(a, me),
                    send_sem=send_sems.at[a, k], recv_sem=recv_sems.at[a, k], device_id=peer, device_id_type=MESH))
                recvs.append(functools.partial(
                    pltpu.make_async_remote_copy,
                    src_ref=src(a, pidx), dst_ref=dst(a, pidx),
                    send_sem=send_sems.at[a, k], recv_sem=recv_sems.at[a, k], device_id=peer, device_id_type=MESH))
        return mine, sends, recvs

    def start(self, ins, outs, sems):
        mine, sends, _ = self._parts(ins, outs, sems)
        for cp in mine + sends:
            cp.start()

    def finish(self, ins, outs, sems):
        mine, sends, recvs = self._parts(ins, outs, sems)
        for make in recvs:
            make().wait_recv()
        for cp in sends:
            cp.wait_send()
        for cp in mine:
            cp.wait()


N_CHIP = 4


class _SiblingSwap:
    def __init__(self, arr):
        self.ins = [arr]
        self.r = arr.shape[0] // N_DEV
        self.out_shape = [jax.ShapeDtypeStruct((N_CHIP * self.r, arr.shape[1]), arr.dtype)]
        self.scratch = [pltpu.SemaphoreType.DMA((N_CHIP,)), pltpu.SemaphoreType.DMA((N_CHIP,))]

    def _copies(self, ins, outs, sems):
        send_sems, recv_sems = sems
        x, y, c = _place()
        r = self.r
        return [pltpu.make_async_remote_copy(
            src_ref=ins[0].at[pl.ds(pl.multiple_of((2 * j + 1 - c) * r, 16), r), :],
            dst_ref=outs[0].at[pl.ds(j * r, r), :],
            send_sem=send_sems.at[j], recv_sem=recv_sems.at[j], device_id=(x, y, 1 - c), device_id_type=MESH)
            for j in range(N_CHIP)]

    def start(self, ins, outs, sems):
        for cp in self._copies(ins, outs, sems):
            cp.start()

    def finish(self, ins, outs, sems):
        for cp in self._copies(ins, outs, sems):
            cp.wait()


class _ChipScatter:
    def __init__(self, arr):
        self.ins = [arr]
        self.r = arr.shape[0] // N_CHIP
        self.out_shape = [jax.ShapeDtypeStruct(arr.shape, arr.dtype)]
        self.scratch = [pltpu.SemaphoreType.DMA((3,)), pltpu.SemaphoreType.DMA((3,)), pltpu.SemaphoreType.DMA]

    def _parts(self, ins, outs, sems):
        send_sems, recv_sems, local_sem = sems
        x, y, c = _place()
        r = self.r
        my_chip = 2 * x + y

        def rows(ref, j):
            return ref.at[pl.ds(pl.multiple_of(j * r, 16), r), :]

        mine = pltpu.make_async_copy(rows(ins[0], my_chip), rows(outs[0], my_chip), local_sem)
        sends, recvs = [], []
        for k, (fx, fy) in enumerate(((1, 0), (0, 1), (1, 1))):
            px, py = (1 - x if fx else x), (1 - y if fy else y)
            peer_chip = 2 * px + py
            sends.append(pltpu.make_async_remote_copy(
                src_ref=rows(ins[0], peer_chip), dst_ref=rows(outs[0], my_chip),
                send_sem=send_sems.at[k], recv_sem=recv_sems.at[k], device_id=(px, py, c), device_id_type=MESH))
            recvs.append(functools.partial(
                pltpu.make_async_remote_copy,
                src_ref=rows(ins[0], peer_chip), dst_ref=rows(outs[0], peer_chip),
                send_sem=send_sems.at[k], recv_sem=recv_sems.at[k], device_id=(px, py, c), device_id_type=MESH))
        return mine, sends, recvs

    def start(self, ins, outs, sems):
        mine, sends, _ = self._parts(ins, outs, sems)
        for cp in [mine] + sends:
            cp.start()

    def finish(self, ins, outs, sems):
        mine, sends, recvs = self._parts(ins, outs, sems)
        for make in recvs:
            make().wait_recv()
        for cp in sends:
            cp.wait_send()
        mine.wait()


def _pair_add(partial, recv):
    r = recv.shape[0] // N_CHIP
    cols = recv.shape[1]
    tr = r // 2 if (r // 2) % 16 == 0 else r
    steps = r // tr
    core = lax.axis_index("c").astype(jnp.int32).reshape(1)

    def body(c_ref, p_ref, s_ref, o_ref):
        o_ref[...] = (p_ref[...].astype(F32) + s_ref[...].astype(F32)).astype(BF16)

    spec = pl.BlockSpec((tr, cols), lambda j, i, c_ref: (j * steps + i, 0))
    return pl.pallas_call(
        body, name="pair_add",
        grid_spec=pltpu.PrefetchScalarGridSpec(
            num_scalar_prefetch=1, grid=(N_CHIP, steps),
            in_specs=[pl.BlockSpec((tr, cols), lambda j, i, c_ref: ((2 * j + c_ref[0]) * steps + i, 0)), spec],
            out_specs=spec),
        out_shape=jax.ShapeDtypeStruct(recv.shape, BF16),
        compiler_params=_cparams("parallel", "parallel"),
    )(core, partial, recv)


class _Both:
    def __init__(self, a, b):
        self.a, self.b = a, b
        self.ins = a.ins + b.ins
        self.out_shape = a.out_shape + b.out_shape
        self.scratch = a.scratch + b.scratch

    def _split(self, ins, outs, sems):
        ni, no, ns = len(self.a.ins), len(self.a.out_shape), len(self.a.scratch)
        return (ins[:ni], outs[:no], sems[:ns]), (ins[ni:], outs[no:], sems[ns:])

    def start(self, ins, outs, sems):
        ra, rb = self._split(ins, outs, sems)
        self.a.start(*ra)
        self.b.start(*rb)

    def finish(self, ins, outs, sems):
        ra, rb = self._split(ins, outs, sems)
        self.a.finish(*ra)
        self.b.finish(*rb)


def _exchange(comm, name):
    n, m = len(comm.ins), len(comm.out_shape)

    def body(*refs):
        ins, outs, sems = refs[:n], refs[n:n + m], refs[n + m:]
        comm.start(ins, outs, sems)
        comm.finish(ins, outs, sems)

    return pl.pallas_call(
        body, name=name, out_shape=comm.out_shape, in_specs=[ANY] * n, out_specs=[ANY] * m, scratch_shapes=comm.scratch,
    )(*comm.ins)


def _call(body, *, name, grid, in_specs, out_specs, out_shape, args, scratch=(), sem="parallel", comm=None):
    if comm is None:
        outs = pl.pallas_call(
            body, name=name, grid=grid, in_specs=list(in_specs), out_specs=list(out_specs), out_shape=list(out_shape),
            scratch_shapes=list(scratch), compiler_params=_cparams(sem))(*args)
        return outs, []
    n_in, n_out, n_sc = len(in_specs), len(out_specs), len(scratch)
    n_ci, n_co = len(comm.ins), len(comm.out_shape)
    last = grid[0] - 1

    def fused(*refs):
        ins, refs = refs[:n_in], refs[n_in:]
        c_ins, refs = refs[:n_ci], refs[n_ci:]
        outs, refs = refs[:n_out], refs[n_out:]
        c_outs, refs = refs[:n_co], refs[n_co:]
        sc, c_sems = refs[:n_sc], refs[n_sc:]
        step = pl.program_id(0)

        @pl.when(step == 0)
        def _():
            comm.start(c_ins, c_outs, c_sems)

        body(*ins, *outs, *sc)

        @pl.when(step == last)
        def _():
            comm.finish(c_ins, c_outs, c_sems)

    outs = pl.pallas_call(
        fused, name=name, grid=grid, in_specs=list(in_specs) + [ANY] * n_ci, out_specs=list(out_specs) + [ANY] * n_co,
        out_shape=list(out_shape) + comm.out_shape, scratch_shapes=list(scratch) + comm.scratch,
        compiler_params=_cparams("arbitrary"))(*args, *comm.ins)
    return outs[:n_out], outs[n_out:]


def _token_specs(tm):
    k = tm // BLK
    return [pl.BlockSpec((BLK, D), functools.partial(lambda i, t: (jnp.maximum(k * i + t - 1, 0), 0), t=t)) for t in range(k)]


def _in_proj(x2d, meta, gain, w_int, b_in, tabs, comm=None):
    p = x2d.shape[0] + BLK
    tm = _row_tile(p)
    k = tm // BLK

    def body(*refs):
        x_refs = refs[:k]
        m_ref, g_ref, w_ref, b_ref, t_ref, h_ref, n1_ref, q_ref, kv_ref, ag_ref, gt_ref = refs[k:]
        i = pl.program_id(0)
        head = jnp.concatenate([jnp.zeros((PAD, D), F32), m_ref[...]], axis=0)
        first = jnp.where(i == 0, head, x_refs[0][...])
        h = jnp.concatenate([first] + [r[...] for r in x_refs[1:]], axis=0) if k > 1 else first
        h_ref[...] = h
        n = _rms(h, g_ref[...]).astype(BF16)
        n1_ref[...] = n
        c, s1, s2 = t_ref[:, 0:128], t_ref[:, 128:256], t_ref[:, 256:384]

        def mm(c0, w):
            return _dot_nt(n, w_ref[c0:c0 + w, :]) + b_ref[:, c0:c0 + w]

        for j in range(4):
            acc = mm(256 * j, 256)
            for t in range(2):
                lo = 256 * j + 128 * t
                q_ref[:, lo:lo + 128] = (_rope(acc[:, 128 * t:128 * (t + 1)], c, s1, s2) * SCALE).astype(BF16)
        acc = mm(1024, 256)
        kv_ref[:, 0:128] = _rope(acc[:, 0:128], c, s1, s2).astype(BF16)
        kv_ref[:, 128:256] = acc[:, 128:256].astype(BF16)
        for j in range(8):
            ag_ref[:, 256 * j:256 * (j + 1)] = mm(QKV_W + 256 * j, 256).astype(BF16)
        for j in range(8):
            gt_ref[:, 256 * j:256 * (j + 1)] = mm(QKV_W + 2048 + 256 * j, 256).astype(BF16)

    def row(w):
        return pl.BlockSpec((tm, w), lambda i: (i, 0))

    return _call(
        body, name="in_proj", grid=(p // tm,),
        in_specs=_token_specs(tm) + [VM, VM, VM, VM, row(384)],
        out_specs=[row(D), row(D), row(D), row(256), row(2048), row(2048)],
        out_shape=[jax.ShapeDtypeStruct((p, D), F32)] + [jax.ShapeDtypeStruct((p, w), BF16) for w in (D, D, 256, 2048, 2048)],
        args=(x2d,) * k + (meta, gain, w_int, b_in, tabs), comm=comm)


N_KEY = 2 * BLK + N_META


def _attn_setup(n, h, q_ref, km_ref, kp_ref, kc_ref):
    lo = lax.broadcasted_iota(jnp.int32, (BLK, BLK), 1) < HEAD_DIM
    lok = lax.broadcasted_iota(jnp.int32, (N_KEY, BLK), 1) < HEAD_DIM

    def dup(lanes):
        cat = jnp.concatenate([kp_ref[:, lanes], kc_ref[:, lanes], km_ref[PAD:BLK, lanes]], axis=0).astype(F32)
        rolled = pltpu.roll(cat, HEAD_DIM, 1)
        return (jnp.where(lok, cat, rolled) if h == 0 else jnp.where(lok, rolled, cat)).astype(BF16)

    k2 = dup(slice(0, 128))
    v2 = dup(slice(128, 256))
    qs = _stack_heads(q_ref, h, lo)

    kr = lax.broadcasted_iota(jnp.int32, (BLK, BLK), 0)
    tq = BLK * n + lax.broadcasted_iota(jnp.int32, (BLK, BLK), 1) - PAD
    t_p = BLK * (n - 1) + kr - PAD
    t_c = BLK * n + kr - PAD
    ok_p = jnp.logical_and(t_p >= N_META, tq - t_p < BLK)
    ok_c = jnp.logical_and(t_c >= N_META, t_c <= tq)
    ok_m = lax.broadcasted_iota(jnp.int32, (N_META, BLK), 0) <= BLK * n + lax.broadcasted_iota(jnp.int32, (N_META, BLK), 1) - PAD
    bias = jnp.concatenate([jnp.where(ok, 0.0, NEG_INF).astype(F32) for ok in (ok_p, ok_c, ok_m)], axis=0)
    return qs, k2, v2, bias, lok


def _attn_head(s, bias, sink):
    s = s + bias
    m = jnp.maximum(jnp.max(s, axis=0, keepdims=True), sink)
    e = jnp.exp(s - m)
    es = jnp.exp(sink - m)
    inv = 1.0 / (jnp.sum(e, axis=0, keepdims=True) + es)
    return e * inv, es * inv


def _stack_heads(ref, h, lo):
    pieces = []
    for jp in range(4):
        v = ref[:, BLK * (4 * h + jp):BLK * (4 * h + jp + 1)]
        zero = jnp.zeros_like(v)
        pieces += [jnp.where(lo, v, zero), jnp.where(lo, zero, v)]
    return jnp.concatenate(pieces, axis=0)


def _unstack_heads(v, jp, lo):
    return jnp.where(lo, v[256 * jp:256 * jp + 128], v[256 * jp + 128:256 * jp + 256])


def _attn_fwd(q, kv, sinks, comm=None):
    p = q.shape[0]
    nb = p // BLK

    def body(q_ref, km_ref, kp_ref, kc_ref, sink_ref, o_ref):
        n = pl.program_id(0)
        lo = lax.broadcasted_iota(jnp.int32, (BLK, BLK), 1) < HEAD_DIM
        for h in range(2):
            qs, k2, v2, bias, _ = _attn_setup(n, h, q_ref, km_ref, kp_ref, kc_ref)
            st = _dot_nt(k2, qs)
            pt = jnp.concatenate(
                [_attn_head(st[:, BLK * g:BLK * (g + 1)], bias, sink_ref[0, 8 * h + g])[0].astype(BF16) for g in range(8)],
                axis=1)
            o = _dot_tn(pt, v2)
            for jp in range(4):
                o_ref[:, BLK * (4 * h + jp):BLK * (4 * h + jp + 1)] = _unstack_heads(o, jp, lo).astype(BF16)

    return _call(
        body, name="attn_fwd", grid=(nb,),
        in_specs=[pl.BlockSpec((BLK, D), lambda i: (i, 0)),
                  pl.BlockSpec((BLK, 256), lambda i: (0, 0)),
                  pl.BlockSpec((BLK, 256), lambda i: (jnp.maximum(i - 1, 0), 0)),
                  pl.BlockSpec((BLK, 256), lambda i: (i, 0)),
                  pl.BlockSpec(memory_space=pltpu.SMEM)],
        out_specs=[pl.BlockSpec((BLK, D), lambda i: (i, 0))],
        out_shape=[jax.ShapeDtypeStruct((p, D), BF16)],
        args=(q, kv, kv, kv, sinks), comm=comm)


def _conv31_fwd(ag, w32, b, comm=None):
    p = ag.shape[0]
    nch = p // BLK

    def body(a_ref, g_ref, w_ref, b_ref, o_ref, gp):
        gp[0:32, :] = jnp.zeros((32, BLK), F32)
        for ci in range(nch):
            r0 = BLK * ci
            glu = a_ref[r0:r0 + BLK, :].astype(F32) * jax.nn.sigmoid(g_ref[r0:r0 + BLK, :].astype(F32))
            if ci == 0:
                glu = jnp.where(_rows(0, BLK) >= PAD, glu, 0.0)
            gp[32 + r0:32 + r0 + BLK, :] = glu
        for ci in range(nch):
            r0 = BLK * ci
            acc = jnp.broadcast_to(b_ref[...], (BLK, BLK))
            for j in range(CONV_K):
                acc = acc + w_ref[j:j + 1, :] * gp[r0 + j + 2:r0 + j + 2 + BLK, :]
            o_ref[r0:r0 + BLK, :] = acc

    return _call(
        body, name="conv31_fwd", grid=(D // BLK,),
        in_specs=[pl.BlockSpec((p, BLK), lambda j: (0, j)), pl.BlockSpec((p, BLK), lambda j: (0, 8 + j)),
                  pl.BlockSpec((32, BLK), lambda j: (0, j)), pl.BlockSpec((1, BLK), lambda j: (0, j))],
        out_specs=[pl.BlockSpec((p, BLK), lambda j: (0, j))],
        out_shape=[jax.ShapeDtypeStruct((p, D), F32)],
        scratch=[pltpu.VMEM((p + 32, BLK), F32)],
        args=(ag, ag, w32, b), comm=comm)


def _mixer_fwd(ao, c0, gates, h0p, wa, wc, wo, vecs):
    p = ao.shape[0]
    tm = _row_tile(p)

    def body(ao_ref, c0_ref, gt_ref, h_ref, wa_ref, wc_ref, wo_ref, v_ref,
             c1_ref, at_ref, cv_ref, mg_ref, mix_ref, h1_ref, n2_ref):
        i = pl.program_id(0)
        c1 = _lnsilu(c0_ref[...], v_ref[0:1, :], v_ref[1:2, :]).astype(BF16)
        c1_ref[...] = c1
        attn = _dot(ao_ref[...], wa_ref[...])
        conv = _dot(c1, wc_ref[...]) + v_ref[2:3, :]
        at_ref[...] = attn.astype(BF16)
        cv_ref[...] = conv.astype(BF16)
        merged = (jax.nn.sigmoid(gt_ref[:, 0:D].astype(F32)) * attn
                  + jax.nn.sigmoid(gt_ref[:, D:2 * D].astype(F32)) * conv).astype(BF16)
        mg_ref[...] = merged
        mix = _dot(merged, wo_ref[...])
        mix_ref[...] = mix
        h1 = jnp.where(_rows(i, tm) >= PAD, h_ref[...] + _rms(mix, v_ref[3:4, :]), 0.0)
        h1_ref[...] = h1
        n2_ref[...] = _rms(h1, v_ref[4:5, :]).astype(BF16)

    def row(w):
        return pl.BlockSpec((tm, w), lambda i: (i, 0))

    return pl.pallas_call(
        body, name="mixer_fwd", grid=(p // tm,),
        in_specs=[row(D), row(D), row(2 * D), row(D), VM, VM, VM, VM],
        out_specs=[row(D)] * 7,
        out_shape=[jax.ShapeDtypeStruct((p, D), t) for t in (BF16, BF16, BF16, BF16, F32, F32, BF16)],
        compiler_params=_cparams("parallel"),
    )(ao, c0, gates, h0p, wa, wc, wo, vecs)


def _mm_nt(a, w_t, name):
    p, k = a.shape
    n = w_t.shape[0]
    tm = _row_tile(p)
    ch = 512

    def body(a_ref, w_ref, o_ref):
        a_v = a_ref[...]
        for c0 in range(0, n, ch):
            o_ref[:, c0:c0 + ch] = _dot_nt(a_v, w_ref[c0:c0 + ch, :]).astype(BF16)

    return pl.pallas_call(
        body, name=name, grid=(p // tm,),
        in_specs=[pl.BlockSpec((tm, k), lambda i: (i, 0)), VM],
        out_specs=pl.BlockSpec((tm, n), lambda i: (i, 0)),
        out_shape=jax.ShapeDtypeStruct((p, n), BF16),
        compiler_params=_cparams("parallel"),
    )(a, w_t)


def _conv3(xp_ref, w_ref, r0):
    return (w_ref[0:1, :] * xp_ref[r0 + 6:r0 + 6 + BLK, :] + w_ref[1:2, :] * xp_ref[r0 + 7:r0 + 7 + BLK, :]
            + w_ref[2:3, :] * xp_ref[r0 + 8:r0 + 8 + BLK, :])


def _ffn_slab_specs(p):
    ncol = FFN // BLK
    return [pl.BlockSpec((p, BLK), lambda j: (0, j)), pl.BlockSpec((p, BLK), lambda j: (0, ncol + j)),
            pl.BlockSpec((FFN_K, BLK), lambda j: (0, j)), pl.BlockSpec((FFN_K, BLK), lambda j: (0, ncol + j)),
            pl.BlockSpec((1, BLK), lambda j: (0, j)), pl.BlockSpec((1, BLK), lambda j: (0, ncol + j))]


def _fill_shifted(dst, src_ref, nch):
    dst[0:8, :] = jnp.zeros((8, BLK), F32)
    for ci in range(nch):
        dst[8 + BLK * ci:8 + BLK * (ci + 1), :] = src_ref[BLK * ci:BLK * (ci + 1), :].astype(F32)


def _ffn_act(u0, fw, fb):
    p = u0.shape[0]
    nch = p // BLK

    def body(g_ref, v_ref, wg_ref, wv_ref, bg_ref, bv_ref, o_ref, dv_ref, dg_ref, xg, xv):
        _fill_shifted(xg, g_ref, nch)
        _fill_shifted(xv, v_ref, nch)
        for ci in range(nch):
            r0 = BLK * ci
            ug = _conv3(xg, wg_ref, r0) + bg_ref[...]
            uv = _conv3(xv, wv_ref, r0) + bv_ref[...]
            sg = jax.nn.sigmoid(ug)
            silu = ug * sg
            o_ref[r0:r0 + BLK, :] = (silu * uv).astype(BF16)
            dv_ref[r0:r0 + BLK, :] = silu.astype(BF16)
            dg_ref[r0:r0 + BLK, :] = (uv * (sg * (1.0 + ug * (1.0 - sg)))).astype(BF16)

    slab = pl.BlockSpec((p, BLK), lambda j: (0, j))
    return pl.pallas_call(
        body, name="ffn_act", grid=(FFN // BLK,),
        in_specs=_ffn_slab_specs(p),
        out_specs=[slab] * 3,
        out_shape=[jax.ShapeDtypeStruct((p, FFN), BF16)] * 3,
        scratch_shapes=[pltpu.VMEM((p + 8, BLK), F32)] * 2,
        compiler_params=_cparams("parallel"),
    )(u0, u0, fw, fw, fb, fb)


def _ffn_down_loss(act, wd, h1, tgt, gain):
    p = act.shape[0]
    tm = _row_tile(p)
    k = tm // BLK

    def body(*refs):
        a_ref, w_ref, h_ref = refs[:3]
        t_refs = refs[3:3 + k]
        g_ref, df_ref, da_ref, dy_ref, acc_ref = refs[3 + k:]
        i = pl.program_id(0)

        @pl.when(i == 0)
        def _():
            acc_ref[...] = jnp.zeros_like(acc_ref)

        ffn = _dot(a_ref[...], w_ref[...])
        t = jnp.concatenate([t_ref[...] for t_ref in t_refs], axis=0) if k > 1 else t_refs[0][...]
        diff = jnp.where(_rows(i, tm) >= BLK, h_ref[...] + _rms(ffn, g_ref[...]) - t, 0.0)
        dy = diff * (1.0 / D)
        dffn, dg = _rms_bwd(ffn, g_ref[...], dy)
        acc_ref[0:1, :] += dg
        acc_ref[1:2, :] += jnp.sum(diff * diff, axis=0, keepdims=True) * (0.5 / D)
        dy_ref[...] = dy
        dfb = dffn.astype(BF16)
        df_ref[...] = dfb
        for c0 in range(0, FFN, 256):
            da_ref[:, c0:c0 + 256] = _dot_nt(dfb, w_ref[c0:c0 + 256, :]).astype(BF16)

    def row(w):
        return pl.BlockSpec((tm, w), lambda i: (i, 0))

    return pl.pallas_call(
        body, name="ffn_down_loss", grid=(p // tm,),
        in_specs=[row(FFN), VM, row(D)] + _token_specs(tm) + [VM],
        out_specs=[row(D), row(FFN), row(D), pl.BlockSpec((8, D), lambda i: (0, 0))],
        out_shape=[jax.ShapeDtypeStruct((p, D), BF16), jax.ShapeDtypeStruct((p, FFN), BF16),
                   jax.ShapeDtypeStruct((p, D), F32), jax.ShapeDtypeStruct((8, D), F32)],
        compiler_params=_cparams("arbitrary"),
    )(act, wd, h1, *([tgt] * k), gain)


def _mm_tn(pieces, b, name, col_sums=False, comm=None):
    p, n = b.shape
    tk = 256
    nblk = [a.shape[1] // tk for a in pieces]
    offs = [sum(nblk[:q]) for q in range(len(pieces))]
    total = sum(nblk)
    npc = len(pieces)

    def body(*refs):
        a_refs, b_ref, o_ref = refs[:npc], refs[npc], refs[npc + 1]
        i = pl.program_id(0)
        for q, a_ref in enumerate(a_refs):
            @pl.when(jnp.logical_and(i >= offs[q], i < offs[q] + nblk[q]))
            def _(a_ref=a_ref):
                a_v = a_ref[...]
                o_ref[...] = _dot_tn(a_v, b_ref[...]).astype(BF16)
                if col_sums:
                    refs[npc + 2][...] = jnp.sum(a_v.astype(F32), axis=0, keepdims=True)

    def a_spec(q):
        return pl.BlockSpec((p, tk), lambda i: (0, jnp.clip(i - offs[q], 0, nblk[q] - 1)))

    out_specs = [pl.BlockSpec((tk, n), lambda i: (i, 0))]
    out_shape = [jax.ShapeDtypeStruct((total * tk, n), BF16)]
    if col_sums:
        out_specs.append(pl.BlockSpec((1, tk), lambda i: (0, i)))
        out_shape.append(jax.ShapeDtypeStruct((1, total * tk), F32))
    res, sent = _call(
        body, name=name, grid=(total,),
        in_specs=[a_spec(q) for q in range(npc)] + [VM],
        out_specs=out_specs, out_shape=out_shape, args=(*pieces, b), comm=comm)
    res = res if col_sums else res[0]
    return res if comm is None else (res, sent)


def _ffn_act_bwd(u0, dact, dact_dg, dact_dv, fw, act, dffn, comm=None):
    p = u0.shape[0]
    nch = p // BLK
    ncol = FFN // BLK

    def body(g_ref, v_ref, wg_ref, wv_ref, da_ref, lg_ref, lv_ref, act_ref, df_ref,
             dg_ref, dv_ref, gwg_ref, gwv_ref, gbg_ref, gbv_ref, gwd_ref, xg, xv, eg, ev):
        gwd_ref[...] = _dot_tn(act_ref[...], df_ref[...]).astype(BF16)
        _fill_shifted(xg, g_ref, nch)
        _fill_shifted(xv, v_ref, nch)
        eg[p:p + 8, :] = jnp.zeros((8, BLK), F32)
        ev[p:p + 8, :] = jnp.zeros((8, BLK), F32)
        for ci in range(nch):
            r0 = BLK * ci
            d = da_ref[r0:r0 + BLK, :].astype(F32)
            eg[r0:r0 + BLK, :] = d * lg_ref[r0:r0 + BLK, :].astype(F32)
            ev[r0:r0 + BLK, :] = d * lv_ref[r0:r0 + BLK, :].astype(F32)
        for e_s, x_s, w_ref, d_ref, gw_ref, gb_ref in ((eg, xg, wg_ref, dg_ref, gwg_ref, gbg_ref),
                                                      (ev, xv, wv_ref, dv_ref, gwv_ref, gbv_ref)):
            sums = [jnp.zeros((BLK, BLK), F32) for _ in range(FFN_K + 1)]
            for ci in range(nch):
                r0 = BLK * ci
                e0 = e_s[r0:r0 + BLK, :]
                du = (w_ref[2:3, :] * e0 + w_ref[1:2, :] * e_s[r0 + 1:r0 + 1 + BLK, :]
                      + w_ref[0:1, :] * e_s[r0 + 2:r0 + 2 + BLK, :])
                if ci == 0:
                    du = jnp.where(_rows(0, BLK) >= PAD, du, 0.0)
                d_ref[r0:r0 + BLK, :] = du.astype(BF16)
                for j in range(FFN_K):
                    sums[j] = sums[j] + e0 * x_s[r0 + 6 + j:r0 + 6 + j + BLK, :]
                sums[FFN_K] = sums[FFN_K] + e0
            for j in range(FFN_K):
                gw_ref[j:j + 1, :] = jnp.sum(sums[j], axis=0, keepdims=True)
            gb_ref[...] = jnp.sum(sums[FFN_K], axis=0, keepdims=True)

    slab = pl.BlockSpec((p, BLK), lambda j: (0, j))
    wspec = pl.BlockSpec((FFN_K, BLK), lambda j: (0, j))
    bspec = pl.BlockSpec((1, BLK), lambda j: (0, j))
    return _call(
        body, name="ffn_act_bwd", grid=(ncol,),
        in_specs=_ffn_slab_specs(p)[:4] + [slab] * 4 + [VM],
        out_specs=[slab, slab, wspec, wspec, bspec, bspec, pl.BlockSpec((BLK, D), lambda j: (j, 0))],
        out_shape=[jax.ShapeDtypeStruct((p, FFN), BF16)] * 2 + [jax.ShapeDtypeStruct((FFN_K, FFN), F32)] * 2
        + [jax.ShapeDtypeStruct((1, FFN), F32)] * 2 + [jax.ShapeDtypeStruct((FFN, D), BF16)],
        scratch=[pltpu.VMEM((p + 8, BLK), F32)] * 4,
        args=(u0, u0, fw, fw, dact, dact_dg, dact_dv, act, dffn), comm=comm)


def _ffn_in_bwd(dug, duv, w_upt, h1, dy, gain, comm=None):
    p = h1.shape[0]
    tm = _row_tile(p)

    def body(dg_ref, dv_ref, w_ref, h_ref, dy_ref, g_ref, o_ref, acc_ref):
        i = pl.program_id(0)

        @pl.when(i == 0)
        def _():
            acc_ref[...] = jnp.zeros_like(acc_ref)

        dn = _dot(dg_ref[...], w_ref[0:FFN, :]) + _dot(dv_ref[...], w_ref[FFN:2 * FFN, :])
        dh, dg = _rms_bwd(h_ref[...], g_ref[...], dn)
        o_ref[...] = dy_ref[...] + dh
        acc_ref[0:1, :] += dg

    def row(w):
        return pl.BlockSpec((tm, w), lambda i: (i, 0))

    return _call(
        body, name="ffn_in_bwd", grid=(p // tm,),
        in_specs=[row(FFN), row(FFN), VM, row(D), row(D), VM],
        out_specs=[row(D), pl.BlockSpec((8, D), lambda i: (0, 0))],
        out_shape=[jax.ShapeDtypeStruct((p, D), F32), jax.ShapeDtypeStruct((8, D), F32)],
        sem="arbitrary", args=(dug, duv, w_upt, h1, dy, gain), comm=comm)


def _mixer_bwd(dh1, mix, attn, conv, gates, c0, wa, wc, wo, vecs, comm=None):
    p = dh1.shape[0]
    tm = _row_tile(p)

    def body(dh_ref, mix_ref, at_ref, cv_ref, gt_ref, c0_ref, wa_ref, wc_ref, wo_ref, v_ref,
             dmix_ref, dat_ref, dcv_ref, dgt_ref, dao_ref, dc0_ref, acc_ref):
        i = pl.program_id(0)

        @pl.when(i == 0)
        def _():
            acc_ref[...] = jnp.zeros_like(acc_ref)

        dmix, dgp = _rms_bwd(mix_ref[...], v_ref[3:4, :], dh_ref[...])
        dmix = dmix.astype(BF16)
        dmix_ref[...] = dmix
        dmg = _dot_nt(dmix, wo_ref[...])
        sa = jax.nn.sigmoid(gt_ref[:, 0:D].astype(F32))
        sc = jax.nn.sigmoid(gt_ref[:, D:2 * D].astype(F32))
        dat = dmg * sa
        dcv = dmg * sc
        dgt_ref[:, 0:D] = (dmg * at_ref[...].astype(F32) * sa * (1.0 - sa)).astype(BF16)
        dgt_ref[:, D:2 * D] = (dmg * cv_ref[...].astype(F32) * sc * (1.0 - sc)).astype(BF16)
        datb = dat.astype(BF16)
        dcvb = dcv.astype(BF16)
        dat_ref[...] = datb
        dcv_ref[...] = dcvb
        dao_ref[...] = _dot_nt(datb, wa_ref[...]).astype(BF16)
        dc1 = _dot_nt(dcvb, wc_ref[...])
        dc0, dlg, dlb = _lnsilu_bwd(c0_ref[...], v_ref[0:1, :], v_ref[1:2, :], dc1)
        dc0_ref[...] = dc0
        acc_ref[0:1, :] += dgp
        acc_ref[1:2, :] += jnp.sum(dcv, axis=0, keepdims=True)
        acc_ref[2:3, :] += dlg
        acc_ref[3:4, :] += dlb

    def row(w):
        return pl.BlockSpec((tm, w), lambda i: (i, 0))

    return _call(
        body, name="mixer_bwd", grid=(p // tm,),
        in_specs=[row(D), row(D), row(D), row(D), row(2 * D), row(D), VM, VM, VM, VM],
        out_specs=[row(D), row(D), row(D), row(2 * D), row(D), row(D), pl.BlockSpec((8, D), lambda i: (0, 0))],
        out_shape=[jax.ShapeDtypeStruct((p, D), BF16)] * 3 + [jax.ShapeDtypeStruct((p, 2 * D), BF16),
                                                             jax.ShapeDtypeStruct((p, D), BF16),
                                                             jax.ShapeDtypeStruct((p, D), F32),
                                                             jax.ShapeDtypeStruct((8, D), F32)],
        sem="arbitrary", args=(dh1, mix, attn, conv, gates, c0, wa, wc, wo, vecs), comm=comm)


def _conv31_bwd(ag, dc0, w32, tn_pairs, comm=None):
    p = ag.shape[0]
    nch = p // BLK
    npair = len(tn_pairs)

    def body(*refs):
        a_ref, g_ref, dc_ref, w_ref = refs[:4]
        tn_a, tn_b = refs[4:4 + npair], refs[4 + npair:4 + 2 * npair]
        da_ref, dg_ref, gw_ref, gb_ref = refs[4 + 2 * npair:8 + 2 * npair]
        tn_o = refs[8 + 2 * npair:8 + 3 * npair]
        gp, dp = refs[8 + 3 * npair:]
        for ta, tb, to in zip(tn_a, tn_b, tn_o):
            to[...] = _dot_tn(ta[...], tb[...]).astype(BF16)
        gp[0:32, :] = jnp.zeros((32, BLK), F32)
        dp[p:p + 32, :] = jnp.zeros((32, BLK), F32)
        bsum = jnp.zeros((BLK, BLK), F32)
        for ci in range(nch):
            r0 = BLK * ci
            glu = a_ref[r0:r0 + BLK, :].astype(F32) * jax.nn.sigmoid(g_ref[r0:r0 + BLK, :].astype(F32))
            if ci == 0:
                glu = jnp.where(_rows(0, BLK) >= PAD, glu, 0.0)
            gp[32 + r0:32 + r0 + BLK, :] = glu
            d = dc_ref[r0:r0 + BLK, :]
            dp[r0:r0 + BLK, :] = d
            bsum = bsum + d
        gb_ref[...] = jnp.sum(bsum, axis=0, keepdims=True)
        for ci in range(nch):
            r0 = BLK * ci
            acc = jnp.zeros((BLK, BLK), F32)
            for j in range(CONV_K):
                acc = acc + w_ref[j:j + 1, :] * dp[r0 + 30 - j:r0 + 30 - j + BLK, :]
            if ci == 0:
                acc = jnp.where(_rows(0, BLK) >= PAD, acc, 0.0)
            a = a_ref[r0:r0 + BLK, :].astype(F32)
            sg = jax.nn.sigmoid(g_ref[r0:r0 + BLK, :].astype(F32))
            da_ref[r0:r0 + BLK, :] = (acc * sg).astype(BF16)
            dg_ref[r0:r0 + BLK, :] = (acc * a * sg * (1.0 - sg)).astype(BF16)
        sub = BLK // 2
        accs = [jnp.zeros((8, BLK), F32) for _ in range(CONV_K)]
        for r0 in range(0, p, sub):
            d = dp[r0:r0 + sub, :]
            for j in range(CONV_K):
                prod = d * gp[r0 + j + 2:r0 + j + 2 + sub, :]
                accs[j] = accs[j] + jnp.sum(prod.reshape(sub // 8, 8, BLK), axis=0)
        for j in range(CONV_K):
            gw_ref[j:j + 1, :] = jnp.sum(accs[j], axis=0, keepdims=True)
        gw_ref[CONV_K:32, :] = jnp.zeros((32 - CONV_K, BLK), F32)

    slab = pl.BlockSpec((p, BLK), lambda j: (0, j))
    return _call(
        body, name="conv31_bwd", grid=(D // BLK,),
        in_specs=[slab, pl.BlockSpec((p, BLK), lambda j: (0, 8 + j)), slab, pl.BlockSpec((32, BLK), lambda j: (0, j))]
        + [slab] * npair + [VM] * npair,
        out_specs=[slab, slab, pl.BlockSpec((32, BLK), lambda j: (0, j)), pl.BlockSpec((1, BLK), lambda j: (0, j))]
        + [pl.BlockSpec((BLK, D), lambda j: (j, 0))] * npair,
        out_shape=[jax.ShapeDtypeStruct((p, D), BF16)] * 2 + [jax.ShapeDtypeStruct((32, D), F32),
                                                             jax.ShapeDtypeStruct((1, D), F32)]
        + [jax.ShapeDtypeStruct((D, D), BF16)] * npair,
        scratch=[pltpu.VMEM((p + 32, BLK), F32)] * 2,
        args=(ag, ag, dc0, w32, *[a for a, _ in tn_pairs], *[b for _, b in tn_pairs]), comm=comm)


def _attn_bwd(q, kv, dao, sinks, tabs, comm=None):
    p = q.shape[0]
    nb = p // BLK

    def body(q_ref, km_ref, kp_ref, kc_ref, do_ref, sink_ref, t_ref, dqkv_ref, dsink_ref, carry, macc):
        i = pl.program_id(0)
        n = nb - 1 - i

        @pl.when(i == 0)
        def _():
            carry[...] = jnp.zeros_like(carry)
            macc[...] = jnp.zeros_like(macc)
            dsink_ref[...] = jnp.zeros_like(dsink_ref)

        lo = lax.broadcasted_iota(jnp.int32, (BLK, BLK), 1) < HEAD_DIM
        lane8 = lax.broadcasted_iota(jnp.int32, (8, BLK), 1)
        c, s1, s2 = t_ref[:, 0:128], -t_ref[:, 128:256], -t_ref[:, 256:384]
        dk = jnp.zeros((N_KEY, BLK), F32)
        dv = jnp.zeros((N_KEY, BLK), F32)
        for h in range(2):
            qs, k2, v2, bias, lok = _attn_setup(n, h, q_ref, km_ref, kp_ref, kc_ref)
            dos = _stack_heads(do_ref, h, lo)
            st = _dot_nt(k2, qs)
            dpt = _dot_nt(v2, dos)
            p_parts, ds_parts = [], []
            for g in range(8):
                cols = slice(BLK * g, BLK * (g + 1))
                pn, ps = _attn_head(st[:, cols], bias, sink_ref[0, 8 * h + g])
                dp = dpt[:, cols]
                delta = jnp.sum(pn * dp, axis=0, keepdims=True)
                ds_parts.append((pn * (dp - delta)).astype(BF16))
                p_parts.append(pn.astype(BF16))
                dsk = -jnp.sum(ps * delta, axis=1, keepdims=True)
                dsink_ref[...] += jnp.where(lane8 == 8 * h + g, dsk, 0.0)
            dst = jnp.concatenate(ds_parts, axis=1)
            pt = jnp.concatenate(p_parts, axis=1)
            dq = _dot_tn(dst, k2)
            for jp in range(4):
                lo_c = BLK * (4 * h + jp)
                dqkv_ref[:, lo_c:lo_c + BLK] = (_rope(_unstack_heads(dq, jp, lo), c, s1, s2) * SCALE).astype(BF16)
            dk2 = _dot(dst, qs)
            dv2 = _dot(pt, dos)
            dk2 = dk2 + pltpu.roll(dk2, HEAD_DIM, 1)
            dv2 = dv2 + pltpu.roll(dv2, HEAD_DIM, 1)
            own = lok if h == 0 else jnp.logical_not(lok)
            dk = jnp.where(own, dk2, dk)
            dv = jnp.where(own, dv2, dv)
        macc[:, 0:BLK] += dk[2 * BLK:N_KEY]
        macc[:, BLK:2 * BLK] += dv[2 * BLK:N_KEY]
        last = (n == 0).astype(F32)
        zpad = jnp.zeros((PAD, BLK), F32)
        dk_c = dk[BLK:2 * BLK] + carry[:, 0:BLK] + last * jnp.concatenate([zpad, macc[:, 0:BLK]], axis=0)
        dv_c = dv[BLK:2 * BLK] + carry[:, BLK:2 * BLK] + last * jnp.concatenate([zpad, macc[:, BLK:2 * BLK]], axis=0)
        carry[:, 0:BLK] = dk[0:BLK]
        carry[:, BLK:2 * BLK] = dv[0:BLK]
        dqkv_ref[:, D:D + BLK] = _rope(dk_c, c, s1, s2).astype(BF16)
        dqkv_ref[:, D + BLK:D + 2 * BLK] = dv_c.astype(BF16)

    def rev(w):
        return pl.BlockSpec((BLK, w), lambda i: (nb - 1 - i, 0))

    return _call(
        body, name="attn_bwd", grid=(nb,),
        in_specs=[rev(D),
                  pl.BlockSpec((BLK, 256), lambda i: (0, 0)),
                  pl.BlockSpec((BLK, 256), lambda i: (jnp.maximum(nb - 2 - i, 0), 0)),
                  rev(256), rev(D),
                  pl.BlockSpec(memory_space=pltpu.SMEM), rev(384)],
        out_specs=[rev(QKV_W), pl.BlockSpec((8, BLK), lambda i: (0, 0))],
        out_shape=[jax.ShapeDtypeStruct((p, QKV_W), BF16), jax.ShapeDtypeStruct((8, BLK), F32)],
        scratch=[pltpu.VMEM((BLK, 256), F32), pltpu.VMEM((N_META, 256), F32)], sem="arbitrary",
        args=(q, kv, kv, kv, dao, sinks, tabs), comm=comm)


def _in_bwd(dqkv, da, dg, dgt, w_int, h0p, dh1, gain, comm=None):
    p = h0p.shape[0]
    tm = _row_tile(p)
    nt = p // tm
    first_rows = tm - BLK

    def body(dq_ref, da_ref, dg_ref, dt_ref, w_ref, h_ref, dh_ref, g_ref, gx_ref, dm_ref, acc_ref, buf, sems):
        i = pl.program_id(0)
        slot = i % 2

        @pl.when(i == 0)
        def _():
            acc_ref[...] = jnp.zeros_like(acc_ref)

        dn = (_dot(dq_ref[...], w_ref[0:QKV_W, :]) + _dot(da_ref[...], w_ref[QKV_W:QKV_W + D, :])
              + _dot(dg_ref[...], w_ref[QKV_W + D:QKV_W + 2 * D, :]) + _dot(dt_ref[...], w_ref[QKV_W + 2 * D:IN_W, :]))
        dh, dgain = _rms_bwd(h_ref[...], g_ref[...], dn)
        dh0 = dh_ref[...] + dh
        acc_ref[0:1, :] += dgain
        buf[slot] = dh0

        @pl.when(i == 0)
        def _():
            dm_ref[...] = dh0[PAD:BLK]

        def first_copy():
            return pltpu.make_async_copy(buf.at[0, pl.ds(BLK, first_rows), :], gx_ref.at[pl.ds(0, first_rows), :], sems.at[0])

        def tile_copy(j, s):
            return pltpu.make_async_copy(buf.at[s], gx_ref.at[pl.ds(pl.multiple_of(j * tm - BLK, BLK), tm), :], sems.at[s])

        if first_rows:
            @pl.when(i == 1)
            def _():
                first_copy().wait()

        @pl.when(i >= 2)
        def _():
            tile_copy(i - 1, 1 - slot).wait()

        if first_rows:
            @pl.when(i == 0)
            def _():
                first_copy().start()

        @pl.when(i > 0)
        def _():
            tile_copy(i, slot).start()

        @pl.when(i == nt - 1)
        def _():
            tile_copy(i, slot).wait()

    def row(w):
        return pl.BlockSpec((tm, w), lambda i: (i, 0))

    return _call(
        body, name="in_bwd", grid=(nt,),
        in_specs=[row(QKV_W), row(D), row(D), row(2 * D), VM, row(D), row(D), VM],
        out_specs=[ANY, pl.BlockSpec((N_META, D), lambda i: (0, 0)), pl.BlockSpec((8, D), lambda i: (0, 0))],
        out_shape=[jax.ShapeDtypeStruct((p - BLK, D), F32), jax.ShapeDtypeStruct((N_META, D), F32),
                   jax.ShapeDtypeStruct((8, D), F32)],
        scratch=[pltpu.VMEM((2, tm, D), F32), pltpu.SemaphoreType.DMA((2,))],
        sem="arbitrary", args=(dqkv, da, dg, dgt, w_int, h0p, dh1, gain), comm=comm)


def _sum_slots(slots, name):
    r = slots.shape[0] // N_DEV
    cols = slots.shape[1]
    tr = r if r <= 352 else (r // 2 if (r // 2) % 16 == 0 else r // 3)
    steps = r // tr

    def body(*refs):
        acc = refs[0][...].astype(F32)
        for s in range(1, N_DEV):
            acc = acc + refs[s][...].astype(F32)
        refs[N_DEV][...] = acc

    return pl.pallas_call(
        body, name=name, grid=(steps,),
        in_specs=[pl.BlockSpec((tr, cols), functools.partial(lambda i, s: (s * steps + i, 0), s=s)) for s in range(N_DEV)],
        out_specs=pl.BlockSpec((tr, cols), lambda i: (i, 0)),
        out_shape=jax.ShapeDtypeStruct((r, cols), F32),
        compiler_params=_cparams("parallel"),
    )(*([slots] * N_DEV))


def _adamw_math(w, g, m, v):
    m_n = ADAM_B1 * m + (1.0 - ADAM_B1) * g
    v_n = ADAM_B2 * v + (1.0 - ADAM_B2) * jnp.square(g)
    m_hat = m_n / (1.0 - ADAM_B1 ** ADAM_STEP)
    v_hat = v_n / (1.0 - ADAM_B2 ** ADAM_STEP)
    return -ADAM_LR * (m_hat / (jnp.sqrt(v_hat) + ADAM_EPS) + ADAM_WD * w), m_n, v_n


def _sum_adamw(parts, w, m, v, name, nslots=N_DEV):
    r, cols = w.shape
    rs = r // len(parts)
    tr = rs if rs <= 352 else (rs // 2 if (rs // 2) % 16 == 0 else rs // 3)
    steps = rs // tr

    def body(*refs):
        w_ref, m_ref, v_ref, g_ref, d_ref, nm_ref, nv_ref = refs[nslots * len(parts):]
        i = pl.program_id(0)
        for q in range(len(parts)):
            @pl.when(i // steps == q)
            def _(q=q):
                g = refs[nslots * q][...].astype(F32)
                for s in range(1, nslots):
                    g = g + refs[nslots * q + s][...].astype(F32)
                g_ref[...] = g
                d_ref[...], nm_ref[...], nv_ref[...] = _adamw_math(w_ref[...], g, m_ref[...], v_ref[...])

    def slot_spec(q, s):
        return pl.BlockSpec((tr, cols), lambda i: (s * steps + jnp.clip(i - q * steps, 0, steps - 1), 0))

    spec = pl.BlockSpec((tr, cols), lambda i: (i, 0))
    return pl.pallas_call(
        body, name=name, grid=(steps * len(parts),),
        in_specs=[slot_spec(q, s) for q in range(len(parts)) for s in range(nslots)] + [spec] * 3,
        out_specs=[spec] * 4, out_shape=[jax.ShapeDtypeStruct((r, cols), F32)] * 4,
        compiler_params=_cparams("parallel"),
    )(*[a for a in parts for _ in range(nslots)], w, m, v)


def _adamw(w, g, m, v, name):
    r, cols = w.shape
    tr = 256 if r % 256 == 0 else r

    def body(w_ref, g_ref, m_ref, v_ref, d_ref, nm_ref, nv_ref):
        d_ref[...], nm_ref[...], nv_ref[...] = _adamw_math(w_ref[...], g_ref[...], m_ref[...], v_ref[...])

    spec = pl.BlockSpec((tr, cols), lambda i: (i, 0))
    return pl.pallas_call(
        body, name=name, grid=(r // tr,),
        in_specs=[spec] * 4, out_specs=[spec] * 3,
        out_shape=[jax.ShapeDtypeStruct((r, cols), F32)] * 3,
        compiler_params=_cparams("parallel"),
    )(w, g, m, v)


def _rope_tables(p):
    half = ROT_DIM // 2
    inv_freq = ROPE_THETA ** (-jnp.arange(half, dtype=F32) * 2.0 / ROT_DIM)
    pos = (jnp.arange(p) - PAD).astype(F32)
    ang = pos[:, None] * inv_freq[None, :]
    lane = jnp.arange(BLK)
    seg = (lane % HEAD_DIM) // half
    cos = jnp.cos(ang)[:, lane % half]
    sin = jnp.sin(ang)[:, lane % half]
    c = jnp.where(seg[None, :] < 2, cos, 1.0)
    s1 = jnp.where(seg[None, :] == 0, -sin, 0.0)
    s2 = jnp.where(seg[None, :] == 1, sin, 0.0)
    return jnp.concatenate([c, s1, s2], axis=1).astype(F32)


def _flat_pack(parts, rows):
    flat = jnp.concatenate([a.reshape(-1).astype(F32) for a in parts])
    return jnp.pad(flat, (0, rows * D - flat.shape[0])).reshape(rows, D)


def _flat_unpack(pack, shapes):
    flat = pack.reshape(-1)
    out, off = [], 0
    for s in shapes:
        size = 1
        for e in s:
            size *= e
        out.append(flat[off:off + size].reshape(s))
        off += size
    return out


def kernel(x, meta_tokens, norm_pre_mix, norm_post_mix, w_in, b_in, attn_sinks, w_attn_proj, conv_dw_w, conv_dw_b, conv_ln_g, conv_ln_b, w_conv_proj, b_conv_proj, w_out, norm_pre_ffn, norm_post_ffn, w_up, ffn_dw_w, ffn_dw_b, w_down, loss_target, m_meta_tokens, m_norm_pre_mix, m_norm_post_mix, m_w_in, m_b_in, m_attn_sinks, m_w_attn_proj, m_conv_dw_w, m_conv_dw_b, m_conv_ln_g, m_conv_ln_b, m_w_conv_proj, m_b_conv_proj, m_w_out, m_norm_pre_ffn, m_norm_post_ffn, m_w_up, m_ffn_dw_w, m_ffn_dw_b, m_w_down, v_meta_tokens, v_norm_pre_mix, v_norm_post_mix, v_w_in, v_b_in, v_attn_sinks, v_w_attn_proj, v_conv_dw_w, v_conv_dw_b, v_conv_ln_g, v_conv_ln_b, v_w_conv_proj, v_b_conv_proj, v_w_out, v_norm_pre_ffn, v_norm_post_ffn, v_w_up, v_ffn_dw_w, v_ffn_dw_b, v_w_down):
    seq = x.shape[1]
    p = seq + BLK
    me = 4 * lax.axis_index("x") + 2 * lax.axis_index("y") + lax.axis_index("c")
    in_cols = w_in.shape[2]
    up_cols = w_up.shape[2]

    small = jnp.zeros((56, up_cols), F32)
    small = small.at[0:N_META, 0:BLK].set(meta_tokens)
    small = small.at[16:16 + CONV_K, 0:BLK].set(conv_dw_w[0])
    small = small.at[48:48 + FFN_K, :].set(ffn_dw_w[0])
    w_int, small_all = _exchange(_Both(_GatherRelay(w_in[0].T.astype(BF16)), _Gather([small])), "gather_w_in")
    small_all = small_all.reshape(N_DEV, 56, up_cols)
    meta_full = small_all[:, 0:N_META, 0:BLK].transpose(1, 0, 2).reshape(N_META, D)
    cdw = small_all[:, 16:16 + CONV_K, 0:BLK].transpose(1, 0, 2).reshape(CONV_K, D)
    cdw32 = jnp.pad(cdw, ((0, 32 - CONV_K), (0, 0)))
    fdw = small_all[:, 48:48 + FFN_K, :].transpose(1, 0, 2).reshape(FFN_K, 2 * FFN)

    tabs = _rope_tables(p)
    vecs = jnp.concatenate([conv_ln_g, conv_ln_b, b_conv_proj, norm_post_mix, norm_pre_ffn, jnp.zeros((3, D), F32)], axis=0)

    (h0p, n1, q, kv, ag, gates), (wa, wc, wo) = _in_proj(
        x[0], meta_full, norm_pre_mix, w_int, b_in, tabs,
        comm=_Gather([w_attn_proj[0].astype(BF16), w_conv_proj[0].astype(BF16), w_out[0].astype(BF16)]))
    (ao,), (w_upt,) = _attn_fwd(q, kv, attn_sinks, comm=_Gather([w_up[0].T.astype(BF16)]))
    (c0,), (wd,) = _conv31_fwd(ag, cdw32, conv_dw_b, comm=_Gather([w_down[0].astype(BF16)]))
    c1, attn, conv, merged, mix, h1, n2 = _mixer_fwd(ao, c0, gates, h0p, wa, wc, wo, vecs)
    u0 = _mm_nt(n2, w_upt, "ffn_up")
    act, dact_dv, dact_dg = _ffn_act(u0, fdw, ffn_dw_b)
    dffn, dact, dy, acc_f = _ffn_down_loss(act, wd, h1, loss_target[0], norm_post_ffn)

    (dug, duv, gfw_g, gfw_v, gfb_g, gfb_v, g_wd), _ = _ffn_act_bwd(u0, dact, dact_dg, dact_dv, fdw, act, dffn)
    g_wupt, (s_wd0,) = _mm_tn([dug, duv], n2, "grad_w_up", comm=_Scatter([g_wd], 0, 2))
    (dh1, acc_u), (s_wd1,) = _ffn_in_bwd(dug, duv, w_upt, h1, dy, norm_pre_ffn, comm=_Scatter([g_wd], 1, 2))
    (dmix, dat, dcv, dgt, dao, dc0, acc_m), (s_wup0,) = _mixer_bwd(
        dh1, mix, attn, conv, gates, c0, wa, wc, wo, vecs, comm=_Scatter([g_wupt], 0, 4))
    (da, dg, g_cdw, g_cdb, g_wo, g_wa, g_wc), (s_wup1, s_wup2, s_wup3) = _conv31_bwd(
        ag, dc0, cdw32, [(merged, dmix), (ao, dat), (c1, dcv)],
        comm=_Both(_Both(_Scatter([g_wupt], 1, 4), _Scatter([g_wupt], 2, 4)), _Scatter([g_wupt], 3, 4)))
    (dqkv, dsink), (s_wa, s_wc, s_wo) = _attn_bwd(q, kv, dao, attn_sinks, tabs, comm=_Scatter([g_wa, g_wc, g_wo]))
    loss_row = jnp.sum(acc_f[1:2, :], axis=1, keepdims=True)
    early = [loss_row, acc_m[0:1], dsink[0:1, 0:16], g_cdw[0:CONV_K], g_cdb,
             acc_m[2:3], acc_m[3:4], acc_m[1:2], acc_u[0:1], acc_f[0:1],
             jnp.concatenate([gfw_g, gfw_v], axis=1), jnp.concatenate([gfb_g, gfb_v], axis=1)]
    (g_wint, g_bin), (gathered_early,) = _mm_tn([dqkv, da, dg, dgt], n1, "grad_w_in", col_sums=True,
                                                comm=_Gather([_flat_pack(early, 64)]))
    (from_sibling,) = _exchange(_SiblingSwap(g_wint), "swap_w_in")
    (grad_x2d, dmeta, acc_i), (s_win,) = _in_bwd(dqkv, da, dg, dgt, w_int, h0p, dh1, norm_pre_mix,
                                                 comm=_ChipScatter(_pair_add(g_wint, from_sibling)))

    big = []
    for nm, parts, nslots, w, m, v, tr in (
            ("w_in", [s_win], N_CHIP, w_in, m_w_in, v_w_in, True), ("w_up", [s_wup0, s_wup1, s_wup2, s_wup3], N_DEV, w_up, m_w_up, v_w_up, True),
            ("w_attn_proj", [s_wa], N_DEV, w_attn_proj, m_w_attn_proj, v_w_attn_proj, False),
            ("w_conv_proj", [s_wc], N_DEV, w_conv_proj, m_w_conv_proj, v_w_conv_proj, False),
            ("w_out", [s_wo], N_DEV, w_out, m_w_out, v_w_out, False),
            ("w_down", [s_wd0, s_wd1], N_DEV, w_down, m_w_down, v_w_down, False)):
        ins = [a[0].T if tr else a[0] for a in (w, m, v)]
        big.append(tuple((o.T if tr else o)[None] for o in _sum_adamw(parts, *ins, "update_" + nm, nslots)))

    late = [dmeta, acc_i[0:1], g_bin]
    (gathered_late,) = _exchange(_Gather([_flat_pack(late, 24)]), "gather_small_grads")
    g_meta, g_npm, g_bi = _flat_unpack(_sum_slots(gathered_late, "sum_late_grads"), [a.shape for a in late])
    tot = _flat_unpack(_sum_slots(gathered_early, "sum_small_grads"), [a.shape for a in early])
    (loss, g_nqm, g_sk, g_cw, g_cb, g_lg, g_lb, g_bc, g_npf, g_nqf, g_fw, g_fb) = tot
    loss = loss.reshape(())
    g_meta = lax.dynamic_slice_in_dim(g_meta, me * BLK, BLK, axis=1)
    g_cw = lax.dynamic_slice_in_dim(g_cw, me * BLK, BLK, axis=1)[None]
    g_fw = lax.dynamic_slice_in_dim(g_fw, me * up_cols, up_cols, axis=1)[None]

    sm_w = [meta_tokens, norm_pre_mix, norm_post_mix, b_in, attn_sinks, conv_dw_w, conv_dw_b, conv_ln_g, conv_ln_b,
            b_conv_proj, norm_pre_ffn, norm_post_ffn, ffn_dw_w, ffn_dw_b]
    sm_g = [g_meta, g_npm, g_nqm, g_bi, g_sk, g_cw, g_cb, g_lg, g_lb, g_bc, g_npf, g_nqf, g_fw, g_fb]
    sm_m = [m_meta_tokens, m_norm_pre_mix, m_norm_post_mix, m_b_in, m_attn_sinks, m_conv_dw_w, m_conv_dw_b, m_conv_ln_g,
            m_conv_ln_b, m_b_conv_proj, m_norm_pre_ffn, m_norm_post_ffn, m_ffn_dw_w, m_ffn_dw_b]
    sm_v = [v_meta_tokens, v_norm_pre_mix, v_norm_post_mix, v_b_in, v_attn_sinks, v_conv_dw_w, v_conv_dw_b, v_conv_ln_g,
            v_conv_ln_b, v_b_conv_proj, v_norm_pre_ffn, v_norm_post_ffn, v_ffn_dw_w, v_ffn_dw_b]
    sm_shapes = [a.shape for a in sm_w]
    upd_rows = 32
    v_pack = _flat_pack(sm_v, upd_rows)
    sm_out = _adamw(_flat_pack(sm_w, upd_rows), _flat_pack(sm_g, upd_rows), _flat_pack(sm_m, upd_rows), v_pack, "adamw_small")
    sm_d, sm_nm, sm_nv = (_flat_unpack(o, sm_shapes) for o in sm_out)

    order = ["meta_tokens", "norm_pre_mix", "norm_post_mix", "w_in", "b_in", "attn_sinks", "w_attn_proj", "conv_dw_w",
             "conv_dw_b", "conv_ln_g", "conv_ln_b", "w_conv_proj", "b_conv_proj", "w_out", "norm_pre_ffn", "norm_post_ffn",
             "w_up", "ffn_dw_w", "ffn_dw_b", "w_down"]
    small_names = ["meta_tokens", "norm_pre_mix", "norm_post_mix", "b_in", "attn_sinks", "conv_dw_w", "conv_dw_b", "conv_ln_g",
                   "conv_ln_b", "b_conv_proj", "norm_pre_ffn", "norm_post_ffn", "ffn_dw_w", "ffn_dw_b"]
    big_names = ["w_in", "w_up", "w_attn_proj", "w_conv_proj", "w_out", "w_down"]
    table = {}
    for k, nm in enumerate(small_names):
        table[nm] = (sm_g[k], sm_d[k], sm_nm[k], sm_nv[k])
    for k, nm in enumerate(big_names):
        table[nm] = big[k]
    grad_x = grad_x2d[None]
    outs = [loss, grad_x]
    for field in range(4):
        outs += [table[nm][field] for nm in order]
    return tuple(outs)
```

```python
import functools

import jax
import jax.numpy as jnp
from jax import lax
from jax.experimental import pallas as pl
from jax.experimental.pallas import tpu as pltpu

F32 = jnp.float32
BF16 = jnp.bfloat16
MESH = pl.DeviceIdType.MESH

D = 1024
HEAD_DIM = 64
N_META = 16
BLK = 128
PAD = BLK - N_META
CONV_K = 31
FFN = 2816
FFN_K = 3
QKV_W = 1280
IN_W = 5376
ROT_DIM = 16
ROPE_THETA = 500000.0
RMS_EPS = 1e-6
LN_EPS = 1e-5
NEG_INF = -1e30
SCALE = HEAD_DIM ** -0.5
N_DEV = 8

ADAM_LR = 0.001
ADAM_B1 = 0.9
ADAM_B2 = 0.999
ADAM_EPS = 1e-08
ADAM_WD = 0.01
ADAM_STEP = 10

VMEM_BYTES_V7X = 64 * 1024 * 1024
VMEM_LIMIT = VMEM_BYTES_V7X - 8 * 1024 * 1024

NT = (((1,), (1,)), ((), ()))
TN = (((0,), (0,)), ((), ()))
VM = pl.BlockSpec(memory_space=pltpu.VMEM)
ANY = pl.BlockSpec(memory_space=pl.ANY)


def _cparams(*sem):
    return pltpu.CompilerParams(dimension_semantics=sem or None, vmem_limit_bytes=VMEM_LIMIT)


def _row_tile(p):
    return 384 if p % 384 == 0 else 128


def _dot(a, b):
    return jnp.dot(a, b, preferred_element_type=F32)


def _dot_nt(a, b):
    return lax.dot_general(a, b, NT, preferred_element_type=F32)


def _dot_tn(a, b):
    return lax.dot_general(a, b, TN, preferred_element_type=F32)


def _rms(x, g):
    return x * lax.rsqrt(jnp.mean(x * x, axis=-1, keepdims=True) + RMS_EPS) * g


def _lnsilu(x, g, b):
    mu = jnp.mean(x, axis=-1, keepdims=True)
    var = jnp.mean(jnp.square(x - mu), axis=-1, keepdims=True)
    z = (x - mu) * lax.rsqrt(var + LN_EPS) * g + b
    return z * jax.nn.sigmoid(z)


def _rms_bwd(x, g, dy):
    r = lax.rsqrt(jnp.mean(x * x, axis=-1, keepdims=True) + RMS_EPS)
    xn = x * r
    u = dy * g
    dg = jnp.sum(dy * xn, axis=0, keepdims=True)
    dx = r * (u - xn * jnp.mean(u * xn, axis=-1, keepdims=True))
    return dx, dg


def _lnsilu_bwd(x, g, b, dout):
    mu = jnp.mean(x, axis=-1, keepdims=True)
    xc = x - mu
    rs = lax.rsqrt(jnp.mean(xc * xc, axis=-1, keepdims=True) + LN_EPS)
    yh = xc * rs
    z = yh * g + b
    sg = jax.nn.sigmoid(z)
    dz = dout * (sg * (1.0 + z * (1.0 - sg)))
    dg = jnp.sum(dz * yh, axis=0, keepdims=True)
    db = jnp.sum(dz, axis=0, keepdims=True)
    dyh = dz * g
    dx = rs * (dyh - jnp.mean(dyh, axis=-1, keepdims=True) - yh * jnp.mean(dyh * yh, axis=-1, keepdims=True))
    return dx, dg, db


def _rope(v, c, s1, s2):
    return v * c + pltpu.roll(v, BLK - 8, 1) * s1 + pltpu.roll(v, 8, 1) * s2


def _rows(i, tm):
    return i * tm + lax.broadcasted_iota(jnp.int32, (tm, 1), 0)


def _place():
    return lax.axis_index("x"), lax.axis_index("y"), lax.axis_index("c")


def _blk(ref, idx, r, dtype):
    return ref.at[pl.ds(pl.multiple_of(idx * r, 16 if dtype == BF16 else 8), r), :]


class _Gather:
    def __init__(self, arrs):
        self.ins = list(arrs)
        n = len(arrs)
        self.out_shape = [jax.ShapeDtypeStruct((N_DEV * a.shape[0], a.shape[1]), a.dtype) for a in arrs]
        self.scratch = [pltpu.SemaphoreType.DMA((n, 7)), pltpu.SemaphoreType.DMA((n, 7)), pltpu.SemaphoreType.DMA((n,))]

    def _parts(self, ins, outs, sems):
        send_sems, recv_sems, local_sems = sems
        n = len(ins)
        x, y, c = _place()
        me, sibling = (x, y, c), (x, y, 1 - c)
        chips = [(1 - x, y), (x, 1 - y), (1 - x, 1 - y)]

        def rows(a, p):
            return _blk(outs[a], 4 * p[0] + 2 * p[1] + p[2], self.ins[a].shape[0], self.ins[a].dtype)

        def copy(a, k, block, to, src=None):
            return pltpu.make_async_remote_copy(
                src_ref=rows(a, block) if src is None else src, dst_ref=rows(a, block),
                send_sem=send_sems.at[a, k], recv_sem=recv_sems.at[a, k], device_id=to, device_id_type=MESH)

        mine = [pltpu.make_async_copy(ins[a], rows(a, me), local_sems.at[a]) for a in range(n)]
        first = []
        for a in range(n):
            first.append(copy(a, 0, me, sibling, src=ins[a]))
            first += [copy(a, 1 + j, me, (*chip, c), src=ins[a]) for j, chip in enumerate(chips)]
        return n, c, me, sibling, chips, copy, mine, first

    def start(self, ins, outs, sems):
        *_, mine, first = self._parts(ins, outs, sems)
        for cp in mine + first:
            cp.start()

    def finish(self, ins, outs, sems):
        n, c, me, sibling, chips, copy, mine, first = self._parts(ins, outs, sems)
        passed = []
        for j, chip in enumerate(chips):
            for a in range(n):
                copy(a, 1 + j, (*chip, c), me).wait_recv()
                fwd = copy(a, 4 + j, (*chip, c), sibling)
                fwd.start()
                passed.append(fwd)
        for a in range(n):
            copy(a, 0, sibling, me).wait_recv()
            for j, chip in enumerate(chips):
                copy(a, 4 + j, (*chip, 1 - c), me).wait_recv()
        for cp in first + passed:
            cp.wait_send()
        for cp in mine:
            cp.wait()


class _GatherRelay:
    def __init__(self, arr):
        self.ins = [arr]
        self.r = arr.shape[0]
        self.out_shape = [jax.ShapeDtypeStruct((N_DEV * self.r, arr.shape[1]), arr.dtype)]
        self.scratch = [pltpu.SemaphoreType.DMA((9,)), pltpu.SemaphoreType.DMA((9,)), pltpu.SemaphoreType.DMA]

    def _parts(self, ins, outs, sems):
        send_sems, recv_sems, local_sem = sems
        x, y, c = _place()
        r, half = self.r, self.r // 2
        out = outs[0]
        me, sib, xn, yn = (x, y, c), (x, y, 1 - c), (1 - x, y, c), (x, 1 - y, c)
        dg = (1 - x, 1 - y, c)

        def rows(p, lo=0, n=r):
            return out.at[pl.ds(pl.multiple_of((4 * p[0] + 2 * p[1] + p[2]) * r + lo, 16), n), :]

        def copy(k, dev_rows, to, src=None):
            return pltpu.make_async_remote_copy(
                src_ref=dev_rows if src is None else src, dst_ref=dev_rows,
                send_sem=send_sems.at[k], recv_sem=recv_sems.at[k], device_id=to, device_id_type=MESH)

        mine = pltpu.make_async_copy(ins[0], rows(me), local_sem)
        first = [copy(0, rows(me), sib, src=ins[0]), copy(1, rows(me), xn, src=ins[0]), copy(2, rows(me), yn, src=ins[0])]
        arrive = {0: rows(sib), 1: rows(xn), 2: rows(yn), 3: rows(dg, 0, half), 4: rows(dg, half, half),
                  5: rows((1 - x, y, 1 - c)), 6: rows((x, 1 - y, 1 - c)),
                  7: rows((1 - x, 1 - y, 1 - c), 0, half), 8: rows((1 - x, 1 - y, 1 - c), half, half)}
        relay = {1: [(3, rows(xn, 0, half), yn), (5, rows(xn), sib)],
                 2: [(4, rows(yn, half, half), xn), (6, rows(yn), sib)],
                 3: [(7, rows(dg, 0, half), sib)], 4: [(8, rows(dg, half, half), sib)]}
        return copy, mine, first, arrive, relay, me

    def start(self, ins, outs, sems):
        _, mine, first, _, _, _ = self._parts(ins, outs, sems)
        for cp in [mine] + first:
            cp.start()

    def finish(self, ins, outs, sems):
        copy, mine, first, arrive, relay, me = self._parts(ins, outs, sems)
        passed = []
        for k in (1, 2, 3, 4):
            copy(k, arrive[k], me).wait_recv()
            for k2, dev_rows, to in relay[k]:
                fwd = copy(k2, dev_rows, to)
                fwd.start()
                passed.append(fwd)
        for k in (0, 5, 6, 7, 8):
            copy(k, arrive[k], me).wait_recv()
        for cp in first + passed:
            cp.wait_send()
        mine.wait()


FLIPS = [(0, 0, 1), (1, 0, 0), (0, 1, 0), (1, 1, 0), (1, 0, 1), (0, 1, 1), (1, 1, 1)]


class _Scatter:
    def __init__(self, arrs, part=0, nparts=1):
        self.ins = list(arrs)
        self.part, self.nparts = part, nparts
        n = len(arrs)
        self.out_shape = [jax.ShapeDtypeStruct((a.shape[0] // nparts, a.shape[1]), a.dtype) for a in arrs]
        self.scratch = [pltpu.SemaphoreType.DMA((n, 7)), pltpu.SemaphoreType.DMA((n, 7)), pltpu.SemaphoreType.DMA((n,))]

    def _parts(self, ins, outs, sems):
        send_sems, recv_sems, local_sems = sems
        n = len(ins)
        x, y, c = _place()
        me = 4 * x + 2 * y + c

        def flip(v, f):
            return 1 - v if f else v

        def src(a, idx):
            r = self.ins[a].shape[0] // N_DEV
            rs = r // self.nparts
            return ins[a].at[pl.ds(pl.multiple_of(idx * r + self.part * rs, 16), rs), :]

        def dst(a, idx):
            rs = self.ins[a].shape[0] // N_DEV // self.nparts
            return outs[a].at[pl.ds(pl.multiple_of(idx * rs, 16), rs), :]

        mine = [pltpu.make_async_copy(src(a, me), dst(a, me), local_sems.at[a]) for a in range(n)]
        sends, recvs = [], []
        for k, f in enumerate(FLIPS):
            peer = (flip(x, f[0]), flip(y, f[1]), flip(c, f[2]))
            pidx = 4 * peer[0] + 2 * peer[1] + peer[2]
            for a in range(n):
                sends.append(pltpu.make_async_remote_copy(
                    src_ref=src(a, pidx), dst_ref=dst(a, me),
                    send_sem=send_sems.at[a, k], recv_sem=recv_sems.at[a, k], device_id=peer, device_id_type=MESH))
                recvs.append(functools.partial(
                    pltpu.make_async_remote_copy,
                    src_ref=src(a, pidx), dst_ref=dst(a, pidx),
                    send_sem=send_sems.at[a, k], recv_sem=recv_sems.at[a, k], device_id=peer, device_id_type=MESH))
        return mine, sends, recvs

    def start(self, ins, outs, sems):
        mine, sends, _ = self._parts(ins, outs, sems)
        for cp in mine + sends:
            cp.start()

    def finish(self, ins, outs, sems):
        mine, sends, recvs = self._parts(ins, outs, sems)
        for make in recvs:
            make().wait_recv()
        for cp in sends:
            cp.wait_send()
        for cp in mine:
            cp.wait()


N_CHIP = 4


class _SiblingSwap:
    def __init__(self, arr):
        self.ins = [arr]
        self.r = arr.shape[0] // N_DEV
        self.out_shape = [jax.ShapeDtypeStruct((N_CHIP * self.r, arr.shape[1]), arr.dtype)]
        self.scratch = [pltpu.SemaphoreType.DMA((N_CHIP,)), pltpu.SemaphoreType.DMA((N_CHIP,))]

    def _copies(self, ins, outs, sems):
        send_sems, recv_sems = sems
        x, y, c = _place()
        r = self.r
        return [pltpu.make_async_remote_copy(
            src_ref=ins[0].at[pl.ds(pl.multiple_of((2 * j + 1 - c) * r, 16), r), :],
            dst_ref=outs[0].at[pl.ds(j * r, r), :],
            send_sem=send_sems.at[j], recv_sem=recv_sems.at[j], device_id=(x, y, 1 - c), device_id_type=MESH)
            for j in range(N_CHIP)]

    def start(self, ins, outs, sems):
        for cp in self._copies(ins, outs, sems):
            cp.start()

    def finish(self, ins, outs, sems):
        for cp in self._copies(ins, outs, sems):
            cp.wait()


class _ChipScatter:
    def __init__(self, arr):
        self.ins = [arr]
        self.r = arr.shape[0] // N_CHIP
        self.out_shape = [jax.ShapeDtypeStruct(arr.shape, arr.dtype)]
        self.scratch = [pltpu.SemaphoreType.DMA((3,)), pltpu.SemaphoreType.DMA((3,)), pltpu.SemaphoreType.DMA]

    def _parts(self, ins, outs, sems):
        send_sems, recv_sems, local_sem = sems
        x, y, c = _place()
        r = self.r
        my_chip = 2 * x + y

        def rows(ref, j):
            return ref.at[pl.ds(pl.multiple_of(j * r, 16), r), :]

        mine = pltpu.make_async_copy(rows(ins[0], my_chip), rows(outs[0], my_chip), local_sem)
        sends, recvs = [], []
        for k, (fx, fy) in enumerate(((1, 0), (0, 1), (1, 1))):
            px, py = (1 - x if fx else x), (1 - y if fy else y)
            peer_chip = 2 * px + py
            sends.append(pltpu.make_async_remote_copy(
                src_ref=rows(ins[0], peer_chip), dst_ref=rows(outs[0], my_chip),
                send_sem=send_sems.at[k], recv_sem=recv_sems.at[k], device_id=(px, py, c), device_id_type=MESH))
            recvs.append(functools.partial(
                pltpu.make_async_remote_copy,
                src_ref=rows(ins[0], peer_chip), dst_ref=rows(outs[0], peer_chip),
                send_sem=send_sems.at[k], recv_sem=recv_sems.at[k], device_id=(px, py, c), device_id_type=MESH))
        return mine, sends, recvs

    def start(self, ins, outs, sems):
        mine, sends, _ = self._parts(ins, outs, sems)
        for cp in [mine] + sends:
            cp.start()

    def finish(self, ins, outs, sems):
        mine, sends, recvs = self._parts(ins, outs, sems)
        for make in recvs:
            make().wait_recv()
        for cp in sends:
            cp.wait_send()
        mine.wait()


def _pair_add(partial, recv):
    r = recv.shape[0] // N_CHIP
    cols = recv.shape[1]
    tr = r // 2 if (r // 2) % 16 == 0 else r
    steps = r // tr
    core = lax.axis_index("c").astype(jnp.int32).reshape(1)

    def body(c_ref, p_ref, s_ref, o_ref):
        o_ref[...] = (p_ref[...].astype(F32) + s_ref[...].astype(F32)).astype(BF16)

    spec = pl.BlockSpec((tr, cols), lambda j, i, c_ref: (j * steps + i, 0))
    return pl.pallas_call(
        body, name="pair_add",
        grid_spec=pltpu.PrefetchScalarGridSpec(
            num_scalar_prefetch=1, grid=(N_CHIP, steps),
            in_specs=[pl.BlockSpec((tr, cols), lambda j, i, c_ref: ((2 * j + c_ref[0]) * steps + i, 0)), spec],
            out_specs=spec),
        out_shape=jax.ShapeDtypeStruct(recv.shape, BF16),
        compiler_params=_cparams("parallel", "parallel"),
    )(core, partial, recv)


class _Both:
    def __init__(self, a, b):
        self.a, self.b = a, b
        self.ins = a.ins + b.ins
        self.out_shape = a.out_shape + b.out_shape
        self.scratch = a.scratch + b.scratch

    def _split(self, ins, outs, sems):
        ni, no, ns = len(self.a.ins), len(self.a.out_shape), len(self.a.scratch)
        return (ins[:ni], outs[:no], sems[:ns]), (ins[ni:], outs[no:], sems[ns:])

    def start(self, ins, outs, sems):
        ra, rb = self._split(ins, outs, sems)
        self.a.start(*ra)
        self.b.start(*rb)

    def finish(self, ins, outs, sems):
        ra, rb = self._split(ins, outs, sems)
        self.a.finish(*ra)
        self.b.finish(*rb)


def _exchange(comm, name):
    n, m = len(comm.ins), len(comm.out_shape)

    def body(*refs):
        ins, outs, sems = refs[:n], refs[n:n + m], refs[n + m:]
        comm.start(ins, outs, sems)
        comm.finish(ins, outs, sems)

    return pl.pallas_call(
        body, name=name, out_shape=comm.out_shape, in_specs=[ANY] * n, out_specs=[ANY] * m, scratch_shapes=comm.scratch,
    )(*comm.ins)


def _call(body, *, name, grid, in_specs, out_specs, out_shape, args, scratch=(), sem="parallel", comm=None):
    if comm is None:
        outs = pl.pallas_call(
            body, name=name, grid=grid, in_specs=list(in_specs), out_specs=list(out_specs), out_shape=list(out_shape),
            scratch_shapes=list(scratch), compiler_params=_cparams(sem))(*args)
        return outs, []
    n_in, n_out, n_sc = len(in_specs), len(out_specs), len(scratch)
    n_ci, n_co = len(comm.ins), len(comm.out_shape)
    last = grid[0] - 1

    def fused(*refs):
        ins, refs = refs[:n_in], refs[n_in:]
        c_ins, refs = refs[:n_ci], refs[n_ci:]
        outs, refs = refs[:n_out], refs[n_out:]
        c_outs, refs = refs[:n_co], refs[n_co:]
        sc, c_sems = refs[:n_sc], refs[n_sc:]
        step = pl.program_id(0)

        @pl.when(step == 0)
        def _():
            comm.start(c_ins, c_outs, c_sems)

        body(*ins, *outs, *sc)

        @pl.when(step == last)
        def _():
            comm.finish(c_ins, c_outs, c_sems)

    outs = pl.pallas_call(
        fused, name=name, grid=grid, in_specs=list(in_specs) + [ANY] * n_ci, out_specs=list(out_specs) + [ANY] * n_co,
        out_shape=list(out_shape) + comm.out_shape, scratch_shapes=list(scratch) + comm.scratch,
        compiler_params=_cparams("arbitrary"))(*args, *comm.ins)
    return outs[:n_out], outs[n_out:]


def _token_specs(tm):
    k = tm // BLK
    return [pl.BlockSpec((BLK, D), functools.partial(lambda i, t: (jnp.maximum(k * i + t - 1, 0), 0), t=t)) for t in range(k)]


def _in_proj(x2d, meta, gain, w_int, b_in, tabs, comm=None):
    p = x2d.shape[0] + BLK
    tm = _row_tile(p)
    k = tm // BLK

    def body(*refs):
        x_refs = refs[:k]
        m_ref, g_ref, w_ref, b_ref, t_ref, h_ref, n1_ref, q_ref, kv_ref, ag_ref, gt_ref = refs[k:]
        i = pl.program_id(0)
        head = jnp.concatenate([jnp.zeros((PAD, D), F32), m_ref[...]], axis=0)
        first = jnp.where(i == 0, head, x_refs[0][...])
        h = jnp.concatenate([first] + [r[...] for r in x_refs[1:]], axis=0) if k > 1 else first
        h_ref[...] = h
        n = _rms(h, g_ref[...]).astype(BF16)
        n1_ref[...] = n
        c, s1, s2 = t_ref[:, 0:128], t_ref[:, 128:256], t_ref[:, 256:384]

        def mm(c0, w):
            return _dot_nt(n, w_ref[c0:c0 + w, :]) + b_ref[:, c0:c0 + w]

        for j in range(4):
            acc = mm(256 * j, 256)
            for t in range(2):
                lo = 256 * j + 128 * t
                q_ref[:, lo:lo + 128] = (_rope(acc[:, 128 * t:128 * (t + 1)], c, s1, s2) * SCALE).astype(BF16)
        acc = mm(1024, 256)
        kv_ref[:, 0:128] = _rope(acc[:, 0:128], c, s1, s2).astype(BF16)
        kv_ref[:, 128:256] = acc[:, 128:256].astype(BF16)
        for j in range(8):
            ag_ref[:, 256 * j:256 * (j + 1)] = mm(QKV_W + 256 * j, 256).astype(BF16)
        for j in range(8):
            gt_ref[:, 256 * j:256 * (j + 1)] = mm(QKV_W + 2048 + 256 * j, 256).astype(BF16)

    def row(w):
        return pl.BlockSpec((tm, w), lambda i: (i, 0))

    return _call(
        body, name="in_proj", grid=(p // tm,),
        in_specs=_token_specs(tm) + [VM, VM, VM, VM, row(384)],
        out_specs=[row(D), row(D), row(D), row(256), row(2048), row(2048)],
        out_shape=[jax.ShapeDtypeStruct((p, D), F32)] + [jax.ShapeDtypeStruct((p, w), BF16) for w in (D, D, 256, 2048, 2048)],
        args=(x2d,) * k + (meta, gain, w_int, b_in, tabs), comm=comm)


N_KEY = 2 * BLK + N_META


def _attn_setup(n, h, q_ref, km_ref, kp_ref, kc_ref):
    lo = lax.broadcasted_iota(jnp.int32, (BLK, BLK), 1) < HEAD_DIM
    lok = lax.broadcasted_iota(jnp.int32, (N_KEY, BLK), 1) < HEAD_DIM

    def dup(lanes):
        cat = jnp.concatenate([kp_ref[:, lanes], kc_ref[:, lanes], km_ref[PAD:BLK, lanes]], axis=0).astype(F32)
        rolled = pltpu.roll(cat, HEAD_DIM, 1)
        return (jnp.where(lok, cat, rolled) if h == 0 else jnp.where(lok, rolled, cat)).astype(BF16)

    k2 = dup(slice(0, 128))
    v2 = dup(slice(128, 256))
    qs = _stack_heads(q_ref, h, lo)

    kr = lax.broadcasted_iota(jnp.int32, (BLK, BLK), 0)
    tq = BLK * n + lax.broadcasted_iota(jnp.int32, (BLK, BLK), 1) - PAD
    t_p = BLK * (n - 1) + kr - PAD
    t_c = BLK * n + kr - PAD
    ok_p = jnp.logical_and(t_p >= N_META, tq - t_p < BLK)
    ok_c = jnp.logical_and(t_c >= N_META, t_c <= tq)
    ok_m = lax.broadcasted_iota(jnp.int32, (N_META, BLK), 0) <= BLK * n + lax.broadcasted_iota(jnp.int32, (N_META, BLK), 1) - PAD
    bias = jnp.concatenate([jnp.where(ok, 0.0, NEG_INF).astype(F32) for ok in (ok_p, ok_c, ok_m)], axis=0)
    return qs, k2, v2, bias, lok


def _attn_head(s, bias, sink):
    s = s + bias
    m = jnp.maximum(jnp.max(s, axis=0, keepdims=True), sink)
    e = jnp.exp(s - m)
    es = jnp.exp(sink - m)
    inv = 1.0 / (jnp.sum(e, axis=0, keepdims=True) + es)
    return e * inv, es * inv


def _stack_heads(ref, h, lo):
    pieces = []
    for jp in range(4):
        v = ref[:, BLK * (4 * h + jp):BLK * (4 * h + jp + 1)]
        zero = jnp.zeros_like(v)
        pieces += [jnp.where(lo, v, zero), jnp.where(lo, zero, v)]
    return jnp.concatenate(pieces, axis=0)


def _unstack_heads(v, jp, lo):
    return jnp.where(lo, v[256 * jp:256 * jp + 128], v[256 * jp + 128:256 * jp + 256])


def _attn_fwd(q, kv, sinks, comm=None):
    p = q.shape[0]
    nb = p // BLK

    def body(q_ref, km_ref, kp_ref, kc_ref, sink_ref, o_ref):
        n = pl.program_id(0)
        lo = lax.broadcasted_iota(jnp.int32, (BLK, BLK), 1) < HEAD_DIM
        for h in range(2):
            qs, k2, v2, bias, _ = _attn_setup(n, h, q_ref, km_ref, kp_ref, kc_ref)
            st = _dot_nt(k2, qs)
            pt = jnp.concatenate(
                [_attn_head(st[:, BLK * g:BLK * (g + 1)], bias, sink_ref[0, 8 * h + g])[0].astype(BF16) for g in range(8)],
                axis=1)
            o = _dot_tn(pt, v2)
            for jp in range(4):
                o_ref[:, BLK * (4 * h + jp):BLK * (4 * h + jp + 1)] = _unstack_heads(o, jp, lo).astype(BF16)

    return _call(
        body, name="attn_fwd", grid=(nb,),
        in_specs=[pl.BlockSpec((BLK, D), lambda i: (i, 0)),
                  pl.BlockSpec((BLK, 256), lambda i: (0, 0)),
                  pl.BlockSpec((BLK, 256), lambda i: (jnp.maximum(i - 1, 0), 0)),
                  pl.BlockSpec((BLK, 256), lambda i: (i, 0)),
                  pl.BlockSpec(memory_space=pltpu.SMEM)],
        out_specs=[pl.BlockSpec((BLK, D), lambda i: (i, 0))],
        out_shape=[jax.ShapeDtypeStruct((p, D), BF16)],
        args=(q, kv, kv, kv, sinks), comm=comm)


def _conv31_fwd(ag, w32, b, comm=None):
    p = ag.shape[0]
    nch = p // BLK

    def body(a_ref, g_ref, w_ref, b_ref, o_ref, gp):
        gp[0:32, :] = jnp.zeros((32, BLK), F32)
        for ci in range(nch):
            r0 = BLK * ci
            glu = a_ref[r0:r0 + BLK, :].astype(F32) * jax.nn.sigmoid(g_ref[r0:r0 + BLK, :].astype(F32))
            if ci == 0:
                glu = jnp.where(_rows(0, BLK) >= PAD, glu, 0.0)
            gp[32 + r0:32 + r0 + BLK, :] = glu
        for ci in range(nch):
            r0 = BLK * ci
            acc = jnp.broadcast_to(b_ref[...], (BLK, BLK))
            for j in range(CONV_K):
                acc = acc + w_ref[j:j + 1, :] * gp[r0 + j + 2:r0 + j + 2 + BLK, :]
            o_ref[r0:r0 + BLK, :] = acc

    return _call(
        body, name="conv31_fwd", grid=(D // BLK,),
        in_specs=[pl.BlockSpec((p, BLK), lambda j: (0, j)), pl.BlockSpec((p, BLK), lambda j: (0, 8 + j)),
                  pl.BlockSpec((32, BLK), lambda j: (0, j)), pl.BlockSpec((1, BLK), lambda j: (0, j))],
        out_specs=[pl.BlockSpec((p, BLK), lambda j: (0, j))],
        out_shape=[jax.ShapeDtypeStruct((p, D), F32)],
        scratch=[pltpu.VMEM((p + 32, BLK), F32)],
        args=(ag, ag, w32, b), comm=comm)


def _mixer_fwd(ao, c0, gates, h0p, wa, wc, wo, vecs):
    p = ao.shape[0]
    tm = _row_tile(p)

    def body(ao_ref, c0_ref, gt_ref, h_ref, wa_ref, wc_ref, wo_ref, v_ref,
             c1_ref, at_ref, cv_ref, mg_ref, mix_ref, h1_ref, n2_ref):
        i = pl.program_id(0)
        c1 = _lnsilu(c0_ref[...], v_ref[0:1, :], v_ref[1:2, :]).astype(BF16)
        c1_ref[...] = c1
        attn = _dot(ao_ref[...], wa_ref[...])
        conv = _dot(c1, wc_ref[...]) + v_ref[2:3, :]
        at_ref[...] = attn.astype(BF16)
        cv_ref[...] = conv.astype(BF16)
        merged = (jax.nn.sigmoid(gt_ref[:, 0:D].astype(F32)) * attn
                  + jax.nn.sigmoid(gt_ref[:, D:2 * D].astype(F32)) * conv).astype(BF16)
        mg_ref[...] = merged
        mix = _dot(merged, wo_ref[...])
        mix_ref[...] = mix
        h1 = jnp.where(_rows(i, tm) >= PAD, h_ref[...] + _rms(mix, v_ref[3:4, :]), 0.0)
        h1_ref[...] = h1
        n2_ref[...] = _rms(h1, v_ref[4:5, :]).astype(BF16)

    def row(w):
        return pl.BlockSpec((tm, w), lambda i: (i, 0))

    return pl.pallas_call(
        body, name="mixer_fwd", grid=(p // tm,),
        in_specs=[row(D), row(D), row(2 * D), row(D), VM, VM, VM, VM],
        out_specs=[row(D)] * 7,
        out_shape=[jax.ShapeDtypeStruct((p, D), t) for t in (BF16, BF16, BF16, BF16, F32, F32, BF16)],
        compiler_params=_cparams("parallel"),
    )(ao, c0, gates, h0p, wa, wc, wo, vecs)


def _mm_nt(a, w_t, name):
    p, k = a.shape
    n = w_t.shape[0]
    tm = _row_tile(p)
    ch = 512

    def body(a_ref, w_ref, o_ref):
        a_v = a_ref[...]
        for c0 in range(0, n, ch):
            o_ref[:, c0:c0 + ch] = _dot_nt(a_v, w_ref[c0:c0 + ch, :]).astype(BF16)

    return pl.pallas_call(
        body, name=name, grid=(p // tm,),
        in_specs=[pl.BlockSpec((tm, k), lambda i: (i, 0)), VM],
        out_specs=pl.BlockSpec((tm, n), lambda i: (i, 0)),
        out_shape=jax.ShapeDtypeStruct((p, n), BF16),
        compiler_params=_cparams("parallel"),
    )(a, w_t)


def _conv3(xp_ref, w_ref, r0):
    return (w_ref[0:1, :] * xp_ref[r0 + 6:r0 + 6 + BLK, :] + w_ref[1:2, :] * xp_ref[r0 + 7:r0 + 7 + BLK, :]
            + w_ref[2:3, :] * xp_ref[r0 + 8:r0 + 8 + BLK, :])


def _ffn_slab_specs(p):
    ncol = FFN // BLK
    return [pl.BlockSpec((p, BLK), lambda j: (0, j)), pl.BlockSpec((p, BLK), lambda j: (0, ncol + j)),
            pl.BlockSpec((FFN_K, BLK), lambda j: (0, j)), pl.BlockSpec((FFN_K, BLK), lambda j: (0, ncol + j)),
            pl.BlockSpec((1, BLK), lambda j: (0, j)), pl.BlockSpec((1, BLK), lambda j: (0, ncol + j))]


def _fill_shifted(dst, src_ref, nch):
    dst[0:8, :] = jnp.zeros((8, BLK), F32)
    for ci in range(nch):
        dst[8 + BLK * ci:8 + BLK * (ci + 1), :] = src_ref[BLK * ci:BLK * (ci + 1), :].astype(F32)


def _ffn_act(u0, fw, fb):
    p = u0.shape[0]
    nch = p // BLK

    def body(g_ref, v_ref, wg_ref, wv_ref, bg_ref, bv_ref, o_ref, dv_ref, dg_ref, xg, xv):
        _fill_shifted(xg, g_ref, nch)
        _fill_shifted(xv, v_ref, nch)
        for ci in range(nch):
            r0 = BLK * ci
            ug = _conv3(xg, wg_ref, r0) + bg_ref[...]
            uv = _conv3(xv, wv_ref, r0) + bv_ref[...]
            sg = jax.nn.sigmoid(ug)
            silu = ug * sg
            o_ref[r0:r0 + BLK, :] = (silu * uv).astype(BF16)
            dv_ref[r0:r0 + BLK, :] = silu.astype(BF16)
            dg_ref[r0:r0 + BLK, :] = (uv * (sg * (1.0 + ug * (1.0 - sg)))).astype(BF16)

    slab = pl.BlockSpec((p, BLK), lambda j: (0, j))
    return pl.pallas_call(
        body, name="ffn_act", grid=(FFN // BLK,),
        in_specs=_ffn_slab_specs(p),
        out_specs=[slab] * 3,
        out_shape=[jax.ShapeDtypeStruct((p, FFN), BF16)] * 3,
        scratch_shapes=[pltpu.VMEM((p + 8, BLK), F32)] * 2,
        compiler_params=_cparams("parallel"),
    )(u0, u0, fw, fw, fb, fb)


def _ffn_down_loss(act, wd, h1, tgt, gain):
    p = act.shape[0]
    tm = _row_tile(p)
    k = tm // BLK

    def body(*refs):
        a_ref, w_ref, h_ref = refs[:3]
        t_refs = refs[3:3 + k]
        g_ref, df_ref, da_ref, dy_ref, acc_ref = refs[3 + k:]
        i = pl.program_id(0)

        @pl.when(i == 0)
        def _():
            acc_ref[...] = jnp.zeros_like(acc_ref)

        ffn = _dot(a_ref[...], w_ref[...])
        t = jnp.concatenate([t_ref[...] for t_ref in t_refs], axis=0) if k > 1 else t_refs[0][...]
        diff = jnp.where(_rows(i, tm) >= BLK, h_ref[...] + _rms(ffn, g_ref[...]) - t, 0.0)
        dy = diff * (1.0 / D)
        dffn, dg = _rms_bwd(ffn, g_ref[...], dy)
        acc_ref[0:1, :] += dg
        acc_ref[1:2, :] += jnp.sum(diff * diff, axis=0, keepdims=True) * (0.5 / D)
        dy_ref[...] = dy
        dfb = dffn.astype(BF16)
        df_ref[...] = dfb
        for c0 in range(0, FFN, 256):
            da_ref[:, c0:c0 + 256] = _dot_nt(dfb, w_ref[c0:c0 + 256, :]).astype(BF16)

    def row(w):
        return pl.BlockSpec((tm, w), lambda i: (i, 0))

    return pl.pallas_call(
        body, name="ffn_down_loss", grid=(p // tm,),
        in_specs=[row(FFN), VM, row(D)] + _token_specs(tm) + [VM],
        out_specs=[row(D), row(FFN), row(D), pl.BlockSpec((8, D), lambda i: (0, 0))],
        out_shape=[jax.ShapeDtypeStruct((p, D), BF16), jax.ShapeDtypeStruct((p, FFN), BF16),
                   jax.ShapeDtypeStruct((p, D), F32), jax.ShapeDtypeStruct((8, D), F32)],
        compiler_params=_cparams("arbitrary"),
    )(act, wd, h1, *([tgt] * k), gain)


def _mm_tn(pieces, b, name, col_sums=False, comm=None):
    p, n = b.shape
    tk = 256
    nblk = [a.shape[1] // tk for a in pieces]
    offs = [sum(nblk[:q]) for q in range(len(pieces))]
    total = sum(nblk)
    npc = len(pieces)

    def body(*refs):
        a_refs, b_ref, o_ref = refs[:npc], refs[npc], refs[npc + 1]
        i = pl.program_id(0)
        for q, a_ref in enumerate(a_refs):
            @pl.when(jnp.logical_and(i >= offs[q], i < offs[q] + nblk[q]))
            def _(a_ref=a_ref):
                a_v = a_ref[...]
                o_ref[...] = _dot_tn(a_v, b_ref[...]).astype(BF16)
                if col_sums:
                    refs[npc + 2][...] = jnp.sum(a_v.astype(F32), axis=0, keepdims=True)

    def a_spec(q):
        return pl.BlockSpec((p, tk), lambda i: (0, jnp.clip(i - offs[q], 0, nblk[q] - 1)))

    out_specs = [pl.BlockSpec((tk, n), lambda i: (i, 0))]
    out_shape = [jax.ShapeDtypeStruct((total * tk, n), BF16)]
    if col_sums:
        out_specs.append(pl.BlockSpec((1, tk), lambda i: (0, i)))
        out_shape.append(jax.ShapeDtypeStruct((1, total * tk), F32))
    res, sent = _call(
        body, name=name, grid=(total,),
        in_specs=[a_spec(q) for q in range(npc)] + [VM],
        out_specs=out_specs, out_shape=out_shape, args=(*pieces, b), comm=comm)
    res = res if col_sums else res[0]
    return res if comm is None else (res, sent)


def _ffn_act_bwd(u0, dact, dact_dg, dact_dv, fw, act, dffn, comm=None):
    p = u0.shape[0]
    nch = p // BLK
    ncol = FFN // BLK

    def body(g_ref, v_ref, wg_ref, wv_ref, da_ref, lg_ref, lv_ref, act_ref, df_ref,
             dg_ref, dv_ref, gwg_ref, gwv_ref, gbg_ref, gbv_ref, gwd_ref, eg, ev):
        gwd_ref[...] = _dot_tn(act_ref[...], df_ref[...]).astype(BF16)
        eg[p:p + 8, :] = jnp.zeros((8, BLK), F32)
        ev[p:p + 8, :] = jnp.zeros((8, BLK), F32)
        for ci in range(nch):
            r0 = BLK * ci
            d = da_ref[r0:r0 + BLK, :].astype(F32)
            eg[r0:r0 + BLK, :] = d * lg_ref[r0:r0 + BLK, :].astype(F32)
            ev[r0:r0 + BLK, :] = d * lv_ref[r0:r0 + BLK, :].astype(F32)
        def fold(v):
            return jnp.sum(v.reshape(BLK // 8, 8, BLK), axis=0)

        for e_s, x_ref, w_ref, d_ref, gw_ref, gb_ref in ((eg, g_ref, wg_ref, dg_ref, gwg_ref, gbg_ref),
                                                        (ev, v_ref, wv_ref, dv_ref, gwv_ref, gbv_ref)):
            sums = [jnp.zeros((8, BLK), F32) for _ in range(FFN_K + 1)]
            for ci in range(nch):
                r0 = BLK * ci
                es = [e_s[r0 + t:r0 + t + BLK, :] for t in range(FFN_K)]
                du = w_ref[2:3, :] * es[0] + w_ref[1:2, :] * es[1] + w_ref[0:1, :] * es[2]
                if ci == 0:
                    du = jnp.where(_rows(0, BLK) >= PAD, du, 0.0)
                d_ref[r0:r0 + BLK, :] = du.astype(BF16)
                x = x_ref[r0:r0 + BLK, :].astype(F32)
                for j in range(FFN_K):
                    sums[j] = sums[j] + fold(es[FFN_K - 1 - j] * x)
                sums[FFN_K] = sums[FFN_K] + fold(es[0])
            for j in range(FFN_K):
                gw_ref[j:j + 1, :] = jnp.sum(sums[j], axis=0, keepdims=True)
            gb_ref[...] = jnp.sum(sums[FFN_K], axis=0, keepdims=True)

    slab = pl.BlockSpec((p, BLK), lambda j: (0, j))
    wspec = pl.BlockSpec((FFN_K, BLK), lambda j: (0, j))
    bspec = pl.BlockSpec((1, BLK), lambda j: (0, j))
    return _call(
        body, name="ffn_act_bwd", grid=(ncol,),
        in_specs=_ffn_slab_specs(p)[:4] + [slab] * 4 + [VM],
        out_specs=[slab, slab, wspec, wspec, bspec, bspec, pl.BlockSpec((BLK, D), lambda j: (j, 0))],
        out_shape=[jax.ShapeDtypeStruct((p, FFN), BF16)] * 2 + [jax.ShapeDtypeStruct((FFN_K, FFN), F32)] * 2
        + [jax.ShapeDtypeStruct((1, FFN), F32)] * 2 + [jax.ShapeDtypeStruct((FFN, D), BF16)],
        scratch=[pltpu.VMEM((p + 8, BLK), F32)] * 2,
        args=(u0, u0, fw, fw, dact, dact_dg, dact_dv, act, dffn), comm=comm)


def _ffn_in_bwd(dug, duv, w_upt, h1, dy, gain, comm=None):
    p = h1.shape[0]
    tm = _row_tile(p)

    def body(dg_ref, dv_ref, w_ref, h_ref, dy_ref, g_ref, o_ref, acc_ref):
        i = pl.program_id(0)

        @pl.when(i == 0)
        def _():
            acc_ref[...] = jnp.zeros_like(acc_ref)

        dn = _dot(dg_ref[...], w_ref[0:FFN, :]) + _dot(dv_ref[...], w_ref[FFN:2 * FFN, :])
        dh, dg = _rms_bwd(h_ref[...], g_ref[...], dn)
        o_ref[...] = dy_ref[...] + dh
        acc_ref[0:1, :] += dg

    def row(w):
        return pl.BlockSpec((tm, w), lambda i: (i, 0))

    return _call(
        body, name="ffn_in_bwd", grid=(p // tm,),
        in_specs=[row(FFN), row(FFN), VM, row(D), row(D), VM],
        out_specs=[row(D), pl.BlockSpec((8, D), lambda i: (0, 0))],
        out_shape=[jax.ShapeDtypeStruct((p, D), F32), jax.ShapeDtypeStruct((8, D), F32)],
        sem="arbitrary", args=(dug, duv, w_upt, h1, dy, gain), comm=comm)


def _mixer_bwd(dh1, mix, attn, conv, gates, c0, wa, wc, wo, vecs, comm=None):
    p = dh1.shape[0]
    tm = _row_tile(p)

    def body(dh_ref, mix_ref, at_ref, cv_ref, gt_ref, c0_ref, wa_ref, wc_ref, wo_ref, v_ref,
             dmix_ref, dat_ref, dcv_ref, dgt_ref, dao_ref, dc0_ref, acc_ref):
        i = pl.program_id(0)

        @pl.when(i == 0)
        def _():
            acc_ref[...] = jnp.zeros_like(acc_ref)

        dmix, dgp = _rms_bwd(mix_ref[...], v_ref[3:4, :], dh_ref[...])
        dmix = dmix.astype(BF16)
        dmix_ref[...] = dmix
        dmg = _dot_nt(dmix, wo_ref[...])
        sa = jax.nn.sigmoid(gt_ref[:, 0:D].astype(F32))
        sc = jax.nn.sigmoid(gt_ref[:, D:2 * D].astype(F32))
        dat = dmg * sa
        dcv = dmg * sc
        dgt_ref[:, 0:D] = (dmg * at_ref[...].astype(F32) * sa * (1.0 - sa)).astype(BF16)
        dgt_ref[:, D:2 * D] = (dmg * cv_ref[...].astype(F32) * sc * (1.0 - sc)).astype(BF16)
        datb = dat.astype(BF16)
        dcvb = dcv.astype(BF16)
        dat_ref[...] = datb
        dcv_ref[...] = dcvb
        dao_ref[...] = _dot_nt(datb, wa_ref[...]).astype(BF16)
        dc1 = _dot_nt(dcvb, wc_ref[...])
        dc0, dlg, dlb = _lnsilu_bwd(c0_ref[...], v_ref[0:1, :], v_ref[1:2, :], dc1)
        dc0_ref[...] = dc0
        acc_ref[0:1, :] += dgp
        acc_ref[1:2, :] += jnp.sum(dcv, axis=0, keepdims=True)
        acc_ref[2:3, :] += dlg
        acc_ref[3:4, :] += dlb

    def row(w):
        return pl.BlockSpec((tm, w), lambda i: (i, 0))

    return _call(
        body, name="mixer_bwd", grid=(p // tm,),
        in_specs=[row(D), row(D), row(D), row(D), row(2 * D), row(D), VM, VM, VM, VM],
        out_specs=[row(D), row(D), row(D), row(2 * D), row(D), row(D), pl.BlockSpec((8, D), lambda i: (0, 0))],
        out_shape=[jax.ShapeDtypeStruct((p, D), BF16)] * 3 + [jax.ShapeDtypeStruct((p, 2 * D), BF16),
                                                             jax.ShapeDtypeStruct((p, D), BF16),
                                                             jax.ShapeDtypeStruct((p, D), F32),
                                                             jax.ShapeDtypeStruct((8, D), F32)],
        sem="arbitrary", args=(dh1, mix, attn, conv, gates, c0, wa, wc, wo, vecs), comm=comm)


def _conv31_bwd(ag, dc0, w32, tn_pairs, comm=None):
    p = ag.shape[0]
    nch = p // BLK
    npair = len(tn_pairs)

    def body(*refs):
        a_ref, g_ref, dc_ref, w_ref = refs[:4]
        tn_a, tn_b = refs[4:4 + npair], refs[4 + npair:4 + 2 * npair]
        da_ref, dg_ref, gw_ref, gb_ref = refs[4 + 2 * npair:8 + 2 * npair]
        tn_o = refs[8 + 2 * npair:8 + 3 * npair]
        gp, dp = refs[8 + 3 * npair:]
        for ta, tb, to in zip(tn_a, tn_b, tn_o):
            to[...] = _dot_tn(ta[...], tb[...]).astype(BF16)
        gp[0:32, :] = jnp.zeros((32, BLK), F32)
        dp[p:p + 32, :] = jnp.zeros((32, BLK), F32)
        bsum = jnp.zeros((BLK, BLK), F32)
        for ci in range(nch):
            r0 = BLK * ci
            glu = a_ref[r0:r0 + BLK, :].astype(F32) * jax.nn.sigmoid(g_ref[r0:r0 + BLK, :].astype(F32))
            if ci == 0:
                glu = jnp.where(_rows(0, BLK) >= PAD, glu, 0.0)
            gp[32 + r0:32 + r0 + BLK, :] = glu
            d = dc_ref[r0:r0 + BLK, :]
            dp[r0:r0 + BLK, :] = d
            bsum = bsum + d
        gb_ref[...] = jnp.sum(bsum, axis=0, keepdims=True)
        for ci in range(nch):
            r0 = BLK * ci
            acc = jnp.zeros((BLK, BLK), F32)
            for j in range(CONV_K):
                acc = acc + w_ref[j:j + 1, :] * dp[r0 + 30 - j:r0 + 30 - j + BLK, :]
            if ci == 0:
                acc = jnp.where(_rows(0, BLK) >= PAD, acc, 0.0)
            a = a_ref[r0:r0 + BLK, :].astype(F32)
            sg = jax.nn.sigmoid(g_ref[r0:r0 + BLK, :].astype(F32))
            da_ref[r0:r0 + BLK, :] = (acc * sg).astype(BF16)
            dg_ref[r0:r0 + BLK, :] = (acc * a * sg * (1.0 - sg)).astype(BF16)
        sub = BLK // 2
        accs = [jnp.zeros((8, BLK), F32) for _ in range(CONV_K)]
        for r0 in range(0, p, sub):
            d = dp[r0:r0 + sub, :]
            for j in range(CONV_K):
                prod = d * gp[r0 + j + 2:r0 + j + 2 + sub, :]
                accs[j] = accs[j] + jnp.sum(prod.reshape(sub // 8, 8, BLK), axis=0)
        for j in range(CONV_K):
            gw_ref[j:j + 1, :] = jnp.sum(accs[j], axis=0, keepdims=True)
        gw_ref[CONV_K:32, :] = jnp.zeros((32 - CONV_K, BLK), F32)

    slab = pl.BlockSpec((p, BLK), lambda j: (0, j))
    return _call(
        body, name="conv31_bwd", grid=(D // BLK,),
        in_specs=[slab, pl.BlockSpec((p, BLK), lambda j: (0, 8 + j)), slab, pl.BlockSpec((32, BLK), lambda j: (0, j))]
        + [slab] * npair + [VM] * npair,
        out_specs=[slab, slab, pl.BlockSpec((32, BLK), lambda j: (0, j)), pl.BlockSpec((1, BLK), lambda j: (0, j))]
        + [pl.BlockSpec((BLK, D), lambda j: (j, 0))] * npair,
        out_shape=[jax.ShapeDtypeStruct((p, D), BF16)] * 2 + [jax.ShapeDtypeStruct((32, D), F32),
                                                             jax.ShapeDtypeStruct((1, D), F32)]
        + [jax.ShapeDtypeStruct((D, D), BF16)] * npair,
        scratch=[pltpu.VMEM((p + 32, BLK), F32)] * 2,
        args=(ag, ag, dc0, w32, *[a for a, _ in tn_pairs], *[b for _, b in tn_pairs]), comm=comm)


def _attn_bwd(q, kv, dao, sinks, tabs, comm=None):
    p = q.shape[0]
    nb = p // BLK

    def body(q_ref, km_ref, kp_ref, kc_ref, do_ref, sink_ref, t_ref, dqkv_ref, dsink_ref, carry, macc):
        i = pl.program_id(0)
        n = nb - 1 - i

        @pl.when(i == 0)
        def _():
            carry[...] = jnp.zeros_like(carry)
            macc[...] = jnp.zeros_like(macc)
            dsink_ref[...] = jnp.zeros_like(dsink_ref)

        lo = lax.broadcasted_iota(jnp.int32, (BLK, BLK), 1) < HEAD_DIM
        lane8 = lax.broadcasted_iota(jnp.int32, (8, BLK), 1)
        c, s1, s2 = t_ref[:, 0:128], -t_ref[:, 128:256], -t_ref[:, 256:384]
        dk = jnp.zeros((N_KEY, BLK), F32)
        dv = jnp.zeros((N_KEY, BLK), F32)
        for h in range(2):
            qs, k2, v2, bias, lok = _attn_setup(n, h, q_ref, km_ref, kp_ref, kc_ref)
            dos = _stack_heads(do_ref, h, lo)
            st = _dot_nt(k2, qs)
            dpt = _dot_nt(v2, dos)
            p_parts, ds_parts = [], []
            for g in range(8):
                cols = slice(BLK * g, BLK * (g + 1))
                pn, ps = _attn_head(st[:, cols], bias, sink_ref[0, 8 * h + g])
                dp = dpt[:, cols]
                delta = jnp.sum(pn * dp, axis=0, keepdims=True)
                ds_parts.append((pn * (dp - delta)).astype(BF16))
                p_parts.append(pn.astype(BF16))
                dsk = -jnp.sum(ps * delta, axis=1, keepdims=True)
                dsink_ref[...] += jnp.where(lane8 == 8 * h + g, dsk, 0.0)
            dst = jnp.concatenate(ds_parts, axis=1)
            pt = jnp.concatenate(p_parts, axis=1)
            dq = _dot_tn(dst, k2)
            for jp in range(4):
                lo_c = BLK * (4 * h + jp)
                dqkv_ref[:, lo_c:lo_c + BLK] = (_rope(_unstack_heads(dq, jp, lo), c, s1, s2) * SCALE).astype(BF16)
            dk2 = _dot(dst, qs)
            dv2 = _dot(pt, dos)
            dk2 = dk2 + pltpu.roll(dk2, HEAD_DIM, 1)
            dv2 = dv2 + pltpu.roll(dv2, HEAD_DIM, 1)
            own = lok if h == 0 else jnp.logical_not(lok)
            dk = jnp.where(own, dk2, dk)
            dv = jnp.where(own, dv2, dv)
        macc[:, 0:BLK] += dk[2 * BLK:N_KEY]
        macc[:, BLK:2 * BLK] += dv[2 * BLK:N_KEY]
        last = (n == 0).astype(F32)
        zpad = jnp.zeros((PAD, BLK), F32)
        dk_c = dk[BLK:2 * BLK] + carry[:, 0:BLK] + last * jnp.concatenate([zpad, macc[:, 0:BLK]], axis=0)
        dv_c = dv[BLK:2 * BLK] + carry[:, BLK:2 * BLK] + last * jnp.concatenate([zpad, macc[:, BLK:2 * BLK]], axis=0)
        carry[:, 0:BLK] = dk[0:BLK]
        carry[:, BLK:2 * BLK] = dv[0:BLK]
        dqkv_ref[:, D:D + BLK] = _rope(dk_c, c, s1, s2).astype(BF16)
        dqkv_ref[:, D + BLK:D + 2 * BLK] = dv_c.astype(BF16)

    def rev(w):
        return pl.BlockSpec((BLK, w), lambda i: (nb - 1 - i, 0))

    return _call(
        body, name="attn_bwd", grid=(nb,),
        in_specs=[rev(D),
                  pl.BlockSpec((BLK, 256), lambda i: (0, 0)),
                  pl.BlockSpec((BLK, 256), lambda i: (jnp.maximum(nb - 2 - i, 0), 0)),
                  rev(256), rev(D),
                  pl.BlockSpec(memory_space=pltpu.SMEM), rev(384)],
        out_specs=[rev(QKV_W), pl.BlockSpec((8, BLK), lambda i: (0, 0))],
        out_shape=[jax.ShapeDtypeStruct((p, QKV_W), BF16), jax.ShapeDtypeStruct((8, BLK), F32)],
        scratch=[pltpu.VMEM((BLK, 256), F32), pltpu.VMEM((N_META, 256), F32)], sem="arbitrary",
        args=(q, kv, kv, kv, dao, sinks, tabs), comm=comm)


def _in_bwd(dqkv, da, dg, dgt, w_int, h0p, dh1, gain, comm=None):
    p = h0p.shape[0]
    tm = _row_tile(p)
    nt = p // tm
    first_rows = tm - BLK

    def body(dq_ref, da_ref, dg_ref, dt_ref, w_ref, h_ref, dh_ref, g_ref, gx_ref, dm_ref, acc_ref, buf, sems):
        i = pl.program_id(0)
        slot = i % 2

        @pl.when(i == 0)
        def _():
            acc_ref[...] = jnp.zeros_like(acc_ref)

        dn = (_dot(dq_ref[...], w_ref[0:QKV_W, :]) + _dot(da_ref[...], w_ref[QKV_W:QKV_W + D, :])
              + _dot(dg_ref[...], w_ref[QKV_W + D:QKV_W + 2 * D, :]) + _dot(dt_ref[...], w_ref[QKV_W + 2 * D:IN_W, :]))
        dh, dgain = _rms_bwd(h_ref[...], g_ref[...], dn)
        dh0 = dh_ref[...] + dh
        acc_ref[0:1, :] += dgain
        buf[slot] = dh0

        @pl.when(i == 0)
        def _():
            dm_ref[...] = dh0[PAD:BLK]

        def first_copy():
            return pltpu.make_async_copy(buf.at[0, pl.ds(BLK, first_rows), :], gx_ref.at[pl.ds(0, first_rows), :], sems.at[0])

        def tile_copy(j, s):
            return pltpu.make_async_copy(buf.at[s], gx_ref.at[pl.ds(pl.multiple_of(j * tm - BLK, BLK), tm), :], sems.at[s])

        if first_rows:
            @pl.when(i == 1)
            def _():
                first_copy().wait()

        @pl.when(i >= 2)
        def _():
            tile_copy(i - 1, 1 - slot).wait()

        if first_rows:
            @pl.when(i == 0)
            def _():
                first_copy().start()

        @pl.when(i > 0)
        def _():
            tile_copy(i, slot).start()

        @pl.when(i == nt - 1)
        def _():
            tile_copy(i, slot).wait()

    def row(w):
        return pl.BlockSpec((tm, w), lambda i: (i, 0))

    return _call(
        body, name="in_bwd", grid=(nt,),
        in_specs=[row(QKV_W), row(D), row(D), row(2 * D), VM, row(D), row(D), VM],
        out_specs=[ANY, pl.BlockSpec((N_META, D), lambda i: (0, 0)), pl.BlockSpec((8, D), lambda i: (0, 0))],
        out_shape=[jax.ShapeDtypeStruct((p - BLK, D), F32), jax.ShapeDtypeStruct((N_META, D), F32),
                   jax.ShapeDtypeStruct((8, D), F32)],
        scratch=[pltpu.VMEM((2, tm, D), F32), pltpu.SemaphoreType.DMA((2,))],
        sem="arbitrary", args=(dqkv, da, dg, dgt, w_int, h0p, dh1, gain), comm=comm)


def _sum_slots(slots, name):
    r = slots.shape[0] // N_DEV
    cols = slots.shape[1]
    tr = r if r <= 352 else (r // 2 if (r // 2) % 16 == 0 else r // 3)
    steps = r // tr

    def body(*refs):
        acc = refs[0][...].astype(F32)
        for s in range(1, N_DEV):
            acc = acc + refs[s][...].astype(F32)
        refs[N_DEV][...] = acc

    return pl.pallas_call(
        body, name=name, grid=(steps,),
        in_specs=[pl.BlockSpec((tr, cols), functools.partial(lambda i, s: (s * steps + i, 0), s=s)) for s in range(N_DEV)],
        out_specs=pl.BlockSpec((tr, cols), lambda i: (i, 0)),
        out_shape=jax.ShapeDtypeStruct((r, cols), F32),
        compiler_params=_cparams("parallel"),
    )(*([slots] * N_DEV))


def _adamw_math(w, g, m, v):
    m_n = ADAM_B1 * m + (1.0 - ADAM_B1) * g
    v_n = ADAM_B2 * v + (1.0 - ADAM_B2) * jnp.square(g)
    m_hat = m_n / (1.0 - ADAM_B1 ** ADAM_STEP)
    v_hat = v_n / (1.0 - ADAM_B2 ** ADAM_STEP)
    return -ADAM_LR * (m_hat / (jnp.sqrt(v_hat) + ADAM_EPS) + ADAM_WD * w), m_n, v_n


def _sum_adamw(parts, w, m, v, name, nslots=N_DEV):
    r, cols = w.shape
    rs = r // len(parts)
    tr = rs if rs <= 352 else (rs // 2 if (rs // 2) % 16 == 0 else rs // 3)
    steps = rs // tr

    def body(*refs):
        w_ref, m_ref, v_ref, g_ref, d_ref, nm_ref, nv_ref = refs[nslots * len(parts):]
        i = pl.program_id(0)
        for q in range(len(parts)):
            @pl.when(i // steps == q)
            def _(q=q):
                g = refs[nslots * q][...].astype(F32)
                for s in range(1, nslots):
                    g = g + refs[nslots * q + s][...].astype(F32)
                g_ref[...] = g
                d_ref[...], nm_ref[...], nv_ref[...] = _adamw_math(w_ref[...], g, m_ref[...], v_ref[...])

    def slot_spec(q, s):
        return pl.BlockSpec((tr, cols), lambda i: (s * steps + jnp.clip(i - q * steps, 0, steps - 1), 0))

    spec = pl.BlockSpec((tr, cols), lambda i: (i, 0))
    return pl.pallas_call(
        body, name=name, grid=(steps * len(parts),),
        in_specs=[slot_spec(q, s) for q in range(len(parts)) for s in range(nslots)] + [spec] * 3,
        out_specs=[spec] * 4, out_shape=[jax.ShapeDtypeStruct((r, cols), F32)] * 4,
        compiler_params=_cparams("parallel"),
    )(*[a for a in parts for _ in range(nslots)], w, m, v)


def _adamw(w, g, m, v, name):
    r, cols = w.shape
    tr = 256 if r % 256 == 0 else r

    def body(w_ref, g_ref, m_ref, v_ref, d_ref, nm_ref, nv_ref):
        d_ref[...], nm_ref[...], nv_ref[...] = _adamw_math(w_ref[...], g_ref[...], m_ref[...], v_ref[...])

    spec = pl.BlockSpec((tr, cols), lambda i: (i, 0))
    return pl.pallas_call(
        body, name=name, grid=(r // tr,),
        in_specs=[spec] * 4, out_specs=[spec] * 3,
        out_shape=[jax.ShapeDtypeStruct((r, cols), F32)] * 3,
        compiler_params=_cparams("parallel"),
    )(w, g, m, v)


def _rope_tables(p):
    half = ROT_DIM // 2
    inv_freq = ROPE_THETA ** (-jnp.arange(half, dtype=F32) * 2.0 / ROT_DIM)
    pos = (jnp.arange(p) - PAD).astype(F32)
    ang = pos[:, None] * inv_freq[None, :]
    lane = jnp.arange(BLK)
    seg = (lane % HEAD_DIM) // half
    cos = jnp.cos(ang)[:, lane % half]
    sin = jnp.sin(ang)[:, lane % half]
    c = jnp.where(seg[None, :] < 2, cos, 1.0)
    s1 = jnp.where(seg[None, :] == 0, -sin, 0.0)
    s2 = jnp.where(seg[None, :] == 1, sin, 0.0)
    return jnp.concatenate([c, s1, s2], axis=1).astype(F32)


def _flat_pack(parts, rows):
    flat = jnp.concatenate([a.reshape(-1).astype(F32) for a in parts])
    return jnp.pad(flat, (0, rows * D - flat.shape[0])).reshape(rows, D)


def _flat_unpack(pack, shapes):
    flat = pack.reshape(-1)
    out, off = [], 0
    for s in shapes:
        size = 1
        for e in s:
            size *= e
        out.append(flat[off:off + size].reshape(s))
        off += size
    return out


def kernel(x, meta_tokens, norm_pre_mix, norm_post_mix, w_in, b_in, attn_sinks, w_attn_proj, conv_dw_w, conv_dw_b, conv_ln_g, conv_ln_b, w_conv_proj, b_conv_proj, w_out, norm_pre_ffn, norm_post_ffn, w_up, ffn_dw_w, ffn_dw_b, w_down, loss_target, m_meta_tokens, m_norm_pre_mix, m_norm_post_mix, m_w_in, m_b_in, m_attn_sinks, m_w_attn_proj, m_conv_dw_w, m_conv_dw_b, m_conv_ln_g, m_conv_ln_b, m_w_conv_proj, m_b_conv_proj, m_w_out, m_norm_pre_ffn, m_norm_post_ffn, m_w_up, m_ffn_dw_w, m_ffn_dw_b, m_w_down, v_meta_tokens, v_norm_pre_mix, v_norm_post_mix, v_w_in, v_b_in, v_attn_sinks, v_w_attn_proj, v_conv_dw_w, v_conv_dw_b, v_conv_ln_g, v_conv_ln_b, v_w_conv_proj, v_b_conv_proj, v_w_out, v_norm_pre_ffn, v_norm_post_ffn, v_w_up, v_ffn_dw_w, v_ffn_dw_b, v_w_down):
    seq = x.shape[1]
    p = seq + BLK
    me = 4 * lax.axis_index("x") + 2 * lax.axis_index("y") + lax.axis_index("c")
    in_cols = w_in.shape[2]
    up_cols = w_up.shape[2]

    small = jnp.zeros((56, up_cols), F32)
    small = small.at[0:N_META, 0:BLK].set(meta_tokens)
    small = small.at[16:16 + CONV_K, 0:BLK].set(conv_dw_w[0])
    small = small.at[48:48 + FFN_K, :].set(ffn_dw_w[0])
    w_int, small_all = _exchange(_Both(_GatherRelay(w_in[0].T.astype(BF16)), _Gather([small])), "gather_w_in")
    small_all = small_all.reshape(N_DEV, 56, up_cols)
    meta_full = small_all[:, 0:N_META, 0:BLK].transpose(1, 0, 2).reshape(N_META, D)
    cdw = small_all[:, 16:16 + CONV_K, 0:BLK].transpose(1, 0, 2).reshape(CONV_K, D)
    cdw32 = jnp.pad(cdw, ((0, 32 - CONV_K), (0, 0)))
    fdw = small_all[:, 48:48 + FFN_K, :].transpose(1, 0, 2).reshape(FFN_K, 2 * FFN)

    tabs = _rope_tables(p)
    vecs = jnp.concatenate([conv_ln_g, conv_ln_b, b_conv_proj, norm_post_mix, norm_pre_ffn, jnp.zeros((3, D), F32)], axis=0)

    (h0p, n1, q, kv, ag, gates), (wa, wc, wo) = _in_proj(
        x[0], meta_full, norm_pre_mix, w_int, b_in, tabs,
        comm=_Gather([w_attn_proj[0].astype(BF16), w_conv_proj[0].astype(BF16), w_out[0].astype(BF16)]))
    (ao,), (w_upt,) = _attn_fwd(q, kv, attn_sinks, comm=_Gather([w_up[0].T.astype(BF16)]))
    (c0,), (wd,) = _conv31_fwd(ag, cdw32, conv_dw_b, comm=_Gather([w_down[0].astype(BF16)]))
    c1, attn, conv, merged, mix, h1, n2 = _mixer_fwd(ao, c0, gates, h0p, wa, wc, wo, vecs)
    u0 = _mm_nt(n2, w_upt, "ffn_up")
    act, dact_dv, dact_dg = _ffn_act(u0, fdw, ffn_dw_b)
    dffn, dact, dy, acc_f = _ffn_down_loss(act, wd, h1, loss_target[0], norm_post_ffn)

    (dug, duv, gfw_g, gfw_v, gfb_g, gfb_v, g_wd), _ = _ffn_act_bwd(u0, dact, dact_dg, dact_dv, fdw, act, dffn)
    g_wupt, (s_wd0,) = _mm_tn([dug, duv], n2, "grad_w_up", comm=_Scatter([g_wd], 0, 2))
    (dh1, acc_u), (s_wd1,) = _ffn_in_bwd(dug, duv, w_upt, h1, dy, norm_pre_ffn, comm=_Scatter([g_wd], 1, 2))
    (dmix, dat, dcv, dgt, dao, dc0, acc_m), (s_wup0,) = _mixer_bwd(
        dh1, mix, attn, conv, gates, c0, wa, wc, wo, vecs, comm=_Scatter([g_wupt], 0, 4))
    (da, dg, g_cdw, g_cdb, g_wo, g_wa, g_wc), (s_wup1, s_wup2, s_wup3) = _conv31_bwd(
        ag, dc0, cdw32, [(merged, dmix), (ao, dat), (c1, dcv)],
        comm=_Both(_Both(_Scatter([g_wupt], 1, 4), _Scatter([g_wupt], 2, 4)), _Scatter([g_wupt], 3, 4)))
    (dqkv, dsink), (s_wa, s_wc, s_wo) = _attn_bwd(q, kv, dao, attn_sinks, tabs, comm=_Scatter([g_wa, g_wc, g_wo]))
    loss_row = jnp.sum(acc_f[1:2, :], axis=1, keepdims=True)
    early = [loss_row, acc_m[0:1], dsink[0:1, 0:16], g_cdw[0:CONV_K], g_cdb,
             acc_m[2:3], acc_m[3:4], acc_m[1:2], acc_u[0:1], acc_f[0:1],
             jnp.concatenate([gfw_g, gfw_v], axis=1), jnp.concatenate([gfb_g, gfb_v], axis=1)]
    (g_wint, g_bin), (gathered_early,) = _mm_tn([dqkv, da, dg, dgt], n1, "grad_w_in", col_sums=True,
                                                comm=_Gather([_flat_pack(early, 64)]))
    (from_sibling,) = _exchange(_SiblingSwap(g_wint), "swap_w_in")
    (grad_x2d, dmeta, acc_i), (s_win,) = _in_bwd(dqkv, da, dg, dgt, w_int, h0p, dh1, norm_pre_mix,
                                                 comm=_ChipScatter(_pair_add(g_wint, from_sibling)))

    big = []
    for nm, parts, nslots, w, m, v, tr in (
            ("w_in", [s_win], N_CHIP, w_in, m_w_in, v_w_in, True), ("w_up", [s_wup0, s_wup1, s_wup2, s_wup3], N_DEV, w_up, m_w_up, v_w_up, True),
            ("w_attn_proj", [s_wa], N_DEV, w_attn_proj, m_w_attn_proj, v_w_attn_proj, False),
            ("w_conv_proj", [s_wc], N_DEV, w_conv_proj, m_w_conv_proj, v_w_conv_proj, False),
            ("w_out", [s_wo], N_DEV, w_out, m_w_out, v_w_out, False),
            ("w_down", [s_wd0, s_wd1], N_DEV, w_down, m_w_down, v_w_down, False)):
        ins = [a[0].T if tr else a[0] for a in (w, m, v)]
        big.append(tuple((o.T if tr else o)[None] for o in _sum_adamw(parts, *ins, "update_" + nm, nslots)))

    late = [dmeta, acc_i[0:1], g_bin]
    (gathered_late,) = _exchange(_Gather([_flat_pack(late, 24)]), "gather_small_grads")
    g_meta, g_npm, g_bi = _flat_unpack(_sum_slots(gathered_late, "sum_late_grads"), [a.shape for a in late])
    tot = _flat_unpack(_sum_slots(gathered_early, "sum_small_grads"), [a.shape for a in early])
    (loss, g_nqm, g_sk, g_cw, g_cb, g_lg, g_lb, g_bc, g_npf, g_nqf, g_fw, g_fb) = tot
    loss = loss.reshape(())
    g_meta = lax.dynamic_slice_in_dim(g_meta, me * BLK, BLK, axis=1)
    g_cw = lax.dynamic_slice_in_dim(g_cw, me * BLK, BLK, axis=1)[None]
    g_fw = lax.dynamic_slice_in_dim(g_fw, me * up_cols, up_cols, axis=1)[None]

    sm_w = [meta_tokens, norm_pre_mix, norm_post_mix, b_in, attn_sinks, conv_dw_w, conv_dw_b, conv_ln_g, conv_ln_b,
            b_conv_proj, norm_pre_ffn, norm_post_ffn, ffn_dw_w, ffn_dw_b]
    sm_g = [g_meta, g_npm, g_nqm, g_bi, g_sk, g_cw, g_cb, g_lg, g_lb, g_bc, g_npf, g_nqf, g_fw, g_fb]
    sm_m = [m_meta_tokens, m_norm_pre_mix, m_norm_post_mix, m_b_in, m_attn_sinks, m_conv_dw_w, m_conv_dw_b, m_conv_ln_g,
            m_conv_ln_b, m_b_conv_proj, m_norm_pre_ffn, m_norm_post_ffn, m_ffn_dw_w, m_ffn_dw_b]
    sm_v = [v_meta_tokens, v_norm_pre_mix, v_norm_post_mix, v_b_in, v_attn_sinks, v_conv_dw_w, v_conv_dw_b, v_conv_ln_g,
            v_conv_ln_b, v_b_conv_proj, v_norm_pre_ffn, v_norm_post_ffn, v_ffn_dw_w, v_ffn_dw_b]
    sm_shapes = [a.shape for a in sm_w]
    upd_rows = 32
    v_pack = _flat_pack(sm_v, upd_rows)
    sm_out = _adamw(_flat_pack(sm_w, upd_rows), _flat_pack(sm_g, upd_rows), _flat_pack(sm_m, upd_rows), v_pack, "adamw_small")
    sm_d, sm_nm, sm_nv = (_flat_unpack(o, sm_shapes) for o in sm_out)

    order = ["meta_tokens", "norm_pre_mix", "norm_post_mix", "w_in", "b_in", "attn_sinks", "w_attn_proj", "conv_dw_w",
             "conv_dw_b", "conv_ln_g", "conv_ln_b", "w_conv_proj", "b_conv_proj", "w_out", "norm_pre_ffn", "norm_post_ffn",
             "w_up", "ffn_dw_w", "ffn_dw_b", "w_down"]
    small_names = ["meta_tokens", "norm_pre_mix", "norm_post_mix", "b_in", "attn_sinks", "conv_dw_w", "conv_dw_b", "conv_ln_g",
                   "conv_ln_b", "b_conv_proj", "norm_pre_ffn", "norm_post_ffn", "ffn_dw_w", "ffn_dw_b"]
    big_names = ["w_in", "w_up", "w_attn_proj", "w_conv_proj", "w_out", "w_down"]
    table = {}
    for k, nm in enumerate(small_names):
        table[nm] = (sm_g[k], sm_d[k], sm_nm[k], sm_nv[k])
    for k, nm in enumerate(big_names):
        table[nm] = big[k]
    grad_x = grad_x2d[None]
    outs = [loss, grad_x]
    for field in range(4):
        outs += [table[nm][field] for nm in order]
    return tuple(outs)
```

```python
import functools

import jax
import jax.numpy as jnp
from jax import lax
from jax.experimental import pallas as pl
from jax.experimental.pallas import tpu as pltpu

F32 = jnp.float32
BF16 = jnp.bfloat16
MESH = pl.DeviceIdType.MESH

D = 1024
HEAD_DIM = 64
N_META = 16
BLK = 128
PAD = BLK - N_META
CONV_K = 31
FFN = 2816
FFN_K = 3
QKV_W = 1280
IN_W = 5376
ROT_DIM = 16
ROPE_THETA = 500000.0
RMS_EPS = 1e-6
LN_EPS = 1e-5
NEG_INF = -1e30
SCALE = HEAD_DIM ** -0.5
N_DEV = 8

ADAM_LR = 0.001
ADAM_B1 = 0.9
ADAM_B2 = 0.999
ADAM_EPS = 1e-08
ADAM_WD = 0.01
ADAM_STEP = 10

VMEM_BYTES_V7X = 64 * 1024 * 1024
VMEM_LIMIT = VMEM_BYTES_V7X - 8 * 1024 * 1024

NT = (((1,), (1,)), ((), ()))
TN = (((0,), (0,)), ((), ()))
VM = pl.BlockSpec(memory_space=pltpu.VMEM)
ANY = pl.BlockSpec(memory_space=pl.ANY)


def _cparams(*sem):
    return pltpu.CompilerParams(dimension_semantics=sem or None, vmem_limit_bytes=VMEM_LIMIT)


def _row_tile(p):
    return 384 if p % 384 == 0 else 128


def _dot(a, b):
    return jnp.dot(a, b, preferred_element_type=F32)


def _dot_nt(a, b):
    return lax.dot_general(a, b, NT, preferred_element_type=F32)


def _dot_tn(a, b):
    return lax.dot_general(a, b, TN, preferred_element_type=F32)


def _rms(x, g):
    return x * lax.rsqrt(jnp.mean(x * x, axis=-1, keepdims=True) + RMS_EPS) * g


def _lnsilu(x, g, b):
    mu = jnp.mean(x, axis=-1, keepdims=True)
    var = jnp.mean(jnp.square(x - mu), axis=-1, keepdims=True)
    z = (x - mu) * lax.rsqrt(var + LN_EPS) * g + b
    return z * jax.nn.sigmoid(z)


def _rms_bwd(x, g, dy):
    r = lax.rsqrt(jnp.mean(x * x, axis=-1, keepdims=True) + RMS_EPS)
    xn = x * r
    u = dy * g
    dg = jnp.sum(dy * xn, axis=0, keepdims=True)
    dx = r * (u - xn * jnp.mean(u * xn, axis=-1, keepdims=True))
    return dx, dg


def _lnsilu_bwd(x, g, b, dout):
    mu = jnp.mean(x, axis=-1, keepdims=True)
    xc = x - mu
    rs = lax.rsqrt(jnp.mean(xc * xc, axis=-1, keepdims=True) + LN_EPS)
    yh = xc * rs
    z = yh * g + b
    sg = jax.nn.sigmoid(z)
    dz = dout * (sg * (1.0 + z * (1.0 - sg)))
    dg = jnp.sum(dz * yh, axis=0, keepdims=True)
    db = jnp.sum(dz, axis=0, keepdims=True)
    dyh = dz * g
    dx = rs * (dyh - jnp.mean(dyh, axis=-1, keepdims=True) - yh * jnp.mean(dyh * yh, axis=-1, keepdims=True))
    return dx, dg, db


def _rope(v, c, s1, s2):
    return v * c + pltpu.roll(v, BLK - 8, 1) * s1 + pltpu.roll(v, 8, 1) * s2


def _rows(i, tm):
    return i * tm + lax.broadcasted_iota(jnp.int32, (tm, 1), 0)


def _place():
    return lax.axis_index("x"), lax.axis_index("y"), lax.axis_index("c")


def _blk(ref, idx, r, dtype):
    return ref.at[pl.ds(pl.multiple_of(idx * r, 16 if dtype == BF16 else 8), r), :]


class _Gather:
    def __init__(self, arrs):
        self.ins = list(arrs)
        n = len(arrs)
        self.out_shape = [jax.ShapeDtypeStruct((N_DEV * a.shape[0], a.shape[1]), a.dtype) for a in arrs]
        self.scratch = [pltpu.SemaphoreType.DMA((n, 7)), pltpu.SemaphoreType.DMA((n, 7)), pltpu.SemaphoreType.DMA((n,))]

    def _parts(self, ins, outs, sems):
        send_sems, recv_sems, local_sems = sems
        n = len(ins)
        x, y, c = _place()
        me, sibling = (x, y, c), (x, y, 1 - c)
        chips = [(1 - x, y), (x, 1 - y), (1 - x, 1 - y)]

        def rows(a, p):
            return _blk(outs[a], 4 * p[0] + 2 * p[1] + p[2], self.ins[a].shape[0], self.ins[a].dtype)

        def copy(a, k, block, to, src=None):
            return pltpu.make_async_remote_copy(
                src_ref=rows(a, block) if src is None else src, dst_ref=rows(a, block),
                send_sem=send_sems.at[a, k], recv_sem=recv_sems.at[a, k], device_id=to, device_id_type=MESH)

        mine = [pltpu.make_async_copy(ins[a], rows(a, me), local_sems.at[a]) for a in range(n)]
        first = []
        for a in range(n):
            first.append(copy(a, 0, me, sibling, src=ins[a]))
            first += [copy(a, 1 + j, me, (*chip, c), src=ins[a]) for j, chip in enumerate(chips)]
        return n, c, me, sibling, chips, copy, mine, first

    def start(self, ins, outs, sems):
        *_, mine, first = self._parts(ins, outs, sems)
        for cp in mine + first:
            cp.start()

    def finish(self, ins, outs, sems):
        n, c, me, sibling, chips, copy, mine, first = self._parts(ins, outs, sems)
        passed = []
        for j, chip in enumerate(chips):
            for a in range(n):
                copy(a, 1 + j, (*chip, c), me).wait_recv()
                fwd = copy(a, 4 + j, (*chip, c), sibling)
                fwd.start()
                passed.append(fwd)
        for a in range(n):
            copy(a, 0, sibling, me).wait_recv()
            for j, chip in enumerate(chips):
                copy(a, 4 + j, (*chip, 1 - c), me).wait_recv()
        for cp in first + passed:
            cp.wait_send()
        for cp in mine:
            cp.wait()


class _GatherRelay:
    N_COPY = 13

    def __init__(self, arr):
        self.ins = [arr]
        self.r = arr.shape[0]
        self.out_shape = [jax.ShapeDtypeStruct((N_DEV * self.r, arr.shape[1]), arr.dtype)]
        self.scratch = [pltpu.SemaphoreType.DMA((self.N_COPY,)), pltpu.SemaphoreType.DMA((self.N_COPY,)),
                        pltpu.SemaphoreType.DMA]

    def _parts(self, ins, outs, sems):
        send_sems, recv_sems, local_sem = sems
        x, y, c = _place()
        r, half = self.r, self.r // 2
        out = outs[0]
        me, sib, xn, yn, dg = (x, y, c), (x, y, 1 - c), (1 - x, y, c), (x, 1 - y, c), (1 - x, 1 - y, c)
        sx, sy, sd = (1 - x, y, 1 - c), (x, 1 - y, 1 - c), (1 - x, 1 - y, 1 - c)
        lo, hi = (0, half), (half, half)

        def rows(p, part=(0, r)):
            return out.at[pl.ds(pl.multiple_of((4 * p[0] + 2 * p[1] + p[2]) * r + part[0], 16), part[1]), :]

        def own(part):
            return ins[0].at[pl.ds(part[0], part[1]), :]

        def copy(k, dev_rows, to, src=None):
            return pltpu.make_async_remote_copy(
                src_ref=dev_rows if src is None else src, dst_ref=dev_rows,
                send_sem=send_sems.at[k], recv_sem=recv_sems.at[k], device_id=to, device_id_type=MESH)

        mine = pltpu.make_async_copy(ins[0], rows(me), local_sem)
        first = [copy(0, rows(me), sib, src=ins[0]),
                 copy(1, rows(me, lo), xn, src=own(lo)), copy(3, rows(me, hi), yn, src=own(hi)),
                 copy(2, rows(me, hi), xn, src=own(hi)), copy(4, rows(me, lo), yn, src=own(lo))]
        arrive = {0: rows(sib), 1: rows(xn, lo), 2: rows(xn, hi), 3: rows(yn, hi), 4: rows(yn, lo),
                  5: rows(dg, lo), 6: rows(dg, hi), 7: rows(sx, lo), 8: rows(sx, hi), 9: rows(sy, hi),
                  10: rows(sy, lo), 11: rows(sd, lo), 12: rows(sd, hi)}
        relay = {1: [(5, rows(xn, lo), yn), (7, rows(xn, lo), sib)], 3: [(6, rows(yn, hi), xn), (9, rows(yn, hi), sib)],
                 2: [(8, rows(xn, hi), sib)], 4: [(10, rows(yn, lo), sib)],
                 5: [(11, rows(dg, lo), sib)], 6: [(12, rows(dg, hi), sib)]}
        return copy, mine, first, arrive, relay, me

    def start(self, ins, outs, sems):
        _, mine, first, _, _, _ = self._parts(ins, outs, sems)
        for cp in [mine] + first:
            cp.start()

    def finish(self, ins, outs, sems):
        copy, mine, first, arrive, relay, me = self._parts(ins, outs, sems)
        passed = []
        for k in (1, 3, 2, 4, 5, 6):
            copy(k, arrive[k], me).wait_recv()
            for k2, dev_rows, to in relay[k]:
                fwd = copy(k2, dev_rows, to)
                fwd.start()
                passed.append(fwd)
        for k in (0, 7, 8, 9, 10, 11, 12):
            copy(k, arrive[k], me).wait_recv()
        for cp in first + passed:
            cp.wait_send()
        mine.wait()


FLIPS = [(0, 0, 1), (1, 0, 0), (0, 1, 0), (1, 1, 0), (1, 0, 1), (0, 1, 1), (1, 1, 1)]


class _Scatter:
    def __init__(self, arrs, part=0, nparts=1):
        self.ins = list(arrs)
        self.part, self.nparts = part, nparts
        n = len(arrs)
        self.out_shape = [jax.ShapeDtypeStruct((a.shape[0] // nparts, a.shape[1]), a.dtype) for a in arrs]
        self.scratch = [pltpu.SemaphoreType.DMA((n, 7)), pltpu.SemaphoreType.DMA((n, 7)), pltpu.SemaphoreType.DMA((n,))]

    def _parts(self, ins, outs, sems):
        send_sems, recv_sems, local_sems = sems
        n = len(ins)
        x, y, c = _place()
        me = 4 * x + 2 * y + c

        def flip(v, f):
            return 1 - v if f else v

        def src(a, idx):
            r = self.ins[a].shape[0] // N_DEV
            rs = r // self.nparts
            return ins[a].at[pl.ds(pl.multiple_of(idx * r + self.part * rs, 16), rs), :]

        def dst(a, idx):
            rs = self.ins[a].shape[0] // N_DEV // self.nparts
            return outs[a].at[pl.ds(pl.multiple_of(idx * rs, 16), rs), :]

        mine = [pltpu.make_async_copy(src(a, me), dst(a, me), local_sems.at[a]) for a in range(n)]
        sends, recvs = [], []
        for k, f in enumerate(FLIPS):
            peer = (flip(x, f[0]), flip(y, f[1]), flip(c, f[2]))
            pidx = 4 * peer[0] + 2 * peer[1] + peer[2]
            for a in range(n):
                sends.append(pltpu.make_async_remote_copy(
                    src_ref=src(a, pidx), dst_ref=dst(a, me),
                    send_sem=send_sems.at[a, k], recv_sem=recv_sems.at[a, k], device_id=peer, device_id_type=MESH))
                recvs.append(functools.partial(
                    pltpu.make_async_remote_copy,
                    src_ref=src(a, pidx), dst_ref=dst(a, pidx),
                    send_sem=send_sems.at[a, k], recv_sem=recv_sems.at[a, k], device_id=peer, device_id_type=MESH))
        return mine, sends, recvs

    def start(self, ins, outs, sems):
        mine, sends, _ = self._parts(ins, outs, sems)
        for cp in mine + sends:
            cp.start()

    def finish(self, ins, outs, sems):
        mine, sends, recvs = self._parts(ins, outs, sems)
        for make in recvs:
            make().wait_recv()
        for cp in sends:
            cp.wait_send()
        for cp in mine:
            cp.wait()


N_CHIP = 4


class _SiblingSwap:
    def __init__(self, arr):
        self.ins = [arr]
        self.r = arr.shape[0] // N_DEV
        self.out_shape = [jax.ShapeDtypeStruct((N_CHIP * self.r, arr.shape[1]), arr.dtype)]
        self.scratch = [pltpu.SemaphoreType.DMA((N_CHIP,)), pltpu.SemaphoreType.DMA((N_CHIP,))]

    def _copies(self, ins, outs, sems):
        send_sems, recv_sems = sems
        x, y, c = _place()
        r = self.r
        return [pltpu.make_async_remote_copy(
            src_ref=ins[0].at[pl.ds(pl.multiple_of((2 * j + 1 - c) * r, 16), r), :],
            dst_ref=outs[0].at[pl.ds(j * r, r), :],
            send_sem=send_sems.at[j], recv_sem=recv_sems.at[j], device_id=(x, y, 1 - c), device_id_type=MESH)
            for j in range(N_CHIP)]

    def start(self, ins, outs, sems):
        for cp in self._copies(ins, outs, sems):
            cp.start()

    def finish(self, ins, outs, sems):
        for cp in self._copies(ins, outs, sems):
            cp.wait()


class _ChipScatter:
    def __init__(self, arr):
        self.ins = [arr]
        self.r = arr.shape[0] // N_CHIP
        self.out_shape = [jax.ShapeDtypeStruct(arr.shape, arr.dtype)]
        self.scratch = [pltpu.SemaphoreType.DMA((3,)), pltpu.SemaphoreType.DMA((3,)), pltpu.SemaphoreType.DMA]

    def _parts(self, ins, outs, sems):
        send_sems, recv_sems, local_sem = sems
        x, y, c = _place()
        r = self.r
        my_chip = 2 * x + y

        def rows(ref, j):
            return ref.at[pl.ds(pl.multiple_of(j * r, 16), r), :]

        mine = pltpu.make_async_copy(rows(ins[0], my_chip), rows(outs[0], my_chip), local_sem)
        sends, recvs = [], []
        for k, (fx, fy) in enumerate(((1, 0), (0, 1), (1, 1))):
            px, py = (1 - x if fx else x), (1 - y if fy else y)
            peer_chip = 2 * px + py
            sends.append(pltpu.make_async_remote_copy(
                src_ref=rows(ins[0], peer_chip), dst_ref=rows(outs[0], my_chip),
                send_sem=send_sems.at[k], recv_sem=recv_sems.at[k], device_id=(px, py, c), device_id_type=MESH))
            recvs.append(functools.partial(
                pltpu.make_async_remote_copy,
                src_ref=rows(ins[0], peer_chip), dst_ref=rows(outs[0], peer_chip),
                send_sem=send_sems.at[k], recv_sem=recv_sems.at[k], device_id=(px, py, c), device_id_type=MESH))
        return mine, sends, recvs

    def start(self, ins, outs, sems):
        mine, sends, _ = self._parts(ins, outs, sems)
        for cp in [mine] + sends:
            cp.start()

    def finish(self, ins, outs, sems):
        mine, sends, recvs = self._parts(ins, outs, sems)
        for make in recvs:
            make().wait_recv()
        for cp in sends:
            cp.wait_send()
        mine.wait()


def _pair_add(partial, recv):
    r = recv.shape[0] // N_CHIP
    cols = recv.shape[1]
    tr = r // 2 if (r // 2) % 16 == 0 else r
    steps = r // tr
    core = lax.axis_index("c").astype(jnp.int32).reshape(1)

    def body(c_ref, p_ref, s_ref, o_ref):
        o_ref[...] = (p_ref[...].astype(F32) + s_ref[...].astype(F32)).astype(BF16)

    spec = pl.BlockSpec((tr, cols), lambda j, i, c_ref: (j * steps + i, 0))
    return pl.pallas_call(
        body, name="pair_add",
        grid_spec=pltpu.PrefetchScalarGridSpec(
            num_scalar_prefetch=1, grid=(N_CHIP, steps),
            in_specs=[pl.BlockSpec((tr, cols), lambda j, i, c_ref: ((2 * j + c_ref[0]) * steps + i, 0)), spec],
            out_specs=spec),
        out_shape=jax.ShapeDtypeStruct(recv.shape, BF16),
        compiler_params=_cparams("parallel", "parallel"),
    )(core, partial, recv)


class _Both:
    def __init__(self, a, b):
        self.a, self.b = a, b
        self.ins = a.ins + b.ins
        self.out_shape = a.out_shape + b.out_shape
        self.scratch = a.scratch + b.scratch

    def _split(self, ins, outs, sems):
        ni, no, ns = len(self.a.ins), len(self.a.out_shape), len(self.a.scratch)
        return (ins[:ni], outs[:no], sems[:ns]), (ins[ni:], outs[no:], sems[ns:])

    def start(self, ins, outs, sems):
        ra, rb = self._split(ins, outs, sems)
        self.a.start(*ra)
        self.b.start(*rb)

    def finish(self, ins, outs, sems):
        ra, rb = self._split(ins, outs, sems)
        self.a.finish(*ra)
        self.b.finish(*rb)


def _exchange(comm, name):
    n, m = len(comm.ins), len(comm.out_shape)

    def body(*refs):
        ins, outs, sems = refs[:n], refs[n:n + m], refs[n + m:]
        comm.start(ins, outs, sems)
        comm.finish(ins, outs, sems)

    return pl.pallas_call(
        body, name=name, out_shape=comm.out_shape, in_specs=[ANY] * n, out_specs=[ANY] * m, scratch_shapes=comm.scratch,
    )(*comm.ins)


def _call(body, *, name, grid, in_specs, out_specs, out_shape, args, scratch=(), sem="parallel", comm=None):
    if comm is None:
        outs = pl.pallas_call(
            body, name=name, grid=grid, in_specs=list(in_specs), out_specs=list(out_specs), out_shape=list(out_shape),
            scratch_shapes=list(scratch), compiler_params=_cparams(sem))(*args)
        return outs, []
    n_in, n_out, n_sc = len(in_specs), len(out_specs), len(scratch)
    n_ci, n_co = len(comm.ins), len(comm.out_shape)
    last = grid[0] - 1

    def fused(*refs):
        ins, refs = refs[:n_in], refs[n_in:]
        c_ins, refs = refs[:n_ci], refs[n_ci:]
        outs, refs = refs[:n_out], refs[n_out:]
        c_outs, refs = refs[:n_co], refs[n_co:]
        sc, c_sems = refs[:n_sc], refs[n_sc:]
        step = pl.program_id(0)

        @pl.when(step == 0)
        def _():
            comm.start(c_ins, c_outs, c_sems)

        body(*ins, *outs, *sc)

        @pl.when(step == last)
        def _():
            comm.finish(c_ins, c_outs, c_sems)

    outs = pl.pallas_call(
        fused, name=name, grid=grid, in_specs=list(in_specs) + [ANY] * n_ci, out_specs=list(out_specs) + [ANY] * n_co,
        out_shape=list(out_shape) + comm.out_shape, scratch_shapes=list(scratch) + comm.scratch,
        compiler_params=_cparams("arbitrary"))(*args, *comm.ins)
    return outs[:n_out], outs[n_out:]


def _token_specs(tm):
    k = tm // BLK
    return [pl.BlockSpec((BLK, D), functools.partial(lambda i, t: (jnp.maximum(k * i + t - 1, 0), 0), t=t)) for t in range(k)]


def _in_proj(x2d, meta, gain, w_int, b_in, tabs, comm=None):
    p = x2d.shape[0] + BLK
    tm = _row_tile(p)
    k = tm // BLK

    def body(*refs):
        x_refs = refs[:k]
        m_ref, g_ref, w_ref, b_ref, t_ref, h_ref, n1_ref, q_ref, kv_ref, ag_ref, gt_ref = refs[k:]
        i = pl.program_id(0)
        head = jnp.concatenate([jnp.zeros((PAD, D), F32), m_ref[...]], axis=0)
        first = jnp.where(i == 0, head, x_refs[0][...])
        h = jnp.concatenate([first] + [r[...] for r in x_refs[1:]], axis=0) if k > 1 else first
        h_ref[...] = h
        n = _rms(h, g_ref[...]).astype(BF16)
        n1_ref[...] = n
        c, s1, s2 = t_ref[:, 0:128], t_ref[:, 128:256], t_ref[:, 256:384]

        def mm(c0, w):
            return _dot_nt(n, w_ref[c0:c0 + w, :]) + b_ref[:, c0:c0 + w]

        for j in range(4):
            acc = mm(256 * j, 256)
            for t in range(2):
                lo = 256 * j + 128 * t
                q_ref[:, lo:lo + 128] = (_rope(acc[:, 128 * t:128 * (t + 1)], c, s1, s2) * SCALE).astype(BF16)
        acc = mm(1024, 256)
        kv_ref[:, 0:128] = _rope(acc[:, 0:128], c, s1, s2).astype(BF16)
        kv_ref[:, 128:256] = acc[:, 128:256].astype(BF16)
        for j in range(8):
            ag_ref[:, 256 * j:256 * (j + 1)] = mm(QKV_W + 256 * j, 256).astype(BF16)
        for j in range(8):
            gt_ref[:, 256 * j:256 * (j + 1)] = mm(QKV_W + 2048 + 256 * j, 256).astype(BF16)

    def row(w):
        return pl.BlockSpec((tm, w), lambda i: (i, 0))

    return _call(
        body, name="in_proj", grid=(p // tm,),
        in_specs=_token_specs(tm) + [VM, VM, VM, VM, row(384)],
        out_specs=[row(D), row(D), row(D), row(256), row(2048), row(2048)],
        out_shape=[jax.ShapeDtypeStruct((p, D), F32)] + [jax.ShapeDtypeStruct((p, w), BF16) for w in (D, D, 256, 2048, 2048)],
        args=(x2d,) * k + (meta, gain, w_int, b_in, tabs), comm=comm)


N_KEY = 2 * BLK + N_META


def _attn_setup(n, h, q_ref, km_ref, kp_ref, kc_ref):
    lo = lax.broadcasted_iota(jnp.int32, (BLK, BLK), 1) < HEAD_DIM
    lok = lax.broadcasted_iota(jnp.int32, (N_KEY, BLK), 1) < HEAD_DIM

    def dup(lanes):
        cat = jnp.concatenate([kp_ref[:, lanes], kc_ref[:, lanes], km_ref[PAD:BLK, lanes]], axis=0).astype(F32)
        rolled = pltpu.roll(cat, HEAD_DIM, 1)
        return (jnp.where(lok, cat, rolled) if h == 0 else jnp.where(lok, rolled, cat)).astype(BF16)

    k2 = dup(slice(0, 128))
    v2 = dup(slice(128, 256))
    qs = _stack_heads(q_ref, h, lo)

    kr = lax.broadcasted_iota(jnp.int32, (BLK, BLK), 0)
    tq = BLK * n + lax.broadcasted_iota(jnp.int32, (BLK, BLK), 1) - PAD
    t_p = BLK * (n - 1) + kr - PAD
    t_c = BLK * n + kr - PAD
    ok_p = jnp.logical_and(t_p >= N_META, tq - t_p < BLK)
    ok_c = jnp.logical_and(t_c >= N_META, t_c <= tq)
    ok_m = lax.broadcasted_iota(jnp.int32, (N_META, BLK), 0) <= BLK * n + lax.broadcasted_iota(jnp.int32, (N_META, BLK), 1) - PAD
    bias = jnp.concatenate([jnp.where(ok, 0.0, NEG_INF).astype(F32) for ok in (ok_p, ok_c, ok_m)], axis=0)
    return qs, k2, v2, bias, lok


def _attn_head(s, bias, sink):
    s = s + bias
    m = jnp.maximum(jnp.max(s, axis=0, keepdims=True), sink)
    e = jnp.exp(s - m)
    es = jnp.exp(sink - m)
    inv = 1.0 / (jnp.sum(e, axis=0, keepdims=True) + es)
    return e * inv, es * inv


def _stack_heads(ref, h, lo):
    pieces = []
    for jp in range(4):
        v = ref[:, BLK * (4 * h + jp):BLK * (4 * h + jp + 1)]
        zero = jnp.zeros_like(v)
        pieces += [jnp.where(lo, v, zero), jnp.where(lo, zero, v)]
    return jnp.concatenate(pieces, axis=0)


def _unstack_heads(v, jp, lo):
    return jnp.where(lo, v[256 * jp:256 * jp + 128], v[256 * jp + 128:256 * jp + 256])


def _attn_fwd(q, kv, sinks, comm=None):
    p = q.shape[0]
    nb = p // BLK

    def body(q_ref, km_ref, kp_ref, kc_ref, sink_ref, o_ref):
        n = pl.program_id(0)
        lo = lax.broadcasted_iota(jnp.int32, (BLK, BLK), 1) < HEAD_DIM
        for h in range(2):
            qs, k2, v2, bias, _ = _attn_setup(n, h, q_ref, km_ref, kp_ref, kc_ref)
            st = _dot_nt(k2, qs)
            pt = jnp.concatenate(
                [_attn_head(st[:, BLK * g:BLK * (g + 1)], bias, sink_ref[0, 8 * h + g])[0].astype(BF16) for g in range(8)],
                axis=1)
            o = _dot_tn(pt, v2)
            for jp in range(4):
                o_ref[:, BLK * (4 * h + jp):BLK * (4 * h + jp + 1)] = _unstack_heads(o, jp, lo).astype(BF16)

    return _call(
        body, name="attn_fwd", grid=(nb,),
        in_specs=[pl.BlockSpec((BLK, D), lambda i: (i, 0)),
                  pl.BlockSpec((BLK, 256), lambda i: (0, 0)),
                  pl.BlockSpec((BLK, 256), lambda i: (jnp.maximum(i - 1, 0), 0)),
                  pl.BlockSpec((BLK, 256), lambda i: (i, 0)),
                  pl.BlockSpec(memory_space=pltpu.SMEM)],
        out_specs=[pl.BlockSpec((BLK, D), lambda i: (i, 0))],
        out_shape=[jax.ShapeDtypeStruct((p, D), BF16)],
        args=(q, kv, kv, kv, sinks), comm=comm)


def _conv31_fwd(ag, w32, b, comm=None):
    p = ag.shape[0]
    nch = p // BLK

    def body(a_ref, g_ref, w_ref, b_ref, o_ref, gp):
        gp[0:32, :] = jnp.zeros((32, BLK), F32)
        for ci in range(nch):
            r0 = BLK * ci
            glu = a_ref[r0:r0 + BLK, :].astype(F32) * jax.nn.sigmoid(g_ref[r0:r0 + BLK, :].astype(F32))
            if ci == 0:
                glu = jnp.where(_rows(0, BLK) >= PAD, glu, 0.0)
            gp[32 + r0:32 + r0 + BLK, :] = glu
        for ci in range(nch):
            r0 = BLK * ci
            acc = jnp.broadcast_to(b_ref[...], (BLK, BLK))
            for j in range(CONV_K):
                acc = acc + w_ref[j:j + 1, :] * gp[r0 + j + 2:r0 + j + 2 + BLK, :]
            o_ref[r0:r0 + BLK, :] = acc

    return _call(
        body, name="conv31_fwd", grid=(D // BLK,),
        in_specs=[pl.BlockSpec((p, BLK), lambda j: (0, j)), pl.BlockSpec((p, BLK), lambda j: (0, 8 + j)),
                  pl.BlockSpec((32, BLK), lambda j: (0, j)), pl.BlockSpec((1, BLK), lambda j: (0, j))],
        out_specs=[pl.BlockSpec((p, BLK), lambda j: (0, j))],
        out_shape=[jax.ShapeDtypeStruct((p, D), F32)],
        scratch=[pltpu.VMEM((p + 32, BLK), F32)],
        args=(ag, ag, w32, b), comm=comm)


def _mixer_fwd(ao, c0, gates, h0p, wa, wc, wo, vecs):
    p = ao.shape[0]
    tm = _row_tile(p)

    def body(ao_ref, c0_ref, gt_ref, h_ref, wa_ref, wc_ref, wo_ref, v_ref,
             c1_ref, at_ref, cv_ref, mg_ref, mix_ref, h1_ref, n2_ref):
        i = pl.program_id(0)
        c1 = _lnsilu(c0_ref[...], v_ref[0:1, :], v_ref[1:2, :]).astype(BF16)
        c1_ref[...] = c1
        attn = _dot(ao_ref[...], wa_ref[...])
        conv = _dot(c1, wc_ref[...]) + v_ref[2:3, :]
        at_ref[...] = attn.astype(BF16)
        cv_ref[...] = conv.astype(BF16)
        merged = (jax.nn.sigmoid(gt_ref[:, 0:D].astype(F32)) * attn
                  + jax.nn.sigmoid(gt_ref[:, D:2 * D].astype(F32)) * conv).astype(BF16)
        mg_ref[...] = merged
        mix = _dot(merged, wo_ref[...])
        mix_ref[...] = mix
        h1 = jnp.where(_rows(i, tm) >= PAD, h_ref[...] + _rms(mix, v_ref[3:4, :]), 0.0)
        h1_ref[...] = h1
        n2_ref[...] = _rms(h1, v_ref[4:5, :]).astype(BF16)

    def row(w):
        return pl.BlockSpec((tm, w), lambda i: (i, 0))

    return pl.pallas_call(
        body, name="mixer_fwd", grid=(p // tm,),
        in_specs=[row(D), row(D), row(2 * D), row(D), VM, VM, VM, VM],
        out_specs=[row(D)] * 7,
        out_shape=[jax.ShapeDtypeStruct((p, D), t) for t in (BF16, BF16, BF16, BF16, F32, F32, BF16)],
        compiler_params=_cparams("parallel"),
    )(ao, c0, gates, h0p, wa, wc, wo, vecs)


def _mm_nt(a, w_t, name):
    p, k = a.shape
    n = w_t.shape[0]
    tm = _row_tile(p)
    ch = 512

    def body(a_ref, w_ref, o_ref):
        a_v = a_ref[...]
        for c0 in range(0, n, ch):
            o_ref[:, c0:c0 + ch] = _dot_nt(a_v, w_ref[c0:c0 + ch, :]).astype(BF16)

    return pl.pallas_call(
        body, name=name, grid=(p // tm,),
        in_specs=[pl.BlockSpec((tm, k), lambda i: (i, 0)), VM],
        out_specs=pl.BlockSpec((tm, n), lambda i: (i, 0)),
        out_shape=jax.ShapeDtypeStruct((p, n), BF16),
        compiler_params=_cparams("parallel"),
    )(a, w_t)


def _conv3(xp_ref, w_ref, r0):
    return (w_ref[0:1, :] * xp_ref[r0 + 6:r0 + 6 + BLK, :] + w_ref[1:2, :] * xp_ref[r0 + 7:r0 + 7 + BLK, :]
            + w_ref[2:3, :] * xp_ref[r0 + 8:r0 + 8 + BLK, :])


def _ffn_slab_specs(p):
    ncol = FFN // BLK
    return [pl.BlockSpec((p, BLK), lambda j: (0, j)), pl.BlockSpec((p, BLK), lambda j: (0, ncol + j)),
            pl.BlockSpec((FFN_K, BLK), lambda j: (0, j)), pl.BlockSpec((FFN_K, BLK), lambda j: (0, ncol + j)),
            pl.BlockSpec((1, BLK), lambda j: (0, j)), pl.BlockSpec((1, BLK), lambda j: (0, ncol + j))]


def _fill_shifted(dst, src_ref, nch):
    dst[0:8, :] = jnp.zeros((8, BLK), F32)
    for ci in range(nch):
        dst[8 + BLK * ci:8 + BLK * (ci + 1), :] = src_ref[BLK * ci:BLK * (ci + 1), :].astype(F32)


def _ffn_act(u0, fw, fb):
    p = u0.shape[0]
    nch = p // BLK

    def body(g_ref, v_ref, wg_ref, wv_ref, bg_ref, bv_ref, o_ref, dv_ref, dg_ref, xg, xv):
        _fill_shifted(xg, g_ref, nch)
        _fill_shifted(xv, v_ref, nch)
        for ci in range(nch):
            r0 = BLK * ci
            ug = _conv3(xg, wg_ref, r0) + bg_ref[...]
            uv = _conv3(xv, wv_ref, r0) + bv_ref[...]
            sg = jax.nn.sigmoid(ug)
            silu = ug * sg
            o_ref[r0:r0 + BLK, :] = (silu * uv).astype(BF16)
            dv_ref[r0:r0 + BLK, :] = silu.astype(BF16)
            dg_ref[r0:r0 + BLK, :] = (uv * (sg * (1.0 + ug * (1.0 - sg)))).astype(BF16)

    slab = pl.BlockSpec((p, BLK), lambda j: (0, j))
    return pl.pallas_call(
        body, name="ffn_act", grid=(FFN // BLK,),
        in_specs=_ffn_slab_specs(p),
        out_specs=[slab] * 3,
        out_shape=[jax.ShapeDtypeStruct((p, FFN), BF16)] * 3,
        scratch_shapes=[pltpu.VMEM((p + 8, BLK), F32)] * 2,
        compiler_params=_cparams("parallel"),
    )(u0, u0, fw, fw, fb, fb)


def _ffn_down_loss(act, wd, h1, tgt, gain):
    p = act.shape[0]
    tm = _row_tile(p)
    k = tm // BLK

    def body(*refs):
        a_ref, w_ref, h_ref = refs[:3]
        t_refs = refs[3:3 + k]
        g_ref, df_ref, da_ref, dy_ref, acc_ref = refs[3 + k:]
        i = pl.program_id(0)

        @pl.when(i == 0)
        def _():
            acc_ref[...] = jnp.zeros_like(acc_ref)

        ffn = _dot(a_ref[...], w_ref[...])
        t = jnp.concatenate([t_ref[...] for t_ref in t_refs], axis=0) if k > 1 else t_refs[0][...]
        diff = jnp.where(_rows(i, tm) >= BLK, h_ref[...] + _rms(ffn, g_ref[...]) - t, 0.0)
        dy = diff * (1.0 / D)
        dffn, dg = _rms_bwd(ffn, g_ref[...], dy)
        acc_ref[0:1, :] += dg
        acc_ref[1:2, :] += jnp.sum(diff * diff, axis=0, keepdims=True) * (0.5 / D)
        dy_ref[...] = dy
        dfb = dffn.astype(BF16)
        df_ref[...] = dfb
        for c0 in range(0, FFN, 256):
            da_ref[:, c0:c0 + 256] = _dot_nt(dfb, w_ref[c0:c0 + 256, :]).astype(BF16)

    def row(w):
        return pl.BlockSpec((tm, w), lambda i: (i, 0))

    return pl.pallas_call(
        body, name="ffn_down_loss", grid=(p // tm,),
        in_specs=[row(FFN), VM, row(D)] + _token_specs(tm) + [VM],
        out_specs=[row(D), row(FFN), row(D), pl.BlockSpec((8, D), lambda i: (0, 0))],
        out_shape=[jax.ShapeDtypeStruct((p, D), BF16), jax.ShapeDtypeStruct((p, FFN), BF16),
                   jax.ShapeDtypeStruct((p, D), F32), jax.ShapeDtypeStruct((8, D), F32)],
        compiler_params=_cparams("arbitrary"),
    )(act, wd, h1, *([tgt] * k), gain)


def _mm_tn(pieces, b, name, col_sums=False, comm=None):
    p, n = b.shape
    tk = 256
    nblk = [a.shape[1] // tk for a in pieces]
    offs = [sum(nblk[:q]) for q in range(len(pieces))]
    total = sum(nblk)
    npc = len(pieces)

    def body(*refs):
        a_refs, b_ref, o_ref = refs[:npc], refs[npc], refs[npc + 1]
        i = pl.program_id(0)
        for q, a_ref in enumerate(a_refs):
            @pl.when(jnp.logical_and(i >= offs[q], i < offs[q] + nblk[q]))
            def _(a_ref=a_ref):
                a_v = a_ref[...]
                o_ref[...] = _dot_tn(a_v, b_ref[...]).astype(BF16)
                if col_sums:
                    refs[npc + 2][...] = jnp.sum(a_v.astype(F32), axis=0, keepdims=True)

    def a_spec(q):
        return pl.BlockSpec((p, tk), lambda i: (0, jnp.clip(i - offs[q], 0, nblk[q] - 1)))

    out_specs = [pl.BlockSpec((tk, n), lambda i: (i, 0))]
    out_shape = [jax.ShapeDtypeStruct((total * tk, n), BF16)]
    if col_sums:
        out_specs.append(pl.BlockSpec((1, tk), lambda i: (0, i)))
        out_shape.append(jax.ShapeDtypeStruct((1, total * tk), F32))
    res, sent = _call(
        body, name=name, grid=(total,),
        in_specs=[a_spec(q) for q in range(npc)] + [VM],
        out_specs=out_specs, out_shape=out_shape, args=(*pieces, b), comm=comm)
    res = res if col_sums else res[0]
    return res if comm is None else (res, sent)


def _ffn_act_bwd(u0, dact, dact_dg, dact_dv, fw, act, dffn, comm=None):
    p = u0.shape[0]
    nch = p // BLK
    ncol = FFN // BLK

    def body(g_ref, v_ref, wg_ref, wv_ref, da_ref, lg_ref, lv_ref, act_ref, df_ref,
             dg_ref, dv_ref, gwg_ref, gwv_ref, gbg_ref, gbv_ref, gwd_ref, eg, ev):
        gwd_ref[...] = _dot_tn(act_ref[...], df_ref[...]).astype(BF16)
        eg[p:p + 8, :] = jnp.zeros((8, BLK), F32)
        ev[p:p + 8, :] = jnp.zeros((8, BLK), F32)
        for ci in range(nch):
            r0 = BLK * ci
            d = da_ref[r0:r0 + BLK, :].astype(F32)
            eg[r0:r0 + BLK, :] = d * lg_ref[r0:r0 + BLK, :].astype(F32)
            ev[r0:r0 + BLK, :] = d * lv_ref[r0:r0 + BLK, :].astype(F32)
        def fold(v):
            return jnp.sum(v.reshape(BLK // 8, 8, BLK), axis=0)

        for e_s, x_ref, w_ref, d_ref, gw_ref, gb_ref in ((eg, g_ref, wg_ref, dg_ref, gwg_ref, gbg_ref),
                                                        (ev, v_ref, wv_ref, dv_ref, gwv_ref, gbv_ref)):
            sums = [jnp.zeros((8, BLK), F32) for _ in range(FFN_K + 1)]
            for ci in range(nch):
                r0 = BLK * ci
                es = [e_s[r0 + t:r0 + t + BLK, :] for t in range(FFN_K)]
                du = w_ref[2:3, :] * es[0] + w_ref[1:2, :] * es[1] + w_ref[0:1, :] * es[2]
                if ci == 0:
                    du = jnp.where(_rows(0, BLK) >= PAD, du, 0.0)
                d_ref[r0:r0 + BLK, :] = du.astype(BF16)
                x = x_ref[r0:r0 + BLK, :].astype(F32)
                for j in range(FFN_K):
                    sums[j] = sums[j] + fold(es[FFN_K - 1 - j] * x)
                sums[FFN_K] = sums[FFN_K] + fold(es[0])
            for j in range(FFN_K):
                gw_ref[j:j + 1, :] = jnp.sum(sums[j], axis=0, keepdims=True)
            gb_ref[...] = jnp.sum(sums[FFN_K], axis=0, keepdims=True)

    slab = pl.BlockSpec((p, BLK), lambda j: (0, j))
    wspec = pl.BlockSpec((FFN_K, BLK), lambda j: (0, j))
    bspec = pl.BlockSpec((1, BLK), lambda j: (0, j))
    return _call(
        body, name="ffn_act_bwd", grid=(ncol,),
        in_specs=_ffn_slab_specs(p)[:4] + [slab] * 4 + [VM],
        out_specs=[slab, slab, wspec, wspec, bspec, bspec, pl.BlockSpec((BLK, D), lambda j: (j, 0))],
        out_shape=[jax.ShapeDtypeStruct((p, FFN), BF16)] * 2 + [jax.ShapeDtypeStruct((FFN_K, FFN), F32)] * 2
        + [jax.ShapeDtypeStruct((1, FFN), F32)] * 2 + [jax.ShapeDtypeStruct((FFN, D), BF16)],
        scratch=[pltpu.VMEM((p + 8, BLK), F32)] * 2,
        args=(u0, u0, fw, fw, dact, dact_dg, dact_dv, act, dffn), comm=comm)


def _ffn_in_bwd(dug, duv, w_upt, h1, dy, gain, comm=None):
    p = h1.shape[0]
    tm = _row_tile(p)

    def body(dg_ref, dv_ref, w_ref, h_ref, dy_ref, g_ref, o_ref, acc_ref):
        i = pl.program_id(0)

        @pl.when(i == 0)
        def _():
            acc_ref[...] = jnp.zeros_like(acc_ref)

        dn = _dot(dg_ref[...], w_ref[0:FFN, :]) + _dot(dv_ref[...], w_ref[FFN:2 * FFN, :])
        dh, dg = _rms_bwd(h_ref[...], g_ref[...], dn)
        o_ref[...] = dy_ref[...] + dh
        acc_ref[0:1, :] += dg

    def row(w):
        return pl.BlockSpec((tm, w), lambda i: (i, 0))

    return _call(
        body, name="ffn_in_bwd", grid=(p // tm,),
        in_specs=[row(FFN), row(FFN), VM, row(D), row(D), VM],
        out_specs=[row(D), pl.BlockSpec((8, D), lambda i: (0, 0))],
        out_shape=[jax.ShapeDtypeStruct((p, D), F32), jax.ShapeDtypeStruct((8, D), F32)],
        sem="arbitrary", args=(dug, duv, w_upt, h1, dy, gain), comm=comm)


def _mixer_bwd(dh1, mix, attn, conv, gates, c0, wa, wc, wo, vecs, comm=None):
    p = dh1.shape[0]
    tm = _row_tile(p)

    def body(dh_ref, mix_ref, at_ref, cv_ref, gt_ref, c0_ref, wa_ref, wc_ref, wo_ref, v_ref,
             dmix_ref, dat_ref, dcv_ref, dgt_ref, dao_ref, dc0_ref, acc_ref):
        i = pl.program_id(0)

        @pl.when(i == 0)
        def _():
            acc_ref[...] = jnp.zeros_like(acc_ref)

        dmix, dgp = _rms_bwd(mix_ref[...], v_ref[3:4, :], dh_ref[...])
        dmix = dmix.astype(BF16)
        dmix_ref[...] = dmix
        dmg = _dot_nt(dmix, wo_ref[...])
        sa = jax.nn.sigmoid(gt_ref[:, 0:D].astype(F32))
        sc = jax.nn.sigmoid(gt_ref[:, D:2 * D].astype(F32))
        dat = dmg * sa
        dcv = dmg * sc
        dgt_ref[:, 0:D] = (dmg * at_ref[...].astype(F32) * sa * (1.0 - sa)).astype(BF16)
        dgt_ref[:, D:2 * D] = (dmg * cv_ref[...].astype(F32) * sc * (1.0 - sc)).astype(BF16)
        datb = dat.astype(BF16)
        dcvb = dcv.astype(BF16)
        dat_ref[...] = datb
        dcv_ref[...] = dcvb
        dao_ref[...] = _dot_nt(datb, wa_ref[...]).astype(BF16)
        dc1 = _dot_nt(dcvb, wc_ref[...])
        dc0, dlg, dlb = _lnsilu_bwd(c0_ref[...], v_ref[0:1, :], v_ref[1:2, :], dc1)
        dc0_ref[...] = dc0
        acc_ref[0:1, :] += dgp
        acc_ref[1:2, :] += jnp.sum(dcv, axis=0, keepdims=True)
        acc_ref[2:3, :] += dlg
        acc_ref[3:4, :] += dlb

    def row(w):
        return pl.BlockSpec((tm, w), lambda i: (i, 0))

    return _call(
        body, name="mixer_bwd", grid=(p // tm,),
        in_specs=[row(D), row(D), row(D), row(D), row(2 * D), row(D), VM, VM, VM, VM],
        out_specs=[row(D), row(D), row(D), row(2 * D), row(D), row(D), pl.BlockSpec((8, D), lambda i: (0, 0))],
        out_shape=[jax.ShapeDtypeStruct((p, D), BF16)] * 3 + [jax.ShapeDtypeStruct((p, 2 * D), BF16),
                                                             jax.ShapeDtypeStruct((p, D), BF16),
                                                             jax.ShapeDtypeStruct((p, D), F32),
                                                             jax.ShapeDtypeStruct((8, D), F32)],
        sem="arbitrary", args=(dh1, mix, attn, conv, gates, c0, wa, wc, wo, vecs), comm=comm)


def _conv31_bwd(ag, dc0, w32, tn_pairs, comm=None):
    p = ag.shape[0]
    nch = p // BLK
    npair = len(tn_pairs)

    def body(*refs):
        a_ref, g_ref, dc_ref, w_ref = refs[:4]
        tn_a, tn_b = refs[4:4 + npair], refs[4 + npair:4 + 2 * npair]
        da_ref, dg_ref, gw_ref, gb_ref = refs[4 + 2 * npair:8 + 2 * npair]
        tn_o = refs[8 + 2 * npair:8 + 3 * npair]
        gp, dp = refs[8 + 3 * npair:]
        for ta, tb, to in zip(tn_a, tn_b, tn_o):
            to[...] = _dot_tn(ta[...], tb[...]).astype(BF16)
        gp[0:32, :] = jnp.zeros((32, BLK), F32)
        dp[p:p + 32, :] = jnp.zeros((32, BLK), F32)
        bsum = jnp.zeros((BLK, BLK), F32)
        for ci in range(nch):
            r0 = BLK * ci
            glu = a_ref[r0:r0 + BLK, :].astype(F32) * jax.nn.sigmoid(g_ref[r0:r0 + BLK, :].astype(F32))
            if ci == 0:
                glu = jnp.where(_rows(0, BLK) >= PAD, glu, 0.0)
            gp[32 + r0:32 + r0 + BLK, :] = glu
            d = dc_ref[r0:r0 + BLK, :]
            dp[r0:r0 + BLK, :] = d
            bsum = bsum + d
        gb_ref[...] = jnp.sum(bsum, axis=0, keepdims=True)
        for ci in range(nch):
            r0 = BLK * ci
            acc = jnp.zeros((BLK, BLK), F32)
            for j in range(CONV_K):
                acc = acc + w_ref[j:j + 1, :] * dp[r0 + 30 - j:r0 + 30 - j + BLK, :]
            if ci == 0:
                acc = jnp.where(_rows(0, BLK) >= PAD, acc, 0.0)
            a = a_ref[r0:r0 + BLK, :].astype(F32)
            sg = jax.nn.sigmoid(g_ref[r0:r0 + BLK, :].astype(F32))
            da_ref[r0:r0 + BLK, :] = (acc * sg).astype(BF16)
            dg_ref[r0:r0 + BLK, :] = (acc * a * sg * (1.0 - sg)).astype(BF16)
        sub = BLK // 2
        accs = [jnp.zeros((8, BLK), F32) for _ in range(CONV_K)]
        for r0 in range(0, p, sub):
            d = dp[r0:r0 + sub, :]
            for j in range(CONV_K):
                prod = d * gp[r0 + j + 2:r0 + j + 2 + sub, :]
                accs[j] = accs[j] + jnp.sum(prod.reshape(sub // 8, 8, BLK), axis=0)
        for j in range(CONV_K):
            gw_ref[j:j + 1, :] = jnp.sum(accs[j], axis=0, keepdims=True)
        gw_ref[CONV_K:32, :] = jnp.zeros((32 - CONV_K, BLK), F32)

    slab = pl.BlockSpec((p, BLK), lambda j: (0, j))
    return _call(
        body, name="conv31_bwd", grid=(D // BLK,),
        in_specs=[slab, pl.BlockSpec((p, BLK), lambda j: (0, 8 + j)), slab, pl.BlockSpec((32, BLK), lambda j: (0, j))]
        + [slab] * npair + [VM] * npair,
        out_specs=[slab, slab, pl.BlockSpec((32, BLK), lambda j: (0, j)), pl.BlockSpec((1, BLK), lambda j: (0, j))]
        + [pl.BlockSpec((BLK, D), lambda j: (j, 0))] * npair,
        out_shape=[jax.ShapeDtypeStruct((p, D), BF16)] * 2 + [jax.ShapeDtypeStruct((32, D), F32),
                                                             jax.ShapeDtypeStruct((1, D), F32)]
        + [jax.ShapeDtypeStruct((D, D), BF16)] * npair,
        scratch=[pltpu.VMEM((p + 32, BLK), F32)] * 2,
        args=(ag, ag, dc0, w32, *[a for a, _ in tn_pairs], *[b for _, b in tn_pairs]), comm=comm)


def _attn_bwd(q, kv, dao, sinks, tabs, comm=None):
    p = q.shape[0]
    nb = p // BLK

    def body(q_ref, km_ref, kp_ref, kc_ref, do_ref, sink_ref, t_ref, dqkv_ref, dsink_ref, carry, macc):
        i = pl.program_id(0)
        n = nb - 1 - i

        @pl.when(i == 0)
        def _():
            carry[...] = jnp.zeros_like(carry)
            macc[...] = jnp.zeros_like(macc)
            dsink_ref[...] = jnp.zeros_like(dsink_ref)

        lo = lax.broadcasted_iota(jnp.int32, (BLK, BLK), 1) < HEAD_DIM
        lane8 = lax.broadcasted_iota(jnp.int32, (8, BLK), 1)
        c, s1, s2 = t_ref[:, 0:128], -t_ref[:, 128:256], -t_ref[:, 256:384]
        dk = jnp.zeros((N_KEY, BLK), F32)
        dv = jnp.zeros((N_KEY, BLK), F32)
        for h in range(2):
            qs, k2, v2, bias, lok = _attn_setup(n, h, q_ref, km_ref, kp_ref, kc_ref)
            dos = _stack_heads(do_ref, h, lo)
            st = _dot_nt(k2, qs)
            dpt = _dot_nt(v2, dos)
            p_parts, ds_parts = [], []
            for g in range(8):
                cols = slice(BLK * g, BLK * (g + 1))
                pn, ps = _attn_head(st[:, cols], bias, sink_ref[0, 8 * h + g])
                dp = dpt[:, cols]
                delta = jnp.sum(pn * dp, axis=0, keepdims=True)
                ds_parts.append((pn * (dp - delta)).astype(BF16))
                p_parts.append(pn.astype(BF16))
                dsk = -jnp.sum(ps * delta, axis=1, keepdims=True)
                dsink_ref[...] += jnp.where(lane8 == 8 * h + g, dsk, 0.0)
            dst = jnp.concatenate(ds_parts, axis=1)
            pt = jnp.concatenate(p_parts, axis=1)
            dq = _dot_tn(dst, k2)
            for jp in range(4):
                lo_c = BLK * (4 * h + jp)
                dqkv_ref[:, lo_c:lo_c + BLK] = (_rope(_unstack_heads(dq, jp, lo), c, s1, s2) * SCALE).astype(BF16)
            dk2 = _dot(dst, qs)
            dv2 = _dot(pt, dos)
            dk2 = dk2 + pltpu.roll(dk2, HEAD_DIM, 1)
            dv2 = dv2 + pltpu.roll(dv2, HEAD_DIM, 1)
            own = lok if h == 0 else jnp.logical_not(lok)
            dk = jnp.where(own, dk2, dk)
            dv = jnp.where(own, dv2, dv)
        macc[:, 0:BLK] += dk[2 * BLK:N_KEY]
        macc[:, BLK:2 * BLK] += dv[2 * BLK:N_KEY]
        last = (n == 0).astype(F32)
        zpad = jnp.zeros((PAD, BLK), F32)
        dk_c = dk[BLK:2 * BLK] + carry[:, 0:BLK] + last * jnp.concatenate([zpad, macc[:, 0:BLK]], axis=0)
        dv_c = dv[BLK:2 * BLK] + carry[:, BLK:2 * BLK] + last * jnp.concatenate([zpad, macc[:, BLK:2 * BLK]], axis=0)
        carry[:, 0:BLK] = dk[0:BLK]
        carry[:, BLK:2 * BLK] = dv[0:BLK]
        dqkv_ref[:, D:D + BLK] = _rope(dk_c, c, s1, s2).astype(BF16)
        dqkv_ref[:, D + BLK:D + 2 * BLK] = dv_c.astype(BF16)

    def rev(w):
        return pl.BlockSpec((BLK, w), lambda i: (nb - 1 - i, 0))

    return _call(
        body, name="attn_bwd", grid=(nb,),
        in_specs=[rev(D),
                  pl.BlockSpec((BLK, 256), lambda i: (0, 0)),
                  pl.BlockSpec((BLK, 256), lambda i: (jnp.maximum(nb - 2 - i, 0), 0)),
                  rev(256), rev(D),
                  pl.BlockSpec(memory_space=pltpu.SMEM), rev(384)],
        out_specs=[rev(QKV_W), pl.BlockSpec((8, BLK), lambda i: (0, 0))],
        out_shape=[jax.ShapeDtypeStruct((p, QKV_W), BF16), jax.ShapeDtypeStruct((8, BLK), F32)],
        scratch=[pltpu.VMEM((BLK, 256), F32), pltpu.VMEM((N_META, 256), F32)], sem="arbitrary",
        args=(q, kv, kv, kv, dao, sinks, tabs), comm=comm)


def _in_bwd(dqkv, da, dg, dgt, w_int, h0p, dh1, gain, comm=None):
    p = h0p.shape[0]
    tm = _row_tile(p)
    nt = p // tm
    first_rows = tm - BLK

    def body(dq_ref, da_ref, dg_ref, dt_ref, w_ref, h_ref, dh_ref, g_ref, gx_ref, dm_ref, acc_ref, buf, sems):
        i = pl.program_id(0)
        slot = i % 2

        @pl.when(i == 0)
        def _():
            acc_ref[...] = jnp.zeros_like(acc_ref)

        dn = (_dot(dq_ref[...], w_ref[0:QKV_W, :]) + _dot(da_ref[...], w_ref[QKV_W:QKV_W + D, :])
              + _dot(dg_ref[...], w_ref[QKV_W + D:QKV_W + 2 * D, :]) + _dot(dt_ref[...], w_ref[QKV_W + 2 * D:IN_W, :]))
        dh, dgain = _rms_bwd(h_ref[...], g_ref[...], dn)
        dh0 = dh_ref[...] + dh
        acc_ref[0:1, :] += dgain
        buf[slot] = dh0

        @pl.when(i == 0)
        def _():
            dm_ref[...] = dh0[PAD:BLK]

        def first_copy():
            return pltpu.make_async_copy(buf.at[0, pl.ds(BLK, first_rows), :], gx_ref.at[pl.ds(0, first_rows), :], sems.at[0])

        def tile_copy(j, s):
            return pltpu.make_async_copy(buf.at[s], gx_ref.at[pl.ds(pl.multiple_of(j * tm - BLK, BLK), tm), :], sems.at[s])

        if first_rows:
            @pl.when(i == 1)
            def _():
                first_copy().wait()

        @pl.when(i >= 2)
        def _():
            tile_copy(i - 1, 1 - slot).wait()

        if first_rows:
            @pl.when(i == 0)
            def _():
                first_copy().start()

        @pl.when(i > 0)
        def _():
            tile_copy(i, slot).start()

        @pl.when(i == nt - 1)
        def _():
            tile_copy(i, slot).wait()

    def row(w):
        return pl.BlockSpec((tm, w), lambda i: (i, 0))

    return _call(
        body, name="in_bwd", grid=(nt,),
        in_specs=[row(QKV_W), row(D), row(D), row(2 * D), VM, row(D), row(D), VM],
        out_specs=[ANY, pl.BlockSpec((N_META, D), lambda i: (0, 0)), pl.BlockSpec((8, D), lambda i: (0, 0))],
        out_shape=[jax.ShapeDtypeStruct((p - BLK, D), F32), jax.ShapeDtypeStruct((N_META, D), F32),
                   jax.ShapeDtypeStruct((8, D), F32)],
        scratch=[pltpu.VMEM((2, tm, D), F32), pltpu.SemaphoreType.DMA((2,))],
        sem="arbitrary", args=(dqkv, da, dg, dgt, w_int, h0p, dh1, gain), comm=comm)


def _sum_slots(slots, name):
    r = slots.shape[0] // N_DEV
    cols = slots.shape[1]
    tr = r if r <= 352 else (r // 2 if (r // 2) % 16 == 0 else r // 3)
    steps = r // tr

    def body(*refs):
        acc = refs[0][...].astype(F32)
        for s in range(1, N_DEV):
            acc = acc + refs[s][...].astype(F32)
        refs[N_DEV][...] = acc

    return pl.pallas_call(
        body, name=name, grid=(steps,),
        in_specs=[pl.BlockSpec((tr, cols), functools.partial(lambda i, s: (s * steps + i, 0), s=s)) for s in range(N_DEV)],
        out_specs=pl.BlockSpec((tr, cols), lambda i: (i, 0)),
        out_shape=jax.ShapeDtypeStruct((r, cols), F32),
        compiler_params=_cparams("parallel"),
    )(*([slots] * N_DEV))


def _adamw_math(w, g, m, v):
    m_n = ADAM_B1 * m + (1.0 - ADAM_B1) * g
    v_n = ADAM_B2 * v + (1.0 - ADAM_B2) * jnp.square(g)
    m_hat = m_n / (1.0 - ADAM_B1 ** ADAM_STEP)
    v_hat = v_n / (1.0 - ADAM_B2 ** ADAM_STEP)
    return -ADAM_LR * (m_hat / (jnp.sqrt(v_hat) + ADAM_EPS) + ADAM_WD * w), m_n, v_n


def _sum_adamw(parts, w, m, v, name, nslots=N_DEV):
    r, cols = w.shape
    rs = r // len(parts)
    tr = rs if rs <= 352 else (rs // 2 if (rs // 2) % 16 == 0 else rs // 3)
    steps = rs // tr

    def body(*refs):
        w_ref, m_ref, v_ref, g_ref, d_ref, nm_ref, nv_ref = refs[nslots * len(parts):]
        i = pl.program_id(0)
        for q in range(len(parts)):
            @pl.when(i // steps == q)
            def _(q=q):
                g = refs[nslots * q][...].astype(F32)
                for s in range(1, nslots):
                    g = g + refs[nslots * q + s][...].astype(F32)
                g_ref[...] = g
                d_ref[...], nm_ref[...], nv_ref[...] = _adamw_math(w_ref[...], g, m_ref[...], v_ref[...])

    def slot_spec(q, s):
        return pl.BlockSpec((tr, cols), lambda i: (s * steps + jnp.clip(i - q * steps, 0, steps - 1), 0))

    spec = pl.BlockSpec((tr, cols), lambda i: (i, 0))
    return pl.pallas_call(
        body, name=name, grid=(steps * len(parts),),
        in_specs=[slot_spec(q, s) for q in range(len(parts)) for s in range(nslots)] + [spec] * 3,
        out_specs=[spec] * 4, out_shape=[jax.ShapeDtypeStruct((r, cols), F32)] * 4,
        compiler_params=_cparams("parallel"),
    )(*[a for a in parts for _ in range(nslots)], w, m, v)


def _adamw(w, g, m, v, name):
    r, cols = w.shape
    tr = 256 if r % 256 == 0 else r

    def body(w_ref, g_ref, m_ref, v_ref, d_ref, nm_ref, nv_ref):
        d_ref[...], nm_ref[...], nv_ref[...] = _adamw_math(w_ref[...], g_ref[...], m_ref[...], v_ref[...])

    spec = pl.BlockSpec((tr, cols), lambda i: (i, 0))
    return pl.pallas_call(
        body, name=name, grid=(r // tr,),
        in_specs=[spec] * 4, out_specs=[spec] * 3,
        out_shape=[jax.ShapeDtypeStruct((r, cols), F32)] * 3,
        compiler_params=_cparams("parallel"),
    )(w, g, m, v)


def _rope_tables(p):
    half = ROT_DIM // 2
    lane = jnp.arange(BLK)
    seg = (lane % HEAD_DIM) // half
    inv_freq = ROPE_THETA ** (-(lane % half).astype(F32) * 2.0 / ROT_DIM)
    pos = (jnp.arange(p) - PAD).astype(F32)
    ang = pos[:, None] * inv_freq[None, :]
    cos = jnp.cos(ang)
    sin = jnp.sin(ang)
    c = jnp.where(seg[None, :] < 2, cos, 1.0)
    s1 = jnp.where(seg[None, :] == 0, -sin, 0.0)
    s2 = jnp.where(seg[None, :] == 1, sin, 0.0)
    return jnp.concatenate([c, s1, s2], axis=1).astype(F32)


def _flat_pack(parts, rows):
    flat = jnp.concatenate([a.reshape(-1).astype(F32) for a in parts])
    return jnp.pad(flat, (0, rows * D - flat.shape[0])).reshape(rows, D)


def _flat_unpack(pack, shapes):
    flat = pack.reshape(-1)
    out, off = [], 0
    for s in shapes:
        size = 1
        for e in s:
            size *= e
        out.append(flat[off:off + size].reshape(s))
        off += size
    return out


def kernel(x, meta_tokens, norm_pre_mix, norm_post_mix, w_in, b_in, attn_sinks, w_attn_proj, conv_dw_w, conv_dw_b, conv_ln_g, conv_ln_b, w_conv_proj, b_conv_proj, w_out, norm_pre_ffn, norm_post_ffn, w_up, ffn_dw_w, ffn_dw_b, w_down, loss_target, m_meta_tokens, m_norm_pre_mix, m_norm_post_mix, m_w_in, m_b_in, m_attn_sinks, m_w_attn_proj, m_conv_dw_w, m_conv_dw_b, m_conv_ln_g, m_conv_ln_b, m_w_conv_proj, m_b_conv_proj, m_w_out, m_norm_pre_ffn, m_norm_post_ffn, m_w_up, m_ffn_dw_w, m_ffn_dw_b, m_w_down, v_meta_tokens, v_norm_pre_mix, v_norm_post_mix, v_w_in, v_b_in, v_attn_sinks, v_w_attn_proj, v_conv_dw_w, v_conv_dw_b, v_conv_ln_g, v_conv_ln_b, v_w_conv_proj, v_b_conv_proj, v_w_out, v_norm_pre_ffn, v_norm_post_ffn, v_w_up, v_ffn_dw_w, v_ffn_dw_b, v_w_down):
    seq = x.shape[1]
    p = seq + BLK
    me = 4 * lax.axis_index("x") + 2 * lax.axis_index("y") + lax.axis_index("c")
    in_cols = w_in.shape[2]
    up_cols = w_up.shape[2]

    small = jnp.zeros((56, up_cols), F32)
    small = small.at[0:N_META, 0:BLK].set(meta_tokens)
    small = small.at[16:16 + CONV_K, 0:BLK].set(conv_dw_w[0])
    small = small.at[48:48 + FFN_K, :].set(ffn_dw_w[0])
    w_int, small_all = _exchange(_Both(_GatherRelay(w_in[0].T.astype(BF16)), _Gather([small])), "gather_w_in")
    small_all = small_all.reshape(N_DEV, 56, up_cols)
    meta_full = small_all[:, 0:N_META, 0:BLK].transpose(1, 0, 2).reshape(N_META, D)
    cdw = small_all[:, 16:16 + CONV_K, 0:BLK].transpose(1, 0, 2).reshape(CONV_K, D)
    cdw32 = jnp.pad(cdw, ((0, 32 - CONV_K), (0, 0)))
    fdw = small_all[:, 48:48 + FFN_K, :].transpose(1, 0, 2).reshape(FFN_K, 2 * FFN)

    tabs = _rope_tables(p)
    vecs = jnp.concatenate([conv_ln_g, conv_ln_b, b_conv_proj, norm_post_mix, norm_pre_ffn, jnp.zeros((3, D), F32)], axis=0)

    (h0p, n1, q, kv, ag, gates), (wa, wc, wo) = _in_proj(
        x[0], meta_full, norm_pre_mix, w_int, b_in, tabs,
        comm=_Gather([w_attn_proj[0].astype(BF16), w_conv_proj[0].astype(BF16), w_out[0].astype(BF16)]))
    (ao,), (w_upt,) = _attn_fwd(q, kv, attn_sinks, comm=_Gather([w_up[0].T.astype(BF16)]))
    (c0,), (wd,) = _conv31_fwd(ag, cdw32, conv_dw_b, comm=_Gather([w_down[0].astype(BF16)]))
    c1, attn, conv, merged, mix, h1, n2 = _mixer_fwd(ao, c0, gates, h0p, wa, wc, wo, vecs)
    u0 = _mm_nt(n2, w_upt, "ffn_up")
    act, dact_dv, dact_dg = _ffn_act(u0, fdw, ffn_dw_b)
    dffn, dact, dy, acc_f = _ffn_down_loss(act, wd, h1, loss_target[0], norm_post_ffn)

    (dug, duv, gfw_g, gfw_v, gfb_g, gfb_v, g_wd), _ = _ffn_act_bwd(u0, dact, dact_dg, dact_dv, fdw, act, dffn)
    g_wupt, (s_wd0,) = _mm_tn([dug, duv], n2, "grad_w_up", comm=_Scatter([g_wd], 0, 2))
    (dh1, acc_u), (s_wd1,) = _ffn_in_bwd(dug, duv, w_upt, h1, dy, norm_pre_ffn, comm=_Scatter([g_wd], 1, 2))
    (dmix, dat, dcv, dgt, dao, dc0, acc_m), (s_wup0,) = _mixer_bwd(
        dh1, mix, attn, conv, gates, c0, wa, wc, wo, vecs, comm=_Scatter([g_wupt], 0, 4))
    (da, dg, g_cdw, g_cdb, g_wo, g_wa, g_wc), (s_wup1, s_wup2, s_wup3) = _conv31_bwd(
        ag, dc0, cdw32, [(merged, dmix), (ao, dat), (c1, dcv)],
        comm=_Both(_Both(_Scatter([g_wupt], 1, 4), _Scatter([g_wupt], 2, 4)), _Scatter([g_wupt], 3, 4)))
    (dqkv, dsink), (s_wa, s_wc, s_wo) = _attn_bwd(q, kv, dao, attn_sinks, tabs, comm=_Scatter([g_wa, g_wc, g_wo]))
    loss_row = jnp.sum(acc_f[1:2, :], axis=1, keepdims=True)
    early = [loss_row, acc_m[0:1], dsink[0:1, 0:16], g_cdw[0:CONV_K], g_cdb,
             acc_m[2:3], acc_m[3:4], acc_m[1:2], acc_u[0:1], acc_f[0:1],
             jnp.concatenate([gfw_g, gfw_v], axis=1), jnp.concatenate([gfb_g, gfb_v], axis=1)]
    (g_wint, g_bin), (gathered_early,) = _mm_tn([dqkv, da, dg, dgt], n1, "grad_w_in", col_sums=True,
                                                comm=_Gather([_flat_pack(early, 64)]))
    (from_sibling,) = _exchange(_SiblingSwap(g_wint), "swap_w_in")
    (grad_x2d, dmeta, acc_i), (s_win,) = _in_bwd(dqkv, da, dg, dgt, w_int, h0p, dh1, norm_pre_mix,
                                                 comm=_ChipScatter(_pair_add(g_wint, from_sibling)))

    big = []
    for nm, parts, nslots, w, m, v, tr in (
            ("w_in", [s_win], N_CHIP, w_in, m_w_in, v_w_in, True), ("w_up", [s_wup0, s_wup1, s_wup2, s_wup3], N_DEV, w_up, m_w_up, v_w_up, True),
            ("w_attn_proj", [s_wa], N_DEV, w_attn_proj, m_w_attn_proj, v_w_attn_proj, False),
            ("w_conv_proj", [s_wc], N_DEV, w_conv_proj, m_w_conv_proj, v_w_conv_proj, False),
            ("w_out", [s_wo], N_DEV, w_out, m_w_out, v_w_out, False),
            ("w_down", [s_wd0, s_wd1], N_DEV, w_down, m_w_down, v_w_down, False)):
        ins = [a[0].T if tr else a[0] for a in (w, m, v)]
        big.append(tuple((o.T if tr else o)[None] for o in _sum_adamw(parts, *ins, "update_" + nm, nslots)))

    late = [dmeta, acc_i[0:1], g_bin]
    (gathered_late,) = _exchange(_Gather([_flat_pack(late, 24)]), "gather_small_grads")
    g_meta, g_npm, g_bi = _flat_unpack(_sum_slots(gathered_late, "sum_late_grads"), [a.shape for a in late])
    tot = _flat_unpack(_sum_slots(gathered_early, "sum_small_grads"), [a.shape for a in early])
    (loss, g_nqm, g_sk, g_cw, g_cb, g_lg, g_lb, g_bc, g_npf, g_nqf, g_fw, g_fb) = tot
    loss = loss.reshape(())
    g_meta = lax.dynamic_slice_in_dim(g_meta, me * BLK, BLK, axis=1)
    g_cw = lax.dynamic_slice_in_dim(g_cw, me * BLK, BLK, axis=1)[None]
    g_fw = lax.dynamic_slice_in_dim(g_fw, me * up_cols, up_cols, axis=1)[None]

    sm_w = [meta_tokens, norm_pre_mix, norm_post_mix, b_in, attn_sinks, conv_dw_w, conv_dw_b, conv_ln_g, conv_ln_b,
            b_conv_proj, norm_pre_ffn, norm_post_ffn, ffn_dw_w, ffn_dw_b]
    sm_g = [g_meta, g_npm, g_nqm, g_bi, g_sk, g_cw, g_cb, g_lg, g_lb, g_bc, g_npf, g_nqf, g_fw, g_fb]
    sm_m = [m_meta_tokens, m_norm_pre_mix, m_norm_post_mix, m_b_in, m_attn_sinks, m_conv_dw_w, m_conv_dw_b, m_conv_ln_g,
            m_conv_ln_b, m_b_conv_proj, m_norm_pre_ffn, m_norm_post_ffn, m_ffn_dw_w, m_ffn_dw_b]
    sm_v = [v_meta_tokens, v_norm_pre_mix, v_norm_post_mix, v_b_in, v_attn_sinks, v_conv_dw_w, v_conv_dw_b, v_conv_ln_g,
            v_conv_ln_b, v_b_conv_proj, v_norm_pre_ffn, v_norm_post_ffn, v_ffn_dw_w, v_ffn_dw_b]
    sm_shapes = [a.shape for a in sm_w]
    upd_rows = 32
    v_pack = _flat_pack(sm_v, upd_rows)
    sm_out = _adamw(_flat_pack(sm_w, upd_rows), _flat_pack(sm_g, upd_rows), _flat_pack(sm_m, upd_rows), v_pack, "adamw_small")
    sm_d, sm_nm, sm_nv = (_flat_unpack(o, sm_shapes) for o in sm_out)

    order = ["meta_tokens", "norm_pre_mix", "norm_post_mix", "w_in", "b_in", "attn_sinks", "w_attn_proj", "conv_dw_w",
             "conv_dw_b", "conv_ln_g", "conv_ln_b", "w_conv_proj", "b_conv_proj", "w_out", "norm_pre_ffn", "norm_post_ffn",
             "w_up", "ffn_dw_w", "ffn_dw_b", "w_down"]
    small_names = ["meta_tokens", "norm_pre_mix", "norm_post_mix", "b_in", "attn_sinks", "conv_dw_w", "conv_dw_b", "conv_ln_g",
                   "conv_ln_b", "b_conv_proj", "norm_pre_ffn", "norm_post_ffn", "ffn_dw_w", "ffn_dw_b"]
    big_names = ["w_in", "w_up", "w_attn_proj", "w_conv_proj", "w_out", "w_down"]
    table = {}
    for k, nm in enumerate(small_names):
        table[nm] = (sm_g[k], sm_d[k], sm_nm[k], sm_nv[k])
    for k, nm in enumerate(big_names):
        table[nm] = big[k]
    grad_x = grad_x2d[None]
    outs = [loss, grad_x]
    for field in range(4):
        outs += [table[nm][field] for nm in order]
    return tuple(outs)
```

```python
import functools

import jax
import jax.numpy as jnp
from jax import lax
from jax.experimental import pallas as pl
from jax.experimental.pallas import tpu as pltpu

F32 = jnp.float32
BF16 = jnp.bfloat16
MESH = pl.DeviceIdType.MESH

D = 1024
HEAD_DIM = 64
N_META = 16
BLK = 128
PAD = BLK - N_META
CONV_K = 31
FFN = 2816
FFN_K = 3
QKV_W = 1280
IN_W = 5376
ROT_DIM = 16
ROPE_THETA = 500000.0
RMS_EPS = 1e-6
LN_EPS = 1e-5
NEG_INF = -1e30
SCALE = HEAD_DIM ** -0.5
N_DEV = 8

ADAM_LR = 0.001
ADAM_B1 = 0.9
ADAM_B2 = 0.999
ADAM_EPS = 1e-08
ADAM_WD = 0.01
ADAM_STEP = 10

VMEM_BYTES_V7X = 64 * 1024 * 1024
VMEM_LIMIT = VMEM_BYTES_V7X - 8 * 1024 * 1024

NT = (((1,), (1,)), ((), ()))
TN = (((0,), (0,)), ((), ()))
VM = pl.BlockSpec(memory_space=pltpu.VMEM)
ANY = pl.BlockSpec(memory_space=pl.ANY)


def _cparams(*sem):
    return pltpu.CompilerParams(dimension_semantics=sem or None, vmem_limit_bytes=VMEM_LIMIT)


def _row_tile(p):
    return 384 if p % 384 == 0 else 128


def _dot(a, b):
    return jnp.dot(a, b, preferred_element_type=F32)


def _dot_nt(a, b):
    return lax.dot_general(a, b, NT, preferred_element_type=F32)


def _dot_tn(a, b):
    return lax.dot_general(a, b, TN, preferred_element_type=F32)


def _rms(x, g):
    return x * lax.rsqrt(jnp.mean(x * x, axis=-1, keepdims=True) + RMS_EPS) * g


def _lnsilu(x, g, b):
    mu = jnp.mean(x, axis=-1, keepdims=True)
    var = jnp.mean(jnp.square(x - mu), axis=-1, keepdims=True)
    z = (x - mu) * lax.rsqrt(var + LN_EPS) * g + b
    return z * jax.nn.sigmoid(z)


def _rms_bwd(x, g, dy):
    r = lax.rsqrt(jnp.mean(x * x, axis=-1, keepdims=True) + RMS_EPS)
    xn = x * r
    u = dy * g
    dg = jnp.sum(dy * xn, axis=0, keepdims=True)
    dx = r * (u - xn * jnp.mean(u * xn, axis=-1, keepdims=True))
    return dx, dg


def _lnsilu_bwd(x, g, b, dout):
    mu = jnp.mean(x, axis=-1, keepdims=True)
    xc = x - mu
    rs = lax.rsqrt(jnp.mean(xc * xc, axis=-1, keepdims=True) + LN_EPS)
    yh = xc * rs
    z = yh * g + b
    sg = jax.nn.sigmoid(z)
    dz = dout * (sg * (1.0 + z * (1.0 - sg)))
    dg = jnp.sum(dz * yh, axis=0, keepdims=True)
    db = jnp.sum(dz, axis=0, keepdims=True)
    dyh = dz * g
    dx = rs * (dyh - jnp.mean(dyh, axis=-1, keepdims=True) - yh * jnp.mean(dyh * yh, axis=-1, keepdims=True))
    return dx, dg, db


def _rope(v, c, s1, s2):
    return v * c + pltpu.roll(v, BLK - 8, 1) * s1 + pltpu.roll(v, 8, 1) * s2


def _rows(i, tm):
    return i * tm + lax.broadcasted_iota(jnp.int32, (tm, 1), 0)


def _place():
    return lax.axis_index("x"), lax.axis_index("y"), lax.axis_index("c")


def _blk(ref, idx, r, dtype):
    return ref.at[pl.ds(pl.multiple_of(idx * r, 16 if dtype == BF16 else 8), r), :]


class _Gather:
    def __init__(self, arrs):
        self.ins = list(arrs)
        n = len(arrs)
        self.out_shape = [jax.ShapeDtypeStruct((N_DEV * a.shape[0], a.shape[1]), a.dtype) for a in arrs]
        self.scratch = [pltpu.SemaphoreType.DMA((n, 7)), pltpu.SemaphoreType.DMA((n, 7)), pltpu.SemaphoreType.DMA((n,))]

    def _parts(self, ins, outs, sems):
        send_sems, recv_sems, local_sems = sems
        n = len(ins)
        x, y, c = _place()
        me, sibling = (x, y, c), (x, y, 1 - c)
        chips = [(1 - x, y), (x, 1 - y), (1 - x, 1 - y)]

        def rows(a, p):
            return _blk(outs[a], 4 * p[0] + 2 * p[1] + p[2], self.ins[a].shape[0], self.ins[a].dtype)

        def copy(a, k, block, to, src=None):
            return pltpu.make_async_remote_copy(
                src_ref=rows(a, block) if src is None else src, dst_ref=rows(a, block),
                send_sem=send_sems.at[a, k], recv_sem=recv_sems.at[a, k], device_id=to, device_id_type=MESH)

        mine = [pltpu.make_async_copy(ins[a], rows(a, me), local_sems.at[a]) for a in range(n)]
        first = []
        for a in range(n):
            first.append(copy(a, 0, me, sibling, src=ins[a]))
            first += [copy(a, 1 + j, me, (*chip, c), src=ins[a]) for j, chip in enumerate(chips)]
        return n, c, me, sibling, chips, copy, mine, first

    def start(self, ins, outs, sems):
        *_, mine, first = self._parts(ins, outs, sems)
        for cp in mine + first:
            cp.start()

    def finish(self, ins, outs, sems):
        n, c, me, sibling, chips, copy, mine, first = self._parts(ins, outs, sems)
        passed = []
        for j, chip in enumerate(chips):
            for a in range(n):
                copy(a, 1 + j, (*chip, c), me).wait_recv()
                fwd = copy(a, 4 + j, (*chip, c), sibling)
                fwd.start()
                passed.append(fwd)
        for a in range(n):
            copy(a, 0, sibling, me).wait_recv()
            for j, chip in enumerate(chips):
                copy(a, 4 + j, (*chip, 1 - c), me).wait_recv()
        for cp in first + passed:
            cp.wait_send()
        for cp in mine:
            cp.wait()


class _GatherRelay:
    N_COPY = 13

    def __init__(self, arr):
        self.ins = [arr]
        self.r = arr.shape[0]
        self.out_shape = [jax.ShapeDtypeStruct((N_DEV * self.r, arr.shape[1]), arr.dtype)]
        self.scratch = [pltpu.SemaphoreType.DMA((self.N_COPY,)), pltpu.SemaphoreType.DMA((self.N_COPY,)),
                        pltpu.SemaphoreType.DMA]

    def _parts(self, ins, outs, sems):
        send_sems, recv_sems, local_sem = sems
        x, y, c = _place()
        r, half = self.r, self.r // 2
        out = outs[0]
        me, sib, xn, yn, dg = (x, y, c), (x, y, 1 - c), (1 - x, y, c), (x, 1 - y, c), (1 - x, 1 - y, c)
        sx, sy, sd = (1 - x, y, 1 - c), (x, 1 - y, 1 - c), (1 - x, 1 - y, 1 - c)
        lo, hi = (0, half), (half, half)

        def rows(p, part=(0, r)):
            return out.at[pl.ds(pl.multiple_of((4 * p[0] + 2 * p[1] + p[2]) * r + part[0], 16), part[1]), :]

        def own(part):
            return ins[0].at[pl.ds(part[0], part[1]), :]

        def copy(k, dev_rows, to, src=None):
            return pltpu.make_async_remote_copy(
                src_ref=dev_rows if src is None else src, dst_ref=dev_rows,
                send_sem=send_sems.at[k], recv_sem=recv_sems.at[k], device_id=to, device_id_type=MESH)

        mine = pltpu.make_async_copy(ins[0], rows(me), local_sem)
        first = [copy(0, rows(me), sib, src=ins[0]),
                 copy(1, rows(me, lo), xn, src=own(lo)), copy(3, rows(me, hi), yn, src=own(hi)),
                 copy(2, rows(me, hi), xn, src=own(hi)), copy(4, rows(me, lo), yn, src=own(lo))]
        arrive = {0: rows(sib), 1: rows(xn, lo), 2: rows(xn, hi), 3: rows(yn, hi), 4: rows(yn, lo),
                  5: rows(dg, lo), 6: rows(dg, hi), 7: rows(sx, lo), 8: rows(sx, hi), 9: rows(sy, hi),
                  10: rows(sy, lo), 11: rows(sd, lo), 12: rows(sd, hi)}
        relay = {1: [(5, rows(xn, lo), yn), (7, rows(xn, lo), sib)], 3: [(6, rows(yn, hi), xn), (9, rows(yn, hi), sib)],
                 2: [(8, rows(xn, hi), sib)], 4: [(10, rows(yn, lo), sib)],
                 5: [(11, rows(dg, lo), sib)], 6: [(12, rows(dg, hi), sib)]}
        return copy, mine, first, arrive, relay, me

    def start(self, ins, outs, sems):
        _, mine, first, _, _, _ = self._parts(ins, outs, sems)
        for cp in [mine] + first:
            cp.start()

    def finish(self, ins, outs, sems):
        copy, mine, first, arrive, relay, me = self._parts(ins, outs, sems)
        passed = []
        for k in (1, 3, 2, 4, 5, 6):
            copy(k, arrive[k], me).wait_recv()
            for k2, dev_rows, to in relay[k]:
                fwd = copy(k2, dev_rows, to)
                fwd.start()
                passed.append(fwd)
        for k in (0, 7, 8, 9, 10, 11, 12):
            copy(k, arrive[k], me).wait_recv()
        for cp in first + passed:
            cp.wait_send()
        mine.wait()


FLIPS = [(0, 0, 1), (1, 0, 0), (0, 1, 0), (1, 1, 0), (1, 0, 1), (0, 1, 1), (1, 1, 1)]


class _Scatter:
    def __init__(self, arrs, part=0, nparts=1):
        self.ins = list(arrs)
        self.part, self.nparts = part, nparts
        n = len(arrs)
        self.out_shape = [jax.ShapeDtypeStruct((a.shape[0] // nparts, a.shape[1]), a.dtype) for a in arrs]
        self.scratch = [pltpu.SemaphoreType.DMA((n, 7)), pltpu.SemaphoreType.DMA((n, 7)), pltpu.SemaphoreType.DMA((n,))]

    def _parts(self, ins, outs, sems):
        send_sems, recv_sems, local_sems = sems
        n = len(ins)
        x, y, c = _place()
        me = 4 * x + 2 * y + c

        def flip(v, f):
            return 1 - v if f else v

        def src(a, idx):
            r = self.ins[a].shape[0] // N_DEV
            rs = r // self.nparts
            return ins[a].at[pl.ds(pl.multiple_of(idx * r + self.part * rs, 16), rs), :]

        def dst(a, idx):
            rs = self.ins[a].shape[0] // N_DEV // self.nparts
            return outs[a].at[pl.ds(pl.multiple_of(idx * rs, 16), rs), :]

        mine = [pltpu.make_async_copy(src(a, me), dst(a, me), local_sems.at[a]) for a in range(n)]
        sends, recvs = [], []
        for k, f in enumerate(FLIPS):
            peer = (flip(x, f[0]), flip(y, f[1]), flip(c, f[2]))
            pidx = 4 * peer[0] + 2 * peer[1] + peer[2]
            for a in range(n):
                sends.append(pltpu.make_async_remote_copy(
                    src_ref=src(a, pidx), dst_ref=dst(a, me),
                    send_sem=send_sems.at[a, k], recv_sem=recv_sems.at[a, k], device_id=peer, device_id_type=MESH))
                recvs.append(functools.partial(
                    pltpu.make_async_remote_copy,
                    src_ref=src(a, pidx), dst_ref=dst(a, pidx),
                    send_sem=send_sems.at[a, k], recv_sem=recv_sems.at[a, k], device_id=peer, device_id_type=MESH))
        return mine, sends, recvs

    def start(self, ins, outs, sems):
        mine, sends, _ = self._parts(ins, outs, sems)
        for cp in mine + sends:
            cp.start()

    def finish(self, ins, outs, sems):
        mine, sends, recvs = self._parts(ins, outs, sems)
        for make in recvs:
            make().wait_recv()
        for cp in sends:
            cp.wait_send()
        for cp in mine:
            cp.wait()


N_CHIP = 4


class _SiblingSwap:
    def __init__(self, arr):
        self.ins = [arr]
        self.r = arr.shape[0] // N_DEV
        self.out_shape = [jax.ShapeDtypeStruct((N_CHIP * self.r, arr.shape[1]), arr.dtype)]
        self.scratch = [pltpu.SemaphoreType.DMA((N_CHIP,)), pltpu.SemaphoreType.DMA((N_CHIP,))]

    def _copies(self, ins, outs, sems):
        send_sems, recv_sems = sems
        x, y, c = _place()
        r = self.r
        return [pltpu.make_async_remote_copy(
            src_ref=ins[0].at[pl.ds(pl.multiple_of((2 * j + 1 - c) * r, 16), r), :],
            dst_ref=outs[0].at[pl.ds(j * r, r), :],
            send_sem=send_sems.at[j], recv_sem=recv_sems.at[j], device_id=(x, y, 1 - c), device_id_type=MESH)
            for j in range(N_CHIP)]

    def start(self, ins, outs, sems):
        for cp in self._copies(ins, outs, sems):
            cp.start()

    def finish(self, ins, outs, sems):
        for cp in self._copies(ins, outs, sems):
            cp.wait()


class _ChipScatter:
    def __init__(self, arr):
        self.ins = [arr]
        self.r = arr.shape[0] // N_CHIP
        self.out_shape = [jax.ShapeDtypeStruct(arr.shape, arr.dtype)]
        self.scratch = [pltpu.SemaphoreType.DMA((3,)), pltpu.SemaphoreType.DMA((3,)), pltpu.SemaphoreType.DMA]

    def _parts(self, ins, outs, sems):
        send_sems, recv_sems, local_sem = sems
        x, y, c = _place()
        r = self.r
        my_chip = 2 * x + y

        def rows(ref, j):
            return ref.at[pl.ds(pl.multiple_of(j * r, 16), r), :]

        mine = pltpu.make_async_copy(rows(ins[0], my_chip), rows(outs[0], my_chip), local_sem)
        sends, recvs = [], []
        for k, (fx, fy) in enumerate(((1, 0), (0, 1), (1, 1))):
            px, py = (1 - x if fx else x), (1 - y if fy else y)
            peer_chip = 2 * px + py
            sends.append(pltpu.make_async_remote_copy(
                src_ref=rows(ins[0], peer_chip), dst_ref=rows(outs[0], my_chip),
                send_sem=send_sems.at[k], recv_sem=recv_sems.at[k], device_id=(px, py, c), device_id_type=MESH))
            recvs.append(functools.partial(
                pltpu.make_async_remote_copy,
                src_ref=rows(ins[0], peer_chip), dst_ref=rows(outs[0], peer_chip),
                send_sem=send_sems.at[k], recv_sem=recv_sems.at[k], device_id=(px, py, c), device_id_type=MESH))
        return mine, sends, recvs

    def start(self, ins, outs, sems):
        mine, sends, _ = self._parts(ins, outs, sems)
        for cp in [mine] + sends:
            cp.start()

    def finish(self, ins, outs, sems):
        mine, sends, recvs = self._parts(ins, outs, sems)
        for make in recvs:
            make().wait_recv()
        for cp in sends:
            cp.wait_send()
        mine.wait()


def _pair_add(partial, recv):
    r = recv.shape[0] // N_CHIP
    cols = recv.shape[1]
    tr = r // 2 if (r // 2) % 16 == 0 else r
    steps = r // tr
    core = lax.axis_index("c").astype(jnp.int32).reshape(1)

    def body(c_ref, p_ref, s_ref, o_ref):
        o_ref[...] = (p_ref[...].astype(F32) + s_ref[...].astype(F32)).astype(BF16)

    spec = pl.BlockSpec((tr, cols), lambda j, i, c_ref: (j * steps + i, 0))
    return pl.pallas_call(
        body, name="pair_add",
        grid_spec=pltpu.PrefetchScalarGridSpec(
            num_scalar_prefetch=1, grid=(N_CHIP, steps),
            in_specs=[pl.BlockSpec((tr, cols), lambda j, i, c_ref: ((2 * j + c_ref[0]) * steps + i, 0)), spec],
            out_specs=spec),
        out_shape=jax.ShapeDtypeStruct(recv.shape, BF16),
        compiler_params=_cparams("parallel", "parallel"),
    )(core, partial, recv)


class _Both:
    def __init__(self, a, b):
        self.a, self.b = a, b
        self.ins = a.ins + b.ins
        self.out_shape = a.out_shape + b.out_shape
        self.scratch = a.scratch + b.scratch

    def _split(self, ins, outs, sems):
        ni, no, ns = len(self.a.ins), len(self.a.out_shape), len(self.a.scratch)
        return (ins[:ni], outs[:no], sems[:ns]), (ins[ni:], outs[no:], sems[ns:])

    def start(self, ins, outs, sems):
        ra, rb = self._split(ins, outs, sems)
        self.a.start(*ra)
        self.b.start(*rb)

    def finish(self, ins, outs, sems):
        ra, rb = self._split(ins, outs, sems)
        self.a.finish(*ra)
        self.b.finish(*rb)


def _exchange(comm, name):
    n, m = len(comm.ins), len(comm.out_shape)

    def body(*refs):
        ins, outs, sems = refs[:n], refs[n:n + m], refs[n + m:]
        comm.start(ins, outs, sems)
        comm.finish(ins, outs, sems)

    return pl.pallas_call(
        body, name=name, out_shape=comm.out_shape, in_specs=[ANY] * n, out_specs=[ANY] * m, scratch_shapes=comm.scratch,
    )(*comm.ins)


def _call(body, *, name, grid, in_specs, out_specs, out_shape, args, scratch=(), sem="parallel", comm=None):
    if comm is None:
        outs = pl.pallas_call(
            body, name=name, grid=grid, in_specs=list(in_specs), out_specs=list(out_specs), out_shape=list(out_shape),
            scratch_shapes=list(scratch), compiler_params=_cparams(sem))(*args)
        return outs, []
    n_in, n_out, n_sc = len(in_specs), len(out_specs), len(scratch)
    n_ci, n_co = len(comm.ins), len(comm.out_shape)
    last = grid[0] - 1

    def fused(*refs):
        ins, refs = refs[:n_in], refs[n_in:]
        c_ins, refs = refs[:n_ci], refs[n_ci:]
        outs, refs = refs[:n_out], refs[n_out:]
        c_outs, refs = refs[:n_co], refs[n_co:]
        sc, c_sems = refs[:n_sc], refs[n_sc:]
        step = pl.program_id(0)

        @pl.when(step == 0)
        def _():
            comm.start(c_ins, c_outs, c_sems)

        body(*ins, *outs, *sc)

        @pl.when(step == last)
        def _():
            comm.finish(c_ins, c_outs, c_sems)

    outs = pl.pallas_call(
        fused, name=name, grid=grid, in_specs=list(in_specs) + [ANY] * n_ci, out_specs=list(out_specs) + [ANY] * n_co,
        out_shape=list(out_shape) + comm.out_shape, scratch_shapes=list(scratch) + comm.scratch,
        compiler_params=_cparams("arbitrary"))(*args, *comm.ins)
    return outs[:n_out], outs[n_out:]


def _token_specs(tm):
    k = tm // BLK
    return [pl.BlockSpec((BLK, D), functools.partial(lambda i, t: (jnp.maximum(k * i + t - 1, 0), 0), t=t)) for t in range(k)]


def _in_proj(x2d, meta, gain, w_int, b_in, tabs, comm=None):
    p = x2d.shape[0] + BLK
    tm = _row_tile(p)
    k = tm // BLK

    def body(*refs):
        x_refs = refs[:k]
        m_ref, g_ref, w_ref, b_ref, t_ref, h_ref, n1_ref, q_ref, kv_ref, ag_ref, gt_ref = refs[k:]
        i = pl.program_id(0)
        head = jnp.concatenate([jnp.zeros((PAD, D), F32), m_ref[...]], axis=0)
        first = jnp.where(i == 0, head, x_refs[0][...])
        h = jnp.concatenate([first] + [r[...] for r in x_refs[1:]], axis=0) if k > 1 else first
        h_ref[...] = h
        n = _rms(h, g_ref[...]).astype(BF16)
        n1_ref[...] = n
        c, s1, s2 = t_ref[:, 0:128], t_ref[:, 128:256], t_ref[:, 256:384]

        def mm(c0, w):
            return _dot_nt(n, w_ref[c0:c0 + w, :]) + b_ref[:, c0:c0 + w]

        for j in range(4):
            acc = mm(256 * j, 256)
            for t in range(2):
                lo = 256 * j + 128 * t
                q_ref[:, lo:lo + 128] = (_rope(acc[:, 128 * t:128 * (t + 1)], c, s1, s2) * SCALE).astype(BF16)
        acc = mm(1024, 256)
        kv_ref[:, 0:128] = _rope(acc[:, 0:128], c, s1, s2).astype(BF16)
        kv_ref[:, 128:256] = acc[:, 128:256].astype(BF16)
        for j in range(8):
            ag_ref[:, 256 * j:256 * (j + 1)] = mm(QKV_W + 256 * j, 256).astype(BF16)
        for j in range(8):
            gt_ref[:, 256 * j:256 * (j + 1)] = mm(QKV_W + 2048 + 256 * j, 256).astype(BF16)

    def row(w):
        return pl.BlockSpec((tm, w), lambda i: (i, 0))

    return _call(
        body, name="in_proj", grid=(p // tm,),
        in_specs=_token_specs(tm) + [VM, VM, VM, VM, row(384)],
        out_specs=[row(D), row(D), row(D), row(256), row(2048), row(2048)],
        out_shape=[jax.ShapeDtypeStruct((p, D), F32)] + [jax.ShapeDtypeStruct((p, w), BF16) for w in (D, D, 256, 2048, 2048)],
        args=(x2d,) * k + (meta, gain, w_int, b_in, tabs), comm=comm)


N_KEY = 2 * BLK + N_META


def _attn_setup(n, h, q_ref, km_ref, kp_ref, kc_ref):
    lo = lax.broadcasted_iota(jnp.int32, (BLK, BLK), 1) < HEAD_DIM
    lok = lax.broadcasted_iota(jnp.int32, (N_KEY, BLK), 1) < HEAD_DIM

    def dup(lanes):
        cat = jnp.concatenate([kp_ref[:, lanes], kc_ref[:, lanes], km_ref[PAD:BLK, lanes]], axis=0).astype(F32)
        rolled = pltpu.roll(cat, HEAD_DIM, 1)
        return (jnp.where(lok, cat, rolled) if h == 0 else jnp.where(lok, rolled, cat)).astype(BF16)

    k2 = dup(slice(0, 128))
    v2 = dup(slice(128, 256))
    qs = _stack_heads(q_ref, h, lo)

    kr = lax.broadcasted_iota(jnp.int32, (BLK, BLK), 0)
    tq = BLK * n + lax.broadcasted_iota(jnp.int32, (BLK, BLK), 1) - PAD
    t_p = BLK * (n - 1) + kr - PAD
    t_c = BLK * n + kr - PAD
    ok_p = jnp.logical_and(t_p >= N_META, tq - t_p < BLK)
    ok_c = jnp.logical_and(t_c >= N_META, t_c <= tq)
    ok_m = lax.broadcasted_iota(jnp.int32, (N_META, BLK), 0) <= BLK * n + lax.broadcasted_iota(jnp.int32, (N_META, BLK), 1) - PAD
    bias = jnp.concatenate([jnp.where(ok, 0.0, NEG_INF).astype(F32) for ok in (ok_p, ok_c, ok_m)], axis=0)
    return qs, k2, v2, bias, lok


def _attn_head(s, bias, sink):
    s = s + bias
    m = jnp.maximum(jnp.max(s, axis=0, keepdims=True), sink)
    e = jnp.exp(s - m)
    es = jnp.exp(sink - m)
    inv = 1.0 / (jnp.sum(e, axis=0, keepdims=True) + es)
    return e * inv, es * inv


def _stack_heads(ref, h, lo):
    pieces = []
    for jp in range(4):
        v = ref[:, BLK * (4 * h + jp):BLK * (4 * h + jp + 1)]
        zero = jnp.zeros_like(v)
        pieces += [jnp.where(lo, v, zero), jnp.where(lo, zero, v)]
    return jnp.concatenate(pieces, axis=0)


def _unstack_heads(v, jp, lo):
    return jnp.where(lo, v[256 * jp:256 * jp + 128], v[256 * jp + 128:256 * jp + 256])


def _attn_fwd(q, kv, sinks, comm=None):
    p = q.shape[0]
    nb = p // BLK

    def body(q_ref, km_ref, kp_ref, kc_ref, sink_ref, o_ref):
        n = pl.program_id(0)
        lo = lax.broadcasted_iota(jnp.int32, (BLK, BLK), 1) < HEAD_DIM
        for h in range(2):
            qs, k2, v2, bias, _ = _attn_setup(n, h, q_ref, km_ref, kp_ref, kc_ref)
            st = _dot_nt(k2, qs)
            pt = jnp.concatenate(
                [_attn_head(st[:, BLK * g:BLK * (g + 1)], bias, sink_ref[0, 8 * h + g])[0].astype(BF16) for g in range(8)],
                axis=1)
            o = _dot_tn(pt, v2)
            for jp in range(4):
                o_ref[:, BLK * (4 * h + jp):BLK * (4 * h + jp + 1)] = _unstack_heads(o, jp, lo).astype(BF16)

    return _call(
        body, name="attn_fwd", grid=(nb,),
        in_specs=[pl.BlockSpec((BLK, D), lambda i: (i, 0)),
                  pl.BlockSpec((BLK, 256), lambda i: (0, 0)),
                  pl.BlockSpec((BLK, 256), lambda i: (jnp.maximum(i - 1, 0), 0)),
                  pl.BlockSpec((BLK, 256), lambda i: (i, 0)),
                  pl.BlockSpec(memory_space=pltpu.SMEM)],
        out_specs=[pl.BlockSpec((BLK, D), lambda i: (i, 0))],
        out_shape=[jax.ShapeDtypeStruct((p, D), BF16)],
        args=(q, kv, kv, kv, sinks), comm=comm)


def _conv31_fwd(ag, w32, b, comm=None):
    p = ag.shape[0]
    nch = p // BLK

    def body(a_ref, g_ref, w_ref, b_ref, o_ref, gp):
        gp[0:32, :] = jnp.zeros((32, BLK), F32)
        for ci in range(nch):
            r0 = BLK * ci
            glu = a_ref[r0:r0 + BLK, :].astype(F32) * jax.nn.sigmoid(g_ref[r0:r0 + BLK, :].astype(F32))
            if ci == 0:
                glu = jnp.where(_rows(0, BLK) >= PAD, glu, 0.0)
            gp[32 + r0:32 + r0 + BLK, :] = glu
        for ci in range(nch):
            r0 = BLK * ci
            acc = jnp.broadcast_to(b_ref[...], (BLK, BLK))
            for j in range(CONV_K):
                acc = acc + w_ref[j:j + 1, :] * gp[r0 + j + 2:r0 + j + 2 + BLK, :]
            o_ref[r0:r0 + BLK, :] = acc

    return _call(
        body, name="conv31_fwd", grid=(D // BLK,),
        in_specs=[pl.BlockSpec((p, BLK), lambda j: (0, j)), pl.BlockSpec((p, BLK), lambda j: (0, 8 + j)),
                  pl.BlockSpec((32, BLK), lambda j: (0, j)), pl.BlockSpec((1, BLK), lambda j: (0, j))],
        out_specs=[pl.BlockSpec((p, BLK), lambda j: (0, j))],
        out_shape=[jax.ShapeDtypeStruct((p, D), F32)],
        scratch=[pltpu.VMEM((p + 32, BLK), F32)],
        args=(ag, ag, w32, b), comm=comm)


def _mixer_fwd(ao, c0, gates, h0p, wa, wc, wo, vecs):
    p = ao.shape[0]
    tm = _row_tile(p)

    def body(ao_ref, c0_ref, gt_ref, h_ref, wa_ref, wc_ref, wo_ref, v_ref,
             c1_ref, at_ref, cv_ref, mg_ref, mix_ref, h1_ref, n2_ref):
        i = pl.program_id(0)
        c1 = _lnsilu(c0_ref[...], v_ref[0:1, :], v_ref[1:2, :]).astype(BF16)
        c1_ref[...] = c1
        attn = _dot(ao_ref[...], wa_ref[...])
        conv = _dot(c1, wc_ref[...]) + v_ref[2:3, :]
        at_ref[...] = attn.astype(BF16)
        cv_ref[...] = conv.astype(BF16)
        merged = (jax.nn.sigmoid(gt_ref[:, 0:D].astype(F32)) * attn
                  + jax.nn.sigmoid(gt_ref[:, D:2 * D].astype(F32)) * conv).astype(BF16)
        mg_ref[...] = merged
        mix = _dot(merged, wo_ref[...])
        mix_ref[...] = mix
        h1 = jnp.where(_rows(i, tm) >= PAD, h_ref[...] + _rms(mix, v_ref[3:4, :]), 0.0)
        h1_ref[...] = h1
        n2_ref[...] = _rms(h1, v_ref[4:5, :]).astype(BF16)

    def row(w):
        return pl.BlockSpec((tm, w), lambda i: (i, 0))

    return pl.pallas_call(
        body, name="mixer_fwd", grid=(p // tm,),
        in_specs=[row(D), row(D), row(2 * D), row(D), VM, VM, VM, VM],
        out_specs=[row(D)] * 7,
        out_shape=[jax.ShapeDtypeStruct((p, D), t) for t in (BF16, BF16, BF16, BF16, F32, F32, BF16)],
        compiler_params=_cparams("parallel"),
    )(ao, c0, gates, h0p, wa, wc, wo, vecs)


def _mm_nt(a, w_t, name):
    p, k = a.shape
    n = w_t.shape[0]
    tm = _row_tile(p)
    ch = 512
    nchunk = n // ch

    def body(a_ref, w_hbm, o_ref, w_ref, sems):
        i = pl.program_id(0)

        def chunk_copy(c):
            return pltpu.make_async_copy(w_hbm.at[pl.ds(c * ch, ch), :], w_ref.at[pl.ds(c * ch, ch), :], sems.at[c])

        @pl.when(i == 0)
        def _():
            for c in range(nchunk):
                chunk_copy(c).start()

        a_v = a_ref[...]
        for c in range(nchunk):
            @pl.when(i == 0)
            def _(c=c):
                chunk_copy(c).wait()

            o_ref[:, c * ch:(c + 1) * ch] = _dot_nt(a_v, w_ref[c * ch:(c + 1) * ch, :]).astype(BF16)

    return pl.pallas_call(
        body, name=name, grid=(p // tm,),
        in_specs=[pl.BlockSpec((tm, k), lambda i: (i, 0)), ANY],
        out_specs=pl.BlockSpec((tm, n), lambda i: (i, 0)),
        out_shape=jax.ShapeDtypeStruct((p, n), BF16),
        scratch_shapes=[pltpu.VMEM((n, k), BF16), pltpu.SemaphoreType.DMA((nchunk,))],
        compiler_params=_cparams("arbitrary"),
    )(a, w_t)


def _conv3(xp_ref, w_ref, r0):
    return (w_ref[0:1, :] * xp_ref[r0 + 6:r0 + 6 + BLK, :] + w_ref[1:2, :] * xp_ref[r0 + 7:r0 + 7 + BLK, :]
            + w_ref[2:3, :] * xp_ref[r0 + 8:r0 + 8 + BLK, :])


def _ffn_slab_specs(p):
    ncol = FFN // BLK
    return [pl.BlockSpec((p, BLK), lambda j: (0, j)), pl.BlockSpec((p, BLK), lambda j: (0, ncol + j)),
            pl.BlockSpec((FFN_K, BLK), lambda j: (0, j)), pl.BlockSpec((FFN_K, BLK), lambda j: (0, ncol + j)),
            pl.BlockSpec((1, BLK), lambda j: (0, j)), pl.BlockSpec((1, BLK), lambda j: (0, ncol + j))]


def _fill_shifted(dst, src_ref, nch):
    dst[0:8, :] = jnp.zeros((8, BLK), F32)
    for ci in range(nch):
        dst[8 + BLK * ci:8 + BLK * (ci + 1), :] = src_ref[BLK * ci:BLK * (ci + 1), :].astype(F32)


def _ffn_act(u0, fw, fb):
    p = u0.shape[0]
    nch = p // BLK

    def body(g_ref, v_ref, wg_ref, wv_ref, bg_ref, bv_ref, o_ref, dv_ref, dg_ref, xg, xv):
        _fill_shifted(xg, g_ref, nch)
        _fill_shifted(xv, v_ref, nch)
        for ci in range(nch):
            r0 = BLK * ci
            ug = _conv3(xg, wg_ref, r0) + bg_ref[...]
            uv = _conv3(xv, wv_ref, r0) + bv_ref[...]
            sg = jax.nn.sigmoid(ug)
            silu = ug * sg
            o_ref[r0:r0 + BLK, :] = (silu * uv).astype(BF16)
            dv_ref[r0:r0 + BLK, :] = silu.astype(BF16)
            dg_ref[r0:r0 + BLK, :] = (uv * (sg * (1.0 + ug * (1.0 - sg)))).astype(BF16)

    slab = pl.BlockSpec((p, BLK), lambda j: (0, j))
    return pl.pallas_call(
        body, name="ffn_act", grid=(FFN // BLK,),
        in_specs=_ffn_slab_specs(p),
        out_specs=[slab] * 3,
        out_shape=[jax.ShapeDtypeStruct((p, FFN), BF16)] * 3,
        scratch_shapes=[pltpu.VMEM((p + 8, BLK), F32)] * 2,
        compiler_params=_cparams("parallel"),
    )(u0, u0, fw, fw, fb, fb)


def _ffn_down_loss(act, wd, h1, tgt, gain):
    p = act.shape[0]
    tm = _row_tile(p)
    k = tm // BLK

    def body(*refs):
        a_ref, w_ref, h_ref = refs[:3]
        t_refs = refs[3:3 + k]
        g_ref, df_ref, da_ref, dy_ref, acc_ref = refs[3 + k:]
        i = pl.program_id(0)

        @pl.when(i == 0)
        def _():
            acc_ref[...] = jnp.zeros_like(acc_ref)

        ffn = _dot(a_ref[...], w_ref[...])
        t = jnp.concatenate([t_ref[...] for t_ref in t_refs], axis=0) if k > 1 else t_refs[0][...]
        diff = jnp.where(_rows(i, tm) >= BLK, h_ref[...] + _rms(ffn, g_ref[...]) - t, 0.0)
        dy = diff * (1.0 / D)
        dffn, dg = _rms_bwd(ffn, g_ref[...], dy)
        acc_ref[0:1, :] += dg
        acc_ref[1:2, :] += jnp.sum(diff * diff, axis=0, keepdims=True) * (0.5 / D)
        dy_ref[...] = dy
        dfb = dffn.astype(BF16)
        df_ref[...] = dfb
        for c0 in range(0, FFN, 256):
            da_ref[:, c0:c0 + 256] = _dot_nt(dfb, w_ref[c0:c0 + 256, :]).astype(BF16)

    def row(w):
        return pl.BlockSpec((tm, w), lambda i: (i, 0))

    return pl.pallas_call(
        body, name="ffn_down_loss", grid=(p // tm,),
        in_specs=[row(FFN), VM, row(D)] + _token_specs(tm) + [VM],
        out_specs=[row(D), row(FFN), row(D), pl.BlockSpec((8, D), lambda i: (0, 0))],
        out_shape=[jax.ShapeDtypeStruct((p, D), BF16), jax.ShapeDtypeStruct((p, FFN), BF16),
                   jax.ShapeDtypeStruct((p, D), F32), jax.ShapeDtypeStruct((8, D), F32)],
        compiler_params=_cparams("arbitrary"),
    )(act, wd, h1, *([tgt] * k), gain)


def _mm_tn(pieces, b, name, col_sums=False, comm=None):
    p, n = b.shape
    tk = 256
    nblk = [a.shape[1] // tk for a in pieces]
    offs = [sum(nblk[:q]) for q in range(len(pieces))]
    total = sum(nblk)
    npc = len(pieces)

    def body(*refs):
        a_refs, b_ref, o_ref = refs[:npc], refs[npc], refs[npc + 1]
        i = pl.program_id(0)
        for q, a_ref in enumerate(a_refs):
            @pl.when(jnp.logical_and(i >= offs[q], i < offs[q] + nblk[q]))
            def _(a_ref=a_ref):
                a_v = a_ref[...]
                o_ref[...] = _dot_tn(a_v, b_ref[...]).astype(BF16)
                if col_sums:
                    refs[npc + 2][...] = jnp.sum(a_v.astype(F32), axis=0, keepdims=True)

    def a_spec(q):
        return pl.BlockSpec((p, tk), lambda i: (0, jnp.clip(i - offs[q], 0, nblk[q] - 1)))

    out_specs = [pl.BlockSpec((tk, n), lambda i: (i, 0))]
    out_shape = [jax.ShapeDtypeStruct((total * tk, n), BF16)]
    if col_sums:
        out_specs.append(pl.BlockSpec((1, tk), lambda i: (0, i)))
        out_shape.append(jax.ShapeDtypeStruct((1, total * tk), F32))
    res, sent = _call(
        body, name=name, grid=(total,),
        in_specs=[a_spec(q) for q in range(npc)] + [VM],
        out_specs=out_specs, out_shape=out_shape, args=(*pieces, b), comm=comm)
    res = res if col_sums else res[0]
    return res if comm is None else (res, sent)


def _ffn_act_bwd(u0, dact, dact_dg, dact_dv, fw, act, dffn, comm=None):
    p = u0.shape[0]
    nch = p // BLK
    ncol = FFN // BLK

    def body(g_ref, v_ref, wg_ref, wv_ref, da_ref, lg_ref, lv_ref, act_ref, df_ref,
             dg_ref, dv_ref, gwg_ref, gwv_ref, gbg_ref, gbv_ref, gwd_ref, eg, ev):
        gwd_ref[...] = _dot_tn(act_ref[...], df_ref[...]).astype(BF16)
        eg[p:p + 8, :] = jnp.zeros((8, BLK), F32)
        ev[p:p + 8, :] = jnp.zeros((8, BLK), F32)
        for ci in range(nch):
            r0 = BLK * ci
            d = da_ref[r0:r0 + BLK, :].astype(F32)
            eg[r0:r0 + BLK, :] = d * lg_ref[r0:r0 + BLK, :].astype(F32)
            ev[r0:r0 + BLK, :] = d * lv_ref[r0:r0 + BLK, :].astype(F32)
        def fold(v):
            return jnp.sum(v.reshape(BLK // 8, 8, BLK), axis=0)

        for e_s, x_ref, w_ref, d_ref, gw_ref, gb_ref in ((eg, g_ref, wg_ref, dg_ref, gwg_ref, gbg_ref),
                                                        (ev, v_ref, wv_ref, dv_ref, gwv_ref, gbv_ref)):
            sums = [jnp.zeros((8, BLK), F32) for _ in range(FFN_K + 1)]
            for ci in range(nch):
                r0 = BLK * ci
                es = [e_s[r0 + t:r0 + t + BLK, :] for t in range(FFN_K)]
                du = w_ref[2:3, :] * es[0] + w_ref[1:2, :] * es[1] + w_ref[0:1, :] * es[2]
                if ci == 0:
                    du = jnp.where(_rows(0, BLK) >= PAD, du, 0.0)
                d_ref[r0:r0 + BLK, :] = du.astype(BF16)
                x = x_ref[r0:r0 + BLK, :].astype(F32)
                for j in range(FFN_K):
                    sums[j] = sums[j] + fold(es[FFN_K - 1 - j] * x)
                sums[FFN_K] = sums[FFN_K] + fold(es[0])
            for j in range(FFN_K):
                gw_ref[j:j + 1, :] = jnp.sum(sums[j], axis=0, keepdims=True)
            gb_ref[...] = jnp.sum(sums[FFN_K], axis=0, keepdims=True)

    slab = pl.BlockSpec((p, BLK), lambda j: (0, j))
    wspec = pl.BlockSpec((FFN_K, BLK), lambda j: (0, j))
    bspec = pl.BlockSpec((1, BLK), lambda j: (0, j))
    return _call(
        body, name="ffn_act_bwd", grid=(ncol,),
        in_specs=_ffn_slab_specs(p)[:4] + [slab] * 4 + [VM],
        out_specs=[slab, slab, wspec, wspec, bspec, bspec, pl.BlockSpec((BLK, D), lambda j: (j, 0))],
        out_shape=[jax.ShapeDtypeStruct((p, FFN), BF16)] * 2 + [jax.ShapeDtypeStruct((FFN_K, FFN), F32)] * 2
        + [jax.ShapeDtypeStruct((1, FFN), F32)] * 2 + [jax.ShapeDtypeStruct((FFN, D), BF16)],
        scratch=[pltpu.VMEM((p + 8, BLK), F32)] * 2,
        args=(u0, u0, fw, fw, dact, dact_dg, dact_dv, act, dffn), comm=comm)


def _ffn_in_bwd(dug, duv, w_upt, h1, dy, gain, comm=None):
    p = h1.shape[0]
    tm = _row_tile(p)

    def body(dg_ref, dv_ref, w_ref, h_ref, dy_ref, g_ref, o_ref, acc_ref):
        i = pl.program_id(0)

        @pl.when(i == 0)
        def _():
            acc_ref[...] = jnp.zeros_like(acc_ref)

        dn = _dot(dg_ref[...], w_ref[0:FFN, :]) + _dot(dv_ref[...], w_ref[FFN:2 * FFN, :])
        dh, dg = _rms_bwd(h_ref[...], g_ref[...], dn)
        o_ref[...] = dy_ref[...] + dh
        acc_ref[0:1, :] += dg

    def row(w):
        return pl.BlockSpec((tm, w), lambda i: (i, 0))

    return _call(
        body, name="ffn_in_bwd", grid=(p // tm,),
        in_specs=[row(FFN), row(FFN), VM, row(D), row(D), VM],
        out_specs=[row(D), pl.BlockSpec((8, D), lambda i: (0, 0))],
        out_shape=[jax.ShapeDtypeStruct((p, D), F32), jax.ShapeDtypeStruct((8, D), F32)],
        sem="arbitrary", args=(dug, duv, w_upt, h1, dy, gain), comm=comm)


def _mixer_bwd(dh1, mix, attn, conv, gates, c0, wa, wc, wo, vecs, comm=None):
    p = dh1.shape[0]
    tm = _row_tile(p)

    def body(dh_ref, mix_ref, at_ref, cv_ref, gt_ref, c0_ref, wa_ref, wc_ref, wo_ref, v_ref,
             dmix_ref, dat_ref, dcv_ref, dgt_ref, dao_ref, dc0_ref, acc_ref):
        i = pl.program_id(0)

        @pl.when(i == 0)
        def _():
            acc_ref[...] = jnp.zeros_like(acc_ref)

        dmix, dgp = _rms_bwd(mix_ref[...], v_ref[3:4, :], dh_ref[...])
        dmix = dmix.astype(BF16)
        dmix_ref[...] = dmix
        dmg = _dot_nt(dmix, wo_ref[...])
        sa = jax.nn.sigmoid(gt_ref[:, 0:D].astype(F32))
        sc = jax.nn.sigmoid(gt_ref[:, D:2 * D].astype(F32))
        dat = dmg * sa
        dcv = dmg * sc
        dgt_ref[:, 0:D] = (dmg * at_ref[...].astype(F32) * sa * (1.0 - sa)).astype(BF16)
        dgt_ref[:, D:2 * D] = (dmg * cv_ref[...].astype(F32) * sc * (1.0 - sc)).astype(BF16)
        datb = dat.astype(BF16)
        dcvb = dcv.astype(BF16)
        dat_ref[...] = datb
        dcv_ref[...] = dcvb
        dao_ref[...] = _dot_nt(datb, wa_ref[...]).astype(BF16)
        dc1 = _dot_nt(dcvb, wc_ref[...])
        dc0, dlg, dlb = _lnsilu_bwd(c0_ref[...], v_ref[0:1, :], v_ref[1:2, :], dc1)
        dc0_ref[...] = dc0
        acc_ref[0:1, :] += dgp
        acc_ref[1:2, :] += jnp.sum(dcv, axis=0, keepdims=True)
        acc_ref[2:3, :] += dlg
        acc_ref[3:4, :] += dlb

    def row(w):
        return pl.BlockSpec((tm, w), lambda i: (i, 0))

    return _call(
        body, name="mixer_bwd", grid=(p // tm,),
        in_specs=[row(D), row(D), row(D), row(D), row(2 * D), row(D), VM, VM, VM, VM],
        out_specs=[row(D), row(D), row(D), row(2 * D), row(D), row(D), pl.BlockSpec((8, D), lambda i: (0, 0))],
        out_shape=[jax.ShapeDtypeStruct((p, D), BF16)] * 3 + [jax.ShapeDtypeStruct((p, 2 * D), BF16),
                                                             jax.ShapeDtypeStruct((p, D), BF16),
                                                             jax.ShapeDtypeStruct((p, D), F32),
                                                             jax.ShapeDtypeStruct((8, D), F32)],
        sem="arbitrary", args=(dh1, mix, attn, conv, gates, c0, wa, wc, wo, vecs), comm=comm)


def _conv31_bwd(ag, dc0, w32, tn_pairs, comm=None):
    p = ag.shape[0]
    nch = p // BLK
    npair = len(tn_pairs)

    def body(*refs):
        a_ref, g_ref, dc_ref, w_ref = refs[:4]
        tn_a, tn_b = refs[4:4 + npair], refs[4 + npair:4 + 2 * npair]
        da_ref, dg_ref, gw_ref, gb_ref = refs[4 + 2 * npair:8 + 2 * npair]
        tn_o = refs[8 + 2 * npair:8 + 3 * npair]
        gp, dp = refs[8 + 3 * npair:]
        for ta, tb, to in zip(tn_a, tn_b, tn_o):
            to[...] = _dot_tn(ta[...], tb[...]).astype(BF16)
        gp[0:32, :] = jnp.zeros((32, BLK), F32)
        dp[p:p + 32, :] = jnp.zeros((32, BLK), F32)
        bsum = jnp.zeros((BLK, BLK), F32)
        for ci in range(nch):
            r0 = BLK * ci
            glu = a_ref[r0:r0 + BLK, :].astype(F32) * jax.nn.sigmoid(g_ref[r0:r0 + BLK, :].astype(F32))
            if ci == 0:
                glu = jnp.where(_rows(0, BLK) >= PAD, glu, 0.0)
            gp[32 + r0:32 + r0 + BLK, :] = glu
            d = dc_ref[r0:r0 + BLK, :]
            dp[r0:r0 + BLK, :] = d
            bsum = bsum + d
        gb_ref[...] = jnp.sum(bsum, axis=0, keepdims=True)
        for ci in range(nch):
            r0 = BLK * ci
            acc = jnp.zeros((BLK, BLK), F32)
            for j in range(CONV_K):
                acc = acc + w_ref[j:j + 1, :] * dp[r0 + 30 - j:r0 + 30 - j + BLK, :]
            if ci == 0:
                acc = jnp.where(_rows(0, BLK) >= PAD, acc, 0.0)
            a = a_ref[r0:r0 + BLK, :].astype(F32)
            sg = jax.nn.sigmoid(g_ref[r0:r0 + BLK, :].astype(F32))
            da_ref[r0:r0 + BLK, :] = (acc * sg).astype(BF16)
            dg_ref[r0:r0 + BLK, :] = (acc * a * sg * (1.0 - sg)).astype(BF16)
        sub = BLK // 2
        accs = [jnp.zeros((8, BLK), F32) for _ in range(CONV_K)]
        for r0 in range(0, p, sub):
            d = dp[r0:r0 + sub, :]
            for j in range(CONV_K):
                prod = d * gp[r0 + j + 2:r0 + j + 2 + sub, :]
                accs[j] = accs[j] + jnp.sum(prod.reshape(sub // 8, 8, BLK), axis=0)
        for j in range(CONV_K):
            gw_ref[j:j + 1, :] = jnp.sum(accs[j], axis=0, keepdims=True)
        gw_ref[CONV_K:32, :] = jnp.zeros((32 - CONV_K, BLK), F32)

    slab = pl.BlockSpec((p, BLK), lambda j: (0, j))
    return _call(
        body, name="conv31_bwd", grid=(D // BLK,),
        in_specs=[slab, pl.BlockSpec((p, BLK), lambda j: (0, 8 + j)), slab, pl.BlockSpec((32, BLK), lambda j: (0, j))]
        + [slab] * npair + [VM] * npair,
        out_specs=[slab, slab, pl.BlockSpec((32, BLK), lambda j: (0, j)), pl.BlockSpec((1, BLK), lambda j: (0, j))]
        + [pl.BlockSpec((BLK, D), lambda j: (j, 0))] * npair,
        out_shape=[jax.ShapeDtypeStruct((p, D), BF16)] * 2 + [jax.ShapeDtypeStruct((32, D), F32),
                                                             jax.ShapeDtypeStruct((1, D), F32)]
        + [jax.ShapeDtypeStruct((D, D), BF16)] * npair,
        scratch=[pltpu.VMEM((p + 32, BLK), F32)] * 2,
        args=(ag, ag, dc0, w32, *[a for a, _ in tn_pairs], *[b for _, b in tn_pairs]), comm=comm)


def _attn_bwd(q, kv, dao, sinks, tabs, comm=None):
    p = q.shape[0]
    nb = p // BLK

    def body(q_ref, km_ref, kp_ref, kc_ref, do_ref, sink_ref, t_ref, dqkv_ref, dsink_ref, carry, macc):
        i = pl.program_id(0)
        n = nb - 1 - i

        @pl.when(i == 0)
        def _():
            carry[...] = jnp.zeros_like(carry)
            macc[...] = jnp.zeros_like(macc)
            dsink_ref[...] = jnp.zeros_like(dsink_ref)

        lo = lax.broadcasted_iota(jnp.int32, (BLK, BLK), 1) < HEAD_DIM
        lane8 = lax.broadcasted_iota(jnp.int32, (8, BLK), 1)
        c, s1, s2 = t_ref[:, 0:128], -t_ref[:, 128:256], -t_ref[:, 256:384]
        dk = jnp.zeros((N_KEY, BLK), F32)
        dv = jnp.zeros((N_KEY, BLK), F32)
        for h in range(2):
            qs, k2, v2, bias, lok = _attn_setup(n, h, q_ref, km_ref, kp_ref, kc_ref)
            dos = _stack_heads(do_ref, h, lo)
            st = _dot_nt(k2, qs)
            dpt = _dot_nt(v2, dos)
            p_parts, ds_parts = [], []
            for g in range(8):
                cols = slice(BLK * g, BLK * (g + 1))
                pn, ps = _attn_head(st[:, cols], bias, sink_ref[0, 8 * h + g])
                dp = dpt[:, cols]
                delta = jnp.sum(pn * dp, axis=0, keepdims=True)
                ds_parts.append((pn * (dp - delta)).astype(BF16))
                p_parts.append(pn.astype(BF16))
                dsk = -jnp.sum(ps * delta, axis=1, keepdims=True)
                dsink_ref[...] += jnp.where(lane8 == 8 * h + g, dsk, 0.0)
            dst = jnp.concatenate(ds_parts, axis=1)
            pt = jnp.concatenate(p_parts, axis=1)
            dq = _dot_tn(dst, k2)
            for jp in range(4):
                lo_c = BLK * (4 * h + jp)
                dqkv_ref[:, lo_c:lo_c + BLK] = (_rope(_unstack_heads(dq, jp, lo), c, s1, s2) * SCALE).astype(BF16)
            dk2 = _dot(dst, qs)
            dv2 = _dot(pt, dos)
            dk2 = dk2 + pltpu.roll(dk2, HEAD_DIM, 1)
            dv2 = dv2 + pltpu.roll(dv2, HEAD_DIM, 1)
            own = lok if h == 0 else jnp.logical_not(lok)
            dk = jnp.where(own, dk2, dk)
            dv = jnp.where(own, dv2, dv)
        macc[:, 0:BLK] += dk[2 * BLK:N_KEY]
        macc[:, BLK:2 * BLK] += dv[2 * BLK:N_KEY]
        last = (n == 0).astype(F32)
        zpad = jnp.zeros((PAD, BLK), F32)
        dk_c = dk[BLK:2 * BLK] + carry[:, 0:BLK] + last * jnp.concatenate([zpad, macc[:, 0:BLK]], axis=0)
        dv_c = dv[BLK:2 * BLK] + carry[:, BLK:2 * BLK] + last * jnp.concatenate([zpad, macc[:, BLK:2 * BLK]], axis=0)
        carry[:, 0:BLK] = dk[0:BLK]
        carry[:, BLK:2 * BLK] = dv[0:BLK]
        dqkv_ref[:, D:D + BLK] = _rope(dk_c, c, s1, s2).astype(BF16)
        dqkv_ref[:, D + BLK:D + 2 * BLK] = dv_c.astype(BF16)

    def rev(w):
        return pl.BlockSpec((BLK, w), lambda i: (nb - 1 - i, 0))

    return _call(
        body, name="attn_bwd", grid=(nb,),
        in_specs=[rev(D),
                  pl.BlockSpec((BLK, 256), lambda i: (0, 0)),
                  pl.BlockSpec((BLK, 256), lambda i: (jnp.maximum(nb - 2 - i, 0), 0)),
                  rev(256), rev(D),
                  pl.BlockSpec(memory_space=pltpu.SMEM), rev(384)],
        out_specs=[rev(QKV_W), pl.BlockSpec((8, BLK), lambda i: (0, 0))],
        out_shape=[jax.ShapeDtypeStruct((p, QKV_W), BF16), jax.ShapeDtypeStruct((8, BLK), F32)],
        scratch=[pltpu.VMEM((BLK, 256), F32), pltpu.VMEM((N_META, 256), F32)], sem="arbitrary",
        args=(q, kv, kv, kv, dao, sinks, tabs), comm=comm)


def _in_bwd(dqkv, da, dg, dgt, w_int, h0p, dh1, gain, comm=None):
    p = h0p.shape[0]
    tm = _row_tile(p)
    nt = p // tm
    first_rows = tm - BLK

    def body(dq_ref, da_ref, dg_ref, dt_ref, w_ref, h_ref, dh_ref, g_ref, gx_ref, dm_ref, acc_ref, buf, sems):
        i = pl.program_id(0)
        slot = i % 2

        @pl.when(i == 0)
        def _():
            acc_ref[...] = jnp.zeros_like(acc_ref)

        dn = (_dot(dq_ref[...], w_ref[0:QKV_W, :]) + _dot(da_ref[...], w_ref[QKV_W:QKV_W + D, :])
              + _dot(dg_ref[...], w_ref[QKV_W + D:QKV_W + 2 * D, :]) + _dot(dt_ref[...], w_ref[QKV_W + 2 * D:IN_W, :]))
        dh, dgain = _rms_bwd(h_ref[...], g_ref[...], dn)
        dh0 = dh_ref[...] + dh
        acc_ref[0:1, :] += dgain
        buf[slot] = dh0

        @pl.when(i == 0)
        def _():
            dm_ref[...] = dh0[PAD:BLK]

        def first_copy():
            return pltpu.make_async_copy(buf.at[0, pl.ds(BLK, first_rows), :], gx_ref.at[pl.ds(0, first_rows), :], sems.at[0])

        def tile_copy(j, s):
            return pltpu.make_async_copy(buf.at[s], gx_ref.at[pl.ds(pl.multiple_of(j * tm - BLK, BLK), tm), :], sems.at[s])

        if first_rows:
            @pl.when(i == 1)
            def _():
                first_copy().wait()

        @pl.when(i >= 2)
        def _():
            tile_copy(i - 1, 1 - slot).wait()

        if first_rows:
            @pl.when(i == 0)
            def _():
                first_copy().start()

        @pl.when(i > 0)
        def _():
            tile_copy(i, slot).start()

        @pl.when(i == nt - 1)
        def _():
            tile_copy(i, slot).wait()

    def row(w):
        return pl.BlockSpec((tm, w), lambda i: (i, 0))

    return _call(
        body, name="in_bwd", grid=(nt,),
        in_specs=[row(QKV_W), row(D), row(D), row(2 * D), VM, row(D), row(D), VM],
        out_specs=[ANY, pl.BlockSpec((N_META, D), lambda i: (0, 0)), pl.BlockSpec((8, D), lambda i: (0, 0))],
        out_shape=[jax.ShapeDtypeStruct((p - BLK, D), F32), jax.ShapeDtypeStruct((N_META, D), F32),
                   jax.ShapeDtypeStruct((8, D), F32)],
        scratch=[pltpu.VMEM((2, tm, D), F32), pltpu.SemaphoreType.DMA((2,))],
        sem="arbitrary", args=(dqkv, da, dg, dgt, w_int, h0p, dh1, gain), comm=comm)


def _sum_slots(slots, name):
    r = slots.shape[0] // N_DEV
    cols = slots.shape[1]
    tr = r if r <= 352 else (r // 2 if (r // 2) % 16 == 0 else r // 3)
    steps = r // tr

    def body(*refs):
        acc = refs[0][...].astype(F32)
        for s in range(1, N_DEV):
            acc = acc + refs[s][...].astype(F32)
        refs[N_DEV][...] = acc

    return pl.pallas_call(
        body, name=name, grid=(steps,),
        in_specs=[pl.BlockSpec((tr, cols), functools.partial(lambda i, s: (s * steps + i, 0), s=s)) for s in range(N_DEV)],
        out_specs=pl.BlockSpec((tr, cols), lambda i: (i, 0)),
        out_shape=jax.ShapeDtypeStruct((r, cols), F32),
        compiler_params=_cparams("parallel"),
    )(*([slots] * N_DEV))


def _adamw_math(w, g, m, v):
    m_n = ADAM_B1 * m + (1.0 - ADAM_B1) * g
    v_n = ADAM_B2 * v + (1.0 - ADAM_B2) * jnp.square(g)
    m_hat = m_n / (1.0 - ADAM_B1 ** ADAM_STEP)
    v_hat = v_n / (1.0 - ADAM_B2 ** ADAM_STEP)
    return -ADAM_LR * (m_hat / (jnp.sqrt(v_hat) + ADAM_EPS) + ADAM_WD * w), m_n, v_n


def _sum_adamw(parts, w, m, v, name, nslots=N_DEV):
    r, cols = w.shape
    rs = r // len(parts)
    tr = rs if rs <= 352 else (rs // 2 if (rs // 2) % 16 == 0 else rs // 3)
    steps = rs // tr

    def body(*refs):
        w_ref, m_ref, v_ref, g_ref, d_ref, nm_ref, nv_ref = refs[nslots * len(parts):]
        i = pl.program_id(0)
        for q in range(len(parts)):
            @pl.when(i // steps == q)
            def _(q=q):
                g = refs[nslots * q][...].astype(F32)
                for s in range(1, nslots):
                    g = g + refs[nslots * q + s][...].astype(F32)
                g_ref[...] = g
                d_ref[...], nm_ref[...], nv_ref[...] = _adamw_math(w_ref[...], g, m_ref[...], v_ref[...])

    def slot_spec(q, s):
        return pl.BlockSpec((tr, cols), lambda i: (s * steps + jnp.clip(i - q * steps, 0, steps - 1), 0))

    spec = pl.BlockSpec((tr, cols), lambda i: (i, 0))
    return pl.pallas_call(
        body, name=name, grid=(steps * len(parts),),
        in_specs=[slot_spec(q, s) for q in range(len(parts)) for s in range(nslots)] + [spec] * 3,
        out_specs=[spec] * 4, out_shape=[jax.ShapeDtypeStruct((r, cols), F32)] * 4,
        compiler_params=_cparams("parallel"),
    )(*[a for a in parts for _ in range(nslots)], w, m, v)


def _adamw(w, g, m, v, name):
    r, cols = w.shape
    tr = 256 if r % 256 == 0 else r

    def body(w_ref, g_ref, m_ref, v_ref, d_ref, nm_ref, nv_ref):
        d_ref[...], nm_ref[...], nv_ref[...] = _adamw_math(w_ref[...], g_ref[...], m_ref[...], v_ref[...])

    spec = pl.BlockSpec((tr, cols), lambda i: (i, 0))
    return pl.pallas_call(
        body, name=name, grid=(r // tr,),
        in_specs=[spec] * 4, out_specs=[spec] * 3,
        out_shape=[jax.ShapeDtypeStruct((r, cols), F32)] * 3,
        compiler_params=_cparams("parallel"),
    )(w, g, m, v)


def _rope_tables(p):
    half = ROT_DIM // 2
    lane = jnp.arange(BLK)
    seg = (lane % HEAD_DIM) // half
    inv_freq = ROPE_THETA ** (-(lane % half).astype(F32) * 2.0 / ROT_DIM)
    pos = (jnp.arange(p) - PAD).astype(F32)
    ang = pos[:, None] * inv_freq[None, :]
    cos = jnp.cos(ang)
    sin = jnp.sin(ang)
    c = jnp.where(seg[None, :] < 2, cos, 1.0)
    s1 = jnp.where(seg[None, :] == 0, -sin, 0.0)
    s2 = jnp.where(seg[None, :] == 1, sin, 0.0)
    return jnp.concatenate([c, s1, s2], axis=1).astype(F32)


def _flat_pack(parts, rows):
    flat = jnp.concatenate([a.reshape(-1).astype(F32) for a in parts])
    return jnp.pad(flat, (0, rows * D - flat.shape[0])).reshape(rows, D)


def _flat_unpack(pack, shapes):
    flat = pack.reshape(-1)
    out, off = [], 0
    for s in shapes:
        size = 1
        for e in s:
            size *= e
        out.append(flat[off:off + size].reshape(s))
        off += size
    return out


def kernel(x, meta_tokens, norm_pre_mix, norm_post_mix, w_in, b_in, attn_sinks, w_attn_proj, conv_dw_w, conv_dw_b, conv_ln_g, conv_ln_b, w_conv_proj, b_conv_proj, w_out, norm_pre_ffn, norm_post_ffn, w_up, ffn_dw_w, ffn_dw_b, w_down, loss_target, m_meta_tokens, m_norm_pre_mix, m_norm_post_mix, m_w_in, m_b_in, m_attn_sinks, m_w_attn_proj, m_conv_dw_w, m_conv_dw_b, m_conv_ln_g, m_conv_ln_b, m_w_conv_proj, m_b_conv_proj, m_w_out, m_norm_pre_ffn, m_norm_post_ffn, m_w_up, m_ffn_dw_w, m_ffn_dw_b, m_w_down, v_meta_tokens, v_norm_pre_mix, v_norm_post_mix, v_w_in, v_b_in, v_attn_sinks, v_w_attn_proj, v_conv_dw_w, v_conv_dw_b, v_conv_ln_g, v_conv_ln_b, v_w_conv_proj, v_b_conv_proj, v_w_out, v_norm_pre_ffn, v_norm_post_ffn, v_w_up, v_ffn_dw_w, v_ffn_dw_b, v_w_down):
    seq = x.shape[1]
    p = seq + BLK
    me = 4 * lax.axis_index("x") + 2 * lax.axis_index("y") + lax.axis_index("c")
    in_cols = w_in.shape[2]
    up_cols = w_up.shape[2]

    small = jnp.zeros((56, up_cols), F32)
    small = small.at[0:N_META, 0:BLK].set(meta_tokens)
    small = small.at[16:16 + CONV_K, 0:BLK].set(conv_dw_w[0])
    small = small.at[48:48 + FFN_K, :].set(ffn_dw_w[0])
    w_int, small_all = _exchange(_Both(_GatherRelay(w_in[0].T.astype(BF16)), _Gather([small])), "gather_w_in")
    small_all = small_all.reshape(N_DEV, 56, up_cols)
    meta_full = small_all[:, 0:N_META, 0:BLK].transpose(1, 0, 2).reshape(N_META, D)
    cdw = small_all[:, 16:16 + CONV_K, 0:BLK].transpose(1, 0, 2).reshape(CONV_K, D)
    cdw32 = jnp.pad(cdw, ((0, 32 - CONV_K), (0, 0)))
    fdw = small_all[:, 48:48 + FFN_K, :].transpose(1, 0, 2).reshape(FFN_K, 2 * FFN)

    tabs = _rope_tables(p)
    vecs = jnp.concatenate([conv_ln_g, conv_ln_b, b_conv_proj, norm_post_mix, norm_pre_ffn, jnp.zeros((3, D), F32)], axis=0)

    (h0p, n1, q, kv, ag, gates), (wa, wc, wo) = _in_proj(
        x[0], meta_full, norm_pre_mix, w_int, b_in, tabs,
        comm=_Gather([w_attn_proj[0].astype(BF16), w_conv_proj[0].astype(BF16), w_out[0].astype(BF16)]))
    (ao,), (w_upt,) = _attn_fwd(q, kv, attn_sinks, comm=_Gather([w_up[0].T.astype(BF16)]))
    (c0,), (wd,) = _conv31_fwd(ag, cdw32, conv_dw_b, comm=_Gather([w_down[0].astype(BF16)]))
    c1, attn, conv, merged, mix, h1, n2 = _mixer_fwd(ao, c0, gates, h0p, wa, wc, wo, vecs)
    u0 = _mm_nt(n2, w_upt, "ffn_up")
    act, dact_dv, dact_dg = _ffn_act(u0, fdw, ffn_dw_b)
    dffn, dact, dy, acc_f = _ffn_down_loss(act, wd, h1, loss_target[0], norm_post_ffn)

    (dug, duv, gfw_g, gfw_v, gfb_g, gfb_v, g_wd), _ = _ffn_act_bwd(u0, dact, dact_dg, dact_dv, fdw, act, dffn)
    g_wupt, (s_wd0,) = _mm_tn([dug, duv], n2, "grad_w_up", comm=_Scatter([g_wd], 0, 2))
    (dh1, acc_u), (s_wd1,) = _ffn_in_bwd(dug, duv, w_upt, h1, dy, norm_pre_ffn, comm=_Scatter([g_wd], 1, 2))
    (dmix, dat, dcv, dgt, dao, dc0, acc_m), (s_wup0,) = _mixer_bwd(
        dh1, mix, attn, conv, gates, c0, wa, wc, wo, vecs, comm=_Scatter([g_wupt], 0, 4))
    (da, dg, g_cdw, g_cdb, g_wo, g_wa, g_wc), (s_wup1, s_wup2, s_wup3) = _conv31_bwd(
        ag, dc0, cdw32, [(merged, dmix), (ao, dat), (c1, dcv)],
        comm=_Both(_Both(_Scatter([g_wupt], 1, 4), _Scatter([g_wupt], 2, 4)), _Scatter([g_wupt], 3, 4)))
    (dqkv, dsink), (s_wa, s_wc, s_wo) = _attn_bwd(q, kv, dao, attn_sinks, tabs, comm=_Scatter([g_wa, g_wc, g_wo]))
    loss_row = jnp.sum(acc_f[1:2, :], axis=1, keepdims=True)
    early = [loss_row, acc_m[0:1], dsink[0:1, 0:16], g_cdw[0:CONV_K], g_cdb,
             acc_m[2:3], acc_m[3:4], acc_m[1:2], acc_u[0:1], acc_f[0:1],
             jnp.concatenate([gfw_g, gfw_v], axis=1), jnp.concatenate([gfb_g, gfb_v], axis=1)]
    (g_wint, g_bin), (gathered_early,) = _mm_tn([dqkv, da, dg, dgt], n1, "grad_w_in", col_sums=True,
                                                comm=_Gather([_flat_pack(early, 64)]))
    (from_sibling,) = _exchange(_SiblingSwap(g_wint), "swap_w_in")
    (grad_x2d, dmeta, acc_i), (s_win,) = _in_bwd(dqkv, da, dg, dgt, w_int, h0p, dh1, norm_pre_mix,
                                                 comm=_ChipScatter(_pair_add(g_wint, from_sibling)))

    big = []
    for nm, parts, nslots, w, m, v, tr in (
            ("w_in", [s_win], N_CHIP, w_in, m_w_in, v_w_in, True), ("w_up", [s_wup0, s_wup1, s_wup2, s_wup3], N_DEV, w_up, m_w_up, v_w_up, True),
            ("w_attn_proj", [s_wa], N_DEV, w_attn_proj, m_w_attn_proj, v_w_attn_proj, False),
            ("w_conv_proj", [s_wc], N_DEV, w_conv_proj, m_w_conv_proj, v_w_conv_proj, False),
            ("w_out", [s_wo], N_DEV, w_out, m_w_out, v_w_out, False),
            ("w_down", [s_wd0, s_wd1], N_DEV, w_down, m_w_down, v_w_down, False)):
        ins = [a[0].T if tr else a[0] for a in (w, m, v)]
        big.append(tuple((o.T if tr else o)[None] for o in _sum_adamw(parts, *ins, "update_" + nm, nslots)))

    late = [dmeta, acc_i[0:1], g_bin]
    (gathered_late,) = _exchange(_Gather([_flat_pack(late, 24)]), "gather_small_grads")
    g_meta, g_npm, g_bi = _flat_unpack(_sum_slots(gathered_late, "sum_late_grads"), [a.shape for a in late])
    tot = _flat_unpack(_sum_slots(gathered_early, "sum_small_grads"), [a.shape for a in early])
    (loss, g_nqm, g_sk, g_cw, g_cb, g_lg, g_lb, g_bc, g_npf, g_nqf, g_fw, g_fb) = tot
    loss = loss.reshape(())
    g_meta = lax.dynamic_slice_in_dim(g_meta, me * BLK, BLK, axis=1)
    g_cw = lax.dynamic_slice_in_dim(g_cw, me * BLK, BLK, axis=1)[None]
    g_fw = lax.dynamic_slice_in_dim(g_fw, me * up_cols, up_cols, axis=1)[None]

    sm_w = [meta_tokens, norm_pre_mix, norm_post_mix, b_in, attn_sinks, conv_dw_w, conv_dw_b, conv_ln_g, conv_ln_b,
            b_conv_proj, norm_pre_ffn, norm_post_ffn, ffn_dw_w, ffn_dw_b]
    sm_g = [g_meta, g_npm, g_nqm, g_bi, g_sk, g_cw, g_cb, g_lg, g_lb, g_bc, g_npf, g_nqf, g_fw, g_fb]
    sm_m = [m_meta_tokens, m_norm_pre_mix, m_norm_post_mix, m_b_in, m_attn_sinks, m_conv_dw_w, m_conv_dw_b, m_conv_ln_g,
            m_conv_ln_b, m_b_conv_proj, m_norm_pre_ffn, m_norm_post_ffn, m_ffn_dw_w, m_ffn_dw_b]
    sm_v = [v_meta_tokens, v_norm_pre_mix, v_norm_post_mix, v_b_in, v_attn_sinks, v_conv_dw_w, v_conv_dw_b, v_conv_ln_g,
            v_conv_ln_b, v_b_conv_proj, v_norm_pre_ffn, v_norm_post_ffn, v_ffn_dw_w, v_ffn_dw_b]
    sm_shapes = [a.shape for a in sm_w]
    upd_rows = 32
    v_pack = _flat_pack(sm_v, upd_rows)
    sm_out = _adamw(_flat_pack(sm_w, upd_rows), _flat_pack(sm_g, upd_rows), _flat_pack(sm_m, upd_rows), v_pack, "adamw_small")
    sm_d, sm_nm, sm_nv = (_flat_unpack(o, sm_shapes) for o in sm_out)

    order = ["meta_tokens", "norm_pre_mix", "norm_post_mix", "w_in", "b_in", "attn_sinks", "w_attn_proj", "conv_dw_w",
             "conv_dw_b", "conv_ln_g", "conv_ln_b", "w_conv_proj", "b_conv_proj", "w_out", "norm_pre_ffn", "norm_post_ffn",
             "w_up", "ffn_dw_w", "ffn_dw_b", "w_down"]
    small_names = ["meta_tokens", "norm_pre_mix", "norm_post_mix", "b_in", "attn_sinks", "conv_dw_w", "conv_dw_b", "conv_ln_g",
                   "conv_ln_b", "b_conv_proj", "norm_pre_ffn", "norm_post_ffn", "ffn_dw_w", "ffn_dw_b"]
    big_names = ["w_in", "w_up", "w_attn_proj", "w_conv_proj", "w_out", "w_down"]
    table = {}
    for k, nm in enumerate(small_names):
        table[nm] = (sm_g[k], sm_d[k], sm_nm[k], sm_nv[k])
    for k, nm in enumerate(big_names):
        table[nm] = big[k]
    grad_x = grad_x2d[None]
    outs = [loss, grad_x]
    for field in range(4):
        outs += [table[nm][field] for nm in order]
    return tuple(outs)
```

```python
import functools

import jax
import jax.numpy as jnp
from jax import lax
from jax.experimental import pallas as pl
from jax.experimental.pallas import tpu as pltpu

F32 = jnp.float32
BF16 = jnp.bfloat16
MESH = pl.DeviceIdType.MESH

D = 1024
HEAD_DIM = 64
N_META = 16
BLK = 128
PAD = BLK - N_META
CONV_K = 31
FFN = 2816
FFN_K = 3
QKV_W = 1280
IN_W = 5376
ROT_DIM = 16
ROPE_THETA = 500000.0
RMS_EPS = 1e-6
LN_EPS = 1e-5
NEG_INF = -1e30
SCALE = HEAD_DIM ** -0.5
N_DEV = 8

ADAM_LR = 0.001
ADAM_B1 = 0.9
ADAM_B2 = 0.999
ADAM_EPS = 1e-08
ADAM_WD = 0.01
ADAM_STEP = 10

VMEM_BYTES_V7X = 64 * 1024 * 1024
VMEM_LIMIT = VMEM_BYTES_V7X - 8 * 1024 * 1024

NT = (((1,), (1,)), ((), ()))
TN = (((0,), (0,)), ((), ()))
VM = pl.BlockSpec(memory_space=pltpu.VMEM)
ANY = pl.BlockSpec(memory_space=pl.ANY)


def _cparams(*sem):
    return pltpu.CompilerParams(dimension_semantics=sem or None, vmem_limit_bytes=VMEM_LIMIT)


def _row_tile(p):
    return 384 if p % 384 == 0 else 128


def _dot(a, b):
    return jnp.dot(a, b, preferred_element_type=F32)


def _dot_nt(a, b):
    return lax.dot_general(a, b, NT, preferred_element_type=F32)


def _dot_tn(a, b):
    return lax.dot_general(a, b, TN, preferred_element_type=F32)


def _rms(x, g):
    return x * lax.rsqrt(jnp.mean(x * x, axis=-1, keepdims=True) + RMS_EPS) * g


def _lnsilu(x, g, b):
    mu = jnp.mean(x, axis=-1, keepdims=True)
    var = jnp.mean(jnp.square(x - mu), axis=-1, keepdims=True)
    z = (x - mu) * lax.rsqrt(var + LN_EPS) * g + b
    return z * jax.nn.sigmoid(z)


def _rms_bwd(x, g, dy):
    r = lax.rsqrt(jnp.mean(x * x, axis=-1, keepdims=True) + RMS_EPS)
    xn = x * r
    u = dy * g
    dg = jnp.sum(dy * xn, axis=0, keepdims=True)
    dx = r * (u - xn * jnp.mean(u * xn, axis=-1, keepdims=True))
    return dx, dg


def _lnsilu_bwd(x, g, b, dout):
    mu = jnp.mean(x, axis=-1, keepdims=True)
    xc = x - mu
    rs = lax.rsqrt(jnp.mean(xc * xc, axis=-1, keepdims=True) + LN_EPS)
    yh = xc * rs
    z = yh * g + b
    sg = jax.nn.sigmoid(z)
    dz = dout * (sg * (1.0 + z * (1.0 - sg)))
    dg = jnp.sum(dz * yh, axis=0, keepdims=True)
    db = jnp.sum(dz, axis=0, keepdims=True)
    dyh = dz * g
    dx = rs * (dyh - jnp.mean(dyh, axis=-1, keepdims=True) - yh * jnp.mean(dyh * yh, axis=-1, keepdims=True))
    return dx, dg, db


def _rope(v, c, s1, s2):
    return v * c + pltpu.roll(v, BLK - 8, 1) * s1 + pltpu.roll(v, 8, 1) * s2


def _rows(i, tm):
    return i * tm + lax.broadcasted_iota(jnp.int32, (tm, 1), 0)


def _place():
    return lax.axis_index("x"), lax.axis_index("y"), lax.axis_index("c")


def _blk(ref, idx, r, dtype):
    return ref.at[pl.ds(pl.multiple_of(idx * r, 16 if dtype == BF16 else 8), r), :]


class _Gather:
    def __init__(self, arrs):
        self.ins = list(arrs)
        n = len(arrs)
        self.out_shape = [jax.ShapeDtypeStruct((N_DEV * a.shape[0], a.shape[1]), a.dtype) for a in arrs]
        self.scratch = [pltpu.SemaphoreType.DMA((n, 7)), pltpu.SemaphoreType.DMA((n, 7)), pltpu.SemaphoreType.DMA((n,))]

    def _parts(self, ins, outs, sems):
        send_sems, recv_sems, local_sems = sems
        n = len(ins)
        x, y, c = _place()
        me, sibling = (x, y, c), (x, y, 1 - c)
        chips = [(1 - x, y), (x, 1 - y), (1 - x, 1 - y)]

        def rows(a, p):
            return _blk(outs[a], 4 * p[0] + 2 * p[1] + p[2], self.ins[a].shape[0], self.ins[a].dtype)

        def copy(a, k, block, to, src=None):
            return pltpu.make_async_remote_copy(
                src_ref=rows(a, block) if src is None else src, dst_ref=rows(a, block),
                send_sem=send_sems.at[a, k], recv_sem=recv_sems.at[a, k], device_id=to, device_id_type=MESH)

        mine = [pltpu.make_async_copy(ins[a], rows(a, me), local_sems.at[a]) for a in range(n)]
        first = []
        for a in range(n):
            first.append(copy(a, 0, me, sibling, src=ins[a]))
            first += [copy(a, 1 + j, me, (*chip, c), src=ins[a]) for j, chip in enumerate(chips)]
        return n, c, me, sibling, chips, copy, mine, first

    def start(self, ins, outs, sems):
        *_, mine, first = self._parts(ins, outs, sems)
        for cp in mine + first:
            cp.start()

    def finish(self, ins, outs, sems):
        n, c, me, sibling, chips, copy, mine, first = self._parts(ins, outs, sems)
        passed = []
        for j, chip in enumerate(chips):
            for a in range(n):
                copy(a, 1 + j, (*chip, c), me).wait_recv()
                fwd = copy(a, 4 + j, (*chip, c), sibling)
                fwd.start()
                passed.append(fwd)
        for a in range(n):
            copy(a, 0, sibling, me).wait_recv()
            for j, chip in enumerate(chips):
                copy(a, 4 + j, (*chip, 1 - c), me).wait_recv()
        for cp in first + passed:
            cp.wait_send()
        for cp in mine:
            cp.wait()


class _GatherRelay:
    N_COPY = 13

    def __init__(self, arr):
        self.ins = [arr]
        self.r = arr.shape[0]
        self.out_shape = [jax.ShapeDtypeStruct((N_DEV * self.r, arr.shape[1]), arr.dtype)]
        self.scratch = [pltpu.SemaphoreType.DMA((self.N_COPY,)), pltpu.SemaphoreType.DMA((self.N_COPY,)),
                        pltpu.SemaphoreType.DMA]

    def _parts(self, ins, outs, sems):
        send_sems, recv_sems, local_sem = sems
        x, y, c = _place()
        r, half = self.r, self.r // 2
        out = outs[0]
        me, sib, xn, yn, dg = (x, y, c), (x, y, 1 - c), (1 - x, y, c), (x, 1 - y, c), (1 - x, 1 - y, c)
        sx, sy, sd = (1 - x, y, 1 - c), (x, 1 - y, 1 - c), (1 - x, 1 - y, 1 - c)
        lo, hi = (0, half), (half, half)

        def rows(p, part=(0, r)):
            return out.at[pl.ds(pl.multiple_of((4 * p[0] + 2 * p[1] + p[2]) * r + part[0], 16), part[1]), :]

        def own(part):
            return ins[0].at[pl.ds(part[0], part[1]), :]

        def copy(k, dev_rows, to, src=None):
            return pltpu.make_async_remote_copy(
                src_ref=dev_rows if src is None else src, dst_ref=dev_rows,
                send_sem=send_sems.at[k], recv_sem=recv_sems.at[k], device_id=to, device_id_type=MESH)

        mine = pltpu.make_async_copy(ins[0], rows(me), local_sem)
        first = [copy(0, rows(me), sib, src=ins[0]),
                 copy(1, rows(me, lo), xn, src=own(lo)), copy(3, rows(me, hi), yn, src=own(hi)),
                 copy(2, rows(me, hi), xn, src=own(hi)), copy(4, rows(me, lo), yn, src=own(lo))]
        arrive = {0: rows(sib), 1: rows(xn, lo), 2: rows(xn, hi), 3: rows(yn, hi), 4: rows(yn, lo),
                  5: rows(dg, lo), 6: rows(dg, hi), 7: rows(sx, lo), 8: rows(sx, hi), 9: rows(sy, hi),
                  10: rows(sy, lo), 11: rows(sd, lo), 12: rows(sd, hi)}
        relay = {1: [(5, rows(xn, lo), yn), (7, rows(xn, lo), sib)], 3: [(6, rows(yn, hi), xn), (9, rows(yn, hi), sib)],
                 2: [(8, rows(xn, hi), sib)], 4: [(10, rows(yn, lo), sib)],
                 5: [(11, rows(dg, lo), sib)], 6: [(12, rows(dg, hi), sib)]}
        return copy, mine, first, arrive, relay, me

    def start(self, ins, outs, sems):
        _, mine, first, _, _, _ = self._parts(ins, outs, sems)
        for cp in [mine] + first:
            cp.start()

    def finish(self, ins, outs, sems):
        copy, mine, first, arrive, relay, me = self._parts(ins, outs, sems)
        passed = []
        for k in (1, 3, 2, 4, 5, 6):
            copy(k, arrive[k], me).wait_recv()
            for k2, dev_rows, to in relay[k]:
                fwd = copy(k2, dev_rows, to)
                fwd.start()
                passed.append(fwd)
        for k in (0, 7, 8, 9, 10, 11, 12):
            copy(k, arrive[k], me).wait_recv()
        for cp in first + passed:
            cp.wait_send()
        mine.wait()


FLIPS = [(0, 0, 1), (1, 0, 0), (0, 1, 0), (1, 1, 0), (1, 0, 1), (0, 1, 1), (1, 1, 1)]


class _Scatter:
    def __init__(self, arrs, part=0, nparts=1):
        self.ins = list(arrs)
        self.part, self.nparts = part, nparts
        n = len(arrs)
        self.out_shape = [jax.ShapeDtypeStruct((a.shape[0] // nparts, a.shape[1]), a.dtype) for a in arrs]
        self.scratch = [pltpu.SemaphoreType.DMA((n, 7)), pltpu.SemaphoreType.DMA((n, 7)), pltpu.SemaphoreType.DMA((n,))]

    def _parts(self, ins, outs, sems):
        send_sems, recv_sems, local_sems = sems
        n = len(ins)
        x, y, c = _place()
        me = 4 * x + 2 * y + c

        def flip(v, f):
            return 1 - v if f else v

        def src(a, idx):
            r = self.ins[a].shape[0] // N_DEV
            rs = r // self.nparts
            return ins[a].at[pl.ds(pl.multiple_of(idx * r + self.part * rs, 16), rs), :]

        def dst(a, idx):
            rs = self.ins[a].shape[0] // N_DEV // self.nparts
            return outs[a].at[pl.ds(pl.multiple_of(idx * rs, 16), rs), :]

        mine = [pltpu.make_async_copy(src(a, me), dst(a, me), local_sems.at[a]) for a in range(n)]
        sends, recvs = [], []
        for k, f in enumerate(FLIPS):
            peer = (flip(x, f[0]), flip(y, f[1]), flip(c, f[2]))
            pidx = 4 * peer[0] + 2 * peer[1] + peer[2]
            for a in range(n):
                sends.append(pltpu.make_async_remote_copy(
                    src_ref=src(a, pidx), dst_ref=dst(a, me),
                    send_sem=send_sems.at[a, k], recv_sem=recv_sems.at[a, k], device_id=peer, device_id_type=MESH))
                recvs.append(functools.partial(
                    pltpu.make_async_remote_copy,
                    src_ref=src(a, pidx), dst_ref=dst(a, pidx),
                    send_sem=send_sems.at[a, k], recv_sem=recv_sems.at[a, k], device_id=peer, device_id_type=MESH))
        return mine, sends, recvs

    def start(self, ins, outs, sems):
        mine, sends, _ = self._parts(ins, outs, sems)
        for cp in mine + sends:
            cp.start()

    def finish(self, ins, outs, sems):
        mine, sends, recvs = self._parts(ins, outs, sems)
        for make in recvs:
            make().wait_recv()
        for cp in sends:
            cp.wait_send()
        for cp in mine:
            cp.wait()


N_CHIP = 4


class _SiblingSwap:
    def __init__(self, arr):
        self.ins = [arr]
        self.r = arr.shape[0] // N_DEV
        self.out_shape = [jax.ShapeDtypeStruct((N_CHIP * self.r, arr.shape[1]), arr.dtype)]
        self.scratch = [pltpu.SemaphoreType.DMA((N_CHIP,)), pltpu.SemaphoreType.DMA((N_CHIP,))]

    def _copies(self, ins, outs, sems):
        send_sems, recv_sems = sems
        x, y, c = _place()
        r = self.r
        return [pltpu.make_async_remote_copy(
            src_ref=ins[0].at[pl.ds(pl.multiple_of((2 * j + 1 - c) * r, 16), r), :],
            dst_ref=outs[0].at[pl.ds(j * r, r), :],
            send_sem=send_sems.at[j], recv_sem=recv_sems.at[j], device_id=(x, y, 1 - c), device_id_type=MESH)
            for j in range(N_CHIP)]

    def start(self, ins, outs, sems):
        for cp in self._copies(ins, outs, sems):
            cp.start()

    def finish(self, ins, outs, sems):
        for cp in self._copies(ins, outs, sems):
            cp.wait()


class _ChipScatter:
    def __init__(self, arr):
        self.ins = [arr]
        self.r = arr.shape[0] // N_CHIP
        self.out_shape = [jax.ShapeDtypeStruct(arr.shape, arr.dtype)]
        self.scratch = [pltpu.SemaphoreType.DMA((3,)), pltpu.SemaphoreType.DMA((3,)), pltpu.SemaphoreType.DMA]

    def _parts(self, ins, outs, sems):
        send_sems, recv_sems, local_sem = sems
        x, y, c = _place()
        r = self.r
        my_chip = 2 * x + y

        def rows(ref, j):
            return ref.at[pl.ds(pl.multiple_of(j * r, 16), r), :]

        mine = pltpu.make_async_copy(rows(ins[0], my_chip), rows(outs[0], my_chip), local_sem)
        sends, recvs = [], []
        for k, (fx, fy) in enumerate(((1, 0), (0, 1), (1, 1))):
            px, py = (1 - x if fx else x), (1 - y if fy else y)
            peer_chip = 2 * px + py
            sends.append(pltpu.make_async_remote_copy(
                src_ref=rows(ins[0], peer_chip), dst_ref=rows(outs[0], my_chip),
                send_sem=send_sems.at[k], recv_sem=recv_sems.at[k], device_id=(px, py, c), device_id_type=MESH))
            recvs.append(functools.partial(
                pltpu.make_async_remote_copy,
                src_ref=rows(ins[0], peer_chip), dst_ref=rows(outs[0], peer_chip),
                send_sem=send_sems.at[k], recv_sem=recv_sems.at[k], device_id=(px, py, c), device_id_type=MESH))
        return mine, sends, recvs

    def start(self, ins, outs, sems):
        mine, sends, _ = self._parts(ins, outs, sems)
        for cp in [mine] + sends:
            cp.start()

    def finish(self, ins, outs, sems):
        mine, sends, recvs = self._parts(ins, outs, sems)
        for make in recvs:
            make().wait_recv()
        for cp in sends:
            cp.wait_send()
        mine.wait()


def _pair_add(partial, recv):
    r = recv.shape[0] // N_CHIP
    cols = recv.shape[1]
    tr = r // 2 if (r // 2) % 16 == 0 else r
    steps = r // tr
    core = lax.axis_index("c").astype(jnp.int32).reshape(1)

    def body(c_ref, p_ref, s_ref, o_ref):
        o_ref[...] = (p_ref[...].astype(F32) + s_ref[...].astype(F32)).astype(BF16)

    spec = pl.BlockSpec((tr, cols), lambda j, i, c_ref: (j * steps + i, 0))
    return pl.pallas_call(
        body, name="pair_add",
        grid_spec=pltpu.PrefetchScalarGridSpec(
            num_scalar_prefetch=1, grid=(N_CHIP, steps),
            in_specs=[pl.BlockSpec((tr, cols), lambda j, i, c_ref: ((2 * j + c_ref[0]) * steps + i, 0)), spec],
            out_specs=spec),
        out_shape=jax.ShapeDtypeStruct(recv.shape, BF16),
        compiler_params=_cparams("parallel", "parallel"),
    )(core, partial, recv)


class _Both:
    def __init__(self, a, b):
        self.a, self.b = a, b
        self.ins = a.ins + b.ins
        self.out_shape = a.out_shape + b.out_shape
        self.scratch = a.scratch + b.scratch

    def _split(self, ins, outs, sems):
        ni, no, ns = len(self.a.ins), len(self.a.out_shape), len(self.a.scratch)
        return (ins[:ni], outs[:no], sems[:ns]), (ins[ni:], outs[no:], sems[ns:])

    def start(self, ins, outs, sems):
        ra, rb = self._split(ins, outs, sems)
        self.a.start(*ra)
        self.b.start(*rb)

    def finish(self, ins, outs, sems):
        ra, rb = self._split(ins, outs, sems)
        self.a.finish(*ra)
        self.b.finish(*rb)


def _exchange(comm, name):
    n, m = len(comm.ins), len(comm.out_shape)

    def body(*refs):
        ins, outs, sems = refs[:n], refs[n:n + m], refs[n + m:]
        comm.start(ins, outs, sems)
        comm.finish(ins, outs, sems)

    return pl.pallas_call(
        body, name=name, out_shape=comm.out_shape, in_specs=[ANY] * n, out_specs=[ANY] * m, scratch_shapes=comm.scratch,
    )(*comm.ins)


def _call(body, *, name, grid, in_specs, out_specs, out_shape, args, scratch=(), sem="parallel", comm=None):
    if comm is None:
        outs = pl.pallas_call(
            body, name=name, grid=grid, in_specs=list(in_specs), out_specs=list(out_specs), out_shape=list(out_shape),
            scratch_shapes=list(scratch), compiler_params=_cparams(sem))(*args)
        return outs, []
    n_in, n_out, n_sc = len(in_specs), len(out_specs), len(scratch)
    n_ci, n_co = len(comm.ins), len(comm.out_shape)
    last = grid[0] - 1

    def fused(*refs):
        ins, refs = refs[:n_in], refs[n_in:]
        c_ins, refs = refs[:n_ci], refs[n_ci:]
        outs, refs = refs[:n_out], refs[n_out:]
        c_outs, refs = refs[:n_co], refs[n_co:]
        sc, c_sems = refs[:n_sc], refs[n_sc:]
        step = pl.program_id(0)

        @pl.when(step == 0)
        def _():
            comm.start(c_ins, c_outs, c_sems)

        body(*ins, *outs, *sc)

        @pl.when(step == last)
        def _():
            comm.finish(c_ins, c_outs, c_sems)

    outs = pl.pallas_call(
        fused, name=name, grid=grid, in_specs=list(in_specs) + [ANY] * n_ci, out_specs=list(out_specs) + [ANY] * n_co,
        out_shape=list(out_shape) + comm.out_shape, scratch_shapes=list(scratch) + comm.scratch,
        compiler_params=_cparams("arbitrary"))(*args, *comm.ins)
    return outs[:n_out], outs[n_out:]


def _token_specs(tm):
    k = tm // BLK
    return [pl.BlockSpec((BLK, D), functools.partial(lambda i, t: (jnp.maximum(k * i + t - 1, 0), 0), t=t)) for t in range(k)]


def _in_proj(x2d, meta, gain, w_int, b_in, tabs, comm=None):
    p = x2d.shape[0] + BLK
    tm = _row_tile(p)
    k = tm // BLK

    def body(*refs):
        x_refs = refs[:k]
        m_ref, g_ref, w_ref, b_ref, t_ref, h_ref, n1_ref, q_ref, kv_ref, ag_ref, gt_ref = refs[k:]
        i = pl.program_id(0)
        head = jnp.concatenate([jnp.zeros((PAD, D), F32), m_ref[...]], axis=0)
        first = jnp.where(i == 0, head, x_refs[0][...])
        h = jnp.concatenate([first] + [r[...] for r in x_refs[1:]], axis=0) if k > 1 else first
        h_ref[...] = h
        n = _rms(h, g_ref[...]).astype(BF16)
        n1_ref[...] = n
        c, s1, s2 = t_ref[:, 0:128], t_ref[:, 128:256], t_ref[:, 256:384]

        def mm(c0, w):
            return _dot_nt(n, w_ref[c0:c0 + w, :]) + b_ref[:, c0:c0 + w]

        for j in range(4):
            acc = mm(256 * j, 256)
            for t in range(2):
                lo = 256 * j + 128 * t
                q_ref[:, lo:lo + 128] = (_rope(acc[:, 128 * t:128 * (t + 1)], c, s1, s2) * SCALE).astype(BF16)
        acc = mm(1024, 256)
        kv_ref[:, 0:128] = _rope(acc[:, 0:128], c, s1, s2).astype(BF16)
        kv_ref[:, 128:256] = acc[:, 128:256].astype(BF16)
        for j in range(8):
            ag_ref[:, 256 * j:256 * (j + 1)] = mm(QKV_W + 256 * j, 256).astype(BF16)
        for j in range(8):
            gt_ref[:, 256 * j:256 * (j + 1)] = mm(QKV_W + 2048 + 256 * j, 256).astype(BF16)

    def row(w):
        return pl.BlockSpec((tm, w), lambda i: (i, 0))

    return _call(
        body, name="in_proj", grid=(p // tm,),
        in_specs=_token_specs(tm) + [VM, VM, VM, VM, row(384)],
        out_specs=[row(D), row(D), row(D), row(256), row(2048), row(2048)],
        out_shape=[jax.ShapeDtypeStruct((p, D), F32)] + [jax.ShapeDtypeStruct((p, w), BF16) for w in (D, D, 256, 2048, 2048)],
        args=(x2d,) * k + (meta, gain, w_int, b_in, tabs), comm=comm)


N_KEY = 2 * BLK + N_META


def _attn_setup(n, h, q_ref, km_ref, kp_ref, kc_ref):
    lo = lax.broadcasted_iota(jnp.int32, (BLK, BLK), 1) < HEAD_DIM
    lok = lax.broadcasted_iota(jnp.int32, (N_KEY, BLK), 1) < HEAD_DIM

    def dup(lanes):
        cat = jnp.concatenate([kp_ref[:, lanes], kc_ref[:, lanes], km_ref[PAD:BLK, lanes]], axis=0).astype(F32)
        rolled = pltpu.roll(cat, HEAD_DIM, 1)
        return (jnp.where(lok, cat, rolled) if h == 0 else jnp.where(lok, rolled, cat)).astype(BF16)

    k2 = dup(slice(0, 128))
    v2 = dup(slice(128, 256))
    qs = _stack_heads(q_ref, h, lo)

    kr = lax.broadcasted_iota(jnp.int32, (BLK, BLK), 0)
    tq = BLK * n + lax.broadcasted_iota(jnp.int32, (BLK, BLK), 1) - PAD
    t_p = BLK * (n - 1) + kr - PAD
    t_c = BLK * n + kr - PAD
    ok_p = jnp.logical_and(t_p >= N_META, tq - t_p < BLK)
    ok_c = jnp.logical_and(t_c >= N_META, t_c <= tq)
    ok_m = lax.broadcasted_iota(jnp.int32, (N_META, BLK), 0) <= BLK * n + lax.broadcasted_iota(jnp.int32, (N_META, BLK), 1) - PAD
    bias = jnp.concatenate([jnp.where(ok, 0.0, NEG_INF).astype(F32) for ok in (ok_p, ok_c, ok_m)], axis=0)
    return qs, k2, v2, bias, lok


def _attn_head(s, bias, sink):
    s = s + bias
    m = jnp.maximum(jnp.max(s, axis=0, keepdims=True), sink)
    e = jnp.exp(s - m)
    es = jnp.exp(sink - m)
    inv = 1.0 / (jnp.sum(e, axis=0, keepdims=True) + es)
    return e * inv, es * inv


def _stack_heads(ref, h, lo):
    pieces = []
    for jp in range(4):
        v = ref[:, BLK * (4 * h + jp):BLK * (4 * h + jp + 1)]
        zero = jnp.zeros_like(v)
        pieces += [jnp.where(lo, v, zero), jnp.where(lo, zero, v)]
    return jnp.concatenate(pieces, axis=0)


def _unstack_heads(v, jp, lo):
    return jnp.where(lo, v[256 * jp:256 * jp + 128], v[256 * jp + 128:256 * jp + 256])


def _attn_fwd(q, kv, sinks, comm=None):
    p = q.shape[0]
    nb = p // BLK

    def body(q_ref, km_ref, kp_ref, kc_ref, sink_ref, o_ref):
        n = pl.program_id(0)
        lo = lax.broadcasted_iota(jnp.int32, (BLK, BLK), 1) < HEAD_DIM
        for h in range(2):
            qs, k2, v2, bias, _ = _attn_setup(n, h, q_ref, km_ref, kp_ref, kc_ref)
            st = _dot_nt(k2, qs)
            pt = jnp.concatenate(
                [_attn_head(st[:, BLK * g:BLK * (g + 1)], bias, sink_ref[0, 8 * h + g])[0].astype(BF16) for g in range(8)],
                axis=1)
            o = _dot_tn(pt, v2)
            for jp in range(4):
                o_ref[:, BLK * (4 * h + jp):BLK * (4 * h + jp + 1)] = _unstack_heads(o, jp, lo).astype(BF16)

    return _call(
        body, name="attn_fwd", grid=(nb,),
        in_specs=[pl.BlockSpec((BLK, D), lambda i: (i, 0)),
                  pl.BlockSpec((BLK, 256), lambda i: (0, 0)),
                  pl.BlockSpec((BLK, 256), lambda i: (jnp.maximum(i - 1, 0), 0)),
                  pl.BlockSpec((BLK, 256), lambda i: (i, 0)),
                  pl.BlockSpec(memory_space=pltpu.SMEM)],
        out_specs=[pl.BlockSpec((BLK, D), lambda i: (i, 0))],
        out_shape=[jax.ShapeDtypeStruct((p, D), BF16)],
        args=(q, kv, kv, kv, sinks), comm=comm)


def _conv31_fwd(ag, w32, b, comm=None):
    p = ag.shape[0]
    nch = p // BLK

    def body(a_ref, g_ref, w_ref, b_ref, o_ref, gp):
        gp[0:32, :] = jnp.zeros((32, BLK), F32)
        for ci in range(nch):
            r0 = BLK * ci
            glu = a_ref[r0:r0 + BLK, :].astype(F32) * jax.nn.sigmoid(g_ref[r0:r0 + BLK, :].astype(F32))
            if ci == 0:
                glu = jnp.where(_rows(0, BLK) >= PAD, glu, 0.0)
            gp[32 + r0:32 + r0 + BLK, :] = glu
        for ci in range(nch):
            r0 = BLK * ci
            acc = jnp.broadcast_to(b_ref[...], (BLK, BLK))
            for j in range(CONV_K):
                acc = acc + w_ref[j:j + 1, :] * gp[r0 + j + 2:r0 + j + 2 + BLK, :]
            o_ref[r0:r0 + BLK, :] = acc

    return _call(
        body, name="conv31_fwd", grid=(D // BLK,),
        in_specs=[pl.BlockSpec((p, BLK), lambda j: (0, j)), pl.BlockSpec((p, BLK), lambda j: (0, 8 + j)),
                  pl.BlockSpec((32, BLK), lambda j: (0, j)), pl.BlockSpec((1, BLK), lambda j: (0, j))],
        out_specs=[pl.BlockSpec((p, BLK), lambda j: (0, j))],
        out_shape=[jax.ShapeDtypeStruct((p, D), F32)],
        scratch=[pltpu.VMEM((p + 32, BLK), F32)],
        args=(ag, ag, w32, b), comm=comm)


def _mixer_fwd(ao, c0, gates, h0p, wa, wc, wo, vecs):
    p = ao.shape[0]
    tm = _row_tile(p)

    def body(ao_ref, c0_ref, gt_ref, h_ref, wa_ref, wc_ref, wo_ref, v_ref,
             c1_ref, at_ref, cv_ref, mg_ref, mix_ref, h1_ref, n2_ref):
        i = pl.program_id(0)
        c1 = _lnsilu(c0_ref[...], v_ref[0:1, :], v_ref[1:2, :]).astype(BF16)
        c1_ref[...] = c1
        attn = _dot(ao_ref[...], wa_ref[...])
        conv = _dot(c1, wc_ref[...]) + v_ref[2:3, :]
        at_ref[...] = attn.astype(BF16)
        cv_ref[...] = conv.astype(BF16)
        merged = (jax.nn.sigmoid(gt_ref[:, 0:D].astype(F32)) * attn
                  + jax.nn.sigmoid(gt_ref[:, D:2 * D].astype(F32)) * conv).astype(BF16)
        mg_ref[...] = merged
        mix = _dot(merged, wo_ref[...])
        mix_ref[...] = mix
        h1 = jnp.where(_rows(i, tm) >= PAD, h_ref[...] + _rms(mix, v_ref[3:4, :]), 0.0)
        h1_ref[...] = h1
        n2_ref[...] = _rms(h1, v_ref[4:5, :]).astype(BF16)

    def row(w):
        return pl.BlockSpec((tm, w), lambda i: (i, 0))

    return pl.pallas_call(
        body, name="mixer_fwd", grid=(p // tm,),
        in_specs=[row(D), row(D), row(2 * D), row(D), VM, VM, VM, VM],
        out_specs=[row(D)] * 7,
        out_shape=[jax.ShapeDtypeStruct((p, D), t) for t in (BF16, BF16, BF16, BF16, F32, F32, BF16)],
        compiler_params=_cparams("parallel"),
    )(ao, c0, gates, h0p, wa, wc, wo, vecs)


def _mm_nt(a, w_t, name):
    p, k = a.shape
    n = w_t.shape[0]
    tm = _row_tile(p)
    ch = 512

    def body(a_ref, w_ref, o_ref):
        a_v = a_ref[...]
        for c0 in range(0, n, ch):
            o_ref[:, c0:c0 + ch] = _dot_nt(a_v, w_ref[c0:c0 + ch, :]).astype(BF16)

    return pl.pallas_call(
        body, name=name, grid=(p // tm,),
        in_specs=[pl.BlockSpec((tm, k), lambda i: (i, 0)), VM],
        out_specs=pl.BlockSpec((tm, n), lambda i: (i, 0)),
        out_shape=jax.ShapeDtypeStruct((p, n), BF16),
        compiler_params=_cparams("parallel"),
    )(a, w_t)


def _ffn_slab_specs(p):
    ncol = FFN // BLK
    return [pl.BlockSpec((p, BLK), lambda j: (0, j)), pl.BlockSpec((p, BLK), lambda j: (0, ncol + j)),
            pl.BlockSpec((FFN_K, BLK), lambda j: (0, j)), pl.BlockSpec((FFN_K, BLK), lambda j: (0, ncol + j)),
            pl.BlockSpec((1, BLK), lambda j: (0, j)), pl.BlockSpec((1, BLK), lambda j: (0, ncol + j))]


def _ffn_act(u0, fw, fb):
    p = u0.shape[0]
    wide = 2 * BLK
    rc = BLK // 2
    nsl = FFN // wide

    def body(g_ref, v_ref, wg_ref, wv_ref, bg_ref, bv_ref, o_ref, dv_ref, dg_ref, xg, xv):
        for x_s, src in ((xg, g_ref), (xv, v_ref)):
            x_s[0:8, :] = jnp.zeros((8, wide), F32)
            for r0 in range(0, p, rc):
                x_s[8 + r0:8 + r0 + rc, :] = src[r0:r0 + rc, :].astype(F32)

        def conv(x_s, w_ref, r0):
            return (w_ref[0:1, :] * x_s[r0 + 6:r0 + 6 + rc, :] + w_ref[1:2, :] * x_s[r0 + 7:r0 + 7 + rc, :]
                    + w_ref[2:3, :] * x_s[r0 + 8:r0 + 8 + rc, :])

        for r0 in range(0, p, rc):
            ug = conv(xg, wg_ref, r0) + bg_ref[...]
            uv = conv(xv, wv_ref, r0) + bv_ref[...]
            sg = jax.nn.sigmoid(ug)
            silu = ug * sg
            o_ref[r0:r0 + rc, :] = (silu * uv).astype(BF16)
            dv_ref[r0:r0 + rc, :] = silu.astype(BF16)
            dg_ref[r0:r0 + rc, :] = (uv * (sg * (1.0 + ug * (1.0 - sg)))).astype(BF16)

    slab = pl.BlockSpec((p, wide), lambda j: (0, j))
    return pl.pallas_call(
        body, name="ffn_act", grid=(nsl,),
        in_specs=[slab, pl.BlockSpec((p, wide), lambda j: (0, nsl + j)),
                  pl.BlockSpec((FFN_K, wide), lambda j: (0, j)), pl.BlockSpec((FFN_K, wide), lambda j: (0, nsl + j)),
                  pl.BlockSpec((1, wide), lambda j: (0, j)), pl.BlockSpec((1, wide), lambda j: (0, nsl + j))],
        out_specs=[slab] * 3,
        out_shape=[jax.ShapeDtypeStruct((p, FFN), BF16)] * 3,
        scratch_shapes=[pltpu.VMEM((p + 8, wide), F32)] * 2,
        compiler_params=_cparams("parallel"),
    )(u0, u0, fw, fw, fb, fb)


def _ffn_down_loss(act, wd, h1, tgt, gain):
    p = act.shape[0]
    tm = _row_tile(p)
    k = tm // BLK

    def body(*refs):
        a_ref, w_ref, h_ref = refs[:3]
        t_refs = refs[3:3 + k]
        g_ref, df_ref, da_ref, dy_ref, acc_ref = refs[3 + k:]
        i = pl.program_id(0)

        @pl.when(i == 0)
        def _():
            acc_ref[...] = jnp.zeros_like(acc_ref)

        ffn = _dot(a_ref[...], w_ref[...])
        t = jnp.concatenate([t_ref[...] for t_ref in t_refs], axis=0) if k > 1 else t_refs[0][...]
        diff = jnp.where(_rows(i, tm) >= BLK, h_ref[...] + _rms(ffn, g_ref[...]) - t, 0.0)
        dy = diff * (1.0 / D)
        dffn, dg = _rms_bwd(ffn, g_ref[...], dy)
        acc_ref[0:1, :] += dg
        acc_ref[1:2, :] += jnp.sum(diff * diff, axis=0, keepdims=True) * (0.5 / D)
        dy_ref[...] = dy
        dfb = dffn.astype(BF16)
        df_ref[...] = dfb
        for c0 in range(0, FFN, 256):
            da_ref[:, c0:c0 + 256] = _dot_nt(dfb, w_ref[c0:c0 + 256, :]).astype(BF16)

    def row(w):
        return pl.BlockSpec((tm, w), lambda i: (i, 0))

    return pl.pallas_call(
        body, name="ffn_down_loss", grid=(p // tm,),
        in_specs=[row(FFN), VM, row(D)] + _token_specs(tm) + [VM],
        out_specs=[row(D), row(FFN), row(D), pl.BlockSpec((8, D), lambda i: (0, 0))],
        out_shape=[jax.ShapeDtypeStruct((p, D), BF16), jax.ShapeDtypeStruct((p, FFN), BF16),
                   jax.ShapeDtypeStruct((p, D), F32), jax.ShapeDtypeStruct((8, D), F32)],
        compiler_params=_cparams("arbitrary"),
    )(act, wd, h1, *([tgt] * k), gain)


def _mm_tn(pieces, b, name, col_sums=False, comm=None):
    p, n = b.shape
    tk = 256
    nblk = [a.shape[1] // tk for a in pieces]
    offs = [sum(nblk[:q]) for q in range(len(pieces))]
    total = sum(nblk)
    npc = len(pieces)

    def body(*refs):
        a_refs, b_ref, o_ref = refs[:npc], refs[npc], refs[npc + 1]
        i = pl.program_id(0)
        for q, a_ref in enumerate(a_refs):
            @pl.when(jnp.logical_and(i >= offs[q], i < offs[q] + nblk[q]))
            def _(a_ref=a_ref):
                a_v = a_ref[...]
                o_ref[...] = _dot_tn(a_v, b_ref[...]).astype(BF16)
                if col_sums:
                    refs[npc + 2][...] = jnp.sum(a_v.astype(F32), axis=0, keepdims=True)

    def a_spec(q):
        return pl.BlockSpec((p, tk), lambda i: (0, jnp.clip(i - offs[q], 0, nblk[q] - 1)))

    out_specs = [pl.BlockSpec((tk, n), lambda i: (i, 0))]
    out_shape = [jax.ShapeDtypeStruct((total * tk, n), BF16)]
    if col_sums:
        out_specs.append(pl.BlockSpec((1, tk), lambda i: (0, i)))
        out_shape.append(jax.ShapeDtypeStruct((1, total * tk), F32))
    res, sent = _call(
        body, name=name, grid=(total,),
        in_specs=[a_spec(q) for q in range(npc)] + [VM],
        out_specs=out_specs, out_shape=out_shape, args=(*pieces, b), comm=comm)
    res = res if col_sums else res[0]
    return res if comm is None else (res, sent)


def _ffn_act_bwd(u0, dact, dact_dg, dact_dv, fw, act, dffn, comm=None):
    p = u0.shape[0]
    nch = p // BLK
    ncol = FFN // BLK

    def body(g_ref, v_ref, wg_ref, wv_ref, da_ref, lg_ref, lv_ref, act_ref, df_ref,
             dg_ref, dv_ref, gwg_ref, gwv_ref, gbg_ref, gbv_ref, gwd_ref, eg, ev):
        gwd_ref[...] = _dot_tn(act_ref[...], df_ref[...]).astype(BF16)
        eg[p:p + 8, :] = jnp.zeros((8, BLK), F32)
        ev[p:p + 8, :] = jnp.zeros((8, BLK), F32)
        for ci in range(nch):
            r0 = BLK * ci
            d = da_ref[r0:r0 + BLK, :].astype(F32)
            eg[r0:r0 + BLK, :] = d * lg_ref[r0:r0 + BLK, :].astype(F32)
            ev[r0:r0 + BLK, :] = d * lv_ref[r0:r0 + BLK, :].astype(F32)
        def fold(v):
            return jnp.sum(v.reshape(BLK // 8, 8, BLK), axis=0)

        for e_s, x_ref, w_ref, d_ref, gw_ref, gb_ref in ((eg, g_ref, wg_ref, dg_ref, gwg_ref, gbg_ref),
                                                        (ev, v_ref, wv_ref, dv_ref, gwv_ref, gbv_ref)):
            sums = [jnp.zeros((8, BLK), F32) for _ in range(FFN_K + 1)]
            for ci in range(nch):
                r0 = BLK * ci
                es = [e_s[r0 + t:r0 + t + BLK, :] for t in range(FFN_K)]
                du = w_ref[2:3, :] * es[0] + w_ref[1:2, :] * es[1] + w_ref[0:1, :] * es[2]
                if ci == 0:
                    du = jnp.where(_rows(0, BLK) >= PAD, du, 0.0)
                d_ref[r0:r0 + BLK, :] = du.astype(BF16)
                x = x_ref[r0:r0 + BLK, :].astype(F32)
                for j in range(FFN_K):
                    sums[j] = sums[j] + fold(es[FFN_K - 1 - j] * x)
                sums[FFN_K] = sums[FFN_K] + fold(es[0])
            for j in range(FFN_K):
                gw_ref[j:j + 1, :] = jnp.sum(sums[j], axis=0, keepdims=True)
            gb_ref[...] = jnp.sum(sums[FFN_K], axis=0, keepdims=True)

    slab = pl.BlockSpec((p, BLK), lambda j: (0, j))
    wspec = pl.BlockSpec((FFN_K, BLK), lambda j: (0, j))
    bspec = pl.BlockSpec((1, BLK), lambda j: (0, j))
    return _call(
        body, name="ffn_act_bwd", grid=(ncol,),
        in_specs=_ffn_slab_specs(p)[:4] + [slab] * 4 + [VM],
        out_specs=[slab, slab, wspec, wspec, bspec, bspec, pl.BlockSpec((BLK, D), lambda j: (j, 0))],
        out_shape=[jax.ShapeDtypeStruct((p, FFN), BF16)] * 2 + [jax.ShapeDtypeStruct((FFN_K, FFN), F32)] * 2
        + [jax.ShapeDtypeStruct((1, FFN), F32)] * 2 + [jax.ShapeDtypeStruct((FFN, D), BF16)],
        scratch=[pltpu.VMEM((p + 8, BLK), F32)] * 2,
        args=(u0, u0, fw, fw, dact, dact_dg, dact_dv, act, dffn), comm=comm)


def _ffn_in_bwd(dug, duv, w_upt, h1, dy, gain, comm=None):
    p = h1.shape[0]
    tm = _row_tile(p)

    def body(dg_ref, dv_ref, w_ref, h_ref, dy_ref, g_ref, o_ref, acc_ref):
        i = pl.program_id(0)

        @pl.when(i == 0)
        def _():
            acc_ref[...] = jnp.zeros_like(acc_ref)

        dn = _dot(dg_ref[...], w_ref[0:FFN, :]) + _dot(dv_ref[...], w_ref[FFN:2 * FFN, :])
        dh, dg = _rms_bwd(h_ref[...], g_ref[...], dn)
        o_ref[...] = dy_ref[...] + dh
        acc_ref[0:1, :] += dg

    def row(w):
        return pl.BlockSpec((tm, w), lambda i: (i, 0))

    return _call(
        body, name="ffn_in_bwd", grid=(p // tm,),
        in_specs=[row(FFN), row(FFN), VM, row(D), row(D), VM],
        out_specs=[row(D), pl.BlockSpec((8, D), lambda i: (0, 0))],
        out_shape=[jax.ShapeDtypeStruct((p, D), F32), jax.ShapeDtypeStruct((8, D), F32)],
        sem="arbitrary", args=(dug, duv, w_upt, h1, dy, gain), comm=comm)


def _mixer_bwd(dh1, mix, attn, conv, gates, c0, wa, wc, wo, vecs, comm=None):
    p = dh1.shape[0]
    tm = _row_tile(p)

    def body(dh_ref, mix_ref, at_ref, cv_ref, gt_ref, c0_ref, wa_ref, wc_ref, wo_ref, v_ref,
             dmix_ref, dat_ref, dcv_ref, dgt_ref, dao_ref, dc0_ref, acc_ref):
        i = pl.program_id(0)

        @pl.when(i == 0)
        def _():
            acc_ref[...] = jnp.zeros_like(acc_ref)

        dmix, dgp = _rms_bwd(mix_ref[...], v_ref[3:4, :], dh_ref[...])
        dmix = dmix.astype(BF16)
        dmix_ref[...] = dmix
        dmg = _dot_nt(dmix, wo_ref[...])
        sa = jax.nn.sigmoid(gt_ref[:, 0:D].astype(F32))
        sc = jax.nn.sigmoid(gt_ref[:, D:2 * D].astype(F32))
        dat = dmg * sa
        dcv = dmg * sc
        dgt_ref[:, 0:D] = (dmg * at_ref[...].astype(F32) * sa * (1.0 - sa)).astype(BF16)
        dgt_ref[:, D:2 * D] = (dmg * cv_ref[...].astype(F32) * sc * (1.0 - sc)).astype(BF16)
        datb = dat.astype(BF16)
        dcvb = dcv.astype(BF16)
        dat_ref[...] = datb
        dcv_ref[...] = dcvb
        dao_ref[...] = _dot_nt(datb, wa_ref[...]).astype(BF16)
        dc1 = _dot_nt(dcvb, wc_ref[...])
        dc0, dlg, dlb = _lnsilu_bwd(c0_ref[...], v_ref[0:1, :], v_ref[1:2, :], dc1)
        dc0_ref[...] = dc0
        acc_ref[0:1, :] += dgp
        acc_ref[1:2, :] += jnp.sum(dcv, axis=0, keepdims=True)
        acc_ref[2:3, :] += dlg
        acc_ref[3:4, :] += dlb

    def row(w):
        return pl.BlockSpec((tm, w), lambda i: (i, 0))

    return _call(
        body, name="mixer_bwd", grid=(p // tm,),
        in_specs=[row(D), row(D), row(D), row(D), row(2 * D), row(D), VM, VM, VM, VM],
        out_specs=[row(D), row(D), row(D), row(2 * D), row(D), row(D), pl.BlockSpec((8, D), lambda i: (0, 0))],
        out_shape=[jax.ShapeDtypeStruct((p, D), BF16)] * 3 + [jax.ShapeDtypeStruct((p, 2 * D), BF16),
                                                             jax.ShapeDtypeStruct((p, D), BF16),
                                                             jax.ShapeDtypeStruct((p, D), F32),
                                                             jax.ShapeDtypeStruct((8, D), F32)],
        sem="arbitrary", args=(dh1, mix, attn, conv, gates, c0, wa, wc, wo, vecs), comm=comm)


def _conv31_bwd(ag, dc0, w32, tn_pairs, comm=None):
    p = ag.shape[0]
    nch = p // BLK
    npair = len(tn_pairs)

    def body(*refs):
        a_ref, g_ref, dc_ref, w_ref = refs[:4]
        tn_a, tn_b = refs[4:4 + npair], refs[4 + npair:4 + 2 * npair]
        da_ref, dg_ref, gw_ref, gb_ref = refs[4 + 2 * npair:8 + 2 * npair]
        tn_o = refs[8 + 2 * npair:8 + 3 * npair]
        gp, dp = refs[8 + 3 * npair:]
        for ta, tb, to in zip(tn_a, tn_b, tn_o):
            to[...] = _dot_tn(ta[...], tb[...]).astype(BF16)
        gp[0:32, :] = jnp.zeros((32, BLK), F32)
        dp[p:p + 32, :] = jnp.zeros((32, BLK), F32)
        bsum = jnp.zeros((BLK, BLK), F32)
        for ci in range(nch):
            r0 = BLK * ci
            glu = a_ref[r0:r0 + BLK, :].astype(F32) * jax.nn.sigmoid(g_ref[r0:r0 + BLK, :].astype(F32))
            if ci == 0:
                glu = jnp.where(_rows(0, BLK) >= PAD, glu, 0.0)
            gp[32 + r0:32 + r0 + BLK, :] = glu
            d = dc_ref[r0:r0 + BLK, :]
            dp[r0:r0 + BLK, :] = d
            bsum = bsum + d
        gb_ref[...] = jnp.sum(bsum, axis=0, keepdims=True)
        for ci in range(nch):
            r0 = BLK * ci
            acc = jnp.zeros((BLK, BLK), F32)
            for j in range(CONV_K):
                acc = acc + w_ref[j:j + 1, :] * dp[r0 + 30 - j:r0 + 30 - j + BLK, :]
            if ci == 0:
                acc = jnp.where(_rows(0, BLK) >= PAD, acc, 0.0)
            a = a_ref[r0:r0 + BLK, :].astype(F32)
            sg = jax.nn.sigmoid(g_ref[r0:r0 + BLK, :].astype(F32))
            da_ref[r0:r0 + BLK, :] = (acc * sg).astype(BF16)
            dg_ref[r0:r0 + BLK, :] = (acc * a * sg * (1.0 - sg)).astype(BF16)
        sub = BLK // 2
        accs = [jnp.zeros((8, BLK), F32) for _ in range(CONV_K)]
        for r0 in range(0, p, sub):
            d = dp[r0:r0 + sub, :]
            for j in range(CONV_K):
                prod = d * gp[r0 + j + 2:r0 + j + 2 + sub, :]
                accs[j] = accs[j] + jnp.sum(prod.reshape(sub // 8, 8, BLK), axis=0)
        for j in range(CONV_K):
            gw_ref[j:j + 1, :] = jnp.sum(accs[j], axis=0, keepdims=True)
        gw_ref[CONV_K:32, :] = jnp.zeros((32 - CONV_K, BLK), F32)

    slab = pl.BlockSpec((p, BLK), lambda j: (0, j))
    return _call(
        body, name="conv31_bwd", grid=(D // BLK,),
        in_specs=[slab, pl.BlockSpec((p, BLK), lambda j: (0, 8 + j)), slab, pl.BlockSpec((32, BLK), lambda j: (0, j))]
        + [slab] * npair + [VM] * npair,
        out_specs=[slab, slab, pl.BlockSpec((32, BLK), lambda j: (0, j)), pl.BlockSpec((1, BLK), lambda j: (0, j))]
        + [pl.BlockSpec((BLK, D), lambda j: (j, 0))] * npair,
        out_shape=[jax.ShapeDtypeStruct((p, D), BF16)] * 2 + [jax.ShapeDtypeStruct((32, D), F32),
                                                             jax.ShapeDtypeStruct((1, D), F32)]
        + [jax.ShapeDtypeStruct((D, D), BF16)] * npair,
        scratch=[pltpu.VMEM((p + 32, BLK), F32)] * 2,
        args=(ag, ag, dc0, w32, *[a for a, _ in tn_pairs], *[b for _, b in tn_pairs]), comm=comm)


def _attn_bwd(q, kv, dao, sinks, tabs, comm=None):
    p = q.shape[0]
    nb = p // BLK

    def body(q_ref, km_ref, kp_ref, kc_ref, do_ref, sink_ref, t_ref, dqkv_ref, dsink_ref, carry, macc):
        i = pl.program_id(0)
        n = nb - 1 - i

        @pl.when(i == 0)
        def _():
            carry[...] = jnp.zeros_like(carry)
            macc[...] = jnp.zeros_like(macc)
            dsink_ref[...] = jnp.zeros_like(dsink_ref)

        lo = lax.broadcasted_iota(jnp.int32, (BLK, BLK), 1) < HEAD_DIM
        lane8 = lax.broadcasted_iota(jnp.int32, (8, BLK), 1)
        c, s1, s2 = t_ref[:, 0:128], -t_ref[:, 128:256], -t_ref[:, 256:384]
        dk = jnp.zeros((N_KEY, BLK), F32)
        dv = jnp.zeros((N_KEY, BLK), F32)
        for h in range(2):
            qs, k2, v2, bias, lok = _attn_setup(n, h, q_ref, km_ref, kp_ref, kc_ref)
            dos = _stack_heads(do_ref, h, lo)
            st = _dot_nt(k2, qs)
            dpt = _dot_nt(v2, dos)
            p_parts, ds_parts = [], []
            for g in range(8):
                cols = slice(BLK * g, BLK * (g + 1))
                pn, ps = _attn_head(st[:, cols], bias, sink_ref[0, 8 * h + g])
                dp = dpt[:, cols]
                delta = jnp.sum(pn * dp, axis=0, keepdims=True)
                ds_parts.append((pn * (dp - delta)).astype(BF16))
                p_parts.append(pn.astype(BF16))
                dsk = -jnp.sum(ps * delta, axis=1, keepdims=True)
                dsink_ref[...] += jnp.where(lane8 == 8 * h + g, dsk, 0.0)
            dst = jnp.concatenate(ds_parts, axis=1)
            pt = jnp.concatenate(p_parts, axis=1)
            dq = _dot_tn(dst, k2)
            for jp in range(4):
                lo_c = BLK * (4 * h + jp)
                dqkv_ref[:, lo_c:lo_c + BLK] = (_rope(_unstack_heads(dq, jp, lo), c, s1, s2) * SCALE).astype(BF16)
            dk2 = _dot(dst, qs)
            dv2 = _dot(pt, dos)
            dk2 = dk2 + pltpu.roll(dk2, HEAD_DIM, 1)
            dv2 = dv2 + pltpu.roll(dv2, HEAD_DIM, 1)
            own = lok if h == 0 else jnp.logical_not(lok)
            dk = jnp.where(own, dk2, dk)
            dv = jnp.where(own, dv2, dv)
        macc[:, 0:BLK] += dk[2 * BLK:N_KEY]
        macc[:, BLK:2 * BLK] += dv[2 * BLK:N_KEY]
        last = (n == 0).astype(F32)
        zpad = jnp.zeros((PAD, BLK), F32)
        dk_c = dk[BLK:2 * BLK] + carry[:, 0:BLK] + last * jnp.concatenate([zpad, macc[:, 0:BLK]], axis=0)
        dv_c = dv[BLK:2 * BLK] + carry[:, BLK:2 * BLK] + last * jnp.concatenate([zpad, macc[:, BLK:2 * BLK]], axis=0)
        carry[:, 0:BLK] = dk[0:BLK]
        carry[:, BLK:2 * BLK] = dv[0:BLK]
        dqkv_ref[:, D:D + BLK] = _rope(dk_c, c, s1, s2).astype(BF16)
        dqkv_ref[:, D + BLK:D + 2 * BLK] = dv_c.astype(BF16)

    def rev(w):
        return pl.BlockSpec((BLK, w), lambda i: (nb - 1 - i, 0))

    return _call(
        body, name="attn_bwd", grid=(nb,),
        in_specs=[rev(D),
                  pl.BlockSpec((BLK, 256), lambda i: (0, 0)),
                  pl.BlockSpec((BLK, 256), lambda i: (jnp.maximum(nb - 2 - i, 0), 0)),
                  rev(256), rev(D),
                  pl.BlockSpec(memory_space=pltpu.SMEM), rev(384)],
        out_specs=[rev(QKV_W), pl.BlockSpec((8, BLK), lambda i: (0, 0))],
        out_shape=[jax.ShapeDtypeStruct((p, QKV_W), BF16), jax.ShapeDtypeStruct((8, BLK), F32)],
        scratch=[pltpu.VMEM((BLK, 256), F32), pltpu.VMEM((N_META, 256), F32)], sem="arbitrary",
        args=(q, kv, kv, kv, dao, sinks, tabs), comm=comm)


def _in_bwd(dqkv, da, dg, dgt, w_int, h0p, dh1, gain, comm=None):
    p = h0p.shape[0]
    tm = _row_tile(p)
    nt = p // tm
    first_rows = tm - BLK

    def body(dq_ref, da_ref, dg_ref, dt_ref, w_ref, h_ref, dh_ref, g_ref, gx_ref, dm_ref, acc_ref, buf, sems):
        i = pl.program_id(0)
        slot = i % 2

        @pl.when(i == 0)
        def _():
            acc_ref[...] = jnp.zeros_like(acc_ref)

        dn = (_dot(dq_ref[...], w_ref[0:QKV_W, :]) + _dot(da_ref[...], w_ref[QKV_W:QKV_W + D, :])
              + _dot(dg_ref[...], w_ref[QKV_W + D:QKV_W + 2 * D, :]) + _dot(dt_ref[...], w_ref[QKV_W + 2 * D:IN_W, :]))
        dh, dgain = _rms_bwd(h_ref[...], g_ref[...], dn)
        dh0 = dh_ref[...] + dh
        acc_ref[0:1, :] += dgain
        buf[slot] = dh0

        @pl.when(i == 0)
        def _():
            dm_ref[...] = dh0[PAD:BLK]

        def first_copy():
            return pltpu.make_async_copy(buf.at[0, pl.ds(BLK, first_rows), :], gx_ref.at[pl.ds(0, first_rows), :], sems.at[0])

        def tile_copy(j, s):
            return pltpu.make_async_copy(buf.at[s], gx_ref.at[pl.ds(pl.multiple_of(j * tm - BLK, BLK), tm), :], sems.at[s])

        if first_rows:
            @pl.when(i == 1)
            def _():
                first_copy().wait()

        @pl.when(i >= 2)
        def _():
            tile_copy(i - 1, 1 - slot).wait()

        if first_rows:
            @pl.when(i == 0)
            def _():
                first_copy().start()

        @pl.when(i > 0)
        def _():
            tile_copy(i, slot).start()

        @pl.when(i == nt - 1)
        def _():
            tile_copy(i, slot).wait()

    def row(w):
        return pl.BlockSpec((tm, w), lambda i: (i, 0))

    return _call(
        body, name="in_bwd", grid=(nt,),
        in_specs=[row(QKV_W), row(D), row(D), row(2 * D), VM, row(D), row(D), VM],
        out_specs=[ANY, pl.BlockSpec((N_META, D), lambda i: (0, 0)), pl.BlockSpec((8, D), lambda i: (0, 0))],
        out_shape=[jax.ShapeDtypeStruct((p - BLK, D), F32), jax.ShapeDtypeStruct((N_META, D), F32),
                   jax.ShapeDtypeStruct((8, D), F32)],
        scratch=[pltpu.VMEM((2, tm, D), F32), pltpu.SemaphoreType.DMA((2,))],
        sem="arbitrary", args=(dqkv, da, dg, dgt, w_int, h0p, dh1, gain), comm=comm)


def _sum_slots(slots, name):
    r = slots.shape[0] // N_DEV
    cols = slots.shape[1]
    tr = r if r <= 352 else (r // 2 if (r // 2) % 16 == 0 else r // 3)
    steps = r // tr

    def body(*refs):
        acc = refs[0][...].astype(F32)
        for s in range(1, N_DEV):
            acc = acc + refs[s][...].astype(F32)
        refs[N_DEV][...] = acc

    return pl.pallas_call(
        body, name=name, grid=(steps,),
        in_specs=[pl.BlockSpec((tr, cols), functools.partial(lambda i, s: (s * steps + i, 0), s=s)) for s in range(N_DEV)],
        out_specs=pl.BlockSpec((tr, cols), lambda i: (i, 0)),
        out_shape=jax.ShapeDtypeStruct((r, cols), F32),
        compiler_params=_cparams("parallel"),
    )(*([slots] * N_DEV))


def _adamw_math(w, g, m, v):
    m_n = ADAM_B1 * m + (1.0 - ADAM_B1) * g
    v_n = ADAM_B2 * v + (1.0 - ADAM_B2) * jnp.square(g)
    m_hat = m_n / (1.0 - ADAM_B1 ** ADAM_STEP)
    v_hat = v_n / (1.0 - ADAM_B2 ** ADAM_STEP)
    return -ADAM_LR * (m_hat / (jnp.sqrt(v_hat) + ADAM_EPS) + ADAM_WD * w), m_n, v_n


def _sum_adamw(parts, w, m, v, name, nslots=N_DEV):
    r, cols = w.shape
    rs = r // len(parts)
    tr = rs if rs <= 352 else (rs // 2 if (rs // 2) % 16 == 0 else rs // 3)
    steps = rs // tr

    def body(*refs):
        w_ref, m_ref, v_ref, g_ref, d_ref, nm_ref, nv_ref = refs[nslots * len(parts):]
        i = pl.program_id(0)
        for q in range(len(parts)):
            @pl.when(i // steps == q)
            def _(q=q):
                g = refs[nslots * q][...].astype(F32)
                for s in range(1, nslots):
                    g = g + refs[nslots * q + s][...].astype(F32)
                g_ref[...] = g
                d_ref[...], nm_ref[...], nv_ref[...] = _adamw_math(w_ref[...], g, m_ref[...], v_ref[...])

    def slot_spec(q, s):
        return pl.BlockSpec((tr, cols), lambda i: (s * steps + jnp.clip(i - q * steps, 0, steps - 1), 0))

    spec = pl.BlockSpec((tr, cols), lambda i: (i, 0))
    return pl.pallas_call(
        body, name=name, grid=(steps * len(parts),),
        in_specs=[slot_spec(q, s) for q in range(len(parts)) for s in range(nslots)] + [spec] * 3,
        out_specs=[spec] * 4, out_shape=[jax.ShapeDtypeStruct((r, cols), F32)] * 4,
        compiler_params=_cparams("parallel"),
    )(*[a for a in parts for _ in range(nslots)], w, m, v)


def _adamw(w, g, m, v, name):
    r, cols = w.shape
    tr = 256 if r % 256 == 0 else r

    def body(w_ref, g_ref, m_ref, v_ref, d_ref, nm_ref, nv_ref):
        d_ref[...], nm_ref[...], nv_ref[...] = _adamw_math(w_ref[...], g_ref[...], m_ref[...], v_ref[...])

    spec = pl.BlockSpec((tr, cols), lambda i: (i, 0))
    return pl.pallas_call(
        body, name=name, grid=(r // tr,),
        in_specs=[spec] * 4, out_specs=[spec] * 3,
        out_shape=[jax.ShapeDtypeStruct((r, cols), F32)] * 3,
        compiler_params=_cparams("parallel"),
    )(w, g, m, v)


def _rope_tables(p):
    half = ROT_DIM // 2
    lane = jnp.arange(BLK)
    seg = (lane % HEAD_DIM) // half
    inv_freq = ROPE_THETA ** (-(lane % half).astype(F32) * 2.0 / ROT_DIM)
    pos = (jnp.arange(p) - PAD).astype(F32)
    ang = pos[:, None] * inv_freq[None, :]
    cos = jnp.cos(ang)
    sin = jnp.sin(ang)
    c = jnp.where(seg[None, :] < 2, cos, 1.0)
    s1 = jnp.where(seg[None, :] == 0, -sin, 0.0)
    s2 = jnp.where(seg[None, :] == 1, sin, 0.0)
    return jnp.concatenate([c, s1, s2], axis=1).astype(F32)


def _flat_pack(parts, rows):
    flat = jnp.concatenate([a.reshape(-1).astype(F32) for a in parts])
    return jnp.pad(flat, (0, rows * D - flat.shape[0])).reshape(rows, D)


def _flat_unpack(pack, shapes):
    flat = pack.reshape(-1)
    out, off = [], 0
    for s in shapes:
        size = 1
        for e in s:
            size *= e
        out.append(flat[off:off + size].reshape(s))
        off += size
    return out


def kernel(x, meta_tokens, norm_pre_mix, norm_post_mix, w_in, b_in, attn_sinks, w_attn_proj, conv_dw_w, conv_dw_b, conv_ln_g, conv_ln_b, w_conv_proj, b_conv_proj, w_out, norm_pre_ffn, norm_post_ffn, w_up, ffn_dw_w, ffn_dw_b, w_down, loss_target, m_meta_tokens, m_norm_pre_mix, m_norm_post_mix, m_w_in, m_b_in, m_attn_sinks, m_w_attn_proj, m_conv_dw_w, m_conv_dw_b, m_conv_ln_g, m_conv_ln_b, m_w_conv_proj, m_b_conv_proj, m_w_out, m_norm_pre_ffn, m_norm_post_ffn, m_w_up, m_ffn_dw_w, m_ffn_dw_b, m_w_down, v_meta_tokens, v_norm_pre_mix, v_norm_post_mix, v_w_in, v_b_in, v_attn_sinks, v_w_attn_proj, v_conv_dw_w, v_conv_dw_b, v_conv_ln_g, v_conv_ln_b, v_w_conv_proj, v_b_conv_proj, v_w_out, v_norm_pre_ffn, v_norm_post_ffn, v_w_up, v_ffn_dw_w, v_ffn_dw_b, v_w_down):
    seq = x.shape[1]
    p = seq + BLK
    me = 4 * lax.axis_index("x") + 2 * lax.axis_index("y") + lax.axis_index("c")
    in_cols = w_in.shape[2]
    up_cols = w_up.shape[2]

    small = jnp.zeros((56, up_cols), F32)
    small = small.at[0:N_META, 0:BLK].set(meta_tokens)
    small = small.at[16:16 + CONV_K, 0:BLK].set(conv_dw_w[0])
    small = small.at[48:48 + FFN_K, :].set(ffn_dw_w[0])
    w_int, small_all = _exchange(_Both(_GatherRelay(w_in[0].T.astype(BF16)), _Gather([small])), "gather_w_in")
    small_all = small_all.reshape(N_DEV, 56, up_cols)
    meta_full = small_all[:, 0:N_META, 0:BLK].transpose(1, 0, 2).reshape(N_META, D)
    cdw = small_all[:, 16:16 + CONV_K, 0:BLK].transpose(1, 0, 2).reshape(CONV_K, D)
    cdw32 = jnp.pad(cdw, ((0, 32 - CONV_K), (0, 0)))
    fdw = small_all[:, 48:48 + FFN_K, :].transpose(1, 0, 2).reshape(FFN_K, 2 * FFN)

    tabs = _rope_tables(p)
    vecs = jnp.concatenate([conv_ln_g, conv_ln_b, b_conv_proj, norm_post_mix, norm_pre_ffn, jnp.zeros((3, D), F32)], axis=0)

    (h0p, n1, q, kv, ag, gates), (wa, wc, wo) = _in_proj(
        x[0], meta_full, norm_pre_mix, w_int, b_in, tabs,
        comm=_Gather([w_attn_proj[0].astype(BF16), w_conv_proj[0].astype(BF16), w_out[0].astype(BF16)]))
    (ao,), (w_upt,) = _attn_fwd(q, kv, attn_sinks, comm=_Gather([w_up[0].T.astype(BF16)]))
    (c0,), (wd,) = _conv31_fwd(ag, cdw32, conv_dw_b, comm=_Gather([w_down[0].astype(BF16)]))
    c1, attn, conv, merged, mix, h1, n2 = _mixer_fwd(ao, c0, gates, h0p, wa, wc, wo, vecs)
    u0 = _mm_nt(n2, w_upt, "ffn_up")
    act, dact_dv, dact_dg = _ffn_act(u0, fdw, ffn_dw_b)
    dffn, dact, dy, acc_f = _ffn_down_loss(act, wd, h1, loss_target[0], norm_post_ffn)

    (dug, duv, gfw_g, gfw_v, gfb_g, gfb_v, g_wd), _ = _ffn_act_bwd(u0, dact, dact_dg, dact_dv, fdw, act, dffn)
    g_wupt, (s_wd0,) = _mm_tn([dug, duv], n2, "grad_w_up", comm=_Scatter([g_wd], 0, 2))
    (dh1, acc_u), (s_wd1,) = _ffn_in_bwd(dug, duv, w_upt, h1, dy, norm_pre_ffn, comm=_Scatter([g_wd], 1, 2))
    (dmix, dat, dcv, dgt, dao, dc0, acc_m), (s_wup0,) = _mixer_bwd(
        dh1, mix, attn, conv, gates, c0, wa, wc, wo, vecs, comm=_Scatter([g_wupt], 0, 4))
    (da, dg, g_cdw, g_cdb, g_wo, g_wa, g_wc), (s_wup1, s_wup2, s_wup3) = _conv31_bwd(
        ag, dc0, cdw32, [(merged, dmix), (ao, dat), (c1, dcv)],
        comm=_Both(_Both(_Scatter([g_wupt], 1, 4), _Scatter([g_wupt], 2, 4)), _Scatter([g_wupt], 3, 4)))
    (dqkv, dsink), (s_wa, s_wc, s_wo) = _attn_bwd(q, kv, dao, attn_sinks, tabs, comm=_Scatter([g_wa, g_wc, g_wo]))
    loss_row = jnp.sum(acc_f[1:2, :], axis=1, keepdims=True)
    early = [loss_row, acc_m[0:1], dsink[0:1, 0:16], g_cdw[0:CONV_K], g_cdb,
             acc_m[2:3], acc_m[3:4], acc_m[1:2], acc_u[0:1], acc_f[0:1],
             jnp.concatenate([gfw_g, gfw_v], axis=1), jnp.concatenate([gfb_g, gfb_v], axis=1)]
    (g_wint, g_bin), (gathered_early,) = _mm_tn([dqkv, da, dg, dgt], n1, "grad_w_in", col_sums=True,
                                                comm=_Gather([_flat_pack(early, 64)]))
    (from_sibling,) = _exchange(_SiblingSwap(g_wint), "swap_w_in")
    (grad_x2d, dmeta, acc_i), (s_win,) = _in_bwd(dqkv, da, dg, dgt, w_int, h0p, dh1, norm_pre_mix,
                                                 comm=_ChipScatter(_pair_add(g_wint, from_sibling)))

    big = []
    for nm, parts, nslots, w, m, v, tr in (
            ("w_in", [s_win], N_CHIP, w_in, m_w_in, v_w_in, True), ("w_up", [s_wup0, s_wup1, s_wup2, s_wup3], N_DEV, w_up, m_w_up, v_w_up, True),
            ("w_attn_proj", [s_wa], N_DEV, w_attn_proj, m_w_attn_proj, v_w_attn_proj, False),
            ("w_conv_proj", [s_wc], N_DEV, w_conv_proj, m_w_conv_proj, v_w_conv_proj, False),
            ("w_out", [s_wo], N_DEV, w_out, m_w_out, v_w_out, False),
            ("w_down", [s_wd0, s_wd1], N_DEV, w_down, m_w_down, v_w_down, False)):
        ins = [a[0].T if tr else a[0] for a in (w, m, v)]
        big.append(tuple((o.T if tr else o)[None] for o in _sum_adamw(parts, *ins, "update_" + nm, nslots)))

    late = [dmeta, acc_i[0:1], g_bin]
    (gathered_late,) = _exchange(_Gather([_flat_pack(late, 24)]), "gather_small_grads")
    g_meta, g_npm, g_bi = _flat_unpack(_sum_slots(gathered_late, "sum_late_grads"), [a.shape for a in late])
    tot = _flat_unpack(_sum_slots(gathered_early, "sum_small_grads"), [a.shape for a in early])
    (loss, g_nqm, g_sk, g_cw, g_cb, g_lg, g_lb, g_bc, g_npf, g_nqf, g_fw, g_fb) = tot
    loss = loss.reshape(())
    g_meta = lax.dynamic_slice_in_dim(g_meta, me * BLK, BLK, axis=1)
    g_cw = lax.dynamic_slice_in_dim(g_cw, me * BLK, BLK, axis=1)[None]
    g_fw = lax.dynamic_slice_in_dim(g_fw, me * up_cols, up_cols, axis=1)[None]

    sm_w = [meta_tokens, norm_pre_mix, norm_post_mix, b_in, attn_sinks, conv_dw_w, conv_dw_b, conv_ln_g, conv_ln_b,
            b_conv_proj, norm_pre_ffn, norm_post_ffn, ffn_dw_w, ffn_dw_b]
    sm_g = [g_meta, g_npm, g_nqm, g_bi, g_sk, g_cw, g_cb, g_lg, g_lb, g_bc, g_npf, g_nqf, g_fw, g_fb]
    sm_m = [m_meta_tokens, m_norm_pre_mix, m_norm_post_mix, m_b_in, m_attn_sinks, m_conv_dw_w, m_conv_dw_b, m_conv_ln_g,
            m_conv_ln_b, m_b_conv_proj, m_norm_pre_ffn, m_norm_post_ffn, m_ffn_dw_w, m_ffn_dw_b]
    sm_v = [v_meta_tokens, v_norm_pre_mix, v_norm_post_mix, v_b_in, v_attn_sinks, v_conv_dw_w, v_conv_dw_b, v_conv_ln_g,
            v_conv_ln_b, v_b_conv_proj, v_norm_pre_ffn, v_norm_post_ffn, v_ffn_dw_w, v_ffn_dw_b]
    sm_shapes = [a.shape for a in sm_w]
    upd_rows = 32
    v_pack = _flat_pack(sm_v, upd_rows)
    sm_out = _adamw(_flat_pack(sm_w, upd_rows), _flat_pack(sm_g, upd_rows), _flat_pack(sm_m, upd_rows), v_pack, "adamw_small")
    sm_d, sm_nm, sm_nv = (_flat_unpack(o, sm_shapes) for o in sm_out)

    order = ["meta_tokens", "norm_pre_mix", "norm_post_mix", "w_in", "b_in", "attn_sinks", "w_attn_proj", "conv_dw_w",
             "conv_dw_b", "conv_ln_g", "conv_ln_b", "w_conv_proj", "b_conv_proj", "w_out", "norm_pre_ffn", "norm_post_ffn",
             "w_up", "ffn_dw_w", "ffn_dw_b", "w_down"]
    small_names = ["meta_tokens", "norm_pre_mix", "norm_post_mix", "b_in", "attn_sinks", "conv_dw_w", "conv_dw_b", "conv_ln_g",
                   "conv_ln_b", "b_conv_proj", "norm_pre_ffn", "norm_post_ffn", "ffn_dw_w", "ffn_dw_b"]
    big_names = ["w_in", "w_up", "w_attn_proj", "w_conv_proj", "w_out", "w_down"]
    table = {}
    for k, nm in enumerate(small_names):
        table[nm] = (sm_g[k], sm_d[k], sm_nm[k], sm_nv[k])
    for k, nm in enumerate(big_names):
        table[nm] = big[k]
    grad_x = grad_x2d[None]
    outs = [loss, grad_x]
    for field in range(4):
        outs += [table[nm][field] for nm in order]
    return tuple(outs)
```

```python
import functools

import jax
import jax.numpy as jnp
from jax import lax
from jax.experimental import pallas as pl
from jax.experimental.pallas import tpu as pltpu

F32 = jnp.float32
BF16 = jnp.bfloat16
MESH = pl.DeviceIdType.MESH

D = 1024
HEAD_DIM = 64
N_META = 16
BLK = 128
PAD = BLK - N_META
CONV_K = 31
FFN = 2816
FFN_K = 3
QKV_W = 1280
IN_W = 5376
ROT_DIM = 16
ROPE_THETA = 500000.0
RMS_EPS = 1e-6
LN_EPS = 1e-5
NEG_INF = -1e30
SCALE = HEAD_DIM ** -0.5
N_DEV = 8

ADAM_LR = 0.001
ADAM_B1 = 0.9
ADAM_B2 = 0.999
ADAM_EPS = 1e-08
ADAM_WD = 0.01
ADAM_STEP = 10

VMEM_BYTES_V7X = 64 * 1024 * 1024
VMEM_LIMIT = VMEM_BYTES_V7X - 8 * 1024 * 1024

NT = (((1,), (1,)), ((), ()))
TN = (((0,), (0,)), ((), ()))
VM = pl.BlockSpec(memory_space=pltpu.VMEM)
ANY = pl.BlockSpec(memory_space=pl.ANY)


def _cparams(*sem):
    return pltpu.CompilerParams(dimension_semantics=sem or None, vmem_limit_bytes=VMEM_LIMIT)


def _row_tile(p):
    return 384 if p % 384 == 0 else 128


def _dot(a, b):
    return jnp.dot(a, b, preferred_element_type=F32)


def _dot_nt(a, b):
    return lax.dot_general(a, b, NT, preferred_element_type=F32)


def _dot_tn(a, b):
    return lax.dot_general(a, b, TN, preferred_element_type=F32)


def _rms(x, g):
    return x * lax.rsqrt(jnp.mean(x * x, axis=-1, keepdims=True) + RMS_EPS) * g


def _lnsilu(x, g, b):
    mu = jnp.mean(x, axis=-1, keepdims=True)
    var = jnp.mean(jnp.square(x - mu), axis=-1, keepdims=True)
    z = (x - mu) * lax.rsqrt(var + LN_EPS) * g + b
    return z * jax.nn.sigmoid(z)


def _rms_bwd(x, g, dy):
    r = lax.rsqrt(jnp.mean(x * x, axis=-1, keepdims=True) + RMS_EPS)
    xn = x * r
    u = dy * g
    dg = jnp.sum(dy * xn, axis=0, keepdims=True)
    dx = r * (u - xn * jnp.mean(u * xn, axis=-1, keepdims=True))
    return dx, dg


def _lnsilu_bwd(x, g, b, dout):
    mu = jnp.mean(x, axis=-1, keepdims=True)
    xc = x - mu
    rs = lax.rsqrt(jnp.mean(xc * xc, axis=-1, keepdims=True) + LN_EPS)
    yh = xc * rs
    z = yh * g + b
    sg = jax.nn.sigmoid(z)
    dz = dout * (sg * (1.0 + z * (1.0 - sg)))
    dg = jnp.sum(dz * yh, axis=0, keepdims=True)
    db = jnp.sum(dz, axis=0, keepdims=True)
    dyh = dz * g
    dx = rs * (dyh - jnp.mean(dyh, axis=-1, keepdims=True) - yh * jnp.mean(dyh * yh, axis=-1, keepdims=True))
    return dx, dg, db


def _rope(v, c, s1, s2):
    return v * c + pltpu.roll(v, BLK - 8, 1) * s1 + pltpu.roll(v, 8, 1) * s2


def _rows(i, tm):
    return i * tm + lax.broadcasted_iota(jnp.int32, (tm, 1), 0)


def _place():
    return lax.axis_index("x"), lax.axis_index("y"), lax.axis_index("c")


def _blk(ref, idx, r, dtype):
    return ref.at[pl.ds(pl.multiple_of(idx * r, 16 if dtype == BF16 else 8), r), :]


class _Gather:
    def __init__(self, arrs):
        self.ins = list(arrs)
        n = len(arrs)
        self.out_shape = [jax.ShapeDtypeStruct((N_DEV * a.shape[0], a.shape[1]), a.dtype) for a in arrs]
        self.scratch = [pltpu.SemaphoreType.DMA((n, 7)), pltpu.SemaphoreType.DMA((n, 7)), pltpu.SemaphoreType.DMA((n,))]

    def _parts(self, ins, outs, sems):
        send_sems, recv_sems, local_sems = sems
        n = len(ins)
        x, y, c = _place()
        me, sibling = (x, y, c), (x, y, 1 - c)
        chips = [(1 - x, y), (x, 1 - y), (1 - x, 1 - y)]

        def rows(a, p):
            return _blk(outs[a], 4 * p[0] + 2 * p[1] + p[2], self.ins[a].shape[0], self.ins[a].dtype)

        def copy(a, k, block, to, src=None):
            return pltpu.make_async_remote_copy(
                src_ref=rows(a, block) if src is None else src, dst_ref=rows(a, block),
                send_sem=send_sems.at[a, k], recv_sem=recv_sems.at[a, k], device_id=to, device_id_type=MESH)

        mine = [pltpu.make_async_copy(ins[a], rows(a, me), local_sems.at[a]) for a in range(n)]
        first = []
        for a in range(n):
            first.append(copy(a, 0, me, sibling, src=ins[a]))
            first += [copy(a, 1 + j, me, (*chip, c), src=ins[a]) for j, chip in enumerate(chips)]
        return n, c, me, sibling, chips, copy, mine, first

    def start(self, ins, outs, sems):
        *_, mine, first = self._parts(ins, outs, sems)
        for cp in mine + first:
            cp.start()

    def finish(self, ins, outs, sems):
        n, c, me, sibling, chips, copy, mine, first = self._parts(ins, outs, sems)
        passed = []
        for j, chip in enumerate(chips):
            for a in range(n):
                copy(a, 1 + j, (*chip, c), me).wait_recv()
                fwd = copy(a, 4 + j, (*chip, c), sibling)
                fwd.start()
                passed.append(fwd)
        for a in range(n):
            copy(a, 0, sibling, me).wait_recv()
            for j, chip in enumerate(chips):
                copy(a, 4 + j, (*chip, 1 - c), me).wait_recv()
        for cp in first + passed:
            cp.wait_send()
        for cp in mine:
            cp.wait()


class _GatherRelay:
    N_COPY = 13

    def __init__(self, arr):
        self.ins = [arr]
        self.r = arr.shape[0]
        self.out_shape = [jax.ShapeDtypeStruct((N_DEV * self.r, arr.shape[1]), arr.dtype)]
        self.scratch = [pltpu.SemaphoreType.DMA((self.N_COPY,)), pltpu.SemaphoreType.DMA((self.N_COPY,)),
                        pltpu.SemaphoreType.DMA]

    def _parts(self, ins, outs, sems):
        send_sems, recv_sems, local_sem = sems
        x, y, c = _place()
        r, half = self.r, self.r // 2
        out = outs[0]
        me, sib, xn, yn, dg = (x, y, c), (x, y, 1 - c), (1 - x, y, c), (x, 1 - y, c), (1 - x, 1 - y, c)
        sx, sy, sd = (1 - x, y, 1 - c), (x, 1 - y, 1 - c), (1 - x, 1 - y, 1 - c)
        lo, hi = (0, half), (half, half)

        def rows(p, part=(0, r)):
            return out.at[pl.ds(pl.multiple_of((4 * p[0] + 2 * p[1] + p[2]) * r + part[0], 16), part[1]), :]

        def own(part):
            return ins[0].at[pl.ds(part[0], part[1]), :]

        def copy(k, dev_rows, to, src=None):
            return pltpu.make_async_remote_copy(
                src_ref=dev_rows if src is None else src, dst_ref=dev_rows,
                send_sem=send_sems.at[k], recv_sem=recv_sems.at[k], device_id=to, device_id_type=MESH)

        mine = pltpu.make_async_copy(ins[0], rows(me), local_sem)
        first = [copy(0, rows(me), sib, src=ins[0]),
                 copy(1, rows(me, lo), xn, src=own(lo)), copy(3, rows(me, hi), yn, src=own(hi)),
                 copy(2, rows(me, hi), xn, src=own(hi)), copy(4, rows(me, lo), yn, src=own(lo))]
        arrive = {0: rows(sib), 1: rows(xn, lo), 2: rows(xn, hi), 3: rows(yn, hi), 4: rows(yn, lo),
                  5: rows(dg, lo), 6: rows(dg, hi), 7: rows(sx, lo), 8: rows(sx, hi), 9: rows(sy, hi),
                  10: rows(sy, lo), 11: rows(sd, lo), 12: rows(sd, hi)}
        relay = {1: [(5, rows(xn, lo), yn), (7, rows(xn, lo), sib)], 3: [(6, rows(yn, hi), xn), (9, rows(yn, hi), sib)],
                 2: [(8, rows(xn, hi), sib)], 4: [(10, rows(yn, lo), sib)],
                 5: [(11, rows(dg, lo), sib)], 6: [(12, rows(dg, hi), sib)]}
        return copy, mine, first, arrive, relay, me

    def start(self, ins, outs, sems):
        _, mine, first, _, _, _ = self._parts(ins, outs, sems)
        for cp in [mine] + first:
            cp.start()

    def finish(self, ins, outs, sems):
        copy, mine, first, arrive, relay, me = self._parts(ins, outs, sems)
        passed = []
        for k in (1, 3, 2, 4, 5, 6):
            copy(k, arrive[k], me).wait_recv()
            for k2, dev_rows, to in relay[k]:
                fwd = copy(k2, dev_rows, to)
                fwd.start()
                passed.append(fwd)
        for k in (0, 7, 8, 9, 10, 11, 12):
            copy(k, arrive[k], me).wait_recv()
        for cp in first + passed:
            cp.wait_send()
        mine.wait()


FLIPS = [(0, 0, 1), (1, 0, 0), (0, 1, 0), (1, 1, 0), (1, 0, 1), (0, 1, 1), (1, 1, 1)]


class _Scatter:
    def __init__(self, arrs, part=0, nparts=1):
        self.ins = list(arrs)
        self.part, self.nparts = part, nparts
        n = len(arrs)
        self.out_shape = [jax.ShapeDtypeStruct((a.shape[0] // nparts, a.shape[1]), a.dtype) for a in arrs]
        self.scratch = [pltpu.SemaphoreType.DMA((n, 7)), pltpu.SemaphoreType.DMA((n, 7)), pltpu.SemaphoreType.DMA((n,))]

    def _parts(self, ins, outs, sems):
        send_sems, recv_sems, local_sems = sems
        n = len(ins)
        x, y, c = _place()
        me = 4 * x + 2 * y + c

        def flip(v, f):
            return 1 - v if f else v

        def src(a, idx):
            r = self.ins[a].shape[0] // N_DEV
            rs = r // self.nparts
            return ins[a].at[pl.ds(pl.multiple_of(idx * r + self.part * rs, 16), rs), :]

        def dst(a, idx):
            rs = self.ins[a].shape[0] // N_DEV // self.nparts
            return outs[a].at[pl.ds(pl.multiple_of(idx * rs, 16), rs), :]

        mine = [pltpu.make_async_copy(src(a, me), dst(a, me), local_sems.at[a]) for a in range(n)]
        sends, recvs = [], []
        for k, f in enumerate(FLIPS):
            peer = (flip(x, f[0]), flip(y, f[1]), flip(c, f[2]))
            pidx = 4 * peer[0] + 2 * peer[1] + peer[2]
            for a in range(n):
                sends.append(pltpu.make_async_remote_copy(
                    src_ref=src(a, pidx), dst_ref=dst(a, me),
                    send_sem=send_sems.at[a, k], recv_sem=recv_sems.at[a, k], device_id=peer, device_id_type=MESH))
                recvs.append(functools.partial(
                    pltpu.make_async_remote_copy,
                    src_ref=src(a, pidx), dst_ref=dst(a, pidx),
                    send_sem=send_sems.at[a, k], recv_sem=recv_sems.at[a, k], device_id=peer, device_id_type=MESH))
        return mine, sends, recvs

    def start(self, ins, outs, sems):
        mine, sends, _ = self._parts(ins, outs, sems)
        for cp in mine + sends:
            cp.start()

    def finish(self, ins, outs, sems):
        mine, sends, recvs = self._parts(ins, outs, sems)
        for make in recvs:
            make().wait_recv()
        for cp in sends:
            cp.wait_send()
        for cp in mine:
            cp.wait()


N_CHIP = 4


class _SiblingSwap:
    def __init__(self, arr):
        self.ins = [arr]
        self.r = arr.shape[0] // N_DEV
        self.out_shape = [jax.ShapeDtypeStruct((N_CHIP * self.r, arr.shape[1]), arr.dtype)]
        self.scratch = [pltpu.SemaphoreType.DMA((N_CHIP,)), pltpu.SemaphoreType.DMA((N_CHIP,))]

    def _copies(self, ins, outs, sems):
        send_sems, recv_sems = sems
        x, y, c = _place()
        r = self.r
        return [pltpu.make_async_remote_copy(
            src_ref=ins[0].at[pl.ds(pl.multiple_of((2 * j + 1 - c) * r, 16), r), :],
            dst_ref=outs[0].at[pl.ds(j * r, r), :],
            send_sem=send_sems.at[j], recv_sem=recv_sems.at[j], device_id=(x, y, 1 - c), device_id_type=MESH)
            for j in range(N_CHIP)]

    def start(self, ins, outs, sems):
        for cp in self._copies(ins, outs, sems):
            cp.start()

    def finish(self, ins, outs, sems):
        for cp in self._copies(ins, outs, sems):
            cp.wait()


class _ChipScatter:
    def __init__(self, arr):
        self.ins = [arr]
        self.r = arr.shape[0] // N_CHIP
        self.out_shape = [jax.ShapeDtypeStruct(arr.shape, arr.dtype)]
        self.scratch = [pltpu.SemaphoreType.DMA((3,)), pltpu.SemaphoreType.DMA((3,)), pltpu.SemaphoreType.DMA]

    def _parts(self, ins, outs, sems):
        send_sems, recv_sems, local_sem = sems
        x, y, c = _place()
        r = self.r
        my_chip = 2 * x + y

        def rows(ref, j):
            return ref.at[pl.ds(pl.multiple_of(j * r, 16), r), :]

        mine = pltpu.make_async_copy(rows(ins[0], my_chip), rows(outs[0], my_chip), local_sem)
        sends, recvs = [], []
        for k, (fx, fy) in enumerate(((1, 0), (0, 1), (1, 1))):
            px, py = (1 - x if fx else x), (1 - y if fy else y)
            peer_chip = 2 * px + py
            sends.append(pltpu.make_async_remote_copy(
                src_ref=rows(ins[0], peer_chip), dst_ref=rows(outs[0], my_chip),
                send_sem=send_sems.at[k], recv_sem=recv_sems.at[k], device_id=(px, py, c), device_id_type=MESH))
            recvs.append(functools.partial(
                pltpu.make_async_remote_copy,
                src_ref=rows(ins[0], peer_chip), dst_ref=rows(outs[0], peer_chip),
                send_sem=send_sems.at[k], recv_sem=recv_sems.at[k], device_id=(px, py, c), device_id_type=MESH))
        return mine, sends, recvs

    def start(self, ins, outs, sems):
        mine, sends, _ = self._parts(ins, outs, sems)
        for cp in [mine] + sends:
            cp.start()

    def finish(self, ins, outs, sems):
        mine, sends, recvs = self._parts(ins, outs, sems)
        for make in recvs:
            make().wait_recv()
        for cp in sends:
            cp.wait_send()
        mine.wait()


def _pair_add(partial, recv):
    r = recv.shape[0] // N_CHIP
    cols = recv.shape[1]
    tr = r // 2 if (r // 2) % 16 == 0 else r
    steps = r // tr
    core = lax.axis_index("c").astype(jnp.int32).reshape(1)

    def body(c_ref, p_ref, s_ref, o_ref):
        o_ref[...] = (p_ref[...].astype(F32) + s_ref[...].astype(F32)).astype(BF16)

    spec = pl.BlockSpec((tr, cols), lambda j, i, c_ref: (j * steps + i, 0))
    return pl.pallas_call(
        body, name="pair_add",
        grid_spec=pltpu.PrefetchScalarGridSpec(
            num_scalar_prefetch=1, grid=(N_CHIP, steps),
            in_specs=[pl.BlockSpec((tr, cols), lambda j, i, c_ref: ((2 * j + c_ref[0]) * steps + i, 0)), spec],
            out_specs=spec),
        out_shape=jax.ShapeDtypeStruct(recv.shape, BF16),
        compiler_params=_cparams("parallel", "parallel"),
    )(core, partial, recv)


class _Both:
    def __init__(self, a, b):
        self.a, self.b = a, b
        self.ins = a.ins + b.ins
        self.out_shape = a.out_shape + b.out_shape
        self.scratch = a.scratch + b.scratch

    def _split(self, ins, outs, sems):
        ni, no, ns = len(self.a.ins), len(self.a.out_shape), len(self.a.scratch)
        return (ins[:ni], outs[:no], sems[:ns]), (ins[ni:], outs[no:], sems[ns:])

    def start(self, ins, outs, sems):
        ra, rb = self._split(ins, outs, sems)
        self.a.start(*ra)
        self.b.start(*rb)

    def finish(self, ins, outs, sems):
        ra, rb = self._split(ins, outs, sems)
        self.a.finish(*ra)
        self.b.finish(*rb)


def _exchange(comm, name):
    n, m = len(comm.ins), len(comm.out_shape)

    def body(*refs):
        ins, outs, sems = refs[:n], refs[n:n + m], refs[n + m:]
        comm.start(ins, outs, sems)
        comm.finish(ins, outs, sems)

    return pl.pallas_call(
        body, name=name, out_shape=comm.out_shape, in_specs=[ANY] * n, out_specs=[ANY] * m, scratch_shapes=comm.scratch,
    )(*comm.ins)


def _call(body, *, name, grid, in_specs, out_specs, out_shape, args, scratch=(), sem="parallel", comm=None):
    if comm is None:
        outs = pl.pallas_call(
            body, name=name, grid=grid, in_specs=list(in_specs), out_specs=list(out_specs), out_shape=list(out_shape),
            scratch_shapes=list(scratch), compiler_params=_cparams(sem))(*args)
        return outs, []
    n_in, n_out, n_sc = len(in_specs), len(out_specs), len(scratch)
    n_ci, n_co = len(comm.ins), len(comm.out_shape)
    last = grid[0] - 1

    def fused(*refs):
        ins, refs = refs[:n_in], refs[n_in:]
        c_ins, refs = refs[:n_ci], refs[n_ci:]
        outs, refs = refs[:n_out], refs[n_out:]
        c_outs, refs = refs[:n_co], refs[n_co:]
        sc, c_sems = refs[:n_sc], refs[n_sc:]
        step = pl.program_id(0)

        @pl.when(step == 0)
        def _():
            comm.start(c_ins, c_outs, c_sems)

        body(*ins, *outs, *sc)

        @pl.when(step == last)
        def _():
            comm.finish(c_ins, c_outs, c_sems)

    outs = pl.pallas_call(
        fused, name=name, grid=grid, in_specs=list(in_specs) + [ANY] * n_ci, out_specs=list(out_specs) + [ANY] * n_co,
        out_shape=list(out_shape) + comm.out_shape, scratch_shapes=list(scratch) + comm.scratch,
        compiler_params=_cparams("arbitrary"))(*args, *comm.ins)
    return outs[:n_out], outs[n_out:]


def _token_specs(tm):
    k = tm // BLK
    return [pl.BlockSpec((BLK, D), functools.partial(lambda i, t: (jnp.maximum(k * i + t - 1, 0), 0), t=t)) for t in range(k)]


def _in_proj(x2d, meta, gain, w_int, b_in, tabs, comm=None):
    p = x2d.shape[0] + BLK
    tm = _row_tile(p)
    k = tm // BLK

    def body(*refs):
        x_refs = refs[:k]
        m_ref, g_ref, w_ref, b_ref, t_ref, h_ref, n1_ref, q_ref, kv_ref, ag_ref, gt_ref = refs[k:]
        i = pl.program_id(0)
        head = jnp.concatenate([jnp.zeros((PAD, D), F32), m_ref[...]], axis=0)
        first = jnp.where(i == 0, head, x_refs[0][...])
        h = jnp.concatenate([first] + [r[...] for r in x_refs[1:]], axis=0) if k > 1 else first
        h_ref[...] = h
        n = _rms(h, g_ref[...]).astype(BF16)
        n1_ref[...] = n
        c, s1, s2 = t_ref[:, 0:128], t_ref[:, 128:256], t_ref[:, 256:384]

        def mm(c0, w):
            return _dot_nt(n, w_ref[c0:c0 + w, :]) + b_ref[:, c0:c0 + w]

        for j in range(4):
            acc = mm(256 * j, 256)
            for t in range(2):
                lo = 256 * j + 128 * t
                q_ref[:, lo:lo + 128] = (_rope(acc[:, 128 * t:128 * (t + 1)], c, s1, s2) * SCALE).astype(BF16)
        acc = mm(1024, 256)
        kv_ref[:, 0:128] = _rope(acc[:, 0:128], c, s1, s2).astype(BF16)
        kv_ref[:, 128:256] = acc[:, 128:256].astype(BF16)
        for j in range(8):
            ag_ref[:, 256 * j:256 * (j + 1)] = mm(QKV_W + 256 * j, 256).astype(BF16)
        for j in range(8):
            gt_ref[:, 256 * j:256 * (j + 1)] = mm(QKV_W + 2048 + 256 * j, 256).astype(BF16)

    def row(w):
        return pl.BlockSpec((tm, w), lambda i: (i, 0))

    return _call(
        body, name="in_proj", grid=(p // tm,),
        in_specs=_token_specs(tm) + [VM, VM, VM, VM, row(384)],
        out_specs=[row(D), row(D), row(D), row(256), row(2048), row(2048)],
        out_shape=[jax.ShapeDtypeStruct((p, D), F32)] + [jax.ShapeDtypeStruct((p, w), BF16) for w in (D, D, 256, 2048, 2048)],
        args=(x2d,) * k + (meta, gain, w_int, b_in, tabs), comm=comm)


N_KEY = 2 * BLK + N_META


def _attn_setup(n, h, q_ref, km_ref, kp_ref, kc_ref):
    lo = lax.broadcasted_iota(jnp.int32, (BLK, BLK), 1) < HEAD_DIM
    lok = lax.broadcasted_iota(jnp.int32, (N_KEY, BLK), 1) < HEAD_DIM

    def dup(lanes):
        cat = jnp.concatenate([kp_ref[:, lanes], kc_ref[:, lanes], km_ref[PAD:BLK, lanes]], axis=0).astype(F32)
        rolled = pltpu.roll(cat, HEAD_DIM, 1)
        return (jnp.where(lok, cat, rolled) if h == 0 else jnp.where(lok, rolled, cat)).astype(BF16)

    k2 = dup(slice(0, 128))
    v2 = dup(slice(128, 256))
    qs = _stack_heads(q_ref, h, lo)

    kr = lax.broadcasted_iota(jnp.int32, (BLK, BLK), 0)
    tq = BLK * n + lax.broadcasted_iota(jnp.int32, (BLK, BLK), 1) - PAD
    t_p = BLK * (n - 1) + kr - PAD
    t_c = BLK * n + kr - PAD
    ok_p = jnp.logical_and(t_p >= N_META, tq - t_p < BLK)
    ok_c = jnp.logical_and(t_c >= N_META, t_c <= tq)
    ok_m = lax.broadcasted_iota(jnp.int32, (N_META, BLK), 0) <= BLK * n + lax.broadcasted_iota(jnp.int32, (N_META, BLK), 1) - PAD
    bias = jnp.concatenate([jnp.where(ok, 0.0, NEG_INF).astype(F32) for ok in (ok_p, ok_c, ok_m)], axis=0)
    return qs, k2, v2, bias, lok


def _attn_head(s, bias, sink):
    s = s + bias
    m = jnp.maximum(jnp.max(s, axis=0, keepdims=True), sink)
    e = jnp.exp(s - m)
    es = jnp.exp(sink - m)
    inv = 1.0 / (jnp.sum(e, axis=0, keepdims=True) + es)
    return e * inv, es * inv


def _stack_heads(ref, h, lo):
    pieces = []
    for jp in range(4):
        v = ref[:, BLK * (4 * h + jp):BLK * (4 * h + jp + 1)]
        zero = jnp.zeros_like(v)
        pieces += [jnp.where(lo, v, zero), jnp.where(lo, zero, v)]
    return jnp.concatenate(pieces, axis=0)


def _unstack_heads(v, jp, lo):
    return jnp.where(lo, v[256 * jp:256 * jp + 128], v[256 * jp + 128:256 * jp + 256])


def _attn_fwd(q, kv, sinks, comm=None):
    p = q.shape[0]
    nb = p // BLK

    def body(q_ref, km_ref, kp_ref, kc_ref, sink_ref, o_ref):
        n = pl.program_id(0)
        lo = lax.broadcasted_iota(jnp.int32, (BLK, BLK), 1) < HEAD_DIM
        for h in range(2):
            qs, k2, v2, bias, _ = _attn_setup(n, h, q_ref, km_ref, kp_ref, kc_ref)
            st = _dot_nt(k2, qs)
            pt = jnp.concatenate(
                [_attn_head(st[:, BLK * g:BLK * (g + 1)], bias, sink_ref[0, 8 * h + g])[0].astype(BF16) for g in range(8)],
                axis=1)
            o = _dot_tn(pt, v2)
            for jp in range(4):
                o_ref[:, BLK * (4 * h + jp):BLK * (4 * h + jp + 1)] = _unstack_heads(o, jp, lo).astype(BF16)

    return _call(
        body, name="attn_fwd", grid=(nb,),
        in_specs=[pl.BlockSpec((BLK, D), lambda i: (i, 0)),
                  pl.BlockSpec((BLK, 256), lambda i: (0, 0)),
                  pl.BlockSpec((BLK, 256), lambda i: (jnp.maximum(i - 1, 0), 0)),
                  pl.BlockSpec((BLK, 256), lambda i: (i, 0)),
                  pl.BlockSpec(memory_space=pltpu.SMEM)],
        out_specs=[pl.BlockSpec((BLK, D), lambda i: (i, 0))],
        out_shape=[jax.ShapeDtypeStruct((p, D), BF16)],
        args=(q, kv, kv, kv, sinks), comm=comm)


def _conv31_fwd(ag, w32, b, comm=None):
    p = ag.shape[0]
    nch = p // BLK

    def body(a_ref, g_ref, w_ref, b_ref, o_ref, gp):
        gp[0:32, :] = jnp.zeros((32, BLK), F32)
        for ci in range(nch):
            r0 = BLK * ci
            glu = a_ref[r0:r0 + BLK, :].astype(F32) * jax.nn.sigmoid(g_ref[r0:r0 + BLK, :].astype(F32))
            if ci == 0:
                glu = jnp.where(_rows(0, BLK) >= PAD, glu, 0.0)
            gp[32 + r0:32 + r0 + BLK, :] = glu
        for ci in range(nch):
            r0 = BLK * ci
            acc = jnp.broadcast_to(b_ref[...], (BLK, BLK))
            for j in range(CONV_K):
                acc = acc + w_ref[j:j + 1, :] * gp[r0 + j + 2:r0 + j + 2 + BLK, :]
            o_ref[r0:r0 + BLK, :] = acc

    return _call(
        body, name="conv31_fwd", grid=(D // BLK,),
        in_specs=[pl.BlockSpec((p, BLK), lambda j: (0, j)), pl.BlockSpec((p, BLK), lambda j: (0, 8 + j)),
                  pl.BlockSpec((32, BLK), lambda j: (0, j)), pl.BlockSpec((1, BLK), lambda j: (0, j))],
        out_specs=[pl.BlockSpec((p, BLK), lambda j: (0, j))],
        out_shape=[jax.ShapeDtypeStruct((p, D), F32)],
        scratch=[pltpu.VMEM((p + 32, BLK), F32)],
        args=(ag, ag, w32, b), comm=comm)


def _mixer_fwd(ao, c0, gates, h0p, wa, wc, wo, vecs):
    p = ao.shape[0]
    tm = _row_tile(p)

    def body(ao_ref, c0_ref, gt_ref, h_ref, wa_ref, wc_ref, wo_ref, v_ref,
             c1_ref, at_ref, cv_ref, mg_ref, mix_ref, h1_ref, n2_ref):
        i = pl.program_id(0)
        c1 = _lnsilu(c0_ref[...], v_ref[0:1, :], v_ref[1:2, :]).astype(BF16)
        c1_ref[...] = c1
        attn = _dot(ao_ref[...], wa_ref[...])
        conv = _dot(c1, wc_ref[...]) + v_ref[2:3, :]
        at_ref[...] = attn.astype(BF16)
        cv_ref[...] = conv.astype(BF16)
        merged = (jax.nn.sigmoid(gt_ref[:, 0:D].astype(F32)) * attn
                  + jax.nn.sigmoid(gt_ref[:, D:2 * D].astype(F32)) * conv).astype(BF16)
        mg_ref[...] = merged
        mix = _dot(merged, wo_ref[...])
        mix_ref[...] = mix
        h1 = jnp.where(_rows(i, tm) >= PAD, h_ref[...] + _rms(mix, v_ref[3:4, :]), 0.0)
        h1_ref[...] = h1
        n2_ref[...] = _rms(h1, v_ref[4:5, :]).astype(BF16)

    def row(w):
        return pl.BlockSpec((tm, w), lambda i: (i, 0))

    return pl.pallas_call(
        body, name="mixer_fwd", grid=(p // tm,),
        in_specs=[row(D), row(D), row(2 * D), row(D), VM, VM, VM, VM],
        out_specs=[row(D)] * 7,
        out_shape=[jax.ShapeDtypeStruct((p, D), t) for t in (BF16, BF16, BF16, BF16, F32, F32, BF16)],
        compiler_params=_cparams("parallel"),
    )(ao, c0, gates, h0p, wa, wc, wo, vecs)


def _mm_nt(a, w_t, name):
    p, k = a.shape
    n = w_t.shape[0]
    tm = _row_tile(p)
    ch = 512

    def body(a_ref, w_ref, o_ref):
        a_v = a_ref[...]
        for c0 in range(0, n, ch):
            o_ref[:, c0:c0 + ch] = _dot_nt(a_v, w_ref[c0:c0 + ch, :]).astype(BF16)

    return pl.pallas_call(
        body, name=name, grid=(p // tm,),
        in_specs=[pl.BlockSpec((tm, k), lambda i: (i, 0)), VM],
        out_specs=pl.BlockSpec((tm, n), lambda i: (i, 0)),
        out_shape=jax.ShapeDtypeStruct((p, n), BF16),
        compiler_params=_cparams("parallel"),
    )(a, w_t)


def _conv3(xp_ref, w_ref, r0):
    return (w_ref[0:1, :] * xp_ref[r0 + 6:r0 + 6 + BLK, :] + w_ref[1:2, :] * xp_ref[r0 + 7:r0 + 7 + BLK, :]
            + w_ref[2:3, :] * xp_ref[r0 + 8:r0 + 8 + BLK, :])


def _ffn_slab_specs(p):
    ncol = FFN // BLK
    return [pl.BlockSpec((p, BLK), lambda j: (0, j)), pl.BlockSpec((p, BLK), lambda j: (0, ncol + j)),
            pl.BlockSpec((FFN_K, BLK), lambda j: (0, j)), pl.BlockSpec((FFN_K, BLK), lambda j: (0, ncol + j)),
            pl.BlockSpec((1, BLK), lambda j: (0, j)), pl.BlockSpec((1, BLK), lambda j: (0, ncol + j))]


def _fill_shifted(dst, src_ref, nch):
    dst[0:8, :] = jnp.zeros((8, BLK), F32)
    for ci in range(nch):
        dst[8 + BLK * ci:8 + BLK * (ci + 1), :] = src_ref[BLK * ci:BLK * (ci + 1), :].astype(F32)


def _ffn_act(u0, fw, fb):
    p = u0.shape[0]
    nch = p // BLK

    def body(g_ref, v_ref, wg_ref, wv_ref, bg_ref, bv_ref, o_ref, dv_ref, dg_ref, xg, xv):
        _fill_shifted(xg, g_ref, nch)
        _fill_shifted(xv, v_ref, nch)
        for ci in range(nch):
            r0 = BLK * ci
            ug = _conv3(xg, wg_ref, r0) + bg_ref[...]
            uv = _conv3(xv, wv_ref, r0) + bv_ref[...]
            sg = jax.nn.sigmoid(ug)
            silu = ug * sg
            o_ref[r0:r0 + BLK, :] = (silu * uv).astype(BF16)
            dv_ref[r0:r0 + BLK, :] = silu.astype(BF16)
            dg_ref[r0:r0 + BLK, :] = (uv * (sg * (1.0 + ug * (1.0 - sg)))).astype(BF16)

    slab = pl.BlockSpec((p, BLK), lambda j: (0, j))
    return pl.pallas_call(
        body, name="ffn_act", grid=(FFN // BLK,),
        in_specs=_ffn_slab_specs(p),
        out_specs=[slab] * 3,
        out_shape=[jax.ShapeDtypeStruct((p, FFN), BF16)] * 3,
        scratch_shapes=[pltpu.VMEM((p + 8, BLK), F32)] * 2,
        compiler_params=_cparams("parallel"),
    )(u0, u0, fw, fw, fb, fb)


def _ffn_down_loss(act, wd, h1, tgt, gain):
    p = act.shape[0]
    tm = _row_tile(p)
    k = tm // BLK

    def body(*refs):
        a_ref, w_ref, h_ref = refs[:3]
        t_refs = refs[3:3 + k]
        g_ref, df_ref, da_ref, dy_ref, acc_ref = refs[3 + k:]
        i = pl.program_id(0)

        @pl.when(i == 0)
        def _():
            acc_ref[...] = jnp.zeros_like(acc_ref)

        ffn = _dot(a_ref[...], w_ref[...])
        t = jnp.concatenate([t_ref[...] for t_ref in t_refs], axis=0) if k > 1 else t_refs[0][...]
        diff = jnp.where(_rows(i, tm) >= BLK, h_ref[...] + _rms(ffn, g_ref[...]) - t, 0.0)
        dy = diff * (1.0 / D)
        dffn, dg = _rms_bwd(ffn, g_ref[...], dy)
        acc_ref[0:1, :] += dg
        acc_ref[1:2, :] += jnp.sum(diff * diff, axis=0, keepdims=True) * (0.5 / D)
        dy_ref[...] = dy
        dfb = dffn.astype(BF16)
        df_ref[...] = dfb
        for c0 in range(0, FFN, 256):
            da_ref[:, c0:c0 + 256] = _dot_nt(dfb, w_ref[c0:c0 + 256, :]).astype(BF16)

    def row(w):
        return pl.BlockSpec((tm, w), lambda i: (i, 0))

    return pl.pallas_call(
        body, name="ffn_down_loss", grid=(p // tm,),
        in_specs=[row(FFN), VM, row(D)] + _token_specs(tm) + [VM],
        out_specs=[row(D), row(FFN), row(D), pl.BlockSpec((8, D), lambda i: (0, 0))],
        out_shape=[jax.ShapeDtypeStruct((p, D), BF16), jax.ShapeDtypeStruct((p, FFN), BF16),
                   jax.ShapeDtypeStruct((p, D), F32), jax.ShapeDtypeStruct((8, D), F32)],
        compiler_params=_cparams("arbitrary"),
    )(act, wd, h1, *([tgt] * k), gain)


def _mm_tn(pieces, b, name, col_sums=False, comm=None):
    p, n = b.shape
    tk = 256
    nblk = [a.shape[1] // tk for a in pieces]
    offs = [sum(nblk[:q]) for q in range(len(pieces))]
    total = sum(nblk)
    npc = len(pieces)

    def body(*refs):
        a_refs, b_ref, o_ref = refs[:npc], refs[npc], refs[npc + 1]
        i = pl.program_id(0)
        for q, a_ref in enumerate(a_refs):
            @pl.when(jnp.logical_and(i >= offs[q], i < offs[q] + nblk[q]))
            def _(a_ref=a_ref):
                a_v = a_ref[...]
                o_ref[...] = _dot_tn(a_v, b_ref[...]).astype(BF16)
                if col_sums:
                    refs[npc + 2][...] = jnp.sum(a_v.astype(F32), axis=0, keepdims=True)

    def a_spec(q):
        return pl.BlockSpec((p, tk), lambda i: (0, jnp.clip(i - offs[q], 0, nblk[q] - 1)))

    out_specs = [pl.BlockSpec((tk, n), lambda i: (i, 0))]
    out_shape = [jax.ShapeDtypeStruct((total * tk, n), BF16)]
    if col_sums:
        out_specs.append(pl.BlockSpec((1, tk), lambda i: (0, i)))
        out_shape.append(jax.ShapeDtypeStruct((1, total * tk), F32))
    res, sent = _call(
        body, name=name, grid=(total,),
        in_specs=[a_spec(q) for q in range(npc)] + [VM],
        out_specs=out_specs, out_shape=out_shape, args=(*pieces, b), comm=comm)
    res = res if col_sums else res[0]
    return res if comm is None else (res, sent)


def _ffn_act_bwd(u0, dact, dact_dg, dact_dv, fw, act, dffn, comm=None):
    p = u0.shape[0]
    nch = p // BLK
    ncol = FFN // BLK

    def body(g_ref, v_ref, wg_ref, wv_ref, da_ref, lg_ref, lv_ref, act_ref, df_ref,
             dg_ref, dv_ref, gwg_ref, gwv_ref, gbg_ref, gbv_ref, gwd_ref, eg, ev):
        gwd_ref[...] = _dot_tn(act_ref[...], df_ref[...]).astype(BF16)
        eg[p:p + 8, :] = jnp.zeros((8, BLK), F32)
        ev[p:p + 8, :] = jnp.zeros((8, BLK), F32)
        for ci in range(nch):
            r0 = BLK * ci
            d = da_ref[r0:r0 + BLK, :].astype(F32)
            eg[r0:r0 + BLK, :] = d * lg_ref[r0:r0 + BLK, :].astype(F32)
            ev[r0:r0 + BLK, :] = d * lv_ref[r0:r0 + BLK, :].astype(F32)
        def fold(v):
            return jnp.sum(v.reshape(BLK // 8, 8, BLK), axis=0)

        for e_s, x_ref, w_ref, d_ref, gw_ref, gb_ref in ((eg, g_ref, wg_ref, dg_ref, gwg_ref, gbg_ref),
                                                        (ev, v_ref, wv_ref, dv_ref, gwv_ref, gbv_ref)):
            sums = [jnp.zeros((8, BLK), F32) for _ in range(FFN_K + 1)]
            for ci in range(nch):
                r0 = BLK * ci
                es = [e_s[r0 + t:r0 + t + BLK, :] for t in range(FFN_K)]
                du = w_ref[2:3, :] * es[0] + w_ref[1:2, :] * es[1] + w_ref[0:1, :] * es[2]
                if ci == 0:
                    du = jnp.where(_rows(0, BLK) >= PAD, du, 0.0)
                d_ref[r0:r0 + BLK, :] = du.astype(BF16)
                x = x_ref[r0:r0 + BLK, :].astype(F32)
                for j in range(FFN_K):
                    sums[j] = sums[j] + fold(es[FFN_K - 1 - j] * x)
                sums[FFN_K] = sums[FFN_K] + fold(es[0])
            for j in range(FFN_K):
                gw_ref[j:j + 1, :] = jnp.sum(sums[j], axis=0, keepdims=True)
            gb_ref[...] = jnp.sum(sums[FFN_K], axis=0, keepdims=True)

    slab = pl.BlockSpec((p, BLK), lambda j: (0, j))
    wspec = pl.BlockSpec((FFN_K, BLK), lambda j: (0, j))
    bspec = pl.BlockSpec((1, BLK), lambda j: (0, j))
    return _call(
        body, name="ffn_act_bwd", grid=(ncol,),
        in_specs=_ffn_slab_specs(p)[:4] + [slab] * 4 + [VM],
        out_specs=[slab, slab, wspec, wspec, bspec, bspec, pl.BlockSpec((BLK, D), lambda j: (j, 0))],
        out_shape=[jax.ShapeDtypeStruct((p, FFN), BF16)] * 2 + [jax.ShapeDtypeStruct((FFN_K, FFN), F32)] * 2
        + [jax.ShapeDtypeStruct((1, FFN), F32)] * 2 + [jax.ShapeDtypeStruct((FFN, D), BF16)],
        scratch=[pltpu.VMEM((p + 8, BLK), F32)] * 2,
        args=(u0, u0, fw, fw, dact, dact_dg, dact_dv, act, dffn), comm=comm)


def _ffn_in_bwd(dug, duv, w_upt, h1, dy, gain, comm=None):
    p = h1.shape[0]
    tm = _row_tile(p)

    def body(dg_ref, dv_ref, w_ref, h_ref, dy_ref, g_ref, o_ref, acc_ref):
        i = pl.program_id(0)

        @pl.when(i == 0)
        def _():
            acc_ref[...] = jnp.zeros_like(acc_ref)

        dn = _dot(dg_ref[...], w_ref[0:FFN, :]) + _dot(dv_ref[...], w_ref[FFN:2 * FFN, :])
        dh, dg = _rms_bwd(h_ref[...], g_ref[...], dn)
        o_ref[...] = dy_ref[...] + dh
        acc_ref[0:1, :] += dg

    def row(w):
        return pl.BlockSpec((tm, w), lambda i: (i, 0))

    return _call(
        body, name="ffn_in_bwd", grid=(p // tm,),
        in_specs=[row(FFN), row(FFN), VM, row(D), row(D), VM],
        out_specs=[row(D), pl.BlockSpec((8, D), lambda i: (0, 0))],
        out_shape=[jax.ShapeDtypeStruct((p, D), F32), jax.ShapeDtypeStruct((8, D), F32)],
        sem="arbitrary", args=(dug, duv, w_upt, h1, dy, gain), comm=comm)


def _mixer_bwd(dh1, mix, attn, conv, gates, c0, wa, wc, wo, vecs, comm=None):
    p = dh1.shape[0]
    tm = _row_tile(p)

    def body(dh_ref, mix_ref, at_ref, cv_ref, gt_ref, c0_ref, wa_ref, wc_ref, wo_ref, v_ref,
             dmix_ref, dat_ref, dcv_ref, dgt_ref, dao_ref, dc0_ref, acc_ref):
        i = pl.program_id(0)

        @pl.when(i == 0)
        def _():
            acc_ref[...] = jnp.zeros_like(acc_ref)

        dmix, dgp = _rms_bwd(mix_ref[...], v_ref[3:4, :], dh_ref[...])
        dmix = dmix.astype(BF16)
        dmix_ref[...] = dmix
        dmg = _dot_nt(dmix, wo_ref[...])
        sa = jax.nn.sigmoid(gt_ref[:, 0:D].astype(F32))
        sc = jax.nn.sigmoid(gt_ref[:, D:2 * D].astype(F32))
        dat = dmg * sa
        dcv = dmg * sc
        dgt_ref[:, 0:D] = (dmg * at_ref[...].astype(F32) * sa * (1.0 - sa)).astype(BF16)
        dgt_ref[:, D:2 * D] = (dmg * cv_ref[...].astype(F32) * sc * (1.0 - sc)).astype(BF16)
        datb = dat.astype(BF16)
        dcvb = dcv.astype(BF16)
        dat_ref[...] = datb
        dcv_ref[...] = dcvb
        dao_ref[...] = _dot_nt(datb, wa_ref[...]).astype(BF16)
        dc1 = _dot_nt(dcvb, wc_ref[...])
        dc0, dlg, dlb = _lnsilu_bwd(c0_ref[...], v_ref[0:1, :], v_ref[1:2, :], dc1)
        dc0_ref[...] = dc0
        acc_ref[0:1, :] += dgp
        acc_ref[1:2, :] += jnp.sum(dcv, axis=0, keepdims=True)
        acc_ref[2:3, :] += dlg
        acc_ref[3:4, :] += dlb

    def row(w):
        return pl.BlockSpec((tm, w), lambda i: (i, 0))

    return _call(
        body, name="mixer_bwd", grid=(p // tm,),
        in_specs=[row(D), row(D), row(D), row(D), row(2 * D), row(D), VM, VM, VM, VM],
        out_specs=[row(D), row(D), row(D), row(2 * D), row(D), row(D), pl.BlockSpec((8, D), lambda i: (0, 0))],
        out_shape=[jax.ShapeDtypeStruct((p, D), BF16)] * 3 + [jax.ShapeDtypeStruct((p, 2 * D), BF16),
                                                             jax.ShapeDtypeStruct((p, D), BF16),
                                                             jax.ShapeDtypeStruct((p, D), F32),
                                                             jax.ShapeDtypeStruct((8, D), F32)],
        sem="arbitrary", args=(dh1, mix, attn, conv, gates, c0, wa, wc, wo, vecs), comm=comm)


def _conv31_bwd(ag, dc0, w32, tn_pairs, comm=None):
    p = ag.shape[0]
    nch = p // BLK
    npair = len(tn_pairs)

    def body(*refs):
        a_ref, g_ref, dc_ref, w_ref = refs[:4]
        tn_a, tn_b = refs[4:4 + npair], refs[4 + npair:4 + 2 * npair]
        da_ref, dg_ref, gw_ref, gb_ref = refs[4 + 2 * npair:8 + 2 * npair]
        tn_o = refs[8 + 2 * npair:8 + 3 * npair]
        gp, dp = refs[8 + 3 * npair:]
        for ta, tb, to in zip(tn_a, tn_b, tn_o):
            to[...] = _dot_tn(ta[...], tb[...]).astype(BF16)
        gp[0:32, :] = jnp.zeros((32, BLK), F32)
        dp[p:p + 32, :] = jnp.zeros((32, BLK), F32)
        bsum = jnp.zeros((BLK, BLK), F32)
        for ci in range(nch):
            r0 = BLK * ci
            glu = a_ref[r0:r0 + BLK, :].astype(F32) * jax.nn.sigmoid(g_ref[r0:r0 + BLK, :].astype(F32))
            if ci == 0:
                glu = jnp.where(_rows(0, BLK) >= PAD, glu, 0.0)
            gp[32 + r0:32 + r0 + BLK, :] = glu
            d = dc_ref[r0:r0 + BLK, :]
            dp[r0:r0 + BLK, :] = d
            bsum = bsum + d
        gb_ref[...] = jnp.sum(bsum, axis=0, keepdims=True)
        for ci in range(nch):
            r0 = BLK * ci
            acc = jnp.zeros((BLK, BLK), F32)
            for j in range(CONV_K):
                acc = acc + w_ref[j:j + 1, :] * dp[r0 + 30 - j:r0 + 30 - j + BLK, :]
            if ci == 0:
                acc = jnp.where(_rows(0, BLK) >= PAD, acc, 0.0)
            a = a_ref[r0:r0 + BLK, :].astype(F32)
            sg = jax.nn.sigmoid(g_ref[r0:r0 + BLK, :].astype(F32))
            da_ref[r0:r0 + BLK, :] = (acc * sg).astype(BF16)
            dg_ref[r0:r0 + BLK, :] = (acc * a * sg * (1.0 - sg)).astype(BF16)
        sub = BLK // 2
        accs = [jnp.zeros((8, BLK), F32) for _ in range(CONV_K)]
        for r0 in range(0, p, sub):
            d = dp[r0:r0 + sub, :]
            for j in range(CONV_K):
                prod = d * gp[r0 + j + 2:r0 + j + 2 + sub, :]
                accs[j] = accs[j] + jnp.sum(prod.reshape(sub // 8, 8, BLK), axis=0)
        for j in range(CONV_K):
            gw_ref[j:j + 1, :] = jnp.sum(accs[j], axis=0, keepdims=True)
        gw_ref[CONV_K:32, :] = jnp.zeros((32 - CONV_K, BLK), F32)

    slab = pl.BlockSpec((p, BLK), lambda j: (0, j))
    return _call(
        body, name="conv31_bwd", grid=(D // BLK,),
        in_specs=[slab, pl.BlockSpec((p, BLK), lambda j: (0, 8 + j)), slab, pl.BlockSpec((32, BLK), lambda j: (0, j))]
        + [slab] * npair + [VM] * npair,
        out_specs=[slab, slab, pl.BlockSpec((32, BLK), lambda j: (0, j)), pl.BlockSpec((1, BLK), lambda j: (0, j))]
        + [pl.BlockSpec((BLK, D), lambda j: (j, 0))] * npair,
        out_shape=[jax.ShapeDtypeStruct((p, D), BF16)] * 2 + [jax.ShapeDtypeStruct((32, D), F32),
                                                             jax.ShapeDtypeStruct((1, D), F32)]
        + [jax.ShapeDtypeStruct((D, D), BF16)] * npair,
        scratch=[pltpu.VMEM((p + 32, BLK), F32)] * 2,
        args=(ag, ag, dc0, w32, *[a for a, _ in tn_pairs], *[b for _, b in tn_pairs]), comm=comm)


def _attn_bwd(q, kv, dao, sinks, tabs, comm=None):
    p = q.shape[0]
    nb = p // BLK

    def body(q_ref, km_ref, kp_ref, kc_ref, do_ref, sink_ref, t_ref, dqkv_ref, dsink_ref, carry, macc):
        i = pl.program_id(0)
        n = nb - 1 - i

        @pl.when(i == 0)
        def _():
            carry[...] = jnp.zeros_like(carry)
            macc[...] = jnp.zeros_like(macc)
            dsink_ref[...] = jnp.zeros_like(dsink_ref)

        lo = lax.broadcasted_iota(jnp.int32, (BLK, BLK), 1) < HEAD_DIM
        lane8 = lax.broadcasted_iota(jnp.int32, (8, BLK), 1)
        c, s1, s2 = t_ref[:, 0:128], -t_ref[:, 128:256], -t_ref[:, 256:384]
        dk = jnp.zeros((N_KEY, BLK), F32)
        dv = jnp.zeros((N_KEY, BLK), F32)
        for h in range(2):
            qs, k2, v2, bias, lok = _attn_setup(n, h, q_ref, km_ref, kp_ref, kc_ref)
            dos = _stack_heads(do_ref, h, lo)
            st = _dot_nt(k2, qs)
            dpt = _dot_nt(v2, dos)
            p_parts, ds_parts = [], []
            for g in range(8):
                cols = slice(BLK * g, BLK * (g + 1))
                pn, ps = _attn_head(st[:, cols], bias, sink_ref[0, 8 * h + g])
                dp = dpt[:, cols]
                delta = jnp.sum(pn * dp, axis=0, keepdims=True)
                ds_parts.append((pn * (dp - delta)).astype(BF16))
                p_parts.append(pn.astype(BF16))
                dsk = -jnp.sum(ps * delta, axis=1, keepdims=True)
                dsink_ref[...] += jnp.where(lane8 == 8 * h + g, dsk, 0.0)
            dst = jnp.concatenate(ds_parts, axis=1)
            pt = jnp.concatenate(p_parts, axis=1)
            dq = _dot_tn(dst, k2)
            for jp in range(4):
                lo_c = BLK * (4 * h + jp)
                dqkv_ref[:, lo_c:lo_c + BLK] = (_rope(_unstack_heads(dq, jp, lo), c, s1, s2) * SCALE).astype(BF16)
            dk2 = _dot(dst, qs)
            dv2 = _dot(pt, dos)
            dk2 = dk2 + pltpu.roll(dk2, HEAD_DIM, 1)
            dv2 = dv2 + pltpu.roll(dv2, HEAD_DIM, 1)
            own = lok if h == 0 else jnp.logical_not(lok)
            dk = jnp.where(own, dk2, dk)
            dv = jnp.where(own, dv2, dv)
        macc[:, 0:BLK] += dk[2 * BLK:N_KEY]
        macc[:, BLK:2 * BLK] += dv[2 * BLK:N_KEY]
        last = (n == 0).astype(F32)
        zpad = jnp.zeros((PAD, BLK), F32)
        dk_c = dk[BLK:2 * BLK] + carry[:, 0:BLK] + last * jnp.concatenate([zpad, macc[:, 0:BLK]], axis=0)
        dv_c = dv[BLK:2 * BLK] + carry[:, BLK:2 * BLK] + last * jnp.concatenate([zpad, macc[:, BLK:2 * BLK]], axis=0)
        carry[:, 0:BLK] = dk[0:BLK]
        carry[:, BLK:2 * BLK] = dv[0:BLK]
        dqkv_ref[:, D:D + BLK] = _rope(dk_c, c, s1, s2).astype(BF16)
        dqkv_ref[:, D + BLK:D + 2 * BLK] = dv_c.astype(BF16)

    def rev(w):
        return pl.BlockSpec((BLK, w), lambda i: (nb - 1 - i, 0))

    return _call(
        body, name="attn_bwd", grid=(nb,),
        in_specs=[rev(D),
                  pl.BlockSpec((BLK, 256), lambda i: (0, 0)),
                  pl.BlockSpec((BLK, 256), lambda i: (jnp.maximum(nb - 2 - i, 0), 0)),
                  rev(256), rev(D),
                  pl.BlockSpec(memory_space=pltpu.SMEM), rev(384)],
        out_specs=[rev(QKV_W), pl.BlockSpec((8, BLK), lambda i: (0, 0))],
        out_shape=[jax.ShapeDtypeStruct((p, QKV_W), BF16), jax.ShapeDtypeStruct((8, BLK), F32)],
        scratch=[pltpu.VMEM((BLK, 256), F32), pltpu.VMEM((N_META, 256), F32)], sem="arbitrary",
        args=(q, kv, kv, kv, dao, sinks, tabs), comm=comm)


def _in_bwd(dqkv, da, dg, dgt, w_int, h0p, dh1, gain, comm=None):
    p = h0p.shape[0]
    tm = _row_tile(p)
    nt = p // tm
    first_rows = tm - BLK

    def body(dq_ref, da_ref, dg_ref, dt_ref, w_ref, h_ref, dh_ref, g_ref, gx_ref, dm_ref, acc_ref, buf, sems):
        i = pl.program_id(0)
        slot = i % 2

        @pl.when(i == 0)
        def _():
            acc_ref[...] = jnp.zeros_like(acc_ref)

        dn = (_dot(dq_ref[...], w_ref[0:QKV_W, :]) + _dot(da_ref[...], w_ref[QKV_W:QKV_W + D, :])
              + _dot(dg_ref[...], w_ref[QKV_W + D:QKV_W + 2 * D, :]) + _dot(dt_ref[...], w_ref[QKV_W + 2 * D:IN_W, :]))
        dh, dgain = _rms_bwd(h_ref[...], g_ref[...], dn)
        dh0 = dh_ref[...] + dh
        acc_ref[0:1, :] += dgain
        buf[slot] = dh0

        @pl.when(i == 0)
        def _():
            dm_ref[...] = dh0[PAD:BLK]

        def first_copy():
            return pltpu.make_async_copy(buf.at[0, pl.ds(BLK, first_rows), :], gx_ref.at[pl.ds(0, first_rows), :], sems.at[0])

        def tile_copy(j, s):
            return pltpu.make_async_copy(buf.at[s], gx_ref.at[pl.ds(pl.multiple_of(j * tm - BLK, BLK), tm), :], sems.at[s])

        if first_rows:
            @pl.when(i == 1)
            def _():
                first_copy().wait()

        @pl.when(i >= 2)
        def _():
            tile_copy(i - 1, 1 - slot).wait()

        if first_rows:
            @pl.when(i == 0)
            def _():
                first_copy().start()

        @pl.when(i > 0)
        def _():
            tile_copy(i, slot).start()

        @pl.when(i == nt - 1)
        def _():
            tile_copy(i, slot).wait()

    def row(w):
        return pl.BlockSpec((tm, w), lambda i: (i, 0))

    return _call(
        body, name="in_bwd", grid=(nt,),
        in_specs=[row(QKV_W), row(D), row(D), row(2 * D), VM, row(D), row(D), VM],
        out_specs=[ANY, pl.BlockSpec((N_META, D), lambda i: (0, 0)), pl.BlockSpec((8, D), lambda i: (0, 0))],
        out_shape=[jax.ShapeDtypeStruct((p - BLK, D), F32), jax.ShapeDtypeStruct((N_META, D), F32),
                   jax.ShapeDtypeStruct((8, D), F32)],
        scratch=[pltpu.VMEM((2, tm, D), F32), pltpu.SemaphoreType.DMA((2,))],
        sem="arbitrary", args=(dqkv, da, dg, dgt, w_int, h0p, dh1, gain), comm=comm)


def _sum_slots(slots, name):
    r = slots.shape[0] // N_DEV
    cols = slots.shape[1]
    tr = r if r <= 352 else (r // 2 if (r // 2) % 16 == 0 else r // 3)
    steps = r // tr

    def body(*refs):
        acc = refs[0][...].astype(F32)
        for s in range(1, N_DEV):
            acc = acc + refs[s][...].astype(F32)
        refs[N_DEV][...] = acc

    return pl.pallas_call(
        body, name=name, grid=(steps,),
        in_specs=[pl.BlockSpec((tr, cols), functools.partial(lambda i, s: (s * steps + i, 0), s=s)) for s in range(N_DEV)],
        out_specs=pl.BlockSpec((tr, cols), lambda i: (i, 0)),
        out_shape=jax.ShapeDtypeStruct((r, cols), F32),
        compiler_params=_cparams("parallel"),
    )(*([slots] * N_DEV))


def _adamw_math(w, g, m, v):
    m_n = ADAM_B1 * m + (1.0 - ADAM_B1) * g
    v_n = ADAM_B2 * v + (1.0 - ADAM_B2) * jnp.square(g)
    m_hat = m_n / (1.0 - ADAM_B1 ** ADAM_STEP)
    v_hat = v_n / (1.0 - ADAM_B2 ** ADAM_STEP)
    return -ADAM_LR * (m_hat / (jnp.sqrt(v_hat) + ADAM_EPS) + ADAM_WD * w), m_n, v_n


def _sum_adamw(parts, w, m, v, name, nslots=N_DEV):
    r, cols = w.shape
    rs = r // len(parts)
    tr = rs if rs <= 352 else (rs // 2 if (rs // 2) % 16 == 0 else rs // 3)
    steps = rs // tr

    def body(*refs):
        w_ref, m_ref, v_ref, g_ref, d_ref, nm_ref, nv_ref = refs[nslots * len(parts):]
        i = pl.program_id(0)
        for q in range(len(parts)):
            @pl.when(i // steps == q)
            def _(q=q):
                g = refs[nslots * q][...].astype(F32)
                for s in range(1, nslots):
                    g = g + refs[nslots * q + s][...].astype(F32)
                g_ref[...] = g
                d_ref[...], nm_ref[...], nv_ref[...] = _adamw_math(w_ref[...], g, m_ref[...], v_ref[...])

    def slot_spec(q, s):
        return pl.BlockSpec((tr, cols), lambda i: (s * steps + jnp.clip(i - q * steps, 0, steps - 1), 0))

    spec = pl.BlockSpec((tr, cols), lambda i: (i, 0))
    return pl.pallas_call(
        body, name=name, grid=(steps * len(parts),),
        in_specs=[slot_spec(q, s) for q in range(len(parts)) for s in range(nslots)] + [spec] * 3,
        out_specs=[spec] * 4, out_shape=[jax.ShapeDtypeStruct((r, cols), F32)] * 4,
        compiler_params=_cparams("parallel"),
    )(*[a for a in parts for _ in range(nslots)], w, m, v)


def _adamw_many(ws, gs, ms, vs, name):
    n = len(ws)

    def body(*refs):
        w, g, m, v = refs[0:n], refs[n:2 * n], refs[2 * n:3 * n], refs[3 * n:4 * n]
        d, nm, nv = refs[4 * n:5 * n], refs[5 * n:6 * n], refs[6 * n:7 * n]
        for k in range(n):
            d[k][...], nm[k][...], nv[k][...] = _adamw_math(w[k][...], g[k][...], m[k][...], v[k][...])

    outs = pl.pallas_call(
        body, name=name, in_specs=[VM] * (4 * n), out_specs=[VM] * (3 * n),
        out_shape=[jax.ShapeDtypeStruct(a.shape, F32) for a in ws] * 3,
    )(*ws, *gs, *ms, *vs)
    return outs[0:n], outs[n:2 * n], outs[2 * n:3 * n]


def _rope_tables(p):
    half = ROT_DIM // 2
    lane = jnp.arange(BLK)
    seg = (lane % HEAD_DIM) // half
    inv_freq = ROPE_THETA ** (-(lane % half).astype(F32) * 2.0 / ROT_DIM)
    pos = (jnp.arange(p) - PAD).astype(F32)
    ang = pos[:, None] * inv_freq[None, :]
    cos = jnp.cos(ang)
    sin = jnp.sin(ang)
    c = jnp.where(seg[None, :] < 2, cos, 1.0)
    s1 = jnp.where(seg[None, :] == 0, -sin, 0.0)
    s2 = jnp.where(seg[None, :] == 1, sin, 0.0)
    return jnp.concatenate([c, s1, s2], axis=1).astype(F32)


def _flat_pack(parts, rows):
    flat = jnp.concatenate([a.reshape(-1).astype(F32) for a in parts])
    return jnp.pad(flat, (0, rows * D - flat.shape[0])).reshape(rows, D)


def _flat_unpack(pack, shapes):
    flat = pack.reshape(-1)
    out, off = [], 0
    for s in shapes:
        size = 1
        for e in s:
            size *= e
        out.append(flat[off:off + size].reshape(s))
        off += size
    return out


def kernel(x, meta_tokens, norm_pre_mix, norm_post_mix, w_in, b_in, attn_sinks, w_attn_proj, conv_dw_w, conv_dw_b, conv_ln_g, conv_ln_b, w_conv_proj, b_conv_proj, w_out, norm_pre_ffn, norm_post_ffn, w_up, ffn_dw_w, ffn_dw_b, w_down, loss_target, m_meta_tokens, m_norm_pre_mix, m_norm_post_mix, m_w_in, m_b_in, m_attn_sinks, m_w_attn_proj, m_conv_dw_w, m_conv_dw_b, m_conv_ln_g, m_conv_ln_b, m_w_conv_proj, m_b_conv_proj, m_w_out, m_norm_pre_ffn, m_norm_post_ffn, m_w_up, m_ffn_dw_w, m_ffn_dw_b, m_w_down, v_meta_tokens, v_norm_pre_mix, v_norm_post_mix, v_w_in, v_b_in, v_attn_sinks, v_w_attn_proj, v_conv_dw_w, v_conv_dw_b, v_conv_ln_g, v_conv_ln_b, v_w_conv_proj, v_b_conv_proj, v_w_out, v_norm_pre_ffn, v_norm_post_ffn, v_w_up, v_ffn_dw_w, v_ffn_dw_b, v_w_down):
    seq = x.shape[1]
    p = seq + BLK
    me = 4 * lax.axis_index("x") + 2 * lax.axis_index("y") + lax.axis_index("c")
    in_cols = w_in.shape[2]
    up_cols = w_up.shape[2]

    small = jnp.zeros((56, up_cols), F32)
    small = small.at[0:N_META, 0:BLK].set(meta_tokens)
    small = small.at[16:16 + CONV_K, 0:BLK].set(conv_dw_w[0])
    small = small.at[48:48 + FFN_K, :].set(ffn_dw_w[0])
    w_int, small_all = _exchange(_Both(_GatherRelay(w_in[0].T.astype(BF16)), _Gather([small])), "gather_w_in")
    small_all = small_all.reshape(N_DEV, 56, up_cols)
    meta_full = small_all[:, 0:N_META, 0:BLK].transpose(1, 0, 2).reshape(N_META, D)
    cdw = small_all[:, 16:16 + CONV_K, 0:BLK].transpose(1, 0, 2).reshape(CONV_K, D)
    cdw32 = jnp.pad(cdw, ((0, 32 - CONV_K), (0, 0)))
    fdw = small_all[:, 48:48 + FFN_K, :].transpose(1, 0, 2).reshape(FFN_K, 2 * FFN)

    tabs = _rope_tables(p)
    vecs = jnp.concatenate([conv_ln_g, conv_ln_b, b_conv_proj, norm_post_mix, norm_pre_ffn, jnp.zeros((3, D), F32)], axis=0)

    (h0p, n1, q, kv, ag, gates), (wa, wc, wo) = _in_proj(
        x[0], meta_full, norm_pre_mix, w_int, b_in, tabs,
        comm=_Gather([w_attn_proj[0].astype(BF16), w_conv_proj[0].astype(BF16), w_out[0].astype(BF16)]))
    (ao,), (w_upt,) = _attn_fwd(q, kv, attn_sinks, comm=_Gather([w_up[0].T.astype(BF16)]))
    (c0,), (wd,) = _conv31_fwd(ag, cdw32, conv_dw_b, comm=_Gather([w_down[0].astype(BF16)]))
    c1, attn, conv, merged, mix, h1, n2 = _mixer_fwd(ao, c0, gates, h0p, wa, wc, wo, vecs)
    u0 = _mm_nt(n2, w_upt, "ffn_up")
    act, dact_dv, dact_dg = _ffn_act(u0, fdw, ffn_dw_b)
    dffn, dact, dy, acc_f = _ffn_down_loss(act, wd, h1, loss_target[0], norm_post_ffn)

    (dug, duv, gfw_g, gfw_v, gfb_g, gfb_v, g_wd), _ = _ffn_act_bwd(u0, dact, dact_dg, dact_dv, fdw, act, dffn)
    g_wupt, (s_wd0,) = _mm_tn([dug, duv], n2, "grad_w_up", comm=_Scatter([g_wd], 0, 2))
    (dh1, acc_u), (s_wd1,) = _ffn_in_bwd(dug, duv, w_upt, h1, dy, norm_pre_ffn, comm=_Scatter([g_wd], 1, 2))
    (dmix, dat, dcv, dgt, dao, dc0, acc_m), (s_wup0,) = _mixer_bwd(
        dh1, mix, attn, conv, gates, c0, wa, wc, wo, vecs, comm=_Scatter([g_wupt], 0, 4))
    (da, dg, g_cdw, g_cdb, g_wo, g_wa, g_wc), (s_wup1, s_wup2, s_wup3) = _conv31_bwd(
        ag, dc0, cdw32, [(merged, dmix), (ao, dat), (c1, dcv)],
        comm=_Both(_Both(_Scatter([g_wupt], 1, 4), _Scatter([g_wupt], 2, 4)), _Scatter([g_wupt], 3, 4)))
    (dqkv, dsink), (s_wa, s_wc, s_wo) = _attn_bwd(q, kv, dao, attn_sinks, tabs, comm=_Scatter([g_wa, g_wc, g_wo]))
    loss_row = jnp.sum(acc_f[1:2, :], axis=1, keepdims=True)
    early = [loss_row, acc_m[0:1], dsink[0:1, 0:16], g_cdw[0:CONV_K], g_cdb,
             acc_m[2:3], acc_m[3:4], acc_m[1:2], acc_u[0:1], acc_f[0:1],
             jnp.concatenate([gfw_g, gfw_v], axis=1), jnp.concatenate([gfb_g, gfb_v], axis=1)]
    (g_wint, g_bin), (gathered_early,) = _mm_tn([dqkv, da, dg, dgt], n1, "grad_w_in", col_sums=True,
                                                comm=_Gather([_flat_pack(early, 64)]))
    (from_sibling,) = _exchange(_SiblingSwap(g_wint), "swap_w_in")
    (grad_x2d, dmeta, acc_i), (s_win,) = _in_bwd(dqkv, da, dg, dgt, w_int, h0p, dh1, norm_pre_mix,
                                                 comm=_ChipScatter(_pair_add(g_wint, from_sibling)))

    big = []
    for nm, parts, nslots, w, m, v, tr in (
            ("w_in", [s_win], N_CHIP, w_in, m_w_in, v_w_in, True), ("w_up", [s_wup0, s_wup1, s_wup2, s_wup3], N_DEV, w_up, m_w_up, v_w_up, True),
            ("w_attn_proj", [s_wa], N_DEV, w_attn_proj, m_w_attn_proj, v_w_attn_proj, False),
            ("w_conv_proj", [s_wc], N_DEV, w_conv_proj, m_w_conv_proj, v_w_conv_proj, False),
            ("w_out", [s_wo], N_DEV, w_out, m_w_out, v_w_out, False),
            ("w_down", [s_wd0, s_wd1], N_DEV, w_down, m_w_down, v_w_down, False)):
        ins = [a[0].T if tr else a[0] for a in (w, m, v)]
        big.append(tuple((o.T if tr else o)[None] for o in _sum_adamw(parts, *ins, "update_" + nm, nslots)))

    late = [dmeta, acc_i[0:1], g_bin]
    (gathered_late,) = _exchange(_Gather([_flat_pack(late, 24)]), "gather_small_grads")
    g_meta, g_npm, g_bi = _flat_unpack(_sum_slots(gathered_late, "sum_late_grads"), [a.shape for a in late])
    tot = _flat_unpack(_sum_slots(gathered_early, "sum_small_grads"), [a.shape for a in early])
    (loss, g_nqm, g_sk, g_cw, g_cb, g_lg, g_lb, g_bc, g_npf, g_nqf, g_fw, g_fb) = tot
    loss = loss.reshape(())
    g_meta = lax.dynamic_slice_in_dim(g_meta, me * BLK, BLK, axis=1)
    g_cw = lax.dynamic_slice_in_dim(g_cw, me * BLK, BLK, axis=1)[None]
    g_fw = lax.dynamic_slice_in_dim(g_fw, me * up_cols, up_cols, axis=1)[None]

    sm_w = [meta_tokens, norm_pre_mix, norm_post_mix, b_in, attn_sinks, conv_dw_w, conv_dw_b, conv_ln_g, conv_ln_b,
            b_conv_proj, norm_pre_ffn, norm_post_ffn, ffn_dw_w, ffn_dw_b]
    sm_g = [g_meta, g_npm, g_nqm, g_bi, g_sk, g_cw, g_cb, g_lg, g_lb, g_bc, g_npf, g_nqf, g_fw, g_fb]
    sm_m = [m_meta_tokens, m_norm_pre_mix, m_norm_post_mix, m_b_in, m_attn_sinks, m_conv_dw_w, m_conv_dw_b, m_conv_ln_g,
            m_conv_ln_b, m_b_conv_proj, m_norm_pre_ffn, m_norm_post_ffn, m_ffn_dw_w, m_ffn_dw_b]
    sm_v = [v_meta_tokens, v_norm_pre_mix, v_norm_post_mix, v_b_in, v_attn_sinks, v_conv_dw_w, v_conv_dw_b, v_conv_ln_g,
            v_conv_ln_b, v_b_conv_proj, v_norm_pre_ffn, v_norm_post_ffn, v_ffn_dw_w, v_ffn_dw_b]
    sm_d, sm_nm, sm_nv = _adamw_many(sm_w, sm_g, sm_m, sm_v, "adamw_small")

    order = ["meta_tokens", "norm_pre_mix", "norm_post_mix", "w_in", "b_in", "attn_sinks", "w_attn_proj", "conv_dw_w",
             "conv_dw_b", "conv_ln_g", "conv_ln_b", "w_conv_proj", "b_conv_proj", "w_out", "norm_pre_ffn", "norm_post_ffn",
             "w_up", "ffn_dw_w", "ffn_dw_b", "w_down"]
    small_names = ["meta_tokens", "norm_pre_mix", "norm_post_mix", "b_in", "attn_sinks", "conv_dw_w", "conv_dw_b", "conv_ln_g",
                   "conv_ln_b", "b_conv_proj", "norm_pre_ffn", "norm_post_ffn", "ffn_dw_w", "ffn_dw_b"]
    big_names = ["w_in", "w_up", "w_attn_proj", "w_conv_proj", "w_out", "w_down"]
    table = {}
    for k, nm in enumerate(small_names):
        table[nm] = (sm_g[k], sm_d[k], sm_nm[k], sm_nv[k])
    for k, nm in enumerate(big_names):
        table[nm] = big[k]
    grad_x = grad_x2d[None]
    outs = [loss, grad_x]
    for field in range(4):
        outs += [table[nm][field] for nm in order]
    return tuple(outs)
```

```python
import functools

import jax
import jax.numpy as jnp
from jax import lax
from jax.experimental import pallas as pl
from jax.experimental.pallas import tpu as pltpu

F32 = jnp.float32
BF16 = jnp.bfloat16
MESH = pl.DeviceIdType.MESH

D = 1024
HEAD_DIM = 64
N_META = 16
BLK = 128
PAD = BLK - N_META
CONV_K = 31
FFN = 2816
FFN_K = 3
QKV_W = 1280
IN_W = 5376
ROT_DIM = 16
ROPE_THETA = 500000.0
RMS_EPS = 1e-6
LN_EPS = 1e-5
NEG_INF = -1e30
SCALE = HEAD_DIM ** -0.5
N_DEV = 8

ADAM_LR = 0.001
ADAM_B1 = 0.9
ADAM_B2 = 0.999
ADAM_EPS = 1e-08
ADAM_WD = 0.01
ADAM_STEP = 10

VMEM_BYTES_V7X = 64 * 1024 * 1024
VMEM_LIMIT = VMEM_BYTES_V7X - 8 * 1024 * 1024

NT = (((1,), (1,)), ((), ()))
TN = (((0,), (0,)), ((), ()))
VM = pl.BlockSpec(memory_space=pltpu.VMEM)
ANY = pl.BlockSpec(memory_space=pl.ANY)


def _cparams(*sem):
    return pltpu.CompilerParams(dimension_semantics=sem or None, vmem_limit_bytes=VMEM_LIMIT)


def _row_tile(p):
    return 384 if p % 384 == 0 else 128


def _dot(a, b):
    return jnp.dot(a, b, preferred_element_type=F32)


def _dot_nt(a, b):
    return lax.dot_general(a, b, NT, preferred_element_type=F32)


def _dot_tn(a, b):
    return lax.dot_general(a, b, TN, preferred_element_type=F32)


def _rms(x, g):
    return x * lax.rsqrt(jnp.mean(x * x, axis=-1, keepdims=True) + RMS_EPS) * g


def _lnsilu(x, g, b):
    mu = jnp.mean(x, axis=-1, keepdims=True)
    var = jnp.mean(jnp.square(x - mu), axis=-1, keepdims=True)
    z = (x - mu) * lax.rsqrt(var + LN_EPS) * g + b
    return z * jax.nn.sigmoid(z)


def _rms_bwd(x, g, dy):
    r = lax.rsqrt(jnp.mean(x * x, axis=-1, keepdims=True) + RMS_EPS)
    xn = x * r
    u = dy * g
    dg = jnp.sum(dy * xn, axis=0, keepdims=True)
    dx = r * (u - xn * jnp.mean(u * xn, axis=-1, keepdims=True))
    return dx, dg


def _lnsilu_bwd(x, g, b, dout):
    mu = jnp.mean(x, axis=-1, keepdims=True)
    xc = x - mu
    rs = lax.rsqrt(jnp.mean(xc * xc, axis=-1, keepdims=True) + LN_EPS)
    yh = xc * rs
    z = yh * g + b
    sg = jax.nn.sigmoid(z)
    dz = dout * (sg * (1.0 + z * (1.0 - sg)))
    dg = jnp.sum(dz * yh, axis=0, keepdims=True)
    db = jnp.sum(dz, axis=0, keepdims=True)
    dyh = dz * g
    dx = rs * (dyh - jnp.mean(dyh, axis=-1, keepdims=True) - yh * jnp.mean(dyh * yh, axis=-1, keepdims=True))
    return dx, dg, db


def _rope(v, c, s1, s2):
    return v * c + pltpu.roll(v, BLK - 8, 1) * s1 + pltpu.roll(v, 8, 1) * s2


def _rows(i, tm):
    return i * tm + lax.broadcasted_iota(jnp.int32, (tm, 1), 0)


def _place():
    return lax.axis_index("x"), lax.axis_index("y"), lax.axis_index("c")


def _blk(ref, idx, r, dtype):
    return ref.at[pl.ds(pl.multiple_of(idx * r, 16 if dtype == BF16 else 8), r), :]


class _Gather:
    def __init__(self, arrs):
        self.ins = list(arrs)
        n = len(arrs)
        self.out_shape = [jax.ShapeDtypeStruct((N_DEV * a.shape[0], a.shape[1]), a.dtype) for a in arrs]
        self.scratch = [pltpu.SemaphoreType.DMA((n, 7)), pltpu.SemaphoreType.DMA((n, 7)), pltpu.SemaphoreType.DMA((n,))]

    def _parts(self, ins, outs, sems):
        send_sems, recv_sems, local_sems = sems
        n = len(ins)
        x, y, c = _place()
        me, sibling = (x, y, c), (x, y, 1 - c)
        chips = [(1 - x, y), (x, 1 - y), (1 - x, 1 - y)]

        def rows(a, p):
            return _blk(outs[a], 4 * p[0] + 2 * p[1] + p[2], self.ins[a].shape[0], self.ins[a].dtype)

        def copy(a, k, block, to, src=None):
            return pltpu.make_async_remote_copy(
                src_ref=rows(a, block) if src is None else src, dst_ref=rows(a, block),
                send_sem=send_sems.at[a, k], recv_sem=recv_sems.at[a, k], device_id=to, device_id_type=MESH)

        mine = [pltpu.make_async_copy(ins[a], rows(a, me), local_sems.at[a]) for a in range(n)]
        first = []
        for a in range(n):
            first.append(copy(a, 0, me, sibling, src=ins[a]))
            first += [copy(a, 1 + j, me, (*chip, c), src=ins[a]) for j, chip in enumerate(chips)]
        return n, c, me, sibling, chips, copy, mine, first

    def start(self, ins, outs, sems):
        *_, mine, first = self._parts(ins, outs, sems)
        for cp in mine + first:
            cp.start()

    def finish(self, ins, outs, sems):
        n, c, me, sibling, chips, copy, mine, first = self._parts(ins, outs, sems)
        passed = []
        for j, chip in enumerate(chips):
            for a in range(n):
                copy(a, 1 + j, (*chip, c), me).wait_recv()
                fwd = copy(a, 4 + j, (*chip, c), sibling)
                fwd.start()
                passed.append(fwd)
        for a in range(n):
            copy(a, 0, sibling, me).wait_recv()
            for j, chip in enumerate(chips):
                copy(a, 4 + j, (*chip, 1 - c), me).wait_recv()
        for cp in first + passed:
            cp.wait_send()
        for cp in mine:
            cp.wait()


class _GatherRelay:
    N_COPY = 13

    def __init__(self, arr):
        self.ins = [arr]
        self.r = arr.shape[0]
        self.out_shape = [jax.ShapeDtypeStruct((N_DEV * self.r, arr.shape[1]), arr.dtype)]
        self.scratch = [pltpu.SemaphoreType.DMA((self.N_COPY,)), pltpu.SemaphoreType.DMA((self.N_COPY,)),
                        pltpu.SemaphoreType.DMA]

    def _parts(self, ins, outs, sems):
        send_sems, recv_sems, local_sem = sems
        x, y, c = _place()
        r, half = self.r, self.r // 2
        out = outs[0]
        me, sib, xn, yn, dg = (x, y, c), (x, y, 1 - c), (1 - x, y, c), (x, 1 - y, c), (1 - x, 1 - y, c)
        sx, sy, sd = (1 - x, y, 1 - c), (x, 1 - y, 1 - c), (1 - x, 1 - y, 1 - c)
        lo, hi = (0, half), (half, half)

        def rows(p, part=(0, r)):
            return out.at[pl.ds(pl.multiple_of((4 * p[0] + 2 * p[1] + p[2]) * r + part[0], 16), part[1]), :]

        def own(part):
            return ins[0].at[pl.ds(part[0], part[1]), :]

        def copy(k, dev_rows, to, src=None):
            return pltpu.make_async_remote_copy(
                src_ref=dev_rows if src is None else src, dst_ref=dev_rows,
                send_sem=send_sems.at[k], recv_sem=recv_sems.at[k], device_id=to, device_id_type=MESH)

        mine = pltpu.make_async_copy(ins[0], rows(me), local_sem)
        first = [copy(0, rows(me), sib, src=ins[0]),
                 copy(1, rows(me, lo), xn, src=own(lo)), copy(3, rows(me, hi), yn, src=own(hi)),
                 copy(2, rows(me, hi), xn, src=own(hi)), copy(4, rows(me, lo), yn, src=own(lo))]
        arrive = {0: rows(sib), 1: rows(xn, lo), 2: rows(xn, hi), 3: rows(yn, hi), 4: rows(yn, lo),
                  5: rows(dg, lo), 6: rows(dg, hi), 7: rows(sx, lo), 8: rows(sx, hi), 9: rows(sy, hi),
                  10: rows(sy, lo), 11: rows(sd, lo), 12: rows(sd, hi)}
        relay = {1: [(5, rows(xn, lo), yn), (7, rows(xn, lo), sib)], 3: [(6, rows(yn, hi), xn), (9, rows(yn, hi), sib)],
                 2: [(8, rows(xn, hi), sib)], 4: [(10, rows(yn, lo), sib)],
                 5: [(11, rows(dg, lo), sib)], 6: [(12, rows(dg, hi), sib)]}
        return copy, mine, first, arrive, relay, me

    def start(self, ins, outs, sems):
        _, mine, first, _, _, _ = self._parts(ins, outs, sems)
        for cp in [mine] + first:
            cp.start()

    def finish(self, ins, outs, sems):
        copy, mine, first, arrive, relay, me = self._parts(ins, outs, sems)
        passed = []
        for k in (1, 3, 2, 4, 5, 6):
            copy(k, arrive[k], me).wait_recv()
            for k2, dev_rows, to in relay[k]:
                fwd = copy(k2, dev_rows, to)
                fwd.start()
                passed.append(fwd)
        for k in (0, 7, 8, 9, 10, 11, 12):
            copy(k, arrive[k], me).wait_recv()
        for cp in first + passed:
            cp.wait_send()
        mine.wait()


FLIPS = [(0, 0, 1), (1, 0, 0), (0, 1, 0), (1, 1, 0), (1, 0, 1), (0, 1, 1), (1, 1, 1)]


class _Scatter:
    def __init__(self, arrs, part=0, nparts=1):
        self.ins = list(arrs)
        self.part, self.nparts = part, nparts
        n = len(arrs)
        self.out_shape = [jax.ShapeDtypeStruct((a.shape[0] // nparts, a.shape[1]), a.dtype) for a in arrs]
        self.scratch = [pltpu.SemaphoreType.DMA((n, 7)), pltpu.SemaphoreType.DMA((n, 7)), pltpu.SemaphoreType.DMA((n,))]

    def _parts(self, ins, outs, sems):
        send_sems, recv_sems, local_sems = sems
        n = len(ins)
        x, y, c = _place()
        me = 4 * x + 2 * y + c

        def flip(v, f):
            return 1 - v if f else v

        def src(a, idx):
            r = self.ins[a].shape[0] // N_DEV
            rs = r // self.nparts
            return ins[a].at[pl.ds(pl.multiple_of(idx * r + self.part * rs, 16), rs), :]

        def dst(a, idx):
            rs = self.ins[a].shape[0] // N_DEV // self.nparts
            return outs[a].at[pl.ds(pl.multiple_of(idx * rs, 16), rs), :]

        mine = [pltpu.make_async_copy(src(a, me), dst(a, me), local_sems.at[a]) for a in range(n)]
        sends, recvs = [], []
        for k, f in enumerate(FLIPS):
            peer = (flip(x, f[0]), flip(y, f[1]), flip(c, f[2]))
            pidx = 4 * peer[0] + 2 * peer[1] + peer[2]
            for a in range(n):
                sends.append(pltpu.make_async_remote_copy(
                    src_ref=src(a, pidx), dst_ref=dst(a, me),
                    send_sem=send_sems.at[a, k], recv_sem=recv_sems.at[a, k], device_id=peer, device_id_type=MESH))
                recvs.append(functools.partial(
                    pltpu.make_async_remote_copy,
                    src_ref=src(a, pidx), dst_ref=dst(a, pidx),
                    send_sem=send_sems.at[a, k], recv_sem=recv_sems.at[a, k], device_id=peer, device_id_type=MESH))
        return mine, sends, recvs

    def start(self, ins, outs, sems):
        mine, sends, _ = self._parts(ins, outs, sems)
        for cp in mine + sends:
            cp.start()

    def finish(self, ins, outs, sems):
        mine, sends, recvs = self._parts(ins, outs, sems)
        for make in recvs:
            make().wait_recv()
        for cp in sends:
            cp.wait_send()
        for cp in mine:
            cp.wait()


N_CHIP = 4


class _SiblingSwap:
    def __init__(self, arr):
        self.ins = [arr]
        self.r = arr.shape[0] // N_DEV
        self.out_shape = [jax.ShapeDtypeStruct((N_CHIP * self.r, arr.shape[1]), arr.dtype)]
        self.scratch = [pltpu.SemaphoreType.DMA((N_CHIP,)), pltpu.SemaphoreType.DMA((N_CHIP,))]

    def _copies(self, ins, outs, sems):
        send_sems, recv_sems = sems
        x, y, c = _place()
        r = self.r
        return [pltpu.make_async_remote_copy(
            src_ref=ins[0].at[pl.ds(pl.multiple_of((2 * j + 1 - c) * r, 16), r), :],
            dst_ref=outs[0].at[pl.ds(j * r, r), :],
            send_sem=send_sems.at[j], recv_sem=recv_sems.at[j], device_id=(x, y, 1 - c), device_id_type=MESH)
            for j in range(N_CHIP)]

    def start(self, ins, outs, sems):
        for cp in self._copies(ins, outs, sems):
            cp.start()

    def finish(self, ins, outs, sems):
        for cp in self._copies(ins, outs, sems):
            cp.wait()


class _ChipScatter:
    def __init__(self, arr):
        self.ins = [arr]
        self.r = arr.shape[0] // N_CHIP
        self.out_shape = [jax.ShapeDtypeStruct(arr.shape, arr.dtype)]
        self.scratch = [pltpu.SemaphoreType.DMA((3,)), pltpu.SemaphoreType.DMA((3,)), pltpu.SemaphoreType.DMA]

    def _parts(self, ins, outs, sems):
        send_sems, recv_sems, local_sem = sems
        x, y, c = _place()
        r = self.r
        my_chip = 2 * x + y

        def rows(ref, j):
            return ref.at[pl.ds(pl.multiple_of(j * r, 16), r), :]

        mine = pltpu.make_async_copy(rows(ins[0], my_chip), rows(outs[0], my_chip), local_sem)
        sends, recvs = [], []
        for k, (fx, fy) in enumerate(((1, 0), (0, 1), (1, 1))):
            px, py = (1 - x if fx else x), (1 - y if fy else y)
            peer_chip = 2 * px + py
            sends.append(pltpu.make_async_remote_copy(
                src_ref=rows(ins[0], peer_chip), dst_ref=rows(outs[0], my_chip),
                send_sem=send_sems.at[k], recv_sem=recv_sems.at[k], device_id=(px, py, c), device_id_type=MESH))
            recvs.append(functools.partial(
                pltpu.make_async_remote_copy,
                src_ref=rows(ins[0], peer_chip), dst_ref=rows(outs[0], peer_chip),
                send_sem=send_sems.at[k], recv_sem=recv_sems.at[k], device_id=(px, py, c), device_id_type=MESH))
        return mine, sends, recvs

    def start(self, ins, outs, sems):
        mine, sends, _ = self._parts(ins, outs, sems)
        for cp in [mine] + sends:
            cp.start()

    def finish(self, ins, outs, sems):
        mine, sends, recvs = self._parts(ins, outs, sems)
        for make in recvs:
            make().wait_recv()
        for cp in sends:
            cp.wait_send()
        mine.wait()


def _pair_add(partial, recv):
    r = recv.shape[0] // N_CHIP
    cols = recv.shape[1]
    tr = r // 2 if (r // 2) % 16 == 0 else r
    steps = r // tr
    core = lax.axis_index("c").astype(jnp.int32).reshape(1)

    def body(c_ref, p_ref, s_ref, o_ref):
        o_ref[...] = (p_ref[...].astype(F32) + s_ref[...].astype(F32)).astype(BF16)

    spec = pl.BlockSpec((tr, cols), lambda j, i, c_ref: (j * steps + i, 0))
    return pl.pallas_call(
        body, name="pair_add",
        grid_spec=pltpu.PrefetchScalarGridSpec(
            num_scalar_prefetch=1, grid=(N_CHIP, steps),
            in_specs=[pl.BlockSpec((tr, cols), lambda j, i, c_ref: ((2 * j + c_ref[0]) * steps + i, 0)), spec],
            out_specs=spec),
        out_shape=jax.ShapeDtypeStruct(recv.shape, BF16),
        compiler_params=_cparams("parallel", "parallel"),
    )(core, partial, recv)


class _Both:
    def __init__(self, a, b):
        self.a, self.b = a, b
        self.ins = a.ins + b.ins
        self.out_shape = a.out_shape + b.out_shape
        self.scratch = a.scratch + b.scratch

    def _split(self, ins, outs, sems):
        ni, no, ns = len(self.a.ins), len(self.a.out_shape), len(self.a.scratch)
        return (ins[:ni], outs[:no], sems[:ns]), (ins[ni:], outs[no:], sems[ns:])

    def start(self, ins, outs, sems):
        ra, rb = self._split(ins, outs, sems)
        self.a.start(*ra)
        self.b.start(*rb)

    def finish(self, ins, outs, sems):
        ra, rb = self._split(ins, outs, sems)
        self.a.finish(*ra)
        self.b.finish(*rb)


def _exchange(comm, name):
    n, m = len(comm.ins), len(comm.out_shape)

    def body(*refs):
        ins, outs, sems = refs[:n], refs[n:n + m], refs[n + m:]
        comm.start(ins, outs, sems)
        comm.finish(ins, outs, sems)

    return pl.pallas_call(
        body, name=name, out_shape=comm.out_shape, in_specs=[ANY] * n, out_specs=[ANY] * m, scratch_shapes=comm.scratch,
    )(*comm.ins)


def _call(body, *, name, grid, in_specs, out_specs, out_shape, args, scratch=(), sem="parallel", comm=None):
    if comm is None:
        outs = pl.pallas_call(
            body, name=name, grid=grid, in_specs=list(in_specs), out_specs=list(out_specs), out_shape=list(out_shape),
            scratch_shapes=list(scratch), compiler_params=_cparams(sem))(*args)
        return outs, []
    n_in, n_out, n_sc = len(in_specs), len(out_specs), len(scratch)
    n_ci, n_co = len(comm.ins), len(comm.out_shape)
    last = grid[0] - 1

    def fused(*refs):
        ins, refs = refs[:n_in], refs[n_in:]
        c_ins, refs = refs[:n_ci], refs[n_ci:]
        outs, refs = refs[:n_out], refs[n_out:]
        c_outs, refs = refs[:n_co], refs[n_co:]
        sc, c_sems = refs[:n_sc], refs[n_sc:]
        step = pl.program_id(0)

        @pl.when(step == 0)
        def _():
            comm.start(c_ins, c_outs, c_sems)

        body(*ins, *outs, *sc)

        @pl.when(step == last)
        def _():
            comm.finish(c_ins, c_outs, c_sems)

    outs = pl.pallas_call(
        fused, name=name, grid=grid, in_specs=list(in_specs) + [ANY] * n_ci, out_specs=list(out_specs) + [ANY] * n_co,
        out_shape=list(out_shape) + comm.out_shape, scratch_shapes=list(scratch) + comm.scratch,
        compiler_params=_cparams("arbitrary"))(*args, *comm.ins)
    return outs[:n_out], outs[n_out:]


def _token_specs(tm):
    k = tm // BLK
    return [pl.BlockSpec((BLK, D), functools.partial(lambda i, t: (jnp.maximum(k * i + t - 1, 0), 0), t=t)) for t in range(k)]


def _in_proj(x2d, meta, gain, w_int, b_in, tabs, comm=None):
    p = x2d.shape[0] + BLK
    tm = _row_tile(p)
    k = tm // BLK

    def body(*refs):
        x_refs = refs[:k]
        m_ref, g_ref, w_ref, b_ref, t_ref, h_ref, n1_ref, q_ref, kv_ref, ag_ref, gt_ref = refs[k:]
        i = pl.program_id(0)
        head = jnp.concatenate([jnp.zeros((PAD, D), F32), m_ref[...]], axis=0)
        first = jnp.where(i == 0, head, x_refs[0][...])
        h = jnp.concatenate([first] + [r[...] for r in x_refs[1:]], axis=0) if k > 1 else first
        h_ref[...] = h
        n = _rms(h, g_ref[...]).astype(BF16)
        n1_ref[...] = n
        c, s1, s2 = t_ref[:, 0:128], t_ref[:, 128:256], t_ref[:, 256:384]

        def mm(c0, w):
            return _dot_nt(n, w_ref[c0:c0 + w, :]) + b_ref[:, c0:c0 + w]

        for j in range(4):
            acc = mm(256 * j, 256)
            for t in range(2):
                lo = 256 * j + 128 * t
                q_ref[:, lo:lo + 128] = (_rope(acc[:, 128 * t:128 * (t + 1)], c, s1, s2) * SCALE).astype(BF16)
        acc = mm(1024, 256)
        kv_ref[:, 0:128] = _rope(acc[:, 0:128], c, s1, s2).astype(BF16)
        kv_ref[:, 128:256] = acc[:, 128:256].astype(BF16)
        for j in range(8):
            ag_ref[:, 256 * j:256 * (j + 1)] = mm(QKV_W + 256 * j, 256).astype(BF16)
        for j in range(8):
            gt_ref[:, 256 * j:256 * (j + 1)] = mm(QKV_W + 2048 + 256 * j, 256).astype(BF16)

    def row(w):
        return pl.BlockSpec((tm, w), lambda i: (i, 0))

    return _call(
        body, name="in_proj", grid=(p // tm,),
        in_specs=_token_specs(tm) + [VM, VM, VM, VM, row(384)],
        out_specs=[row(D), row(D), row(D), row(256), row(2048), row(2048)],
        out_shape=[jax.ShapeDtypeStruct((p, D), F32)] + [jax.ShapeDtypeStruct((p, w), BF16) for w in (D, D, 256, 2048, 2048)],
        args=(x2d,) * k + (meta, gain, w_int, b_in, tabs), comm=comm)


N_KEY = 2 * BLK + N_META


def _attn_setup(n, h, q_ref, km_ref, kp_ref, kc_ref):
    lo = lax.broadcasted_iota(jnp.int32, (BLK, BLK), 1) < HEAD_DIM
    lok = lax.broadcasted_iota(jnp.int32, (N_KEY, BLK), 1) < HEAD_DIM

    def dup(lanes):
        cat = jnp.concatenate([kp_ref[:, lanes], kc_ref[:, lanes], km_ref[PAD:BLK, lanes]], axis=0).astype(F32)
        rolled = pltpu.roll(cat, HEAD_DIM, 1)
        return (jnp.where(lok, cat, rolled) if h == 0 else jnp.where(lok, rolled, cat)).astype(BF16)

    k2 = dup(slice(0, 128))
    v2 = dup(slice(128, 256))
    qs = _stack_heads(q_ref, h, lo)

    kr = lax.broadcasted_iota(jnp.int32, (BLK, BLK), 0)
    tq = BLK * n + lax.broadcasted_iota(jnp.int32, (BLK, BLK), 1) - PAD
    t_p = BLK * (n - 1) + kr - PAD
    t_c = BLK * n + kr - PAD
    ok_p = jnp.logical_and(t_p >= N_META, tq - t_p < BLK)
    ok_c = jnp.logical_and(t_c >= N_META, t_c <= tq)
    ok_m = lax.broadcasted_iota(jnp.int32, (N_META, BLK), 0) <= BLK * n + lax.broadcasted_iota(jnp.int32, (N_META, BLK), 1) - PAD
    bias = jnp.concatenate([jnp.where(ok, 0.0, NEG_INF).astype(F32) for ok in (ok_p, ok_c, ok_m)], axis=0)
    return qs, k2, v2, bias, lok


def _attn_head(s, bias, sink):
    s = s + bias
    m = jnp.maximum(jnp.max(s, axis=0, keepdims=True), sink)
    e = jnp.exp(s - m)
    es = jnp.exp(sink - m)
    inv = 1.0 / (jnp.sum(e, axis=0, keepdims=True) + es)
    return e * inv, es * inv


def _stack_heads(ref, h, lo):
    pieces = []
    for jp in range(4):
        v = ref[:, BLK * (4 * h + jp):BLK * (4 * h + jp + 1)]
        zero = jnp.zeros_like(v)
        pieces += [jnp.where(lo, v, zero), jnp.where(lo, zero, v)]
    return jnp.concatenate(pieces, axis=0)


def _unstack_heads(v, jp, lo):
    return jnp.where(lo, v[256 * jp:256 * jp + 128], v[256 * jp + 128:256 * jp + 256])


def _attn_fwd(q, kv, sinks, comm=None):
    p = q.shape[0]
    nb = p // BLK

    def body(q_ref, km_ref, kp_ref, kc_ref, sink_ref, o_ref):
        n = pl.program_id(0)
        lo = lax.broadcasted_iota(jnp.int32, (BLK, BLK), 1) < HEAD_DIM
        for h in range(2):
            qs, k2, v2, bias, _ = _attn_setup(n, h, q_ref, km_ref, kp_ref, kc_ref)
            st = _dot_nt(k2, qs)
            pt = jnp.concatenate(
                [_attn_head(st[:, BLK * g:BLK * (g + 1)], bias, sink_ref[0, 8 * h + g])[0].astype(BF16) for g in range(8)],
                axis=1)
            o = _dot_tn(pt, v2)
            for jp in range(4):
                o_ref[:, BLK * (4 * h + jp):BLK * (4 * h + jp + 1)] = _unstack_heads(o, jp, lo).astype(BF16)

    return _call(
        body, name="attn_fwd", grid=(nb,),
        in_specs=[pl.BlockSpec((BLK, D), lambda i: (i, 0)),
                  pl.BlockSpec((BLK, 256), lambda i: (0, 0)),
                  pl.BlockSpec((BLK, 256), lambda i: (jnp.maximum(i - 1, 0), 0)),
                  pl.BlockSpec((BLK, 256), lambda i: (i, 0)),
                  pl.BlockSpec(memory_space=pltpu.SMEM)],
        out_specs=[pl.BlockSpec((BLK, D), lambda i: (i, 0))],
        out_shape=[jax.ShapeDtypeStruct((p, D), BF16)],
        args=(q, kv, kv, kv, sinks), comm=comm)


def _conv31_fwd(ag, w32, b, comm=None):
    p = ag.shape[0]
    nch = p // BLK

    def body(a_ref, g_ref, w_ref, b_ref, o_ref, gp):
        gp[0:32, :] = jnp.zeros((32, BLK), F32)
        for ci in range(nch):
            r0 = BLK * ci
            glu = a_ref[r0:r0 + BLK, :].astype(F32) * jax.nn.sigmoid(g_ref[r0:r0 + BLK, :].astype(F32))
            if ci == 0:
                glu = jnp.where(_rows(0, BLK) >= PAD, glu, 0.0)
            gp[32 + r0:32 + r0 + BLK, :] = glu
        for ci in range(nch):
            r0 = BLK * ci
            acc = jnp.broadcast_to(b_ref[...], (BLK, BLK))
            for j in range(CONV_K):
                acc = acc + w_ref[j:j + 1, :] * gp[r0 + j + 2:r0 + j + 2 + BLK, :]
            o_ref[r0:r0 + BLK, :] = acc

    return _call(
        body, name="conv31_fwd", grid=(D // BLK,),
        in_specs=[pl.BlockSpec((p, BLK), lambda j: (0, j)), pl.BlockSpec((p, BLK), lambda j: (0, 8 + j)),
                  pl.BlockSpec((32, BLK), lambda j: (0, j)), pl.BlockSpec((1, BLK), lambda j: (0, j))],
        out_specs=[pl.BlockSpec((p, BLK), lambda j: (0, j))],
        out_shape=[jax.ShapeDtypeStruct((p, D), F32)],
        scratch=[pltpu.VMEM((p + 32, BLK), F32)],
        args=(ag, ag, w32, b), comm=comm)


def _mixer_fwd(ao, c0, gates, h0p, wa, wc, wo, vecs):
    p = ao.shape[0]
    tm = _row_tile(p)

    def body(ao_ref, c0_ref, gt_ref, h_ref, wa_ref, wc_ref, wo_ref, v_ref,
             c1_ref, at_ref, cv_ref, mg_ref, mix_ref, h1_ref, n2_ref):
        i = pl.program_id(0)
        c1 = _lnsilu(c0_ref[...], v_ref[0:1, :], v_ref[1:2, :]).astype(BF16)
        c1_ref[...] = c1
        attn = _dot(ao_ref[...], wa_ref[...])
        conv = _dot(c1, wc_ref[...]) + v_ref[2:3, :]
        at_ref[...] = attn.astype(BF16)
        cv_ref[...] = conv.astype(BF16)
        merged = (jax.nn.sigmoid(gt_ref[:, 0:D].astype(F32)) * attn
                  + jax.nn.sigmoid(gt_ref[:, D:2 * D].astype(F32)) * conv).astype(BF16)
        mg_ref[...] = merged
        mix = _dot(merged, wo_ref[...])
        mix_ref[...] = mix
        h1 = jnp.where(_rows(i, tm) >= PAD, h_ref[...] + _rms(mix, v_ref[3:4, :]), 0.0)
        h1_ref[...] = h1
        n2_ref[...] = _rms(h1, v_ref[4:5, :]).astype(BF16)

    def row(w):
        return pl.BlockSpec((tm, w), lambda i: (i, 0))

    return pl.pallas_call(
        body, name="mixer_fwd", grid=(p // tm,),
        in_specs=[row(D), row(D), row(2 * D), row(D), VM, VM, VM, VM],
        out_specs=[row(D)] * 7,
        out_shape=[jax.ShapeDtypeStruct((p, D), t) for t in (BF16, BF16, BF16, BF16, F32, F32, BF16)],
        compiler_params=_cparams("parallel"),
    )(ao, c0, gates, h0p, wa, wc, wo, vecs)


def _mm_nt(a, w_t, name):
    p, k = a.shape
    n = w_t.shape[0]
    tm = _row_tile(p)
    ch = 512

    def body(a_ref, w_ref, o_ref):
        a_v = a_ref[...]
        for c0 in range(0, n, ch):
            o_ref[:, c0:c0 + ch] = _dot_nt(a_v, w_ref[c0:c0 + ch, :]).astype(BF16)

    return pl.pallas_call(
        body, name=name, grid=(p // tm,),
        in_specs=[pl.BlockSpec((tm, k), lambda i: (i, 0)), VM],
        out_specs=pl.BlockSpec((tm, n), lambda i: (i, 0)),
        out_shape=jax.ShapeDtypeStruct((p, n), BF16),
        compiler_params=_cparams("parallel"),
    )(a, w_t)


def _conv3(xp_ref, w_ref, r0):
    return (w_ref[0:1, :] * xp_ref[r0 + 6:r0 + 6 + BLK, :] + w_ref[1:2, :] * xp_ref[r0 + 7:r0 + 7 + BLK, :]
            + w_ref[2:3, :] * xp_ref[r0 + 8:r0 + 8 + BLK, :])


def _ffn_slab_specs(p):
    ncol = FFN // BLK
    return [pl.BlockSpec((p, BLK), lambda j: (0, j)), pl.BlockSpec((p, BLK), lambda j: (0, ncol + j)),
            pl.BlockSpec((FFN_K, BLK), lambda j: (0, j)), pl.BlockSpec((FFN_K, BLK), lambda j: (0, ncol + j)),
            pl.BlockSpec((1, BLK), lambda j: (0, j)), pl.BlockSpec((1, BLK), lambda j: (0, ncol + j))]


def _fill_shifted(dst, src_ref, nch):
    dst[0:8, :] = jnp.zeros((8, BLK), F32)
    for ci in range(nch):
        dst[8 + BLK * ci:8 + BLK * (ci + 1), :] = src_ref[BLK * ci:BLK * (ci + 1), :].astype(F32)


def _ffn_act(u0, fw, fb):
    p = u0.shape[0]
    nch = p // BLK

    def body(g_ref, v_ref, wg_ref, wv_ref, bg_ref, bv_ref, o_ref, dv_ref, dg_ref, xg, xv):
        _fill_shifted(xg, g_ref, nch)
        _fill_shifted(xv, v_ref, nch)
        for ci in range(nch):
            r0 = BLK * ci
            ug = _conv3(xg, wg_ref, r0) + bg_ref[...]
            uv = _conv3(xv, wv_ref, r0) + bv_ref[...]
            sg = jax.nn.sigmoid(ug)
            silu = ug * sg
            o_ref[r0:r0 + BLK, :] = (silu * uv).astype(BF16)
            dv_ref[r0:r0 + BLK, :] = silu.astype(BF16)
            dg_ref[r0:r0 + BLK, :] = (uv * (sg * (1.0 + ug * (1.0 - sg)))).astype(BF16)

    slab = pl.BlockSpec((p, BLK), lambda j: (0, j))
    return pl.pallas_call(
        body, name="ffn_act", grid=(FFN // BLK,),
        in_specs=_ffn_slab_specs(p),
        out_specs=[slab] * 3,
        out_shape=[jax.ShapeDtypeStruct((p, FFN), BF16)] * 3,
        scratch_shapes=[pltpu.VMEM((p + 8, BLK), F32)] * 2,
        compiler_params=_cparams("parallel"),
    )(u0, u0, fw, fw, fb, fb)


def _ffn_down_loss(act, wd, h1, tgt, gain):
    p = act.shape[0]
    tm = _row_tile(p)
    k = tm // BLK

    def body(*refs):
        a_ref, w_ref, h_ref = refs[:3]
        t_refs = refs[3:3 + k]
        g_ref, df_ref, da_ref, dy_ref, acc_ref = refs[3 + k:]
        i = pl.program_id(0)

        @pl.when(i == 0)
        def _():
            acc_ref[...] = jnp.zeros_like(acc_ref)

        ffn = _dot(a_ref[...], w_ref[...])
        t = jnp.concatenate([t_ref[...] for t_ref in t_refs], axis=0) if k > 1 else t_refs[0][...]
        diff = jnp.where(_rows(i, tm) >= BLK, h_ref[...] + _rms(ffn, g_ref[...]) - t, 0.0)
        dy = diff * (1.0 / D)
        dffn, dg = _rms_bwd(ffn, g_ref[...], dy)
        acc_ref[0:1, :] += dg
        acc_ref[1:2, :] += jnp.sum(diff * diff, axis=0, keepdims=True) * (0.5 / D)
        dy_ref[...] = dy
        dfb = dffn.astype(BF16)
        df_ref[...] = dfb
        for c0 in range(0, FFN, 256):
            da_ref[:, c0:c0 + 256] = _dot_nt(dfb, w_ref[c0:c0 + 256, :]).astype(BF16)

    def row(w):
        return pl.BlockSpec((tm, w), lambda i: (i, 0))

    return pl.pallas_call(
        body, name="ffn_down_loss", grid=(p // tm,),
        in_specs=[row(FFN), VM, row(D)] + _token_specs(tm) + [VM],
        out_specs=[row(D), row(FFN), row(D), pl.BlockSpec((8, D), lambda i: (0, 0))],
        out_shape=[jax.ShapeDtypeStruct((p, D), BF16), jax.ShapeDtypeStruct((p, FFN), BF16),
                   jax.ShapeDtypeStruct((p, D), F32), jax.ShapeDtypeStruct((8, D), F32)],
        compiler_params=_cparams("arbitrary"),
    )(act, wd, h1, *([tgt] * k), gain)


def _mm_tn(pieces, b, name, col_sums=False, comm=None):
    p, n = b.shape
    tk = 256
    nblk = [a.shape[1] // tk for a in pieces]
    offs = [sum(nblk[:q]) for q in range(len(pieces))]
    total = sum(nblk)
    npc = len(pieces)

    def body(*refs):
        a_refs, b_ref, o_ref = refs[:npc], refs[npc], refs[npc + 1]
        i = pl.program_id(0)
        for q, a_ref in enumerate(a_refs):
            @pl.when(jnp.logical_and(i >= offs[q], i < offs[q] + nblk[q]))
            def _(a_ref=a_ref):
                a_v = a_ref[...]
                o_ref[...] = _dot_tn(a_v, b_ref[...]).astype(BF16)
                if col_sums:
                    refs[npc + 2][...] = jnp.sum(a_v.astype(F32), axis=0, keepdims=True)

    def a_spec(q):
        return pl.BlockSpec((p, tk), lambda i: (0, jnp.clip(i - offs[q], 0, nblk[q] - 1)))

    out_specs = [pl.BlockSpec((tk, n), lambda i: (i, 0))]
    out_shape = [jax.ShapeDtypeStruct((total * tk, n), BF16)]
    if col_sums:
        out_specs.append(pl.BlockSpec((1, tk), lambda i: (0, i)))
        out_shape.append(jax.ShapeDtypeStruct((1, total * tk), F32))
    res, sent = _call(
        body, name=name, grid=(total,),
        in_specs=[a_spec(q) for q in range(npc)] + [VM],
        out_specs=out_specs, out_shape=out_shape, args=(*pieces, b), comm=comm)
    res = res if col_sums else res[0]
    return res if comm is None else (res, sent)


def _ffn_act_bwd(u0, dact, dact_dg, dact_dv, fw, act, dffn, comm=None):
    p = u0.shape[0]
    nch = p // BLK
    ncol = FFN // BLK

    def body(g_ref, v_ref, wg_ref, wv_ref, da_ref, lg_ref, lv_ref, act_ref, df_ref,
             dg_ref, dv_ref, gwg_ref, gwv_ref, gbg_ref, gbv_ref, gwd_ref, eg, ev):
        gwd_ref[...] = _dot_tn(act_ref[...], df_ref[...]).astype(BF16)
        eg[p:p + 8, :] = jnp.zeros((8, BLK), F32)
        ev[p:p + 8, :] = jnp.zeros((8, BLK), F32)
        for ci in range(nch):
            r0 = BLK * ci
            d = da_ref[r0:r0 + BLK, :].astype(F32)
            eg[r0:r0 + BLK, :] = d * lg_ref[r0:r0 + BLK, :].astype(F32)
            ev[r0:r0 + BLK, :] = d * lv_ref[r0:r0 + BLK, :].astype(F32)
        def fold(v):
            return jnp.sum(v.reshape(BLK // 8, 8, BLK), axis=0)

        for e_s, x_ref, w_ref, d_ref, gw_ref, gb_ref in ((eg, g_ref, wg_ref, dg_ref, gwg_ref, gbg_ref),
                                                        (ev, v_ref, wv_ref, dv_ref, gwv_ref, gbv_ref)):
            sums = [jnp.zeros((8, BLK), F32) for _ in range(FFN_K + 1)]
            for ci in range(nch):
                r0 = BLK * ci
                es = [e_s[r0 + t:r0 + t + BLK, :] for t in range(FFN_K)]
                du = w_ref[2:3, :] * es[0] + w_ref[1:2, :] * es[1] + w_ref[0:1, :] * es[2]
                if ci == 0:
                    du = jnp.where(_rows(0, BLK) >= PAD, du, 0.0)
                d_ref[r0:r0 + BLK, :] = du.astype(BF16)
                x = x_ref[r0:r0 + BLK, :].astype(F32)
                for j in range(FFN_K):
                    sums[j] = sums[j] + fold(es[FFN_K - 1 - j] * x)
                sums[FFN_K] = sums[FFN_K] + fold(es[0])
            for j in range(FFN_K):
                gw_ref[j:j + 1, :] = jnp.sum(sums[j], axis=0, keepdims=True)
            gb_ref[...] = jnp.sum(sums[FFN_K], axis=0, keepdims=True)

    slab = pl.BlockSpec((p, BLK), lambda j: (0, j))
    wspec = pl.BlockSpec((FFN_K, BLK), lambda j: (0, j))
    bspec = pl.BlockSpec((1, BLK), lambda j: (0, j))
    return _call(
        body, name="ffn_act_bwd", grid=(ncol,),
        in_specs=_ffn_slab_specs(p)[:4] + [slab] * 4 + [VM],
        out_specs=[slab, slab, wspec, wspec, bspec, bspec, pl.BlockSpec((BLK, D), lambda j: (j, 0))],
        out_shape=[jax.ShapeDtypeStruct((p, FFN), BF16)] * 2 + [jax.ShapeDtypeStruct((FFN_K, FFN), F32)] * 2
        + [jax.ShapeDtypeStruct((1, FFN), F32)] * 2 + [jax.ShapeDtypeStruct((FFN, D), BF16)],
        scratch=[pltpu.VMEM((p + 8, BLK), F32)] * 2,
        args=(u0, u0, fw, fw, dact, dact_dg, dact_dv, act, dffn), comm=comm)


def _ffn_in_bwd(dug, duv, w_upt, h1, dy, gain, comm=None):
    p = h1.shape[0]
    tm = _row_tile(p)

    def body(dg_ref, dv_ref, w_ref, h_ref, dy_ref, g_ref, o_ref, acc_ref):
        i = pl.program_id(0)

        @pl.when(i == 0)
        def _():
            acc_ref[...] = jnp.zeros_like(acc_ref)

        dn = _dot(dg_ref[...], w_ref[0:FFN, :]) + _dot(dv_ref[...], w_ref[FFN:2 * FFN, :])
        dh, dg = _rms_bwd(h_ref[...], g_ref[...], dn)
        o_ref[...] = dy_ref[...] + dh
        acc_ref[0:1, :] += dg

    def row(w):
        return pl.BlockSpec((tm, w), lambda i: (i, 0))

    return _call(
        body, name="ffn_in_bwd", grid=(p // tm,),
        in_specs=[row(FFN), row(FFN), VM, row(D), row(D), VM],
        out_specs=[row(D), pl.BlockSpec((8, D), lambda i: (0, 0))],
        out_shape=[jax.ShapeDtypeStruct((p, D), F32), jax.ShapeDtypeStruct((8, D), F32)],
        sem="arbitrary", args=(dug, duv, w_upt, h1, dy, gain), comm=comm)


def _mixer_bwd(dh1, mix, attn, conv, gates, c0, wa, wc, wo, vecs, comm=None):
    p = dh1.shape[0]
    tm = _row_tile(p)

    def body(dh_ref, mix_ref, at_ref, cv_ref, gt_ref, c0_ref, wa_ref, wc_ref, wo_ref, v_ref,
             dmix_ref, dat_ref, dcv_ref, dgt_ref, dao_ref, dc0_ref, acc_ref):
        i = pl.program_id(0)

        @pl.when(i == 0)
        def _():
            acc_ref[...] = jnp.zeros_like(acc_ref)

        dmix, dgp = _rms_bwd(mix_ref[...], v_ref[3:4, :], dh_ref[...])
        dmix = dmix.astype(BF16)
        dmix_ref[...] = dmix
        dmg = _dot_nt(dmix, wo_ref[...])
        sa = jax.nn.sigmoid(gt_ref[:, 0:D].astype(F32))
        sc = jax.nn.sigmoid(gt_ref[:, D:2 * D].astype(F32))
        dat = dmg * sa
        dcv = dmg * sc
        dgt_ref[:, 0:D] = (dmg * at_ref[...].astype(F32) * sa * (1.0 - sa)).astype(BF16)
        dgt_ref[:, D:2 * D] = (dmg * cv_ref[...].astype(F32) * sc * (1.0 - sc)).astype(BF16)
        datb = dat.astype(BF16)
        dcvb = dcv.astype(BF16)
        dat_ref[...] = datb
        dcv_ref[...] = dcvb
        dao_ref[...] = _dot_nt(datb, wa_ref[...]).astype(BF16)
        dc1 = _dot_nt(dcvb, wc_ref[...])
        dc0, dlg, dlb = _lnsilu_bwd(c0_ref[...], v_ref[0:1, :], v_ref[1:2, :], dc1)
        dc0_ref[...] = dc0
        acc_ref[0:1, :] += dgp
        acc_ref[1:2, :] += jnp.sum(dcv, axis=0, keepdims=True)
        acc_ref[2:3, :] += dlg
        acc_ref[3:4, :] += dlb

    def row(w):
        return pl.BlockSpec((tm, w), lambda i: (i, 0))

    return _call(
        body, name="mixer_bwd", grid=(p // tm,),
        in_specs=[row(D), row(D), row(D), row(D), row(2 * D), row(D), VM, VM, VM, VM],
        out_specs=[row(D), row(D), row(D), row(2 * D), row(D), row(D), pl.BlockSpec((8, D), lambda i: (0, 0))],
        out_shape=[jax.ShapeDtypeStruct((p, D), BF16)] * 3 + [jax.ShapeDtypeStruct((p, 2 * D), BF16),
                                                             jax.ShapeDtypeStruct((p, D), BF16),
                                                             jax.ShapeDtypeStruct((p, D), F32),
                                                             jax.ShapeDtypeStruct((8, D), F32)],
        sem="arbitrary", args=(dh1, mix, attn, conv, gates, c0, wa, wc, wo, vecs), comm=comm)


def _conv31_bwd(ag, dc0, w32, tn_pairs, comm=None):
    p = ag.shape[0]
    nch = p // BLK
    npair = len(tn_pairs)

    def body(*refs):
        a_ref, g_ref, dc_ref, w_ref = refs[:4]
        tn_a, tn_b = refs[4:4 + npair], refs[4 + npair:4 + 2 * npair]
        da_ref, dg_ref, gw_ref, gb_ref = refs[4 + 2 * npair:8 + 2 * npair]
        tn_o = refs[8 + 2 * npair:8 + 3 * npair]
        gp, dp = refs[8 + 3 * npair:]
        for ta, tb, to in zip(tn_a, tn_b, tn_o):
            to[...] = _dot_tn(ta[...], tb[...]).astype(BF16)
        gp[0:32, :] = jnp.zeros((32, BLK), F32)
        dp[p:p + 32, :] = jnp.zeros((32, BLK), F32)
        bsum = jnp.zeros((BLK, BLK), F32)
        for ci in range(nch):
            r0 = BLK * ci
            glu = a_ref[r0:r0 + BLK, :].astype(F32) * jax.nn.sigmoid(g_ref[r0:r0 + BLK, :].astype(F32))
            if ci == 0:
                glu = jnp.where(_rows(0, BLK) >= PAD, glu, 0.0)
            gp[32 + r0:32 + r0 + BLK, :] = glu
            d = dc_ref[r0:r0 + BLK, :]
            dp[r0:r0 + BLK, :] = d
            bsum = bsum + d
        gb_ref[...] = jnp.sum(bsum, axis=0, keepdims=True)
        for ci in range(nch):
            r0 = BLK * ci
            acc = jnp.zeros((BLK, BLK), F32)
            for j in range(CONV_K):
                acc = acc + w_ref[j:j + 1, :] * dp[r0 + 30 - j:r0 + 30 - j + BLK, :]
            if ci == 0:
                acc = jnp.where(_rows(0, BLK) >= PAD, acc, 0.0)
            a = a_ref[r0:r0 + BLK, :].astype(F32)
            sg = jax.nn.sigmoid(g_ref[r0:r0 + BLK, :].astype(F32))
            da_ref[r0:r0 + BLK, :] = (acc * sg).astype(BF16)
            dg_ref[r0:r0 + BLK, :] = (acc * a * sg * (1.0 - sg)).astype(BF16)
        sub = BLK // 2
        accs = [jnp.zeros((8, BLK), F32) for _ in range(CONV_K)]
        for r0 in range(0, p, sub):
            d = dp[r0:r0 + sub, :]
            for j in range(CONV_K):
                prod = d * gp[r0 + j + 2:r0 + j + 2 + sub, :]
                accs[j] = accs[j] + jnp.sum(prod.reshape(sub // 8, 8, BLK), axis=0)
        for j in range(CONV_K):
            gw_ref[j:j + 1, :] = jnp.sum(accs[j], axis=0, keepdims=True)
        gw_ref[CONV_K:32, :] = jnp.zeros((32 - CONV_K, BLK), F32)

    slab = pl.BlockSpec((p, BLK), lambda j: (0, j))
    return _call(
        body, name="conv31_bwd", grid=(D // BLK,),
        in_specs=[slab, pl.BlockSpec((p, BLK), lambda j: (0, 8 + j)), slab, pl.BlockSpec((32, BLK), lambda j: (0, j))]
        + [slab] * npair + [VM] * npair,
        out_specs=[slab, slab, pl.BlockSpec((32, BLK), lambda j: (0, j)), pl.BlockSpec((1, BLK), lambda j: (0, j))]
        + [pl.BlockSpec((BLK, D), lambda j: (j, 0))] * npair,
        out_shape=[jax.ShapeDtypeStruct((p, D), BF16)] * 2 + [jax.ShapeDtypeStruct((32, D), F32),
                                                             jax.ShapeDtypeStruct((1, D), F32)]
        + [jax.ShapeDtypeStruct((D, D), BF16)] * npair,
        scratch=[pltpu.VMEM((p + 32, BLK), F32)] * 2,
        args=(ag, ag, dc0, w32, *[a for a, _ in tn_pairs], *[b for _, b in tn_pairs]), comm=comm)


def _attn_bwd(q, kv, dao, sinks, tabs, comm=None):
    p = q.shape[0]
    nb = p // BLK

    def body(q_ref, km_ref, kp_ref, kc_ref, do_ref, sink_ref, t_ref, dqkv_ref, dsink_ref, carry, macc):
        i = pl.program_id(0)
        n = nb - 1 - i

        @pl.when(i == 0)
        def _():
            carry[...] = jnp.zeros_like(carry)
            macc[...] = jnp.zeros_like(macc)
            dsink_ref[...] = jnp.zeros_like(dsink_ref)

        lo = lax.broadcasted_iota(jnp.int32, (BLK, BLK), 1) < HEAD_DIM
        lane8 = lax.broadcasted_iota(jnp.int32, (8, BLK), 1)
        c, s1, s2 = t_ref[:, 0:128], -t_ref[:, 128:256], -t_ref[:, 256:384]
        dk = jnp.zeros((N_KEY, BLK), F32)
        dv = jnp.zeros((N_KEY, BLK), F32)
        for h in range(2):
            qs, k2, v2, bias, lok = _attn_setup(n, h, q_ref, km_ref, kp_ref, kc_ref)
            dos = _stack_heads(do_ref, h, lo)
            st = _dot_nt(k2, qs)
            dpt = _dot_nt(v2, dos)
            p_parts, ds_parts = [], []
            for g in range(8):
                cols = slice(BLK * g, BLK * (g + 1))
                pn, ps = _attn_head(st[:, cols], bias, sink_ref[0, 8 * h + g])
                dp = dpt[:, cols]
                delta = jnp.sum(pn * dp, axis=0, keepdims=True)
                ds_parts.append((pn * (dp - delta)).astype(BF16))
                p_parts.append(pn.astype(BF16))
                dsk = -jnp.sum(ps * delta, axis=1, keepdims=True)
                dsink_ref[...] += jnp.where(lane8 == 8 * h + g, dsk, 0.0)
            dst = jnp.concatenate(ds_parts, axis=1)
            pt = jnp.concatenate(p_parts, axis=1)
            dq = _dot_tn(dst, k2)
            for jp in range(4):
                lo_c = BLK * (4 * h + jp)
                dqkv_ref[:, lo_c:lo_c + BLK] = (_rope(_unstack_heads(dq, jp, lo), c, s1, s2) * SCALE).astype(BF16)
            dk2 = _dot(dst, qs)
            dv2 = _dot(pt, dos)
            dk2 = dk2 + pltpu.roll(dk2, HEAD_DIM, 1)
            dv2 = dv2 + pltpu.roll(dv2, HEAD_DIM, 1)
            own = lok if h == 0 else jnp.logical_not(lok)
            dk = jnp.where(own, dk2, dk)
            dv = jnp.where(own, dv2, dv)
        macc[:, 0:BLK] += dk[2 * BLK:N_KEY]
        macc[:, BLK:2 * BLK] += dv[2 * BLK:N_KEY]
        last = (n == 0).astype(F32)
        zpad = jnp.zeros((PAD, BLK), F32)
        dk_c = dk[BLK:2 * BLK] + carry[:, 0:BLK] + last * jnp.concatenate([zpad, macc[:, 0:BLK]], axis=0)
        dv_c = dv[BLK:2 * BLK] + carry[:, BLK:2 * BLK] + last * jnp.concatenate([zpad, macc[:, BLK:2 * BLK]], axis=0)
        carry[:, 0:BLK] = dk[0:BLK]
        carry[:, BLK:2 * BLK] = dv[0:BLK]
        dqkv_ref[:, D:D + BLK] = _rope(dk_c, c, s1, s2).astype(BF16)
        dqkv_ref[:, D + BLK:D + 2 * BLK] = dv_c.astype(BF16)

    def rev(w):
        return pl.BlockSpec((BLK, w), lambda i: (nb - 1 - i, 0))

    return _call(
        body, name="attn_bwd", grid=(nb,),
        in_specs=[rev(D),
                  pl.BlockSpec((BLK, 256), lambda i: (0, 0)),
                  pl.BlockSpec((BLK, 256), lambda i: (jnp.maximum(nb - 2 - i, 0), 0)),
                  rev(256), rev(D),
                  pl.BlockSpec(memory_space=pltpu.SMEM), rev(384)],
        out_specs=[rev(QKV_W), pl.BlockSpec((8, BLK), lambda i: (0, 0))],
        out_shape=[jax.ShapeDtypeStruct((p, QKV_W), BF16), jax.ShapeDtypeStruct((8, BLK), F32)],
        scratch=[pltpu.VMEM((BLK, 256), F32), pltpu.VMEM((N_META, 256), F32)], sem="arbitrary",
        args=(q, kv, kv, kv, dao, sinks, tabs), comm=comm)


def _in_bwd(dqkv, da, dg, dgt, w_int, h0p, dh1, gain, comm=None):
    p = h0p.shape[0]
    tm = _row_tile(p)
    nt = p // tm
    first_rows = tm - BLK

    def body(dq_ref, da_ref, dg_ref, dt_ref, w_ref, h_ref, dh_ref, g_ref, gx_ref, dm_ref, acc_ref, buf, sems):
        i = pl.program_id(0)
        slot = i % 2

        @pl.when(i == 0)
        def _():
            acc_ref[...] = jnp.zeros_like(acc_ref)

        dn = (_dot(dq_ref[...], w_ref[0:QKV_W, :]) + _dot(da_ref[...], w_ref[QKV_W:QKV_W + D, :])
              + _dot(dg_ref[...], w_ref[QKV_W + D:QKV_W + 2 * D, :]) + _dot(dt_ref[...], w_ref[QKV_W + 2 * D:IN_W, :]))
        dh, dgain = _rms_bwd(h_ref[...], g_ref[...], dn)
        dh0 = dh_ref[...] + dh
        acc_ref[0:1, :] += dgain
        buf[slot] = dh0

        @pl.when(i == 0)
        def _():
            dm_ref[...] = dh0[PAD:BLK]

        def first_copy():
            return pltpu.make_async_copy(buf.at[0, pl.ds(BLK, first_rows), :], gx_ref.at[pl.ds(0, first_rows), :], sems.at[0])

        def tile_copy(j, s):
            return pltpu.make_async_copy(buf.at[s], gx_ref.at[pl.ds(pl.multiple_of(j * tm - BLK, BLK), tm), :], sems.at[s])

        if first_rows:
            @pl.when(i == 1)
            def _():
                first_copy().wait()

        @pl.when(i >= 2)
        def _():
            tile_copy(i - 1, 1 - slot).wait()

        if first_rows:
            @pl.when(i == 0)
            def _():
                first_copy().start()

        @pl.when(i > 0)
        def _():
            tile_copy(i, slot).start()

        @pl.when(i == nt - 1)
        def _():
            tile_copy(i, slot).wait()

    def row(w):
        return pl.BlockSpec((tm, w), lambda i: (i, 0))

    return _call(
        body, name="in_bwd", grid=(nt,),
        in_specs=[row(QKV_W), row(D), row(D), row(2 * D), VM, row(D), row(D), VM],
        out_specs=[ANY, pl.BlockSpec((N_META, D), lambda i: (0, 0)), pl.BlockSpec((8, D), lambda i: (0, 0))],
        out_shape=[jax.ShapeDtypeStruct((p - BLK, D), F32), jax.ShapeDtypeStruct((N_META, D), F32),
                   jax.ShapeDtypeStruct((8, D), F32)],
        scratch=[pltpu.VMEM((2, tm, D), F32), pltpu.SemaphoreType.DMA((2,))],
        sem="arbitrary", args=(dqkv, da, dg, dgt, w_int, h0p, dh1, gain), comm=comm)


def _sum_slots(slots, name):
    r = slots.shape[0] // N_DEV
    cols = slots.shape[1]
    tr = r if r <= 352 else (r // 2 if (r // 2) % 16 == 0 else r // 3)
    steps = r // tr

    def body(*refs):
        acc = refs[0][...].astype(F32)
        for s in range(1, N_DEV):
            acc = acc + refs[s][...].astype(F32)
        refs[N_DEV][...] = acc

    return pl.pallas_call(
        body, name=name, grid=(steps,),
        in_specs=[pl.BlockSpec((tr, cols), functools.partial(lambda i, s: (s * steps + i, 0), s=s)) for s in range(N_DEV)],
        out_specs=pl.BlockSpec((tr, cols), lambda i: (i, 0)),
        out_shape=jax.ShapeDtypeStruct((r, cols), F32),
        compiler_params=_cparams("parallel"),
    )(*([slots] * N_DEV))


def _adamw_math(w, g, m, v):
    m_n = ADAM_B1 * m + (1.0 - ADAM_B1) * g
    v_n = ADAM_B2 * v + (1.0 - ADAM_B2) * jnp.square(g)
    m_hat = m_n / (1.0 - ADAM_B1 ** ADAM_STEP)
    v_hat = v_n / (1.0 - ADAM_B2 ** ADAM_STEP)
    return -ADAM_LR * (m_hat / (jnp.sqrt(v_hat) + ADAM_EPS) + ADAM_WD * w), m_n, v_n


def _sum_adamw(parts, w, m, v, name, nslots=N_DEV):
    r, cols = w.shape
    rs = r // len(parts)
    tr = rs if rs <= 352 else (rs // 2 if (rs // 2) % 16 == 0 else rs // 3)
    steps = rs // tr

    def body(*refs):
        w_ref, m_ref, v_ref, g_ref, d_ref, nm_ref, nv_ref = refs[nslots * len(parts):]
        i = pl.program_id(0)
        for q in range(len(parts)):
            @pl.when(i // steps == q)
            def _(q=q):
                g = refs[nslots * q][...].astype(F32)
                for s in range(1, nslots):
                    g = g + refs[nslots * q + s][...].astype(F32)
                g_ref[...] = g
                d_ref[...], nm_ref[...], nv_ref[...] = _adamw_math(w_ref[...], g, m_ref[...], v_ref[...])

    def slot_spec(q, s):
        return pl.BlockSpec((tr, cols), lambda i: (s * steps + jnp.clip(i - q * steps, 0, steps - 1), 0))

    spec = pl.BlockSpec((tr, cols), lambda i: (i, 0))
    return pl.pallas_call(
        body, name=name, grid=(steps * len(parts),),
        in_specs=[slot_spec(q, s) for q in range(len(parts)) for s in range(nslots)] + [spec] * 3,
        out_specs=[spec] * 4, out_shape=[jax.ShapeDtypeStruct((r, cols), F32)] * 4,
        compiler_params=_cparams("parallel"),
    )(*[a for a in parts for _ in range(nslots)], w, m, v)


def _adamw_many(ws, gs, ms, vs, name):
    n = len(ws)

    def body(*refs):
        w, g, m, v = refs[0:n], refs[n:2 * n], refs[2 * n:3 * n], refs[3 * n:4 * n]
        d, nm, nv = refs[4 * n:5 * n], refs[5 * n:6 * n], refs[6 * n:7 * n]
        for k in range(n):
            d[k][...], nm[k][...], nv[k][...] = _adamw_math(w[k][...], g[k][...], m[k][...], v[k][...])

    outs = pl.pallas_call(
        body, name=name, in_specs=[VM] * (4 * n), out_specs=[VM] * (3 * n),
        out_shape=[jax.ShapeDtypeStruct(a.shape, F32) for a in ws] * 3,
    )(*ws, *gs, *ms, *vs)
    return outs[0:n], outs[n:2 * n], outs[2 * n:3 * n]


def _rope_tables(p):
    half = ROT_DIM // 2
    lane = jnp.arange(BLK)
    seg = (lane % HEAD_DIM) // half
    inv_freq = ROPE_THETA ** (-(lane % half).astype(F32) * 2.0 / ROT_DIM)
    pos = (jnp.arange(p) - PAD).astype(F32)
    ang = pos[:, None] * inv_freq[None, :]
    cos = jnp.cos(ang)
    sin = jnp.sin(ang)
    c = jnp.where(seg[None, :] < 2, cos, 1.0)
    s1 = jnp.where(seg[None, :] == 0, -sin, 0.0)
    s2 = jnp.where(seg[None, :] == 1, sin, 0.0)
    return jnp.concatenate([c, s1, s2], axis=1).astype(F32)


def _flat_pack(parts, rows):
    flat = jnp.concatenate([a.reshape(-1).astype(F32) for a in parts])
    return jnp.pad(flat, (0, rows * D - flat.shape[0])).reshape(rows, D)


def _flat_unpack(pack, shapes):
    flat = pack.reshape(-1)
    out, off = [], 0
    for s in shapes:
        size = 1
        for e in s:
            size *= e
        out.append(flat[off:off + size].reshape(s))
        off += size
    return out


def kernel(x, meta_tokens, norm_pre_mix, norm_post_mix, w_in, b_in, attn_sinks, w_attn_proj, conv_dw_w, conv_dw_b, conv_ln_g, conv_ln_b, w_conv_proj, b_conv_proj, w_out, norm_pre_ffn, norm_post_ffn, w_up, ffn_dw_w, ffn_dw_b, w_down, loss_target, m_meta_tokens, m_norm_pre_mix, m_norm_post_mix, m_w_in, m_b_in, m_attn_sinks, m_w_attn_proj, m_conv_dw_w, m_conv_dw_b, m_conv_ln_g, m_conv_ln_b, m_w_conv_proj, m_b_conv_proj, m_w_out, m_norm_pre_ffn, m_norm_post_ffn, m_w_up, m_ffn_dw_w, m_ffn_dw_b, m_w_down, v_meta_tokens, v_norm_pre_mix, v_norm_post_mix, v_w_in, v_b_in, v_attn_sinks, v_w_attn_proj, v_conv_dw_w, v_conv_dw_b, v_conv_ln_g, v_conv_ln_b, v_w_conv_proj, v_b_conv_proj, v_w_out, v_norm_pre_ffn, v_norm_post_ffn, v_w_up, v_ffn_dw_w, v_ffn_dw_b, v_w_down):
    seq = x.shape[1]
    p = seq + BLK
    me = 4 * lax.axis_index("x") + 2 * lax.axis_index("y") + lax.axis_index("c")
    in_cols = w_in.shape[2]
    up_cols = w_up.shape[2]

    small = jnp.zeros((56, up_cols), F32)
    small = small.at[0:N_META, 0:BLK].set(meta_tokens)
    small = small.at[16:16 + CONV_K, 0:BLK].set(conv_dw_w[0])
    small = small.at[48:48 + FFN_K, :].set(ffn_dw_w[0])
    w_int, small_all = _exchange(_Both(_GatherRelay(w_in[0].T.astype(BF16)), _Gather([small])), "gather_w_in")
    small_all = small_all.reshape(N_DEV, 56, up_cols)
    meta_full = small_all[:, 0:N_META, 0:BLK].transpose(1, 0, 2).reshape(N_META, D)
    cdw = small_all[:, 16:16 + CONV_K, 0:BLK].transpose(1, 0, 2).reshape(CONV_K, D)
    cdw32 = jnp.pad(cdw, ((0, 32 - CONV_K), (0, 0)))
    fdw = small_all[:, 48:48 + FFN_K, :].transpose(1, 0, 2).reshape(FFN_K, 2 * FFN)

    tabs = _rope_tables(p)
    vecs = jnp.concatenate([conv_ln_g, conv_ln_b, b_conv_proj, norm_post_mix, norm_pre_ffn, jnp.zeros((3, D), F32)], axis=0)

    (h0p, n1, q, kv, ag, gates), (wa, wc, wo) = _in_proj(
        x[0], meta_full, norm_pre_mix, w_int, b_in, tabs,
        comm=_Gather([w_attn_proj[0].astype(BF16), w_conv_proj[0].astype(BF16), w_out[0].astype(BF16)]))
    (ao,), (w_upt,) = _attn_fwd(q, kv, attn_sinks, comm=_Gather([w_up[0].T.astype(BF16)]))
    (c0,), (wd,) = _conv31_fwd(ag, cdw32, conv_dw_b, comm=_Gather([w_down[0].astype(BF16)]))
    c1, attn, conv, merged, mix, h1, n2 = _mixer_fwd(ao, c0, gates, h0p, wa, wc, wo, vecs)
    u0 = _mm_nt(n2, w_upt, "ffn_up")
    act, dact_dv, dact_dg = _ffn_act(u0, fdw, ffn_dw_b)
    dffn, dact, dy, acc_f = _ffn_down_loss(act, wd, h1, loss_target[0], norm_post_ffn)

    (dug, duv, gfw_g, gfw_v, gfb_g, gfb_v, g_wd), _ = _ffn_act_bwd(u0, dact, dact_dg, dact_dv, fdw, act, dffn)
    g_wupt, (s_wd0,) = _mm_tn([dug, duv], n2, "grad_w_up", comm=_Scatter([g_wd], 0, 2))
    (dh1, acc_u), (s_wd1,) = _ffn_in_bwd(dug, duv, w_upt, h1, dy, norm_pre_ffn, comm=_Scatter([g_wd], 1, 2))
    (dmix, dat, dcv, dgt, dao, dc0, acc_m), (s_wup0,) = _mixer_bwd(
        dh1, mix, attn, conv, gates, c0, wa, wc, wo, vecs, comm=_Scatter([g_wupt], 0, 4))
    (da, dg, g_cdw, g_cdb, g_wo, g_wa, g_wc), (s_wup1, s_wup2, s_wup3) = _conv31_bwd(
        ag, dc0, cdw32, [(merged, dmix), (ao, dat), (c1, dcv)],
        comm=_Both(_Both(_Scatter([g_wupt], 1, 4), _Scatter([g_wupt], 2, 4)), _Scatter([g_wupt], 3, 4)))
    (dqkv, dsink), (s_wa, s_wc, s_wo) = _attn_bwd(q, kv, dao, attn_sinks, tabs, comm=_Scatter([g_wa, g_wc, g_wo]))
    loss_row = jnp.sum(acc_f[1:2, :], axis=1, keepdims=True)
    early = [loss_row, acc_m[0:1], dsink[0:1, 0:16], g_cdw[0:CONV_K], g_cdb,
             acc_m[2:3], acc_m[3:4], acc_m[1:2], acc_u[0:1], acc_f[0:1],
             jnp.concatenate([gfw_g, gfw_v], axis=1), jnp.concatenate([gfb_g, gfb_v], axis=1)]
    (g_wint, g_bin), (gathered_early,) = _mm_tn([dqkv, da, dg, dgt], n1, "grad_w_in", col_sums=True,
                                                comm=_Gather([_flat_pack(early, 64)]))
    (from_sibling,) = _exchange(_SiblingSwap(g_wint), "swap_w_in")
    (grad_x2d, dmeta, acc_i), (s_win,) = _in_bwd(dqkv, da, dg, dgt, w_int, h0p, dh1, norm_pre_mix,
                                                 comm=_ChipScatter(_pair_add(g_wint, from_sibling)))

    big = []
    for nm, parts, nslots, w, m, v, tr in (
            ("w_in", [s_win], N_CHIP, w_in, m_w_in, v_w_in, True), ("w_up", [s_wup0, s_wup1, s_wup2, s_wup3], N_DEV, w_up, m_w_up, v_w_up, True),
            ("w_attn_proj", [s_wa], N_DEV, w_attn_proj, m_w_attn_proj, v_w_attn_proj, False),
            ("w_conv_proj", [s_wc], N_DEV, w_conv_proj, m_w_conv_proj, v_w_conv_proj, False),
            ("w_out", [s_wo], N_DEV, w_out, m_w_out, v_w_out, False),
            ("w_down", [s_wd0, s_wd1], N_DEV, w_down, m_w_down, v_w_down, False)):
        ins = [a[0].T if tr else a[0] for a in (w, m, v)]
        big.append(tuple((o.T if tr else o)[None] for o in _sum_adamw(parts, *ins, "update_" + nm, nslots)))

    late = [dmeta, acc_i[0:1], g_bin]
    (gathered_late,) = _exchange(_Gather([_flat_pack(late, 24)]), "gather_small_grads")
    g_meta, g_npm, g_bi = _flat_unpack(_sum_slots(gathered_late, "sum_late_grads"), [a.shape for a in late])
    tot = _flat_unpack(_sum_slots(gathered_early, "sum_small_grads"), [a.shape for a in early])
    (loss, g_nqm, g_sk, g_cw, g_cb, g_lg, g_lb, g_bc, g_npf, g_nqf, g_fw, g_fb) = tot
    loss = loss.reshape(())
    g_meta = lax.dynamic_slice_in_dim(g_meta, me * BLK, BLK, axis=1)
    g_cw = lax.dynamic_slice_in_dim(g_cw, me * BLK, BLK, axis=1)[None]
    g_fw = lax.dynamic_slice_in_dim(g_fw, me * up_cols, up_cols, axis=1)[None]

    sm_w = [meta_tokens, norm_pre_mix, norm_post_mix, b_in, attn_sinks, conv_dw_w, conv_dw_b, conv_ln_g, conv_ln_b,
            b_conv_proj, norm_pre_ffn, norm_post_ffn, ffn_dw_w, ffn_dw_b]
    sm_g = [g_meta, g_npm, g_nqm, g_bi, g_sk, g_cw, g_cb, g_lg, g_lb, g_bc, g_npf, g_nqf, g_fw, g_fb]
    sm_m = [m_meta_tokens, m_norm_pre_mix, m_norm_post_mix, m_b_in, m_attn_sinks, m_conv_dw_w, m_conv_dw_b, m_conv_ln_g,
            m_conv_ln_b, m_b_conv_proj, m_norm_pre_ffn, m_norm_post_ffn, m_ffn_dw_w, m_ffn_dw_b]
    sm_v = [v_meta_tokens, v_norm_pre_mix, v_norm_post_mix, v_b_in, v_attn_sinks, v_conv_dw_w, v_conv_dw_b, v_conv_ln_g,
            v_conv_ln_b, v_b_conv_proj, v_norm_pre_ffn, v_norm_post_ffn, v_ffn_dw_w, v_ffn_dw_b]
    swap = lambda a: jnp.transpose(a, (1, 0, 2)) if a.ndim == 3 else a
    sm_d, sm_nm, sm_nv = ([swap(o) for o in outs] for outs in
                          _adamw_many(*([swap(a) for a in group] for group in (sm_w, sm_g, sm_m, sm_v)), "adamw_small"))

    order = ["meta_tokens", "norm_pre_mix", "norm_post_mix", "w_in", "b_in", "attn_sinks", "w_attn_proj", "conv_dw_w",
             "conv_dw_b", "conv_ln_g", "conv_ln_b", "w_conv_proj", "b_conv_proj", "w_out", "norm_pre_ffn", "norm_post_ffn",
             "w_up", "ffn_dw_w", "ffn_dw_b", "w_down"]
    small_names = ["meta_tokens", "norm_pre_mix", "norm_post_mix", "b_in", "attn_sinks", "conv_dw_w", "conv_dw_b", "conv_ln_g",
                   "conv_ln_b", "b_conv_proj", "norm_pre_ffn", "norm_post_ffn", "ffn_dw_w", "ffn_dw_b"]
    big_names = ["w_in", "w_up", "w_attn_proj", "w_conv_proj", "w_out", "w_down"]
    table = {}
    for k, nm in enumerate(small_names):
        table[nm] = (sm_g[k], sm_d[k], sm_nm[k], sm_nv[k])
    for k, nm in enumerate(big_names):
        table[nm] = big[k]
    grad_x = grad_x2d[None]
    outs = [loss, grad_x]
    for field in range(4):
        outs += [table[nm][field] for nm in order]
    return tuple(outs)
```

```python
import functools

import jax
import jax.numpy as jnp
from jax import lax
from jax.experimental import pallas as pl
from jax.experimental.pallas import tpu as pltpu

F32 = jnp.float32
BF16 = jnp.bfloat16
MESH = pl.DeviceIdType.MESH

D = 1024
HEAD_DIM = 64
N_META = 16
BLK = 128
PAD = BLK - N_META
CONV_K = 31
FFN = 2816
FFN_K = 3
QKV_W = 1280
IN_W = 5376
ROT_DIM = 16
ROPE_THETA = 500000.0
RMS_EPS = 1e-6
LN_EPS = 1e-5
NEG_INF = -1e30
SCALE = HEAD_DIM ** -0.5
N_DEV = 8

ADAM_LR = 0.001
ADAM_B1 = 0.9
ADAM_B2 = 0.999
ADAM_EPS = 1e-08
ADAM_WD = 0.01
ADAM_STEP = 10

VMEM_BYTES_V7X = 64 * 1024 * 1024
VMEM_LIMIT = VMEM_BYTES_V7X - 8 * 1024 * 1024

NT = (((1,), (1,)), ((), ()))
TN = (((0,), (0,)), ((), ()))
VM = pl.BlockSpec(memory_space=pltpu.VMEM)
ANY = pl.BlockSpec(memory_space=pl.ANY)


def _cparams(*sem):
    return pltpu.CompilerParams(dimension_semantics=sem or None, vmem_limit_bytes=VMEM_LIMIT)


def _row_tile(p):
    return 384 if p % 384 == 0 else 128


def _dot(a, b):
    return jnp.dot(a, b, preferred_element_type=F32)


def _dot_nt(a, b):
    return lax.dot_general(a, b, NT, preferred_element_type=F32)


def _dot_tn(a, b):
    return lax.dot_general(a, b, TN, preferred_element_type=F32)


def _rms(x, g):
    return x * lax.rsqrt(jnp.mean(x * x, axis=-1, keepdims=True) + RMS_EPS) * g


def _lnsilu(x, g, b):
    mu = jnp.mean(x, axis=-1, keepdims=True)
    var = jnp.mean(jnp.square(x - mu), axis=-1, keepdims=True)
    z = (x - mu) * lax.rsqrt(var + LN_EPS) * g + b
    return z * jax.nn.sigmoid(z)


def _rms_bwd(x, g, dy):
    r = lax.rsqrt(jnp.mean(x * x, axis=-1, keepdims=True) + RMS_EPS)
    xn = x * r
    u = dy * g
    dg = jnp.sum(dy * xn, axis=0, keepdims=True)
    dx = r * (u - xn * jnp.mean(u * xn, axis=-1, keepdims=True))
    return dx, dg


def _lnsilu_bwd(x, g, b, dout):
    mu = jnp.mean(x, axis=-1, keepdims=True)
    xc = x - mu
    rs = lax.rsqrt(jnp.mean(xc * xc, axis=-1, keepdims=True) + LN_EPS)
    yh = xc * rs
    z = yh * g + b
    sg = jax.nn.sigmoid(z)
    dz = dout * (sg * (1.0 + z * (1.0 - sg)))
    dg = jnp.sum(dz * yh, axis=0, keepdims=True)
    db = jnp.sum(dz, axis=0, keepdims=True)
    dyh = dz * g
    dx = rs * (dyh - jnp.mean(dyh, axis=-1, keepdims=True) - yh * jnp.mean(dyh * yh, axis=-1, keepdims=True))
    return dx, dg, db


def _rope(v, c, s1, s2):
    return v * c + pltpu.roll(v, BLK - 8, 1) * s1 + pltpu.roll(v, 8, 1) * s2


def _rows(i, tm):
    return i * tm + lax.broadcasted_iota(jnp.int32, (tm, 1), 0)


def _place():
    return lax.axis_index("x"), lax.axis_index("y"), lax.axis_index("c")


class _Gather:
    def __init__(self, arrs, part=0, nparts=1, into=None):
        n = len(arrs)
        self.n, self.part, self.nparts = n, part, nparts
        self.ins = list(arrs) + list(into or [])
        self.alias = {n + k: k for k in range(n)} if into else {}
        self.out_shape = [jax.ShapeDtypeStruct((N_DEV * a.shape[0], a.shape[1]), a.dtype) for a in arrs]
        self.scratch = [pltpu.SemaphoreType.DMA((n, 7)), pltpu.SemaphoreType.DMA((n, 7)), pltpu.SemaphoreType.DMA((n,))]

    def _parts(self, ins, outs, sems):
        send_sems, recv_sems, local_sems = sems
        n = self.n
        x, y, c = _place()
        me, sibling = (x, y, c), (x, y, 1 - c)
        chips = [(1 - x, y), (x, 1 - y), (1 - x, 1 - y)]

        def share(a):
            rs = self.ins[a].shape[0] // self.nparts
            return self.part * rs, rs

        def rows(a, p):
            lo, rs = share(a)
            align = 16 if self.ins[a].dtype == BF16 else 8
            start = pl.multiple_of((4 * p[0] + 2 * p[1] + p[2]) * self.ins[a].shape[0] + lo, align)
            return outs[a].at[pl.ds(start, rs), :]

        ins = [ins[a].at[pl.ds(share(a)[0], share(a)[1]), :] for a in range(n)]

        def copy(a, k, block, to, src=None):
            return pltpu.make_async_remote_copy(
                src_ref=rows(a, block) if src is None else src, dst_ref=rows(a, block),
                send_sem=send_sems.at[a, k], recv_sem=recv_sems.at[a, k], device_id=to, device_id_type=MESH)

        mine = [pltpu.make_async_copy(ins[a], rows(a, me), local_sems.at[a]) for a in range(n)]
        first = []
        for a in range(n):
            first.append(copy(a, 0, me, sibling, src=ins[a]))
            first += [copy(a, 1 + j, me, (*chip, c), src=ins[a]) for j, chip in enumerate(chips)]
        return n, c, me, sibling, chips, copy, mine, first

    def start(self, ins, outs, sems):
        *_, mine, first = self._parts(ins, outs, sems)
        for cp in mine + first:
            cp.start()

    def finish(self, ins, outs, sems):
        n, c, me, sibling, chips, copy, mine, first = self._parts(ins, outs, sems)
        passed = []
        for j, chip in enumerate(chips):
            for a in range(n):
                copy(a, 1 + j, (*chip, c), me).wait_recv()
                fwd = copy(a, 4 + j, (*chip, c), sibling)
                fwd.start()
                passed.append(fwd)
        for a in range(n):
            copy(a, 0, sibling, me).wait_recv()
            for j, chip in enumerate(chips):
                copy(a, 4 + j, (*chip, 1 - c), me).wait_recv()
        for cp in first + passed:
            cp.wait_send()
        for cp in mine:
            cp.wait()


class _GatherRelay:
    N_COPY = 13

    def __init__(self, arr):
        self.ins = [arr]
        self.r = arr.shape[0]
        self.out_shape = [jax.ShapeDtypeStruct((N_DEV * self.r, arr.shape[1]), arr.dtype)]
        self.scratch = [pltpu.SemaphoreType.DMA((self.N_COPY,)), pltpu.SemaphoreType.DMA((self.N_COPY,)),
                        pltpu.SemaphoreType.DMA]

    def _parts(self, ins, outs, sems):
        send_sems, recv_sems, local_sem = sems
        x, y, c = _place()
        r, half = self.r, self.r // 2
        out = outs[0]
        me, sib, xn, yn, dg = (x, y, c), (x, y, 1 - c), (1 - x, y, c), (x, 1 - y, c), (1 - x, 1 - y, c)
        sx, sy, sd = (1 - x, y, 1 - c), (x, 1 - y, 1 - c), (1 - x, 1 - y, 1 - c)
        lo, hi = (0, half), (half, half)

        def rows(p, part=(0, r)):
            return out.at[pl.ds(pl.multiple_of((4 * p[0] + 2 * p[1] + p[2]) * r + part[0], 16), part[1]), :]

        def own(part):
            return ins[0].at[pl.ds(part[0], part[1]), :]

        def copy(k, dev_rows, to, src=None):
            return pltpu.make_async_remote_copy(
                src_ref=dev_rows if src is None else src, dst_ref=dev_rows,
                send_sem=send_sems.at[k], recv_sem=recv_sems.at[k], device_id=to, device_id_type=MESH)

        mine = pltpu.make_async_copy(ins[0], rows(me), local_sem)
        first = [copy(0, rows(me), sib, src=ins[0]),
                 copy(1, rows(me, lo), xn, src=own(lo)), copy(3, rows(me, hi), yn, src=own(hi)),
                 copy(2, rows(me, hi), xn, src=own(hi)), copy(4, rows(me, lo), yn, src=own(lo))]
        arrive = {0: rows(sib), 1: rows(xn, lo), 2: rows(xn, hi), 3: rows(yn, hi), 4: rows(yn, lo),
                  5: rows(dg, lo), 6: rows(dg, hi), 7: rows(sx, lo), 8: rows(sx, hi), 9: rows(sy, hi),
                  10: rows(sy, lo), 11: rows(sd, lo), 12: rows(sd, hi)}
        relay = {1: [(5, rows(xn, lo), yn), (7, rows(xn, lo), sib)], 3: [(6, rows(yn, hi), xn), (9, rows(yn, hi), sib)],
                 2: [(8, rows(xn, hi), sib)], 4: [(10, rows(yn, lo), sib)],
                 5: [(11, rows(dg, lo), sib)], 6: [(12, rows(dg, hi), sib)]}
        return copy, mine, first, arrive, relay, me

    def start(self, ins, outs, sems):
        _, mine, first, _, _, _ = self._parts(ins, outs, sems)
        for cp in [mine] + first:
            cp.start()

    def finish(self, ins, outs, sems):
        copy, mine, first, arrive, relay, me = self._parts(ins, outs, sems)
        passed = []
        for k in (1, 3, 2, 4, 5, 6):
            copy(k, arrive[k], me).wait_recv()
            for k2, dev_rows, to in relay[k]:
                fwd = copy(k2, dev_rows, to)
                fwd.start()
                passed.append(fwd)
        for k in (0, 7, 8, 9, 10, 11, 12):
            copy(k, arrive[k], me).wait_recv()
        for cp in first + passed:
            cp.wait_send()
        mine.wait()


FLIPS = [(0, 0, 1), (1, 0, 0), (0, 1, 0), (1, 1, 0), (1, 0, 1), (0, 1, 1), (1, 1, 1)]


class _Scatter:
    def __init__(self, arrs, part=0, nparts=1):
        self.ins = list(arrs)
        self.part, self.nparts = part, nparts
        n = len(arrs)
        self.out_shape = [jax.ShapeDtypeStruct((a.shape[0] // nparts, a.shape[1]), a.dtype) for a in arrs]
        self.scratch = [pltpu.SemaphoreType.DMA((n, 7)), pltpu.SemaphoreType.DMA((n, 7)), pltpu.SemaphoreType.DMA((n,))]

    def _parts(self, ins, outs, sems):
        send_sems, recv_sems, local_sems = sems
        n = len(ins)
        x, y, c = _place()
        me = 4 * x + 2 * y + c

        def flip(v, f):
            return 1 - v if f else v

        def src(a, idx):
            r = self.ins[a].shape[0] // N_DEV
            rs = r // self.nparts
            return ins[a].at[pl.ds(pl.multiple_of(idx * r + self.part * rs, 16), rs), :]

        def dst(a, idx):
            rs = self.ins[a].shape[0] // N_DEV // self.nparts
            return outs[a].at[pl.ds(pl.multiple_of(idx * rs, 16), rs), :]

        mine = [pltpu.make_async_copy(src(a, me), dst(a, me), local_sems.at[a]) for a in range(n)]
        sends, recvs = [], []
        for k, f in enumerate(FLIPS):
            peer = (flip(x, f[0]), flip(y, f[1]), flip(c, f[2]))
            pidx = 4 * peer[0] + 2 * peer[1] + peer[2]
            for a in range(n):
                sends.append(pltpu.make_async_remote_copy(
                    src_ref=src(a, pidx), dst_ref=dst(a, me),
                    send_sem=send_sems.at[a, k], recv_sem=recv_sems.at[a, k], device_id=peer, device_id_type=MESH))
                recvs.append(functools.partial(
                    pltpu.make_async_remote_copy,
                    src_ref=src(a, pidx), dst_ref=dst(a, pidx),
                    send_sem=send_sems.at[a, k], recv_sem=recv_sems.at[a, k], device_id=peer, device_id_type=MESH))
        return mine, sends, recvs

    def start(self, ins, outs, sems):
        mine, sends, _ = self._parts(ins, outs, sems)
        for cp in mine + sends:
            cp.start()

    def finish(self, ins, outs, sems):
        mine, sends, recvs = self._parts(ins, outs, sems)
        for make in recvs:
            make().wait_recv()
        for cp in sends:
            cp.wait_send()
        for cp in mine:
            cp.wait()


N_CHIP = 4


class _SiblingSwap:
    def __init__(self, arr):
        self.ins = [arr]
        self.r = arr.shape[0] // N_DEV
        self.out_shape = [jax.ShapeDtypeStruct((N_CHIP * self.r, arr.shape[1]), arr.dtype)]
        self.scratch = [pltpu.SemaphoreType.DMA((N_CHIP,)), pltpu.SemaphoreType.DMA((N_CHIP,))]

    def _copies(self, ins, outs, sems):
        send_sems, recv_sems = sems
        x, y, c = _place()
        r = self.r
        return [pltpu.make_async_remote_copy(
            src_ref=ins[0].at[pl.ds(pl.multiple_of((2 * j + 1 - c) * r, 16), r), :],
            dst_ref=outs[0].at[pl.ds(j * r, r), :],
            send_sem=send_sems.at[j], recv_sem=recv_sems.at[j], device_id=(x, y, 1 - c), device_id_type=MESH)
            for j in range(N_CHIP)]

    def start(self, ins, outs, sems):
        for cp in self._copies(ins, outs, sems):
            cp.start()

    def finish(self, ins, outs, sems):
        for cp in self._copies(ins, outs, sems):
            cp.wait()


class _ChipScatter:
    def __init__(self, arr):
        self.ins = [arr]
        self.r = arr.shape[0] // N_CHIP
        self.out_shape = [jax.ShapeDtypeStruct(arr.shape, arr.dtype)]
        self.scratch = [pltpu.SemaphoreType.DMA((3,)), pltpu.SemaphoreType.DMA((3,)), pltpu.SemaphoreType.DMA]

    def _parts(self, ins, outs, sems):
        send_sems, recv_sems, local_sem = sems
        x, y, c = _place()
        r = self.r
        my_chip = 2 * x + y

        def rows(ref, j):
            return ref.at[pl.ds(pl.multiple_of(j * r, 16), r), :]

        mine = pltpu.make_async_copy(rows(ins[0], my_chip), rows(outs[0], my_chip), local_sem)
        sends, recvs = [], []
        for k, (fx, fy) in enumerate(((1, 0), (0, 1), (1, 1))):
            px, py = (1 - x if fx else x), (1 - y if fy else y)
            peer_chip = 2 * px + py
            sends.append(pltpu.make_async_remote_copy(
                src_ref=rows(ins[0], peer_chip), dst_ref=rows(outs[0], my_chip),
                send_sem=send_sems.at[k], recv_sem=recv_sems.at[k], device_id=(px, py, c), device_id_type=MESH))
            recvs.append(functools.partial(
                pltpu.make_async_remote_copy,
                src_ref=rows(ins[0], peer_chip), dst_ref=rows(outs[0], peer_chip),
                send_sem=send_sems.at[k], recv_sem=recv_sems.at[k], device_id=(px, py, c), device_id_type=MESH))
        return mine, sends, recvs

    def start(self, ins, outs, sems):
        mine, sends, _ = self._parts(ins, outs, sems)
        for cp in [mine] + sends:
            cp.start()

    def finish(self, ins, outs, sems):
        mine, sends, recvs = self._parts(ins, outs, sems)
        for make in recvs:
            make().wait_recv()
        for cp in sends:
            cp.wait_send()
        mine.wait()


def _pair_add(partial, recv):
    r = recv.shape[0] // N_CHIP
    cols = recv.shape[1]
    tr = r // 2 if (r // 2) % 16 == 0 else r
    steps = r // tr
    core = lax.axis_index("c").astype(jnp.int32).reshape(1)

    def body(c_ref, p_ref, s_ref, o_ref):
        o_ref[...] = (p_ref[...].astype(F32) + s_ref[...].astype(F32)).astype(BF16)

    spec = pl.BlockSpec((tr, cols), lambda j, i, c_ref: (j * steps + i, 0))
    return pl.pallas_call(
        body, name="pair_add",
        grid_spec=pltpu.PrefetchScalarGridSpec(
            num_scalar_prefetch=1, grid=(N_CHIP, steps),
            in_specs=[pl.BlockSpec((tr, cols), lambda j, i, c_ref: ((2 * j + c_ref[0]) * steps + i, 0)), spec],
            out_specs=spec),
        out_shape=jax.ShapeDtypeStruct(recv.shape, BF16),
        compiler_params=_cparams("parallel", "parallel"),
    )(core, partial, recv)


class _Both:
    def __init__(self, a, b):
        self.a, self.b = a, b
        self.ins = a.ins + b.ins
        self.out_shape = a.out_shape + b.out_shape
        self.scratch = a.scratch + b.scratch

    def _split(self, ins, outs, sems):
        ni, no, ns = len(self.a.ins), len(self.a.out_shape), len(self.a.scratch)
        return (ins[:ni], outs[:no], sems[:ns]), (ins[ni:], outs[no:], sems[ns:])

    def start(self, ins, outs, sems):
        ra, rb = self._split(ins, outs, sems)
        self.a.start(*ra)
        self.b.start(*rb)

    def finish(self, ins, outs, sems):
        ra, rb = self._split(ins, outs, sems)
        self.a.finish(*ra)
        self.b.finish(*rb)


def _exchange(comm, name):
    n, m = len(comm.ins), len(comm.out_shape)

    def body(*refs):
        ins, outs, sems = refs[:n], refs[n:n + m], refs[n + m:]
        comm.start(ins, outs, sems)
        comm.finish(ins, outs, sems)

    return pl.pallas_call(
        body, name=name, out_shape=comm.out_shape, in_specs=[ANY] * n, out_specs=[ANY] * m, scratch_shapes=comm.scratch,
    )(*comm.ins)


def _call(body, *, name, grid, in_specs, out_specs, out_shape, args, scratch=(), sem="parallel", comm=None):
    if comm is None:
        outs = pl.pallas_call(
            body, name=name, grid=grid, in_specs=list(in_specs), out_specs=list(out_specs), out_shape=list(out_shape),
            scratch_shapes=list(scratch), compiler_params=_cparams(sem))(*args)
        return outs, []
    n_in, n_out, n_sc = len(in_specs), len(out_specs), len(scratch)
    n_ci, n_co = len(comm.ins), len(comm.out_shape)
    last = grid[0] - 1

    def fused(*refs):
        ins, refs = refs[:n_in], refs[n_in:]
        c_ins, refs = refs[:n_ci], refs[n_ci:]
        outs, refs = refs[:n_out], refs[n_out:]
        c_outs, refs = refs[:n_co], refs[n_co:]
        sc, c_sems = refs[:n_sc], refs[n_sc:]
        step = pl.program_id(0)

        @pl.when(step == 0)
        def _():
            comm.start(c_ins, c_outs, c_sems)

        body(*ins, *outs, *sc)

        @pl.when(step == last)
        def _():
            comm.finish(c_ins, c_outs, c_sems)

    aliases = {n_in + i: n_out + o for i, o in getattr(comm, "alias", {}).items()}
    outs = pl.pallas_call(
        fused, name=name, grid=grid, in_specs=list(in_specs) + [ANY] * n_ci, out_specs=list(out_specs) + [ANY] * n_co,
        out_shape=list(out_shape) + comm.out_shape, scratch_shapes=list(scratch) + comm.scratch,
        input_output_aliases=aliases, compiler_params=_cparams("arbitrary"))(*args, *comm.ins)
    return outs[:n_out], outs[n_out:]


def _token_specs(tm):
    k = tm // BLK
    return [pl.BlockSpec((BLK, D), functools.partial(lambda i, t: (jnp.maximum(k * i + t - 1, 0), 0), t=t)) for t in range(k)]


def _in_proj(x2d, meta, gain, w_int, b_in, tabs, comm=None):
    p = x2d.shape[0] + BLK
    tm = _row_tile(p)
    k = tm // BLK

    def body(*refs):
        x_refs = refs[:k]
        m_ref, g_ref, w_ref, b_ref, t_ref, h_ref, n1_ref, q_ref, kv_ref, ag_ref, gt_ref = refs[k:]
        i = pl.program_id(0)
        head = jnp.concatenate([jnp.zeros((PAD, D), F32), m_ref[...]], axis=0)
        first = jnp.where(i == 0, head, x_refs[0][...])
        h = jnp.concatenate([first] + [r[...] for r in x_refs[1:]], axis=0) if k > 1 else first
        h_ref[...] = h
        n = _rms(h, g_ref[...]).astype(BF16)
        n1_ref[...] = n
        c, s1, s2 = t_ref[:, 0:128], t_ref[:, 128:256], t_ref[:, 256:384]

        def mm(c0, w):
            return _dot_nt(n, w_ref[c0:c0 + w, :]) + b_ref[:, c0:c0 + w]

        for j in range(4):
            acc = mm(256 * j, 256)
            for t in range(2):
                lo = 256 * j + 128 * t
                q_ref[:, lo:lo + 128] = (_rope(acc[:, 128 * t:128 * (t + 1)], c, s1, s2) * SCALE).astype(BF16)
        acc = mm(1024, 256)
        kv_ref[:, 0:128] = _rope(acc[:, 0:128], c, s1, s2).astype(BF16)
        kv_ref[:, 128:256] = acc[:, 128:256].astype(BF16)
        for j in range(8):
            ag_ref[:, 256 * j:256 * (j + 1)] = mm(QKV_W + 256 * j, 256).astype(BF16)
        for j in range(8):
            gt_ref[:, 256 * j:256 * (j + 1)] = mm(QKV_W + 2048 + 256 * j, 256).astype(BF16)

    def row(w):
        return pl.BlockSpec((tm, w), lambda i: (i, 0))

    return _call(
        body, name="in_proj", grid=(p // tm,),
        in_specs=_token_specs(tm) + [VM, VM, VM, VM, row(384)],
        out_specs=[row(D), row(D), row(D), row(256), row(2048), row(2048)],
        out_shape=[jax.ShapeDtypeStruct((p, D), F32)] + [jax.ShapeDtypeStruct((p, w), BF16) for w in (D, D, 256, 2048, 2048)],
        args=(x2d,) * k + (meta, gain, w_int, b_in, tabs), comm=comm)


N_KEY = 2 * BLK + N_META


def _attn_setup(n, h, q_ref, km_ref, kp_ref, kc_ref):
    lo = lax.broadcasted_iota(jnp.int32, (BLK, BLK), 1) < HEAD_DIM
    lok = lax.broadcasted_iota(jnp.int32, (N_KEY, BLK), 1) < HEAD_DIM

    def dup(lanes):
        cat = jnp.concatenate([kp_ref[:, lanes], kc_ref[:, lanes], km_ref[PAD:BLK, lanes]], axis=0).astype(F32)
        rolled = pltpu.roll(cat, HEAD_DIM, 1)
        return (jnp.where(lok, cat, rolled) if h == 0 else jnp.where(lok, rolled, cat)).astype(BF16)

    k2 = dup(slice(0, 128))
    v2 = dup(slice(128, 256))
    qs = _stack_heads(q_ref, h, lo)

    kr = lax.broadcasted_iota(jnp.int32, (BLK, BLK), 0)
    tq = BLK * n + lax.broadcasted_iota(jnp.int32, (BLK, BLK), 1) - PAD
    t_p = BLK * (n - 1) + kr - PAD
    t_c = BLK * n + kr - PAD
    ok_p = jnp.logical_and(t_p >= N_META, tq - t_p < BLK)
    ok_c = jnp.logical_and(t_c >= N_META, t_c <= tq)
    ok_m = lax.broadcasted_iota(jnp.int32, (N_META, BLK), 0) <= BLK * n + lax.broadcasted_iota(jnp.int32, (N_META, BLK), 1) - PAD
    bias = jnp.concatenate([jnp.where(ok, 0.0, NEG_INF).astype(F32) for ok in (ok_p, ok_c, ok_m)], axis=0)
    return qs, k2, v2, bias, lok


def _attn_head(s, bias, sink):
    s = s + bias
    m = jnp.maximum(jnp.max(s, axis=0, keepdims=True), sink)
    e = jnp.exp(s - m)
    es = jnp.exp(sink - m)
    inv = 1.0 / (jnp.sum(e, axis=0, keepdims=True) + es)
    return e * inv, es * inv


def _stack_heads(ref, h, lo):
    pieces = []
    for jp in range(4):
        v = ref[:, BLK * (4 * h + jp):BLK * (4 * h + jp + 1)]
        zero = jnp.zeros_like(v)
        pieces += [jnp.where(lo, v, zero), jnp.where(lo, zero, v)]
    return jnp.concatenate(pieces, axis=0)


def _unstack_heads(v, jp, lo):
    return jnp.where(lo, v[256 * jp:256 * jp + 128], v[256 * jp + 128:256 * jp + 256])


def _attn_fwd(q, kv, sinks, comm=None):
    p = q.shape[0]
    nb = p // BLK

    def body(q_ref, km_ref, kp_ref, kc_ref, sink_ref, o_ref):
        n = pl.program_id(0)
        lo = lax.broadcasted_iota(jnp.int32, (BLK, BLK), 1) < HEAD_DIM
        for h in range(2):
            qs, k2, v2, bias, _ = _attn_setup(n, h, q_ref, km_ref, kp_ref, kc_ref)
            st = _dot_nt(k2, qs)
            pt = jnp.concatenate(
                [_attn_head(st[:, BLK * g:BLK * (g + 1)], bias, sink_ref[0, 8 * h + g])[0].astype(BF16) for g in range(8)],
                axis=1)
            o = _dot_tn(pt, v2)
            for jp in range(4):
                o_ref[:, BLK * (4 * h + jp):BLK * (4 * h + jp + 1)] = _unstack_heads(o, jp, lo).astype(BF16)

    return _call(
        body, name="attn_fwd", grid=(nb,),
        in_specs=[pl.BlockSpec((BLK, D), lambda i: (i, 0)),
                  pl.BlockSpec((BLK, 256), lambda i: (0, 0)),
                  pl.BlockSpec((BLK, 256), lambda i: (jnp.maximum(i - 1, 0), 0)),
                  pl.BlockSpec((BLK, 256), lambda i: (i, 0)),
                  pl.BlockSpec(memory_space=pltpu.SMEM)],
        out_specs=[pl.BlockSpec((BLK, D), lambda i: (i, 0))],
        out_shape=[jax.ShapeDtypeStruct((p, D), BF16)],
        args=(q, kv, kv, kv, sinks), comm=comm)


def _conv31_fwd(ag, w32, b, comm=None):
    p = ag.shape[0]
    nch = p // BLK

    def body(a_ref, g_ref, w_ref, b_ref, o_ref, gp):
        gp[0:32, :] = jnp.zeros((32, BLK), F32)
        for ci in range(nch):
            r0 = BLK * ci
            glu = a_ref[r0:r0 + BLK, :].astype(F32) * jax.nn.sigmoid(g_ref[r0:r0 + BLK, :].astype(F32))
            if ci == 0:
                glu = jnp.where(_rows(0, BLK) >= PAD, glu, 0.0)
            gp[32 + r0:32 + r0 + BLK, :] = glu
        for ci in range(nch):
            r0 = BLK * ci
            acc = jnp.broadcast_to(b_ref[...], (BLK, BLK))
            for j in range(CONV_K):
                acc = acc + w_ref[j:j + 1, :] * gp[r0 + j + 2:r0 + j + 2 + BLK, :]
            o_ref[r0:r0 + BLK, :] = acc

    return _call(
        body, name="conv31_fwd", grid=(D // BLK,),
        in_specs=[pl.BlockSpec((p, BLK), lambda j: (0, j)), pl.BlockSpec((p, BLK), lambda j: (0, 8 + j)),
                  pl.BlockSpec((32, BLK), lambda j: (0, j)), pl.BlockSpec((1, BLK), lambda j: (0, j))],
        out_specs=[pl.BlockSpec((p, BLK), lambda j: (0, j))],
        out_shape=[jax.ShapeDtypeStruct((p, D), F32)],
        scratch=[pltpu.VMEM((p + 32, BLK), F32)],
        args=(ag, ag, w32, b), comm=comm)


def _mixer_fwd(ao, c0, gates, h0p, wa, wc, wo, vecs, comm=None):
    p = ao.shape[0]
    tm = _row_tile(p)

    def body(ao_ref, c0_ref, gt_ref, h_ref, wa_ref, wc_ref, wo_ref, v_ref,
             c1_ref, at_ref, cv_ref, mg_ref, mix_ref, h1_ref, n2_ref):
        i = pl.program_id(0)
        c1 = _lnsilu(c0_ref[...], v_ref[0:1, :], v_ref[1:2, :]).astype(BF16)
        c1_ref[...] = c1
        attn = _dot(ao_ref[...], wa_ref[...])
        conv = _dot(c1, wc_ref[...]) + v_ref[2:3, :]
        at_ref[...] = attn.astype(BF16)
        cv_ref[...] = conv.astype(BF16)
        merged = (jax.nn.sigmoid(gt_ref[:, 0:D].astype(F32)) * attn
                  + jax.nn.sigmoid(gt_ref[:, D:2 * D].astype(F32)) * conv).astype(BF16)
        mg_ref[...] = merged
        mix = _dot(merged, wo_ref[...])
        mix_ref[...] = mix
        h1 = jnp.where(_rows(i, tm) >= PAD, h_ref[...] + _rms(mix, v_ref[3:4, :]), 0.0)
        h1_ref[...] = h1
        n2_ref[...] = _rms(h1, v_ref[4:5, :]).astype(BF16)

    def row(w):
        return pl.BlockSpec((tm, w), lambda i: (i, 0))

    return _call(
        body, name="mixer_fwd", grid=(p // tm,),
        in_specs=[row(D), row(D), row(2 * D), row(D), VM, VM, VM, VM],
        out_specs=[row(D)] * 7,
        out_shape=[jax.ShapeDtypeStruct((p, D), t) for t in (BF16, BF16, BF16, BF16, F32, F32, BF16)],
        args=(ao, c0, gates, h0p, wa, wc, wo, vecs), comm=comm)


def _mm_nt(a, w_t, name):
    p, k = a.shape
    n = w_t.shape[0]
    tm = _row_tile(p)
    ch = 512

    def body(a_ref, w_ref, o_ref):
        a_v = a_ref[...]
        for c0 in range(0, n, ch):
            o_ref[:, c0:c0 + ch] = _dot_nt(a_v, w_ref[c0:c0 + ch, :]).astype(BF16)

    return pl.pallas_call(
        body, name=name, grid=(p // tm,),
        in_specs=[pl.BlockSpec((tm, k), lambda i: (i, 0)), VM],
        out_specs=pl.BlockSpec((tm, n), lambda i: (i, 0)),
        out_shape=jax.ShapeDtypeStruct((p, n), BF16),
        compiler_params=_cparams("parallel"),
    )(a, w_t)


def _conv3(xp_ref, w_ref, r0):
    return (w_ref[0:1, :] * xp_ref[r0 + 6:r0 + 6 + BLK, :] + w_ref[1:2, :] * xp_ref[r0 + 7:r0 + 7 + BLK, :]
            + w_ref[2:3, :] * xp_ref[r0 + 8:r0 + 8 + BLK, :])


def _ffn_slab_specs(p):
    ncol = FFN // BLK
    return [pl.BlockSpec((p, BLK), lambda j: (0, j)), pl.BlockSpec((p, BLK), lambda j: (0, ncol + j)),
            pl.BlockSpec((FFN_K, BLK), lambda j: (0, j)), pl.BlockSpec((FFN_K, BLK), lambda j: (0, ncol + j)),
            pl.BlockSpec((1, BLK), lambda j: (0, j)), pl.BlockSpec((1, BLK), lambda j: (0, ncol + j))]


def _fill_shifted(dst, src_ref, nch):
    dst[0:8, :] = jnp.zeros((8, BLK), F32)
    for ci in range(nch):
        dst[8 + BLK * ci:8 + BLK * (ci + 1), :] = src_ref[BLK * ci:BLK * (ci + 1), :].astype(F32)


def _ffn_act(u0, fw, fb):
    p = u0.shape[0]
    nch = p // BLK

    def body(g_ref, v_ref, wg_ref, wv_ref, bg_ref, bv_ref, o_ref, dv_ref, dg_ref, xg, xv):
        _fill_shifted(xg, g_ref, nch)
        _fill_shifted(xv, v_ref, nch)
        for ci in range(nch):
            r0 = BLK * ci
            ug = _conv3(xg, wg_ref, r0) + bg_ref[...]
            uv = _conv3(xv, wv_ref, r0) + bv_ref[...]
            sg = jax.nn.sigmoid(ug)
            silu = ug * sg
            o_ref[r0:r0 + BLK, :] = (silu * uv).astype(BF16)
            dv_ref[r0:r0 + BLK, :] = silu.astype(BF16)
            dg_ref[r0:r0 + BLK, :] = (uv * (sg * (1.0 + ug * (1.0 - sg)))).astype(BF16)

    slab = pl.BlockSpec((p, BLK), lambda j: (0, j))
    return pl.pallas_call(
        body, name="ffn_act", grid=(FFN // BLK,),
        in_specs=_ffn_slab_specs(p),
        out_specs=[slab] * 3,
        out_shape=[jax.ShapeDtypeStruct((p, FFN), BF16)] * 3,
        scratch_shapes=[pltpu.VMEM((p + 8, BLK), F32)] * 2,
        compiler_params=_cparams("parallel"),
    )(u0, u0, fw, fw, fb, fb)


def _ffn_down_loss(act, wd, h1, tgt, gain):
    p = act.shape[0]
    tm = _row_tile(p)
    k = tm // BLK

    def body(*refs):
        a_ref, w_ref, h_ref = refs[:3]
        t_refs = refs[3:3 + k]
        g_ref, df_ref, da_ref, dy_ref, acc_ref = refs[3 + k:]
        i = pl.program_id(0)

        @pl.when(i == 0)
        def _():
            acc_ref[...] = jnp.zeros_like(acc_ref)

        ffn = _dot(a_ref[...], w_ref[...])
        t = jnp.concatenate([t_ref[...] for t_ref in t_refs], axis=0) if k > 1 else t_refs[0][...]
        diff = jnp.where(_rows(i, tm) >= BLK, h_ref[...] + _rms(ffn, g_ref[...]) - t, 0.0)
        dy = diff * (1.0 / D)
        dffn, dg = _rms_bwd(ffn, g_ref[...], dy)
        acc_ref[0:1, :] += dg
        acc_ref[1:2, :] += jnp.sum(diff * diff, axis=0, keepdims=True) * (0.5 / D)
        dy_ref[...] = dy
        dfb = dffn.astype(BF16)
        df_ref[...] = dfb
        for c0 in range(0, FFN, 256):
            da_ref[:, c0:c0 + 256] = _dot_nt(dfb, w_ref[c0:c0 + 256, :]).astype(BF16)

    def row(w):
        return pl.BlockSpec((tm, w), lambda i: (i, 0))

    return pl.pallas_call(
        body, name="ffn_down_loss", grid=(p // tm,),
        in_specs=[row(FFN), VM, row(D)] + _token_specs(tm) + [VM],
        out_specs=[row(D), row(FFN), row(D), pl.BlockSpec((8, D), lambda i: (0, 0))],
        out_shape=[jax.ShapeDtypeStruct((p, D), BF16), jax.ShapeDtypeStruct((p, FFN), BF16),
                   jax.ShapeDtypeStruct((p, D), F32), jax.ShapeDtypeStruct((8, D), F32)],
        compiler_params=_cparams("arbitrary"),
    )(act, wd, h1, *([tgt] * k), gain)


def _mm_tn(pieces, b, name, col_sums=False, comm=None):
    p, n = b.shape
    tk = 256
    nblk = [a.shape[1] // tk for a in pieces]
    offs = [sum(nblk[:q]) for q in range(len(pieces))]
    total = sum(nblk)
    npc = len(pieces)

    def body(*refs):
        a_refs, b_ref, o_ref = refs[:npc], refs[npc], refs[npc + 1]
        i = pl.program_id(0)
        for q, a_ref in enumerate(a_refs):
            @pl.when(jnp.logical_and(i >= offs[q], i < offs[q] + nblk[q]))
            def _(a_ref=a_ref):
                a_v = a_ref[...]
                o_ref[...] = _dot_tn(a_v, b_ref[...]).astype(BF16)
                if col_sums:
                    refs[npc + 2][...] = jnp.sum(a_v.astype(F32), axis=0, keepdims=True)

    def a_spec(q):
        return pl.BlockSpec((p, tk), lambda i: (0, jnp.clip(i - offs[q], 0, nblk[q] - 1)))

    out_specs = [pl.BlockSpec((tk, n), lambda i: (i, 0))]
    out_shape = [jax.ShapeDtypeStruct((total * tk, n), BF16)]
    if col_sums:
        out_specs.append(pl.BlockSpec((1, tk), lambda i: (0, i)))
        out_shape.append(jax.ShapeDtypeStruct((1, total * tk), F32))
    res, sent = _call(
        body, name=name, grid=(total,),
        in_specs=[a_spec(q) for q in range(npc)] + [VM],
        out_specs=out_specs, out_shape=out_shape, args=(*pieces, b), comm=comm)
    res = res if col_sums else res[0]
    return res if comm is None else (res, sent)


def _ffn_act_bwd(u0, dact, dact_dg, dact_dv, fw, act, dffn, comm=None):
    p = u0.shape[0]
    nch = p // BLK
    ncol = FFN // BLK

    def body(g_ref, v_ref, wg_ref, wv_ref, da_ref, lg_ref, lv_ref, act_ref, df_ref,
             dg_ref, dv_ref, gwg_ref, gwv_ref, gbg_ref, gbv_ref, gwd_ref, eg, ev):
        gwd_ref[...] = _dot_tn(act_ref[...], df_ref[...]).astype(BF16)
        eg[p:p + 8, :] = jnp.zeros((8, BLK), F32)
        ev[p:p + 8, :] = jnp.zeros((8, BLK), F32)
        for ci in range(nch):
            r0 = BLK * ci
            d = da_ref[r0:r0 + BLK, :].astype(F32)
            eg[r0:r0 + BLK, :] = d * lg_ref[r0:r0 + BLK, :].astype(F32)
            ev[r0:r0 + BLK, :] = d * lv_ref[r0:r0 + BLK, :].astype(F32)
        def fold(v):
            return jnp.sum(v.reshape(BLK // 8, 8, BLK), axis=0)

        for e_s, x_ref, w_ref, d_ref, gw_ref, gb_ref in ((eg, g_ref, wg_ref, dg_ref, gwg_ref, gbg_ref),
                                                        (ev, v_ref, wv_ref, dv_ref, gwv_ref, gbv_ref)):
            sums = [jnp.zeros((8, BLK), F32) for _ in range(FFN_K + 1)]
            for ci in range(nch):
                r0 = BLK * ci
                es = [e_s[r0 + t:r0 + t + BLK, :] for t in range(FFN_K)]
                du = w_ref[2:3, :] * es[0] + w_ref[1:2, :] * es[1] + w_ref[0:1, :] * es[2]
                if ci == 0:
                    du = jnp.where(_rows(0, BLK) >= PAD, du, 0.0)
                d_ref[r0:r0 + BLK, :] = du.astype(BF16)
                x = x_ref[r0:r0 + BLK, :].astype(F32)
                for j in range(FFN_K):
                    sums[j] = sums[j] + fold(es[FFN_K - 1 - j] * x)
                sums[FFN_K] = sums[FFN_K] + fold(es[0])
            for j in range(FFN_K):
                gw_ref[j:j + 1, :] = jnp.sum(sums[j], axis=0, keepdims=True)
            gb_ref[...] = jnp.sum(sums[FFN_K], axis=0, keepdims=True)

    slab = pl.BlockSpec((p, BLK), lambda j: (0, j))
    wspec = pl.BlockSpec((FFN_K, BLK), lambda j: (0, j))
    bspec = pl.BlockSpec((1, BLK), lambda j: (0, j))
    return _call(
        body, name="ffn_act_bwd", grid=(ncol,),
        in_specs=_ffn_slab_specs(p)[:4] + [slab] * 4 + [VM],
        out_specs=[slab, slab, wspec, wspec, bspec, bspec, pl.BlockSpec((BLK, D), lambda j: (j, 0))],
        out_shape=[jax.ShapeDtypeStruct((p, FFN), BF16)] * 2 + [jax.ShapeDtypeStruct((FFN_K, FFN), F32)] * 2
        + [jax.ShapeDtypeStruct((1, FFN), F32)] * 2 + [jax.ShapeDtypeStruct((FFN, D), BF16)],
        scratch=[pltpu.VMEM((p + 8, BLK), F32)] * 2,
        args=(u0, u0, fw, fw, dact, dact_dg, dact_dv, act, dffn), comm=comm)


def _ffn_in_bwd(dug, duv, w_upt, h1, dy, gain, comm=None):
    p = h1.shape[0]
    tm = _row_tile(p)

    def body(dg_ref, dv_ref, w_ref, h_ref, dy_ref, g_ref, o_ref, acc_ref):
        i = pl.program_id(0)

        @pl.when(i == 0)
        def _():
            acc_ref[...] = jnp.zeros_like(acc_ref)

        dn = _dot(dg_ref[...], w_ref[0:FFN, :]) + _dot(dv_ref[...], w_ref[FFN:2 * FFN, :])
        dh, dg = _rms_bwd(h_ref[...], g_ref[...], dn)
        o_ref[...] = dy_ref[...] + dh
        acc_ref[0:1, :] += dg

    def row(w):
        return pl.BlockSpec((tm, w), lambda i: (i, 0))

    return _call(
        body, name="ffn_in_bwd", grid=(p // tm,),
        in_specs=[row(FFN), row(FFN), VM, row(D), row(D), VM],
        out_specs=[row(D), pl.BlockSpec((8, D), lambda i: (0, 0))],
        out_shape=[jax.ShapeDtypeStruct((p, D), F32), jax.ShapeDtypeStruct((8, D), F32)],
        sem="arbitrary", args=(dug, duv, w_upt, h1, dy, gain), comm=comm)


def _mixer_bwd(dh1, mix, attn, conv, gates, c0, wa, wc, wo, vecs, comm=None):
    p = dh1.shape[0]
    tm = _row_tile(p)

    def body(dh_ref, mix_ref, at_ref, cv_ref, gt_ref, c0_ref, wa_ref, wc_ref, wo_ref, v_ref,
             dmix_ref, dat_ref, dcv_ref, dgt_ref, dao_ref, dc0_ref, acc_ref):
        i = pl.program_id(0)

        @pl.when(i == 0)
        def _():
            acc_ref[...] = jnp.zeros_like(acc_ref)

        dmix, dgp = _rms_bwd(mix_ref[...], v_ref[3:4, :], dh_ref[...])
        dmix = dmix.astype(BF16)
        dmix_ref[...] = dmix
        dmg = _dot_nt(dmix, wo_ref[...])
        sa = jax.nn.sigmoid(gt_ref[:, 0:D].astype(F32))
        sc = jax.nn.sigmoid(gt_ref[:, D:2 * D].astype(F32))
        dat = dmg * sa
        dcv = dmg * sc
        dgt_ref[:, 0:D] = (dmg * at_ref[...].astype(F32) * sa * (1.0 - sa)).astype(BF16)
        dgt_ref[:, D:2 * D] = (dmg * cv_ref[...].astype(F32) * sc * (1.0 - sc)).astype(BF16)
        datb = dat.astype(BF16)
        dcvb = dcv.astype(BF16)
        dat_ref[...] = datb
        dcv_ref[...] = dcvb
        dao_ref[...] = _dot_nt(datb, wa_ref[...]).astype(BF16)
        dc1 = _dot_nt(dcvb, wc_ref[...])
        dc0, dlg, dlb = _lnsilu_bwd(c0_ref[...], v_ref[0:1, :], v_ref[1:2, :], dc1)
        dc0_ref[...] = dc0
        acc_ref[0:1, :] += dgp
        acc_ref[1:2, :] += jnp.sum(dcv, axis=0, keepdims=True)
        acc_ref[2:3, :] += dlg
        acc_ref[3:4, :] += dlb

    def row(w):
        return pl.BlockSpec((tm, w), lambda i: (i, 0))

    return _call(
        body, name="mixer_bwd", grid=(p // tm,),
        in_specs=[row(D), row(D), row(D), row(D), row(2 * D), row(D), VM, VM, VM, VM],
        out_specs=[row(D), row(D), row(D), row(2 * D), row(D), row(D), pl.BlockSpec((8, D), lambda i: (0, 0))],
        out_shape=[jax.ShapeDtypeStruct((p, D), BF16)] * 3 + [jax.ShapeDtypeStruct((p, 2 * D), BF16),
                                                             jax.ShapeDtypeStruct((p, D), BF16),
                                                             jax.ShapeDtypeStruct((p, D), F32),
                                                             jax.ShapeDtypeStruct((8, D), F32)],
        sem="arbitrary", args=(dh1, mix, attn, conv, gates, c0, wa, wc, wo, vecs), comm=comm)


def _conv31_bwd(ag, dc0, w32, tn_pairs, comm=None):
    p = ag.shape[0]
    nch = p // BLK
    npair = len(tn_pairs)

    def body(*refs):
        a_ref, g_ref, dc_ref, w_ref = refs[:4]
        tn_a, tn_b = refs[4:4 + npair], refs[4 + npair:4 + 2 * npair]
        da_ref, dg_ref, gw_ref, gb_ref = refs[4 + 2 * npair:8 + 2 * npair]
        tn_o = refs[8 + 2 * npair:8 + 3 * npair]
        gp, dp = refs[8 + 3 * npair:]
        for ta, tb, to in zip(tn_a, tn_b, tn_o):
            to[...] = _dot_tn(ta[...], tb[...]).astype(BF16)
        gp[0:32, :] = jnp.zeros((32, BLK), F32)
        dp[p:p + 32, :] = jnp.zeros((32, BLK), F32)
        bsum = jnp.zeros((BLK, BLK), F32)
        for ci in range(nch):
            r0 = BLK * ci
            glu = a_ref[r0:r0 + BLK, :].astype(F32) * jax.nn.sigmoid(g_ref[r0:r0 + BLK, :].astype(F32))
            if ci == 0:
                glu = jnp.where(_rows(0, BLK) >= PAD, glu, 0.0)
            gp[32 + r0:32 + r0 + BLK, :] = glu
            d = dc_ref[r0:r0 + BLK, :]
            dp[r0:r0 + BLK, :] = d
            bsum = bsum + d
        gb_ref[...] = jnp.sum(bsum, axis=0, keepdims=True)
        for ci in range(nch):
            r0 = BLK * ci
            acc = jnp.zeros((BLK, BLK), F32)
            for j in range(CONV_K):
                acc = acc + w_ref[j:j + 1, :] * dp[r0 + 30 - j:r0 + 30 - j + BLK, :]
            if ci == 0:
                acc = jnp.where(_rows(0, BLK) >= PAD, acc, 0.0)
            a = a_ref[r0:r0 + BLK, :].astype(F32)
            sg = jax.nn.sigmoid(g_ref[r0:r0 + BLK, :].astype(F32))
            da_ref[r0:r0 + BLK, :] = (acc * sg).astype(BF16)
            dg_ref[r0:r0 + BLK, :] = (acc * a * sg * (1.0 - sg)).astype(BF16)
        sub = BLK // 2
        accs = [jnp.zeros((8, BLK), F32) for _ in range(CONV_K)]
        for r0 in range(0, p, sub):
            d = dp[r0:r0 + sub, :]
            for j in range(CONV_K):
                prod = d * gp[r0 + j + 2:r0 + j + 2 + sub, :]
                accs[j] = accs[j] + jnp.sum(prod.reshape(sub // 8, 8, BLK), axis=0)
        for j in range(CONV_K):
            gw_ref[j:j + 1, :] = jnp.sum(accs[j], axis=0, keepdims=True)
        gw_ref[CONV_K:32, :] = jnp.zeros((32 - CONV_K, BLK), F32)

    slab = pl.BlockSpec((p, BLK), lambda j: (0, j))
    return _call(
        body, name="conv31_bwd", grid=(D // BLK,),
        in_specs=[slab, pl.BlockSpec((p, BLK), lambda j: (0, 8 + j)), slab, pl.BlockSpec((32, BLK), lambda j: (0, j))]
        + [slab] * npair + [VM] * npair,
        out_specs=[slab, slab, pl.BlockSpec((32, BLK), lambda j: (0, j)), pl.BlockSpec((1, BLK), lambda j: (0, j))]
        + [pl.BlockSpec((BLK, D), lambda j: (j, 0))] * npair,
        out_shape=[jax.ShapeDtypeStruct((p, D), BF16)] * 2 + [jax.ShapeDtypeStruct((32, D), F32),
                                                             jax.ShapeDtypeStruct((1, D), F32)]
        + [jax.ShapeDtypeStruct((D, D), BF16)] * npair,
        scratch=[pltpu.VMEM((p + 32, BLK), F32)] * 2,
        args=(ag, ag, dc0, w32, *[a for a, _ in tn_pairs], *[b for _, b in tn_pairs]), comm=comm)


def _attn_bwd(q, kv, dao, sinks, tabs, comm=None):
    p = q.shape[0]
    nb = p // BLK

    def body(q_ref, km_ref, kp_ref, kc_ref, do_ref, sink_ref, t_ref, dqkv_ref, dsink_ref, carry, macc):
        i = pl.program_id(0)
        n = nb - 1 - i

        @pl.when(i == 0)
        def _():
            carry[...] = jnp.zeros_like(carry)
            macc[...] = jnp.zeros_like(macc)
            dsink_ref[...] = jnp.zeros_like(dsink_ref)

        lo = lax.broadcasted_iota(jnp.int32, (BLK, BLK), 1) < HEAD_DIM
        lane8 = lax.broadcasted_iota(jnp.int32, (8, BLK), 1)
        c, s1, s2 = t_ref[:, 0:128], -t_ref[:, 128:256], -t_ref[:, 256:384]
        dk = jnp.zeros((N_KEY, BLK), F32)
        dv = jnp.zeros((N_KEY, BLK), F32)
        for h in range(2):
            qs, k2, v2, bias, lok = _attn_setup(n, h, q_ref, km_ref, kp_ref, kc_ref)
            dos = _stack_heads(do_ref, h, lo)
            st = _dot_nt(k2, qs)
            dpt = _dot_nt(v2, dos)
            p_parts, ds_parts = [], []
            for g in range(8):
                cols = slice(BLK * g, BLK * (g + 1))
                pn, ps = _attn_head(st[:, cols], bias, sink_ref[0, 8 * h + g])
                dp = dpt[:, cols]
                delta = jnp.sum(pn * dp, axis=0, keepdims=True)
                ds_parts.append((pn * (dp - delta)).astype(BF16))
                p_parts.append(pn.astype(BF16))
                dsk = -jnp.sum(ps * delta, axis=1, keepdims=True)
                dsink_ref[...] += jnp.where(lane8 == 8 * h + g, dsk, 0.0)
            dst = jnp.concatenate(ds_parts, axis=1)
            pt = jnp.concatenate(p_parts, axis=1)
            dq = _dot_tn(dst, k2)
            for jp in range(4):
                lo_c = BLK * (4 * h + jp)
                dqkv_ref[:, lo_c:lo_c + BLK] = (_rope(_unstack_heads(dq, jp, lo), c, s1, s2) * SCALE).astype(BF16)
            dk2 = _dot(dst, qs)
            dv2 = _dot(pt, dos)
            dk2 = dk2 + pltpu.roll(dk2, HEAD_DIM, 1)
            dv2 = dv2 + pltpu.roll(dv2, HEAD_DIM, 1)
            own = lok if h == 0 else jnp.logical_not(lok)
            dk = jnp.where(own, dk2, dk)
            dv = jnp.where(own, dv2, dv)
        macc[:, 0:BLK] += dk[2 * BLK:N_KEY]
        macc[:, BLK:2 * BLK] += dv[2 * BLK:N_KEY]
        last = (n == 0).astype(F32)
        zpad = jnp.zeros((PAD, BLK), F32)
        dk_c = dk[BLK:2 * BLK] + carry[:, 0:BLK] + last * jnp.concatenate([zpad, macc[:, 0:BLK]], axis=0)
        dv_c = dv[BLK:2 * BLK] + carry[:, BLK:2 * BLK] + last * jnp.concatenate([zpad, macc[:, BLK:2 * BLK]], axis=0)
        carry[:, 0:BLK] = dk[0:BLK]
        carry[:, BLK:2 * BLK] = dv[0:BLK]
        dqkv_ref[:, D:D + BLK] = _rope(dk_c, c, s1, s2).astype(BF16)
        dqkv_ref[:, D + BLK:D + 2 * BLK] = dv_c.astype(BF16)

    def rev(w):
        return pl.BlockSpec((BLK, w), lambda i: (nb - 1 - i, 0))

    return _call(
        body, name="attn_bwd", grid=(nb,),
        in_specs=[rev(D),
                  pl.BlockSpec((BLK, 256), lambda i: (0, 0)),
                  pl.BlockSpec((BLK, 256), lambda i: (jnp.maximum(nb - 2 - i, 0), 0)),
                  rev(256), rev(D),
                  pl.BlockSpec(memory_space=pltpu.SMEM), rev(384)],
        out_specs=[rev(QKV_W), pl.BlockSpec((8, BLK), lambda i: (0, 0))],
        out_shape=[jax.ShapeDtypeStruct((p, QKV_W), BF16), jax.ShapeDtypeStruct((8, BLK), F32)],
        scratch=[pltpu.VMEM((BLK, 256), F32), pltpu.VMEM((N_META, 256), F32)], sem="arbitrary",
        args=(q, kv, kv, kv, dao, sinks, tabs), comm=comm)


def _in_bwd(dqkv, da, dg, dgt, w_int, h0p, dh1, gain, comm=None):
    p = h0p.shape[0]
    tm = _row_tile(p)
    nt = p // tm
    first_rows = tm - BLK

    def body(dq_ref, da_ref, dg_ref, dt_ref, w_ref, h_ref, dh_ref, g_ref, gx_ref, dm_ref, acc_ref, buf, sems):
        i = pl.program_id(0)
        slot = i % 2

        @pl.when(i == 0)
        def _():
            acc_ref[...] = jnp.zeros_like(acc_ref)

        dn = (_dot(dq_ref[...], w_ref[0:QKV_W, :]) + _dot(da_ref[...], w_ref[QKV_W:QKV_W + D, :])
              + _dot(dg_ref[...], w_ref[QKV_W + D:QKV_W + 2 * D, :]) + _dot(dt_ref[...], w_ref[QKV_W + 2 * D:IN_W, :]))
        dh, dgain = _rms_bwd(h_ref[...], g_ref[...], dn)
        dh0 = dh_ref[...] + dh
        acc_ref[0:1, :] += dgain
        buf[slot] = dh0

        @pl.when(i == 0)
        def _():
            dm_ref[...] = dh0[PAD:BLK]

        def first_copy():
            return pltpu.make_async_copy(buf.at[0, pl.ds(BLK, first_rows), :], gx_ref.at[pl.ds(0, first_rows), :], sems.at[0])

        def tile_copy(j, s):
            return pltpu.make_async_copy(buf.at[s], gx_ref.at[pl.ds(pl.multiple_of(j * tm - BLK, BLK), tm), :], sems.at[s])

        if first_rows:
            @pl.when(i == 1)
            def _():
                first_copy().wait()

        @pl.when(i >= 2)
        def _():
            tile_copy(i - 1, 1 - slot).wait()

        if first_rows:
            @pl.when(i == 0)
            def _():
                first_copy().start()

        @pl.when(i > 0)
        def _():
            tile_copy(i, slot).start()

        @pl.when(i == nt - 1)
        def _():
            tile_copy(i, slot).wait()

    def row(w):
        return pl.BlockSpec((tm, w), lambda i: (i, 0))

    return _call(
        body, name="in_bwd", grid=(nt,),
        in_specs=[row(QKV_W), row(D), row(D), row(2 * D), VM, row(D), row(D), VM],
        out_specs=[ANY, pl.BlockSpec((N_META, D), lambda i: (0, 0)), pl.BlockSpec((8, D), lambda i: (0, 0))],
        out_shape=[jax.ShapeDtypeStruct((p - BLK, D), F32), jax.ShapeDtypeStruct((N_META, D), F32),
                   jax.ShapeDtypeStruct((8, D), F32)],
        scratch=[pltpu.VMEM((2, tm, D), F32), pltpu.SemaphoreType.DMA((2,))],
        sem="arbitrary", args=(dqkv, da, dg, dgt, w_int, h0p, dh1, gain), comm=comm)


def _sum_slots(slots, name):
    r = slots.shape[0] // N_DEV
    cols = slots.shape[1]
    tr = r if r <= 352 else (r // 2 if (r // 2) % 16 == 0 else r // 3)
    steps = r // tr

    def body(*refs):
        acc = refs[0][...].astype(F32)
        for s in range(1, N_DEV):
            acc = acc + refs[s][...].astype(F32)
        refs[N_DEV][...] = acc

    return pl.pallas_call(
        body, name=name, grid=(steps,),
        in_specs=[pl.BlockSpec((tr, cols), functools.partial(lambda i, s: (s * steps + i, 0), s=s)) for s in range(N_DEV)],
        out_specs=pl.BlockSpec((tr, cols), lambda i: (i, 0)),
        out_shape=jax.ShapeDtypeStruct((r, cols), F32),
        compiler_params=_cparams("parallel"),
    )(*([slots] * N_DEV))


def _adamw_math(w, g, m, v):
    m_n = ADAM_B1 * m + (1.0 - ADAM_B1) * g
    v_n = ADAM_B2 * v + (1.0 - ADAM_B2) * jnp.square(g)
    m_hat = m_n / (1.0 - ADAM_B1 ** ADAM_STEP)
    v_hat = v_n / (1.0 - ADAM_B2 ** ADAM_STEP)
    return -ADAM_LR * (m_hat / (jnp.sqrt(v_hat) + ADAM_EPS) + ADAM_WD * w), m_n, v_n


def _sum_adamw(parts, w, m, v, name, nslots=N_DEV):
    r, cols = w.shape
    rs = r // len(parts)
    tr = rs if rs <= 352 else (rs // 2 if (rs // 2) % 16 == 0 else rs // 3)
    steps = rs // tr

    def body(*refs):
        w_ref, m_ref, v_ref, g_ref, d_ref, nm_ref, nv_ref = refs[nslots * len(parts):]
        i = pl.program_id(0)
        for q in range(len(parts)):
            @pl.when(i // steps == q)
            def _(q=q):
                g = refs[nslots * q][...].astype(F32)
                for s in range(1, nslots):
                    g = g + refs[nslots * q + s][...].astype(F32)
                g_ref[...] = g
                d_ref[...], nm_ref[...], nv_ref[...] = _adamw_math(w_ref[...], g, m_ref[...], v_ref[...])

    def slot_spec(q, s):
        return pl.BlockSpec((tr, cols), lambda i: (s * steps + jnp.clip(i - q * steps, 0, steps - 1), 0))

    spec = pl.BlockSpec((tr, cols), lambda i: (i, 0))
    return pl.pallas_call(
        body, name=name, grid=(steps * len(parts),),
        in_specs=[slot_spec(q, s) for q in range(len(parts)) for s in range(nslots)] + [spec] * 3,
        out_specs=[spec] * 4, out_shape=[jax.ShapeDtypeStruct((r, cols), F32)] * 4,
        compiler_params=_cparams("parallel"),
    )(*[a for a in parts for _ in range(nslots)], w, m, v)


def _adamw_many(ws, gs, ms, vs, name):
    n = len(ws)

    def body(*refs):
        w, g, m, v = refs[0:n], refs[n:2 * n], refs[2 * n:3 * n], refs[3 * n:4 * n]
        d, nm, nv = refs[4 * n:5 * n], refs[5 * n:6 * n], refs[6 * n:7 * n]
        for k in range(n):
            d[k][...], nm[k][...], nv[k][...] = _adamw_math(w[k][...], g[k][...], m[k][...], v[k][...])

    outs = pl.pallas_call(
        body, name=name, in_specs=[VM] * (4 * n), out_specs=[VM] * (3 * n),
        out_shape=[jax.ShapeDtypeStruct(a.shape, F32) for a in ws] * 3,
    )(*ws, *gs, *ms, *vs)
    return outs[0:n], outs[n:2 * n], outs[2 * n:3 * n]


def _rope_tables(p):
    half = ROT_DIM // 2
    lane = jnp.arange(BLK)
    seg = (lane % HEAD_DIM) // half
    inv_freq = ROPE_THETA ** (-(lane % half).astype(F32) * 2.0 / ROT_DIM)
    pos = (jnp.arange(p) - PAD).astype(F32)
    ang = pos[:, None] * inv_freq[None, :]
    cos = jnp.cos(ang)
    sin = jnp.sin(ang)
    c = jnp.where(seg[None, :] < 2, cos, 1.0)
    s1 = jnp.where(seg[None, :] == 0, -sin, 0.0)
    s2 = jnp.where(seg[None, :] == 1, sin, 0.0)
    return jnp.concatenate([c, s1, s2], axis=1).astype(F32)


def _flat_pack(parts, rows):
    flat = jnp.concatenate([a.reshape(-1).astype(F32) for a in parts])
    return jnp.pad(flat, (0, rows * D - flat.shape[0])).reshape(rows, D)


def _flat_unpack(pack, shapes):
    flat = pack.reshape(-1)
    out, off = [], 0
    for s in shapes:
        size = 1
        for e in s:
            size *= e
        out.append(flat[off:off + size].reshape(s))
        off += size
    return out


def kernel(x, meta_tokens, norm_pre_mix, norm_post_mix, w_in, b_in, attn_sinks, w_attn_proj, conv_dw_w, conv_dw_b, conv_ln_g, conv_ln_b, w_conv_proj, b_conv_proj, w_out, norm_pre_ffn, norm_post_ffn, w_up, ffn_dw_w, ffn_dw_b, w_down, loss_target, m_meta_tokens, m_norm_pre_mix, m_norm_post_mix, m_w_in, m_b_in, m_attn_sinks, m_w_attn_proj, m_conv_dw_w, m_conv_dw_b, m_conv_ln_g, m_conv_ln_b, m_w_conv_proj, m_b_conv_proj, m_w_out, m_norm_pre_ffn, m_norm_post_ffn, m_w_up, m_ffn_dw_w, m_ffn_dw_b, m_w_down, v_meta_tokens, v_norm_pre_mix, v_norm_post_mix, v_w_in, v_b_in, v_attn_sinks, v_w_attn_proj, v_conv_dw_w, v_conv_dw_b, v_conv_ln_g, v_conv_ln_b, v_w_conv_proj, v_b_conv_proj, v_w_out, v_norm_pre_ffn, v_norm_post_ffn, v_w_up, v_ffn_dw_w, v_ffn_dw_b, v_w_down):
    seq = x.shape[1]
    p = seq + BLK
    me = 4 * lax.axis_index("x") + 2 * lax.axis_index("y") + lax.axis_index("c")
    in_cols = w_in.shape[2]
    up_cols = w_up.shape[2]

    small = jnp.zeros((56, up_cols), F32)
    small = small.at[0:N_META, 0:BLK].set(meta_tokens)
    small = small.at[16:16 + CONV_K, 0:BLK].set(conv_dw_w[0])
    small = small.at[48:48 + FFN_K, :].set(ffn_dw_w[0])
    w_int, small_all = _exchange(_Both(_GatherRelay(w_in[0].T.astype(BF16)), _Gather([small])), "gather_w_in")
    small_all = small_all.reshape(N_DEV, 56, up_cols)
    meta_full = small_all[:, 0:N_META, 0:BLK].transpose(1, 0, 2).reshape(N_META, D)
    cdw = small_all[:, 16:16 + CONV_K, 0:BLK].transpose(1, 0, 2).reshape(CONV_K, D)
    cdw32 = jnp.pad(cdw, ((0, 32 - CONV_K), (0, 0)))
    fdw = small_all[:, 48:48 + FFN_K, :].transpose(1, 0, 2).reshape(FFN_K, 2 * FFN)

    tabs = _rope_tables(p)
    vecs = jnp.concatenate([conv_ln_g, conv_ln_b, b_conv_proj, norm_post_mix, norm_pre_ffn, jnp.zeros((3, D), F32)], axis=0)

    (h0p, n1, q, kv, ag, gates), (wa, wc, wo) = _in_proj(
        x[0], meta_full, norm_pre_mix, w_int, b_in, tabs,
        comm=_Gather([w_attn_proj[0].astype(BF16), w_conv_proj[0].astype(BF16), w_out[0].astype(BF16)]))
    w_up_shard = w_up[0].T.astype(BF16)
    (ao,), (w_upt_half,) = _attn_fwd(q, kv, attn_sinks, comm=_Gather([w_up_shard], 0, 2))
    (c0,), (w_upt,) = _conv31_fwd(ag, cdw32, conv_dw_b, comm=_Gather([w_up_shard], 1, 2, into=[w_upt_half]))
    (c1, attn, conv, merged, mix, h1, n2), (wd,) = _mixer_fwd(
        ao, c0, gates, h0p, wa, wc, wo, vecs, comm=_Gather([w_down[0].astype(BF16)]))
    u0 = _mm_nt(n2, w_upt, "ffn_up")
    act, dact_dv, dact_dg = _ffn_act(u0, fdw, ffn_dw_b)
    dffn, dact, dy, acc_f = _ffn_down_loss(act, wd, h1, loss_target[0], norm_post_ffn)

    (dug, duv, gfw_g, gfw_v, gfb_g, gfb_v, g_wd), _ = _ffn_act_bwd(u0, dact, dact_dg, dact_dv, fdw, act, dffn)
    g_wupt, (s_wd0,) = _mm_tn([dug, duv], n2, "grad_w_up", comm=_Scatter([g_wd], 0, 2))
    (dh1, acc_u), (s_wd1,) = _ffn_in_bwd(dug, duv, w_upt, h1, dy, norm_pre_ffn, comm=_Scatter([g_wd], 1, 2))
    (dmix, dat, dcv, dgt, dao, dc0, acc_m), (s_wup0,) = _mixer_bwd(
        dh1, mix, attn, conv, gates, c0, wa, wc, wo, vecs, comm=_Scatter([g_wupt], 0, 4))
    (da, dg, g_cdw, g_cdb, g_wo, g_wa, g_wc), (s_wup1, s_wup2, s_wup3) = _conv31_bwd(
        ag, dc0, cdw32, [(merged, dmix), (ao, dat), (c1, dcv)],
        comm=_Both(_Both(_Scatter([g_wupt], 1, 4), _Scatter([g_wupt], 2, 4)), _Scatter([g_wupt], 3, 4)))
    (dqkv, dsink), (s_wa, s_wc, s_wo) = _attn_bwd(q, kv, dao, attn_sinks, tabs, comm=_Scatter([g_wa, g_wc, g_wo]))
    loss_row = jnp.sum(acc_f[1:2, :], axis=1, keepdims=True)
    early = [loss_row, acc_m[0:1], dsink[0:1, 0:16], g_cdw[0:CONV_K], g_cdb,
             acc_m[2:3], acc_m[3:4], acc_m[1:2], acc_u[0:1], acc_f[0:1],
             jnp.concatenate([gfw_g, gfw_v], axis=1), jnp.concatenate([gfb_g, gfb_v], axis=1)]
    (g_wint, g_bin), (gathered_early,) = _mm_tn([dqkv, da, dg, dgt], n1, "grad_w_in", col_sums=True,
                                                comm=_Gather([_flat_pack(early, 64)]))
    (from_sibling,) = _exchange(_SiblingSwap(g_wint), "swap_w_in")
    (grad_x2d, dmeta, acc_i), (s_win,) = _in_bwd(dqkv, da, dg, dgt, w_int, h0p, dh1, norm_pre_mix,
                                                 comm=_ChipScatter(_pair_add(g_wint, from_sibling)))

    big = []
    for nm, parts, nslots, w, m, v, tr in (
            ("w_in", [s_win], N_CHIP, w_in, m_w_in, v_w_in, True), ("w_up", [s_wup0, s_wup1, s_wup2, s_wup3], N_DEV, w_up, m_w_up, v_w_up, True),
            ("w_attn_proj", [s_wa], N_DEV, w_attn_proj, m_w_attn_proj, v_w_attn_proj, False),
            ("w_conv_proj", [s_wc], N_DEV, w_conv_proj, m_w_conv_proj, v_w_conv_proj, False),
            ("w_out", [s_wo], N_DEV, w_out, m_w_out, v_w_out, False),
            ("w_down", [s_wd0, s_wd1], N_DEV, w_down, m_w_down, v_w_down, False)):
        ins = [a[0].T if tr else a[0] for a in (w, m, v)]
        big.append(tuple((o.T if tr else o)[None] for o in _sum_adamw(parts, *ins, "update_" + nm, nslots)))

    late = [dmeta, acc_i[0:1], g_bin]
    (gathered_late,) = _exchange(_Gather([_flat_pack(late, 24)]), "gather_small_grads")
    g_meta, g_npm, g_bi = _flat_unpack(_sum_slots(gathered_late, "sum_late_grads"), [a.shape for a in late])
    tot = _flat_unpack(_sum_slots(gathered_early, "sum_small_grads"), [a.shape for a in early])
    (loss, g_nqm, g_sk, g_cw, g_cb, g_lg, g_lb, g_bc, g_npf, g_nqf, g_fw, g_fb) = tot
    loss = loss.reshape(())
    g_meta = lax.dynamic_slice_in_dim(g_meta, me * BLK, BLK, axis=1)
    g_cw = lax.dynamic_slice_in_dim(g_cw, me * BLK, BLK, axis=1)[None]
    g_fw = lax.dynamic_slice_in_dim(g_fw, me * up_cols, up_cols, axis=1)[None]

    sm_w = [meta_tokens, norm_pre_mix, norm_post_mix, b_in, attn_sinks, conv_dw_w, conv_dw_b, conv_ln_g, conv_ln_b,
            b_conv_proj, norm_pre_ffn, norm_post_ffn, ffn_dw_w, ffn_dw_b]
    sm_g = [g_meta, g_npm, g_nqm, g_bi, g_sk, g_cw, g_cb, g_lg, g_lb, g_bc, g_npf, g_nqf, g_fw, g_fb]
    sm_m = [m_meta_tokens, m_norm_pre_mix, m_norm_post_mix, m_b_in, m_attn_sinks, m_conv_dw_w, m_conv_dw_b, m_conv_ln_g,
            m_conv_ln_b, m_b_conv_proj, m_norm_pre_ffn, m_norm_post_ffn, m_ffn_dw_w, m_ffn_dw_b]
    sm_v = [v_meta_tokens, v_norm_pre_mix, v_norm_post_mix, v_b_in, v_attn_sinks, v_conv_dw_w, v_conv_dw_b, v_conv_ln_g,
            v_conv_ln_b, v_b_conv_proj, v_norm_pre_ffn, v_norm_post_ffn, v_ffn_dw_w, v_ffn_dw_b]
    swap = lambda a: jnp.transpose(a, (1, 0, 2)) if a.ndim == 3 else a
    sm_d, sm_nm, sm_nv = ([swap(o) for o in outs] for outs in
                          _adamw_many(*([swap(a) for a in group] for group in (sm_w, sm_g, sm_m, sm_v)), "adamw_small"))

    order = ["meta_tokens", "norm_pre_mix", "norm_post_mix", "w_in", "b_in", "attn_sinks", "w_attn_proj", "conv_dw_w",
             "conv_dw_b", "conv_ln_g", "conv_ln_b", "w_conv_proj", "b_conv_proj", "w_out", "norm_pre_ffn", "norm_post_ffn",
             "w_up", "ffn_dw_w", "ffn_dw_b", "w_down"]
    small_names = ["meta_tokens", "norm_pre_mix", "norm_post_mix", "b_in", "attn_sinks", "conv_dw_w", "conv_dw_b", "conv_ln_g",
                   "conv_ln_b", "b_conv_proj", "norm_pre_ffn", "norm_post_ffn", "ffn_dw_w", "ffn_dw_b"]
    big_names = ["w_in", "w_up", "w_attn_proj", "w_conv_proj", "w_out", "w_down"]
    table = {}
    for k, nm in enumerate(small_names):
        table[nm] = (sm_g[k], sm_d[k], sm_nm[k], sm_nv[k])
    for k, nm in enumerate(big_names):
        table[nm] = big[k]
    grad_x = grad_x2d[None]
    outs = [loss, grad_x]
    for field in range(4):
        outs += [table[nm][field] for nm in order]
    return tuple(outs)
```

```python
import functools

import jax
import jax.numpy as jnp
from jax import lax
from jax.experimental import pallas as pl
from jax.experimental.pallas import tpu as pltpu

F32 = jnp.float32
BF16 = jnp.bfloat16
MESH = pl.DeviceIdType.MESH

D = 1024
HEAD_DIM = 64
N_META = 16
BLK = 128
PAD = BLK - N_META
CONV_K = 31
FFN = 2816
FFN_K = 3
QKV_W = 1280
IN_W = 5376
ROT_DIM = 16
ROPE_THETA = 500000.0
RMS_EPS = 1e-6
LN_EPS = 1e-5
NEG_INF = -1e30
SCALE = HEAD_DIM ** -0.5
N_DEV = 8

ADAM_LR = 0.001
ADAM_B1 = 0.9
ADAM_B2 = 0.999
ADAM_EPS = 1e-08
ADAM_WD = 0.01
ADAM_STEP = 10

VMEM_BYTES_V7X = 64 * 1024 * 1024
VMEM_LIMIT = VMEM_BYTES_V7X - 8 * 1024 * 1024

NT = (((1,), (1,)), ((), ()))
TN = (((0,), (0,)), ((), ()))
VM = pl.BlockSpec(memory_space=pltpu.VMEM)
ANY = pl.BlockSpec(memory_space=pl.ANY)


def _cparams(*sem):
    return pltpu.CompilerParams(dimension_semantics=sem or None, vmem_limit_bytes=VMEM_LIMIT)


def _row_tile(p):
    return 384 if p % 384 == 0 else 128


def _dot(a, b):
    return jnp.dot(a, b, preferred_element_type=F32)


def _dot_nt(a, b):
    return lax.dot_general(a, b, NT, preferred_element_type=F32)


def _dot_tn(a, b):
    return lax.dot_general(a, b, TN, preferred_element_type=F32)


def _rms(x, g):
    return x * lax.rsqrt(jnp.mean(x * x, axis=-1, keepdims=True) + RMS_EPS) * g


def _lnsilu(x, g, b):
    mu = jnp.mean(x, axis=-1, keepdims=True)
    var = jnp.mean(jnp.square(x - mu), axis=-1, keepdims=True)
    z = (x - mu) * lax.rsqrt(var + LN_EPS) * g + b
    return z * jax.nn.sigmoid(z)


def _rms_bwd(x, g, dy):
    r = lax.rsqrt(jnp.mean(x * x, axis=-1, keepdims=True) + RMS_EPS)
    xn = x * r
    u = dy * g
    dg = jnp.sum(dy * xn, axis=0, keepdims=True)
    dx = r * (u - xn * jnp.mean(u * xn, axis=-1, keepdims=True))
    return dx, dg


def _lnsilu_bwd(x, g, b, dout):
    mu = jnp.mean(x, axis=-1, keepdims=True)
    xc = x - mu
    rs = lax.rsqrt(jnp.mean(xc * xc, axis=-1, keepdims=True) + LN_EPS)
    yh = xc * rs
    z = yh * g + b
    sg = jax.nn.sigmoid(z)
    dz = dout * (sg * (1.0 + z * (1.0 - sg)))
    dg = jnp.sum(dz * yh, axis=0, keepdims=True)
    db = jnp.sum(dz, axis=0, keepdims=True)
    dyh = dz * g
    dx = rs * (dyh - jnp.mean(dyh, axis=-1, keepdims=True) - yh * jnp.mean(dyh * yh, axis=-1, keepdims=True))
    return dx, dg, db


def _rope(v, c, s1, s2):
    return v * c + pltpu.roll(v, BLK - 8, 1) * s1 + pltpu.roll(v, 8, 1) * s2


def _rows(i, tm):
    return i * tm + lax.broadcasted_iota(jnp.int32, (tm, 1), 0)


def _place():
    return lax.axis_index("x"), lax.axis_index("y"), lax.axis_index("c")


class _Gather:
    def __init__(self, arrs, part=0, nparts=1, into=None):
        n = len(arrs)
        self.n, self.part, self.nparts = n, part, nparts
        self.ins = list(arrs) + list(into or [])
        self.alias = {n + k: k for k in range(n)} if into else {}
        self.out_shape = [jax.ShapeDtypeStruct((N_DEV * a.shape[0], a.shape[1]), a.dtype) for a in arrs]
        self.scratch = [pltpu.SemaphoreType.DMA((n, 7)), pltpu.SemaphoreType.DMA((n, 7)), pltpu.SemaphoreType.DMA((n,))]

    def _parts(self, ins, outs, sems):
        send_sems, recv_sems, local_sems = sems
        n = self.n
        x, y, c = _place()
        me, sibling = (x, y, c), (x, y, 1 - c)
        chips = [(1 - x, y), (x, 1 - y), (1 - x, 1 - y)]

        def share(a):
            rs = self.ins[a].shape[0] // self.nparts
            return self.part * rs, rs

        def rows(a, p):
            lo, rs = share(a)
            align = 16 if self.ins[a].dtype == BF16 else 8
            start = pl.multiple_of((4 * p[0] + 2 * p[1] + p[2]) * self.ins[a].shape[0] + lo, align)
            return outs[a].at[pl.ds(start, rs), :]

        ins = [ins[a].at[pl.ds(share(a)[0], share(a)[1]), :] for a in range(n)]

        def copy(a, k, block, to, src=None):
            return pltpu.make_async_remote_copy(
                src_ref=rows(a, block) if src is None else src, dst_ref=rows(a, block),
                send_sem=send_sems.at[a, k], recv_sem=recv_sems.at[a, k], device_id=to, device_id_type=MESH)

        mine = [pltpu.make_async_copy(ins[a], rows(a, me), local_sems.at[a]) for a in range(n)]
        first = []
        for a in range(n):
            first.append(copy(a, 0, me, sibling, src=ins[a]))
            first += [copy(a, 1 + j, me, (*chip, c), src=ins[a]) for j, chip in enumerate(chips)]
        return n, c, me, sibling, chips, copy, mine, first

    def start(self, ins, outs, sems):
        *_, mine, first = self._parts(ins, outs, sems)
        for cp in mine + first:
            cp.start()

    def finish(self, ins, outs, sems):
        n, c, me, sibling, chips, copy, mine, first = self._parts(ins, outs, sems)
        passed = []
        for j, chip in enumerate(chips):
            for a in range(n):
                copy(a, 1 + j, (*chip, c), me).wait_recv()
                fwd = copy(a, 4 + j, (*chip, c), sibling)
                fwd.start()
                passed.append(fwd)
        for a in range(n):
            copy(a, 0, sibling, me).wait_recv()
            for j, chip in enumerate(chips):
                copy(a, 4 + j, (*chip, 1 - c), me).wait_recv()
        for cp in first + passed:
            cp.wait_send()
        for cp in mine:
            cp.wait()


class _GatherRelay:
    N_COPY = 13

    def __init__(self, arr):
        self.ins = [arr]
        self.r = arr.shape[0]
        self.out_shape = [jax.ShapeDtypeStruct((N_DEV * self.r, arr.shape[1]), arr.dtype)]
        self.scratch = [pltpu.SemaphoreType.DMA((self.N_COPY,)), pltpu.SemaphoreType.DMA((self.N_COPY,)),
                        pltpu.SemaphoreType.DMA]

    def _parts(self, ins, outs, sems):
        send_sems, recv_sems, local_sem = sems
        x, y, c = _place()
        r, half = self.r, self.r // 2
        out = outs[0]
        me, sib, xn, yn, dg = (x, y, c), (x, y, 1 - c), (1 - x, y, c), (x, 1 - y, c), (1 - x, 1 - y, c)
        sx, sy, sd = (1 - x, y, 1 - c), (x, 1 - y, 1 - c), (1 - x, 1 - y, 1 - c)
        lo, hi = (0, half), (half, half)

        def rows(p, part=(0, r)):
            return out.at[pl.ds(pl.multiple_of((4 * p[0] + 2 * p[1] + p[2]) * r + part[0], 16), part[1]), :]

        def own(part):
            return ins[0].at[pl.ds(part[0], part[1]), :]

        def copy(k, dev_rows, to, src=None):
            return pltpu.make_async_remote_copy(
                src_ref=dev_rows if src is None else src, dst_ref=dev_rows,
                send_sem=send_sems.at[k], recv_sem=recv_sems.at[k], device_id=to, device_id_type=MESH)

        mine = pltpu.make_async_copy(ins[0], rows(me), local_sem)
        first = [copy(0, rows(me), sib, src=ins[0]),
                 copy(1, rows(me, lo), xn, src=own(lo)), copy(3, rows(me, hi), yn, src=own(hi)),
                 copy(2, rows(me, hi), xn, src=own(hi)), copy(4, rows(me, lo), yn, src=own(lo))]
        arrive = {0: rows(sib), 1: rows(xn, lo), 2: rows(xn, hi), 3: rows(yn, hi), 4: rows(yn, lo),
                  5: rows(dg, lo), 6: rows(dg, hi), 7: rows(sx, lo), 8: rows(sx, hi), 9: rows(sy, hi),
                  10: rows(sy, lo), 11: rows(sd, lo), 12: rows(sd, hi)}
        relay = {1: [(5, rows(xn, lo), yn), (7, rows(xn, lo), sib)], 3: [(6, rows(yn, hi), xn), (9, rows(yn, hi), sib)],
                 2: [(8, rows(xn, hi), sib)], 4: [(10, rows(yn, lo), sib)],
                 5: [(11, rows(dg, lo), sib)], 6: [(12, rows(dg, hi), sib)]}
        return copy, mine, first, arrive, relay, me

    def start(self, ins, outs, sems):
        _, mine, first, _, _, _ = self._parts(ins, outs, sems)
        for cp in [mine] + first:
            cp.start()

    def finish(self, ins, outs, sems):
        copy, mine, first, arrive, relay, me = self._parts(ins, outs, sems)
        passed = []
        for k in (1, 3, 2, 4, 5, 6):
            copy(k, arrive[k], me).wait_recv()
            for k2, dev_rows, to in relay[k]:
                fwd = copy(k2, dev_rows, to)
                fwd.start()
                passed.append(fwd)
        for k in (0, 7, 8, 9, 10, 11, 12):
            copy(k, arrive[k], me).wait_recv()
        for cp in first + passed:
            cp.wait_send()
        mine.wait()


FLIPS = [(0, 0, 1), (1, 0, 0), (0, 1, 0), (1, 1, 0), (1, 0, 1), (0, 1, 1), (1, 1, 1)]


class _Scatter:
    def __init__(self, arrs, part=0, nparts=1):
        self.ins = list(arrs)
        self.part, self.nparts = part, nparts
        n = len(arrs)
        self.out_shape = [jax.ShapeDtypeStruct((a.shape[0] // nparts, a.shape[1]), a.dtype) for a in arrs]
        self.scratch = [pltpu.SemaphoreType.DMA((n, 7)), pltpu.SemaphoreType.DMA((n, 7)), pltpu.SemaphoreType.DMA((n,))]

    def _parts(self, ins, outs, sems):
        send_sems, recv_sems, local_sems = sems
        n = len(ins)
        x, y, c = _place()
        me = 4 * x + 2 * y + c

        def flip(v, f):
            return 1 - v if f else v

        def src(a, idx):
            r = self.ins[a].shape[0] // N_DEV
            rs = r // self.nparts
            return ins[a].at[pl.ds(pl.multiple_of(idx * r + self.part * rs, 16), rs), :]

        def dst(a, idx):
            rs = self.ins[a].shape[0] // N_DEV // self.nparts
            return outs[a].at[pl.ds(pl.multiple_of(idx * rs, 16), rs), :]

        mine = [pltpu.make_async_copy(src(a, me), dst(a, me), local_sems.at[a]) for a in range(n)]
        sends, recvs = [], []
        for k, f in enumerate(FLIPS):
            peer = (flip(x, f[0]), flip(y, f[1]), flip(c, f[2]))
            pidx = 4 * peer[0] + 2 * peer[1] + peer[2]
            for a in range(n):
                sends.append(pltpu.make_async_remote_copy(
                    src_ref=src(a, pidx), dst_ref=dst(a, me),
                    send_sem=send_sems.at[a, k], recv_sem=recv_sems.at[a, k], device_id=peer, device_id_type=MESH))
                recvs.append(functools.partial(
                    pltpu.make_async_remote_copy,
                    src_ref=src(a, pidx), dst_ref=dst(a, pidx),
                    send_sem=send_sems.at[a, k], recv_sem=recv_sems.at[a, k], device_id=peer, device_id_type=MESH))
        return mine, sends, recvs

    def start(self, ins, outs, sems):
        mine, sends, _ = self._parts(ins, outs, sems)
        for cp in mine + sends:
            cp.start()

    def finish(self, ins, outs, sems):
        mine, sends, recvs = self._parts(ins, outs, sems)
        for make in recvs:
            make().wait_recv()
        for cp in sends:
            cp.wait_send()
        for cp in mine:
            cp.wait()


N_CHIP = 4


class _SiblingSwap:
    def __init__(self, arr):
        self.ins = [arr]
        self.r = arr.shape[0] // N_DEV
        self.out_shape = [jax.ShapeDtypeStruct((N_CHIP * self.r, arr.shape[1]), arr.dtype)]
        self.scratch = [pltpu.SemaphoreType.DMA((N_CHIP,)), pltpu.SemaphoreType.DMA((N_CHIP,))]

    def _copies(self, ins, outs, sems):
        send_sems, recv_sems = sems
        x, y, c = _place()
        r = self.r
        return [pltpu.make_async_remote_copy(
            src_ref=ins[0].at[pl.ds(pl.multiple_of((2 * j + 1 - c) * r, 16), r), :],
            dst_ref=outs[0].at[pl.ds(j * r, r), :],
            send_sem=send_sems.at[j], recv_sem=recv_sems.at[j], device_id=(x, y, 1 - c), device_id_type=MESH)
            for j in range(N_CHIP)]

    def start(self, ins, outs, sems):
        for cp in self._copies(ins, outs, sems):
            cp.start()

    def finish(self, ins, outs, sems):
        for cp in self._copies(ins, outs, sems):
            cp.wait()


class _ChipScatter:
    def __init__(self, arr):
        self.ins = [arr]
        self.r = arr.shape[0] // N_CHIP
        self.out_shape = [jax.ShapeDtypeStruct(arr.shape, arr.dtype)]
        self.scratch = [pltpu.SemaphoreType.DMA((3,)), pltpu.SemaphoreType.DMA((3,)), pltpu.SemaphoreType.DMA]

    def _parts(self, ins, outs, sems):
        send_sems, recv_sems, local_sem = sems
        x, y, c = _place()
        r = self.r
        my_chip = 2 * x + y

        def rows(ref, j):
            return ref.at[pl.ds(pl.multiple_of(j * r, 16), r), :]

        mine = pltpu.make_async_copy(rows(ins[0], my_chip), rows(outs[0], my_chip), local_sem)
        sends, recvs = [], []
        for k, (fx, fy) in enumerate(((1, 0), (0, 1), (1, 1))):
            px, py = (1 - x if fx else x), (1 - y if fy else y)
            peer_chip = 2 * px + py
            sends.append(pltpu.make_async_remote_copy(
                src_ref=rows(ins[0], peer_chip), dst_ref=rows(outs[0], my_chip),
                send_sem=send_sems.at[k], recv_sem=recv_sems.at[k], device_id=(px, py, c), device_id_type=MESH))
            recvs.append(functools.partial(
                pltpu.make_async_remote_copy,
                src_ref=rows(ins[0], peer_chip), dst_ref=rows(outs[0], peer_chip),
                send_sem=send_sems.at[k], recv_sem=recv_sems.at[k], device_id=(px, py, c), device_id_type=MESH))
        return mine, sends, recvs

    def start(self, ins, outs, sems):
        mine, sends, _ = self._parts(ins, outs, sems)
        for cp in [mine] + sends:
            cp.start()

    def finish(self, ins, outs, sems):
        mine, sends, recvs = self._parts(ins, outs, sems)
        for make in recvs:
            make().wait_recv()
        for cp in sends:
            cp.wait_send()
        mine.wait()


def _pair_add(partial, recv):
    r = recv.shape[0] // N_CHIP
    cols = recv.shape[1]
    tr = r // 2 if (r // 2) % 16 == 0 else r
    steps = r // tr
    core = lax.axis_index("c").astype(jnp.int32).reshape(1)

    def body(c_ref, p_ref, s_ref, o_ref):
        o_ref[...] = (p_ref[...].astype(F32) + s_ref[...].astype(F32)).astype(BF16)

    spec = pl.BlockSpec((tr, cols), lambda j, i, c_ref: (j * steps + i, 0))
    return pl.pallas_call(
        body, name="pair_add",
        grid_spec=pltpu.PrefetchScalarGridSpec(
            num_scalar_prefetch=1, grid=(N_CHIP, steps),
            in_specs=[pl.BlockSpec((tr, cols), lambda j, i, c_ref: ((2 * j + c_ref[0]) * steps + i, 0)), spec],
            out_specs=spec),
        out_shape=jax.ShapeDtypeStruct(recv.shape, BF16),
        compiler_params=_cparams("parallel", "parallel"),
    )(core, partial, recv)


class _Both:
    def __init__(self, a, b):
        self.a, self.b = a, b
        self.ins = a.ins + b.ins
        self.out_shape = a.out_shape + b.out_shape
        self.scratch = a.scratch + b.scratch

    def _split(self, ins, outs, sems):
        ni, no, ns = len(self.a.ins), len(self.a.out_shape), len(self.a.scratch)
        return (ins[:ni], outs[:no], sems[:ns]), (ins[ni:], outs[no:], sems[ns:])

    def start(self, ins, outs, sems):
        ra, rb = self._split(ins, outs, sems)
        self.a.start(*ra)
        self.b.start(*rb)

    def finish(self, ins, outs, sems):
        ra, rb = self._split(ins, outs, sems)
        self.a.finish(*ra)
        self.b.finish(*rb)


def _exchange(comm, name):
    n, m = len(comm.ins), len(comm.out_shape)

    def body(*refs):
        ins, outs, sems = refs[:n], refs[n:n + m], refs[n + m:]
        comm.start(ins, outs, sems)
        comm.finish(ins, outs, sems)

    return pl.pallas_call(
        body, name=name, out_shape=comm.out_shape, in_specs=[ANY] * n, out_specs=[ANY] * m, scratch_shapes=comm.scratch,
    )(*comm.ins)


def _call(body, *, name, grid, in_specs, out_specs, out_shape, args, scratch=(), sem="parallel", comm=None):
    if comm is None:
        outs = pl.pallas_call(
            body, name=name, grid=grid, in_specs=list(in_specs), out_specs=list(out_specs), out_shape=list(out_shape),
            scratch_shapes=list(scratch), compiler_params=_cparams(sem))(*args)
        return outs, []
    n_in, n_out, n_sc = len(in_specs), len(out_specs), len(scratch)
    n_ci, n_co = len(comm.ins), len(comm.out_shape)
    last = grid[0] - 1

    def fused(*refs):
        ins, refs = refs[:n_in], refs[n_in:]
        c_ins, refs = refs[:n_ci], refs[n_ci:]
        outs, refs = refs[:n_out], refs[n_out:]
        c_outs, refs = refs[:n_co], refs[n_co:]
        sc, c_sems = refs[:n_sc], refs[n_sc:]
        step = pl.program_id(0)

        @pl.when(step == 0)
        def _():
            comm.start(c_ins, c_outs, c_sems)

        body(*ins, *outs, *sc)

        @pl.when(step == last)
        def _():
            comm.finish(c_ins, c_outs, c_sems)

    aliases = {n_in + i: n_out + o for i, o in getattr(comm, "alias", {}).items()}
    outs = pl.pallas_call(
        fused, name=name, grid=grid, in_specs=list(in_specs) + [ANY] * n_ci, out_specs=list(out_specs) + [ANY] * n_co,
        out_shape=list(out_shape) + comm.out_shape, scratch_shapes=list(scratch) + comm.scratch,
        input_output_aliases=aliases, compiler_params=_cparams("arbitrary"))(*args, *comm.ins)
    return outs[:n_out], outs[n_out:]


def _token_specs(tm):
    k = tm // BLK
    return [pl.BlockSpec((BLK, D), functools.partial(lambda i, t: (jnp.maximum(k * i + t - 1, 0), 0), t=t)) for t in range(k)]


def _in_proj(x2d, meta, gain, w_int, b_in, tabs, comm=None):
    p = x2d.shape[0] + BLK
    tm = _row_tile(p)
    k = tm // BLK

    def body(*refs):
        x_refs = refs[:k]
        m_ref, g_ref, w_ref, b_ref, t_ref, h_ref, n1_ref, q_ref, kv_ref, ag_ref, gt_ref = refs[k:]
        i = pl.program_id(0)
        head = jnp.concatenate([jnp.zeros((PAD, D), F32), m_ref[...]], axis=0)
        first = jnp.where(i == 0, head, x_refs[0][...])
        h = jnp.concatenate([first] + [r[...] for r in x_refs[1:]], axis=0) if k > 1 else first
        h_ref[...] = h
        n = _rms(h, g_ref[...]).astype(BF16)
        n1_ref[...] = n
        c, s1, s2 = t_ref[:, 0:128], t_ref[:, 128:256], t_ref[:, 256:384]

        def mm(c0, w):
            return _dot_nt(n, w_ref[c0:c0 + w, :]) + b_ref[:, c0:c0 + w]

        for j in range(4):
            acc = mm(256 * j, 256)
            for t in range(2):
                lo = 256 * j + 128 * t
                q_ref[:, lo:lo + 128] = (_rope(acc[:, 128 * t:128 * (t + 1)], c, s1, s2) * SCALE).astype(BF16)
        acc = mm(1024, 256)
        kv_ref[:, 0:128] = _rope(acc[:, 0:128], c, s1, s2).astype(BF16)
        kv_ref[:, 128:256] = acc[:, 128:256].astype(BF16)
        for j in range(8):
            ag_ref[:, 256 * j:256 * (j + 1)] = mm(QKV_W + 256 * j, 256).astype(BF16)
        for j in range(8):
            gt_ref[:, 256 * j:256 * (j + 1)] = mm(QKV_W + 2048 + 256 * j, 256).astype(BF16)

    def row(w):
        return pl.BlockSpec((tm, w), lambda i: (i, 0))

    return _call(
        body, name="in_proj", grid=(p // tm,),
        in_specs=_token_specs(tm) + [VM, VM, VM, VM, row(384)],
        out_specs=[row(D), row(D), row(D), row(256), row(2048), row(2048)],
        out_shape=[jax.ShapeDtypeStruct((p, D), F32)] + [jax.ShapeDtypeStruct((p, w), BF16) for w in (D, D, 256, 2048, 2048)],
        args=(x2d,) * k + (meta, gain, w_int, b_in, tabs), comm=comm)


N_KEY = 2 * BLK + N_META


def _attn_setup(n, h, q_ref, km_ref, kp_ref, kc_ref):
    lo = lax.broadcasted_iota(jnp.int32, (BLK, BLK), 1) < HEAD_DIM
    lok = lax.broadcasted_iota(jnp.int32, (N_KEY, BLK), 1) < HEAD_DIM

    def dup(lanes):
        cat = jnp.concatenate([kp_ref[:, lanes], kc_ref[:, lanes], km_ref[PAD:BLK, lanes]], axis=0).astype(F32)
        rolled = pltpu.roll(cat, HEAD_DIM, 1)
        return (jnp.where(lok, cat, rolled) if h == 0 else jnp.where(lok, rolled, cat)).astype(BF16)

    k2 = dup(slice(0, 128))
    v2 = dup(slice(128, 256))
    qs = _stack_heads(q_ref, h, lo)

    kr = lax.broadcasted_iota(jnp.int32, (BLK, BLK), 0)
    tq = BLK * n + lax.broadcasted_iota(jnp.int32, (BLK, BLK), 1) - PAD
    t_p = BLK * (n - 1) + kr - PAD
    t_c = BLK * n + kr - PAD
    ok_p = jnp.logical_and(t_p >= N_META, tq - t_p < BLK)
    ok_c = jnp.logical_and(t_c >= N_META, t_c <= tq)
    ok_m = lax.broadcasted_iota(jnp.int32, (N_META, BLK), 0) <= BLK * n + lax.broadcasted_iota(jnp.int32, (N_META, BLK), 1) - PAD
    bias = jnp.concatenate([jnp.where(ok, 0.0, NEG_INF).astype(F32) for ok in (ok_p, ok_c, ok_m)], axis=0)
    return qs, k2, v2, bias, lok


def _attn_head(s, bias, sink):
    s = s + bias
    m = jnp.maximum(jnp.max(s, axis=0, keepdims=True), sink)
    e = jnp.exp(s - m)
    es = jnp.exp(sink - m)
    inv = 1.0 / (jnp.sum(e, axis=0, keepdims=True) + es)
    return e * inv, es * inv


def _stack_heads(ref, h, lo):
    pieces = []
    for jp in range(4):
        v = ref[:, BLK * (4 * h + jp):BLK * (4 * h + jp + 1)]
        zero = jnp.zeros_like(v)
        pieces += [jnp.where(lo, v, zero), jnp.where(lo, zero, v)]
    return jnp.concatenate(pieces, axis=0)


def _unstack_heads(v, jp, lo):
    return jnp.where(lo, v[256 * jp:256 * jp + 128], v[256 * jp + 128:256 * jp + 256])


def _attn_fwd(q, kv, sinks, comm=None):
    p = q.shape[0]
    nb = p // BLK

    def body(q_ref, km_ref, kp_ref, kc_ref, sink_ref, o_ref):
        n = pl.program_id(0)
        lo = lax.broadcasted_iota(jnp.int32, (BLK, BLK), 1) < HEAD_DIM
        for h in range(2):
            qs, k2, v2, bias, _ = _attn_setup(n, h, q_ref, km_ref, kp_ref, kc_ref)
            st = _dot_nt(k2, qs)
            pt = jnp.concatenate(
                [_attn_head(st[:, BLK * g:BLK * (g + 1)], bias, sink_ref[0, 8 * h + g])[0].astype(BF16) for g in range(8)],
                axis=1)
            o = _dot_tn(pt, v2)
            for jp in range(4):
                o_ref[:, BLK * (4 * h + jp):BLK * (4 * h + jp + 1)] = _unstack_heads(o, jp, lo).astype(BF16)

    return _call(
        body, name="attn_fwd", grid=(nb,),
        in_specs=[pl.BlockSpec((BLK, D), lambda i: (i, 0)),
                  pl.BlockSpec((BLK, 256), lambda i: (0, 0)),
                  pl.BlockSpec((BLK, 256), lambda i: (jnp.maximum(i - 1, 0), 0)),
                  pl.BlockSpec((BLK, 256), lambda i: (i, 0)),
                  pl.BlockSpec(memory_space=pltpu.SMEM)],
        out_specs=[pl.BlockSpec((BLK, D), lambda i: (i, 0))],
        out_shape=[jax.ShapeDtypeStruct((p, D), BF16)],
        args=(q, kv, kv, kv, sinks), comm=comm)


def _conv31_fwd(ag, w32, b, comm=None):
    p = ag.shape[0]
    nch = p // BLK

    def body(a_ref, g_ref, w_ref, b_ref, o_ref, gp):
        gp[0:32, :] = jnp.zeros((32, BLK), F32)
        for ci in range(nch):
            r0 = BLK * ci
            glu = a_ref[r0:r0 + BLK, :].astype(F32) * jax.nn.sigmoid(g_ref[r0:r0 + BLK, :].astype(F32))
            if ci == 0:
                glu = jnp.where(_rows(0, BLK) >= PAD, glu, 0.0)
            gp[32 + r0:32 + r0 + BLK, :] = glu
        for ci in range(nch):
            r0 = BLK * ci
            acc = jnp.broadcast_to(b_ref[...], (BLK, BLK))
            for j in range(CONV_K):
                acc = acc + w_ref[j:j + 1, :] * gp[r0 + j + 2:r0 + j + 2 + BLK, :]
            o_ref[r0:r0 + BLK, :] = acc

    return _call(
        body, name="conv31_fwd", grid=(D // BLK,),
        in_specs=[pl.BlockSpec((p, BLK), lambda j: (0, j)), pl.BlockSpec((p, BLK), lambda j: (0, 8 + j)),
                  pl.BlockSpec((32, BLK), lambda j: (0, j)), pl.BlockSpec((1, BLK), lambda j: (0, j))],
        out_specs=[pl.BlockSpec((p, BLK), lambda j: (0, j))],
        out_shape=[jax.ShapeDtypeStruct((p, D), F32)],
        scratch=[pltpu.VMEM((p + 32, BLK), F32)],
        args=(ag, ag, w32, b), comm=comm)


def _mixer_fwd(ao, c0, gates, h0p, wa, wc, wo, vecs, comm=None):
    p = ao.shape[0]
    tm = _row_tile(p)

    def body(ao_ref, c0_ref, gt_ref, h_ref, wa_ref, wc_ref, wo_ref, v_ref,
             c1_ref, at_ref, cv_ref, mg_ref, mix_ref, h1_ref, n2_ref):
        i = pl.program_id(0)
        c1 = _lnsilu(c0_ref[...], v_ref[0:1, :], v_ref[1:2, :]).astype(BF16)
        c1_ref[...] = c1
        attn = _dot(ao_ref[...], wa_ref[...])
        conv = _dot(c1, wc_ref[...]) + v_ref[2:3, :]
        at_ref[...] = attn.astype(BF16)
        cv_ref[...] = conv.astype(BF16)
        merged = (jax.nn.sigmoid(gt_ref[:, 0:D].astype(F32)) * attn
                  + jax.nn.sigmoid(gt_ref[:, D:2 * D].astype(F32)) * conv).astype(BF16)
        mg_ref[...] = merged
        mix = _dot(merged, wo_ref[...])
        mix_ref[...] = mix
        h1 = jnp.where(_rows(i, tm) >= PAD, h_ref[...] + _rms(mix, v_ref[3:4, :]), 0.0)
        h1_ref[...] = h1
        n2_ref[...] = _rms(h1, v_ref[4:5, :]).astype(BF16)

    def row(w):
        return pl.BlockSpec((tm, w), lambda i: (i, 0))

    return _call(
        body, name="mixer_fwd", grid=(p // tm,),
        in_specs=[row(D), row(D), row(2 * D), row(D), VM, VM, VM, VM],
        out_specs=[row(D)] * 7,
        out_shape=[jax.ShapeDtypeStruct((p, D), t) for t in (BF16, BF16, BF16, BF16, F32, F32, BF16)],
        args=(ao, c0, gates, h0p, wa, wc, wo, vecs), comm=comm)


def _mm_nt(a, w_t, name, comm=None):
    p, k = a.shape
    n = w_t.shape[0]
    tm = _row_tile(p)
    ch = 512

    def body(a_ref, w_ref, o_ref):
        a_v = a_ref[...]
        for c0 in range(0, n, ch):
            o_ref[:, c0:c0 + ch] = _dot_nt(a_v, w_ref[c0:c0 + ch, :]).astype(BF16)

    return _call(
        body, name=name, grid=(p // tm,),
        in_specs=[pl.BlockSpec((tm, k), lambda i: (i, 0)), VM],
        out_specs=[pl.BlockSpec((tm, n), lambda i: (i, 0))],
        out_shape=[jax.ShapeDtypeStruct((p, n), BF16)],
        args=(a, w_t), comm=comm)


def _conv3(xp_ref, w_ref, r0):
    return (w_ref[0:1, :] * xp_ref[r0 + 6:r0 + 6 + BLK, :] + w_ref[1:2, :] * xp_ref[r0 + 7:r0 + 7 + BLK, :]
            + w_ref[2:3, :] * xp_ref[r0 + 8:r0 + 8 + BLK, :])


def _ffn_slab_specs(p):
    ncol = FFN // BLK
    return [pl.BlockSpec((p, BLK), lambda j: (0, j)), pl.BlockSpec((p, BLK), lambda j: (0, ncol + j)),
            pl.BlockSpec((FFN_K, BLK), lambda j: (0, j)), pl.BlockSpec((FFN_K, BLK), lambda j: (0, ncol + j)),
            pl.BlockSpec((1, BLK), lambda j: (0, j)), pl.BlockSpec((1, BLK), lambda j: (0, ncol + j))]


def _fill_shifted(dst, src_ref, nch):
    dst[0:8, :] = jnp.zeros((8, BLK), F32)
    for ci in range(nch):
        dst[8 + BLK * ci:8 + BLK * (ci + 1), :] = src_ref[BLK * ci:BLK * (ci + 1), :].astype(F32)


def _ffn_act(u0, fw, fb):
    p = u0.shape[0]
    nch = p // BLK

    def body(g_ref, v_ref, wg_ref, wv_ref, bg_ref, bv_ref, o_ref, dv_ref, dg_ref, xg, xv):
        _fill_shifted(xg, g_ref, nch)
        _fill_shifted(xv, v_ref, nch)
        for ci in range(nch):
            r0 = BLK * ci
            ug = _conv3(xg, wg_ref, r0) + bg_ref[...]
            uv = _conv3(xv, wv_ref, r0) + bv_ref[...]
            sg = jax.nn.sigmoid(ug)
            silu = ug * sg
            o_ref[r0:r0 + BLK, :] = (silu * uv).astype(BF16)
            dv_ref[r0:r0 + BLK, :] = silu.astype(BF16)
            dg_ref[r0:r0 + BLK, :] = (uv * (sg * (1.0 + ug * (1.0 - sg)))).astype(BF16)

    slab = pl.BlockSpec((p, BLK), lambda j: (0, j))
    return pl.pallas_call(
        body, name="ffn_act", grid=(FFN // BLK,),
        in_specs=_ffn_slab_specs(p),
        out_specs=[slab] * 3,
        out_shape=[jax.ShapeDtypeStruct((p, FFN), BF16)] * 3,
        scratch_shapes=[pltpu.VMEM((p + 8, BLK), F32)] * 2,
        compiler_params=_cparams("parallel"),
    )(u0, u0, fw, fw, fb, fb)


def _ffn_down_loss(act, wd, h1, tgt, gain):
    p = act.shape[0]
    tm = _row_tile(p)
    k = tm // BLK

    def body(*refs):
        a_ref, w_ref, h_ref = refs[:3]
        t_refs = refs[3:3 + k]
        g_ref, df_ref, da_ref, dy_ref, acc_ref = refs[3 + k:]
        i = pl.program_id(0)

        @pl.when(i == 0)
        def _():
            acc_ref[...] = jnp.zeros_like(acc_ref)

        ffn = _dot(a_ref[...], w_ref[...])
        t = jnp.concatenate([t_ref[...] for t_ref in t_refs], axis=0) if k > 1 else t_refs[0][...]
        diff = jnp.where(_rows(i, tm) >= BLK, h_ref[...] + _rms(ffn, g_ref[...]) - t, 0.0)
        dy = diff * (1.0 / D)
        dffn, dg = _rms_bwd(ffn, g_ref[...], dy)
        acc_ref[0:1, :] += dg
        acc_ref[1:2, :] += jnp.sum(diff * diff, axis=0, keepdims=True) * (0.5 / D)
        dy_ref[...] = dy
        dfb = dffn.astype(BF16)
        df_ref[...] = dfb
        for c0 in range(0, FFN, 256):
            da_ref[:, c0:c0 + 256] = _dot_nt(dfb, w_ref[c0:c0 + 256, :]).astype(BF16)

    def row(w):
        return pl.BlockSpec((tm, w), lambda i: (i, 0))

    return pl.pallas_call(
        body, name="ffn_down_loss", grid=(p // tm,),
        in_specs=[row(FFN), VM, row(D)] + _token_specs(tm) + [VM],
        out_specs=[row(D), row(FFN), row(D), pl.BlockSpec((8, D), lambda i: (0, 0))],
        out_shape=[jax.ShapeDtypeStruct((p, D), BF16), jax.ShapeDtypeStruct((p, FFN), BF16),
                   jax.ShapeDtypeStruct((p, D), F32), jax.ShapeDtypeStruct((8, D), F32)],
        compiler_params=_cparams("arbitrary"),
    )(act, wd, h1, *([tgt] * k), gain)


def _mm_tn(pieces, b, name, col_sums=False, comm=None):
    p, n = b.shape
    tk = 256
    nblk = [a.shape[1] // tk for a in pieces]
    offs = [sum(nblk[:q]) for q in range(len(pieces))]
    total = sum(nblk)
    npc = len(pieces)

    def body(*refs):
        a_refs, b_ref, o_ref = refs[:npc], refs[npc], refs[npc + 1]
        i = pl.program_id(0)
        for q, a_ref in enumerate(a_refs):
            @pl.when(jnp.logical_and(i >= offs[q], i < offs[q] + nblk[q]))
            def _(a_ref=a_ref):
                a_v = a_ref[...]
                o_ref[...] = _dot_tn(a_v, b_ref[...]).astype(BF16)
                if col_sums:
                    refs[npc + 2][...] = jnp.sum(a_v.astype(F32), axis=0, keepdims=True)

    def a_spec(q):
        return pl.BlockSpec((p, tk), lambda i: (0, jnp.clip(i - offs[q], 0, nblk[q] - 1)))

    out_specs = [pl.BlockSpec((tk, n), lambda i: (i, 0))]
    out_shape = [jax.ShapeDtypeStruct((total * tk, n), BF16)]
    if col_sums:
        out_specs.append(pl.BlockSpec((1, tk), lambda i: (0, i)))
        out_shape.append(jax.ShapeDtypeStruct((1, total * tk), F32))
    res, sent = _call(
        body, name=name, grid=(total,),
        in_specs=[a_spec(q) for q in range(npc)] + [VM],
        out_specs=out_specs, out_shape=out_shape, args=(*pieces, b), comm=comm)
    res = res if col_sums else res[0]
    return res if comm is None else (res, sent)


def _ffn_act_bwd(u0, dact, dact_dg, dact_dv, fw, act, dffn, comm=None):
    p = u0.shape[0]
    nch = p // BLK
    ncol = FFN // BLK

    def body(g_ref, v_ref, wg_ref, wv_ref, da_ref, lg_ref, lv_ref, act_ref, df_ref,
             dg_ref, dv_ref, gwg_ref, gwv_ref, gbg_ref, gbv_ref, gwd_ref, eg, ev):
        gwd_ref[...] = _dot_tn(act_ref[...], df_ref[...]).astype(BF16)
        eg[p:p + 8, :] = jnp.zeros((8, BLK), F32)
        ev[p:p + 8, :] = jnp.zeros((8, BLK), F32)
        for ci in range(nch):
            r0 = BLK * ci
            d = da_ref[r0:r0 + BLK, :].astype(F32)
            eg[r0:r0 + BLK, :] = d * lg_ref[r0:r0 + BLK, :].astype(F32)
            ev[r0:r0 + BLK, :] = d * lv_ref[r0:r0 + BLK, :].astype(F32)
        def fold(v):
            return jnp.sum(v.reshape(BLK // 8, 8, BLK), axis=0)

        for e_s, x_ref, w_ref, d_ref, gw_ref, gb_ref in ((eg, g_ref, wg_ref, dg_ref, gwg_ref, gbg_ref),
                                                        (ev, v_ref, wv_ref, dv_ref, gwv_ref, gbv_ref)):
            sums = [jnp.zeros((8, BLK), F32) for _ in range(FFN_K + 1)]
            for ci in range(nch):
                r0 = BLK * ci
                es = [e_s[r0 + t:r0 + t + BLK, :] for t in range(FFN_K)]
                du = w_ref[2:3, :] * es[0] + w_ref[1:2, :] * es[1] + w_ref[0:1, :] * es[2]
                if ci == 0:
                    du = jnp.where(_rows(0, BLK) >= PAD, du, 0.0)
                d_ref[r0:r0 + BLK, :] = du.astype(BF16)
                x = x_ref[r0:r0 + BLK, :].astype(F32)
                for j in range(FFN_K):
                    sums[j] = sums[j] + fold(es[FFN_K - 1 - j] * x)
                sums[FFN_K] = sums[FFN_K] + fold(es[0])
            for j in range(FFN_K):
                gw_ref[j:j + 1, :] = jnp.sum(sums[j], axis=0, keepdims=True)
            gb_ref[...] = jnp.sum(sums[FFN_K], axis=0, keepdims=True)

    slab = pl.BlockSpec((p, BLK), lambda j: (0, j))
    wspec = pl.BlockSpec((FFN_K, BLK), lambda j: (0, j))
    bspec = pl.BlockSpec((1, BLK), lambda j: (0, j))
    return _call(
        body, name="ffn_act_bwd", grid=(ncol,),
        in_specs=_ffn_slab_specs(p)[:4] + [slab] * 4 + [VM],
        out_specs=[slab, slab, wspec, wspec, bspec, bspec, pl.BlockSpec((BLK, D), lambda j: (j, 0))],
        out_shape=[jax.ShapeDtypeStruct((p, FFN), BF16)] * 2 + [jax.ShapeDtypeStruct((FFN_K, FFN), F32)] * 2
        + [jax.ShapeDtypeStruct((1, FFN), F32)] * 2 + [jax.ShapeDtypeStruct((FFN, D), BF16)],
        scratch=[pltpu.VMEM((p + 8, BLK), F32)] * 2,
        args=(u0, u0, fw, fw, dact, dact_dg, dact_dv, act, dffn), comm=comm)


def _ffn_in_bwd(dug, duv, w_upt, h1, dy, gain, comm=None):
    p = h1.shape[0]
    tm = _row_tile(p)

    def body(dg_ref, dv_ref, w_ref, h_ref, dy_ref, g_ref, o_ref, acc_ref):
        i = pl.program_id(0)

        @pl.when(i == 0)
        def _():
            acc_ref[...] = jnp.zeros_like(acc_ref)

        dn = _dot(dg_ref[...], w_ref[0:FFN, :]) + _dot(dv_ref[...], w_ref[FFN:2 * FFN, :])
        dh, dg = _rms_bwd(h_ref[...], g_ref[...], dn)
        o_ref[...] = dy_ref[...] + dh
        acc_ref[0:1, :] += dg

    def row(w):
        return pl.BlockSpec((tm, w), lambda i: (i, 0))

    return _call(
        body, name="ffn_in_bwd", grid=(p // tm,),
        in_specs=[row(FFN), row(FFN), VM, row(D), row(D), VM],
        out_specs=[row(D), pl.BlockSpec((8, D), lambda i: (0, 0))],
        out_shape=[jax.ShapeDtypeStruct((p, D), F32), jax.ShapeDtypeStruct((8, D), F32)],
        sem="arbitrary", args=(dug, duv, w_upt, h1, dy, gain), comm=comm)


def _mixer_bwd(dh1, mix, attn, conv, gates, c0, wa, wc, wo, vecs, comm=None):
    p = dh1.shape[0]
    tm = _row_tile(p)

    def body(dh_ref, mix_ref, at_ref, cv_ref, gt_ref, c0_ref, wa_ref, wc_ref, wo_ref, v_ref,
             dmix_ref, dat_ref, dcv_ref, dgt_ref, dao_ref, dc0_ref, acc_ref):
        i = pl.program_id(0)

        @pl.when(i == 0)
        def _():
            acc_ref[...] = jnp.zeros_like(acc_ref)

        dmix, dgp = _rms_bwd(mix_ref[...], v_ref[3:4, :], dh_ref[...])
        dmix = dmix.astype(BF16)
        dmix_ref[...] = dmix
        dmg = _dot_nt(dmix, wo_ref[...])
        sa = jax.nn.sigmoid(gt_ref[:, 0:D].astype(F32))
        sc = jax.nn.sigmoid(gt_ref[:, D:2 * D].astype(F32))
        dat = dmg * sa
        dcv = dmg * sc
        dgt_ref[:, 0:D] = (dmg * at_ref[...].astype(F32) * sa * (1.0 - sa)).astype(BF16)
        dgt_ref[:, D:2 * D] = (dmg * cv_ref[...].astype(F32) * sc * (1.0 - sc)).astype(BF16)
        datb = dat.astype(BF16)
        dcvb = dcv.astype(BF16)
        dat_ref[...] = datb
        dcv_ref[...] = dcvb
        dao_ref[...] = _dot_nt(datb, wa_ref[...]).astype(BF16)
        dc1 = _dot_nt(dcvb, wc_ref[...])
        dc0, dlg, dlb = _lnsilu_bwd(c0_ref[...], v_ref[0:1, :], v_ref[1:2, :], dc1)
        dc0_ref[...] = dc0
        acc_ref[0:1, :] += dgp
        acc_ref[1:2, :] += jnp.sum(dcv, axis=0, keepdims=True)
        acc_ref[2:3, :] += dlg
        acc_ref[3:4, :] += dlb

    def row(w):
        return pl.BlockSpec((tm, w), lambda i: (i, 0))

    return _call(
        body, name="mixer_bwd", grid=(p // tm,),
        in_specs=[row(D), row(D), row(D), row(D), row(2 * D), row(D), VM, VM, VM, VM],
        out_specs=[row(D), row(D), row(D), row(2 * D), row(D), row(D), pl.BlockSpec((8, D), lambda i: (0, 0))],
        out_shape=[jax.ShapeDtypeStruct((p, D), BF16)] * 3 + [jax.ShapeDtypeStruct((p, 2 * D), BF16),
                                                             jax.ShapeDtypeStruct((p, D), BF16),
                                                             jax.ShapeDtypeStruct((p, D), F32),
                                                             jax.ShapeDtypeStruct((8, D), F32)],
        sem="arbitrary", args=(dh1, mix, attn, conv, gates, c0, wa, wc, wo, vecs), comm=comm)


def _conv31_bwd(ag, dc0, w32, tn_pairs, comm=None):
    p = ag.shape[0]
    nch = p // BLK
    npair = len(tn_pairs)

    def body(*refs):
        a_ref, g_ref, dc_ref, w_ref = refs[:4]
        tn_a, tn_b = refs[4:4 + npair], refs[4 + npair:4 + 2 * npair]
        da_ref, dg_ref, gw_ref, gb_ref = refs[4 + 2 * npair:8 + 2 * npair]
        tn_o = refs[8 + 2 * npair:8 + 3 * npair]
        gp, dp = refs[8 + 3 * npair:]
        for ta, tb, to in zip(tn_a, tn_b, tn_o):
            to[...] = _dot_tn(ta[...], tb[...]).astype(BF16)
        gp[0:32, :] = jnp.zeros((32, BLK), F32)
        dp[p:p + 32, :] = jnp.zeros((32, BLK), F32)
        bsum = jnp.zeros((BLK, BLK), F32)
        for ci in range(nch):
            r0 = BLK * ci
            glu = a_ref[r0:r0 + BLK, :].astype(F32) * jax.nn.sigmoid(g_ref[r0:r0 + BLK, :].astype(F32))
            if ci == 0:
                glu = jnp.where(_rows(0, BLK) >= PAD, glu, 0.0)
            gp[32 + r0:32 + r0 + BLK, :] = glu
            d = dc_ref[r0:r0 + BLK, :]
            dp[r0:r0 + BLK, :] = d
            bsum = bsum + d
        gb_ref[...] = jnp.sum(bsum, axis=0, keepdims=True)
        for ci in range(nch):
            r0 = BLK * ci
            acc = jnp.zeros((BLK, BLK), F32)
            for j in range(CONV_K):
                acc = acc + w_ref[j:j + 1, :] * dp[r0 + 30 - j:r0 + 30 - j + BLK, :]
            if ci == 0:
                acc = jnp.where(_rows(0, BLK) >= PAD, acc, 0.0)
            a = a_ref[r0:r0 + BLK, :].astype(F32)
            sg = jax.nn.sigmoid(g_ref[r0:r0 + BLK, :].astype(F32))
            da_ref[r0:r0 + BLK, :] = (acc * sg).astype(BF16)
            dg_ref[r0:r0 + BLK, :] = (acc * a * sg * (1.0 - sg)).astype(BF16)
        sub = BLK // 2
        accs = [jnp.zeros((8, BLK), F32) for _ in range(CONV_K)]
        for r0 in range(0, p, sub):
            d = dp[r0:r0 + sub, :]
            for j in range(CONV_K):
                prod = d * gp[r0 + j + 2:r0 + j + 2 + sub, :]
                accs[j] = accs[j] + jnp.sum(prod.reshape(sub // 8, 8, BLK), axis=0)
        for j in range(CONV_K):
            gw_ref[j:j + 1, :] = jnp.sum(accs[j], axis=0, keepdims=True)
        gw_ref[CONV_K:32, :] = jnp.zeros((32 - CONV_K, BLK), F32)

    slab = pl.BlockSpec((p, BLK), lambda j: (0, j))
    return _call(
        body, name="conv31_bwd", grid=(D // BLK,),
        in_specs=[slab, pl.BlockSpec((p, BLK), lambda j: (0, 8 + j)), slab, pl.BlockSpec((32, BLK), lambda j: (0, j))]
        + [slab] * npair + [VM] * npair,
        out_specs=[slab, slab, pl.BlockSpec((32, BLK), lambda j: (0, j)), pl.BlockSpec((1, BLK), lambda j: (0, j))]
        + [pl.BlockSpec((BLK, D), lambda j: (j, 0))] * npair,
        out_shape=[jax.ShapeDtypeStruct((p, D), BF16)] * 2 + [jax.ShapeDtypeStruct((32, D), F32),
                                                             jax.ShapeDtypeStruct((1, D), F32)]
        + [jax.ShapeDtypeStruct((D, D), BF16)] * npair,
        scratch=[pltpu.VMEM((p + 32, BLK), F32)] * 2,
        args=(ag, ag, dc0, w32, *[a for a, _ in tn_pairs], *[b for _, b in tn_pairs]), comm=comm)


def _attn_bwd(q, kv, dao, sinks, tabs, comm=None):
    p = q.shape[0]
    nb = p // BLK

    def body(q_ref, km_ref, kp_ref, kc_ref, do_ref, sink_ref, t_ref, dqkv_ref, dsink_ref, carry, macc):
        i = pl.program_id(0)
        n = nb - 1 - i

        @pl.when(i == 0)
        def _():
            carry[...] = jnp.zeros_like(carry)
            macc[...] = jnp.zeros_like(macc)
            dsink_ref[...] = jnp.zeros_like(dsink_ref)

        lo = lax.broadcasted_iota(jnp.int32, (BLK, BLK), 1) < HEAD_DIM
        lane8 = lax.broadcasted_iota(jnp.int32, (8, BLK), 1)
        c, s1, s2 = t_ref[:, 0:128], -t_ref[:, 128:256], -t_ref[:, 256:384]
        dk = jnp.zeros((N_KEY, BLK), F32)
        dv = jnp.zeros((N_KEY, BLK), F32)
        for h in range(2):
            qs, k2, v2, bias, lok = _attn_setup(n, h, q_ref, km_ref, kp_ref, kc_ref)
            dos = _stack_heads(do_ref, h, lo)
            st = _dot_nt(k2, qs)
            dpt = _dot_nt(v2, dos)
            p_parts, ds_parts = [], []
            for g in range(8):
                cols = slice(BLK * g, BLK * (g + 1))
                pn, ps = _attn_head(st[:, cols], bias, sink_ref[0, 8 * h + g])
                dp = dpt[:, cols]
                delta = jnp.sum(pn * dp, axis=0, keepdims=True)
                ds_parts.append((pn * (dp - delta)).astype(BF16))
                p_parts.append(pn.astype(BF16))
                dsk = -jnp.sum(ps * delta, axis=1, keepdims=True)
                dsink_ref[...] += jnp.where(lane8 == 8 * h + g, dsk, 0.0)
            dst = jnp.concatenate(ds_parts, axis=1)
            pt = jnp.concatenate(p_parts, axis=1)
            dq = _dot_tn(dst, k2)
            for jp in range(4):
                lo_c = BLK * (4 * h + jp)
                dqkv_ref[:, lo_c:lo_c + BLK] = (_rope(_unstack_heads(dq, jp, lo), c, s1, s2) * SCALE).astype(BF16)
            dk2 = _dot(dst, qs)
            dv2 = _dot(pt, dos)
            dk2 = dk2 + pltpu.roll(dk2, HEAD_DIM, 1)
            dv2 = dv2 + pltpu.roll(dv2, HEAD_DIM, 1)
            own = lok if h == 0 else jnp.logical_not(lok)
            dk = jnp.where(own, dk2, dk)
            dv = jnp.where(own, dv2, dv)
        macc[:, 0:BLK] += dk[2 * BLK:N_KEY]
        macc[:, BLK:2 * BLK] += dv[2 * BLK:N_KEY]
        last = (n == 0).astype(F32)
        zpad = jnp.zeros((PAD, BLK), F32)
        dk_c = dk[BLK:2 * BLK] + carry[:, 0:BLK] + last * jnp.concatenate([zpad, macc[:, 0:BLK]], axis=0)
        dv_c = dv[BLK:2 * BLK] + carry[:, BLK:2 * BLK] + last * jnp.concatenate([zpad, macc[:, BLK:2 * BLK]], axis=0)
        carry[:, 0:BLK] = dk[0:BLK]
        carry[:, BLK:2 * BLK] = dv[0:BLK]
        dqkv_ref[:, D:D + BLK] = _rope(dk_c, c, s1, s2).astype(BF16)
        dqkv_ref[:, D + BLK:D + 2 * BLK] = dv_c.astype(BF16)

    def rev(w):
        return pl.BlockSpec((BLK, w), lambda i: (nb - 1 - i, 0))

    return _call(
        body, name="attn_bwd", grid=(nb,),
        in_specs=[rev(D),
                  pl.BlockSpec((BLK, 256), lambda i: (0, 0)),
                  pl.BlockSpec((BLK, 256), lambda i: (jnp.maximum(nb - 2 - i, 0), 0)),
                  rev(256), rev(D),
                  pl.BlockSpec(memory_space=pltpu.SMEM), rev(384)],
        out_specs=[rev(QKV_W), pl.BlockSpec((8, BLK), lambda i: (0, 0))],
        out_shape=[jax.ShapeDtypeStruct((p, QKV_W), BF16), jax.ShapeDtypeStruct((8, BLK), F32)],
        scratch=[pltpu.VMEM((BLK, 256), F32), pltpu.VMEM((N_META, 256), F32)], sem="arbitrary",
        args=(q, kv, kv, kv, dao, sinks, tabs), comm=comm)


def _in_bwd(dqkv, da, dg, dgt, w_int, h0p, dh1, gain, comm=None):
    p = h0p.shape[0]
    tm = _row_tile(p)
    nt = p // tm
    first_rows = tm - BLK

    def body(dq_ref, da_ref, dg_ref, dt_ref, w_ref, h_ref, dh_ref, g_ref, gx_ref, dm_ref, acc_ref, buf, sems):
        i = pl.program_id(0)
        slot = i % 2

        @pl.when(i == 0)
        def _():
            acc_ref[...] = jnp.zeros_like(acc_ref)

        dn = (_dot(dq_ref[...], w_ref[0:QKV_W, :]) + _dot(da_ref[...], w_ref[QKV_W:QKV_W + D, :])
              + _dot(dg_ref[...], w_ref[QKV_W + D:QKV_W + 2 * D, :]) + _dot(dt_ref[...], w_ref[QKV_W + 2 * D:IN_W, :]))
        dh, dgain = _rms_bwd(h_ref[...], g_ref[...], dn)
        dh0 = dh_ref[...] + dh
        acc_ref[0:1, :] += dgain
        buf[slot] = dh0

        @pl.when(i == 0)
        def _():
            dm_ref[...] = dh0[PAD:BLK]

        def first_copy():
            return pltpu.make_async_copy(buf.at[0, pl.ds(BLK, first_rows), :], gx_ref.at[pl.ds(0, first_rows), :], sems.at[0])

        def tile_copy(j, s):
            return pltpu.make_async_copy(buf.at[s], gx_ref.at[pl.ds(pl.multiple_of(j * tm - BLK, BLK), tm), :], sems.at[s])

        if first_rows:
            @pl.when(i == 1)
            def _():
                first_copy().wait()

        @pl.when(i >= 2)
        def _():
            tile_copy(i - 1, 1 - slot).wait()

        if first_rows:
            @pl.when(i == 0)
            def _():
                first_copy().start()

        @pl.when(i > 0)
        def _():
            tile_copy(i, slot).start()

        @pl.when(i == nt - 1)
        def _():
            tile_copy(i, slot).wait()

    def row(w):
        return pl.BlockSpec((tm, w), lambda i: (i, 0))

    return _call(
        body, name="in_bwd", grid=(nt,),
        in_specs=[row(QKV_W), row(D), row(D), row(2 * D), VM, row(D), row(D), VM],
        out_specs=[ANY, pl.BlockSpec((N_META, D), lambda i: (0, 0)), pl.BlockSpec((8, D), lambda i: (0, 0))],
        out_shape=[jax.ShapeDtypeStruct((p - BLK, D), F32), jax.ShapeDtypeStruct((N_META, D), F32),
                   jax.ShapeDtypeStruct((8, D), F32)],
        scratch=[pltpu.VMEM((2, tm, D), F32), pltpu.SemaphoreType.DMA((2,))],
        sem="arbitrary", args=(dqkv, da, dg, dgt, w_int, h0p, dh1, gain), comm=comm)


def _sum_slots(slots, name):
    r = slots.shape[0] // N_DEV
    cols = slots.shape[1]
    tr = r if r <= 352 else (r // 2 if (r // 2) % 16 == 0 else r // 3)
    steps = r // tr

    def body(*refs):
        acc = refs[0][...].astype(F32)
        for s in range(1, N_DEV):
            acc = acc + refs[s][...].astype(F32)
        refs[N_DEV][...] = acc

    return pl.pallas_call(
        body, name=name, grid=(steps,),
        in_specs=[pl.BlockSpec((tr, cols), functools.partial(lambda i, s: (s * steps + i, 0), s=s)) for s in range(N_DEV)],
        out_specs=pl.BlockSpec((tr, cols), lambda i: (i, 0)),
        out_shape=jax.ShapeDtypeStruct((r, cols), F32),
        compiler_params=_cparams("parallel"),
    )(*([slots] * N_DEV))


def _adamw_math(w, g, m, v):
    m_n = ADAM_B1 * m + (1.0 - ADAM_B1) * g
    v_n = ADAM_B2 * v + (1.0 - ADAM_B2) * jnp.square(g)
    m_hat = m_n / (1.0 - ADAM_B1 ** ADAM_STEP)
    v_hat = v_n / (1.0 - ADAM_B2 ** ADAM_STEP)
    return -ADAM_LR * (m_hat / (jnp.sqrt(v_hat) + ADAM_EPS) + ADAM_WD * w), m_n, v_n


def _sum_adamw(parts, w, m, v, name, nslots=N_DEV):
    r, cols = w.shape
    rs = r // len(parts)
    tr = rs if rs <= 352 else (rs // 2 if (rs // 2) % 16 == 0 else rs // 3)
    steps = rs // tr

    def body(*refs):
        w_ref, m_ref, v_ref, g_ref, d_ref, nm_ref, nv_ref = refs[nslots * len(parts):]
        i = pl.program_id(0)
        for q in range(len(parts)):
            @pl.when(i // steps == q)
            def _(q=q):
                g = refs[nslots * q][...].astype(F32)
                for s in range(1, nslots):
                    g = g + refs[nslots * q + s][...].astype(F32)
                g_ref[...] = g
                d_ref[...], nm_ref[...], nv_ref[...] = _adamw_math(w_ref[...], g, m_ref[...], v_ref[...])

    def slot_spec(q, s):
        return pl.BlockSpec((tr, cols), lambda i: (s * steps + jnp.clip(i - q * steps, 0, steps - 1), 0))

    spec = pl.BlockSpec((tr, cols), lambda i: (i, 0))
    return pl.pallas_call(
        body, name=name, grid=(steps * len(parts),),
        in_specs=[slot_spec(q, s) for q in range(len(parts)) for s in range(nslots)] + [spec] * 3,
        out_specs=[spec] * 4, out_shape=[jax.ShapeDtypeStruct((r, cols), F32)] * 4,
        compiler_params=_cparams("parallel"),
    )(*[a for a in parts for _ in range(nslots)], w, m, v)


def _adamw_many(ws, gs, ms, vs, name):
    n = len(ws)

    def body(*refs):
        w, g, m, v = refs[0:n], refs[n:2 * n], refs[2 * n:3 * n], refs[3 * n:4 * n]
        d, nm, nv = refs[4 * n:5 * n], refs[5 * n:6 * n], refs[6 * n:7 * n]
        for k in range(n):
            d[k][...], nm[k][...], nv[k][...] = _adamw_math(w[k][...], g[k][...], m[k][...], v[k][...])

    outs = pl.pallas_call(
        body, name=name, in_specs=[VM] * (4 * n), out_specs=[VM] * (3 * n),
        out_shape=[jax.ShapeDtypeStruct(a.shape, F32) for a in ws] * 3,
    )(*ws, *gs, *ms, *vs)
    return outs[0:n], outs[n:2 * n], outs[2 * n:3 * n]


def _rope_tables(p):
    half = ROT_DIM // 2
    lane = jnp.arange(BLK)
    seg = (lane % HEAD_DIM) // half
    inv_freq = ROPE_THETA ** (-(lane % half).astype(F32) * 2.0 / ROT_DIM)
    pos = (jnp.arange(p) - PAD).astype(F32)
    ang = pos[:, None] * inv_freq[None, :]
    cos = jnp.cos(ang)
    sin = jnp.sin(ang)
    c = jnp.where(seg[None, :] < 2, cos, 1.0)
    s1 = jnp.where(seg[None, :] == 0, -sin, 0.0)
    s2 = jnp.where(seg[None, :] == 1, sin, 0.0)
    return jnp.concatenate([c, s1, s2], axis=1).astype(F32)


def _flat_pack(parts, rows):
    flat = jnp.concatenate([a.reshape(-1).astype(F32) for a in parts])
    return jnp.pad(flat, (0, rows * D - flat.shape[0])).reshape(rows, D)


def _flat_unpack(pack, shapes):
    flat = pack.reshape(-1)
    out, off = [], 0
    for s in shapes:
        size = 1
        for e in s:
            size *= e
        out.append(flat[off:off + size].reshape(s))
        off += size
    return out


def kernel(x, meta_tokens, norm_pre_mix, norm_post_mix, w_in, b_in, attn_sinks, w_attn_proj, conv_dw_w, conv_dw_b, conv_ln_g, conv_ln_b, w_conv_proj, b_conv_proj, w_out, norm_pre_ffn, norm_post_ffn, w_up, ffn_dw_w, ffn_dw_b, w_down, loss_target, m_meta_tokens, m_norm_pre_mix, m_norm_post_mix, m_w_in, m_b_in, m_attn_sinks, m_w_attn_proj, m_conv_dw_w, m_conv_dw_b, m_conv_ln_g, m_conv_ln_b, m_w_conv_proj, m_b_conv_proj, m_w_out, m_norm_pre_ffn, m_norm_post_ffn, m_w_up, m_ffn_dw_w, m_ffn_dw_b, m_w_down, v_meta_tokens, v_norm_pre_mix, v_norm_post_mix, v_w_in, v_b_in, v_attn_sinks, v_w_attn_proj, v_conv_dw_w, v_conv_dw_b, v_conv_ln_g, v_conv_ln_b, v_w_conv_proj, v_b_conv_proj, v_w_out, v_norm_pre_ffn, v_norm_post_ffn, v_w_up, v_ffn_dw_w, v_ffn_dw_b, v_w_down):
    seq = x.shape[1]
    p = seq + BLK
    me = 4 * lax.axis_index("x") + 2 * lax.axis_index("y") + lax.axis_index("c")
    in_cols = w_in.shape[2]
    up_cols = w_up.shape[2]

    small = jnp.zeros((56, up_cols), F32)
    small = small.at[0:N_META, 0:BLK].set(meta_tokens)
    small = small.at[16:16 + CONV_K, 0:BLK].set(conv_dw_w[0])
    small = small.at[48:48 + FFN_K, :].set(ffn_dw_w[0])
    w_int, small_all = _exchange(_Both(_GatherRelay(w_in[0].T.astype(BF16)), _Gather([small])), "gather_w_in")
    small_all = small_all.reshape(N_DEV, 56, up_cols)
    meta_full = small_all[:, 0:N_META, 0:BLK].transpose(1, 0, 2).reshape(N_META, D)
    cdw = small_all[:, 16:16 + CONV_K, 0:BLK].transpose(1, 0, 2).reshape(CONV_K, D)
    cdw32 = jnp.pad(cdw, ((0, 32 - CONV_K), (0, 0)))
    fdw = small_all[:, 48:48 + FFN_K, :].transpose(1, 0, 2).reshape(FFN_K, 2 * FFN)

    tabs = _rope_tables(p)
    vecs = jnp.concatenate([conv_ln_g, conv_ln_b, b_conv_proj, norm_post_mix, norm_pre_ffn, jnp.zeros((3, D), F32)], axis=0)

    (h0p, n1, q, kv, ag, gates), (wa, wc, wo) = _in_proj(
        x[0], meta_full, norm_pre_mix, w_int, b_in, tabs,
        comm=_Gather([w_attn_proj[0].astype(BF16), w_conv_proj[0].astype(BF16), w_out[0].astype(BF16)]))
    w_up_shard = w_up[0].T.astype(BF16)
    (ao,), (w_upt_half,) = _attn_fwd(q, kv, attn_sinks, comm=_Gather([w_up_shard], 0, 2))
    (c0,), (w_upt,) = _conv31_fwd(ag, cdw32, conv_dw_b, comm=_Gather([w_up_shard], 1, 2, into=[w_upt_half]))
    (c1, attn, conv, merged, mix, h1, n2), _ = _mixer_fwd(ao, c0, gates, h0p, wa, wc, wo, vecs)
    (u0,), (wd,) = _mm_nt(n2, w_upt, "ffn_up", comm=_Gather([w_down[0].astype(BF16)]))
    act, dact_dv, dact_dg = _ffn_act(u0, fdw, ffn_dw_b)
    dffn, dact, dy, acc_f = _ffn_down_loss(act, wd, h1, loss_target[0], norm_post_ffn)

    (dug, duv, gfw_g, gfw_v, gfb_g, gfb_v, g_wd), _ = _ffn_act_bwd(u0, dact, dact_dg, dact_dv, fdw, act, dffn)
    g_wupt, (s_wd0,) = _mm_tn([dug, duv], n2, "grad_w_up", comm=_Scatter([g_wd], 0, 2))
    (dh1, acc_u), (s_wd1,) = _ffn_in_bwd(dug, duv, w_upt, h1, dy, norm_pre_ffn, comm=_Scatter([g_wd], 1, 2))
    (dmix, dat, dcv, dgt, dao, dc0, acc_m), (s_wup0,) = _mixer_bwd(
        dh1, mix, attn, conv, gates, c0, wa, wc, wo, vecs, comm=_Scatter([g_wupt], 0, 4))
    (da, dg, g_cdw, g_cdb, g_wo, g_wa, g_wc), (s_wup1, s_wup2, s_wup3) = _conv31_bwd(
        ag, dc0, cdw32, [(merged, dmix), (ao, dat), (c1, dcv)],
        comm=_Both(_Both(_Scatter([g_wupt], 1, 4), _Scatter([g_wupt], 2, 4)), _Scatter([g_wupt], 3, 4)))
    (dqkv, dsink), (s_wa, s_wc, s_wo) = _attn_bwd(q, kv, dao, attn_sinks, tabs, comm=_Scatter([g_wa, g_wc, g_wo]))
    loss_row = jnp.sum(acc_f[1:2, :], axis=1, keepdims=True)
    early = [loss_row, acc_m[0:1], dsink[0:1, 0:16], g_cdw[0:CONV_K], g_cdb,
             acc_m[2:3], acc_m[3:4], acc_m[1:2], acc_u[0:1], acc_f[0:1],
             jnp.concatenate([gfw_g, gfw_v], axis=1), jnp.concatenate([gfb_g, gfb_v], axis=1)]
    (g_wint, g_bin), (gathered_early,) = _mm_tn([dqkv, da, dg, dgt], n1, "grad_w_in", col_sums=True,
                                                comm=_Gather([_flat_pack(early, 64)]))
    (from_sibling,) = _exchange(_SiblingSwap(g_wint), "swap_w_in")
    (grad_x2d, dmeta, acc_i), (s_win,) = _in_bwd(dqkv, da, dg, dgt, w_int, h0p, dh1, norm_pre_mix,
                                                 comm=_ChipScatter(_pair_add(g_wint, from_sibling)))

    big = []
    for nm, parts, nslots, w, m, v, tr in (
            ("w_in", [s_win], N_CHIP, w_in, m_w_in, v_w_in, True), ("w_up", [s_wup0, s_wup1, s_wup2, s_wup3], N_DEV, w_up, m_w_up, v_w_up, True),
            ("w_attn_proj", [s_wa], N_DEV, w_attn_proj, m_w_attn_proj, v_w_attn_proj, False),
            ("w_conv_proj", [s_wc], N_DEV, w_conv_proj, m_w_conv_proj, v_w_conv_proj, False),
            ("w_out", [s_wo], N_DEV, w_out, m_w_out, v_w_out, False),
            ("w_down", [s_wd0, s_wd1], N_DEV, w_down, m_w_down, v_w_down, False)):
        ins = [a[0].T if tr else a[0] for a in (w, m, v)]
        big.append(tuple((o.T if tr else o)[None] for o in _sum_adamw(parts, *ins, "update_" + nm, nslots)))

    late = [dmeta, acc_i[0:1], g_bin]
    (gathered_late,) = _exchange(_Gather([_flat_pack(late, 24)]), "gather_small_grads")
    g_meta, g_npm, g_bi = _flat_unpack(_sum_slots(gathered_late, "sum_late_grads"), [a.shape for a in late])
    tot = _flat_unpack(_sum_slots(gathered_early, "sum_small_grads"), [a.shape for a in early])
    (loss, g_nqm, g_sk, g_cw, g_cb, g_lg, g_lb, g_bc, g_npf, g_nqf, g_fw, g_fb) = tot
    loss = loss.reshape(())
    g_meta = lax.dynamic_slice_in_dim(g_meta, me * BLK, BLK, axis=1)
    g_cw = lax.dynamic_slice_in_dim(g_cw, me * BLK, BLK, axis=1)[None]
    g_fw = lax.dynamic_slice_in_dim(g_fw, me * up_cols, up_cols, axis=1)[None]

    sm_w = [meta_tokens, norm_pre_mix, norm_post_mix, b_in, attn_sinks, conv_dw_w, conv_dw_b, conv_ln_g, conv_ln_b,
            b_conv_proj, norm_pre_ffn, norm_post_ffn, ffn_dw_w, ffn_dw_b]
    sm_g = [g_meta, g_npm, g_nqm, g_bi, g_sk, g_cw, g_cb, g_lg, g_lb, g_bc, g_npf, g_nqf, g_fw, g_fb]
    sm_m = [m_meta_tokens, m_norm_pre_mix, m_norm_post_mix, m_b_in, m_attn_sinks, m_conv_dw_w, m_conv_dw_b, m_conv_ln_g,
            m_conv_ln_b, m_b_conv_proj, m_norm_pre_ffn, m_norm_post_ffn, m_ffn_dw_w, m_ffn_dw_b]
    sm_v = [v_meta_tokens, v_norm_pre_mix, v_norm_post_mix, v_b_in, v_attn_sinks, v_conv_dw_w, v_conv_dw_b, v_conv_ln_g,
            v_conv_ln_b, v_b_conv_proj, v_norm_pre_ffn, v_norm_post_ffn, v_ffn_dw_w, v_ffn_dw_b]
    swap = lambda a: jnp.transpose(a, (1, 0, 2)) if a.ndim == 3 else a
    sm_d, sm_nm, sm_nv = ([swap(o) for o in outs] for outs in
                          _adamw_many(*([swap(a) for a in group] for group in (sm_w, sm_g, sm_m, sm_v)), "adamw_small"))

    order = ["meta_tokens", "norm_pre_mix", "norm_post_mix", "w_in", "b_in", "attn_sinks", "w_attn_proj", "conv_dw_w",
             "conv_dw_b", "conv_ln_g", "conv_ln_b", "w_conv_proj", "b_conv_proj", "w_out", "norm_pre_ffn", "norm_post_ffn",
             "w_up", "ffn_dw_w", "ffn_dw_b", "w_down"]
    small_names = ["meta_tokens", "norm_pre_mix", "norm_post_mix", "b_in", "attn_sinks", "conv_dw_w", "conv_dw_b", "conv_ln_g",
                   "conv_ln_b", "b_conv_proj", "norm_pre_ffn", "norm_post_ffn", "ffn_dw_w", "ffn_dw_b"]
    big_names = ["w_in", "w_up", "w_attn_proj", "w_conv_proj", "w_out", "w_down"]
    table = {}
    for k, nm in enumerate(small_names):
        table[nm] = (sm_g[k], sm_d[k], sm_nm[k], sm_nv[k])
    for k, nm in enumerate(big_names):
        table[nm] = big[k]
    grad_x = grad_x2d[None]
    outs = [loss, grad_x]
    for field in range(4):
        outs += [table[nm][field] for nm in order]
    return tuple(outs)
```

```python
import functools

import jax
import jax.numpy as jnp
from jax import lax
from jax.experimental import pallas as pl
from jax.experimental.pallas import tpu as pltpu

F32 = jnp.float32
BF16 = jnp.bfloat16
MESH = pl.DeviceIdType.MESH

D = 1024
HEAD_DIM = 64
N_META = 16
BLK = 128
PAD = BLK - N_META
CONV_K = 31
FFN = 2816
FFN_K = 3
QKV_W = 1280
IN_W = 5376
ROT_DIM = 16
ROPE_THETA = 500000.0
RMS_EPS = 1e-6
LN_EPS = 1e-5
NEG_INF = -1e30
SCALE = HEAD_DIM ** -0.5
N_DEV = 8

ADAM_LR = 0.001
ADAM_B1 = 0.9
ADAM_B2 = 0.999
ADAM_EPS = 1e-08
ADAM_WD = 0.01
ADAM_STEP = 10

VMEM_BYTES_V7X = 64 * 1024 * 1024
VMEM_LIMIT = VMEM_BYTES_V7X - 8 * 1024 * 1024

NT = (((1,), (1,)), ((), ()))
TN = (((0,), (0,)), ((), ()))
VM = pl.BlockSpec(memory_space=pltpu.VMEM)
ANY = pl.BlockSpec(memory_space=pl.ANY)


def _cparams(*sem):
    return pltpu.CompilerParams(dimension_semantics=sem or None, vmem_limit_bytes=VMEM_LIMIT)


def _row_tile(p):
    return 384 if p % 384 == 0 else 128


def _dot(a, b):
    return jnp.dot(a, b, preferred_element_type=F32)


def _dot_nt(a, b):
    return lax.dot_general(a, b, NT, preferred_element_type=F32)


def _dot_tn(a, b):
    return lax.dot_general(a, b, TN, preferred_element_type=F32)


def _rms(x, g):
    return x * lax.rsqrt(jnp.mean(x * x, axis=-1, keepdims=True) + RMS_EPS) * g


def _lnsilu(x, g, b):
    mu = jnp.mean(x, axis=-1, keepdims=True)
    var = jnp.mean(jnp.square(x - mu), axis=-1, keepdims=True)
    z = (x - mu) * lax.rsqrt(var + LN_EPS) * g + b
    return z * jax.nn.sigmoid(z)


def _rms_bwd(x, g, dy):
    r = lax.rsqrt(jnp.mean(x * x, axis=-1, keepdims=True) + RMS_EPS)
    xn = x * r
    u = dy * g
    dg = jnp.sum(dy * xn, axis=0, keepdims=True)
    dx = r * (u - xn * jnp.mean(u * xn, axis=-1, keepdims=True))
    return dx, dg


def _lnsilu_bwd(x, g, b, dout):
    mu = jnp.mean(x, axis=-1, keepdims=True)
    xc = x - mu
    rs = lax.rsqrt(jnp.mean(xc * xc, axis=-1, keepdims=True) + LN_EPS)
    yh = xc * rs
    z = yh * g + b
    sg = jax.nn.sigmoid(z)
    dz = dout * (sg * (1.0 + z * (1.0 - sg)))
    dg = jnp.sum(dz * yh, axis=0, keepdims=True)
    db = jnp.sum(dz, axis=0, keepdims=True)
    dyh = dz * g
    dx = rs * (dyh - jnp.mean(dyh, axis=-1, keepdims=True) - yh * jnp.mean(dyh * yh, axis=-1, keepdims=True))
    return dx, dg, db


def _rope(v, c, s1, s2):
    return v * c + pltpu.roll(v, BLK - 8, 1) * s1 + pltpu.roll(v, 8, 1) * s2


def _rows(i, tm):
    return i * tm + lax.broadcasted_iota(jnp.int32, (tm, 1), 0)


def _place():
    return lax.axis_index("x"), lax.axis_index("y"), lax.axis_index("c")


def _blk(ref, idx, r, dtype):
    return ref.at[pl.ds(pl.multiple_of(idx * r, 16 if dtype == BF16 else 8), r), :]


class _Gather:
    def __init__(self, arrs):
        self.ins = list(arrs)
        n = len(arrs)
        self.out_shape = [jax.ShapeDtypeStruct((N_DEV * a.shape[0], a.shape[1]), a.dtype) for a in arrs]
        self.scratch = [pltpu.SemaphoreType.DMA((n, 7)), pltpu.SemaphoreType.DMA((n, 7)), pltpu.SemaphoreType.DMA((n,))]

    def _parts(self, ins, outs, sems):
        send_sems, recv_sems, local_sems = sems
        n = len(ins)
        x, y, c = _place()
        me, sibling = (x, y, c), (x, y, 1 - c)
        chips = [(1 - x, y), (x, 1 - y), (1 - x, 1 - y)]

        def rows(a, p):
            return _blk(outs[a], 4 * p[0] + 2 * p[1] + p[2], self.ins[a].shape[0], self.ins[a].dtype)

        def copy(a, k, block, to, src=None):
            return pltpu.make_async_remote_copy(
                src_ref=rows(a, block) if src is None else src, dst_ref=rows(a, block),
                send_sem=send_sems.at[a, k], recv_sem=recv_sems.at[a, k], device_id=to, device_id_type=MESH)

        mine = [pltpu.make_async_copy(ins[a], rows(a, me), local_sems.at[a]) for a in range(n)]
        first = []
        for a in range(n):
            first.append(copy(a, 0, me, sibling, src=ins[a]))
            first += [copy(a, 1 + j, me, (*chip, c), src=ins[a]) for j, chip in enumerate(chips)]
        return n, c, me, sibling, chips, copy, mine, first

    def start(self, ins, outs, sems):
        *_, mine, first = self._parts(ins, outs, sems)
        for cp in mine + first:
            cp.start()

    def finish(self, ins, outs, sems):
        n, c, me, sibling, chips, copy, mine, first = self._parts(ins, outs, sems)
        passed = []
        for j, chip in enumerate(chips):
            for a in range(n):
                copy(a, 1 + j, (*chip, c), me).wait_recv()
                fwd = copy(a, 4 + j, (*chip, c), sibling)
                fwd.start()
                passed.append(fwd)
        for a in range(n):
            copy(a, 0, sibling, me).wait_recv()
            for j, chip in enumerate(chips):
                copy(a, 4 + j, (*chip, 1 - c), me).wait_recv()
        for cp in first + passed:
            cp.wait_send()
        for cp in mine:
            cp.wait()


class _GatherRelay:
    N_COPY = 13

    def __init__(self, arr):
        self.ins = [arr]
        self.r = arr.shape[0]
        self.out_shape = [jax.ShapeDtypeStruct((N_DEV * self.r, arr.shape[1]), arr.dtype)]
        self.scratch = [pltpu.SemaphoreType.DMA((self.N_COPY,)), pltpu.SemaphoreType.DMA((self.N_COPY,)),
                        pltpu.SemaphoreType.DMA]

    def _parts(self, ins, outs, sems):
        send_sems, recv_sems, local_sem = sems
        x, y, c = _place()
        r, half = self.r, self.r // 2
        out = outs[0]
        me, sib, xn, yn, dg = (x, y, c), (x, y, 1 - c), (1 - x, y, c), (x, 1 - y, c), (1 - x, 1 - y, c)
        sx, sy, sd = (1 - x, y, 1 - c), (x, 1 - y, 1 - c), (1 - x, 1 - y, 1 - c)
        lo, hi = (0, half), (half, half)

        def rows(p, part=(0, r)):
            return out.at[pl.ds(pl.multiple_of((4 * p[0] + 2 * p[1] + p[2]) * r + part[0], 16), part[1]), :]

        def own(part):
            return ins[0].at[pl.ds(part[0], part[1]), :]

        def copy(k, dev_rows, to, src=None):
            return pltpu.make_async_remote_copy(
                src_ref=dev_rows if src is None else src, dst_ref=dev_rows,
                send_sem=send_sems.at[k], recv_sem=recv_sems.at[k], device_id=to, device_id_type=MESH)

        mine = pltpu.make_async_copy(ins[0], rows(me), local_sem)
        first = [copy(0, rows(me), sib, src=ins[0]),
                 copy(1, rows(me, lo), xn, src=own(lo)), copy(3, rows(me, hi), yn, src=own(hi)),
                 copy(2, rows(me, hi), xn, src=own(hi)), copy(4, rows(me, lo), yn, src=own(lo))]
        arrive = {0: rows(sib), 1: rows(xn, lo), 2: rows(xn, hi), 3: rows(yn, hi), 4: rows(yn, lo),
                  5: rows(dg, lo), 6: rows(dg, hi), 7: rows(sx, lo), 8: rows(sx, hi), 9: rows(sy, hi),
                  10: rows(sy, lo), 11: rows(sd, lo), 12: rows(sd, hi)}
        relay = {1: [(5, rows(xn, lo), yn), (7, rows(xn, lo), sib)], 3: [(6, rows(yn, hi), xn), (9, rows(yn, hi), sib)],
                 2: [(8, rows(xn, hi), sib)], 4: [(10, rows(yn, lo), sib)],
                 5: [(11, rows(dg, lo), sib)], 6: [(12, rows(dg, hi), sib)]}
        return copy, mine, first, arrive, relay, me

    def start(self, ins, outs, sems):
        _, mine, first, _, _, _ = self._parts(ins, outs, sems)
        for cp in [mine] + first:
            cp.start()

    def finish(self, ins, outs, sems):
        copy, mine, first, arrive, relay, me = self._parts(ins, outs, sems)
        passed = []
        for k in (1, 3, 2, 4, 5, 6):
            copy(k, arrive[k], me).wait_recv()
            for k2, dev_rows, to in relay[k]:
                fwd = copy(k2, dev_rows, to)
                fwd.start()
                passed.append(fwd)
        for k in (0, 7, 8, 9, 10, 11, 12):
            copy(k, arrive[k], me).wait_recv()
        for cp in first + passed:
            cp.wait_send()
        mine.wait()


FLIPS = [(0, 0, 1), (1, 0, 0), (0, 1, 0), (1, 1, 0), (1, 0, 1), (0, 1, 1), (1, 1, 1)]


class _Scatter:
    def __init__(self, arrs, part=0, nparts=1):
        self.ins = list(arrs)
        self.part, self.nparts = part, nparts
        n = len(arrs)
        self.out_shape = [jax.ShapeDtypeStruct((a.shape[0] // nparts, a.shape[1]), a.dtype) for a in arrs]
        self.scratch = [pltpu.SemaphoreType.DMA((n, 7)), pltpu.SemaphoreType.DMA((n, 7)), pltpu.SemaphoreType.DMA((n,))]

    def _parts(self, ins, outs, sems):
        send_sems, recv_sems, local_sems = sems
        n = len(ins)
        x, y, c = _place()
        me = 4 * x + 2 * y + c

        def flip(v, f):
            return 1 - v if f else v

        def src(a, idx):
            r = self.ins[a].shape[0] // N_DEV
            rs = r // self.nparts
            return ins[a].at[pl.ds(pl.multiple_of(idx * r + self.part * rs, 16), rs), :]

        def dst(a, idx):
            rs = self.ins[a].shape[0] // N_DEV // self.nparts
            return outs[a].at[pl.ds(pl.multiple_of(idx * rs, 16), rs), :]

        mine = [pltpu.make_async_copy(src(a, me), dst(a, me), local_sems.at[a]) for a in range(n)]
        sends, recvs = [], []
        for k, f in enumerate(FLIPS):
            peer = (flip(x, f[0]), flip(y, f[1]), flip(c, f[2]))
            pidx = 4 * peer[0] + 2 * peer[1] + peer[2]
            for a in range(n):
                sends.append(pltpu.make_async_remote_copy(
                    src_ref=src(a, pidx), dst_ref=dst(a, me),
                    send_sem=send_sems.at[a, k], recv_sem=recv_sems.at[a, k], device_id=peer, device_id_type=MESH))
                recvs.append(functools.partial(
                    pltpu.make_async_remote_copy,
                    src_ref=src(a, pidx), dst_ref=dst(a, pidx),
                    send_sem=send_sems.at[a, k], recv_sem=recv_sems.at[a, k], device_id=peer, device_id_type=MESH))
        return mine, sends, recvs

    def start(self, ins, outs, sems):
        mine, sends, _ = self._parts(ins, outs, sems)
        for cp in mine + sends:
            cp.start()

    def finish(self, ins, outs, sems):
        mine, sends, recvs = self._parts(ins, outs, sems)
        for make in recvs:
            make().wait_recv()
        for cp in sends:
            cp.wait_send()
        for cp in mine:
            cp.wait()


N_CHIP = 4


class _SiblingSwap:
    def __init__(self, arr):
        self.ins = [arr]
        self.r = arr.shape[0] // N_DEV
        self.out_shape = [jax.ShapeDtypeStruct((N_CHIP * self.r, arr.shape[1]), arr.dtype)]
        self.scratch = [pltpu.SemaphoreType.DMA((N_CHIP,)), pltpu.SemaphoreType.DMA((N_CHIP,))]

    def _copies(self, ins, outs, sems):
        send_sems, recv_sems = sems
        x, y, c = _place()
        r = self.r
        return [pltpu.make_async_remote_copy(
            src_ref=ins[0].at[pl.ds(pl.multiple_of((2 * j + 1 - c) * r, 16), r), :],
            dst_ref=outs[0].at[pl.ds(j * r, r), :],
            send_sem=send_sems.at[j], recv_sem=recv_sems.at[j], device_id=(x, y, 1 - c), device_id_type=MESH)
            for j in range(N_CHIP)]

    def start(self, ins, outs, sems):
        for cp in self._copies(ins, outs, sems):
            cp.start()

    def finish(self, ins, outs, sems):
        for cp in self._copies(ins, outs, sems):
            cp.wait()


class _ChipScatter:
    def __init__(self, arr):
        self.ins = [arr]
        self.r = arr.shape[0] // N_CHIP
        self.out_shape = [jax.ShapeDtypeStruct(arr.shape, arr.dtype)]
        self.scratch = [pltpu.SemaphoreType.DMA((3,)), pltpu.SemaphoreType.DMA((3,)), pltpu.SemaphoreType.DMA]

    def _parts(self, ins, outs, sems):
        send_sems, recv_sems, local_sem = sems
        x, y, c = _place()
        r = self.r
        my_chip = 2 * x + y

        def rows(ref, j):
            return ref.at[pl.ds(pl.multiple_of(j * r, 16), r), :]

        mine = pltpu.make_async_copy(rows(ins[0], my_chip), rows(outs[0], my_chip), local_sem)
        sends, recvs = [], []
        for k, (fx, fy) in enumerate(((1, 0), (0, 1), (1, 1))):
            px, py = (1 - x if fx else x), (1 - y if fy else y)
            peer_chip = 2 * px + py
            sends.append(pltpu.make_async_remote_copy(
                src_ref=rows(ins[0], peer_chip), dst_ref=rows(outs[0], my_chip),
                send_sem=send_sems.at[k], recv_sem=recv_sems.at[k], device_id=(px, py, c), device_id_type=MESH))
            recvs.append(functools.partial(
                pltpu.make_async_remote_copy,
                src_ref=rows(ins[0], peer_chip), dst_ref=rows(outs[0], peer_chip),
                send_sem=send_sems.at[k], recv_sem=recv_sems.at[k], device_id=(px, py, c), device_id_type=MESH))
        return mine, sends, recvs

    def start(self, ins, outs, sems):
        mine, sends, _ = self._parts(ins, outs, sems)
        for cp in [mine] + sends:
            cp.start()

    def finish(self, ins, outs, sems):
        mine, sends, recvs = self._parts(ins, outs, sems)
        for make in recvs:
            make().wait_recv()
        for cp in sends:
            cp.wait_send()
        mine.wait()


def _pair_add(partial, recv):
    r = recv.shape[0] // N_CHIP
    cols = recv.shape[1]
    tr = r // 2 if (r // 2) % 16 == 0 else r
    steps = r // tr
    core = lax.axis_index("c").astype(jnp.int32).reshape(1)

    def body(c_ref, p_ref, s_ref, o_ref):
        o_ref[...] = (p_ref[...].astype(F32) + s_ref[...].astype(F32)).astype(BF16)

    spec = pl.BlockSpec((tr, cols), lambda j, i, c_ref: (j * steps + i, 0))
    return pl.pallas_call(
        body, name="pair_add",
        grid_spec=pltpu.PrefetchScalarGridSpec(
            num_scalar_prefetch=1, grid=(N_CHIP, steps),
            in_specs=[pl.BlockSpec((tr, cols), lambda j, i, c_ref: ((2 * j + c_ref[0]) * steps + i, 0)), spec],
            out_specs=spec),
        out_shape=jax.ShapeDtypeStruct(recv.shape, BF16),
        compiler_params=_cparams("parallel", "parallel"),
    )(core, partial, recv)


class _Both:
    def __init__(self, a, b):
        self.a, self.b = a, b
        self.ins = a.ins + b.ins
        self.out_shape = a.out_shape + b.out_shape
        self.scratch = a.scratch + b.scratch

    def _split(self, ins, outs, sems):
        ni, no, ns = len(self.a.ins), len(self.a.out_shape), len(self.a.scratch)
        return (ins[:ni], outs[:no], sems[:ns]), (ins[ni:], outs[no:], sems[ns:])

    def start(self, ins, outs, sems):
        ra, rb = self._split(ins, outs, sems)
        self.a.start(*ra)
        self.b.start(*rb)

    def finish(self, ins, outs, sems):
        ra, rb = self._split(ins, outs, sems)
        self.a.finish(*ra)
        self.b.finish(*rb)


def _exchange(comm, name):
    n, m = len(comm.ins), len(comm.out_shape)

    def body(*refs):
        ins, outs, sems = refs[:n], refs[n:n + m], refs[n + m:]
        comm.start(ins, outs, sems)
        comm.finish(ins, outs, sems)

    return pl.pallas_call(
        body, name=name, out_shape=comm.out_shape, in_specs=[ANY] * n, out_specs=[ANY] * m, scratch_shapes=comm.scratch,
    )(*comm.ins)


def _call(body, *, name, grid, in_specs, out_specs, out_shape, args, scratch=(), sem="parallel", comm=None):
    if comm is None:
        outs = pl.pallas_call(
            body, name=name, grid=grid, in_specs=list(in_specs), out_specs=list(out_specs), out_shape=list(out_shape),
            scratch_shapes=list(scratch), compiler_params=_cparams(sem))(*args)
        return outs, []
    n_in, n_out, n_sc = len(in_specs), len(out_specs), len(scratch)
    n_ci, n_co = len(comm.ins), len(comm.out_shape)
    last = grid[0] - 1

    def fused(*refs):
        ins, refs = refs[:n_in], refs[n_in:]
        c_ins, refs = refs[:n_ci], refs[n_ci:]
        outs, refs = refs[:n_out], refs[n_out:]
        c_outs, refs = refs[:n_co], refs[n_co:]
        sc, c_sems = refs[:n_sc], refs[n_sc:]
        step = pl.program_id(0)

        @pl.when(step == 0)
        def _():
            comm.start(c_ins, c_outs, c_sems)

        body(*ins, *outs, *sc)

        @pl.when(step == last)
        def _():
            comm.finish(c_ins, c_outs, c_sems)

    outs = pl.pallas_call(
        fused, name=name, grid=grid, in_specs=list(in_specs) + [ANY] * n_ci, out_specs=list(out_specs) + [ANY] * n_co,
        out_shape=list(out_shape) + comm.out_shape, scratch_shapes=list(scratch) + comm.scratch,
        compiler_params=_cparams("arbitrary"))(*args, *comm.ins)
    return outs[:n_out], outs[n_out:]


def _token_specs(tm):
    k = tm // BLK
    return [pl.BlockSpec((BLK, D), functools.partial(lambda i, t: (jnp.maximum(k * i + t - 1, 0), 0), t=t)) for t in range(k)]


def _in_proj(x2d, meta, gain, w_int, b_in, tabs, comm=None):
    p = x2d.shape[0] + BLK
    tm = _row_tile(p)
    k = tm // BLK

    def body(*refs):
        x_refs = refs[:k]
        m_ref, g_ref, w_ref, b_ref, t_ref, h_ref, n1_ref, q_ref, kv_ref, ag_ref, gt_ref = refs[k:]
        i = pl.program_id(0)
        head = jnp.concatenate([jnp.zeros((PAD, D), F32), m_ref[...]], axis=0)
        first = jnp.where(i == 0, head, x_refs[0][...])
        h = jnp.concatenate([first] + [r[...] for r in x_refs[1:]], axis=0) if k > 1 else first
        h_ref[...] = h
        n = _rms(h, g_ref[...]).astype(BF16)
        n1_ref[...] = n
        c, s1, s2 = t_ref[:, 0:128], t_ref[:, 128:256], t_ref[:, 256:384]

        def mm(c0, w):
            return _dot_nt(n, w_ref[c0:c0 + w, :]) + b_ref[:, c0:c0 + w]

        for j in range(4):
            acc = mm(256 * j, 256)
            for t in range(2):
                lo = 256 * j + 128 * t
                q_ref[:, lo:lo + 128] = (_rope(acc[:, 128 * t:128 * (t + 1)], c, s1, s2) * SCALE).astype(BF16)
        acc = mm(1024, 256)
        kv_ref[:, 0:128] = _rope(acc[:, 0:128], c, s1, s2).astype(BF16)
        kv_ref[:, 128:256] = acc[:, 128:256].astype(BF16)
        for j in range(8):
            ag_ref[:, 256 * j:256 * (j + 1)] = mm(QKV_W + 256 * j, 256).astype(BF16)
        for j in range(8):
            gt_ref[:, 256 * j:256 * (j + 1)] = mm(QKV_W + 2048 + 256 * j, 256).astype(BF16)

    def row(w):
        return pl.BlockSpec((tm, w), lambda i: (i, 0))

    return _call(
        body, name="in_proj", grid=(p // tm,),
        in_specs=_token_specs(tm) + [VM, VM, VM, VM, row(384)],
        out_specs=[row(D), row(D), row(D), row(256), row(2048), row(2048)],
        out_shape=[jax.ShapeDtypeStruct((p, D), F32)] + [jax.ShapeDtypeStruct((p, w), BF16) for w in (D, D, 256, 2048, 2048)],
        args=(x2d,) * k + (meta, gain, w_int, b_in, tabs), comm=comm)


N_KEY = 2 * BLK + N_META


def _attn_setup(n, h, q_ref, km_ref, kp_ref, kc_ref):
    lo = lax.broadcasted_iota(jnp.int32, (BLK, BLK), 1) < HEAD_DIM
    lok = lax.broadcasted_iota(jnp.int32, (N_KEY, BLK), 1) < HEAD_DIM

    def dup(lanes):
        cat = jnp.concatenate([kp_ref[:, lanes], kc_ref[:, lanes], km_ref[PAD:BLK, lanes]], axis=0).astype(F32)
        rolled = pltpu.roll(cat, HEAD_DIM, 1)
        return (jnp.where(lok, cat, rolled) if h == 0 else jnp.where(lok, rolled, cat)).astype(BF16)

    k2 = dup(slice(0, 128))
    v2 = dup(slice(128, 256))
    qs = _stack_heads(q_ref, h, lo)

    kr = lax.broadcasted_iota(jnp.int32, (BLK, BLK), 0)
    tq = BLK * n + lax.broadcasted_iota(jnp.int32, (BLK, BLK), 1) - PAD
    t_p = BLK * (n - 1) + kr - PAD
    t_c = BLK * n + kr - PAD
    ok_p = jnp.logical_and(t_p >= N_META, tq - t_p < BLK)
    ok_c = jnp.logical_and(t_c >= N_META, t_c <= tq)
    ok_m = lax.broadcasted_iota(jnp.int32, (N_META, BLK), 0) <= BLK * n + lax.broadcasted_iota(jnp.int32, (N_META, BLK), 1) - PAD
    bias = jnp.concatenate([jnp.where(ok, 0.0, NEG_INF).astype(F32) for ok in (ok_p, ok_c, ok_m)], axis=0)
    return qs, k2, v2, bias, lok


def _attn_head(s, bias, sink):
    s = s + bias
    m = jnp.maximum(jnp.max(s, axis=0, keepdims=True), sink)
    e = jnp.exp(s - m)
    es = jnp.exp(sink - m)
    inv = 1.0 / (jnp.sum(e, axis=0, keepdims=True) + es)
    return e * inv, es * inv


def _stack_heads(ref, h, lo):
    pieces = []
    for jp in range(4):
        v = ref[:, BLK * (4 * h + jp):BLK * (4 * h + jp + 1)]
        zero = jnp.zeros_like(v)
        pieces += [jnp.where(lo, v, zero), jnp.where(lo, zero, v)]
    return jnp.concatenate(pieces, axis=0)


def _unstack_heads(v, jp, lo):
    return jnp.where(lo, v[256 * jp:256 * jp + 128], v[256 * jp + 128:256 * jp + 256])


def _attn_fwd(q, kv, sinks, comm=None):
    p = q.shape[0]
    nb = p // BLK

    def body(q_ref, km_ref, kp_ref, kc_ref, sink_ref, o_ref):
        n = pl.program_id(0)
        lo = lax.broadcasted_iota(jnp.int32, (BLK, BLK), 1) < HEAD_DIM
        for h in range(2):
            qs, k2, v2, bias, _ = _attn_setup(n, h, q_ref, km_ref, kp_ref, kc_ref)
            st = _dot_nt(k2, qs)
            pt = jnp.concatenate(
                [_attn_head(st[:, BLK * g:BLK * (g + 1)], bias, sink_ref[0, 8 * h + g])[0].astype(BF16) for g in range(8)],
                axis=1)
            o = _dot_tn(pt, v2)
            for jp in range(4):
                o_ref[:, BLK * (4 * h + jp):BLK * (4 * h + jp + 1)] = _unstack_heads(o, jp, lo).astype(BF16)

    return _call(
        body, name="attn_fwd", grid=(nb,),
        in_specs=[pl.BlockSpec((BLK, D), lambda i: (i, 0)),
                  pl.BlockSpec((BLK, 256), lambda i: (0, 0)),
                  pl.BlockSpec((BLK, 256), lambda i: (jnp.maximum(i - 1, 0), 0)),
                  pl.BlockSpec((BLK, 256), lambda i: (i, 0)),
                  pl.BlockSpec(memory_space=pltpu.SMEM)],
        out_specs=[pl.BlockSpec((BLK, D), lambda i: (i, 0))],
        out_shape=[jax.ShapeDtypeStruct((p, D), BF16)],
        args=(q, kv, kv, kv, sinks), comm=comm)


def _conv31_fwd(ag, w32, b, comm=None):
    p = ag.shape[0]
    nch = p // BLK

    def body(a_ref, g_ref, w_ref, b_ref, o_ref, gp):
        gp[0:32, :] = jnp.zeros((32, BLK), F32)
        for ci in range(nch):
            r0 = BLK * ci
            glu = a_ref[r0:r0 + BLK, :].astype(F32) * jax.nn.sigmoid(g_ref[r0:r0 + BLK, :].astype(F32))
            if ci == 0:
                glu = jnp.where(_rows(0, BLK) >= PAD, glu, 0.0)
            gp[32 + r0:32 + r0 + BLK, :] = glu
        for ci in range(nch):
            r0 = BLK * ci
            acc = jnp.broadcast_to(b_ref[...], (BLK, BLK))
            for j in range(CONV_K):
                acc = acc + w_ref[j:j + 1, :] * gp[r0 + j + 2:r0 + j + 2 + BLK, :]
            o_ref[r0:r0 + BLK, :] = acc

    return _call(
        body, name="conv31_fwd", grid=(D // BLK,),
        in_specs=[pl.BlockSpec((p, BLK), lambda j: (0, j)), pl.BlockSpec((p, BLK), lambda j: (0, 8 + j)),
                  pl.BlockSpec((32, BLK), lambda j: (0, j)), pl.BlockSpec((1, BLK), lambda j: (0, j))],
        out_specs=[pl.BlockSpec((p, BLK), lambda j: (0, j))],
        out_shape=[jax.ShapeDtypeStruct((p, D), F32)],
        scratch=[pltpu.VMEM((p + 32, BLK), F32)],
        args=(ag, ag, w32, b), comm=comm)


def _mixer_fwd(ao, c0, gates, h0p, wa, wc, wo, vecs):
    p = ao.shape[0]
    tm = _row_tile(p)

    def body(ao_ref, c0_ref, gt_ref, h_ref, wa_ref, wc_ref, wo_ref, v_ref,
             c1_ref, at_ref, cv_ref, mg_ref, mix_ref, h1_ref, n2_ref):
        i = pl.program_id(0)
        c1 = _lnsilu(c0_ref[...], v_ref[0:1, :], v_ref[1:2, :]).astype(BF16)
        c1_ref[...] = c1
        attn = _dot(ao_ref[...], wa_ref[...])
        conv = _dot(c1, wc_ref[...]) + v_ref[2:3, :]
        at_ref[...] = attn.astype(BF16)
        cv_ref[...] = conv.astype(BF16)
        merged = (jax.nn.sigmoid(gt_ref[:, 0:D].astype(F32)) * attn
                  + jax.nn.sigmoid(gt_ref[:, D:2 * D].astype(F32)) * conv).astype(BF16)
        mg_ref[...] = merged
        mix = _dot(merged, wo_ref[...])
        mix_ref[...] = mix
        h1 = jnp.where(_rows(i, tm) >= PAD, h_ref[...] + _rms(mix, v_ref[3:4, :]), 0.0)
        h1_ref[...] = h1
        n2_ref[...] = _rms(h1, v_ref[4:5, :]).astype(BF16)

    def row(w):
        return pl.BlockSpec((tm, w), lambda i: (i, 0))

    return pl.pallas_call(
        body, name="mixer_fwd", grid=(p // tm,),
        in_specs=[row(D), row(D), row(2 * D), row(D), VM, VM, VM, VM],
        out_specs=[row(D)] * 7,
        out_shape=[jax.ShapeDtypeStruct((p, D), t) for t in (BF16, BF16, BF16, BF16, F32, F32, BF16)],
        compiler_params=_cparams("parallel"),
    )(ao, c0, gates, h0p, wa, wc, wo, vecs)


def _mm_nt(a, w_t, name):
    p, k = a.shape
    n = w_t.shape[0]
    tm = _row_tile(p)
    ch = 512

    def body(a_ref, w_ref, o_ref):
        a_v = a_ref[...]
        for c0 in range(0, n, ch):
            o_ref[:, c0:c0 + ch] = _dot_nt(a_v, w_ref[c0:c0 + ch, :]).astype(BF16)

    return pl.pallas_call(
        body, name=name, grid=(p // tm,),
        in_specs=[pl.BlockSpec((tm, k), lambda i: (i, 0)), VM],
        out_specs=pl.BlockSpec((tm, n), lambda i: (i, 0)),
        out_shape=jax.ShapeDtypeStruct((p, n), BF16),
        compiler_params=_cparams("parallel"),
    )(a, w_t)


def _conv3(xp_ref, w_ref, r0):
    return (w_ref[0:1, :] * xp_ref[r0 + 6:r0 + 6 + BLK, :] + w_ref[1:2, :] * xp_ref[r0 + 7:r0 + 7 + BLK, :]
            + w_ref[2:3, :] * xp_ref[r0 + 8:r0 + 8 + BLK, :])


def _ffn_slab_specs(p):
    ncol = FFN // BLK
    return [pl.BlockSpec((p, BLK), lambda j: (0, j)), pl.BlockSpec((p, BLK), lambda j: (0, ncol + j)),
            pl.BlockSpec((FFN_K, BLK), lambda j: (0, j)), pl.BlockSpec((FFN_K, BLK), lambda j: (0, ncol + j)),
            pl.BlockSpec((1, BLK), lambda j: (0, j)), pl.BlockSpec((1, BLK), lambda j: (0, ncol + j))]


def _fill_shifted(dst, src_ref, nch):
    dst[0:8, :] = jnp.zeros((8, BLK), F32)
    for ci in range(nch):
        dst[8 + BLK * ci:8 + BLK * (ci + 1), :] = src_ref[BLK * ci:BLK * (ci + 1), :].astype(F32)


def _ffn_act(u0, fw, fb):
    p = u0.shape[0]
    nch = p // BLK

    def body(g_ref, v_ref, wg_ref, wv_ref, bg_ref, bv_ref, o_ref, dv_ref, dg_ref, xg, xv):
        _fill_shifted(xg, g_ref, nch)
        _fill_shifted(xv, v_ref, nch)
        for ci in range(nch):
            r0 = BLK * ci
            ug = _conv3(xg, wg_ref, r0) + bg_ref[...]
            uv = _conv3(xv, wv_ref, r0) + bv_ref[...]
            sg = jax.nn.sigmoid(ug)
            silu = ug * sg
            o_ref[r0:r0 + BLK, :] = (silu * uv).astype(BF16)
            dv_ref[r0:r0 + BLK, :] = silu.astype(BF16)
            dg_ref[r0:r0 + BLK, :] = (uv * (sg * (1.0 + ug * (1.0 - sg)))).astype(BF16)

    slab = pl.BlockSpec((p, BLK), lambda j: (0, j))
    return pl.pallas_call(
        body, name="ffn_act", grid=(FFN // BLK,),
        in_specs=_ffn_slab_specs(p),
        out_specs=[slab] * 3,
        out_shape=[jax.ShapeDtypeStruct((p, FFN), BF16)] * 3,
        scratch_shapes=[pltpu.VMEM((p + 8, BLK), F32)] * 2,
        compiler_params=_cparams("parallel"),
    )(u0, u0, fw, fw, fb, fb)


def _ffn_down_loss(act, wd, h1, tgt, gain):
    p = act.shape[0]
    tm = _row_tile(p)
    k = tm // BLK

    def body(*refs):
        a_ref, w_ref, h_ref = refs[:3]
        t_refs = refs[3:3 + k]
        g_ref, df_ref, da_ref, dy_ref, acc_ref = refs[3 + k:]
        i = pl.program_id(0)

        @pl.when(i == 0)
        def _():
            acc_ref[...] = jnp.zeros_like(acc_ref)

        ffn = _dot(a_ref[...], w_ref[...])
        t = jnp.concatenate([t_ref[...] for t_ref in t_refs], axis=0) if k > 1 else t_refs[0][...]
        diff = jnp.where(_rows(i, tm) >= BLK, h_ref[...] + _rms(ffn, g_ref[...]) - t, 0.0)
        dy = diff * (1.0 / D)
        dffn, dg = _rms_bwd(ffn, g_ref[...], dy)
        acc_ref[0:1, :] += dg
        acc_ref[1:2, :] += jnp.sum(diff * diff, axis=0, keepdims=True) * (0.5 / D)
        dy_ref[...] = dy
        dfb = dffn.astype(BF16)
        df_ref[...] = dfb
        for c0 in range(0, FFN, 256):
            da_ref[:, c0:c0 + 256] = _dot_nt(dfb, w_ref[c0:c0 + 256, :]).astype(BF16)

    def row(w):
        return pl.BlockSpec((tm, w), lambda i: (i, 0))

    return pl.pallas_call(
        body, name="ffn_down_loss", grid=(p // tm,),
        in_specs=[row(FFN), VM, row(D)] + _token_specs(tm) + [VM],
        out_specs=[row(D), row(FFN), row(D), pl.BlockSpec((8, D), lambda i: (0, 0))],
        out_shape=[jax.ShapeDtypeStruct((p, D), BF16), jax.ShapeDtypeStruct((p, FFN), BF16),
                   jax.ShapeDtypeStruct((p, D), F32), jax.ShapeDtypeStruct((8, D), F32)],
        compiler_params=_cparams("arbitrary"),
    )(act, wd, h1, *([tgt] * k), gain)


def _mm_tn(pieces, b, name, col_sums=False, comm=None):
    p, n = b.shape
    tk = 256
    nblk = [a.shape[1] // tk for a in pieces]
    offs = [sum(nblk[:q]) for q in range(len(pieces))]
    total = sum(nblk)
    npc = len(pieces)

    def body(*refs):
        a_refs, b_ref, o_ref = refs[:npc], refs[npc], refs[npc + 1]
        i = pl.program_id(0)
        for q, a_ref in enumerate(a_refs):
            @pl.when(jnp.logical_and(i >= offs[q], i < offs[q] + nblk[q]))
            def _(a_ref=a_ref):
                a_v = a_ref[...]
                o_ref[...] = _dot_tn(a_v, b_ref[...]).astype(BF16)
                if col_sums:
                    refs[npc + 2][...] = jnp.sum(a_v.astype(F32), axis=0, keepdims=True)

    def a_spec(q):
        return pl.BlockSpec((p, tk), lambda i: (0, jnp.clip(i - offs[q], 0, nblk[q] - 1)))

    out_specs = [pl.BlockSpec((tk, n), lambda i: (i, 0))]
    out_shape = [jax.ShapeDtypeStruct((total * tk, n), BF16)]
    if col_sums:
        out_specs.append(pl.BlockSpec((1, tk), lambda i: (0, i)))
        out_shape.append(jax.ShapeDtypeStruct((1, total * tk), F32))
    res, sent = _call(
        body, name=name, grid=(total,),
        in_specs=[a_spec(q) for q in range(npc)] + [VM],
        out_specs=out_specs, out_shape=out_shape, args=(*pieces, b), comm=comm)
    res = res if col_sums else res[0]
    return res if comm is None else (res, sent)


def _ffn_act_bwd(u0, dact, dact_dg, dact_dv, fw, act, dffn, comm=None):
    p = u0.shape[0]
    nch = p // BLK
    ncol = FFN // BLK

    def body(g_ref, v_ref, wg_ref, wv_ref, da_ref, lg_ref, lv_ref, act_ref, df_ref,
             dg_ref, dv_ref, gwg_ref, gwv_ref, gbg_ref, gbv_ref, gwd_ref, eg, ev):
        gwd_ref[...] = _dot_tn(act_ref[...], df_ref[...]).astype(BF16)
        eg[p:p + 8, :] = jnp.zeros((8, BLK), F32)
        ev[p:p + 8, :] = jnp.zeros((8, BLK), F32)
        for ci in range(nch):
            r0 = BLK * ci
            d = da_ref[r0:r0 + BLK, :].astype(F32)
            eg[r0:r0 + BLK, :] = d * lg_ref[r0:r0 + BLK, :].astype(F32)
            ev[r0:r0 + BLK, :] = d * lv_ref[r0:r0 + BLK, :].astype(F32)
        def fold(v):
            return jnp.sum(v.reshape(BLK // 8, 8, BLK), axis=0)

        for e_s, x_ref, w_ref, d_ref, gw_ref, gb_ref in ((eg, g_ref, wg_ref, dg_ref, gwg_ref, gbg_ref),
                                                        (ev, v_ref, wv_ref, dv_ref, gwv_ref, gbv_ref)):
            sums = [jnp.zeros((8, BLK), F32) for _ in range(FFN_K + 1)]
            for ci in range(nch):
                r0 = BLK * ci
                es = [e_s[r0 + t:r0 + t + BLK, :] for t in range(FFN_K)]
                du = w_ref[2:3, :] * es[0] + w_ref[1:2, :] * es[1] + w_ref[0:1, :] * es[2]
                if ci == 0:
                    du = jnp.where(_rows(0, BLK) >= PAD, du, 0.0)
                d_ref[r0:r0 + BLK, :] = du.astype(BF16)
                x = x_ref[r0:r0 + BLK, :].astype(F32)
                for j in range(FFN_K):
                    sums[j] = sums[j] + fold(es[FFN_K - 1 - j] * x)
                sums[FFN_K] = sums[FFN_K] + fold(es[0])
            for j in range(FFN_K):
                gw_ref[j:j + 1, :] = jnp.sum(sums[j], axis=0, keepdims=True)
            gb_ref[...] = jnp.sum(sums[FFN_K], axis=0, keepdims=True)

    slab = pl.BlockSpec((p, BLK), lambda j: (0, j))
    wspec = pl.BlockSpec((FFN_K, BLK), lambda j: (0, j))
    bspec = pl.BlockSpec((1, BLK), lambda j: (0, j))
    return _call(
        body, name="ffn_act_bwd", grid=(ncol,),
        in_specs=_ffn_slab_specs(p)[:4] + [slab] * 4 + [VM],
        out_specs=[slab, slab, wspec, wspec, bspec, bspec, pl.BlockSpec((BLK, D), lambda j: (j, 0))],
        out_shape=[jax.ShapeDtypeStruct((p, FFN), BF16)] * 2 + [jax.ShapeDtypeStruct((FFN_K, FFN), F32)] * 2
        + [jax.ShapeDtypeStruct((1, FFN), F32)] * 2 + [jax.ShapeDtypeStruct((FFN, D), BF16)],
        scratch=[pltpu.VMEM((p + 8, BLK), F32)] * 2,
        args=(u0, u0, fw, fw, dact, dact_dg, dact_dv, act, dffn), comm=comm)


def _ffn_in_bwd(dug, duv, w_upt, h1, dy, gain, comm=None):
    p = h1.shape[0]
    tm = _row_tile(p)

    def body(dg_ref, dv_ref, w_ref, h_ref, dy_ref, g_ref, o_ref, acc_ref):
        i = pl.program_id(0)

        @pl.when(i == 0)
        def _():
            acc_ref[...] = jnp.zeros_like(acc_ref)

        dn = _dot(dg_ref[...], w_ref[0:FFN, :]) + _dot(dv_ref[...], w_ref[FFN:2 * FFN, :])
        dh, dg = _rms_bwd(h_ref[...], g_ref[...], dn)
        o_ref[...] = dy_ref[...] + dh
        acc_ref[0:1, :] += dg

    def row(w):
        return pl.BlockSpec((tm, w), lambda i: (i, 0))

    return _call(
        body, name="ffn_in_bwd", grid=(p // tm,),
        in_specs=[row(FFN), row(FFN), VM, row(D), row(D), VM],
        out_specs=[row(D), pl.BlockSpec((8, D), lambda i: (0, 0))],
        out_shape=[jax.ShapeDtypeStruct((p, D), F32), jax.ShapeDtypeStruct((8, D), F32)],
        sem="arbitrary", args=(dug, duv, w_upt, h1, dy, gain), comm=comm)


def _mixer_bwd(dh1, mix, attn, conv, gates, c0, wa, wc, wo, vecs, comm=None):
    p = dh1.shape[0]
    tm = _row_tile(p)

    def body(dh_ref, mix_ref, at_ref, cv_ref, gt_ref, c0_ref, wa_ref, wc_ref, wo_ref, v_ref,
             dmix_ref, dat_ref, dcv_ref, dgt_ref, dao_ref, dc0_ref, acc_ref):
        i = pl.program_id(0)

        @pl.when(i == 0)
        def _():
            acc_ref[...] = jnp.zeros_like(acc_ref)

        dmix, dgp = _rms_bwd(mix_ref[...], v_ref[3:4, :], dh_ref[...])
        dmix = dmix.astype(BF16)
        dmix_ref[...] = dmix
        dmg = _dot_nt(dmix, wo_ref[...])
        sa = jax.nn.sigmoid(gt_ref[:, 0:D].astype(F32))
        sc = jax.nn.sigmoid(gt_ref[:, D:2 * D].astype(F32))
        dat = dmg * sa
        dcv = dmg * sc
        dgt_ref[:, 0:D] = (dmg * at_ref[...].astype(F32) * sa * (1.0 - sa)).astype(BF16)
        dgt_ref[:, D:2 * D] = (dmg * cv_ref[...].astype(F32) * sc * (1.0 - sc)).astype(BF16)
        datb = dat.astype(BF16)
        dcvb = dcv.astype(BF16)
        dat_ref[...] = datb
        dcv_ref[...] = dcvb
        dao_ref[...] = _dot_nt(datb, wa_ref[...]).astype(BF16)
        dc1 = _dot_nt(dcvb, wc_ref[...])
        dc0, dlg, dlb = _lnsilu_bwd(c0_ref[...], v_ref[0:1, :], v_ref[1:2, :], dc1)
        dc0_ref[...] = dc0
        acc_ref[0:1, :] += dgp
        acc_ref[1:2, :] += jnp.sum(dcv, axis=0, keepdims=True)
        acc_ref[2:3, :] += dlg
        acc_ref[3:4, :] += dlb

    def row(w):
        return pl.BlockSpec((tm, w), lambda i: (i, 0))

    return _call(
        body, name="mixer_bwd", grid=(p // tm,),
        in_specs=[row(D), row(D), row(D), row(D), row(2 * D), row(D), VM, VM, VM, VM],
        out_specs=[row(D), row(D), row(D), row(2 * D), row(D), row(D), pl.BlockSpec((8, D), lambda i: (0, 0))],
        out_shape=[jax.ShapeDtypeStruct((p, D), BF16)] * 3 + [jax.ShapeDtypeStruct((p, 2 * D), BF16),
                                                             jax.ShapeDtypeStruct((p, D), BF16),
                                                             jax.ShapeDtypeStruct((p, D), F32),
                                                             jax.ShapeDtypeStruct((8, D), F32)],
        sem="arbitrary", args=(dh1, mix, attn, conv, gates, c0, wa, wc, wo, vecs), comm=comm)


def _conv31_bwd(ag, dc0, w32, tn_pairs, comm=None):
    p = ag.shape[0]
    nch = p // BLK
    npair = len(tn_pairs)

    def body(*refs):
        a_ref, g_ref, dc_ref, w_ref = refs[:4]
        tn_a, tn_b = refs[4:4 + npair], refs[4 + npair:4 + 2 * npair]
        da_ref, dg_ref, gw_ref, gb_ref = refs[4 + 2 * npair:8 + 2 * npair]
        tn_o = refs[8 + 2 * npair:8 + 3 * npair]
        gp, dp = refs[8 + 3 * npair:]
        for ta, tb, to in zip(tn_a, tn_b, tn_o):
            to[...] = _dot_tn(ta[...], tb[...]).astype(BF16)
        gp[0:32, :] = jnp.zeros((32, BLK), F32)
        dp[p:p + 32, :] = jnp.zeros((32, BLK), F32)
        bsum = jnp.zeros((BLK, BLK), F32)
        for ci in range(nch):
            r0 = BLK * ci
            glu = a_ref[r0:r0 + BLK, :].astype(F32) * jax.nn.sigmoid(g_ref[r0:r0 + BLK, :].astype(F32))
            if ci == 0:
                glu = jnp.where(_rows(0, BLK) >= PAD, glu, 0.0)
            gp[32 + r0:32 + r0 + BLK, :] = glu
            d = dc_ref[r0:r0 + BLK, :]
            dp[r0:r0 + BLK, :] = d
            bsum = bsum + d
        gb_ref[...] = jnp.sum(bsum, axis=0, keepdims=True)
        for ci in range(nch):
            r0 = BLK * ci
            acc = jnp.zeros((BLK, BLK), F32)
            for j in range(CONV_K):
                acc = acc + w_ref[j:j + 1, :] * dp[r0 + 30 - j:r0 + 30 - j + BLK, :]
            if ci == 0:
                acc = jnp.where(_rows(0, BLK) >= PAD, acc, 0.0)
            a = a_ref[r0:r0 + BLK, :].astype(F32)
            sg = jax.nn.sigmoid(g_ref[r0:r0 + BLK, :].astype(F32))
            da_ref[r0:r0 + BLK, :] = (acc * sg).astype(BF16)
            dg_ref[r0:r0 + BLK, :] = (acc * a * sg * (1.0 - sg)).astype(BF16)
        sub = BLK // 2
        accs = [jnp.zeros((8, BLK), F32) for _ in range(CONV_K)]
        for r0 in range(0, p, sub):
            d = dp[r0:r0 + sub, :]
            for j in range(CONV_K):
                prod = d * gp[r0 + j + 2:r0 + j + 2 + sub, :]
                accs[j] = accs[j] + jnp.sum(prod.reshape(sub // 8, 8, BLK), axis=0)
        for j in range(CONV_K):
            gw_ref[j:j + 1, :] = jnp.sum(accs[j], axis=0, keepdims=True)
        gw_ref[CONV_K:32, :] = jnp.zeros((32 - CONV_K, BLK), F32)

    slab = pl.BlockSpec((p, BLK), lambda j: (0, j))
    return _call(
        body, name="conv31_bwd", grid=(D // BLK,),
        in_specs=[slab, pl.BlockSpec((p, BLK), lambda j: (0, 8 + j)), slab, pl.BlockSpec((32, BLK), lambda j: (0, j))]
        + [slab] * npair + [VM] * npair,
        out_specs=[slab, slab, pl.BlockSpec((32, BLK), lambda j: (0, j)), pl.BlockSpec((1, BLK), lambda j: (0, j))]
        + [pl.BlockSpec((BLK, D), lambda j: (j, 0))] * npair,
        out_shape=[jax.ShapeDtypeStruct((p, D), BF16)] * 2 + [jax.ShapeDtypeStruct((32, D), F32),
                                                             jax.ShapeDtypeStruct((1, D), F32)]
        + [jax.ShapeDtypeStruct((D, D), BF16)] * npair,
        scratch=[pltpu.VMEM((p + 32, BLK), F32)] * 2,
        args=(ag, ag, dc0, w32, *[a for a, _ in tn_pairs], *[b for _, b in tn_pairs]), comm=comm)


def _attn_bwd(q, kv, dao, sinks, tabs, comm=None):
    p = q.shape[0]
    nb = p // BLK

    def body(q_ref, km_ref, kp_ref, kc_ref, do_ref, sink_ref, t_ref, dqkv_ref, dsink_ref, carry, macc):
        i = pl.program_id(0)
        n = nb - 1 - i

        @pl.when(i == 0)
        def _():
            carry[...] = jnp.zeros_like(carry)
            macc[...] = jnp.zeros_like(macc)
            dsink_ref[...] = jnp.zeros_like(dsink_ref)

        lo = lax.broadcasted_iota(jnp.int32, (BLK, BLK), 1) < HEAD_DIM
        lane8 = lax.broadcasted_iota(jnp.int32, (8, BLK), 1)
        c, s1, s2 = t_ref[:, 0:128], -t_ref[:, 128:256], -t_ref[:, 256:384]
        dk = jnp.zeros((N_KEY, BLK), F32)
        dv = jnp.zeros((N_KEY, BLK), F32)
        for h in range(2):
            qs, k2, v2, bias, lok = _attn_setup(n, h, q_ref, km_ref, kp_ref, kc_ref)
            dos = _stack_heads(do_ref, h, lo)
            st = _dot_nt(k2, qs)
            dpt = _dot_nt(v2, dos)
            p_parts, ds_parts = [], []
            for g in range(8):
                cols = slice(BLK * g, BLK * (g + 1))
                pn, ps = _attn_head(st[:, cols], bias, sink_ref[0, 8 * h + g])
                dp = dpt[:, cols]
                delta = jnp.sum(pn * dp, axis=0, keepdims=True)
                ds_parts.append((pn * (dp - delta)).astype(BF16))
                p_parts.append(pn.astype(BF16))
                dsk = -jnp.sum(ps * delta, axis=1, keepdims=True)
                dsink_ref[...] += jnp.where(lane8 == 8 * h + g, dsk, 0.0)
            dst = jnp.concatenate(ds_parts, axis=1)
            pt = jnp.concatenate(p_parts, axis=1)
            dq = _dot_tn(dst, k2)
            for jp in range(4):
                lo_c = BLK * (4 * h + jp)
                dqkv_ref[:, lo_c:lo_c + BLK] = (_rope(_unstack_heads(dq, jp, lo), c, s1, s2) * SCALE).astype(BF16)
            dk2 = _dot(dst, qs)
            dv2 = _dot(pt, dos)
            dk2 = dk2 + pltpu.roll(dk2, HEAD_DIM, 1)
            dv2 = dv2 + pltpu.roll(dv2, HEAD_DIM, 1)
            own = lok if h == 0 else jnp.logical_not(lok)
            dk = jnp.where(own, dk2, dk)
            dv = jnp.where(own, dv2, dv)
        macc[:, 0:BLK] += dk[2 * BLK:N_KEY]
        macc[:, BLK:2 * BLK] += dv[2 * BLK:N_KEY]
        last = (n == 0).astype(F32)
        zpad = jnp.zeros((PAD, BLK), F32)
        dk_c = dk[BLK:2 * BLK] + carry[:, 0:BLK] + last * jnp.concatenate([zpad, macc[:, 0:BLK]], axis=0)
        dv_c = dv[BLK:2 * BLK] + carry[:, BLK:2 * BLK] + last * jnp.concatenate([zpad, macc[:, BLK:2 * BLK]], axis=0)
        carry[:, 0:BLK] = dk[0:BLK]
        carry[:, BLK:2 * BLK] = dv[0:BLK]
        dqkv_ref[:, D:D + BLK] = _rope(dk_c, c, s1, s2).astype(BF16)
        dqkv_ref[:, D + BLK:D + 2 * BLK] = dv_c.astype(BF16)

    def rev(w):
        return pl.BlockSpec((BLK, w), lambda i: (nb - 1 - i, 0))

    return _call(
        body, name="attn_bwd", grid=(nb,),
        in_specs=[rev(D),
                  pl.BlockSpec((BLK, 256), lambda i: (0, 0)),
                  pl.BlockSpec((BLK, 256), lambda i: (jnp.maximum(nb - 2 - i, 0), 0)),
                  rev(256), rev(D),
                  pl.BlockSpec(memory_space=pltpu.SMEM), rev(384)],
        out_specs=[rev(QKV_W), pl.BlockSpec((8, BLK), lambda i: (0, 0))],
        out_shape=[jax.ShapeDtypeStruct((p, QKV_W), BF16), jax.ShapeDtypeStruct((8, BLK), F32)],
        scratch=[pltpu.VMEM((BLK, 256), F32), pltpu.VMEM((N_META, 256), F32)], sem="arbitrary",
        args=(q, kv, kv, kv, dao, sinks, tabs), comm=comm)


def _in_bwd(dqkv, da, dg, dgt, w_int, h0p, dh1, gain, comm=None):
    p = h0p.shape[0]
    tm = _row_tile(p)
    nt = p // tm
    first_rows = tm - BLK

    def body(dq_ref, da_ref, dg_ref, dt_ref, w_ref, h_ref, dh_ref, g_ref, gx_ref, dm_ref, acc_ref, buf, sems):
        i = pl.program_id(0)
        slot = i % 2

        @pl.when(i == 0)
        def _():
            acc_ref[...] = jnp.zeros_like(acc_ref)

        dn = (_dot(dq_ref[...], w_ref[0:QKV_W, :]) + _dot(da_ref[...], w_ref[QKV_W:QKV_W + D, :])
              + _dot(dg_ref[...], w_ref[QKV_W + D:QKV_W + 2 * D, :]) + _dot(dt_ref[...], w_ref[QKV_W + 2 * D:IN_W, :]))
        dh, dgain = _rms_bwd(h_ref[...], g_ref[...], dn)
        dh0 = dh_ref[...] + dh
        acc_ref[0:1, :] += dgain
        buf[slot] = dh0

        @pl.when(i == 0)
        def _():
            dm_ref[...] = dh0[PAD:BLK]

        def first_copy():
            return pltpu.make_async_copy(buf.at[0, pl.ds(BLK, first_rows), :], gx_ref.at[pl.ds(0, first_rows), :], sems.at[0])

        def tile_copy(j, s):
            return pltpu.make_async_copy(buf.at[s], gx_ref.at[pl.ds(pl.multiple_of(j * tm - BLK, BLK), tm), :], sems.at[s])

        if first_rows:
            @pl.when(i == 1)
            def _():
                first_copy().wait()

        @pl.when(i >= 2)
        def _():
            tile_copy(i - 1, 1 - slot).wait()

        if first_rows:
            @pl.when(i == 0)
            def _():
                first_copy().start()

        @pl.when(i > 0)
        def _():
            tile_copy(i, slot).start()

        @pl.when(i == nt - 1)
        def _():
            tile_copy(i, slot).wait()

    def row(w):
        return pl.BlockSpec((tm, w), lambda i: (i, 0))

    return _call(
        body, name="in_bwd", grid=(nt,),
        in_specs=[row(QKV_W), row(D), row(D), row(2 * D), VM, row(D), row(D), VM],
        out_specs=[ANY, pl.BlockSpec((N_META, D), lambda i: (0, 0)), pl.BlockSpec((8, D), lambda i: (0, 0))],
        out_shape=[jax.ShapeDtypeStruct((p - BLK, D), F32), jax.ShapeDtypeStruct((N_META, D), F32),
                   jax.ShapeDtypeStruct((8, D), F32)],
        scratch=[pltpu.VMEM((2, tm, D), F32), pltpu.SemaphoreType.DMA((2,))],
        sem="arbitrary", args=(dqkv, da, dg, dgt, w_int, h0p, dh1, gain), comm=comm)


def _sum_slots(slots, name):
    r = slots.shape[0] // N_DEV
    cols = slots.shape[1]
    tr = r if r <= 352 else (r // 2 if (r // 2) % 16 == 0 else r // 3)
    steps = r // tr

    def body(*refs):
        acc = refs[0][...].astype(F32)
        for s in range(1, N_DEV):
            acc = acc + refs[s][...].astype(F32)
        refs[N_DEV][...] = acc

    return pl.pallas_call(
        body, name=name, grid=(steps,),
        in_specs=[pl.BlockSpec((tr, cols), functools.partial(lambda i, s: (s * steps + i, 0), s=s)) for s in range(N_DEV)],
        out_specs=pl.BlockSpec((tr, cols), lambda i: (i, 0)),
        out_shape=jax.ShapeDtypeStruct((r, cols), F32),
        compiler_params=_cparams("parallel"),
    )(*([slots] * N_DEV))


def _adamw_math(w, g, m, v):
    m_n = ADAM_B1 * m + (1.0 - ADAM_B1) * g
    v_n = ADAM_B2 * v + (1.0 - ADAM_B2) * jnp.square(g)
    m_hat = m_n / (1.0 - ADAM_B1 ** ADAM_STEP)
    v_hat = v_n / (1.0 - ADAM_B2 ** ADAM_STEP)
    return -ADAM_LR * (m_hat / (jnp.sqrt(v_hat) + ADAM_EPS) + ADAM_WD * w), m_n, v_n


def _sum_adamw(parts, w, m, v, name, nslots=N_DEV):
    r, cols = w.shape
    rs = r // len(parts)
    tr = rs if rs <= 352 else (rs // 2 if (rs // 2) % 16 == 0 else rs // 3)
    steps = rs // tr

    def body(*refs):
        w_ref, m_ref, v_ref, g_ref, d_ref, nm_ref, nv_ref = refs[nslots * len(parts):]
        i = pl.program_id(0)
        for q in range(len(parts)):
            @pl.when(i // steps == q)
            def _(q=q):
                g = refs[nslots * q][...].astype(F32)
                for s in range(1, nslots):
                    g = g + refs[nslots * q + s][...].astype(F32)
                g_ref[...] = g
                d_ref[...], nm_ref[...], nv_ref[...] = _adamw_math(w_ref[...], g, m_ref[...], v_ref[...])

    def slot_spec(q, s):
        return pl.BlockSpec((tr, cols), lambda i: (s * steps + jnp.clip(i - q * steps, 0, steps - 1), 0))

    spec = pl.BlockSpec((tr, cols), lambda i: (i, 0))
    return pl.pallas_call(
        body, name=name, grid=(steps * len(parts),),
        in_specs=[slot_spec(q, s) for q in range(len(parts)) for s in range(nslots)] + [spec] * 3,
        out_specs=[spec] * 4, out_shape=[jax.ShapeDtypeStruct((r, cols), F32)] * 4,
        compiler_params=_cparams("parallel"),
    )(*[a for a in parts for _ in range(nslots)], w, m, v)


def _adamw_many(ws, gs, ms, vs, name):
    n = len(ws)

    def body(*refs):
        w, g, m, v = refs[0:n], refs[n:2 * n], refs[2 * n:3 * n], refs[3 * n:4 * n]
        d, nm, nv = refs[4 * n:5 * n], refs[5 * n:6 * n], refs[6 * n:7 * n]
        for k in range(n):
            d[k][...], nm[k][...], nv[k][...] = _adamw_math(w[k][...], g[k][...], m[k][...], v[k][...])

    outs = pl.pallas_call(
        body, name=name, in_specs=[VM] * (4 * n), out_specs=[VM] * (3 * n),
        out_shape=[jax.ShapeDtypeStruct(a.shape, F32) for a in ws] * 3,
    )(*ws, *gs, *ms, *vs)
    return outs[0:n], outs[n:2 * n], outs[2 * n:3 * n]


def _rope_tables(p):
    half = ROT_DIM // 2
    lane = jnp.arange(BLK)
    seg = (lane % HEAD_DIM) // half
    inv_freq = ROPE_THETA ** (-(lane % half).astype(F32) * 2.0 / ROT_DIM)
    pos = (jnp.arange(p) - PAD).astype(F32)
    ang = pos[:, None] * inv_freq[None, :]
    cos = jnp.cos(ang)
    sin = jnp.sin(ang)
    c = jnp.where(seg[None, :] < 2, cos, 1.0)
    s1 = jnp.where(seg[None, :] == 0, -sin, 0.0)
    s2 = jnp.where(seg[None, :] == 1, sin, 0.0)
    return jnp.concatenate([c, s1, s2], axis=1).astype(F32)


def _flat_pack(parts, rows):
    flat = jnp.concatenate([a.reshape(-1).astype(F32) for a in parts])
    return jnp.pad(flat, (0, rows * D - flat.shape[0])).reshape(rows, D)


def _flat_unpack(pack, shapes):
    flat = pack.reshape(-1)
    out, off = [], 0
    for s in shapes:
        size = 1
        for e in s:
            size *= e
        out.append(flat[off:off + size].reshape(s))
        off += size
    return out


def kernel(x, meta_tokens, norm_pre_mix, norm_post_mix, w_in, b_in, attn_sinks, w_attn_proj, conv_dw_w, conv_dw_b, conv_ln_g, conv_ln_b, w_conv_proj, b_conv_proj, w_out, norm_pre_ffn, norm_post_ffn, w_up, ffn_dw_w, ffn_dw_b, w_down, loss_target, m_meta_tokens, m_norm_pre_mix, m_norm_post_mix, m_w_in, m_b_in, m_attn_sinks, m_w_attn_proj, m_conv_dw_w, m_conv_dw_b, m_conv_ln_g, m_conv_ln_b, m_w_conv_proj, m_b_conv_proj, m_w_out, m_norm_pre_ffn, m_norm_post_ffn, m_w_up, m_ffn_dw_w, m_ffn_dw_b, m_w_down, v_meta_tokens, v_norm_pre_mix, v_norm_post_mix, v_w_in, v_b_in, v_attn_sinks, v_w_attn_proj, v_conv_dw_w, v_conv_dw_b, v_conv_ln_g, v_conv_ln_b, v_w_conv_proj, v_b_conv_proj, v_w_out, v_norm_pre_ffn, v_norm_post_ffn, v_w_up, v_ffn_dw_w, v_ffn_dw_b, v_w_down):
    seq = x.shape[1]
    p = seq + BLK
    me = 4 * lax.axis_index("x") + 2 * lax.axis_index("y") + lax.axis_index("c")
    in_cols = w_in.shape[2]
    up_cols = w_up.shape[2]

    small = jnp.zeros((56, up_cols), F32)
    small = small.at[0:N_META, 0:BLK].set(meta_tokens)
    small = small.at[16:16 + CONV_K, 0:BLK].set(conv_dw_w[0])
    small = small.at[48:48 + FFN_K, :].set(ffn_dw_w[0])
    w_int, small_all = _exchange(_Both(_GatherRelay(w_in[0].T.astype(BF16)), _Gather([small])), "gather_w_in")
    small_all = small_all.reshape(N_DEV, 56, up_cols)
    meta_full = small_all[:, 0:N_META, 0:BLK].transpose(1, 0, 2).reshape(N_META, D)
    cdw = small_all[:, 16:16 + CONV_K, 0:BLK].transpose(1, 0, 2).reshape(CONV_K, D)
    cdw32 = jnp.pad(cdw, ((0, 32 - CONV_K), (0, 0)))
    fdw = small_all[:, 48:48 + FFN_K, :].transpose(1, 0, 2).reshape(FFN_K, 2 * FFN)

    tabs = _rope_tables(p)
    vecs = jnp.concatenate([conv_ln_g, conv_ln_b, b_conv_proj, norm_post_mix, norm_pre_ffn, jnp.zeros((3, D), F32)], axis=0)

    (h0p, n1, q, kv, ag, gates), (wa, wc, wo) = _in_proj(
        x[0], meta_full, norm_pre_mix, w_int, b_in, tabs,
        comm=_Gather([w_attn_proj[0].astype(BF16), w_conv_proj[0].astype(BF16), w_out[0].astype(BF16)]))
    (ao,), (w_upt,) = _attn_fwd(q, kv, attn_sinks, comm=_GatherRelay(w_up[0].T.astype(BF16)))
    (c0,), (wd,) = _conv31_fwd(ag, cdw32, conv_dw_b, comm=_Gather([w_down[0].astype(BF16)]))
    c1, attn, conv, merged, mix, h1, n2 = _mixer_fwd(ao, c0, gates, h0p, wa, wc, wo, vecs)
    u0 = _mm_nt(n2, w_upt, "ffn_up")
    act, dact_dv, dact_dg = _ffn_act(u0, fdw, ffn_dw_b)
    dffn, dact, dy, acc_f = _ffn_down_loss(act, wd, h1, loss_target[0], norm_post_ffn)

    (dug, duv, gfw_g, gfw_v, gfb_g, gfb_v, g_wd), _ = _ffn_act_bwd(u0, dact, dact_dg, dact_dv, fdw, act, dffn)
    g_wupt, (s_wd0,) = _mm_tn([dug, duv], n2, "grad_w_up", comm=_Scatter([g_wd], 0, 2))
    (dh1, acc_u), (s_wd1,) = _ffn_in_bwd(dug, duv, w_upt, h1, dy, norm_pre_ffn, comm=_Scatter([g_wd], 1, 2))
    (dmix, dat, dcv, dgt, dao, dc0, acc_m), (s_wup0,) = _mixer_bwd(
        dh1, mix, attn, conv, gates, c0, wa, wc, wo, vecs, comm=_Scatter([g_wupt], 0, 4))
    (da, dg, g_cdw, g_cdb, g_wo, g_wa, g_wc), (s_wup1, s_wup2, s_wup3) = _conv31_bwd(
        ag, dc0, cdw32, [(merged, dmix), (ao, dat), (c1, dcv)],
        comm=_Both(_Both(_Scatter([g_wupt], 1, 4), _Scatter([g_wupt], 2, 4)), _Scatter([g_wupt], 3, 4)))
    (dqkv, dsink), (s_wa, s_wc, s_wo) = _attn_bwd(q, kv, dao, attn_sinks, tabs, comm=_Scatter([g_wa, g_wc, g_wo]))
    loss_row = jnp.sum(acc_f[1:2, :], axis=1, keepdims=True)
    early = [loss_row, acc_m[0:1], dsink[0:1, 0:16], g_cdw[0:CONV_K], g_cdb,
             acc_m[2:3], acc_m[3:4], acc_m[1:2], acc_u[0:1], acc_f[0:1],
             jnp.concatenate([gfw_g, gfw_v], axis=1), jnp.concatenate([gfb_g, gfb_v], axis=1)]
    (g_wint, g_bin), (gathered_early,) = _mm_tn([dqkv, da, dg, dgt], n1, "grad_w_in", col_sums=True,
                                                comm=_Gather([_flat_pack(early, 64)]))
    (from_sibling,) = _exchange(_SiblingSwap(g_wint), "swap_w_in")
    (grad_x2d, dmeta, acc_i), (s_win,) = _in_bwd(dqkv, da, dg, dgt, w_int, h0p, dh1, norm_pre_mix,
                                                 comm=_ChipScatter(_pair_add(g_wint, from_sibling)))

    big = []
    for nm, parts, nslots, w, m, v, tr in (
            ("w_in", [s_win], N_CHIP, w_in, m_w_in, v_w_in, True), ("w_up", [s_wup0, s_wup1, s_wup2, s_wup3], N_DEV, w_up, m_w_up, v_w_up, True),
            ("w_attn_proj", [s_wa], N_DEV, w_attn_proj, m_w_attn_proj, v_w_attn_proj, False),
            ("w_conv_proj", [s_wc], N_DEV, w_conv_proj, m_w_conv_proj, v_w_conv_proj, False),
            ("w_out", [s_wo], N_DEV, w_out, m_w_out, v_w_out, False),
            ("w_down", [s_wd0, s_wd1], N_DEV, w_down, m_w_down, v_w_down, False)):
        ins = [a[0].T if tr else a[0] for a in (w, m, v)]
        big.append(tuple((o.T if tr else o)[None] for o in _sum_adamw(parts, *ins, "update_" + nm, nslots)))

    late = [dmeta, acc_i[0:1], g_bin]
    (gathered_late,) = _exchange(_Gather([_flat_pack(late, 24)]), "gather_small_grads")
    g_meta, g_npm, g_bi = _flat_unpack(_sum_slots(gathered_late, "sum_late_grads"), [a.shape for a in late])
    tot = _flat_unpack(_sum_slots(gathered_early, "sum_small_grads"), [a.shape for a in early])
    (loss, g_nqm, g_sk, g_cw, g_cb, g_lg, g_lb, g_bc, g_npf, g_nqf, g_fw, g_fb) = tot
    loss = loss.reshape(())
    g_meta = lax.dynamic_slice_in_dim(g_meta, me * BLK, BLK, axis=1)
    g_cw = lax.dynamic_slice_in_dim(g_cw, me * BLK, BLK, axis=1)[None]
    g_fw = lax.dynamic_slice_in_dim(g_fw, me * up_cols, up_cols, axis=1)[None]

    sm_w = [meta_tokens, norm_pre_mix, norm_post_mix, b_in, attn_sinks, conv_dw_w, conv_dw_b, conv_ln_g, conv_ln_b,
            b_conv_proj, norm_pre_ffn, norm_post_ffn, ffn_dw_w, ffn_dw_b]
    sm_g = [g_meta, g_npm, g_nqm, g_bi, g_sk, g_cw, g_cb, g_lg, g_lb, g_bc, g_npf, g_nqf, g_fw, g_fb]
    sm_m = [m_meta_tokens, m_norm_pre_mix, m_norm_post_mix, m_b_in, m_attn_sinks, m_conv_dw_w, m_conv_dw_b, m_conv_ln_g,
            m_conv_ln_b, m_b_conv_proj, m_norm_pre_ffn, m_norm_post_ffn, m_ffn_dw_w, m_ffn_dw_b]
    sm_v = [v_meta_tokens, v_norm_pre_mix, v_norm_post_mix, v_b_in, v_attn_sinks, v_conv_dw_w, v_conv_dw_b, v_conv_ln_g,
            v_conv_ln_b, v_b_conv_proj, v_norm_pre_ffn, v_norm_post_ffn, v_ffn_dw_w, v_ffn_dw_b]
    swap = lambda a: jnp.transpose(a, (1, 0, 2)) if a.ndim == 3 else a
    sm_d, sm_nm, sm_nv = ([swap(o) for o in outs] for outs in
                          _adamw_many(*([swap(a) for a in group] for group in (sm_w, sm_g, sm_m, sm_v)), "adamw_small"))

    order = ["meta_tokens", "norm_pre_mix", "norm_post_mix", "w_in", "b_in", "attn_sinks", "w_attn_proj", "conv_dw_w",
             "conv_dw_b", "conv_ln_g", "conv_ln_b", "w_conv_proj", "b_conv_proj", "w_out", "norm_pre_ffn", "norm_post_ffn",
             "w_up", "ffn_dw_w", "ffn_dw_b", "w_down"]
    small_names = ["meta_tokens", "norm_pre_mix", "norm_post_mix", "b_in", "attn_sinks", "conv_dw_w", "conv_dw_b", "conv_ln_g",
                   "conv_ln_b", "b_conv_proj", "norm_pre_ffn", "norm_post_ffn", "ffn_dw_w", "ffn_dw_b"]
    big_names = ["w_in", "w_up", "w_attn_proj", "w_conv_proj", "w_out", "w_down"]
    table = {}
    for k, nm in enumerate(small_names):
        table[nm] = (sm_g[k], sm_d[k], sm_nm[k], sm_nv[k])
    for k, nm in enumerate(big_names):
        table[nm] = big[k]
    grad_x = grad_x2d[None]
    outs = [loss, grad_x]
    for field in range(4):
        outs += [table[nm][field] for nm in order]
    return tuple(outs)
```

```python
import functools

import jax
import jax.numpy as jnp
from jax import lax
from jax.experimental import pallas as pl
from jax.experimental.pallas import tpu as pltpu

F32 = jnp.float32
BF16 = jnp.bfloat16
MESH = pl.DeviceIdType.MESH

D = 1024
HEAD_DIM = 64
N_META = 16
BLK = 128
PAD = BLK - N_META
CONV_K = 31
FFN = 2816
FFN_K = 3
QKV_W = 1280
IN_W = 5376
ROT_DIM = 16
ROPE_THETA = 500000.0
RMS_EPS = 1e-6
LN_EPS = 1e-5
NEG_INF = -1e30
SCALE = HEAD_DIM ** -0.5
N_DEV = 8

ADAM_LR = 0.001
ADAM_B1 = 0.9
ADAM_B2 = 0.999
ADAM_EPS = 1e-08
ADAM_WD = 0.01
ADAM_STEP = 10

VMEM_BYTES_V7X = 64 * 1024 * 1024
VMEM_LIMIT = VMEM_BYTES_V7X - 8 * 1024 * 1024

NT = (((1,), (1,)), ((), ()))
TN = (((0,), (0,)), ((), ()))
VM = pl.BlockSpec(memory_space=pltpu.VMEM)
ANY = pl.BlockSpec(memory_space=pl.ANY)


def _cparams(*sem):
    return pltpu.CompilerParams(dimension_semantics=sem or None, vmem_limit_bytes=VMEM_LIMIT)


def _row_tile(p):
    return 384 if p % 384 == 0 else 128


def _dot(a, b):
    return jnp.dot(a, b, preferred_element_type=F32)


def _dot_nt(a, b):
    return lax.dot_general(a, b, NT, preferred_element_type=F32)


def _dot_tn(a, b):
    return lax.dot_general(a, b, TN, preferred_element_type=F32)


def _rms(x, g):
    return x * lax.rsqrt(jnp.mean(x * x, axis=-1, keepdims=True) + RMS_EPS) * g


def _lnsilu(x, g, b):
    mu = jnp.mean(x, axis=-1, keepdims=True)
    var = jnp.mean(jnp.square(x - mu), axis=-1, keepdims=True)
    z = (x - mu) * lax.rsqrt(var + LN_EPS) * g + b
    return z * jax.nn.sigmoid(z)


def _rms_bwd(x, g, dy):
    r = lax.rsqrt(jnp.mean(x * x, axis=-1, keepdims=True) + RMS_EPS)
    xn = x * r
    u = dy * g
    dg = jnp.sum(dy * xn, axis=0, keepdims=True)
    dx = r * (u - xn * jnp.mean(u * xn, axis=-1, keepdims=True))
    return dx, dg


def _lnsilu_bwd(x, g, b, dout):
    mu = jnp.mean(x, axis=-1, keepdims=True)
    xc = x - mu
    rs = lax.rsqrt(jnp.mean(xc * xc, axis=-1, keepdims=True) + LN_EPS)
    yh = xc * rs
    z = yh * g + b
    sg = jax.nn.sigmoid(z)
    dz = dout * (sg * (1.0 + z * (1.0 - sg)))
    dg = jnp.sum(dz * yh, axis=0, keepdims=True)
    db = jnp.sum(dz, axis=0, keepdims=True)
    dyh = dz * g
    dx = rs * (dyh - jnp.mean(dyh, axis=-1, keepdims=True) - yh * jnp.mean(dyh * yh, axis=-1, keepdims=True))
    return dx, dg, db


def _rope(v, c, s1, s2):
    return v * c + pltpu.roll(v, BLK - 8, 1) * s1 + pltpu.roll(v, 8, 1) * s2


def _rows(i, tm):
    return i * tm + lax.broadcasted_iota(jnp.int32, (tm, 1), 0)


def _place():
    return lax.axis_index("x"), lax.axis_index("y"), lax.axis_index("c")


def _blk(ref, idx, r, dtype):
    return ref.at[pl.ds(pl.multiple_of(idx * r, 16 if dtype == BF16 else 8), r), :]


class _Gather:
    def __init__(self, arrs):
        self.ins = list(arrs)
        n = len(arrs)
        self.out_shape = [jax.ShapeDtypeStruct((N_DEV * a.shape[0], a.shape[1]), a.dtype) for a in arrs]
        self.scratch = [pltpu.SemaphoreType.DMA((n, 7)), pltpu.SemaphoreType.DMA((n, 7)), pltpu.SemaphoreType.DMA((n,))]

    def _parts(self, ins, outs, sems):
        send_sems, recv_sems, local_sems = sems
        n = len(ins)
        x, y, c = _place()
        me, sibling = (x, y, c), (x, y, 1 - c)
        chips = [(1 - x, y), (x, 1 - y), (1 - x, 1 - y)]

        def rows(a, p):
            return _blk(outs[a], 4 * p[0] + 2 * p[1] + p[2], self.ins[a].shape[0], self.ins[a].dtype)

        def copy(a, k, block, to, src=None):
            return pltpu.make_async_remote_copy(
                src_ref=rows(a, block) if src is None else src, dst_ref=rows(a, block),
                send_sem=send_sems.at[a, k], recv_sem=recv_sems.at[a, k], device_id=to, device_id_type=MESH)

        mine = [pltpu.make_async_copy(ins[a], rows(a, me), local_sems.at[a]) for a in range(n)]
        first = []
        for a in range(n):
            first.append(copy(a, 0, me, sibling, src=ins[a]))
            first += [copy(a, 1 + j, me, (*chip, c), src=ins[a]) for j, chip in enumerate(chips)]
        return n, c, me, sibling, chips, copy, mine, first

    def start(self, ins, outs, sems):
        *_, mine, first = self._parts(ins, outs, sems)
        for cp in mine + first:
            cp.start()

    def middle(self, ins, outs, sems):
        n, c, me, sibling, chips, copy, _, _ = self._parts(ins, outs, sems)
        for j, chip in enumerate(chips):
            for a in range(n):
                copy(a, 1 + j, (*chip, c), me).wait_recv()
                copy(a, 4 + j, (*chip, c), sibling).start()

    def finish(self, ins, outs, sems, middle_done=False):
        if not middle_done:
            self.middle(ins, outs, sems)
        n, c, me, sibling, chips, copy, mine, first = self._parts(ins, outs, sems)
        passed = [copy(a, 4 + j, (*chip, c), sibling) for j, chip in enumerate(chips) for a in range(n)]
        for a in range(n):
            copy(a, 0, sibling, me).wait_recv()
            for j, chip in enumerate(chips):
                copy(a, 4 + j, (*chip, 1 - c), me).wait_recv()
        for cp in first + passed:
            cp.wait_send()
        for cp in mine:
            cp.wait()


class _GatherRelay:
    N_COPY = 13

    def __init__(self, arr):
        self.ins = [arr]
        self.r = arr.shape[0]
        self.out_shape = [jax.ShapeDtypeStruct((N_DEV * self.r, arr.shape[1]), arr.dtype)]
        self.scratch = [pltpu.SemaphoreType.DMA((self.N_COPY,)), pltpu.SemaphoreType.DMA((self.N_COPY,)),
                        pltpu.SemaphoreType.DMA]

    def _parts(self, ins, outs, sems):
        send_sems, recv_sems, local_sem = sems
        x, y, c = _place()
        r, half = self.r, self.r // 2
        out = outs[0]
        me, sib, xn, yn, dg = (x, y, c), (x, y, 1 - c), (1 - x, y, c), (x, 1 - y, c), (1 - x, 1 - y, c)
        sx, sy, sd = (1 - x, y, 1 - c), (x, 1 - y, 1 - c), (1 - x, 1 - y, 1 - c)
        lo, hi = (0, half), (half, half)

        def rows(p, part=(0, r)):
            return out.at[pl.ds(pl.multiple_of((4 * p[0] + 2 * p[1] + p[2]) * r + part[0], 16), part[1]), :]

        def own(part):
            return ins[0].at[pl.ds(part[0], part[1]), :]

        def copy(k, dev_rows, to, src=None):
            return pltpu.make_async_remote_copy(
                src_ref=dev_rows if src is None else src, dst_ref=dev_rows,
                send_sem=send_sems.at[k], recv_sem=recv_sems.at[k], device_id=to, device_id_type=MESH)

        mine = pltpu.make_async_copy(ins[0], rows(me), local_sem)
        first = [copy(0, rows(me), sib, src=ins[0]),
                 copy(1, rows(me, lo), xn, src=own(lo)), copy(3, rows(me, hi), yn, src=own(hi)),
                 copy(2, rows(me, hi), xn, src=own(hi)), copy(4, rows(me, lo), yn, src=own(lo))]
        arrive = {0: rows(sib), 1: rows(xn, lo), 2: rows(xn, hi), 3: rows(yn, hi), 4: rows(yn, lo),
                  5: rows(dg, lo), 6: rows(dg, hi), 7: rows(sx, lo), 8: rows(sx, hi), 9: rows(sy, hi),
                  10: rows(sy, lo), 11: rows(sd, lo), 12: rows(sd, hi)}
        relay = {1: [(5, rows(xn, lo), yn), (7, rows(xn, lo), sib)], 3: [(6, rows(yn, hi), xn), (9, rows(yn, hi), sib)],
                 2: [(8, rows(xn, hi), sib)], 4: [(10, rows(yn, lo), sib)],
                 5: [(11, rows(dg, lo), sib)], 6: [(12, rows(dg, hi), sib)]}
        return copy, mine, first, arrive, relay, me

    def start(self, ins, outs, sems):
        _, mine, first, _, _, _ = self._parts(ins, outs, sems)
        for cp in [mine] + first:
            cp.start()

    def finish(self, ins, outs, sems):
        copy, mine, first, arrive, relay, me = self._parts(ins, outs, sems)
        passed = []
        for k in (1, 3, 2, 4, 5, 6):
            copy(k, arrive[k], me).wait_recv()
            for k2, dev_rows, to in relay[k]:
                fwd = copy(k2, dev_rows, to)
                fwd.start()
                passed.append(fwd)
        for k in (0, 7, 8, 9, 10, 11, 12):
            copy(k, arrive[k], me).wait_recv()
        for cp in first + passed:
            cp.wait_send()
        mine.wait()


FLIPS = [(0, 0, 1), (1, 0, 0), (0, 1, 0), (1, 1, 0), (1, 0, 1), (0, 1, 1), (1, 1, 1)]


class _Scatter:
    def __init__(self, arrs, part=0, nparts=1):
        self.ins = list(arrs)
        self.part, self.nparts = part, nparts
        n = len(arrs)
        self.out_shape = [jax.ShapeDtypeStruct((a.shape[0] // nparts, a.shape[1]), a.dtype) for a in arrs]
        self.scratch = [pltpu.SemaphoreType.DMA((n, 7)), pltpu.SemaphoreType.DMA((n, 7)), pltpu.SemaphoreType.DMA((n,))]

    def _parts(self, ins, outs, sems):
        send_sems, recv_sems, local_sems = sems
        n = len(ins)
        x, y, c = _place()
        me = 4 * x + 2 * y + c

        def flip(v, f):
            return 1 - v if f else v

        def src(a, idx):
            r = self.ins[a].shape[0] // N_DEV
            rs = r // self.nparts
            return ins[a].at[pl.ds(pl.multiple_of(idx * r + self.part * rs, 16), rs), :]

        def dst(a, idx):
            rs = self.ins[a].shape[0] // N_DEV // self.nparts
            return outs[a].at[pl.ds(pl.multiple_of(idx * rs, 16), rs), :]

        mine = [pltpu.make_async_copy(src(a, me), dst(a, me), local_sems.at[a]) for a in range(n)]
        sends, recvs = [], []
        for k, f in enumerate(FLIPS):
            peer = (flip(x, f[0]), flip(y, f[1]), flip(c, f[2]))
            pidx = 4 * peer[0] + 2 * peer[1] + peer[2]
            for a in range(n):
                sends.append(pltpu.make_async_remote_copy(
                    src_ref=src(a, pidx), dst_ref=dst(a, me),
                    send_sem=send_sems.at[a, k], recv_sem=recv_sems.at[a, k], device_id=peer, device_id_type=MESH))
                recvs.append(functools.partial(
                    pltpu.make_async_remote_copy,
                    src_ref=src(a, pidx), dst_ref=dst(a, pidx),
                    send_sem=send_sems.at[a, k], recv_sem=recv_sems.at[a, k], device_id=peer, device_id_type=MESH))
        return mine, sends, recvs

    def start(self, ins, outs, sems):
        mine, sends, _ = self._parts(ins, outs, sems)
        for cp in mine + sends:
            cp.start()

    def finish(self, ins, outs, sems):
        mine, sends, recvs = self._parts(ins, outs, sems)
        for make in recvs:
            make().wait_recv()
        for cp in sends:
            cp.wait_send()
        for cp in mine:
            cp.wait()


N_CHIP = 4


class _SiblingSwap:
    def __init__(self, arr):
        self.ins = [arr]
        self.r = arr.shape[0] // N_DEV
        self.out_shape = [jax.ShapeDtypeStruct((N_CHIP * self.r, arr.shape[1]), arr.dtype)]
        self.scratch = [pltpu.SemaphoreType.DMA((N_CHIP,)), pltpu.SemaphoreType.DMA((N_CHIP,))]

    def _copies(self, ins, outs, sems):
        send_sems, recv_sems = sems
        x, y, c = _place()
        r = self.r
        return [pltpu.make_async_remote_copy(
            src_ref=ins[0].at[pl.ds(pl.multiple_of((2 * j + 1 - c) * r, 16), r), :],
            dst_ref=outs[0].at[pl.ds(j * r, r), :],
            send_sem=send_sems.at[j], recv_sem=recv_sems.at[j], device_id=(x, y, 1 - c), device_id_type=MESH)
            for j in range(N_CHIP)]

    def start(self, ins, outs, sems):
        for cp in self._copies(ins, outs, sems):
            cp.start()

    def finish(self, ins, outs, sems):
        for cp in self._copies(ins, outs, sems):
            cp.wait()


class _ChipScatter:
    def __init__(self, arr):
        self.ins = [arr]
        self.r = arr.shape[0] // N_CHIP
        self.out_shape = [jax.ShapeDtypeStruct(arr.shape, arr.dtype)]
        self.scratch = [pltpu.SemaphoreType.DMA((3,)), pltpu.SemaphoreType.DMA((3,)), pltpu.SemaphoreType.DMA]

    def _parts(self, ins, outs, sems):
        send_sems, recv_sems, local_sem = sems
        x, y, c = _place()
        r = self.r
        my_chip = 2 * x + y

        def rows(ref, j):
            return ref.at[pl.ds(pl.multiple_of(j * r, 16), r), :]

        mine = pltpu.make_async_copy(rows(ins[0], my_chip), rows(outs[0], my_chip), local_sem)
        sends, recvs = [], []
        for k, (fx, fy) in enumerate(((1, 0), (0, 1), (1, 1))):
            px, py = (1 - x if fx else x), (1 - y if fy else y)
            peer_chip = 2 * px + py
            sends.append(pltpu.make_async_remote_copy(
                src_ref=rows(ins[0], peer_chip), dst_ref=rows(outs[0], my_chip),
                send_sem=send_sems.at[k], recv_sem=recv_sems.at[k], device_id=(px, py, c), device_id_type=MESH))
            recvs.append(functools.partial(
                pltpu.make_async_remote_copy,
                src_ref=rows(ins[0], peer_chip), dst_ref=rows(outs[0], peer_chip),
                send_sem=send_sems.at[k], recv_sem=recv_sems.at[k], device_id=(px, py, c), device_id_type=MESH))
        return mine, sends, recvs

    def start(self, ins, outs, sems):
        mine, sends, _ = self._parts(ins, outs, sems)
        for cp in [mine] + sends:
            cp.start()

    def finish(self, ins, outs, sems):
        mine, sends, recvs = self._parts(ins, outs, sems)
        for make in recvs:
            make().wait_recv()
        for cp in sends:
            cp.wait_send()
        mine.wait()


def _pair_add(partial, recv):
    r = recv.shape[0] // N_CHIP
    cols = recv.shape[1]
    tr = r // 2 if (r // 2) % 16 == 0 else r
    steps = r // tr
    core = lax.axis_index("c").astype(jnp.int32).reshape(1)

    def body(c_ref, p_ref, s_ref, o_ref):
        o_ref[...] = (p_ref[...].astype(F32) + s_ref[...].astype(F32)).astype(BF16)

    spec = pl.BlockSpec((tr, cols), lambda j, i, c_ref: (j * steps + i, 0))
    return pl.pallas_call(
        body, name="pair_add",
        grid_spec=pltpu.PrefetchScalarGridSpec(
            num_scalar_prefetch=1, grid=(N_CHIP, steps),
            in_specs=[pl.BlockSpec((tr, cols), lambda j, i, c_ref: ((2 * j + c_ref[0]) * steps + i, 0)), spec],
            out_specs=spec),
        out_shape=jax.ShapeDtypeStruct(recv.shape, BF16),
        compiler_params=_cparams("parallel", "parallel"),
    )(core, partial, recv)


class _Both:
    def __init__(self, a, b):
        self.a, self.b = a, b
        self.ins = a.ins + b.ins
        self.out_shape = a.out_shape + b.out_shape
        self.scratch = a.scratch + b.scratch

    def _split(self, ins, outs, sems):
        ni, no, ns = len(self.a.ins), len(self.a.out_shape), len(self.a.scratch)
        return (ins[:ni], outs[:no], sems[:ns]), (ins[ni:], outs[no:], sems[ns:])

    def start(self, ins, outs, sems):
        ra, rb = self._split(ins, outs, sems)
        self.a.start(*ra)
        self.b.start(*rb)

    def finish(self, ins, outs, sems):
        ra, rb = self._split(ins, outs, sems)
        self.a.finish(*ra)
        self.b.finish(*rb)


def _exchange(comm, name):
    n, m = len(comm.ins), len(comm.out_shape)

    def body(*refs):
        ins, outs, sems = refs[:n], refs[n:n + m], refs[n + m:]
        comm.start(ins, outs, sems)
        comm.finish(ins, outs, sems)

    return pl.pallas_call(
        body, name=name, out_shape=comm.out_shape, in_specs=[ANY] * n, out_specs=[ANY] * m, scratch_shapes=comm.scratch,
    )(*comm.ins)


def _call(body, *, name, grid, in_specs, out_specs, out_shape, args, scratch=(), sem="parallel", comm=None,
          comm_mid=None):
    if comm is None:
        outs = pl.pallas_call(
            body, name=name, grid=grid, in_specs=list(in_specs), out_specs=list(out_specs), out_shape=list(out_shape),
            scratch_shapes=list(scratch), compiler_params=_cparams(sem))(*args)
        return outs, []
    n_in, n_out, n_sc = len(in_specs), len(out_specs), len(scratch)
    n_ci, n_co = len(comm.ins), len(comm.out_shape)
    last = grid[0] - 1

    def fused(*refs):
        ins, refs = refs[:n_in], refs[n_in:]
        c_ins, refs = refs[:n_ci], refs[n_ci:]
        outs, refs = refs[:n_out], refs[n_out:]
        c_outs, refs = refs[:n_co], refs[n_co:]
        sc, c_sems = refs[:n_sc], refs[n_sc:]
        step = pl.program_id(0)

        @pl.when(step == 0)
        def _():
            comm.start(c_ins, c_outs, c_sems)

        body(*ins, *outs, *sc)

        if comm_mid is not None:
            @pl.when(step == comm_mid)
            def _():
                comm.middle(c_ins, c_outs, c_sems)

        @pl.when(step == last)
        def _():
            if comm_mid is not None:
                comm.finish(c_ins, c_outs, c_sems, middle_done=True)
            else:
                comm.finish(c_ins, c_outs, c_sems)

    outs = pl.pallas_call(
        fused, name=name, grid=grid, in_specs=list(in_specs) + [ANY] * n_ci, out_specs=list(out_specs) + [ANY] * n_co,
        out_shape=list(out_shape) + comm.out_shape, scratch_shapes=list(scratch) + comm.scratch,
        compiler_params=_cparams("arbitrary"))(*args, *comm.ins)
    return outs[:n_out], outs[n_out:]


def _token_specs(tm):
    k = tm // BLK
    return [pl.BlockSpec((BLK, D), functools.partial(lambda i, t: (jnp.maximum(k * i + t - 1, 0), 0), t=t)) for t in range(k)]


def _in_proj(x2d, meta, gain, w_int, b_in, tabs, comm=None):
    p = x2d.shape[0] + BLK
    tm = _row_tile(p)
    k = tm // BLK

    def body(*refs):
        x_refs = refs[:k]
        m_ref, g_ref, w_ref, b_ref, t_ref, h_ref, n1_ref, q_ref, kv_ref, ag_ref, gt_ref = refs[k:]
        i = pl.program_id(0)
        head = jnp.concatenate([jnp.zeros((PAD, D), F32), m_ref[...]], axis=0)
        first = jnp.where(i == 0, head, x_refs[0][...])
        h = jnp.concatenate([first] + [r[...] for r in x_refs[1:]], axis=0) if k > 1 else first
        h_ref[...] = h
        n = _rms(h, g_ref[...]).astype(BF16)
        n1_ref[...] = n
        c, s1, s2 = t_ref[:, 0:128], t_ref[:, 128:256], t_ref[:, 256:384]

        def mm(c0, w):
            return _dot_nt(n, w_ref[c0:c0 + w, :]) + b_ref[:, c0:c0 + w]

        for j in range(4):
            acc = mm(256 * j, 256)
            for t in range(2):
                lo = 256 * j + 128 * t
                q_ref[:, lo:lo + 128] = (_rope(acc[:, 128 * t:128 * (t + 1)], c, s1, s2) * SCALE).astype(BF16)
        acc = mm(1024, 256)
        kv_ref[:, 0:128] = _rope(acc[:, 0:128], c, s1, s2).astype(BF16)
        kv_ref[:, 128:256] = acc[:, 128:256].astype(BF16)
        for j in range(8):
            ag_ref[:, 256 * j:256 * (j + 1)] = mm(QKV_W + 256 * j, 256).astype(BF16)
        for j in range(8):
            gt_ref[:, 256 * j:256 * (j + 1)] = mm(QKV_W + 2048 + 256 * j, 256).astype(BF16)

    def row(w):
        return pl.BlockSpec((tm, w), lambda i: (i, 0))

    return _call(
        body, name="in_proj", grid=(p // tm,),
        in_specs=_token_specs(tm) + [VM, VM, VM, VM, row(384)],
        out_specs=[row(D), row(D), row(D), row(256), row(2048), row(2048)],
        out_shape=[jax.ShapeDtypeStruct((p, D), F32)] + [jax.ShapeDtypeStruct((p, w), BF16) for w in (D, D, 256, 2048, 2048)],
        args=(x2d,) * k + (meta, gain, w_int, b_in, tabs), comm=comm,
        comm_mid=None if comm is None else (3 * (p // tm)) // 4)


N_KEY = 2 * BLK + N_META


def _attn_setup(n, h, q_ref, km_ref, kp_ref, kc_ref):
    lo = lax.broadcasted_iota(jnp.int32, (BLK, BLK), 1) < HEAD_DIM
    lok = lax.broadcasted_iota(jnp.int32, (N_KEY, BLK), 1) < HEAD_DIM

    def dup(lanes):
        cat = jnp.concatenate([kp_ref[:, lanes], kc_ref[:, lanes], km_ref[PAD:BLK, lanes]], axis=0).astype(F32)
        rolled = pltpu.roll(cat, HEAD_DIM, 1)
        return (jnp.where(lok, cat, rolled) if h == 0 else jnp.where(lok, rolled, cat)).astype(BF16)

    k2 = dup(slice(0, 128))
    v2 = dup(slice(128, 256))
    qs = _stack_heads(q_ref, h, lo)

    kr = lax.broadcasted_iota(jnp.int32, (BLK, BLK), 0)
    tq = BLK * n + lax.broadcasted_iota(jnp.int32, (BLK, BLK), 1) - PAD
    t_p = BLK * (n - 1) + kr - PAD
    t_c = BLK * n + kr - PAD
    ok_p = jnp.logical_and(t_p >= N_META, tq - t_p < BLK)
    ok_c = jnp.logical_and(t_c >= N_META, t_c <= tq)
    ok_m = lax.broadcasted_iota(jnp.int32, (N_META, BLK), 0) <= BLK * n + lax.broadcasted_iota(jnp.int32, (N_META, BLK), 1) - PAD
    bias = jnp.concatenate([jnp.where(ok, 0.0, NEG_INF).astype(F32) for ok in (ok_p, ok_c, ok_m)], axis=0)
    return qs, k2, v2, bias, lok


def _attn_head(s, bias, sink):
    s = s + bias
    m = jnp.maximum(jnp.max(s, axis=0, keepdims=True), sink)
    e = jnp.exp(s - m)
    es = jnp.exp(sink - m)
    inv = 1.0 / (jnp.sum(e, axis=0, keepdims=True) + es)
    return e * inv, es * inv


def _stack_heads(ref, h, lo):
    pieces = []
    for jp in range(4):
        v = ref[:, BLK * (4 * h + jp):BLK * (4 * h + jp + 1)]
        zero = jnp.zeros_like(v)
        pieces += [jnp.where(lo, v, zero), jnp.where(lo, zero, v)]
    return jnp.concatenate(pieces, axis=0)


def _unstack_heads(v, jp, lo):
    return jnp.where(lo, v[256 * jp:256 * jp + 128], v[256 * jp + 128:256 * jp + 256])


def _attn_fwd(q, kv, sinks, comm=None):
    p = q.shape[0]
    nb = p // BLK

    def body(q_ref, km_ref, kp_ref, kc_ref, sink_ref, o_ref):
        n = pl.program_id(0)
        lo = lax.broadcasted_iota(jnp.int32, (BLK, BLK), 1) < HEAD_DIM
        for h in range(2):
            qs, k2, v2, bias, _ = _attn_setup(n, h, q_ref, km_ref, kp_ref, kc_ref)
            st = _dot_nt(k2, qs)
            pt = jnp.concatenate(
                [_attn_head(st[:, BLK * g:BLK * (g + 1)], bias, sink_ref[0, 8 * h + g])[0].astype(BF16) for g in range(8)],
                axis=1)
            o = _dot_tn(pt, v2)
            for jp in range(4):
                o_ref[:, BLK * (4 * h + jp):BLK * (4 * h + jp + 1)] = _unstack_heads(o, jp, lo).astype(BF16)

    return _call(
        body, name="attn_fwd", grid=(nb,),
        in_specs=[pl.BlockSpec((BLK, D), lambda i: (i, 0)),
                  pl.BlockSpec((BLK, 256), lambda i: (0, 0)),
                  pl.BlockSpec((BLK, 256), lambda i: (jnp.maximum(i - 1, 0), 0)),
                  pl.BlockSpec((BLK, 256), lambda i: (i, 0)),
                  pl.BlockSpec(memory_space=pltpu.SMEM)],
        out_specs=[pl.BlockSpec((BLK, D), lambda i: (i, 0))],
        out_shape=[jax.ShapeDtypeStruct((p, D), BF16)],
        args=(q, kv, kv, kv, sinks), comm=comm)


def _conv31_fwd(ag, w32, b, comm=None):
    p = ag.shape[0]
    nch = p // BLK

    def body(a_ref, g_ref, w_ref, b_ref, o_ref, gp):
        gp[0:32, :] = jnp.zeros((32, BLK), F32)
        for ci in range(nch):
            r0 = BLK * ci
            glu = a_ref[r0:r0 + BLK, :].astype(F32) * jax.nn.sigmoid(g_ref[r0:r0 + BLK, :].astype(F32))
            if ci == 0:
                glu = jnp.where(_rows(0, BLK) >= PAD, glu, 0.0)
            gp[32 + r0:32 + r0 + BLK, :] = glu
        for ci in range(nch):
            r0 = BLK * ci
            acc = jnp.broadcast_to(b_ref[...], (BLK, BLK))
            for j in range(CONV_K):
                acc = acc + w_ref[j:j + 1, :] * gp[r0 + j + 2:r0 + j + 2 + BLK, :]
            o_ref[r0:r0 + BLK, :] = acc

    return _call(
        body, name="conv31_fwd", grid=(D // BLK,),
        in_specs=[pl.BlockSpec((p, BLK), lambda j: (0, j)), pl.BlockSpec((p, BLK), lambda j: (0, 8 + j)),
                  pl.BlockSpec((32, BLK), lambda j: (0, j)), pl.BlockSpec((1, BLK), lambda j: (0, j))],
        out_specs=[pl.BlockSpec((p, BLK), lambda j: (0, j))],
        out_shape=[jax.ShapeDtypeStruct((p, D), F32)],
        scratch=[pltpu.VMEM((p + 32, BLK), F32)],
        args=(ag, ag, w32, b), comm=comm, comm_mid=None if comm is None else (3 * (D // BLK)) // 4)


def _mixer_fwd(ao, c0, gates, h0p, wa, wc, wo, vecs):
    p = ao.shape[0]
    tm = _row_tile(p)

    def body(ao_ref, c0_ref, gt_ref, h_ref, wa_ref, wc_ref, wo_ref, v_ref,
             c1_ref, at_ref, cv_ref, mg_ref, mix_ref, h1_ref, n2_ref):
        i = pl.program_id(0)
        c1 = _lnsilu(c0_ref[...], v_ref[0:1, :], v_ref[1:2, :]).astype(BF16)
        c1_ref[...] = c1
        attn = _dot(ao_ref[...], wa_ref[...])
        conv = _dot(c1, wc_ref[...]) + v_ref[2:3, :]
        at_ref[...] = attn.astype(BF16)
        cv_ref[...] = conv.astype(BF16)
        merged = (jax.nn.sigmoid(gt_ref[:, 0:D].astype(F32)) * attn
                  + jax.nn.sigmoid(gt_ref[:, D:2 * D].astype(F32)) * conv).astype(BF16)
        mg_ref[...] = merged
        mix = _dot(merged, wo_ref[...])
        mix_ref[...] = mix
        h1 = jnp.where(_rows(i, tm) >= PAD, h_ref[...] + _rms(mix, v_ref[3:4, :]), 0.0)
        h1_ref[...] = h1
        n2_ref[...] = _rms(h1, v_ref[4:5, :]).astype(BF16)

    def row(w):
        return pl.BlockSpec((tm, w), lambda i: (i, 0))

    return pl.pallas_call(
        body, name="mixer_fwd", grid=(p // tm,),
        in_specs=[row(D), row(D), row(2 * D), row(D), VM, VM, VM, VM],
        out_specs=[row(D)] * 7,
        out_shape=[jax.ShapeDtypeStruct((p, D), t) for t in (BF16, BF16, BF16, BF16, F32, F32, BF16)],
        compiler_params=_cparams("parallel"),
    )(ao, c0, gates, h0p, wa, wc, wo, vecs)


def _mm_nt(a, w_t, name):
    p, k = a.shape
    n = w_t.shape[0]
    tm = _row_tile(p)
    ch = 512

    def body(a_ref, w_ref, o_ref):
        a_v = a_ref[...]
        for c0 in range(0, n, ch):
            o_ref[:, c0:c0 + ch] = _dot_nt(a_v, w_ref[c0:c0 + ch, :]).astype(BF16)

    return pl.pallas_call(
        body, name=name, grid=(p // tm,),
        in_specs=[pl.BlockSpec((tm, k), lambda i: (i, 0)), VM],
        out_specs=pl.BlockSpec((tm, n), lambda i: (i, 0)),
        out_shape=jax.ShapeDtypeStruct((p, n), BF16),
        compiler_params=_cparams("parallel"),
    )(a, w_t)


def _conv3(xp_ref, w_ref, r0):
    return (w_ref[0:1, :] * xp_ref[r0 + 6:r0 + 6 + BLK, :] + w_ref[1:2, :] * xp_ref[r0 + 7:r0 + 7 + BLK, :]
            + w_ref[2:3, :] * xp_ref[r0 + 8:r0 + 8 + BLK, :])


def _ffn_slab_specs(p):
    ncol = FFN // BLK
    return [pl.BlockSpec((p, BLK), lambda j: (0, j)), pl.BlockSpec((p, BLK), lambda j: (0, ncol + j)),
            pl.BlockSpec((FFN_K, BLK), lambda j: (0, j)), pl.BlockSpec((FFN_K, BLK), lambda j: (0, ncol + j)),
            pl.BlockSpec((1, BLK), lambda j: (0, j)), pl.BlockSpec((1, BLK), lambda j: (0, ncol + j))]


def _fill_shifted(dst, src_ref, nch):
    dst[0:8, :] = jnp.zeros((8, BLK), F32)
    for ci in range(nch):
        dst[8 + BLK * ci:8 + BLK * (ci + 1), :] = src_ref[BLK * ci:BLK * (ci + 1), :].astype(F32)


def _ffn_act(u0, fw, fb):
    p = u0.shape[0]
    nch = p // BLK

    def body(g_ref, v_ref, wg_ref, wv_ref, bg_ref, bv_ref, o_ref, dv_ref, dg_ref, xg, xv):
        _fill_shifted(xg, g_ref, nch)
        _fill_shifted(xv, v_ref, nch)
        for ci in range(nch):
            r0 = BLK * ci
            ug = _conv3(xg, wg_ref, r0) + bg_ref[...]
            uv = _conv3(xv, wv_ref, r0) + bv_ref[...]
            sg = jax.nn.sigmoid(ug)
            silu = ug * sg
            o_ref[r0:r0 + BLK, :] = (silu * uv).astype(BF16)
            dv_ref[r0:r0 + BLK, :] = silu.astype(BF16)
            dg_ref[r0:r0 + BLK, :] = (uv * (sg * (1.0 + ug * (1.0 - sg)))).astype(BF16)

    slab = pl.BlockSpec((p, BLK), lambda j: (0, j))
    return pl.pallas_call(
        body, name="ffn_act", grid=(FFN // BLK,),
        in_specs=_ffn_slab_specs(p),
        out_specs=[slab] * 3,
        out_shape=[jax.ShapeDtypeStruct((p, FFN), BF16)] * 3,
        scratch_shapes=[pltpu.VMEM((p + 8, BLK), F32)] * 2,
        compiler_params=_cparams("parallel"),
    )(u0, u0, fw, fw, fb, fb)


def _ffn_down_loss(act, wd, h1, tgt, gain):
    p = act.shape[0]
    tm = _row_tile(p)
    k = tm // BLK

    def body(*refs):
        a_ref, w_ref, h_ref = refs[:3]
        t_refs = refs[3:3 + k]
        g_ref, df_ref, da_ref, dy_ref, acc_ref = refs[3 + k:]
        i = pl.program_id(0)

        @pl.when(i == 0)
        def _():
            acc_ref[...] = jnp.zeros_like(acc_ref)

        ffn = _dot(a_ref[...], w_ref[...])
        t = jnp.concatenate([t_ref[...] for t_ref in t_refs], axis=0) if k > 1 else t_refs[0][...]
        diff = jnp.where(_rows(i, tm) >= BLK, h_ref[...] + _rms(ffn, g_ref[...]) - t, 0.0)
        dy = diff * (1.0 / D)
        dffn, dg = _rms_bwd(ffn, g_ref[...], dy)
        acc_ref[0:1, :] += dg
        acc_ref[1:2, :] += jnp.sum(diff * diff, axis=0, keepdims=True) * (0.5 / D)
        dy_ref[...] = dy
        dfb = dffn.astype(BF16)
        df_ref[...] = dfb
        for c0 in range(0, FFN, 256):
            da_ref[:, c0:c0 + 256] = _dot_nt(dfb, w_ref[c0:c0 + 256, :]).astype(BF16)

    def row(w):
        return pl.BlockSpec((tm, w), lambda i: (i, 0))

    return pl.pallas_call(
        body, name="ffn_down_loss", grid=(p // tm,),
        in_specs=[row(FFN), VM, row(D)] + _token_specs(tm) + [VM],
        out_specs=[row(D), row(FFN), row(D), pl.BlockSpec((8, D), lambda i: (0, 0))],
        out_shape=[jax.ShapeDtypeStruct((p, D), BF16), jax.ShapeDtypeStruct((p, FFN), BF16),
                   jax.ShapeDtypeStruct((p, D), F32), jax.ShapeDtypeStruct((8, D), F32)],
        compiler_params=_cparams("arbitrary"),
    )(act, wd, h1, *([tgt] * k), gain)


def _mm_tn(pieces, b, name, col_sums=False, comm=None):
    p, n = b.shape
    tk = 256
    nblk = [a.shape[1] // tk for a in pieces]
    offs = [sum(nblk[:q]) for q in range(len(pieces))]
    total = sum(nblk)
    npc = len(pieces)

    def body(*refs):
        a_refs, b_ref, o_ref = refs[:npc], refs[npc], refs[npc + 1]
        i = pl.program_id(0)
        for q, a_ref in enumerate(a_refs):
            @pl.when(jnp.logical_and(i >= offs[q], i < offs[q] + nblk[q]))
            def _(a_ref=a_ref):
                a_v = a_ref[...]
                o_ref[...] = _dot_tn(a_v, b_ref[...]).astype(BF16)
                if col_sums:
                    refs[npc + 2][...] = jnp.sum(a_v.astype(F32), axis=0, keepdims=True)

    def a_spec(q):
        return pl.BlockSpec((p, tk), lambda i: (0, jnp.clip(i - offs[q], 0, nblk[q] - 1)))

    out_specs = [pl.BlockSpec((tk, n), lambda i: (i, 0))]
    out_shape = [jax.ShapeDtypeStruct((total * tk, n), BF16)]
    if col_sums:
        out_specs.append(pl.BlockSpec((1, tk), lambda i: (0, i)))
        out_shape.append(jax.ShapeDtypeStruct((1, total * tk), F32))
    res, sent = _call(
        body, name=name, grid=(total,),
        in_specs=[a_spec(q) for q in range(npc)] + [VM],
        out_specs=out_specs, out_shape=out_shape, args=(*pieces, b), comm=comm)
    res = res if col_sums else res[0]
    return res if comm is None else (res, sent)


def _ffn_act_bwd(u0, dact, dact_dg, dact_dv, fw, act, dffn, comm=None):
    p = u0.shape[0]
    nch = p // BLK
    ncol = FFN // BLK

    def body(g_ref, v_ref, wg_ref, wv_ref, da_ref, lg_ref, lv_ref, act_ref, df_ref,
             dg_ref, dv_ref, gwg_ref, gwv_ref, gbg_ref, gbv_ref, gwd_ref, eg, ev):
        gwd_ref[...] = _dot_tn(act_ref[...], df_ref[...]).astype(BF16)
        eg[p:p + 8, :] = jnp.zeros((8, BLK), F32)
        ev[p:p + 8, :] = jnp.zeros((8, BLK), F32)
        for ci in range(nch):
            r0 = BLK * ci
            d = da_ref[r0:r0 + BLK, :].astype(F32)
            eg[r0:r0 + BLK, :] = d * lg_ref[r0:r0 + BLK, :].astype(F32)
            ev[r0:r0 + BLK, :] = d * lv_ref[r0:r0 + BLK, :].astype(F32)
        def fold(v):
            return jnp.sum(v.reshape(BLK // 8, 8, BLK), axis=0)

        for e_s, x_ref, w_ref, d_ref, gw_ref, gb_ref in ((eg, g_ref, wg_ref, dg_ref, gwg_ref, gbg_ref),
                                                        (ev, v_ref, wv_ref, dv_ref, gwv_ref, gbv_ref)):
            sums = [jnp.zeros((8, BLK), F32) for _ in range(FFN_K + 1)]
            for ci in range(nch):
                r0 = BLK * ci
                es = [e_s[r0 + t:r0 + t + BLK, :] for t in range(FFN_K)]
                du = w_ref[2:3, :] * es[0] + w_ref[1:2, :] * es[1] + w_ref[0:1, :] * es[2]
                if ci == 0:
                    du = jnp.where(_rows(0, BLK) >= PAD, du, 0.0)
                d_ref[r0:r0 + BLK, :] = du.astype(BF16)
                x = x_ref[r0:r0 + BLK, :].astype(F32)
                for j in range(FFN_K):
                    sums[j] = sums[j] + fold(es[FFN_K - 1 - j] * x)
                sums[FFN_K] = sums[FFN_K] + fold(es[0])
            for j in range(FFN_K):
                gw_ref[j:j + 1, :] = jnp.sum(sums[j], axis=0, keepdims=True)
            gb_ref[...] = jnp.sum(sums[FFN_K], axis=0, keepdims=True)

    slab = pl.BlockSpec((p, BLK), lambda j: (0, j))
    wspec = pl.BlockSpec((FFN_K, BLK), lambda j: (0, j))
    bspec = pl.BlockSpec((1, BLK), lambda j: (0, j))
    return _call(
        body, name="ffn_act_bwd", grid=(ncol,),
        in_specs=_ffn_slab_specs(p)[:4] + [slab] * 4 + [VM],
        out_specs=[slab, slab, wspec, wspec, bspec, bspec, pl.BlockSpec((BLK, D), lambda j: (j, 0))],
        out_shape=[jax.ShapeDtypeStruct((p, FFN), BF16)] * 2 + [jax.ShapeDtypeStruct((FFN_K, FFN), F32)] * 2
        + [jax.ShapeDtypeStruct((1, FFN), F32)] * 2 + [jax.ShapeDtypeStruct((FFN, D), BF16)],
        scratch=[pltpu.VMEM((p + 8, BLK), F32)] * 2,
        args=(u0, u0, fw, fw, dact, dact_dg, dact_dv, act, dffn), comm=comm)


def _ffn_in_bwd(dug, duv, w_upt, h1, dy, gain, comm=None):
    p = h1.shape[0]
    tm = _row_tile(p)

    def body(dg_ref, dv_ref, w_ref, h_ref, dy_ref, g_ref, o_ref, acc_ref):
        i = pl.program_id(0)

        @pl.when(i == 0)
        def _():
            acc_ref[...] = jnp.zeros_like(acc_ref)

        dn = _dot(dg_ref[...], w_ref[0:FFN, :]) + _dot(dv_ref[...], w_ref[FFN:2 * FFN, :])
        dh, dg = _rms_bwd(h_ref[...], g_ref[...], dn)
        o_ref[...] = dy_ref[...] + dh
        acc_ref[0:1, :] += dg

    def row(w):
        return pl.BlockSpec((tm, w), lambda i: (i, 0))

    return _call(
        body, name="ffn_in_bwd", grid=(p // tm,),
        in_specs=[row(FFN), row(FFN), VM, row(D), row(D), VM],
        out_specs=[row(D), pl.BlockSpec((8, D), lambda i: (0, 0))],
        out_shape=[jax.ShapeDtypeStruct((p, D), F32), jax.ShapeDtypeStruct((8, D), F32)],
        sem="arbitrary", args=(dug, duv, w_upt, h1, dy, gain), comm=comm)


def _mixer_bwd(dh1, mix, attn, conv, gates, c0, wa, wc, wo, vecs, comm=None):
    p = dh1.shape[0]
    tm = _row_tile(p)

    def body(dh_ref, mix_ref, at_ref, cv_ref, gt_ref, c0_ref, wa_ref, wc_ref, wo_ref, v_ref,
             dmix_ref, dat_ref, dcv_ref, dgt_ref, dao_ref, dc0_ref, acc_ref):
        i = pl.program_id(0)

        @pl.when(i == 0)
        def _():
            acc_ref[...] = jnp.zeros_like(acc_ref)

        dmix, dgp = _rms_bwd(mix_ref[...], v_ref[3:4, :], dh_ref[...])
        dmix = dmix.astype(BF16)
        dmix_ref[...] = dmix
        dmg = _dot_nt(dmix, wo_ref[...])
        sa = jax.nn.sigmoid(gt_ref[:, 0:D].astype(F32))
        sc = jax.nn.sigmoid(gt_ref[:, D:2 * D].astype(F32))
        dat = dmg * sa
        dcv = dmg * sc
        dgt_ref[:, 0:D] = (dmg * at_ref[...].astype(F32) * sa * (1.0 - sa)).astype(BF16)
        dgt_ref[:, D:2 * D] = (dmg * cv_ref[...].astype(F32) * sc * (1.0 - sc)).astype(BF16)
        datb = dat.astype(BF16)
        dcvb = dcv.astype(BF16)
        dat_ref[...] = datb
        dcv_ref[...] = dcvb
        dao_ref[...] = _dot_nt(datb, wa_ref[...]).astype(BF16)
        dc1 = _dot_nt(dcvb, wc_ref[...])
        dc0, dlg, dlb = _lnsilu_bwd(c0_ref[...], v_ref[0:1, :], v_ref[1:2, :], dc1)
        dc0_ref[...] = dc0
        acc_ref[0:1, :] += dgp
        acc_ref[1:2, :] += jnp.sum(dcv, axis=0, keepdims=True)
        acc_ref[2:3, :] += dlg
        acc_ref[3:4, :] += dlb

    def row(w):
        return pl.BlockSpec((tm, w), lambda i: (i, 0))

    return _call(
        body, name="mixer_bwd", grid=(p // tm,),
        in_specs=[row(D), row(D), row(D), row(D), row(2 * D), row(D), VM, VM, VM, VM],
        out_specs=[row(D), row(D), row(D), row(2 * D), row(D), row(D), pl.BlockSpec((8, D), lambda i: (0, 0))],
        out_shape=[jax.ShapeDtypeStruct((p, D), BF16)] * 3 + [jax.ShapeDtypeStruct((p, 2 * D), BF16),
                                                             jax.ShapeDtypeStruct((p, D), BF16),
                                                             jax.ShapeDtypeStruct((p, D), F32),
                                                             jax.ShapeDtypeStruct((8, D), F32)],
        sem="arbitrary", args=(dh1, mix, attn, conv, gates, c0, wa, wc, wo, vecs), comm=comm)


def _conv31_bwd(ag, dc0, w32, tn_pairs, comm=None):
    p = ag.shape[0]
    nch = p // BLK
    npair = len(tn_pairs)

    def body(*refs):
        a_ref, g_ref, dc_ref, w_ref = refs[:4]
        tn_a, tn_b = refs[4:4 + npair], refs[4 + npair:4 + 2 * npair]
        da_ref, dg_ref, gw_ref, gb_ref = refs[4 + 2 * npair:8 + 2 * npair]
        tn_o = refs[8 + 2 * npair:8 + 3 * npair]
        gp, dp = refs[8 + 3 * npair:]
        for ta, tb, to in zip(tn_a, tn_b, tn_o):
            to[...] = _dot_tn(ta[...], tb[...]).astype(BF16)
        gp[0:32, :] = jnp.zeros((32, BLK), F32)
        dp[p:p + 32, :] = jnp.zeros((32, BLK), F32)
        bsum = jnp.zeros((BLK, BLK), F32)
        for ci in range(nch):
            r0 = BLK * ci
            glu = a_ref[r0:r0 + BLK, :].astype(F32) * jax.nn.sigmoid(g_ref[r0:r0 + BLK, :].astype(F32))
            if ci == 0:
                glu = jnp.where(_rows(0, BLK) >= PAD, glu, 0.0)
            gp[32 + r0:32 + r0 + BLK, :] = glu
            d = dc_ref[r0:r0 + BLK, :]
            dp[r0:r0 + BLK, :] = d
            bsum = bsum + d
        gb_ref[...] = jnp.sum(bsum, axis=0, keepdims=True)
        for ci in range(nch):
            r0 = BLK * ci
            acc = jnp.zeros((BLK, BLK), F32)
            for j in range(CONV_K):
                acc = acc + w_ref[j:j + 1, :] * dp[r0 + 30 - j:r0 + 30 - j + BLK, :]
            if ci == 0:
                acc = jnp.where(_rows(0, BLK) >= PAD, acc, 0.0)
            a = a_ref[r0:r0 + BLK, :].astype(F32)
            sg = jax.nn.sigmoid(g_ref[r0:r0 + BLK, :].astype(F32))
            da_ref[r0:r0 + BLK, :] = (acc * sg).astype(BF16)
            dg_ref[r0:r0 + BLK, :] = (acc * a * sg * (1.0 - sg)).astype(BF16)
        sub = BLK // 2
        accs = [jnp.zeros((8, BLK), F32) for _ in range(CONV_K)]
        for r0 in range(0, p, sub):
            d = dp[r0:r0 + sub, :]
            for j in range(CONV_K):
                prod = d * gp[r0 + j + 2:r0 + j + 2 + sub, :]
                accs[j] = accs[j] + jnp.sum(prod.reshape(sub // 8, 8, BLK), axis=0)
        for j in range(CONV_K):
            gw_ref[j:j + 1, :] = jnp.sum(accs[j], axis=0, keepdims=True)
        gw_ref[CONV_K:32, :] = jnp.zeros((32 - CONV_K, BLK), F32)

    slab = pl.BlockSpec((p, BLK), lambda j: (0, j))
    return _call(
        body, name="conv31_bwd", grid=(D // BLK,),
        in_specs=[slab, pl.BlockSpec((p, BLK), lambda j: (0, 8 + j)), slab, pl.BlockSpec((32, BLK), lambda j: (0, j))]
        + [slab] * npair + [VM] * npair,
        out_specs=[slab, slab, pl.BlockSpec((32, BLK), lambda j: (0, j)), pl.BlockSpec((1, BLK), lambda j: (0, j))]
        + [pl.BlockSpec((BLK, D), lambda j: (j, 0))] * npair,
        out_shape=[jax.ShapeDtypeStruct((p, D), BF16)] * 2 + [jax.ShapeDtypeStruct((32, D), F32),
                                                             jax.ShapeDtypeStruct((1, D), F32)]
        + [jax.ShapeDtypeStruct((D, D), BF16)] * npair,
        scratch=[pltpu.VMEM((p + 32, BLK), F32)] * 2,
        args=(ag, ag, dc0, w32, *[a for a, _ in tn_pairs], *[b for _, b in tn_pairs]), comm=comm)


def _attn_bwd(q, kv, dao, sinks, tabs, comm=None):
    p = q.shape[0]
    nb = p // BLK

    def body(q_ref, km_ref, kp_ref, kc_ref, do_ref, sink_ref, t_ref, dqkv_ref, dsink_ref, carry, macc):
        i = pl.program_id(0)
        n = nb - 1 - i

        @pl.when(i == 0)
        def _():
            carry[...] = jnp.zeros_like(carry)
            macc[...] = jnp.zeros_like(macc)
            dsink_ref[...] = jnp.zeros_like(dsink_ref)

        lo = lax.broadcasted_iota(jnp.int32, (BLK, BLK), 1) < HEAD_DIM
        lane8 = lax.broadcasted_iota(jnp.int32, (8, BLK), 1)
        c, s1, s2 = t_ref[:, 0:128], -t_ref[:, 128:256], -t_ref[:, 256:384]
        dk = jnp.zeros((N_KEY, BLK), F32)
        dv = jnp.zeros((N_KEY, BLK), F32)
        for h in range(2):
            qs, k2, v2, bias, lok = _attn_setup(n, h, q_ref, km_ref, kp_ref, kc_ref)
            dos = _stack_heads(do_ref, h, lo)
            st = _dot_nt(k2, qs)
            dpt = _dot_nt(v2, dos)
            p_parts, ds_parts = [], []
            for g in range(8):
                cols = slice(BLK * g, BLK * (g + 1))
                pn, ps = _attn_head(st[:, cols], bias, sink_ref[0, 8 * h + g])
                dp = dpt[:, cols]
                delta = jnp.sum(pn * dp, axis=0, keepdims=True)
                ds_parts.append((pn * (dp - delta)).astype(BF16))
                p_parts.append(pn.astype(BF16))
                dsk = -jnp.sum(ps * delta, axis=1, keepdims=True)
                dsink_ref[...] += jnp.where(lane8 == 8 * h + g, dsk, 0.0)
            dst = jnp.concatenate(ds_parts, axis=1)
            pt = jnp.concatenate(p_parts, axis=1)
            dq = _dot_tn(dst, k2)
            for jp in range(4):
                lo_c = BLK * (4 * h + jp)
                dqkv_ref[:, lo_c:lo_c + BLK] = (_rope(_unstack_heads(dq, jp, lo), c, s1, s2) * SCALE).astype(BF16)
            dk2 = _dot(dst, qs)
            dv2 = _dot(pt, dos)
            dk2 = dk2 + pltpu.roll(dk2, HEAD_DIM, 1)
            dv2 = dv2 + pltpu.roll(dv2, HEAD_DIM, 1)
            own = lok if h == 0 else jnp.logical_not(lok)
            dk = jnp.where(own, dk2, dk)
            dv = jnp.where(own, dv2, dv)
        macc[:, 0:BLK] += dk[2 * BLK:N_KEY]
        macc[:, BLK:2 * BLK] += dv[2 * BLK:N_KEY]
        last = (n == 0).astype(F32)
        zpad = jnp.zeros((PAD, BLK), F32)
        dk_c = dk[BLK:2 * BLK] + carry[:, 0:BLK] + last * jnp.concatenate([zpad, macc[:, 0:BLK]], axis=0)
        dv_c = dv[BLK:2 * BLK] + carry[:, BLK:2 * BLK] + last * jnp.concatenate([zpad, macc[:, BLK:2 * BLK]], axis=0)
        carry[:, 0:BLK] = dk[0:BLK]
        carry[:, BLK:2 * BLK] = dv[0:BLK]
        dqkv_ref[:, D:D + BLK] = _rope(dk_c, c, s1, s2).astype(BF16)
        dqkv_ref[:, D + BLK:D + 2 * BLK] = dv_c.astype(BF16)

    def rev(w):
        return pl.BlockSpec((BLK, w), lambda i: (nb - 1 - i, 0))

    return _call(
        body, name="attn_bwd", grid=(nb,),
        in_specs=[rev(D),
                  pl.BlockSpec((BLK, 256), lambda i: (0, 0)),
                  pl.BlockSpec((BLK, 256), lambda i: (jnp.maximum(nb - 2 - i, 0), 0)),
                  rev(256), rev(D),
                  pl.BlockSpec(memory_space=pltpu.SMEM), rev(384)],
        out_specs=[rev(QKV_W), pl.BlockSpec((8, BLK), lambda i: (0, 0))],
        out_shape=[jax.ShapeDtypeStruct((p, QKV_W), BF16), jax.ShapeDtypeStruct((8, BLK), F32)],
        scratch=[pltpu.VMEM((BLK, 256), F32), pltpu.VMEM((N_META, 256), F32)], sem="arbitrary",
        args=(q, kv, kv, kv, dao, sinks, tabs), comm=comm)


def _in_bwd(dqkv, da, dg, dgt, w_int, h0p, dh1, gain, comm=None):
    p = h0p.shape[0]
    tm = _row_tile(p)
    nt = p // tm
    first_rows = tm - BLK

    def body(dq_ref, da_ref, dg_ref, dt_ref, w_ref, h_ref, dh_ref, g_ref, gx_ref, dm_ref, acc_ref, buf, sems):
        i = pl.program_id(0)
        slot = i % 2

        @pl.when(i == 0)
        def _():
            acc_ref[...] = jnp.zeros_like(acc_ref)

        dn = (_dot(dq_ref[...], w_ref[0:QKV_W, :]) + _dot(da_ref[...], w_ref[QKV_W:QKV_W + D, :])
              + _dot(dg_ref[...], w_ref[QKV_W + D:QKV_W + 2 * D, :]) + _dot(dt_ref[...], w_ref[QKV_W + 2 * D:IN_W, :]))
        dh, dgain = _rms_bwd(h_ref[...], g_ref[...], dn)
        dh0 = dh_ref[...] + dh
        acc_ref[0:1, :] += dgain
        buf[slot] = dh0

        @pl.when(i == 0)
        def _():
            dm_ref[...] = dh0[PAD:BLK]

        def first_copy():
            return pltpu.make_async_copy(buf.at[0, pl.ds(BLK, first_rows), :], gx_ref.at[pl.ds(0, first_rows), :], sems.at[0])

        def tile_copy(j, s):
            return pltpu.make_async_copy(buf.at[s], gx_ref.at[pl.ds(pl.multiple_of(j * tm - BLK, BLK), tm), :], sems.at[s])

        if first_rows:
            @pl.when(i == 1)
            def _():
                first_copy().wait()

        @pl.when(i >= 2)
        def _():
            tile_copy(i - 1, 1 - slot).wait()

        if first_rows:
            @pl.when(i == 0)
            def _():
                first_copy().start()

        @pl.when(i > 0)
        def _():
            tile_copy(i, slot).start()

        @pl.when(i == nt - 1)
        def _():
            tile_copy(i, slot).wait()

    def row(w):
        return pl.BlockSpec((tm, w), lambda i: (i, 0))

    return _call(
        body, name="in_bwd", grid=(nt,),
        in_specs=[row(QKV_W), row(D), row(D), row(2 * D), VM, row(D), row(D), VM],
        out_specs=[ANY, pl.BlockSpec((N_META, D), lambda i: (0, 0)), pl.BlockSpec((8, D), lambda i: (0, 0))],
        out_shape=[jax.ShapeDtypeStruct((p - BLK, D), F32), jax.ShapeDtypeStruct((N_META, D), F32),
                   jax.ShapeDtypeStruct((8, D), F32)],
        scratch=[pltpu.VMEM((2, tm, D), F32), pltpu.SemaphoreType.DMA((2,))],
        sem="arbitrary", args=(dqkv, da, dg, dgt, w_int, h0p, dh1, gain), comm=comm)


def _sum_slots(slots, name):
    r = slots.shape[0] // N_DEV
    cols = slots.shape[1]
    tr = r if r <= 352 else (r // 2 if (r // 2) % 16 == 0 else r // 3)
    steps = r // tr

    def body(*refs):
        acc = refs[0][...].astype(F32)
        for s in range(1, N_DEV):
            acc = acc + refs[s][...].astype(F32)
        refs[N_DEV][...] = acc

    return pl.pallas_call(
        body, name=name, grid=(steps,),
        in_specs=[pl.BlockSpec((tr, cols), functools.partial(lambda i, s: (s * steps + i, 0), s=s)) for s in range(N_DEV)],
        out_specs=pl.BlockSpec((tr, cols), lambda i: (i, 0)),
        out_shape=jax.ShapeDtypeStruct((r, cols), F32),
        compiler_params=_cparams("parallel"),
    )(*([slots] * N_DEV))


def _adamw_math(w, g, m, v):
    m_n = ADAM_B1 * m + (1.0 - ADAM_B1) * g
    v_n = ADAM_B2 * v + (1.0 - ADAM_B2) * jnp.square(g)
    m_hat = m_n / (1.0 - ADAM_B1 ** ADAM_STEP)
    v_hat = v_n / (1.0 - ADAM_B2 ** ADAM_STEP)
    return -ADAM_LR * (m_hat / (jnp.sqrt(v_hat) + ADAM_EPS) + ADAM_WD * w), m_n, v_n


def _sum_adamw(parts, w, m, v, name, nslots=N_DEV):
    r, cols = w.shape
    rs = r // len(parts)
    tr = rs if rs <= 352 else (rs // 2 if (rs // 2) % 16 == 0 else rs // 3)
    steps = rs // tr

    def body(*refs):
        w_ref, m_ref, v_ref, g_ref, d_ref, nm_ref, nv_ref = refs[nslots * len(parts):]
        i = pl.program_id(0)
        for q in range(len(parts)):
            @pl.when(i // steps == q)
            def _(q=q):
                g = refs[nslots * q][...].astype(F32)
                for s in range(1, nslots):
                    g = g + refs[nslots * q + s][...].astype(F32)
                g_ref[...] = g
                d_ref[...], nm_ref[...], nv_ref[...] = _adamw_math(w_ref[...], g, m_ref[...], v_ref[...])

    def slot_spec(q, s):
        return pl.BlockSpec((tr, cols), lambda i: (s * steps + jnp.clip(i - q * steps, 0, steps - 1), 0))

    spec = pl.BlockSpec((tr, cols), lambda i: (i, 0))
    return pl.pallas_call(
        body, name=name, grid=(steps * len(parts),),
        in_specs=[slot_spec(q, s) for q in range(len(parts)) for s in range(nslots)] + [spec] * 3,
        out_specs=[spec] * 4, out_shape=[jax.ShapeDtypeStruct((r, cols), F32)] * 4,
        compiler_params=_cparams("parallel"),
    )(*[a for a in parts for _ in range(nslots)], w, m, v)


def _adamw_many(ws, gs, ms, vs, name):
    n = len(ws)

    def body(*refs):
        w, g, m, v = refs[0:n], refs[n:2 * n], refs[2 * n:3 * n], refs[3 * n:4 * n]
        d, nm, nv = refs[4 * n:5 * n], refs[5 * n:6 * n], refs[6 * n:7 * n]
        for k in range(n):
            d[k][...], nm[k][...], nv[k][...] = _adamw_math(w[k][...], g[k][...], m[k][...], v[k][...])

    outs = pl.pallas_call(
        body, name=name, in_specs=[VM] * (4 * n), out_specs=[VM] * (3 * n),
        out_shape=[jax.ShapeDtypeStruct(a.shape, F32) for a in ws] * 3,
    )(*ws, *gs, *ms, *vs)
    return outs[0:n], outs[n:2 * n], outs[2 * n:3 * n]


def _rope_tables(p):
    half = ROT_DIM // 2
    lane = jnp.arange(BLK)
    seg = (lane % HEAD_DIM) // half
    inv_freq = ROPE_THETA ** (-(lane % half).astype(F32) * 2.0 / ROT_DIM)
    pos = (jnp.arange(p) - PAD).astype(F32)
    ang = pos[:, None] * inv_freq[None, :]
    cos = jnp.cos(ang)
    sin = jnp.sin(ang)
    c = jnp.where(seg[None, :] < 2, cos, 1.0)
    s1 = jnp.where(seg[None, :] == 0, -sin, 0.0)
    s2 = jnp.where(seg[None, :] == 1, sin, 0.0)
    return jnp.concatenate([c, s1, s2], axis=1).astype(F32)


def _flat_pack(parts, rows):
    flat = jnp.concatenate([a.reshape(-1).astype(F32) for a in parts])
    return jnp.pad(flat, (0, rows * D - flat.shape[0])).reshape(rows, D)


def _flat_unpack(pack, shapes):
    flat = pack.reshape(-1)
    out, off = [], 0
    for s in shapes:
        size = 1
        for e in s:
            size *= e
        out.append(flat[off:off + size].reshape(s))
        off += size
    return out


def kernel(x, meta_tokens, norm_pre_mix, norm_post_mix, w_in, b_in, attn_sinks, w_attn_proj, conv_dw_w, conv_dw_b, conv_ln_g, conv_ln_b, w_conv_proj, b_conv_proj, w_out, norm_pre_ffn, norm_post_ffn, w_up, ffn_dw_w, ffn_dw_b, w_down, loss_target, m_meta_tokens, m_norm_pre_mix, m_norm_post_mix, m_w_in, m_b_in, m_attn_sinks, m_w_attn_proj, m_conv_dw_w, m_conv_dw_b, m_conv_ln_g, m_conv_ln_b, m_w_conv_proj, m_b_conv_proj, m_w_out, m_norm_pre_ffn, m_norm_post_ffn, m_w_up, m_ffn_dw_w, m_ffn_dw_b, m_w_down, v_meta_tokens, v_norm_pre_mix, v_norm_post_mix, v_w_in, v_b_in, v_attn_sinks, v_w_attn_proj, v_conv_dw_w, v_conv_dw_b, v_conv_ln_g, v_conv_ln_b, v_w_conv_proj, v_b_conv_proj, v_w_out, v_norm_pre_ffn, v_norm_post_ffn, v_w_up, v_ffn_dw_w, v_ffn_dw_b, v_w_down):
    seq = x.shape[1]
    p = seq + BLK
    me = 4 * lax.axis_index("x") + 2 * lax.axis_index("y") + lax.axis_index("c")
    in_cols = w_in.shape[2]
    up_cols = w_up.shape[2]

    small = jnp.zeros((56, up_cols), F32)
    small = small.at[0:N_META, 0:BLK].set(meta_tokens)
    small = small.at[16:16 + CONV_K, 0:BLK].set(conv_dw_w[0])
    small = small.at[48:48 + FFN_K, :].set(ffn_dw_w[0])
    w_int, small_all = _exchange(_Both(_GatherRelay(w_in[0].T.astype(BF16)), _Gather([small])), "gather_w_in")
    small_all = small_all.reshape(N_DEV, 56, up_cols)
    meta_full = small_all[:, 0:N_META, 0:BLK].transpose(1, 0, 2).reshape(N_META, D)
    cdw = small_all[:, 16:16 + CONV_K, 0:BLK].transpose(1, 0, 2).reshape(CONV_K, D)
    cdw32 = jnp.pad(cdw, ((0, 32 - CONV_K), (0, 0)))
    fdw = small_all[:, 48:48 + FFN_K, :].transpose(1, 0, 2).reshape(FFN_K, 2 * FFN)

    tabs = _rope_tables(p)
    vecs = jnp.concatenate([conv_ln_g, conv_ln_b, b_conv_proj, norm_post_mix, norm_pre_ffn, jnp.zeros((3, D), F32)], axis=0)

    (h0p, n1, q, kv, ag, gates), (wa, wc, wo) = _in_proj(
        x[0], meta_full, norm_pre_mix, w_int, b_in, tabs,
        comm=_Gather([w_attn_proj[0].astype(BF16), w_conv_proj[0].astype(BF16), w_out[0].astype(BF16)]))
    (ao,), (w_upt,) = _attn_fwd(q, kv, attn_sinks, comm=_Gather([w_up[0].T.astype(BF16)]))
    (c0,), (wd,) = _conv31_fwd(ag, cdw32, conv_dw_b, comm=_Gather([w_down[0].astype(BF16)]))
    c1, attn, conv, merged, mix, h1, n2 = _mixer_fwd(ao, c0, gates, h0p, wa, wc, wo, vecs)
    u0 = _mm_nt(n2, w_upt, "ffn_up")
    act, dact_dv, dact_dg = _ffn_act(u0, fdw, ffn_dw_b)
    dffn, dact, dy, acc_f = _ffn_down_loss(act, wd, h1, loss_target[0], norm_post_ffn)

    (dug, duv, gfw_g, gfw_v, gfb_g, gfb_v, g_wd), _ = _ffn_act_bwd(u0, dact, dact_dg, dact_dv, fdw, act, dffn)
    g_wupt, (s_wd0,) = _mm_tn([dug, duv], n2, "grad_w_up", comm=_Scatter([g_wd], 0, 2))
    (dh1, acc_u), (s_wd1,) = _ffn_in_bwd(dug, duv, w_upt, h1, dy, norm_pre_ffn, comm=_Scatter([g_wd], 1, 2))
    (dmix, dat, dcv, dgt, dao, dc0, acc_m), (s_wup0,) = _mixer_bwd(
        dh1, mix, attn, conv, gates, c0, wa, wc, wo, vecs, comm=_Scatter([g_wupt], 0, 4))
    (da, dg, g_cdw, g_cdb, g_wo, g_wa, g_wc), (s_wup1, s_wup2, s_wup3) = _conv31_bwd(
        ag, dc0, cdw32, [(merged, dmix), (ao, dat), (c1, dcv)],
        comm=_Both(_Both(_Scatter([g_wupt], 1, 4), _Scatter([g_wupt], 2, 4)), _Scatter([g_wupt], 3, 4)))
    (dqkv, dsink), (s_wa, s_wc, s_wo) = _attn_bwd(q, kv, dao, attn_sinks, tabs, comm=_Scatter([g_wa, g_wc, g_wo]))
    loss_row = jnp.sum(acc_f[1:2, :], axis=1, keepdims=True)
    early = [loss_row, acc_m[0:1], dsink[0:1, 0:16], g_cdw[0:CONV_K], g_cdb,
             acc_m[2:3], acc_m[3:4], acc_m[1:2], acc_u[0:1], acc_f[0:1],
             jnp.concatenate([gfw_g, gfw_v], axis=1), jnp.concatenate([gfb_g, gfb_v], axis=1)]
    (g_wint, g_bin), (gathered_early,) = _mm_tn([dqkv, da, dg, dgt], n1, "grad_w_in", col_sums=True,
                                                comm=_Gather([_flat_pack(early, 64)]))
    (from_sibling,) = _exchange(_SiblingSwap(g_wint), "swap_w_in")
    (grad_x2d, dmeta, acc_i), (s_win,) = _in_bwd(dqkv, da, dg, dgt, w_int, h0p, dh1, norm_pre_mix,
                                                 comm=_ChipScatter(_pair_add(g_wint, from_sibling)))

    big = []
    for nm, parts, nslots, w, m, v, tr in (
            ("w_in", [s_win], N_CHIP, w_in, m_w_in, v_w_in, True), ("w_up", [s_wup0, s_wup1, s_wup2, s_wup3], N_DEV, w_up, m_w_up, v_w_up, True),
            ("w_attn_proj", [s_wa], N_DEV, w_attn_proj, m_w_attn_proj, v_w_attn_proj, False),
            ("w_conv_proj", [s_wc], N_DEV, w_conv_proj, m_w_conv_proj, v_w_conv_proj, False),
            ("w_out", [s_wo], N_DEV, w_out, m_w_out, v_w_out, False),
            ("w_down", [s_wd0, s_wd1], N_DEV, w_down, m_w_down, v_w_down, False)):
        ins = [a[0].T if tr else a[0] for a in (w, m, v)]
        big.append(tuple((o.T if tr else o)[None] for o in _sum_adamw(parts, *ins, "update_" + nm, nslots)))

    late = [dmeta, acc_i[0:1], g_bin]
    (gathered_late,) = _exchange(_Gather([_flat_pack(late, 24)]), "gather_small_grads")
    g_meta, g_npm, g_bi = _flat_unpack(_sum_slots(gathered_late, "sum_late_grads"), [a.shape for a in late])
    tot = _flat_unpack(_sum_slots(gathered_early, "sum_small_grads"), [a.shape for a in early])
    (loss, g_nqm, g_sk, g_cw, g_cb, g_lg, g_lb, g_bc, g_npf, g_nqf, g_fw, g_fb) = tot
    loss = loss.reshape(())
    g_meta = lax.dynamic_slice_in_dim(g_meta, me * BLK, BLK, axis=1)
    g_cw = lax.dynamic_slice_in_dim(g_cw, me * BLK, BLK, axis=1)[None]
    g_fw = lax.dynamic_slice_in_dim(g_fw, me * up_cols, up_cols, axis=1)[None]

    sm_w = [meta_tokens, norm_pre_mix, norm_post_mix, b_in, attn_sinks, conv_dw_w, conv_dw_b, conv_ln_g, conv_ln_b,
            b_conv_proj, norm_pre_ffn, norm_post_ffn, ffn_dw_w, ffn_dw_b]
    sm_g = [g_meta, g_npm, g_nqm, g_bi, g_sk, g_cw, g_cb, g_lg, g_lb, g_bc, g_npf, g_nqf, g_fw, g_fb]
    sm_m = [m_meta_tokens, m_norm_pre_mix, m_norm_post_mix, m_b_in, m_attn_sinks, m_conv_dw_w, m_conv_dw_b, m_conv_ln_g,
            m_conv_ln_b, m_b_conv_proj, m_norm_pre_ffn, m_norm_post_ffn, m_ffn_dw_w, m_ffn_dw_b]
    sm_v = [v_meta_tokens, v_norm_pre_mix, v_norm_post_mix, v_b_in, v_attn_sinks, v_conv_dw_w, v_conv_dw_b, v_conv_ln_g,
            v_conv_ln_b, v_b_conv_proj, v_norm_pre_ffn, v_norm_post_ffn, v_ffn_dw_w, v_ffn_dw_b]
    swap = lambda a: jnp.transpose(a, (1, 0, 2)) if a.ndim == 3 else a
    sm_d, sm_nm, sm_nv = ([swap(o) for o in outs] for outs in
                          _adamw_many(*([swap(a) for a in group] for group in (sm_w, sm_g, sm_m, sm_v)), "adamw_small"))

    order = ["meta_tokens", "norm_pre_mix", "norm_post_mix", "w_in", "b_in", "attn_sinks", "w_attn_proj", "conv_dw_w",
             "conv_dw_b", "conv_ln_g", "conv_ln_b", "w_conv_proj", "b_conv_proj", "w_out", "norm_pre_ffn", "norm_post_ffn",
             "w_up", "ffn_dw_w", "ffn_dw_b", "w_down"]
    small_names = ["meta_tokens", "norm_pre_mix", "norm_post_mix", "b_in", "attn_sinks", "conv_dw_w", "conv_dw_b", "conv_ln_g",
                   "conv_ln_b", "b_conv_proj", "norm_pre_ffn", "norm_post_ffn", "ffn_dw_w", "ffn_dw_b"]
    big_names = ["w_in", "w_up", "w_attn_proj", "w_conv_proj", "w_out", "w_down"]
    table = {}
    for k, nm in enumerate(small_names):
        table[nm] = (sm_g[k], sm_d[k], sm_nm[k], sm_nv[k])
    for k, nm in enumerate(big_names):
        table[nm] = big[k]
    grad_x = grad_x2d[None]
    outs = [loss, grad_x]
    for field in range(4):
        outs += [table[nm][field] for nm in order]
    return tuple(outs)
```

```python
import functools

import jax
import jax.numpy as jnp
from jax import lax
from jax.experimental import pallas as pl
from jax.experimental.pallas import tpu as pltpu

F32 = jnp.float32
BF16 = jnp.bfloat16
MESH = pl.DeviceIdType.MESH

D = 1024
HEAD_DIM = 64
N_META = 16
BLK = 128
PAD = BLK - N_META
CONV_K = 31
FFN = 2816
FFN_K = 3
QKV_W = 1280
IN_W = 5376
ROT_DIM = 16
ROPE_THETA = 500000.0
RMS_EPS = 1e-6
LN_EPS = 1e-5
NEG_INF = -1e30
SCALE = HEAD_DIM ** -0.5
N_DEV = 8

ADAM_LR = 0.001
ADAM_B1 = 0.9
ADAM_B2 = 0.999
ADAM_EPS = 1e-08
ADAM_WD = 0.01
ADAM_STEP = 10

VMEM_BYTES_V7X = 64 * 1024 * 1024
VMEM_LIMIT = VMEM_BYTES_V7X - 8 * 1024 * 1024

NT = (((1,), (1,)), ((), ()))
TN = (((0,), (0,)), ((), ()))
VM = pl.BlockSpec(memory_space=pltpu.VMEM)
ANY = pl.BlockSpec(memory_space=pl.ANY)


def _cparams(*sem):
    return pltpu.CompilerParams(dimension_semantics=sem or None, vmem_limit_bytes=VMEM_LIMIT)


def _row_tile(p):
    return 384 if p % 384 == 0 else 128


def _dot(a, b):
    return jnp.dot(a, b, preferred_element_type=F32)


def _dot_nt(a, b):
    return lax.dot_general(a, b, NT, preferred_element_type=F32)


def _dot_tn(a, b):
    return lax.dot_general(a, b, TN, preferred_element_type=F32)


def _rms(x, g):
    return x * lax.rsqrt(jnp.mean(x * x, axis=-1, keepdims=True) + RMS_EPS) * g


def _lnsilu(x, g, b):
    mu = jnp.mean(x, axis=-1, keepdims=True)
    var = jnp.mean(jnp.square(x - mu), axis=-1, keepdims=True)
    z = (x - mu) * lax.rsqrt(var + LN_EPS) * g + b
    return z * jax.nn.sigmoid(z)


def _rms_bwd(x, g, dy):
    r = lax.rsqrt(jnp.mean(x * x, axis=-1, keepdims=True) + RMS_EPS)
    xn = x * r
    u = dy * g
    dg = jnp.sum(dy * xn, axis=0, keepdims=True)
    dx = r * (u - xn * jnp.mean(u * xn, axis=-1, keepdims=True))
    return dx, dg


def _lnsilu_bwd(x, g, b, dout):
    mu = jnp.mean(x, axis=-1, keepdims=True)
    xc = x - mu
    rs = lax.rsqrt(jnp.mean(xc * xc, axis=-1, keepdims=True) + LN_EPS)
    yh = xc * rs
    z = yh * g + b
    sg = jax.nn.sigmoid(z)
    dz = dout * (sg * (1.0 + z * (1.0 - sg)))
    dg = jnp.sum(dz * yh, axis=0, keepdims=True)
    db = jnp.sum(dz, axis=0, keepdims=True)
    dyh = dz * g
    dx = rs * (dyh - jnp.mean(dyh, axis=-1, keepdims=True) - yh * jnp.mean(dyh * yh, axis=-1, keepdims=True))
    return dx, dg, db


def _rope(v, c, s1, s2):
    return v * c + pltpu.roll(v, BLK - 8, 1) * s1 + pltpu.roll(v, 8, 1) * s2


def _rows(i, tm):
    return i * tm + lax.broadcasted_iota(jnp.int32, (tm, 1), 0)


def _place():
    return lax.axis_index("x"), lax.axis_index("y"), lax.axis_index("c")


def _blk(ref, idx, r, dtype):
    return ref.at[pl.ds(pl.multiple_of(idx * r, 16 if dtype == BF16 else 8), r), :]


class _Gather:
    def __init__(self, arrs):
        self.ins = list(arrs)
        n = len(arrs)
        self.out_shape = [jax.ShapeDtypeStruct((N_DEV * a.shape[0], a.shape[1]), a.dtype) for a in arrs]
        self.scratch = [pltpu.SemaphoreType.DMA((n, 7)), pltpu.SemaphoreType.DMA((n, 7)), pltpu.SemaphoreType.DMA((n,))]

    def _parts(self, ins, outs, sems):
        send_sems, recv_sems, local_sems = sems
        n = len(ins)
        x, y, c = _place()
        me, sibling = (x, y, c), (x, y, 1 - c)
        chips = [(1 - x, y), (x, 1 - y), (1 - x, 1 - y)]

        def rows(a, p):
            return _blk(outs[a], 4 * p[0] + 2 * p[1] + p[2], self.ins[a].shape[0], self.ins[a].dtype)

        def copy(a, k, block, to, src=None):
            return pltpu.make_async_remote_copy(
                src_ref=rows(a, block) if src is None else src, dst_ref=rows(a, block),
                send_sem=send_sems.at[a, k], recv_sem=recv_sems.at[a, k], device_id=to, device_id_type=MESH)

        mine = [pltpu.make_async_copy(ins[a], rows(a, me), local_sems.at[a]) for a in range(n)]
        first = []
        for a in range(n):
            first.append(copy(a, 0, me, sibling, src=ins[a]))
            first += [copy(a, 1 + j, me, (*chip, c), src=ins[a]) for j, chip in enumerate(chips)]
        return n, c, me, sibling, chips, copy, mine, first

    def start(self, ins, outs, sems):
        *_, mine, first = self._parts(ins, outs, sems)
        for cp in mine + first:
            cp.start()

    def middle(self, ins, outs, sems):
        n, c, me, sibling, chips, copy, _, _ = self._parts(ins, outs, sems)
        for j, chip in enumerate(chips):
            for a in range(n):
                copy(a, 1 + j, (*chip, c), me).wait_recv()
                copy(a, 4 + j, (*chip, c), sibling).start()

    def finish(self, ins, outs, sems, middle_done=False):
        if not middle_done:
            self.middle(ins, outs, sems)
        n, c, me, sibling, chips, copy, mine, first = self._parts(ins, outs, sems)
        passed = [copy(a, 4 + j, (*chip, c), sibling) for j, chip in enumerate(chips) for a in range(n)]
        for a in range(n):
            copy(a, 0, sibling, me).wait_recv()
            for j, chip in enumerate(chips):
                copy(a, 4 + j, (*chip, 1 - c), me).wait_recv()
        for cp in first + passed:
            cp.wait_send()
        for cp in mine:
            cp.wait()


class _GatherRelay:
    N_COPY = 13

    def __init__(self, arr):
        self.ins = [arr]
        self.r = arr.shape[0]
        self.out_shape = [jax.ShapeDtypeStruct((N_DEV * self.r, arr.shape[1]), arr.dtype)]
        self.scratch = [pltpu.SemaphoreType.DMA((self.N_COPY,)), pltpu.SemaphoreType.DMA((self.N_COPY,)),
                        pltpu.SemaphoreType.DMA]

    def _parts(self, ins, outs, sems):
        send_sems, recv_sems, local_sem = sems
        x, y, c = _place()
        r, half = self.r, self.r // 2
        out = outs[0]
        me, sib, xn, yn, dg = (x, y, c), (x, y, 1 - c), (1 - x, y, c), (x, 1 - y, c), (1 - x, 1 - y, c)
        sx, sy, sd = (1 - x, y, 1 - c), (x, 1 - y, 1 - c), (1 - x, 1 - y, 1 - c)
        lo, hi = (0, half), (half, half)

        def rows(p, part=(0, r)):
            return out.at[pl.ds(pl.multiple_of((4 * p[0] + 2 * p[1] + p[2]) * r + part[0], 16), part[1]), :]

        def own(part):
            return ins[0].at[pl.ds(part[0], part[1]), :]

        def copy(k, dev_rows, to, src=None):
            return pltpu.make_async_remote_copy(
                src_ref=dev_rows if src is None else src, dst_ref=dev_rows,
                send_sem=send_sems.at[k], recv_sem=recv_sems.at[k], device_id=to, device_id_type=MESH)

        mine = pltpu.make_async_copy(ins[0], rows(me), local_sem)
        first = [copy(0, rows(me), sib, src=ins[0]),
                 copy(1, rows(me, lo), xn, src=own(lo)), copy(3, rows(me, hi), yn, src=own(hi)),
                 copy(2, rows(me, hi), xn, src=own(hi)), copy(4, rows(me, lo), yn, src=own(lo))]
        arrive = {0: rows(sib), 1: rows(xn, lo), 2: rows(xn, hi), 3: rows(yn, hi), 4: rows(yn, lo),
                  5: rows(dg, lo), 6: rows(dg, hi), 7: rows(sx, lo), 8: rows(sx, hi), 9: rows(sy, hi),
                  10: rows(sy, lo), 11: rows(sd, lo), 12: rows(sd, hi)}
        relay = {1: [(5, rows(xn, lo), yn), (7, rows(xn, lo), sib)], 3: [(6, rows(yn, hi), xn), (9, rows(yn, hi), sib)],
                 2: [(8, rows(xn, hi), sib)], 4: [(10, rows(yn, lo), sib)],
                 5: [(11, rows(dg, lo), sib)], 6: [(12, rows(dg, hi), sib)]}
        return copy, mine, first, arrive, relay, me

    def start(self, ins, outs, sems):
        _, mine, first, _, _, _ = self._parts(ins, outs, sems)
        for cp in [mine] + first:
            cp.start()

    def finish(self, ins, outs, sems):
        copy, mine, first, arrive, relay, me = self._parts(ins, outs, sems)
        passed = []
        for k in (1, 3, 2, 4, 5, 6):
            copy(k, arrive[k], me).wait_recv()
            for k2, dev_rows, to in relay[k]:
                fwd = copy(k2, dev_rows, to)
                fwd.start()
                passed.append(fwd)
        for k in (0, 7, 8, 9, 10, 11, 12):
            copy(k, arrive[k], me).wait_recv()
        for cp in first + passed:
            cp.wait_send()
        mine.wait()


FLIPS = [(0, 0, 1), (1, 0, 0), (0, 1, 0), (1, 1, 0), (1, 0, 1), (0, 1, 1), (1, 1, 1)]


class _Scatter:
    def __init__(self, arrs, part=0, nparts=1):
        self.ins = list(arrs)
        self.part, self.nparts = part, nparts
        n = len(arrs)
        self.out_shape = [jax.ShapeDtypeStruct((a.shape[0] // nparts, a.shape[1]), a.dtype) for a in arrs]
        self.scratch = [pltpu.SemaphoreType.DMA((n, 7)), pltpu.SemaphoreType.DMA((n, 7)), pltpu.SemaphoreType.DMA((n,))]

    def _parts(self, ins, outs, sems):
        send_sems, recv_sems, local_sems = sems
        n = len(ins)
        x, y, c = _place()
        me = 4 * x + 2 * y + c

        def flip(v, f):
            return 1 - v if f else v

        def src(a, idx):
            r = self.ins[a].shape[0] // N_DEV
            rs = r // self.nparts
            return ins[a].at[pl.ds(pl.multiple_of(idx * r + self.part * rs, 16), rs), :]

        def dst(a, idx):
            rs = self.ins[a].shape[0] // N_DEV // self.nparts
            return outs[a].at[pl.ds(pl.multiple_of(idx * rs, 16), rs), :]

        mine = [pltpu.make_async_copy(src(a, me), dst(a, me), local_sems.at[a]) for a in range(n)]
        sends, recvs = [], []
        for k, f in enumerate(FLIPS):
            peer = (flip(x, f[0]), flip(y, f[1]), flip(c, f[2]))
            pidx = 4 * peer[0] + 2 * peer[1] + peer[2]
            for a in range(n):
                sends.append(pltpu.make_async_remote_copy(
                    src_ref=src(a, pidx), dst_ref=dst(a, me),
                    send_sem=send_sems.at[a, k], recv_sem=recv_sems.at[a, k], device_id=peer, device_id_type=MESH))
                recvs.append(functools.partial(
                    pltpu.make_async_remote_copy,
                    src_ref=src(a, pidx), dst_ref=dst(a, pidx),
                    send_sem=send_sems.at[a, k], recv_sem=recv_sems.at[a, k], device_id=peer, device_id_type=MESH))
        return mine, sends, recvs

    def start(self, ins, outs, sems):
        mine, sends, _ = self._parts(ins, outs, sems)
        for cp in mine + sends:
            cp.start()

    def finish(self, ins, outs, sems):
        mine, sends, recvs = self._parts(ins, outs, sems)
        for make in recvs:
            make().wait_recv()
        for cp in sends:
            cp.wait_send()
        for cp in mine:
            cp.wait()


N_CHIP = 4


class _SiblingSwap:
    def __init__(self, arr):
        self.ins = [arr]
        self.r = arr.shape[0] // N_DEV
        self.out_shape = [jax.ShapeDtypeStruct((N_CHIP * self.r, arr.shape[1]), arr.dtype)]
        self.scratch = [pltpu.SemaphoreType.DMA((N_CHIP,)), pltpu.SemaphoreType.DMA((N_CHIP,))]

    def _copies(self, ins, outs, sems):
        send_sems, recv_sems = sems
        x, y, c = _place()
        r = self.r
        return [pltpu.make_async_remote_copy(
            src_ref=ins[0].at[pl.ds(pl.multiple_of((2 * j + 1 - c) * r, 16), r), :],
            dst_ref=outs[0].at[pl.ds(j * r, r), :],
            send_sem=send_sems.at[j], recv_sem=recv_sems.at[j], device_id=(x, y, 1 - c), device_id_type=MESH)
            for j in range(N_CHIP)]

    def start(self, ins, outs, sems):
        for cp in self._copies(ins, outs, sems):
            cp.start()

    def finish(self, ins, outs, sems):
        for cp in self._copies(ins, outs, sems):
            cp.wait()


class _ChipScatter:
    def __init__(self, arr):
        self.ins = [arr]
        self.r = arr.shape[0] // N_CHIP
        self.out_shape = [jax.ShapeDtypeStruct(arr.shape, arr.dtype)]
        self.scratch = [pltpu.SemaphoreType.DMA((3,)), pltpu.SemaphoreType.DMA((3,)), pltpu.SemaphoreType.DMA]

    def _parts(self, ins, outs, sems):
        send_sems, recv_sems, local_sem = sems
        x, y, c = _place()
        r = self.r
        my_chip = 2 * x + y

        def rows(ref, j):
            return ref.at[pl.ds(pl.multiple_of(j * r, 16), r), :]

        mine = pltpu.make_async_copy(rows(ins[0], my_chip), rows(outs[0], my_chip), local_sem)
        sends, recvs = [], []
        for k, (fx, fy) in enumerate(((1, 0), (0, 1), (1, 1))):
            px, py = (1 - x if fx else x), (1 - y if fy else y)
            peer_chip = 2 * px + py
            sends.append(pltpu.make_async_remote_copy(
                src_ref=rows(ins[0], peer_chip), dst_ref=rows(outs[0], my_chip),
                send_sem=send_sems.at[k], recv_sem=recv_sems.at[k], device_id=(px, py, c), device_id_type=MESH))
            recvs.append(functools.partial(
                pltpu.make_async_remote_copy,
                src_ref=rows(ins[0], peer_chip), dst_ref=rows(outs[0], peer_chip),
                send_sem=send_sems.at[k], recv_sem=recv_sems.at[k], device_id=(px, py, c), device_id_type=MESH))
        return mine, sends, recvs

    def start(self, ins, outs, sems):
        mine, sends, _ = self._parts(ins, outs, sems)
        for cp in [mine] + sends:
            cp.start()

    def finish(self, ins, outs, sems):
        mine, sends, recvs = self._parts(ins, outs, sems)
        for make in recvs:
            make().wait_recv()
        for cp in sends:
            cp.wait_send()
        mine.wait()


def _pair_add(partial, recv):
    r = recv.shape[0] // N_CHIP
    cols = recv.shape[1]
    tr = r // 2 if (r // 2) % 16 == 0 else r
    steps = r // tr
    core = lax.axis_index("c").astype(jnp.int32).reshape(1)

    def body(c_ref, p_ref, s_ref, o_ref):
        o_ref[...] = (p_ref[...].astype(F32) + s_ref[...].astype(F32)).astype(BF16)

    spec = pl.BlockSpec((tr, cols), lambda j, i, c_ref: (j * steps + i, 0))
    return pl.pallas_call(
        body, name="pair_add",
        grid_spec=pltpu.PrefetchScalarGridSpec(
            num_scalar_prefetch=1, grid=(N_CHIP, steps),
            in_specs=[pl.BlockSpec((tr, cols), lambda j, i, c_ref: ((2 * j + c_ref[0]) * steps + i, 0)), spec],
            out_specs=spec),
        out_shape=jax.ShapeDtypeStruct(recv.shape, BF16),
        compiler_params=_cparams("parallel", "parallel"),
    )(core, partial, recv)


class _Both:
    def __init__(self, a, b):
        self.a, self.b = a, b
        self.ins = a.ins + b.ins
        self.out_shape = a.out_shape + b.out_shape
        self.scratch = a.scratch + b.scratch

    def _split(self, ins, outs, sems):
        ni, no, ns = len(self.a.ins), len(self.a.out_shape), len(self.a.scratch)
        return (ins[:ni], outs[:no], sems[:ns]), (ins[ni:], outs[no:], sems[ns:])

    def start(self, ins, outs, sems):
        ra, rb = self._split(ins, outs, sems)
        self.a.start(*ra)
        self.b.start(*rb)

    def finish(self, ins, outs, sems):
        ra, rb = self._split(ins, outs, sems)
        self.a.finish(*ra)
        self.b.finish(*rb)


def _exchange(comm, name):
    n, m = len(comm.ins), len(comm.out_shape)

    def body(*refs):
        ins, outs, sems = refs[:n], refs[n:n + m], refs[n + m:]
        comm.start(ins, outs, sems)
        comm.finish(ins, outs, sems)

    return pl.pallas_call(
        body, name=name, out_shape=comm.out_shape, in_specs=[ANY] * n, out_specs=[ANY] * m, scratch_shapes=comm.scratch,
    )(*comm.ins)


def _call(body, *, name, grid, in_specs, out_specs, out_shape, args, scratch=(), sem="parallel", comm=None,
          comm_mid=None):
    if comm is None:
        outs = pl.pallas_call(
            body, name=name, grid=grid, in_specs=list(in_specs), out_specs=list(out_specs), out_shape=list(out_shape),
            scratch_shapes=list(scratch), compiler_params=_cparams(sem))(*args)
        return outs, []
    n_in, n_out, n_sc = len(in_specs), len(out_specs), len(scratch)
    n_ci, n_co = len(comm.ins), len(comm.out_shape)
    last = grid[0] - 1

    def fused(*refs):
        ins, refs = refs[:n_in], refs[n_in:]
        c_ins, refs = refs[:n_ci], refs[n_ci:]
        outs, refs = refs[:n_out], refs[n_out:]
        c_outs, refs = refs[:n_co], refs[n_co:]
        sc, c_sems = refs[:n_sc], refs[n_sc:]
        step = pl.program_id(0)

        @pl.when(step == 0)
        def _():
            comm.start(c_ins, c_outs, c_sems)

        body(*ins, *outs, *sc)

        if comm_mid is not None:
            @pl.when(step == comm_mid)
            def _():
                comm.middle(c_ins, c_outs, c_sems)

        @pl.when(step == last)
        def _():
            if comm_mid is not None:
                comm.finish(c_ins, c_outs, c_sems, middle_done=True)
            else:
                comm.finish(c_ins, c_outs, c_sems)

    outs = pl.pallas_call(
        fused, name=name, grid=grid, in_specs=list(in_specs) + [ANY] * n_ci, out_specs=list(out_specs) + [ANY] * n_co,
        out_shape=list(out_shape) + comm.out_shape, scratch_shapes=list(scratch) + comm.scratch,
        compiler_params=_cparams("arbitrary"))(*args, *comm.ins)
    return outs[:n_out], outs[n_out:]


def _token_specs(tm):
    k = tm // BLK
    return [pl.BlockSpec((BLK, D), functools.partial(lambda i, t: (jnp.maximum(k * i + t - 1, 0), 0), t=t)) for t in range(k)]


def _in_proj(x2d, meta, gain, w_int, b_in, tabs, comm=None):
    p = x2d.shape[0] + BLK
    tm = _row_tile(p)
    k = tm // BLK

    def body(*refs):
        x_refs = refs[:k]
        m_ref, g_ref, w_ref, b_ref, t_ref, h_ref, n1_ref, q_ref, kv_ref, ag_ref, gt_ref = refs[k:]
        i = pl.program_id(0)
        head = jnp.concatenate([jnp.zeros((PAD, D), F32), m_ref[...]], axis=0)
        first = jnp.where(i == 0, head, x_refs[0][...])
        h = jnp.concatenate([first] + [r[...] for r in x_refs[1:]], axis=0) if k > 1 else first
        h_ref[...] = h
        n = _rms(h, g_ref[...]).astype(BF16)
        n1_ref[...] = n
        c, s1, s2 = t_ref[:, 0:128], t_ref[:, 128:256], t_ref[:, 256:384]

        def mm(c0, w):
            return _dot_nt(n, w_ref[c0:c0 + w, :]) + b_ref[:, c0:c0 + w]

        for j in range(4):
            acc = mm(256 * j, 256)
            for t in range(2):
                lo = 256 * j + 128 * t
                q_ref[:, lo:lo + 128] = (_rope(acc[:, 128 * t:128 * (t + 1)], c, s1, s2) * SCALE).astype(BF16)
        acc = mm(1024, 256)
        kv_ref[:, 0:128] = _rope(acc[:, 0:128], c, s1, s2).astype(BF16)
        kv_ref[:, 128:256] = acc[:, 128:256].astype(BF16)
        for j in range(8):
            ag_ref[:, 256 * j:256 * (j + 1)] = mm(QKV_W + 256 * j, 256).astype(BF16)
        for j in range(8):
            gt_ref[:, 256 * j:256 * (j + 1)] = mm(QKV_W + 2048 + 256 * j, 256).astype(BF16)

    def row(w):
        return pl.BlockSpec((tm, w), lambda i: (i, 0))

    return _call(
        body, name="in_proj", grid=(p // tm,),
        in_specs=_token_specs(tm) + [VM, VM, VM, VM, row(384)],
        out_specs=[row(D), row(D), row(D), row(256), row(2048), row(2048)],
        out_shape=[jax.ShapeDtypeStruct((p, D), F32)] + [jax.ShapeDtypeStruct((p, w), BF16) for w in (D, D, 256, 2048, 2048)],
        args=(x2d,) * k + (meta, gain, w_int, b_in, tabs), comm=comm,
        comm_mid=None if comm is None else (3 * (p // tm)) // 4)


N_KEY = 2 * BLK + N_META


def _attn_setup(n, h, q_ref, km_ref, kp_ref, kc_ref):
    lo = lax.broadcasted_iota(jnp.int32, (BLK, BLK), 1) < HEAD_DIM
    lok = lax.broadcasted_iota(jnp.int32, (N_KEY, BLK), 1) < HEAD_DIM

    def dup(lanes):
        cat = jnp.concatenate([kp_ref[:, lanes], kc_ref[:, lanes], km_ref[PAD:BLK, lanes]], axis=0).astype(F32)
        rolled = pltpu.roll(cat, HEAD_DIM, 1)
        return (jnp.where(lok, cat, rolled) if h == 0 else jnp.where(lok, rolled, cat)).astype(BF16)

    k2 = dup(slice(0, 128))
    v2 = dup(slice(128, 256))
    qs = _stack_heads(q_ref, h, lo)

    kr = lax.broadcasted_iota(jnp.int32, (BLK, BLK), 0)
    tq = BLK * n + lax.broadcasted_iota(jnp.int32, (BLK, BLK), 1) - PAD
    t_p = BLK * (n - 1) + kr - PAD
    t_c = BLK * n + kr - PAD
    ok_p = jnp.logical_and(t_p >= N_META, tq - t_p < BLK)
    ok_c = jnp.logical_and(t_c >= N_META, t_c <= tq)
    ok_m = lax.broadcasted_iota(jnp.int32, (N_META, BLK), 0) <= BLK * n + lax.broadcasted_iota(jnp.int32, (N_META, BLK), 1) - PAD
    bias = jnp.concatenate([jnp.where(ok, 0.0, NEG_INF).astype(F32) for ok in (ok_p, ok_c, ok_m)], axis=0)
    return qs, k2, v2, bias, lok


def _attn_head(s, bias, sink):
    s = s + bias
    m = jnp.maximum(jnp.max(s, axis=0, keepdims=True), sink)
    e = jnp.exp(s - m)
    es = jnp.exp(sink - m)
    inv = 1.0 / (jnp.sum(e, axis=0, keepdims=True) + es)
    return e * inv, es * inv


def _stack_heads(ref, h, lo):
    pieces = []
    for jp in range(4):
        v = ref[:, BLK * (4 * h + jp):BLK * (4 * h + jp + 1)]
        zero = jnp.zeros_like(v)
        pieces += [jnp.where(lo, v, zero), jnp.where(lo, zero, v)]
    return jnp.concatenate(pieces, axis=0)


def _unstack_heads(v, jp, lo):
    return jnp.where(lo, v[256 * jp:256 * jp + 128], v[256 * jp + 128:256 * jp + 256])


def _attn_fwd(q, kv, sinks, comm=None):
    p = q.shape[0]
    nb = p // BLK

    def body(q_ref, km_ref, kp_ref, kc_ref, sink_ref, o_ref):
        n = pl.program_id(0)
        lo = lax.broadcasted_iota(jnp.int32, (BLK, BLK), 1) < HEAD_DIM
        for h in range(2):
            qs, k2, v2, bias, _ = _attn_setup(n, h, q_ref, km_ref, kp_ref, kc_ref)
            st = _dot_nt(k2, qs)
            pt = jnp.concatenate(
                [_attn_head(st[:, BLK * g:BLK * (g + 1)], bias, sink_ref[0, 8 * h + g])[0].astype(BF16) for g in range(8)],
                axis=1)
            o = _dot_tn(pt, v2)
            for jp in range(4):
                o_ref[:, BLK * (4 * h + jp):BLK * (4 * h + jp + 1)] = _unstack_heads(o, jp, lo).astype(BF16)

    return _call(
        body, name="attn_fwd", grid=(nb,),
        in_specs=[pl.BlockSpec((BLK, D), lambda i: (i, 0)),
                  pl.BlockSpec((BLK, 256), lambda i: (0, 0)),
                  pl.BlockSpec((BLK, 256), lambda i: (jnp.maximum(i - 1, 0), 0)),
                  pl.BlockSpec((BLK, 256), lambda i: (i, 0)),
                  pl.BlockSpec(memory_space=pltpu.SMEM)],
        out_specs=[pl.BlockSpec((BLK, D), lambda i: (i, 0))],
        out_shape=[jax.ShapeDtypeStruct((p, D), BF16)],
        args=(q, kv, kv, kv, sinks), comm=comm)


def _conv31_fwd(ag, w32, b, comm=None):
    p = ag.shape[0]
    nch = p // BLK

    def body(a_ref, g_ref, w_ref, b_ref, o_ref, gp):
        gp[0:32, :] = jnp.zeros((32, BLK), F32)
        for ci in range(nch):
            r0 = BLK * ci
            glu = a_ref[r0:r0 + BLK, :].astype(F32) * jax.nn.sigmoid(g_ref[r0:r0 + BLK, :].astype(F32))
            if ci == 0:
                glu = jnp.where(_rows(0, BLK) >= PAD, glu, 0.0)
            gp[32 + r0:32 + r0 + BLK, :] = glu
        for ci in range(nch):
            r0 = BLK * ci
            acc = jnp.broadcast_to(b_ref[...], (BLK, BLK))
            for j in range(CONV_K):
                acc = acc + w_ref[j:j + 1, :] * gp[r0 + j + 2:r0 + j + 2 + BLK, :]
            o_ref[r0:r0 + BLK, :] = acc

    return _call(
        body, name="conv31_fwd", grid=(D // BLK,),
        in_specs=[pl.BlockSpec((p, BLK), lambda j: (0, j)), pl.BlockSpec((p, BLK), lambda j: (0, 8 + j)),
                  pl.BlockSpec((32, BLK), lambda j: (0, j)), pl.BlockSpec((1, BLK), lambda j: (0, j))],
        out_specs=[pl.BlockSpec((p, BLK), lambda j: (0, j))],
        out_shape=[jax.ShapeDtypeStruct((p, D), F32)],
        scratch=[pltpu.VMEM((p + 32, BLK), F32)],
        args=(ag, ag, w32, b), comm=comm, comm_mid=None if comm is None else (3 * (D // BLK)) // 4)


def _mixer_fwd(ao, c0, gates, h0p, wa, wc, wo, vecs):
    p = ao.shape[0]
    tm = _row_tile(p)

    def body(ao_ref, c0_ref, gt_ref, h_ref, wa_ref, wc_ref, wo_ref, v_ref,
             c1_ref, at_ref, cv_ref, mg_ref, mix_ref, h1_ref, n2_ref):
        i = pl.program_id(0)
        c1 = _lnsilu(c0_ref[...], v_ref[0:1, :], v_ref[1:2, :]).astype(BF16)
        c1_ref[...] = c1
        attn = _dot(ao_ref[...], wa_ref[...])
        conv = _dot(c1, wc_ref[...]) + v_ref[2:3, :]
        at_ref[...] = attn.astype(BF16)
        cv_ref[...] = conv.astype(BF16)
        merged = (jax.nn.sigmoid(gt_ref[:, 0:D].astype(F32)) * attn
                  + jax.nn.sigmoid(gt_ref[:, D:2 * D].astype(F32)) * conv).astype(BF16)
        mg_ref[...] = merged
        mix = _dot(merged, wo_ref[...])
        mix_ref[...] = mix
        h1 = jnp.where(_rows(i, tm) >= PAD, h_ref[...] + _rms(mix, v_ref[3:4, :]), 0.0)
        h1_ref[...] = h1
        n2_ref[...] = _rms(h1, v_ref[4:5, :]).astype(BF16)

    def row(w):
        return pl.BlockSpec((tm, w), lambda i: (i, 0))

    return pl.pallas_call(
        body, name="mixer_fwd", grid=(p // tm,),
        in_specs=[row(D), row(D), row(2 * D), row(D), VM, VM, VM, VM],
        out_specs=[row(D)] * 7,
        out_shape=[jax.ShapeDtypeStruct((p, D), t) for t in (BF16, BF16, BF16, BF16, F32, F32, BF16)],
        compiler_params=_cparams("parallel"),
    )(ao, c0, gates, h0p, wa, wc, wo, vecs)


def _mm_nt(a, w_t, name):
    p, k = a.shape
    n = w_t.shape[0]
    tm = _row_tile(p)
    ch = 512

    def body(a_ref, w_ref, o_ref):
        a_v = a_ref[...]
        for c0 in range(0, n, ch):
            o_ref[:, c0:c0 + ch] = _dot_nt(a_v, w_ref[c0:c0 + ch, :]).astype(BF16)

    return pl.pallas_call(
        body, name=name, grid=(p // tm,),
        in_specs=[pl.BlockSpec((tm, k), lambda i: (i, 0)), VM],
        out_specs=pl.BlockSpec((tm, n), lambda i: (i, 0)),
        out_shape=jax.ShapeDtypeStruct((p, n), BF16),
        compiler_params=_cparams("parallel"),
    )(a, w_t)


def _conv3(xp_ref, w_ref, r0):
    return (w_ref[0:1, :] * xp_ref[r0 + 6:r0 + 6 + BLK, :] + w_ref[1:2, :] * xp_ref[r0 + 7:r0 + 7 + BLK, :]
            + w_ref[2:3, :] * xp_ref[r0 + 8:r0 + 8 + BLK, :])


def _ffn_slab_specs(p):
    ncol = FFN // BLK
    return [pl.BlockSpec((p, BLK), lambda j: (0, j)), pl.BlockSpec((p, BLK), lambda j: (0, ncol + j)),
            pl.BlockSpec((FFN_K, BLK), lambda j: (0, j)), pl.BlockSpec((FFN_K, BLK), lambda j: (0, ncol + j)),
            pl.BlockSpec((1, BLK), lambda j: (0, j)), pl.BlockSpec((1, BLK), lambda j: (0, ncol + j))]


def _fill_shifted(dst, src_ref, nch):
    dst[0:8, :] = jnp.zeros((8, BLK), F32)
    for ci in range(nch):
        dst[8 + BLK * ci:8 + BLK * (ci + 1), :] = src_ref[BLK * ci:BLK * (ci + 1), :].astype(F32)


def _ffn_act(u0, fw, fb):
    p = u0.shape[0]
    nch = p // BLK

    def body(g_ref, v_ref, wg_ref, wv_ref, bg_ref, bv_ref, o_ref, dv_ref, dg_ref, xg, xv):
        _fill_shifted(xg, g_ref, nch)
        _fill_shifted(xv, v_ref, nch)
        for ci in range(nch):
            r0 = BLK * ci
            ug = _conv3(xg, wg_ref, r0) + bg_ref[...]
            uv = _conv3(xv, wv_ref, r0) + bv_ref[...]
            sg = jax.nn.sigmoid(ug)
            silu = ug * sg
            o_ref[r0:r0 + BLK, :] = (silu * uv).astype(BF16)
            dv_ref[r0:r0 + BLK, :] = silu.astype(BF16)
            dg_ref[r0:r0 + BLK, :] = (uv * (sg * (1.0 + ug * (1.0 - sg)))).astype(BF16)

    slab = pl.BlockSpec((p, BLK), lambda j: (0, j))
    return pl.pallas_call(
        body, name="ffn_act", grid=(FFN // BLK,),
        in_specs=_ffn_slab_specs(p),
        out_specs=[slab] * 3,
        out_shape=[jax.ShapeDtypeStruct((p, FFN), BF16)] * 3,
        scratch_shapes=[pltpu.VMEM((p + 8, BLK), F32)] * 2,
        compiler_params=_cparams("parallel"),
    )(u0, u0, fw, fw, fb, fb)


def _ffn_down_loss(act, wd, h1, tgt, gain):
    p = act.shape[0]
    tm = _row_tile(p)
    k = tm // BLK

    def body(*refs):
        a_ref, w_ref, h_ref = refs[:3]
        t_refs = refs[3:3 + k]
        g_ref, df_ref, da_ref, dy_ref, acc_ref = refs[3 + k:]
        i = pl.program_id(0)

        @pl.when(i == 0)
        def _():
            acc_ref[...] = jnp.zeros_like(acc_ref)

        ffn = _dot(a_ref[...], w_ref[...])
        t = jnp.concatenate([t_ref[...] for t_ref in t_refs], axis=0) if k > 1 else t_refs[0][...]
        diff = jnp.where(_rows(i, tm) >= BLK, h_ref[...] + _rms(ffn, g_ref[...]) - t, 0.0)
        dy = diff * (1.0 / D)
        dffn, dg = _rms_bwd(ffn, g_ref[...], dy)
        acc_ref[0:1, :] += dg
        acc_ref[1:2, :] += jnp.sum(diff * diff, axis=0, keepdims=True) * (0.5 / D)
        dy_ref[...] = dy
        dfb = dffn.astype(BF16)
        df_ref[...] = dfb
        for c0 in range(0, FFN, 256):
            da_ref[:, c0:c0 + 256] = _dot_nt(dfb, w_ref[c0:c0 + 256, :]).astype(BF16)

    def row(w):
        return pl.BlockSpec((tm, w), lambda i: (i, 0))

    return pl.pallas_call(
        body, name="ffn_down_loss", grid=(p // tm,),
        in_specs=[row(FFN), VM, row(D)] + _token_specs(tm) + [VM],
        out_specs=[row(D), row(FFN), row(D), pl.BlockSpec((8, D), lambda i: (0, 0))],
        out_shape=[jax.ShapeDtypeStruct((p, D), BF16), jax.ShapeDtypeStruct((p, FFN), BF16),
                   jax.ShapeDtypeStruct((p, D), F32), jax.ShapeDtypeStruct((8, D), F32)],
        compiler_params=_cparams("arbitrary"),
    )(act, wd, h1, *([tgt] * k), gain)


def _mm_tn(pieces, b, name, col_sums=False, comm=None):
    p, n = b.shape
    tk = 256
    nblk = [a.shape[1] // tk for a in pieces]
    offs = [sum(nblk[:q]) for q in range(len(pieces))]
    total = sum(nblk)
    npc = len(pieces)

    def body(*refs):
        a_refs, b_ref, o_ref = refs[:npc], refs[npc], refs[npc + 1]
        i = pl.program_id(0)
        for q, a_ref in enumerate(a_refs):
            @pl.when(jnp.logical_and(i >= offs[q], i < offs[q] + nblk[q]))
            def _(a_ref=a_ref):
                a_v = a_ref[...]
                o_ref[...] = _dot_tn(a_v, b_ref[...]).astype(BF16)
                if col_sums:
                    refs[npc + 2][...] = jnp.sum(a_v.astype(F32), axis=0, keepdims=True)

    def a_spec(q):
        return pl.BlockSpec((p, tk), lambda i: (0, jnp.clip(i - offs[q], 0, nblk[q] - 1)))

    out_specs = [pl.BlockSpec((tk, n), lambda i: (i, 0))]
    out_shape = [jax.ShapeDtypeStruct((total * tk, n), BF16)]
    if col_sums:
        out_specs.append(pl.BlockSpec((1, tk), lambda i: (0, i)))
        out_shape.append(jax.ShapeDtypeStruct((1, total * tk), F32))
    res, sent = _call(
        body, name=name, grid=(total,),
        in_specs=[a_spec(q) for q in range(npc)] + [VM],
        out_specs=out_specs, out_shape=out_shape, args=(*pieces, b), comm=comm,
        comm_mid=(3 * total) // 4 if hasattr(comm, "middle") else None)
    res = res if col_sums else res[0]
    return res if comm is None else (res, sent)


def _ffn_act_bwd(u0, dact, dact_dg, dact_dv, fw, act, dffn, comm=None):
    p = u0.shape[0]
    nch = p // BLK
    ncol = FFN // BLK

    def body(g_ref, v_ref, wg_ref, wv_ref, da_ref, lg_ref, lv_ref, act_ref, df_ref,
             dg_ref, dv_ref, gwg_ref, gwv_ref, gbg_ref, gbv_ref, gwd_ref, eg, ev):
        gwd_ref[...] = _dot_tn(act_ref[...], df_ref[...]).astype(BF16)
        eg[p:p + 8, :] = jnp.zeros((8, BLK), F32)
        ev[p:p + 8, :] = jnp.zeros((8, BLK), F32)
        for ci in range(nch):
            r0 = BLK * ci
            d = da_ref[r0:r0 + BLK, :].astype(F32)
            eg[r0:r0 + BLK, :] = d * lg_ref[r0:r0 + BLK, :].astype(F32)
            ev[r0:r0 + BLK, :] = d * lv_ref[r0:r0 + BLK, :].astype(F32)
        def fold(v):
            return jnp.sum(v.reshape(BLK // 8, 8, BLK), axis=0)

        for e_s, x_ref, w_ref, d_ref, gw_ref, gb_ref in ((eg, g_ref, wg_ref, dg_ref, gwg_ref, gbg_ref),
                                                        (ev, v_ref, wv_ref, dv_ref, gwv_ref, gbv_ref)):
            sums = [jnp.zeros((8, BLK), F32) for _ in range(FFN_K + 1)]
            for ci in range(nch):
                r0 = BLK * ci
                es = [e_s[r0 + t:r0 + t + BLK, :] for t in range(FFN_K)]
                du = w_ref[2:3, :] * es[0] + w_ref[1:2, :] * es[1] + w_ref[0:1, :] * es[2]
                if ci == 0:
                    du = jnp.where(_rows(0, BLK) >= PAD, du, 0.0)
                d_ref[r0:r0 + BLK, :] = du.astype(BF16)
                x = x_ref[r0:r0 + BLK, :].astype(F32)
                for j in range(FFN_K):
                    sums[j] = sums[j] + fold(es[FFN_K - 1 - j] * x)
                sums[FFN_K] = sums[FFN_K] + fold(es[0])
            for j in range(FFN_K):
                gw_ref[j:j + 1, :] = jnp.sum(sums[j], axis=0, keepdims=True)
            gb_ref[...] = jnp.sum(sums[FFN_K], axis=0, keepdims=True)

    slab = pl.BlockSpec((p, BLK), lambda j: (0, j))
    wspec = pl.BlockSpec((FFN_K, BLK), lambda j: (0, j))
    bspec = pl.BlockSpec((1, BLK), lambda j: (0, j))
    return _call(
        body, name="ffn_act_bwd", grid=(ncol,),
        in_specs=_ffn_slab_specs(p)[:4] + [slab] * 4 + [VM],
        out_specs=[slab, slab, wspec, wspec, bspec, bspec, pl.BlockSpec((BLK, D), lambda j: (j, 0))],
        out_shape=[jax.ShapeDtypeStruct((p, FFN), BF16)] * 2 + [jax.ShapeDtypeStruct((FFN_K, FFN), F32)] * 2
        + [jax.ShapeDtypeStruct((1, FFN), F32)] * 2 + [jax.ShapeDtypeStruct((FFN, D), BF16)],
        scratch=[pltpu.VMEM((p + 8, BLK), F32)] * 2,
        args=(u0, u0, fw, fw, dact, dact_dg, dact_dv, act, dffn), comm=comm)


def _ffn_in_bwd(dug, duv, w_upt, h1, dy, gain, comm=None):
    p = h1.shape[0]
    tm = _row_tile(p)

    def body(dg_ref, dv_ref, w_ref, h_ref, dy_ref, g_ref, o_ref, acc_ref):
        i = pl.program_id(0)

        @pl.when(i == 0)
        def _():
            acc_ref[...] = jnp.zeros_like(acc_ref)

        dn = _dot(dg_ref[...], w_ref[0:FFN, :]) + _dot(dv_ref[...], w_ref[FFN:2 * FFN, :])
        dh, dg = _rms_bwd(h_ref[...], g_ref[...], dn)
        o_ref[...] = dy_ref[...] + dh
        acc_ref[0:1, :] += dg

    def row(w):
        return pl.BlockSpec((tm, w), lambda i: (i, 0))

    return _call(
        body, name="ffn_in_bwd", grid=(p // tm,),
        in_specs=[row(FFN), row(FFN), VM, row(D), row(D), VM],
        out_specs=[row(D), pl.BlockSpec((8, D), lambda i: (0, 0))],
        out_shape=[jax.ShapeDtypeStruct((p, D), F32), jax.ShapeDtypeStruct((8, D), F32)],
        sem="arbitrary", args=(dug, duv, w_upt, h1, dy, gain), comm=comm)


def _mixer_bwd(dh1, mix, attn, conv, gates, c0, wa, wc, wo, vecs, comm=None):
    p = dh1.shape[0]
    tm = _row_tile(p)

    def body(dh_ref, mix_ref, at_ref, cv_ref, gt_ref, c0_ref, wa_ref, wc_ref, wo_ref, v_ref,
             dmix_ref, dat_ref, dcv_ref, dgt_ref, dao_ref, dc0_ref, acc_ref):
        i = pl.program_id(0)

        @pl.when(i == 0)
        def _():
            acc_ref[...] = jnp.zeros_like(acc_ref)

        dmix, dgp = _rms_bwd(mix_ref[...], v_ref[3:4, :], dh_ref[...])
        dmix = dmix.astype(BF16)
        dmix_ref[...] = dmix
        dmg = _dot_nt(dmix, wo_ref[...])
        sa = jax.nn.sigmoid(gt_ref[:, 0:D].astype(F32))
        sc = jax.nn.sigmoid(gt_ref[:, D:2 * D].astype(F32))
        dat = dmg * sa
        dcv = dmg * sc
        dgt_ref[:, 0:D] = (dmg * at_ref[...].astype(F32) * sa * (1.0 - sa)).astype(BF16)
        dgt_ref[:, D:2 * D] = (dmg * cv_ref[...].astype(F32) * sc * (1.0 - sc)).astype(BF16)
        datb = dat.astype(BF16)
        dcvb = dcv.astype(BF16)
        dat_ref[...] = datb
        dcv_ref[...] = dcvb
        dao_ref[...] = _dot_nt(datb, wa_ref[...]).astype(BF16)
        dc1 = _dot_nt(dcvb, wc_ref[...])
        dc0, dlg, dlb = _lnsilu_bwd(c0_ref[...], v_ref[0:1, :], v_ref[1:2, :], dc1)
        dc0_ref[...] = dc0
        acc_ref[0:1, :] += dgp
        acc_ref[1:2, :] += jnp.sum(dcv, axis=0, keepdims=True)
        acc_ref[2:3, :] += dlg
        acc_ref[3:4, :] += dlb

    def row(w):
        return pl.BlockSpec((tm, w), lambda i: (i, 0))

    return _call(
        body, name="mixer_bwd", grid=(p // tm,),
        in_specs=[row(D), row(D), row(D), row(D), row(2 * D), row(D), VM, VM, VM, VM],
        out_specs=[row(D), row(D), row(D), row(2 * D), row(D), row(D), pl.BlockSpec((8, D), lambda i: (0, 0))],
        out_shape=[jax.ShapeDtypeStruct((p, D), BF16)] * 3 + [jax.ShapeDtypeStruct((p, 2 * D), BF16),
                                                             jax.ShapeDtypeStruct((p, D), BF16),
                                                             jax.ShapeDtypeStruct((p, D), F32),
                                                             jax.ShapeDtypeStruct((8, D), F32)],
        sem="arbitrary", args=(dh1, mix, attn, conv, gates, c0, wa, wc, wo, vecs), comm=comm)


def _conv31_bwd(ag, dc0, w32, tn_pairs, comm=None):
    p = ag.shape[0]
    nch = p // BLK
    npair = len(tn_pairs)

    def body(*refs):
        a_ref, g_ref, dc_ref, w_ref = refs[:4]
        tn_a, tn_b = refs[4:4 + npair], refs[4 + npair:4 + 2 * npair]
        da_ref, dg_ref, gw_ref, gb_ref = refs[4 + 2 * npair:8 + 2 * npair]
        tn_o = refs[8 + 2 * npair:8 + 3 * npair]
        gp, dp = refs[8 + 3 * npair:]
        for ta, tb, to in zip(tn_a, tn_b, tn_o):
            to[...] = _dot_tn(ta[...], tb[...]).astype(BF16)
        gp[0:32, :] = jnp.zeros((32, BLK), F32)
        dp[p:p + 32, :] = jnp.zeros((32, BLK), F32)
        bsum = jnp.zeros((BLK, BLK), F32)
        for ci in range(nch):
            r0 = BLK * ci
            glu = a_ref[r0:r0 + BLK, :].astype(F32) * jax.nn.sigmoid(g_ref[r0:r0 + BLK, :].astype(F32))
            if ci == 0:
                glu = jnp.where(_rows(0, BLK) >= PAD, glu, 0.0)
            gp[32 + r0:32 + r0 + BLK, :] = glu
            d = dc_ref[r0:r0 + BLK, :]
            dp[r0:r0 + BLK, :] = d
            bsum = bsum + d
        gb_ref[...] = jnp.sum(bsum, axis=0, keepdims=True)
        for ci in range(nch):
            r0 = BLK * ci
            acc = jnp.zeros((BLK, BLK), F32)
            for j in range(CONV_K):
                acc = acc + w_ref[j:j + 1, :] * dp[r0 + 30 - j:r0 + 30 - j + BLK, :]
            if ci == 0:
                acc = jnp.where(_rows(0, BLK) >= PAD, acc, 0.0)
            a = a_ref[r0:r0 + BLK, :].astype(F32)
            sg = jax.nn.sigmoid(g_ref[r0:r0 + BLK, :].astype(F32))
            da_ref[r0:r0 + BLK, :] = (acc * sg).astype(BF16)
            dg_ref[r0:r0 + BLK, :] = (acc * a * sg * (1.0 - sg)).astype(BF16)
        sub = BLK // 2
        accs = [jnp.zeros((8, BLK), F32) for _ in range(CONV_K)]
        for r0 in range(0, p, sub):
            d = dp[r0:r0 + sub, :]
            for j in range(CONV_K):
                prod = d * gp[r0 + j + 2:r0 + j + 2 + sub, :]
                accs[j] = accs[j] + jnp.sum(prod.reshape(sub // 8, 8, BLK), axis=0)
        for j in range(CONV_K):
            gw_ref[j:j + 1, :] = jnp.sum(accs[j], axis=0, keepdims=True)
        gw_ref[CONV_K:32, :] = jnp.zeros((32 - CONV_K, BLK), F32)

    slab = pl.BlockSpec((p, BLK), lambda j: (0, j))
    return _call(
        body, name="conv31_bwd", grid=(D // BLK,),
        in_specs=[slab, pl.BlockSpec((p, BLK), lambda j: (0, 8 + j)), slab, pl.BlockSpec((32, BLK), lambda j: (0, j))]
        + [slab] * npair + [VM] * npair,
        out_specs=[slab, slab, pl.BlockSpec((32, BLK), lambda j: (0, j)), pl.BlockSpec((1, BLK), lambda j: (0, j))]
        + [pl.BlockSpec((BLK, D), lambda j: (j, 0))] * npair,
        out_shape=[jax.ShapeDtypeStruct((p, D), BF16)] * 2 + [jax.ShapeDtypeStruct((32, D), F32),
                                                             jax.ShapeDtypeStruct((1, D), F32)]
        + [jax.ShapeDtypeStruct((D, D), BF16)] * npair,
        scratch=[pltpu.VMEM((p + 32, BLK), F32)] * 2,
        args=(ag, ag, dc0, w32, *[a for a, _ in tn_pairs], *[b for _, b in tn_pairs]), comm=comm)


def _attn_bwd(q, kv, dao, sinks, tabs, comm=None):
    p = q.shape[0]
    nb = p // BLK

    def body(q_ref, km_ref, kp_ref, kc_ref, do_ref, sink_ref, t_ref, dqkv_ref, dsink_ref, carry, macc):
        i = pl.program_id(0)
        n = nb - 1 - i

        @pl.when(i == 0)
        def _():
            carry[...] = jnp.zeros_like(carry)
            macc[...] = jnp.zeros_like(macc)
            dsink_ref[...] = jnp.zeros_like(dsink_ref)

        lo = lax.broadcasted_iota(jnp.int32, (BLK, BLK), 1) < HEAD_DIM
        lane8 = lax.broadcasted_iota(jnp.int32, (8, BLK), 1)
        c, s1, s2 = t_ref[:, 0:128], -t_ref[:, 128:256], -t_ref[:, 256:384]
        dk = jnp.zeros((N_KEY, BLK), F32)
        dv = jnp.zeros((N_KEY, BLK), F32)
        for h in range(2):
            qs, k2, v2, bias, lok = _attn_setup(n, h, q_ref, km_ref, kp_ref, kc_ref)
            dos = _stack_heads(do_ref, h, lo)
            st = _dot_nt(k2, qs)
            dpt = _dot_nt(v2, dos)
            p_parts, ds_parts = [], []
            for g in range(8):
                cols = slice(BLK * g, BLK * (g + 1))
                pn, ps = _attn_head(st[:, cols], bias, sink_ref[0, 8 * h + g])
                dp = dpt[:, cols]
                delta = jnp.sum(pn * dp, axis=0, keepdims=True)
                ds_parts.append((pn * (dp - delta)).astype(BF16))
                p_parts.append(pn.astype(BF16))
                dsk = -jnp.sum(ps * delta, axis=1, keepdims=True)
                dsink_ref[...] += jnp.where(lane8 == 8 * h + g, dsk, 0.0)
            dst = jnp.concatenate(ds_parts, axis=1)
            pt = jnp.concatenate(p_parts, axis=1)
            dq = _dot_tn(dst, k2)
            for jp in range(4):
                lo_c = BLK * (4 * h + jp)
                dqkv_ref[:, lo_c:lo_c + BLK] = (_rope(_unstack_heads(dq, jp, lo), c, s1, s2) * SCALE).astype(BF16)
            dk2 = _dot(dst, qs)
            dv2 = _dot(pt, dos)
            dk2 = dk2 + pltpu.roll(dk2, HEAD_DIM, 1)
            dv2 = dv2 + pltpu.roll(dv2, HEAD_DIM, 1)
            own = lok if h == 0 else jnp.logical_not(lok)
            dk = jnp.where(own, dk2, dk)
            dv = jnp.where(own, dv2, dv)
        macc[:, 0:BLK] += dk[2 * BLK:N_KEY]
        macc[:, BLK:2 * BLK] += dv[2 * BLK:N_KEY]
        last = (n == 0).astype(F32)
        zpad = jnp.zeros((PAD, BLK), F32)
        dk_c = dk[BLK:2 * BLK] + carry[:, 0:BLK] + last * jnp.concatenate([zpad, macc[:, 0:BLK]], axis=0)
        dv_c = dv[BLK:2 * BLK] + carry[:, BLK:2 * BLK] + last * jnp.concatenate([zpad, macc[:, BLK:2 * BLK]], axis=0)
        carry[:, 0:BLK] = dk[0:BLK]
        carry[:, BLK:2 * BLK] = dv[0:BLK]
        dqkv_ref[:, D:D + BLK] = _rope(dk_c, c, s1, s2).astype(BF16)
        dqkv_ref[:, D + BLK:D + 2 * BLK] = dv_c.astype(BF16)

    def rev(w):
        return pl.BlockSpec((BLK, w), lambda i: (nb - 1 - i, 0))

    return _call(
        body, name="attn_bwd", grid=(nb,),
        in_specs=[rev(D),
                  pl.BlockSpec((BLK, 256), lambda i: (0, 0)),
                  pl.BlockSpec((BLK, 256), lambda i: (jnp.maximum(nb - 2 - i, 0), 0)),
                  rev(256), rev(D),
                  pl.BlockSpec(memory_space=pltpu.SMEM), rev(384)],
        out_specs=[rev(QKV_W), pl.BlockSpec((8, BLK), lambda i: (0, 0))],
        out_shape=[jax.ShapeDtypeStruct((p, QKV_W), BF16), jax.ShapeDtypeStruct((8, BLK), F32)],
        scratch=[pltpu.VMEM((BLK, 256), F32), pltpu.VMEM((N_META, 256), F32)], sem="arbitrary",
        args=(q, kv, kv, kv, dao, sinks, tabs), comm=comm)


def _in_bwd(dqkv, da, dg, dgt, w_int, h0p, dh1, gain, comm=None):
    p = h0p.shape[0]
    tm = _row_tile(p)
    nt = p // tm
    first_rows = tm - BLK

    def body(dq_ref, da_ref, dg_ref, dt_ref, w_ref, h_ref, dh_ref, g_ref, gx_ref, dm_ref, acc_ref, buf, sems):
        i = pl.program_id(0)
        slot = i % 2

        @pl.when(i == 0)
        def _():
            acc_ref[...] = jnp.zeros_like(acc_ref)

        dn = (_dot(dq_ref[...], w_ref[0:QKV_W, :]) + _dot(da_ref[...], w_ref[QKV_W:QKV_W + D, :])
              + _dot(dg_ref[...], w_ref[QKV_W + D:QKV_W + 2 * D, :]) + _dot(dt_ref[...], w_ref[QKV_W + 2 * D:IN_W, :]))
        dh, dgain = _rms_bwd(h_ref[...], g_ref[...], dn)
        dh0 = dh_ref[...] + dh
        acc_ref[0:1, :] += dgain
        buf[slot] = dh0

        @pl.when(i == 0)
        def _():
            dm_ref[...] = dh0[PAD:BLK]

        def first_copy():
            return pltpu.make_async_copy(buf.at[0, pl.ds(BLK, first_rows), :], gx_ref.at[pl.ds(0, first_rows), :], sems.at[0])

        def tile_copy(j, s):
            return pltpu.make_async_copy(buf.at[s], gx_ref.at[pl.ds(pl.multiple_of(j * tm - BLK, BLK), tm), :], sems.at[s])

        if first_rows:
            @pl.when(i == 1)
            def _():
                first_copy().wait()

        @pl.when(i >= 2)
        def _():
            tile_copy(i - 1, 1 - slot).wait()

        if first_rows:
            @pl.when(i == 0)
            def _():
                first_copy().start()

        @pl.when(i > 0)
        def _():
            tile_copy(i, slot).start()

        @pl.when(i == nt - 1)
        def _():
            tile_copy(i, slot).wait()

    def row(w):
        return pl.BlockSpec((tm, w), lambda i: (i, 0))

    return _call(
        body, name="in_bwd", grid=(nt,),
        in_specs=[row(QKV_W), row(D), row(D), row(2 * D), VM, row(D), row(D), VM],
        out_specs=[ANY, pl.BlockSpec((N_META, D), lambda i: (0, 0)), pl.BlockSpec((8, D), lambda i: (0, 0))],
        out_shape=[jax.ShapeDtypeStruct((p - BLK, D), F32), jax.ShapeDtypeStruct((N_META, D), F32),
                   jax.ShapeDtypeStruct((8, D), F32)],
        scratch=[pltpu.VMEM((2, tm, D), F32), pltpu.SemaphoreType.DMA((2,))],
        sem="arbitrary", args=(dqkv, da, dg, dgt, w_int, h0p, dh1, gain), comm=comm)


def _sum_slots(slots, name):
    r = slots.shape[0] // N_DEV
    cols = slots.shape[1]
    tr = r if r <= 352 else (r // 2 if (r // 2) % 16 == 0 else r // 3)
    steps = r // tr

    def body(*refs):
        acc = refs[0][...].astype(F32)
        for s in range(1, N_DEV):
            acc = acc + refs[s][...].astype(F32)
        refs[N_DEV][...] = acc

    return pl.pallas_call(
        body, name=name, grid=(steps,),
        in_specs=[pl.BlockSpec((tr, cols), functools.partial(lambda i, s: (s * steps + i, 0), s=s)) for s in range(N_DEV)],
        out_specs=pl.BlockSpec((tr, cols), lambda i: (i, 0)),
        out_shape=jax.ShapeDtypeStruct((r, cols), F32),
        compiler_params=_cparams("parallel"),
    )(*([slots] * N_DEV))


def _adamw_math(w, g, m, v):
    m_n = ADAM_B1 * m + (1.0 - ADAM_B1) * g
    v_n = ADAM_B2 * v + (1.0 - ADAM_B2) * jnp.square(g)
    m_hat = m_n / (1.0 - ADAM_B1 ** ADAM_STEP)
    v_hat = v_n / (1.0 - ADAM_B2 ** ADAM_STEP)
    return -ADAM_LR * (m_hat / (jnp.sqrt(v_hat) + ADAM_EPS) + ADAM_WD * w), m_n, v_n


def _sum_adamw(parts, w, m, v, name, nslots=N_DEV):
    r, cols = w.shape
    rs = r // len(parts)
    tr = rs if rs <= 352 else (rs // 2 if (rs // 2) % 16 == 0 else rs // 3)
    steps = rs // tr

    def body(*refs):
        w_ref, m_ref, v_ref, g_ref, d_ref, nm_ref, nv_ref = refs[nslots * len(parts):]
        i = pl.program_id(0)
        for q in range(len(parts)):
            @pl.when(i // steps == q)
            def _(q=q):
                g = refs[nslots * q][...].astype(F32)
                for s in range(1, nslots):
                    g = g + refs[nslots * q + s][...].astype(F32)
                g_ref[...] = g
                d_ref[...], nm_ref[...], nv_ref[...] = _adamw_math(w_ref[...], g, m_ref[...], v_ref[...])

    def slot_spec(q, s):
        return pl.BlockSpec((tr, cols), lambda i: (s * steps + jnp.clip(i - q * steps, 0, steps - 1), 0))

    spec = pl.BlockSpec((tr, cols), lambda i: (i, 0))
    return pl.pallas_call(
        body, name=name, grid=(steps * len(parts),),
        in_specs=[slot_spec(q, s) for q in range(len(parts)) for s in range(nslots)] + [spec] * 3,
        out_specs=[spec] * 4, out_shape=[jax.ShapeDtypeStruct((r, cols), F32)] * 4,
        compiler_params=_cparams("parallel"),
    )(*[a for a in parts for _ in range(nslots)], w, m, v)


def _adamw_many(ws, gs, ms, vs, name):
    n = len(ws)

    def body(*refs):
        w, g, m, v = refs[0:n], refs[n:2 * n], refs[2 * n:3 * n], refs[3 * n:4 * n]
        d, nm, nv = refs[4 * n:5 * n], refs[5 * n:6 * n], refs[6 * n:7 * n]
        for k in range(n):
            d[k][...], nm[k][...], nv[k][...] = _adamw_math(w[k][...], g[k][...], m[k][...], v[k][...])

    outs = pl.pallas_call(
        body, name=name, in_specs=[VM] * (4 * n), out_specs=[VM] * (3 * n),
        out_shape=[jax.ShapeDtypeStruct(a.shape, F32) for a in ws] * 3,
    )(*ws, *gs, *ms, *vs)
    return outs[0:n], outs[n:2 * n], outs[2 * n:3 * n]


def _rope_tables(p):
    half = ROT_DIM // 2
    lane = jnp.arange(BLK)
    seg = (lane % HEAD_DIM) // half
    inv_freq = ROPE_THETA ** (-(lane % half).astype(F32) * 2.0 / ROT_DIM)
    pos = (jnp.arange(p) - PAD).astype(F32)
    ang = pos[:, None] * inv_freq[None, :]
    cos = jnp.cos(ang)
    sin = jnp.sin(ang)
    c = jnp.where(seg[None, :] < 2, cos, 1.0)
    s1 = jnp.where(seg[None, :] == 0, -sin, 0.0)
    s2 = jnp.where(seg[None, :] == 1, sin, 0.0)
    return jnp.concatenate([c, s1, s2], axis=1).astype(F32)


def _flat_pack(parts, rows):
    flat = jnp.concatenate([a.reshape(-1).astype(F32) for a in parts])
    return jnp.pad(flat, (0, rows * D - flat.shape[0])).reshape(rows, D)


def _flat_unpack(pack, shapes):
    flat = pack.reshape(-1)
    out, off = [], 0
    for s in shapes:
        size = 1
        for e in s:
            size *= e
        out.append(flat[off:off + size].reshape(s))
        off += size
    return out


def kernel(x, meta_tokens, norm_pre_mix, norm_post_mix, w_in, b_in, attn_sinks, w_attn_proj, conv_dw_w, conv_dw_b, conv_ln_g, conv_ln_b, w_conv_proj, b_conv_proj, w_out, norm_pre_ffn, norm_post_ffn, w_up, ffn_dw_w, ffn_dw_b, w_down, loss_target, m_meta_tokens, m_norm_pre_mix, m_norm_post_mix, m_w_in, m_b_in, m_attn_sinks, m_w_attn_proj, m_conv_dw_w, m_conv_dw_b, m_conv_ln_g, m_conv_ln_b, m_w_conv_proj, m_b_conv_proj, m_w_out, m_norm_pre_ffn, m_norm_post_ffn, m_w_up, m_ffn_dw_w, m_ffn_dw_b, m_w_down, v_meta_tokens, v_norm_pre_mix, v_norm_post_mix, v_w_in, v_b_in, v_attn_sinks, v_w_attn_proj, v_conv_dw_w, v_conv_dw_b, v_conv_ln_g, v_conv_ln_b, v_w_conv_proj, v_b_conv_proj, v_w_out, v_norm_pre_ffn, v_norm_post_ffn, v_w_up, v_ffn_dw_w, v_ffn_dw_b, v_w_down):
    seq = x.shape[1]
    p = seq + BLK
    me = 4 * lax.axis_index("x") + 2 * lax.axis_index("y") + lax.axis_index("c")
    in_cols = w_in.shape[2]
    up_cols = w_up.shape[2]

    small = jnp.zeros((56, up_cols), F32)
    small = small.at[0:N_META, 0:BLK].set(meta_tokens)
    small = small.at[16:16 + CONV_K, 0:BLK].set(conv_dw_w[0])
    small = small.at[48:48 + FFN_K, :].set(ffn_dw_w[0])
    w_int, small_all = _exchange(_Both(_GatherRelay(w_in[0].T.astype(BF16)), _Gather([small])), "gather_w_in")
    small_all = small_all.reshape(N_DEV, 56, up_cols)
    meta_full = small_all[:, 0:N_META, 0:BLK].transpose(1, 0, 2).reshape(N_META, D)
    cdw = small_all[:, 16:16 + CONV_K, 0:BLK].transpose(1, 0, 2).reshape(CONV_K, D)
    cdw32 = jnp.pad(cdw, ((0, 32 - CONV_K), (0, 0)))
    fdw = small_all[:, 48:48 + FFN_K, :].transpose(1, 0, 2).reshape(FFN_K, 2 * FFN)

    tabs = _rope_tables(p)
    vecs = jnp.concatenate([conv_ln_g, conv_ln_b, b_conv_proj, norm_post_mix, norm_pre_ffn, jnp.zeros((3, D), F32)], axis=0)

    (h0p, n1, q, kv, ag, gates), (wa, wc, wo) = _in_proj(
        x[0], meta_full, norm_pre_mix, w_int, b_in, tabs,
        comm=_Gather([w_attn_proj[0].astype(BF16), w_conv_proj[0].astype(BF16), w_out[0].astype(BF16)]))
    (ao,), (w_upt,) = _attn_fwd(q, kv, attn_sinks, comm=_Gather([w_up[0].T.astype(BF16)]))
    (c0,), (wd,) = _conv31_fwd(ag, cdw32, conv_dw_b, comm=_Gather([w_down[0].astype(BF16)]))
    c1, attn, conv, merged, mix, h1, n2 = _mixer_fwd(ao, c0, gates, h0p, wa, wc, wo, vecs)
    u0 = _mm_nt(n2, w_upt, "ffn_up")
    act, dact_dv, dact_dg = _ffn_act(u0, fdw, ffn_dw_b)
    dffn, dact, dy, acc_f = _ffn_down_loss(act, wd, h1, loss_target[0], norm_post_ffn)

    (dug, duv, gfw_g, gfw_v, gfb_g, gfb_v, g_wd), _ = _ffn_act_bwd(u0, dact, dact_dg, dact_dv, fdw, act, dffn)
    g_wupt, (s_wd0,) = _mm_tn([dug, duv], n2, "grad_w_up", comm=_Scatter([g_wd], 0, 2))
    (dh1, acc_u), (s_wd1,) = _ffn_in_bwd(dug, duv, w_upt, h1, dy, norm_pre_ffn, comm=_Scatter([g_wd], 1, 2))
    (dmix, dat, dcv, dgt, dao, dc0, acc_m), (s_wup0,) = _mixer_bwd(
        dh1, mix, attn, conv, gates, c0, wa, wc, wo, vecs, comm=_Scatter([g_wupt], 0, 4))
    (da, dg, g_cdw, g_cdb, g_wo, g_wa, g_wc), (s_wup1, s_wup2, s_wup3) = _conv31_bwd(
        ag, dc0, cdw32, [(merged, dmix), (ao, dat), (c1, dcv)],
        comm=_Both(_Both(_Scatter([g_wupt], 1, 4), _Scatter([g_wupt], 2, 4)), _Scatter([g_wupt], 3, 4)))
    (dqkv, dsink), (s_wa, s_wc, s_wo) = _attn_bwd(q, kv, dao, attn_sinks, tabs, comm=_Scatter([g_wa, g_wc, g_wo]))
    loss_row = jnp.sum(acc_f[1:2, :], axis=1, keepdims=True)
    early = [loss_row, acc_m[0:1], dsink[0:1, 0:16], g_cdw[0:CONV_K], g_cdb,
             acc_m[2:3], acc_m[3:4], acc_m[1:2], acc_u[0:1], acc_f[0:1],
             jnp.concatenate([gfw_g, gfw_v], axis=1), jnp.concatenate([gfb_g, gfb_v], axis=1)]
    (g_wint, g_bin), (gathered_early,) = _mm_tn([dqkv, da, dg, dgt], n1, "grad_w_in", col_sums=True,
                                                comm=_Gather([_flat_pack(early, 64)]))
    (from_sibling,) = _exchange(_SiblingSwap(g_wint), "swap_w_in")
    (grad_x2d, dmeta, acc_i), (s_win,) = _in_bwd(dqkv, da, dg, dgt, w_int, h0p, dh1, norm_pre_mix,
                                                 comm=_ChipScatter(_pair_add(g_wint, from_sibling)))

    big = []
    for nm, parts, nslots, w, m, v, tr in (
            ("w_in", [s_win], N_CHIP, w_in, m_w_in, v_w_in, True), ("w_up", [s_wup0, s_wup1, s_wup2, s_wup3], N_DEV, w_up, m_w_up, v_w_up, True),
            ("w_attn_proj", [s_wa], N_DEV, w_attn_proj, m_w_attn_proj, v_w_attn_proj, False),
            ("w_conv_proj", [s_wc], N_DEV, w_conv_proj, m_w_conv_proj, v_w_conv_proj, False),
            ("w_out", [s_wo], N_DEV, w_out, m_w_out, v_w_out, False),
            ("w_down", [s_wd0, s_wd1], N_DEV, w_down, m_w_down, v_w_down, False)):
        ins = [a[0].T if tr else a[0] for a in (w, m, v)]
        big.append(tuple((o.T if tr else o)[None] for o in _sum_adamw(parts, *ins, "update_" + nm, nslots)))

    late = [dmeta, acc_i[0:1], g_bin]
    (gathered_late,) = _exchange(_Gather([_flat_pack(late, 24)]), "gather_small_grads")
    g_meta, g_npm, g_bi = _flat_unpack(_sum_slots(gathered_late, "sum_late_grads"), [a.shape for a in late])
    tot = _flat_unpack(_sum_slots(gathered_early, "sum_small_grads"), [a.shape for a in early])
    (loss, g_nqm, g_sk, g_cw, g_cb, g_lg, g_lb, g_bc, g_npf, g_nqf, g_fw, g_fb) = tot
    loss = loss.reshape(())
    g_meta = lax.dynamic_slice_in_dim(g_meta, me * BLK, BLK, axis=1)
    g_cw = lax.dynamic_slice_in_dim(g_cw, me * BLK, BLK, axis=1)[None]
    g_fw = lax.dynamic_slice_in_dim(g_fw, me * up_cols, up_cols, axis=1)[None]

    sm_w = [meta_tokens, norm_pre_mix, norm_post_mix, b_in, attn_sinks, conv_dw_w, conv_dw_b, conv_ln_g, conv_ln_b,
            b_conv_proj, norm_pre_ffn, norm_post_ffn, ffn_dw_w, ffn_dw_b]
    sm_g = [g_meta, g_npm, g_nqm, g_bi, g_sk, g_cw, g_cb, g_lg, g_lb, g_bc, g_npf, g_nqf, g_fw, g_fb]
    sm_m = [m_meta_tokens, m_norm_pre_mix, m_norm_post_mix, m_b_in, m_attn_sinks, m_conv_dw_w, m_conv_dw_b, m_conv_ln_g,
            m_conv_ln_b, m_b_conv_proj, m_norm_pre_ffn, m_norm_post_ffn, m_ffn_dw_w, m_ffn_dw_b]
    sm_v = [v_meta_tokens, v_norm_pre_mix, v_norm_post_mix, v_b_in, v_attn_sinks, v_conv_dw_w, v_conv_dw_b, v_conv_ln_g,
            v_conv_ln_b, v_b_conv_proj, v_norm_pre_ffn, v_norm_post_ffn, v_ffn_dw_w, v_ffn_dw_b]
    swap = lambda a: jnp.transpose(a, (1, 0, 2)) if a.ndim == 3 else a
    sm_d, sm_nm, sm_nv = ([swap(o) for o in outs] for outs in
                          _adamw_many(*([swap(a) for a in group] for group in (sm_w, sm_g, sm_m, sm_v)), "adamw_small"))

    order = ["meta_tokens", "norm_pre_mix", "norm_post_mix", "w_in", "b_in", "attn_sinks", "w_attn_proj", "conv_dw_w",
             "conv_dw_b", "conv_ln_g", "conv_ln_b", "w_conv_proj", "b_conv_proj", "w_out", "norm_pre_ffn", "norm_post_ffn",
             "w_up", "ffn_dw_w", "ffn_dw_b", "w_down"]
    small_names = ["meta_tokens", "norm_pre_mix", "norm_post_mix", "b_in", "attn_sinks", "conv_dw_w", "conv_dw_b", "conv_ln_g",
                   "conv_ln_b", "b_conv_proj", "norm_pre_ffn", "norm_post_ffn", "ffn_dw_w", "ffn_dw_b"]
    big_names = ["w_in", "w_up", "w_attn_proj", "w_conv_proj", "w_out", "w_down"]
    table = {}
    for k, nm in enumerate(small_names):
        table[nm] = (sm_g[k], sm_d[k], sm_nm[k], sm_nv[k])
    for k, nm in enumerate(big_names):
        table[nm] = big[k]
    grad_x = grad_x2d[None]
    outs = [loss, grad_x]
    for field in range(4):
        outs += [table[nm][field] for nm in order]
    return tuple(outs)
```

```python
import functools

import jax
import jax.numpy as jnp
from jax import lax
from jax.experimental import pallas as pl
from jax.experimental.pallas import tpu as pltpu

F32 = jnp.float32
BF16 = jnp.bfloat16
MESH = pl.DeviceIdType.MESH

D = 1024
HEAD_DIM = 64
N_META = 16
BLK = 128
PAD = BLK - N_META
CONV_K = 31
FFN = 2816
FFN_K = 3
QKV_W = 1280
IN_W = 5376
ROT_DIM = 16
ROPE_THETA = 500000.0
RMS_EPS = 1e-6
LN_EPS = 1e-5
NEG_INF = -1e30
SCALE = HEAD_DIM ** -0.5
N_DEV = 8

ADAM_LR = 0.001
ADAM_B1 = 0.9
ADAM_B2 = 0.999
ADAM_EPS = 1e-08
ADAM_WD = 0.01
ADAM_STEP = 10

VMEM_BYTES_V7X = 64 * 1024 * 1024
VMEM_LIMIT = VMEM_BYTES_V7X - 8 * 1024 * 1024

NT = (((1,), (1,)), ((), ()))
TN = (((0,), (0,)), ((), ()))
VM = pl.BlockSpec(memory_space=pltpu.VMEM)
ANY = pl.BlockSpec(memory_space=pl.ANY)


def _cparams(*sem):
    return pltpu.CompilerParams(dimension_semantics=sem or None, vmem_limit_bytes=VMEM_LIMIT)


def _row_tile(p):
    return 384 if p % 384 == 0 else 128


def _dot(a, b):
    return jnp.dot(a, b, preferred_element_type=F32)


def _dot_nt(a, b):
    return lax.dot_general(a, b, NT, preferred_element_type=F32)


def _dot_tn(a, b):
    return lax.dot_general(a, b, TN, preferred_element_type=F32)


def _rms(x, g):
    return x * lax.rsqrt(jnp.mean(x * x, axis=-1, keepdims=True) + RMS_EPS) * g


def _lnsilu(x, g, b):
    mu = jnp.mean(x, axis=-1, keepdims=True)
    var = jnp.mean(jnp.square(x - mu), axis=-1, keepdims=True)
    z = (x - mu) * lax.rsqrt(var + LN_EPS) * g + b
    return z * jax.nn.sigmoid(z)


def _rms_bwd(x, g, dy):
    r = lax.rsqrt(jnp.mean(x * x, axis=-1, keepdims=True) + RMS_EPS)
    xn = x * r
    u = dy * g
    dg = jnp.sum(dy * xn, axis=0, keepdims=True)
    dx = r * (u - xn * jnp.mean(u * xn, axis=-1, keepdims=True))
    return dx, dg


def _lnsilu_bwd(x, g, b, dout):
    mu = jnp.mean(x, axis=-1, keepdims=True)
    xc = x - mu
    rs = lax.rsqrt(jnp.mean(xc * xc, axis=-1, keepdims=True) + LN_EPS)
    yh = xc * rs
    z = yh * g + b
    sg = jax.nn.sigmoid(z)
    dz = dout * (sg * (1.0 + z * (1.0 - sg)))
    dg = jnp.sum(dz * yh, axis=0, keepdims=True)
    db = jnp.sum(dz, axis=0, keepdims=True)
    dyh = dz * g
    dx = rs * (dyh - jnp.mean(dyh, axis=-1, keepdims=True) - yh * jnp.mean(dyh * yh, axis=-1, keepdims=True))
    return dx, dg, db


def _rope(v, c, s1, s2):
    return v * c + pltpu.roll(v, BLK - 8, 1) * s1 + pltpu.roll(v, 8, 1) * s2


def _rows(i, tm):
    return i * tm + lax.broadcasted_iota(jnp.int32, (tm, 1), 0)


def _place():
    return lax.axis_index("x"), lax.axis_index("y"), lax.axis_index("c")


def _blk(ref, idx, r, dtype):
    return ref.at[pl.ds(pl.multiple_of(idx * r, 16 if dtype == BF16 else 8), r), :]


class _Gather:
    def __init__(self, arrs):
        self.ins = list(arrs)
        n = len(arrs)
        self.out_shape = [jax.ShapeDtypeStruct((N_DEV * a.shape[0], a.shape[1]), a.dtype) for a in arrs]
        self.scratch = [pltpu.SemaphoreType.DMA((n, 7)), pltpu.SemaphoreType.DMA((n, 7)), pltpu.SemaphoreType.DMA((n,))]

    def _parts(self, ins, outs, sems):
        send_sems, recv_sems, local_sems = sems
        n = len(ins)
        x, y, c = _place()
        me, sibling = (x, y, c), (x, y, 1 - c)
        chips = [(1 - x, y), (x, 1 - y), (1 - x, 1 - y)]

        def rows(a, p):
            return _blk(outs[a], 4 * p[0] + 2 * p[1] + p[2], self.ins[a].shape[0], self.ins[a].dtype)

        def copy(a, k, block, to, src=None):
            return pltpu.make_async_remote_copy(
                src_ref=rows(a, block) if src is None else src, dst_ref=rows(a, block),
                send_sem=send_sems.at[a, k], recv_sem=recv_sems.at[a, k], device_id=to, device_id_type=MESH)

        mine = [pltpu.make_async_copy(ins[a], rows(a, me), local_sems.at[a]) for a in range(n)]
        first = []
        for a in range(n):
            first.append(copy(a, 0, me, sibling, src=ins[a]))
            first += [copy(a, 1 + j, me, (*chip, c), src=ins[a]) for j, chip in enumerate(chips)]
        return n, c, me, sibling, chips, copy, mine, first

    def start(self, ins, outs, sems):
        *_, mine, first = self._parts(ins, outs, sems)
        for cp in mine + first:
            cp.start()

    def middle(self, ins, outs, sems):
        n, c, me, sibling, chips, copy, _, _ = self._parts(ins, outs, sems)
        for j, chip in enumerate(chips):
            for a in range(n):
                copy(a, 1 + j, (*chip, c), me).wait_recv()
                copy(a, 4 + j, (*chip, c), sibling).start()

    def finish(self, ins, outs, sems, middle_done=False):
        if not middle_done:
            self.middle(ins, outs, sems)
        n, c, me, sibling, chips, copy, mine, first = self._parts(ins, outs, sems)
        passed = [copy(a, 4 + j, (*chip, c), sibling) for j, chip in enumerate(chips) for a in range(n)]
        for a in range(n):
            copy(a, 0, sibling, me).wait_recv()
            for j, chip in enumerate(chips):
                copy(a, 4 + j, (*chip, 1 - c), me).wait_recv()
        for cp in first + passed:
            cp.wait_send()
        for cp in mine:
            cp.wait()


class _GatherRelay:
    N_COPY = 13

    def __init__(self, arr):
        self.ins = [arr]
        self.r = arr.shape[0]
        self.out_shape = [jax.ShapeDtypeStruct((N_DEV * self.r, arr.shape[1]), arr.dtype)]
        self.scratch = [pltpu.SemaphoreType.DMA((self.N_COPY,)), pltpu.SemaphoreType.DMA((self.N_COPY,)),
                        pltpu.SemaphoreType.DMA]

    def _parts(self, ins, outs, sems):
        send_sems, recv_sems, local_sem = sems
        x, y, c = _place()
        r, half = self.r, self.r // 2
        out = outs[0]
        me, sib, xn, yn, dg = (x, y, c), (x, y, 1 - c), (1 - x, y, c), (x, 1 - y, c), (1 - x, 1 - y, c)
        sx, sy, sd = (1 - x, y, 1 - c), (x, 1 - y, 1 - c), (1 - x, 1 - y, 1 - c)
        lo, hi = (0, half), (half, half)

        def rows(p, part=(0, r)):
            return out.at[pl.ds(pl.multiple_of((4 * p[0] + 2 * p[1] + p[2]) * r + part[0], 16), part[1]), :]

        def own(part):
            return ins[0].at[pl.ds(part[0], part[1]), :]

        def copy(k, dev_rows, to, src=None):
            return pltpu.make_async_remote_copy(
                src_ref=dev_rows if src is None else src, dst_ref=dev_rows,
                send_sem=send_sems.at[k], recv_sem=recv_sems.at[k], device_id=to, device_id_type=MESH)

        mine = pltpu.make_async_copy(ins[0], rows(me), local_sem)
        first = [copy(0, rows(me), sib, src=ins[0]),
                 copy(1, rows(me, lo), xn, src=own(lo)), copy(3, rows(me, hi), yn, src=own(hi)),
                 copy(2, rows(me, hi), xn, src=own(hi)), copy(4, rows(me, lo), yn, src=own(lo))]
        arrive = {0: rows(sib), 1: rows(xn, lo), 2: rows(xn, hi), 3: rows(yn, hi), 4: rows(yn, lo),
                  5: rows(dg, lo), 6: rows(dg, hi), 7: rows(sx, lo), 8: rows(sx, hi), 9: rows(sy, hi),
                  10: rows(sy, lo), 11: rows(sd, lo), 12: rows(sd, hi)}
        relay = {1: [(5, rows(xn, lo), yn), (7, rows(xn, lo), sib)], 3: [(6, rows(yn, hi), xn), (9, rows(yn, hi), sib)],
                 2: [(8, rows(xn, hi), sib)], 4: [(10, rows(yn, lo), sib)],
                 5: [(11, rows(dg, lo), sib)], 6: [(12, rows(dg, hi), sib)]}
        return copy, mine, first, arrive, relay, me

    def start(self, ins, outs, sems):
        _, mine, first, _, _, _ = self._parts(ins, outs, sems)
        for cp in [mine] + first:
            cp.start()

    def finish(self, ins, outs, sems):
        copy, mine, first, arrive, relay, me = self._parts(ins, outs, sems)
        passed = []
        for k in (1, 3, 2, 4, 5, 6):
            copy(k, arrive[k], me).wait_recv()
            for k2, dev_rows, to in relay[k]:
                fwd = copy(k2, dev_rows, to)
                fwd.start()
                passed.append(fwd)
        for k in (0, 7, 8, 9, 10, 11, 12):
            copy(k, arrive[k], me).wait_recv()
        for cp in first + passed:
            cp.wait_send()
        mine.wait()


FLIPS = [(0, 0, 1), (1, 0, 0), (0, 1, 0), (1, 1, 0), (1, 0, 1), (0, 1, 1), (1, 1, 1)]


class _Scatter:
    def __init__(self, arrs, part=0, nparts=1):
        self.ins = list(arrs)
        self.part, self.nparts = part, nparts
        n = len(arrs)
        self.out_shape = [jax.ShapeDtypeStruct((a.shape[0] // nparts, a.shape[1]), a.dtype) for a in arrs]
        self.scratch = [pltpu.SemaphoreType.DMA((n, 7)), pltpu.SemaphoreType.DMA((n, 7)), pltpu.SemaphoreType.DMA((n,))]

    def _parts(self, ins, outs, sems):
        send_sems, recv_sems, local_sems = sems
        n = len(ins)
        x, y, c = _place()
        me = 4 * x + 2 * y + c

        def flip(v, f):
            return 1 - v if f else v

        def src(a, idx):
            r = self.ins[a].shape[0] // N_DEV
            rs = r // self.nparts
            return ins[a].at[pl.ds(pl.multiple_of(idx * r + self.part * rs, 16), rs), :]

        def dst(a, idx):
            rs = self.ins[a].shape[0] // N_DEV // self.nparts
            return outs[a].at[pl.ds(pl.multiple_of(idx * rs, 16), rs), :]

        mine = [pltpu.make_async_copy(src(a, me), dst(a, me), local_sems.at[a]) for a in range(n)]
        sends, recvs = [], []
        for k, f in enumerate(FLIPS):
            peer = (flip(x, f[0]), flip(y, f[1]), flip(c, f[2]))
            pidx = 4 * peer[0] + 2 * peer[1] + peer[2]
            for a in range(n):
                sends.append(pltpu.make_async_remote_copy(
                    src_ref=src(a, pidx), dst_ref=dst(a, me),
                    send_sem=send_sems.at[a, k], recv_sem=recv_sems.at[a, k], device_id=peer, device_id_type=MESH))
                recvs.append(functools.partial(
                    pltpu.make_async_remote_copy,
                    src_ref=src(a, pidx), dst_ref=dst(a, pidx),
                    send_sem=send_sems.at[a, k], recv_sem=recv_sems.at[a, k], device_id=peer, device_id_type=MESH))
        return mine, sends, recvs

    def start(self, ins, outs, sems):
        mine, sends, _ = self._parts(ins, outs, sems)
        for cp in mine + sends:
            cp.start()

    def finish(self, ins, outs, sems):
        mine, sends, recvs = self._parts(ins, outs, sems)
        for make in recvs:
            make().wait_recv()
        for cp in sends:
            cp.wait_send()
        for cp in mine:
            cp.wait()


N_CHIP = 4


class _SiblingSwap:
    def __init__(self, arr):
        self.ins = [arr]
        self.r = arr.shape[0] // N_DEV
        self.out_shape = [jax.ShapeDtypeStruct((N_CHIP * self.r, arr.shape[1]), arr.dtype)]
        self.scratch = [pltpu.SemaphoreType.DMA((N_CHIP,)), pltpu.SemaphoreType.DMA((N_CHIP,))]

    def _copies(self, ins, outs, sems):
        send_sems, recv_sems = sems
        x, y, c = _place()
        r = self.r
        return [pltpu.make_async_remote_copy(
            src_ref=ins[0].at[pl.ds(pl.multiple_of((2 * j + 1 - c) * r, 16), r), :],
            dst_ref=outs[0].at[pl.ds(j * r, r), :],
            send_sem=send_sems.at[j], recv_sem=recv_sems.at[j], device_id=(x, y, 1 - c), device_id_type=MESH)
            for j in range(N_CHIP)]

    def start(self, ins, outs, sems):
        for cp in self._copies(ins, outs, sems):
            cp.start()

    def finish(self, ins, outs, sems):
        for cp in self._copies(ins, outs, sems):
            cp.wait()


class _ChipScatter:
    def __init__(self, arr):
        self.ins = [arr]
        self.r = arr.shape[0] // N_CHIP
        self.out_shape = [jax.ShapeDtypeStruct(arr.shape, arr.dtype)]
        self.scratch = [pltpu.SemaphoreType.DMA((3,)), pltpu.SemaphoreType.DMA((3,)), pltpu.SemaphoreType.DMA]

    def _parts(self, ins, outs, sems):
        send_sems, recv_sems, local_sem = sems
        x, y, c = _place()
        r = self.r
        my_chip = 2 * x + y

        def rows(ref, j):
            return ref.at[pl.ds(pl.multiple_of(j * r, 16), r), :]

        mine = pltpu.make_async_copy(rows(ins[0], my_chip), rows(outs[0], my_chip), local_sem)
        sends, recvs = [], []
        for k, (fx, fy) in enumerate(((1, 0), (0, 1), (1, 1))):
            px, py = (1 - x if fx else x), (1 - y if fy else y)
            peer_chip = 2 * px + py
            sends.append(pltpu.make_async_remote_copy(
                src_ref=rows(ins[0], peer_chip), dst_ref=rows(outs[0], my_chip),
                send_sem=send_sems.at[k], recv_sem=recv_sems.at[k], device_id=(px, py, c), device_id_type=MESH))
            recvs.append(functools.partial(
                pltpu.make_async_remote_copy,
                src_ref=rows(ins[0], peer_chip), dst_ref=rows(outs[0], peer_chip),
                send_sem=send_sems.at[k], recv_sem=recv_sems.at[k], device_id=(px, py, c), device_id_type=MESH))
        return mine, sends, recvs

    def start(self, ins, outs, sems):
        mine, sends, _ = self._parts(ins, outs, sems)
        for cp in [mine] + sends:
            cp.start()

    def finish(self, ins, outs, sems):
        mine, sends, recvs = self._parts(ins, outs, sems)
        for make in recvs:
            make().wait_recv()
        for cp in sends:
            cp.wait_send()
        mine.wait()


def _pair_add(partial, recv):
    r = recv.shape[0] // N_CHIP
    cols = recv.shape[1]
    tr = r // 2 if (r // 2) % 16 == 0 else r
    steps = r // tr
    core = lax.axis_index("c").astype(jnp.int32).reshape(1)

    def body(c_ref, p_ref, s_ref, o_ref):
        o_ref[...] = (p_ref[...].astype(F32) + s_ref[...].astype(F32)).astype(BF16)

    spec = pl.BlockSpec((tr, cols), lambda j, i, c_ref: (j * steps + i, 0))
    return pl.pallas_call(
        body, name="pair_add",
        grid_spec=pltpu.PrefetchScalarGridSpec(
            num_scalar_prefetch=1, grid=(N_CHIP, steps),
            in_specs=[pl.BlockSpec((tr, cols), lambda j, i, c_ref: ((2 * j + c_ref[0]) * steps + i, 0)), spec],
            out_specs=spec),
        out_shape=jax.ShapeDtypeStruct(recv.shape, BF16),
        compiler_params=_cparams("parallel", "parallel"),
    )(core, partial, recv)


class _Both:
    def __init__(self, a, b):
        self.a, self.b = a, b
        self.ins = a.ins + b.ins
        self.out_shape = a.out_shape + b.out_shape
        self.scratch = a.scratch + b.scratch

    def _split(self, ins, outs, sems):
        ni, no, ns = len(self.a.ins), len(self.a.out_shape), len(self.a.scratch)
        return (ins[:ni], outs[:no], sems[:ns]), (ins[ni:], outs[no:], sems[ns:])

    def start(self, ins, outs, sems):
        ra, rb = self._split(ins, outs, sems)
        self.a.start(*ra)
        self.b.start(*rb)

    def finish(self, ins, outs, sems):
        ra, rb = self._split(ins, outs, sems)
        self.a.finish(*ra)
        self.b.finish(*rb)


def _exchange(comm, name):
    n, m = len(comm.ins), len(comm.out_shape)

    def body(*refs):
        ins, outs, sems = refs[:n], refs[n:n + m], refs[n + m:]
        comm.start(ins, outs, sems)
        comm.finish(ins, outs, sems)

    return pl.pallas_call(
        body, name=name, out_shape=comm.out_shape, in_specs=[ANY] * n, out_specs=[ANY] * m, scratch_shapes=comm.scratch,
    )(*comm.ins)


def _call(body, *, name, grid, in_specs, out_specs, out_shape, args, scratch=(), sem="parallel", comm=None,
          comm_mid=None):
    if comm is None:
        outs = pl.pallas_call(
            body, name=name, grid=grid, in_specs=list(in_specs), out_specs=list(out_specs), out_shape=list(out_shape),
            scratch_shapes=list(scratch), compiler_params=_cparams(sem))(*args)
        return outs, []
    n_in, n_out, n_sc = len(in_specs), len(out_specs), len(scratch)
    n_ci, n_co = len(comm.ins), len(comm.out_shape)
    last = grid[0] - 1

    def fused(*refs):
        ins, refs = refs[:n_in], refs[n_in:]
        c_ins, refs = refs[:n_ci], refs[n_ci:]
        outs, refs = refs[:n_out], refs[n_out:]
        c_outs, refs = refs[:n_co], refs[n_co:]
        sc, c_sems = refs[:n_sc], refs[n_sc:]
        step = pl.program_id(0)

        @pl.when(step == 0)
        def _():
            comm.start(c_ins, c_outs, c_sems)

        body(*ins, *outs, *sc)

        if comm_mid is not None:
            @pl.when(step == comm_mid)
            def _():
                comm.middle(c_ins, c_outs, c_sems)

        @pl.when(step == last)
        def _():
            if comm_mid is not None:
                comm.finish(c_ins, c_outs, c_sems, middle_done=True)
            else:
                comm.finish(c_ins, c_outs, c_sems)

    outs = pl.pallas_call(
        fused, name=name, grid=grid, in_specs=list(in_specs) + [ANY] * n_ci, out_specs=list(out_specs) + [ANY] * n_co,
        out_shape=list(out_shape) + comm.out_shape, scratch_shapes=list(scratch) + comm.scratch,
        compiler_params=_cparams("arbitrary"))(*args, *comm.ins)
    return outs[:n_out], outs[n_out:]


def _token_specs(tm):
    k = tm // BLK
    return [pl.BlockSpec((BLK, D), functools.partial(lambda i, t: (jnp.maximum(k * i + t - 1, 0), 0), t=t)) for t in range(k)]


def _in_proj(x2d, meta, gain, w_int, b_in, tabs, comm=None):
    p = x2d.shape[0] + BLK
    tm = _row_tile(p)
    k = tm // BLK

    def body(*refs):
        x_refs = refs[:k]
        m_ref, g_ref, w_ref, b_ref, t_ref, h_ref, n1_ref, q_ref, kv_ref, ag_ref, gt_ref = refs[k:]
        i = pl.program_id(0)
        head = jnp.concatenate([jnp.zeros((PAD, D), F32), m_ref[...]], axis=0)
        first = jnp.where(i == 0, head, x_refs[0][...])
        h = jnp.concatenate([first] + [r[...] for r in x_refs[1:]], axis=0) if k > 1 else first
        h_ref[...] = h
        n = _rms(h, g_ref[...]).astype(BF16)
        n1_ref[...] = n
        c, s1, s2 = t_ref[:, 0:128], t_ref[:, 128:256], t_ref[:, 256:384]

        def mm(c0, w):
            return _dot_nt(n, w_ref[c0:c0 + w, :]) + b_ref[:, c0:c0 + w]

        for j in range(4):
            acc = mm(256 * j, 256)
            for t in range(2):
                lo = 256 * j + 128 * t
                q_ref[:, lo:lo + 128] = (_rope(acc[:, 128 * t:128 * (t + 1)], c, s1, s2) * SCALE).astype(BF16)
        acc = mm(1024, 256)
        kv_ref[:, 0:128] = _rope(acc[:, 0:128], c, s1, s2).astype(BF16)
        kv_ref[:, 128:256] = acc[:, 128:256].astype(BF16)
        for j in range(8):
            ag_ref[:, 256 * j:256 * (j + 1)] = mm(QKV_W + 256 * j, 256).astype(BF16)
        for j in range(8):
            gt_ref[:, 256 * j:256 * (j + 1)] = mm(QKV_W + 2048 + 256 * j, 256).astype(BF16)

    def row(w):
        return pl.BlockSpec((tm, w), lambda i: (i, 0))

    return _call(
        body, name="in_proj", grid=(p // tm,),
        in_specs=_token_specs(tm) + [VM, VM, VM, VM, row(384)],
        out_specs=[row(D), row(D), row(D), row(256), row(2048), row(2048)],
        out_shape=[jax.ShapeDtypeStruct((p, D), F32)] + [jax.ShapeDtypeStruct((p, w), BF16) for w in (D, D, 256, 2048, 2048)],
        args=(x2d,) * k + (meta, gain, w_int, b_in, tabs), comm=comm,
        comm_mid=None if comm is None else (3 * (p // tm)) // 4)


N_KEY = 2 * BLK + N_META


def _attn_setup(n, h, q_ref, km_ref, kp_ref, kc_ref):
    lo = lax.broadcasted_iota(jnp.int32, (BLK, BLK), 1) < HEAD_DIM
    lok = lax.broadcasted_iota(jnp.int32, (N_KEY, BLK), 1) < HEAD_DIM

    def dup(lanes):
        cat = jnp.concatenate([kp_ref[:, lanes], kc_ref[:, lanes], km_ref[PAD:BLK, lanes]], axis=0).astype(F32)
        rolled = pltpu.roll(cat, HEAD_DIM, 1)
        return (jnp.where(lok, cat, rolled) if h == 0 else jnp.where(lok, rolled, cat)).astype(BF16)

    k2 = dup(slice(0, 128))
    v2 = dup(slice(128, 256))
    qs = _stack_heads(q_ref, h, lo)

    kr = lax.broadcasted_iota(jnp.int32, (BLK, BLK), 0)
    tq = BLK * n + lax.broadcasted_iota(jnp.int32, (BLK, BLK), 1) - PAD
    t_p = BLK * (n - 1) + kr - PAD
    t_c = BLK * n + kr - PAD
    ok_p = jnp.logical_and(t_p >= N_META, tq - t_p < BLK)
    ok_c = jnp.logical_and(t_c >= N_META, t_c <= tq)
    ok_m = lax.broadcasted_iota(jnp.int32, (N_META, BLK), 0) <= BLK * n + lax.broadcasted_iota(jnp.int32, (N_META, BLK), 1) - PAD
    bias = jnp.concatenate([jnp.where(ok, 0.0, NEG_INF).astype(F32) for ok in (ok_p, ok_c, ok_m)], axis=0)
    return qs, k2, v2, bias, lok


def _attn_head(s, bias, sink):
    s = s + bias
    m = jnp.maximum(jnp.max(s, axis=0, keepdims=True), sink)
    e = jnp.exp(s - m)
    es = jnp.exp(sink - m)
    inv = 1.0 / (jnp.sum(e, axis=0, keepdims=True) + es)
    return e * inv, es * inv


def _stack_heads(ref, h, lo):
    pieces = []
    for jp in range(4):
        v = ref[:, BLK * (4 * h + jp):BLK * (4 * h + jp + 1)]
        zero = jnp.zeros_like(v)
        pieces += [jnp.where(lo, v, zero), jnp.where(lo, zero, v)]
    return jnp.concatenate(pieces, axis=0)


def _unstack_heads(v, jp, lo):
    return jnp.where(lo, v[256 * jp:256 * jp + 128], v[256 * jp + 128:256 * jp + 256])


def _attn_fwd(q, kv, sinks, comm=None):
    p = q.shape[0]
    nb = p // BLK

    def body(q_ref, km_ref, kp_ref, kc_ref, sink_ref, o_ref):
        n = pl.program_id(0)
        lo = lax.broadcasted_iota(jnp.int32, (BLK, BLK), 1) < HEAD_DIM
        for h in range(2):
            qs, k2, v2, bias, _ = _attn_setup(n, h, q_ref, km_ref, kp_ref, kc_ref)
            st = _dot_nt(k2, qs)
            pt = jnp.concatenate(
                [_attn_head(st[:, BLK * g:BLK * (g + 1)], bias, sink_ref[0, 8 * h + g])[0].astype(BF16) for g in range(8)],
                axis=1)
            o = _dot_tn(pt, v2)
            for jp in range(4):
                o_ref[:, BLK * (4 * h + jp):BLK * (4 * h + jp + 1)] = _unstack_heads(o, jp, lo).astype(BF16)

    return _call(
        body, name="attn_fwd", grid=(nb,),
        in_specs=[pl.BlockSpec((BLK, D), lambda i: (i, 0)),
                  pl.BlockSpec((BLK, 256), lambda i: (0, 0)),
                  pl.BlockSpec((BLK, 256), lambda i: (jnp.maximum(i - 1, 0), 0)),
                  pl.BlockSpec((BLK, 256), lambda i: (i, 0)),
                  pl.BlockSpec(memory_space=pltpu.SMEM)],
        out_specs=[pl.BlockSpec((BLK, D), lambda i: (i, 0))],
        out_shape=[jax.ShapeDtypeStruct((p, D), BF16)],
        args=(q, kv, kv, kv, sinks), comm=comm)


def _conv31_fwd(ag, w32, b, comm=None):
    p = ag.shape[0]
    nch = p // BLK

    def body(a_ref, g_ref, w_ref, b_ref, o_ref, gp):
        gp[0:32, :] = jnp.zeros((32, BLK), F32)
        for ci in range(nch):
            r0 = BLK * ci
            glu = a_ref[r0:r0 + BLK, :].astype(F32) * jax.nn.sigmoid(g_ref[r0:r0 + BLK, :].astype(F32))
            if ci == 0:
                glu = jnp.where(_rows(0, BLK) >= PAD, glu, 0.0)
            gp[32 + r0:32 + r0 + BLK, :] = glu
        for ci in range(nch):
            r0 = BLK * ci
            acc = jnp.broadcast_to(b_ref[...], (BLK, BLK))
            for j in range(CONV_K):
                acc = acc + w_ref[j:j + 1, :] * gp[r0 + j + 2:r0 + j + 2 + BLK, :]
            o_ref[r0:r0 + BLK, :] = acc

    return _call(
        body, name="conv31_fwd", grid=(D // BLK,),
        in_specs=[pl.BlockSpec((p, BLK), lambda j: (0, j)), pl.BlockSpec((p, BLK), lambda j: (0, 8 + j)),
                  pl.BlockSpec((32, BLK), lambda j: (0, j)), pl.BlockSpec((1, BLK), lambda j: (0, j))],
        out_specs=[pl.BlockSpec((p, BLK), lambda j: (0, j))],
        out_shape=[jax.ShapeDtypeStruct((p, D), F32)],
        scratch=[pltpu.VMEM((p + 32, BLK), F32)],
        args=(ag, ag, w32, b), comm=comm, comm_mid=None if comm is None else (3 * (D // BLK)) // 4)


def _mixer_fwd(ao, c0, gates, h0p, wa, wc, wo, vecs):
    p = ao.shape[0]
    tm = _row_tile(p)

    def body(ao_ref, c0_ref, gt_ref, h_ref, wa_ref, wc_ref, wo_ref, v_ref,
             c1_ref, at_ref, cv_ref, mg_ref, mix_ref, h1_ref, n2_ref):
        i = pl.program_id(0)
        c1 = _lnsilu(c0_ref[...], v_ref[0:1, :], v_ref[1:2, :]).astype(BF16)
        c1_ref[...] = c1
        attn = _dot(ao_ref[...], wa_ref[...])
        conv = _dot(c1, wc_ref[...]) + v_ref[2:3, :]
        at_ref[...] = attn.astype(BF16)
        cv_ref[...] = conv.astype(BF16)
        merged = (jax.nn.sigmoid(gt_ref[:, 0:D].astype(F32)) * attn
                  + jax.nn.sigmoid(gt_ref[:, D:2 * D].astype(F32)) * conv).astype(BF16)
        mg_ref[...] = merged
        mix = _dot(merged, wo_ref[...])
        mix_ref[...] = mix
        h1 = jnp.where(_rows(i, tm) >= PAD, h_ref[...] + _rms(mix, v_ref[3:4, :]), 0.0)
        h1_ref[...] = h1
        n2_ref[...] = _rms(h1, v_ref[4:5, :]).astype(BF16)

    def row(w):
        return pl.BlockSpec((tm, w), lambda i: (i, 0))

    return pl.pallas_call(
        body, name="mixer_fwd", grid=(p // tm,),
        in_specs=[row(D), row(D), row(2 * D), row(D), VM, VM, VM, VM],
        out_specs=[row(D)] * 7,
        out_shape=[jax.ShapeDtypeStruct((p, D), t) for t in (BF16, BF16, BF16, BF16, F32, F32, BF16)],
        compiler_params=_cparams("parallel"),
    )(ao, c0, gates, h0p, wa, wc, wo, vecs)


def _mm_nt(a, w_t, name):
    p, k = a.shape
    n = w_t.shape[0]
    tm = _row_tile(p)
    ch = 512

    def body(a_ref, w_ref, o_ref):
        a_v = a_ref[...]
        for c0 in range(0, n, ch):
            o_ref[:, c0:c0 + ch] = _dot_nt(a_v, w_ref[c0:c0 + ch, :]).astype(BF16)

    return pl.pallas_call(
        body, name=name, grid=(p // tm,),
        in_specs=[pl.BlockSpec((tm, k), lambda i: (i, 0)), VM],
        out_specs=pl.BlockSpec((tm, n), lambda i: (i, 0)),
        out_shape=jax.ShapeDtypeStruct((p, n), BF16),
        compiler_params=_cparams("parallel"),
    )(a, w_t)


def _conv3(xp_ref, w_ref, r0):
    return (w_ref[0:1, :] * xp_ref[r0 + 6:r0 + 6 + BLK, :] + w_ref[1:2, :] * xp_ref[r0 + 7:r0 + 7 + BLK, :]
            + w_ref[2:3, :] * xp_ref[r0 + 8:r0 + 8 + BLK, :])


def _ffn_slab_specs(p):
    ncol = FFN // BLK
    return [pl.BlockSpec((p, BLK), lambda j: (0, j)), pl.BlockSpec((p, BLK), lambda j: (0, ncol + j)),
            pl.BlockSpec((FFN_K, BLK), lambda j: (0, j)), pl.BlockSpec((FFN_K, BLK), lambda j: (0, ncol + j)),
            pl.BlockSpec((1, BLK), lambda j: (0, j)), pl.BlockSpec((1, BLK), lambda j: (0, ncol + j))]


def _fill_shifted(dst, src_ref, nch):
    dst[0:8, :] = jnp.zeros((8, BLK), F32)
    for ci in range(nch):
        dst[8 + BLK * ci:8 + BLK * (ci + 1), :] = src_ref[BLK * ci:BLK * (ci + 1), :].astype(F32)


def _ffn_act(u0, fw, fb):
    p = u0.shape[0]
    nch = p // BLK

    def body(g_ref, v_ref, wg_ref, wv_ref, bg_ref, bv_ref, o_ref, dv_ref, dg_ref, xg, xv):
        _fill_shifted(xg, g_ref, nch)
        _fill_shifted(xv, v_ref, nch)
        for ci in range(nch):
            r0 = BLK * ci
            ug = _conv3(xg, wg_ref, r0) + bg_ref[...]
            uv = _conv3(xv, wv_ref, r0) + bv_ref[...]
            sg = jax.nn.sigmoid(ug)
            silu = ug * sg
            o_ref[r0:r0 + BLK, :] = (silu * uv).astype(BF16)
            dv_ref[r0:r0 + BLK, :] = silu.astype(BF16)
            dg_ref[r0:r0 + BLK, :] = (uv * (sg * (1.0 + ug * (1.0 - sg)))).astype(BF16)

    slab = pl.BlockSpec((p, BLK), lambda j: (0, j))
    return pl.pallas_call(
        body, name="ffn_act", grid=(FFN // BLK,),
        in_specs=_ffn_slab_specs(p),
        out_specs=[slab] * 3,
        out_shape=[jax.ShapeDtypeStruct((p, FFN), BF16)] * 3,
        scratch_shapes=[pltpu.VMEM((p + 8, BLK), F32)] * 2,
        compiler_params=_cparams("parallel"),
    )(u0, u0, fw, fw, fb, fb)


def _ffn_down_loss(act, wd, h1, tgt, gain):
    p = act.shape[0]
    tm = _row_tile(p)
    k = tm // BLK

    def body(*refs):
        a_ref, w_ref, h_ref = refs[:3]
        t_refs = refs[3:3 + k]
        g_ref, df_ref, da_ref, dy_ref, acc_ref = refs[3 + k:]
        i = pl.program_id(0)

        @pl.when(i == 0)
        def _():
            acc_ref[...] = jnp.zeros_like(acc_ref)

        ffn = _dot(a_ref[...], w_ref[...])
        t = jnp.concatenate([t_ref[...] for t_ref in t_refs], axis=0) if k > 1 else t_refs[0][...]
        diff = jnp.where(_rows(i, tm) >= BLK, h_ref[...] + _rms(ffn, g_ref[...]) - t, 0.0)
        dy = diff * (1.0 / D)
        dffn, dg = _rms_bwd(ffn, g_ref[...], dy)
        acc_ref[0:1, :] += dg
        acc_ref[1:2, :] += jnp.sum(diff * diff, axis=0, keepdims=True) * (0.5 / D)
        dy_ref[...] = dy
        dfb = dffn.astype(BF16)
        df_ref[...] = dfb
        for c0 in range(0, FFN, 256):
            da_ref[:, c0:c0 + 256] = _dot_nt(dfb, w_ref[c0:c0 + 256, :]).astype(BF16)

    def row(w):
        return pl.BlockSpec((tm, w), lambda i: (i, 0))

    return pl.pallas_call(
        body, name="ffn_down_loss", grid=(p // tm,),
        in_specs=[row(FFN), VM, row(D)] + _token_specs(tm) + [VM],
        out_specs=[row(D), row(FFN), row(D), pl.BlockSpec((8, D), lambda i: (0, 0))],
        out_shape=[jax.ShapeDtypeStruct((p, D), BF16), jax.ShapeDtypeStruct((p, FFN), BF16),
                   jax.ShapeDtypeStruct((p, D), F32), jax.ShapeDtypeStruct((8, D), F32)],
        compiler_params=_cparams("arbitrary"),
    )(act, wd, h1, *([tgt] * k), gain)


def _mm_tn(pieces, b, name, col_sums=False, comm=None):
    p, n = b.shape
    tk = 256
    nblk = [a.shape[1] // tk for a in pieces]
    offs = [sum(nblk[:q]) for q in range(len(pieces))]
    total = sum(nblk)
    npc = len(pieces)

    def body(*refs):
        a_refs, b_ref, o_ref = refs[:npc], refs[npc], refs[npc + 1]
        i = pl.program_id(0)
        for q, a_ref in enumerate(a_refs):
            @pl.when(jnp.logical_and(i >= offs[q], i < offs[q] + nblk[q]))
            def _(a_ref=a_ref):
                a_v = a_ref[...]
                o_ref[...] = _dot_tn(a_v, b_ref[...]).astype(BF16)
                if col_sums:
                    refs[npc + 2][...] = jnp.sum(a_v.astype(F32), axis=0, keepdims=True)

    def a_spec(q):
        return pl.BlockSpec((p, tk), lambda i: (0, jnp.clip(i - offs[q], 0, nblk[q] - 1)))

    out_specs = [pl.BlockSpec((tk, n), lambda i: (i, 0))]
    out_shape = [jax.ShapeDtypeStruct((total * tk, n), BF16)]
    if col_sums:
        out_specs.append(pl.BlockSpec((1, tk), lambda i: (0, i)))
        out_shape.append(jax.ShapeDtypeStruct((1, total * tk), F32))
    res, sent = _call(
        body, name=name, grid=(total,),
        in_specs=[a_spec(q) for q in range(npc)] + [VM],
        out_specs=out_specs, out_shape=out_shape, args=(*pieces, b), comm=comm,
        comm_mid=(3 * total) // 4 if hasattr(comm, "middle") else None)
    res = res if col_sums else res[0]
    return res if comm is None else (res, sent)


def _ffn_act_bwd(u0, dact, dact_dg, dact_dv, fw, act, dffn, comm=None):
    p = u0.shape[0]
    nch = p // BLK
    ncol = FFN // BLK

    def body(g_ref, v_ref, wg_ref, wv_ref, da_ref, lg_ref, lv_ref, act_ref, df_ref,
             dg_ref, dv_ref, gwg_ref, gwv_ref, gbg_ref, gbv_ref, gwd_ref, eg, ev):
        gwd_ref[...] = _dot_tn(act_ref[...], df_ref[...]).astype(BF16)
        eg[p:p + 8, :] = jnp.zeros((8, BLK), F32)
        ev[p:p + 8, :] = jnp.zeros((8, BLK), F32)
        for ci in range(nch):
            r0 = BLK * ci
            d = da_ref[r0:r0 + BLK, :].astype(F32)
            eg[r0:r0 + BLK, :] = d * lg_ref[r0:r0 + BLK, :].astype(F32)
            ev[r0:r0 + BLK, :] = d * lv_ref[r0:r0 + BLK, :].astype(F32)
        def fold(v):
            return jnp.sum(v.reshape(BLK // 8, 8, BLK), axis=0)

        for e_s, x_ref, w_ref, d_ref, gw_ref, gb_ref in ((eg, g_ref, wg_ref, dg_ref, gwg_ref, gbg_ref),
                                                        (ev, v_ref, wv_ref, dv_ref, gwv_ref, gbv_ref)):
            sums = [jnp.zeros((8, BLK), F32) for _ in range(FFN_K + 1)]
            for ci in range(nch):
                r0 = BLK * ci
                es = [e_s[r0 + t:r0 + t + BLK, :] for t in range(FFN_K)]
                du = w_ref[2:3, :] * es[0] + w_ref[1:2, :] * es[1] + w_ref[0:1, :] * es[2]
                if ci == 0:
                    du = jnp.where(_rows(0, BLK) >= PAD, du, 0.0)
                d_ref[r0:r0 + BLK, :] = du.astype(BF16)
                x = x_ref[r0:r0 + BLK, :].astype(F32)
                for j in range(FFN_K):
                    sums[j] = sums[j] + fold(es[FFN_K - 1 - j] * x)
                sums[FFN_K] = sums[FFN_K] + fold(es[0])
            for j in range(FFN_K):
                gw_ref[j:j + 1, :] = jnp.sum(sums[j], axis=0, keepdims=True)
            gb_ref[...] = jnp.sum(sums[FFN_K], axis=0, keepdims=True)

    slab = pl.BlockSpec((p, BLK), lambda j: (0, j))
    wspec = pl.BlockSpec((FFN_K, BLK), lambda j: (0, j))
    bspec = pl.BlockSpec((1, BLK), lambda j: (0, j))
    return _call(
        body, name="ffn_act_bwd", grid=(ncol,),
        in_specs=_ffn_slab_specs(p)[:4] + [slab] * 4 + [VM],
        out_specs=[slab, slab, wspec, wspec, bspec, bspec, pl.BlockSpec((BLK, D), lambda j: (j, 0))],
        out_shape=[jax.ShapeDtypeStruct((p, FFN), BF16)] * 2 + [jax.ShapeDtypeStruct((FFN_K, FFN), F32)] * 2
        + [jax.ShapeDtypeStruct((1, FFN), F32)] * 2 + [jax.ShapeDtypeStruct((FFN, D), BF16)],
        scratch=[pltpu.VMEM((p + 8, BLK), F32)] * 2,
        args=(u0, u0, fw, fw, dact, dact_dg, dact_dv, act, dffn), comm=comm)


def _ffn_in_bwd(dug, duv, w_upt, h1, dy, gain, comm=None):
    p = h1.shape[0]
    tm = _row_tile(p)

    def body(dg_ref, dv_ref, w_ref, h_ref, dy_ref, g_ref, o_ref, acc_ref):
        i = pl.program_id(0)

        @pl.when(i == 0)
        def _():
            acc_ref[...] = jnp.zeros_like(acc_ref)

        dn = _dot(dg_ref[...], w_ref[0:FFN, :]) + _dot(dv_ref[...], w_ref[FFN:2 * FFN, :])
        dh, dg = _rms_bwd(h_ref[...], g_ref[...], dn)
        o_ref[...] = dy_ref[...] + dh
        acc_ref[0:1, :] += dg

    def row(w):
        return pl.BlockSpec((tm, w), lambda i: (i, 0))

    return _call(
        body, name="ffn_in_bwd", grid=(p // tm,),
        in_specs=[row(FFN), row(FFN), VM, row(D), row(D), VM],
        out_specs=[row(D), pl.BlockSpec((8, D), lambda i: (0, 0))],
        out_shape=[jax.ShapeDtypeStruct((p, D), F32), jax.ShapeDtypeStruct((8, D), F32)],
        sem="arbitrary", args=(dug, duv, w_upt, h1, dy, gain), comm=comm)


def _mixer_bwd(dh1, mix, attn, conv, gates, c0, wa, wc, wo, vecs, comm=None):
    p = dh1.shape[0]
    tm = _row_tile(p)

    def body(dh_ref, mix_ref, at_ref, cv_ref, gt_ref, c0_ref, wa_ref, wc_ref, wo_ref, v_ref,
             dmix_ref, dat_ref, dcv_ref, dgt_ref, dao_ref, dc0_ref, acc_ref):
        i = pl.program_id(0)

        @pl.when(i == 0)
        def _():
            acc_ref[...] = jnp.zeros_like(acc_ref)

        dmix, dgp = _rms_bwd(mix_ref[...], v_ref[3:4, :], dh_ref[...])
        dmix = dmix.astype(BF16)
        dmix_ref[...] = dmix
        dmg = _dot_nt(dmix, wo_ref[...])
        sa = jax.nn.sigmoid(gt_ref[:, 0:D].astype(F32))
        sc = jax.nn.sigmoid(gt_ref[:, D:2 * D].astype(F32))
        dat = dmg * sa
        dcv = dmg * sc
        dgt_ref[:, 0:D] = (dmg * at_ref[...].astype(F32) * sa * (1.0 - sa)).astype(BF16)
        dgt_ref[:, D:2 * D] = (dmg * cv_ref[...].astype(F32) * sc * (1.0 - sc)).astype(BF16)
        datb = dat.astype(BF16)
        dcvb = dcv.astype(BF16)
        dat_ref[...] = datb
        dcv_ref[...] = dcvb
        dao_ref[...] = _dot_nt(datb, wa_ref[...]).astype(BF16)
        dc1 = _dot_nt(dcvb, wc_ref[...])
        dc0, dlg, dlb = _lnsilu_bwd(c0_ref[...], v_ref[0:1, :], v_ref[1:2, :], dc1)
        dc0_ref[...] = dc0
        acc_ref[0:1, :] += dgp
        acc_ref[1:2, :] += jnp.sum(dcv, axis=0, keepdims=True)
        acc_ref[2:3, :] += dlg
        acc_ref[3:4, :] += dlb

    def row(w):
        return pl.BlockSpec((tm, w), lambda i: (i, 0))

    return _call(
        body, name="mixer_bwd", grid=(p // tm,),
        in_specs=[row(D), row(D), row(D), row(D), row(2 * D), row(D), VM, VM, VM, VM],
        out_specs=[row(D), row(D), row(D), row(2 * D), row(D), row(D), pl.BlockSpec((8, D), lambda i: (0, 0))],
        out_shape=[jax.ShapeDtypeStruct((p, D), BF16)] * 3 + [jax.ShapeDtypeStruct((p, 2 * D), BF16),
                                                             jax.ShapeDtypeStruct((p, D), BF16),
                                                             jax.ShapeDtypeStruct((p, D), F32),
                                                             jax.ShapeDtypeStruct((8, D), F32)],
        sem="arbitrary", args=(dh1, mix, attn, conv, gates, c0, wa, wc, wo, vecs), comm=comm)


def _conv31_bwd(ag, dc0, w32, tn_pairs, comm=None):
    p = ag.shape[0]
    nch = p // BLK
    npair = len(tn_pairs)

    def body(*refs):
        a_ref, g_ref, dc_ref, w_ref = refs[:4]
        tn_a, tn_b = refs[4:4 + npair], refs[4 + npair:4 + 2 * npair]
        da_ref, dg_ref, gw_ref, gb_ref = refs[4 + 2 * npair:8 + 2 * npair]
        tn_o = refs[8 + 2 * npair:8 + 3 * npair]
        gp, dp = refs[8 + 3 * npair:]
        for ta, tb, to in zip(tn_a, tn_b, tn_o):
            to[...] = _dot_tn(ta[...], tb[...]).astype(BF16)
        gp[0:32, :] = jnp.zeros((32, BLK), F32)
        dp[p:p + 32, :] = jnp.zeros((32, BLK), F32)
        bsum = jnp.zeros((BLK, BLK), F32)
        for ci in range(nch):
            r0 = BLK * ci
            glu = a_ref[r0:r0 + BLK, :].astype(F32) * jax.nn.sigmoid(g_ref[r0:r0 + BLK, :].astype(F32))
            if ci == 0:
                glu = jnp.where(_rows(0, BLK) >= PAD, glu, 0.0)
            gp[32 + r0:32 + r0 + BLK, :] = glu
            d = dc_ref[r0:r0 + BLK, :]
            dp[r0:r0 + BLK, :] = d
            bsum = bsum + d
        gb_ref[...] = jnp.sum(bsum, axis=0, keepdims=True)
        for ci in range(nch):
            r0 = BLK * ci
            acc = jnp.zeros((BLK, BLK), F32)
            for j in range(CONV_K):
                acc = acc + w_ref[j:j + 1, :] * dp[r0 + 30 - j:r0 + 30 - j + BLK, :]
            if ci == 0:
                acc = jnp.where(_rows(0, BLK) >= PAD, acc, 0.0)
            a = a_ref[r0:r0 + BLK, :].astype(F32)
            sg = jax.nn.sigmoid(g_ref[r0:r0 + BLK, :].astype(F32))
            da_ref[r0:r0 + BLK, :] = (acc * sg).astype(BF16)
            dg_ref[r0:r0 + BLK, :] = (acc * a * sg * (1.0 - sg)).astype(BF16)
        sub = BLK // 2
        accs = [jnp.zeros((8, BLK), F32) for _ in range(CONV_K)]
        for r0 in range(0, p, sub):
            d = dp[r0:r0 + sub, :]
            for j in range(CONV_K):
                prod = d * gp[r0 + j + 2:r0 + j + 2 + sub, :]
                accs[j] = accs[j] + jnp.sum(prod.reshape(sub // 8, 8, BLK), axis=0)
        for j in range(CONV_K):
            gw_ref[j:j + 1, :] = jnp.sum(accs[j], axis=0, keepdims=True)
        gw_ref[CONV_K:32, :] = jnp.zeros((32 - CONV_K, BLK), F32)

    slab = pl.BlockSpec((p, BLK), lambda j: (0, j))
    return _call(
        body, name="conv31_bwd", grid=(D // BLK,),
        in_specs=[slab, pl.BlockSpec((p, BLK), lambda j: (0, 8 + j)), slab, pl.BlockSpec((32, BLK), lambda j: (0, j))]
        + [slab] * npair + [VM] * npair,
        out_specs=[slab, slab, pl.BlockSpec((32, BLK), lambda j: (0, j)), pl.BlockSpec((1, BLK), lambda j: (0, j))]
        + [pl.BlockSpec((BLK, D), lambda j: (j, 0))] * npair,
        out_shape=[jax.ShapeDtypeStruct((p, D), BF16)] * 2 + [jax.ShapeDtypeStruct((32, D), F32),
                                                             jax.ShapeDtypeStruct((1, D), F32)]
        + [jax.ShapeDtypeStruct((D, D), BF16)] * npair,
        scratch=[pltpu.VMEM((p + 32, BLK), F32)] * 2,
        args=(ag, ag, dc0, w32, *[a for a, _ in tn_pairs], *[b for _, b in tn_pairs]), comm=comm)


def _attn_bwd(q, kv, dao, sinks, tabs, comm=None):
    p = q.shape[0]
    nb = p // BLK

    def body(q_ref, km_ref, kp_ref, kc_ref, do_ref, sink_ref, t_ref, dqkv_ref, dsink_ref, carry, macc):
        i = pl.program_id(0)
        n = nb - 1 - i

        @pl.when(i == 0)
        def _():
            carry[...] = jnp.zeros_like(carry)
            macc[...] = jnp.zeros_like(macc)
            dsink_ref[...] = jnp.zeros_like(dsink_ref)

        lo = lax.broadcasted_iota(jnp.int32, (BLK, BLK), 1) < HEAD_DIM
        lane8 = lax.broadcasted_iota(jnp.int32, (8, BLK), 1)
        c, s1, s2 = t_ref[:, 0:128], -t_ref[:, 128:256], -t_ref[:, 256:384]
        dk = jnp.zeros((N_KEY, BLK), F32)
        dv = jnp.zeros((N_KEY, BLK), F32)
        for h in range(2):
            qs, k2, v2, bias, lok = _attn_setup(n, h, q_ref, km_ref, kp_ref, kc_ref)
            dos = _stack_heads(do_ref, h, lo)
            st = _dot_nt(k2, qs)
            dpt = _dot_nt(v2, dos)
            p_parts, ds_parts = [], []
            for g in range(8):
                cols = slice(BLK * g, BLK * (g + 1))
                pn, ps = _attn_head(st[:, cols], bias, sink_ref[0, 8 * h + g])
                dp = dpt[:, cols]
                delta = jnp.sum(pn * dp, axis=0, keepdims=True)
                ds_parts.append((pn * (dp - delta)).astype(BF16))
                p_parts.append(pn.astype(BF16))
                dsk = -jnp.sum(ps * delta, axis=1, keepdims=True)
                dsink_ref[...] += jnp.where(lane8 == 8 * h + g, dsk, 0.0)
            dst = jnp.concatenate(ds_parts, axis=1)
            pt = jnp.concatenate(p_parts, axis=1)
            dq = _dot_tn(dst, k2)
            for jp in range(4):
                lo_c = BLK * (4 * h + jp)
                dqkv_ref[:, lo_c:lo_c + BLK] = (_rope(_unstack_heads(dq, jp, lo), c, s1, s2) * SCALE).astype(BF16)
            dk2 = _dot(dst, qs)
            dv2 = _dot(pt, dos)
            dk2 = dk2 + pltpu.roll(dk2, HEAD_DIM, 1)
            dv2 = dv2 + pltpu.roll(dv2, HEAD_DIM, 1)
            own = lok if h == 0 else jnp.logical_not(lok)
            dk = jnp.where(own, dk2, dk)
            dv = jnp.where(own, dv2, dv)
        macc[:, 0:BLK] += dk[2 * BLK:N_KEY]
        macc[:, BLK:2 * BLK] += dv[2 * BLK:N_KEY]
        last = (n == 0).astype(F32)
        zpad = jnp.zeros((PAD, BLK), F32)
        dk_c = dk[BLK:2 * BLK] + carry[:, 0:BLK] + last * jnp.concatenate([zpad, macc[:, 0:BLK]], axis=0)
        dv_c = dv[BLK:2 * BLK] + carry[:, BLK:2 * BLK] + last * jnp.concatenate([zpad, macc[:, BLK:2 * BLK]], axis=0)
        carry[:, 0:BLK] = dk[0:BLK]
        carry[:, BLK:2 * BLK] = dv[0:BLK]
        dqkv_ref[:, D:D + BLK] = _rope(dk_c, c, s1, s2).astype(BF16)
        dqkv_ref[:, D + BLK:D + 2 * BLK] = dv_c.astype(BF16)

    def rev(w):
        return pl.BlockSpec((BLK, w), lambda i: (nb - 1 - i, 0))

    return _call(
        body, name="attn_bwd", grid=(nb,),
        in_specs=[rev(D),
                  pl.BlockSpec((BLK, 256), lambda i: (0, 0)),
                  pl.BlockSpec((BLK, 256), lambda i: (jnp.maximum(nb - 2 - i, 0), 0)),
                  rev(256), rev(D),
                  pl.BlockSpec(memory_space=pltpu.SMEM), rev(384)],
        out_specs=[rev(QKV_W), pl.BlockSpec((8, BLK), lambda i: (0, 0))],
        out_shape=[jax.ShapeDtypeStruct((p, QKV_W), BF16), jax.ShapeDtypeStruct((8, BLK), F32)],
        scratch=[pltpu.VMEM((BLK, 256), F32), pltpu.VMEM((N_META, 256), F32)], sem="arbitrary",
        args=(q, kv, kv, kv, dao, sinks, tabs), comm=comm)


def _in_bwd(dqkv, da, dg, dgt, w_int, h0p, dh1, gain, comm=None):
    p = h0p.shape[0]
    tm = _row_tile(p)
    nt = p // tm
    first_rows = tm - BLK

    def body(dq_ref, da_ref, dg_ref, dt_ref, w_ref, h_ref, dh_ref, g_ref, gx_ref, dm_ref, acc_ref, buf, sems):
        i = pl.program_id(0)
        slot = i % 2

        @pl.when(i == 0)
        def _():
            acc_ref[...] = jnp.zeros_like(acc_ref)

        dn = (_dot(dq_ref[...], w_ref[0:QKV_W, :]) + _dot(da_ref[...], w_ref[QKV_W:QKV_W + D, :])
              + _dot(dg_ref[...], w_ref[QKV_W + D:QKV_W + 2 * D, :]) + _dot(dt_ref[...], w_ref[QKV_W + 2 * D:IN_W, :]))
        dh, dgain = _rms_bwd(h_ref[...], g_ref[...], dn)
        dh0 = dh_ref[...] + dh
        acc_ref[0:1, :] += dgain
        buf[slot] = dh0

        @pl.when(i == 0)
        def _():
            dm_ref[...] = dh0[PAD:BLK]

        def first_copy():
            return pltpu.make_async_copy(buf.at[0, pl.ds(BLK, first_rows), :], gx_ref.at[pl.ds(0, first_rows), :], sems.at[0])

        def tile_copy(j, s):
            return pltpu.make_async_copy(buf.at[s], gx_ref.at[pl.ds(pl.multiple_of(j * tm - BLK, BLK), tm), :], sems.at[s])

        if first_rows:
            @pl.when(i == 1)
            def _():
                first_copy().wait()

        @pl.when(i >= 2)
        def _():
            tile_copy(i - 1, 1 - slot).wait()

        if first_rows:
            @pl.when(i == 0)
            def _():
                first_copy().start()

        @pl.when(i > 0)
        def _():
            tile_copy(i, slot).start()

        @pl.when(i == nt - 1)
        def _():
            tile_copy(i, slot).wait()

    def row(w):
        return pl.BlockSpec((tm, w), lambda i: (i, 0))

    return _call(
        body, name="in_bwd", grid=(nt,),
        in_specs=[row(QKV_W), row(D), row(D), row(2 * D), VM, row(D), row(D), VM],
        out_specs=[ANY, pl.BlockSpec((N_META, D), lambda i: (0, 0)), pl.BlockSpec((8, D), lambda i: (0, 0))],
        out_shape=[jax.ShapeDtypeStruct((p - BLK, D), F32), jax.ShapeDtypeStruct((N_META, D), F32),
                   jax.ShapeDtypeStruct((8, D), F32)],
        scratch=[pltpu.VMEM((2, tm, D), F32), pltpu.SemaphoreType.DMA((2,))],
        sem="arbitrary", args=(dqkv, da, dg, dgt, w_int, h0p, dh1, gain), comm=comm)


def _sum_slots(slots, name):
    r = slots.shape[0] // N_DEV
    cols = slots.shape[1]
    tr = r if r <= 352 else (r // 2 if (r // 2) % 16 == 0 else r // 3)
    steps = r // tr

    def body(*refs):
        acc = refs[0][...].astype(F32)
        for s in range(1, N_DEV):
            acc = acc + refs[s][...].astype(F32)
        refs[N_DEV][...] = acc

    return pl.pallas_call(
        body, name=name, grid=(steps,),
        in_specs=[pl.BlockSpec((tr, cols), functools.partial(lambda i, s: (s * steps + i, 0), s=s)) for s in range(N_DEV)],
        out_specs=pl.BlockSpec((tr, cols), lambda i: (i, 0)),
        out_shape=jax.ShapeDtypeStruct((r, cols), F32),
        compiler_params=_cparams("parallel"),
    )(*([slots] * N_DEV))


def _adamw_math(w, g, m, v):
    m_n = ADAM_B1 * m + (1.0 - ADAM_B1) * g
    v_n = ADAM_B2 * v + (1.0 - ADAM_B2) * jnp.square(g)
    m_hat = m_n / (1.0 - ADAM_B1 ** ADAM_STEP)
    v_hat = v_n / (1.0 - ADAM_B2 ** ADAM_STEP)
    return -ADAM_LR * (m_hat / (jnp.sqrt(v_hat) + ADAM_EPS) + ADAM_WD * w), m_n, v_n


def _sum_adamw(parts, w, m, v, name, nslots=N_DEV, comm=None):
    r, cols = w.shape
    rs = r // len(parts)
    tr = rs if rs <= 352 else (rs // 2 if (rs // 2) % 16 == 0 else rs // 3)
    steps = rs // tr

    def body(*refs):
        w_ref, m_ref, v_ref, g_ref, d_ref, nm_ref, nv_ref = refs[nslots * len(parts):]
        i = pl.program_id(0)
        for q in range(len(parts)):
            @pl.when(i // steps == q)
            def _(q=q):
                g = refs[nslots * q][...].astype(F32)
                for s in range(1, nslots):
                    g = g + refs[nslots * q + s][...].astype(F32)
                g_ref[...] = g
                d_ref[...], nm_ref[...], nv_ref[...] = _adamw_math(w_ref[...], g, m_ref[...], v_ref[...])

    def slot_spec(q, s):
        return pl.BlockSpec((tr, cols), lambda i: (s * steps + jnp.clip(i - q * steps, 0, steps - 1), 0))

    spec = pl.BlockSpec((tr, cols), lambda i: (i, 0))
    outs, sent = _call(
        body, name=name, grid=(steps * len(parts),),
        in_specs=[slot_spec(q, s) for q in range(len(parts)) for s in range(nslots)] + [spec] * 3,
        out_specs=[spec] * 4, out_shape=[jax.ShapeDtypeStruct((r, cols), F32)] * 4,
        args=(*[a for a in parts for _ in range(nslots)], w, m, v), comm=comm)
    return outs if comm is None else (outs, sent)


def _adamw_many(ws, gs, ms, vs, name):
    n = len(ws)

    def body(*refs):
        w, g, m, v = refs[0:n], refs[n:2 * n], refs[2 * n:3 * n], refs[3 * n:4 * n]
        d, nm, nv = refs[4 * n:5 * n], refs[5 * n:6 * n], refs[6 * n:7 * n]
        for k in range(n):
            d[k][...], nm[k][...], nv[k][...] = _adamw_math(w[k][...], g[k][...], m[k][...], v[k][...])

    outs = pl.pallas_call(
        body, name=name, in_specs=[VM] * (4 * n), out_specs=[VM] * (3 * n),
        out_shape=[jax.ShapeDtypeStruct(a.shape, F32) for a in ws] * 3,
    )(*ws, *gs, *ms, *vs)
    return outs[0:n], outs[n:2 * n], outs[2 * n:3 * n]


def _rope_tables(p):
    half = ROT_DIM // 2
    lane = jnp.arange(BLK)
    seg = (lane % HEAD_DIM) // half
    inv_freq = ROPE_THETA ** (-(lane % half).astype(F32) * 2.0 / ROT_DIM)
    pos = (jnp.arange(p) - PAD).astype(F32)
    ang = pos[:, None] * inv_freq[None, :]
    cos = jnp.cos(ang)
    sin = jnp.sin(ang)
    c = jnp.where(seg[None, :] < 2, cos, 1.0)
    s1 = jnp.where(seg[None, :] == 0, -sin, 0.0)
    s2 = jnp.where(seg[None, :] == 1, sin, 0.0)
    return jnp.concatenate([c, s1, s2], axis=1).astype(F32)


def _flat_pack(parts, rows):
    flat = jnp.concatenate([a.reshape(-1).astype(F32) for a in parts])
    return jnp.pad(flat, (0, rows * D - flat.shape[0])).reshape(rows, D)


def _flat_unpack(pack, shapes):
    flat = pack.reshape(-1)
    out, off = [], 0
    for s in shapes:
        size = 1
        for e in s:
            size *= e
        out.append(flat[off:off + size].reshape(s))
        off += size
    return out


def kernel(x, meta_tokens, norm_pre_mix, norm_post_mix, w_in, b_in, attn_sinks, w_attn_proj, conv_dw_w, conv_dw_b, conv_ln_g, conv_ln_b, w_conv_proj, b_conv_proj, w_out, norm_pre_ffn, norm_post_ffn, w_up, ffn_dw_w, ffn_dw_b, w_down, loss_target, m_meta_tokens, m_norm_pre_mix, m_norm_post_mix, m_w_in, m_b_in, m_attn_sinks, m_w_attn_proj, m_conv_dw_w, m_conv_dw_b, m_conv_ln_g, m_conv_ln_b, m_w_conv_proj, m_b_conv_proj, m_w_out, m_norm_pre_ffn, m_norm_post_ffn, m_w_up, m_ffn_dw_w, m_ffn_dw_b, m_w_down, v_meta_tokens, v_norm_pre_mix, v_norm_post_mix, v_w_in, v_b_in, v_attn_sinks, v_w_attn_proj, v_conv_dw_w, v_conv_dw_b, v_conv_ln_g, v_conv_ln_b, v_w_conv_proj, v_b_conv_proj, v_w_out, v_norm_pre_ffn, v_norm_post_ffn, v_w_up, v_ffn_dw_w, v_ffn_dw_b, v_w_down):
    seq = x.shape[1]
    p = seq + BLK
    me = 4 * lax.axis_index("x") + 2 * lax.axis_index("y") + lax.axis_index("c")
    in_cols = w_in.shape[2]
    up_cols = w_up.shape[2]

    small = jnp.zeros((56, up_cols), F32)
    small = small.at[0:N_META, 0:BLK].set(meta_tokens)
    small = small.at[16:16 + CONV_K, 0:BLK].set(conv_dw_w[0])
    small = small.at[48:48 + FFN_K, :].set(ffn_dw_w[0])
    w_int, small_all = _exchange(_Both(_GatherRelay(w_in[0].T.astype(BF16)), _Gather([small])), "gather_w_in")
    small_all = small_all.reshape(N_DEV, 56, up_cols)
    meta_full = small_all[:, 0:N_META, 0:BLK].transpose(1, 0, 2).reshape(N_META, D)
    cdw = small_all[:, 16:16 + CONV_K, 0:BLK].transpose(1, 0, 2).reshape(CONV_K, D)
    cdw32 = jnp.pad(cdw, ((0, 32 - CONV_K), (0, 0)))
    fdw = small_all[:, 48:48 + FFN_K, :].transpose(1, 0, 2).reshape(FFN_K, 2 * FFN)

    tabs = _rope_tables(p)
    vecs = jnp.concatenate([conv_ln_g, conv_ln_b, b_conv_proj, norm_post_mix, norm_pre_ffn, jnp.zeros((3, D), F32)], axis=0)

    (h0p, n1, q, kv, ag, gates), (wa, wc, wo) = _in_proj(
        x[0], meta_full, norm_pre_mix, w_int, b_in, tabs,
        comm=_Gather([w_attn_proj[0].astype(BF16), w_conv_proj[0].astype(BF16), w_out[0].astype(BF16)]))
    (ao,), (w_upt,) = _attn_fwd(q, kv, attn_sinks, comm=_Gather([w_up[0].T.astype(BF16)]))
    (c0,), (wd,) = _conv31_fwd(ag, cdw32, conv_dw_b, comm=_Gather([w_down[0].astype(BF16)]))
    c1, attn, conv, merged, mix, h1, n2 = _mixer_fwd(ao, c0, gates, h0p, wa, wc, wo, vecs)
    u0 = _mm_nt(n2, w_upt, "ffn_up")
    act, dact_dv, dact_dg = _ffn_act(u0, fdw, ffn_dw_b)
    dffn, dact, dy, acc_f = _ffn_down_loss(act, wd, h1, loss_target[0], norm_post_ffn)

    (dug, duv, gfw_g, gfw_v, gfb_g, gfb_v, g_wd), _ = _ffn_act_bwd(u0, dact, dact_dg, dact_dv, fdw, act, dffn)
    g_wupt, (s_wd0,) = _mm_tn([dug, duv], n2, "grad_w_up", comm=_Scatter([g_wd], 0, 2))
    (dh1, acc_u), (s_wd1,) = _ffn_in_bwd(dug, duv, w_upt, h1, dy, norm_pre_ffn, comm=_Scatter([g_wd], 1, 2))
    (dmix, dat, dcv, dgt, dao, dc0, acc_m), (s_wup0,) = _mixer_bwd(
        dh1, mix, attn, conv, gates, c0, wa, wc, wo, vecs, comm=_Scatter([g_wupt], 0, 4))
    (da, dg, g_cdw, g_cdb, g_wo, g_wa, g_wc), (s_wup1, s_wup2, s_wup3) = _conv31_bwd(
        ag, dc0, cdw32, [(merged, dmix), (ao, dat), (c1, dcv)],
        comm=_Both(_Both(_Scatter([g_wupt], 1, 4), _Scatter([g_wupt], 2, 4)), _Scatter([g_wupt], 3, 4)))
    (dqkv, dsink), (s_wa, s_wc, s_wo) = _attn_bwd(q, kv, dao, attn_sinks, tabs, comm=_Scatter([g_wa, g_wc, g_wo]))
    loss_row = jnp.sum(acc_f[1:2, :], axis=1, keepdims=True)
    early = [loss_row, acc_m[0:1], dsink[0:1, 0:16], g_cdw[0:CONV_K], g_cdb,
             acc_m[2:3], acc_m[3:4], acc_m[1:2], acc_u[0:1], acc_f[0:1],
             jnp.concatenate([gfw_g, gfw_v], axis=1), jnp.concatenate([gfb_g, gfb_v], axis=1)]
    (g_wint, g_bin), (gathered_early,) = _mm_tn([dqkv, da, dg, dgt], n1, "grad_w_in", col_sums=True,
                                                comm=_Gather([_flat_pack(early, 64)]))
    (from_sibling,) = _exchange(_SiblingSwap(g_wint), "swap_w_in")
    (grad_x2d, dmeta, acc_i), (s_win,) = _in_bwd(dqkv, da, dg, dgt, w_int, h0p, dh1, norm_pre_mix,
                                                 comm=_ChipScatter(_pair_add(g_wint, from_sibling)))

    late = [dmeta, acc_i[0:1], g_bin]
    big = []
    for nm, parts, nslots, w, m, v, tr in (
            ("w_in", [s_win], N_CHIP, w_in, m_w_in, v_w_in, True), ("w_up", [s_wup0, s_wup1, s_wup2, s_wup3], N_DEV, w_up, m_w_up, v_w_up, True),
            ("w_attn_proj", [s_wa], N_DEV, w_attn_proj, m_w_attn_proj, v_w_attn_proj, False),
            ("w_conv_proj", [s_wc], N_DEV, w_conv_proj, m_w_conv_proj, v_w_conv_proj, False),
            ("w_out", [s_wo], N_DEV, w_out, m_w_out, v_w_out, False),
            ("w_down", [s_wd0, s_wd1], N_DEV, w_down, m_w_down, v_w_down, False)):
        ins = [a[0].T if tr else a[0] for a in (w, m, v)]
        if nm == "w_up":
            outs, (gathered_late,) = _sum_adamw(parts, *ins, "update_" + nm, nslots, comm=_Gather([_flat_pack(late, 24)]))
        else:
            outs = _sum_adamw(parts, *ins, "update_" + nm, nslots)
        big.append(tuple((o.T if tr else o)[None] for o in outs))

    g_meta, g_npm, g_bi = _flat_unpack(_sum_slots(gathered_late, "sum_late_grads"), [a.shape for a in late])
    tot = _flat_unpack(_sum_slots(gathered_early, "sum_small_grads"), [a.shape for a in early])
    (loss, g_nqm, g_sk, g_cw, g_cb, g_lg, g_lb, g_bc, g_npf, g_nqf, g_fw, g_fb) = tot
    loss = loss.reshape(())
    g_meta = lax.dynamic_slice_in_dim(g_meta, me * BLK, BLK, axis=1)
    g_cw = lax.dynamic_slice_in_dim(g_cw, me * BLK, BLK, axis=1)[None]
    g_fw = lax.dynamic_slice_in_dim(g_fw, me * up_cols, up_cols, axis=1)[None]

    sm_w = [meta_tokens, norm_pre_mix, norm_post_mix, b_in, attn_sinks, conv_dw_w, conv_dw_b, conv_ln_g, conv_ln_b,
            b_conv_proj, norm_pre_ffn, norm_post_ffn, ffn_dw_w, ffn_dw_b]
    sm_g = [g_meta, g_npm, g_nqm, g_bi, g_sk, g_cw, g_cb, g_lg, g_lb, g_bc, g_npf, g_nqf, g_fw, g_fb]
    sm_m = [m_meta_tokens, m_norm_pre_mix, m_norm_post_mix, m_b_in, m_attn_sinks, m_conv_dw_w, m_conv_dw_b, m_conv_ln_g,
            m_conv_ln_b, m_b_conv_proj, m_norm_pre_ffn, m_norm_post_ffn, m_ffn_dw_w, m_ffn_dw_b]
    sm_v = [v_meta_tokens, v_norm_pre_mix, v_norm_post_mix, v_b_in, v_attn_sinks, v_conv_dw_w, v_conv_dw_b, v_conv_ln_g,
            v_conv_ln_b, v_b_conv_proj, v_norm_pre_ffn, v_norm_post_ffn, v_ffn_dw_w, v_ffn_dw_b]
    swap = lambda a: jnp.transpose(a, (1, 0, 2)) if a.ndim == 3 else a
    sm_d, sm_nm, sm_nv = ([swap(o) for o in outs] for outs in
                          _adamw_many(*([swap(a) for a in group] for group in (sm_w, sm_g, sm_m, sm_v)), "adamw_small"))

    order = ["meta_tokens", "norm_pre_mix", "norm_post_mix", "w_in", "b_in", "attn_sinks", "w_attn_proj", "conv_dw_w",
             "conv_dw_b", "conv_ln_g", "conv_ln_b", "w_conv_proj", "b_conv_proj", "w_out", "norm_pre_ffn", "norm_post_ffn",
             "w_up", "ffn_dw_w", "ffn_dw_b", "w_down"]
    small_names = ["meta_tokens", "norm_pre_mix", "norm_post_mix", "b_in", "attn_sinks", "conv_dw_w", "conv_dw_b", "conv_ln_g",
                   "conv_ln_b", "b_conv_proj", "norm_pre_ffn", "norm_post_ffn", "ffn_dw_w", "ffn_dw_b"]
    big_names = ["w_in", "w_up", "w_attn_proj", "w_conv_proj", "w_out", "w_down"]
    table = {}
    for k, nm in enumerate(small_names):
        table[nm] = (sm_g[k], sm_d[k], sm_nm[k], sm_nv[k])
    for k, nm in enumerate(big_names):
        table[nm] = big[k]
    grad_x = grad_x2d[None]
    outs = [loss, grad_x]
    for field in range(4):
        outs += [table[nm][field] for nm in order]
    return tuple(outs)
```

```python
import functools

import jax
import jax.numpy as jnp
from jax import lax
from jax.experimental import pallas as pl
from jax.experimental.pallas import tpu as pltpu

F32 = jnp.float32
BF16 = jnp.bfloat16
MESH = pl.DeviceIdType.MESH

D = 1024
HEAD_DIM = 64
N_META = 16
BLK = 128
PAD = BLK - N_META
CONV_K = 31
FFN = 2816
FFN_K = 3
QKV_W = 1280
IN_W = 5376
ROT_DIM = 16
ROPE_THETA = 500000.0
RMS_EPS = 1e-6
LN_EPS = 1e-5
NEG_INF = -1e30
SCALE = HEAD_DIM ** -0.5
N_DEV = 8

ADAM_LR = 0.001
ADAM_B1 = 0.9
ADAM_B2 = 0.999
ADAM_EPS = 1e-08
ADAM_WD = 0.01
ADAM_STEP = 10

VMEM_BYTES_V7X = 64 * 1024 * 1024
VMEM_LIMIT = VMEM_BYTES_V7X - 8 * 1024 * 1024

NT = (((1,), (1,)), ((), ()))
TN = (((0,), (0,)), ((), ()))
VM = pl.BlockSpec(memory_space=pltpu.VMEM)
ANY = pl.BlockSpec(memory_space=pl.ANY)


def _cparams(*sem):
    return pltpu.CompilerParams(dimension_semantics=sem or None, vmem_limit_bytes=VMEM_LIMIT)


def _row_tile(p):
    return 384 if p % 384 == 0 else 128


def _dot(a, b):
    return jnp.dot(a, b, preferred_element_type=F32)


def _dot_nt(a, b):
    return lax.dot_general(a, b, NT, preferred_element_type=F32)


def _dot_tn(a, b):
    return lax.dot_general(a, b, TN, preferred_element_type=F32)


def _rms(x, g):
    return x * lax.rsqrt(jnp.mean(x * x, axis=-1, keepdims=True) + RMS_EPS) * g


def _lnsilu(x, g, b):
    mu = jnp.mean(x, axis=-1, keepdims=True)
    var = jnp.mean(jnp.square(x - mu), axis=-1, keepdims=True)
    z = (x - mu) * lax.rsqrt(var + LN_EPS) * g + b
    return z * jax.nn.sigmoid(z)


def _rms_bwd(x, g, dy):
    r = lax.rsqrt(jnp.mean(x * x, axis=-1, keepdims=True) + RMS_EPS)
    xn = x * r
    u = dy * g
    dg = jnp.sum(dy * xn, axis=0, keepdims=True)
    dx = r * (u - xn * jnp.mean(u * xn, axis=-1, keepdims=True))
    return dx, dg


def _lnsilu_bwd(x, g, b, dout):
    mu = jnp.mean(x, axis=-1, keepdims=True)
    xc = x - mu
    rs = lax.rsqrt(jnp.mean(xc * xc, axis=-1, keepdims=True) + LN_EPS)
    yh = xc * rs
    z = yh * g + b
    sg = jax.nn.sigmoid(z)
    dz = dout * (sg * (1.0 + z * (1.0 - sg)))
    dg = jnp.sum(dz * yh, axis=0, keepdims=True)
    db = jnp.sum(dz, axis=0, keepdims=True)
    dyh = dz * g
    dx = rs * (dyh - jnp.mean(dyh, axis=-1, keepdims=True) - yh * jnp.mean(dyh * yh, axis=-1, keepdims=True))
    return dx, dg, db


def _rope(v, c, s1, s2):
    return v * c + pltpu.roll(v, BLK - 8, 1) * s1 + pltpu.roll(v, 8, 1) * s2


def _rows(i, tm):
    return i * tm + lax.broadcasted_iota(jnp.int32, (tm, 1), 0)


def _place():
    return lax.axis_index("x"), lax.axis_index("y"), lax.axis_index("c")


def _blk(ref, idx, r, dtype):
    return ref.at[pl.ds(pl.multiple_of(idx * r, 16 if dtype == BF16 else 8), r), :]


class _Gather:
    def __init__(self, arrs):
        self.ins = list(arrs)
        n = len(arrs)
        self.out_shape = [jax.ShapeDtypeStruct((N_DEV * a.shape[0], a.shape[1]), a.dtype) for a in arrs]
        self.scratch = [pltpu.SemaphoreType.DMA((n, 7)), pltpu.SemaphoreType.DMA((n, 7)), pltpu.SemaphoreType.DMA((n,))]

    def _parts(self, ins, outs, sems):
        send_sems, recv_sems, local_sems = sems
        n = len(ins)
        x, y, c = _place()
        me, sibling = (x, y, c), (x, y, 1 - c)
        chips = [(1 - x, y), (x, 1 - y), (1 - x, 1 - y)]

        def rows(a, p):
            return _blk(outs[a], 4 * p[0] + 2 * p[1] + p[2], self.ins[a].shape[0], self.ins[a].dtype)

        def copy(a, k, block, to, src=None):
            return pltpu.make_async_remote_copy(
                src_ref=rows(a, block) if src is None else src, dst_ref=rows(a, block),
                send_sem=send_sems.at[a, k], recv_sem=recv_sems.at[a, k], device_id=to, device_id_type=MESH)

        mine = [pltpu.make_async_copy(ins[a], rows(a, me), local_sems.at[a]) for a in range(n)]
        first = []
        for a in range(n):
            first.append(copy(a, 0, me, sibling, src=ins[a]))
            first += [copy(a, 1 + j, me, (*chip, c), src=ins[a]) for j, chip in enumerate(chips)]
        return n, c, me, sibling, chips, copy, mine, first

    def start(self, ins, outs, sems):
        *_, mine, first = self._parts(ins, outs, sems)
        for cp in mine + first:
            cp.start()

    def middle(self, ins, outs, sems):
        n, c, me, sibling, chips, copy, _, _ = self._parts(ins, outs, sems)
        for j, chip in enumerate(chips):
            for a in range(n):
                copy(a, 1 + j, (*chip, c), me).wait_recv()
                copy(a, 4 + j, (*chip, c), sibling).start()

    def finish(self, ins, outs, sems, middle_done=False):
        if not middle_done:
            self.middle(ins, outs, sems)
        n, c, me, sibling, chips, copy, mine, first = self._parts(ins, outs, sems)
        passed = [copy(a, 4 + j, (*chip, c), sibling) for j, chip in enumerate(chips) for a in range(n)]
        for a in range(n):
            copy(a, 0, sibling, me).wait_recv()
            for j, chip in enumerate(chips):
                copy(a, 4 + j, (*chip, 1 - c), me).wait_recv()
        for cp in first + passed:
            cp.wait_send()
        for cp in mine:
            cp.wait()


class _GatherRelay:
    N_COPY = 13

    def __init__(self, arr):
        self.ins = [arr]
        self.r = arr.shape[0]
        self.out_shape = [jax.ShapeDtypeStruct((N_DEV * self.r, arr.shape[1]), arr.dtype)]
        self.scratch = [pltpu.SemaphoreType.DMA((self.N_COPY,)), pltpu.SemaphoreType.DMA((self.N_COPY,)),
                        pltpu.SemaphoreType.DMA]

    def _parts(self, ins, outs, sems):
        send_sems, recv_sems, local_sem = sems
        x, y, c = _place()
        r, half = self.r, self.r // 2
        out = outs[0]
        me, sib, xn, yn, dg = (x, y, c), (x, y, 1 - c), (1 - x, y, c), (x, 1 - y, c), (1 - x, 1 - y, c)
        sx, sy, sd = (1 - x, y, 1 - c), (x, 1 - y, 1 - c), (1 - x, 1 - y, 1 - c)
        lo, hi = (0, half), (half, half)

        def rows(p, part=(0, r)):
            return out.at[pl.ds(pl.multiple_of((4 * p[0] + 2 * p[1] + p[2]) * r + part[0], 16), part[1]), :]

        def own(part):
            return ins[0].at[pl.ds(part[0], part[1]), :]

        def copy(k, dev_rows, to, src=None):
            return pltpu.make_async_remote_copy(
                src_ref=dev_rows if src is None else src, dst_ref=dev_rows,
                send_sem=send_sems.at[k], recv_sem=recv_sems.at[k], device_id=to, device_id_type=MESH)

        mine = pltpu.make_async_copy(ins[0], rows(me), local_sem)
        first = [copy(0, rows(me), sib, src=ins[0]),
                 copy(1, rows(me, lo), xn, src=own(lo)), copy(3, rows(me, hi), yn, src=own(hi)),
                 copy(2, rows(me, hi), xn, src=own(hi)), copy(4, rows(me, lo), yn, src=own(lo))]
        arrive = {0: rows(sib), 1: rows(xn, lo), 2: rows(xn, hi), 3: rows(yn, hi), 4: rows(yn, lo),
                  5: rows(dg, lo), 6: rows(dg, hi), 7: rows(sx, lo), 8: rows(sx, hi), 9: rows(sy, hi),
                  10: rows(sy, lo), 11: rows(sd, lo), 12: rows(sd, hi)}
        relay = {1: [(5, rows(xn, lo), yn), (7, rows(xn, lo), sib)], 3: [(6, rows(yn, hi), xn), (9, rows(yn, hi), sib)],
                 2: [(8, rows(xn, hi), sib)], 4: [(10, rows(yn, lo), sib)],
                 5: [(11, rows(dg, lo), sib)], 6: [(12, rows(dg, hi), sib)]}
        return copy, mine, first, arrive, relay, me

    def start(self, ins, outs, sems):
        _, mine, first, _, _, _ = self._parts(ins, outs, sems)
        for cp in [mine] + first:
            cp.start()

    def finish(self, ins, outs, sems):
        copy, mine, first, arrive, relay, me = self._parts(ins, outs, sems)
        passed = []
        for k in (1, 3, 2, 4, 5, 6):
            copy(k, arrive[k], me).wait_recv()
            for k2, dev_rows, to in relay[k]:
                fwd = copy(k2, dev_rows, to)
                fwd.start()
                passed.append(fwd)
        for k in (0, 7, 8, 9, 10, 11, 12):
            copy(k, arrive[k], me).wait_recv()
        for cp in first + passed:
            cp.wait_send()
        mine.wait()


FLIPS = [(0, 0, 1), (1, 0, 0), (0, 1, 0), (1, 1, 0), (1, 0, 1), (0, 1, 1), (1, 1, 1)]


class _Scatter:
    def __init__(self, arrs, part=0, nparts=1):
        self.ins = list(arrs)
        self.part, self.nparts = part, nparts
        n = len(arrs)
        self.out_shape = [jax.ShapeDtypeStruct((a.shape[0] // nparts, a.shape[1]), a.dtype) for a in arrs]
        self.scratch = [pltpu.SemaphoreType.DMA((n, 7)), pltpu.SemaphoreType.DMA((n, 7)), pltpu.SemaphoreType.DMA((n,))]

    def _parts(self, ins, outs, sems):
        send_sems, recv_sems, local_sems = sems
        n = len(ins)
        x, y, c = _place()
        me = 4 * x + 2 * y + c

        def flip(v, f):
            return 1 - v if f else v

        def src(a, idx):
            r = self.ins[a].shape[0] // N_DEV
            rs = r // self.nparts
            return ins[a].at[pl.ds(pl.multiple_of(idx * r + self.part * rs, 16), rs), :]

        def dst(a, idx):
            rs = self.ins[a].shape[0] // N_DEV // self.nparts
            return outs[a].at[pl.ds(pl.multiple_of(idx * rs, 16), rs), :]

        mine = [pltpu.make_async_copy(src(a, me), dst(a, me), local_sems.at[a]) for a in range(n)]
        sends, recvs = [], []
        for k, f in enumerate(FLIPS):
            peer = (flip(x, f[0]), flip(y, f[1]), flip(c, f[2]))
            pidx = 4 * peer[0] + 2 * peer[1] + peer[2]
            for a in range(n):
                sends.append(pltpu.make_async_remote_copy(
                    src_ref=src(a, pidx), dst_ref=dst(a, me),
                    send_sem=send_sems.at[a, k], recv_sem=recv_sems.at[a, k], device_id=peer, device_id_type=MESH))
                recvs.append(functools.partial(
                    pltpu.make_async_remote_copy,
                    src_ref=src(a, pidx), dst_ref=dst(a, pidx),
                    send_sem=send_sems.at[a, k], recv_sem=recv_sems.at[a, k], device_id=peer, device_id_type=MESH))
        return mine, sends, recvs

    def start(self, ins, outs, sems):
        mine, sends, _ = self._parts(ins, outs, sems)
        for cp in mine + sends:
            cp.start()

    def finish(self, ins, outs, sems):
        mine, sends, recvs = self._parts(ins, outs, sems)
        for make in recvs:
            make().wait_recv()
        for cp in sends:
            cp.wait_send()
        for cp in mine:
            cp.wait()


N_CHIP = 4


class _SiblingSwap:
    def __init__(self, arr):
        self.ins = [arr]
        self.r = arr.shape[0] // N_DEV
        self.out_shape = [jax.ShapeDtypeStruct((N_CHIP * self.r, arr.shape[1]), arr.dtype)]
        self.scratch = [pltpu.SemaphoreType.DMA((N_CHIP,)), pltpu.SemaphoreType.DMA((N_CHIP,))]

    def _copies(self, ins, outs, sems):
        send_sems, recv_sems = sems
        x, y, c = _place()
        r = self.r
        return [pltpu.make_async_remote_copy(
            src_ref=ins[0].at[pl.ds(pl.multiple_of((2 * j + 1 - c) * r, 16), r), :],
            dst_ref=outs[0].at[pl.ds(j * r, r), :],
            send_sem=send_sems.at[j], recv_sem=recv_sems.at[j], device_id=(x, y, 1 - c), device_id_type=MESH)
            for j in range(N_CHIP)]

    def start(self, ins, outs, sems):
        for cp in self._copies(ins, outs, sems):
            cp.start()

    def finish(self, ins, outs, sems):
        for cp in self._copies(ins, outs, sems):
            cp.wait()


class _ChipScatter:
    def __init__(self, arr):
        self.ins = [arr]
        self.r = arr.shape[0] // N_CHIP
        self.out_shape = [jax.ShapeDtypeStruct(arr.shape, arr.dtype)]
        self.scratch = [pltpu.SemaphoreType.DMA((3,)), pltpu.SemaphoreType.DMA((3,)), pltpu.SemaphoreType.DMA]

    def _parts(self, ins, outs, sems):
        send_sems, recv_sems, local_sem = sems
        x, y, c = _place()
        r = self.r
        my_chip = 2 * x + y

        def rows(ref, j):
            return ref.at[pl.ds(pl.multiple_of(j * r, 16), r), :]

        mine = pltpu.make_async_copy(rows(ins[0], my_chip), rows(outs[0], my_chip), local_sem)
        sends, recvs = [], []
        for k, (fx, fy) in enumerate(((1, 0), (0, 1), (1, 1))):
            px, py = (1 - x if fx else x), (1 - y if fy else y)
            peer_chip = 2 * px + py
            sends.append(pltpu.make_async_remote_copy(
                src_ref=rows(ins[0], peer_chip), dst_ref=rows(outs[0], my_chip),
                send_sem=send_sems.at[k], recv_sem=recv_sems.at[k], device_id=(px, py, c), device_id_type=MESH))
            recvs.append(functools.partial(
                pltpu.make_async_remote_copy,
                src_ref=rows(ins[0], peer_chip), dst_ref=rows(outs[0], peer_chip),
                send_sem=send_sems.at[k], recv_sem=recv_sems.at[k], device_id=(px, py, c), device_id_type=MESH))
        return mine, sends, recvs

    def start(self, ins, outs, sems):
        mine, sends, _ = self._parts(ins, outs, sems)
        for cp in [mine] + sends:
            cp.start()

    def finish(self, ins, outs, sems):
        mine, sends, recvs = self._parts(ins, outs, sems)
        for make in recvs:
            make().wait_recv()
        for cp in sends:
            cp.wait_send()
        mine.wait()


def _pair_add(partial, recv):
    r = recv.shape[0] // N_CHIP
    cols = recv.shape[1]
    tr = r // 2 if (r // 2) % 16 == 0 else r
    steps = r // tr
    core = lax.axis_index("c").astype(jnp.int32).reshape(1)

    def body(c_ref, p_ref, s_ref, o_ref):
        o_ref[...] = (p_ref[...].astype(F32) + s_ref[...].astype(F32)).astype(BF16)

    spec = pl.BlockSpec((tr, cols), lambda j, i, c_ref: (j * steps + i, 0))
    return pl.pallas_call(
        body, name="pair_add",
        grid_spec=pltpu.PrefetchScalarGridSpec(
            num_scalar_prefetch=1, grid=(N_CHIP, steps),
            in_specs=[pl.BlockSpec((tr, cols), lambda j, i, c_ref: ((2 * j + c_ref[0]) * steps + i, 0)), spec],
            out_specs=spec),
        out_shape=jax.ShapeDtypeStruct(recv.shape, BF16),
        compiler_params=_cparams("parallel", "parallel"),
    )(core, partial, recv)


class _Both:
    def __init__(self, a, b):
        self.a, self.b = a, b
        self.ins = a.ins + b.ins
        self.out_shape = a.out_shape + b.out_shape
        self.scratch = a.scratch + b.scratch

    def _split(self, ins, outs, sems):
        ni, no, ns = len(self.a.ins), len(self.a.out_shape), len(self.a.scratch)
        return (ins[:ni], outs[:no], sems[:ns]), (ins[ni:], outs[no:], sems[ns:])

    def start(self, ins, outs, sems):
        ra, rb = self._split(ins, outs, sems)
        self.a.start(*ra)
        self.b.start(*rb)

    def finish(self, ins, outs, sems):
        ra, rb = self._split(ins, outs, sems)
        self.a.finish(*ra)
        self.b.finish(*rb)


def _exchange(comm, name):
    n, m = len(comm.ins), len(comm.out_shape)

    def body(*refs):
        ins, outs, sems = refs[:n], refs[n:n + m], refs[n + m:]
        comm.start(ins, outs, sems)
        comm.finish(ins, outs, sems)

    return pl.pallas_call(
        body, name=name, out_shape=comm.out_shape, in_specs=[ANY] * n, out_specs=[ANY] * m, scratch_shapes=comm.scratch,
    )(*comm.ins)


def _call(body, *, name, grid, in_specs, out_specs, out_shape, args, scratch=(), sem="parallel", comm=None,
          comm_mid=None):
    if comm is None:
        outs = pl.pallas_call(
            body, name=name, grid=grid, in_specs=list(in_specs), out_specs=list(out_specs), out_shape=list(out_shape),
            scratch_shapes=list(scratch), compiler_params=_cparams(sem))(*args)
        return outs, []
    n_in, n_out, n_sc = len(in_specs), len(out_specs), len(scratch)
    n_ci, n_co = len(comm.ins), len(comm.out_shape)
    last = grid[0] - 1

    def fused(*refs):
        ins, refs = refs[:n_in], refs[n_in:]
        c_ins, refs = refs[:n_ci], refs[n_ci:]
        outs, refs = refs[:n_out], refs[n_out:]
        c_outs, refs = refs[:n_co], refs[n_co:]
        sc, c_sems = refs[:n_sc], refs[n_sc:]
        step = pl.program_id(0)

        @pl.when(step == 0)
        def _():
            comm.start(c_ins, c_outs, c_sems)

        body(*ins, *outs, *sc)

        if comm_mid is not None:
            @pl.when(step == comm_mid)
            def _():
                comm.middle(c_ins, c_outs, c_sems)

        @pl.when(step == last)
        def _():
            if comm_mid is not None:
                comm.finish(c_ins, c_outs, c_sems, middle_done=True)
            else:
                comm.finish(c_ins, c_outs, c_sems)

    outs = pl.pallas_call(
        fused, name=name, grid=grid, in_specs=list(in_specs) + [ANY] * n_ci, out_specs=list(out_specs) + [ANY] * n_co,
        out_shape=list(out_shape) + comm.out_shape, scratch_shapes=list(scratch) + comm.scratch,
        compiler_params=_cparams("arbitrary"))(*args, *comm.ins)
    return outs[:n_out], outs[n_out:]


def _token_specs(tm):
    k = tm // BLK
    return [pl.BlockSpec((BLK, D), functools.partial(lambda i, t: (jnp.maximum(k * i + t - 1, 0), 0), t=t)) for t in range(k)]


def _in_proj(x2d, meta, gain, w_int, b_in, tabs, comm=None):
    p = x2d.shape[0] + BLK
    tm = _row_tile(p)
    k = tm // BLK

    def body(*refs):
        x_refs = refs[:k]
        m_ref, g_ref, w_ref, b_ref, t_ref, h_ref, n1_ref, q_ref, kv_ref, ag_ref, gt_ref = refs[k:]
        i = pl.program_id(0)
        head = jnp.concatenate([jnp.zeros((PAD, D), F32), m_ref[...]], axis=0)
        first = jnp.where(i == 0, head, x_refs[0][...])
        h = jnp.concatenate([first] + [r[...] for r in x_refs[1:]], axis=0) if k > 1 else first
        h_ref[...] = h
        n = _rms(h, g_ref[...]).astype(BF16)
        n1_ref[...] = n
        c, s1, s2 = t_ref[:, 0:128], t_ref[:, 128:256], t_ref[:, 256:384]

        def mm(c0, w):
            return _dot_nt(n, w_ref[c0:c0 + w, :]) + b_ref[:, c0:c0 + w]

        for j in range(4):
            acc = mm(256 * j, 256)
            for t in range(2):
                lo = 256 * j + 128 * t
                q_ref[:, lo:lo + 128] = (_rope(acc[:, 128 * t:128 * (t + 1)], c, s1, s2) * SCALE).astype(BF16)
        acc = mm(1024, 256)
        kv_ref[:, 0:128] = _rope(acc[:, 0:128], c, s1, s2).astype(BF16)
        kv_ref[:, 128:256] = acc[:, 128:256].astype(BF16)
        for j in range(8):
            ag_ref[:, 256 * j:256 * (j + 1)] = mm(QKV_W + 256 * j, 256).astype(BF16)
        for j in range(8):
            gt_ref[:, 256 * j:256 * (j + 1)] = mm(QKV_W + 2048 + 256 * j, 256).astype(BF16)

    def row(w):
        return pl.BlockSpec((tm, w), lambda i: (i, 0))

    return _call(
        body, name="in_proj", grid=(p // tm,),
        in_specs=_token_specs(tm) + [VM, VM, VM, VM, row(384)],
        out_specs=[row(D), row(D), row(D), row(256), row(2048), row(2048)],
        out_shape=[jax.ShapeDtypeStruct((p, D), F32)] + [jax.ShapeDtypeStruct((p, w), BF16) for w in (D, D, 256, 2048, 2048)],
        args=(x2d,) * k + (meta, gain, w_int, b_in, tabs), comm=comm,
        comm_mid=None if comm is None else (3 * (p // tm)) // 4)


N_KEY = 2 * BLK + N_META


def _attn_setup(n, h, q_ref, km_ref, kp_ref, kc_ref):
    lo = lax.broadcasted_iota(jnp.int32, (BLK, BLK), 1) < HEAD_DIM
    lok = lax.broadcasted_iota(jnp.int32, (N_KEY, BLK), 1) < HEAD_DIM

    def dup(lanes):
        cat = jnp.concatenate([kp_ref[:, lanes], kc_ref[:, lanes], km_ref[PAD:BLK, lanes]], axis=0).astype(F32)
        rolled = pltpu.roll(cat, HEAD_DIM, 1)
        return (jnp.where(lok, cat, rolled) if h == 0 else jnp.where(lok, rolled, cat)).astype(BF16)

    k2 = dup(slice(0, 128))
    v2 = dup(slice(128, 256))
    qs = _stack_heads(q_ref, h, lo)

    kr = lax.broadcasted_iota(jnp.int32, (BLK, BLK), 0)
    tq = BLK * n + lax.broadcasted_iota(jnp.int32, (BLK, BLK), 1) - PAD
    t_p = BLK * (n - 1) + kr - PAD
    t_c = BLK * n + kr - PAD
    ok_p = jnp.logical_and(t_p >= N_META, tq - t_p < BLK)
    ok_c = jnp.logical_and(t_c >= N_META, t_c <= tq)
    ok_m = lax.broadcasted_iota(jnp.int32, (N_META, BLK), 0) <= BLK * n + lax.broadcasted_iota(jnp.int32, (N_META, BLK), 1) - PAD
    bias = jnp.concatenate([jnp.where(ok, 0.0, NEG_INF).astype(F32) for ok in (ok_p, ok_c, ok_m)], axis=0)
    return qs, k2, v2, bias, lok


def _attn_head(s, bias, sink):
    s = s + bias
    m = jnp.maximum(jnp.max(s, axis=0, keepdims=True), sink)
    e = jnp.exp(s - m)
    es = jnp.exp(sink - m)
    inv = 1.0 / (jnp.sum(e, axis=0, keepdims=True) + es)
    return e * inv, es * inv


def _stack_heads(ref, h, lo):
    pieces = []
    for jp in range(4):
        v = ref[:, BLK * (4 * h + jp):BLK * (4 * h + jp + 1)]
        zero = jnp.zeros_like(v)
        pieces += [jnp.where(lo, v, zero), jnp.where(lo, zero, v)]
    return jnp.concatenate(pieces, axis=0)


def _unstack_heads(v, jp, lo):
    return jnp.where(lo, v[256 * jp:256 * jp + 128], v[256 * jp + 128:256 * jp + 256])


def _attn_fwd(q, kv, sinks, comm=None):
    p = q.shape[0]
    nb = p // BLK

    def body(q_ref, km_ref, kp_ref, kc_ref, sink_ref, o_ref):
        n = pl.program_id(0)
        lo = lax.broadcasted_iota(jnp.int32, (BLK, BLK), 1) < HEAD_DIM
        for h in range(2):
            qs, k2, v2, bias, _ = _attn_setup(n, h, q_ref, km_ref, kp_ref, kc_ref)
            st = _dot_nt(k2, qs)
            pt = jnp.concatenate(
                [_attn_head(st[:, BLK * g:BLK * (g + 1)], bias, sink_ref[0, 8 * h + g])[0].astype(BF16) for g in range(8)],
                axis=1)
            o = _dot_tn(pt, v2)
            for jp in range(4):
                o_ref[:, BLK * (4 * h + jp):BLK * (4 * h + jp + 1)] = _unstack_heads(o, jp, lo).astype(BF16)

    return _call(
        body, name="attn_fwd", grid=(nb,),
        in_specs=[pl.BlockSpec((BLK, D), lambda i: (i, 0)),
                  pl.BlockSpec((BLK, 256), lambda i: (0, 0)),
                  pl.BlockSpec((BLK, 256), lambda i: (jnp.maximum(i - 1, 0), 0)),
                  pl.BlockSpec((BLK, 256), lambda i: (i, 0)),
                  pl.BlockSpec(memory_space=pltpu.SMEM)],
        out_specs=[pl.BlockSpec((BLK, D), lambda i: (i, 0))],
        out_shape=[jax.ShapeDtypeStruct((p, D), BF16)],
        args=(q, kv, kv, kv, sinks), comm=comm)


def _conv31_fwd(ag, w32, b, comm=None):
    p = ag.shape[0]
    nch = p // BLK

    def body(a_ref, g_ref, w_ref, b_ref, o_ref, gp):
        gp[0:32, :] = jnp.zeros((32, BLK), F32)
        for ci in range(nch):
            r0 = BLK * ci
            glu = a_ref[r0:r0 + BLK, :].astype(F32) * jax.nn.sigmoid(g_ref[r0:r0 + BLK, :].astype(F32))
            if ci == 0:
                glu = jnp.where(_rows(0, BLK) >= PAD, glu, 0.0)
            gp[32 + r0:32 + r0 + BLK, :] = glu
        for ci in range(nch):
            r0 = BLK * ci
            acc = jnp.broadcast_to(b_ref[...], (BLK, BLK))
            for j in range(CONV_K):
                acc = acc + w_ref[j:j + 1, :] * gp[r0 + j + 2:r0 + j + 2 + BLK, :]
            o_ref[r0:r0 + BLK, :] = acc

    return _call(
        body, name="conv31_fwd", grid=(D // BLK,),
        in_specs=[pl.BlockSpec((p, BLK), lambda j: (0, j)), pl.BlockSpec((p, BLK), lambda j: (0, 8 + j)),
                  pl.BlockSpec((32, BLK), lambda j: (0, j)), pl.BlockSpec((1, BLK), lambda j: (0, j))],
        out_specs=[pl.BlockSpec((p, BLK), lambda j: (0, j))],
        out_shape=[jax.ShapeDtypeStruct((p, D), F32)],
        scratch=[pltpu.VMEM((p + 32, BLK), F32)],
        args=(ag, ag, w32, b), comm=comm, comm_mid=None if comm is None else (3 * (D // BLK)) // 4)


def _mixer_fwd(ao, c0, gates, h0p, wa, wc, wo, vecs):
    p = ao.shape[0]
    tm = _row_tile(p)

    def body(ao_ref, c0_ref, gt_ref, h_ref, wa_ref, wc_ref, wo_ref, v_ref,
             c1_ref, at_ref, cv_ref, mg_ref, mix_ref, h1_ref, n2_ref):
        i = pl.program_id(0)
        c1 = _lnsilu(c0_ref[...], v_ref[0:1, :], v_ref[1:2, :]).astype(BF16)
        c1_ref[...] = c1
        attn = _dot(ao_ref[...], wa_ref[...])
        conv = _dot(c1, wc_ref[...]) + v_ref[2:3, :]
        at_ref[...] = attn.astype(BF16)
        cv_ref[...] = conv.astype(BF16)
        merged = (jax.nn.sigmoid(gt_ref[:, 0:D].astype(F32)) * attn
                  + jax.nn.sigmoid(gt_ref[:, D:2 * D].astype(F32)) * conv).astype(BF16)
        mg_ref[...] = merged
        mix = _dot(merged, wo_ref[...])
        mix_ref[...] = mix
        h1 = jnp.where(_rows(i, tm) >= PAD, h_ref[...] + _rms(mix, v_ref[3:4, :]), 0.0)
        h1_ref[...] = h1
        n2_ref[...] = _rms(h1, v_ref[4:5, :]).astype(BF16)

    def row(w):
        return pl.BlockSpec((tm, w), lambda i: (i, 0))

    return pl.pallas_call(
        body, name="mixer_fwd", grid=(p // tm,),
        in_specs=[row(D), row(D), row(2 * D), row(D), VM, VM, VM, VM],
        out_specs=[row(D)] * 7,
        out_shape=[jax.ShapeDtypeStruct((p, D), t) for t in (BF16, BF16, BF16, BF16, F32, F32, BF16)],
        compiler_params=_cparams("parallel"),
    )(ao, c0, gates, h0p, wa, wc, wo, vecs)


def _mm_nt(a, w_t, name):
    p, k = a.shape
    n = w_t.shape[0]
    tm = _row_tile(p)
    ch = 512

    def body(a_ref, w_ref, o_ref):
        a_v = a_ref[...]
        for c0 in range(0, n, ch):
            o_ref[:, c0:c0 + ch] = _dot_nt(a_v, w_ref[c0:c0 + ch, :]).astype(BF16)

    return pl.pallas_call(
        body, name=name, grid=(p // tm,),
        in_specs=[pl.BlockSpec((tm, k), lambda i: (i, 0)), VM],
        out_specs=pl.BlockSpec((tm, n), lambda i: (i, 0)),
        out_shape=jax.ShapeDtypeStruct((p, n), BF16),
        compiler_params=_cparams("parallel"),
    )(a, w_t)


def _conv3(xp_ref, w_ref, r0):
    return (w_ref[0:1, :] * xp_ref[r0 + 6:r0 + 6 + BLK, :] + w_ref[1:2, :] * xp_ref[r0 + 7:r0 + 7 + BLK, :]
            + w_ref[2:3, :] * xp_ref[r0 + 8:r0 + 8 + BLK, :])


def _ffn_slab_specs(p):
    ncol = FFN // BLK
    return [pl.BlockSpec((p, BLK), lambda j: (0, j)), pl.BlockSpec((p, BLK), lambda j: (0, ncol + j)),
            pl.BlockSpec((FFN_K, BLK), lambda j: (0, j)), pl.BlockSpec((FFN_K, BLK), lambda j: (0, ncol + j)),
            pl.BlockSpec((1, BLK), lambda j: (0, j)), pl.BlockSpec((1, BLK), lambda j: (0, ncol + j))]


def _fill_shifted(dst, src_ref, nch):
    dst[0:8, :] = jnp.zeros((8, BLK), F32)
    for ci in range(nch):
        dst[8 + BLK * ci:8 + BLK * (ci + 1), :] = src_ref[BLK * ci:BLK * (ci + 1), :].astype(F32)


def _ffn_act(u0, fw, fb):
    p = u0.shape[0]
    nch = p // BLK
    nsl = FFN // BLK
    depth = 3

    def body(u_hbm, wg_ref, wv_ref, bg_ref, bv_ref, o_ref, dv_ref, dg_ref, xg, xv, buf, sems):
        j = pl.program_id(0)

        def fetch(s, half):
            col = pl.multiple_of((half * nsl + s) * BLK, BLK)
            return pltpu.make_async_copy(u_hbm.at[:, pl.ds(col, BLK)], buf.at[s % depth, half], sems.at[s % depth, half])

        @pl.when(j == 0)
        def _():
            for s in range(depth - 1):
                for half in range(2):
                    fetch(s, half).start()

        @pl.when(j + depth - 1 < nsl)
        def _():
            for half in range(2):
                fetch(j + depth - 1, half).start()

        for half in range(2):
            fetch(j, half).wait()
        g_ref, v_ref = buf.at[j % depth, 0], buf.at[j % depth, 1]
        _fill_shifted(xg, g_ref, nch)
        _fill_shifted(xv, v_ref, nch)
        for ci in range(nch):
            r0 = BLK * ci
            ug = _conv3(xg, wg_ref, r0) + bg_ref[...]
            uv = _conv3(xv, wv_ref, r0) + bv_ref[...]
            sg = jax.nn.sigmoid(ug)
            silu = ug * sg
            o_ref[r0:r0 + BLK, :] = (silu * uv).astype(BF16)
            dv_ref[r0:r0 + BLK, :] = silu.astype(BF16)
            dg_ref[r0:r0 + BLK, :] = (uv * (sg * (1.0 + ug * (1.0 - sg)))).astype(BF16)

    slab = pl.BlockSpec((p, BLK), lambda j: (0, j))
    return pl.pallas_call(
        body, name="ffn_act", grid=(nsl,),
        in_specs=[ANY] + _ffn_slab_specs(p)[2:],
        out_specs=[slab] * 3,
        out_shape=[jax.ShapeDtypeStruct((p, FFN), BF16)] * 3,
        scratch_shapes=[pltpu.VMEM((p + 8, BLK), F32)] * 2
        + [pltpu.VMEM((depth, 2, p, BLK), BF16), pltpu.SemaphoreType.DMA((depth, 2))],
        compiler_params=_cparams("arbitrary"),
    )(u0, fw, fw, fb, fb)


def _ffn_down_loss(act, wd, h1, tgt, gain):
    p = act.shape[0]
    tm = _row_tile(p)
    k = tm // BLK

    def body(*refs):
        a_ref, w_ref, h_ref = refs[:3]
        t_refs = refs[3:3 + k]
        g_ref, df_ref, da_ref, dy_ref, acc_ref = refs[3 + k:]
        i = pl.program_id(0)

        @pl.when(i == 0)
        def _():
            acc_ref[...] = jnp.zeros_like(acc_ref)

        ffn = _dot(a_ref[...], w_ref[...])
        t = jnp.concatenate([t_ref[...] for t_ref in t_refs], axis=0) if k > 1 else t_refs[0][...]
        diff = jnp.where(_rows(i, tm) >= BLK, h_ref[...] + _rms(ffn, g_ref[...]) - t, 0.0)
        dy = diff * (1.0 / D)
        dffn, dg = _rms_bwd(ffn, g_ref[...], dy)
        acc_ref[0:1, :] += dg
        acc_ref[1:2, :] += jnp.sum(diff * diff, axis=0, keepdims=True) * (0.5 / D)
        dy_ref[...] = dy
        dfb = dffn.astype(BF16)
        df_ref[...] = dfb
        for c0 in range(0, FFN, 256):
            da_ref[:, c0:c0 + 256] = _dot_nt(dfb, w_ref[c0:c0 + 256, :]).astype(BF16)

    def row(w):
        return pl.BlockSpec((tm, w), lambda i: (i, 0))

    return pl.pallas_call(
        body, name="ffn_down_loss", grid=(p // tm,),
        in_specs=[row(FFN), VM, row(D)] + _token_specs(tm) + [VM],
        out_specs=[row(D), row(FFN), row(D), pl.BlockSpec((8, D), lambda i: (0, 0))],
        out_shape=[jax.ShapeDtypeStruct((p, D), BF16), jax.ShapeDtypeStruct((p, FFN), BF16),
                   jax.ShapeDtypeStruct((p, D), F32), jax.ShapeDtypeStruct((8, D), F32)],
        compiler_params=_cparams("arbitrary"),
    )(act, wd, h1, *([tgt] * k), gain)


def _mm_tn(pieces, b, name, col_sums=False, comm=None):
    p, n = b.shape
    tk = 256
    nblk = [a.shape[1] // tk for a in pieces]
    offs = [sum(nblk[:q]) for q in range(len(pieces))]
    total = sum(nblk)
    npc = len(pieces)

    def body(*refs):
        a_refs, b_ref, o_ref = refs[:npc], refs[npc], refs[npc + 1]
        i = pl.program_id(0)
        for q, a_ref in enumerate(a_refs):
            @pl.when(jnp.logical_and(i >= offs[q], i < offs[q] + nblk[q]))
            def _(a_ref=a_ref):
                a_v = a_ref[...]
                o_ref[...] = _dot_tn(a_v, b_ref[...]).astype(BF16)
                if col_sums:
                    refs[npc + 2][...] = jnp.sum(a_v.astype(F32), axis=0, keepdims=True)

    def a_spec(q):
        return pl.BlockSpec((p, tk), lambda i: (0, jnp.clip(i - offs[q], 0, nblk[q] - 1)))

    out_specs = [pl.BlockSpec((tk, n), lambda i: (i, 0))]
    out_shape = [jax.ShapeDtypeStruct((total * tk, n), BF16)]
    if col_sums:
        out_specs.append(pl.BlockSpec((1, tk), lambda i: (0, i)))
        out_shape.append(jax.ShapeDtypeStruct((1, total * tk), F32))
    res, sent = _call(
        body, name=name, grid=(total,),
        in_specs=[a_spec(q) for q in range(npc)] + [VM],
        out_specs=out_specs, out_shape=out_shape, args=(*pieces, b), comm=comm,
        comm_mid=(3 * total) // 4 if hasattr(comm, "middle") else None)
    res = res if col_sums else res[0]
    return res if comm is None else (res, sent)


def _ffn_act_bwd(u0, dact, dact_dg, dact_dv, fw, act, dffn, comm=None):
    p = u0.shape[0]
    nch = p // BLK
    ncol = FFN // BLK

    def body(g_ref, v_ref, wg_ref, wv_ref, da_ref, lg_ref, lv_ref, act_ref, df_ref,
             dg_ref, dv_ref, gwg_ref, gwv_ref, gbg_ref, gbv_ref, gwd_ref, eg, ev):
        gwd_ref[...] = _dot_tn(act_ref[...], df_ref[...]).astype(BF16)
        eg[p:p + 8, :] = jnp.zeros((8, BLK), F32)
        ev[p:p + 8, :] = jnp.zeros((8, BLK), F32)
        for ci in range(nch):
            r0 = BLK * ci
            d = da_ref[r0:r0 + BLK, :].astype(F32)
            eg[r0:r0 + BLK, :] = d * lg_ref[r0:r0 + BLK, :].astype(F32)
            ev[r0:r0 + BLK, :] = d * lv_ref[r0:r0 + BLK, :].astype(F32)
        def fold(v):
            return jnp.sum(v.reshape(BLK // 8, 8, BLK), axis=0)

        for e_s, x_ref, w_ref, d_ref, gw_ref, gb_ref in ((eg, g_ref, wg_ref, dg_ref, gwg_ref, gbg_ref),
                                                        (ev, v_ref, wv_ref, dv_ref, gwv_ref, gbv_ref)):
            sums = [jnp.zeros((8, BLK), F32) for _ in range(FFN_K + 1)]
            for ci in range(nch):
                r0 = BLK * ci
                es = [e_s[r0 + t:r0 + t + BLK, :] for t in range(FFN_K)]
                du = w_ref[2:3, :] * es[0] + w_ref[1:2, :] * es[1] + w_ref[0:1, :] * es[2]
                if ci == 0:
                    du = jnp.where(_rows(0, BLK) >= PAD, du, 0.0)
                d_ref[r0:r0 + BLK, :] = du.astype(BF16)
                x = x_ref[r0:r0 + BLK, :].astype(F32)
                for j in range(FFN_K):
                    sums[j] = sums[j] + fold(es[FFN_K - 1 - j] * x)
                sums[FFN_K] = sums[FFN_K] + fold(es[0])
            for j in range(FFN_K):
                gw_ref[j:j + 1, :] = jnp.sum(sums[j], axis=0, keepdims=True)
            gb_ref[...] = jnp.sum(sums[FFN_K], axis=0, keepdims=True)

    slab = pl.BlockSpec((p, BLK), lambda j: (0, j))
    wspec = pl.BlockSpec((FFN_K, BLK), lambda j: (0, j))
    bspec = pl.BlockSpec((1, BLK), lambda j: (0, j))
    return _call(
        body, name="ffn_act_bwd", grid=(ncol,),
        in_specs=_ffn_slab_specs(p)[:4] + [slab] * 4 + [VM],
        out_specs=[slab, slab, wspec, wspec, bspec, bspec, pl.BlockSpec((BLK, D), lambda j: (j, 0))],
        out_shape=[jax.ShapeDtypeStruct((p, FFN), BF16)] * 2 + [jax.ShapeDtypeStruct((FFN_K, FFN), F32)] * 2
        + [jax.ShapeDtypeStruct((1, FFN), F32)] * 2 + [jax.ShapeDtypeStruct((FFN, D), BF16)],
        scratch=[pltpu.VMEM((p + 8, BLK), F32)] * 2,
        args=(u0, u0, fw, fw, dact, dact_dg, dact_dv, act, dffn), comm=comm)


def _ffn_in_bwd(dug, duv, w_upt, h1, dy, gain, comm=None):
    p = h1.shape[0]
    tm = _row_tile(p)

    def body(dg_ref, dv_ref, w_ref, h_ref, dy_ref, g_ref, o_ref, acc_ref):
        i = pl.program_id(0)

        @pl.when(i == 0)
        def _():
            acc_ref[...] = jnp.zeros_like(acc_ref)

        dn = _dot(dg_ref[...], w_ref[0:FFN, :]) + _dot(dv_ref[...], w_ref[FFN:2 * FFN, :])
        dh, dg = _rms_bwd(h_ref[...], g_ref[...], dn)
        o_ref[...] = dy_ref[...] + dh
        acc_ref[0:1, :] += dg

    def row(w):
        return pl.BlockSpec((tm, w), lambda i: (i, 0))

    return _call(
        body, name="ffn_in_bwd", grid=(p // tm,),
        in_specs=[row(FFN), row(FFN), VM, row(D), row(D), VM],
        out_specs=[row(D), pl.BlockSpec((8, D), lambda i: (0, 0))],
        out_shape=[jax.ShapeDtypeStruct((p, D), F32), jax.ShapeDtypeStruct((8, D), F32)],
        sem="arbitrary", args=(dug, duv, w_upt, h1, dy, gain), comm=comm)


def _mixer_bwd(dh1, mix, attn, conv, gates, c0, wa, wc, wo, vecs, comm=None):
    p = dh1.shape[0]
    tm = _row_tile(p)

    def body(dh_ref, mix_ref, at_ref, cv_ref, gt_ref, c0_ref, wa_ref, wc_ref, wo_ref, v_ref,
             dmix_ref, dat_ref, dcv_ref, dgt_ref, dao_ref, dc0_ref, acc_ref):
        i = pl.program_id(0)

        @pl.when(i == 0)
        def _():
            acc_ref[...] = jnp.zeros_like(acc_ref)

        dmix, dgp = _rms_bwd(mix_ref[...], v_ref[3:4, :], dh_ref[...])
        dmix = dmix.astype(BF16)
        dmix_ref[...] = dmix
        dmg = _dot_nt(dmix, wo_ref[...])
        sa = jax.nn.sigmoid(gt_ref[:, 0:D].astype(F32))
        sc = jax.nn.sigmoid(gt_ref[:, D:2 * D].astype(F32))
        dat = dmg * sa
        dcv = dmg * sc
        dgt_ref[:, 0:D] = (dmg * at_ref[...].astype(F32) * sa * (1.0 - sa)).astype(BF16)
        dgt_ref[:, D:2 * D] = (dmg * cv_ref[...].astype(F32) * sc * (1.0 - sc)).astype(BF16)
        datb = dat.astype(BF16)
        dcvb = dcv.astype(BF16)
        dat_ref[...] = datb
        dcv_ref[...] = dcvb
        dao_ref[...] = _dot_nt(datb, wa_ref[...]).astype(BF16)
        dc1 = _dot_nt(dcvb, wc_ref[...])
        dc0, dlg, dlb = _lnsilu_bwd(c0_ref[...], v_ref[0:1, :], v_ref[1:2, :], dc1)
        dc0_ref[...] = dc0
        acc_ref[0:1, :] += dgp
        acc_ref[1:2, :] += jnp.sum(dcv, axis=0, keepdims=True)
        acc_ref[2:3, :] += dlg
        acc_ref[3:4, :] += dlb

    def row(w):
        return pl.BlockSpec((tm, w), lambda i: (i, 0))

    return _call(
        body, name="mixer_bwd", grid=(p // tm,),
        in_specs=[row(D), row(D), row(D), row(D), row(2 * D), row(D), VM, VM, VM, VM],
        out_specs=[row(D), row(D), row(D), row(2 * D), row(D), row(D), pl.BlockSpec((8, D), lambda i: (0, 0))],
        out_shape=[jax.ShapeDtypeStruct((p, D), BF16)] * 3 + [jax.ShapeDtypeStruct((p, 2 * D), BF16),
                                                             jax.ShapeDtypeStruct((p, D), BF16),
                                                             jax.ShapeDtypeStruct((p, D), F32),
                                                             jax.ShapeDtypeStruct((8, D), F32)],
        sem="arbitrary", args=(dh1, mix, attn, conv, gates, c0, wa, wc, wo, vecs), comm=comm)


def _conv31_bwd(ag, dc0, w32, tn_pairs, comm=None):
    p = ag.shape[0]
    nch = p // BLK
    npair = len(tn_pairs)

    def body(*refs):
        a_ref, g_ref, dc_ref, w_ref = refs[:4]
        tn_a, tn_b = refs[4:4 + npair], refs[4 + npair:4 + 2 * npair]
        da_ref, dg_ref, gw_ref, gb_ref = refs[4 + 2 * npair:8 + 2 * npair]
        tn_o = refs[8 + 2 * npair:8 + 3 * npair]
        gp, dp = refs[8 + 3 * npair:]
        for ta, tb, to in zip(tn_a, tn_b, tn_o):
            to[...] = _dot_tn(ta[...], tb[...]).astype(BF16)
        gp[0:32, :] = jnp.zeros((32, BLK), F32)
        dp[p:p + 32, :] = jnp.zeros((32, BLK), F32)
        bsum = jnp.zeros((BLK, BLK), F32)
        for ci in range(nch):
            r0 = BLK * ci
            glu = a_ref[r0:r0 + BLK, :].astype(F32) * jax.nn.sigmoid(g_ref[r0:r0 + BLK, :].astype(F32))
            if ci == 0:
                glu = jnp.where(_rows(0, BLK) >= PAD, glu, 0.0)
            gp[32 + r0:32 + r0 + BLK, :] = glu
            d = dc_ref[r0:r0 + BLK, :]
            dp[r0:r0 + BLK, :] = d
            bsum = bsum + d
        gb_ref[...] = jnp.sum(bsum, axis=0, keepdims=True)
        for ci in range(nch):
            r0 = BLK * ci
            acc = jnp.zeros((BLK, BLK), F32)
            for j in range(CONV_K):
                acc = acc + w_ref[j:j + 1, :] * dp[r0 + 30 - j:r0 + 30 - j + BLK, :]
            if ci == 0:
                acc = jnp.where(_rows(0, BLK) >= PAD, acc, 0.0)
            a = a_ref[r0:r0 + BLK, :].astype(F32)
            sg = jax.nn.sigmoid(g_ref[r0:r0 + BLK, :].astype(F32))
            da_ref[r0:r0 + BLK, :] = (acc * sg).astype(BF16)
            dg_ref[r0:r0 + BLK, :] = (acc * a * sg * (1.0 - sg)).astype(BF16)
        sub = BLK // 2
        accs = [jnp.zeros((8, BLK), F32) for _ in range(CONV_K)]
        for r0 in range(0, p, sub):
            d = dp[r0:r0 + sub, :]
            for j in range(CONV_K):
                prod = d * gp[r0 + j + 2:r0 + j + 2 + sub, :]
                accs[j] = accs[j] + jnp.sum(prod.reshape(sub // 8, 8, BLK), axis=0)
        for j in range(CONV_K):
            gw_ref[j:j + 1, :] = jnp.sum(accs[j], axis=0, keepdims=True)
        gw_ref[CONV_K:32, :] = jnp.zeros((32 - CONV_K, BLK), F32)

    slab = pl.BlockSpec((p, BLK), lambda j: (0, j))
    return _call(
        body, name="conv31_bwd", grid=(D // BLK,),
        in_specs=[slab, pl.BlockSpec((p, BLK), lambda j: (0, 8 + j)), slab, pl.BlockSpec((32, BLK), lambda j: (0, j))]
        + [slab] * npair + [VM] * npair,
        out_specs=[slab, slab, pl.BlockSpec((32, BLK), lambda j: (0, j)), pl.BlockSpec((1, BLK), lambda j: (0, j))]
        + [pl.BlockSpec((BLK, D), lambda j: (j, 0))] * npair,
        out_shape=[jax.ShapeDtypeStruct((p, D), BF16)] * 2 + [jax.ShapeDtypeStruct((32, D), F32),
                                                             jax.ShapeDtypeStruct((1, D), F32)]
        + [jax.ShapeDtypeStruct((D, D), BF16)] * npair,
        scratch=[pltpu.VMEM((p + 32, BLK), F32)] * 2,
        args=(ag, ag, dc0, w32, *[a for a, _ in tn_pairs], *[b for _, b in tn_pairs]), comm=comm)


def _attn_bwd(q, kv, dao, sinks, tabs, comm=None):
    p = q.shape[0]
    nb = p // BLK

    def body(q_ref, km_ref, kp_ref, kc_ref, do_ref, sink_ref, t_ref, dqkv_ref, dsink_ref, carry, macc):
        i = pl.program_id(0)
        n = nb - 1 - i

        @pl.when(i == 0)
        def _():
            carry[...] = jnp.zeros_like(carry)
            macc[...] = jnp.zeros_like(macc)
            dsink_ref[...] = jnp.zeros_like(dsink_ref)

        lo = lax.broadcasted_iota(jnp.int32, (BLK, BLK), 1) < HEAD_DIM
        lane8 = lax.broadcasted_iota(jnp.int32, (8, BLK), 1)
        c, s1, s2 = t_ref[:, 0:128], -t_ref[:, 128:256], -t_ref[:, 256:384]
        dk = jnp.zeros((N_KEY, BLK), F32)
        dv = jnp.zeros((N_KEY, BLK), F32)
        for h in range(2):
            qs, k2, v2, bias, lok = _attn_setup(n, h, q_ref, km_ref, kp_ref, kc_ref)
            dos = _stack_heads(do_ref, h, lo)
            st = _dot_nt(k2, qs)
            dpt = _dot_nt(v2, dos)
            p_parts, ds_parts = [], []
            for g in range(8):
                cols = slice(BLK * g, BLK * (g + 1))
                pn, ps = _attn_head(st[:, cols], bias, sink_ref[0, 8 * h + g])
                dp = dpt[:, cols]
                delta = jnp.sum(pn * dp, axis=0, keepdims=True)
                ds_parts.append((pn * (dp - delta)).astype(BF16))
                p_parts.append(pn.astype(BF16))
                dsk = -jnp.sum(ps * delta, axis=1, keepdims=True)
                dsink_ref[...] += jnp.where(lane8 == 8 * h + g, dsk, 0.0)
            dst = jnp.concatenate(ds_parts, axis=1)
            pt = jnp.concatenate(p_parts, axis=1)
            dq = _dot_tn(dst, k2)
            for jp in range(4):
                lo_c = BLK * (4 * h + jp)
                dqkv_ref[:, lo_c:lo_c + BLK] = (_rope(_unstack_heads(dq, jp, lo), c, s1, s2) * SCALE).astype(BF16)
            dk2 = _dot(dst, qs)
            dv2 = _dot(pt, dos)
            dk2 = dk2 + pltpu.roll(dk2, HEAD_DIM, 1)
            dv2 = dv2 + pltpu.roll(dv2, HEAD_DIM, 1)
            own = lok if h == 0 else jnp.logical_not(lok)
            dk = jnp.where(own, dk2, dk)
            dv = jnp.where(own, dv2, dv)
        macc[:, 0:BLK] += dk[2 * BLK:N_KEY]
        macc[:, BLK:2 * BLK] += dv[2 * BLK:N_KEY]
        last = (n == 0).astype(F32)
        zpad = jnp.zeros((PAD, BLK), F32)
        dk_c = dk[BLK:2 * BLK] + carry[:, 0:BLK] + last * jnp.concatenate([zpad, macc[:, 0:BLK]], axis=0)
        dv_c = dv[BLK:2 * BLK] + carry[:, BLK:2 * BLK] + last * jnp.concatenate([zpad, macc[:, BLK:2 * BLK]], axis=0)
        carry[:, 0:BLK] = dk[0:BLK]
        carry[:, BLK:2 * BLK] = dv[0:BLK]
        dqkv_ref[:, D:D + BLK] = _rope(dk_c, c, s1, s2).astype(BF16)
        dqkv_ref[:, D + BLK:D + 2 * BLK] = dv_c.astype(BF16)

    def rev(w):
        return pl.BlockSpec((BLK, w), lambda i: (nb - 1 - i, 0))

    return _call(
        body, name="attn_bwd", grid=(nb,),
        in_specs=[rev(D),
                  pl.BlockSpec((BLK, 256), lambda i: (0, 0)),
                  pl.BlockSpec((BLK, 256), lambda i: (jnp.maximum(nb - 2 - i, 0), 0)),
                  rev(256), rev(D),
                  pl.BlockSpec(memory_space=pltpu.SMEM), rev(384)],
        out_specs=[rev(QKV_W), pl.BlockSpec((8, BLK), lambda i: (0, 0))],
        out_shape=[jax.ShapeDtypeStruct((p, QKV_W), BF16), jax.ShapeDtypeStruct((8, BLK), F32)],
        scratch=[pltpu.VMEM((BLK, 256), F32), pltpu.VMEM((N_META, 256), F32)], sem="arbitrary",
        args=(q, kv, kv, kv, dao, sinks, tabs), comm=comm)


def _in_bwd(dqkv, da, dg, dgt, w_int, h0p, dh1, gain, comm=None):
    p = h0p.shape[0]
    tm = _row_tile(p)
    nt = p // tm
    first_rows = tm - BLK

    def body(dq_ref, da_ref, dg_ref, dt_ref, w_ref, h_ref, dh_ref, g_ref, gx_ref, dm_ref, acc_ref, buf, sems):
        i = pl.program_id(0)
        slot = i % 2

        @pl.when(i == 0)
        def _():
            acc_ref[...] = jnp.zeros_like(acc_ref)

        dn = (_dot(dq_ref[...], w_ref[0:QKV_W, :]) + _dot(da_ref[...], w_ref[QKV_W:QKV_W + D, :])
              + _dot(dg_ref[...], w_ref[QKV_W + D:QKV_W + 2 * D, :]) + _dot(dt_ref[...], w_ref[QKV_W + 2 * D:IN_W, :]))
        dh, dgain = _rms_bwd(h_ref[...], g_ref[...], dn)
        dh0 = dh_ref[...] + dh
        acc_ref[0:1, :] += dgain
        buf[slot] = dh0

        @pl.when(i == 0)
        def _():
            dm_ref[...] = dh0[PAD:BLK]

        def first_copy():
            return pltpu.make_async_copy(buf.at[0, pl.ds(BLK, first_rows), :], gx_ref.at[pl.ds(0, first_rows), :], sems.at[0])

        def tile_copy(j, s):
            return pltpu.make_async_copy(buf.at[s], gx_ref.at[pl.ds(pl.multiple_of(j * tm - BLK, BLK), tm), :], sems.at[s])

        if first_rows:
            @pl.when(i == 1)
            def _():
                first_copy().wait()

        @pl.when(i >= 2)
        def _():
            tile_copy(i - 1, 1 - slot).wait()

        if first_rows:
            @pl.when(i == 0)
            def _():
                first_copy().start()

        @pl.when(i > 0)
        def _():
            tile_copy(i, slot).start()

        @pl.when(i == nt - 1)
        def _():
            tile_copy(i, slot).wait()

    def row(w):
        return pl.BlockSpec((tm, w), lambda i: (i, 0))

    return _call(
        body, name="in_bwd", grid=(nt,),
        in_specs=[row(QKV_W), row(D), row(D), row(2 * D), VM, row(D), row(D), VM],
        out_specs=[ANY, pl.BlockSpec((N_META, D), lambda i: (0, 0)), pl.BlockSpec((8, D), lambda i: (0, 0))],
        out_shape=[jax.ShapeDtypeStruct((p - BLK, D), F32), jax.ShapeDtypeStruct((N_META, D), F32),
                   jax.ShapeDtypeStruct((8, D), F32)],
        scratch=[pltpu.VMEM((2, tm, D), F32), pltpu.SemaphoreType.DMA((2,))],
        sem="arbitrary", args=(dqkv, da, dg, dgt, w_int, h0p, dh1, gain), comm=comm)


def _sum_slots(slots, name):
    r = slots.shape[0] // N_DEV
    cols = slots.shape[1]
    tr = r if r <= 352 else (r // 2 if (r // 2) % 16 == 0 else r // 3)
    steps = r // tr

    def body(*refs):
        acc = refs[0][...].astype(F32)
        for s in range(1, N_DEV):
            acc = acc + refs[s][...].astype(F32)
        refs[N_DEV][...] = acc

    return pl.pallas_call(
        body, name=name, grid=(steps,),
        in_specs=[pl.BlockSpec((tr, cols), functools.partial(lambda i, s: (s * steps + i, 0), s=s)) for s in range(N_DEV)],
        out_specs=pl.BlockSpec((tr, cols), lambda i: (i, 0)),
        out_shape=jax.ShapeDtypeStruct((r, cols), F32),
        compiler_params=_cparams("parallel"),
    )(*([slots] * N_DEV))


def _adamw_math(w, g, m, v):
    m_n = ADAM_B1 * m + (1.0 - ADAM_B1) * g
    v_n = ADAM_B2 * v + (1.0 - ADAM_B2) * jnp.square(g)
    m_hat = m_n / (1.0 - ADAM_B1 ** ADAM_STEP)
    v_hat = v_n / (1.0 - ADAM_B2 ** ADAM_STEP)
    return -ADAM_LR * (m_hat / (jnp.sqrt(v_hat) + ADAM_EPS) + ADAM_WD * w), m_n, v_n


def _sum_adamw(parts, w, m, v, name, nslots=N_DEV):
    r, cols = w.shape
    rs = r // len(parts)
    tr = rs if rs <= 352 else (rs // 2 if (rs // 2) % 16 == 0 else rs // 3)
    steps = rs // tr

    def body(*refs):
        w_ref, m_ref, v_ref, g_ref, d_ref, nm_ref, nv_ref = refs[nslots * len(parts):]
        i = pl.program_id(0)
        for q in range(len(parts)):
            @pl.when(i // steps == q)
            def _(q=q):
                g = refs[nslots * q][...].astype(F32)
                for s in range(1, nslots):
                    g = g + refs[nslots * q + s][...].astype(F32)
                g_ref[...] = g
                d_ref[...], nm_ref[...], nv_ref[...] = _adamw_math(w_ref[...], g, m_ref[...], v_ref[...])

    def slot_spec(q, s):
        return pl.BlockSpec((tr, cols), lambda i: (s * steps + jnp.clip(i - q * steps, 0, steps - 1), 0))

    spec = pl.BlockSpec((tr, cols), lambda i: (i, 0))
    return pl.pallas_call(
        body, name=name, grid=(steps * len(parts),),
        in_specs=[slot_spec(q, s) for q in range(len(parts)) for s in range(nslots)] + [spec] * 3,
        out_specs=[spec] * 4, out_shape=[jax.ShapeDtypeStruct((r, cols), F32)] * 4,
        compiler_params=_cparams("parallel"),
    )(*[a for a in parts for _ in range(nslots)], w, m, v)


def _adamw_many(ws, gs, ms, vs, name):
    n = len(ws)

    def body(*refs):
        w, g, m, v = refs[0:n], refs[n:2 * n], refs[2 * n:3 * n], refs[3 * n:4 * n]
        d, nm, nv = refs[4 * n:5 * n], refs[5 * n:6 * n], refs[6 * n:7 * n]
        for k in range(n):
            d[k][...], nm[k][...], nv[k][...] = _adamw_math(w[k][...], g[k][...], m[k][...], v[k][...])

    outs = pl.pallas_call(
        body, name=name, in_specs=[VM] * (4 * n), out_specs=[VM] * (3 * n),
        out_shape=[jax.ShapeDtypeStruct(a.shape, F32) for a in ws] * 3,
    )(*ws, *gs, *ms, *vs)
    return outs[0:n], outs[n:2 * n], outs[2 * n:3 * n]


def _rope_tables(p):
    half = ROT_DIM // 2
    lane = jnp.arange(BLK)
    seg = (lane % HEAD_DIM) // half
    inv_freq = ROPE_THETA ** (-(lane % half).astype(F32) * 2.0 / ROT_DIM)
    pos = (jnp.arange(p) - PAD).astype(F32)
    ang = pos[:, None] * inv_freq[None, :]
    cos = jnp.cos(ang)
    sin = jnp.sin(ang)
    c = jnp.where(seg[None, :] < 2, cos, 1.0)
    s1 = jnp.where(seg[None, :] == 0, -sin, 0.0)
    s2 = jnp.where(seg[None, :] == 1, sin, 0.0)
    return jnp.concatenate([c, s1, s2], axis=1).astype(F32)


def _flat_pack(parts, rows):
    flat = jnp.concatenate([a.reshape(-1).astype(F32) for a in parts])
    return jnp.pad(flat, (0, rows * D - flat.shape[0])).reshape(rows, D)


def _flat_unpack(pack, shapes):
    flat = pack.reshape(-1)
    out, off = [], 0
    for s in shapes:
        size = 1
        for e in s:
            size *= e
        out.append(flat[off:off + size].reshape(s))
        off += size
    return out


def kernel(x, meta_tokens, norm_pre_mix, norm_post_mix, w_in, b_in, attn_sinks, w_attn_proj, conv_dw_w, conv_dw_b, conv_ln_g, conv_ln_b, w_conv_proj, b_conv_proj, w_out, norm_pre_ffn, norm_post_ffn, w_up, ffn_dw_w, ffn_dw_b, w_down, loss_target, m_meta_tokens, m_norm_pre_mix, m_norm_post_mix, m_w_in, m_b_in, m_attn_sinks, m_w_attn_proj, m_conv_dw_w, m_conv_dw_b, m_conv_ln_g, m_conv_ln_b, m_w_conv_proj, m_b_conv_proj, m_w_out, m_norm_pre_ffn, m_norm_post_ffn, m_w_up, m_ffn_dw_w, m_ffn_dw_b, m_w_down, v_meta_tokens, v_norm_pre_mix, v_norm_post_mix, v_w_in, v_b_in, v_attn_sinks, v_w_attn_proj, v_conv_dw_w, v_conv_dw_b, v_conv_ln_g, v_conv_ln_b, v_w_conv_proj, v_b_conv_proj, v_w_out, v_norm_pre_ffn, v_norm_post_ffn, v_w_up, v_ffn_dw_w, v_ffn_dw_b, v_w_down):
    seq = x.shape[1]
    p = seq + BLK
    me = 4 * lax.axis_index("x") + 2 * lax.axis_index("y") + lax.axis_index("c")
    in_cols = w_in.shape[2]
    up_cols = w_up.shape[2]

    small = jnp.zeros((56, up_cols), F32)
    small = small.at[0:N_META, 0:BLK].set(meta_tokens)
    small = small.at[16:16 + CONV_K, 0:BLK].set(conv_dw_w[0])
    small = small.at[48:48 + FFN_K, :].set(ffn_dw_w[0])
    w_int, small_all = _exchange(_Both(_GatherRelay(w_in[0].T.astype(BF16)), _Gather([small])), "gather_w_in")
    small_all = small_all.reshape(N_DEV, 56, up_cols)
    meta_full = small_all[:, 0:N_META, 0:BLK].transpose(1, 0, 2).reshape(N_META, D)
    cdw = small_all[:, 16:16 + CONV_K, 0:BLK].transpose(1, 0, 2).reshape(CONV_K, D)
    cdw32 = jnp.pad(cdw, ((0, 32 - CONV_K), (0, 0)))
    fdw = small_all[:, 48:48 + FFN_K, :].transpose(1, 0, 2).reshape(FFN_K, 2 * FFN)

    tabs = _rope_tables(p)
    vecs = jnp.concatenate([conv_ln_g, conv_ln_b, b_conv_proj, norm_post_mix, norm_pre_ffn, jnp.zeros((3, D), F32)], axis=0)

    (h0p, n1, q, kv, ag, gates), (wa, wc, wo) = _in_proj(
        x[0], meta_full, norm_pre_mix, w_int, b_in, tabs,
        comm=_Gather([w_attn_proj[0].astype(BF16), w_conv_proj[0].astype(BF16), w_out[0].astype(BF16)]))
    (ao,), (w_upt,) = _attn_fwd(q, kv, attn_sinks, comm=_Gather([w_up[0].T.astype(BF16)]))
    (c0,), (wd,) = _conv31_fwd(ag, cdw32, conv_dw_b, comm=_Gather([w_down[0].astype(BF16)]))
    c1, attn, conv, merged, mix, h1, n2 = _mixer_fwd(ao, c0, gates, h0p, wa, wc, wo, vecs)
    u0 = _mm_nt(n2, w_upt, "ffn_up")
    act, dact_dv, dact_dg = _ffn_act(u0, fdw, ffn_dw_b)
    dffn, dact, dy, acc_f = _ffn_down_loss(act, wd, h1, loss_target[0], norm_post_ffn)

    (dug, duv, gfw_g, gfw_v, gfb_g, gfb_v, g_wd), _ = _ffn_act_bwd(u0, dact, dact_dg, dact_dv, fdw, act, dffn)
    g_wupt, (s_wd0,) = _mm_tn([dug, duv], n2, "grad_w_up", comm=_Scatter([g_wd], 0, 2))
    (dh1, acc_u), (s_wd1,) = _ffn_in_bwd(dug, duv, w_upt, h1, dy, norm_pre_ffn, comm=_Scatter([g_wd], 1, 2))
    (dmix, dat, dcv, dgt, dao, dc0, acc_m), (s_wup0,) = _mixer_bwd(
        dh1, mix, attn, conv, gates, c0, wa, wc, wo, vecs, comm=_Scatter([g_wupt], 0, 4))
    (da, dg, g_cdw, g_cdb, g_wo, g_wa, g_wc), (s_wup1, s_wup2, s_wup3) = _conv31_bwd(
        ag, dc0, cdw32, [(merged, dmix), (ao, dat), (c1, dcv)],
        comm=_Both(_Both(_Scatter([g_wupt], 1, 4), _Scatter([g_wupt], 2, 4)), _Scatter([g_wupt], 3, 4)))
    (dqkv, dsink), (s_wa, s_wc, s_wo) = _attn_bwd(q, kv, dao, attn_sinks, tabs, comm=_Scatter([g_wa, g_wc, g_wo]))
    loss_row = jnp.sum(acc_f[1:2, :], axis=1, keepdims=True)
    early = [loss_row, acc_m[0:1], dsink[0:1, 0:16], g_cdw[0:CONV_K], g_cdb,
             acc_m[2:3], acc_m[3:4], acc_m[1:2], acc_u[0:1], acc_f[0:1],
             jnp.concatenate([gfw_g, gfw_v], axis=1), jnp.concatenate([gfb_g, gfb_v], axis=1)]
    (g_wint, g_bin), (gathered_early,) = _mm_tn([dqkv, da, dg, dgt], n1, "grad_w_in", col_sums=True,
                                                comm=_Gather([_flat_pack(early, 64)]))
    (from_sibling,) = _exchange(_SiblingSwap(g_wint), "swap_w_in")
    (grad_x2d, dmeta, acc_i), (s_win,) = _in_bwd(dqkv, da, dg, dgt, w_int, h0p, dh1, norm_pre_mix,
                                                 comm=_ChipScatter(_pair_add(g_wint, from_sibling)))

    big = []
    for nm, parts, nslots, w, m, v, tr in (
            ("w_in", [s_win], N_CHIP, w_in, m_w_in, v_w_in, True), ("w_up", [s_wup0, s_wup1, s_wup2, s_wup3], N_DEV, w_up, m_w_up, v_w_up, True),
            ("w_attn_proj", [s_wa], N_DEV, w_attn_proj, m_w_attn_proj, v_w_attn_proj, False),
            ("w_conv_proj", [s_wc], N_DEV, w_conv_proj, m_w_conv_proj, v_w_conv_proj, False),
            ("w_out", [s_wo], N_DEV, w_out, m_w_out, v_w_out, False),
            ("w_down", [s_wd0, s_wd1], N_DEV, w_down, m_w_down, v_w_down, False)):
        ins = [a[0].T if tr else a[0] for a in (w, m, v)]
        big.append(tuple((o.T if tr else o)[None] for o in _sum_adamw(parts, *ins, "update_" + nm, nslots)))

    late = [dmeta, acc_i[0:1], g_bin]
    (gathered_late,) = _exchange(_Gather([_flat_pack(late, 24)]), "gather_small_grads")
    g_meta, g_npm, g_bi = _flat_unpack(_sum_slots(gathered_late, "sum_late_grads"), [a.shape for a in late])
    tot = _flat_unpack(_sum_slots(gathered_early, "sum_small_grads"), [a.shape for a in early])
    (loss, g_nqm, g_sk, g_cw, g_cb, g_lg, g_lb, g_bc, g_npf, g_nqf, g_fw, g_fb) = tot
    loss = loss.reshape(())
    g_meta = lax.dynamic_slice_in_dim(g_meta, me * BLK, BLK, axis=1)
    g_cw = lax.dynamic_slice_in_dim(g_cw, me * BLK, BLK, axis=1)[None]
    g_fw = lax.dynamic_slice_in_dim(g_fw, me * up_cols, up_cols, axis=1)[None]

    sm_w = [meta_tokens, norm_pre_mix, norm_post_mix, b_in, attn_sinks, conv_dw_w, conv_dw_b, conv_ln_g, conv_ln_b,
            b_conv_proj, norm_pre_ffn, norm_post_ffn, ffn_dw_w, ffn_dw_b]
    sm_g = [g_meta, g_npm, g_nqm, g_bi, g_sk, g_cw, g_cb, g_lg, g_lb, g_bc, g_npf, g_nqf, g_fw, g_fb]
    sm_m = [m_meta_tokens, m_norm_pre_mix, m_norm_post_mix, m_b_in, m_attn_sinks, m_conv_dw_w, m_conv_dw_b, m_conv_ln_g,
            m_conv_ln_b, m_b_conv_proj, m_norm_pre_ffn, m_norm_post_ffn, m_ffn_dw_w, m_ffn_dw_b]
    sm_v = [v_meta_tokens, v_norm_pre_mix, v_norm_post_mix, v_b_in, v_attn_sinks, v_conv_dw_w, v_conv_dw_b, v_conv_ln_g,
            v_conv_ln_b, v_b_conv_proj, v_norm_pre_ffn, v_norm_post_ffn, v_ffn_dw_w, v_ffn_dw_b]
    swap = lambda a: jnp.transpose(a, (1, 0, 2)) if a.ndim == 3 else a
    sm_d, sm_nm, sm_nv = ([swap(o) for o in outs] for outs in
                          _adamw_many(*([swap(a) for a in group] for group in (sm_w, sm_g, sm_m, sm_v)), "adamw_small"))

    order = ["meta_tokens", "norm_pre_mix", "norm_post_mix", "w_in", "b_in", "attn_sinks", "w_attn_proj", "conv_dw_w",
             "conv_dw_b", "conv_ln_g", "conv_ln_b", "w_conv_proj", "b_conv_proj", "w_out", "norm_pre_ffn", "norm_post_ffn",
             "w_up", "ffn_dw_w", "ffn_dw_b", "w_down"]
    small_names = ["meta_tokens", "norm_pre_mix", "norm_post_mix", "b_in", "attn_sinks", "conv_dw_w", "conv_dw_b", "conv_ln_g",
                   "conv_ln_b", "b_conv_proj", "norm_pre_ffn", "norm_post_ffn", "ffn_dw_w", "ffn_dw_b"]
    big_names = ["w_in", "w_up", "w_attn_proj", "w_conv_proj", "w_out", "w_down"]
    table = {}
    for k, nm in enumerate(small_names):
        table[nm] = (sm_g[k], sm_d[k], sm_nm[k], sm_nv[k])
    for k, nm in enumerate(big_names):
        table[nm] = big[k]
    grad_x = grad_x2d[None]
    outs = [loss, grad_x]
    for field in range(4):
        outs += [table[nm][field] for nm in order]
    return tuple(outs)
```

```python
import functools

import jax
import jax.numpy as jnp
from jax import lax
from jax.experimental import pallas as pl
from jax.experimental.pallas import tpu as pltpu

F32 = jnp.float32
BF16 = jnp.bfloat16
MESH = pl.DeviceIdType.MESH

D = 1024
HEAD_DIM = 64
N_META = 16
BLK = 128
PAD = BLK - N_META
CONV_K = 31
FFN = 2816
FFN_K = 3
QKV_W = 1280
IN_W = 5376
ROT_DIM = 16
ROPE_THETA = 500000.0
RMS_EPS = 1e-6
LN_EPS = 1e-5
NEG_INF = -1e30
SCALE = HEAD_DIM ** -0.5
N_DEV = 8

ADAM_LR = 0.001
ADAM_B1 = 0.9
ADAM_B2 = 0.999
ADAM_EPS = 1e-08
ADAM_WD = 0.01
ADAM_STEP = 10

VMEM_BYTES_V7X = 64 * 1024 * 1024
VMEM_LIMIT = VMEM_BYTES_V7X - 8 * 1024 * 1024

NT = (((1,), (1,)), ((), ()))
TN = (((0,), (0,)), ((), ()))
VM = pl.BlockSpec(memory_space=pltpu.VMEM)
ANY = pl.BlockSpec(memory_space=pl.ANY)


def _cparams(*sem):
    return pltpu.CompilerParams(dimension_semantics=sem or None, vmem_limit_bytes=VMEM_LIMIT)


def _row_tile(p):
    return 384 if p % 384 == 0 else 128


def _dot(a, b):
    return jnp.dot(a, b, preferred_element_type=F32)


def _dot_nt(a, b):
    return lax.dot_general(a, b, NT, preferred_element_type=F32)


def _dot_tn(a, b):
    return lax.dot_general(a, b, TN, preferred_element_type=F32)


def _rms(x, g):
    return x * lax.rsqrt(jnp.mean(x * x, axis=-1, keepdims=True) + RMS_EPS) * g


def _lnsilu(x, g, b):
    mu = jnp.mean(x, axis=-1, keepdims=True)
    var = jnp.mean(jnp.square(x - mu), axis=-1, keepdims=True)
    z = (x - mu) * lax.rsqrt(var + LN_EPS) * g + b
    return z * jax.nn.sigmoid(z)


def _rms_bwd(x, g, dy):
    r = lax.rsqrt(jnp.mean(x * x, axis=-1, keepdims=True) + RMS_EPS)
    xn = x * r
    u = dy * g
    dg = jnp.sum(dy * xn, axis=0, keepdims=True)
    dx = r * (u - xn * jnp.mean(u * xn, axis=-1, keepdims=True))
    return dx, dg


def _lnsilu_bwd(x, g, b, dout):
    mu = jnp.mean(x, axis=-1, keepdims=True)
    xc = x - mu
    rs = lax.rsqrt(jnp.mean(xc * xc, axis=-1, keepdims=True) + LN_EPS)
    yh = xc * rs
    z = yh * g + b
    sg = jax.nn.sigmoid(z)
    dz = dout * (sg * (1.0 + z * (1.0 - sg)))
    dg = jnp.sum(dz * yh, axis=0, keepdims=True)
    db = jnp.sum(dz, axis=0, keepdims=True)
    dyh = dz * g
    dx = rs * (dyh - jnp.mean(dyh, axis=-1, keepdims=True) - yh * jnp.mean(dyh * yh, axis=-1, keepdims=True))
    return dx, dg, db


def _rope(v, c, s1, s2):
    return v * c + pltpu.roll(v, BLK - 8, 1) * s1 + pltpu.roll(v, 8, 1) * s2


def _rows(i, tm):
    return i * tm + lax.broadcasted_iota(jnp.int32, (tm, 1), 0)


def _place():
    return lax.axis_index("x"), lax.axis_index("y"), lax.axis_index("c")


def _blk(ref, idx, r, dtype):
    return ref.at[pl.ds(pl.multiple_of(idx * r, 16 if dtype == BF16 else 8), r), :]


class _Gather:
    def __init__(self, arrs):
        self.ins = list(arrs)
        n = len(arrs)
        self.out_shape = [jax.ShapeDtypeStruct((N_DEV * a.shape[0], a.shape[1]), a.dtype) for a in arrs]
        self.scratch = [pltpu.SemaphoreType.DMA((n, 7)), pltpu.SemaphoreType.DMA((n, 7)), pltpu.SemaphoreType.DMA((n,))]

    def _parts(self, ins, outs, sems):
        send_sems, recv_sems, local_sems = sems
        n = len(ins)
        x, y, c = _place()
        me, sibling = (x, y, c), (x, y, 1 - c)
        chips = [(1 - x, y), (x, 1 - y), (1 - x, 1 - y)]

        def rows(a, p):
            return _blk(outs[a], 4 * p[0] + 2 * p[1] + p[2], self.ins[a].shape[0], self.ins[a].dtype)

        def copy(a, k, block, to, src=None):
            return pltpu.make_async_remote_copy(
                src_ref=rows(a, block) if src is None else src, dst_ref=rows(a, block),
                send_sem=send_sems.at[a, k], recv_sem=recv_sems.at[a, k], device_id=to, device_id_type=MESH)

        mine = [pltpu.make_async_copy(ins[a], rows(a, me), local_sems.at[a]) for a in range(n)]
        first = []
        for a in range(n):
            first.append(copy(a, 0, me, sibling, src=ins[a]))
            first += [copy(a, 1 + j, me, (*chip, c), src=ins[a]) for j, chip in enumerate(chips)]
        return n, c, me, sibling, chips, copy, mine, first

    def start(self, ins, outs, sems):
        *_, mine, first = self._parts(ins, outs, sems)
        for cp in mine + first:
            cp.start()

    def middle(self, ins, outs, sems):
        n, c, me, sibling, chips, copy, _, _ = self._parts(ins, outs, sems)
        for j, chip in enumerate(chips):
            for a in range(n):
                copy(a, 1 + j, (*chip, c), me).wait_recv()
                copy(a, 4 + j, (*chip, c), sibling).start()

    def finish(self, ins, outs, sems, middle_done=False):
        if not middle_done:
            self.middle(ins, outs, sems)
        n, c, me, sibling, chips, copy, mine, first = self._parts(ins, outs, sems)
        passed = [copy(a, 4 + j, (*chip, c), sibling) for j, chip in enumerate(chips) for a in range(n)]
        for a in range(n):
            copy(a, 0, sibling, me).wait_recv()
            for j, chip in enumerate(chips):
                copy(a, 4 + j, (*chip, 1 - c), me).wait_recv()
        for cp in first + passed:
            cp.wait_send()
        for cp in mine:
            cp.wait()


class _GatherRelay:
    N_COPY = 13

    def __init__(self, arr):
        self.ins = [arr]
        self.r = arr.shape[0]
        self.out_shape = [jax.ShapeDtypeStruct((N_DEV * self.r, arr.shape[1]), arr.dtype)]
        self.scratch = [pltpu.SemaphoreType.DMA((self.N_COPY,)), pltpu.SemaphoreType.DMA((self.N_COPY,)),
                        pltpu.SemaphoreType.DMA]

    def _parts(self, ins, outs, sems):
        send_sems, recv_sems, local_sem = sems
        x, y, c = _place()
        r, half = self.r, self.r // 2
        out = outs[0]
        me, sib, xn, yn, dg = (x, y, c), (x, y, 1 - c), (1 - x, y, c), (x, 1 - y, c), (1 - x, 1 - y, c)
        sx, sy, sd = (1 - x, y, 1 - c), (x, 1 - y, 1 - c), (1 - x, 1 - y, 1 - c)
        lo, hi = (0, half), (half, half)

        def rows(p, part=(0, r)):
            return out.at[pl.ds(pl.multiple_of((4 * p[0] + 2 * p[1] + p[2]) * r + part[0], 16), part[1]), :]

        def own(part):
            return ins[0].at[pl.ds(part[0], part[1]), :]

        def copy(k, dev_rows, to, src=None):
            return pltpu.make_async_remote_copy(
                src_ref=dev_rows if src is None else src, dst_ref=dev_rows,
                send_sem=send_sems.at[k], recv_sem=recv_sems.at[k], device_id=to, device_id_type=MESH)

        mine = pltpu.make_async_copy(ins[0], rows(me), local_sem)
        first = [copy(0, rows(me), sib, src=ins[0]),
                 copy(1, rows(me, lo), xn, src=own(lo)), copy(3, rows(me, hi), yn, src=own(hi)),
                 copy(2, rows(me, hi), xn, src=own(hi)), copy(4, rows(me, lo), yn, src=own(lo))]
        arrive = {0: rows(sib), 1: rows(xn, lo), 2: rows(xn, hi), 3: rows(yn, hi), 4: rows(yn, lo),
                  5: rows(dg, lo), 6: rows(dg, hi), 7: rows(sx, lo), 8: rows(sx, hi), 9: rows(sy, hi),
                  10: rows(sy, lo), 11: rows(sd, lo), 12: rows(sd, hi)}
        relay = {1: [(5, rows(xn, lo), yn), (7, rows(xn, lo), sib)], 3: [(6, rows(yn, hi), xn), (9, rows(yn, hi), sib)],
                 2: [(8, rows(xn, hi), sib)], 4: [(10, rows(yn, lo), sib)],
                 5: [(11, rows(dg, lo), sib)], 6: [(12, rows(dg, hi), sib)]}
        return copy, mine, first, arrive, relay, me

    def start(self, ins, outs, sems):
        _, mine, first, _, _, _ = self._parts(ins, outs, sems)
        for cp in [mine] + first:
            cp.start()

    def finish(self, ins, outs, sems):
        copy, mine, first, arrive, relay, me = self._parts(ins, outs, sems)
        passed = []
        for k in (1, 3, 2, 4, 5, 6):
            copy(k, arrive[k], me).wait_recv()
            for k2, dev_rows, to in relay[k]:
                fwd = copy(k2, dev_rows, to)
                fwd.start()
                passed.append(fwd)
        for k in (0, 7, 8, 9, 10, 11, 12):
            copy(k, arrive[k], me).wait_recv()
        for cp in first + passed:
            cp.wait_send()
        mine.wait()


FLIPS = [(0, 0, 1), (1, 0, 0), (0, 1, 0), (1, 1, 0), (1, 0, 1), (0, 1, 1), (1, 1, 1)]


class _Scatter:
    def __init__(self, arrs, part=0, nparts=1):
        self.ins = list(arrs)
        self.part, self.nparts = part, nparts
        n = len(arrs)
        self.out_shape = [jax.ShapeDtypeStruct((a.shape[0] // nparts, a.shape[1]), a.dtype) for a in arrs]
        self.scratch = [pltpu.SemaphoreType.DMA((n, 7)), pltpu.SemaphoreType.DMA((n, 7)), pltpu.SemaphoreType.DMA((n,))]

    def _parts(self, ins, outs, sems):
        send_sems, recv_sems, local_sems = sems
        n = len(ins)
        x, y, c = _place()
        me = 4 * x + 2 * y + c

        def flip(v, f):
            return 1 - v if f else v

        def src(a, idx):
            r = self.ins[a].shape[0] // N_DEV
            rs = r // self.nparts
            return ins[a].at[pl.ds(pl.multiple_of(idx * r + self.part * rs, 16), rs), :]

        def dst(a, idx):
            rs = self.ins[a].shape[0] // N_DEV // self.nparts
            return outs[a].at[pl.ds(pl.multiple_of(idx * rs, 16), rs), :]

        mine = [pltpu.make_async_copy(src(a, me), dst(a, me), local_sems.at[a]) for a in range(n)]
        sends, recvs = [], []
        for k, f in enumerate(FLIPS):
            peer = (flip(x, f[0]), flip(y, f[1]), flip(c, f[2]))
            pidx = 4 * peer[0] + 2 * peer[1] + peer[2]
            for a in range(n):
                sends.append(pltpu.make_async_remote_copy(
                    src_ref=src(a, pidx), dst_ref=dst(a, me),
                    send_sem=send_sems.at[a, k], recv_sem=recv_sems.at[a, k], device_id=peer, device_id_type=MESH))
                recvs.append(functools.partial(
                    pltpu.make_async_remote_copy,
                    src_ref=src(a, pidx), dst_ref=dst(a, pidx),
                    send_sem=send_sems.at[a, k], recv_sem=recv_sems.at[a, k], device_id=peer, device_id_type=MESH))
        return mine, sends, recvs

    def start(self, ins, outs, sems):
        mine, sends, _ = self._parts(ins, outs, sems)
        for cp in mine + sends:
            cp.start()

    def finish(self, ins, outs, sems):
        mine, sends, recvs = self._parts(ins, outs, sems)
        for make in recvs:
            make().wait_recv()
        for cp in sends:
            cp.wait_send()
        for cp in mine:
            cp.wait()


N_CHIP = 4


class _SiblingSwap:
    def __init__(self, arr):
        self.ins = [arr]
        self.r = arr.shape[0] // N_DEV
        self.out_shape = [jax.ShapeDtypeStruct((N_CHIP * self.r, arr.shape[1]), arr.dtype)]
        self.scratch = [pltpu.SemaphoreType.DMA((N_CHIP,)), pltpu.SemaphoreType.DMA((N_CHIP,))]

    def _copies(self, ins, outs, sems):
        send_sems, recv_sems = sems
        x, y, c = _place()
        r = self.r
        return [pltpu.make_async_remote_copy(
            src_ref=ins[0].at[pl.ds(pl.multiple_of((2 * j + 1 - c) * r, 16), r), :],
            dst_ref=outs[0].at[pl.ds(j * r, r), :],
            send_sem=send_sems.at[j], recv_sem=recv_sems.at[j], device_id=(x, y, 1 - c), device_id_type=MESH)
            for j in range(N_CHIP)]

    def start(self, ins, outs, sems):
        for cp in self._copies(ins, outs, sems):
            cp.start()

    def finish(self, ins, outs, sems):
        for cp in self._copies(ins, outs, sems):
            cp.wait()


class _ChipScatter:
    def __init__(self, arr):
        self.ins = [arr]
        self.r = arr.shape[0] // N_CHIP
        self.out_shape = [jax.ShapeDtypeStruct(arr.shape, arr.dtype)]
        self.scratch = [pltpu.SemaphoreType.DMA((3,)), pltpu.SemaphoreType.DMA((3,)), pltpu.SemaphoreType.DMA]

    def _parts(self, ins, outs, sems):
        send_sems, recv_sems, local_sem = sems
        x, y, c = _place()
        r = self.r
        my_chip = 2 * x + y

        def rows(ref, j):
            return ref.at[pl.ds(pl.multiple_of(j * r, 16), r), :]

        mine = pltpu.make_async_copy(rows(ins[0], my_chip), rows(outs[0], my_chip), local_sem)
        sends, recvs = [], []
        for k, (fx, fy) in enumerate(((1, 0), (0, 1), (1, 1))):
            px, py = (1 - x if fx else x), (1 - y if fy else y)
            peer_chip = 2 * px + py
            sends.append(pltpu.make_async_remote_copy(
                src_ref=rows(ins[0], peer_chip), dst_ref=rows(outs[0], my_chip),
                send_sem=send_sems.at[k], recv_sem=recv_sems.at[k], device_id=(px, py, c), device_id_type=MESH))
            recvs.append(functools.partial(
                pltpu.make_async_remote_copy,
                src_ref=rows(ins[0], peer_chip), dst_ref=rows(outs[0], peer_chip),
                send_sem=send_sems.at[k], recv_sem=recv_sems.at[k], device_id=(px, py, c), device_id_type=MESH))
        return mine, sends, recvs

    def start(self, ins, outs, sems):
        mine, sends, _ = self._parts(ins, outs, sems)
        for cp in [mine] + sends:
            cp.start()

    def finish(self, ins, outs, sems):
        mine, sends, recvs = self._parts(ins, outs, sems)
        for make in recvs:
            make().wait_recv()
        for cp in sends:
            cp.wait_send()
        mine.wait()


def _pair_add(partial, recv):
    r = recv.shape[0] // N_CHIP
    cols = recv.shape[1]
    tr = r // 2 if (r // 2) % 16 == 0 else r
    steps = r // tr
    core = lax.axis_index("c").astype(jnp.int32).reshape(1)

    def body(c_ref, p_ref, s_ref, o_ref):
        o_ref[...] = (p_ref[...].astype(F32) + s_ref[...].astype(F32)).astype(BF16)

    spec = pl.BlockSpec((tr, cols), lambda j, i, c_ref: (j * steps + i, 0))
    return pl.pallas_call(
        body, name="pair_add",
        grid_spec=pltpu.PrefetchScalarGridSpec(
            num_scalar_prefetch=1, grid=(N_CHIP, steps),
            in_specs=[pl.BlockSpec((tr, cols), lambda j, i, c_ref: ((2 * j + c_ref[0]) * steps + i, 0)), spec],
            out_specs=spec),
        out_shape=jax.ShapeDtypeStruct(recv.shape, BF16),
        compiler_params=_cparams("parallel", "parallel"),
    )(core, partial, recv)


class _Both:
    def __init__(self, a, b):
        self.a, self.b = a, b
        self.ins = a.ins + b.ins
        self.out_shape = a.out_shape + b.out_shape
        self.scratch = a.scratch + b.scratch

    def _split(self, ins, outs, sems):
        ni, no, ns = len(self.a.ins), len(self.a.out_shape), len(self.a.scratch)
        return (ins[:ni], outs[:no], sems[:ns]), (ins[ni:], outs[no:], sems[ns:])

    def start(self, ins, outs, sems):
        ra, rb = self._split(ins, outs, sems)
        self.a.start(*ra)
        self.b.start(*rb)

    def finish(self, ins, outs, sems):
        ra, rb = self._split(ins, outs, sems)
        self.a.finish(*ra)
        self.b.finish(*rb)


def _exchange(comm, name):
    n, m = len(comm.ins), len(comm.out_shape)

    def body(*refs):
        ins, outs, sems = refs[:n], refs[n:n + m], refs[n + m:]
        comm.start(ins, outs, sems)
        comm.finish(ins, outs, sems)

    return pl.pallas_call(
        body, name=name, out_shape=comm.out_shape, in_specs=[ANY] * n, out_specs=[ANY] * m, scratch_shapes=comm.scratch,
    )(*comm.ins)


def _call(body, *, name, grid, in_specs, out_specs, out_shape, args, scratch=(), sem="parallel", comm=None,
          comm_mid=None):
    if comm is None:
        outs = pl.pallas_call(
            body, name=name, grid=grid, in_specs=list(in_specs), out_specs=list(out_specs), out_shape=list(out_shape),
            scratch_shapes=list(scratch), compiler_params=_cparams(sem))(*args)
        return outs, []
    n_in, n_out, n_sc = len(in_specs), len(out_specs), len(scratch)
    n_ci, n_co = len(comm.ins), len(comm.out_shape)
    last = grid[0] - 1

    def fused(*refs):
        ins, refs = refs[:n_in], refs[n_in:]
        c_ins, refs = refs[:n_ci], refs[n_ci:]
        outs, refs = refs[:n_out], refs[n_out:]
        c_outs, refs = refs[:n_co], refs[n_co:]
        sc, c_sems = refs[:n_sc], refs[n_sc:]
        step = pl.program_id(0)

        @pl.when(step == 0)
        def _():
            comm.start(c_ins, c_outs, c_sems)

        body(*ins, *outs, *sc)

        if comm_mid is not None:
            @pl.when(step == comm_mid)
            def _():
                comm.middle(c_ins, c_outs, c_sems)

        @pl.when(step == last)
        def _():
            if comm_mid is not None:
                comm.finish(c_ins, c_outs, c_sems, middle_done=True)
            else:
                comm.finish(c_ins, c_outs, c_sems)

    outs = pl.pallas_call(
        fused, name=name, grid=grid, in_specs=list(in_specs) + [ANY] * n_ci, out_specs=list(out_specs) + [ANY] * n_co,
        out_shape=list(out_shape) + comm.out_shape, scratch_shapes=list(scratch) + comm.scratch,
        compiler_params=_cparams("arbitrary"))(*args, *comm.ins)
    return outs[:n_out], outs[n_out:]


def _token_specs(tm):
    k = tm // BLK
    return [pl.BlockSpec((BLK, D), functools.partial(lambda i, t: (jnp.maximum(k * i + t - 1, 0), 0), t=t)) for t in range(k)]


def _in_proj(x2d, meta, gain, w_int, b_in, tabs, comm=None):
    p = x2d.shape[0] + BLK
    tm = _row_tile(p)
    k = tm // BLK

    def body(*refs):
        x_refs = refs[:k]
        m_ref, g_ref, w_ref, b_ref, t_ref, h_ref, n1_ref, q_ref, kv_ref, ag_ref, gt_ref = refs[k:]
        i = pl.program_id(0)
        head = jnp.concatenate([jnp.zeros((PAD, D), F32), m_ref[...]], axis=0)
        first = jnp.where(i == 0, head, x_refs[0][...])
        h = jnp.concatenate([first] + [r[...] for r in x_refs[1:]], axis=0) if k > 1 else first
        h_ref[...] = h
        n = _rms(h, g_ref[...]).astype(BF16)
        n1_ref[...] = n
        c, s1, s2 = t_ref[:, 0:128], t_ref[:, 128:256], t_ref[:, 256:384]

        def mm(c0, w):
            return _dot_nt(n, w_ref[c0:c0 + w, :]) + b_ref[:, c0:c0 + w]

        for j in range(4):
            acc = mm(256 * j, 256)
            for t in range(2):
                lo = 256 * j + 128 * t
                q_ref[:, lo:lo + 128] = (_rope(acc[:, 128 * t:128 * (t + 1)], c, s1, s2) * SCALE).astype(BF16)
        acc = mm(1024, 256)
        kv_ref[:, 0:128] = _rope(acc[:, 0:128], c, s1, s2).astype(BF16)
        kv_ref[:, 128:256] = acc[:, 128:256].astype(BF16)
        for j in range(8):
            ag_ref[:, 256 * j:256 * (j + 1)] = mm(QKV_W + 256 * j, 256).astype(BF16)
        for j in range(8):
            gt_ref[:, 256 * j:256 * (j + 1)] = mm(QKV_W + 2048 + 256 * j, 256).astype(BF16)

    def row(w):
        return pl.BlockSpec((tm, w), lambda i: (i, 0))

    return _call(
        body, name="in_proj", grid=(p // tm,),
        in_specs=_token_specs(tm) + [VM, VM, VM, VM, row(384)],
        out_specs=[row(D), row(D), row(D), row(256), row(2048), row(2048)],
        out_shape=[jax.ShapeDtypeStruct((p, D), F32)] + [jax.ShapeDtypeStruct((p, w), BF16) for w in (D, D, 256, 2048, 2048)],
        args=(x2d,) * k + (meta, gain, w_int, b_in, tabs), comm=comm,
        comm_mid=None if comm is None else (3 * (p // tm)) // 4)


N_KEY = 2 * BLK + N_META


def _attn_setup(n, h, q_ref, km_ref, kp_ref, kc_ref):
    lo = lax.broadcasted_iota(jnp.int32, (BLK, BLK), 1) < HEAD_DIM
    lok = lax.broadcasted_iota(jnp.int32, (N_KEY, BLK), 1) < HEAD_DIM

    def dup(lanes):
        cat = jnp.concatenate([kp_ref[:, lanes], kc_ref[:, lanes], km_ref[PAD:BLK, lanes]], axis=0).astype(F32)
        rolled = pltpu.roll(cat, HEAD_DIM, 1)
        return (jnp.where(lok, cat, rolled) if h == 0 else jnp.where(lok, rolled, cat)).astype(BF16)

    k2 = dup(slice(0, 128))
    v2 = dup(slice(128, 256))
    qs = _stack_heads(q_ref, h, lo)

    kr = lax.broadcasted_iota(jnp.int32, (BLK, BLK), 0)
    tq = BLK * n + lax.broadcasted_iota(jnp.int32, (BLK, BLK), 1) - PAD
    t_p = BLK * (n - 1) + kr - PAD
    t_c = BLK * n + kr - PAD
    ok_p = jnp.logical_and(t_p >= N_META, tq - t_p < BLK)
    ok_c = jnp.logical_and(t_c >= N_META, t_c <= tq)
    ok_m = lax.broadcasted_iota(jnp.int32, (N_META, BLK), 0) <= BLK * n + lax.broadcasted_iota(jnp.int32, (N_META, BLK), 1) - PAD
    bias = jnp.concatenate([jnp.where(ok, 0.0, NEG_INF).astype(F32) for ok in (ok_p, ok_c, ok_m)], axis=0)
    return qs, k2, v2, bias, lok


def _attn_head(s, bias, sink):
    s = s + bias
    m = jnp.maximum(jnp.max(s, axis=0, keepdims=True), sink)
    e = jnp.exp(s - m)
    es = jnp.exp(sink - m)
    inv = 1.0 / (jnp.sum(e, axis=0, keepdims=True) + es)
    return e * inv, es * inv


def _stack_heads(ref, h, lo):
    pieces = []
    for jp in range(4):
        v = ref[:, BLK * (4 * h + jp):BLK * (4 * h + jp + 1)]
        zero = jnp.zeros_like(v)
        pieces += [jnp.where(lo, v, zero), jnp.where(lo, zero, v)]
    return jnp.concatenate(pieces, axis=0)


def _unstack_heads(v, jp, lo):
    return jnp.where(lo, v[256 * jp:256 * jp + 128], v[256 * jp + 128:256 * jp + 256])


def _attn_fwd(q, kv, sinks, comm=None):
    p = q.shape[0]
    nb = p // BLK

    def body(q_ref, km_ref, kp_ref, kc_ref, sink_ref, o_ref):
        n = pl.program_id(0)
        lo = lax.broadcasted_iota(jnp.int32, (BLK, BLK), 1) < HEAD_DIM
        for h in range(2):
            qs, k2, v2, bias, _ = _attn_setup(n, h, q_ref, km_ref, kp_ref, kc_ref)
            st = _dot_nt(k2, qs)
            pt = jnp.concatenate(
                [_attn_head(st[:, BLK * g:BLK * (g + 1)], bias, sink_ref[0, 8 * h + g])[0].astype(BF16) for g in range(8)],
                axis=1)
            o = _dot_tn(pt, v2)
            for jp in range(4):
                o_ref[:, BLK * (4 * h + jp):BLK * (4 * h + jp + 1)] = _unstack_heads(o, jp, lo).astype(BF16)

    return _call(
        body, name="attn_fwd", grid=(nb,),
        in_specs=[pl.BlockSpec((BLK, D), lambda i: (i, 0)),
                  pl.BlockSpec((BLK, 256), lambda i: (0, 0)),
                  pl.BlockSpec((BLK, 256), lambda i: (jnp.maximum(i - 1, 0), 0)),
                  pl.BlockSpec((BLK, 256), lambda i: (i, 0)),
                  pl.BlockSpec(memory_space=pltpu.SMEM)],
        out_specs=[pl.BlockSpec((BLK, D), lambda i: (i, 0))],
        out_shape=[jax.ShapeDtypeStruct((p, D), BF16)],
        args=(q, kv, kv, kv, sinks), comm=comm)


def _conv31_fwd(ag, w32, b, comm=None):
    p = ag.shape[0]
    nch = p // BLK

    def body(a_ref, g_ref, w_ref, b_ref, o_ref, gp):
        gp[0:32, :] = jnp.zeros((32, BLK), F32)
        for ci in range(nch):
            r0 = BLK * ci
            glu = a_ref[r0:r0 + BLK, :].astype(F32) * jax.nn.sigmoid(g_ref[r0:r0 + BLK, :].astype(F32))
            if ci == 0:
                glu = jnp.where(_rows(0, BLK) >= PAD, glu, 0.0)
            gp[32 + r0:32 + r0 + BLK, :] = glu
        for ci in range(nch):
            r0 = BLK * ci
            acc = jnp.broadcast_to(b_ref[...], (BLK, BLK))
            for j in range(CONV_K):
                acc = acc + w_ref[j:j + 1, :] * gp[r0 + j + 2:r0 + j + 2 + BLK, :]
            o_ref[r0:r0 + BLK, :] = acc

    return _call(
        body, name="conv31_fwd", grid=(D // BLK,),
        in_specs=[pl.BlockSpec((p, BLK), lambda j: (0, j)), pl.BlockSpec((p, BLK), lambda j: (0, 8 + j)),
                  pl.BlockSpec((32, BLK), lambda j: (0, j)), pl.BlockSpec((1, BLK), lambda j: (0, j))],
        out_specs=[pl.BlockSpec((p, BLK), lambda j: (0, j))],
        out_shape=[jax.ShapeDtypeStruct((p, D), F32)],
        scratch=[pltpu.VMEM((p + 32, BLK), F32)],
        args=(ag, ag, w32, b), comm=comm, comm_mid=None if comm is None else (3 * (D // BLK)) // 4)


def _mixer_fwd(ao, c0, gates, h0p, wa, wc, wo, vecs):
    p = ao.shape[0]
    tm = _row_tile(p)

    def body(ao_ref, c0_ref, gt_ref, h_ref, wa_ref, wc_ref, wo_ref, v_ref,
             c1_ref, at_ref, cv_ref, mg_ref, mix_ref, h1_ref, n2_ref):
        i = pl.program_id(0)
        c1 = _lnsilu(c0_ref[...], v_ref[0:1, :], v_ref[1:2, :]).astype(BF16)
        c1_ref[...] = c1
        attn = _dot(ao_ref[...], wa_ref[...])
        conv = _dot(c1, wc_ref[...]) + v_ref[2:3, :]
        at_ref[...] = attn.astype(BF16)
        cv_ref[...] = conv.astype(BF16)
        merged = (jax.nn.sigmoid(gt_ref[:, 0:D].astype(F32)) * attn
                  + jax.nn.sigmoid(gt_ref[:, D:2 * D].astype(F32)) * conv).astype(BF16)
        mg_ref[...] = merged
        mix = _dot(merged, wo_ref[...])
        mix_ref[...] = mix
        h1 = jnp.where(_rows(i, tm) >= PAD, h_ref[...] + _rms(mix, v_ref[3:4, :]), 0.0)
        h1_ref[...] = h1
        n2_ref[...] = _rms(h1, v_ref[4:5, :]).astype(BF16)

    def row(w):
        return pl.BlockSpec((tm, w), lambda i: (i, 0))

    return pl.pallas_call(
        body, name="mixer_fwd", grid=(p // tm,),
        in_specs=[row(D), row(D), row(2 * D), row(D), VM, VM, VM, VM],
        out_specs=[row(D)] * 7,
        out_shape=[jax.ShapeDtypeStruct((p, D), t) for t in (BF16, BF16, BF16, BF16, F32, F32, BF16)],
        compiler_params=_cparams("parallel"),
    )(ao, c0, gates, h0p, wa, wc, wo, vecs)


def _mm_nt(a, w_t, name):
    p, k = a.shape
    n = w_t.shape[0]
    tm = _row_tile(p)
    ch = 512

    def body(a_ref, w_ref, o_ref):
        a_v = a_ref[...]
        for c0 in range(0, n, ch):
            o_ref[:, c0:c0 + ch] = _dot_nt(a_v, w_ref[c0:c0 + ch, :]).astype(BF16)

    return pl.pallas_call(
        body, name=name, grid=(p // tm,),
        in_specs=[pl.BlockSpec((tm, k), lambda i: (i, 0)), VM],
        out_specs=pl.BlockSpec((tm, n), lambda i: (i, 0)),
        out_shape=jax.ShapeDtypeStruct((p, n), BF16),
        compiler_params=_cparams("parallel"),
    )(a, w_t)


def _conv3(xp_ref, w_ref, r0):
    return (w_ref[0:1, :] * xp_ref[r0 + 6:r0 + 6 + BLK, :] + w_ref[1:2, :] * xp_ref[r0 + 7:r0 + 7 + BLK, :]
            + w_ref[2:3, :] * xp_ref[r0 + 8:r0 + 8 + BLK, :])


def _ffn_slab_specs(p):
    ncol = FFN // BLK
    return [pl.BlockSpec((p, BLK), lambda j: (0, j)), pl.BlockSpec((p, BLK), lambda j: (0, ncol + j)),
            pl.BlockSpec((FFN_K, BLK), lambda j: (0, j)), pl.BlockSpec((FFN_K, BLK), lambda j: (0, ncol + j)),
            pl.BlockSpec((1, BLK), lambda j: (0, j)), pl.BlockSpec((1, BLK), lambda j: (0, ncol + j))]


def _fill_shifted(dst, src_ref, nch):
    dst[0:8, :] = jnp.zeros((8, BLK), F32)
    for ci in range(nch):
        dst[8 + BLK * ci:8 + BLK * (ci + 1), :] = src_ref[BLK * ci:BLK * (ci + 1), :].astype(F32)


def _ffn_act(u0, fw, fb):
    p = u0.shape[0]
    nch = p // BLK
    nsl = FFN // BLK
    depth = 3

    def body(u_hbm, wg_ref, wv_ref, bg_ref, bv_ref, o_ref, dv_ref, dg_ref, xg, xv, buf, sems):
        j = pl.program_id(0)

        def fetch(s, half):
            col = pl.multiple_of((half * nsl + s) * BLK, BLK)
            return pltpu.make_async_copy(u_hbm.at[:, pl.ds(col, BLK)], buf.at[s % depth, half], sems.at[s % depth, half])

        @pl.when(j == 0)
        def _():
            for s in range(depth - 1):
                for half in range(2):
                    fetch(s, half).start()

        @pl.when(j + depth - 1 < nsl)
        def _():
            for half in range(2):
                fetch(j + depth - 1, half).start()

        for half in range(2):
            fetch(j, half).wait()
        g_ref, v_ref = buf.at[j % depth, 0], buf.at[j % depth, 1]
        _fill_shifted(xg, g_ref, nch)
        _fill_shifted(xv, v_ref, nch)
        for ci in range(nch):
            r0 = BLK * ci
            ug = _conv3(xg, wg_ref, r0) + bg_ref[...]
            uv = _conv3(xv, wv_ref, r0) + bv_ref[...]
            sg = jax.nn.sigmoid(ug)
            silu = ug * sg
            o_ref[r0:r0 + BLK, :] = (silu * uv).astype(BF16)
            dv_ref[r0:r0 + BLK, :] = silu.astype(BF16)
            dg_ref[r0:r0 + BLK, :] = (uv * (sg * (1.0 + ug * (1.0 - sg)))).astype(BF16)

    slab = pl.BlockSpec((p, BLK), lambda j: (0, j))
    return pl.pallas_call(
        body, name="ffn_act", grid=(nsl,),
        in_specs=[ANY] + _ffn_slab_specs(p)[2:],
        out_specs=[slab] * 3,
        out_shape=[jax.ShapeDtypeStruct((p, FFN), BF16)] * 3,
        scratch_shapes=[pltpu.VMEM((p + 8, BLK), F32)] * 2
        + [pltpu.VMEM((depth, 2, p, BLK), BF16), pltpu.SemaphoreType.DMA((depth, 2))],
        compiler_params=_cparams("arbitrary"),
    )(u0, fw, fw, fb, fb)


def _ffn_down_loss(act, wd, h1, tgt, gain):
    p = act.shape[0]
    tm = _row_tile(p)
    k = tm // BLK

    def body(*refs):
        a_ref, w_ref, h_ref = refs[:3]
        t_refs = refs[3:3 + k]
        g_ref, df_ref, da_ref, dy_ref, acc_ref = refs[3 + k:]
        i = pl.program_id(0)

        @pl.when(i == 0)
        def _():
            acc_ref[...] = jnp.zeros_like(acc_ref)

        ffn = _dot(a_ref[...], w_ref[...])
        t = jnp.concatenate([t_ref[...] for t_ref in t_refs], axis=0) if k > 1 else t_refs[0][...]
        diff = jnp.where(_rows(i, tm) >= BLK, h_ref[...] + _rms(ffn, g_ref[...]) - t, 0.0)
        dy = diff * (1.0 / D)
        dffn, dg = _rms_bwd(ffn, g_ref[...], dy)
        acc_ref[0:1, :] += dg
        acc_ref[1:2, :] += jnp.sum(diff * diff, axis=0, keepdims=True) * (0.5 / D)
        dy_ref[...] = dy
        dfb = dffn.astype(BF16)
        df_ref[...] = dfb
        for c0 in range(0, FFN, 256):
            da_ref[:, c0:c0 + 256] = _dot_nt(dfb, w_ref[c0:c0 + 256, :]).astype(BF16)

    def row(w):
        return pl.BlockSpec((tm, w), lambda i: (i, 0))

    return pl.pallas_call(
        body, name="ffn_down_loss", grid=(p // tm,),
        in_specs=[row(FFN), VM, row(D)] + _token_specs(tm) + [VM],
        out_specs=[row(D), row(FFN), row(D), pl.BlockSpec((8, D), lambda i: (0, 0))],
        out_shape=[jax.ShapeDtypeStruct((p, D), BF16), jax.ShapeDtypeStruct((p, FFN), BF16),
                   jax.ShapeDtypeStruct((p, D), F32), jax.ShapeDtypeStruct((8, D), F32)],
        compiler_params=_cparams("arbitrary"),
    )(act, wd, h1, *([tgt] * k), gain)


def _mm_tn(pieces, b, name, col_sums=False, comm=None):
    p, n = b.shape
    tk = 256
    nblk = [a.shape[1] // tk for a in pieces]
    offs = [sum(nblk[:q]) for q in range(len(pieces))]
    total = sum(nblk)
    npc = len(pieces)

    def body(*refs):
        a_refs, b_ref, o_ref = refs[:npc], refs[npc], refs[npc + 1]
        i = pl.program_id(0)
        for q, a_ref in enumerate(a_refs):
            @pl.when(jnp.logical_and(i >= offs[q], i < offs[q] + nblk[q]))
            def _(a_ref=a_ref):
                a_v = a_ref[...]
                o_ref[...] = _dot_tn(a_v, b_ref[...]).astype(BF16)
                if col_sums:
                    refs[npc + 2][...] = jnp.sum(a_v.astype(F32), axis=0, keepdims=True)

    def a_spec(q):
        return pl.BlockSpec((p, tk), lambda i: (0, jnp.clip(i - offs[q], 0, nblk[q] - 1)))

    out_specs = [pl.BlockSpec((tk, n), lambda i: (i, 0))]
    out_shape = [jax.ShapeDtypeStruct((total * tk, n), BF16)]
    if col_sums:
        out_specs.append(pl.BlockSpec((1, tk), lambda i: (0, i)))
        out_shape.append(jax.ShapeDtypeStruct((1, total * tk), F32))
    res, sent = _call(
        body, name=name, grid=(total,),
        in_specs=[a_spec(q) for q in range(npc)] + [VM],
        out_specs=out_specs, out_shape=out_shape, args=(*pieces, b), comm=comm,
        comm_mid=(3 * total) // 4 if hasattr(comm, "middle") else None)
    res = res if col_sums else res[0]
    return res if comm is None else (res, sent)


def _ffn_act_bwd(u0, dact, dact_dg, dact_dv, fw, act, dffn, comm=None):
    p = u0.shape[0]
    nch = p // BLK
    ncol = FFN // BLK

    depth = 3

    def body(u_hbm, da_hbm, lg_hbm, lv_hbm, act_hbm, wg_ref, wv_ref, df_ref,
             dg_ref, dv_ref, gwg_ref, gwv_ref, gbg_ref, gbv_ref, gwd_ref, eg, ev, buf, sems):
        j = pl.program_id(0)
        streams = ((u_hbm, 0), (u_hbm, ncol), (da_hbm, 0), (lg_hbm, 0), (lv_hbm, 0), (act_hbm, 0))

        def fetch(s, k):
            src, base = streams[k]
            col = pl.multiple_of((base + s) * BLK, BLK)
            return pltpu.make_async_copy(src.at[:, pl.ds(col, BLK)], buf.at[s % depth, k], sems.at[s % depth, k])

        @pl.when(j == 0)
        def _():
            for s in range(depth - 1):
                for k in range(len(streams)):
                    fetch(s, k).start()

        @pl.when(j + depth - 1 < ncol)
        def _():
            for k in range(len(streams)):
                fetch(j + depth - 1, k).start()

        for k in range(len(streams)):
            fetch(j, k).wait()
        g_ref, v_ref, da_ref, lg_ref, lv_ref, act_ref = (buf.at[j % depth, k] for k in range(len(streams)))
        gwd_ref[...] = _dot_tn(act_ref[...], df_ref[...]).astype(BF16)
        eg[p:p + 8, :] = jnp.zeros((8, BLK), F32)
        ev[p:p + 8, :] = jnp.zeros((8, BLK), F32)
        for ci in range(nch):
            r0 = BLK * ci
            d = da_ref[r0:r0 + BLK, :].astype(F32)
            eg[r0:r0 + BLK, :] = d * lg_ref[r0:r0 + BLK, :].astype(F32)
            ev[r0:r0 + BLK, :] = d * lv_ref[r0:r0 + BLK, :].astype(F32)
        def fold(v):
            return jnp.sum(v.reshape(BLK // 8, 8, BLK), axis=0)

        for e_s, x_ref, w_ref, d_ref, gw_ref, gb_ref in ((eg, g_ref, wg_ref, dg_ref, gwg_ref, gbg_ref),
                                                        (ev, v_ref, wv_ref, dv_ref, gwv_ref, gbv_ref)):
            sums = [jnp.zeros((8, BLK), F32) for _ in range(FFN_K + 1)]
            for ci in range(nch):
                r0 = BLK * ci
                es = [e_s[r0 + t:r0 + t + BLK, :] for t in range(FFN_K)]
                du = w_ref[2:3, :] * es[0] + w_ref[1:2, :] * es[1] + w_ref[0:1, :] * es[2]
                if ci == 0:
                    du = jnp.where(_rows(0, BLK) >= PAD, du, 0.0)
                d_ref[r0:r0 + BLK, :] = du.astype(BF16)
                x = x_ref[r0:r0 + BLK, :].astype(F32)
                for j in range(FFN_K):
                    sums[j] = sums[j] + fold(es[FFN_K - 1 - j] * x)
                sums[FFN_K] = sums[FFN_K] + fold(es[0])
            for j in range(FFN_K):
                gw_ref[j:j + 1, :] = jnp.sum(sums[j], axis=0, keepdims=True)
            gb_ref[...] = jnp.sum(sums[FFN_K], axis=0, keepdims=True)

    slab = pl.BlockSpec((p, BLK), lambda j: (0, j))
    wspec = pl.BlockSpec((FFN_K, BLK), lambda j: (0, j))
    bspec = pl.BlockSpec((1, BLK), lambda j: (0, j))
    return _call(
        body, name="ffn_act_bwd", grid=(ncol,),
        in_specs=[ANY] * 5 + _ffn_slab_specs(p)[2:4] + [VM],
        out_specs=[slab, slab, wspec, wspec, bspec, bspec, pl.BlockSpec((BLK, D), lambda j: (j, 0))],
        out_shape=[jax.ShapeDtypeStruct((p, FFN), BF16)] * 2 + [jax.ShapeDtypeStruct((FFN_K, FFN), F32)] * 2
        + [jax.ShapeDtypeStruct((1, FFN), F32)] * 2 + [jax.ShapeDtypeStruct((FFN, D), BF16)],
        scratch=[pltpu.VMEM((p + 8, BLK), F32)] * 2
        + [pltpu.VMEM((depth, 6, p, BLK), BF16), pltpu.SemaphoreType.DMA((depth, 6))],
        sem="arbitrary", args=(u0, dact, dact_dg, dact_dv, act, fw, fw, dffn), comm=comm)


def _ffn_in_bwd(dug, duv, w_upt, h1, dy, gain, comm=None):
    p = h1.shape[0]
    tm = _row_tile(p)

    def body(dg_ref, dv_ref, w_ref, h_ref, dy_ref, g_ref, o_ref, acc_ref):
        i = pl.program_id(0)

        @pl.when(i == 0)
        def _():
            acc_ref[...] = jnp.zeros_like(acc_ref)

        dn = _dot(dg_ref[...], w_ref[0:FFN, :]) + _dot(dv_ref[...], w_ref[FFN:2 * FFN, :])
        dh, dg = _rms_bwd(h_ref[...], g_ref[...], dn)
        o_ref[...] = dy_ref[...] + dh
        acc_ref[0:1, :] += dg

    def row(w):
        return pl.BlockSpec((tm, w), lambda i: (i, 0))

    return _call(
        body, name="ffn_in_bwd", grid=(p // tm,),
        in_specs=[row(FFN), row(FFN), VM, row(D), row(D), VM],
        out_specs=[row(D), pl.BlockSpec((8, D), lambda i: (0, 0))],
        out_shape=[jax.ShapeDtypeStruct((p, D), F32), jax.ShapeDtypeStruct((8, D), F32)],
        sem="arbitrary", args=(dug, duv, w_upt, h1, dy, gain), comm=comm)


def _mixer_bwd(dh1, mix, attn, conv, gates, c0, wa, wc, wo, vecs, comm=None):
    p = dh1.shape[0]
    tm = _row_tile(p)

    def body(dh_ref, mix_ref, at_ref, cv_ref, gt_ref, c0_ref, wa_ref, wc_ref, wo_ref, v_ref,
             dmix_ref, dat_ref, dcv_ref, dgt_ref, dao_ref, dc0_ref, acc_ref):
        i = pl.program_id(0)

        @pl.when(i == 0)
        def _():
            acc_ref[...] = jnp.zeros_like(acc_ref)

        dmix, dgp = _rms_bwd(mix_ref[...], v_ref[3:4, :], dh_ref[...])
        dmix = dmix.astype(BF16)
        dmix_ref[...] = dmix
        dmg = _dot_nt(dmix, wo_ref[...])
        sa = jax.nn.sigmoid(gt_ref[:, 0:D].astype(F32))
        sc = jax.nn.sigmoid(gt_ref[:, D:2 * D].astype(F32))
        dat = dmg * sa
        dcv = dmg * sc
        dgt_ref[:, 0:D] = (dmg * at_ref[...].astype(F32) * sa * (1.0 - sa)).astype(BF16)
        dgt_ref[:, D:2 * D] = (dmg * cv_ref[...].astype(F32) * sc * (1.0 - sc)).astype(BF16)
        datb = dat.astype(BF16)
        dcvb = dcv.astype(BF16)
        dat_ref[...] = datb
        dcv_ref[...] = dcvb
        dao_ref[...] = _dot_nt(datb, wa_ref[...]).astype(BF16)
        dc1 = _dot_nt(dcvb, wc_ref[...])
        dc0, dlg, dlb = _lnsilu_bwd(c0_ref[...], v_ref[0:1, :], v_ref[1:2, :], dc1)
        dc0_ref[...] = dc0
        acc_ref[0:1, :] += dgp
        acc_ref[1:2, :] += jnp.sum(dcv, axis=0, keepdims=True)
        acc_ref[2:3, :] += dlg
        acc_ref[3:4, :] += dlb

    def row(w):
        return pl.BlockSpec((tm, w), lambda i: (i, 0))

    return _call(
        body, name="mixer_bwd", grid=(p // tm,),
        in_specs=[row(D), row(D), row(D), row(D), row(2 * D), row(D), VM, VM, VM, VM],
        out_specs=[row(D), row(D), row(D), row(2 * D), row(D), row(D), pl.BlockSpec((8, D), lambda i: (0, 0))],
        out_shape=[jax.ShapeDtypeStruct((p, D), BF16)] * 3 + [jax.ShapeDtypeStruct((p, 2 * D), BF16),
                                                             jax.ShapeDtypeStruct((p, D), BF16),
                                                             jax.ShapeDtypeStruct((p, D), F32),
                                                             jax.ShapeDtypeStruct((8, D), F32)],
        sem="arbitrary", args=(dh1, mix, attn, conv, gates, c0, wa, wc, wo, vecs), comm=comm)


def _conv31_bwd(ag, dc0, w32, tn_pairs, comm=None):
    p = ag.shape[0]
    nch = p // BLK
    npair = len(tn_pairs)

    def body(*refs):
        a_ref, g_ref, dc_ref, w_ref = refs[:4]
        tn_a, tn_b = refs[4:4 + npair], refs[4 + npair:4 + 2 * npair]
        da_ref, dg_ref, gw_ref, gb_ref = refs[4 + 2 * npair:8 + 2 * npair]
        tn_o = refs[8 + 2 * npair:8 + 3 * npair]
        gp, dp = refs[8 + 3 * npair:]
        for ta, tb, to in zip(tn_a, tn_b, tn_o):
            to[...] = _dot_tn(ta[...], tb[...]).astype(BF16)
        gp[0:32, :] = jnp.zeros((32, BLK), F32)
        dp[p:p + 32, :] = jnp.zeros((32, BLK), F32)
        bsum = jnp.zeros((BLK, BLK), F32)
        for ci in range(nch):
            r0 = BLK * ci
            glu = a_ref[r0:r0 + BLK, :].astype(F32) * jax.nn.sigmoid(g_ref[r0:r0 + BLK, :].astype(F32))
            if ci == 0:
                glu = jnp.where(_rows(0, BLK) >= PAD, glu, 0.0)
            gp[32 + r0:32 + r0 + BLK, :] = glu
            d = dc_ref[r0:r0 + BLK, :]
            dp[r0:r0 + BLK, :] = d
            bsum = bsum + d
        gb_ref[...] = jnp.sum(bsum, axis=0, keepdims=True)
        for ci in range(nch):
            r0 = BLK * ci
            acc = jnp.zeros((BLK, BLK), F32)
            for j in range(CONV_K):
                acc = acc + w_ref[j:j + 1, :] * dp[r0 + 30 - j:r0 + 30 - j + BLK, :]
            if ci == 0:
                acc = jnp.where(_rows(0, BLK) >= PAD, acc, 0.0)
            a = a_ref[r0:r0 + BLK, :].astype(F32)
            sg = jax.nn.sigmoid(g_ref[r0:r0 + BLK, :].astype(F32))
            da_ref[r0:r0 + BLK, :] = (acc * sg).astype(BF16)
            dg_ref[r0:r0 + BLK, :] = (acc * a * sg * (1.0 - sg)).astype(BF16)
        sub = BLK // 2
        accs = [jnp.zeros((8, BLK), F32) for _ in range(CONV_K)]
        for r0 in range(0, p, sub):
            d = dp[r0:r0 + sub, :]
            for j in range(CONV_K):
                prod = d * gp[r0 + j + 2:r0 + j + 2 + sub, :]
                accs[j] = accs[j] + jnp.sum(prod.reshape(sub // 8, 8, BLK), axis=0)
        for j in range(CONV_K):
            gw_ref[j:j + 1, :] = jnp.sum(accs[j], axis=0, keepdims=True)
        gw_ref[CONV_K:32, :] = jnp.zeros((32 - CONV_K, BLK), F32)

    slab = pl.BlockSpec((p, BLK), lambda j: (0, j))
    return _call(
        body, name="conv31_bwd", grid=(D // BLK,),
        in_specs=[slab, pl.BlockSpec((p, BLK), lambda j: (0, 8 + j)), slab, pl.BlockSpec((32, BLK), lambda j: (0, j))]
        + [slab] * npair + [VM] * npair,
        out_specs=[slab, slab, pl.BlockSpec((32, BLK), lambda j: (0, j)), pl.BlockSpec((1, BLK), lambda j: (0, j))]
        + [pl.BlockSpec((BLK, D), lambda j: (j, 0))] * npair,
        out_shape=[jax.ShapeDtypeStruct((p, D), BF16)] * 2 + [jax.ShapeDtypeStruct((32, D), F32),
                                                             jax.ShapeDtypeStruct((1, D), F32)]
        + [jax.ShapeDtypeStruct((D, D), BF16)] * npair,
        scratch=[pltpu.VMEM((p + 32, BLK), F32)] * 2,
        args=(ag, ag, dc0, w32, *[a for a, _ in tn_pairs], *[b for _, b in tn_pairs]), comm=comm)


def _attn_bwd(q, kv, dao, sinks, tabs, comm=None):
    p = q.shape[0]
    nb = p // BLK

    def body(q_ref, km_ref, kp_ref, kc_ref, do_ref, sink_ref, t_ref, dqkv_ref, dsink_ref, carry, macc):
        i = pl.program_id(0)
        n = nb - 1 - i

        @pl.when(i == 0)
        def _():
            carry[...] = jnp.zeros_like(carry)
            macc[...] = jnp.zeros_like(macc)
            dsink_ref[...] = jnp.zeros_like(dsink_ref)

        lo = lax.broadcasted_iota(jnp.int32, (BLK, BLK), 1) < HEAD_DIM
        lane8 = lax.broadcasted_iota(jnp.int32, (8, BLK), 1)
        c, s1, s2 = t_ref[:, 0:128], -t_ref[:, 128:256], -t_ref[:, 256:384]
        dk = jnp.zeros((N_KEY, BLK), F32)
        dv = jnp.zeros((N_KEY, BLK), F32)
        for h in range(2):
            qs, k2, v2, bias, lok = _attn_setup(n, h, q_ref, km_ref, kp_ref, kc_ref)
            dos = _stack_heads(do_ref, h, lo)
            st = _dot_nt(k2, qs)
            dpt = _dot_nt(v2, dos)
            p_parts, ds_parts = [], []
            for g in range(8):
                cols = slice(BLK * g, BLK * (g + 1))
                pn, ps = _attn_head(st[:, cols], bias, sink_ref[0, 8 * h + g])
                dp = dpt[:, cols]
                delta = jnp.sum(pn * dp, axis=0, keepdims=True)
                ds_parts.append((pn * (dp - delta)).astype(BF16))
                p_parts.append(pn.astype(BF16))
                dsk = -jnp.sum(ps * delta, axis=1, keepdims=True)
                dsink_ref[...] += jnp.where(lane8 == 8 * h + g, dsk, 0.0)
            dst = jnp.concatenate(ds_parts, axis=1)
            pt = jnp.concatenate(p_parts, axis=1)
            dq = _dot_tn(dst, k2)
            for jp in range(4):
                lo_c = BLK * (4 * h + jp)
                dqkv_ref[:, lo_c:lo_c + BLK] = (_rope(_unstack_heads(dq, jp, lo), c, s1, s2) * SCALE).astype(BF16)
            dk2 = _dot(dst, qs)
            dv2 = _dot(pt, dos)
            dk2 = dk2 + pltpu.roll(dk2, HEAD_DIM, 1)
            dv2 = dv2 + pltpu.roll(dv2, HEAD_DIM, 1)
            own = lok if h == 0 else jnp.logical_not(lok)
            dk = jnp.where(own, dk2, dk)
            dv = jnp.where(own, dv2, dv)
        macc[:, 0:BLK] += dk[2 * BLK:N_KEY]
        macc[:, BLK:2 * BLK] += dv[2 * BLK:N_KEY]
        last = (n == 0).astype(F32)
        zpad = jnp.zeros((PAD, BLK), F32)
        dk_c = dk[BLK:2 * BLK] + carry[:, 0:BLK] + last * jnp.concatenate([zpad, macc[:, 0:BLK]], axis=0)
        dv_c = dv[BLK:2 * BLK] + carry[:, BLK:2 * BLK] + last * jnp.concatenate([zpad, macc[:, BLK:2 * BLK]], axis=0)
        carry[:, 0:BLK] = dk[0:BLK]
        carry[:, BLK:2 * BLK] = dv[0:BLK]
        dqkv_ref[:, D:D + BLK] = _rope(dk_c, c, s1, s2).astype(BF16)
        dqkv_ref[:, D + BLK:D + 2 * BLK] = dv_c.astype(BF16)

    def rev(w):
        return pl.BlockSpec((BLK, w), lambda i: (nb - 1 - i, 0))

    return _call(
        body, name="attn_bwd", grid=(nb,),
        in_specs=[rev(D),
                  pl.BlockSpec((BLK, 256), lambda i: (0, 0)),
                  pl.BlockSpec((BLK, 256), lambda i: (jnp.maximum(nb - 2 - i, 0), 0)),
                  rev(256), rev(D),
                  pl.BlockSpec(memory_space=pltpu.SMEM), rev(384)],
        out_specs=[rev(QKV_W), pl.BlockSpec((8, BLK), lambda i: (0, 0))],
        out_shape=[jax.ShapeDtypeStruct((p, QKV_W), BF16), jax.ShapeDtypeStruct((8, BLK), F32)],
        scratch=[pltpu.VMEM((BLK, 256), F32), pltpu.VMEM((N_META, 256), F32)], sem="arbitrary",
        args=(q, kv, kv, kv, dao, sinks, tabs), comm=comm)


def _in_bwd(dqkv, da, dg, dgt, w_int, h0p, dh1, gain, comm=None):
    p = h0p.shape[0]
    tm = _row_tile(p)
    nt = p // tm
    first_rows = tm - BLK

    def body(dq_ref, da_ref, dg_ref, dt_ref, w_ref, h_ref, dh_ref, g_ref, gx_ref, dm_ref, acc_ref, buf, sems):
        i = pl.program_id(0)
        slot = i % 2

        @pl.when(i == 0)
        def _():
            acc_ref[...] = jnp.zeros_like(acc_ref)

        dn = (_dot(dq_ref[...], w_ref[0:QKV_W, :]) + _dot(da_ref[...], w_ref[QKV_W:QKV_W + D, :])
              + _dot(dg_ref[...], w_ref[QKV_W + D:QKV_W + 2 * D, :]) + _dot(dt_ref[...], w_ref[QKV_W + 2 * D:IN_W, :]))
        dh, dgain = _rms_bwd(h_ref[...], g_ref[...], dn)
        dh0 = dh_ref[...] + dh
        acc_ref[0:1, :] += dgain
        buf[slot] = dh0

        @pl.when(i == 0)
        def _():
            dm_ref[...] = dh0[PAD:BLK]

        def first_copy():
            return pltpu.make_async_copy(buf.at[0, pl.ds(BLK, first_rows), :], gx_ref.at[pl.ds(0, first_rows), :], sems.at[0])

        def tile_copy(j, s):
            return pltpu.make_async_copy(buf.at[s], gx_ref.at[pl.ds(pl.multiple_of(j * tm - BLK, BLK), tm), :], sems.at[s])

        if first_rows:
            @pl.when(i == 1)
            def _():
                first_copy().wait()

        @pl.when(i >= 2)
        def _():
            tile_copy(i - 1, 1 - slot).wait()

        if first_rows:
            @pl.when(i == 0)
            def _():
                first_copy().start()

        @pl.when(i > 0)
        def _():
            tile_copy(i, slot).start()

        @pl.when(i == nt - 1)
        def _():
            tile_copy(i, slot).wait()

    def row(w):
        return pl.BlockSpec((tm, w), lambda i: (i, 0))

    return _call(
        body, name="in_bwd", grid=(nt,),
        in_specs=[row(QKV_W), row(D), row(D), row(2 * D), VM, row(D), row(D), VM],
        out_specs=[ANY, pl.BlockSpec((N_META, D), lambda i: (0, 0)), pl.BlockSpec((8, D), lambda i: (0, 0))],
        out_shape=[jax.ShapeDtypeStruct((p - BLK, D), F32), jax.ShapeDtypeStruct((N_META, D), F32),
                   jax.ShapeDtypeStruct((8, D), F32)],
        scratch=[pltpu.VMEM((2, tm, D), F32), pltpu.SemaphoreType.DMA((2,))],
        sem="arbitrary", args=(dqkv, da, dg, dgt, w_int, h0p, dh1, gain), comm=comm)


def _sum_slots(slots, name):
    r = slots.shape[0] // N_DEV
    cols = slots.shape[1]
    tr = r if r <= 352 else (r // 2 if (r // 2) % 16 == 0 else r // 3)
    steps = r // tr

    def body(*refs):
        acc = refs[0][...].astype(F32)
        for s in range(1, N_DEV):
            acc = acc + refs[s][...].astype(F32)
        refs[N_DEV][...] = acc

    return pl.pallas_call(
        body, name=name, grid=(steps,),
        in_specs=[pl.BlockSpec((tr, cols), functools.partial(lambda i, s: (s * steps + i, 0), s=s)) for s in range(N_DEV)],
        out_specs=pl.BlockSpec((tr, cols), lambda i: (i, 0)),
        out_shape=jax.ShapeDtypeStruct((r, cols), F32),
        compiler_params=_cparams("parallel"),
    )(*([slots] * N_DEV))


def _adamw_math(w, g, m, v):
    m_n = ADAM_B1 * m + (1.0 - ADAM_B1) * g
    v_n = ADAM_B2 * v + (1.0 - ADAM_B2) * jnp.square(g)
    m_hat = m_n / (1.0 - ADAM_B1 ** ADAM_STEP)
    v_hat = v_n / (1.0 - ADAM_B2 ** ADAM_STEP)
    return -ADAM_LR * (m_hat / (jnp.sqrt(v_hat) + ADAM_EPS) + ADAM_WD * w), m_n, v_n


def _sum_adamw(parts, w, m, v, name, nslots=N_DEV):
    r, cols = w.shape
    rs = r // len(parts)
    tr = rs if rs <= 352 else (rs // 2 if (rs // 2) % 16 == 0 else rs // 3)
    steps = rs // tr

    def body(*refs):
        w_ref, m_ref, v_ref, g_ref, d_ref, nm_ref, nv_ref = refs[nslots * len(parts):]
        i = pl.program_id(0)
        for q in range(len(parts)):
            @pl.when(i // steps == q)
            def _(q=q):
                g = refs[nslots * q][...].astype(F32)
                for s in range(1, nslots):
                    g = g + refs[nslots * q + s][...].astype(F32)
                g_ref[...] = g
                d_ref[...], nm_ref[...], nv_ref[...] = _adamw_math(w_ref[...], g, m_ref[...], v_ref[...])

    def slot_spec(q, s):
        return pl.BlockSpec((tr, cols), lambda i: (s * steps + jnp.clip(i - q * steps, 0, steps - 1), 0))

    spec = pl.BlockSpec((tr, cols), lambda i: (i, 0))
    return pl.pallas_call(
        body, name=name, grid=(steps * len(parts),),
        in_specs=[slot_spec(q, s) for q in range(len(parts)) for s in range(nslots)] + [spec] * 3,
        out_specs=[spec] * 4, out_shape=[jax.ShapeDtypeStruct((r, cols), F32)] * 4,
        compiler_params=_cparams("parallel"),
    )(*[a for a in parts for _ in range(nslots)], w, m, v)


def _adamw_many(ws, gs, ms, vs, name):
    n = len(ws)

    def body(*refs):
        w, g, m, v = refs[0:n], refs[n:2 * n], refs[2 * n:3 * n], refs[3 * n:4 * n]
        d, nm, nv = refs[4 * n:5 * n], refs[5 * n:6 * n], refs[6 * n:7 * n]
        for k in range(n):
            d[k][...], nm[k][...], nv[k][...] = _adamw_math(w[k][...], g[k][...], m[k][...], v[k][...])

    outs = pl.pallas_call(
        body, name=name, in_specs=[VM] * (4 * n), out_specs=[VM] * (3 * n),
        out_shape=[jax.ShapeDtypeStruct(a.shape, F32) for a in ws] * 3,
    )(*ws, *gs, *ms, *vs)
    return outs[0:n], outs[n:2 * n], outs[2 * n:3 * n]


def _rope_tables(p):
    half = ROT_DIM // 2
    lane = jnp.arange(BLK)
    seg = (lane % HEAD_DIM) // half
    inv_freq = ROPE_THETA ** (-(lane % half).astype(F32) * 2.0 / ROT_DIM)
    pos = (jnp.arange(p) - PAD).astype(F32)
    ang = pos[:, None] * inv_freq[None, :]
    cos = jnp.cos(ang)
    sin = jnp.sin(ang)
    c = jnp.where(seg[None, :] < 2, cos, 1.0)
    s1 = jnp.where(seg[None, :] == 0, -sin, 0.0)
    s2 = jnp.where(seg[None, :] == 1, sin, 0.0)
    return jnp.concatenate([c, s1, s2], axis=1).astype(F32)


def _flat_pack(parts, rows):
    flat = jnp.concatenate([a.reshape(-1).astype(F32) for a in parts])
    return jnp.pad(flat, (0, rows * D - flat.shape[0])).reshape(rows, D)


def _flat_unpack(pack, shapes):
    flat = pack.reshape(-1)
    out, off = [], 0
    for s in shapes:
        size = 1
        for e in s:
            size *= e
        out.append(flat[off:off + size].reshape(s))
        off += size
    return out


def kernel(x, meta_tokens, norm_pre_mix, norm_post_mix, w_in, b_in, attn_sinks, w_attn_proj, conv_dw_w, conv_dw_b, conv_ln_g, conv_ln_b, w_conv_proj, b_conv_proj, w_out, norm_pre_ffn, norm_post_ffn, w_up, ffn_dw_w, ffn_dw_b, w_down, loss_target, m_meta_tokens, m_norm_pre_mix, m_norm_post_mix, m_w_in, m_b_in, m_attn_sinks, m_w_attn_proj, m_conv_dw_w, m_conv_dw_b, m_conv_ln_g, m_conv_ln_b, m_w_conv_proj, m_b_conv_proj, m_w_out, m_norm_pre_ffn, m_norm_post_ffn, m_w_up, m_ffn_dw_w, m_ffn_dw_b, m_w_down, v_meta_tokens, v_norm_pre_mix, v_norm_post_mix, v_w_in, v_b_in, v_attn_sinks, v_w_attn_proj, v_conv_dw_w, v_conv_dw_b, v_conv_ln_g, v_conv_ln_b, v_w_conv_proj, v_b_conv_proj, v_w_out, v_norm_pre_ffn, v_norm_post_ffn, v_w_up, v_ffn_dw_w, v_ffn_dw_b, v_w_down):
    seq = x.shape[1]
    p = seq + BLK
    me = 4 * lax.axis_index("x") + 2 * lax.axis_index("y") + lax.axis_index("c")
    in_cols = w_in.shape[2]
    up_cols = w_up.shape[2]

    small = jnp.zeros((56, up_cols), F32)
    small = small.at[0:N_META, 0:BLK].set(meta_tokens)
    small = small.at[16:16 + CONV_K, 0:BLK].set(conv_dw_w[0])
    small = small.at[48:48 + FFN_K, :].set(ffn_dw_w[0])
    w_int, small_all = _exchange(_Both(_GatherRelay(w_in[0].T.astype(BF16)), _Gather([small])), "gather_w_in")
    small_all = small_all.reshape(N_DEV, 56, up_cols)
    meta_full = small_all[:, 0:N_META, 0:BLK].transpose(1, 0, 2).reshape(N_META, D)
    cdw = small_all[:, 16:16 + CONV_K, 0:BLK].transpose(1, 0, 2).reshape(CONV_K, D)
    cdw32 = jnp.pad(cdw, ((0, 32 - CONV_K), (0, 0)))
    fdw = small_all[:, 48:48 + FFN_K, :].transpose(1, 0, 2).reshape(FFN_K, 2 * FFN)

    tabs = _rope_tables(p)
    vecs = jnp.concatenate([conv_ln_g, conv_ln_b, b_conv_proj, norm_post_mix, norm_pre_ffn, jnp.zeros((3, D), F32)], axis=0)

    (h0p, n1, q, kv, ag, gates), (wa, wc, wo) = _in_proj(
        x[0], meta_full, norm_pre_mix, w_int, b_in, tabs,
        comm=_Gather([w_attn_proj[0].astype(BF16), w_conv_proj[0].astype(BF16), w_out[0].astype(BF16)]))
    (ao,), (w_upt,) = _attn_fwd(q, kv, attn_sinks, comm=_Gather([w_up[0].T.astype(BF16)]))
    (c0,), (wd,) = _conv31_fwd(ag, cdw32, conv_dw_b, comm=_Gather([w_down[0].astype(BF16)]))
    c1, attn, conv, merged, mix, h1, n2 = _mixer_fwd(ao, c0, gates, h0p, wa, wc, wo, vecs)
    u0 = _mm_nt(n2, w_upt, "ffn_up")
    act, dact_dv, dact_dg = _ffn_act(u0, fdw, ffn_dw_b)
    dffn, dact, dy, acc_f = _ffn_down_loss(act, wd, h1, loss_target[0], norm_post_ffn)

    (dug, duv, gfw_g, gfw_v, gfb_g, gfb_v, g_wd), _ = _ffn_act_bwd(u0, dact, dact_dg, dact_dv, fdw, act, dffn)
    g_wupt, (s_wd0,) = _mm_tn([dug, duv], n2, "grad_w_up", comm=_Scatter([g_wd], 0, 2))
    (dh1, acc_u), (s_wd1,) = _ffn_in_bwd(dug, duv, w_upt, h1, dy, norm_pre_ffn, comm=_Scatter([g_wd], 1, 2))
    (dmix, dat, dcv, dgt, dao, dc0, acc_m), (s_wup0,) = _mixer_bwd(
        dh1, mix, attn, conv, gates, c0, wa, wc, wo, vecs, comm=_Scatter([g_wupt], 0, 4))
    (da, dg, g_cdw, g_cdb, g_wo, g_wa, g_wc), (s_wup1, s_wup2, s_wup3) = _conv31_bwd(
        ag, dc0, cdw32, [(merged, dmix), (ao, dat), (c1, dcv)],
        comm=_Both(_Both(_Scatter([g_wupt], 1, 4), _Scatter([g_wupt], 2, 4)), _Scatter([g_wupt], 3, 4)))
    (dqkv, dsink), (s_wa, s_wc, s_wo) = _attn_bwd(q, kv, dao, attn_sinks, tabs, comm=_Scatter([g_wa, g_wc, g_wo]))
    loss_row = jnp.sum(acc_f[1:2, :], axis=1, keepdims=True)
    early = [loss_row, acc_m[0:1], dsink[0:1, 0:16], g_cdw[0:CONV_K], g_cdb,
             acc_m[2:3], acc_m[3:4], acc_m[1:2], acc_u[0:1], acc_f[0:1],
             jnp.concatenate([gfw_g, gfw_v], axis=1), jnp.concatenate([gfb_g, gfb_v], axis=1)]
    (g_wint, g_bin), (gathered_early,) = _mm_tn([dqkv, da, dg, dgt], n1, "grad_w_in", col_sums=True,
                                                comm=_Gather([_flat_pack(early, 64)]))
    (from_sibling,) = _exchange(_SiblingSwap(g_wint), "swap_w_in")
    (grad_x2d, dmeta, acc_i), (s_win,) = _in_bwd(dqkv, da, dg, dgt, w_int, h0p, dh1, norm_pre_mix,
                                                 comm=_ChipScatter(_pair_add(g_wint, from_sibling)))

    big = []
    for nm, parts, nslots, w, m, v, tr in (
            ("w_in", [s_win], N_CHIP, w_in, m_w_in, v_w_in, True), ("w_up", [s_wup0, s_wup1, s_wup2, s_wup3], N_DEV, w_up, m_w_up, v_w_up, True),
            ("w_attn_proj", [s_wa], N_DEV, w_attn_proj, m_w_attn_proj, v_w_attn_proj, False),
            ("w_conv_proj", [s_wc], N_DEV, w_conv_proj, m_w_conv_proj, v_w_conv_proj, False),
            ("w_out", [s_wo], N_DEV, w_out, m_w_out, v_w_out, False),
            ("w_down", [s_wd0, s_wd1], N_DEV, w_down, m_w_down, v_w_down, False)):
        ins = [a[0].T if tr else a[0] for a in (w, m, v)]
        big.append(tuple((o.T if tr else o)[None] for o in _sum_adamw(parts, *ins, "update_" + nm, nslots)))

    late = [dmeta, acc_i[0:1], g_bin]
    (gathered_late,) = _exchange(_Gather([_flat_pack(late, 24)]), "gather_small_grads")
    g_meta, g_npm, g_bi = _flat_unpack(_sum_slots(gathered_late, "sum_late_grads"), [a.shape for a in late])
    tot = _flat_unpack(_sum_slots(gathered_early, "sum_small_grads"), [a.shape for a in early])
    (loss, g_nqm, g_sk, g_cw, g_cb, g_lg, g_lb, g_bc, g_npf, g_nqf, g_fw, g_fb) = tot
    loss = loss.reshape(())
    g_meta = lax.dynamic_slice_in_dim(g_meta, me * BLK, BLK, axis=1)
    g_cw = lax.dynamic_slice_in_dim(g_cw, me * BLK, BLK, axis=1)[None]
    g_fw = lax.dynamic_slice_in_dim(g_fw, me * up_cols, up_cols, axis=1)[None]

    sm_w = [meta_tokens, norm_pre_mix, norm_post_mix, b_in, attn_sinks, conv_dw_w, conv_dw_b, conv_ln_g, conv_ln_b,
            b_conv_proj, norm_pre_ffn, norm_post_ffn, ffn_dw_w, ffn_dw_b]
    sm_g = [g_meta, g_npm, g_nqm, g_bi, g_sk, g_cw, g_cb, g_lg, g_lb, g_bc, g_npf, g_nqf, g_fw, g_fb]
    sm_m = [m_meta_tokens, m_norm_pre_mix, m_norm_post_mix, m_b_in, m_attn_sinks, m_conv_dw_w, m_conv_dw_b, m_conv_ln_g,
            m_conv_ln_b, m_b_conv_proj, m_norm_pre_ffn, m_norm_post_ffn, m_ffn_dw_w, m_ffn_dw_b]
    sm_v = [v_meta_tokens, v_norm_pre_mix, v_norm_post_mix, v_b_in, v_attn_sinks, v_conv_dw_w, v_conv_dw_b, v_conv_ln_g,
            v_conv_ln_b, v_b_conv_proj, v_norm_pre_ffn, v_norm_post_ffn, v_ffn_dw_w, v_ffn_dw_b]
    swap = lambda a: jnp.transpose(a, (1, 0, 2)) if a.ndim == 3 else a
    sm_d, sm_nm, sm_nv = ([swap(o) for o in outs] for outs in
                          _adamw_many(*([swap(a) for a in group] for group in (sm_w, sm_g, sm_m, sm_v)), "adamw_small"))

    order = ["meta_tokens", "norm_pre_mix", "norm_post_mix", "w_in", "b_in", "attn_sinks", "w_attn_proj", "conv_dw_w",
             "conv_dw_b", "conv_ln_g", "conv_ln_b", "w_conv_proj", "b_conv_proj", "w_out", "norm_pre_ffn", "norm_post_ffn",
             "w_up", "ffn_dw_w", "ffn_dw_b", "w_down"]
    small_names = ["meta_tokens", "norm_pre_mix", "norm_post_mix", "b_in", "attn_sinks", "conv_dw_w", "conv_dw_b", "conv_ln_g",
                   "conv_ln_b", "b_conv_proj", "norm_pre_ffn", "norm_post_ffn", "ffn_dw_w", "ffn_dw_b"]
    big_names = ["w_in", "w_up", "w_attn_proj", "w_conv_proj", "w_out", "w_down"]
    table = {}
    for k, nm in enumerate(small_names):
        table[nm] = (sm_g[k], sm_d[k], sm_nm[k], sm_nv[k])
    for k, nm in enumerate(big_names):
        table[nm] = big[k]
    grad_x = grad_x2d[None]
    outs = [loss, grad_x]
    for field in range(4):
        outs += [table[nm][field] for nm in order]
    return tuple(outs)
```
